```python
import math
import jax, jax.numpy as jnp
from jax import lax
import numpy as np

D_MODEL = 1024
BATCH = 8
SEQ = 2048
DEPTH = 1

MEM_LEN = 256
EPS = 1e-6
RET_HEADS = 4
RET_DK = 128
RET_DV = 256
CHUNK = 128
ROPE_BASE = 10000.0
Q_W = RET_HEADS * RET_DK
V_W = RET_HEADS * RET_DV
LRU_WIDTH = 1024
LRU_BLOCKS = 8
LRU_BLOCK = LRU_WIDTH // LRU_BLOCKS
CONV_W = 4
LRU_C = 8.0
IN_W = 2 * Q_W + 2 * V_W + 2 * LRU_WIDTH
N_BRANCH = 2
D_FF = 2816
X_HEADS = 4
X_HD = D_MODEL // X_HEADS

kernel_name = "hybrid_retention_rglru_macaron_xattn"


def rmsnorm(x, g):
    xf = x.astype(jnp.float32)
    y = xf * lax.rsqrt(jnp.mean(xf * xf, axis=-1, keepdims=True) + EPS)
    return (y * g.astype(jnp.float32)).astype(x.dtype)


def swiglu(h, w1, w3, w2):
    return (jax.nn.silu(h @ w1) * (h @ w3)) @ w2


def rotary(t, pos):
    dk = t.shape[-1]
    inv_freq = ROPE_BASE ** (-jnp.arange(0, dk, 2, dtype=jnp.float32) / dk)
    ang = pos[:, None] * inv_freq[None, :]
    cos = jnp.cos(ang)[None, :, None, :]
    sin = jnp.sin(ang)[None, :, None, :]
    t1, t2 = t[..., : dk // 2], t[..., dk // 2:]
    return jnp.concatenate([t1 * cos - t2 * sin, t2 * cos + t1 * sin], axis=-1)


def retention_chunkwise(q, k, v):
    B, S, H, dk = q.shape
    dv = v.shape[-1]
    nC = S // CHUNK
    log_gamma = jnp.log(1.0 - 2.0 ** (-5.0 - jnp.arange(H, dtype=jnp.float32)))
    q = q.reshape(B, nC, CHUNK, H, dk)
    k = k.reshape(B, nC, CHUNK, H, dk) * (dk ** -0.5)
    v = v.reshape(B, nC, CHUNK, H, dv)
    pos = jnp.arange(CHUNK, dtype=jnp.float32)
    rel = pos[:, None] - pos[None, :]
    decay = jnp.where(rel[None] >= 0, jnp.exp(rel[None] * log_gamma[:, None, None]), 0.0)
    scores = jnp.einsum('bnihd,bnjhd->bnhij', q, k) * decay[None, None]
    inner = jnp.einsum('bnhij,bnjhe->bnihe', scores, v)
    k_decay = jnp.exp((CHUNK - 1.0 - pos)[None, :] * log_gamma[:, None])
    kv = jnp.einsum('bnjhd,bnjhe,hj->nbhde', k, v, k_decay)
    chunk_decay = jnp.exp(CHUNK * log_gamma)[None, :, None, None]

    def step(state, kv_c):
        return chunk_decay * state + kv_c, state

    init = jnp.zeros((B, H, dk, dv), jnp.float32)
    _, prev = lax.scan(step, init, kv)
    prev = jnp.moveaxis(prev, 0, 1)
    q_decay = jnp.exp((pos + 1.0)[:, None] * log_gamma[None, :])
    cross = jnp.einsum('bnihd,bnhde->bnihe', q, prev) * q_decay[None, None, :, :, None]
    return (inner + cross).reshape(B, S, H, dv)


def head_groupnorm(y, g):
    mu = jnp.mean(y, axis=-1, keepdims=True)
    var = jnp.mean(jnp.square(y - mu), axis=-1, keepdims=True)
    yn = (y - mu) * lax.rsqrt(var + EPS)
    B, S, H, dv = y.shape
    return yn.reshape(B, S, H * dv) * g.astype(jnp.float32)


def causal_depthwise_conv(x, w, b):
    S = x.shape[1]
    xp = jnp.pad(x, ((0, 0), (CONV_W - 1, 0), (0, 0)))
    y = xp[:, 0:S] * w[0]
    for tap in range(1, CONV_W):
        y = y + xp[:, tap:tap + S] * w[tap]
    return y + b


def rg_lru(x, w_r, b_r, w_i, b_i, lam):
    B, S, W = x.shape
    xb = x.reshape(B, S, LRU_BLOCKS, LRU_BLOCK)
    r = jax.nn.sigmoid(jnp.einsum('bsgi,gij->bsgj', xb, w_r).reshape(B, S, W) + b_r)
    i = jax.nn.sigmoid(jnp.einsum('bsgi,gij->bsgj', xb, w_i).reshape(B, S, W) + b_i)
    log_a = -LRU_C * r * jax.nn.softplus(-lam)
    a = jnp.exp(log_a)
    mult = jnp.sqrt(-jnp.expm1(2.0 * log_a))
    bx = mult * (i * x)

    def combine(c1, c2):
        a1, b1 = c1
        a2, b2 = c2
        return a1 * a2, a2 * b1 + b2

    _, h = lax.associative_scan(combine, (a, bx), axis=1)
    return h


def _fwd_setup_inputs(seed: int = 0) -> dict:
    key = jax.random.key(seed)
    ks = iter(jax.random.split(key, 64))

    def nrm(shape, scale):
        return jax.random.normal(next(ks), shape, jnp.float32) * scale

    def gain(shape):
        return 1.0 + nrm(shape, 0.02)

    L, D = DEPTH, D_MODEL
    lam_a = jax.random.uniform(next(ks), (L, LRU_WIDTH), jnp.float32, 0.9, 0.999)
    return {
        "x": nrm((BATCH, SEQ, D), 1.0),
        "mem": nrm((BATCH, MEM_LEN, D), 1.0),
        "ffn1_norm": gain((L, D)),
        "ffn1_w1": nrm((L, D, D_FF), D ** -0.5),
        "ffn1_w3": nrm((L, D, D_FF), D ** -0.5),
        "ffn1_w2": nrm((L, D_FF, D), D_FF ** -0.5),
        "mix_norm": gain((L, D)),
        "w_in": nrm((L, D, IN_W), D ** -0.5),
        "ret_gn": gain((L, V_W)),
        "w_ret_o": nrm((L, V_W, D), V_W ** -0.5),
        "conv_w": nrm((L, CONV_W, LRU_WIDTH), CONV_W ** -0.5),
        "conv_b": nrm((L, LRU_WIDTH), 0.01),
        "w_rgate": nrm((L, LRU_BLOCKS, LRU_BLOCK, LRU_BLOCK), LRU_BLOCK ** -0.5),
        "b_rgate": nrm((L, LRU_WIDTH), 0.01),
        "w_igate": nrm((L, LRU_BLOCKS, LRU_BLOCK, LRU_BLOCK), LRU_BLOCK ** -0.5),
        "b_igate": nrm((L, LRU_WIDTH), 0.01),
        "lru_lambda": jnp.log(lam_a) - jnp.log1p(-lam_a),
        "w_lru_o": nrm((L, LRU_WIDTH, D), LRU_WIDTH ** -0.5),
        "w_branch_gate": nrm((L, D, N_BRANCH * D), D ** -0.5),
        "b_branch_gate": nrm((L, N_BRANCH * D), 0.01),
        "w_out": nrm((L, D, D), D ** -0.5),
        "xattn_norm": gain((L, D)),
        "mem_norm": gain((L, D)),
        "w_xq": nrm((L, D, D), D ** -0.5),
        "w_xk": nrm((L, D, D), D ** -0.5),
        "w_xv": nrm((L, D, D), D ** -0.5),
        "w_xo": nrm((L, D, D), D ** -0.5),
        "ffn2_norm": gain((L, D)),
        "ffn2_w1": nrm((L, D, D_FF), D ** -0.5),
        "ffn2_w3": nrm((L, D, D_FF), D ** -0.5),
        "ffn2_w2": nrm((L, D_FF, D), D_FF ** -0.5),
        "final_norm": gain((D,)),
    }


def _fwd_reference(x, mem, ffn1_norm, ffn1_w1, ffn1_w3, ffn1_w2, mix_norm, w_in, ret_gn, w_ret_o,
              conv_w, conv_b, w_rgate, b_rgate, w_igate, b_igate, lru_lambda, w_lru_o,
              w_branch_gate, b_branch_gate, w_out, xattn_norm, mem_norm, w_xq, w_xk, w_xv, w_xo,
              ffn2_norm, ffn2_w1, ffn2_w3, ffn2_w2, final_norm):
    B, S, D = x.shape
    M = mem.shape[1]
    pos = jnp.arange(S, dtype=jnp.float32)
    for l in range(DEPTH):
        x = x + 0.5 * swiglu(rmsnorm(x, ffn1_norm[l]), ffn1_w1[l], ffn1_w3[l], ffn1_w2[l])

        h = rmsnorm(x, mix_norm[l])
        u = h @ w_in[l]
        o1 = Q_W
        o2 = o1 + Q_W
        o3 = o2 + V_W
        o4 = o3 + V_W
        o5 = o4 + LRU_WIDTH
        q = u[..., :o1].reshape(B, S, RET_HEADS, RET_DK).astype(jnp.float32)
        k = u[..., o1:o2].reshape(B, S, RET_HEADS, RET_DK).astype(jnp.float32)
        v = u[..., o2:o3].reshape(B, S, RET_HEADS, RET_DV).astype(jnp.float32)
        g_ret = u[..., o3:o4]
        x_lru = u[..., o4:o5]
        g_lru = u[..., o5:]

        ret = retention_chunkwise(rotary(q, pos), rotary(k, pos), v)
        ret = head_groupnorm(ret, ret_gn[l])
        y_ret = (jax.nn.silu(g_ret.astype(jnp.float32)) * ret).astype(x.dtype) @ w_ret_o[l]

        xc = causal_depthwise_conv(x_lru, conv_w[l], conv_b[l]).astype(jnp.float32)
        hl = rg_lru(xc, w_rgate[l].astype(jnp.float32), b_rgate[l].astype(jnp.float32),
                    w_igate[l].astype(jnp.float32), b_igate[l].astype(jnp.float32),
                    lru_lambda[l].astype(jnp.float32))
        y_lru = (hl * jax.nn.gelu(g_lru.astype(jnp.float32))).astype(x.dtype) @ w_lru_o[l]

        gates = jax.nn.sigmoid(h @ w_branch_gate[l] + b_branch_gate[l])
        merged = gates[..., :D] * y_ret + gates[..., D:] * y_lru
        x = x + merged @ w_out[l]

        hq = rmsnorm(x, xattn_norm[l])
        m = rmsnorm(mem, mem_norm[l])
        xq = (hq @ w_xq[l]).reshape(B, S, X_HEADS, X_HD)
        xk = (m @ w_xk[l]).reshape(B, M, X_HEADS, X_HD)
        xv = (m @ w_xv[l]).reshape(B, M, X_HEADS, X_HD)
        sc = jnp.einsum('bshd,bmhd->bhsm', xq.astype(jnp.float32), xk.astype(jnp.float32)) * (X_HD ** -0.5)
        p = jax.nn.softmax(sc, axis=-1)
        xo = jnp.einsum('bhsm,bmhd->bshd', p, xv.astype(jnp.float32)).reshape(B, S, D).astype(x.dtype)
        x = x + xo @ w_xo[l]

        x = x + 0.5 * swiglu(rmsnorm(x, ffn2_norm[l]), ffn2_w1[l], ffn2_w3[l], ffn2_w2[l])
    return rmsnorm(x, final_norm)


import jax as _jax
import jax.numpy as _jnp

TWIN_FORMAT = 'train_step'
FWD_PARAMS = ['x', 'mem', 'ffn1_norm', 'ffn1_w1', 'ffn1_w3', 'ffn1_w2', 'mix_norm', 'w_in', 'ret_gn', 'w_ret_o', 'conv_w', 'conv_b', 'w_rgate', 'b_rgate', 'w_igate', 'b_igate', 'lru_lambda', 'w_lru_o', 'w_branch_gate', 'b_branch_gate', 'w_out', 'xattn_norm', 'mem_norm', 'w_xq', 'w_xk', 'w_xv', 'w_xo', 'ffn2_norm', 'ffn2_w1', 'ffn2_w3', 'ffn2_w2', 'final_norm']
TWIN_WEIGHTS = ['ffn1_norm', 'ffn1_w1', 'ffn1_w3', 'ffn1_w2', 'mix_norm', 'w_in', 'ret_gn', 'w_ret_o', 'conv_w', 'conv_b', 'w_rgate', 'b_rgate', 'w_igate', 'b_igate', 'lru_lambda', 'w_lru_o', 'w_branch_gate', 'b_branch_gate', 'w_out', 'xattn_norm', 'mem_norm', 'w_xq', 'w_xk', 'w_xv', 'w_xo', 'ffn2_norm', 'ffn2_w1', 'ffn2_w3', 'ffn2_w2', 'final_norm']
TWIN_DIFF_INPUT = 'x'
TWIN_INPUTS = ['x', 'mem', 'ffn1_norm', 'ffn1_w1', 'ffn1_w3', 'ffn1_w2', 'mix_norm', 'w_in', 'ret_gn', 'w_ret_o', 'conv_w', 'conv_b', 'w_rgate', 'b_rgate', 'w_igate', 'b_igate', 'lru_lambda', 'w_lru_o', 'w_branch_gate', 'b_branch_gate', 'w_out', 'xattn_norm', 'mem_norm', 'w_xq', 'w_xk', 'w_xv', 'w_xo', 'ffn2_norm', 'ffn2_w1', 'ffn2_w3', 'ffn2_w2', 'final_norm', 'loss_target', 'm_ffn1_norm', 'm_ffn1_w1', 'm_ffn1_w3', 'm_ffn1_w2', 'm_mix_norm', 'm_w_in', 'm_ret_gn', 'm_w_ret_o', 'm_conv_w', 'm_conv_b', 'm_w_rgate', 'm_b_rgate', 'm_w_igate', 'm_b_igate', 'm_lru_lambda', 'm_w_lru_o', 'm_w_branch_gate', 'm_b_branch_gate', 'm_w_out', 'm_xattn_norm', 'm_mem_norm', 'm_w_xq', 'm_w_xk', 'm_w_xv', 'm_w_xo', 'm_ffn2_norm', 'm_ffn2_w1', 'm_ffn2_w3', 'm_ffn2_w2', 'm_final_norm', 'v_ffn1_norm', 'v_ffn1_w1', 'v_ffn1_w3', 'v_ffn1_w2', 'v_mix_norm', 'v_w_in', 'v_ret_gn', 'v_w_ret_o', 'v_conv_w', 'v_conv_b', 'v_w_rgate', 'v_b_rgate', 'v_w_igate', 'v_b_igate', 'v_lru_lambda', 'v_w_lru_o', 'v_w_branch_gate', 'v_b_branch_gate', 'v_w_out', 'v_xattn_norm', 'v_mem_norm', 'v_w_xq', 'v_w_xk', 'v_w_xv', 'v_w_xo', 'v_ffn2_norm', 'v_ffn2_w1', 'v_ffn2_w3', 'v_ffn2_w2', 'v_final_norm']
TWIN_OUTPUTS = ['loss', 'grad_x', 'grad_ffn1_norm', 'grad_ffn1_w1', 'grad_ffn1_w3', 'grad_ffn1_w2', 'grad_mix_norm', 'grad_w_in', 'grad_ret_gn', 'grad_w_ret_o', 'grad_conv_w', 'grad_conv_b', 'grad_w_rgate', 'grad_b_rgate', 'grad_w_igate', 'grad_b_igate', 'grad_lru_lambda', 'grad_w_lru_o', 'grad_w_branch_gate', 'grad_b_branch_gate', 'grad_w_out', 'grad_xattn_norm', 'grad_mem_norm', 'grad_w_xq', 'grad_w_xk', 'grad_w_xv', 'grad_w_xo', 'grad_ffn2_norm', 'grad_ffn2_w1', 'grad_ffn2_w3', 'grad_ffn2_w2', 'grad_final_norm', 'delta_ffn1_norm', 'delta_ffn1_w1', 'delta_ffn1_w3', 'delta_ffn1_w2', 'delta_mix_norm', 'delta_w_in', 'delta_ret_gn', 'delta_w_ret_o', 'delta_conv_w', 'delta_conv_b', 'delta_w_rgate', 'delta_b_rgate', 'delta_w_igate', 'delta_b_igate', 'delta_lru_lambda', 'delta_w_lru_o', 'delta_w_branch_gate', 'delta_b_branch_gate', 'delta_w_out', 'delta_xattn_norm', 'delta_mem_norm', 'delta_w_xq', 'delta_w_xk', 'delta_w_xv', 'delta_w_xo', 'delta_ffn2_norm', 'delta_ffn2_w1', 'delta_ffn2_w3', 'delta_ffn2_w2', 'delta_final_norm', 'new_m_ffn1_norm', 'new_m_ffn1_w1', 'new_m_ffn1_w3', 'new_m_ffn1_w2', 'new_m_mix_norm', 'new_m_w_in', 'new_m_ret_gn', 'new_m_w_ret_o', 'new_m_conv_w', 'new_m_conv_b', 'new_m_w_rgate', 'new_m_b_rgate', 'new_m_w_igate', 'new_m_b_igate', 'new_m_lru_lambda', 'new_m_w_lru_o', 'new_m_w_branch_gate', 'new_m_b_branch_gate', 'new_m_w_out', 'new_m_xattn_norm', 'new_m_mem_norm', 'new_m_w_xq', 'new_m_w_xk', 'new_m_w_xv', 'new_m_w_xo', 'new_m_ffn2_norm', 'new_m_ffn2_w1', 'new_m_ffn2_w3', 'new_m_ffn2_w2', 'new_m_final_norm', 'new_v_ffn1_norm', 'new_v_ffn1_w1', 'new_v_ffn1_w3', 'new_v_ffn1_w2', 'new_v_mix_norm', 'new_v_w_in', 'new_v_ret_gn', 'new_v_w_ret_o', 'new_v_conv_w', 'new_v_conv_b', 'new_v_w_rgate', 'new_v_b_rgate', 'new_v_w_igate', 'new_v_b_igate', 'new_v_lru_lambda', 'new_v_w_lru_o', 'new_v_w_branch_gate', 'new_v_b_branch_gate', 'new_v_w_out', 'new_v_xattn_norm', 'new_v_mem_norm', 'new_v_w_xq', 'new_v_w_xk', 'new_v_w_xv', 'new_v_w_xo', 'new_v_ffn2_norm', 'new_v_ffn2_w1', 'new_v_ffn2_w3', 'new_v_ffn2_w2', 'new_v_final_norm']
TWIN_LEAF_KINDS = {'loss': 'loss', 'grad_x': 'grad_x', 'grad_ffn1_norm': 'grad_w', 'grad_ffn1_w1': 'grad_w', 'grad_ffn1_w3': 'grad_w', 'grad_ffn1_w2': 'grad_w', 'grad_mix_norm': 'grad_w', 'grad_w_in': 'grad_w', 'grad_ret_gn': 'grad_w', 'grad_w_ret_o': 'grad_w', 'grad_conv_w': 'grad_w', 'grad_conv_b': 'grad_w', 'grad_w_rgate': 'grad_w', 'grad_b_rgate': 'grad_w', 'grad_w_igate': 'grad_w', 'grad_b_igate': 'grad_w', 'grad_lru_lambda': 'grad_w', 'grad_w_lru_o': 'grad_w', 'grad_w_branch_gate': 'grad_w', 'grad_b_branch_gate': 'grad_w', 'grad_w_out': 'grad_w', 'grad_xattn_norm': 'grad_w', 'grad_mem_norm': 'grad_w', 'grad_w_xq': 'grad_w', 'grad_w_xk': 'grad_w', 'grad_w_xv': 'grad_w', 'grad_w_xo': 'grad_w', 'grad_ffn2_norm': 'grad_w', 'grad_ffn2_w1': 'grad_w', 'grad_ffn2_w3': 'grad_w', 'grad_ffn2_w2': 'grad_w', 'grad_final_norm': 'grad_w', 'delta_ffn1_norm': 'delta_w', 'delta_ffn1_w1': 'delta_w', 'delta_ffn1_w3': 'delta_w', 'delta_ffn1_w2': 'delta_w', 'delta_mix_norm': 'delta_w', 'delta_w_in': 'delta_w', 'delta_ret_gn': 'delta_w', 'delta_w_ret_o': 'delta_w', 'delta_conv_w': 'delta_w', 'delta_conv_b': 'delta_w', 'delta_w_rgate': 'delta_w', 'delta_b_rgate': 'delta_w', 'delta_w_igate': 'delta_w', 'delta_b_igate': 'delta_w', 'delta_lru_lambda': 'delta_w', 'delta_w_lru_o': 'delta_w', 'delta_w_branch_gate': 'delta_w', 'delta_b_branch_gate': 'delta_w', 'delta_w_out': 'delta_w', 'delta_xattn_norm': 'delta_w', 'delta_mem_norm': 'delta_w', 'delta_w_xq': 'delta_w', 'delta_w_xk': 'delta_w', 'delta_w_xv': 'delta_w', 'delta_w_xo': 'delta_w', 'delta_ffn2_norm': 'delta_w', 'delta_ffn2_w1': 'delta_w', 'delta_ffn2_w3': 'delta_w', 'delta_ffn2_w2': 'delta_w', 'delta_final_norm': 'delta_w', 'new_m_ffn1_norm': 'new_m', 'new_m_ffn1_w1': 'new_m', 'new_m_ffn1_w3': 'new_m', 'new_m_ffn1_w2': 'new_m', 'new_m_mix_norm': 'new_m', 'new_m_w_in': 'new_m', 'new_m_ret_gn': 'new_m', 'new_m_w_ret_o': 'new_m', 'new_m_conv_w': 'new_m', 'new_m_conv_b': 'new_m', 'new_m_w_rgate': 'new_m', 'new_m_b_rgate': 'new_m', 'new_m_w_igate': 'new_m', 'new_m_b_igate': 'new_m', 'new_m_lru_lambda': 'new_m', 'new_m_w_lru_o': 'new_m', 'new_m_w_branch_gate': 'new_m', 'new_m_b_branch_gate': 'new_m', 'new_m_w_out': 'new_m', 'new_m_xattn_norm': 'new_m', 'new_m_mem_norm': 'new_m', 'new_m_w_xq': 'new_m', 'new_m_w_xk': 'new_m', 'new_m_w_xv': 'new_m', 'new_m_w_xo': 'new_m', 'new_m_ffn2_norm': 'new_m', 'new_m_ffn2_w1': 'new_m', 'new_m_ffn2_w3': 'new_m', 'new_m_ffn2_w2': 'new_m', 'new_m_final_norm': 'new_m', 'new_v_ffn1_norm': 'new_v', 'new_v_ffn1_w1': 'new_v', 'new_v_ffn1_w3': 'new_v', 'new_v_ffn1_w2': 'new_v', 'new_v_mix_norm': 'new_v', 'new_v_w_in': 'new_v', 'new_v_ret_gn': 'new_v', 'new_v_w_ret_o': 'new_v', 'new_v_conv_w': 'new_v', 'new_v_conv_b': 'new_v', 'new_v_w_rgate': 'new_v', 'new_v_b_rgate': 'new_v', 'new_v_w_igate': 'new_v', 'new_v_b_igate': 'new_v', 'new_v_lru_lambda': 'new_v', 'new_v_w_lru_o': 'new_v', 'new_v_w_branch_gate': 'new_v', 'new_v_b_branch_gate': 'new_v', 'new_v_w_out': 'new_v', 'new_v_xattn_norm': 'new_v', 'new_v_mem_norm': 'new_v', 'new_v_w_xq': 'new_v', 'new_v_w_xk': 'new_v', 'new_v_w_xv': 'new_v', 'new_v_w_xo': 'new_v', 'new_v_ffn2_norm': 'new_v', 'new_v_ffn2_w1': 'new_v', 'new_v_ffn2_w3': 'new_v', 'new_v_ffn2_w2': 'new_v', 'new_v_final_norm': 'new_v'}


def _forward(args):
    return _fwd_reference(*[args[k] for k in FWD_PARAMS])


def _output_shape():
    out = _jax.eval_shape(lambda: _forward(_fwd_setup_inputs(0)))
    return out.shape, out.dtype

N_MICROBATCH = 1
ADAM_LR = 0.001
ADAM_B1 = 0.9
ADAM_B2 = 0.999
ADAM_EPS = 1e-08
ADAM_WD = 0.01
ADAM_STEP = 10
PER_EXAMPLE_BATCH_AXIS = {'x': 0, 'mem': 0, 'loss_target': 0}
SHARED_INPUTS = []
_WEIGHT_DTYPES = {'ffn1_norm': _jnp.float32, 'ffn1_w1': _jnp.float32, 'ffn1_w3': _jnp.float32, 'ffn1_w2': _jnp.float32, 'mix_norm': _jnp.float32, 'w_in': _jnp.float32, 'ret_gn': _jnp.float32, 'w_ret_o': _jnp.float32, 'conv_w': _jnp.float32, 'conv_b': _jnp.float32, 'w_rgate': _jnp.float32, 'b_rgate': _jnp.float32, 'w_igate': _jnp.float32, 'b_igate': _jnp.float32, 'lru_lambda': _jnp.float32, 'w_lru_o': _jnp.float32, 'w_branch_gate': _jnp.float32, 'b_branch_gate': _jnp.float32, 'w_out': _jnp.float32, 'xattn_norm': _jnp.float32, 'mem_norm': _jnp.float32, 'w_xq': _jnp.float32, 'w_xk': _jnp.float32, 'w_xv': _jnp.float32, 'w_xo': _jnp.float32, 'ffn2_norm': _jnp.float32, 'ffn2_w1': _jnp.float32, 'ffn2_w3': _jnp.float32, 'ffn2_w2': _jnp.float32, 'final_norm': _jnp.float32}
MOMENT_SCALE = {'ffn1_norm': 6.131886e-02, 'ffn1_w1': 2.636854e-02, 'ffn1_w3': 2.552816e-02, 'ffn1_w2': 4.239484e-02, 'mix_norm': 8.668683e-02, 'w_in': 3.752309e-02, 'ret_gn': 3.857033e-02, 'w_ret_o': 3.768942e-02, 'conv_w': 2.515471e-02, 'conv_b': 1.123463e-01, 'w_rgate': 5.755872e-03, 'b_rgate': 5.248657e-03, 'w_igate': 9.951900e-03, 'b_igate': 8.880327e-03, 'lru_lambda': 1.038270e-02, 'w_lru_o': 2.318256e-02, 'w_branch_gate': 1.218554e-02, 'b_branch_gate': 1.237924e-02, 'w_out': 4.398011e-02, 'xattn_norm': 1.287910e-02, 'mem_norm': 1.755164e-02, 'w_xq': 1.220299e-02, 'w_xk': 1.212672e-02, 'w_xv': 1.234521e-02, 'w_xo': 1.234515e-02, 'ffn2_norm': 4.578404e-02, 'ffn2_w1': 2.013546e-02, 'ffn2_w3': 1.953833e-02, 'ffn2_w2': 3.247520e-02, 'final_norm': 1.598514e+01}


def _to_microbatches(a, axis):
    t = _jnp.moveaxis(a, axis, 0)
    t = t.reshape((N_MICROBATCH, t.shape[0] // N_MICROBATCH) + t.shape[1:])
    return _jnp.moveaxis(t, 1, axis + 1)


def setup_inputs(seed: int = 0) -> dict:
    inp = _fwd_setup_inputs(seed)
    key = _jax.random.fold_in(_jax.random.key(seed), 7919)
    shape, _ = _output_shape()
    out = dict(inp)
    out["loss_target"] = _jax.random.normal(_jax.random.fold_in(key, 0), shape, _jnp.float32)
    for i, name in enumerate(TWIN_WEIGHTS):
        w = inp[name].astype(_jnp.float32)
        if MOMENT_SCALE is None:
            s = _jnp.sqrt(_jnp.mean(_jnp.square(w)) + 1e-30)
        else:
            s = MOMENT_SCALE[name]
        km, kv = _jax.random.split(_jax.random.fold_in(key, i + 1))
        out[name] = w
        out["m_" + name] = s * _jax.random.normal(km, w.shape, _jnp.float32)
        out["v_" + name] = (s * s) * _jax.random.uniform(kv, w.shape, _jnp.float32, 0.5, 1.5)
    if N_MICROBATCH > 1:
        for name, axis in PER_EXAMPLE_BATCH_AXIS.items():
            out[name] = _to_microbatches(out[name], axis)
    return {'x': out['x'], 'mem': out['mem'], 'ffn1_norm': out['ffn1_norm'], 'ffn1_w1': out['ffn1_w1'], 'ffn1_w3': out['ffn1_w3'], 'ffn1_w2': out['ffn1_w2'], 'mix_norm': out['mix_norm'], 'w_in': out['w_in'], 'ret_gn': out['ret_gn'], 'w_ret_o': out['w_ret_o'], 'conv_w': out['conv_w'], 'conv_b': out['conv_b'], 'w_rgate': out['w_rgate'], 'b_rgate': out['b_rgate'], 'w_igate': out['w_igate'], 'b_igate': out['b_igate'], 'lru_lambda': out['lru_lambda'], 'w_lru_o': out['w_lru_o'], 'w_branch_gate': out['w_branch_gate'], 'b_branch_gate': out['b_branch_gate'], 'w_out': out['w_out'], 'xattn_norm': out['xattn_norm'], 'mem_norm': out['mem_norm'], 'w_xq': out['w_xq'], 'w_xk': out['w_xk'], 'w_xv': out['w_xv'], 'w_xo': out['w_xo'], 'ffn2_norm': out['ffn2_norm'], 'ffn2_w1': out['ffn2_w1'], 'ffn2_w3': out['ffn2_w3'], 'ffn2_w2': out['ffn2_w2'], 'final_norm': out['final_norm'], 'loss_target': out['loss_target'], 'm_ffn1_norm': out['m_ffn1_norm'], 'm_ffn1_w1': out['m_ffn1_w1'], 'm_ffn1_w3': out['m_ffn1_w3'], 'm_ffn1_w2': out['m_ffn1_w2'], 'm_mix_norm': out['m_mix_norm'], 'm_w_in': out['m_w_in'], 'm_ret_gn': out['m_ret_gn'], 'm_w_ret_o': out['m_w_ret_o'], 'm_conv_w': out['m_conv_w'], 'm_conv_b': out['m_conv_b'], 'm_w_rgate': out['m_w_rgate'], 'm_b_rgate': out['m_b_rgate'], 'm_w_igate': out['m_w_igate'], 'm_b_igate': out['m_b_igate'], 'm_lru_lambda': out['m_lru_lambda'], 'm_w_lru_o': out['m_w_lru_o'], 'm_w_branch_gate': out['m_w_branch_gate'], 'm_b_branch_gate': out['m_b_branch_gate'], 'm_w_out': out['m_w_out'], 'm_xattn_norm': out['m_xattn_norm'], 'm_mem_norm': out['m_mem_norm'], 'm_w_xq': out['m_w_xq'], 'm_w_xk': out['m_w_xk'], 'm_w_xv': out['m_w_xv'], 'm_w_xo': out['m_w_xo'], 'm_ffn2_norm': out['m_ffn2_norm'], 'm_ffn2_w1': out['m_ffn2_w1'], 'm_ffn2_w3': out['m_ffn2_w3'], 'm_ffn2_w2': out['m_ffn2_w2'], 'm_final_norm': out['m_final_norm'], 'v_ffn1_norm': out['v_ffn1_norm'], 'v_ffn1_w1': out['v_ffn1_w1'], 'v_ffn1_w3': out['v_ffn1_w3'], 'v_ffn1_w2': out['v_ffn1_w2'], 'v_mix_norm': out['v_mix_norm'], 'v_w_in': out['v_w_in'], 'v_ret_gn': out['v_ret_gn'], 'v_w_ret_o': out['v_w_ret_o'], 'v_conv_w': out['v_conv_w'], 'v_conv_b': out['v_conv_b'], 'v_w_rgate': out['v_w_rgate'], 'v_b_rgate': out['v_b_rgate'], 'v_w_igate': out['v_w_igate'], 'v_b_igate': out['v_b_igate'], 'v_lru_lambda': out['v_lru_lambda'], 'v_w_lru_o': out['v_w_lru_o'], 'v_w_branch_gate': out['v_w_branch_gate'], 'v_b_branch_gate': out['v_b_branch_gate'], 'v_w_out': out['v_w_out'], 'v_xattn_norm': out['v_xattn_norm'], 'v_mem_norm': out['v_mem_norm'], 'v_w_xq': out['v_w_xq'], 'v_w_xk': out['v_w_xk'], 'v_w_xv': out['v_w_xv'], 'v_w_xo': out['v_w_xo'], 'v_ffn2_norm': out['v_ffn2_norm'], 'v_ffn2_w1': out['v_ffn2_w1'], 'v_ffn2_w3': out['v_ffn2_w3'], 'v_ffn2_w2': out['v_ffn2_w2'], 'v_final_norm': out['v_final_norm']}


def _loss(weights, diff, rest, loss_target):
    with _jax.named_scope("forward"):
        args = {**rest, TWIN_DIFF_INPUT: diff, **{k: w.astype(_WEIGHT_DTYPES[k]) for k, w in weights.items()}}
        y = _forward(args)
    with _jax.named_scope("loss_head"):
        err = _jnp.square(y.astype(_jnp.float32) - loss_target)
        return 0.5 * _jnp.sum(_jnp.mean(err, axis=-1)) if err.ndim else 0.5 * err


def _adamw(w, g, m, v):
    m = ADAM_B1 * m + (1.0 - ADAM_B1) * g
    v = ADAM_B2 * v + (1.0 - ADAM_B2) * _jnp.square(g)
    m_hat = m / (1.0 - ADAM_B1 ** ADAM_STEP)
    v_hat = v / (1.0 - ADAM_B2 ** ADAM_STEP)
    delta = -ADAM_LR * (m_hat / (_jnp.sqrt(v_hat) + ADAM_EPS) + ADAM_WD * w)
    return delta, m, v


def reference(x, mem, ffn1_norm, ffn1_w1, ffn1_w3, ffn1_w2, mix_norm, w_in, ret_gn, w_ret_o, conv_w, conv_b, w_rgate, b_rgate, w_igate, b_igate, lru_lambda, w_lru_o, w_branch_gate, b_branch_gate, w_out, xattn_norm, mem_norm, w_xq, w_xk, w_xv, w_xo, ffn2_norm, ffn2_w1, ffn2_w3, ffn2_w2, final_norm, loss_target, m_ffn1_norm, m_ffn1_w1, m_ffn1_w3, m_ffn1_w2, m_mix_norm, m_w_in, m_ret_gn, m_w_ret_o, m_conv_w, m_conv_b, m_w_rgate, m_b_rgate, m_w_igate, m_b_igate, m_lru_lambda, m_w_lru_o, m_w_branch_gate, m_b_branch_gate, m_w_out, m_xattn_norm, m_mem_norm, m_w_xq, m_w_xk, m_w_xv, m_w_xo, m_ffn2_norm, m_ffn2_w1, m_ffn2_w3, m_ffn2_w2, m_final_norm, v_ffn1_norm, v_ffn1_w1, v_ffn1_w3, v_ffn1_w2, v_mix_norm, v_w_in, v_ret_gn, v_w_ret_o, v_conv_w, v_conv_b, v_w_rgate, v_b_rgate, v_w_igate, v_b_igate, v_lru_lambda, v_w_lru_o, v_w_branch_gate, v_b_branch_gate, v_w_out, v_xattn_norm, v_mem_norm, v_w_xq, v_w_xk, v_w_xv, v_w_xo, v_ffn2_norm, v_ffn2_w1, v_ffn2_w3, v_ffn2_w2, v_final_norm):
    given = dict(x=x, mem=mem, ffn1_norm=ffn1_norm, ffn1_w1=ffn1_w1, ffn1_w3=ffn1_w3, ffn1_w2=ffn1_w2, mix_norm=mix_norm, w_in=w_in, ret_gn=ret_gn, w_ret_o=w_ret_o, conv_w=conv_w, conv_b=conv_b, w_rgate=w_rgate, b_rgate=b_rgate, w_igate=w_igate, b_igate=b_igate, lru_lambda=lru_lambda, w_lru_o=w_lru_o, w_branch_gate=w_branch_gate, b_branch_gate=b_branch_gate, w_out=w_out, xattn_norm=xattn_norm, mem_norm=mem_norm, w_xq=w_xq, w_xk=w_xk, w_xv=w_xv, w_xo=w_xo, ffn2_norm=ffn2_norm, ffn2_w1=ffn2_w1, ffn2_w3=ffn2_w3, ffn2_w2=ffn2_w2, final_norm=final_norm, loss_target=loss_target, m_ffn1_norm=m_ffn1_norm, m_ffn1_w1=m_ffn1_w1, m_ffn1_w3=m_ffn1_w3, m_ffn1_w2=m_ffn1_w2, m_mix_norm=m_mix_norm, m_w_in=m_w_in, m_ret_gn=m_ret_gn, m_w_ret_o=m_w_ret_o, m_conv_w=m_conv_w, m_conv_b=m_conv_b, m_w_rgate=m_w_rgate, m_b_rgate=m_b_rgate, m_w_igate=m_w_igate, m_b_igate=m_b_igate, m_lru_lambda=m_lru_lambda, m_w_lru_o=m_w_lru_o, m_w_branch_gate=m_w_branch_gate, m_b_branch_gate=m_b_branch_gate, m_w_out=m_w_out, m_xattn_norm=m_xattn_norm, m_mem_norm=m_mem_norm, m_w_xq=m_w_xq, m_w_xk=m_w_xk, m_w_xv=m_w_xv, m_w_xo=m_w_xo, m_ffn2_norm=m_ffn2_norm, m_ffn2_w1=m_ffn2_w1, m_ffn2_w3=m_ffn2_w3, m_ffn2_w2=m_ffn2_w2, m_final_norm=m_final_norm, v_ffn1_norm=v_ffn1_norm, v_ffn1_w1=v_ffn1_w1, v_ffn1_w3=v_ffn1_w3, v_ffn1_w2=v_ffn1_w2, v_mix_norm=v_mix_norm, v_w_in=v_w_in, v_ret_gn=v_ret_gn, v_w_ret_o=v_w_ret_o, v_conv_w=v_conv_w, v_conv_b=v_conv_b, v_w_rgate=v_w_rgate, v_b_rgate=v_b_rgate, v_w_igate=v_w_igate, v_b_igate=v_b_igate, v_lru_lambda=v_lru_lambda, v_w_lru_o=v_w_lru_o, v_w_branch_gate=v_w_branch_gate, v_b_branch_gate=v_b_branch_gate, v_w_out=v_w_out, v_xattn_norm=v_xattn_norm, v_mem_norm=v_mem_norm, v_w_xq=v_w_xq, v_w_xk=v_w_xk, v_w_xv=v_w_xv, v_w_xo=v_w_xo, v_ffn2_norm=v_ffn2_norm, v_ffn2_w1=v_ffn2_w1, v_ffn2_w3=v_ffn2_w3, v_ffn2_w2=v_ffn2_w2, v_final_norm=v_final_norm)
    weights = {n: given[n] for n in TWIN_WEIGHTS}
    shared = {n: given[n] for n in SHARED_INPUTS}
    per_example = {n: given[n] for n in ['x', 'mem']}
    grad_fn = _jax.value_and_grad(_loss, argnums=(0, 1))

    def one_microbatch(ex, loss_target):
        ex = dict(ex)
        diff = ex.pop(TWIN_DIFF_INPUT)
        return grad_fn(weights, diff, {**shared, **ex}, loss_target)

    if N_MICROBATCH == 1:
        loss, (grad_w, grad_x) = one_microbatch(per_example, given["loss_target"])
    else:
        def body(carry, xs):
            loss_sum, grad_sum = carry
            l_k, (gw_k, gx_k) = one_microbatch(xs[0], xs[1])
            with _jax.named_scope("update"):
                return (loss_sum + l_k, _jax.tree.map(_jnp.add, grad_sum, gw_k)), gx_k

        init = (_jnp.zeros((), _jnp.float32), _jax.tree.map(_jnp.zeros_like, weights))
        (loss, grad_w), grad_x = _jax.lax.scan(body, init, (per_example, given["loss_target"]))
    with _jax.named_scope("update"):
        delta_w, new_m, new_v = {}, {}, {}
        for n in TWIN_WEIGHTS:
            delta_w[n], new_m[n], new_v[n] = _adamw(weights[n], grad_w[n], given["m_" + n], given["v_" + n])
    return (loss, grad_x, *[grad_w[n] for n in TWIN_WEIGHTS], *[delta_w[n] for n in TWIN_WEIGHTS],
            *[new_m[n] for n in TWIN_WEIGHTS], *[new_v[n] for n in TWIN_WEIGHTS])
```

```python
import functools
import math

import numpy as np
import jax
import jax.numpy as jnp
from jax import lax
from jax.experimental import pallas as pl
from jax.experimental.pallas import tpu as pltpu

F32 = jnp.float32
BF16 = jnp.bfloat16
MESH = pl.DeviceIdType.MESH

D = 1024
EPS = 1e-6
RET_HEADS = 4
RET_DK = 128
RET_DV = 256
CHUNK = 128
ROPE_BASE = 10000.0
LRU_BLOCKS = 8
LRU_BLOCK = 128
CONV_TAPS = 4
LRU_C = 8.0
D_FF = 2816
X_HEADS = 4
X_HD = 256
N_CHIPS = 4
FF_BLK = D_FF // N_CHIPS
IN_BLK = 5120 // N_CHIPS
BG_BLK = 2048 // N_CHIPS
SQ_BLK = D // N_CHIPS

ADAM_LR = 0.001
ADAM_B1 = 0.9
ADAM_B2 = 0.999
ADAM_EPS = 1e-08
ADAM_WD = 0.01
ADAM_STEP = 10

VMEM_LIMIT_BYTES = 56 * 1024 * 1024
ROW_TILE = 512
SCAN_TILE = 256

_DN = {
    "nn": (((1,), (0,)), ((), ())),
    "nt": (((1,), (1,)), ((), ())),
    "tn": (((0,), (0,)), ((), ())),
}


def _cparams(n_axes):
    return pltpu.CompilerParams(dimension_semantics=("arbitrary",) * n_axes,
                                vmem_limit_bytes=VMEM_LIMIT_BYTES)


def _dot(a, b, kind):
    if b.ndim == 3:
        b = b.reshape(b.shape[0] * b.shape[1], b.shape[2])
    return lax.dot_general(a.astype(BF16), b.astype(BF16), _DN[kind], preferred_element_type=F32)


def _sigmoid(x):
    return 1.0 / (1.0 + jnp.exp(-x))


def _log1p_pos(e):
    u = 1.0 + e
    return jnp.where(u == 1.0, e, jnp.log(u) * (e / jnp.where(u == 1.0, 1.0, u - 1.0)))


def _expm1(x):
    u = jnp.exp(x)
    lu = jnp.log(u)
    safe = jnp.where(lu == 0.0, 1.0, lu)
    return jnp.where(u == 1.0, x, (u - 1.0) * (x / safe))


def _softplus(z):
    return jnp.maximum(z, 0.0) + _log1p_pos(jnp.exp(-jnp.abs(z)))


_GELU_C = math.sqrt(2.0 / math.pi)


def _gelu_and_grad(x):
    x2 = x * x
    t = jnp.tanh(_GELU_C * (x + 0.044715 * x * x2))
    g = 0.5 * x * (1.0 + t)
    dg = 0.5 * (1.0 + t) + 0.5 * x * (1.0 - t * t) * (_GELU_C * (1.0 + 3.0 * 0.044715 * x2))
    return g, dg


def _rms_fwd(x, g):
    r = lax.rsqrt(jnp.mean(x * x, axis=-1, keepdims=True) + EPS)
    return (x * r) * g


def _rms_bwd(x, g, dh):
    r = lax.rsqrt(jnp.mean(x * x, axis=-1, keepdims=True) + EPS)
    n = x * r
    dyg = dh * g
    dx = r * (dyg - n * jnp.mean(dyg * n, axis=-1, keepdims=True))
    return dx, jnp.sum(dh * n, axis=0, keepdims=True)


def _accumulate(ref, val, first):
    @pl.when(first)
    def _():
        ref[...] = val

    @pl.when(jnp.logical_not(first))
    def _():
        ref[...] += val


def _sds(shape, dtype):
    return jax.ShapeDtypeStruct(tuple(shape), dtype)


def _spec(shape, fn):
    return pl.BlockSpec(tuple(shape), fn)


def _gemm(name, terms, grid, outs, acc_shape, extras=(), epilogue=None):
    kinds = [t[4] for t in terms]
    nt, ne, no = len(terms), len(extras), len(outs)
    nred = grid[-1]
    nax = len(grid)

    def body(*refs):
        trefs = refs[:2 * nt]
        erefs = refs[2 * nt:2 * nt + ne]
        orefs = refs[2 * nt + ne:2 * nt + ne + no]
        ids = [pl.program_id(k) for k in range(nax)]
        tot = None
        for t in range(nt):
            d = _dot(trefs[2 * t][...], trefs[2 * t + 1][...], kinds[t])
            tot = d if tot is None else tot + d

        def finish(acc):
            if epilogue is None:
                orefs[0][...] = acc.astype(orefs[0].dtype)
            else:
                epilogue(acc, erefs, orefs, ids)

        if nred == 1:
            finish(tot)
        else:
            acc_ref = refs[-1]
            r = ids[-1]

            @pl.when(r == 0)
            def _():
                acc_ref[...] = tot

            @pl.when(r > 0)
            def _():
                acc_ref[...] += tot

            @pl.when(r == nred - 1)
            def _():
                finish(acc_ref[...])

    operands, in_specs = [], []
    for a, a_spec, b, b_spec, _ in terms:
        operands += [a, b]
        in_specs += [a_spec, b_spec]
    for e, e_spec in extras:
        operands.append(e)
        in_specs.append(e_spec)
    scratch = [pltpu.VMEM(tuple(acc_shape), F32)] if nred > 1 else []
    res = pl.pallas_call(
        body, name=name, grid=tuple(grid),
        in_specs=in_specs,
        out_specs=[o[1] for o in outs],
        out_shape=[o[0] for o in outs],
        scratch_shapes=scratch,
        compiler_params=_cparams(nax),
    )(*operands)
    return res


def _rowwise(name, fn, ins, outs, grid):
    ni = len(ins)
    nax = len(grid)

    def body(*refs):
        ids = [pl.program_id(k) for k in range(nax)]
        fn(refs[:ni], refs[ni:], ids)

    return pl.pallas_call(
        body, name=name, grid=tuple(grid),
        in_specs=[i[1] for i in ins],
        out_specs=[o[1] for o in outs],
        out_shape=[o[0] for o in outs],
        compiler_params=_cparams(nax),
    )(*[i[0] for i in ins])


def _ffn_up(name, h, wcol, w1_idx, w3_idx):
    T = h.shape[0]
    tm = ROW_TILE

    def body(h_ref, w1_ref, w3_ref, a_ref, b_ref, s_ref):
        hv = h_ref[...]
        a = _dot(hv, w1_ref[...], "nn")
        b = _dot(hv, w3_ref[...], "nn")
        a_ref[...] = a
        b_ref[...] = b
        s_ref[...] = ((a * _sigmoid(a)) * b).astype(BF16)

    blk = _spec((None, tm, FF_BLK), lambda j, i: (j, i, 0))
    return pl.pallas_call(
        body, name=name, grid=(N_CHIPS, T // tm),
        in_specs=[_spec((tm, D), lambda j, i: (i, 0)),
                  _spec((None, None, D, FF_BLK), lambda j, i: (j, w1_idx, 0, 0)),
                  _spec((None, None, D, FF_BLK), lambda j, i: (j, w3_idx, 0, 0))],
        out_specs=[blk, blk, blk],
        out_shape=[_sds((N_CHIPS, T, FF_BLK), F32), _sds((N_CHIPS, T, FF_BLK), F32),
                   _sds((N_CHIPS, T, FF_BLK), BF16)],
        compiler_params=_cparams(2),
    )(h, wcol, wcol)


def _ffn_down(name, s, wrow2, w2_idx, x_res, g_next=None):
    T = x_res.shape[0]
    tm = ROW_TILE
    row = lambda i, j, r: (i, 0)

    def epilogue(acc, erefs, orefs, ids):
        xo = erefs[0][...] + 0.5 * acc
        orefs[0][...] = xo
        if g_next is not None:
            orefs[1][...] = _rms_fwd(xo, erefs[1][...]).astype(BF16)

    extras = [(x_res, _spec((tm, D), row))]
    outs = [(_sds((T, D), F32), _spec((tm, D), row))]
    if g_next is not None:
        extras.append((g_next, _spec((1, D), lambda i, j, r: (0, 0))))
        outs.append((_sds((T, D), BF16), _spec((tm, D), row)))
    return _gemm(
        name,
        [(s, _spec((None, tm, FF_BLK), lambda i, j, r: (r, i, 0)),
          wrow2, _spec((None, None, FF_BLK, D), lambda i, j, r: (r, w2_idx, 0, 0)), "nn")],
        (T // tm, 1, N_CHIPS), outs, (tm, D), extras, epilogue)


def _ffn_bwd_mid(name, dx, wrow2, w2_idx, a, b):
    T = dx.shape[0]
    tm = ROW_TILE

    def body(dx_ref, w2_ref, a_ref, b_ref, dab_ref):
        ds = _dot(0.5 * dx_ref[...], w2_ref[...], "nt")
        av = a_ref[...]
        sg = _sigmoid(av)
        dab_ref[0] = (ds * b_ref[...] * (sg * (1.0 + av * (1.0 - sg)))).astype(BF16)
        dab_ref[1] = (ds * (av * sg)).astype(BF16)

    blk = _spec((None, tm, FF_BLK), lambda j, i: (j, i, 0))
    return pl.pallas_call(
        body, name=name, grid=(N_CHIPS, T // tm),
        in_specs=[_spec((tm, D), lambda j, i: (i, 0)),
                  _spec((None, None, FF_BLK, D), lambda j, i: (j, w2_idx, 0, 0)),
                  blk, blk],
        out_specs=_spec((2, None, tm, FF_BLK), lambda j, i: (0, j, i, 0)),
        out_shape=_sds((2, N_CHIPS, T, FF_BLK), BF16),
        compiler_params=_cparams(2),
    )(dx, wrow2, a, b)


def _rms_bwd_epilogue(acc, erefs, orefs, ids):
    dx, dgp = _rms_bwd(erefs[0][...], erefs[1][...], acc)
    orefs[0][...] = dx + erefs[2][...]
    _accumulate(orefs[1], dgp, ids[0] == 0)


def _rms_bwd_io(x, g, dres, T, tm):
    row = lambda i, j, r: (i, 0)
    vec = lambda i, j, r: (0, 0)
    extras = [(x, _spec((tm, D), row)), (g, _spec((1, D), vec)), (dres, _spec((tm, D), row))]
    outs = [(_sds((T, D), F32), _spec((tm, D), row)), (_sds((1, D), F32), _spec((1, D), vec))]
    return extras, outs


def _ffn_bwd(tag, dx_out, h, a, b, s, wcol, w1_idx, w3_idx, wrow2, w2_idx, x_in, g):
    T = dx_out.shape[0]
    tm = ROW_TILE
    tk = ROW_TILE
    dab = _ffn_bwd_mid(tag + "_bwd_mid", dx_out, wrow2, w2_idx, a, b)

    def half_scale(acc, erefs, orefs, ids):
        orefs[0][...] = 0.5 * acc

    dw2 = _gemm(
        tag + "_dw2",
        [(s, _spec((None, tk, FF_BLK), lambda j, n, r: (j, r, 0)),
          dx_out, _spec((tk, D), lambda j, n, r: (r, 0)), "tn")],
        (N_CHIPS, 1, T // tk),
        [(_sds((N_CHIPS, FF_BLK, D), F32), _spec((None, FF_BLK, D), lambda j, n, r: (j, 0, 0)))],
        (FF_BLK, D), (), half_scale)[0]
    dw13 = _gemm(
        tag + "_dw13",
        [(h, _spec((tk, D), lambda w, j, r: (r, 0)),
          dab, _spec((None, None, tk, FF_BLK), lambda w, j, r: (w, j, r, 0)), "tn")],
        (2, N_CHIPS, T // tk),
        [(_sds((2, N_CHIPS, D, FF_BLK), F32),
          _spec((None, None, D, FF_BLK), lambda w, j, r: (w, j, 0, 0)))],
        (D, FF_BLK))[0]
    extras, outs = _rms_bwd_io(x_in, g, dx_out, T, tm)
    dx_in, dg = _gemm(
        tag + "_dh",
        [(dab, _spec((None, None, tm, FF_BLK), lambda i, j, r: (0, r, i, 0)),
          wcol, _spec((None, None, D, FF_BLK), lambda i, j, r: (r, w1_idx, 0, 0)), "nt"),
         (dab, _spec((None, None, tm, FF_BLK), lambda i, j, r: (1, r, i, 0)),
          wcol, _spec((None, None, D, FF_BLK), lambda i, j, r: (r, w3_idx, 0, 0)), "nt")],
        (T // tm, 1, N_CHIPS), outs, (tm, D), extras, _rms_bwd_epilogue)
    return dx_in, dg, dw13, dw2


def _proj_sq(name, a, wsq, idx, kind, out_dtype=F32, extras=(), epilogue=None, outs=None):
    M = a.shape[0]
    tm = min(ROW_TILE, M)
    if outs is None:
        outs = [(_sds((M, D), out_dtype), _spec((tm, D), lambda i, j, r: (i, 0)))]
    return _gemm(
        name,
        [(a, _spec((tm, D), lambda i, j, r: (i, 0)),
          wsq, _spec((N_CHIPS, None, SQ_BLK, D), lambda i, j, r: (0, idx, 0, 0)), kind)],
        (M // tm, 1, 1), outs, (tm, D), extras, epilogue)


def _dw_sq(name, a, b):
    M = a.shape[0]
    tk = min(ROW_TILE, M)
    return _gemm(
        name,
        [(a, _spec((tk, SQ_BLK), lambda j, n, r: (r, j)), b, _spec((tk, D), lambda j, n, r: (r, 0)), "tn")],
        (N_CHIPS, 1, M // tk),
        [(_sds((N_CHIPS, SQ_BLK, D), F32), _spec((None, SQ_BLK, D), lambda j, n, r: (j, 0, 0)))],
        (SQ_BLK, D))[0]


def _retention_constants(T):
    pos = jnp.arange(T, dtype=F32)
    inv_freq = ROPE_BASE ** (-jnp.arange(0, RET_DK, 2, dtype=F32) / RET_DK)
    ang = pos[:, None] * inv_freq[None, :]
    cosf = jnp.concatenate([jnp.cos(ang), jnp.cos(ang)], axis=1)
    sins = jnp.concatenate([-jnp.sin(ang), jnp.sin(ang)], axis=1)
    lg = jnp.log(1.0 - 2.0 ** (-5.0 - jnp.arange(RET_HEADS, dtype=F32)))
    p = jnp.arange(CHUNK, dtype=F32)
    rel = p[:, None] - p[None, :]
    dmat = jnp.where(rel[None] >= 0, jnp.exp(rel[None] * lg[:, None, None]), 0.0)
    kd = jnp.exp((CHUNK - 1.0 - p)[None, :] * lg[:, None])[:, :, None]
    qd = jnp.exp((p + 1.0)[None, :] * lg[:, None])[:, :, None]
    cd = jnp.exp(CHUNK * lg)[:, None, None]
    return cosf, sins, dmat, kd, qd, cd


def _rot(t, cosv, sinv):
    return t * cosv + pltpu.roll(t, RET_DK // 2, 1) * sinv


def _unrot(t, cosv, sinv):
    return t * cosv - pltpu.roll(t, RET_DK // 2, 1) * sinv


def _ret_const_specs(cm):
    return [
        _spec((CHUNK, RET_DK), lambda h, c: (cm(c), 0)),
        _spec((CHUNK, RET_DK), lambda h, c: (cm(c), 0)),
        _spec((None, CHUNK, CHUNK), lambda h, c: (h, 0, 0)),
        _spec((None, CHUNK, 1), lambda h, c: (h, 0, 0)),
        _spec((None, CHUNK, 1), lambda h, c: (h, 0, 0)),
        _spec((None, 1, 1), lambda h, c: (h, 0, 0)),
    ]


def _ret_fwd(u, consts, ret_gn):
    T = u.shape[0]
    nC = T // CHUNK
    kscale = RET_DK ** -0.5

    def body(q_ref, k_ref, v_ref, g_ref, cos_ref, sin_ref, dm_ref, kd_ref, qd_ref, cd_ref, gn_ref,
             qr_ref, kr_ref, ret_ref, yr_ref, st_ref, state):
        @pl.when(pl.program_id(1) == 0)
        def _():
            state[...] = jnp.zeros_like(state)

        cosv, sinv = cos_ref[...], sin_ref[...]
        q = _rot(q_ref[...], cosv, sinv)
        k = _rot(k_ref[...], cosv, sinv) * kscale
        v = v_ref[...]
        qr_ref[...] = q
        kr_ref[...] = k
        prev = state[...]
        st_ref[...] = prev
        s = _dot(q, k, "nt") * dm_ref[...]
        ret = _dot(s, v, "nn") + _dot(q, prev, "nn") * qd_ref[...]
        state[...] = cd_ref[...] * prev + _dot(k * kd_ref[...], v, "tn")
        ret_ref[...] = ret
        mu = jnp.mean(ret, axis=-1, keepdims=True)
        xc = ret - mu
        yn = xc * lax.rsqrt(jnp.mean(xc * xc, axis=-1, keepdims=True) + EPS)
        g = g_ref[...]
        yr_ref[...] = ((g * _sigmoid(g)) * (yn * gn_ref[...])).astype(BF16)

    cm = lambda c: c
    in_specs = [
        _spec((CHUNK, RET_DK), lambda h, c: (c, h)),
        _spec((CHUNK, RET_DK), lambda h, c: (c, RET_HEADS + h)),
        _spec((CHUNK, RET_DV), lambda h, c: (c, 4 + h)),
        _spec((CHUNK, RET_DV), lambda h, c: (c, 8 + h)),
    ] + _ret_const_specs(cm) + [_spec((1, RET_DV), lambda h, c: (0, h))]
    qk_out = _spec((CHUNK, RET_DK), lambda h, c: (c, h))
    v_out = _spec((CHUNK, RET_DV), lambda h, c: (c, h))
    return pl.pallas_call(
        body, name="ret_fwd", grid=(RET_HEADS, nC),
        in_specs=in_specs,
        out_specs=[qk_out, qk_out, v_out, v_out,
                   _spec((None, None, RET_DK, RET_DV), lambda h, c: (h, c, 0, 0))],
        out_shape=[_sds((T, 512), F32), _sds((T, 512), F32), _sds((T, D), F32), _sds((T, D), BF16),
                   _sds((RET_HEADS, nC, RET_DK, RET_DV), F32)],
        scratch_shapes=[pltpu.VMEM((RET_DK, RET_DV), F32)],
        compiler_params=_cparams(2),
    )(u, u, u, u, *consts, ret_gn)


def _ret_bwd(dyr, ret, u, qr, kr, states, consts, ret_gn):
    T = u.shape[0]
    nC = T // CHUNK
    kscale = RET_DK ** -0.5

    def body(dyr_ref, ret_ref, g_ref, q_ref, k_ref, v_ref, st_ref,
             cos_ref, sin_ref, dm_ref, kd_ref, qd_ref, cd_ref, gn_ref,
             dq_ref, dk_ref, dv_ref, dg_ref, dgn_ref, gstate):
        first = pl.program_id(1) == 0

        @pl.when(first)
        def _():
            gstate[...] = jnp.zeros_like(gstate)

        ret = ret_ref[...]
        mu = jnp.mean(ret, axis=-1, keepdims=True)
        xc = ret - mu
        rs = lax.rsqrt(jnp.mean(xc * xc, axis=-1, keepdims=True) + EPS)
        yn = xc * rs
        gn = gn_ref[...]
        g = g_ref[...]
        sg = _sigmoid(g)
        dyr_v = dyr_ref[...]
        dretn = dyr_v * (g * sg)
        dg_ref[...] = (dyr_v * (yn * gn) * (sg * (1.0 + g * (1.0 - sg)))).astype(BF16)
        _accumulate(dgn_ref, jnp.sum(dretn * yn, axis=0, keepdims=True), first)
        dyn = dretn * gn
        d_o = rs * (dyn - jnp.mean(dyn, axis=-1, keepdims=True)
                    - yn * jnp.mean(dyn * yn, axis=-1, keepdims=True))

        q, k, v = q_ref[...], k_ref[...], v_ref[...]
        dmat, kd, qd = dm_ref[...], kd_ref[...], qd_ref[...]
        prev = st_ref[...]
        gnext = gstate[...]
        s = _dot(q, k, "nt") * dmat
        ds = _dot(d_o, v, "nt") * dmat
        doq = d_o * qd
        dq = _dot(ds, k, "nn") + _dot(doq, prev, "nt")
        dk = _dot(ds, q, "tn") + _dot(v, gnext, "nt") * kd
        dv = _dot(s, d_o, "tn") + _dot(k * kd, gnext, "nn")
        gstate[...] = cd_ref[...] * gnext + _dot(q, doq, "tn")
        cosv, sinv = cos_ref[...], sin_ref[...]
        dq_ref[...] = _unrot(dq, cosv, sinv).astype(BF16)
        dk_ref[...] = _unrot(dk * kscale, cosv, sinv).astype(BF16)
        dv_ref[...] = dv.astype(BF16)

    cm = lambda c: nC - 1 - c
    vspec = lambda off: _spec((CHUNK, RET_DV), lambda h, c: (cm(c), off + h))
    qspec = _spec((CHUNK, RET_DK), lambda h, c: (cm(c), h))
    in_specs = [vspec(0), vspec(0), vspec(8), qspec, qspec, vspec(4),
                _spec((None, None, RET_DK, RET_DV), lambda h, c: (h, cm(c), 0, 0)),
                ] + _ret_const_specs(cm) + [_spec((1, RET_DV), lambda h, c: (0, h))]
    return pl.pallas_call(
        body, name="ret_bwd", grid=(RET_HEADS, nC),
        in_specs=in_specs,
        out_specs=[qspec, qspec, vspec(0), vspec(0), _spec((1, RET_DV), lambda h, c: (0, h))],
        out_shape=[_sds((T, 512), BF16), _sds((T, 512), BF16), _sds((T, D), BF16), _sds((T, D), BF16),
                   _sds((1, D), F32)],
        scratch_shapes=[pltpu.VMEM((RET_DK, RET_DV), F32)],
        compiler_params=_cparams(2),
    )(dyr, ret, u, qr, kr, u, states, *consts, ret_gn)


def _shift_down(x, s):
    rows = lax.broadcasted_iota(jnp.int32, x.shape, 0)
    return jnp.where(rows >= s, pltpu.roll(x, s, 0), 0.0)


def _shift_up(x, s):
    n = x.shape[0]
    rows = lax.broadcasted_iota(jnp.int32, x.shape, 0)
    return jnp.where(rows < n - s, pltpu.roll(x, n - s, 0), 0.0)


def _lru_specs(T):
    col = lambda off: _spec((T, LRU_BLOCK), lambda g: (0, off + g))
    vec = _spec((1, LRU_BLOCK), lambda g: (0, g))
    wblk = _spec((None, LRU_BLOCK, LRU_BLOCK), lambda g: (g, 0, 0))
    cw = _spec((CONV_TAPS, LRU_BLOCK), lambda g: (0, g))
    return col, vec, wblk, cw


def _lru_gates_fwd(u, conv_w, conv_b, w_r, b_r, w_i, b_i, lam):
    T = u.shape[0]
    col, vec, wblk, cw = _lru_specs(T)

    def body(x_ref, cw_ref, cb_ref, wr_ref, br_ref, wi_ref, bi_ref, lam_ref,
             xc_ref, r_ref, i_ref, a_ref, bx_ref):
        x = x_ref[...]
        w = cw_ref[...]
        xc = (_shift_down(x, 3) * w[0:1] + _shift_down(x, 2) * w[1:2] + _shift_down(x, 1) * w[2:3]
              + x * w[3:4] + cb_ref[...])
        r = _sigmoid(_dot(xc, wr_ref[...], "nn") + br_ref[...])
        i = _sigmoid(_dot(xc, wi_ref[...], "nn") + bi_ref[...])
        la = (-LRU_C) * r * _softplus(-lam_ref[...])
        xc_ref[...] = xc
        r_ref[...] = r
        i_ref[...] = i
        a_ref[...] = jnp.exp(la)
        bx_ref[...] = jnp.sqrt(-_expm1(2.0 * la)) * (i * xc)

    out = col(0)
    return pl.pallas_call(
        body, name="lru_gates_fwd", grid=(LRU_BLOCKS,),
        in_specs=[col(24), cw, vec, wblk, vec, wblk, vec, vec],
        out_specs=[out] * 5,
        out_shape=[_sds((T, D), F32)] * 5,
        compiler_params=_cparams(1),
    )(u, conv_w, conv_b, w_r, b_r, w_i, b_i, lam)


def _lru_scan(name, a3, b3, reverse):
    T = a3.shape[0]
    nt = T // SCAN_TILE
    unroll = 8

    def body(a_ref, b_ref, o_ref, carry):
        @pl.when(pl.program_id(0) == 0)
        def _():
            carry[...] = jnp.zeros_like(carry)

        if not reverse:
            def step(t, h):
                h = a_ref[t] * h + b_ref[t]
                o_ref[t] = h
                return h
        else:
            def step(k, c):
                t = SCAN_TILE - 1 - k
                l = b_ref[t] + c
                o_ref[t] = l
                return a_ref[t] * l
        carry[...] = lax.fori_loop(0, SCAN_TILE, step, carry[...], unroll=unroll)

    idx = (lambda i: (nt - 1 - i, 0, 0)) if reverse else (lambda i: (i, 0, 0))
    blk = _spec((SCAN_TILE, LRU_BLOCKS, LRU_BLOCK), idx)
    return pl.pallas_call(
        body, name=name, grid=(nt,),
        in_specs=[blk, blk], out_specs=blk,
        out_shape=_sds((T, LRU_BLOCKS, LRU_BLOCK), F32),
        scratch_shapes=[pltpu.VMEM((LRU_BLOCKS, LRU_BLOCK), F32)],
        compiler_params=_cparams(1),
    )(a3, b3)


def _lru_gates_bwd(lmb, hl, a, r, i, xc, u, conv_w, w_r, w_i, lam):
    T = u.shape[0]
    col, vec, wblk, cw = _lru_specs(T)

    def body(l_ref, h_ref, a_ref, r_ref, i_ref, xc_ref, x_ref, cw_ref, wr_ref, wi_ref, lam_ref,
             dx_ref, dwr_ref, dwi_ref, dvec_ref, dcw_ref):
        l = l_ref[...]
        av, rv, iv, xc = a_ref[...], r_ref[...], i_ref[...], xc_ref[...]
        lam_v = lam_ref[...]
        sp = _softplus(-lam_v)
        la = (-LRU_C) * rv * sp
        mult = jnp.sqrt(-_expm1(2.0 * la))
        da = l * _shift_down(h_ref[...], 1)
        dmult = l * (iv * xc)
        di = l * mult * xc
        dxc = l * mult * iv
        dla = da * av - dmult * (av * av) / mult
        dzr = (dla * ((-LRU_C) * sp)) * rv * (1.0 - rv)
        dzi = di * iv * (1.0 - iv)
        dsp = jnp.sum(dla * ((-LRU_C) * rv), axis=0, keepdims=True)
        dlam = dsp * (-_sigmoid(-lam_v))
        dwr_ref[...] = _dot(xc, dzr, "tn")
        dwi_ref[...] = _dot(xc, dzi, "tn")
        dxc = dxc + _dot(dzr, wr_ref[...], "nt") + _dot(dzi, wi_ref[...], "nt")
        x = x_ref[...]
        w = cw_ref[...]
        dx = (dxc * w[3:4] + _shift_up(dxc, 1) * w[2:3] + _shift_up(dxc, 2) * w[1:2]
              + _shift_up(dxc, 3) * w[0:1])
        dx_ref[...] = dx.astype(BF16)
        dvec_ref[...] = jnp.concatenate(
            [jnp.sum(dzr, axis=0, keepdims=True), jnp.sum(dzi, axis=0, keepdims=True), dlam,
             jnp.sum(dxc, axis=0, keepdims=True)], axis=0)
        dcw_ref[...] = jnp.concatenate(
            [jnp.sum(dxc * _shift_down(x, 3 - tap), axis=0, keepdims=True) if tap < 3
             else jnp.sum(dxc * x, axis=0, keepdims=True) for tap in range(CONV_TAPS)], axis=0)

    c0 = col(0)
    return pl.pallas_call(
        body, name="lru_gates_bwd", grid=(LRU_BLOCKS,),
        in_specs=[c0, c0, c0, c0, c0, c0, col(24), cw, wblk, wblk, vec],
        out_specs=[c0, wblk, wblk, cw, cw],
        out_shape=[_sds((T, D), BF16), _sds((LRU_BLOCKS, LRU_BLOCK, LRU_BLOCK), F32),
                   _sds((LRU_BLOCKS, LRU_BLOCK, LRU_BLOCK), F32), _sds((4, D), F32), _sds((CONV_TAPS, D), F32)],
        compiler_params=_cparams(1),
    )(lmb, hl, a, r, i, xc, u, conv_w, w_r, w_i, lam)


def _xattn_probs(q, k):
    sc = _dot(q, k, "nt") * (X_HD ** -0.5)
    e = jnp.exp(sc - jnp.max(sc, axis=-1, keepdims=True))
    return e / jnp.sum(e, axis=-1, keepdims=True)


def _xattn_fwd(xq, xk, xv):
    T = xq.shape[0]
    tq = ROW_TILE
    M = xk.shape[0]

    def body(q_ref, k_ref, v_ref, o_ref):
        p = _xattn_probs(q_ref[...], k_ref[...])
        o_ref[...] = _dot(p, v_ref[...], "nn").astype(BF16)

    qs = _spec((tq, X_HD), lambda h, i: (i, h))
    kv = _spec((M, X_HD), lambda h, i: (0, h))
    return pl.pallas_call(
        body, name="xattn_fwd", grid=(X_HEADS, T // tq),
        in_specs=[qs, kv, kv], out_specs=qs, out_shape=_sds((T, D), BF16),
        compiler_params=_cparams(2),
    )(xq, xk, xv)


def _xattn_bwd(xq, xk, xv, dxo):
    T = xq.shape[0]
    tq = ROW_TILE
    M = xk.shape[0]

    def body(q_ref, k_ref, v_ref, do_ref, dq_ref, dk_ref, dv_ref):
        first = pl.program_id(1) == 0
        q, k, v, do = q_ref[...], k_ref[...], v_ref[...], do_ref[...]
        p = _xattn_probs(q, k)
        dp = _dot(do, v, "nt")
        ds = p * (dp - jnp.sum(dp * p, axis=-1, keepdims=True)) * (X_HD ** -0.5)
        dq_ref[...] = _dot(ds, k, "nn").astype(BF16)
        _accumulate(dk_ref, _dot(ds, q, "tn"), first)
        _accumulate(dv_ref, _dot(p, do, "tn"), first)

    qs = _spec((tq, X_HD), lambda h, i: (i, h))
    kv = _spec((M, X_HD), lambda h, i: (0, h))
    return pl.pallas_call(
        body, name="xattn_bwd", grid=(X_HEADS, T // tq),
        in_specs=[qs, kv, kv, qs], out_specs=[qs, kv, kv],
        out_shape=[_sds((T, D), BF16), _sds((M, D), F32), _sds((M, D), F32)],
        compiler_params=_cparams(2),
    )(xq, xk, xv, dxo)


def _final_loss(x, g, tgt):
    T = x.shape[0]
    tm = ROW_TILE

    def fn(irefs, orefs, ids):
        xv, gv = irefs[0][...], irefs[1][...]
        err = _rms_fwd(xv, gv) - irefs[2][...]
        lp = 0.5 * jnp.sum(jnp.mean(err * err, axis=-1, keepdims=True), axis=0, keepdims=True)
        first = ids[0] == 0
        _accumulate(orefs[0], jnp.broadcast_to(lp, (1, 128)), first)
        dx, dgp = _rms_bwd(xv, gv, err * (1.0 / D))
        orefs[1][...] = dx
        _accumulate(orefs[2], dgp, first)

    row = _spec((tm, D), lambda i: (i, 0))
    vec = _spec((1, D), lambda i: (0, 0))
    return _rowwise(
        "final_loss", fn, [(x, row), (g, vec), (tgt, row)],
        [(_sds((1, 128), F32), _spec((1, 128), lambda i: (0, 0))), (_sds((T, D), F32), row),
         (_sds((1, D), F32), vec)],
        (T // tm,))


def _adamw(name, w, g, m, v):
    R, C = w.shape
    tr = R
    for cand in (512, 352, 256):
        if R % cand == 0:
            tr = cand
            break
    c1 = 1.0 - ADAM_B1 ** ADAM_STEP
    c2 = 1.0 - ADAM_B2 ** ADAM_STEP

    def fn(irefs, orefs, ids):
        wv, gv, mv, vv = (r[...] for r in irefs)
        mn = ADAM_B1 * mv + (1.0 - ADAM_B1) * gv
        vn = ADAM_B2 * vv + (1.0 - ADAM_B2) * (gv * gv)
        orefs[0][...] = -ADAM_LR * ((mn / c1) / (jnp.sqrt(vn / c2) + ADAM_EPS) + ADAM_WD * wv)
        orefs[1][...] = mn
        orefs[2][...] = vn

    blk = _spec((tr, C), lambda i: (i, 0))
    return _rowwise(name, fn, [(w, blk), (g, blk), (m, blk), (v, blk)],
                    [(_sds((R, C), F32), blk)] * 3, (R // tr,))


def _rmsnorm(name, x, g):
    M = x.shape[0]
    tm = min(ROW_TILE, M)

    def fn(irefs, orefs, ids):
        orefs[0][...] = _rms_fwd(irefs[0][...], irefs[1][...]).astype(BF16)

    row = _spec((tm, D), lambda i: (i, 0))
    return _rowwise(name, fn, [(x, row), (g, _spec((1, D), lambda i: (0, 0)))],
                    [(_sds((M, D), BF16), row)], (M // tm,))[0]


COL_FFN1_W1, COL_FFN1_W3, COL_FFN2_W1, COL_FFN2_W3 = range(4)
ROW_FFN1_W2, ROW_FFN2_W2 = range(2)
SQ_RET_O, SQ_LRU_O, SQ_OUT, SQ_XQ, SQ_XK, SQ_XV, SQ_XO = range(7)


def _local_step(x, mem, tgt, gw, sm):
    T = x.shape[0]
    tm = ROW_TILE
    col, row2, sq, win, wbg = gw["col"], gw["row2"], gw["sq"], gw["win"], gw["wbg"]
    row3 = lambda i, j, r: (i, 0)
    vec3 = lambda i, j, r: (0, 0)
    rowD = _spec((tm, D), row3)
    vecD = _spec((1, D), vec3)

    def residual_norm(acc, erefs, orefs, ids):
        xo = erefs[0][...] + acc
        orefs[0][...] = xo
        orefs[1][...] = _rms_fwd(xo, erefs[1][...]).astype(BF16)

    def res_norm_io(x_res, g):
        return ([(x_res, rowD), (g, vecD)],
                [(_sds((T, D), F32), rowD), (_sds((T, D), BF16), rowD)])

    h1 = _rmsnorm("ffn1_norm", x, sm["ffn1_norm"])
    a1, b1, s1 = _ffn_up("ffn1_up", h1, col, COL_FFN1_W1, COL_FFN1_W3)
    x1, h2 = _ffn_down("ffn1_down", s1, row2, ROW_FFN1_W2, x, sm["mix_norm"])

    u = _gemm(
        "mix_in",
        [(h2, rowD, win, _spec((None, None, D, IN_BLK), lambda i, j, r: (j, 0, 0, 0)), "nn")],
        (T // tm, N_CHIPS, 1),
        [(_sds((T, 5120), F32), _spec((tm, IN_BLK), lambda i, j, r: (i, j)))], (tm, IN_BLK))[0]

    def gate_epilogue(acc, erefs, orefs, ids):
        orefs[0][...] = _sigmoid(acc + erefs[0][...])

    gates = _gemm(
        "mix_gates",
        [(h2, rowD, wbg, _spec((None, None, D, BG_BLK), lambda i, j, r: (j, 0, 0, 0)), "nn")],
        (T // tm, N_CHIPS, 1),
        [(_sds((T, 2 * D), F32), _spec((tm, BG_BLK), lambda i, j, r: (i, j)))], (tm, BG_BLK),
        [(sm["b_branch_gate"], _spec((1, BG_BLK), lambda i, j, r: (0, j)))], gate_epilogue)[0]

    consts = _retention_constants(T)
    qr, kr, ret, yr, states = _ret_fwd(u, consts, sm["ret_gn"])

    xc, rg, ig, av, bx = _lru_gates_fwd(u, gw["conv_w"], sm["conv_b"], sm["w_rgate"], sm["b_rgate"],
                                        sm["w_igate"], sm["b_igate"], sm["lru_lambda"])
    a3 = av.reshape(T, LRU_BLOCKS, LRU_BLOCK)
    hl = _lru_scan("lru_scan_fwd", a3, bx.reshape(T, LRU_BLOCKS, LRU_BLOCK), False).reshape(T, D)

    row1 = _spec((tm, D), lambda i: (i, 0))
    glru1 = _spec((tm, D), lambda i: (i, 4))

    def lru_out(irefs, orefs, ids):
        gl, _ = _gelu_and_grad(irefs[1][...])
        orefs[0][...] = (irefs[0][...] * gl).astype(BF16)

    yl = _rowwise("lru_out", lru_out, [(hl, row1), (u, glru1)], [(_sds((T, D), BF16), row1)], (T // tm,))[0]

    y_ret = _proj_sq("y_ret", yr, sq, SQ_RET_O, "nn")[0]

    def merge_epilogue(acc, erefs, orefs, ids):
        orefs[0][...] = acc
        orefs[1][...] = (erefs[0][...] * erefs[2][...] + erefs[1][...] * acc).astype(BF16)

    y_lru, merged = _proj_sq(
        "y_lru", yl, sq, SQ_LRU_O, "nn",
        extras=[(gates, _spec((tm, D), lambda i, j, r: (i, 0))), (gates, _spec((tm, D), lambda i, j, r: (i, 1))),
                (y_ret, rowD)],
        epilogue=merge_epilogue,
        outs=[(_sds((T, D), F32), rowD), (_sds((T, D), BF16), rowD)])

    ex, ou = res_norm_io(x1, sm["xattn_norm"])
    x2, hq = _proj_sq("mix_out", merged, sq, SQ_OUT, "nn", extras=ex, epilogue=residual_norm, outs=ou)

    m = _rmsnorm("mem_norm", mem, sm["mem_norm"])
    xq = _proj_sq("xq", hq, sq, SQ_XQ, "nn", BF16)[0]
    xk = _proj_sq("xk", m, sq, SQ_XK, "nn", BF16)[0]
    xv = _proj_sq("xv", m, sq, SQ_XV, "nn", BF16)[0]
    xo = _xattn_fwd(xq, xk, xv)
    ex, ou = res_norm_io(x2, sm["ffn2_norm"])
    x3, h3 = _proj_sq("xattn_out", xo, sq, SQ_XO, "nn", extras=ex, epilogue=residual_norm, outs=ou)

    a2, b2, s2 = _ffn_up("ffn2_up", h3, col, COL_FFN2_W1, COL_FFN2_W3)
    x4 = _ffn_down("ffn2_down", s2, row2, ROW_FFN2_W2, x3)[0]
    loss, dx4, dg_final = _final_loss(x4, sm["final_norm"], tgt)

    dx3, dg_ffn2, dw13_2, dw2_2 = _ffn_bwd("ffn2", dx4, h3, a2, b2, s2, col, COL_FFN2_W1, COL_FFN2_W3,
                                           row2, ROW_FFN2_W2, x3, sm["ffn2_norm"])

    dxo = _proj_sq("d_xo", dx3, sq, SQ_XO, "nt", BF16)[0]
    dw_xo = _dw_sq("dw_xo", xo, dx3)
    dxq, dxk, dxv = _xattn_bwd(xq, xk, xv, dxo)
    dw_xq = _dw_sq("dw_xq", hq, dxq)
    ex, ou = _rms_bwd_io(x2, sm["xattn_norm"], dx3, T, tm)
    dx2, dg_xattn = _proj_sq("d_hq", dxq, sq, SQ_XQ, "nt", extras=ex, epilogue=_rms_bwd_epilogue, outs=ou)
    dw_xk = _dw_sq("dw_xk", m, dxk)
    dw_xv = _dw_sq("dw_xv", m, dxv)

    M = mem.shape[0]

    def mem_norm_epilogue(acc, erefs, orefs, ids):
        _, dgp = _rms_bwd(erefs[0][...], erefs[1][...], acc)
        orefs[0][...] = dgp

    wsq_spec = lambda idx: _spec((N_CHIPS, None, SQ_BLK, D), lambda i, j, r: (0, idx, 0, 0))
    memD = _spec((M, D), row3)
    dg_mem = _gemm(
        "d_mem_norm",
        [(dxk, memD, sq, wsq_spec(SQ_XK), "nt"), (dxv, memD, sq, wsq_spec(SQ_XV), "nt")],
        (1, 1, 1), [(_sds((1, D), F32), vecD)], (M, D),
        [(mem, memD), (sm["mem_norm"], vecD)], mem_norm_epilogue)[0]

    def merged_bwd_epilogue(acc, erefs, orefs, ids):
        gr, gl, yrv, ylv = (e[...] for e in erefs)
        orefs[0][...] = (acc * gr).astype(BF16)
        orefs[1][...] = (acc * gl).astype(BF16)
        dgr = acc * yrv * gr * (1.0 - gr)
        dgl = acc * ylv * gl * (1.0 - gl)
        orefs[2][:, :D] = dgr.astype(BF16)
        orefs[2][:, D:] = dgl.astype(BF16)
        dbb = jnp.concatenate([jnp.sum(dgr, axis=0, keepdims=True), jnp.sum(dgl, axis=0, keepdims=True)], axis=1)
        _accumulate(orefs[3], dbb, ids[0] == 0)

    dy_ret, dy_lru, dgpre, db_bg = _proj_sq(
        "d_merged", dx2, sq, SQ_OUT, "nt",
        extras=[(gates, _spec((tm, D), lambda i, j, r: (i, 0))), (gates, _spec((tm, D), lambda i, j, r: (i, 1))),
                (y_ret, rowD), (y_lru, rowD)],
        epilogue=merged_bwd_epilogue,
        outs=[(_sds((T, D), BF16), rowD), (_sds((T, D), BF16), rowD),
              (_sds((T, 2 * D), BF16), _spec((tm, 2 * D), row3)),
              (_sds((1, 2 * D), F32), _spec((1, 2 * D), vec3))])
    dw_out = _dw_sq("dw_out", merged, dx2)
    dyr = _proj_sq("d_yr", dy_ret, sq, SQ_RET_O, "nt")[0]
    dw_ro = _dw_sq("dw_ret_o", yr, dy_ret)
    dyl = _proj_sq("d_yl", dy_lru, sq, SQ_LRU_O, "nt")[0]
    dw_lo = _dw_sq("dw_lru_o", yl, dy_lru)

    dq, dk, dv, dgr, dg_retgn = _ret_bwd(dyr, ret, u, qr, kr, states, consts, sm["ret_gn"])

    def lru_out_bwd(irefs, orefs, ids):
        gl, dgl = _gelu_and_grad(irefs[2][...])
        dyl_v = irefs[0][...]
        orefs[0][...] = dyl_v * gl
        orefs[1][...] = (dyl_v * irefs[1][...] * dgl).astype(BF16)

    dhl, dglru = _rowwise("lru_out_bwd", lru_out_bwd, [(dyl, row1), (hl, row1), (u, glru1)],
                          [(_sds((T, D), F32), row1), (_sds((T, D), BF16), row1)], (T // tm,))
    lmb = _lru_scan("lru_scan_bwd", a3, dhl.reshape(T, LRU_BLOCKS, LRU_BLOCK), True).reshape(T, D)
    dxl, dw_r, dw_i, dvec, dcw = _lru_gates_bwd(lmb, hl, av, rg, ig, xc, u, gw["conv_w"],
                                                sm["w_rgate"], sm["w_igate"], sm["lru_lambda"])

    du = jnp.concatenate([dq, dk, dv, dgr, dxl, dglru], axis=1)
    tk = ROW_TILE
    dw_in = _gemm(
        "dw_in",
        [(h2, _spec((tk, D), lambda j, n, r: (r, 0)), du, _spec((tk, IN_BLK), lambda j, n, r: (r, j)), "tn")],
        (N_CHIPS, 1, T // tk),
        [(_sds((N_CHIPS, D, IN_BLK), F32), _spec((None, D, IN_BLK), lambda j, n, r: (j, 0, 0)))],
        (D, IN_BLK))[0]
    dw_bg = _gemm(
        "dw_bg",
        [(h2, _spec((tk, D), lambda j, n, r: (r, 0)), dgpre, _spec((tk, BG_BLK), lambda j, n, r: (r, j)), "tn")],
        (N_CHIPS, 1, T // tk),
        [(_sds((N_CHIPS, D, BG_BLK), F32), _spec((None, D, BG_BLK), lambda j, n, r: (j, 0, 0)))],
        (D, BG_BLK))[0]
    ex, ou = _rms_bwd_io(x1, sm["mix_norm"], dx2, T, tm)
    dx1, dg_mix = _gemm(
        "d_h2",
        [(du, _spec((tm, IN_BLK), lambda i, j, r: (i, r)),
          win, _spec((None, None, D, IN_BLK), lambda i, j, r: (r, 0, 0, 0)), "nt"),
         (dgpre, _spec((tm, BG_BLK), lambda i, j, r: (i, r)),
          wbg, _spec((None, None, D, BG_BLK), lambda i, j, r: (r, 0, 0, 0)), "nt")],
        (T // tm, 1, N_CHIPS), ou, (tm, D), ex, _rms_bwd_epilogue)

    grad_x, dg_ffn1, dw13_1, dw2_1 = _ffn_bwd("ffn1", dx1, h1, a1, b1, s1, col, COL_FFN1_W1, COL_FFN1_W3,
                                              row2, ROW_FFN1_W2, x, sm["ffn1_norm"])

    big = {
        "ffn1_w13": dw13_1, "ffn1_w2": dw2_1[None], "w_in": dw_in[None], "w_ret_o": dw_ro[None],
        "w_lru_o": dw_lo[None], "w_branch_gate": dw_bg[None], "w_out": dw_out[None], "w_xq": dw_xq[None],
        "w_xk": dw_xk[None], "w_xv": dw_xv[None], "w_xo": dw_xo[None], "ffn2_w13": dw13_2, "ffn2_w2": dw2_2[None],
    }
    small = {
        "ffn1_norm": dg_ffn1, "mix_norm": dg_mix, "ret_gn": dg_retgn, "conv_b": dvec[3:4],
        "b_rgate": dvec[0:1], "b_igate": dvec[1:2], "lru_lambda": dvec[2:3], "xattn_norm": dg_xattn,
        "mem_norm": dg_mem, "ffn2_norm": dg_ffn2, "final_norm": dg_final, "b_branch_gate": db_bg,
        "conv_w": dcw, "w_rgate": dw_r, "w_igate": dw_i,
    }
    return loss, grad_x, big, small


ANY_SPEC = pl.BlockSpec(memory_space=pl.ANY)
VMEM_SPEC = pl.BlockSpec(memory_space=pltpu.VMEM)
N_PEER_CHIPS = N_CHIPS - 1


def _mesh_position():
    x, y, c = lax.axis_index("x"), lax.axis_index("y"), lax.axis_index("c")
    chips = [(1 - x, y), (x, 1 - y), (1 - x, 1 - y)]
    return x, y, c, chips


def _chip_index(x, y):
    return 2 * x + y


def _rows_half(ref, axis, h):
    n = ref.shape[axis] // 2
    idx = [slice(None)] * len(ref.shape)
    idx[axis] = pl.ds(pl.multiple_of(h * n, 16), n)
    return ref.at[tuple(idx)]


def _remote(src, dst, send_sem, recv_sem, device):
    return pltpu.make_async_remote_copy(src_ref=src, dst_ref=dst, send_sem=send_sem, recv_sem=recv_sem,
                                        device_id=device, device_id_type=MESH)


def _allgather_weights(arrs, splits):
    n = len(arrs)

    def body(*refs):
        ins, outs = refs[:n], refs[n:2 * n]
        send_sems, recv_sems, fwd_send, fwd_recv, loc_sems = refs[2 * n:]
        x, y, c, chips = _mesh_position()
        s_me = _chip_index(x, y)
        sibling = (x, y, 1 - c)

        def mine(g):
            return _rows_half(ins[g], 1, c) if splits[g] else ins[g]

        def landing(g, s, h):
            o = outs[g].at[s]
            return _rows_half(o, 1, h) if splits[g] else o

        local = [pltpu.make_async_copy(ins[g], outs[g].at[s_me], loc_sems.at[g]) for g in range(n)]
        for cp in local:
            cp.start()
        sends = []
        for g in range(n):
            for k, chip in enumerate(chips):
                sends.append(_remote(mine(g), landing(g, s_me, c), send_sems.at[3 * g + k],
                                     recv_sems.at[3 * g + k], (*chip, c)))
        for cp in sends:
            cp.start()
        forwards = []
        for g in range(n):
            for k, chip in enumerate(chips):
                s_k = _chip_index(*chip)
                got = landing(g, s_k, c)
                _remote(mine(g), got, send_sems.at[3 * g + k], recv_sems.at[3 * g + k], (*chip, c)).wait_recv()
                if splits[g]:
                    fw = _remote(got, got, fwd_send.at[3 * g + k], fwd_recv.at[3 * g + k], sibling)
                    fw.start()
                    forwards.append(fw)
        for g in range(n):
            if splits[g]:
                for k, chip in enumerate(chips):
                    other = landing(g, _chip_index(*chip), 1 - c)
                    _remote(other, other, fwd_send.at[3 * g + k], fwd_recv.at[3 * g + k], sibling).wait_recv()
        for cp in sends + forwards:
            cp.wait_send()
        for cp in local:
            cp.wait()

    return pl.pallas_call(
        body, name="allgather_weights",
        in_specs=[ANY_SPEC] * n, out_specs=[ANY_SPEC] * n,
        out_shape=[_sds((N_CHIPS,) + a.shape, a.dtype) for a in arrs],
        scratch_shapes=[pltpu.SemaphoreType.DMA((3 * n,)), pltpu.SemaphoreType.DMA((3 * n,)),
                        pltpu.SemaphoreType.DMA((3 * n,)), pltpu.SemaphoreType.DMA((3 * n,)),
                        pltpu.SemaphoreType.DMA((n,))],
    )(*arrs)


def _rs_pair_swap(arrs):
    n = len(arrs)

    def body(*refs):
        ins, outs = refs[:n], refs[n:2 * n]
        send_sems, recv_sems = refs[2 * n:]
        x, y, c, _ = _mesh_position()
        copies = [_remote(_rows_half(ins[a], 2, 1 - c), outs[a], send_sems.at[a], recv_sems.at[a], (x, y, 1 - c))
                  for a in range(n)]
        for cp in copies:
            cp.start()
        for cp in copies:
            cp.wait()

    return pl.pallas_call(
        body, name="rs_pair_swap",
        in_specs=[ANY_SPEC] * n, out_specs=[ANY_SPEC] * n,
        out_shape=[_sds(a.shape[:2] + (a.shape[2] // 2, a.shape[3]), a.dtype) for a in arrs],
        scratch_shapes=[pltpu.SemaphoreType.DMA((n,)), pltpu.SemaphoreType.DMA((n,))],
    )(*arrs)


def _rs_pair_sum(name, full, got, core):
    nw, ns, R, C = full.shape
    half = R // 2

    def body(core_ref, a_ref, b_ref, o_ref):
        o_ref[...] = (a_ref[...] + b_ref[...]).astype(BF16)

    blk = lambda fn: pl.BlockSpec((None, None, half, C), fn)
    return pl.pallas_call(
        body, name=name,
        grid_spec=pltpu.PrefetchScalarGridSpec(
            num_scalar_prefetch=1, grid=(nw, ns),
            in_specs=[blk(lambda w, s, core_ref: (w, s, core_ref[0], 0)), blk(lambda w, s, core_ref: (w, s, 0, 0))],
            out_specs=blk(lambda w, s, core_ref: (w, s, 0, 0))),
        out_shape=_sds((nw, ns, half, C), BF16),
        compiler_params=_cparams(2),
    )(core, full, got)


def _rs_chip_exchange(arrs):
    n = len(arrs)

    def body(*refs):
        ins, outs = refs[:n], refs[n:2 * n]
        send_sems, recv_sems, loc_sems = refs[2 * n:]
        x, y, c, chips = _mesh_position()
        s_me = _chip_index(x, y)
        local = [pltpu.make_async_copy(ins[a].at[:, s_me], outs[a].at[:, s_me], loc_sems.at[a]) for a in range(n)]
        for cp in local:
            cp.start()
        sends = []
        for a in range(n):
            for k, chip in enumerate(chips):
                sends.append(_remote(ins[a].at[:, _chip_index(*chip)], outs[a].at[:, s_me],
                                     send_sems.at[3 * a + k], recv_sems.at[3 * a + k], (*chip, c)))
        for cp in sends:
            cp.start()
        for a in range(n):
            for k, chip in enumerate(chips):
                got = outs[a].at[:, _chip_index(*chip)]
                _remote(got, got, send_sems.at[3 * a + k], recv_sems.at[3 * a + k], (*chip, c)).wait_recv()
        for cp in sends:
            cp.wait_send()
        for cp in local:
            cp.wait()

    return pl.pallas_call(
        body, name="rs_chip_exchange",
        in_specs=[ANY_SPEC] * n, out_specs=[ANY_SPEC] * n,
        out_shape=[_sds(a.shape, a.dtype) for a in arrs],
        scratch_shapes=[pltpu.SemaphoreType.DMA((3 * n,)), pltpu.SemaphoreType.DMA((3 * n,)),
                        pltpu.SemaphoreType.DMA((n,))],
    )(*arrs)


def _rs_chip_sum(name, parts):
    nw, ns, H, C = parts.shape

    def fn(irefs, orefs, ids):
        p = irefs[0]
        tot = p[0].astype(F32)
        for s in range(1, ns):
            tot = tot + p[s].astype(F32)
        orefs[0][...] = tot

    return _rowwise(name, fn, [(parts, _spec((None, ns, H, C), lambda w: (w, 0, 0, 0)))],
                    [(_sds((nw, H, C), F32), _spec((None, H, C), lambda w: (w, 0, 0)))], (nw,))[0]


def _rs_pair_gather(arrs):
    n = len(arrs)

    def body(*refs):
        ins, outs = refs[:n], refs[n:2 * n]
        send_sems, recv_sems, loc_sems = refs[2 * n:]
        x, y, c, _ = _mesh_position()
        local = [pltpu.make_async_copy(ins[a], _rows_half(outs[a], 1, c), loc_sems.at[a]) for a in range(n)]
        sends = [_remote(ins[a], _rows_half(outs[a], 1, c), send_sems.at[a], recv_sems.at[a], (x, y, 1 - c))
                 for a in range(n)]
        for cp in local + sends:
            cp.start()
        for a in range(n):
            other = _rows_half(outs[a], 1, 1 - c)
            _remote(other, other, send_sems.at[a], recv_sems.at[a], (x, y, 1 - c)).wait_recv()
        for cp in sends:
            cp.wait_send()
        for cp in local:
            cp.wait()

    return pl.pallas_call(
        body, name="rs_pair_gather",
        in_specs=[ANY_SPEC] * n, out_specs=[ANY_SPEC] * n,
        out_shape=[_sds((a.shape[0], 2 * a.shape[1], a.shape[2]), a.dtype) for a in arrs],
        scratch_shapes=[pltpu.SemaphoreType.DMA((n,)), pltpu.SemaphoreType.DMA((n,)),
                        pltpu.SemaphoreType.DMA((n,))],
    )(*arrs)


def _small_allreduce(v):
    R, C = v.shape

    def body(v_ref, o_ref, sib_buf, pair_buf, chip_buf, send_sems, recv_sems):
        x, y, c, chips = _mesh_position()
        s_me = _chip_index(x, y)
        swap = _remote(v_ref, sib_buf, send_sems.at[0], recv_sems.at[0], (x, y, 1 - c))
        swap.start()
        swap.wait()
        pair_buf[...] = v_ref[...] + sib_buf[...]
        chip_buf[s_me] = pair_buf[...]
        sends = [_remote(pair_buf, chip_buf.at[s_me], send_sems.at[1 + k], recv_sems.at[1 + k], (*chip, c))
                 for k, chip in enumerate(chips)]
        for cp in sends:
            cp.start()
        for k, chip in enumerate(chips):
            got = chip_buf.at[_chip_index(*chip)]
            _remote(got, got, send_sems.at[1 + k], recv_sems.at[1 + k], (*chip, c)).wait_recv()
        for cp in sends:
            cp.wait_send()
        o_ref[...] = ((chip_buf[0] + chip_buf[1]) + chip_buf[2]) + chip_buf[3]

    return pl.pallas_call(
        body, name="small_allreduce",
        in_specs=[VMEM_SPEC], out_specs=VMEM_SPEC, out_shape=_sds((R, C), F32),
        scratch_shapes=[pltpu.VMEM((R, C), F32), pltpu.VMEM((R, C), F32), pltpu.VMEM((N_CHIPS, R, C), F32),
                        pltpu.SemaphoreType.DMA((1 + N_PEER_CHIPS,)), pltpu.SemaphoreType.DMA((1 + N_PEER_CHIPS,))],
        compiler_params=pltpu.CompilerParams(vmem_limit_bytes=VMEM_LIMIT_BYTES),
    )(v)


BIG_WEIGHTS = ["ffn1_w1", "ffn1_w3", "ffn1_w2", "w_in", "w_ret_o", "w_lru_o", "w_branch_gate", "w_out",
               "w_xq", "w_xk", "w_xv", "w_xo", "ffn2_w1", "ffn2_w3", "ffn2_w2"]
SMALL_LAYOUT = [("ffn1_norm", 1), ("mix_norm", 1), ("ret_gn", 1), ("conv_b", 1), ("b_rgate", 1), ("b_igate", 1),
                ("lru_lambda", 1), ("xattn_norm", 1), ("mem_norm", 1), ("ffn2_norm", 1), ("final_norm", 1),
                ("b_branch_gate", 2), ("conv_w", CONV_TAPS), ("w_rgate", LRU_BLOCK), ("w_igate", LRU_BLOCK)]
SMALL_ROWS = 280
WEIGHT_ORDER = ["ffn1_norm", "ffn1_w1", "ffn1_w3", "ffn1_w2", "mix_norm", "w_in", "ret_gn", "w_ret_o", "conv_w",
                "conv_b", "w_rgate", "b_rgate", "w_igate", "b_igate", "lru_lambda", "w_lru_o", "w_branch_gate",
                "b_branch_gate", "w_out", "xattn_norm", "mem_norm", "w_xq", "w_xk", "w_xv", "w_xo", "ffn2_norm",
                "ffn2_w1", "ffn2_w3", "ffn2_w2", "final_norm"]


def _pack_small(parts):
    rows = [parts[name].reshape(n, D) for name, n in SMALL_LAYOUT]
    used = sum(n for _, n in SMALL_LAYOUT)
    rows.append(jnp.zeros((SMALL_ROWS - used, D), F32))
    return jnp.concatenate(rows, axis=0)


def _unpack_small(packed, shapes):
    out, r = {}, 0
    for name, n in SMALL_LAYOUT:
        out[name] = packed[r:r + n].reshape(shapes[name])
        r += n
    return out


def kernel(x, mem, ffn1_norm, ffn1_w1, ffn1_w3, ffn1_w2, mix_norm, w_in, ret_gn, w_ret_o, conv_w, conv_b, w_rgate, b_rgate, w_igate, b_igate, lru_lambda, w_lru_o, w_branch_gate, b_branch_gate, w_out, xattn_norm, mem_norm, w_xq, w_xk, w_xv, w_xo, ffn2_norm, ffn2_w1, ffn2_w3, ffn2_w2, final_norm, loss_target, m_ffn1_norm, m_ffn1_w1, m_ffn1_w3, m_ffn1_w2, m_mix_norm, m_w_in, m_ret_gn, m_w_ret_o, m_conv_w, m_conv_b, m_w_rgate, m_b_rgate, m_w_igate, m_b_igate, m_lru_lambda, m_w_lru_o, m_w_branch_gate, m_b_branch_gate, m_w_out, m_xattn_norm, m_mem_norm, m_w_xq, m_w_xk, m_w_xv, m_w_xo, m_ffn2_norm, m_ffn2_w1, m_ffn2_w3, m_ffn2_w2, m_final_norm, v_ffn1_norm, v_ffn1_w1, v_ffn1_w3, v_ffn1_w2, v_mix_norm, v_w_in, v_ret_gn, v_w_ret_o, v_conv_w, v_conv_b, v_w_rgate, v_b_rgate, v_w_igate, v_b_igate, v_lru_lambda, v_w_lru_o, v_w_branch_gate, v_b_branch_gate, v_w_out, v_xattn_norm, v_mem_norm, v_w_xq, v_w_xk, v_w_xv, v_w_xo, v_ffn2_norm, v_ffn2_w1, v_ffn2_w3, v_ffn2_w2, v_final_norm):
    given = dict(locals())
    w = {n: given[n] for n in WEIGHT_ORDER}
    mom = {n: given["m_" + n] for n in WEIGHT_ORDER}
    var = {n: given["v_" + n] for n in WEIGHT_ORDER}
    chip = _chip_index(lax.axis_index("x"), lax.axis_index("y"))
    core = lax.axis_index("c").astype(jnp.int32).reshape(1)

    stack = lambda names: jnp.stack([w[n][0] for n in names], axis=0).astype(BF16)
    shards = [stack(["ffn1_w1", "ffn1_w3", "ffn2_w1", "ffn2_w3"]), stack(["ffn1_w2", "ffn2_w2"]),
              stack(["w_ret_o", "w_lru_o", "w_out", "w_xq", "w_xk", "w_xv", "w_xo"]),
              stack(["w_in"]), stack(["w_branch_gate"]), w["conv_w"]]
    col, row2, sq, win, wbg, convg = _allgather_weights(shards, [True, True, True, True, True, False])
    gw = {"col": col, "row2": row2, "sq": sq, "win": win, "wbg": wbg,
          "conv_w": convg[:, 0].transpose(1, 0, 2).reshape(CONV_TAPS, D)}
    sm = {n: w[n] for n in ["ffn1_norm", "mix_norm", "ret_gn", "conv_b", "b_rgate", "b_igate", "lru_lambda",
                            "xattn_norm", "mem_norm", "ffn2_norm", "b_branch_gate"]}
    sm["final_norm"] = w["final_norm"].reshape(1, D)
    sm["w_rgate"] = w["w_rgate"][0]
    sm["w_igate"] = w["w_igate"][0]

    loss_part, grad_x, big, small = _local_step(x[0], mem[0], loss_target[0], gw, sm)
    loss = lax.psum(loss_part[0, 0], ("x", "y", "c"))

    names = list(big)
    fulls = [big[n] for n in names]
    gots = _rs_pair_swap(fulls)
    pair_sums = [_rs_pair_sum("rs_pair_sum_" + n, f, g, core) for n, f, g in zip(names, fulls, gots)]
    by_source = _rs_chip_exchange(pair_sums)
    halves = [_rs_chip_sum("rs_chip_sum_" + n, p) for n, p in zip(names, by_source)]
    reduced = dict(zip(names, _rs_pair_gather(halves)))
    grads = {}
    for n in BIG_WEIGHTS:
        if n in reduced:
            grads[n] = reduced[n][0]
    for tag in ("ffn1", "ffn2"):
        grads[tag + "_w1"] = reduced[tag + "_w13"][0]
        grads[tag + "_w3"] = reduced[tag + "_w13"][1]

    small_shapes = {n: w[n].shape for n, _ in SMALL_LAYOUT}
    small_shapes["conv_w"] = (CONV_TAPS, D)
    small_sum = _small_allreduce(_pack_small(small))

    outs = {}
    for n in BIG_WEIGHTS:
        d, nm, nv = _adamw("adamw_" + n, w[n][0], grads[n], mom[n][0], var[n][0])
        outs[n] = (grads[n][None], d[None], nm[None], nv[None])
    conv_grad = lax.dynamic_slice(small_sum[13:13 + CONV_TAPS], (0, chip * SQ_BLK), (CONV_TAPS, SQ_BLK))
    small_w = {n: w[n] for n, _ in SMALL_LAYOUT}
    small_m = {n: mom[n] for n, _ in SMALL_LAYOUT}
    small_v = {n: var[n] for n, _ in SMALL_LAYOUT}
    pad_cols = lambda a: jnp.pad(a[0], ((0, 0), (0, D - SQ_BLK)))
    for dct in (small_w, small_m, small_v):
        dct["conv_w"] = pad_cols(dct["conv_w"])
    g_pack = lax.dynamic_update_slice(small_sum, jnp.pad(conv_grad, ((0, 0), (0, D - SQ_BLK))), (13, 0))
    d_pack, m_pack, v_pack = _adamw("adamw_small", _pack_small(small_w), g_pack, _pack_small(small_m),
                                    _pack_small(small_v))
    unpacked = [_unpack_small(p, small_shapes) for p in (g_pack, d_pack, m_pack, v_pack)]
    for n, _ in SMALL_LAYOUT:
        if n == "conv_w":
            outs[n] = tuple(u[n][:, :SQ_BLK][None] for u in unpacked)
        else:
            outs[n] = tuple(u[n] for u in unpacked)

    result = [loss, grad_x[None]]
    for k in range(4):
        result += [outs[n][k] for n in WEIGHT_ORDER]
    return tuple(result)
```

```python
import functools
import math

import numpy as np
import jax
import jax.numpy as jnp
from jax import lax
from jax.experimental import pallas as pl
from jax.experimental.pallas import tpu as pltpu

F32 = jnp.float32
BF16 = jnp.bfloat16
MESH = pl.DeviceIdType.MESH

D = 1024
EPS = 1e-6
RET_HEADS = 4
RET_DK = 128
RET_DV = 256
CHUNK = 128
ROPE_BASE = 10000.0
LRU_BLOCKS = 8
LRU_BLOCK = 128
CONV_TAPS = 4
LRU_C = 8.0
D_FF = 2816
X_HEADS = 4
X_HD = 256
N_CHIPS = 4
FF_BLK = D_FF // N_CHIPS
IN_BLK = 5120 // N_CHIPS
BG_BLK = 2048 // N_CHIPS
SQ_BLK = D // N_CHIPS

ADAM_LR = 0.001
ADAM_B1 = 0.9
ADAM_B2 = 0.999
ADAM_EPS = 1e-08
ADAM_WD = 0.01
ADAM_STEP = 10

VMEM_LIMIT_BYTES = 56 * 1024 * 1024
ROW_TILE = 512
SCAN_TILE = 256

_DN = {
    "nn": (((1,), (0,)), ((), ())),
    "nt": (((1,), (1,)), ((), ())),
    "tn": (((0,), (0,)), ((), ())),
}


def _cparams(n_axes):
    return pltpu.CompilerParams(dimension_semantics=("arbitrary",) * n_axes,
                                vmem_limit_bytes=VMEM_LIMIT_BYTES)


def _dot(a, b, kind):
    if b.ndim == 3:
        b = b.reshape(b.shape[0] * b.shape[1], b.shape[2])
    return lax.dot_general(a.astype(BF16), b.astype(BF16), _DN[kind], preferred_element_type=F32)


def _sigmoid(x):
    return 1.0 / (1.0 + jnp.exp(-x))


def _log1p_pos(e):
    u = 1.0 + e
    return jnp.where(u == 1.0, e, jnp.log(u) * (e / jnp.where(u == 1.0, 1.0, u - 1.0)))


def _expm1(x):
    u = jnp.exp(x)
    lu = jnp.log(u)
    safe = jnp.where(lu == 0.0, 1.0, lu)
    return jnp.where(u == 1.0, x, (u - 1.0) * (x / safe))


def _softplus(z):
    return jnp.maximum(z, 0.0) + _log1p_pos(jnp.exp(-jnp.abs(z)))


_GELU_C = math.sqrt(2.0 / math.pi)


def _gelu_and_grad(x):
    x2 = x * x
    t = jnp.tanh(_GELU_C * (x + 0.044715 * x * x2))
    g = 0.5 * x * (1.0 + t)
    dg = 0.5 * (1.0 + t) + 0.5 * x * (1.0 - t * t) * (_GELU_C * (1.0 + 3.0 * 0.044715 * x2))
    return g, dg


def _rms_fwd(x, g):
    r = lax.rsqrt(jnp.mean(x * x, axis=-1, keepdims=True) + EPS)
    return (x * r) * g


def _rms_bwd(x, g, dh):
    r = lax.rsqrt(jnp.mean(x * x, axis=-1, keepdims=True) + EPS)
    n = x * r
    dyg = dh * g
    dx = r * (dyg - n * jnp.mean(dyg * n, axis=-1, keepdims=True))
    return dx, jnp.sum(dh * n, axis=0, keepdims=True)


def _accumulate(ref, val, first):
    @pl.when(first)
    def _():
        ref[...] = val

    @pl.when(jnp.logical_not(first))
    def _():
        ref[...] += val


def _sds(shape, dtype):
    return jax.ShapeDtypeStruct(tuple(shape), dtype)


def _spec(shape, fn):
    return pl.BlockSpec(tuple(shape), fn)


def _gemm(name, terms, grid, outs, acc_shape, extras=(), epilogue=None):
    kinds = [t[4] for t in terms]
    nt, ne, no = len(terms), len(extras), len(outs)
    nred = grid[-1]
    nax = len(grid)

    def body(*refs):
        trefs = refs[:2 * nt]
        erefs = refs[2 * nt:2 * nt + ne]
        orefs = refs[2 * nt + ne:2 * nt + ne + no]
        ids = [pl.program_id(k) for k in range(nax)]
        tot = None
        for t in range(nt):
            d = _dot(trefs[2 * t][...], trefs[2 * t + 1][...], kinds[t])
            tot = d if tot is None else tot + d

        def finish(acc):
            if epilogue is None:
                orefs[0][...] = acc.astype(orefs[0].dtype)
            else:
                epilogue(acc, erefs, orefs, ids)

        if nred == 1:
            finish(tot)
        else:
            acc_ref = refs[-1]
            r = ids[-1]

            @pl.when(r == 0)
            def _():
                acc_ref[...] = tot

            @pl.when(r > 0)
            def _():
                acc_ref[...] += tot

            @pl.when(r == nred - 1)
            def _():
                finish(acc_ref[...])

    operands, in_specs = [], []
    for a, a_spec, b, b_spec, _ in terms:
        operands += [a, b]
        in_specs += [a_spec, b_spec]
    for e, e_spec in extras:
        operands.append(e)
        in_specs.append(e_spec)
    scratch = [pltpu.VMEM(tuple(acc_shape), F32)] if nred > 1 else []
    res = pl.pallas_call(
        body, name=name, grid=tuple(grid),
        in_specs=in_specs,
        out_specs=[o[1] for o in outs],
        out_shape=[o[0] for o in outs],
        scratch_shapes=scratch,
        compiler_params=_cparams(nax),
    )(*operands)
    return res


def _rowwise(name, fn, ins, outs, grid):
    ni = len(ins)
    nax = len(grid)

    def body(*refs):
        ids = [pl.program_id(k) for k in range(nax)]
        fn(refs[:ni], refs[ni:], ids)

    return pl.pallas_call(
        body, name=name, grid=tuple(grid),
        in_specs=[i[1] for i in ins],
        out_specs=[o[1] for o in outs],
        out_shape=[o[0] for o in outs],
        compiler_params=_cparams(nax),
    )(*[i[0] for i in ins])


def _ffn_up(name, h, wcol, w1_idx, w3_idx):
    T = h.shape[0]
    tm = ROW_TILE

    def body(h_ref, w1_ref, w3_ref, a_ref, b_ref, s_ref):
        hv = h_ref[...]
        a = _dot(hv, w1_ref[...], "nn")
        b = _dot(hv, w3_ref[...], "nn")
        a_ref[...] = a
        b_ref[...] = b
        s_ref[...] = ((a * _sigmoid(a)) * b).astype(BF16)

    blk = _spec((None, tm, FF_BLK), lambda j, i: (j, i, 0))
    return pl.pallas_call(
        body, name=name, grid=(N_CHIPS, T // tm),
        in_specs=[_spec((tm, D), lambda j, i: (i, 0)),
                  _spec((None, None, D, FF_BLK), lambda j, i: (j, w1_idx, 0, 0)),
                  _spec((None, None, D, FF_BLK), lambda j, i: (j, w3_idx, 0, 0))],
        out_specs=[blk, blk, blk],
        out_shape=[_sds((N_CHIPS, T, FF_BLK), F32), _sds((N_CHIPS, T, FF_BLK), F32),
                   _sds((N_CHIPS, T, FF_BLK), BF16)],
        compiler_params=_cparams(2),
    )(h, wcol, wcol)


def _ffn_down(name, s, wrow2, w2_idx, x_res, g_next=None):
    T = x_res.shape[0]
    tm = ROW_TILE
    row = lambda i, j, r: (i, 0)

    def epilogue(acc, erefs, orefs, ids):
        xo = erefs[0][...] + 0.5 * acc
        orefs[0][...] = xo
        if g_next is not None:
            orefs[1][...] = _rms_fwd(xo, erefs[1][...]).astype(BF16)

    extras = [(x_res, _spec((tm, D), row))]
    outs = [(_sds((T, D), F32), _spec((tm, D), row))]
    if g_next is not None:
        extras.append((g_next, _spec((1, D), lambda i, j, r: (0, 0))))
        outs.append((_sds((T, D), BF16), _spec((tm, D), row)))
    return _gemm(
        name,
        [(s, _spec((None, tm, FF_BLK), lambda i, j, r: (r, i, 0)),
          wrow2, _spec((None, None, FF_BLK, D), lambda i, j, r: (r, w2_idx, 0, 0)), "nn")],
        (T // tm, 1, N_CHIPS), outs, (tm, D), extras, epilogue)


def _ffn_bwd_mid(name, dx, wrow2, w2_idx, a, b):
    T = dx.shape[0]
    tm = ROW_TILE

    def body(dx_ref, w2_ref, a_ref, b_ref, dab_ref):
        ds = _dot(0.5 * dx_ref[...], w2_ref[...], "nt")
        av = a_ref[...]
        sg = _sigmoid(av)
        dab_ref[0] = (ds * b_ref[...] * (sg * (1.0 + av * (1.0 - sg)))).astype(BF16)
        dab_ref[1] = (ds * (av * sg)).astype(BF16)

    blk = _spec((None, tm, FF_BLK), lambda j, i: (j, i, 0))
    return pl.pallas_call(
        body, name=name, grid=(N_CHIPS, T // tm),
        in_specs=[_spec((tm, D), lambda j, i: (i, 0)),
                  _spec((None, None, FF_BLK, D), lambda j, i: (j, w2_idx, 0, 0)),
                  blk, blk],
        out_specs=_spec((2, None, tm, FF_BLK), lambda j, i: (0, j, i, 0)),
        out_shape=_sds((2, N_CHIPS, T, FF_BLK), BF16),
        compiler_params=_cparams(2),
    )(dx, wrow2, a, b)


def _rms_bwd_epilogue(acc, erefs, orefs, ids):
    dx, dgp = _rms_bwd(erefs[0][...], erefs[1][...], acc)
    orefs[0][...] = dx + erefs[2][...]
    _accumulate(orefs[1], dgp, ids[0] == 0)


def _rms_bwd_io(x, g, dres, T, tm):
    row = lambda i, j, r: (i, 0)
    vec = lambda i, j, r: (0, 0)
    extras = [(x, _spec((tm, D), row)), (g, _spec((1, D), vec)), (dres, _spec((tm, D), row))]
    outs = [(_sds((T, D), F32), _spec((tm, D), row)), (_sds((1, D), F32), _spec((1, D), vec))]
    return extras, outs


def _ffn_bwd(tag, dx_out, h, a, b, s, wcol, w1_idx, w3_idx, wrow2, w2_idx, x_in, g):
    T = dx_out.shape[0]
    tm = ROW_TILE
    tk = ROW_TILE
    dab = _ffn_bwd_mid(tag + "_bwd_mid", dx_out, wrow2, w2_idx, a, b)

    def half_scale(acc, erefs, orefs, ids):
        orefs[0][...] = 0.5 * acc

    dw2 = _gemm(
        tag + "_dw2",
        [(s, _spec((None, tk, FF_BLK), lambda j, n, r: (j, r, 0)),
          dx_out, _spec((tk, D), lambda j, n, r: (r, 0)), "tn")],
        (N_CHIPS, 1, T // tk),
        [(_sds((N_CHIPS, FF_BLK, D), F32), _spec((None, FF_BLK, D), lambda j, n, r: (j, 0, 0)))],
        (FF_BLK, D), (), half_scale)[0]
    dw13 = _gemm(
        tag + "_dw13",
        [(h, _spec((tk, D), lambda w, j, r: (r, 0)),
          dab, _spec((None, None, tk, FF_BLK), lambda w, j, r: (w, j, r, 0)), "tn")],
        (2, N_CHIPS, T // tk),
        [(_sds((2, N_CHIPS, D, FF_BLK), F32),
          _spec((None, None, D, FF_BLK), lambda w, j, r: (w, j, 0, 0)))],
        (D, FF_BLK))[0]
    extras, outs = _rms_bwd_io(x_in, g, dx_out, T, tm)
    dx_in, dg = _gemm(
        tag + "_dh",
        [(dab, _spec((None, None, tm, FF_BLK), lambda i, j, r: (0, r, i, 0)),
          wcol, _spec((None, None, D, FF_BLK), lambda i, j, r: (r, w1_idx, 0, 0)), "nt"),
         (dab, _spec((None, None, tm, FF_BLK), lambda i, j, r: (1, r, i, 0)),
          wcol, _spec((None, None, D, FF_BLK), lambda i, j, r: (r, w3_idx, 0, 0)), "nt")],
        (T // tm, 1, N_CHIPS), outs, (tm, D), extras, _rms_bwd_epilogue)
    return dx_in, dg, dw13, dw2


def _proj_sq(name, a, wsq, idx, kind, out_dtype=F32, extras=(), epilogue=None, outs=None):
    M = a.shape[0]
    tm = min(ROW_TILE, M)
    if outs is None:
        outs = [(_sds((M, D), out_dtype), _spec((tm, D), lambda i, j, r: (i, 0)))]
    return _gemm(
        name,
        [(a, _spec((tm, D), lambda i, j, r: (i, 0)),
          wsq, _spec((N_CHIPS, None, SQ_BLK, D), lambda i, j, r: (0, idx, 0, 0)), kind)],
        (M // tm, 1, 1), outs, (tm, D), extras, epilogue)


def _dw_sq(name, a, b):
    M = a.shape[0]
    tk = min(ROW_TILE, M)
    return _gemm(
        name,
        [(a, _spec((tk, SQ_BLK), lambda j, n, r: (r, j)), b, _spec((tk, D), lambda j, n, r: (r, 0)), "tn")],
        (N_CHIPS, 1, M // tk),
        [(_sds((N_CHIPS, SQ_BLK, D), F32), _spec((None, SQ_BLK, D), lambda j, n, r: (j, 0, 0)))],
        (SQ_BLK, D))[0]


def _retention_constants(T):
    pos = jnp.arange(T, dtype=F32)
    inv_freq = ROPE_BASE ** (-jnp.arange(0, RET_DK, 2, dtype=F32) / RET_DK)
    ang = pos[:, None] * inv_freq[None, :]
    cosf = jnp.concatenate([jnp.cos(ang), jnp.cos(ang)], axis=1)
    sins = jnp.concatenate([-jnp.sin(ang), jnp.sin(ang)], axis=1)
    lg = jnp.log(1.0 - 2.0 ** (-5.0 - jnp.arange(RET_HEADS, dtype=F32)))
    p = jnp.arange(CHUNK, dtype=F32)
    rel = p[:, None] - p[None, :]
    dmat = jnp.where(rel[None] >= 0, jnp.exp(rel[None] * lg[:, None, None]), 0.0)
    kd = jnp.exp((CHUNK - 1.0 - p)[None, :] * lg[:, None])[:, :, None]
    qd = jnp.exp((p + 1.0)[None, :] * lg[:, None])[:, :, None]
    cd = jnp.exp(CHUNK * lg)[:, None, None]
    return cosf, sins, dmat, kd, qd, cd


def _rot(t, cosv, sinv):
    return t * cosv + pltpu.roll(t, RET_DK // 2, 1) * sinv


def _unrot(t, cosv, sinv):
    return t * cosv - pltpu.roll(t, RET_DK // 2, 1) * sinv


def _ret_const_specs(cm):
    return [
        _spec((CHUNK, RET_DK), lambda h, c: (cm(c), 0)),
        _spec((CHUNK, RET_DK), lambda h, c: (cm(c), 0)),
        _spec((None, CHUNK, CHUNK), lambda h, c: (h, 0, 0)),
        _spec((None, CHUNK, 1), lambda h, c: (h, 0, 0)),
        _spec((None, CHUNK, 1), lambda h, c: (h, 0, 0)),
        _spec((None, 1, 1), lambda h, c: (h, 0, 0)),
    ]


def _ret_fwd(u, consts, ret_gn):
    T = u.shape[0]
    nC = T // CHUNK
    kscale = RET_DK ** -0.5

    def body(q_ref, k_ref, v_ref, g_ref, cos_ref, sin_ref, dm_ref, kd_ref, qd_ref, cd_ref, gn_ref,
             qr_ref, kr_ref, ret_ref, yr_ref, st_ref, state):
        @pl.when(pl.program_id(1) == 0)
        def _():
            state[...] = jnp.zeros_like(state)

        cosv, sinv = cos_ref[...], sin_ref[...]
        q = _rot(q_ref[...], cosv, sinv)
        k = _rot(k_ref[...], cosv, sinv) * kscale
        v = v_ref[...]
        qr_ref[...] = q
        kr_ref[...] = k
        prev = state[...]
        st_ref[...] = prev
        s = _dot(q, k, "nt") * dm_ref[...]
        ret = _dot(s, v, "nn") + _dot(q, prev, "nn") * qd_ref[...]
        state[...] = cd_ref[...] * prev + _dot(k * kd_ref[...], v, "tn")
        ret_ref[...] = ret
        mu = jnp.mean(ret, axis=-1, keepdims=True)
        xc = ret - mu
        yn = xc * lax.rsqrt(jnp.mean(xc * xc, axis=-1, keepdims=True) + EPS)
        g = g_ref[...]
        yr_ref[...] = ((g * _sigmoid(g)) * (yn * gn_ref[...])).astype(BF16)

    cm = lambda c: c
    in_specs = [
        _spec((CHUNK, RET_DK), lambda h, c: (c, h)),
        _spec((CHUNK, RET_DK), lambda h, c: (c, RET_HEADS + h)),
        _spec((CHUNK, RET_DV), lambda h, c: (c, 4 + h)),
        _spec((CHUNK, RET_DV), lambda h, c: (c, 8 + h)),
    ] + _ret_const_specs(cm) + [_spec((1, RET_DV), lambda h, c: (0, h))]
    qk_out = _spec((CHUNK, RET_DK), lambda h, c: (c, h))
    v_out = _spec((CHUNK, RET_DV), lambda h, c: (c, h))
    return pl.pallas_call(
        body, name="ret_fwd", grid=(RET_HEADS, nC),
        in_specs=in_specs,
        out_specs=[qk_out, qk_out, v_out, v_out,
                   _spec((None, None, RET_DK, RET_DV), lambda h, c: (h, c, 0, 0))],
        out_shape=[_sds((T, 512), F32), _sds((T, 512), F32), _sds((T, D), F32), _sds((T, D), BF16),
                   _sds((RET_HEADS, nC, RET_DK, RET_DV), F32)],
        scratch_shapes=[pltpu.VMEM((RET_DK, RET_DV), F32)],
        compiler_params=_cparams(2),
    )(u, u, u, u, *consts, ret_gn)


def _ret_bwd(dyr, ret, u, qr, kr, states, consts, ret_gn):
    T = u.shape[0]
    nC = T // CHUNK
    kscale = RET_DK ** -0.5

    def body(dyr_ref, ret_ref, g_ref, q_ref, k_ref, v_ref, st_ref,
             cos_ref, sin_ref, dm_ref, kd_ref, qd_ref, cd_ref, gn_ref,
             dq_ref, dk_ref, dv_ref, dg_ref, dgn_ref, gstate):
        first = pl.program_id(1) == 0

        @pl.when(first)
        def _():
            gstate[...] = jnp.zeros_like(gstate)

        ret = ret_ref[...]
        mu = jnp.mean(ret, axis=-1, keepdims=True)
        xc = ret - mu
        rs = lax.rsqrt(jnp.mean(xc * xc, axis=-1, keepdims=True) + EPS)
        yn = xc * rs
        gn = gn_ref[...]
        g = g_ref[...]
        sg = _sigmoid(g)
        dyr_v = dyr_ref[...]
        dretn = dyr_v * (g * sg)
        dg_ref[...] = (dyr_v * (yn * gn) * (sg * (1.0 + g * (1.0 - sg)))).astype(BF16)
        _accumulate(dgn_ref, jnp.sum(dretn * yn, axis=0, keepdims=True), first)
        dyn = dretn * gn
        d_o = rs * (dyn - jnp.mean(dyn, axis=-1, keepdims=True)
                    - yn * jnp.mean(dyn * yn, axis=-1, keepdims=True))

        q, k, v = q_ref[...], k_ref[...], v_ref[...]
        dmat, kd, qd = dm_ref[...], kd_ref[...], qd_ref[...]
        prev = st_ref[...]
        gnext = gstate[...]
        s = _dot(q, k, "nt") * dmat
        ds = _dot(d_o, v, "nt") * dmat
        doq = d_o * qd
        dq = _dot(ds, k, "nn") + _dot(doq, prev, "nt")
        dk = _dot(ds, q, "tn") + _dot(v, gnext, "nt") * kd
        dv = _dot(s, d_o, "tn") + _dot(k * kd, gnext, "nn")
        gstate[...] = cd_ref[...] * gnext + _dot(q, doq, "tn")
        cosv, sinv = cos_ref[...], sin_ref[...]
        dq_ref[...] = _unrot(dq, cosv, sinv).astype(BF16)
        dk_ref[...] = _unrot(dk * kscale, cosv, sinv).astype(BF16)
        dv_ref[...] = dv.astype(BF16)

    cm = lambda c: nC - 1 - c
    vspec = lambda off: _spec((CHUNK, RET_DV), lambda h, c: (cm(c), off + h))
    qspec = _spec((CHUNK, RET_DK), lambda h, c: (cm(c), h))
    in_specs = [vspec(0), vspec(0), vspec(8), qspec, qspec, vspec(4),
                _spec((None, None, RET_DK, RET_DV), lambda h, c: (h, cm(c), 0, 0)),
                ] + _ret_const_specs(cm) + [_spec((1, RET_DV), lambda h, c: (0, h))]
    return pl.pallas_call(
        body, name="ret_bwd", grid=(RET_HEADS, nC),
        in_specs=in_specs,
        out_specs=[qspec, qspec, vspec(0), vspec(0), _spec((1, RET_DV), lambda h, c: (0, h))],
        out_shape=[_sds((T, 512), BF16), _sds((T, 512), BF16), _sds((T, D), BF16), _sds((T, D), BF16),
                   _sds((1, D), F32)],
        scratch_shapes=[pltpu.VMEM((RET_DK, RET_DV), F32)],
        compiler_params=_cparams(2),
    )(dyr, ret, u, qr, kr, u, states, *consts, ret_gn)


def _shift_down(x, s):
    rows = lax.broadcasted_iota(jnp.int32, x.shape, 0)
    return jnp.where(rows >= s, pltpu.roll(x, s, 0), 0.0)


def _shift_up(x, s):
    n = x.shape[0]
    rows = lax.broadcasted_iota(jnp.int32, x.shape, 0)
    return jnp.where(rows < n - s, pltpu.roll(x, n - s, 0), 0.0)


def _lru_specs(T):
    col = lambda off: _spec((T, LRU_BLOCK), lambda g: (0, off + g))
    vec = _spec((1, LRU_BLOCK), lambda g: (0, g))
    wblk = _spec((None, LRU_BLOCK, LRU_BLOCK), lambda g: (g, 0, 0))
    cw = _spec((CONV_TAPS, LRU_BLOCK), lambda g: (0, g))
    return col, vec, wblk, cw


def _lru_gates_fwd(u, conv_w, conv_b, w_r, b_r, w_i, b_i, lam):
    T = u.shape[0]
    col, vec, wblk, cw = _lru_specs(T)

    def body(x_ref, cw_ref, cb_ref, wr_ref, br_ref, wi_ref, bi_ref, lam_ref,
             xc_ref, r_ref, i_ref, a_ref, bx_ref):
        x = x_ref[...]
        w = cw_ref[...]
        xc = (_shift_down(x, 3) * w[0:1] + _shift_down(x, 2) * w[1:2] + _shift_down(x, 1) * w[2:3]
              + x * w[3:4] + cb_ref[...])
        r = _sigmoid(_dot(xc, wr_ref[...], "nn") + br_ref[...])
        i = _sigmoid(_dot(xc, wi_ref[...], "nn") + bi_ref[...])
        la = (-LRU_C) * r * _softplus(-lam_ref[...])
        xc_ref[...] = xc
        r_ref[...] = r
        i_ref[...] = i
        a_ref[...] = jnp.exp(la)
        bx_ref[...] = jnp.sqrt(-_expm1(2.0 * la)) * (i * xc)

    out = col(0)
    return pl.pallas_call(
        body, name="lru_gates_fwd", grid=(LRU_BLOCKS,),
        in_specs=[col(24), cw, vec, wblk, vec, wblk, vec, vec],
        out_specs=[out] * 5,
        out_shape=[_sds((T, D), F32)] * 5,
        compiler_params=_cparams(1),
    )(u, conv_w, conv_b, w_r, b_r, w_i, b_i, lam)


def _lru_scan(name, a3, b3, reverse):
    T = a3.shape[0]
    nt = T // SCAN_TILE
    unroll = 8

    def body(a_ref, b_ref, o_ref, carry):
        @pl.when(pl.program_id(0) == 0)
        def _():
            carry[...] = jnp.zeros_like(carry)

        if not reverse:
            def step(t, h):
                h = a_ref[t] * h + b_ref[t]
                o_ref[t] = h
                return h
        else:
            def step(k, c):
                t = SCAN_TILE - 1 - k
                l = b_ref[t] + c
                o_ref[t] = l
                return a_ref[t] * l
        carry[...] = lax.fori_loop(0, SCAN_TILE, step, carry[...], unroll=unroll)

    idx = (lambda i: (nt - 1 - i, 0, 0)) if reverse else (lambda i: (i, 0, 0))
    blk = _spec((SCAN_TILE, LRU_BLOCKS, LRU_BLOCK), idx)
    return pl.pallas_call(
        body, name=name, grid=(nt,),
        in_specs=[blk, blk], out_specs=blk,
        out_shape=_sds((T, LRU_BLOCKS, LRU_BLOCK), F32),
        scratch_shapes=[pltpu.VMEM((LRU_BLOCKS, LRU_BLOCK), F32)],
        compiler_params=_cparams(1),
    )(a3, b3)


def _lru_gates_bwd(lmb, hl, a, r, i, xc, u, conv_w, w_r, w_i, lam):
    T = u.shape[0]
    col, vec, wblk, cw = _lru_specs(T)

    def body(l_ref, h_ref, a_ref, r_ref, i_ref, xc_ref, x_ref, cw_ref, wr_ref, wi_ref, lam_ref,
             dx_ref, dwr_ref, dwi_ref, dvec_ref, dcw_ref):
        l = l_ref[...]
        av, rv, iv, xc = a_ref[...], r_ref[...], i_ref[...], xc_ref[...]
        lam_v = lam_ref[...]
        sp = _softplus(-lam_v)
        la = (-LRU_C) * rv * sp
        mult = jnp.sqrt(-_expm1(2.0 * la))
        da = l * _shift_down(h_ref[...], 1)
        dmult = l * (iv * xc)
        di = l * mult * xc
        dxc = l * mult * iv
        dla = da * av - dmult * (av * av) / mult
        dzr = (dla * ((-LRU_C) * sp)) * rv * (1.0 - rv)
        dzi = di * iv * (1.0 - iv)
        dsp = jnp.sum(dla * ((-LRU_C) * rv), axis=0, keepdims=True)
        dlam = dsp * (-_sigmoid(-lam_v))
        dwr_ref[...] = _dot(xc, dzr, "tn")
        dwi_ref[...] = _dot(xc, dzi, "tn")
        dxc = dxc + _dot(dzr, wr_ref[...], "nt") + _dot(dzi, wi_ref[...], "nt")
        x = x_ref[...]
        w = cw_ref[...]
        dx = (dxc * w[3:4] + _shift_up(dxc, 1) * w[2:3] + _shift_up(dxc, 2) * w[1:2]
              + _shift_up(dxc, 3) * w[0:1])
        dx_ref[...] = dx.astype(BF16)
        dvec_ref[...] = jnp.concatenate(
            [jnp.sum(dzr, axis=0, keepdims=True), jnp.sum(dzi, axis=0, keepdims=True), dlam,
             jnp.sum(dxc, axis=0, keepdims=True)], axis=0)
        dcw_ref[...] = jnp.concatenate(
            [jnp.sum(dxc * _shift_down(x, 3 - tap), axis=0, keepdims=True) if tap < 3
             else jnp.sum(dxc * x, axis=0, keepdims=True) for tap in range(CONV_TAPS)], axis=0)

    c0 = col(0)
    return pl.pallas_call(
        body, name="lru_gates_bwd", grid=(LRU_BLOCKS,),
        in_specs=[c0, c0, c0, c0, c0, c0, col(24), cw, wblk, wblk, vec],
        out_specs=[c0, wblk, wblk, cw, cw],
        out_shape=[_sds((T, D), BF16), _sds((LRU_BLOCKS, LRU_BLOCK, LRU_BLOCK), F32),
                   _sds((LRU_BLOCKS, LRU_BLOCK, LRU_BLOCK), F32), _sds((4, D), F32), _sds((CONV_TAPS, D), F32)],
        compiler_params=_cparams(1),
    )(lmb, hl, a, r, i, xc, u, conv_w, w_r, w_i, lam)


def _xattn_probs(q, k):
    sc = _dot(q, k, "nt") * (X_HD ** -0.5)
    e = jnp.exp(sc - jnp.max(sc, axis=-1, keepdims=True))
    return e / jnp.sum(e, axis=-1, keepdims=True)


def _xattn_fwd(xq, xk, xv):
    T = xq.shape[0]
    tq = ROW_TILE
    M = xk.shape[0]

    def body(q_ref, k_ref, v_ref, o_ref):
        p = _xattn_probs(q_ref[...], k_ref[...])
        o_ref[...] = _dot(p, v_ref[...], "nn").astype(BF16)

    qs = _spec((tq, X_HD), lambda h, i: (i, h))
    kv = _spec((M, X_HD), lambda h, i: (0, h))
    return pl.pallas_call(
        body, name="xattn_fwd", grid=(X_HEADS, T // tq),
        in_specs=[qs, kv, kv], out_specs=qs, out_shape=_sds((T, D), BF16),
        compiler_params=_cparams(2),
    )(xq, xk, xv)


def _xattn_bwd(xq, xk, xv, dxo):
    T = xq.shape[0]
    tq = ROW_TILE
    M = xk.shape[0]

    def body(q_ref, k_ref, v_ref, do_ref, dq_ref, dk_ref, dv_ref):
        first = pl.program_id(1) == 0
        q, k, v, do = q_ref[...], k_ref[...], v_ref[...], do_ref[...]
        p = _xattn_probs(q, k)
        dp = _dot(do, v, "nt")
        ds = p * (dp - jnp.sum(dp * p, axis=-1, keepdims=True)) * (X_HD ** -0.5)
        dq_ref[...] = _dot(ds, k, "nn").astype(BF16)
        _accumulate(dk_ref, _dot(ds, q, "tn"), first)
        _accumulate(dv_ref, _dot(p, do, "tn"), first)

    qs = _spec((tq, X_HD), lambda h, i: (i, h))
    kv = _spec((M, X_HD), lambda h, i: (0, h))
    return pl.pallas_call(
        body, name="xattn_bwd", grid=(X_HEADS, T // tq),
        in_specs=[qs, kv, kv, qs], out_specs=[qs, kv, kv],
        out_shape=[_sds((T, D), BF16), _sds((M, D), F32), _sds((M, D), F32)],
        compiler_params=_cparams(2),
    )(xq, xk, xv, dxo)


def _final_loss(x, g, tgt):
    T = x.shape[0]
    tm = ROW_TILE

    def fn(irefs, orefs, ids):
        xv, gv = irefs[0][...], irefs[1][...]
        err = _rms_fwd(xv, gv) - irefs[2][...]
        lp = 0.5 * jnp.sum(jnp.mean(err * err, axis=-1, keepdims=True), axis=0, keepdims=True)
        first = ids[0] == 0
        _accumulate(orefs[0], jnp.broadcast_to(lp, (1, 128)), first)
        dx, dgp = _rms_bwd(xv, gv, err * (1.0 / D))
        orefs[1][...] = dx
        _accumulate(orefs[2], dgp, first)

    row = _spec((tm, D), lambda i: (i, 0))
    vec = _spec((1, D), lambda i: (0, 0))
    return _rowwise(
        "final_loss", fn, [(x, row), (g, vec), (tgt, row)],
        [(_sds((1, 128), F32), _spec((1, 128), lambda i: (0, 0))), (_sds((T, D), F32), row),
         (_sds((1, D), F32), vec)],
        (T // tm,))


def _adamw(name, w, g, m, v):
    R, C = w.shape
    tr = R
    for cand in (512, 352, 256):
        if R % cand == 0:
            tr = cand
            break

    def fn(irefs, orefs, ids):
        delta, mn, vn = _adamw_update(*(r[...] for r in irefs))
        orefs[0][...] = delta
        orefs[1][...] = mn
        orefs[2][...] = vn

    blk = _spec((tr, C), lambda i: (i, 0))
    return _rowwise(name, fn, [(w, blk), (g, blk), (m, blk), (v, blk)],
                    [(_sds((R, C), F32), blk)] * 3, (R // tr,))


def _adamw_update(wv, gv, mv, vv):
    c1 = 1.0 - ADAM_B1 ** ADAM_STEP
    c2 = 1.0 - ADAM_B2 ** ADAM_STEP
    mn = ADAM_B1 * mv + (1.0 - ADAM_B1) * gv
    vn = ADAM_B2 * vv + (1.0 - ADAM_B2) * (gv * gv)
    delta = -ADAM_LR * ((mn / c1) / (jnp.sqrt(vn / c2) + ADAM_EPS) + ADAM_WD * wv)
    return delta, mn, vn


def _adamw_halves(name, w, mine, theirs, widx, m, v, core):
    R, C = w.shape
    H = R // 2
    tr = H
    while tr * C * 4 > (1 << 20) and tr % 16 == 0:
        tr //= 2
    nb = H // tr

    def body(core_ref, w_ref, mine_ref, theirs_ref, m_ref, v_ref, g_out, d_out, m_out, v_out):
        gv = jnp.where(pl.program_id(0) == core_ref[0], mine_ref[...], theirs_ref[...])
        delta, mn, vn = _adamw_update(w_ref[...], gv, m_ref[...], v_ref[...])
        g_out[...] = gv
        d_out[...] = delta
        m_out[...] = mn
        v_out[...] = vn

    full = pl.BlockSpec((tr, C), lambda h, i, core_ref: (h * nb + i, 0))
    half = pl.BlockSpec((None, tr, C), lambda h, i, core_ref: (widx, i, 0))
    return pl.pallas_call(
        body, name=name,
        grid_spec=pltpu.PrefetchScalarGridSpec(
            num_scalar_prefetch=1, grid=(2, nb),
            in_specs=[full, half, half, full, full], out_specs=[full] * 4),
        out_shape=[_sds((R, C), F32)] * 4,
        compiler_params=_cparams(2),
    )(core, w, mine, theirs, m, v)


def _rmsnorm(name, x, g):
    M = x.shape[0]
    tm = min(ROW_TILE, M)

    def fn(irefs, orefs, ids):
        orefs[0][...] = _rms_fwd(irefs[0][...], irefs[1][...]).astype(BF16)

    row = _spec((tm, D), lambda i: (i, 0))
    return _rowwise(name, fn, [(x, row), (g, _spec((1, D), lambda i: (0, 0)))],
                    [(_sds((M, D), BF16), row)], (M // tm,))[0]


COL_FFN1_W1, COL_FFN1_W3, COL_FFN2_W1, COL_FFN2_W3 = range(4)
ROW_FFN1_W2, ROW_FFN2_W2 = range(2)
SQ_RET_O, SQ_LRU_O, SQ_OUT, SQ_XQ, SQ_XK, SQ_XV, SQ_XO = range(7)


def _local_step(x, mem, tgt, gw, sm):
    T = x.shape[0]
    tm = ROW_TILE
    col, row2, sq, win, wbg = gw["col"], gw["row2"], gw["sq"], gw["win"], gw["wbg"]
    row3 = lambda i, j, r: (i, 0)
    vec3 = lambda i, j, r: (0, 0)
    rowD = _spec((tm, D), row3)
    vecD = _spec((1, D), vec3)

    def residual_norm(acc, erefs, orefs, ids):
        xo = erefs[0][...] + acc
        orefs[0][...] = xo
        orefs[1][...] = _rms_fwd(xo, erefs[1][...]).astype(BF16)

    def res_norm_io(x_res, g):
        return ([(x_res, rowD), (g, vecD)],
                [(_sds((T, D), F32), rowD), (_sds((T, D), BF16), rowD)])

    h1 = _rmsnorm("ffn1_norm", x, sm["ffn1_norm"])
    a1, b1, s1 = _ffn_up("ffn1_up", h1, col, COL_FFN1_W1, COL_FFN1_W3)
    x1, h2 = _ffn_down("ffn1_down", s1, row2, ROW_FFN1_W2, x, sm["mix_norm"])

    u = _gemm(
        "mix_in",
        [(h2, rowD, win, _spec((None, None, D, IN_BLK), lambda i, j, r: (j, 0, 0, 0)), "nn")],
        (T // tm, N_CHIPS, 1),
        [(_sds((T, 5120), F32), _spec((tm, IN_BLK), lambda i, j, r: (i, j)))], (tm, IN_BLK))[0]

    def gate_epilogue(acc, erefs, orefs, ids):
        orefs[0][...] = _sigmoid(acc + erefs[0][...])

    gates = _gemm(
        "mix_gates",
        [(h2, rowD, wbg, _spec((None, None, D, BG_BLK), lambda i, j, r: (j, 0, 0, 0)), "nn")],
        (T // tm, N_CHIPS, 1),
        [(_sds((T, 2 * D), F32), _spec((tm, BG_BLK), lambda i, j, r: (i, j)))], (tm, BG_BLK),
        [(sm["b_branch_gate"], _spec((1, BG_BLK), lambda i, j, r: (0, j)))], gate_epilogue)[0]

    consts = _retention_constants(T)
    qr, kr, ret, yr, states = _ret_fwd(u, consts, sm["ret_gn"])

    xc, rg, ig, av, bx = _lru_gates_fwd(u, gw["conv_w"], sm["conv_b"], sm["w_rgate"], sm["b_rgate"],
                                        sm["w_igate"], sm["b_igate"], sm["lru_lambda"])
    a3 = av.reshape(T, LRU_BLOCKS, LRU_BLOCK)
    hl = _lru_scan("lru_scan_fwd", a3, bx.reshape(T, LRU_BLOCKS, LRU_BLOCK), False).reshape(T, D)

    row1 = _spec((tm, D), lambda i: (i, 0))
    glru1 = _spec((tm, D), lambda i: (i, 4))

    def lru_out(irefs, orefs, ids):
        gl, _ = _gelu_and_grad(irefs[1][...])
        orefs[0][...] = (irefs[0][...] * gl).astype(BF16)

    yl = _rowwise("lru_out", lru_out, [(hl, row1), (u, glru1)], [(_sds((T, D), BF16), row1)], (T // tm,))[0]

    y_ret = _proj_sq("y_ret", yr, sq, SQ_RET_O, "nn")[0]

    def merge_epilogue(acc, erefs, orefs, ids):
        orefs[0][...] = acc
        orefs[1][...] = (erefs[0][...] * erefs[2][...] + erefs[1][...] * acc).astype(BF16)

    y_lru, merged = _proj_sq(
        "y_lru", yl, sq, SQ_LRU_O, "nn",
        extras=[(gates, _spec((tm, D), lambda i, j, r: (i, 0))), (gates, _spec((tm, D), lambda i, j, r: (i, 1))),
                (y_ret, rowD)],
        epilogue=merge_epilogue,
        outs=[(_sds((T, D), F32), rowD), (_sds((T, D), BF16), rowD)])

    ex, ou = res_norm_io(x1, sm["xattn_norm"])
    x2, hq = _proj_sq("mix_out", merged, sq, SQ_OUT, "nn", extras=ex, epilogue=residual_norm, outs=ou)

    m = _rmsnorm("mem_norm", mem, sm["mem_norm"])
    xq = _proj_sq("xq", hq, sq, SQ_XQ, "nn", BF16)[0]
    xk = _proj_sq("xk", m, sq, SQ_XK, "nn", BF16)[0]
    xv = _proj_sq("xv", m, sq, SQ_XV, "nn", BF16)[0]
    xo = _xattn_fwd(xq, xk, xv)
    ex, ou = res_norm_io(x2, sm["ffn2_norm"])
    x3, h3 = _proj_sq("xattn_out", xo, sq, SQ_XO, "nn", extras=ex, epilogue=residual_norm, outs=ou)

    a2, b2, s2 = _ffn_up("ffn2_up", h3, col, COL_FFN2_W1, COL_FFN2_W3)
    x4 = _ffn_down("ffn2_down", s2, row2, ROW_FFN2_W2, x3)[0]
    loss, dx4, dg_final = _final_loss(x4, sm["final_norm"], tgt)

    dx3, dg_ffn2, dw13_2, dw2_2 = _ffn_bwd("ffn2", dx4, h3, a2, b2, s2, col, COL_FFN2_W1, COL_FFN2_W3,
                                           row2, ROW_FFN2_W2, x3, sm["ffn2_norm"])

    dxo = _proj_sq("d_xo", dx3, sq, SQ_XO, "nt", BF16)[0]
    dw_xo = _dw_sq("dw_xo", xo, dx3)
    dxq, dxk, dxv = _xattn_bwd(xq, xk, xv, dxo)
    dw_xq = _dw_sq("dw_xq", hq, dxq)
    ex, ou = _rms_bwd_io(x2, sm["xattn_norm"], dx3, T, tm)
    dx2, dg_xattn = _proj_sq("d_hq", dxq, sq, SQ_XQ, "nt", extras=ex, epilogue=_rms_bwd_epilogue, outs=ou)
    dw_xk = _dw_sq("dw_xk", m, dxk)
    dw_xv = _dw_sq("dw_xv", m, dxv)

    M = mem.shape[0]

    def mem_norm_epilogue(acc, erefs, orefs, ids):
        _, dgp = _rms_bwd(erefs[0][...], erefs[1][...], acc)
        orefs[0][...] = dgp

    wsq_spec = lambda idx: _spec((N_CHIPS, None, SQ_BLK, D), lambda i, j, r: (0, idx, 0, 0))
    memD = _spec((M, D), row3)
    dg_mem = _gemm(
        "d_mem_norm",
        [(dxk, memD, sq, wsq_spec(SQ_XK), "nt"), (dxv, memD, sq, wsq_spec(SQ_XV), "nt")],
        (1, 1, 1), [(_sds((1, D), F32), vecD)], (M, D),
        [(mem, memD), (sm["mem_norm"], vecD)], mem_norm_epilogue)[0]

    def merged_bwd_epilogue(acc, erefs, orefs, ids):
        gr, gl, yrv, ylv = (e[...] for e in erefs)
        orefs[0][...] = (acc * gr).astype(BF16)
        orefs[1][...] = (acc * gl).astype(BF16)
        dgr = acc * yrv * gr * (1.0 - gr)
        dgl = acc * ylv * gl * (1.0 - gl)
        orefs[2][:, :D] = dgr.astype(BF16)
        orefs[2][:, D:] = dgl.astype(BF16)
        dbb = jnp.concatenate([jnp.sum(dgr, axis=0, keepdims=True), jnp.sum(dgl, axis=0, keepdims=True)], axis=1)
        _accumulate(orefs[3], dbb, ids[0] == 0)

    dy_ret, dy_lru, dgpre, db_bg = _proj_sq(
        "d_merged", dx2, sq, SQ_OUT, "nt",
        extras=[(gates, _spec((tm, D), lambda i, j, r: (i, 0))), (gates, _spec((tm, D), lambda i, j, r: (i, 1))),
                (y_ret, rowD), (y_lru, rowD)],
        epilogue=merged_bwd_epilogue,
        outs=[(_sds((T, D), BF16), rowD), (_sds((T, D), BF16), rowD),
              (_sds((T, 2 * D), BF16), _spec((tm, 2 * D), row3)),
              (_sds((1, 2 * D), F32), _spec((1, 2 * D), vec3))])
    dw_out = _dw_sq("dw_out", merged, dx2)
    dyr = _proj_sq("d_yr", dy_ret, sq, SQ_RET_O, "nt")[0]
    dw_ro = _dw_sq("dw_ret_o", yr, dy_ret)
    dyl = _proj_sq("d_yl", dy_lru, sq, SQ_LRU_O, "nt")[0]
    dw_lo = _dw_sq("dw_lru_o", yl, dy_lru)

    dq, dk, dv, dgr, dg_retgn = _ret_bwd(dyr, ret, u, qr, kr, states, consts, sm["ret_gn"])

    def lru_out_bwd(irefs, orefs, ids):
        gl, dgl = _gelu_and_grad(irefs[2][...])
        dyl_v = irefs[0][...]
        orefs[0][...] = dyl_v * gl
        orefs[1][...] = (dyl_v * irefs[1][...] * dgl).astype(BF16)

    dhl, dglru = _rowwise("lru_out_bwd", lru_out_bwd, [(dyl, row1), (hl, row1), (u, glru1)],
                          [(_sds((T, D), F32), row1), (_sds((T, D), BF16), row1)], (T // tm,))
    lmb = _lru_scan("lru_scan_bwd", a3, dhl.reshape(T, LRU_BLOCKS, LRU_BLOCK), True).reshape(T, D)
    dxl, dw_r, dw_i, dvec, dcw = _lru_gates_bwd(lmb, hl, av, rg, ig, xc, u, gw["conv_w"],
                                                sm["w_rgate"], sm["w_igate"], sm["lru_lambda"])

    du = jnp.concatenate([dq, dk, dv, dgr, dxl, dglru], axis=1)
    tk = ROW_TILE
    dw_in = _gemm(
        "dw_in",
        [(h2, _spec((tk, D), lambda j, n, r: (r, 0)), du, _spec((tk, IN_BLK), lambda j, n, r: (r, j)), "tn")],
        (N_CHIPS, 1, T // tk),
        [(_sds((N_CHIPS, D, IN_BLK), F32), _spec((None, D, IN_BLK), lambda j, n, r: (j, 0, 0)))],
        (D, IN_BLK))[0]
    dw_bg = _gemm(
        "dw_bg",
        [(h2, _spec((tk, D), lambda j, n, r: (r, 0)), dgpre, _spec((tk, BG_BLK), lambda j, n, r: (r, j)), "tn")],
        (N_CHIPS, 1, T // tk),
        [(_sds((N_CHIPS, D, BG_BLK), F32), _spec((None, D, BG_BLK), lambda j, n, r: (j, 0, 0)))],
        (D, BG_BLK))[0]
    ex, ou = _rms_bwd_io(x1, sm["mix_norm"], dx2, T, tm)
    dx1, dg_mix = _gemm(
        "d_h2",
        [(du, _spec((tm, IN_BLK), lambda i, j, r: (i, r)),
          win, _spec((None, None, D, IN_BLK), lambda i, j, r: (r, 0, 0, 0)), "nt"),
         (dgpre, _spec((tm, BG_BLK), lambda i, j, r: (i, r)),
          wbg, _spec((None, None, D, BG_BLK), lambda i, j, r: (r, 0, 0, 0)), "nt")],
        (T // tm, 1, N_CHIPS), ou, (tm, D), ex, _rms_bwd_epilogue)

    grad_x, dg_ffn1, dw13_1, dw2_1 = _ffn_bwd("ffn1", dx1, h1, a1, b1, s1, col, COL_FFN1_W1, COL_FFN1_W3,
                                              row2, ROW_FFN1_W2, x, sm["ffn1_norm"])

    big = {
        "ffn1_w13": dw13_1, "ffn1_w2": dw2_1[None], "w_in": dw_in[None], "w_ret_o": dw_ro[None],
        "w_lru_o": dw_lo[None], "w_branch_gate": dw_bg[None], "w_out": dw_out[None], "w_xq": dw_xq[None],
        "w_xk": dw_xk[None], "w_xv": dw_xv[None], "w_xo": dw_xo[None], "ffn2_w13": dw13_2, "ffn2_w2": dw2_2[None],
    }
    small = {
        "ffn1_norm": dg_ffn1, "mix_norm": dg_mix, "ret_gn": dg_retgn, "conv_b": dvec[3:4],
        "b_rgate": dvec[0:1], "b_igate": dvec[1:2], "lru_lambda": dvec[2:3], "xattn_norm": dg_xattn,
        "mem_norm": dg_mem, "ffn2_norm": dg_ffn2, "final_norm": dg_final, "b_branch_gate": db_bg,
        "conv_w": dcw, "w_rgate": dw_r, "w_igate": dw_i,
    }
    return loss, grad_x, big, small


ANY_SPEC = pl.BlockSpec(memory_space=pl.ANY)
VMEM_SPEC = pl.BlockSpec(memory_space=pltpu.VMEM)
N_PEER_CHIPS = N_CHIPS - 1


def _mesh_position():
    x, y, c = lax.axis_index("x"), lax.axis_index("y"), lax.axis_index("c")
    chips = [(1 - x, y), (x, 1 - y), (1 - x, 1 - y)]
    return x, y, c, chips


def _chip_index(x, y):
    return 2 * x + y


def _rows_half(ref, axis, h):
    n = ref.shape[axis] // 2
    idx = [slice(None)] * len(ref.shape)
    idx[axis] = pl.ds(pl.multiple_of(h * n, 16), n)
    return ref.at[tuple(idx)]


def _remote(src, dst, send_sem, recv_sem, device):
    return pltpu.make_async_remote_copy(src_ref=src, dst_ref=dst, send_sem=send_sem, recv_sem=recv_sem,
                                        device_id=device, device_id_type=MESH)


def _allgather_weights(arrs, splits):
    n = len(arrs)
    chip_me = _chip_index(lax.axis_index("x"), lax.axis_index("y"))
    bases = [lax.dynamic_update_slice(lax.empty((N_CHIPS,) + a.shape, a.dtype), a[None],
                                      (chip_me,) + (0,) * a.ndim) for a in arrs]

    def body(*refs):
        ins, outs = refs[:n], refs[2 * n:3 * n]
        send_sems, recv_sems, fwd_send, fwd_recv = refs[3 * n:]
        x, y, c, chips = _mesh_position()
        s_me = _chip_index(x, y)
        sibling = (x, y, 1 - c)

        def mine(g):
            return _rows_half(ins[g], 1, c) if splits[g] else ins[g]

        def landing(g, s, h):
            o = outs[g].at[s]
            return _rows_half(o, 1, h) if splits[g] else o

        sends = []
        for g in range(n):
            for k, chip in enumerate(chips):
                sends.append(_remote(mine(g), landing(g, s_me, c), send_sems.at[3 * g + k],
                                     recv_sems.at[3 * g + k], (*chip, c)))
        for cp in sends:
            cp.start()
        forwards = []
        for g in range(n):
            for k, chip in enumerate(chips):
                s_k = _chip_index(*chip)
                got = landing(g, s_k, c)
                _remote(mine(g), got, send_sems.at[3 * g + k], recv_sems.at[3 * g + k], (*chip, c)).wait_recv()
                if splits[g]:
                    fw = _remote(got, got, fwd_send.at[3 * g + k], fwd_recv.at[3 * g + k], sibling)
                    fw.start()
                    forwards.append(fw)
        for g in range(n):
            if splits[g]:
                for k, chip in enumerate(chips):
                    other = landing(g, _chip_index(*chip), 1 - c)
                    _remote(other, other, fwd_send.at[3 * g + k], fwd_recv.at[3 * g + k], sibling).wait_recv()
        for cp in sends + forwards:
            cp.wait_send()

    return pl.pallas_call(
        body, name="allgather_weights",
        in_specs=[ANY_SPEC] * (2 * n), out_specs=[ANY_SPEC] * n,
        out_shape=[_sds(b.shape, b.dtype) for b in bases],
        input_output_aliases={n + g: g for g in range(n)},
        scratch_shapes=[pltpu.SemaphoreType.DMA((3 * n,)), pltpu.SemaphoreType.DMA((3 * n,)),
                        pltpu.SemaphoreType.DMA((3 * n,)), pltpu.SemaphoreType.DMA((3 * n,))],
    )(*arrs, *bases)


def _rs_pair_swap(arrs):
    n = len(arrs)

    def body(*refs):
        ins, outs = refs[:n], refs[n:2 * n]
        send_sems, recv_sems = refs[2 * n:]
        x, y, c, _ = _mesh_position()
        copies = [_remote(_rows_half(ins[a], 2, 1 - c), outs[a], send_sems.at[a], recv_sems.at[a], (x, y, 1 - c))
                  for a in range(n)]
        for cp in copies:
            cp.start()
        for cp in copies:
            cp.wait()

    return pl.pallas_call(
        body, name="rs_pair_swap",
        in_specs=[ANY_SPEC] * n, out_specs=[ANY_SPEC] * n,
        out_shape=[_sds(a.shape[:2] + (a.shape[2] // 2, a.shape[3]), a.dtype) for a in arrs],
        scratch_shapes=[pltpu.SemaphoreType.DMA((n,)), pltpu.SemaphoreType.DMA((n,))],
    )(*arrs)


def _rs_pair_sum(name, full, got, core):
    nw, ns, R, C = full.shape
    half = R // 2

    def body(core_ref, a_ref, b_ref, o_ref):
        o_ref[...] = (a_ref[...] + b_ref[...]).astype(BF16)

    blk = lambda fn: pl.BlockSpec((None, None, half, C), fn)
    return pl.pallas_call(
        body, name=name,
        grid_spec=pltpu.PrefetchScalarGridSpec(
            num_scalar_prefetch=1, grid=(nw, ns),
            in_specs=[blk(lambda w, s, core_ref: (w, s, core_ref[0], 0)), blk(lambda w, s, core_ref: (w, s, 0, 0))],
            out_specs=blk(lambda w, s, core_ref: (w, s, 0, 0))),
        out_shape=_sds((nw, ns, half, C), BF16),
        compiler_params=_cparams(2),
    )(core, full, got)


def _rs_chip_exchange(arrs):
    n = len(arrs)

    def body(*refs):
        ins, outs = refs[:n], refs[n:2 * n]
        send_sems, recv_sems = refs[2 * n:]
        x, y, c, chips = _mesh_position()
        s_me = _chip_index(x, y)
        sends = []
        for a in range(n):
            for k, chip in enumerate(chips):
                sends.append(_remote(ins[a].at[:, _chip_index(*chip)], outs[a].at[:, s_me],
                                     send_sems.at[3 * a + k], recv_sems.at[3 * a + k], (*chip, c)))
        for cp in sends:
            cp.start()
        for a in range(n):
            for k, chip in enumerate(chips):
                got = outs[a].at[:, _chip_index(*chip)]
                _remote(got, got, send_sems.at[3 * a + k], recv_sems.at[3 * a + k], (*chip, c)).wait_recv()
        for cp in sends:
            cp.wait_send()

    return pl.pallas_call(
        body, name="rs_chip_exchange",
        in_specs=[ANY_SPEC] * n, out_specs=[ANY_SPEC] * n,
        out_shape=[_sds(a.shape, a.dtype) for a in arrs],
        scratch_shapes=[pltpu.SemaphoreType.DMA((3 * n,)), pltpu.SemaphoreType.DMA((3 * n,))],
    )(*arrs)


def _rs_chip_sum(name, own, parts, chip):
    nw, ns, H, C = parts.shape

    def body(chip_ref, own_ref, *rest):
        prefs, o_ref = rest[:ns], rest[ns]
        me = chip_ref[0]
        own_v = own_ref[...].astype(F32)
        tot = None
        for s in range(ns):
            term = jnp.where(me == s, own_v, prefs[s][...].astype(F32))
            tot = term if tot is None else tot + term
        o_ref[...] = tot

    blk = lambda fn: pl.BlockSpec((None, None, H, C), fn)

    def part_spec(s):
        return blk(lambda w, chip_ref: (w, jnp.where(chip_ref[0] == s, (s + 1) % ns, s), 0, 0))

    return pl.pallas_call(
        body, name=name,
        grid_spec=pltpu.PrefetchScalarGridSpec(
            num_scalar_prefetch=1, grid=(nw,),
            in_specs=[blk(lambda w, chip_ref: (w, chip_ref[0], 0, 0))] + [part_spec(s) for s in range(ns)],
            out_specs=pl.BlockSpec((None, H, C), lambda w, chip_ref: (w, 0, 0))),
        out_shape=_sds((nw, H, C), F32),
        compiler_params=_cparams(1),
    )(chip, own, *([parts] * ns))


def _rs_pair_gather(arrs):
    n = len(arrs)

    def body(*refs):
        ins, outs = refs[:n], refs[n:2 * n]
        send_sems, recv_sems = refs[2 * n:]
        x, y, c, _ = _mesh_position()
        copies = [_remote(ins[a], outs[a], send_sems.at[a], recv_sems.at[a], (x, y, 1 - c)) for a in range(n)]
        for cp in copies:
            cp.start()
        for cp in copies:
            cp.wait()

    return pl.pallas_call(
        body, name="rs_pair_gather",
        in_specs=[ANY_SPEC] * n, out_specs=[ANY_SPEC] * n,
        out_shape=[_sds(a.shape, a.dtype) for a in arrs],
        scratch_shapes=[pltpu.SemaphoreType.DMA((n,)), pltpu.SemaphoreType.DMA((n,))],
    )(*arrs)


def _small_allreduce(v):
    R, C = v.shape

    def body(v_ref, o_ref, sib_buf, pair_buf, chip_buf, send_sems, recv_sems):
        x, y, c, chips = _mesh_position()
        s_me = _chip_index(x, y)
        swap = _remote(v_ref, sib_buf, send_sems.at[0], recv_sems.at[0], (x, y, 1 - c))
        swap.start()
        swap.wait()
        pair_buf[...] = v_ref[...] + sib_buf[...]
        chip_buf[s_me] = pair_buf[...]
        sends = [_remote(pair_buf, chip_buf.at[s_me], send_sems.at[1 + k], recv_sems.at[1 + k], (*chip, c))
                 for k, chip in enumerate(chips)]
        for cp in sends:
            cp.start()
        for k, chip in enumerate(chips):
            got = chip_buf.at[_chip_index(*chip)]
            _remote(got, got, send_sems.at[1 + k], recv_sems.at[1 + k], (*chip, c)).wait_recv()
        for cp in sends:
            cp.wait_send()
        o_ref[...] = ((chip_buf[0] + chip_buf[1]) + chip_buf[2]) + chip_buf[3]

    return pl.pallas_call(
        body, name="small_allreduce",
        in_specs=[VMEM_SPEC], out_specs=VMEM_SPEC, out_shape=_sds((R, C), F32),
        scratch_shapes=[pltpu.VMEM((R, C), F32), pltpu.VMEM((R, C), F32), pltpu.VMEM((N_CHIPS, R, C), F32),
                        pltpu.SemaphoreType.DMA((1 + N_PEER_CHIPS,)), pltpu.SemaphoreType.DMA((1 + N_PEER_CHIPS,))],
        compiler_params=pltpu.CompilerParams(vmem_limit_bytes=VMEM_LIMIT_BYTES),
    )(v)


BIG_WEIGHTS = ["ffn1_w1", "ffn1_w3", "ffn1_w2", "w_in", "w_ret_o", "w_lru_o", "w_branch_gate", "w_out",
               "w_xq", "w_xk", "w_xv", "w_xo", "ffn2_w1", "ffn2_w3", "ffn2_w2"]
SMALL_LAYOUT = [("ffn1_norm", 1), ("mix_norm", 1), ("ret_gn", 1), ("conv_b", 1), ("b_rgate", 1), ("b_igate", 1),
                ("lru_lambda", 1), ("xattn_norm", 1), ("mem_norm", 1), ("ffn2_norm", 1), ("final_norm", 1),
                ("b_branch_gate", 2), ("conv_w", CONV_TAPS), ("w_rgate", LRU_BLOCK), ("w_igate", LRU_BLOCK)]
SMALL_ROWS = 280
WEIGHT_ORDER = ["ffn1_norm", "ffn1_w1", "ffn1_w3", "ffn1_w2", "mix_norm", "w_in", "ret_gn", "w_ret_o", "conv_w",
                "conv_b", "w_rgate", "b_rgate", "w_igate", "b_igate", "lru_lambda", "w_lru_o", "w_branch_gate",
                "b_branch_gate", "w_out", "xattn_norm", "mem_norm", "w_xq", "w_xk", "w_xv", "w_xo", "ffn2_norm",
                "ffn2_w1", "ffn2_w3", "ffn2_w2", "final_norm"]


def _pack_small(parts):
    rows = [parts[name].reshape(n, D) for name, n in SMALL_LAYOUT]
    used = sum(n for _, n in SMALL_LAYOUT)
    rows.append(jnp.zeros((SMALL_ROWS - used, D), F32))
    return jnp.concatenate(rows, axis=0)


def _unpack_small(packed, shapes):
    out, r = {}, 0
    for name, n in SMALL_LAYOUT:
        out[name] = packed[r:r + n].reshape(shapes[name])
        r += n
    return out


def kernel(x, mem, ffn1_norm, ffn1_w1, ffn1_w3, ffn1_w2, mix_norm, w_in, ret_gn, w_ret_o, conv_w, conv_b, w_rgate, b_rgate, w_igate, b_igate, lru_lambda, w_lru_o, w_branch_gate, b_branch_gate, w_out, xattn_norm, mem_norm, w_xq, w_xk, w_xv, w_xo, ffn2_norm, ffn2_w1, ffn2_w3, ffn2_w2, final_norm, loss_target, m_ffn1_norm, m_ffn1_w1, m_ffn1_w3, m_ffn1_w2, m_mix_norm, m_w_in, m_ret_gn, m_w_ret_o, m_conv_w, m_conv_b, m_w_rgate, m_b_rgate, m_w_igate, m_b_igate, m_lru_lambda, m_w_lru_o, m_w_branch_gate, m_b_branch_gate, m_w_out, m_xattn_norm, m_mem_norm, m_w_xq, m_w_xk, m_w_xv, m_w_xo, m_ffn2_norm, m_ffn2_w1, m_ffn2_w3, m_ffn2_w2, m_final_norm, v_ffn1_norm, v_ffn1_w1, v_ffn1_w3, v_ffn1_w2, v_mix_norm, v_w_in, v_ret_gn, v_w_ret_o, v_conv_w, v_conv_b, v_w_rgate, v_b_rgate, v_w_igate, v_b_igate, v_lru_lambda, v_w_lru_o, v_w_branch_gate, v_b_branch_gate, v_w_out, v_xattn_norm, v_mem_norm, v_w_xq, v_w_xk, v_w_xv, v_w_xo, v_ffn2_norm, v_ffn2_w1, v_ffn2_w3, v_ffn2_w2, v_final_norm):
    given = dict(locals())
    w = {n: given[n] for n in WEIGHT_ORDER}
    mom = {n: given["m_" + n] for n in WEIGHT_ORDER}
    var = {n: given["v_" + n] for n in WEIGHT_ORDER}
    chip = _chip_index(lax.axis_index("x"), lax.axis_index("y"))
    core = lax.axis_index("c").astype(jnp.int32).reshape(1)

    stack = lambda names: jnp.stack([w[n][0] for n in names], axis=0).astype(BF16)
    shards = [stack(["ffn1_w1", "ffn1_w3", "ffn2_w1", "ffn2_w3"]), stack(["ffn1_w2", "ffn2_w2"]),
              stack(["w_ret_o", "w_lru_o", "w_out", "w_xq", "w_xk", "w_xv", "w_xo"]),
              stack(["w_in"]), stack(["w_branch_gate"]), w["conv_w"]]
    col, row2, sq, win, wbg, convg = _allgather_weights(shards, [True, True, True, True, True, False])
    gw = {"col": col, "row2": row2, "sq": sq, "win": win, "wbg": wbg,
          "conv_w": convg[:, 0].transpose(1, 0, 2).reshape(CONV_TAPS, D)}
    sm = {n: w[n] for n in ["ffn1_norm", "mix_norm", "ret_gn", "conv_b", "b_rgate", "b_igate", "lru_lambda",
                            "xattn_norm", "mem_norm", "ffn2_norm", "b_branch_gate"]}
    sm["final_norm"] = w["final_norm"].reshape(1, D)
    sm["w_rgate"] = w["w_rgate"][0]
    sm["w_igate"] = w["w_igate"][0]

    loss_part, grad_x, big, small = _local_step(x[0], mem[0], loss_target[0], gw, sm)
    loss = lax.psum(loss_part[0, 0], ("x", "y", "c"))

    names = list(big)
    fulls = [big[n] for n in names]
    gots = _rs_pair_swap(fulls)
    pair_sums = [_rs_pair_sum("rs_pair_sum_" + n, f, g, core) for n, f, g in zip(names, fulls, gots)]
    by_source = _rs_chip_exchange(pair_sums)
    chip_id = chip.astype(jnp.int32).reshape(1)
    halves = [_rs_chip_sum("rs_chip_sum_" + n, own, p, chip_id) for n, own, p in zip(names, pair_sums, by_source)]
    sibling_halves = _rs_pair_gather(halves)
    grad_halves = {}
    for n, mine, theirs in zip(names, halves, sibling_halves):
        if n.endswith("_w13"):
            grad_halves[n[:-2] + "1"] = (mine, theirs, 0)
            grad_halves[n[:-2] + "3"] = (mine, theirs, 1)
        else:
            grad_halves[n] = (mine, theirs, 0)

    small_shapes = {n: w[n].shape for n, _ in SMALL_LAYOUT}
    small_shapes["conv_w"] = (CONV_TAPS, D)
    small_sum = _small_allreduce(_pack_small(small))

    outs = {}
    for n in BIG_WEIGHTS:
        mine, theirs, widx = grad_halves[n]
        g, d, nm, nv = _adamw_halves("adamw_" + n, w[n][0], mine, theirs, widx, mom[n][0], var[n][0], core)
        outs[n] = (g[None], d[None], nm[None], nv[None])
    conv_grad = lax.dynamic_slice(small_sum[13:13 + CONV_TAPS], (0, chip * SQ_BLK), (CONV_TAPS, SQ_BLK))
    small_w = {n: w[n] for n, _ in SMALL_LAYOUT}
    small_m = {n: mom[n] for n, _ in SMALL_LAYOUT}
    small_v = {n: var[n] for n, _ in SMALL_LAYOUT}
    pad_cols = lambda a: jnp.pad(a[0], ((0, 0), (0, D - SQ_BLK)))
    for dct in (small_w, small_m, small_v):
        dct["conv_w"] = pad_cols(dct["conv_w"])
    g_pack = lax.dynamic_update_slice(small_sum, jnp.pad(conv_grad, ((0, 0), (0, D - SQ_BLK))), (13, 0))
    d_pack, m_pack, v_pack = _adamw("adamw_small", _pack_small(small_w), g_pack, _pack_small(small_m),
                                    _pack_small(small_v))
    unpacked = [_unpack_small(p, small_shapes) for p in (g_pack, d_pack, m_pack, v_pack)]
    for n, _ in SMALL_LAYOUT:
        if n == "conv_w":
            outs[n] = tuple(u[n][:, :SQ_BLK][None] for u in unpacked)
        else:
            outs[n] = tuple(u[n] for u in unpacked)

    result = [loss, grad_x[None]]
    for k in range(4):
        result += [outs[n][k] for n in WEIGHT_ORDER]
    return tuple(result)
```

```python
import functools
import math

import numpy as np
import jax
import jax.numpy as jnp
from jax import lax
from jax.experimental import pallas as pl
from jax.experimental.pallas import tpu as pltpu

F32 = jnp.float32
BF16 = jnp.bfloat16
MESH = pl.DeviceIdType.MESH

D = 1024
EPS = 1e-6
RET_HEADS = 4
RET_DK = 128
RET_DV = 256
CHUNK = 128
ROPE_BASE = 10000.0
LRU_BLOCKS = 8
LRU_BLOCK = 128
CONV_TAPS = 4
LRU_C = 8.0
D_FF = 2816
X_HEADS = 4
X_HD = 256
N_CHIPS = 4
FF_BLK = D_FF // N_CHIPS
IN_BLK = 5120 // N_CHIPS
BG_BLK = 2048 // N_CHIPS
SQ_BLK = D // N_CHIPS

ADAM_LR = 0.001
ADAM_B1 = 0.9
ADAM_B2 = 0.999
ADAM_EPS = 1e-08
ADAM_WD = 0.01
ADAM_STEP = 10

VMEM_LIMIT_BYTES = 56 * 1024 * 1024
ROW_TILE = 512
SCAN_TILE = 256

_DN = {
    "nn": (((1,), (0,)), ((), ())),
    "nt": (((1,), (1,)), ((), ())),
    "tn": (((0,), (0,)), ((), ())),
}


def _cparams(n_axes):
    return pltpu.CompilerParams(dimension_semantics=("arbitrary",) * n_axes,
                                vmem_limit_bytes=VMEM_LIMIT_BYTES)


def _dot(a, b, kind):
    if b.ndim == 3:
        b = b.reshape(b.shape[0] * b.shape[1], b.shape[2])
    return lax.dot_general(a.astype(BF16), b.astype(BF16), _DN[kind], preferred_element_type=F32)


def _sigmoid(x):
    return 1.0 / (1.0 + jnp.exp(-x))


def _log1p_pos(e):
    u = 1.0 + e
    return jnp.where(u == 1.0, e, jnp.log(u) * (e / jnp.where(u == 1.0, 1.0, u - 1.0)))


def _expm1(x):
    u = jnp.exp(x)
    lu = jnp.log(u)
    safe = jnp.where(lu == 0.0, 1.0, lu)
    return jnp.where(u == 1.0, x, (u - 1.0) * (x / safe))


def _softplus(z):
    return jnp.maximum(z, 0.0) + _log1p_pos(jnp.exp(-jnp.abs(z)))


_GELU_C = math.sqrt(2.0 / math.pi)


def _gelu_and_grad(x):
    x2 = x * x
    t = jnp.tanh(_GELU_C * (x + 0.044715 * x * x2))
    g = 0.5 * x * (1.0 + t)
    dg = 0.5 * (1.0 + t) + 0.5 * x * (1.0 - t * t) * (_GELU_C * (1.0 + 3.0 * 0.044715 * x2))
    return g, dg


def _rms_fwd(x, g):
    r = lax.rsqrt(jnp.mean(x * x, axis=-1, keepdims=True) + EPS)
    return (x * r) * g


def _rms_bwd(x, g, dh):
    r = lax.rsqrt(jnp.mean(x * x, axis=-1, keepdims=True) + EPS)
    n = x * r
    dyg = dh * g
    dx = r * (dyg - n * jnp.mean(dyg * n, axis=-1, keepdims=True))
    return dx, jnp.sum(dh * n, axis=0, keepdims=True)


def _accumulate(ref, val, first):
    @pl.when(first)
    def _():
        ref[...] = val

    @pl.when(jnp.logical_not(first))
    def _():
        ref[...] += val


def _sds(shape, dtype):
    return jax.ShapeDtypeStruct(tuple(shape), dtype)


def _spec(shape, fn):
    return pl.BlockSpec(tuple(shape), fn)


class _Task:
    def __init__(self, operands, out_shapes, aliases, nsem, make, finish):
        self.operands, self.out_shapes, self.aliases = operands, out_shapes, aliases
        self.nsem, self.make, self.finish = nsem, make, finish


class _Plan:
    def __init__(self):
        self.tasks, self.after = {}, {}


_plan = None


def _pcall(body, *, name, grid, in_specs, out_specs, out_shape, scratch_shapes=(), num_prefetch=0):
    single = not isinstance(out_shape, (list, tuple))
    out_shape = [out_shape] if single else list(out_shape)
    out_specs = [out_specs] if single else list(out_specs)
    in_specs = list(in_specs)
    scratch_shapes = list(scratch_shapes)
    tasks = _plan.tasks.pop(name, []) if _plan is not None else []
    after = _plan.after.pop(name, []) if _plan is not None else []
    nax = len(grid)

    def run(*operands):
        n_in = len(operands) - num_prefetch
        n_out = len(out_shape)
        t_ops = [t.operands() for t in tasks]
        t_outs = [t.out_shapes() for t in tasks]
        c_ops = [a for ops in t_ops for a in ops]
        c_outs = [s for outs in t_outs for s in outs]
        aliases = {}
        i0, o0 = num_prefetch + n_in, n_out
        for t, ops, outs in zip(tasks, t_ops, t_outs):
            for i_loc, o_loc in t.aliases.items():
                aliases[i0 + i_loc] = o0 + o_loc
            i0 += len(ops)
            o0 += len(outs)
        nsem = sum(t.nsem for t in tasks)

        def wrapped(*refs):
            p = num_prefetch
            pre, ins = refs[:p], refs[p:p + n_in]
            cins = refs[p + n_in:p + n_in + len(c_ops)]
            q = p + n_in + len(c_ops)
            outs, couts = refs[q:q + n_out], refs[q + n_out:q + n_out + len(c_outs)]
            q += n_out + len(c_outs)
            scr = refs[q:q + len(scratch_shapes)]

            def descriptors():
                send_sems, recv_sems = refs[q + len(scratch_shapes):]
                starts, arrivals = [], []
                ci = co = so = 0
                for t, ops, souts in zip(tasks, t_ops, t_outs):
                    s, a = t.make(cins[ci:ci + len(ops)], couts[co:co + len(souts)],
                                  functools.partial(lambda base, k: send_sems.at[base + k], so),
                                  functools.partial(lambda base, k: recv_sems.at[base + k], so))
                    starts += s
                    arrivals += a
                    ci, co, so = ci + len(ops), co + len(souts), so + t.nsem
                return starts, arrivals

            if tasks:
                ids = [pl.program_id(k) for k in range(nax)]
                first = functools.reduce(jnp.logical_and, [i == 0 for i in ids])
                last = functools.reduce(jnp.logical_and, [i == g - 1 for i, g in zip(ids, grid)])

                @pl.when(first)
                def _():
                    for cp in descriptors()[0]:
                        cp.start()

            body(*pre, *ins, *outs, *scr)

            if tasks:
                @pl.when(last)
                def _():
                    starts, arrivals = descriptors()
                    for arrival in arrivals:
                        arrival().wait_recv()
                    for cp in starts:
                        cp.wait_send()

        sems = [pltpu.SemaphoreType.DMA((nsem,)), pltpu.SemaphoreType.DMA((nsem,))] if tasks else []
        res = pl.pallas_call(
            wrapped, name=name,
            grid_spec=pltpu.PrefetchScalarGridSpec(
                num_scalar_prefetch=num_prefetch, grid=tuple(grid),
                in_specs=in_specs + [ANY_SPEC] * len(c_ops),
                out_specs=out_specs + [ANY_SPEC] * len(c_outs),
                scratch_shapes=scratch_shapes + sems),
            out_shape=out_shape + c_outs,
            input_output_aliases=aliases,
            compiler_params=_cparams(nax),
        )(*operands, *c_ops)
        co = n_out
        for t, souts in zip(tasks, t_outs):
            t.finish(res[co:co + len(souts)])
            co += len(souts)
        for fn in after:
            fn()
        return res[0] if single else list(res[:n_out])

    return run


def _comm_call(name):
    def body(o_ref):
        o_ref[...] = jnp.zeros_like(o_ref)

    _pcall(body, name=name, grid=(1,), in_specs=[], out_specs=_spec((8, 128), lambda i: (0, 0)),
           out_shape=_sds((8, 128), F32))()


def _gemm(name, terms, grid, outs, acc_shape, extras=(), epilogue=None):
    kinds = [t[4] for t in terms]
    nt, ne, no = len(terms), len(extras), len(outs)
    nred = grid[-1]
    nax = len(grid)

    def body(*refs):
        trefs = refs[:2 * nt]
        erefs = refs[2 * nt:2 * nt + ne]
        orefs = refs[2 * nt + ne:2 * nt + ne + no]
        ids = [pl.program_id(k) for k in range(nax)]
        tot = None
        for t in range(nt):
            d = _dot(trefs[2 * t][...], trefs[2 * t + 1][...], kinds[t])
            tot = d if tot is None else tot + d

        def finish(acc):
            if epilogue is None:
                orefs[0][...] = acc.astype(orefs[0].dtype)
            else:
                epilogue(acc, erefs, orefs, ids)

        if nred == 1:
            finish(tot)
        else:
            acc_ref = refs[-1]
            r = ids[-1]

            @pl.when(r == 0)
            def _():
                acc_ref[...] = tot

            @pl.when(r > 0)
            def _():
                acc_ref[...] += tot

            @pl.when(r == nred - 1)
            def _():
                finish(acc_ref[...])

    operands, in_specs = [], []
    for a, a_spec, b, b_spec, _ in terms:
        operands += [a, b]
        in_specs += [a_spec, b_spec]
    for e, e_spec in extras:
        operands.append(e)
        in_specs.append(e_spec)
    scratch = [pltpu.VMEM(tuple(acc_shape), F32)] if nred > 1 else []
    return _pcall(body, name=name, grid=tuple(grid), in_specs=in_specs, out_specs=[o[1] for o in outs],
                  out_shape=[o[0] for o in outs], scratch_shapes=scratch)(*operands)


def _rowwise(name, fn, ins, outs, grid):
    ni = len(ins)
    nax = len(grid)

    def body(*refs):
        ids = [pl.program_id(k) for k in range(nax)]
        fn(refs[:ni], refs[ni:], ids)

    return _pcall(body, name=name, grid=tuple(grid), in_specs=[i[1] for i in ins],
                  out_specs=[o[1] for o in outs], out_shape=[o[0] for o in outs])(*[i[0] for i in ins])


def _ffn_up(name, h, wcol, w1_idx, w3_idx):
    T = h.shape[0]
    tm = ROW_TILE

    def body(h_ref, w1_ref, w3_ref, a_ref, b_ref, s_ref):
        hv = h_ref[...]
        a = _dot(hv, w1_ref[...], "nn")
        b = _dot(hv, w3_ref[...], "nn")
        a_ref[...] = a
        b_ref[...] = b
        s_ref[...] = ((a * _sigmoid(a)) * b).astype(BF16)

    blk = _spec((None, tm, FF_BLK), lambda j, i: (j, i, 0))
    return _pcall(
        body, name=name, grid=(N_CHIPS, T // tm),
        in_specs=[_spec((tm, D), lambda j, i: (i, 0)),
                  _spec((None, None, D, FF_BLK), lambda j, i: (j, w1_idx, 0, 0)),
                  _spec((None, None, D, FF_BLK), lambda j, i: (j, w3_idx, 0, 0))],
        out_specs=[blk, blk, blk],
        out_shape=[_sds((N_CHIPS, T, FF_BLK), F32), _sds((N_CHIPS, T, FF_BLK), F32),
                   _sds((N_CHIPS, T, FF_BLK), BF16)],
    )(h, wcol, wcol)


def _ffn_down(name, s, wrow2, w2_idx, x_res, g_next=None):
    T = x_res.shape[0]
    tm = ROW_TILE
    row = lambda i, j, r: (i, 0)

    def epilogue(acc, erefs, orefs, ids):
        xo = erefs[0][...] + 0.5 * acc
        orefs[0][...] = xo
        if g_next is not None:
            orefs[1][...] = _rms_fwd(xo, erefs[1][...]).astype(BF16)

    extras = [(x_res, _spec((tm, D), row))]
    outs = [(_sds((T, D), F32), _spec((tm, D), row))]
    if g_next is not None:
        extras.append((g_next, _spec((1, D), lambda i, j, r: (0, 0))))
        outs.append((_sds((T, D), BF16), _spec((tm, D), row)))
    return _gemm(
        name,
        [(s, _spec((None, tm, FF_BLK), lambda i, j, r: (r, i, 0)),
          wrow2, _spec((None, None, FF_BLK, D), lambda i, j, r: (r, w2_idx, 0, 0)), "nn")],
        (T // tm, 1, N_CHIPS), outs, (tm, D), extras, epilogue)


def _ffn_bwd_mid(name, dx, wrow2, w2_idx, a, b):
    T = dx.shape[0]
    tm = ROW_TILE

    def body(dx_ref, w2_ref, a_ref, b_ref, dab_ref):
        ds = _dot(0.5 * dx_ref[...], w2_ref[...], "nt")
        av = a_ref[...]
        sg = _sigmoid(av)
        dab_ref[0] = (ds * b_ref[...] * (sg * (1.0 + av * (1.0 - sg)))).astype(BF16)
        dab_ref[1] = (ds * (av * sg)).astype(BF16)

    blk = _spec((None, tm, FF_BLK), lambda j, i: (j, i, 0))
    return _pcall(
        body, name=name, grid=(N_CHIPS, T // tm),
        in_specs=[_spec((tm, D), lambda j, i: (i, 0)),
                  _spec((None, None, FF_BLK, D), lambda j, i: (j, w2_idx, 0, 0)),
                  blk, blk],
        out_specs=_spec((2, None, tm, FF_BLK), lambda j, i: (0, j, i, 0)),
        out_shape=_sds((2, N_CHIPS, T, FF_BLK), BF16),
    )(dx, wrow2, a, b)


def _rms_bwd_epilogue(acc, erefs, orefs, ids):
    dx, dgp = _rms_bwd(erefs[0][...], erefs[1][...], acc)
    orefs[0][...] = dx + erefs[2][...]
    _accumulate(orefs[1], dgp, ids[0] == 0)


def _rms_bwd_io(x, g, dres, T, tm):
    row = lambda i, j, r: (i, 0)
    vec = lambda i, j, r: (0, 0)
    extras = [(x, _spec((tm, D), row)), (g, _spec((1, D), vec)), (dres, _spec((tm, D), row))]
    outs = [(_sds((T, D), F32), _spec((tm, D), row)), (_sds((1, D), F32), _spec((1, D), vec))]
    return extras, outs


def _ffn_bwd(tag, dx_out, h, a, b, s, wcol, w1_idx, w3_idx, wrow2, w2_idx, x_in, g, big):
    T = dx_out.shape[0]
    tm = ROW_TILE
    tk = ROW_TILE
    dab = _ffn_bwd_mid(tag + "_bwd_mid", dx_out, wrow2, w2_idx, a, b)

    def half_scale(acc, erefs, orefs, ids):
        orefs[0][...] = 0.5 * acc

    big[tag + "_w2"] = _gemm(
        tag + "_dw2",
        [(s, _spec((None, tk, FF_BLK), lambda j, n, r: (j, r, 0)),
          dx_out, _spec((tk, D), lambda j, n, r: (r, 0)), "tn")],
        (N_CHIPS, 1, T // tk),
        [(_sds((N_CHIPS, FF_BLK, D), F32), _spec((None, FF_BLK, D), lambda j, n, r: (j, 0, 0)))],
        (FF_BLK, D), (), half_scale)[0][None]
    for widx, wname in ((0, "_w1"), (1, "_w3")):
        big[tag + wname] = _gemm(
            tag + "_d" + wname[1:],
            [(h, _spec((tk, D), lambda j, n, r: (r, 0)),
              dab, _spec((None, None, tk, FF_BLK), functools.partial(lambda w, j, n, r: (w, j, r, 0), widx)), "tn")],
            (N_CHIPS, 1, T // tk),
            [(_sds((N_CHIPS, D, FF_BLK), F32), _spec((None, D, FF_BLK), lambda j, n, r: (j, 0, 0)))],
            (D, FF_BLK))[0][None]
    extras, outs = _rms_bwd_io(x_in, g, dx_out, T, tm)
    dx_in, dg = _gemm(
        tag + "_dh",
        [(dab, _spec((None, None, tm, FF_BLK), lambda i, j, r: (0, r, i, 0)),
          wcol, _spec((None, None, D, FF_BLK), lambda i, j, r: (r, w1_idx, 0, 0)), "nt"),
         (dab, _spec((None, None, tm, FF_BLK), lambda i, j, r: (1, r, i, 0)),
          wcol, _spec((None, None, D, FF_BLK), lambda i, j, r: (r, w3_idx, 0, 0)), "nt")],
        (T // tm, 1, N_CHIPS), outs, (tm, D), extras, _rms_bwd_epilogue)
    return dx_in, dg


def _proj_sq(name, a, wsq, idx, kind, out_dtype=F32, extras=(), epilogue=None, outs=None):
    M = a.shape[0]
    tm = min(ROW_TILE, M)
    if outs is None:
        outs = [(_sds((M, D), out_dtype), _spec((tm, D), lambda i, j, r: (i, 0)))]
    return _gemm(
        name,
        [(a, _spec((tm, D), lambda i, j, r: (i, 0)),
          wsq, _spec((N_CHIPS, None, SQ_BLK, D), lambda i, j, r: (0, idx, 0, 0)), kind)],
        (M // tm, 1, 1), outs, (tm, D), extras, epilogue)


def _dw_sq(name, a, b):
    M = a.shape[0]
    tk = min(ROW_TILE, M)
    return _gemm(
        name,
        [(a, _spec((tk, SQ_BLK), lambda j, n, r: (r, j)), b, _spec((tk, D), lambda j, n, r: (r, 0)), "tn")],
        (N_CHIPS, 1, M // tk),
        [(_sds((N_CHIPS, SQ_BLK, D), F32), _spec((None, SQ_BLK, D), lambda j, n, r: (j, 0, 0)))],
        (SQ_BLK, D))[0]


def _retention_constants(T):
    pos = jnp.arange(T, dtype=F32)
    inv_freq = ROPE_BASE ** (-jnp.arange(0, RET_DK, 2, dtype=F32) / RET_DK)
    ang = pos[:, None] * inv_freq[None, :]
    cosf = jnp.concatenate([jnp.cos(ang), jnp.cos(ang)], axis=1)
    sins = jnp.concatenate([-jnp.sin(ang), jnp.sin(ang)], axis=1)
    lg = jnp.log(1.0 - 2.0 ** (-5.0 - jnp.arange(RET_HEADS, dtype=F32)))
    p = jnp.arange(CHUNK, dtype=F32)
    rel = p[:, None] - p[None, :]
    dmat = jnp.where(rel[None] >= 0, jnp.exp(rel[None] * lg[:, None, None]), 0.0)
    kd = jnp.exp((CHUNK - 1.0 - p)[None, :] * lg[:, None])[:, :, None]
    qd = jnp.exp((p + 1.0)[None, :] * lg[:, None])[:, :, None]
    cd = jnp.exp(CHUNK * lg)[:, None, None]
    return cosf, sins, dmat, kd, qd, cd


def _rot(t, cosv, sinv):
    return t * cosv + pltpu.roll(t, RET_DK // 2, 1) * sinv


def _unrot(t, cosv, sinv):
    return t * cosv - pltpu.roll(t, RET_DK // 2, 1) * sinv


def _ret_const_specs(cm):
    return [
        _spec((CHUNK, RET_DK), lambda h, c: (cm(c), 0)),
        _spec((CHUNK, RET_DK), lambda h, c: (cm(c), 0)),
        _spec((None, CHUNK, CHUNK), lambda h, c: (h, 0, 0)),
        _spec((None, CHUNK, 1), lambda h, c: (h, 0, 0)),
        _spec((None, CHUNK, 1), lambda h, c: (h, 0, 0)),
        _spec((None, 1, 1), lambda h, c: (h, 0, 0)),
    ]


def _ret_fwd(u, consts, ret_gn):
    T = u.shape[0]
    nC = T // CHUNK
    kscale = RET_DK ** -0.5

    def body(q_ref, k_ref, v_ref, g_ref, cos_ref, sin_ref, dm_ref, kd_ref, qd_ref, cd_ref, gn_ref,
             qr_ref, kr_ref, ret_ref, yr_ref, st_ref, state):
        @pl.when(pl.program_id(1) == 0)
        def _():
            state[...] = jnp.zeros_like(state)

        cosv, sinv = cos_ref[...], sin_ref[...]
        q = _rot(q_ref[...], cosv, sinv)
        k = _rot(k_ref[...], cosv, sinv) * kscale
        v = v_ref[...]
        qr_ref[...] = q
        kr_ref[...] = k
        prev = state[...]
        st_ref[...] = prev
        s = _dot(q, k, "nt") * dm_ref[...]
        ret = _dot(s, v, "nn") + _dot(q, prev, "nn") * qd_ref[...]
        state[...] = cd_ref[...] * prev + _dot(k * kd_ref[...], v, "tn")
        ret_ref[...] = ret
        mu = jnp.mean(ret, axis=-1, keepdims=True)
        xc = ret - mu
        yn = xc * lax.rsqrt(jnp.mean(xc * xc, axis=-1, keepdims=True) + EPS)
        g = g_ref[...]
        yr_ref[...] = ((g * _sigmoid(g)) * (yn * gn_ref[...])).astype(BF16)

    cm = lambda c: c
    in_specs = [
        _spec((CHUNK, RET_DK), lambda h, c: (c, h)),
        _spec((CHUNK, RET_DK), lambda h, c: (c, RET_HEADS + h)),
        _spec((CHUNK, RET_DV), lambda h, c: (c, 4 + h)),
        _spec((CHUNK, RET_DV), lambda h, c: (c, 8 + h)),
    ] + _ret_const_specs(cm) + [_spec((1, RET_DV), lambda h, c: (0, h))]
    qk_out = _spec((CHUNK, RET_DK), lambda h, c: (c, h))
    v_out = _spec((CHUNK, RET_DV), lambda h, c: (c, h))
    return _pcall(
        body, name="ret_fwd", grid=(RET_HEADS, nC),
        in_specs=in_specs,
        out_specs=[qk_out, qk_out, v_out, v_out,
                   _spec((None, None, RET_DK, RET_DV), lambda h, c: (h, c, 0, 0))],
        out_shape=[_sds((T, 512), F32), _sds((T, 512), F32), _sds((T, D), F32), _sds((T, D), BF16),
                   _sds((RET_HEADS, nC, RET_DK, RET_DV), F32)],
        scratch_shapes=[pltpu.VMEM((RET_DK, RET_DV), F32)],
    )(u, u, u, u, *consts, ret_gn)


def _ret_bwd(dyr, ret, u, qr, kr, states, consts, ret_gn):
    T = u.shape[0]
    nC = T // CHUNK
    kscale = RET_DK ** -0.5

    def body(dyr_ref, ret_ref, g_ref, q_ref, k_ref, v_ref, st_ref,
             cos_ref, sin_ref, dm_ref, kd_ref, qd_ref, cd_ref, gn_ref,
             dq_ref, dk_ref, dv_ref, dg_ref, dgn_ref, gstate):
        first = pl.program_id(1) == 0

        @pl.when(first)
        def _():
            gstate[...] = jnp.zeros_like(gstate)

        ret = ret_ref[...]
        mu = jnp.mean(ret, axis=-1, keepdims=True)
        xc = ret - mu
        rs = lax.rsqrt(jnp.mean(xc * xc, axis=-1, keepdims=True) + EPS)
        yn = xc * rs
        gn = gn_ref[...]
        g = g_ref[...]
        sg = _sigmoid(g)
        dyr_v = dyr_ref[...]
        dretn = dyr_v * (g * sg)
        dg_ref[...] = (dyr_v * (yn * gn) * (sg * (1.0 + g * (1.0 - sg)))).astype(BF16)
        _accumulate(dgn_ref, jnp.sum(dretn * yn, axis=0, keepdims=True), first)
        dyn = dretn * gn
        d_o = rs * (dyn - jnp.mean(dyn, axis=-1, keepdims=True)
                    - yn * jnp.mean(dyn * yn, axis=-1, keepdims=True))

        q, k, v = q_ref[...], k_ref[...], v_ref[...]
        dmat, kd, qd = dm_ref[...], kd_ref[...], qd_ref[...]
        prev = st_ref[...]
        gnext = gstate[...]
        s = _dot(q, k, "nt") * dmat
        ds = _dot(d_o, v, "nt") * dmat
        doq = d_o * qd
        dq = _dot(ds, k, "nn") + _dot(doq, prev, "nt")
        dk = _dot(ds, q, "tn") + _dot(v, gnext, "nt") * kd
        dv = _dot(s, d_o, "tn") + _dot(k * kd, gnext, "nn")
        gstate[...] = cd_ref[...] * gnext + _dot(q, doq, "tn")
        cosv, sinv = cos_ref[...], sin_ref[...]
        dq_ref[...] = _unrot(dq, cosv, sinv).astype(BF16)
        dk_ref[...] = _unrot(dk * kscale, cosv, sinv).astype(BF16)
        dv_ref[...] = dv.astype(BF16)

    cm = lambda c: nC - 1 - c
    vspec = lambda off: _spec((CHUNK, RET_DV), lambda h, c: (cm(c), off + h))
    qspec = _spec((CHUNK, RET_DK), lambda h, c: (cm(c), h))
    in_specs = [vspec(0), vspec(0), vspec(8), qspec, qspec, vspec(4),
                _spec((None, None, RET_DK, RET_DV), lambda h, c: (h, cm(c), 0, 0)),
                ] + _ret_const_specs(cm) + [_spec((1, RET_DV), lambda h, c: (0, h))]
    return _pcall(
        body, name="ret_bwd", grid=(RET_HEADS, nC),
        in_specs=in_specs,
        out_specs=[qspec, qspec, vspec(0), vspec(0), _spec((1, RET_DV), lambda h, c: (0, h))],
        out_shape=[_sds((T, 512), BF16), _sds((T, 512), BF16), _sds((T, D), BF16), _sds((T, D), BF16),
                   _sds((1, D), F32)],
        scratch_shapes=[pltpu.VMEM((RET_DK, RET_DV), F32)],
    )(dyr, ret, u, qr, kr, u, states, *consts, ret_gn)


def _shift_down(x, s):
    rows = lax.broadcasted_iota(jnp.int32, x.shape, 0)
    return jnp.where(rows >= s, pltpu.roll(x, s, 0), 0.0)


def _shift_up(x, s):
    n = x.shape[0]
    rows = lax.broadcasted_iota(jnp.int32, x.shape, 0)
    return jnp.where(rows < n - s, pltpu.roll(x, n - s, 0), 0.0)


def _lru_specs(T):
    col = lambda off: _spec((T, LRU_BLOCK), lambda g: (0, off + g))
    vec = _spec((1, LRU_BLOCK), lambda g: (0, g))
    wblk = _spec((None, LRU_BLOCK, LRU_BLOCK), lambda g: (g, 0, 0))
    cw = _spec((CONV_TAPS, LRU_BLOCK), lambda g: (0, g))
    return col, vec, wblk, cw


def _lru_gates_fwd(u, conv_w, conv_b, w_r, b_r, w_i, b_i, lam):
    T = u.shape[0]
    col, vec, wblk, cw = _lru_specs(T)

    def body(x_ref, cw_ref, cb_ref, wr_ref, br_ref, wi_ref, bi_ref, lam_ref,
             xc_ref, r_ref, i_ref, a_ref, bx_ref):
        x = x_ref[...]
        w = cw_ref[...]
        xc = (_shift_down(x, 3) * w[0:1] + _shift_down(x, 2) * w[1:2] + _shift_down(x, 1) * w[2:3]
              + x * w[3:4] + cb_ref[...])
        r = _sigmoid(_dot(xc, wr_ref[...], "nn") + br_ref[...])
        i = _sigmoid(_dot(xc, wi_ref[...], "nn") + bi_ref[...])
        la = (-LRU_C) * r * _softplus(-lam_ref[...])
        xc_ref[...] = xc
        r_ref[...] = r
        i_ref[...] = i
        a_ref[...] = jnp.exp(la)
        bx_ref[...] = jnp.sqrt(-_expm1(2.0 * la)) * (i * xc)

    out = col(0)
    return _pcall(
        body, name="lru_gates_fwd", grid=(LRU_BLOCKS,),
        in_specs=[col(24), cw, vec, wblk, vec, wblk, vec, vec],
        out_specs=[out] * 5,
        out_shape=[_sds((T, D), F32)] * 5,
    )(u, conv_w, conv_b, w_r, b_r, w_i, b_i, lam)


def _lru_scan(name, a3, b3, reverse):
    T = a3.shape[0]
    nt = T // SCAN_TILE
    unroll = 8

    def body(a_ref, b_ref, o_ref, carry):
        @pl.when(pl.program_id(0) == 0)
        def _():
            carry[...] = jnp.zeros_like(carry)

        if not reverse:
            def step(t, h):
                h = a_ref[t] * h + b_ref[t]
                o_ref[t] = h
                return h
        else:
            def step(k, c):
                t = SCAN_TILE - 1 - k
                l = b_ref[t] + c
                o_ref[t] = l
                return a_ref[t] * l
        carry[...] = lax.fori_loop(0, SCAN_TILE, step, carry[...], unroll=unroll)

    idx = (lambda i: (nt - 1 - i, 0, 0)) if reverse else (lambda i: (i, 0, 0))
    blk = _spec((SCAN_TILE, LRU_BLOCKS, LRU_BLOCK), idx)
    return _pcall(
        body, name=name, grid=(nt,),
        in_specs=[blk, blk], out_specs=blk,
        out_shape=_sds((T, LRU_BLOCKS, LRU_BLOCK), F32),
        scratch_shapes=[pltpu.VMEM((LRU_BLOCKS, LRU_BLOCK), F32)],
    )(a3, b3)


def _lru_gates_bwd(lmb, hl, a, r, i, xc, u, conv_w, w_r, w_i, lam):
    T = u.shape[0]
    col, vec, wblk, cw = _lru_specs(T)

    def body(l_ref, h_ref, a_ref, r_ref, i_ref, xc_ref, x_ref, cw_ref, wr_ref, wi_ref, lam_ref,
             dx_ref, dwr_ref, dwi_ref, dvec_ref, dcw_ref):
        l = l_ref[...]
        av, rv, iv, xc = a_ref[...], r_ref[...], i_ref[...], xc_ref[...]
        lam_v = lam_ref[...]
        sp = _softplus(-lam_v)
        la = (-LRU_C) * rv * sp
        mult = jnp.sqrt(-_expm1(2.0 * la))
        da = l * _shift_down(h_ref[...], 1)
        dmult = l * (iv * xc)
        di = l * mult * xc
        dxc = l * mult * iv
        dla = da * av - dmult * (av * av) / mult
        dzr = (dla * ((-LRU_C) * sp)) * rv * (1.0 - rv)
        dzi = di * iv * (1.0 - iv)
        dsp = jnp.sum(dla * ((-LRU_C) * rv), axis=0, keepdims=True)
        dlam = dsp * (-_sigmoid(-lam_v))
        dwr_ref[...] = _dot(xc, dzr, "tn")
        dwi_ref[...] = _dot(xc, dzi, "tn")
        dxc = dxc + _dot(dzr, wr_ref[...], "nt") + _dot(dzi, wi_ref[...], "nt")
        x = x_ref[...]
        w = cw_ref[...]
        dx = (dxc * w[3:4] + _shift_up(dxc, 1) * w[2:3] + _shift_up(dxc, 2) * w[1:2]
              + _shift_up(dxc, 3) * w[0:1])
        dx_ref[...] = dx.astype(BF16)
        dvec_ref[...] = jnp.concatenate(
            [jnp.sum(dzr, axis=0, keepdims=True), jnp.sum(dzi, axis=0, keepdims=True), dlam,
             jnp.sum(dxc, axis=0, keepdims=True)], axis=0)
        dcw_ref[...] = jnp.concatenate(
            [jnp.sum(dxc * _shift_down(x, 3 - tap), axis=0, keepdims=True) if tap < 3
             else jnp.sum(dxc * x, axis=0, keepdims=True) for tap in range(CONV_TAPS)], axis=0)

    c0 = col(0)
    return _pcall(
        body, name="lru_gates_bwd", grid=(LRU_BLOCKS,),
        in_specs=[c0, c0, c0, c0, c0, c0, col(24), cw, wblk, wblk, vec],
        out_specs=[c0, wblk, wblk, cw, cw],
        out_shape=[_sds((T, D), BF16), _sds((LRU_BLOCKS, LRU_BLOCK, LRU_BLOCK), F32),
                   _sds((LRU_BLOCKS, LRU_BLOCK, LRU_BLOCK), F32), _sds((4, D), F32), _sds((CONV_TAPS, D), F32)],
    )(lmb, hl, a, r, i, xc, u, conv_w, w_r, w_i, lam)


def _xattn_probs(q, k):
    sc = _dot(q, k, "nt") * (X_HD ** -0.5)
    e = jnp.exp(sc - jnp.max(sc, axis=-1, keepdims=True))
    return e / jnp.sum(e, axis=-1, keepdims=True)


def _xattn_fwd(xq, xk, xv):
    T = xq.shape[0]
    tq = ROW_TILE
    M = xk.shape[0]

    def body(q_ref, k_ref, v_ref, o_ref):
        p = _xattn_probs(q_ref[...], k_ref[...])
        o_ref[...] = _dot(p, v_ref[...], "nn").astype(BF16)

    qs = _spec((tq, X_HD), lambda h, i: (i, h))
    kv = _spec((M, X_HD), lambda h, i: (0, h))
    return _pcall(
        body, name="xattn_fwd", grid=(X_HEADS, T // tq),
        in_specs=[qs, kv, kv], out_specs=qs, out_shape=_sds((T, D), BF16),
    )(xq, xk, xv)


def _xattn_bwd(xq, xk, xv, dxo):
    T = xq.shape[0]
    tq = ROW_TILE
    M = xk.shape[0]

    def body(q_ref, k_ref, v_ref, do_ref, dq_ref, dk_ref, dv_ref):
        first = pl.program_id(1) == 0
        q, k, v, do = q_ref[...], k_ref[...], v_ref[...], do_ref[...]
        p = _xattn_probs(q, k)
        dp = _dot(do, v, "nt")
        ds = p * (dp - jnp.sum(dp * p, axis=-1, keepdims=True)) * (X_HD ** -0.5)
        dq_ref[...] = _dot(ds, k, "nn").astype(BF16)
        _accumulate(dk_ref, _dot(ds, q, "tn"), first)
        _accumulate(dv_ref, _dot(p, do, "tn"), first)

    qs = _spec((tq, X_HD), lambda h, i: (i, h))
    kv = _spec((M, X_HD), lambda h, i: (0, h))
    return _pcall(
        body, name="xattn_bwd", grid=(X_HEADS, T // tq),
        in_specs=[qs, kv, kv, qs], out_specs=[qs, kv, kv],
        out_shape=[_sds((T, D), BF16), _sds((M, D), F32), _sds((M, D), F32)],
    )(xq, xk, xv, dxo)


def _final_loss(x, g, tgt):
    T = x.shape[0]
    tm = ROW_TILE

    def fn(irefs, orefs, ids):
        xv, gv = irefs[0][...], irefs[1][...]
        err = _rms_fwd(xv, gv) - irefs[2][...]
        lp = 0.5 * jnp.sum(jnp.mean(err * err, axis=-1, keepdims=True), axis=0, keepdims=True)
        first = ids[0] == 0
        _accumulate(orefs[0], jnp.broadcast_to(lp, (1, 128)), first)
        dx, dgp = _rms_bwd(xv, gv, err * (1.0 / D))
        orefs[1][...] = dx
        _accumulate(orefs[2], dgp, first)

    row = _spec((tm, D), lambda i: (i, 0))
    vec = _spec((1, D), lambda i: (0, 0))
    return _rowwise(
        "final_loss", fn, [(x, row), (g, vec), (tgt, row)],
        [(_sds((1, 128), F32), _spec((1, 128), lambda i: (0, 0))), (_sds((T, D), F32), row),
         (_sds((1, D), F32), vec)],
        (T // tm,))


def _adamw(name, w, g, m, v):
    R, C = w.shape
    tr = R
    for cand in (512, 352, 256):
        if R % cand == 0:
            tr = cand
            break

    def fn(irefs, orefs, ids):
        delta, mn, vn = _adamw_update(*(r[...] for r in irefs))
        orefs[0][...] = delta
        orefs[1][...] = mn
        orefs[2][...] = vn

    blk = _spec((tr, C), lambda i: (i, 0))
    return _rowwise(name, fn, [(w, blk), (g, blk), (m, blk), (v, blk)],
                    [(_sds((R, C), F32), blk)] * 3, (R // tr,))


def _adamw_update(wv, gv, mv, vv):
    c1 = 1.0 - ADAM_B1 ** ADAM_STEP
    c2 = 1.0 - ADAM_B2 ** ADAM_STEP
    mn = ADAM_B1 * mv + (1.0 - ADAM_B1) * gv
    vn = ADAM_B2 * vv + (1.0 - ADAM_B2) * (gv * gv)
    delta = -ADAM_LR * ((mn / c1) / (jnp.sqrt(vn / c2) + ADAM_EPS) + ADAM_WD * wv)
    return delta, mn, vn


def _adamw_halves(name, w, mine, theirs, widx, m, v, core):
    R, C = w.shape
    H = R // 2
    tr = H
    while tr * C * 4 > (1 << 20) and tr % 16 == 0:
        tr //= 2
    nb = H // tr

    def body(core_ref, w_ref, mine_ref, theirs_ref, m_ref, v_ref, g_out, d_out, m_out, v_out):
        gv = jnp.where(pl.program_id(0) == core_ref[0], mine_ref[...], theirs_ref[...])
        delta, mn, vn = _adamw_update(w_ref[...], gv, m_ref[...], v_ref[...])
        g_out[...] = gv
        d_out[...] = delta
        m_out[...] = mn
        v_out[...] = vn

    full = pl.BlockSpec((tr, C), lambda h, i, core_ref: (h * nb + i, 0))
    half = pl.BlockSpec((None, tr, C), lambda h, i, core_ref: (widx, i, 0))
    return _pcall(
        body, name=name, grid=(2, nb), num_prefetch=1,
        in_specs=[full, half, half, full, full], out_specs=[full] * 4,
        out_shape=[_sds((R, C), F32)] * 4,
    )(core, w, mine, theirs, m, v)


def _rmsnorm(name, x, g):
    M = x.shape[0]
    tm = min(ROW_TILE, M)

    def fn(irefs, orefs, ids):
        orefs[0][...] = _rms_fwd(irefs[0][...], irefs[1][...]).astype(BF16)

    row = _spec((tm, D), lambda i: (i, 0))
    return _rowwise(name, fn, [(x, row), (g, _spec((1, D), lambda i: (0, 0)))],
                    [(_sds((M, D), BF16), row)], (M // tm,))[0]


WEIGHT_AT = {
    "ffn1_w1": ("col1", 0), "ffn1_w3": ("col1", 1), "ffn1_w2": ("row2a", 0),
    "w_ret_o": ("sqA", 0), "w_lru_o": ("sqA", 1), "w_out": ("sqA", 2),
    "w_xq": ("sqB", 0), "w_xk": ("sqB", 1), "w_xv": ("sqB", 2), "w_xo": ("sqB", 3),
    "ffn2_w1": ("col2", 0), "ffn2_w3": ("col2", 1), "ffn2_w2": ("row2b", 0),
}


def _local_step(x, mem, tgt, gw, sm, big):
    T = x.shape[0]
    tm = ROW_TILE

    def wt(name):
        key, idx = WEIGHT_AT[name]
        return gw[key], idx

    row3 = lambda i, j, r: (i, 0)
    vec3 = lambda i, j, r: (0, 0)
    rowD = _spec((tm, D), row3)
    vecD = _spec((1, D), vec3)

    def residual_norm(acc, erefs, orefs, ids):
        xo = erefs[0][...] + acc
        orefs[0][...] = xo
        orefs[1][...] = _rms_fwd(xo, erefs[1][...]).astype(BF16)

    def res_norm_io(x_res, g):
        return ([(x_res, rowD), (g, vecD)],
                [(_sds((T, D), F32), rowD), (_sds((T, D), BF16), rowD)])

    h1 = _rmsnorm("ffn1_norm", x, sm["ffn1_norm"])
    a1, b1, s1 = _ffn_up("ffn1_up", h1, *wt("ffn1_w1"), wt("ffn1_w3")[1])
    x1, h2 = _ffn_down("ffn1_down", s1, *wt("ffn1_w2"), x, sm["mix_norm"])

    u = _gemm(
        "mix_in",
        [(h2, rowD, gw["win"], _spec((None, None, D, IN_BLK), lambda i, j, r: (j, 0, 0, 0)), "nn")],
        (T // tm, N_CHIPS, 1),
        [(_sds((T, 5120), F32), _spec((tm, IN_BLK), lambda i, j, r: (i, j)))], (tm, IN_BLK))[0]

    def gate_epilogue(acc, erefs, orefs, ids):
        orefs[0][...] = _sigmoid(acc + erefs[0][...])

    gates = _gemm(
        "mix_gates",
        [(h2, rowD, gw["wbg"], _spec((None, None, D, BG_BLK), lambda i, j, r: (j, 0, 0, 0)), "nn")],
        (T // tm, N_CHIPS, 1),
        [(_sds((T, 2 * D), F32), _spec((tm, BG_BLK), lambda i, j, r: (i, j)))], (tm, BG_BLK),
        [(sm["b_branch_gate"], _spec((1, BG_BLK), lambda i, j, r: (0, j)))], gate_epilogue)[0]

    consts = _retention_constants(T)
    qr, kr, ret, yr, states = _ret_fwd(u, consts, sm["ret_gn"])

    conv_w = gw["conv"][:, 0].transpose(1, 0, 2).reshape(CONV_TAPS, D)
    xc, rg, ig, av, bx = _lru_gates_fwd(u, conv_w, sm["conv_b"], sm["w_rgate"], sm["b_rgate"],
                                        sm["w_igate"], sm["b_igate"], sm["lru_lambda"])
    a3 = av.reshape(T, LRU_BLOCKS, LRU_BLOCK)
    hl = _lru_scan("lru_scan_fwd", a3, bx.reshape(T, LRU_BLOCKS, LRU_BLOCK), False).reshape(T, D)

    row1 = _spec((tm, D), lambda i: (i, 0))
    glru1 = _spec((tm, D), lambda i: (i, 4))

    def lru_out(irefs, orefs, ids):
        gl, _ = _gelu_and_grad(irefs[1][...])
        orefs[0][...] = (irefs[0][...] * gl).astype(BF16)

    yl = _rowwise("lru_out", lru_out, [(hl, row1), (u, glru1)], [(_sds((T, D), BF16), row1)], (T // tm,))[0]

    y_ret = _proj_sq("y_ret", yr, *wt("w_ret_o"), "nn")[0]

    def merge_epilogue(acc, erefs, orefs, ids):
        orefs[0][...] = acc
        orefs[1][...] = (erefs[0][...] * erefs[2][...] + erefs[1][...] * acc).astype(BF16)

    y_lru, merged = _proj_sq(
        "y_lru", yl, *wt("w_lru_o"), "nn",
        extras=[(gates, _spec((tm, D), lambda i, j, r: (i, 0))), (gates, _spec((tm, D), lambda i, j, r: (i, 1))),
                (y_ret, rowD)],
        epilogue=merge_epilogue,
        outs=[(_sds((T, D), F32), rowD), (_sds((T, D), BF16), rowD)])

    ex, ou = res_norm_io(x1, sm["xattn_norm"])
    x2, hq = _proj_sq("mix_out", merged, *wt("w_out"), "nn", extras=ex, epilogue=residual_norm, outs=ou)

    m = _rmsnorm("mem_norm", mem, sm["mem_norm"])
    xq = _proj_sq("xq", hq, *wt("w_xq"), "nn", BF16)[0]
    xk = _proj_sq("xk", m, *wt("w_xk"), "nn", BF16)[0]
    xv = _proj_sq("xv", m, *wt("w_xv"), "nn", BF16)[0]
    xo = _xattn_fwd(xq, xk, xv)
    ex, ou = res_norm_io(x2, sm["ffn2_norm"])
    x3, h3 = _proj_sq("xattn_out", xo, *wt("w_xo"), "nn", extras=ex, epilogue=residual_norm, outs=ou)

    a2, b2, s2 = _ffn_up("ffn2_up", h3, *wt("ffn2_w1"), wt("ffn2_w3")[1])
    x4 = _ffn_down("ffn2_down", s2, *wt("ffn2_w2"), x3)[0]
    loss, dx4, dg_final = _final_loss(x4, sm["final_norm"], tgt)

    dx3, dg_ffn2 = _ffn_bwd("ffn2", dx4, h3, a2, b2, s2, *wt("ffn2_w1"), wt("ffn2_w3")[1],
                            *wt("ffn2_w2"), x3, sm["ffn2_norm"], big)

    dxo = _proj_sq("d_xo", dx3, *wt("w_xo"), "nt", BF16)[0]
    big["w_xo"] = _dw_sq("dw_xo", xo, dx3)[None]
    dxq, dxk, dxv = _xattn_bwd(xq, xk, xv, dxo)
    big["w_xq"] = _dw_sq("dw_xq", hq, dxq)[None]
    ex, ou = _rms_bwd_io(x2, sm["xattn_norm"], dx3, T, tm)
    dx2, dg_xattn = _proj_sq("d_hq", dxq, *wt("w_xq"), "nt", extras=ex, epilogue=_rms_bwd_epilogue, outs=ou)
    big["w_xk"] = _dw_sq("dw_xk", m, dxk)[None]
    big["w_xv"] = _dw_sq("dw_xv", m, dxv)[None]

    M = mem.shape[0]

    def mem_norm_epilogue(acc, erefs, orefs, ids):
        _, dgp = _rms_bwd(erefs[0][...], erefs[1][...], acc)
        orefs[0][...] = dgp

    wsq_spec = lambda idx: _spec((N_CHIPS, None, SQ_BLK, D), lambda i, j, r: (0, idx, 0, 0))
    memD = _spec((M, D), row3)
    dg_mem = _gemm(
        "d_mem_norm",
        [(dxk, memD, wt("w_xk")[0], wsq_spec(wt("w_xk")[1]), "nt"),
         (dxv, memD, wt("w_xv")[0], wsq_spec(wt("w_xv")[1]), "nt")],
        (1, 1, 1), [(_sds((1, D), F32), vecD)], (M, D),
        [(mem, memD), (sm["mem_norm"], vecD)], mem_norm_epilogue)[0]

    def merged_bwd_epilogue(acc, erefs, orefs, ids):
        gr, gl, yrv, ylv = (e[...] for e in erefs)
        orefs[0][...] = (acc * gr).astype(BF16)
        orefs[1][...] = (acc * gl).astype(BF16)
        dgr = acc * yrv * gr * (1.0 - gr)
        dgl = acc * ylv * gl * (1.0 - gl)
        orefs[2][:, :D] = dgr.astype(BF16)
        orefs[2][:, D:] = dgl.astype(BF16)
        dbb = jnp.concatenate([jnp.sum(dgr, axis=0, keepdims=True), jnp.sum(dgl, axis=0, keepdims=True)], axis=1)
        _accumulate(orefs[3], dbb, ids[0] == 0)

    dy_ret, dy_lru, dgpre, db_bg = _proj_sq(
        "d_merged", dx2, *wt("w_out"), "nt",
        extras=[(gates, _spec((tm, D), lambda i, j, r: (i, 0))), (gates, _spec((tm, D), lambda i, j, r: (i, 1))),
                (y_ret, rowD), (y_lru, rowD)],
        epilogue=merged_bwd_epilogue,
        outs=[(_sds((T, D), BF16), rowD), (_sds((T, D), BF16), rowD),
              (_sds((T, 2 * D), BF16), _spec((tm, 2 * D), row3)),
              (_sds((1, 2 * D), F32), _spec((1, 2 * D), vec3))])
    big["w_out"] = _dw_sq("dw_out", merged, dx2)[None]
    dyr = _proj_sq("d_yr", dy_ret, *wt("w_ret_o"), "nt")[0]
    big["w_ret_o"] = _dw_sq("dw_ret_o", yr, dy_ret)[None]
    dyl = _proj_sq("d_yl", dy_lru, *wt("w_lru_o"), "nt")[0]
    big["w_lru_o"] = _dw_sq("dw_lru_o", yl, dy_lru)[None]

    dq, dk, dv, dgr, dg_retgn = _ret_bwd(dyr, ret, u, qr, kr, states, consts, sm["ret_gn"])

    def lru_out_bwd(irefs, orefs, ids):
        gl, dgl = _gelu_and_grad(irefs[2][...])
        dyl_v = irefs[0][...]
        orefs[0][...] = dyl_v * gl
        orefs[1][...] = (dyl_v * irefs[1][...] * dgl).astype(BF16)

    dhl, dglru = _rowwise("lru_out_bwd", lru_out_bwd, [(dyl, row1), (hl, row1), (u, glru1)],
                          [(_sds((T, D), F32), row1), (_sds((T, D), BF16), row1)], (T // tm,))
    lmb = _lru_scan("lru_scan_bwd", a3, dhl.reshape(T, LRU_BLOCKS, LRU_BLOCK), True).reshape(T, D)
    dxl, dw_r, dw_i, dvec, dcw = _lru_gates_bwd(lmb, hl, av, rg, ig, xc, u, conv_w,
                                                sm["w_rgate"], sm["w_igate"], sm["lru_lambda"])

    du = jnp.concatenate([dq, dk, dv, dgr, dxl, dglru], axis=1)
    tk = ROW_TILE
    big["w_in"] = _gemm(
        "dw_in",
        [(h2, _spec((tk, D), lambda j, n, r: (r, 0)), du, _spec((tk, IN_BLK), lambda j, n, r: (r, j)), "tn")],
        (N_CHIPS, 1, T // tk),
        [(_sds((N_CHIPS, D, IN_BLK), F32), _spec((None, D, IN_BLK), lambda j, n, r: (j, 0, 0)))],
        (D, IN_BLK))[0][None]
    big["w_branch_gate"] = _gemm(
        "dw_bg",
        [(h2, _spec((tk, D), lambda j, n, r: (r, 0)), dgpre, _spec((tk, BG_BLK), lambda j, n, r: (r, j)), "tn")],
        (N_CHIPS, 1, T // tk),
        [(_sds((N_CHIPS, D, BG_BLK), F32), _spec((None, D, BG_BLK), lambda j, n, r: (j, 0, 0)))],
        (D, BG_BLK))[0][None]
    ex, ou = _rms_bwd_io(x1, sm["mix_norm"], dx2, T, tm)
    dx1, dg_mix = _gemm(
        "d_h2",
        [(du, _spec((tm, IN_BLK), lambda i, j, r: (i, r)),
          gw["win"], _spec((None, None, D, IN_BLK), lambda i, j, r: (r, 0, 0, 0)), "nt"),
         (dgpre, _spec((tm, BG_BLK), lambda i, j, r: (i, r)),
          gw["wbg"], _spec((None, None, D, BG_BLK), lambda i, j, r: (r, 0, 0, 0)), "nt")],
        (T // tm, 1, N_CHIPS), ou, (tm, D), ex, _rms_bwd_epilogue)

    grad_x, dg_ffn1 = _ffn_bwd("ffn1", dx1, h1, a1, b1, s1, *wt("ffn1_w1"), wt("ffn1_w3")[1],
                               *wt("ffn1_w2"), x, sm["ffn1_norm"], big)

    small = {
        "ffn1_norm": dg_ffn1, "mix_norm": dg_mix, "ret_gn": dg_retgn, "conv_b": dvec[3:4],
        "b_rgate": dvec[0:1], "b_igate": dvec[1:2], "lru_lambda": dvec[2:3], "xattn_norm": dg_xattn,
        "mem_norm": dg_mem, "ffn2_norm": dg_ffn2, "final_norm": dg_final, "b_branch_gate": db_bg,
        "conv_w": dcw, "w_rgate": dw_r, "w_igate": dw_i,
    }
    return loss, grad_x, small


ANY_SPEC = pl.BlockSpec(memory_space=pl.ANY)
VMEM_SPEC = pl.BlockSpec(memory_space=pltpu.VMEM)
N_PEER_CHIPS = N_CHIPS - 1


def _mesh_position():
    x, y, c = lax.axis_index("x"), lax.axis_index("y"), lax.axis_index("c")
    chips = [(1 - x, y), (x, 1 - y), (1 - x, 1 - y)]
    return x, y, c, chips


def _chip_index(x, y):
    return 2 * x + y


def _rows_half(ref, axis, h):
    n = ref.shape[axis] // 2
    idx = [slice(None)] * len(ref.shape)
    idx[axis] = pl.ds(pl.multiple_of(h * n, 16), n)
    return ref.at[tuple(idx)]


def _remote(src, dst, send_sem, recv_sem, device):
    return pltpu.make_async_remote_copy(src_ref=src, dst_ref=dst, send_sem=send_sem, recv_sem=recv_sem,
                                        device_id=device, device_id_type=MESH)


def _gather_chips_task(shards, split, landed):
    keys = list(shards)
    n = len(keys)

    def operands():
        chip_me = _chip_index(lax.axis_index("x"), lax.axis_index("y"))
        bases = [lax.dynamic_update_slice(lax.empty((N_CHIPS,) + shards[k].shape, shards[k].dtype), shards[k][None],
                                          (chip_me,) + (0,) * shards[k].ndim) for k in keys]
        return [shards[k] for k in keys] + bases

    def make(ins, outs, send_sem, recv_sem):
        x, y, c, chips = _mesh_position()
        s_me = _chip_index(x, y)
        starts, arrivals = [], []
        for g in range(n):
            mine = _rows_half(ins[g], 1, c) if split else ins[g]
            for k, chip in enumerate(chips):
                def landing(s):
                    o = outs[g].at[s]
                    return _rows_half(o, 1, c) if split else o
                starts.append(_remote(mine, landing(s_me), send_sem(3 * g + k), recv_sem(3 * g + k), (*chip, c)))
                got = landing(_chip_index(*chip))
                arrivals.append(functools.partial(_remote, got, got, send_sem(3 * g + k), recv_sem(3 * g + k),
                                                  (*chip, c)))
        return starts, arrivals

    def finish(res):
        landed.update(zip(keys, res))

    return _Task(operands, lambda: [_sds((N_CHIPS,) + shards[k].shape, shards[k].dtype) for k in keys],
                 {n + g: g for g in range(n)}, 3 * n, make, finish)


def _gather_sibling_task(keys, landed, ready):
    n = len(keys)

    def make(ins, outs, send_sem, recv_sem):
        x, y, c, chips = _mesh_position()
        starts, arrivals = [], []
        for g in range(n):
            for k, chip in enumerate(chips):
                o = outs[g].at[_chip_index(*chip)]
                got, other = _rows_half(o, 1, c), _rows_half(o, 1, 1 - c)
                starts.append(_remote(got, got, send_sem(3 * g + k), recv_sem(3 * g + k), (x, y, 1 - c)))
                arrivals.append(functools.partial(_remote, other, other, send_sem(3 * g + k), recv_sem(3 * g + k),
                                                  (x, y, 1 - c)))
        return starts, arrivals

    def finish(res):
        ready.update(zip(keys, res))

    return _Task(lambda: [landed[k] for k in keys], lambda: [_sds(landed[k].shape, landed[k].dtype) for k in keys],
                 {g: g for g in range(n)}, 3 * n, make, finish)


def _pair_swap_task(names, big, got):
    n = len(names)

    def make(ins, outs, send_sem, recv_sem):
        x, y, c, _ = _mesh_position()
        copies = [_remote(_rows_half(ins[a], 2, 1 - c), outs[a], send_sem(a), recv_sem(a), (x, y, 1 - c))
                  for a in range(n)]
        return copies, [functools.partial(lambda cp: cp, cp) for cp in copies]

    def shapes():
        return [_sds(big[k].shape[:2] + (big[k].shape[2] // 2, big[k].shape[3]), F32) for k in names]

    return _Task(lambda: [big[k] for k in names], shapes, {}, n, make, lambda res: got.update(zip(names, res)))


def _rs_pair_sum(name, full, got, core):
    nw, ns, R, C = full.shape
    half = R // 2

    def body(core_ref, a_ref, b_ref, o_ref):
        o_ref[...] = (a_ref[...] + b_ref[...]).astype(BF16)

    blk = lambda fn: pl.BlockSpec((None, None, half, C), fn)
    return _pcall(
        body, name=name, grid=(nw, ns), num_prefetch=1,
        in_specs=[blk(lambda w, s, core_ref: (w, s, core_ref[0], 0)), blk(lambda w, s, core_ref: (w, s, 0, 0))],
        out_specs=blk(lambda w, s, core_ref: (w, s, 0, 0)),
        out_shape=_sds((nw, ns, half, C), BF16),
    )(core, full, got)


def _chip_exchange_task(names, pair_sums, by_source):
    n = len(names)

    def make(ins, outs, send_sem, recv_sem):
        x, y, c, chips = _mesh_position()
        s_me = _chip_index(x, y)
        starts, arrivals = [], []
        for a in range(n):
            for k, chip in enumerate(chips):
                s_k = _chip_index(*chip)
                starts.append(_remote(ins[a].at[:, s_k], outs[a].at[:, s_me], send_sem(3 * a + k), recv_sem(3 * a + k),
                                      (*chip, c)))
                got = outs[a].at[:, s_k]
                arrivals.append(functools.partial(_remote, got, got, send_sem(3 * a + k), recv_sem(3 * a + k),
                                                  (*chip, c)))
        return starts, arrivals

    return _Task(lambda: [pair_sums[k] for k in names],
                 lambda: [_sds(pair_sums[k].shape, pair_sums[k].dtype) for k in names],
                 {}, 3 * n, make, lambda res: by_source.update(zip(names, res)))


def _rs_chip_sum(name, own, parts, chip):
    nw, ns, H, C = parts.shape

    def body(chip_ref, own_ref, *rest):
        prefs, o_ref = rest[:ns], rest[ns]
        me = chip_ref[0]
        own_v = own_ref[...].astype(F32)
        tot = None
        for s in range(ns):
            term = jnp.where(me == s, own_v, prefs[s][...].astype(F32))
            tot = term if tot is None else tot + term
        o_ref[...] = tot

    blk = lambda fn: pl.BlockSpec((None, None, H, C), fn)

    def part_spec(s):
        return blk(lambda w, chip_ref: (w, jnp.where(chip_ref[0] == s, (s + 1) % ns, s), 0, 0))

    return _pcall(
        body, name=name, grid=(nw,), num_prefetch=1,
        in_specs=[blk(lambda w, chip_ref: (w, chip_ref[0], 0, 0))] + [part_spec(s) for s in range(ns)],
        out_specs=pl.BlockSpec((None, H, C), lambda w, chip_ref: (w, 0, 0)),
        out_shape=_sds((nw, H, C), F32),
    )(chip, own, *([parts] * ns))


def _pair_gather_task(names, halves, sibling_halves):
    n = len(names)

    def make(ins, outs, send_sem, recv_sem):
        x, y, c, _ = _mesh_position()
        copies = [_remote(ins[a], outs[a], send_sem(a), recv_sem(a), (x, y, 1 - c)) for a in range(n)]
        return copies, [functools.partial(lambda cp: cp, cp) for cp in copies]

    return _Task(lambda: [halves[k] for k in names], lambda: [_sds(halves[k].shape, F32) for k in names],
                 {}, n, make, lambda res: sibling_halves.update(zip(names, res)))


def _small_allreduce(v):
    R, C = v.shape

    def body(v_ref, o_ref, sib_buf, pair_buf, chip_buf, send_sems, recv_sems):
        x, y, c, chips = _mesh_position()
        s_me = _chip_index(x, y)
        swap = _remote(v_ref, sib_buf, send_sems.at[0], recv_sems.at[0], (x, y, 1 - c))
        swap.start()
        swap.wait()
        pair_buf[...] = v_ref[...] + sib_buf[...]
        chip_buf[s_me] = pair_buf[...]
        sends = [_remote(pair_buf, chip_buf.at[s_me], send_sems.at[1 + k], recv_sems.at[1 + k], (*chip, c))
                 for k, chip in enumerate(chips)]
        for cp in sends:
            cp.start()
        for k, chip in enumerate(chips):
            got = chip_buf.at[_chip_index(*chip)]
            _remote(got, got, send_sems.at[1 + k], recv_sems.at[1 + k], (*chip, c)).wait_recv()
        for cp in sends:
            cp.wait_send()
        o_ref[...] = ((chip_buf[0] + chip_buf[1]) + chip_buf[2]) + chip_buf[3]

    return pl.pallas_call(
        body, name="small_allreduce",
        in_specs=[VMEM_SPEC], out_specs=VMEM_SPEC, out_shape=_sds((R, C), F32),
        scratch_shapes=[pltpu.VMEM((R, C), F32), pltpu.VMEM((R, C), F32), pltpu.VMEM((N_CHIPS, R, C), F32),
                        pltpu.SemaphoreType.DMA((1 + N_PEER_CHIPS,)), pltpu.SemaphoreType.DMA((1 + N_PEER_CHIPS,))],
        compiler_params=pltpu.CompilerParams(vmem_limit_bytes=VMEM_LIMIT_BYTES),
    )(v)


SMALL_LAYOUT = [("ffn1_norm", 1), ("mix_norm", 1), ("ret_gn", 1), ("conv_b", 1), ("b_rgate", 1), ("b_igate", 1),
                ("lru_lambda", 1), ("xattn_norm", 1), ("mem_norm", 1), ("ffn2_norm", 1), ("final_norm", 1),
                ("b_branch_gate", 2), ("conv_w", CONV_TAPS), ("w_rgate", LRU_BLOCK), ("w_igate", LRU_BLOCK)]
SMALL_ROWS = 280
WEIGHT_ORDER = ["ffn1_norm", "ffn1_w1", "ffn1_w3", "ffn1_w2", "mix_norm", "w_in", "ret_gn", "w_ret_o", "conv_w",
                "conv_b", "w_rgate", "b_rgate", "w_igate", "b_igate", "lru_lambda", "w_lru_o", "w_branch_gate",
                "b_branch_gate", "w_out", "xattn_norm", "mem_norm", "w_xq", "w_xk", "w_xv", "w_xo", "ffn2_norm",
                "ffn2_w1", "ffn2_w3", "ffn2_w2", "final_norm"]


def _pack_small(parts):
    rows = [parts[name].reshape(n, D) for name, n in SMALL_LAYOUT]
    used = sum(n for _, n in SMALL_LAYOUT)
    rows.append(jnp.zeros((SMALL_ROWS - used, D), F32))
    return jnp.concatenate(rows, axis=0)


def _unpack_small(packed, shapes):
    out, r = {}, 0
    for name, n in SMALL_LAYOUT:
        out[name] = packed[r:r + n].reshape(shapes[name])
        r += n
    return out


def kernel(x, mem, ffn1_norm, ffn1_w1, ffn1_w3, ffn1_w2, mix_norm, w_in, ret_gn, w_ret_o, conv_w, conv_b, w_rgate, b_rgate, w_igate, b_igate, lru_lambda, w_lru_o, w_branch_gate, b_branch_gate, w_out, xattn_norm, mem_norm, w_xq, w_xk, w_xv, w_xo, ffn2_norm, ffn2_w1, ffn2_w3, ffn2_w2, final_norm, loss_target, m_ffn1_norm, m_ffn1_w1, m_ffn1_w3, m_ffn1_w2, m_mix_norm, m_w_in, m_ret_gn, m_w_ret_o, m_conv_w, m_conv_b, m_w_rgate, m_b_rgate, m_w_igate, m_b_igate, m_lru_lambda, m_w_lru_o, m_w_branch_gate, m_b_branch_gate, m_w_out, m_xattn_norm, m_mem_norm, m_w_xq, m_w_xk, m_w_xv, m_w_xo, m_ffn2_norm, m_ffn2_w1, m_ffn2_w3, m_ffn2_w2, m_final_norm, v_ffn1_norm, v_ffn1_w1, v_ffn1_w3, v_ffn1_w2, v_mix_norm, v_w_in, v_ret_gn, v_w_ret_o, v_conv_w, v_conv_b, v_w_rgate, v_b_rgate, v_w_igate, v_b_igate, v_lru_lambda, v_w_lru_o, v_w_branch_gate, v_b_branch_gate, v_w_out, v_xattn_norm, v_mem_norm, v_w_xq, v_w_xk, v_w_xv, v_w_xo, v_ffn2_norm, v_ffn2_w1, v_ffn2_w3, v_ffn2_w2, v_final_norm):
    given = dict(locals())
    w = {n: given[n] for n in WEIGHT_ORDER}
    mom = {n: given["m_" + n] for n in WEIGHT_ORDER}
    var = {n: given["v_" + n] for n in WEIGHT_ORDER}
    chip = _chip_index(lax.axis_index("x"), lax.axis_index("y"))
    core = lax.axis_index("c").astype(jnp.int32).reshape(1)

    chip_id = chip.astype(jnp.int32).reshape(1)
    sm = {n: w[n] for n in ["ffn1_norm", "mix_norm", "ret_gn", "conv_b", "b_rgate", "b_igate", "lru_lambda",
                            "xattn_norm", "mem_norm", "ffn2_norm", "b_branch_gate"]}
    sm["final_norm"] = w["final_norm"].reshape(1, D)
    sm["w_rgate"] = w["w_rgate"][0]
    sm["w_igate"] = w["w_igate"][0]

    stack = lambda names: jnp.stack([w[n][0] for n in names], axis=0).astype(BF16)
    shard = {"col1": stack(["ffn1_w1", "ffn1_w3"]), "row2a": stack(["ffn1_w2"]), "win": stack(["w_in"]),
             "wbg": stack(["w_branch_gate"]), "sqA": stack(["w_ret_o", "w_lru_o", "w_out"]),
             "sqB": stack(["w_xq", "w_xk", "w_xv", "w_xo"]), "col2": stack(["ffn2_w1", "ffn2_w3"]),
             "row2b": stack(["ffn2_w2"]), "conv": w["conv_w"]}
    gw, landed = {}, {}
    over_chips = lambda keys: _gather_chips_task({k: shard[k] for k in keys}, True, landed)
    to_sibling = lambda keys: _gather_sibling_task(keys, landed, gw)

    big, got, pair_sums, by_source, halves, sibling_halves, outs = {}, {}, {}, {}, {}, {}, {}
    pair_swap = lambda names: _pair_swap_task(names, big, got)
    exchange = lambda names: _chip_exchange_task(names, pair_sums, by_source)
    pair_gather = lambda names: _pair_gather_task(names, halves, sibling_halves)

    def pair_sum(names):
        for n in names:
            pair_sums[n] = _rs_pair_sum("rs_pair_sum_" + n, big[n], got[n], core)

    def chip_sum(names):
        for n in names:
            halves[n] = _rs_chip_sum("rs_chip_sum_" + n, pair_sums[n], by_source[n], chip_id)

    def adamw(names):
        for n in names:
            g, d, nm, nv = _adamw_halves("adamw_" + n, w[n][0], halves[n], sibling_halves[n], 0, mom[n][0],
                                         var[n][0], core)
            outs[n] = (g[None], d[None], nm[None], nv[None])

    do = lambda fn, names: functools.partial(fn, names)
    ffn2_grads = ["ffn2_w2", "ffn2_w1", "ffn2_w3"]
    xattn_grads = ["w_xo", "w_xq", "w_xk", "w_xv"]
    mix_out_grads = ["w_out", "w_ret_o", "w_lru_o"]
    mix_in_grads = ["w_in", "w_branch_gate"]
    plan = _Plan()
    plan.tasks = {
        "ag_first_chips": [over_chips(["col1", "row2a"])],
        "ag_first_sibling": [to_sibling(["col1", "row2a"])],
        "ffn1_up": [over_chips(["win", "wbg"])],
        "ffn1_down": [to_sibling(["win", "wbg"]), over_chips(["sqA"]),
                      _gather_chips_task({"conv": shard["conv"]}, False, gw)],
        "mix_in": [to_sibling(["sqA"]), over_chips(["sqB"])],
        "mix_gates": [to_sibling(["sqB"])],
        "ret_fwd": [over_chips(["col2"])],
        "lru_gates_fwd": [to_sibling(["col2"]), over_chips(["row2b"])],
        "y_ret": [to_sibling(["row2b"])],
        "ffn2_dh": [pair_swap(ffn2_grads)],
        "dw_xo": [exchange(["ffn2_w2"])],
        "xattn_bwd": [exchange(["ffn2_w1"])],
        "dw_xq": [exchange(["ffn2_w3"])],
        "d_merged": [pair_swap(xattn_grads), pair_gather(ffn2_grads)],
        "dw_out": [exchange(["w_xo", "w_xq"])],
        "dw_ret_o": [exchange(["w_xk", "w_xv"])],
        "ret_bwd": [pair_swap(mix_out_grads), pair_gather(xattn_grads)],
        "lru_gates_bwd": [exchange(mix_out_grads)],
        "dw_in": [pair_gather(mix_out_grads)],
        "d_h2": [pair_swap(mix_in_grads)],
        "ffn1_bwd_mid": [exchange(["w_in"])],
        "ffn1_dw2": [exchange(["w_branch_gate"])],
        "ffn1_dw1": [pair_gather(mix_in_grads), pair_swap(["ffn1_w2"])],
        "ffn1_dw3": [pair_swap(["ffn1_w1"]), exchange(["ffn1_w2"])],
        "ffn1_dh": [pair_swap(["ffn1_w3"]), exchange(["ffn1_w1"])],
        "rs_last": [exchange(["ffn1_w3"]), pair_gather(["ffn1_w2", "ffn1_w1"])],
        "adamw_ffn1_w2": [pair_gather(["ffn1_w3"])],
    }
    plan.after = {
        "ffn2_dh": [do(pair_sum, ffn2_grads)],
        "dw_xq": [do(chip_sum, ffn2_grads)],
        "d_merged": [do(pair_sum, xattn_grads), do(adamw, ffn2_grads)],
        "dw_ret_o": [do(chip_sum, xattn_grads)],
        "ret_bwd": [do(pair_sum, mix_out_grads), do(adamw, xattn_grads)],
        "lru_gates_bwd": [do(chip_sum, mix_out_grads)],
        "dw_in": [do(adamw, mix_out_grads)],
        "d_h2": [do(pair_sum, mix_in_grads)],
        "ffn1_dw2": [do(chip_sum, mix_in_grads)],
        "ffn1_dw1": [do(pair_sum, ["ffn1_w2"]), do(adamw, mix_in_grads)],
        "ffn1_dw3": [do(pair_sum, ["ffn1_w1"])],
        "ffn1_dh": [do(pair_sum, ["ffn1_w3"]), do(chip_sum, ["ffn1_w2", "ffn1_w1"])],
        "rs_last": [do(chip_sum, ["ffn1_w3"]), do(adamw, ["ffn1_w2", "ffn1_w1", "ffn1_w3"])],
    }
    global _plan
    _plan = plan
    try:
        _comm_call("ag_first_chips")
        _comm_call("ag_first_sibling")
        loss_part, grad_x, small = _local_step(x[0], mem[0], loss_target[0], gw, sm, big)
        _comm_call("rs_last")
    finally:
        _plan = None
    assert not plan.tasks and not plan.after, (list(plan.tasks), list(plan.after))
    loss = lax.psum(loss_part[0, 0], ("x", "y", "c"))

    small_shapes = {n: w[n].shape for n, _ in SMALL_LAYOUT}
    small_shapes["conv_w"] = (CONV_TAPS, D)
    small_sum = _small_allreduce(_pack_small(small))
    conv_grad = lax.dynamic_slice(small_sum[13:13 + CONV_TAPS], (0, chip * SQ_BLK), (CONV_TAPS, SQ_BLK))
    small_w = {n: w[n] for n, _ in SMALL_LAYOUT}
    small_m = {n: mom[n] for n, _ in SMALL_LAYOUT}
    small_v = {n: var[n] for n, _ in SMALL_LAYOUT}
    pad_cols = lambda a: jnp.pad(a[0], ((0, 0), (0, D - SQ_BLK)))
    for dct in (small_w, small_m, small_v):
        dct["conv_w"] = pad_cols(dct["conv_w"])
    g_pack = lax.dynamic_update_slice(small_sum, jnp.pad(conv_grad, ((0, 0), (0, D - SQ_BLK))), (13, 0))
    d_pack, m_pack, v_pack = _adamw("adamw_small", _pack_small(small_w), g_pack, _pack_small(small_m),
                                    _pack_small(small_v))
    unpacked = [_unpack_small(p, small_shapes) for p in (g_pack, d_pack, m_pack, v_pack)]
    for n, _ in SMALL_LAYOUT:
        if n == "conv_w":
            outs[n] = tuple(u[n][:, :SQ_BLK][None] for u in unpacked)
        else:
            outs[n] = tuple(u[n] for u in unpacked)

    result = [loss, grad_x[None]]
    for k in range(4):
        result += [outs[n][k] for n in WEIGHT_ORDER]
    return tuple(result)
```

```python
import functools
import math

import numpy as np
import jax
import jax.numpy as jnp
from jax import lax
from jax.experimental import pallas as pl
from jax.experimental.pallas import tpu as pltpu

F32 = jnp.float32
BF16 = jnp.bfloat16
MESH = pl.DeviceIdType.MESH

D = 1024
EPS = 1e-6
RET_HEADS = 4
RET_DK = 128
RET_DV = 256
CHUNK = 128
ROPE_BASE = 10000.0
LRU_BLOCKS = 8
LRU_BLOCK = 128
CONV_TAPS = 4
LRU_C = 8.0
D_FF = 2816
X_HEADS = 4
X_HD = 256
N_CHIPS = 4
FF_BLK = D_FF // N_CHIPS
IN_BLK = 5120 // N_CHIPS
BG_BLK = 2048 // N_CHIPS
SQ_BLK = D // N_CHIPS

ADAM_LR = 0.001
ADAM_B1 = 0.9
ADAM_B2 = 0.999
ADAM_EPS = 1e-08
ADAM_WD = 0.01
ADAM_STEP = 10

VMEM_LIMIT_BYTES = 56 * 1024 * 1024
ROW_TILE = 512
WIDE_ROW_TILE = 1024
SCAN_TILE = 256

_DN = {
    "nn": (((1,), (0,)), ((), ())),
    "nt": (((1,), (1,)), ((), ())),
    "tn": (((0,), (0,)), ((), ())),
}


def _cparams(n_axes):
    return pltpu.CompilerParams(dimension_semantics=("arbitrary",) * n_axes,
                                vmem_limit_bytes=VMEM_LIMIT_BYTES)


def _dot(a, b, kind):
    if b.ndim == 3:
        b = b.reshape(b.shape[0] * b.shape[1], b.shape[2])
    return lax.dot_general(a.astype(BF16), b.astype(BF16), _DN[kind], preferred_element_type=F32)


def _sigmoid(x):
    return 1.0 / (1.0 + jnp.exp(-x))


def _log1p_pos(e):
    u = 1.0 + e
    return jnp.where(u == 1.0, e, jnp.log(u) * (e / jnp.where(u == 1.0, 1.0, u - 1.0)))


def _expm1(x):
    u = jnp.exp(x)
    lu = jnp.log(u)
    safe = jnp.where(lu == 0.0, 1.0, lu)
    return jnp.where(u == 1.0, x, (u - 1.0) * (x / safe))


def _softplus(z):
    return jnp.maximum(z, 0.0) + _log1p_pos(jnp.exp(-jnp.abs(z)))


_GELU_C = math.sqrt(2.0 / math.pi)


def _gelu_and_grad(x):
    x2 = x * x
    t = jnp.tanh(_GELU_C * (x + 0.044715 * x * x2))
    g = 0.5 * x * (1.0 + t)
    dg = 0.5 * (1.0 + t) + 0.5 * x * (1.0 - t * t) * (_GELU_C * (1.0 + 3.0 * 0.044715 * x2))
    return g, dg


def _rms_fwd(x, g):
    r = lax.rsqrt(jnp.mean(x * x, axis=-1, keepdims=True) + EPS)
    return (x * r) * g


def _rms_bwd(x, g, dh):
    r = lax.rsqrt(jnp.mean(x * x, axis=-1, keepdims=True) + EPS)
    n = x * r
    dyg = dh * g
    dx = r * (dyg - n * jnp.mean(dyg * n, axis=-1, keepdims=True))
    return dx, jnp.sum(dh * n, axis=0, keepdims=True)


def _accumulate(ref, val, first):
    @pl.when(first)
    def _():
        ref[...] = val

    @pl.when(jnp.logical_not(first))
    def _():
        ref[...] += val


def _sds(shape, dtype):
    return jax.ShapeDtypeStruct(tuple(shape), dtype)


def _spec(shape, fn):
    return pl.BlockSpec(tuple(shape), fn)


class _Task:
    def __init__(self, operands, out_shapes, aliases, nsem, make, finish):
        self.operands, self.out_shapes, self.aliases = operands, out_shapes, aliases
        self.nsem, self.make, self.finish = nsem, make, finish


class _Plan:
    def __init__(self):
        self.tasks, self.after = {}, {}


_plan = None


def _pcall(body, *, name, grid, in_specs, out_specs, out_shape, scratch_shapes=(), num_prefetch=0):
    single = not isinstance(out_shape, (list, tuple))
    out_shape = [out_shape] if single else list(out_shape)
    out_specs = [out_specs] if single else list(out_specs)
    in_specs = list(in_specs)
    scratch_shapes = list(scratch_shapes)
    tasks = _plan.tasks.pop(name, []) if _plan is not None else []
    after = _plan.after.pop(name, []) if _plan is not None else []
    nax = len(grid)

    def run(*operands):
        n_in = len(operands) - num_prefetch
        n_out = len(out_shape)
        t_ops = [t.operands() for t in tasks]
        t_outs = [t.out_shapes() for t in tasks]
        c_ops = [a for ops in t_ops for a in ops]
        c_outs = [s for outs in t_outs for s in outs]
        aliases = {}
        i0, o0 = num_prefetch + n_in, n_out
        for t, ops, outs in zip(tasks, t_ops, t_outs):
            for i_loc, o_loc in t.aliases.items():
                aliases[i0 + i_loc] = o0 + o_loc
            i0 += len(ops)
            o0 += len(outs)
        nsem = sum(t.nsem for t in tasks)

        def wrapped(*refs):
            p = num_prefetch
            pre, ins = refs[:p], refs[p:p + n_in]
            cins = refs[p + n_in:p + n_in + len(c_ops)]
            q = p + n_in + len(c_ops)
            outs, couts = refs[q:q + n_out], refs[q + n_out:q + n_out + len(c_outs)]
            q += n_out + len(c_outs)
            scr = refs[q:q + len(scratch_shapes)]

            def descriptors():
                send_sems, recv_sems = refs[q + len(scratch_shapes):]
                starts, arrivals = [], []
                ci = co = so = 0
                for t, ops, souts in zip(tasks, t_ops, t_outs):
                    s, a = t.make(cins[ci:ci + len(ops)], couts[co:co + len(souts)],
                                  functools.partial(lambda base, k: send_sems.at[base + k], so),
                                  functools.partial(lambda base, k: recv_sems.at[base + k], so))
                    starts += s
                    arrivals += a
                    ci, co, so = ci + len(ops), co + len(souts), so + t.nsem
                return starts, arrivals

            if tasks:
                ids = [pl.program_id(k) for k in range(nax)]
                first = functools.reduce(jnp.logical_and, [i == 0 for i in ids])
                last = functools.reduce(jnp.logical_and, [i == g - 1 for i, g in zip(ids, grid)])

                @pl.when(first)
                def _():
                    for cp in descriptors()[0]:
                        cp.start()

            body(*pre, *ins, *outs, *scr)

            if tasks:
                @pl.when(last)
                def _():
                    starts, arrivals = descriptors()
                    for arrival in arrivals:
                        arrival().wait_recv()
                    for cp in starts:
                        cp.wait_send()

        sems = [pltpu.SemaphoreType.DMA((nsem,)), pltpu.SemaphoreType.DMA((nsem,))] if tasks else []
        res = pl.pallas_call(
            wrapped, name=name,
            grid_spec=pltpu.PrefetchScalarGridSpec(
                num_scalar_prefetch=num_prefetch, grid=tuple(grid),
                in_specs=in_specs + [ANY_SPEC] * len(c_ops),
                out_specs=out_specs + [ANY_SPEC] * len(c_outs),
                scratch_shapes=scratch_shapes + sems),
            out_shape=out_shape + c_outs,
            input_output_aliases=aliases,
            compiler_params=_cparams(nax),
        )(*operands, *c_ops)
        co = n_out
        for t, souts in zip(tasks, t_outs):
            t.finish(res[co:co + len(souts)])
            co += len(souts)
        for fn in after:
            fn()
        return res[0] if single else list(res[:n_out])

    return run


def _comm_call(name):
    def body(o_ref):
        o_ref[...] = jnp.zeros_like(o_ref)

    _pcall(body, name=name, grid=(1,), in_specs=[], out_specs=_spec((8, 128), lambda i: (0, 0)),
           out_shape=_sds((8, 128), F32))()


def _gemm(name, terms, grid, outs, acc_shape, extras=(), epilogue=None):
    kinds = [t[4] for t in terms]
    nt, ne, no = len(terms), len(extras), len(outs)
    nred = grid[-1]
    nax = len(grid)

    def body(*refs):
        trefs = refs[:2 * nt]
        erefs = refs[2 * nt:2 * nt + ne]
        orefs = refs[2 * nt + ne:2 * nt + ne + no]
        ids = [pl.program_id(k) for k in range(nax)]
        tot = None
        for t in range(nt):
            d = _dot(trefs[2 * t][...], trefs[2 * t + 1][...], kinds[t])
            tot = d if tot is None else tot + d

        def finish(acc):
            if epilogue is None:
                orefs[0][...] = acc.astype(orefs[0].dtype)
            else:
                epilogue(acc, erefs, orefs, ids)

        if nred == 1:
            finish(tot)
        else:
            acc_ref = refs[-1]
            r = ids[-1]

            @pl.when(r == 0)
            def _():
                acc_ref[...] = tot

            @pl.when(r > 0)
            def _():
                acc_ref[...] += tot

            @pl.when(r == nred - 1)
            def _():
                finish(acc_ref[...])

    operands, in_specs = [], []
    for a, a_spec, b, b_spec, _ in terms:
        operands += [a, b]
        in_specs += [a_spec, b_spec]
    for e, e_spec in extras:
        operands.append(e)
        in_specs.append(e_spec)
    scratch = [pltpu.VMEM(tuple(acc_shape), F32)] if nred > 1 else []
    return _pcall(body, name=name, grid=tuple(grid), in_specs=in_specs, out_specs=[o[1] for o in outs],
                  out_shape=[o[0] for o in outs], scratch_shapes=scratch)(*operands)


def _rowwise(name, fn, ins, outs, grid):
    ni = len(ins)
    nax = len(grid)

    def body(*refs):
        ids = [pl.program_id(k) for k in range(nax)]
        fn(refs[:ni], refs[ni:], ids)

    return _pcall(body, name=name, grid=tuple(grid), in_specs=[i[1] for i in ins],
                  out_specs=[o[1] for o in outs], out_shape=[o[0] for o in outs])(*[i[0] for i in ins])


def _ffn_up(name, h, wcol, w1_idx, w3_idx):
    T = h.shape[0]
    tm = min(WIDE_ROW_TILE, T)

    def body(h_ref, w1_ref, w3_ref, a_ref, b_ref, s_ref):
        hv = h_ref[...]
        a = _dot(hv, w1_ref[...], "nt")
        b = _dot(hv, w3_ref[...], "nt")
        a_ref[...] = a.astype(BF16)
        b_ref[...] = b.astype(BF16)
        s_ref[...] = ((a * _sigmoid(a)) * b).astype(BF16)

    blk = _spec((None, tm, FF_BLK), lambda j, i: (j, i, 0))
    return _pcall(
        body, name=name, grid=(N_CHIPS, T // tm),
        in_specs=[_spec((tm, D), lambda j, i: (i, 0)),
                  _spec((None, None, FF_BLK, D), lambda j, i: (j, w1_idx, 0, 0)),
                  _spec((None, None, FF_BLK, D), lambda j, i: (j, w3_idx, 0, 0))],
        out_specs=[blk, blk, blk],
        out_shape=[_sds((N_CHIPS, T, FF_BLK), BF16)] * 3,
    )(h, wcol, wcol)


def _ffn_down(name, s, wrow2, w2_idx, x_res, g_next=None):
    T = x_res.shape[0]
    tm = min(WIDE_ROW_TILE, T)
    row = lambda i, j, r: (i, 0)

    def epilogue(acc, erefs, orefs, ids):
        xo = erefs[0][...] + 0.5 * acc
        orefs[0][...] = xo
        if g_next is not None:
            orefs[1][...] = _rms_fwd(xo, erefs[1][...]).astype(BF16)

    extras = [(x_res, _spec((tm, D), row))]
    outs = [(_sds((T, D), F32), _spec((tm, D), row))]
    if g_next is not None:
        extras.append((g_next, _spec((1, D), lambda i, j, r: (0, 0))))
        outs.append((_sds((T, D), BF16), _spec((tm, D), row)))
    return _gemm(
        name,
        [(s, _spec((None, tm, FF_BLK), lambda i, j, r: (r, i, 0)),
          wrow2, _spec((None, None, FF_BLK, D), lambda i, j, r: (r, w2_idx, 0, 0)), "nn")],
        (T // tm, 1, N_CHIPS), outs, (tm, D), extras, epilogue)


def _ffn_bwd_mid(name, dx, wrow2, w2_idx, a, b):
    T = dx.shape[0]
    tm = min(WIDE_ROW_TILE, T)

    def body(dx_ref, w2_ref, a_ref, b_ref, dab_ref):
        ds = _dot(0.5 * dx_ref[...], w2_ref[...], "nt")
        av = a_ref[...].astype(F32)
        sg = _sigmoid(av)
        dab_ref[0] = (ds * b_ref[...].astype(F32) * (sg * (1.0 + av * (1.0 - sg)))).astype(BF16)
        dab_ref[1] = (ds * (av * sg)).astype(BF16)

    blk = _spec((None, tm, FF_BLK), lambda j, i: (j, i, 0))
    return _pcall(
        body, name=name, grid=(N_CHIPS, T // tm),
        in_specs=[_spec((tm, D), lambda j, i: (i, 0)),
                  _spec((None, None, FF_BLK, D), lambda j, i: (j, w2_idx, 0, 0)),
                  blk, blk],
        out_specs=_spec((2, None, tm, FF_BLK), lambda j, i: (0, j, i, 0)),
        out_shape=_sds((2, N_CHIPS, T, FF_BLK), BF16),
    )(dx, wrow2, a, b)


def _rms_bwd_epilogue(acc, erefs, orefs, ids):
    dx, dgp = _rms_bwd(erefs[0][...], erefs[1][...], acc)
    orefs[0][...] = dx + erefs[2][...]
    _accumulate(orefs[1], dgp, ids[0] == 0)


def _rms_bwd_io(x, g, dres, T, tm):
    row = lambda i, j, r: (i, 0)
    vec = lambda i, j, r: (0, 0)
    extras = [(x, _spec((tm, D), row)), (g, _spec((1, D), vec)), (dres, _spec((tm, D), row))]
    outs = [(_sds((T, D), F32), _spec((tm, D), row)), (_sds((1, D), F32), _spec((1, D), vec))]
    return extras, outs


def _ffn_bwd(tag, dx_out, h, a, b, s, wcol, w1_idx, w3_idx, wrow2, w2_idx, x_in, g, big):
    T = dx_out.shape[0]
    tm = ROW_TILE
    tk = ROW_TILE
    dab = _ffn_bwd_mid(tag + "_bwd_mid", dx_out, wrow2, w2_idx, a, b)

    def half_scale(acc, erefs, orefs, ids):
        orefs[0][...] = 0.5 * acc

    big[tag + "_w2"] = _gemm(
        tag + "_dw2",
        [(s, _spec((None, tk, FF_BLK), lambda j, n, r: (j, r, 0)),
          dx_out, _spec((tk, D), lambda j, n, r: (r, 0)), "tn")],
        (N_CHIPS, 1, T // tk),
        [(_sds((N_CHIPS, FF_BLK, D), F32), _spec((None, FF_BLK, D), lambda j, n, r: (j, 0, 0)))],
        (FF_BLK, D), (), half_scale)[0][None]
    for widx, wname in ((0, "_w1"), (1, "_w3")):
        big[tag + wname] = _gemm(
            tag + "_d" + wname[1:],
            [(dab, _spec((None, None, tk, FF_BLK), functools.partial(lambda w, j, n, r: (w, j, r, 0), widx)),
              h, _spec((tk, D), lambda j, n, r: (r, 0)), "tn")],
            (N_CHIPS, 1, T // tk),
            [(_sds((N_CHIPS, FF_BLK, D), F32), _spec((None, FF_BLK, D), lambda j, n, r: (j, 0, 0)))],
            (FF_BLK, D))[0][None]
    tw = min(WIDE_ROW_TILE, T)
    extras, outs = _rms_bwd_io(x_in, g, dx_out, T, tw)
    dx_in, dg = _gemm(
        tag + "_dh",
        [(dab, _spec((None, None, tw, FF_BLK), lambda i, j, r: (0, r, i, 0)),
          wcol, _spec((None, None, FF_BLK, D), lambda i, j, r: (r, w1_idx, 0, 0)), "nn"),
         (dab, _spec((None, None, tw, FF_BLK), lambda i, j, r: (1, r, i, 0)),
          wcol, _spec((None, None, FF_BLK, D), lambda i, j, r: (r, w3_idx, 0, 0)), "nn")],
        (T // tw, 1, N_CHIPS), outs, (tw, D), extras, _rms_bwd_epilogue)
    return dx_in, dg


def _proj_sq(name, a, wsq, idx, kind, out_dtype=F32, extras=(), epilogue=None, outs=None):
    M = a.shape[0]
    tm = min(ROW_TILE, M)
    if outs is None:
        outs = [(_sds((M, D), out_dtype), _spec((tm, D), lambda i, j, r: (i, 0)))]
    return _gemm(
        name,
        [(a, _spec((tm, D), lambda i, j, r: (i, 0)),
          wsq, _spec((N_CHIPS, None, SQ_BLK, D), lambda i, j, r: (0, idx, 0, 0)), kind)],
        (M // tm, 1, 1), outs, (tm, D), extras, epilogue)


def _dw_sq(name, a, b):
    M = a.shape[0]
    tk = min(ROW_TILE, M)
    return _gemm(
        name,
        [(a, _spec((tk, SQ_BLK), lambda j, n, r: (r, j)), b, _spec((tk, D), lambda j, n, r: (r, 0)), "tn")],
        (N_CHIPS, 1, M // tk),
        [(_sds((N_CHIPS, SQ_BLK, D), F32), _spec((None, SQ_BLK, D), lambda j, n, r: (j, 0, 0)))],
        (SQ_BLK, D))[0]


def _retention_constants(T):
    pos = jnp.arange(T, dtype=F32)
    inv_freq = ROPE_BASE ** (-jnp.arange(0, RET_DK, 2, dtype=F32) / RET_DK)
    ang = pos[:, None] * inv_freq[None, :]
    cosf = jnp.concatenate([jnp.cos(ang), jnp.cos(ang)], axis=1)
    sins = jnp.concatenate([-jnp.sin(ang), jnp.sin(ang)], axis=1)
    lg = jnp.log(1.0 - 2.0 ** (-5.0 - jnp.arange(RET_HEADS, dtype=F32)))
    p = jnp.arange(CHUNK, dtype=F32)
    rel = p[:, None] - p[None, :]
    dmat = jnp.where(rel[None] >= 0, jnp.exp(rel[None] * lg[:, None, None]), 0.0)
    kd = jnp.exp((CHUNK - 1.0 - p)[None, :] * lg[:, None])[:, :, None]
    qd = jnp.exp((p + 1.0)[None, :] * lg[:, None])[:, :, None]
    cd = jnp.exp(CHUNK * lg)[:, None, None]
    return cosf, sins, dmat, kd, qd, cd


def _rot(t, cosv, sinv):
    return t * cosv + pltpu.roll(t, RET_DK // 2, 1) * sinv


def _unrot(t, cosv, sinv):
    return t * cosv - pltpu.roll(t, RET_DK // 2, 1) * sinv


def _ret_const_specs(cm):
    return [
        _spec((CHUNK, RET_DK), lambda h, c: (cm(c), 0)),
        _spec((CHUNK, RET_DK), lambda h, c: (cm(c), 0)),
        _spec((None, CHUNK, CHUNK), lambda h, c: (h, 0, 0)),
        _spec((None, CHUNK, 1), lambda h, c: (h, 0, 0)),
        _spec((None, CHUNK, 1), lambda h, c: (h, 0, 0)),
        _spec((None, 1, 1), lambda h, c: (h, 0, 0)),
    ]


def _ret_fwd(u, consts, ret_gn):
    T = u.shape[0]
    nC = T // CHUNK
    kscale = RET_DK ** -0.5

    def body(q_ref, k_ref, v_ref, g_ref, cos_ref, sin_ref, dm_ref, kd_ref, qd_ref, cd_ref, gn_ref,
             qr_ref, kr_ref, ret_ref, yr_ref, st_ref, state):
        @pl.when(pl.program_id(1) == 0)
        def _():
            state[...] = jnp.zeros_like(state)

        cosv, sinv = cos_ref[...], sin_ref[...]
        q = _rot(q_ref[...], cosv, sinv)
        k = _rot(k_ref[...], cosv, sinv) * kscale
        v = v_ref[...]
        qr_ref[...] = q
        kr_ref[...] = k
        prev = state[...]
        st_ref[...] = prev
        s = _dot(q, k, "nt") * dm_ref[...]
        ret = _dot(s, v, "nn") + _dot(q, prev, "nn") * qd_ref[...]
        state[...] = cd_ref[...] * prev + _dot(k * kd_ref[...], v, "tn")
        ret_ref[...] = ret
        mu = jnp.mean(ret, axis=-1, keepdims=True)
        xc = ret - mu
        yn = xc * lax.rsqrt(jnp.mean(xc * xc, axis=-1, keepdims=True) + EPS)
        g = g_ref[...]
        yr_ref[...] = ((g * _sigmoid(g)) * (yn * gn_ref[...])).astype(BF16)

    cm = lambda c: c
    in_specs = [
        _spec((CHUNK, RET_DK), lambda h, c: (c, h)),
        _spec((CHUNK, RET_DK), lambda h, c: (c, RET_HEADS + h)),
        _spec((CHUNK, RET_DV), lambda h, c: (c, 4 + h)),
        _spec((CHUNK, RET_DV), lambda h, c: (c, 8 + h)),
    ] + _ret_const_specs(cm) + [_spec((1, RET_DV), lambda h, c: (0, h))]
    qk_out = _spec((CHUNK, RET_DK), lambda h, c: (c, h))
    v_out = _spec((CHUNK, RET_DV), lambda h, c: (c, h))
    return _pcall(
        body, name="ret_fwd", grid=(RET_HEADS, nC),
        in_specs=in_specs,
        out_specs=[qk_out, qk_out, v_out, v_out,
                   _spec((None, None, RET_DK, RET_DV), lambda h, c: (h, c, 0, 0))],
        out_shape=[_sds((T, 512), F32), _sds((T, 512), F32), _sds((T, D), F32), _sds((T, D), BF16),
                   _sds((RET_HEADS, nC, RET_DK, RET_DV), F32)],
        scratch_shapes=[pltpu.VMEM((RET_DK, RET_DV), F32)],
    )(u, u, u, u, *consts, ret_gn)


def _ret_bwd(dyr, ret, u, qr, kr, states, consts, ret_gn):
    T = u.shape[0]
    nC = T // CHUNK
    kscale = RET_DK ** -0.5

    def body(dyr_ref, ret_ref, g_ref, q_ref, k_ref, v_ref, st_ref,
             cos_ref, sin_ref, dm_ref, kd_ref, qd_ref, cd_ref, gn_ref,
             dq_ref, dk_ref, dv_ref, dg_ref, dgn_ref, gstate):
        first = pl.program_id(1) == 0

        @pl.when(first)
        def _():
            gstate[...] = jnp.zeros_like(gstate)

        ret = ret_ref[...]
        mu = jnp.mean(ret, axis=-1, keepdims=True)
        xc = ret - mu
        rs = lax.rsqrt(jnp.mean(xc * xc, axis=-1, keepdims=True) + EPS)
        yn = xc * rs
        gn = gn_ref[...]
        g = g_ref[...]
        sg = _sigmoid(g)
        dyr_v = dyr_ref[...]
        dretn = dyr_v * (g * sg)
        dg_ref[...] = (dyr_v * (yn * gn) * (sg * (1.0 + g * (1.0 - sg)))).astype(BF16)
        _accumulate(dgn_ref, jnp.sum(dretn * yn, axis=0, keepdims=True), first)
        dyn = dretn * gn
        d_o = rs * (dyn - jnp.mean(dyn, axis=-1, keepdims=True)
                    - yn * jnp.mean(dyn * yn, axis=-1, keepdims=True))

        q, k, v = q_ref[...], k_ref[...], v_ref[...]
        dmat, kd, qd = dm_ref[...], kd_ref[...], qd_ref[...]
        prev = st_ref[...]
        gnext = gstate[...]
        s = _dot(q, k, "nt") * dmat
        ds = _dot(d_o, v, "nt") * dmat
        doq = d_o * qd
        dq = _dot(ds, k, "nn") + _dot(doq, prev, "nt")
        dk = _dot(ds, q, "tn") + _dot(v, gnext, "nt") * kd
        dv = _dot(s, d_o, "tn") + _dot(k * kd, gnext, "nn")
        gstate[...] = cd_ref[...] * gnext + _dot(q, doq, "tn")
        cosv, sinv = cos_ref[...], sin_ref[...]
        dq_ref[...] = _unrot(dq, cosv, sinv).astype(BF16)
        dk_ref[...] = _unrot(dk * kscale, cosv, sinv).astype(BF16)
        dv_ref[...] = dv.astype(BF16)

    cm = lambda c: nC - 1 - c
    vspec = lambda off: _spec((CHUNK, RET_DV), lambda h, c: (cm(c), off + h))
    qspec = _spec((CHUNK, RET_DK), lambda h, c: (cm(c), h))
    in_specs = [vspec(0), vspec(0), vspec(8), qspec, qspec, vspec(4),
                _spec((None, None, RET_DK, RET_DV), lambda h, c: (h, cm(c), 0, 0)),
                ] + _ret_const_specs(cm) + [_spec((1, RET_DV), lambda h, c: (0, h))]
    return _pcall(
        body, name="ret_bwd", grid=(RET_HEADS, nC),
        in_specs=in_specs,
        out_specs=[qspec, qspec, vspec(0), vspec(0), _spec((1, RET_DV), lambda h, c: (0, h))],
        out_shape=[_sds((T, 512), BF16), _sds((T, 512), BF16), _sds((T, D), BF16), _sds((T, D), BF16),
                   _sds((1, D), F32)],
        scratch_shapes=[pltpu.VMEM((RET_DK, RET_DV), F32)],
    )(dyr, ret, u, qr, kr, u, states, *consts, ret_gn)


def _shift_down(x, s):
    rows = lax.broadcasted_iota(jnp.int32, x.shape, 0)
    return jnp.where(rows >= s, pltpu.roll(x, s, 0), 0.0)


def _shift_up(x, s):
    n = x.shape[0]
    rows = lax.broadcasted_iota(jnp.int32, x.shape, 0)
    return jnp.where(rows < n - s, pltpu.roll(x, n - s, 0), 0.0)


def _lru_specs(T):
    col = lambda off: _spec((T, LRU_BLOCK), lambda g: (0, off + g))
    vec = _spec((1, LRU_BLOCK), lambda g: (0, g))
    wblk = _spec((None, LRU_BLOCK, LRU_BLOCK), lambda g: (g, 0, 0))
    cw = _spec((CONV_TAPS, LRU_BLOCK), lambda g: (0, g))
    return col, vec, wblk, cw


def _lru_gates_fwd(u, conv_w, conv_b, w_r, b_r, w_i, b_i, lam):
    T = u.shape[0]
    col, vec, wblk, cw = _lru_specs(T)

    def body(x_ref, cw_ref, cb_ref, wr_ref, br_ref, wi_ref, bi_ref, lam_ref,
             xc_ref, r_ref, i_ref, a_ref, bx_ref):
        x = x_ref[...]
        w = cw_ref[...]
        xc = (_shift_down(x, 3) * w[0:1] + _shift_down(x, 2) * w[1:2] + _shift_down(x, 1) * w[2:3]
              + x * w[3:4] + cb_ref[...])
        r = _sigmoid(_dot(xc, wr_ref[...], "nn") + br_ref[...])
        i = _sigmoid(_dot(xc, wi_ref[...], "nn") + bi_ref[...])
        la = (-LRU_C) * r * _softplus(-lam_ref[...])
        xc_ref[...] = xc
        r_ref[...] = r
        i_ref[...] = i
        a_ref[...] = jnp.exp(la)
        bx_ref[...] = jnp.sqrt(-_expm1(2.0 * la)) * (i * xc)

    out = col(0)
    return _pcall(
        body, name="lru_gates_fwd", grid=(LRU_BLOCKS,),
        in_specs=[col(24), cw, vec, wblk, vec, wblk, vec, vec],
        out_specs=[out] * 5,
        out_shape=[_sds((T, D), F32)] * 5,
    )(u, conv_w, conv_b, w_r, b_r, w_i, b_i, lam)


def _lru_scan(name, a3, b3, reverse):
    T = a3.shape[0]
    nt = T // SCAN_TILE
    unroll = 8

    def body(a_ref, b_ref, o_ref, carry):
        @pl.when(pl.program_id(0) == 0)
        def _():
            carry[...] = jnp.zeros_like(carry)

        if not reverse:
            def step(t, h):
                h = a_ref[t] * h + b_ref[t]
                o_ref[t] = h
                return h
        else:
            def step(k, c):
                t = SCAN_TILE - 1 - k
                l = b_ref[t] + c
                o_ref[t] = l
                return a_ref[t] * l
        carry[...] = lax.fori_loop(0, SCAN_TILE, step, carry[...], unroll=unroll)

    idx = (lambda i: (nt - 1 - i, 0, 0)) if reverse else (lambda i: (i, 0, 0))
    blk = _spec((SCAN_TILE, LRU_BLOCKS, LRU_BLOCK), idx)
    return _pcall(
        body, name=name, grid=(nt,),
        in_specs=[blk, blk], out_specs=blk,
        out_shape=_sds((T, LRU_BLOCKS, LRU_BLOCK), F32),
        scratch_shapes=[pltpu.VMEM((LRU_BLOCKS, LRU_BLOCK), F32)],
    )(a3, b3)


def _lru_gates_bwd(lmb, hl, a, r, i, xc, u, conv_w, w_r, w_i, lam):
    T = u.shape[0]
    col, vec, wblk, cw = _lru_specs(T)

    def body(l_ref, h_ref, a_ref, r_ref, i_ref, xc_ref, x_ref, cw_ref, wr_ref, wi_ref, lam_ref,
             dx_ref, dwr_ref, dwi_ref, dvec_ref, dcw_ref):
        l = l_ref[...]
        av, rv, iv, xc = a_ref[...], r_ref[...], i_ref[...], xc_ref[...]
        lam_v = lam_ref[...]
        sp = _softplus(-lam_v)
        la = (-LRU_C) * rv * sp
        mult = jnp.sqrt(-_expm1(2.0 * la))
        da = l * _shift_down(h_ref[...], 1)
        dmult = l * (iv * xc)
        di = l * mult * xc
        dxc = l * mult * iv
        dla = da * av - dmult * (av * av) / mult
        dzr = (dla * ((-LRU_C) * sp)) * rv * (1.0 - rv)
        dzi = di * iv * (1.0 - iv)
        dsp = jnp.sum(dla * ((-LRU_C) * rv), axis=0, keepdims=True)
        dlam = dsp * (-_sigmoid(-lam_v))
        dwr_ref[...] = _dot(xc, dzr, "tn")
        dwi_ref[...] = _dot(xc, dzi, "tn")
        dxc = dxc + _dot(dzr, wr_ref[...], "nt") + _dot(dzi, wi_ref[...], "nt")
        x = x_ref[...]
        w = cw_ref[...]
        dx = (dxc * w[3:4] + _shift_up(dxc, 1) * w[2:3] + _shift_up(dxc, 2) * w[1:2]
              + _shift_up(dxc, 3) * w[0:1])
        dx_ref[...] = dx.astype(BF16)
        dvec_ref[...] = jnp.concatenate(
            [jnp.sum(dzr, axis=0, keepdims=True), jnp.sum(dzi, axis=0, keepdims=True), dlam,
             jnp.sum(dxc, axis=0, keepdims=True)], axis=0)
        dcw_ref[...] = jnp.concatenate(
            [jnp.sum(dxc * _shift_down(x, 3 - tap), axis=0, keepdims=True) if tap < 3
             else jnp.sum(dxc * x, axis=0, keepdims=True) for tap in range(CONV_TAPS)], axis=0)

    c0 = col(0)
    return _pcall(
        body, name="lru_gates_bwd", grid=(LRU_BLOCKS,),
        in_specs=[c0, c0, c0, c0, c0, c0, col(24), cw, wblk, wblk, vec],
        out_specs=[c0, wblk, wblk, cw, cw],
        out_shape=[_sds((T, D), BF16), _sds((LRU_BLOCKS, LRU_BLOCK, LRU_BLOCK), F32),
                   _sds((LRU_BLOCKS, LRU_BLOCK, LRU_BLOCK), F32), _sds((4, D), F32), _sds((CONV_TAPS, D), F32)],
    )(lmb, hl, a, r, i, xc, u, conv_w, w_r, w_i, lam)


def _xattn_probs(q, k):
    sc = _dot(q, k, "nt") * (X_HD ** -0.5)
    e = jnp.exp(sc - jnp.max(sc, axis=-1, keepdims=True))
    return e / jnp.sum(e, axis=-1, keepdims=True)


def _xattn_fwd(xq, xk, xv):
    T = xq.shape[0]
    tq = ROW_TILE
    M = xk.shape[0]

    def body(q_ref, k_ref, v_ref, o_ref):
        p = _xattn_probs(q_ref[...], k_ref[...])
        o_ref[...] = _dot(p, v_ref[...], "nn").astype(BF16)

    qs = _spec((tq, X_HD), lambda h, i: (i, h))
    kv = _spec((M, X_HD), lambda h, i: (0, h))
    return _pcall(
        body, name="xattn_fwd", grid=(X_HEADS, T // tq),
        in_specs=[qs, kv, kv], out_specs=qs, out_shape=_sds((T, D), BF16),
    )(xq, xk, xv)


def _xattn_bwd(xq, xk, xv, dxo):
    T = xq.shape[0]
    tq = ROW_TILE
    M = xk.shape[0]

    def body(q_ref, k_ref, v_ref, do_ref, dq_ref, dk_ref, dv_ref):
        first = pl.program_id(1) == 0
        q, k, v, do = q_ref[...], k_ref[...], v_ref[...], do_ref[...]
        p = _xattn_probs(q, k)
        dp = _dot(do, v, "nt")
        ds = p * (dp - jnp.sum(dp * p, axis=-1, keepdims=True)) * (X_HD ** -0.5)
        dq_ref[...] = _dot(ds, k, "nn").astype(BF16)
        _accumulate(dk_ref, _dot(ds, q, "tn"), first)
        _accumulate(dv_ref, _dot(p, do, "tn"), first)

    qs = _spec((tq, X_HD), lambda h, i: (i, h))
    kv = _spec((M, X_HD), lambda h, i: (0, h))
    return _pcall(
        body, name="xattn_bwd", grid=(X_HEADS, T // tq),
        in_specs=[qs, kv, kv, qs], out_specs=[qs, kv, kv],
        out_shape=[_sds((T, D), BF16), _sds((M, D), F32), _sds((M, D), F32)],
    )(xq, xk, xv, dxo)


def _final_loss(x, g, tgt):
    T = x.shape[0]
    tm = ROW_TILE

    def fn(irefs, orefs, ids):
        xv, gv = irefs[0][...], irefs[1][...]
        err = _rms_fwd(xv, gv) - irefs[2][...]
        lp = 0.5 * jnp.sum(jnp.mean(err * err, axis=-1, keepdims=True), axis=0, keepdims=True)
        first = ids[0] == 0
        _accumulate(orefs[0], jnp.broadcast_to(lp, (1, 128)), first)
        dx, dgp = _rms_bwd(xv, gv, err * (1.0 / D))
        orefs[1][...] = dx
        _accumulate(orefs[2], dgp, first)

    row = _spec((tm, D), lambda i: (i, 0))
    vec = _spec((1, D), lambda i: (0, 0))
    return _rowwise(
        "final_loss", fn, [(x, row), (g, vec), (tgt, row)],
        [(_sds((1, 128), F32), _spec((1, 128), lambda i: (0, 0))), (_sds((T, D), F32), row),
         (_sds((1, D), F32), vec)],
        (T // tm,))


def _adamw(name, w, g, m, v):
    R, C = w.shape
    tr = R
    for cand in (512, 352, 256):
        if R % cand == 0:
            tr = cand
            break

    def fn(irefs, orefs, ids):
        delta, mn, vn = _adamw_update(*(r[...] for r in irefs))
        orefs[0][...] = delta
        orefs[1][...] = mn
        orefs[2][...] = vn

    blk = _spec((tr, C), lambda i: (i, 0))
    return _rowwise(name, fn, [(w, blk), (g, blk), (m, blk), (v, blk)],
                    [(_sds((R, C), F32), blk)] * 3, (R // tr,))


def _adamw_update(wv, gv, mv, vv):
    c1 = 1.0 - ADAM_B1 ** ADAM_STEP
    c2 = 1.0 - ADAM_B2 ** ADAM_STEP
    mn = ADAM_B1 * mv + (1.0 - ADAM_B1) * gv
    vn = ADAM_B2 * vv + (1.0 - ADAM_B2) * (gv * gv)
    delta = -ADAM_LR * ((mn / c1) / (jnp.sqrt(vn / c2) + ADAM_EPS) + ADAM_WD * wv)
    return delta, mn, vn


def _adamw_halves(name, w, mine, theirs, widx, m, v, core):
    R, C = w.shape
    H = R // 2
    tr = H
    while tr * C * 4 > (1 << 20) and tr % 16 == 0:
        tr //= 2
    nb = H // tr

    def body(core_ref, w_ref, mine_ref, theirs_ref, m_ref, v_ref, g_out, d_out, m_out, v_out):
        gv = jnp.where(pl.program_id(0) == core_ref[0], mine_ref[...], theirs_ref[...])
        delta, mn, vn = _adamw_update(w_ref[...], gv, m_ref[...], v_ref[...])
        g_out[...] = gv
        d_out[...] = delta
        m_out[...] = mn
        v_out[...] = vn

    full = pl.BlockSpec((tr, C), lambda h, i, core_ref: (h * nb + i, 0))
    half = pl.BlockSpec((None, tr, C), lambda h, i, core_ref: (widx, i, 0))
    return _pcall(
        body, name=name, grid=(2, nb), num_prefetch=1,
        in_specs=[full, half, half, full, full], out_specs=[full] * 4,
        out_shape=[_sds((R, C), F32)] * 4,
    )(core, w, mine, theirs, m, v)


def _rmsnorm(name, x, g):
    M = x.shape[0]
    tm = min(ROW_TILE, M)

    def fn(irefs, orefs, ids):
        orefs[0][...] = _rms_fwd(irefs[0][...], irefs[1][...]).astype(BF16)

    row = _spec((tm, D), lambda i: (i, 0))
    return _rowwise(name, fn, [(x, row), (g, _spec((1, D), lambda i: (0, 0)))],
                    [(_sds((M, D), BF16), row)], (M // tm,))[0]


WEIGHT_AT = {
    "ffn1_w1": ("col1", 0), "ffn1_w3": ("col1", 1), "ffn1_w2": ("row2a", 0),
    "w_ret_o": ("sqA", 0), "w_lru_o": ("sqA", 1), "w_out": ("sqA", 2),
    "w_xq": ("sqB", 0), "w_xk": ("sqB", 1), "w_xv": ("sqC", 0), "w_xo": ("sqC", 1),
    "ffn2_w1": ("col2", 0), "ffn2_w3": ("col2", 1), "ffn2_w2": ("row2b", 0),
}


def _local_step(x, mem, tgt, gw, sm, big):
    T = x.shape[0]
    tm = ROW_TILE

    def wt(name):
        key, idx = WEIGHT_AT[name]
        return gw[key], idx

    row3 = lambda i, j, r: (i, 0)
    vec3 = lambda i, j, r: (0, 0)
    rowD = _spec((tm, D), row3)
    vecD = _spec((1, D), vec3)

    def residual_norm(acc, erefs, orefs, ids):
        xo = erefs[0][...] + acc
        orefs[0][...] = xo
        orefs[1][...] = _rms_fwd(xo, erefs[1][...]).astype(BF16)

    def res_norm_io(x_res, g):
        return ([(x_res, rowD), (g, vecD)],
                [(_sds((T, D), F32), rowD), (_sds((T, D), BF16), rowD)])

    h1 = _rmsnorm("ffn1_norm", x, sm["ffn1_norm"])
    a1, b1, s1 = _ffn_up("ffn1_up", h1, *wt("ffn1_w1"), wt("ffn1_w3")[1])
    x1, h2 = _ffn_down("ffn1_down", s1, *wt("ffn1_w2"), x, sm["mix_norm"])

    tw = min(WIDE_ROW_TILE, T)
    wideD = _spec((tw, D), row3)
    u = _gemm(
        "mix_in",
        [(h2, wideD, gw["win"], _spec((None, None, D, IN_BLK), lambda i, j, r: (j, 0, 0, 0)), "nn")],
        (T // tw, N_CHIPS, 1),
        [(_sds((T, 5120), F32), _spec((tw, IN_BLK), lambda i, j, r: (i, j)))], (tw, IN_BLK))[0]

    def gate_epilogue(acc, erefs, orefs, ids):
        orefs[0][...] = _sigmoid(acc + erefs[0][...])

    gates = _gemm(
        "mix_gates",
        [(h2, wideD, gw["wbg"], _spec((None, None, D, BG_BLK), lambda i, j, r: (j, 0, 0, 0)), "nn")],
        (T // tw, N_CHIPS, 1),
        [(_sds((T, 2 * D), F32), _spec((tw, BG_BLK), lambda i, j, r: (i, j)))], (tw, BG_BLK),
        [(sm["b_branch_gate"], _spec((1, BG_BLK), lambda i, j, r: (0, j)))], gate_epilogue)[0]

    consts = _retention_constants(T)
    qr, kr, ret, yr, states = _ret_fwd(u, consts, sm["ret_gn"])

    conv_w = gw["conv"][:, 0].transpose(1, 0, 2).reshape(CONV_TAPS, D)
    xc, rg, ig, av, bx = _lru_gates_fwd(u, conv_w, sm["conv_b"], sm["w_rgate"], sm["b_rgate"],
                                        sm["w_igate"], sm["b_igate"], sm["lru_lambda"])
    a3 = av.reshape(T, LRU_BLOCKS, LRU_BLOCK)
    hl = _lru_scan("lru_scan_fwd", a3, bx.reshape(T, LRU_BLOCKS, LRU_BLOCK), False).reshape(T, D)

    row1 = _spec((tm, D), lambda i: (i, 0))
    glru1 = _spec((tm, D), lambda i: (i, 4))

    def lru_out(irefs, orefs, ids):
        gl, _ = _gelu_and_grad(irefs[1][...])
        orefs[0][...] = (irefs[0][...] * gl).astype(BF16)

    yl = _rowwise("lru_out", lru_out, [(hl, row1), (u, glru1)], [(_sds((T, D), BF16), row1)], (T // tm,))[0]

    y_ret = _proj_sq("y_ret", yr, *wt("w_ret_o"), "nn")[0]

    def merge_epilogue(acc, erefs, orefs, ids):
        orefs[0][...] = acc
        orefs[1][...] = (erefs[0][...] * erefs[2][...] + erefs[1][...] * acc).astype(BF16)

    y_lru, merged = _proj_sq(
        "y_lru", yl, *wt("w_lru_o"), "nn",
        extras=[(gates, _spec((tm, D), lambda i, j, r: (i, 0))), (gates, _spec((tm, D), lambda i, j, r: (i, 1))),
                (y_ret, rowD)],
        epilogue=merge_epilogue,
        outs=[(_sds((T, D), F32), rowD), (_sds((T, D), BF16), rowD)])

    ex, ou = res_norm_io(x1, sm["xattn_norm"])
    x2, hq = _proj_sq("mix_out", merged, *wt("w_out"), "nn", extras=ex, epilogue=residual_norm, outs=ou)

    m = _rmsnorm("mem_norm", mem, sm["mem_norm"])
    xq = _proj_sq("xq", hq, *wt("w_xq"), "nn", BF16)[0]
    xk = _proj_sq("xk", m, *wt("w_xk"), "nn", BF16)[0]
    xv = _proj_sq("xv", m, *wt("w_xv"), "nn", BF16)[0]
    xo = _xattn_fwd(xq, xk, xv)
    ex, ou = res_norm_io(x2, sm["ffn2_norm"])
    x3, h3 = _proj_sq("xattn_out", xo, *wt("w_xo"), "nn", extras=ex, epilogue=residual_norm, outs=ou)

    a2, b2, s2 = _ffn_up("ffn2_up", h3, *wt("ffn2_w1"), wt("ffn2_w3")[1])
    x4 = _ffn_down("ffn2_down", s2, *wt("ffn2_w2"), x3)[0]
    loss, dx4, dg_final = _final_loss(x4, sm["final_norm"], tgt)

    dx3, dg_ffn2 = _ffn_bwd("ffn2", dx4, h3, a2, b2, s2, *wt("ffn2_w1"), wt("ffn2_w3")[1],
                            *wt("ffn2_w2"), x3, sm["ffn2_norm"], big)

    dxo = _proj_sq("d_xo", dx3, *wt("w_xo"), "nt", BF16)[0]
    big["w_xo"] = _dw_sq("dw_xo", xo, dx3)[None]
    dxq, dxk, dxv = _xattn_bwd(xq, xk, xv, dxo)
    big["w_xq"] = _dw_sq("dw_xq", hq, dxq)[None]
    ex, ou = _rms_bwd_io(x2, sm["xattn_norm"], dx3, T, tm)
    dx2, dg_xattn = _proj_sq("d_hq", dxq, *wt("w_xq"), "nt", extras=ex, epilogue=_rms_bwd_epilogue, outs=ou)
    big["w_xk"] = _dw_sq("dw_xk", m, dxk)[None]
    big["w_xv"] = _dw_sq("dw_xv", m, dxv)[None]

    M = mem.shape[0]

    def mem_norm_epilogue(acc, erefs, orefs, ids):
        _, dgp = _rms_bwd(erefs[0][...], erefs[1][...], acc)
        orefs[0][...] = dgp

    wsq_spec = lambda idx: _spec((N_CHIPS, None, SQ_BLK, D), lambda i, j, r: (0, idx, 0, 0))
    memD = _spec((M, D), row3)
    dg_mem = _gemm(
        "d_mem_norm",
        [(dxk, memD, wt("w_xk")[0], wsq_spec(wt("w_xk")[1]), "nt"),
         (dxv, memD, wt("w_xv")[0], wsq_spec(wt("w_xv")[1]), "nt")],
        (1, 1, 1), [(_sds((1, D), F32), vecD)], (M, D),
        [(mem, memD), (sm["mem_norm"], vecD)], mem_norm_epilogue)[0]

    def merged_bwd_epilogue(acc, erefs, orefs, ids):
        gr, gl, yrv, ylv = (e[...] for e in erefs)
        orefs[0][...] = (acc * gr).astype(BF16)
        orefs[1][...] = (acc * gl).astype(BF16)
        dgr = acc * yrv * gr * (1.0 - gr)
        dgl = acc * ylv * gl * (1.0 - gl)
        orefs[2][:, :D] = dgr.astype(BF16)
        orefs[2][:, D:] = dgl.astype(BF16)
        dbb = jnp.concatenate([jnp.sum(dgr, axis=0, keepdims=True), jnp.sum(dgl, axis=0, keepdims=True)], axis=1)
        _accumulate(orefs[3], dbb, ids[0] == 0)

    dy_ret, dy_lru, dgpre, db_bg = _proj_sq(
        "d_merged", dx2, *wt("w_out"), "nt",
        extras=[(gates, _spec((tm, D), lambda i, j, r: (i, 0))), (gates, _spec((tm, D), lambda i, j, r: (i, 1))),
                (y_ret, rowD), (y_lru, rowD)],
        epilogue=merged_bwd_epilogue,
        outs=[(_sds((T, D), BF16), rowD), (_sds((T, D), BF16), rowD),
              (_sds((T, 2 * D), BF16), _spec((tm, 2 * D), row3)),
              (_sds((1, 2 * D), F32), _spec((1, 2 * D), vec3))])
    big["w_out"] = _dw_sq("dw_out", merged, dx2)[None]
    dyr = _proj_sq("d_yr", dy_ret, *wt("w_ret_o"), "nt")[0]
    big["w_ret_o"] = _dw_sq("dw_ret_o", yr, dy_ret)[None]
    dyl = _proj_sq("d_yl", dy_lru, *wt("w_lru_o"), "nt")[0]
    big["w_lru_o"] = _dw_sq("dw_lru_o", yl, dy_lru)[None]

    dq, dk, dv, dgr, dg_retgn = _ret_bwd(dyr, ret, u, qr, kr, states, consts, sm["ret_gn"])

    def lru_out_bwd(irefs, orefs, ids):
        gl, dgl = _gelu_and_grad(irefs[2][...])
        dyl_v = irefs[0][...]
        orefs[0][...] = dyl_v * gl
        orefs[1][...] = (dyl_v * irefs[1][...] * dgl).astype(BF16)

    dhl, dglru = _rowwise("lru_out_bwd", lru_out_bwd, [(dyl, row1), (hl, row1), (u, glru1)],
                          [(_sds((T, D), F32), row1), (_sds((T, D), BF16), row1)], (T // tm,))
    lmb = _lru_scan("lru_scan_bwd", a3, dhl.reshape(T, LRU_BLOCKS, LRU_BLOCK), True).reshape(T, D)
    dxl, dw_r, dw_i, dvec, dcw = _lru_gates_bwd(lmb, hl, av, rg, ig, xc, u, conv_w,
                                                sm["w_rgate"], sm["w_igate"], sm["lru_lambda"])

    du = jnp.concatenate([dq, dk, dv, dgr, dxl, dglru], axis=1)
    tk = ROW_TILE
    big["w_in"] = _gemm(
        "dw_in",
        [(h2, _spec((tk, D), lambda j, n, r: (r, 0)), du, _spec((tk, IN_BLK), lambda j, n, r: (r, j)), "tn")],
        (N_CHIPS, 1, T // tk),
        [(_sds((N_CHIPS, D, IN_BLK), F32), _spec((None, D, IN_BLK), lambda j, n, r: (j, 0, 0)))],
        (D, IN_BLK))[0][None]
    big["w_branch_gate"] = _gemm(
        "dw_bg",
        [(h2, _spec((tk, D), lambda j, n, r: (r, 0)), dgpre, _spec((tk, BG_BLK), lambda j, n, r: (r, j)), "tn")],
        (N_CHIPS, 1, T // tk),
        [(_sds((N_CHIPS, D, BG_BLK), F32), _spec((None, D, BG_BLK), lambda j, n, r: (j, 0, 0)))],
        (D, BG_BLK))[0][None]
    ex, ou = _rms_bwd_io(x1, sm["mix_norm"], dx2, T, tw)
    dx1, dg_mix = _gemm(
        "d_h2",
        [(du, _spec((tw, IN_BLK), lambda i, j, r: (i, r)),
          gw["win"], _spec((None, None, D, IN_BLK), lambda i, j, r: (r, 0, 0, 0)), "nt"),
         (dgpre, _spec((tw, BG_BLK), lambda i, j, r: (i, r)),
          gw["wbg"], _spec((None, None, D, BG_BLK), lambda i, j, r: (r, 0, 0, 0)), "nt")],
        (T // tw, 1, N_CHIPS), ou, (tw, D), ex, _rms_bwd_epilogue)

    grad_x, dg_ffn1 = _ffn_bwd("ffn1", dx1, h1, a1, b1, s1, *wt("ffn1_w1"), wt("ffn1_w3")[1],
                               *wt("ffn1_w2"), x, sm["ffn1_norm"], big)

    small = {
        "ffn1_norm": dg_ffn1, "mix_norm": dg_mix, "ret_gn": dg_retgn, "conv_b": dvec[3:4],
        "b_rgate": dvec[0:1], "b_igate": dvec[1:2], "lru_lambda": dvec[2:3], "xattn_norm": dg_xattn,
        "mem_norm": dg_mem, "ffn2_norm": dg_ffn2, "final_norm": dg_final, "b_branch_gate": db_bg,
        "conv_w": dcw, "w_rgate": dw_r, "w_igate": dw_i,
    }
    return loss, grad_x, small


ANY_SPEC = pl.BlockSpec(memory_space=pl.ANY)
VMEM_SPEC = pl.BlockSpec(memory_space=pltpu.VMEM)
N_PEER_CHIPS = N_CHIPS - 1


def _mesh_position():
    x, y, c = lax.axis_index("x"), lax.axis_index("y"), lax.axis_index("c")
    chips = [(1 - x, y), (x, 1 - y), (1 - x, 1 - y)]
    return x, y, c, chips


def _chip_index(x, y):
    return 2 * x + y


def _rows_half(ref, axis, h):
    n = ref.shape[axis] // 2
    idx = [slice(None)] * len(ref.shape)
    idx[axis] = pl.ds(pl.multiple_of(h * n, 16), n)
    return ref.at[tuple(idx)]


def _remote(src, dst, send_sem, recv_sem, device):
    return pltpu.make_async_remote_copy(src_ref=src, dst_ref=dst, send_sem=send_sem, recv_sem=recv_sem,
                                        device_id=device, device_id_type=MESH)


def _gather_chips_task(shards, split, landed):
    keys = list(shards)
    n = len(keys)

    def operands():
        chip_me = _chip_index(lax.axis_index("x"), lax.axis_index("y"))
        bases = [lax.dynamic_update_slice(lax.empty((N_CHIPS,) + shards[k].shape, shards[k].dtype), shards[k][None],
                                          (chip_me,) + (0,) * shards[k].ndim) for k in keys]
        return [shards[k] for k in keys] + bases

    def make(ins, outs, send_sem, recv_sem):
        x, y, c, chips = _mesh_position()
        s_me = _chip_index(x, y)
        starts, arrivals = [], []
        for g in range(n):
            mine = _rows_half(ins[g], 1, c) if split else ins[g]
            for k, chip in enumerate(chips):
                def landing(s):
                    o = outs[g].at[s]
                    return _rows_half(o, 1, c) if split else o
                starts.append(_remote(mine, landing(s_me), send_sem(3 * g + k), recv_sem(3 * g + k), (*chip, c)))
                got = landing(_chip_index(*chip))
                arrivals.append(functools.partial(_remote, got, got, send_sem(3 * g + k), recv_sem(3 * g + k),
                                                  (*chip, c)))
        return starts, arrivals

    def finish(res):
        landed.update(zip(keys, res))

    return _Task(operands, lambda: [_sds((N_CHIPS,) + shards[k].shape, shards[k].dtype) for k in keys],
                 {n + g: g for g in range(n)}, 3 * n, make, finish)


def _gather_sibling_task(keys, landed, ready):
    n = len(keys)

    def make(ins, outs, send_sem, recv_sem):
        x, y, c, chips = _mesh_position()
        starts, arrivals = [], []
        for g in range(n):
            for k, chip in enumerate(chips):
                o = outs[g].at[_chip_index(*chip)]
                got, other = _rows_half(o, 1, c), _rows_half(o, 1, 1 - c)
                starts.append(_remote(got, got, send_sem(3 * g + k), recv_sem(3 * g + k), (x, y, 1 - c)))
                arrivals.append(functools.partial(_remote, other, other, send_sem(3 * g + k), recv_sem(3 * g + k),
                                                  (x, y, 1 - c)))
        return starts, arrivals

    def finish(res):
        ready.update(zip(keys, res))

    return _Task(lambda: [landed[k] for k in keys], lambda: [_sds(landed[k].shape, landed[k].dtype) for k in keys],
                 {g: g for g in range(n)}, 3 * n, make, finish)


def _pair_swap_task(names, big, got):
    n = len(names)

    def make(ins, outs, send_sem, recv_sem):
        x, y, c, _ = _mesh_position()
        copies = [_remote(_rows_half(ins[a], 2, 1 - c), outs[a], send_sem(a), recv_sem(a), (x, y, 1 - c))
                  for a in range(n)]
        return copies, [functools.partial(lambda cp: cp, cp) for cp in copies]

    def shapes():
        return [_sds(big[k].shape[:2] + (big[k].shape[2] // 2, big[k].shape[3]), F32) for k in names]

    return _Task(lambda: [big[k] for k in names], shapes, {}, n, make, lambda res: got.update(zip(names, res)))


def _rs_pair_sum(name, full, got, core):
    nw, ns, R, C = full.shape
    half = R // 2

    def body(core_ref, a_ref, b_ref, o_ref):
        o_ref[...] = (a_ref[...] + b_ref[...]).astype(BF16)

    blk = lambda fn: pl.BlockSpec((None, None, half, C), fn)
    return _pcall(
        body, name=name, grid=(nw, ns), num_prefetch=1,
        in_specs=[blk(lambda w, s, core_ref: (w, s, core_ref[0], 0)), blk(lambda w, s, core_ref: (w, s, 0, 0))],
        out_specs=blk(lambda w, s, core_ref: (w, s, 0, 0)),
        out_shape=_sds((nw, ns, half, C), BF16),
    )(core, full, got)


def _chip_exchange_task(names, pair_sums, by_source, part=0, nparts=1):
    n = len(names)

    def rows(ref):
        h = ref.shape[1] // nparts
        return ref.at[:, pl.ds(part * h, h), :]

    def make(ins, outs, send_sem, recv_sem):
        x, y, c, chips = _mesh_position()
        s_me = _chip_index(x, y)
        starts, arrivals = [], []
        for a in range(n):
            for k, chip in enumerate(chips):
                s_k = _chip_index(*chip)
                starts.append(_remote(rows(ins[a].at[:, s_k]), rows(outs[a].at[:, s_me]), send_sem(3 * a + k),
                                      recv_sem(3 * a + k), (*chip, c)))
                got = rows(outs[a].at[:, s_k])
                arrivals.append(functools.partial(_remote, got, got, send_sem(3 * a + k), recv_sem(3 * a + k),
                                                  (*chip, c)))
        return starts, arrivals

    def operands():
        return [pair_sums[k] for k in names] + ([by_source[k] for k in names] if part else [])

    return _Task(operands, lambda: [_sds(pair_sums[k].shape, pair_sums[k].dtype) for k in names],
                 {n + a: a for a in range(n)} if part else {}, 3 * n, make,
                 lambda res: by_source.update(zip(names, res)))


def _rs_chip_sum(name, own, parts, chip):
    nw, ns, H, C = parts.shape

    def body(chip_ref, own_ref, *rest):
        prefs, o_ref = rest[:ns], rest[ns]
        me = chip_ref[0]
        own_v = own_ref[...].astype(F32)
        tot = None
        for s in range(ns):
            term = jnp.where(me == s, own_v, prefs[s][...].astype(F32))
            tot = term if tot is None else tot + term
        o_ref[...] = tot

    blk = lambda fn: pl.BlockSpec((None, None, H, C), fn)

    def part_spec(s):
        return blk(lambda w, chip_ref: (w, jnp.where(chip_ref[0] == s, (s + 1) % ns, s), 0, 0))

    return _pcall(
        body, name=name, grid=(nw,), num_prefetch=1,
        in_specs=[blk(lambda w, chip_ref: (w, chip_ref[0], 0, 0))] + [part_spec(s) for s in range(ns)],
        out_specs=pl.BlockSpec((None, H, C), lambda w, chip_ref: (w, 0, 0)),
        out_shape=_sds((nw, H, C), F32),
    )(chip, own, *([parts] * ns))


def _pair_gather_task(names, halves, sibling_halves):
    n = len(names)

    def make(ins, outs, send_sem, recv_sem):
        x, y, c, _ = _mesh_position()
        copies = [_remote(ins[a], outs[a], send_sem(a), recv_sem(a), (x, y, 1 - c)) for a in range(n)]
        return copies, [functools.partial(lambda cp: cp, cp) for cp in copies]

    return _Task(lambda: [halves[k] for k in names], lambda: [_sds(halves[k].shape, F32) for k in names],
                 {}, n, make, lambda res: sibling_halves.update(zip(names, res)))


def _small_allreduce(v):
    R, C = v.shape

    def body(v_ref, o_ref, sib_buf, pair_buf, chip_buf, send_sems, recv_sems):
        x, y, c, chips = _mesh_position()
        s_me = _chip_index(x, y)
        swap = _remote(v_ref, sib_buf, send_sems.at[0], recv_sems.at[0], (x, y, 1 - c))
        swap.start()
        swap.wait()
        pair_buf[...] = v_ref[...] + sib_buf[...]
        chip_buf[s_me] = pair_buf[...]
        sends = [_remote(pair_buf, chip_buf.at[s_me], send_sems.at[1 + k], recv_sems.at[1 + k], (*chip, c))
                 for k, chip in enumerate(chips)]
        for cp in sends:
            cp.start()
        for k, chip in enumerate(chips):
            got = chip_buf.at[_chip_index(*chip)]
            _remote(got, got, send_sems.at[1 + k], recv_sems.at[1 + k], (*chip, c)).wait_recv()
        for cp in sends:
            cp.wait_send()
        o_ref[...] = ((chip_buf[0] + chip_buf[1]) + chip_buf[2]) + chip_buf[3]

    return pl.pallas_call(
        body, name="small_allreduce",
        in_specs=[VMEM_SPEC], out_specs=VMEM_SPEC, out_shape=_sds((R, C), F32),
        scratch_shapes=[pltpu.VMEM((R, C), F32), pltpu.VMEM((R, C), F32), pltpu.VMEM((N_CHIPS, R, C), F32),
                        pltpu.SemaphoreType.DMA((1 + N_PEER_CHIPS,)), pltpu.SemaphoreType.DMA((1 + N_PEER_CHIPS,))],
        compiler_params=pltpu.CompilerParams(vmem_limit_bytes=VMEM_LIMIT_BYTES),
    )(v)


TRANSPOSED_WEIGHTS = ("ffn1_w1", "ffn1_w3", "ffn2_w1", "ffn2_w3")
SMALL_LAYOUT = [("ffn1_norm", 1), ("mix_norm", 1), ("ret_gn", 1), ("conv_b", 1), ("b_rgate", 1), ("b_igate", 1),
                ("lru_lambda", 1), ("xattn_norm", 1), ("mem_norm", 1), ("ffn2_norm", 1), ("final_norm", 1),
                ("b_branch_gate", 2), ("conv_w", CONV_TAPS), ("w_rgate", LRU_BLOCK), ("w_igate", LRU_BLOCK)]
SMALL_ROWS = 280
WEIGHT_ORDER = ["ffn1_norm", "ffn1_w1", "ffn1_w3", "ffn1_w2", "mix_norm", "w_in", "ret_gn", "w_ret_o", "conv_w",
                "conv_b", "w_rgate", "b_rgate", "w_igate", "b_igate", "lru_lambda", "w_lru_o", "w_branch_gate",
                "b_branch_gate", "w_out", "xattn_norm", "mem_norm", "w_xq", "w_xk", "w_xv", "w_xo", "ffn2_norm",
                "ffn2_w1", "ffn2_w3", "ffn2_w2", "final_norm"]


def _pack_small(parts):
    rows = [parts[name].reshape(n, D) for name, n in SMALL_LAYOUT]
    used = sum(n for _, n in SMALL_LAYOUT)
    rows.append(jnp.zeros((SMALL_ROWS - used, D), F32))
    return jnp.concatenate(rows, axis=0)


def _unpack_small(packed, shapes):
    out, r = {}, 0
    for name, n in SMALL_LAYOUT:
        out[name] = packed[r:r + n].reshape(shapes[name])
        r += n
    return out


def kernel(x, mem, ffn1_norm, ffn1_w1, ffn1_w3, ffn1_w2, mix_norm, w_in, ret_gn, w_ret_o, conv_w, conv_b, w_rgate, b_rgate, w_igate, b_igate, lru_lambda, w_lru_o, w_branch_gate, b_branch_gate, w_out, xattn_norm, mem_norm, w_xq, w_xk, w_xv, w_xo, ffn2_norm, ffn2_w1, ffn2_w3, ffn2_w2, final_norm, loss_target, m_ffn1_norm, m_ffn1_w1, m_ffn1_w3, m_ffn1_w2, m_mix_norm, m_w_in, m_ret_gn, m_w_ret_o, m_conv_w, m_conv_b, m_w_rgate, m_b_rgate, m_w_igate, m_b_igate, m_lru_lambda, m_w_lru_o, m_w_branch_gate, m_b_branch_gate, m_w_out, m_xattn_norm, m_mem_norm, m_w_xq, m_w_xk, m_w_xv, m_w_xo, m_ffn2_norm, m_ffn2_w1, m_ffn2_w3, m_ffn2_w2, m_final_norm, v_ffn1_norm, v_ffn1_w1, v_ffn1_w3, v_ffn1_w2, v_mix_norm, v_w_in, v_ret_gn, v_w_ret_o, v_conv_w, v_conv_b, v_w_rgate, v_b_rgate, v_w_igate, v_b_igate, v_lru_lambda, v_w_lru_o, v_w_branch_gate, v_b_branch_gate, v_w_out, v_xattn_norm, v_mem_norm, v_w_xq, v_w_xk, v_w_xv, v_w_xo, v_ffn2_norm, v_ffn2_w1, v_ffn2_w3, v_ffn2_w2, v_final_norm):
    given = dict(locals())
    w = {n: given[n] for n in WEIGHT_ORDER}
    mom = {n: given["m_" + n] for n in WEIGHT_ORDER}
    var = {n: given["v_" + n] for n in WEIGHT_ORDER}
    chip = _chip_index(lax.axis_index("x"), lax.axis_index("y"))
    core = lax.axis_index("c").astype(jnp.int32).reshape(1)

    chip_id = chip.astype(jnp.int32).reshape(1)
    sm = {n: w[n] for n in ["ffn1_norm", "mix_norm", "ret_gn", "conv_b", "b_rgate", "b_igate", "lru_lambda",
                            "xattn_norm", "mem_norm", "ffn2_norm", "b_branch_gate"]}
    sm["final_norm"] = w["final_norm"].reshape(1, D)
    sm["w_rgate"] = w["w_rgate"][0]
    sm["w_igate"] = w["w_igate"][0]

    local = lambda a, n: jnp.swapaxes(a[0], 0, 1) if n in TRANSPOSED_WEIGHTS else a[0]
    stack = lambda names: jnp.stack([local(w[n], n) for n in names], axis=0).astype(BF16)
    shard = {"col1": stack(["ffn1_w1", "ffn1_w3"]), "row2a": stack(["ffn1_w2"]), "win": stack(["w_in"]),
             "wbg": stack(["w_branch_gate"]), "sqA": stack(["w_ret_o", "w_lru_o", "w_out"]),
             "sqB": stack(["w_xq", "w_xk"]), "sqC": stack(["w_xv", "w_xo"]), "col2": stack(["ffn2_w1", "ffn2_w3"]),
             "row2b": stack(["ffn2_w2"]), "conv": w["conv_w"]}
    gw, landed = {}, {}
    over_chips = lambda keys: _gather_chips_task({k: shard[k] for k in keys}, True, landed)
    to_sibling = lambda keys: _gather_sibling_task(keys, landed, gw)

    big, got, pair_sums, by_source, halves, sibling_halves, outs = {}, {}, {}, {}, {}, {}, {}
    pair_swap = lambda names: _pair_swap_task(names, big, got)
    exchange = lambda names, part=0, nparts=1: _chip_exchange_task(names, pair_sums, by_source, part, nparts)
    pair_gather = lambda names: _pair_gather_task(names, halves, sibling_halves)

    def pair_sum(names):
        for n in names:
            pair_sums[n] = _rs_pair_sum("rs_pair_sum_" + n, big[n], got[n], core)

    def chip_sum(names):
        for n in names:
            halves[n] = _rs_chip_sum("rs_chip_sum_" + n, pair_sums[n], by_source[n], chip_id)

    def adamw(names):
        for n in names:
            res = _adamw_halves("adamw_" + n, local(w[n], n), halves[n], sibling_halves[n], 0, local(mom[n], n),
                                local(var[n], n), core)
            outs[n] = tuple((jnp.swapaxes(r, 0, 1) if n in TRANSPOSED_WEIGHTS else r)[None] for r in res)

    do = lambda fn, names: functools.partial(fn, names)
    ffn2_grads = ["ffn2_w2", "ffn2_w1", "ffn2_w3"]
    xattn_grads = ["w_xo", "w_xq", "w_xk", "w_xv"]
    mix_out_grads = ["w_out", "w_ret_o", "w_lru_o"]
    mix_in_grads = ["w_in", "w_branch_gate"]
    plan = _Plan()
    plan.tasks = {
        "ag_first_chips": [over_chips(["col1", "row2a"])],
        "ag_first_sibling": [to_sibling(["col1", "row2a"])],
        "ffn1_up": [over_chips(["win"])],
        "ffn1_down": [to_sibling(["win"]), over_chips(["wbg"])],
        "mix_in": [to_sibling(["wbg"]), over_chips(["sqA"]), _gather_chips_task({"conv": shard["conv"]}, False, gw)],
        "mix_gates": [to_sibling(["sqA"]), over_chips(["row2b"])],
        "ret_fwd": [over_chips(["col2"]), to_sibling(["row2b"])],
        "lru_gates_fwd": [to_sibling(["col2"]), over_chips(["sqB"])],
        "lru_scan_fwd": [over_chips(["sqC"])],
        "y_lru": [to_sibling(["sqB", "sqC"])],
        "ffn2_dh": [pair_swap(ffn2_grads)],
        "dw_xo": [exchange(["ffn2_w2"], 0, 2)],
        "xattn_bwd": [exchange(["ffn2_w2"], 1, 2)],
        "dw_xq": [exchange(["ffn2_w1"], 0, 2)],
        "d_hq": [exchange(["ffn2_w1"], 1, 2)],
        "d_merged": [pair_swap(xattn_grads)],
        "dw_out": [exchange(["w_xo"])],
        "d_yr": [exchange(["w_xq"])],
        "dw_ret_o": [exchange(["w_xk"])],
        "dw_lru_o": [exchange(["w_xv"])],
        "ret_bwd": [exchange(["ffn2_w3"]), pair_swap(mix_out_grads), pair_gather(xattn_grads)],
        "lru_scan_bwd": [pair_gather(ffn2_grads)],
        "lru_gates_bwd": [exchange(mix_out_grads)],
        "dw_in": [pair_gather(mix_out_grads)],
        "d_h2": [pair_swap(mix_in_grads)],
        "ffn1_bwd_mid": [exchange(["w_in"], 0, 2), exchange(["w_branch_gate"], 0, 2)],
        "ffn1_dw2": [exchange(["w_in"], 1, 2)],
        "ffn1_dw1": [exchange(["w_branch_gate"], 1, 2), pair_swap(["ffn1_w2"])],
        "ffn1_dw3": [pair_gather(mix_in_grads), pair_swap(["ffn1_w1"]), exchange(["ffn1_w2"], 0, 2)],
        "ffn1_dh": [pair_swap(["ffn1_w3"]), exchange(["ffn1_w2"], 1, 2), exchange(["ffn1_w1"], 0, 2)],
        "rs_last": [exchange(["ffn1_w1"], 1, 2), exchange(["ffn1_w3"]), pair_gather(["ffn1_w2"])],
        "adamw_ffn1_w2": [pair_gather(["ffn1_w1", "ffn1_w3"])],
    }
    plan.after = {
        "ffn2_dh": [do(pair_sum, ffn2_grads)],
        "d_merged": [do(pair_sum, xattn_grads)],
        "dw_lru_o": [do(chip_sum, xattn_grads)],
        "ret_bwd": [do(chip_sum, ffn2_grads), do(pair_sum, mix_out_grads), do(adamw, xattn_grads)],
        "lru_scan_bwd": [do(adamw, ffn2_grads)],
        "lru_gates_bwd": [do(chip_sum, mix_out_grads)],
        "dw_in": [do(adamw, mix_out_grads)],
        "d_h2": [do(pair_sum, mix_in_grads)],
        "ffn1_dw1": [do(chip_sum, mix_in_grads), do(pair_sum, ["ffn1_w2"])],
        "ffn1_dw3": [do(pair_sum, ["ffn1_w1"]), do(adamw, mix_in_grads)],
        "ffn1_dh": [do(pair_sum, ["ffn1_w3"]), do(chip_sum, ["ffn1_w2"])],
        "rs_last": [do(chip_sum, ["ffn1_w1", "ffn1_w3"]), do(adamw, ["ffn1_w2", "ffn1_w1", "ffn1_w3"])],
    }
    global _plan
    _plan = plan
    try:
        _comm_call("ag_first_chips")
        _comm_call("ag_first_sibling")
        loss_part, grad_x, small = _local_step(x[0], mem[0], loss_target[0], gw, sm, big)
        _comm_call("rs_last")
    finally:
        _plan = None
    assert not plan.tasks and not plan.after, (list(plan.tasks), list(plan.after))
    loss = lax.psum(loss_part[0, 0], ("x", "y", "c"))

    small_shapes = {n: w[n].shape for n, _ in SMALL_LAYOUT}
    small_shapes["conv_w"] = (CONV_TAPS, D)
    small_sum = _small_allreduce(_pack_small(small))
    conv_grad = lax.dynamic_slice(small_sum[13:13 + CONV_TAPS], (0, chip * SQ_BLK), (CONV_TAPS, SQ_BLK))
    small_w = {n: w[n] for n, _ in SMALL_LAYOUT}
    small_m = {n: mom[n] for n, _ in SMALL_LAYOUT}
    small_v = {n: var[n] for n, _ in SMALL_LAYOUT}
    pad_cols = lambda a: jnp.pad(a[0], ((0, 0), (0, D - SQ_BLK)))
    for dct in (small_w, small_m, small_v):
        dct["conv_w"] = pad_cols(dct["conv_w"])
    g_pack = lax.dynamic_update_slice(small_sum, jnp.pad(conv_grad, ((0, 0), (0, D - SQ_BLK))), (13, 0))
    d_pack, m_pack, v_pack = _adamw("adamw_small", _pack_small(small_w), g_pack, _pack_small(small_m),
                                    _pack_small(small_v))
    unpacked = [_unpack_small(p, small_shapes) for p in (g_pack, d_pack, m_pack, v_pack)]
    for n, _ in SMALL_LAYOUT:
        if n == "conv_w":
            outs[n] = tuple(u[n][:, :SQ_BLK][None] for u in unpacked)
        else:
            outs[n] = tuple(u[n] for u in unpacked)

    result = [loss, grad_x[None]]
    for k in range(4):
        result += [outs[n][k] for n in WEIGHT_ORDER]
    return tuple(result)
```

```python
import functools
import math

import numpy as np
import jax
import jax.numpy as jnp
from jax import lax
from jax.experimental import pallas as pl
from jax.experimental.pallas import tpu as pltpu

F32 = jnp.float32
BF16 = jnp.bfloat16
MESH = pl.DeviceIdType.MESH

D = 1024
EPS = 1e-6
RET_HEADS = 4
RET_DK = 128
RET_DV = 256
CHUNK = 128
ROPE_BASE = 10000.0
LRU_BLOCKS = 8
LRU_BLOCK = 128
CONV_TAPS = 4
LRU_C = 8.0
D_FF = 2816
X_HEADS = 4
X_HD = 256
N_CHIPS = 4
FF_BLK = D_FF // N_CHIPS
IN_BLK = 5120 // N_CHIPS
BG_BLK = 2048 // N_CHIPS
SQ_BLK = D // N_CHIPS

ADAM_LR = 0.001
ADAM_B1 = 0.9
ADAM_B2 = 0.999
ADAM_EPS = 1e-08
ADAM_WD = 0.01
ADAM_STEP = 10

VMEM_LIMIT_BYTES = 56 * 1024 * 1024
ROW_TILE = 512
WIDE_ROW_TILE = 1024
SCAN_TILE = 256

_DN = {
    "nn": (((1,), (0,)), ((), ())),
    "nt": (((1,), (1,)), ((), ())),
    "tn": (((0,), (0,)), ((), ())),
}


def _cparams(n_axes):
    return pltpu.CompilerParams(dimension_semantics=("arbitrary",) * n_axes,
                                vmem_limit_bytes=VMEM_LIMIT_BYTES)


def _dot(a, b, kind):
    if b.ndim == 3:
        b = b.reshape(b.shape[0] * b.shape[1], b.shape[2])
    return lax.dot_general(a.astype(BF16), b.astype(BF16), _DN[kind], preferred_element_type=F32)


def _sigmoid(x):
    return 1.0 / (1.0 + jnp.exp(-x))


def _log1p_pos(e):
    u = 1.0 + e
    return jnp.where(u == 1.0, e, jnp.log(u) * (e / jnp.where(u == 1.0, 1.0, u - 1.0)))


def _expm1(x):
    u = jnp.exp(x)
    lu = jnp.log(u)
    safe = jnp.where(lu == 0.0, 1.0, lu)
    return jnp.where(u == 1.0, x, (u - 1.0) * (x / safe))


def _softplus(z):
    return jnp.maximum(z, 0.0) + _log1p_pos(jnp.exp(-jnp.abs(z)))


_GELU_C = math.sqrt(2.0 / math.pi)


def _gelu_and_grad(x):
    x2 = x * x
    t = jnp.tanh(_GELU_C * (x + 0.044715 * x * x2))
    g = 0.5 * x * (1.0 + t)
    dg = 0.5 * (1.0 + t) + 0.5 * x * (1.0 - t * t) * (_GELU_C * (1.0 + 3.0 * 0.044715 * x2))
    return g, dg


def _rms_fwd(x, g):
    r = lax.rsqrt(jnp.mean(x * x, axis=-1, keepdims=True) + EPS)
    return (x * r) * g


def _rms_bwd(x, g, dh):
    r = lax.rsqrt(jnp.mean(x * x, axis=-1, keepdims=True) + EPS)
    n = x * r
    dyg = dh * g
    dx = r * (dyg - n * jnp.mean(dyg * n, axis=-1, keepdims=True))
    return dx, jnp.sum(dh * n, axis=0, keepdims=True)


def _accumulate(ref, val, first):
    @pl.when(first)
    def _():
        ref[...] = val

    @pl.when(jnp.logical_not(first))
    def _():
        ref[...] += val


def _sds(shape, dtype):
    return jax.ShapeDtypeStruct(tuple(shape), dtype)


def _spec(shape, fn):
    return pl.BlockSpec(tuple(shape), fn)


class _Task:
    def __init__(self, operands, out_shapes, aliases, nsem, make, finish):
        self.operands, self.out_shapes, self.aliases = operands, out_shapes, aliases
        self.nsem, self.make, self.finish = nsem, make, finish


class _Plan:
    def __init__(self):
        self.tasks, self.after = {}, {}


_plan = None


def _pcall(body, *, name, grid, in_specs, out_specs, out_shape, scratch_shapes=(), num_prefetch=0):
    single = not isinstance(out_shape, (list, tuple))
    out_shape = [out_shape] if single else list(out_shape)
    out_specs = [out_specs] if single else list(out_specs)
    in_specs = list(in_specs)
    scratch_shapes = list(scratch_shapes)
    tasks = _plan.tasks.pop(name, []) if _plan is not None else []
    after = _plan.after.pop(name, []) if _plan is not None else []
    nax = len(grid)

    def run(*operands):
        n_in = len(operands) - num_prefetch
        n_out = len(out_shape)
        t_ops = [t.operands() for t in tasks]
        t_outs = [t.out_shapes() for t in tasks]
        c_ops = [a for ops in t_ops for a in ops]
        c_outs = [s for outs in t_outs for s in outs]
        aliases = {}
        i0, o0 = num_prefetch + n_in, n_out
        for t, ops, outs in zip(tasks, t_ops, t_outs):
            for i_loc, o_loc in t.aliases.items():
                aliases[i0 + i_loc] = o0 + o_loc
            i0 += len(ops)
            o0 += len(outs)
        nsem = sum(t.nsem for t in tasks)

        def wrapped(*refs):
            p = num_prefetch
            pre, ins = refs[:p], refs[p:p + n_in]
            cins = refs[p + n_in:p + n_in + len(c_ops)]
            q = p + n_in + len(c_ops)
            outs, couts = refs[q:q + n_out], refs[q + n_out:q + n_out + len(c_outs)]
            q += n_out + len(c_outs)
            scr = refs[q:q + len(scratch_shapes)]

            def descriptors():
                send_sems, recv_sems = refs[q + len(scratch_shapes):]
                starts, arrivals = [], []
                ci = co = so = 0
                for t, ops, souts in zip(tasks, t_ops, t_outs):
                    s, a = t.make(cins[ci:ci + len(ops)], couts[co:co + len(souts)],
                                  functools.partial(lambda base, k: send_sems.at[base + k], so),
                                  functools.partial(lambda base, k: recv_sems.at[base + k], so))
                    starts += s
                    arrivals += a
                    ci, co, so = ci + len(ops), co + len(souts), so + t.nsem
                return starts, arrivals

            if tasks:
                ids = [pl.program_id(k) for k in range(nax)]
                first = functools.reduce(jnp.logical_and, [i == 0 for i in ids])
                last = functools.reduce(jnp.logical_and, [i == g - 1 for i, g in zip(ids, grid)])

                @pl.when(first)
                def _():
                    for cp in descriptors()[0]:
                        cp.start()

            body(*pre, *ins, *outs, *scr)

            if tasks:
                @pl.when(last)
                def _():
                    starts, arrivals = descriptors()
                    for arrival in arrivals:
                        arrival().wait_recv()
                    for cp in starts:
                        cp.wait_send()

        sems = [pltpu.SemaphoreType.DMA((nsem,)), pltpu.SemaphoreType.DMA((nsem,))] if tasks else []
        res = pl.pallas_call(
            wrapped, name=name,
            grid_spec=pltpu.PrefetchScalarGridSpec(
                num_scalar_prefetch=num_prefetch, grid=tuple(grid),
                in_specs=in_specs + [ANY_SPEC] * len(c_ops),
                out_specs=out_specs + [ANY_SPEC] * len(c_outs),
                scratch_shapes=scratch_shapes + sems),
            out_shape=out_shape + c_outs,
            input_output_aliases=aliases,
            compiler_params=_cparams(nax),
        )(*operands, *c_ops)
        co = n_out
        for t, souts in zip(tasks, t_outs):
            t.finish(res[co:co + len(souts)])
            co += len(souts)
        for fn in after:
            fn()
        return res[0] if single else list(res[:n_out])

    return run


def _comm_call(name):
    def body(o_ref):
        o_ref[...] = jnp.zeros_like(o_ref)

    _pcall(body, name=name, grid=(1,), in_specs=[], out_specs=_spec((8, 128), lambda i: (0, 0)),
           out_shape=_sds((8, 128), F32))()


def _gemm(name, terms, grid, outs, acc_shape, extras=(), epilogue=None):
    kinds = [t[4] for t in terms]
    nt, ne, no = len(terms), len(extras), len(outs)
    nred = grid[-1]
    nax = len(grid)

    def body(*refs):
        trefs = refs[:2 * nt]
        erefs = refs[2 * nt:2 * nt + ne]
        orefs = refs[2 * nt + ne:2 * nt + ne + no]
        ids = [pl.program_id(k) for k in range(nax)]
        tot = None
        for t in range(nt):
            d = _dot(trefs[2 * t][...], trefs[2 * t + 1][...], kinds[t])
            tot = d if tot is None else tot + d

        def finish(acc):
            if epilogue is None:
                orefs[0][...] = acc.astype(orefs[0].dtype)
            else:
                epilogue(acc, erefs, orefs, ids)

        if nred == 1:
            finish(tot)
        else:
            acc_ref = refs[-1]
            r = ids[-1]

            @pl.when(r == 0)
            def _():
                acc_ref[...] = tot

            @pl.when(r > 0)
            def _():
                acc_ref[...] += tot

            @pl.when(r == nred - 1)
            def _():
                finish(acc_ref[...])

    operands, in_specs = [], []
    for a, a_spec, b, b_spec, _ in terms:
        operands += [a, b]
        in_specs += [a_spec, b_spec]
    for e, e_spec in extras:
        operands.append(e)
        in_specs.append(e_spec)
    scratch = [pltpu.VMEM(tuple(acc_shape), F32)] if nred > 1 else []
    return _pcall(body, name=name, grid=tuple(grid), in_specs=in_specs, out_specs=[o[1] for o in outs],
                  out_shape=[o[0] for o in outs], scratch_shapes=scratch)(*operands)


def _rowwise(name, fn, ins, outs, grid):
    ni = len(ins)
    nax = len(grid)

    def body(*refs):
        ids = [pl.program_id(k) for k in range(nax)]
        fn(refs[:ni], refs[ni:], ids)

    return _pcall(body, name=name, grid=tuple(grid), in_specs=[i[1] for i in ins],
                  out_specs=[o[1] for o in outs], out_shape=[o[0] for o in outs])(*[i[0] for i in ins])


def _ffn_up(name, h, wcol, w1_idx, w3_idx):
    T = h.shape[0]
    tm = min(WIDE_ROW_TILE, T)

    def body(h_ref, w1_ref, w3_ref, a_ref, b_ref, s_ref):
        hv = h_ref[...]
        a = _dot(hv, w1_ref[...], "nt")
        b = _dot(hv, w3_ref[...], "nt")
        a_ref[...] = a.astype(BF16)
        b_ref[...] = b.astype(BF16)
        s_ref[...] = ((a * _sigmoid(a)) * b).astype(BF16)

    blk = _spec((None, tm, FF_BLK), lambda j, i: (j, i, 0))
    return _pcall(
        body, name=name, grid=(N_CHIPS, T // tm),
        in_specs=[_spec((tm, D), lambda j, i: (i, 0)),
                  _spec((None, None, FF_BLK, D), lambda j, i: (j, w1_idx, 0, 0)),
                  _spec((None, None, FF_BLK, D), lambda j, i: (j, w3_idx, 0, 0))],
        out_specs=[blk, blk, blk],
        out_shape=[_sds((N_CHIPS, T, FF_BLK), BF16)] * 3,
    )(h, wcol, wcol)


def _ffn_down(name, s, wrow2, w2_idx, x_res, g_next=None):
    T = x_res.shape[0]
    tm = min(WIDE_ROW_TILE, T)
    row = lambda i, j, r: (i, 0)

    def epilogue(acc, erefs, orefs, ids):
        xo = erefs[0][...] + 0.5 * acc
        orefs[0][...] = xo
        if g_next is not None:
            orefs[1][...] = _rms_fwd(xo, erefs[1][...]).astype(BF16)

    extras = [(x_res, _spec((tm, D), row))]
    outs = [(_sds((T, D), F32), _spec((tm, D), row))]
    if g_next is not None:
        extras.append((g_next, _spec((1, D), lambda i, j, r: (0, 0))))
        outs.append((_sds((T, D), BF16), _spec((tm, D), row)))
    return _gemm(
        name,
        [(s, _spec((None, tm, FF_BLK), lambda i, j, r: (r, i, 0)),
          wrow2, _spec((None, None, FF_BLK, D), lambda i, j, r: (r, w2_idx, 0, 0)), "nn")],
        (T // tm, 1, N_CHIPS), outs, (tm, D), extras, epilogue)


def _ffn_bwd_mid(name, dx, wrow2, w2_idx, a, b):
    T = dx.shape[0]
    tm = min(WIDE_ROW_TILE, T)

    def body(dx_ref, w2_ref, a_ref, b_ref, dab_ref):
        ds = _dot(0.5 * dx_ref[...], w2_ref[...], "nt")
        av = a_ref[...].astype(F32)
        sg = _sigmoid(av)
        dab_ref[0] = (ds * b_ref[...].astype(F32) * (sg * (1.0 + av * (1.0 - sg)))).astype(BF16)
        dab_ref[1] = (ds * (av * sg)).astype(BF16)

    blk = _spec((None, tm, FF_BLK), lambda j, i: (j, i, 0))
    return _pcall(
        body, name=name, grid=(N_CHIPS, T // tm),
        in_specs=[_spec((tm, D), lambda j, i: (i, 0)),
                  _spec((None, None, FF_BLK, D), lambda j, i: (j, w2_idx, 0, 0)),
                  blk, blk],
        out_specs=_spec((2, None, tm, FF_BLK), lambda j, i: (0, j, i, 0)),
        out_shape=_sds((2, N_CHIPS, T, FF_BLK), BF16),
    )(dx, wrow2, a, b)


def _rms_bwd_epilogue(acc, erefs, orefs, ids):
    dx, dgp = _rms_bwd(erefs[0][...], erefs[1][...], acc)
    orefs[0][...] = dx + erefs[2][...]
    _accumulate(orefs[1], dgp, ids[0] == 0)


def _rms_bwd_io(x, g, dres, T, tm):
    row = lambda i, j, r: (i, 0)
    vec = lambda i, j, r: (0, 0)
    extras = [(x, _spec((tm, D), row)), (g, _spec((1, D), vec)), (dres, _spec((tm, D), row))]
    outs = [(_sds((T, D), F32), _spec((tm, D), row)), (_sds((1, D), F32), _spec((1, D), vec))]
    return extras, outs


def _ffn_bwd(tag, dx_out, h, a, b, s, wcol, w1_idx, w3_idx, wrow2, w2_idx, x_in, g, big):
    T = dx_out.shape[0]
    tm = ROW_TILE
    tk = ROW_TILE
    dab = _ffn_bwd_mid(tag + "_bwd_mid", dx_out, wrow2, w2_idx, a, b)

    def half_scale(acc, erefs, orefs, ids):
        orefs[0][...] = 0.5 * acc

    big[tag + "_w2"] = _gemm(
        tag + "_dw2",
        [(s, _spec((None, tk, FF_BLK), lambda j, n, r: (j, r, 0)),
          dx_out, _spec((tk, D), lambda j, n, r: (r, 0)), "tn")],
        (N_CHIPS, 1, T // tk),
        [(_sds((N_CHIPS, FF_BLK, D), F32), _spec((None, FF_BLK, D), lambda j, n, r: (j, 0, 0)))],
        (FF_BLK, D), (), half_scale)[0][None]
    for widx, wname in ((0, "_w1"), (1, "_w3")):
        big[tag + wname] = _gemm(
            tag + "_d" + wname[1:],
            [(dab, _spec((None, None, tk, FF_BLK), functools.partial(lambda w, j, n, r: (w, j, r, 0), widx)),
              h, _spec((tk, D), lambda j, n, r: (r, 0)), "tn")],
            (N_CHIPS, 1, T // tk),
            [(_sds((N_CHIPS, FF_BLK, D), F32), _spec((None, FF_BLK, D), lambda j, n, r: (j, 0, 0)))],
            (FF_BLK, D))[0][None]
    tw = min(WIDE_ROW_TILE, T)
    extras, outs = _rms_bwd_io(x_in, g, dx_out, T, tw)
    dx_in, dg = _gemm(
        tag + "_dh",
        [(dab, _spec((None, None, tw, FF_BLK), lambda i, j, r: (0, r, i, 0)),
          wcol, _spec((None, None, FF_BLK, D), lambda i, j, r: (r, w1_idx, 0, 0)), "nn"),
         (dab, _spec((None, None, tw, FF_BLK), lambda i, j, r: (1, r, i, 0)),
          wcol, _spec((None, None, FF_BLK, D), lambda i, j, r: (r, w3_idx, 0, 0)), "nn")],
        (T // tw, 1, N_CHIPS), outs, (tw, D), extras, _rms_bwd_epilogue)
    return dx_in, dg


def _proj_sq(name, a, wsq, idx, kind, out_dtype=F32, extras=(), epilogue=None, outs=None):
    M = a.shape[0]
    tm = min(ROW_TILE, M)
    if outs is None:
        outs = [(_sds((M, D), out_dtype), _spec((tm, D), lambda i, j, r: (i, 0)))]
    return _gemm(
        name,
        [(a, _spec((tm, D), lambda i, j, r: (i, 0)),
          wsq, _spec((N_CHIPS, None, SQ_BLK, D), lambda i, j, r: (0, idx, 0, 0)), kind)],
        (M // tm, 1, 1), outs, (tm, D), extras, epilogue)


def _dw_sq(name, a, b):
    M = a.shape[0]
    tk = min(ROW_TILE, M)
    whole = _gemm(
        name,
        [(a, _spec((tk, D), lambda i, j, r: (r, 0)), b, _spec((tk, D), lambda i, j, r: (r, 0)), "tn")],
        (1, 1, M // tk),
        [(_sds((D, D), F32), _spec((D, D), lambda i, j, r: (0, 0)))],
        (D, D))[0]
    return whole.reshape(N_CHIPS, SQ_BLK, D)


def _retention_constants(T):
    pos = jnp.arange(T, dtype=F32)
    inv_freq = ROPE_BASE ** (-jnp.arange(0, RET_DK, 2, dtype=F32) / RET_DK)
    ang = pos[:, None] * inv_freq[None, :]
    cosf = jnp.concatenate([jnp.cos(ang), jnp.cos(ang)], axis=1)
    sins = jnp.concatenate([-jnp.sin(ang), jnp.sin(ang)], axis=1)
    lg = jnp.log(1.0 - 2.0 ** (-5.0 - jnp.arange(RET_HEADS, dtype=F32)))
    p = jnp.arange(CHUNK, dtype=F32)
    rel = p[:, None] - p[None, :]
    dmat = jnp.where(rel[None] >= 0, jnp.exp(rel[None] * lg[:, None, None]), 0.0)
    kd = jnp.exp((CHUNK - 1.0 - p)[None, :] * lg[:, None])[:, :, None]
    qd = jnp.exp((p + 1.0)[None, :] * lg[:, None])[:, :, None]
    cd = jnp.exp(CHUNK * lg)[:, None, None]
    return cosf, sins, dmat, kd, qd, cd


def _rot(t, cosv, sinv):
    return t * cosv + pltpu.roll(t, RET_DK // 2, 1) * sinv


def _unrot(t, cosv, sinv):
    return t * cosv - pltpu.roll(t, RET_DK // 2, 1) * sinv


def _ret_const_specs(cm):
    whole = lambda shape: _spec(shape, lambda c: (0,) * len(shape))
    return [
        _spec((CHUNK, RET_DK), lambda c: (cm(c), 0)),
        _spec((CHUNK, RET_DK), lambda c: (cm(c), 0)),
        whole((RET_HEADS, CHUNK, CHUNK)), whole((RET_HEADS, CHUNK, 1)), whole((RET_HEADS, CHUNK, 1)),
        whole((RET_HEADS, 1, 1)),
    ]


def _head(h, width):
    return slice(h * width, (h + 1) * width)


def _ret_fwd(u, consts, ret_gn):
    T = u.shape[0]
    nC = T // CHUNK
    kscale = RET_DK ** -0.5

    def body(q_ref, k_ref, v_ref, g_ref, cos_ref, sin_ref, dm_ref, kd_ref, qd_ref, cd_ref, gn_ref,
             qr_ref, kr_ref, ret_ref, yr_ref, st_ref, state):
        @pl.when(pl.program_id(0) == 0)
        def _():
            state[...] = jnp.zeros_like(state)

        cosv, sinv = cos_ref[...], sin_ref[...]
        for h in range(RET_HEADS):
            hk, hv = _head(h, RET_DK), _head(h, RET_DV)
            q = _rot(q_ref[:, hk], cosv, sinv)
            k = _rot(k_ref[:, hk], cosv, sinv) * kscale
            v = v_ref[:, hv]
            qr_ref[:, hk] = q
            kr_ref[:, hk] = k
            prev = state[h]
            st_ref[h] = prev
            s = _dot(q, k, "nt") * dm_ref[h]
            ret = _dot(s, v, "nn") + _dot(q, prev, "nn") * qd_ref[h]
            state[h] = cd_ref[h] * prev + _dot(k * kd_ref[h], v, "tn")
            ret_ref[:, hv] = ret
            mu = jnp.mean(ret, axis=-1, keepdims=True)
            xc = ret - mu
            yn = xc * lax.rsqrt(jnp.mean(xc * xc, axis=-1, keepdims=True) + EPS)
            g = g_ref[:, hv]
            yr_ref[:, hv] = ((g * _sigmoid(g)) * (yn * gn_ref[:, hv])).astype(BF16)

    cm = lambda c: c
    qk_w, v_w = RET_HEADS * RET_DK, RET_HEADS * RET_DV
    in_specs = [
        _spec((CHUNK, qk_w), lambda c: (c, 0)), _spec((CHUNK, qk_w), lambda c: (c, 1)),
        _spec((CHUNK, v_w), lambda c: (c, 1)), _spec((CHUNK, v_w), lambda c: (c, 2)),
    ] + _ret_const_specs(cm) + [_spec((1, v_w), lambda c: (0, 0))]
    qk_out = _spec((CHUNK, qk_w), lambda c: (c, 0))
    v_out = _spec((CHUNK, v_w), lambda c: (c, 0))
    return _pcall(
        body, name="ret_fwd", grid=(nC,),
        in_specs=in_specs,
        out_specs=[qk_out, qk_out, v_out, v_out,
                   _spec((RET_HEADS, None, RET_DK, RET_DV), lambda c: (0, c, 0, 0))],
        out_shape=[_sds((T, qk_w), F32), _sds((T, qk_w), F32), _sds((T, v_w), F32), _sds((T, v_w), BF16),
                   _sds((RET_HEADS, nC, RET_DK, RET_DV), F32)],
        scratch_shapes=[pltpu.VMEM((RET_HEADS, RET_DK, RET_DV), F32)],
    )(u, u, u, u, *consts, ret_gn)


def _ret_bwd(dyr, ret, u, qr, kr, states, consts, ret_gn):
    T = u.shape[0]
    nC = T // CHUNK
    kscale = RET_DK ** -0.5

    def body(dyr_ref, ret_ref, g_ref, q_ref, k_ref, v_ref, st_ref,
             cos_ref, sin_ref, dm_ref, kd_ref, qd_ref, cd_ref, gn_ref,
             dq_ref, dk_ref, dv_ref, dg_ref, dgn_ref, gstate):
        first = pl.program_id(0) == 0

        @pl.when(first)
        def _():
            gstate[...] = jnp.zeros_like(gstate)

        cosv, sinv = cos_ref[...], sin_ref[...]
        dgn_parts = []
        for h in range(RET_HEADS):
            hk, hv = _head(h, RET_DK), _head(h, RET_DV)
            ret = ret_ref[:, hv]
            mu = jnp.mean(ret, axis=-1, keepdims=True)
            xc = ret - mu
            rs = lax.rsqrt(jnp.mean(xc * xc, axis=-1, keepdims=True) + EPS)
            yn = xc * rs
            gn = gn_ref[:, hv]
            g = g_ref[:, hv]
            sg = _sigmoid(g)
            dyr_v = dyr_ref[:, hv]
            dretn = dyr_v * (g * sg)
            dg_ref[:, hv] = (dyr_v * (yn * gn) * (sg * (1.0 + g * (1.0 - sg)))).astype(BF16)
            dgn_parts.append(jnp.sum(dretn * yn, axis=0, keepdims=True))
            dyn = dretn * gn
            d_o = rs * (dyn - jnp.mean(dyn, axis=-1, keepdims=True)
                        - yn * jnp.mean(dyn * yn, axis=-1, keepdims=True))

            q, k, v = q_ref[:, hk], k_ref[:, hk], v_ref[:, hv]
            dmat, kd, qd = dm_ref[h], kd_ref[h], qd_ref[h]
            prev = st_ref[h]
            gnext = gstate[h]
            s = _dot(q, k, "nt") * dmat
            ds = _dot(d_o, v, "nt") * dmat
            doq = d_o * qd
            dq = _dot(ds, k, "nn") + _dot(doq, prev, "nt")
            dk = _dot(ds, q, "tn") + _dot(v, gnext, "nt") * kd
            dv = _dot(s, d_o, "tn") + _dot(k * kd, gnext, "nn")
            gstate[h] = cd_ref[h] * gnext + _dot(q, doq, "tn")
            dq_ref[:, hk] = _unrot(dq, cosv, sinv).astype(BF16)
            dk_ref[:, hk] = _unrot(dk * kscale, cosv, sinv).astype(BF16)
            dv_ref[:, hv] = dv.astype(BF16)
        _accumulate(dgn_ref, jnp.concatenate(dgn_parts, axis=1), first)

    cm = lambda c: nC - 1 - c
    qk_w, v_w = RET_HEADS * RET_DK, RET_HEADS * RET_DV
    vspec = lambda blk: _spec((CHUNK, v_w), lambda c: (cm(c), blk))
    qspec = _spec((CHUNK, qk_w), lambda c: (cm(c), 0))
    in_specs = [vspec(0), vspec(0), vspec(2), qspec, qspec, vspec(1),
                _spec((RET_HEADS, None, RET_DK, RET_DV), lambda c: (0, cm(c), 0, 0)),
                ] + _ret_const_specs(cm) + [_spec((1, v_w), lambda c: (0, 0))]
    return _pcall(
        body, name="ret_bwd", grid=(nC,),
        in_specs=in_specs,
        out_specs=[qspec, qspec, vspec(0), vspec(0), _spec((1, v_w), lambda c: (0, 0))],
        out_shape=[_sds((T, qk_w), BF16), _sds((T, qk_w), BF16), _sds((T, v_w), BF16), _sds((T, v_w), BF16),
                   _sds((1, v_w), F32)],
        scratch_shapes=[pltpu.VMEM((RET_HEADS, RET_DK, RET_DV), F32)],
    )(dyr, ret, u, qr, kr, u, states, *consts, ret_gn)


def _shift_down(x, s):
    rows = lax.broadcasted_iota(jnp.int32, x.shape, 0)
    return jnp.where(rows >= s, pltpu.roll(x, s, 0), 0.0)


def _shift_up(x, s):
    n = x.shape[0]
    rows = lax.broadcasted_iota(jnp.int32, x.shape, 0)
    return jnp.where(rows < n - s, pltpu.roll(x, n - s, 0), 0.0)


def _lru_specs(T):
    col = lambda off: _spec((T, LRU_BLOCK), lambda g: (0, off + g))
    vec = _spec((1, LRU_BLOCK), lambda g: (0, g))
    wblk = _spec((None, LRU_BLOCK, LRU_BLOCK), lambda g: (g, 0, 0))
    cw = _spec((CONV_TAPS, LRU_BLOCK), lambda g: (0, g))
    return col, vec, wblk, cw


def _lru_gates_fwd(u, conv_w, conv_b, w_r, b_r, w_i, b_i, lam):
    T = u.shape[0]
    col, vec, wblk, cw = _lru_specs(T)

    def body(x_ref, cw_ref, cb_ref, wr_ref, br_ref, wi_ref, bi_ref, lam_ref,
             xc_ref, r_ref, i_ref, a_ref, bx_ref):
        x = x_ref[...]
        w = cw_ref[...]
        xc = (_shift_down(x, 3) * w[0:1] + _shift_down(x, 2) * w[1:2] + _shift_down(x, 1) * w[2:3]
              + x * w[3:4] + cb_ref[...])
        r = _sigmoid(_dot(xc, wr_ref[...], "nn") + br_ref[...])
        i = _sigmoid(_dot(xc, wi_ref[...], "nn") + bi_ref[...])
        la = (-LRU_C) * r * _softplus(-lam_ref[...])
        xc_ref[...] = xc
        r_ref[...] = r
        i_ref[...] = i
        a_ref[...] = jnp.exp(la)
        bx_ref[...] = jnp.sqrt(-_expm1(2.0 * la)) * (i * xc)

    out = col(0)
    return _pcall(
        body, name="lru_gates_fwd", grid=(LRU_BLOCKS,),
        in_specs=[col(24), cw, vec, wblk, vec, wblk, vec, vec],
        out_specs=[out] * 5,
        out_shape=[_sds((T, D), F32)] * 5,
    )(u, conv_w, conv_b, w_r, b_r, w_i, b_i, lam)


def _lru_scan(name, a3, b3, reverse):
    T = a3.shape[0]
    nt = T // SCAN_TILE
    unroll = 8

    def body(a_ref, b_ref, o_ref, carry):
        @pl.when(pl.program_id(0) == 0)
        def _():
            carry[...] = jnp.zeros_like(carry)

        if not reverse:
            def step(t, h):
                h = a_ref[t] * h + b_ref[t]
                o_ref[t] = h
                return h
        else:
            def step(k, c):
                t = SCAN_TILE - 1 - k
                l = b_ref[t] + c
                o_ref[t] = l
                return a_ref[t] * l
        carry[...] = lax.fori_loop(0, SCAN_TILE, step, carry[...], unroll=unroll)

    idx = (lambda i: (nt - 1 - i, 0, 0)) if reverse else (lambda i: (i, 0, 0))
    blk = _spec((SCAN_TILE, LRU_BLOCKS, LRU_BLOCK), idx)
    return _pcall(
        body, name=name, grid=(nt,),
        in_specs=[blk, blk], out_specs=blk,
        out_shape=_sds((T, LRU_BLOCKS, LRU_BLOCK), F32),
        scratch_shapes=[pltpu.VMEM((LRU_BLOCKS, LRU_BLOCK), F32)],
    )(a3, b3)


def _lru_gates_bwd(lmb, hl, a, r, i, xc, u, conv_w, w_r, w_i, lam):
    T = u.shape[0]
    col, vec, wblk, cw = _lru_specs(T)

    def body(l_ref, h_ref, a_ref, r_ref, i_ref, xc_ref, x_ref, cw_ref, wr_ref, wi_ref, lam_ref,
             dx_ref, dwr_ref, dwi_ref, dvec_ref, dcw_ref):
        l = l_ref[...]
        av, rv, iv, xc = a_ref[...], r_ref[...], i_ref[...], xc_ref[...]
        lam_v = lam_ref[...]
        sp = _softplus(-lam_v)
        la = (-LRU_C) * rv * sp
        mult = jnp.sqrt(-_expm1(2.0 * la))
        da = l * _shift_down(h_ref[...], 1)
        dmult = l * (iv * xc)
        di = l * mult * xc
        dxc = l * mult * iv
        dla = da * av - dmult * (av * av) / mult
        dzr = (dla * ((-LRU_C) * sp)) * rv * (1.0 - rv)
        dzi = di * iv * (1.0 - iv)
        dsp = jnp.sum(dla * ((-LRU_C) * rv), axis=0, keepdims=True)
        dlam = dsp * (-_sigmoid(-lam_v))
        dwr_ref[...] = _dot(xc, dzr, "tn")
        dwi_ref[...] = _dot(xc, dzi, "tn")
        dxc = dxc + _dot(dzr, wr_ref[...], "nt") + _dot(dzi, wi_ref[...], "nt")
        x = x_ref[...]
        w = cw_ref[...]
        dx = (dxc * w[3:4] + _shift_up(dxc, 1) * w[2:3] + _shift_up(dxc, 2) * w[1:2]
              + _shift_up(dxc, 3) * w[0:1])
        dx_ref[...] = dx.astype(BF16)
        dvec_ref[...] = jnp.concatenate(
            [jnp.sum(dzr, axis=0, keepdims=True), jnp.sum(dzi, axis=0, keepdims=True), dlam,
             jnp.sum(dxc, axis=0, keepdims=True)], axis=0)
        dcw_ref[...] = jnp.concatenate(
            [jnp.sum(dxc * _shift_down(x, 3 - tap), axis=0, keepdims=True) if tap < 3
             else jnp.sum(dxc * x, axis=0, keepdims=True) for tap in range(CONV_TAPS)], axis=0)

    c0 = col(0)
    return _pcall(
        body, name="lru_gates_bwd", grid=(LRU_BLOCKS,),
        in_specs=[c0, c0, c0, c0, c0, c0, col(24), cw, wblk, wblk, vec],
        out_specs=[c0, wblk, wblk, cw, cw],
        out_shape=[_sds((T, D), BF16), _sds((LRU_BLOCKS, LRU_BLOCK, LRU_BLOCK), F32),
                   _sds((LRU_BLOCKS, LRU_BLOCK, LRU_BLOCK), F32), _sds((4, D), F32), _sds((CONV_TAPS, D), F32)],
    )(lmb, hl, a, r, i, xc, u, conv_w, w_r, w_i, lam)


def _xattn_probs(q, k):
    sc = _dot(q, k, "nt") * (X_HD ** -0.5)
    e = jnp.exp(sc - jnp.max(sc, axis=-1, keepdims=True))
    return e / jnp.sum(e, axis=-1, keepdims=True)


def _xattn_fwd(xq, xk, xv):
    T = xq.shape[0]
    tq = ROW_TILE
    M = xk.shape[0]

    def body(q_ref, k_ref, v_ref, o_ref):
        p = _xattn_probs(q_ref[...], k_ref[...])
        o_ref[...] = _dot(p, v_ref[...], "nn").astype(BF16)

    qs = _spec((tq, X_HD), lambda h, i: (i, h))
    kv = _spec((M, X_HD), lambda h, i: (0, h))
    return _pcall(
        body, name="xattn_fwd", grid=(X_HEADS, T // tq),
        in_specs=[qs, kv, kv], out_specs=qs, out_shape=_sds((T, D), BF16),
    )(xq, xk, xv)


def _xattn_bwd(xq, xk, xv, dxo):
    T = xq.shape[0]
    tq = ROW_TILE
    M = xk.shape[0]

    def body(q_ref, k_ref, v_ref, do_ref, dq_ref, dk_ref, dv_ref):
        first = pl.program_id(1) == 0
        q, k, v, do = q_ref[...], k_ref[...], v_ref[...], do_ref[...]
        p = _xattn_probs(q, k)
        dp = _dot(do, v, "nt")
        ds = p * (dp - jnp.sum(dp * p, axis=-1, keepdims=True)) * (X_HD ** -0.5)
        dq_ref[...] = _dot(ds, k, "nn").astype(BF16)
        _accumulate(dk_ref, _dot(ds, q, "tn"), first)
        _accumulate(dv_ref, _dot(p, do, "tn"), first)

    qs = _spec((tq, X_HD), lambda h, i: (i, h))
    kv = _spec((M, X_HD), lambda h, i: (0, h))
    return _pcall(
        body, name="xattn_bwd", grid=(X_HEADS, T // tq),
        in_specs=[qs, kv, kv, qs], out_specs=[qs, kv, kv],
        out_shape=[_sds((T, D), BF16), _sds((M, D), F32), _sds((M, D), F32)],
    )(xq, xk, xv, dxo)


def _final_loss(x, g, tgt):
    T = x.shape[0]
    tm = ROW_TILE

    def fn(irefs, orefs, ids):
        xv, gv = irefs[0][...], irefs[1][...]
        err = _rms_fwd(xv, gv) - irefs[2][...]
        lp = 0.5 * jnp.sum(jnp.mean(err * err, axis=-1, keepdims=True), axis=0, keepdims=True)
        first = ids[0] == 0
        _accumulate(orefs[0], jnp.broadcast_to(lp, (1, 128)), first)
        dx, dgp = _rms_bwd(xv, gv, err * (1.0 / D))
        orefs[1][...] = dx
        _accumulate(orefs[2], dgp, first)

    row = _spec((tm, D), lambda i: (i, 0))
    vec = _spec((1, D), lambda i: (0, 0))
    return _rowwise(
        "final_loss", fn, [(x, row), (g, vec), (tgt, row)],
        [(_sds((1, 128), F32), _spec((1, 128), lambda i: (0, 0))), (_sds((T, D), F32), row),
         (_sds((1, D), F32), vec)],
        (T // tm,))


def _adamw(name, w, g, m, v):
    R, C = w.shape
    tr = R
    for cand in (512, 352, 256):
        if R % cand == 0:
            tr = cand
            break

    def fn(irefs, orefs, ids):
        delta, mn, vn = _adamw_update(*(r[...] for r in irefs))
        orefs[0][...] = delta
        orefs[1][...] = mn
        orefs[2][...] = vn

    blk = _spec((tr, C), lambda i: (i, 0))
    return _rowwise(name, fn, [(w, blk), (g, blk), (m, blk), (v, blk)],
                    [(_sds((R, C), F32), blk)] * 3, (R // tr,))


def _adamw_update(wv, gv, mv, vv):
    c1 = 1.0 - ADAM_B1 ** ADAM_STEP
    c2 = 1.0 - ADAM_B2 ** ADAM_STEP
    mn = ADAM_B1 * mv + (1.0 - ADAM_B1) * gv
    vn = ADAM_B2 * vv + (1.0 - ADAM_B2) * (gv * gv)
    delta = -ADAM_LR * ((mn / c1) / (jnp.sqrt(vn / c2) + ADAM_EPS) + ADAM_WD * wv)
    return delta, mn, vn


def _adamw_halves(name, w, mine, theirs, widx, m, v, core):
    R, C = w.shape
    H = R // 2
    tr = H
    while tr * C * 4 > (1 << 20) and tr % 16 == 0:
        tr //= 2
    nb = H // tr

    def body(core_ref, w_ref, mine_ref, theirs_ref, m_ref, v_ref, g_out, d_out, m_out, v_out):
        gv = jnp.where(pl.program_id(0) == core_ref[0], mine_ref[...], theirs_ref[...])
        delta, mn, vn = _adamw_update(w_ref[...], gv, m_ref[...], v_ref[...])
        g_out[...] = gv
        d_out[...] = delta
        m_out[...] = mn
        v_out[...] = vn

    full = pl.BlockSpec((tr, C), lambda h, i, core_ref: (h * nb + i, 0))
    mine_spec = pl.BlockSpec((None, tr, C), lambda h, i, core_ref: (widx, jnp.where(h == core_ref[0], i, 0), 0))
    theirs_spec = pl.BlockSpec((None, tr, C), lambda h, i, core_ref: (widx, jnp.where(h == core_ref[0], 0, i), 0))
    return _pcall(
        body, name=name, grid=(2, nb), num_prefetch=1,
        in_specs=[full, mine_spec, theirs_spec, full, full], out_specs=[full] * 4,
        out_shape=[_sds((R, C), F32)] * 4,
    )(core, w, mine, theirs, m, v)


def _rmsnorm(name, x, g):
    M = x.shape[0]
    tm = min(ROW_TILE, M)

    def fn(irefs, orefs, ids):
        orefs[0][...] = _rms_fwd(irefs[0][...], irefs[1][...]).astype(BF16)

    row = _spec((tm, D), lambda i: (i, 0))
    return _rowwise(name, fn, [(x, row), (g, _spec((1, D), lambda i: (0, 0)))],
                    [(_sds((M, D), BF16), row)], (M // tm,))[0]


WEIGHT_AT = {
    "ffn1_w1": ("col1", 0), "ffn1_w3": ("col1", 1), "ffn1_w2": ("row2a", 0),
    "w_ret_o": ("sqA", 0), "w_lru_o": ("sqA", 1), "w_out": ("sqA", 2),
    "w_xq": ("sqB", 0), "w_xk": ("sqB", 1), "w_xv": ("sqC", 0), "w_xo": ("sqC", 1),
    "ffn2_w1": ("col2", 0), "ffn2_w3": ("col2", 1), "ffn2_w2": ("row2b", 0),
}


def _local_step(x, mem, tgt, gw, sm, big):
    T = x.shape[0]
    tm = ROW_TILE

    def wt(name):
        key, idx = WEIGHT_AT[name]
        return gw[key], idx

    row3 = lambda i, j, r: (i, 0)
    vec3 = lambda i, j, r: (0, 0)
    rowD = _spec((tm, D), row3)
    vecD = _spec((1, D), vec3)

    def residual_norm(acc, erefs, orefs, ids):
        xo = erefs[0][...] + acc
        orefs[0][...] = xo
        orefs[1][...] = _rms_fwd(xo, erefs[1][...]).astype(BF16)

    def res_norm_io(x_res, g):
        return ([(x_res, rowD), (g, vecD)],
                [(_sds((T, D), F32), rowD), (_sds((T, D), BF16), rowD)])

    h1 = _rmsnorm("ffn1_norm", x, sm["ffn1_norm"])
    a1, b1, s1 = _ffn_up("ffn1_up", h1, *wt("ffn1_w1"), wt("ffn1_w3")[1])
    x1, h2 = _ffn_down("ffn1_down", s1, *wt("ffn1_w2"), x, sm["mix_norm"])

    tw = min(WIDE_ROW_TILE, T)
    wideD = _spec((tw, D), row3)
    u = _gemm(
        "mix_in",
        [(h2, wideD, gw["win"], _spec((None, None, D, IN_BLK), lambda i, j, r: (j, 0, 0, 0)), "nn")],
        (T // tw, N_CHIPS, 1),
        [(_sds((T, 5120), F32), _spec((tw, IN_BLK), lambda i, j, r: (i, j)))], (tw, IN_BLK))[0]

    def gate_epilogue(acc, erefs, orefs, ids):
        orefs[0][...] = _sigmoid(acc + erefs[0][...])

    gates = _gemm(
        "mix_gates",
        [(h2, wideD, gw["wbg"], _spec((None, None, D, BG_BLK), lambda i, j, r: (j, 0, 0, 0)), "nn")],
        (T // tw, N_CHIPS, 1),
        [(_sds((T, 2 * D), F32), _spec((tw, BG_BLK), lambda i, j, r: (i, j)))], (tw, BG_BLK),
        [(sm["b_branch_gate"], _spec((1, BG_BLK), lambda i, j, r: (0, j)))], gate_epilogue)[0]

    consts = _retention_constants(T)
    qr, kr, ret, yr, states = _ret_fwd(u, consts, sm["ret_gn"])

    conv_w = gw["conv"][:, 0].transpose(1, 0, 2).reshape(CONV_TAPS, D)
    xc, rg, ig, av, bx = _lru_gates_fwd(u, conv_w, sm["conv_b"], sm["w_rgate"], sm["b_rgate"],
                                        sm["w_igate"], sm["b_igate"], sm["lru_lambda"])
    a3 = av.reshape(T, LRU_BLOCKS, LRU_BLOCK)
    hl = _lru_scan("lru_scan_fwd", a3, bx.reshape(T, LRU_BLOCKS, LRU_BLOCK), False).reshape(T, D)

    row1 = _spec((tm, D), lambda i: (i, 0))
    glru1 = _spec((tm, D), lambda i: (i, 4))

    def lru_out(irefs, orefs, ids):
        gl, _ = _gelu_and_grad(irefs[1][...])
        orefs[0][...] = (irefs[0][...] * gl).astype(BF16)

    yl = _rowwise("lru_out", lru_out, [(hl, row1), (u, glru1)], [(_sds((T, D), BF16), row1)], (T // tm,))[0]

    y_ret = _proj_sq("y_ret", yr, *wt("w_ret_o"), "nn")[0]

    def merge_epilogue(acc, erefs, orefs, ids):
        orefs[0][...] = acc
        orefs[1][...] = (erefs[0][...] * erefs[2][...] + erefs[1][...] * acc).astype(BF16)

    y_lru, merged = _proj_sq(
        "y_lru", yl, *wt("w_lru_o"), "nn",
        extras=[(gates, _spec((tm, D), lambda i, j, r: (i, 0))), (gates, _spec((tm, D), lambda i, j, r: (i, 1))),
                (y_ret, rowD)],
        epilogue=merge_epilogue,
        outs=[(_sds((T, D), F32), rowD), (_sds((T, D), BF16), rowD)])

    ex, ou = res_norm_io(x1, sm["xattn_norm"])
    x2, hq = _proj_sq("mix_out", merged, *wt("w_out"), "nn", extras=ex, epilogue=residual_norm, outs=ou)

    m = _rmsnorm("mem_norm", mem, sm["mem_norm"])
    xq = _proj_sq("xq", hq, *wt("w_xq"), "nn", BF16)[0]
    xk = _proj_sq("xk", m, *wt("w_xk"), "nn", BF16)[0]
    xv = _proj_sq("xv", m, *wt("w_xv"), "nn", BF16)[0]
    xo = _xattn_fwd(xq, xk, xv)
    ex, ou = res_norm_io(x2, sm["ffn2_norm"])
    x3, h3 = _proj_sq("xattn_out", xo, *wt("w_xo"), "nn", extras=ex, epilogue=residual_norm, outs=ou)

    a2, b2, s2 = _ffn_up("ffn2_up", h3, *wt("ffn2_w1"), wt("ffn2_w3")[1])
    x4 = _ffn_down("ffn2_down", s2, *wt("ffn2_w2"), x3)[0]
    loss, dx4, dg_final = _final_loss(x4, sm["final_norm"], tgt)

    dx3, dg_ffn2 = _ffn_bwd("ffn2", dx4, h3, a2, b2, s2, *wt("ffn2_w1"), wt("ffn2_w3")[1],
                            *wt("ffn2_w2"), x3, sm["ffn2_norm"], big)

    dxo = _proj_sq("d_xo", dx3, *wt("w_xo"), "nt", BF16)[0]
    big["w_xo"] = _dw_sq("dw_xo", xo, dx3)[None]
    dxq, dxk, dxv = _xattn_bwd(xq, xk, xv, dxo)
    big["w_xq"] = _dw_sq("dw_xq", hq, dxq)[None]
    ex, ou = _rms_bwd_io(x2, sm["xattn_norm"], dx3, T, tm)
    dx2, dg_xattn = _proj_sq("d_hq", dxq, *wt("w_xq"), "nt", extras=ex, epilogue=_rms_bwd_epilogue, outs=ou)
    big["w_xk"] = _dw_sq("dw_xk", m, dxk)[None]
    big["w_xv"] = _dw_sq("dw_xv", m, dxv)[None]

    M = mem.shape[0]

    def mem_norm_epilogue(acc, erefs, orefs, ids):
        _, dgp = _rms_bwd(erefs[0][...], erefs[1][...], acc)
        orefs[0][...] = dgp

    wsq_spec = lambda idx: _spec((N_CHIPS, None, SQ_BLK, D), lambda i, j, r: (0, idx, 0, 0))
    memD = _spec((M, D), row3)
    dg_mem = _gemm(
        "d_mem_norm",
        [(dxk, memD, wt("w_xk")[0], wsq_spec(wt("w_xk")[1]), "nt"),
         (dxv, memD, wt("w_xv")[0], wsq_spec(wt("w_xv")[1]), "nt")],
        (1, 1, 1), [(_sds((1, D), F32), vecD)], (M, D),
        [(mem, memD), (sm["mem_norm"], vecD)], mem_norm_epilogue)[0]

    def merged_bwd_epilogue(acc, erefs, orefs, ids):
        gr, gl, yrv, ylv = (e[...] for e in erefs)
        orefs[0][...] = (acc * gr).astype(BF16)
        orefs[1][...] = (acc * gl).astype(BF16)
        dgr = acc * yrv * gr * (1.0 - gr)
        dgl = acc * ylv * gl * (1.0 - gl)
        orefs[2][:, :D] = dgr.astype(BF16)
        orefs[2][:, D:] = dgl.astype(BF16)
        dbb = jnp.concatenate([jnp.sum(dgr, axis=0, keepdims=True), jnp.sum(dgl, axis=0, keepdims=True)], axis=1)
        _accumulate(orefs[3], dbb, ids[0] == 0)

    dy_ret, dy_lru, dgpre, db_bg = _proj_sq(
        "d_merged", dx2, *wt("w_out"), "nt",
        extras=[(gates, _spec((tm, D), lambda i, j, r: (i, 0))), (gates, _spec((tm, D), lambda i, j, r: (i, 1))),
                (y_ret, rowD), (y_lru, rowD)],
        epilogue=merged_bwd_epilogue,
        outs=[(_sds((T, D), BF16), rowD), (_sds((T, D), BF16), rowD),
              (_sds((T, 2 * D), BF16), _spec((tm, 2 * D), row3)),
              (_sds((1, 2 * D), F32), _spec((1, 2 * D), vec3))])
    big["w_out"] = _dw_sq("dw_out", merged, dx2)[None]
    dyr = _proj_sq("d_yr", dy_ret, *wt("w_ret_o"), "nt")[0]
    big["w_ret_o"] = _dw_sq("dw_ret_o", yr, dy_ret)[None]
    dyl = _proj_sq("d_yl", dy_lru, *wt("w_lru_o"), "nt")[0]
    big["w_lru_o"] = _dw_sq("dw_lru_o", yl, dy_lru)[None]

    dq, dk, dv, dgr, dg_retgn = _ret_bwd(dyr, ret, u, qr, kr, states, consts, sm["ret_gn"])

    def lru_out_bwd(irefs, orefs, ids):
        gl, dgl = _gelu_and_grad(irefs[2][...])
        dyl_v = irefs[0][...]
        orefs[0][...] = dyl_v * gl
        orefs[1][...] = (dyl_v * irefs[1][...] * dgl).astype(BF16)

    dhl, dglru = _rowwise("lru_out_bwd", lru_out_bwd, [(dyl, row1), (hl, row1), (u, glru1)],
                          [(_sds((T, D), F32), row1), (_sds((T, D), BF16), row1)], (T // tm,))
    lmb = _lru_scan("lru_scan_bwd", a3, dhl.reshape(T, LRU_BLOCKS, LRU_BLOCK), True).reshape(T, D)
    dxl, dw_r, dw_i, dvec, dcw = _lru_gates_bwd(lmb, hl, av, rg, ig, xc, u, conv_w,
                                                sm["w_rgate"], sm["w_igate"], sm["lru_lambda"])

    du = jnp.concatenate([dq, dk, dv, dgr, dxl, dglru], axis=1)
    tk = ROW_TILE
    big["w_in"] = _gemm(
        "dw_in",
        [(h2, _spec((tk, D), lambda j, n, r: (r, 0)), du, _spec((tk, IN_BLK), lambda j, n, r: (r, j)), "tn")],
        (N_CHIPS, 1, T // tk),
        [(_sds((N_CHIPS, D, IN_BLK), F32), _spec((None, D, IN_BLK), lambda j, n, r: (j, 0, 0)))],
        (D, IN_BLK))[0][None]
    big["w_branch_gate"] = _gemm(
        "dw_bg",
        [(h2, _spec((tk, D), lambda j, n, r: (r, 0)), dgpre, _spec((tk, BG_BLK), lambda j, n, r: (r, j)), "tn")],
        (N_CHIPS, 1, T // tk),
        [(_sds((N_CHIPS, D, BG_BLK), F32), _spec((None, D, BG_BLK), lambda j, n, r: (j, 0, 0)))],
        (D, BG_BLK))[0][None]
    ex, ou = _rms_bwd_io(x1, sm["mix_norm"], dx2, T, tw)
    dx1, dg_mix = _gemm(
        "d_h2",
        [(du, _spec((tw, IN_BLK), lambda i, j, r: (i, r)),
          gw["win"], _spec((None, None, D, IN_BLK), lambda i, j, r: (r, 0, 0, 0)), "nt"),
         (dgpre, _spec((tw, BG_BLK), lambda i, j, r: (i, r)),
          gw["wbg"], _spec((None, None, D, BG_BLK), lambda i, j, r: (r, 0, 0, 0)), "nt")],
        (T // tw, 1, N_CHIPS), ou, (tw, D), ex, _rms_bwd_epilogue)

    grad_x, dg_ffn1 = _ffn_bwd("ffn1", dx1, h1, a1, b1, s1, *wt("ffn1_w1"), wt("ffn1_w3")[1],
                               *wt("ffn1_w2"), x, sm["ffn1_norm"], big)

    small = {
        "ffn1_norm": dg_ffn1, "mix_norm": dg_mix, "ret_gn": dg_retgn, "conv_b": dvec[3:4],
        "b_rgate": dvec[0:1], "b_igate": dvec[1:2], "lru_lambda": dvec[2:3], "xattn_norm": dg_xattn,
        "mem_norm": dg_mem, "ffn2_norm": dg_ffn2, "final_norm": dg_final, "b_branch_gate": db_bg,
        "conv_w": dcw, "w_rgate": dw_r, "w_igate": dw_i,
    }
    return loss, grad_x, small


ANY_SPEC = pl.BlockSpec(memory_space=pl.ANY)
VMEM_SPEC = pl.BlockSpec(memory_space=pltpu.VMEM)
N_PEER_CHIPS = N_CHIPS - 1


def _mesh_position():
    x, y, c = lax.axis_index("x"), lax.axis_index("y"), lax.axis_index("c")
    chips = [(1 - x, y), (x, 1 - y), (1 - x, 1 - y)]
    return x, y, c, chips


def _chip_index(x, y):
    return 2 * x + y


def _rows_half(ref, axis, h):
    n = ref.shape[axis] // 2
    idx = [slice(None)] * len(ref.shape)
    idx[axis] = pl.ds(pl.multiple_of(h * n, 16), n)
    return ref.at[tuple(idx)]


def _remote(src, dst, send_sem, recv_sem, device):
    return pltpu.make_async_remote_copy(src_ref=src, dst_ref=dst, send_sem=send_sem, recv_sem=recv_sem,
                                        device_id=device, device_id_type=MESH)


def _gather_chips_task(shards, split, landed):
    keys = list(shards)
    n = len(keys)

    def operands():
        chip_me = _chip_index(lax.axis_index("x"), lax.axis_index("y"))
        bases = [lax.dynamic_update_slice(lax.empty((N_CHIPS,) + shards[k].shape, shards[k].dtype), shards[k][None],
                                          (chip_me,) + (0,) * shards[k].ndim) for k in keys]
        return [shards[k] for k in keys] + bases

    def make(ins, outs, send_sem, recv_sem):
        x, y, c, chips = _mesh_position()
        s_me = _chip_index(x, y)
        starts, arrivals = [], []
        for g in range(n):
            mine = _rows_half(ins[g], 1, c) if split else ins[g]
            for k, chip in enumerate(chips):
                def landing(s):
                    o = outs[g].at[s]
                    return _rows_half(o, 1, c) if split else o
                starts.append(_remote(mine, landing(s_me), send_sem(3 * g + k), recv_sem(3 * g + k), (*chip, c)))
                got = landing(_chip_index(*chip))
                arrivals.append(functools.partial(_remote, got, got, send_sem(3 * g + k), recv_sem(3 * g + k),
                                                  (*chip, c)))
        return starts, arrivals

    def finish(res):
        landed.update(zip(keys, res))

    return _Task(operands, lambda: [_sds((N_CHIPS,) + shards[k].shape, shards[k].dtype) for k in keys],
                 {n + g: g for g in range(n)}, 3 * n, make, finish)


def _gather_sibling_task(keys, landed, ready):
    n = len(keys)

    def make(ins, outs, send_sem, recv_sem):
        x, y, c, chips = _mesh_position()
        starts, arrivals = [], []
        for g in range(n):
            for k, chip in enumerate(chips):
                o = outs[g].at[_chip_index(*chip)]
                got, other = _rows_half(o, 1, c), _rows_half(o, 1, 1 - c)
                starts.append(_remote(got, got, send_sem(3 * g + k), recv_sem(3 * g + k), (x, y, 1 - c)))
                arrivals.append(functools.partial(_remote, other, other, send_sem(3 * g + k), recv_sem(3 * g + k),
                                                  (x, y, 1 - c)))
        return starts, arrivals

    def finish(res):
        ready.update(zip(keys, res))

    return _Task(lambda: [landed[k] for k in keys], lambda: [_sds(landed[k].shape, landed[k].dtype) for k in keys],
                 {g: g for g in range(n)}, 3 * n, make, finish)


def _pair_swap_task(names, big, got):
    n = len(names)

    def make(ins, outs, send_sem, recv_sem):
        x, y, c, _ = _mesh_position()
        copies = [_remote(_rows_half(ins[a], 2, 1 - c), outs[a], send_sem(a), recv_sem(a), (x, y, 1 - c))
                  for a in range(n)]
        return copies, [functools.partial(lambda cp: cp, cp) for cp in copies]

    def shapes():
        return [_sds(big[k].shape[:2] + (big[k].shape[2] // 2, big[k].shape[3]), F32) for k in names]

    return _Task(lambda: [big[k] for k in names], shapes, {}, n, make, lambda res: got.update(zip(names, res)))


def _rs_pair_sum(name, full, got, core):
    nw, ns, R, C = full.shape
    half = R // 2

    def body(core_ref, a_ref, b_ref, o_ref):
        o_ref[...] = (a_ref[...] + b_ref[...]).astype(BF16)

    blk = lambda fn: pl.BlockSpec((None, None, half, C), fn)
    return _pcall(
        body, name=name, grid=(nw, ns), num_prefetch=1,
        in_specs=[blk(lambda w, s, core_ref: (w, s, core_ref[0], 0)), blk(lambda w, s, core_ref: (w, s, 0, 0))],
        out_specs=blk(lambda w, s, core_ref: (w, s, 0, 0)),
        out_shape=_sds((nw, ns, half, C), BF16),
    )(core, full, got)


def _chip_exchange_task(names, pair_sums, by_source, part=0, nparts=1):
    n = len(names)

    def rows(ref):
        h = ref.shape[1] // nparts
        return ref.at[:, pl.ds(part * h, h), :]

    def make(ins, outs, send_sem, recv_sem):
        x, y, c, chips = _mesh_position()
        s_me = _chip_index(x, y)
        starts, arrivals = [], []
        for a in range(n):
            for k, chip in enumerate(chips):
                s_k = _chip_index(*chip)
                starts.append(_remote(rows(ins[a].at[:, s_k]), rows(outs[a].at[:, s_me]), send_sem(3 * a + k),
                                      recv_sem(3 * a + k), (*chip, c)))
                got = rows(outs[a].at[:, s_k])
                arrivals.append(functools.partial(_remote, got, got, send_sem(3 * a + k), recv_sem(3 * a + k),
                                                  (*chip, c)))
        return starts, arrivals

    def operands():
        return [pair_sums[k] for k in names] + ([by_source[k] for k in names] if part else [])

    return _Task(operands, lambda: [_sds(pair_sums[k].shape, pair_sums[k].dtype) for k in names],
                 {n + a: a for a in range(n)} if part else {}, 3 * n, make,
                 lambda res: by_source.update(zip(names, res)))


def _rs_chip_sum(name, own, parts, chip):
    nw, ns, H, C = parts.shape

    def body(chip_ref, own_ref, *rest):
        prefs, o_ref = rest[:ns], rest[ns]
        me = chip_ref[0]
        own_v = own_ref[...].astype(F32)
        tot = None
        for s in range(ns):
            term = jnp.where(me == s, own_v, prefs[s][...].astype(F32))
            tot = term if tot is None else tot + term
        o_ref[...] = tot

    blk = lambda fn: pl.BlockSpec((None, None, H, C), fn)

    def part_spec(s):
        return blk(lambda w, chip_ref: (w, jnp.where(chip_ref[0] == s, (s + 1) % ns, s), 0, 0))

    return _pcall(
        body, name=name, grid=(nw,), num_prefetch=1,
        in_specs=[blk(lambda w, chip_ref: (w, chip_ref[0], 0, 0))] + [part_spec(s) for s in range(ns)],
        out_specs=pl.BlockSpec((None, H, C), lambda w, chip_ref: (w, 0, 0)),
        out_shape=_sds((nw, H, C), F32),
    )(chip, own, *([parts] * ns))


def _pair_gather_task(names, halves, sibling_halves):
    n = len(names)

    def make(ins, outs, send_sem, recv_sem):
        x, y, c, _ = _mesh_position()
        copies = [_remote(ins[a], outs[a], send_sem(a), recv_sem(a), (x, y, 1 - c)) for a in range(n)]
        return copies, [functools.partial(lambda cp: cp, cp) for cp in copies]

    return _Task(lambda: [halves[k] for k in names], lambda: [_sds(halves[k].shape, F32) for k in names],
                 {}, n, make, lambda res: sibling_halves.update(zip(names, res)))


def _small_allreduce(v):
    R, C = v.shape
    Q = R // N_CHIPS
    nsem = 1 + 2 * N_PEER_CHIPS

    def body(v_ref, o_ref, sib_buf, pair_buf, part_buf, send_sems, recv_sems):
        x, y, c, chips = _mesh_position()
        s_me = _chip_index(x, y)

        def quarter(ref, s):
            return ref.at[pl.ds(pl.multiple_of(s * Q, 8), Q)]

        def exchange(first_sem, src, dst_of, arrival_of):
            sends = [_remote(src(_chip_index(*chip)), dst_of(s_me), send_sems.at[first_sem + k],
                             recv_sems.at[first_sem + k], (*chip, c)) for k, chip in enumerate(chips)]
            for cp in sends:
                cp.start()
            for k, chip in enumerate(chips):
                got = arrival_of(_chip_index(*chip))
                _remote(got, got, send_sems.at[first_sem + k], recv_sems.at[first_sem + k], (*chip, c)).wait_recv()
            for cp in sends:
                cp.wait_send()

        swap = _remote(v_ref, sib_buf, send_sems.at[0], recv_sems.at[0], (x, y, 1 - c))
        swap.start()
        swap.wait()
        pair_buf[...] = v_ref[...] + sib_buf[...]
        exchange(1, lambda s_k: quarter(pair_buf, s_k), lambda s: part_buf.at[s], lambda s_k: part_buf.at[s_k])
        part_buf[s_me] = quarter(pair_buf, s_me)[...]
        o_ref[pl.ds(pl.multiple_of(s_me * Q, 8), Q), :] = ((part_buf[0] + part_buf[1]) + part_buf[2]) + part_buf[3]
        exchange(1 + N_PEER_CHIPS, lambda s_k: quarter(o_ref, s_me), lambda s: quarter(o_ref, s),
                 lambda s_k: quarter(o_ref, s_k))

    return pl.pallas_call(
        body, name="small_allreduce",
        in_specs=[VMEM_SPEC], out_specs=VMEM_SPEC, out_shape=_sds((R, C), F32),
        scratch_shapes=[pltpu.VMEM((R, C), F32), pltpu.VMEM((R, C), F32), pltpu.VMEM((N_CHIPS, Q, C), F32),
                        pltpu.SemaphoreType.DMA((nsem,)), pltpu.SemaphoreType.DMA((nsem,))],
        compiler_params=pltpu.CompilerParams(vmem_limit_bytes=VMEM_LIMIT_BYTES),
    )(v)


TRANSPOSED_WEIGHTS = ("ffn1_w1", "ffn1_w3", "ffn2_w1", "ffn2_w3")
SMALL_LAYOUT = [("ffn1_norm", 1), ("mix_norm", 1), ("ret_gn", 1), ("conv_b", 1), ("b_rgate", 1), ("b_igate", 1),
                ("lru_lambda", 1), ("xattn_norm", 1), ("mem_norm", 1), ("ffn2_norm", 1), ("final_norm", 1),
                ("b_branch_gate", 2), ("conv_w", CONV_TAPS), ("w_rgate", LRU_BLOCK), ("w_igate", LRU_BLOCK)]
SMALL_ROWS = 288
WEIGHT_ORDER = ["ffn1_norm", "ffn1_w1", "ffn1_w3", "ffn1_w2", "mix_norm", "w_in", "ret_gn", "w_ret_o", "conv_w",
                "conv_b", "w_rgate", "b_rgate", "w_igate", "b_igate", "lru_lambda", "w_lru_o", "w_branch_gate",
                "b_branch_gate", "w_out", "xattn_norm", "mem_norm", "w_xq", "w_xk", "w_xv", "w_xo", "ffn2_norm",
                "ffn2_w1", "ffn2_w3", "ffn2_w2", "final_norm"]


def _pack_small(parts):
    rows = [parts[name].reshape(n, D) for name, n in SMALL_LAYOUT]
    used = sum(n for _, n in SMALL_LAYOUT)
    rows.append(jnp.zeros((SMALL_ROWS - used, D), F32))
    return jnp.concatenate(rows, axis=0)


def _unpack_small(packed, shapes):
    out, r = {}, 0
    for name, n in SMALL_LAYOUT:
        out[name] = packed[r:r + n].reshape(shapes[name])
        r += n
    return out


def kernel(x, mem, ffn1_norm, ffn1_w1, ffn1_w3, ffn1_w2, mix_norm, w_in, ret_gn, w_ret_o, conv_w, conv_b, w_rgate, b_rgate, w_igate, b_igate, lru_lambda, w_lru_o, w_branch_gate, b_branch_gate, w_out, xattn_norm, mem_norm, w_xq, w_xk, w_xv, w_xo, ffn2_norm, ffn2_w1, ffn2_w3, ffn2_w2, final_norm, loss_target, m_ffn1_norm, m_ffn1_w1, m_ffn1_w3, m_ffn1_w2, m_mix_norm, m_w_in, m_ret_gn, m_w_ret_o, m_conv_w, m_conv_b, m_w_rgate, m_b_rgate, m_w_igate, m_b_igate, m_lru_lambda, m_w_lru_o, m_w_branch_gate, m_b_branch_gate, m_w_out, m_xattn_norm, m_mem_norm, m_w_xq, m_w_xk, m_w_xv, m_w_xo, m_ffn2_norm, m_ffn2_w1, m_ffn2_w3, m_ffn2_w2, m_final_norm, v_ffn1_norm, v_ffn1_w1, v_ffn1_w3, v_ffn1_w2, v_mix_norm, v_w_in, v_ret_gn, v_w_ret_o, v_conv_w, v_conv_b, v_w_rgate, v_b_rgate, v_w_igate, v_b_igate, v_lru_lambda, v_w_lru_o, v_w_branch_gate, v_b_branch_gate, v_w_out, v_xattn_norm, v_mem_norm, v_w_xq, v_w_xk, v_w_xv, v_w_xo, v_ffn2_norm, v_ffn2_w1, v_ffn2_w3, v_ffn2_w2, v_final_norm):
    given = dict(locals())
    w = {n: given[n] for n in WEIGHT_ORDER}
    mom = {n: given["m_" + n] for n in WEIGHT_ORDER}
    var = {n: given["v_" + n] for n in WEIGHT_ORDER}
    chip = _chip_index(lax.axis_index("x"), lax.axis_index("y"))
    core = lax.axis_index("c").astype(jnp.int32).reshape(1)

    chip_id = chip.astype(jnp.int32).reshape(1)
    sm = {n: w[n] for n in ["ffn1_norm", "mix_norm", "ret_gn", "conv_b", "b_rgate", "b_igate", "lru_lambda",
                            "xattn_norm", "mem_norm", "ffn2_norm", "b_branch_gate"]}
    sm["final_norm"] = w["final_norm"].reshape(1, D)
    sm["w_rgate"] = w["w_rgate"][0]
    sm["w_igate"] = w["w_igate"][0]

    local = lambda a, n: jnp.swapaxes(a[0], 0, 1) if n in TRANSPOSED_WEIGHTS else a[0]
    stack = lambda names: jnp.stack([local(w[n], n) for n in names], axis=0).astype(BF16)
    shard = {"col1": stack(["ffn1_w1", "ffn1_w3"]), "row2a": stack(["ffn1_w2"]), "win": stack(["w_in"]),
             "wbg": stack(["w_branch_gate"]), "sqA": stack(["w_ret_o", "w_lru_o", "w_out"]),
             "sqB": stack(["w_xq", "w_xk"]), "sqC": stack(["w_xv", "w_xo"]), "col2": stack(["ffn2_w1", "ffn2_w3"]),
             "row2b": stack(["ffn2_w2"]), "conv": w["conv_w"]}
    gw, landed = {}, {}
    over_chips = lambda keys: _gather_chips_task({k: shard[k] for k in keys}, True, landed)
    to_sibling = lambda keys: _gather_sibling_task(keys, landed, gw)

    big, got, pair_sums, by_source, halves, sibling_halves, outs = {}, {}, {}, {}, {}, {}, {}
    pair_swap = lambda names: _pair_swap_task(names, big, got)
    exchange = lambda names, part=0, nparts=1: _chip_exchange_task(names, pair_sums, by_source, part, nparts)
    pair_gather = lambda names: _pair_gather_task(names, halves, sibling_halves)

    def pair_sum(names):
        for n in names:
            pair_sums[n] = _rs_pair_sum("rs_pair_sum_" + n, big[n], got[n], core)

    def chip_sum(names):
        for n in names:
            halves[n] = _rs_chip_sum("rs_chip_sum_" + n, pair_sums[n], by_source[n], chip_id)

    def adamw(names):
        for n in names:
            res = _adamw_halves("adamw_" + n, local(w[n], n), halves[n], sibling_halves[n], 0, local(mom[n], n),
                                local(var[n], n), core)
            outs[n] = tuple((jnp.swapaxes(r, 0, 1) if n in TRANSPOSED_WEIGHTS else r)[None] for r in res)

    do = lambda fn, names: functools.partial(fn, names)
    ffn2_grads = ["ffn2_w2", "ffn2_w1", "ffn2_w3"]
    xattn_grads = ["w_xo", "w_xq", "w_xk", "w_xv"]
    mix_out_grads = ["w_out", "w_ret_o", "w_lru_o"]
    mix_in_grads = ["w_in", "w_branch_gate"]
    plan = _Plan()
    plan.tasks = {
        "ag_first_chips": [over_chips(["col1", "row2a"])],
        "ag_first_sibling": [to_sibling(["col1", "row2a"])],
        "ffn1_up": [over_chips(["win"])],
        "ffn1_down": [to_sibling(["win"]), over_chips(["wbg"])],
        "mix_in": [to_sibling(["wbg"]), over_chips(["sqA"]), _gather_chips_task({"conv": shard["conv"]}, False, gw)],
        "mix_gates": [to_sibling(["sqA"]), over_chips(["row2b"])],
        "ret_fwd": [over_chips(["col2"]), to_sibling(["row2b"])],
        "lru_gates_fwd": [to_sibling(["col2"]), over_chips(["sqB"])],
        "lru_scan_fwd": [over_chips(["sqC"])],
        "y_lru": [to_sibling(["sqB", "sqC"])],
        "ffn2_dh": [pair_swap(ffn2_grads)],
        "dw_xo": [exchange(["ffn2_w2"], 0, 2)],
        "xattn_bwd": [exchange(["ffn2_w2"], 1, 2)],
        "dw_xq": [exchange(["ffn2_w1"], 0, 2)],
        "d_hq": [exchange(["ffn2_w1"], 1, 2)],
        "d_merged": [pair_swap(xattn_grads)],
        "dw_out": [exchange(["w_xo"])],
        "d_yr": [exchange(["w_xq"])],
        "dw_ret_o": [exchange(["w_xk"])],
        "dw_lru_o": [exchange(["w_xv"])],
        "ret_bwd": [exchange(["ffn2_w3"]), pair_swap(mix_out_grads), pair_gather(xattn_grads)],
        "lru_scan_bwd": [pair_gather(ffn2_grads)],
        "lru_gates_bwd": [exchange(mix_out_grads)],
        "dw_in": [pair_gather(mix_out_grads)],
        "d_h2": [pair_swap(mix_in_grads)],
        "ffn1_bwd_mid": [exchange(["w_in"], 0, 2), exchange(["w_branch_gate"], 0, 2)],
        "ffn1_dw2": [exchange(["w_in"], 1, 2)],
        "ffn1_dw1": [exchange(["w_branch_gate"], 1, 2), pair_swap(["ffn1_w2"])],
        "ffn1_dw3": [pair_gather(mix_in_grads), pair_swap(["ffn1_w1"]), exchange(["ffn1_w2"], 0, 2)],
        "ffn1_dh": [pair_swap(["ffn1_w3"]), exchange(["ffn1_w2"], 1, 2), exchange(["ffn1_w1"], 0, 2)],
        "rs_last": [exchange(["ffn1_w1"], 1, 2), exchange(["ffn1_w3"]), pair_gather(["ffn1_w2"])],
        "adamw_ffn1_w2": [pair_gather(["ffn1_w1", "ffn1_w3"])],
    }
    plan.after = {
        "ffn2_dh": [do(pair_sum, ffn2_grads)],
        "d_merged": [do(pair_sum, xattn_grads)],
        "dw_lru_o": [do(chip_sum, xattn_grads)],
        "ret_bwd": [do(chip_sum, ffn2_grads), do(pair_sum, mix_out_grads), do(adamw, xattn_grads)],
        "lru_scan_bwd": [do(adamw, ffn2_grads)],
        "lru_gates_bwd": [do(chip_sum, mix_out_grads)],
        "dw_in": [do(adamw, mix_out_grads)],
        "d_h2": [do(pair_sum, mix_in_grads)],
        "ffn1_dw1": [do(chip_sum, mix_in_grads), do(pair_sum, ["ffn1_w2"])],
        "ffn1_dw3": [do(pair_sum, ["ffn1_w1"]), do(adamw, mix_in_grads)],
        "ffn1_dh": [do(pair_sum, ["ffn1_w3"]), do(chip_sum, ["ffn1_w2"])],
        "rs_last": [do(chip_sum, ["ffn1_w1", "ffn1_w3"]), do(adamw, ["ffn1_w2", "ffn1_w1", "ffn1_w3"])],
    }
    global _plan
    _plan = plan
    try:
        _comm_call("ag_first_chips")
        _comm_call("ag_first_sibling")
        loss_part, grad_x, small = _local_step(x[0], mem[0], loss_target[0], gw, sm, big)
        _comm_call("rs_last")
    finally:
        _plan = None
    assert not plan.tasks and not plan.after, (list(plan.tasks), list(plan.after))
    loss = lax.psum(loss_part[0, 0], ("x", "y", "c"))

    small_shapes = {n: w[n].shape for n, _ in SMALL_LAYOUT}
    small_shapes["conv_w"] = (CONV_TAPS, D)
    small_sum = _small_allreduce(_pack_small(small))
    conv_grad = lax.dynamic_slice(small_sum[13:13 + CONV_TAPS], (0, chip * SQ_BLK), (CONV_TAPS, SQ_BLK))
    small_w = {n: w[n] for n, _ in SMALL_LAYOUT}
    small_m = {n: mom[n] for n, _ in SMALL_LAYOUT}
    small_v = {n: var[n] for n, _ in SMALL_LAYOUT}
    pad_cols = lambda a: jnp.pad(a[0], ((0, 0), (0, D - SQ_BLK)))
    for dct in (small_w, small_m, small_v):
        dct["conv_w"] = pad_cols(dct["conv_w"])
    g_pack = lax.dynamic_update_slice(small_sum, jnp.pad(conv_grad, ((0, 0), (0, D - SQ_BLK))), (13, 0))
    d_pack, m_pack, v_pack = _adamw("adamw_small", _pack_small(small_w), g_pack, _pack_small(small_m),
                                    _pack_small(small_v))
    unpacked = [_unpack_small(p, small_shapes) for p in (g_pack, d_pack, m_pack, v_pack)]
    for n, _ in SMALL_LAYOUT:
        if n == "conv_w":
            outs[n] = tuple(u[n][:, :SQ_BLK][None] for u in unpacked)
        else:
            outs[n] = tuple(u[n] for u in unpacked)

    result = [loss, grad_x[None]]
    for k in range(4):
        result += [outs[n][k] for n in WEIGHT_ORDER]
    return tuple(result)
```

```python
import functools
import math

import numpy as np
import jax
import jax.numpy as jnp
from jax import lax
from jax.experimental import pallas as pl
from jax.experimental.pallas import tpu as pltpu

F32 = jnp.float32
BF16 = jnp.bfloat16
MESH = pl.DeviceIdType.MESH

D = 1024
EPS = 1e-6
RET_HEADS = 4
RET_DK = 128
RET_DV = 256
CHUNK = 128
ROPE_BASE = 10000.0
LRU_BLOCKS = 8
LRU_BLOCK = 128
CONV_TAPS = 4
LRU_C = 8.0
D_FF = 2816
X_HEADS = 4
X_HD = 256
N_CHIPS = 4
FF_BLK = D_FF // N_CHIPS
IN_BLK = 5120 // N_CHIPS
BG_BLK = 2048 // N_CHIPS
SQ_BLK = D // N_CHIPS

ADAM_LR = 0.001
ADAM_B1 = 0.9
ADAM_B2 = 0.999
ADAM_EPS = 1e-08
ADAM_WD = 0.01
ADAM_STEP = 10

VMEM_LIMIT_BYTES = 56 * 1024 * 1024
ROW_TILE = 512
WIDE_ROW_TILE = 1024
SCAN_TILE = 256

_DN = {
    "nn": (((1,), (0,)), ((), ())),
    "nt": (((1,), (1,)), ((), ())),
    "tn": (((0,), (0,)), ((), ())),
}


def _cparams(n_axes):
    return pltpu.CompilerParams(dimension_semantics=("arbitrary",) * n_axes,
                                vmem_limit_bytes=VMEM_LIMIT_BYTES)


def _dot(a, b, kind):
    if b.ndim == 3:
        b = b.reshape(b.shape[0] * b.shape[1], b.shape[2])
    return lax.dot_general(a.astype(BF16), b.astype(BF16), _DN[kind], preferred_element_type=F32)


def _sigmoid(x):
    return 1.0 / (1.0 + jnp.exp(-x))


def _log1p_pos(e):
    u = 1.0 + e
    return jnp.where(u == 1.0, e, jnp.log(u) * (e / jnp.where(u == 1.0, 1.0, u - 1.0)))


def _expm1(x):
    u = jnp.exp(x)
    lu = jnp.log(u)
    safe = jnp.where(lu == 0.0, 1.0, lu)
    return jnp.where(u == 1.0, x, (u - 1.0) * (x / safe))


def _softplus(z):
    return jnp.maximum(z, 0.0) + _log1p_pos(jnp.exp(-jnp.abs(z)))


_GELU_C = math.sqrt(2.0 / math.pi)


def _gelu_and_grad(x):
    x2 = x * x
    t = jnp.tanh(_GELU_C * (x + 0.044715 * x * x2))
    g = 0.5 * x * (1.0 + t)
    dg = 0.5 * (1.0 + t) + 0.5 * x * (1.0 - t * t) * (_GELU_C * (1.0 + 3.0 * 0.044715 * x2))
    return g, dg


def _rms_fwd(x, g):
    r = lax.rsqrt(jnp.mean(x * x, axis=-1, keepdims=True) + EPS)
    return (x * r) * g


def _rms_bwd(x, g, dh):
    r = lax.rsqrt(jnp.mean(x * x, axis=-1, keepdims=True) + EPS)
    n = x * r
    dyg = dh * g
    dx = r * (dyg - n * jnp.mean(dyg * n, axis=-1, keepdims=True))
    return dx, jnp.sum(dh * n, axis=0, keepdims=True)


def _accumulate(ref, val, first):
    @pl.when(first)
    def _():
        ref[...] = val

    @pl.when(jnp.logical_not(first))
    def _():
        ref[...] += val


def _sds(shape, dtype):
    return jax.ShapeDtypeStruct(tuple(shape), dtype)


def _spec(shape, fn):
    return pl.BlockSpec(tuple(shape), fn)


class _Task:
    def __init__(self, operands, out_shapes, aliases, nsem, make, finish):
        self.operands, self.out_shapes, self.aliases = operands, out_shapes, aliases
        self.nsem, self.make, self.finish = nsem, make, finish


class _Plan:
    def __init__(self):
        self.tasks, self.after = {}, {}


_plan = None


def _pcall(body, *, name, grid, in_specs, out_specs, out_shape, scratch_shapes=(), num_prefetch=0):
    single = not isinstance(out_shape, (list, tuple))
    out_shape = [out_shape] if single else list(out_shape)
    out_specs = [out_specs] if single else list(out_specs)
    in_specs = list(in_specs)
    scratch_shapes = list(scratch_shapes)
    tasks = _plan.tasks.pop(name, []) if _plan is not None else []
    after = _plan.after.pop(name, []) if _plan is not None else []
    nax = len(grid)

    def run(*operands):
        n_in = len(operands) - num_prefetch
        n_out = len(out_shape)
        t_ops = [t.operands() for t in tasks]
        t_outs = [t.out_shapes() for t in tasks]
        c_ops = [a for ops in t_ops for a in ops]
        c_outs = [s for outs in t_outs for s in outs]
        aliases = {}
        i0, o0 = num_prefetch + n_in, n_out
        for t, ops, outs in zip(tasks, t_ops, t_outs):
            for i_loc, o_loc in t.aliases.items():
                aliases[i0 + i_loc] = o0 + o_loc
            i0 += len(ops)
            o0 += len(outs)
        nsem = sum(t.nsem for t in tasks)

        def wrapped(*refs):
            p = num_prefetch
            pre, ins = refs[:p], refs[p:p + n_in]
            cins = refs[p + n_in:p + n_in + len(c_ops)]
            q = p + n_in + len(c_ops)
            outs, couts = refs[q:q + n_out], refs[q + n_out:q + n_out + len(c_outs)]
            q += n_out + len(c_outs)
            scr = refs[q:q + len(scratch_shapes)]

            def descriptors():
                send_sems, recv_sems = refs[q + len(scratch_shapes):]
                starts, arrivals = [], []
                ci = co = so = 0
                for t, ops, souts in zip(tasks, t_ops, t_outs):
                    s, a = t.make(cins[ci:ci + len(ops)], couts[co:co + len(souts)],
                                  functools.partial(lambda base, k: send_sems.at[base + k], so),
                                  functools.partial(lambda base, k: recv_sems.at[base + k], so))
                    starts += s
                    arrivals += a
                    ci, co, so = ci + len(ops), co + len(souts), so + t.nsem
                return starts, arrivals

            if tasks:
                ids = [pl.program_id(k) for k in range(nax)]
                first = functools.reduce(jnp.logical_and, [i == 0 for i in ids])
                last = functools.reduce(jnp.logical_and, [i == g - 1 for i, g in zip(ids, grid)])

                @pl.when(first)
                def _():
                    for cp in descriptors()[0]:
                        cp.start()

            body(*pre, *ins, *outs, *scr)

            if tasks:
                @pl.when(last)
                def _():
                    starts, arrivals = descriptors()
                    for arrival in arrivals:
                        arrival().wait_recv()
                    for cp in starts:
                        cp.wait_send()

        sems = [pltpu.SemaphoreType.DMA((nsem,)), pltpu.SemaphoreType.DMA((nsem,))] if tasks else []
        res = pl.pallas_call(
            wrapped, name=name,
            grid_spec=pltpu.PrefetchScalarGridSpec(
                num_scalar_prefetch=num_prefetch, grid=tuple(grid),
                in_specs=in_specs + [ANY_SPEC] * len(c_ops),
                out_specs=out_specs + [ANY_SPEC] * len(c_outs),
                scratch_shapes=scratch_shapes + sems),
            out_shape=out_shape + c_outs,
            input_output_aliases=aliases,
            compiler_params=_cparams(nax),
        )(*operands, *c_ops)
        co = n_out
        for t, souts in zip(tasks, t_outs):
            t.finish(res[co:co + len(souts)])
            co += len(souts)
        for fn in after:
            fn()
        return res[0] if single else list(res[:n_out])

    return run


def _comm_call(name):
    def body(o_ref):
        o_ref[...] = jnp.zeros_like(o_ref)

    _pcall(body, name=name, grid=(1,), in_specs=[], out_specs=_spec((8, 128), lambda i: (0, 0)),
           out_shape=_sds((8, 128), F32))()


def _gemm(name, terms, grid, outs, acc_shape, extras=(), epilogue=None):
    kinds = [t[4] for t in terms]
    nt, ne, no = len(terms), len(extras), len(outs)
    nred = grid[-1]
    nax = len(grid)

    def body(*refs):
        trefs = refs[:2 * nt]
        erefs = refs[2 * nt:2 * nt + ne]
        orefs = refs[2 * nt + ne:2 * nt + ne + no]
        ids = [pl.program_id(k) for k in range(nax)]
        tot = None
        for t in range(nt):
            d = _dot(trefs[2 * t][...], trefs[2 * t + 1][...], kinds[t])
            tot = d if tot is None else tot + d

        def finish(acc):
            if epilogue is None:
                orefs[0][...] = acc.astype(orefs[0].dtype)
            else:
                epilogue(acc, erefs, orefs, ids)

        if nred == 1:
            finish(tot)
        else:
            acc_ref = refs[-1]
            r = ids[-1]

            @pl.when(r == 0)
            def _():
                acc_ref[...] = tot

            @pl.when(r > 0)
            def _():
                acc_ref[...] += tot

            @pl.when(r == nred - 1)
            def _():
                finish(acc_ref[...])

    operands, in_specs = [], []
    for a, a_spec, b, b_spec, _ in terms:
        operands += [a, b]
        in_specs += [a_spec, b_spec]
    for e, e_spec in extras:
        operands.append(e)
        in_specs.append(e_spec)
    scratch = [pltpu.VMEM(tuple(acc_shape), F32)] if nred > 1 else []
    return _pcall(body, name=name, grid=tuple(grid), in_specs=in_specs, out_specs=[o[1] for o in outs],
                  out_shape=[o[0] for o in outs], scratch_shapes=scratch)(*operands)


def _rowwise(name, fn, ins, outs, grid):
    ni = len(ins)
    nax = len(grid)

    def body(*refs):
        ids = [pl.program_id(k) for k in range(nax)]
        fn(refs[:ni], refs[ni:], ids)

    return _pcall(body, name=name, grid=tuple(grid), in_specs=[i[1] for i in ins],
                  out_specs=[o[1] for o in outs], out_shape=[o[0] for o in outs])(*[i[0] for i in ins])


def _ffn_up(name, h, wcol, w1_idx, w3_idx):
    T = h.shape[0]
    tm = min(WIDE_ROW_TILE, T)

    def body(h_ref, w1_ref, w3_ref, a_ref, b_ref, s_ref):
        hv = h_ref[...]
        a = _dot(hv, w1_ref[...], "nt")
        b = _dot(hv, w3_ref[...], "nt")
        a_ref[...] = a.astype(BF16)
        b_ref[...] = b.astype(BF16)
        s_ref[...] = ((a * _sigmoid(a)) * b).astype(BF16)

    blk = _spec((None, tm, FF_BLK), lambda j, i: (j, i, 0))
    return _pcall(
        body, name=name, grid=(N_CHIPS, T // tm),
        in_specs=[_spec((tm, D), lambda j, i: (i, 0)),
                  _spec((None, None, FF_BLK, D), lambda j, i: (j, w1_idx, 0, 0)),
                  _spec((None, None, FF_BLK, D), lambda j, i: (j, w3_idx, 0, 0))],
        out_specs=[blk, blk, blk],
        out_shape=[_sds((N_CHIPS, T, FF_BLK), BF16)] * 3,
    )(h, wcol, wcol)


def _ffn_down(name, s, wrow2, w2_idx, x_res, g_next=None):
    T = x_res.shape[0]
    tm = min(WIDE_ROW_TILE, T)
    row = lambda i, j, r: (i, 0)

    def epilogue(acc, erefs, orefs, ids):
        xo = erefs[0][...] + 0.5 * acc
        orefs[0][...] = xo
        if g_next is not None:
            orefs[1][...] = _rms_fwd(xo, erefs[1][...]).astype(BF16)

    extras = [(x_res, _spec((tm, D), row))]
    outs = [(_sds((T, D), F32), _spec((tm, D), row))]
    if g_next is not None:
        extras.append((g_next, _spec((1, D), lambda i, j, r: (0, 0))))
        outs.append((_sds((T, D), BF16), _spec((tm, D), row)))
    return _gemm(
        name,
        [(s, _spec((None, tm, FF_BLK), lambda i, j, r: (r, i, 0)),
          wrow2, _spec((None, None, FF_BLK, D), lambda i, j, r: (r, w2_idx, 0, 0)), "nn")],
        (T // tm, 1, N_CHIPS), outs, (tm, D), extras, epilogue)


def _ffn_bwd_mid(name, dx, wrow2, w2_idx, a, b):
    T = dx.shape[0]
    tm = min(WIDE_ROW_TILE, T)

    def body(dx_ref, w2_ref, a_ref, b_ref, dab_ref):
        ds = _dot(0.5 * dx_ref[...], w2_ref[...], "nt")
        av = a_ref[...].astype(F32)
        sg = _sigmoid(av)
        dab_ref[0] = (ds * b_ref[...].astype(F32) * (sg * (1.0 + av * (1.0 - sg)))).astype(BF16)
        dab_ref[1] = (ds * (av * sg)).astype(BF16)

    blk = _spec((None, tm, FF_BLK), lambda j, i: (j, i, 0))
    return _pcall(
        body, name=name, grid=(N_CHIPS, T // tm),
        in_specs=[_spec((tm, D), lambda j, i: (i, 0)),
                  _spec((None, None, FF_BLK, D), lambda j, i: (j, w2_idx, 0, 0)),
                  blk, blk],
        out_specs=_spec((2, None, tm, FF_BLK), lambda j, i: (0, j, i, 0)),
        out_shape=_sds((2, N_CHIPS, T, FF_BLK), BF16),
    )(dx, wrow2, a, b)


def _rms_bwd_epilogue(acc, erefs, orefs, ids):
    dx, dgp = _rms_bwd(erefs[0][...], erefs[1][...], acc)
    orefs[0][...] = dx + erefs[2][...]
    _accumulate(orefs[1], dgp, ids[0] == 0)


def _rms_bwd_io(x, g, dres, T, tm):
    row = lambda i, j, r: (i, 0)
    vec = lambda i, j, r: (0, 0)
    extras = [(x, _spec((tm, D), row)), (g, _spec((1, D), vec)), (dres, _spec((tm, D), row))]
    outs = [(_sds((T, D), F32), _spec((tm, D), row)), (_sds((1, D), F32), _spec((1, D), vec))]
    return extras, outs


def _ffn_bwd(tag, dx_out, h, a, b, s, wcol, w1_idx, w3_idx, wrow2, w2_idx, x_in, g, big):
    T = dx_out.shape[0]
    tk = T
    dab = _ffn_bwd_mid(tag + "_bwd_mid", dx_out, wrow2, w2_idx, a, b)

    def half_scale(acc, erefs, orefs, ids):
        orefs[0][...] = 0.5 * acc

    big[tag + "_w2"] = _gemm(
        tag + "_dw2",
        [(s, _spec((None, tk, FF_BLK), lambda j, n, r: (j, r, 0)),
          dx_out, _spec((tk, D), lambda j, n, r: (r, 0)), "tn")],
        (N_CHIPS, 1, T // tk),
        [(_sds((N_CHIPS, FF_BLK, D), F32), _spec((None, FF_BLK, D), lambda j, n, r: (j, 0, 0)))],
        (FF_BLK, D), (), half_scale)[0][None]
    for widx, wname in ((0, "_w1"), (1, "_w3")):
        big[tag + wname] = _gemm(
            tag + "_d" + wname[1:],
            [(dab, _spec((None, None, tk, FF_BLK), functools.partial(lambda w, j, n, r: (w, j, r, 0), widx)),
              h, _spec((tk, D), lambda j, n, r: (r, 0)), "tn")],
            (N_CHIPS, 1, T // tk),
            [(_sds((N_CHIPS, FF_BLK, D), F32), _spec((None, FF_BLK, D), lambda j, n, r: (j, 0, 0)))],
            (FF_BLK, D))[0][None]
    tw = min(WIDE_ROW_TILE, T)
    extras, outs = _rms_bwd_io(x_in, g, dx_out, T, tw)
    dx_in, dg = _gemm(
        tag + "_dh",
        [(dab, _spec((None, None, tw, FF_BLK), lambda i, j, r: (0, r, i, 0)),
          wcol, _spec((None, None, FF_BLK, D), lambda i, j, r: (r, w1_idx, 0, 0)), "nn"),
         (dab, _spec((None, None, tw, FF_BLK), lambda i, j, r: (1, r, i, 0)),
          wcol, _spec((None, None, FF_BLK, D), lambda i, j, r: (r, w3_idx, 0, 0)), "nn")],
        (T // tw, 1, N_CHIPS), outs, (tw, D), extras, _rms_bwd_epilogue)
    return dx_in, dg


def _proj_sq(name, a, wsq, idx, kind, out_dtype=F32, extras=(), epilogue=None, outs=None):
    M = a.shape[0]
    tm = min(ROW_TILE, M)
    if outs is None:
        outs = [(_sds((M, D), out_dtype), _spec((tm, D), lambda i, j, r: (i, 0)))]
    return _gemm(
        name,
        [(a, _spec((tm, D), lambda i, j, r: (i, 0)),
          wsq, _spec((N_CHIPS, None, SQ_BLK, D), lambda i, j, r: (0, idx, 0, 0)), kind)],
        (M // tm, 1, 1), outs, (tm, D), extras, epilogue)


def _dw_sq(name, a, b):
    M = a.shape[0]
    tk = M
    whole = _gemm(
        name,
        [(a, _spec((tk, D), lambda i, j, r: (r, 0)), b, _spec((tk, D), lambda i, j, r: (r, 0)), "tn")],
        (1, 1, M // tk),
        [(_sds((D, D), F32), _spec((D, D), lambda i, j, r: (0, 0)))],
        (D, D))[0]
    return whole.reshape(N_CHIPS, SQ_BLK, D)


def _retention_constants(T):
    pos = jnp.arange(T, dtype=F32)
    inv_freq = ROPE_BASE ** (-jnp.arange(0, RET_DK, 2, dtype=F32) / RET_DK)
    ang = pos[:, None] * inv_freq[None, :]
    cosf = jnp.concatenate([jnp.cos(ang), jnp.cos(ang)], axis=1)
    sins = jnp.concatenate([-jnp.sin(ang), jnp.sin(ang)], axis=1)
    lg = jnp.log(1.0 - 2.0 ** (-5.0 - jnp.arange(RET_HEADS, dtype=F32)))
    p = jnp.arange(CHUNK, dtype=F32)
    rel = p[:, None] - p[None, :]
    dmat = jnp.where(rel[None] >= 0, jnp.exp(rel[None] * lg[:, None, None]), 0.0)
    kd = jnp.exp((CHUNK - 1.0 - p)[None, :] * lg[:, None])[:, :, None]
    qd = jnp.exp((p + 1.0)[None, :] * lg[:, None])[:, :, None]
    cd = jnp.exp(CHUNK * lg)[:, None, None]
    return cosf, sins, dmat, kd, qd, cd


def _rot(t, cosv, sinv):
    return t * cosv + pltpu.roll(t, RET_DK // 2, 1) * sinv


def _unrot(t, cosv, sinv):
    return t * cosv - pltpu.roll(t, RET_DK // 2, 1) * sinv


def _ret_const_specs(cm):
    whole = lambda shape: _spec(shape, lambda c: (0,) * len(shape))
    return [
        _spec((CHUNK, RET_DK), lambda c: (cm(c), 0)),
        _spec((CHUNK, RET_DK), lambda c: (cm(c), 0)),
        whole((RET_HEADS, CHUNK, CHUNK)), whole((RET_HEADS, CHUNK, 1)), whole((RET_HEADS, CHUNK, 1)),
        whole((RET_HEADS, 1, 1)),
    ]


def _head(h, width):
    return slice(h * width, (h + 1) * width)


def _ret_fwd(u, consts, ret_gn):
    T = u.shape[0]
    nC = T // CHUNK
    kscale = RET_DK ** -0.5

    def body(q_ref, k_ref, v_ref, g_ref, cos_ref, sin_ref, dm_ref, kd_ref, qd_ref, cd_ref, gn_ref,
             qr_ref, kr_ref, ret_ref, yr_ref, st_ref, state):
        @pl.when(pl.program_id(0) == 0)
        def _():
            state[...] = jnp.zeros_like(state)

        cosv, sinv = cos_ref[...], sin_ref[...]
        for h in range(RET_HEADS):
            hk, hv = _head(h, RET_DK), _head(h, RET_DV)
            q = _rot(q_ref[:, hk], cosv, sinv)
            k = _rot(k_ref[:, hk], cosv, sinv) * kscale
            v = v_ref[:, hv]
            qr_ref[:, hk] = q
            kr_ref[:, hk] = k
            prev = state[h]
            st_ref[h] = prev
            s = _dot(q, k, "nt") * dm_ref[h]
            ret = _dot(s, v, "nn") + _dot(q, prev, "nn") * qd_ref[h]
            state[h] = cd_ref[h] * prev + _dot(k * kd_ref[h], v, "tn")
            ret_ref[:, hv] = ret
            mu = jnp.mean(ret, axis=-1, keepdims=True)
            xc = ret - mu
            yn = xc * lax.rsqrt(jnp.mean(xc * xc, axis=-1, keepdims=True) + EPS)
            g = g_ref[:, hv]
            yr_ref[:, hv] = ((g * _sigmoid(g)) * (yn * gn_ref[:, hv])).astype(BF16)

    cm = lambda c: c
    qk_w, v_w = RET_HEADS * RET_DK, RET_HEADS * RET_DV
    in_specs = [
        _spec((CHUNK, qk_w), lambda c: (c, 0)), _spec((CHUNK, qk_w), lambda c: (c, 1)),
        _spec((CHUNK, v_w), lambda c: (c, 1)), _spec((CHUNK, v_w), lambda c: (c, 2)),
    ] + _ret_const_specs(cm) + [_spec((1, v_w), lambda c: (0, 0))]
    qk_out = _spec((CHUNK, qk_w), lambda c: (c, 0))
    v_out = _spec((CHUNK, v_w), lambda c: (c, 0))
    return _pcall(
        body, name="ret_fwd", grid=(nC,),
        in_specs=in_specs,
        out_specs=[qk_out, qk_out, v_out, v_out,
                   _spec((RET_HEADS, None, RET_DK, RET_DV), lambda c: (0, c, 0, 0))],
        out_shape=[_sds((T, qk_w), F32), _sds((T, qk_w), F32), _sds((T, v_w), F32), _sds((T, v_w), BF16),
                   _sds((RET_HEADS, nC, RET_DK, RET_DV), F32)],
        scratch_shapes=[pltpu.VMEM((RET_HEADS, RET_DK, RET_DV), F32)],
    )(u, u, u, u, *consts, ret_gn)


def _ret_bwd(dyr, ret, u, qr, kr, states, consts, ret_gn):
    T = u.shape[0]
    nC = T // CHUNK
    kscale = RET_DK ** -0.5

    def body(dyr_ref, ret_ref, g_ref, q_ref, k_ref, v_ref, st_ref,
             cos_ref, sin_ref, dm_ref, kd_ref, qd_ref, cd_ref, gn_ref,
             dq_ref, dk_ref, dv_ref, dg_ref, dgn_ref, gstate):
        first = pl.program_id(0) == 0

        @pl.when(first)
        def _():
            gstate[...] = jnp.zeros_like(gstate)

        cosv, sinv = cos_ref[...], sin_ref[...]
        dgn_parts = []
        for h in range(RET_HEADS):
            hk, hv = _head(h, RET_DK), _head(h, RET_DV)
            ret = ret_ref[:, hv]
            mu = jnp.mean(ret, axis=-1, keepdims=True)
            xc = ret - mu
            rs = lax.rsqrt(jnp.mean(xc * xc, axis=-1, keepdims=True) + EPS)
            yn = xc * rs
            gn = gn_ref[:, hv]
            g = g_ref[:, hv]
            sg = _sigmoid(g)
            dyr_v = dyr_ref[:, hv]
            dretn = dyr_v * (g * sg)
            dg_ref[:, hv] = (dyr_v * (yn * gn) * (sg * (1.0 + g * (1.0 - sg)))).astype(BF16)
            dgn_parts.append(jnp.sum(dretn * yn, axis=0, keepdims=True))
            dyn = dretn * gn
            d_o = rs * (dyn - jnp.mean(dyn, axis=-1, keepdims=True)
                        - yn * jnp.mean(dyn * yn, axis=-1, keepdims=True))

            q, k, v = q_ref[:, hk], k_ref[:, hk], v_ref[:, hv]
            dmat, kd, qd = dm_ref[h], kd_ref[h], qd_ref[h]
            prev = st_ref[h]
            gnext = gstate[h]
            s = _dot(q, k, "nt") * dmat
            ds = _dot(d_o, v, "nt") * dmat
            doq = d_o * qd
            dq = _dot(ds, k, "nn") + _dot(doq, prev, "nt")
            dk = _dot(ds, q, "tn") + _dot(v, gnext, "nt") * kd
            dv = _dot(s, d_o, "tn") + _dot(k * kd, gnext, "nn")
            gstate[h] = cd_ref[h] * gnext + _dot(q, doq, "tn")
            dq_ref[:, hk] = _unrot(dq, cosv, sinv).astype(BF16)
            dk_ref[:, hk] = _unrot(dk * kscale, cosv, sinv).astype(BF16)
            dv_ref[:, hv] = dv.astype(BF16)
        _accumulate(dgn_ref, jnp.concatenate(dgn_parts, axis=1), first)

    cm = lambda c: nC - 1 - c
    qk_w, v_w = RET_HEADS * RET_DK, RET_HEADS * RET_DV
    vspec = lambda blk: _spec((CHUNK, v_w), lambda c: (cm(c), blk))
    qspec = _spec((CHUNK, qk_w), lambda c: (cm(c), 0))
    in_specs = [vspec(0), vspec(0), vspec(2), qspec, qspec, vspec(1),
                _spec((RET_HEADS, None, RET_DK, RET_DV), lambda c: (0, cm(c), 0, 0)),
                ] + _ret_const_specs(cm) + [_spec((1, v_w), lambda c: (0, 0))]
    return _pcall(
        body, name="ret_bwd", grid=(nC,),
        in_specs=in_specs,
        out_specs=[qspec, qspec, vspec(0), vspec(0), _spec((1, v_w), lambda c: (0, 0))],
        out_shape=[_sds((T, qk_w), BF16), _sds((T, qk_w), BF16), _sds((T, v_w), BF16), _sds((T, v_w), BF16),
                   _sds((1, v_w), F32)],
        scratch_shapes=[pltpu.VMEM((RET_HEADS, RET_DK, RET_DV), F32)],
    )(dyr, ret, u, qr, kr, u, states, *consts, ret_gn)


def _shift_down(x, s):
    rows = lax.broadcasted_iota(jnp.int32, x.shape, 0)
    return jnp.where(rows >= s, pltpu.roll(x, s, 0), 0.0)


def _shift_up(x, s):
    n = x.shape[0]
    rows = lax.broadcasted_iota(jnp.int32, x.shape, 0)
    return jnp.where(rows < n - s, pltpu.roll(x, n - s, 0), 0.0)


def _lru_specs(T):
    col = lambda off: _spec((T, LRU_BLOCK), lambda g: (0, off + g))
    vec = _spec((1, LRU_BLOCK), lambda g: (0, g))
    wblk = _spec((None, LRU_BLOCK, LRU_BLOCK), lambda g: (g, 0, 0))
    cw = _spec((CONV_TAPS, LRU_BLOCK), lambda g: (0, g))
    return col, vec, wblk, cw


def _lru_gates_fwd(u, conv_w, conv_b, w_r, b_r, w_i, b_i, lam):
    T = u.shape[0]
    col, vec, wblk, cw = _lru_specs(T)

    def body(x_ref, cw_ref, cb_ref, wr_ref, br_ref, wi_ref, bi_ref, lam_ref,
             xc_ref, r_ref, i_ref, a_ref, bx_ref):
        x = x_ref[...]
        w = cw_ref[...]
        xc = (_shift_down(x, 3) * w[0:1] + _shift_down(x, 2) * w[1:2] + _shift_down(x, 1) * w[2:3]
              + x * w[3:4] + cb_ref[...])
        r = _sigmoid(_dot(xc, wr_ref[...], "nn") + br_ref[...])
        i = _sigmoid(_dot(xc, wi_ref[...], "nn") + bi_ref[...])
        la = (-LRU_C) * r * _softplus(-lam_ref[...])
        xc_ref[...] = xc
        r_ref[...] = r
        i_ref[...] = i
        a_ref[...] = jnp.exp(la)
        bx_ref[...] = jnp.sqrt(-_expm1(2.0 * la)) * (i * xc)

    out = col(0)
    return _pcall(
        body, name="lru_gates_fwd", grid=(LRU_BLOCKS,),
        in_specs=[col(24), cw, vec, wblk, vec, wblk, vec, vec],
        out_specs=[out] * 5,
        out_shape=[_sds((T, D), F32)] * 5,
    )(u, conv_w, conv_b, w_r, b_r, w_i, b_i, lam)


def _lru_scan(name, a3, b3, reverse):
    T = a3.shape[0]
    nt = T // SCAN_TILE
    unroll = 8

    def body(a_ref, b_ref, o_ref, carry):
        @pl.when(pl.program_id(0) == 0)
        def _():
            carry[...] = jnp.zeros_like(carry)

        if not reverse:
            def step(t, h):
                h = a_ref[t] * h + b_ref[t]
                o_ref[t] = h
                return h
        else:
            def step(k, c):
                t = SCAN_TILE - 1 - k
                l = b_ref[t] + c
                o_ref[t] = l
                return a_ref[t] * l
        carry[...] = lax.fori_loop(0, SCAN_TILE, step, carry[...], unroll=unroll)

    idx = (lambda i: (nt - 1 - i, 0, 0)) if reverse else (lambda i: (i, 0, 0))
    blk = _spec((SCAN_TILE, LRU_BLOCKS, LRU_BLOCK), idx)
    return _pcall(
        body, name=name, grid=(nt,),
        in_specs=[blk, blk], out_specs=blk,
        out_shape=_sds((T, LRU_BLOCKS, LRU_BLOCK), F32),
        scratch_shapes=[pltpu.VMEM((LRU_BLOCKS, LRU_BLOCK), F32)],
    )(a3, b3)


def _lru_gates_bwd(lmb, hl, a, r, i, xc, u, conv_w, w_r, w_i, lam):
    T = u.shape[0]
    col, vec, wblk, cw = _lru_specs(T)

    def body(l_ref, h_ref, a_ref, r_ref, i_ref, xc_ref, x_ref, cw_ref, wr_ref, wi_ref, lam_ref,
             dx_ref, dwr_ref, dwi_ref, dvec_ref, dcw_ref):
        l = l_ref[...]
        av, rv, iv, xc = a_ref[...], r_ref[...], i_ref[...], xc_ref[...]
        lam_v = lam_ref[...]
        sp = _softplus(-lam_v)
        la = (-LRU_C) * rv * sp
        mult = jnp.sqrt(-_expm1(2.0 * la))
        da = l * _shift_down(h_ref[...], 1)
        dmult = l * (iv * xc)
        di = l * mult * xc
        dxc = l * mult * iv
        dla = da * av - dmult * (av * av) / mult
        dzr = (dla * ((-LRU_C) * sp)) * rv * (1.0 - rv)
        dzi = di * iv * (1.0 - iv)
        dsp = jnp.sum(dla * ((-LRU_C) * rv), axis=0, keepdims=True)
        dlam = dsp * (-_sigmoid(-lam_v))
        dwr_ref[...] = _dot(xc, dzr, "tn")
        dwi_ref[...] = _dot(xc, dzi, "tn")
        dxc = dxc + _dot(dzr, wr_ref[...], "nt") + _dot(dzi, wi_ref[...], "nt")
        x = x_ref[...]
        w = cw_ref[...]
        dx = (dxc * w[3:4] + _shift_up(dxc, 1) * w[2:3] + _shift_up(dxc, 2) * w[1:2]
              + _shift_up(dxc, 3) * w[0:1])
        dx_ref[...] = dx.astype(BF16)
        dvec_ref[...] = jnp.concatenate(
            [jnp.sum(dzr, axis=0, keepdims=True), jnp.sum(dzi, axis=0, keepdims=True), dlam,
             jnp.sum(dxc, axis=0, keepdims=True)], axis=0)
        dcw_ref[...] = jnp.concatenate(
            [jnp.sum(dxc * _shift_down(x, 3 - tap), axis=0, keepdims=True) if tap < 3
             else jnp.sum(dxc * x, axis=0, keepdims=True) for tap in range(CONV_TAPS)], axis=0)

    c0 = col(0)
    return _pcall(
        body, name="lru_gates_bwd", grid=(LRU_BLOCKS,),
        in_specs=[c0, c0, c0, c0, c0, c0, col(24), cw, wblk, wblk, vec],
        out_specs=[c0, wblk, wblk, cw, cw],
        out_shape=[_sds((T, D), BF16), _sds((LRU_BLOCKS, LRU_BLOCK, LRU_BLOCK), F32),
                   _sds((LRU_BLOCKS, LRU_BLOCK, LRU_BLOCK), F32), _sds((4, D), F32), _sds((CONV_TAPS, D), F32)],
    )(lmb, hl, a, r, i, xc, u, conv_w, w_r, w_i, lam)


def _xattn_probs(q, k):
    sc = _dot(q, k, "nt") * (X_HD ** -0.5)
    e = jnp.exp(sc - jnp.max(sc, axis=-1, keepdims=True))
    return e / jnp.sum(e, axis=-1, keepdims=True)


def _xattn_fwd(xq, xk, xv):
    T = xq.shape[0]
    tq = ROW_TILE
    M = xk.shape[0]

    def body(q_ref, k_ref, v_ref, o_ref):
        p = _xattn_probs(q_ref[...], k_ref[...])
        o_ref[...] = _dot(p, v_ref[...], "nn").astype(BF16)

    qs = _spec((tq, X_HD), lambda h, i: (i, h))
    kv = _spec((M, X_HD), lambda h, i: (0, h))
    return _pcall(
        body, name="xattn_fwd", grid=(X_HEADS, T // tq),
        in_specs=[qs, kv, kv], out_specs=qs, out_shape=_sds((T, D), BF16),
    )(xq, xk, xv)


def _xattn_bwd(xq, xk, xv, dxo):
    T = xq.shape[0]
    tq = ROW_TILE
    M = xk.shape[0]

    def body(q_ref, k_ref, v_ref, do_ref, dq_ref, dk_ref, dv_ref):
        first = pl.program_id(1) == 0
        q, k, v, do = q_ref[...], k_ref[...], v_ref[...], do_ref[...]
        p = _xattn_probs(q, k)
        dp = _dot(do, v, "nt")
        ds = p * (dp - jnp.sum(dp * p, axis=-1, keepdims=True)) * (X_HD ** -0.5)
        dq_ref[...] = _dot(ds, k, "nn").astype(BF16)
        _accumulate(dk_ref, _dot(ds, q, "tn"), first)
        _accumulate(dv_ref, _dot(p, do, "tn"), first)

    qs = _spec((tq, X_HD), lambda h, i: (i, h))
    kv = _spec((M, X_HD), lambda h, i: (0, h))
    return _pcall(
        body, name="xattn_bwd", grid=(X_HEADS, T // tq),
        in_specs=[qs, kv, kv, qs], out_specs=[qs, kv, kv],
        out_shape=[_sds((T, D), BF16), _sds((M, D), F32), _sds((M, D), F32)],
    )(xq, xk, xv, dxo)


def _final_loss(x, g, tgt):
    T = x.shape[0]
    tm = ROW_TILE

    def fn(irefs, orefs, ids):
        xv, gv = irefs[0][...], irefs[1][...]
        err = _rms_fwd(xv, gv) - irefs[2][...]
        lp = 0.5 * jnp.sum(jnp.mean(err * err, axis=-1, keepdims=True), axis=0, keepdims=True)
        first = ids[0] == 0
        _accumulate(orefs[0], jnp.broadcast_to(lp, (1, 128)), first)
        dx, dgp = _rms_bwd(xv, gv, err * (1.0 / D))
        orefs[1][...] = dx
        _accumulate(orefs[2], dgp, first)

    row = _spec((tm, D), lambda i: (i, 0))
    vec = _spec((1, D), lambda i: (0, 0))
    return _rowwise(
        "final_loss", fn, [(x, row), (g, vec), (tgt, row)],
        [(_sds((1, 128), F32), _spec((1, 128), lambda i: (0, 0))), (_sds((T, D), F32), row),
         (_sds((1, D), F32), vec)],
        (T // tm,))


def _adamw(name, w, g, m, v):
    R, C = w.shape
    tr = R
    for cand in (512, 352, 256):
        if R % cand == 0:
            tr = cand
            break

    def fn(irefs, orefs, ids):
        delta, mn, vn = _adamw_update(*(r[...] for r in irefs))
        orefs[0][...] = delta
        orefs[1][...] = mn
        orefs[2][...] = vn

    blk = _spec((tr, C), lambda i: (i, 0))
    return _rowwise(name, fn, [(w, blk), (g, blk), (m, blk), (v, blk)],
                    [(_sds((R, C), F32), blk)] * 3, (R // tr,))


def _adamw_update(wv, gv, mv, vv):
    c1 = 1.0 - ADAM_B1 ** ADAM_STEP
    c2 = 1.0 - ADAM_B2 ** ADAM_STEP
    mn = ADAM_B1 * mv + (1.0 - ADAM_B1) * gv
    vn = ADAM_B2 * vv + (1.0 - ADAM_B2) * (gv * gv)
    delta = -ADAM_LR * ((mn / c1) / (jnp.sqrt(vn / c2) + ADAM_EPS) + ADAM_WD * wv)
    return delta, mn, vn


def _adamw_halves(name, w, mine, theirs, widx, m, v, core):
    R, C = w.shape
    H = R // 2
    tr = H
    while tr * C * 4 > (1 << 20) and tr % 16 == 0:
        tr //= 2
    nb = H // tr

    def body(core_ref, w_ref, mine_ref, theirs_ref, m_ref, v_ref, g_out, d_out, m_out, v_out):
        gv = jnp.where(pl.program_id(0) == core_ref[0], mine_ref[...], theirs_ref[...])
        delta, mn, vn = _adamw_update(w_ref[...], gv, m_ref[...], v_ref[...])
        g_out[...] = gv
        d_out[...] = delta
        m_out[...] = mn
        v_out[...] = vn

    full = pl.BlockSpec((tr, C), lambda h, i, core_ref: (h * nb + i, 0))
    mine_spec = pl.BlockSpec((None, tr, C), lambda h, i, core_ref: (widx, jnp.where(h == core_ref[0], i, 0), 0))
    theirs_spec = pl.BlockSpec((None, tr, C), lambda h, i, core_ref: (widx, jnp.where(h == core_ref[0], 0, i), 0))
    return _pcall(
        body, name=name, grid=(2, nb), num_prefetch=1,
        in_specs=[full, mine_spec, theirs_spec, full, full], out_specs=[full] * 4,
        out_shape=[_sds((R, C), F32)] * 4,
    )(core, w, mine, theirs, m, v)


def _rmsnorm(name, x, g):
    M = x.shape[0]
    tm = min(ROW_TILE, M)

    def fn(irefs, orefs, ids):
        orefs[0][...] = _rms_fwd(irefs[0][...], irefs[1][...]).astype(BF16)

    row = _spec((tm, D), lambda i: (i, 0))
    return _rowwise(name, fn, [(x, row), (g, _spec((1, D), lambda i: (0, 0)))],
                    [(_sds((M, D), BF16), row)], (M // tm,))[0]


WEIGHT_AT = {
    "ffn1_w1": ("col1", 0), "ffn1_w3": ("col1", 1), "ffn1_w2": ("row2a", 0),
    "w_ret_o": ("sqA", 0), "w_lru_o": ("sqA", 1), "w_out": ("sqA", 2),
    "w_xq": ("sqB", 0), "w_xk": ("sqB", 1), "w_xv": ("sqC", 0), "w_xo": ("sqC", 1),
    "ffn2_w1": ("col2", 0), "ffn2_w3": ("col2", 1), "ffn2_w2": ("row2b", 0),
}


def _local_step(x, mem, tgt, gw, sm, big):
    T = x.shape[0]
    tm = ROW_TILE

    def wt(name):
        key, idx = WEIGHT_AT[name]
        return gw[key], idx

    row3 = lambda i, j, r: (i, 0)
    vec3 = lambda i, j, r: (0, 0)
    rowD = _spec((tm, D), row3)
    vecD = _spec((1, D), vec3)

    def residual_norm(acc, erefs, orefs, ids):
        xo = erefs[0][...] + acc
        orefs[0][...] = xo
        orefs[1][...] = _rms_fwd(xo, erefs[1][...]).astype(BF16)

    def res_norm_io(x_res, g):
        return ([(x_res, rowD), (g, vecD)],
                [(_sds((T, D), F32), rowD), (_sds((T, D), BF16), rowD)])

    h1 = _rmsnorm("ffn1_norm", x, sm["ffn1_norm"])
    a1, b1, s1 = _ffn_up("ffn1_up", h1, *wt("ffn1_w1"), wt("ffn1_w3")[1])
    x1, h2 = _ffn_down("ffn1_down", s1, *wt("ffn1_w2"), x, sm["mix_norm"])

    tw = min(WIDE_ROW_TILE, T)
    wideD = _spec((tw, D), row3)
    u = _gemm(
        "mix_in",
        [(h2, wideD, gw["win"], _spec((None, None, D, IN_BLK), lambda i, j, r: (j, 0, 0, 0)), "nn")],
        (T // tw, N_CHIPS, 1),
        [(_sds((T, 5120), F32), _spec((tw, IN_BLK), lambda i, j, r: (i, j)))], (tw, IN_BLK))[0]

    def gate_epilogue(acc, erefs, orefs, ids):
        orefs[0][...] = _sigmoid(acc + erefs[0][...])

    gates = _gemm(
        "mix_gates",
        [(h2, wideD, gw["wbg"], _spec((None, None, D, BG_BLK), lambda i, j, r: (j, 0, 0, 0)), "nn")],
        (T // tw, N_CHIPS, 1),
        [(_sds((T, 2 * D), F32), _spec((tw, BG_BLK), lambda i, j, r: (i, j)))], (tw, BG_BLK),
        [(sm["b_branch_gate"], _spec((1, BG_BLK), lambda i, j, r: (0, j)))], gate_epilogue)[0]

    consts = _retention_constants(T)
    qr, kr, ret, yr, states = _ret_fwd(u, consts, sm["ret_gn"])

    conv_w = gw["conv"][:, 0].transpose(1, 0, 2).reshape(CONV_TAPS, D)
    xc, rg, ig, av, bx = _lru_gates_fwd(u, conv_w, sm["conv_b"], sm["w_rgate"], sm["b_rgate"],
                                        sm["w_igate"], sm["b_igate"], sm["lru_lambda"])
    a3 = av.reshape(T, LRU_BLOCKS, LRU_BLOCK)
    hl = _lru_scan("lru_scan_fwd", a3, bx.reshape(T, LRU_BLOCKS, LRU_BLOCK), False).reshape(T, D)

    row1 = _spec((tm, D), lambda i: (i, 0))
    glru1 = _spec((tm, D), lambda i: (i, 4))

    def lru_out(irefs, orefs, ids):
        gl, _ = _gelu_and_grad(irefs[1][...])
        orefs[0][...] = (irefs[0][...] * gl).astype(BF16)

    yl = _rowwise("lru_out", lru_out, [(hl, row1), (u, glru1)], [(_sds((T, D), BF16), row1)], (T // tm,))[0]

    y_ret = _proj_sq("y_ret", yr, *wt("w_ret_o"), "nn")[0]

    def merge_epilogue(acc, erefs, orefs, ids):
        orefs[0][...] = acc
        orefs[1][...] = (erefs[0][...] * erefs[2][...] + erefs[1][...] * acc).astype(BF16)

    y_lru, merged = _proj_sq(
        "y_lru", yl, *wt("w_lru_o"), "nn",
        extras=[(gates, _spec((tm, D), lambda i, j, r: (i, 0))), (gates, _spec((tm, D), lambda i, j, r: (i, 1))),
                (y_ret, rowD)],
        epilogue=merge_epilogue,
        outs=[(_sds((T, D), F32), rowD), (_sds((T, D), BF16), rowD)])

    ex, ou = res_norm_io(x1, sm["xattn_norm"])
    x2, hq = _proj_sq("mix_out", merged, *wt("w_out"), "nn", extras=ex, epilogue=residual_norm, outs=ou)

    m = _rmsnorm("mem_norm", mem, sm["mem_norm"])
    xq = _proj_sq("xq", hq, *wt("w_xq"), "nn", BF16)[0]
    xk = _proj_sq("xk", m, *wt("w_xk"), "nn", BF16)[0]
    xv = _proj_sq("xv", m, *wt("w_xv"), "nn", BF16)[0]
    xo = _xattn_fwd(xq, xk, xv)
    ex, ou = res_norm_io(x2, sm["ffn2_norm"])
    x3, h3 = _proj_sq("xattn_out", xo, *wt("w_xo"), "nn", extras=ex, epilogue=residual_norm, outs=ou)

    a2, b2, s2 = _ffn_up("ffn2_up", h3, *wt("ffn2_w1"), wt("ffn2_w3")[1])
    x4 = _ffn_down("ffn2_down", s2, *wt("ffn2_w2"), x3)[0]
    loss, dx4, dg_final = _final_loss(x4, sm["final_norm"], tgt)

    dx3, dg_ffn2 = _ffn_bwd("ffn2", dx4, h3, a2, b2, s2, *wt("ffn2_w1"), wt("ffn2_w3")[1],
                            *wt("ffn2_w2"), x3, sm["ffn2_norm"], big)

    dxo = _proj_sq("d_xo", dx3, *wt("w_xo"), "nt", BF16)[0]
    big["w_xo"] = _dw_sq("dw_xo", xo, dx3)[None]
    dxq, dxk, dxv = _xattn_bwd(xq, xk, xv, dxo)
    big["w_xq"] = _dw_sq("dw_xq", hq, dxq)[None]
    ex, ou = _rms_bwd_io(x2, sm["xattn_norm"], dx3, T, tm)
    dx2, dg_xattn = _proj_sq("d_hq", dxq, *wt("w_xq"), "nt", extras=ex, epilogue=_rms_bwd_epilogue, outs=ou)
    big["w_xk"] = _dw_sq("dw_xk", m, dxk)[None]
    big["w_xv"] = _dw_sq("dw_xv", m, dxv)[None]

    M = mem.shape[0]

    def mem_norm_epilogue(acc, erefs, orefs, ids):
        _, dgp = _rms_bwd(erefs[0][...], erefs[1][...], acc)
        orefs[0][...] = dgp

    wsq_spec = lambda idx: _spec((N_CHIPS, None, SQ_BLK, D), lambda i, j, r: (0, idx, 0, 0))
    memD = _spec((M, D), row3)
    dg_mem = _gemm(
        "d_mem_norm",
        [(dxk, memD, wt("w_xk")[0], wsq_spec(wt("w_xk")[1]), "nt"),
         (dxv, memD, wt("w_xv")[0], wsq_spec(wt("w_xv")[1]), "nt")],
        (1, 1, 1), [(_sds((1, D), F32), vecD)], (M, D),
        [(mem, memD), (sm["mem_norm"], vecD)], mem_norm_epilogue)[0]

    def merged_bwd_epilogue(acc, erefs, orefs, ids):
        gr, gl, yrv, ylv = (e[...] for e in erefs)
        orefs[0][...] = (acc * gr).astype(BF16)
        orefs[1][...] = (acc * gl).astype(BF16)
        dgr = acc * yrv * gr * (1.0 - gr)
        dgl = acc * ylv * gl * (1.0 - gl)
        orefs[2][:, :D] = dgr.astype(BF16)
        orefs[2][:, D:] = dgl.astype(BF16)
        dbb = jnp.concatenate([jnp.sum(dgr, axis=0, keepdims=True), jnp.sum(dgl, axis=0, keepdims=True)], axis=1)
        _accumulate(orefs[3], dbb, ids[0] == 0)

    dy_ret, dy_lru, dgpre, db_bg = _proj_sq(
        "d_merged", dx2, *wt("w_out"), "nt",
        extras=[(gates, _spec((tm, D), lambda i, j, r: (i, 0))), (gates, _spec((tm, D), lambda i, j, r: (i, 1))),
                (y_ret, rowD), (y_lru, rowD)],
        epilogue=merged_bwd_epilogue,
        outs=[(_sds((T, D), BF16), rowD), (_sds((T, D), BF16), rowD),
              (_sds((T, 2 * D), BF16), _spec((tm, 2 * D), row3)),
              (_sds((1, 2 * D), F32), _spec((1, 2 * D), vec3))])
    big["w_out"] = _dw_sq("dw_out", merged, dx2)[None]
    dyr = _proj_sq("d_yr", dy_ret, *wt("w_ret_o"), "nt")[0]
    big["w_ret_o"] = _dw_sq("dw_ret_o", yr, dy_ret)[None]
    dyl = _proj_sq("d_yl", dy_lru, *wt("w_lru_o"), "nt")[0]
    big["w_lru_o"] = _dw_sq("dw_lru_o", yl, dy_lru)[None]

    dq, dk, dv, dgr, dg_retgn = _ret_bwd(dyr, ret, u, qr, kr, states, consts, sm["ret_gn"])

    def lru_out_bwd(irefs, orefs, ids):
        gl, dgl = _gelu_and_grad(irefs[2][...])
        dyl_v = irefs[0][...]
        orefs[0][...] = dyl_v * gl
        orefs[1][...] = (dyl_v * irefs[1][...] * dgl).astype(BF16)

    dhl, dglru = _rowwise("lru_out_bwd", lru_out_bwd, [(dyl, row1), (hl, row1), (u, glru1)],
                          [(_sds((T, D), F32), row1), (_sds((T, D), BF16), row1)], (T // tm,))
    lmb = _lru_scan("lru_scan_bwd", a3, dhl.reshape(T, LRU_BLOCKS, LRU_BLOCK), True).reshape(T, D)
    dxl, dw_r, dw_i, dvec, dcw = _lru_gates_bwd(lmb, hl, av, rg, ig, xc, u, conv_w,
                                                sm["w_rgate"], sm["w_igate"], sm["lru_lambda"])

    du = jnp.concatenate([dq, dk, dv, dgr, dxl, dglru], axis=1)
    tk = T
    big["w_in"] = _gemm(
        "dw_in",
        [(h2, _spec((tk, D), lambda j, n, r: (r, 0)), du, _spec((tk, IN_BLK), lambda j, n, r: (r, j)), "tn")],
        (N_CHIPS, 1, T // tk),
        [(_sds((N_CHIPS, D, IN_BLK), F32), _spec((None, D, IN_BLK), lambda j, n, r: (j, 0, 0)))],
        (D, IN_BLK))[0][None]
    big["w_branch_gate"] = _gemm(
        "dw_bg",
        [(h2, _spec((tk, D), lambda j, n, r: (r, 0)), dgpre, _spec((tk, BG_BLK), lambda j, n, r: (r, j)), "tn")],
        (N_CHIPS, 1, T // tk),
        [(_sds((N_CHIPS, D, BG_BLK), F32), _spec((None, D, BG_BLK), lambda j, n, r: (j, 0, 0)))],
        (D, BG_BLK))[0][None]
    ex, ou = _rms_bwd_io(x1, sm["mix_norm"], dx2, T, tw)
    dx1, dg_mix = _gemm(
        "d_h2",
        [(du, _spec((tw, IN_BLK), lambda i, j, r: (i, r)),
          gw["win"], _spec((None, None, D, IN_BLK), lambda i, j, r: (r, 0, 0, 0)), "nt"),
         (dgpre, _spec((tw, BG_BLK), lambda i, j, r: (i, r)),
          gw["wbg"], _spec((None, None, D, BG_BLK), lambda i, j, r: (r, 0, 0, 0)), "nt")],
        (T // tw, 1, N_CHIPS), ou, (tw, D), ex, _rms_bwd_epilogue)

    grad_x, dg_ffn1 = _ffn_bwd("ffn1", dx1, h1, a1, b1, s1, *wt("ffn1_w1"), wt("ffn1_w3")[1],
                               *wt("ffn1_w2"), x, sm["ffn1_norm"], big)

    small = {
        "ffn1_norm": dg_ffn1, "mix_norm": dg_mix, "ret_gn": dg_retgn, "conv_b": dvec[3:4],
        "b_rgate": dvec[0:1], "b_igate": dvec[1:2], "lru_lambda": dvec[2:3], "xattn_norm": dg_xattn,
        "mem_norm": dg_mem, "ffn2_norm": dg_ffn2, "final_norm": dg_final, "b_branch_gate": db_bg,
        "conv_w": dcw, "w_rgate": dw_r, "w_igate": dw_i,
    }
    return loss, grad_x, small


ANY_SPEC = pl.BlockSpec(memory_space=pl.ANY)
VMEM_SPEC = pl.BlockSpec(memory_space=pltpu.VMEM)
N_PEER_CHIPS = N_CHIPS - 1


def _mesh_position():
    x, y, c = lax.axis_index("x"), lax.axis_index("y"), lax.axis_index("c")
    chips = [(1 - x, y), (x, 1 - y), (1 - x, 1 - y)]
    return x, y, c, chips


def _chip_index(x, y):
    return 2 * x + y


def _rows_half(ref, axis, h):
    n = ref.shape[axis] // 2
    idx = [slice(None)] * len(ref.shape)
    idx[axis] = pl.ds(pl.multiple_of(h * n, 16), n)
    return ref.at[tuple(idx)]


def _remote(src, dst, send_sem, recv_sem, device):
    return pltpu.make_async_remote_copy(src_ref=src, dst_ref=dst, send_sem=send_sem, recv_sem=recv_sem,
                                        device_id=device, device_id_type=MESH)


def _gather_chips_task(shards, split, landed):
    keys = list(shards)
    n = len(keys)

    def operands():
        chip_me = _chip_index(lax.axis_index("x"), lax.axis_index("y"))
        bases = [lax.dynamic_update_slice(lax.empty((N_CHIPS,) + shards[k].shape, shards[k].dtype), shards[k][None],
                                          (chip_me,) + (0,) * shards[k].ndim) for k in keys]
        return [shards[k] for k in keys] + bases

    def make(ins, outs, send_sem, recv_sem):
        x, y, c, chips = _mesh_position()
        s_me = _chip_index(x, y)
        starts, arrivals = [], []
        for g in range(n):
            mine = _rows_half(ins[g], 1, c) if split else ins[g]
            for k, chip in enumerate(chips):
                def landing(s):
                    o = outs[g].at[s]
                    return _rows_half(o, 1, c) if split else o
                starts.append(_remote(mine, landing(s_me), send_sem(3 * g + k), recv_sem(3 * g + k), (*chip, c)))
                got = landing(_chip_index(*chip))
                arrivals.append(functools.partial(_remote, got, got, send_sem(3 * g + k), recv_sem(3 * g + k),
                                                  (*chip, c)))
        return starts, arrivals

    def finish(res):
        landed.update(zip(keys, res))

    return _Task(operands, lambda: [_sds((N_CHIPS,) + shards[k].shape, shards[k].dtype) for k in keys],
                 {n + g: g for g in range(n)}, 3 * n, make, finish)


def _gather_sibling_task(keys, landed, ready):
    n = len(keys)

    def make(ins, outs, send_sem, recv_sem):
        x, y, c, chips = _mesh_position()
        starts, arrivals = [], []
        for g in range(n):
            for k, chip in enumerate(chips):
                o = outs[g].at[_chip_index(*chip)]
                got, other = _rows_half(o, 1, c), _rows_half(o, 1, 1 - c)
                starts.append(_remote(got, got, send_sem(3 * g + k), recv_sem(3 * g + k), (x, y, 1 - c)))
                arrivals.append(functools.partial(_remote, other, other, send_sem(3 * g + k), recv_sem(3 * g + k),
                                                  (x, y, 1 - c)))
        return starts, arrivals

    def finish(res):
        ready.update(zip(keys, res))

    return _Task(lambda: [landed[k] for k in keys], lambda: [_sds(landed[k].shape, landed[k].dtype) for k in keys],
                 {g: g for g in range(n)}, 3 * n, make, finish)


def _pair_swap_task(names, big, got):
    n = len(names)

    def make(ins, outs, send_sem, recv_sem):
        x, y, c, _ = _mesh_position()
        copies = [_remote(_rows_half(ins[a], 2, 1 - c), outs[a], send_sem(a), recv_sem(a), (x, y, 1 - c))
                  for a in range(n)]
        return copies, [functools.partial(lambda cp: cp, cp) for cp in copies]

    def shapes():
        return [_sds(big[k].shape[:2] + (big[k].shape[2] // 2, big[k].shape[3]), F32) for k in names]

    return _Task(lambda: [big[k] for k in names], shapes, {}, n, make, lambda res: got.update(zip(names, res)))


def _rs_pair_sum(name, full, got, core):
    nw, ns, R, C = full.shape
    half = R // 2

    def body(core_ref, a_ref, b_ref, o_ref):
        o_ref[...] = (a_ref[...] + b_ref[...]).astype(BF16)

    blk = lambda fn: pl.BlockSpec((None, None, half, C), fn)
    return _pcall(
        body, name=name, grid=(nw, ns), num_prefetch=1,
        in_specs=[blk(lambda w, s, core_ref: (w, s, core_ref[0], 0)), blk(lambda w, s, core_ref: (w, s, 0, 0))],
        out_specs=blk(lambda w, s, core_ref: (w, s, 0, 0)),
        out_shape=_sds((nw, ns, half, C), BF16),
    )(core, full, got)


def _chip_exchange_task(names, pair_sums, by_source, part=0, nparts=1):
    n = len(names)

    def rows(ref):
        h = ref.shape[1] // nparts
        return ref.at[:, pl.ds(part * h, h), :]

    def make(ins, outs, send_sem, recv_sem):
        x, y, c, chips = _mesh_position()
        s_me = _chip_index(x, y)
        starts, arrivals = [], []
        for a in range(n):
            for k, chip in enumerate(chips):
                s_k = _chip_index(*chip)
                starts.append(_remote(rows(ins[a].at[:, s_k]), rows(outs[a].at[:, s_me]), send_sem(3 * a + k),
                                      recv_sem(3 * a + k), (*chip, c)))
                got = rows(outs[a].at[:, s_k])
                arrivals.append(functools.partial(_remote, got, got, send_sem(3 * a + k), recv_sem(3 * a + k),
                                                  (*chip, c)))
        return starts, arrivals

    def operands():
        return [pair_sums[k] for k in names] + ([by_source[k] for k in names] if part else [])

    return _Task(operands, lambda: [_sds(pair_sums[k].shape, pair_sums[k].dtype) for k in names],
                 {n + a: a for a in range(n)} if part else {}, 3 * n, make,
                 lambda res: by_source.update(zip(names, res)))


def _rs_chip_sum(name, own, parts, chip):
    nw, ns, H, C = parts.shape

    def body(chip_ref, own_ref, *rest):
        prefs, o_ref = rest[:ns], rest[ns]
        me = chip_ref[0]
        own_v = own_ref[...].astype(F32)
        tot = None
        for s in range(ns):
            term = jnp.where(me == s, own_v, prefs[s][...].astype(F32))
            tot = term if tot is None else tot + term
        o_ref[...] = tot

    blk = lambda fn: pl.BlockSpec((None, None, H, C), fn)

    def part_spec(s):
        return blk(lambda w, chip_ref: (w, jnp.where(chip_ref[0] == s, (s + 1) % ns, s), 0, 0))

    return _pcall(
        body, name=name, grid=(nw,), num_prefetch=1,
        in_specs=[blk(lambda w, chip_ref: (w, chip_ref[0], 0, 0))] + [part_spec(s) for s in range(ns)],
        out_specs=pl.BlockSpec((None, H, C), lambda w, chip_ref: (w, 0, 0)),
        out_shape=_sds((nw, H, C), F32),
    )(chip, own, *([parts] * ns))


def _pair_gather_task(names, halves, sibling_halves):
    n = len(names)

    def make(ins, outs, send_sem, recv_sem):
        x, y, c, _ = _mesh_position()
        copies = [_remote(ins[a], outs[a], send_sem(a), recv_sem(a), (x, y, 1 - c)) for a in range(n)]
        return copies, [functools.partial(lambda cp: cp, cp) for cp in copies]

    return _Task(lambda: [halves[k] for k in names], lambda: [_sds(halves[k].shape, F32) for k in names],
                 {}, n, make, lambda res: sibling_halves.update(zip(names, res)))


def _small_allreduce(v):
    R, C = v.shape
    Q = R // N_CHIPS
    nsem = 1 + 2 * N_PEER_CHIPS

    def body(v_ref, o_ref, sib_buf, pair_buf, part_buf, send_sems, recv_sems):
        x, y, c, chips = _mesh_position()
        s_me = _chip_index(x, y)

        def quarter(ref, s):
            return ref.at[pl.ds(pl.multiple_of(s * Q, 8), Q)]

        def exchange(first_sem, src, dst_of, arrival_of):
            sends = [_remote(src(_chip_index(*chip)), dst_of(s_me), send_sems.at[first_sem + k],
                             recv_sems.at[first_sem + k], (*chip, c)) for k, chip in enumerate(chips)]
            for cp in sends:
                cp.start()
            for k, chip in enumerate(chips):
                got = arrival_of(_chip_index(*chip))
                _remote(got, got, send_sems.at[first_sem + k], recv_sems.at[first_sem + k], (*chip, c)).wait_recv()
            for cp in sends:
                cp.wait_send()

        swap = _remote(v_ref, sib_buf, send_sems.at[0], recv_sems.at[0], (x, y, 1 - c))
        swap.start()
        swap.wait()
        pair_buf[...] = v_ref[...] + sib_buf[...]
        exchange(1, lambda s_k: quarter(pair_buf, s_k), lambda s: part_buf.at[s], lambda s_k: part_buf.at[s_k])
        part_buf[s_me] = quarter(pair_buf, s_me)[...]
        o_ref[pl.ds(pl.multiple_of(s_me * Q, 8), Q), :] = ((part_buf[0] + part_buf[1]) + part_buf[2]) + part_buf[3]
        exchange(1 + N_PEER_CHIPS, lambda s_k: quarter(o_ref, s_me), lambda s: quarter(o_ref, s),
                 lambda s_k: quarter(o_ref, s_k))

    return pl.pallas_call(
        body, name="small_allreduce",
        in_specs=[VMEM_SPEC], out_specs=VMEM_SPEC, out_shape=_sds((R, C), F32),
        scratch_shapes=[pltpu.VMEM((R, C), F32), pltpu.VMEM((R, C), F32), pltpu.VMEM((N_CHIPS, Q, C), F32),
                        pltpu.SemaphoreType.DMA((nsem,)), pltpu.SemaphoreType.DMA((nsem,))],
        compiler_params=pltpu.CompilerParams(vmem_limit_bytes=VMEM_LIMIT_BYTES),
    )(v)


TRANSPOSED_WEIGHTS = ("ffn1_w1", "ffn1_w3", "ffn2_w1", "ffn2_w3")
SMALL_LAYOUT = [("ffn1_norm", 1), ("mix_norm", 1), ("ret_gn", 1), ("conv_b", 1), ("b_rgate", 1), ("b_igate", 1),
                ("lru_lambda", 1), ("xattn_norm", 1), ("mem_norm", 1), ("ffn2_norm", 1), ("final_norm", 1),
                ("b_branch_gate", 2), ("conv_w", CONV_TAPS), ("w_rgate", LRU_BLOCK), ("w_igate", LRU_BLOCK)]
SMALL_ROWS = 288
WEIGHT_ORDER = ["ffn1_norm", "ffn1_w1", "ffn1_w3", "ffn1_w2", "mix_norm", "w_in", "ret_gn", "w_ret_o", "conv_w",
                "conv_b", "w_rgate", "b_rgate", "w_igate", "b_igate", "lru_lambda", "w_lru_o", "w_branch_gate",
                "b_branch_gate", "w_out", "xattn_norm", "mem_norm", "w_xq", "w_xk", "w_xv", "w_xo", "ffn2_norm",
                "ffn2_w1", "ffn2_w3", "ffn2_w2", "final_norm"]


def _pack_small(parts):
    rows = [parts[name].reshape(n, D) for name, n in SMALL_LAYOUT]
    used = sum(n for _, n in SMALL_LAYOUT)
    rows.append(jnp.zeros((SMALL_ROWS - used, D), F32))
    return jnp.concatenate(rows, axis=0)


def _unpack_small(packed, shapes):
    out, r = {}, 0
    for name, n in SMALL_LAYOUT:
        out[name] = packed[r:r + n].reshape(shapes[name])
        r += n
    return out


def kernel(x, mem, ffn1_norm, ffn1_w1, ffn1_w3, ffn1_w2, mix_norm, w_in, ret_gn, w_ret_o, conv_w, conv_b, w_rgate, b_rgate, w_igate, b_igate, lru_lambda, w_lru_o, w_branch_gate, b_branch_gate, w_out, xattn_norm, mem_norm, w_xq, w_xk, w_xv, w_xo, ffn2_norm, ffn2_w1, ffn2_w3, ffn2_w2, final_norm, loss_target, m_ffn1_norm, m_ffn1_w1, m_ffn1_w3, m_ffn1_w2, m_mix_norm, m_w_in, m_ret_gn, m_w_ret_o, m_conv_w, m_conv_b, m_w_rgate, m_b_rgate, m_w_igate, m_b_igate, m_lru_lambda, m_w_lru_o, m_w_branch_gate, m_b_branch_gate, m_w_out, m_xattn_norm, m_mem_norm, m_w_xq, m_w_xk, m_w_xv, m_w_xo, m_ffn2_norm, m_ffn2_w1, m_ffn2_w3, m_ffn2_w2, m_final_norm, v_ffn1_norm, v_ffn1_w1, v_ffn1_w3, v_ffn1_w2, v_mix_norm, v_w_in, v_ret_gn, v_w_ret_o, v_conv_w, v_conv_b, v_w_rgate, v_b_rgate, v_w_igate, v_b_igate, v_lru_lambda, v_w_lru_o, v_w_branch_gate, v_b_branch_gate, v_w_out, v_xattn_norm, v_mem_norm, v_w_xq, v_w_xk, v_w_xv, v_w_xo, v_ffn2_norm, v_ffn2_w1, v_ffn2_w3, v_ffn2_w2, v_final_norm):
    given = dict(locals())
    w = {n: given[n] for n in WEIGHT_ORDER}
    mom = {n: given["m_" + n] for n in WEIGHT_ORDER}
    var = {n: given["v_" + n] for n in WEIGHT_ORDER}
    chip = _chip_index(lax.axis_index("x"), lax.axis_index("y"))
    core = lax.axis_index("c").astype(jnp.int32).reshape(1)

    chip_id = chip.astype(jnp.int32).reshape(1)
    sm = {n: w[n] for n in ["ffn1_norm", "mix_norm", "ret_gn", "conv_b", "b_rgate", "b_igate", "lru_lambda",
                            "xattn_norm", "mem_norm", "ffn2_norm", "b_branch_gate"]}
    sm["final_norm"] = w["final_norm"].reshape(1, D)
    sm["w_rgate"] = w["w_rgate"][0]
    sm["w_igate"] = w["w_igate"][0]

    local = lambda a, n: jnp.swapaxes(a[0], 0, 1) if n in TRANSPOSED_WEIGHTS else a[0]
    stack = lambda names: jnp.stack([local(w[n], n) for n in names], axis=0).astype(BF16)
    shard = {"col1": stack(["ffn1_w1", "ffn1_w3"]), "row2a": stack(["ffn1_w2"]), "win": stack(["w_in"]),
             "wbg": stack(["w_branch_gate"]), "sqA": stack(["w_ret_o", "w_lru_o", "w_out"]),
             "sqB": stack(["w_xq", "w_xk"]), "sqC": stack(["w_xv", "w_xo"]), "col2": stack(["ffn2_w1", "ffn2_w3"]),
             "row2b": stack(["ffn2_w2"]), "conv": w["conv_w"]}
    gw, landed = {}, {}
    over_chips = lambda keys: _gather_chips_task({k: shard[k] for k in keys}, True, landed)
    to_sibling = lambda keys: _gather_sibling_task(keys, landed, gw)

    big, got, pair_sums, by_source, halves, sibling_halves, outs = {}, {}, {}, {}, {}, {}, {}
    pair_swap = lambda names: _pair_swap_task(names, big, got)
    exchange = lambda names, part=0, nparts=1: _chip_exchange_task(names, pair_sums, by_source, part, nparts)
    pair_gather = lambda names: _pair_gather_task(names, halves, sibling_halves)

    def pair_sum(names):
        for n in names:
            pair_sums[n] = _rs_pair_sum("rs_pair_sum_" + n, big[n], got[n], core)

    def chip_sum(names):
        for n in names:
            halves[n] = _rs_chip_sum("rs_chip_sum_" + n, pair_sums[n], by_source[n], chip_id)

    def adamw(names):
        for n in names:
            res = _adamw_halves("adamw_" + n, local(w[n], n), halves[n], sibling_halves[n], 0, local(mom[n], n),
                                local(var[n], n), core)
            outs[n] = tuple((jnp.swapaxes(r, 0, 1) if n in TRANSPOSED_WEIGHTS else r)[None] for r in res)

    do = lambda fn, names: functools.partial(fn, names)
    ffn2_grads = ["ffn2_w2", "ffn2_w1", "ffn2_w3"]
    xattn_grads = ["w_xo", "w_xq", "w_xk", "w_xv"]
    mix_out_grads = ["w_out", "w_ret_o", "w_lru_o"]
    mix_in_grads = ["w_in", "w_branch_gate"]
    plan = _Plan()
    plan.tasks = {
        "ag_first_chips": [over_chips(["col1", "row2a"])],
        "ag_first_sibling": [to_sibling(["col1", "row2a"])],
        "ffn1_up": [over_chips(["win"])],
        "ffn1_down": [to_sibling(["win"]), over_chips(["wbg"])],
        "mix_in": [to_sibling(["wbg"]), over_chips(["sqA"]), _gather_chips_task({"conv": shard["conv"]}, False, gw)],
        "mix_gates": [to_sibling(["sqA"]), over_chips(["row2b"])],
        "ret_fwd": [over_chips(["col2"]), to_sibling(["row2b"])],
        "lru_gates_fwd": [to_sibling(["col2"]), over_chips(["sqB"])],
        "lru_scan_fwd": [over_chips(["sqC"])],
        "y_lru": [to_sibling(["sqB", "sqC"])],
        "ffn2_dh": [pair_swap(ffn2_grads)],
        "dw_xo": [exchange(["ffn2_w2"], 0, 2)],
        "xattn_bwd": [exchange(["ffn2_w2"], 1, 2)],
        "dw_xq": [exchange(["ffn2_w1"], 0, 2)],
        "d_hq": [exchange(["ffn2_w1"], 1, 2)],
        "d_merged": [pair_swap(xattn_grads)],
        "dw_out": [exchange(["w_xo"])],
        "d_yr": [exchange(["w_xq"])],
        "dw_ret_o": [exchange(["w_xk"])],
        "dw_lru_o": [exchange(["w_xv"])],
        "ret_bwd": [exchange(["ffn2_w3"]), pair_swap(mix_out_grads), pair_gather(xattn_grads)],
        "lru_scan_bwd": [pair_gather(ffn2_grads)],
        "lru_gates_bwd": [exchange(mix_out_grads)],
        "dw_in": [pair_gather(mix_out_grads)],
        "d_h2": [pair_swap(mix_in_grads)],
        "ffn1_bwd_mid": [exchange(["w_in"], 0, 2), exchange(["w_branch_gate"], 0, 2)],
        "ffn1_dw2": [exchange(["w_in"], 1, 2)],
        "ffn1_dw1": [exchange(["w_branch_gate"], 1, 2), pair_swap(["ffn1_w2"])],
        "ffn1_dw3": [pair_gather(mix_in_grads), pair_swap(["ffn1_w1"]), exchange(["ffn1_w2"], 0, 2)],
        "ffn1_dh": [pair_swap(["ffn1_w3"]), exchange(["ffn1_w2"], 1, 2), exchange(["ffn1_w1"], 0, 2)],
        "rs_last": [exchange(["ffn1_w1"], 1, 2), exchange(["ffn1_w3"]), pair_gather(["ffn1_w2"])],
        "adamw_ffn1_w2": [pair_gather(["ffn1_w1", "ffn1_w3"])],
    }
    plan.after = {
        "ffn2_dh": [do(pair_sum, ffn2_grads)],
        "d_merged": [do(pair_sum, xattn_grads)],
        "dw_lru_o": [do(chip_sum, xattn_grads)],
        "ret_bwd": [do(chip_sum, ffn2_grads), do(pair_sum, mix_out_grads), do(adamw, xattn_grads)],
        "lru_scan_bwd": [do(adamw, ffn2_grads)],
        "lru_gates_bwd": [do(chip_sum, mix_out_grads)],
        "dw_in": [do(adamw, mix_out_grads)],
        "d_h2": [do(pair_sum, mix_in_grads)],
        "ffn1_dw1": [do(chip_sum, mix_in_grads), do(pair_sum, ["ffn1_w2"])],
        "ffn1_dw3": [do(pair_sum, ["ffn1_w1"]), do(adamw, mix_in_grads)],
        "ffn1_dh": [do(pair_sum, ["ffn1_w3"]), do(chip_sum, ["ffn1_w2"])],
        "rs_last": [do(chip_sum, ["ffn1_w1", "ffn1_w3"]), do(adamw, ["ffn1_w2", "ffn1_w1", "ffn1_w3"])],
    }
    global _plan
    _plan = plan
    try:
        _comm_call("ag_first_chips")
        _comm_call("ag_first_sibling")
        loss_part, grad_x, small = _local_step(x[0], mem[0], loss_target[0], gw, sm, big)
        _comm_call("rs_last")
    finally:
        _plan = None
    assert not plan.tasks and not plan.after, (list(plan.tasks), list(plan.after))
    loss = lax.psum(loss_part[0, 0], ("x", "y", "c"))

    small_shapes = {n: w[n].shape for n, _ in SMALL_LAYOUT}
    small_shapes["conv_w"] = (CONV_TAPS, D)
    small_sum = _small_allreduce(_pack_small(small))
    conv_grad = lax.dynamic_slice(small_sum[13:13 + CONV_TAPS], (0, chip * SQ_BLK), (CONV_TAPS, SQ_BLK))
    small_w = {n: w[n] for n, _ in SMALL_LAYOUT}
    small_m = {n: mom[n] for n, _ in SMALL_LAYOUT}
    small_v = {n: var[n] for n, _ in SMALL_LAYOUT}
    pad_cols = lambda a: jnp.pad(a[0], ((0, 0), (0, D - SQ_BLK)))
    for dct in (small_w, small_m, small_v):
        dct["conv_w"] = pad_cols(dct["conv_w"])
    g_pack = lax.dynamic_update_slice(small_sum, jnp.pad(conv_grad, ((0, 0), (0, D - SQ_BLK))), (13, 0))
    d_pack, m_pack, v_pack = _adamw("adamw_small", _pack_small(small_w), g_pack, _pack_small(small_m),
                                    _pack_small(small_v))
    unpacked = [_unpack_small(p, small_shapes) for p in (g_pack, d_pack, m_pack, v_pack)]
    for n, _ in SMALL_LAYOUT:
        if n == "conv_w":
            outs[n] = tuple(u[n][:, :SQ_BLK][None] for u in unpacked)
        else:
            outs[n] = tuple(u[n] for u in unpacked)

    result = [loss, grad_x[None]]
    for k in range(4):
        result += [outs[n][k] for n in WEIGHT_ORDER]
    return tuple(result)
```

```python
import functools
import math

import numpy as np
import jax
import jax.numpy as jnp
from jax import lax
from jax.experimental import pallas as pl
from jax.experimental.pallas import tpu as pltpu

F32 = jnp.float32
BF16 = jnp.bfloat16
MESH = pl.DeviceIdType.MESH

D = 1024
EPS = 1e-6
RET_HEADS = 4
RET_DK = 128
RET_DV = 256
CHUNK = 128
ROPE_BASE = 10000.0
LRU_BLOCKS = 8
LRU_BLOCK = 128
CONV_TAPS = 4
LRU_C = 8.0
D_FF = 2816
X_HEADS = 4
X_HD = 256
N_CHIPS = 4
FF_BLK = D_FF // N_CHIPS
IN_BLK = 5120 // N_CHIPS
BG_BLK = 2048 // N_CHIPS
SQ_BLK = D // N_CHIPS

ADAM_LR = 0.001
ADAM_B1 = 0.9
ADAM_B2 = 0.999
ADAM_EPS = 1e-08
ADAM_WD = 0.01
ADAM_STEP = 10

VMEM_LIMIT_BYTES = 56 * 1024 * 1024
ROW_TILE = 512
WIDE_ROW_TILE = 1024
SCAN_TILE = 256

_DN = {
    "nn": (((1,), (0,)), ((), ())),
    "nt": (((1,), (1,)), ((), ())),
    "tn": (((0,), (0,)), ((), ())),
}


def _cparams(n_axes):
    return pltpu.CompilerParams(dimension_semantics=("arbitrary",) * n_axes,
                                vmem_limit_bytes=VMEM_LIMIT_BYTES)


def _dot(a, b, kind):
    if b.ndim == 3:
        b = b.reshape(b.shape[0] * b.shape[1], b.shape[2])
    return lax.dot_general(a.astype(BF16), b.astype(BF16), _DN[kind], preferred_element_type=F32)


def _sigmoid(x):
    return 1.0 / (1.0 + jnp.exp(-x))


def _log1p_pos(e):
    u = 1.0 + e
    return jnp.where(u == 1.0, e, jnp.log(u) * (e / jnp.where(u == 1.0, 1.0, u - 1.0)))


def _expm1(x):
    u = jnp.exp(x)
    lu = jnp.log(u)
    safe = jnp.where(lu == 0.0, 1.0, lu)
    return jnp.where(u == 1.0, x, (u - 1.0) * (x / safe))


def _softplus(z):
    return jnp.maximum(z, 0.0) + _log1p_pos(jnp.exp(-jnp.abs(z)))


_GELU_C = math.sqrt(2.0 / math.pi)


def _gelu_and_grad(x):
    x2 = x * x
    t = jnp.tanh(_GELU_C * (x + 0.044715 * x * x2))
    g = 0.5 * x * (1.0 + t)
    dg = 0.5 * (1.0 + t) + 0.5 * x * (1.0 - t * t) * (_GELU_C * (1.0 + 3.0 * 0.044715 * x2))
    return g, dg


def _rms_fwd(x, g):
    r = lax.rsqrt(jnp.mean(x * x, axis=-1, keepdims=True) + EPS)
    return (x * r) * g


def _rms_bwd(x, g, dh):
    r = lax.rsqrt(jnp.mean(x * x, axis=-1, keepdims=True) + EPS)
    n = x * r
    dyg = dh * g
    dx = r * (dyg - n * jnp.mean(dyg * n, axis=-1, keepdims=True))
    return dx, jnp.sum(dh * n, axis=0, keepdims=True)


def _accumulate(ref, val, first):
    @pl.when(first)
    def _():
        ref[...] = val

    @pl.when(jnp.logical_not(first))
    def _():
        ref[...] += val


def _sds(shape, dtype):
    return jax.ShapeDtypeStruct(tuple(shape), dtype)


def _spec(shape, fn):
    return pl.BlockSpec(tuple(shape), fn)


class _Task:
    def __init__(self, operands, out_shapes, aliases, nsem, make, finish):
        self.operands, self.out_shapes, self.aliases = operands, out_shapes, aliases
        self.nsem, self.make, self.finish = nsem, make, finish


class _Plan:
    def __init__(self):
        self.tasks, self.after = {}, {}


_plan = None


def _pcall(body, *, name, grid, in_specs, out_specs, out_shape, scratch_shapes=(), num_prefetch=0):
    single = not isinstance(out_shape, (list, tuple))
    out_shape = [out_shape] if single else list(out_shape)
    out_specs = [out_specs] if single else list(out_specs)
    in_specs = list(in_specs)
    scratch_shapes = list(scratch_shapes)
    tasks = _plan.tasks.pop(name, []) if _plan is not None else []
    after = _plan.after.pop(name, []) if _plan is not None else []
    nax = len(grid)

    def run(*operands):
        n_in = len(operands) - num_prefetch
        n_out = len(out_shape)
        t_ops = [t.operands() for t in tasks]
        t_outs = [t.out_shapes() for t in tasks]
        c_ops = [a for ops in t_ops for a in ops]
        c_outs = [s for outs in t_outs for s in outs]
        aliases = {}
        i0, o0 = num_prefetch + n_in, n_out
        for t, ops, outs in zip(tasks, t_ops, t_outs):
            for i_loc, o_loc in t.aliases.items():
                aliases[i0 + i_loc] = o0 + o_loc
            i0 += len(ops)
            o0 += len(outs)
        nsem = sum(t.nsem for t in tasks)

        def wrapped(*refs):
            p = num_prefetch
            pre, ins = refs[:p], refs[p:p + n_in]
            cins = refs[p + n_in:p + n_in + len(c_ops)]
            q = p + n_in + len(c_ops)
            outs, couts = refs[q:q + n_out], refs[q + n_out:q + n_out + len(c_outs)]
            q += n_out + len(c_outs)
            scr = refs[q:q + len(scratch_shapes)]

            def descriptors():
                send_sems, recv_sems = refs[q + len(scratch_shapes):]
                starts, arrivals = [], []
                ci = co = so = 0
                for t, ops, souts in zip(tasks, t_ops, t_outs):
                    s, a = t.make(cins[ci:ci + len(ops)], couts[co:co + len(souts)],
                                  functools.partial(lambda base, k: send_sems.at[base + k], so),
                                  functools.partial(lambda base, k: recv_sems.at[base + k], so))
                    starts += s
                    arrivals += a
                    ci, co, so = ci + len(ops), co + len(souts), so + t.nsem
                return starts, arrivals

            if tasks:
                ids = [pl.program_id(k) for k in range(nax)]
                first = functools.reduce(jnp.logical_and, [i == 0 for i in ids])
                last = functools.reduce(jnp.logical_and, [i == g - 1 for i, g in zip(ids, grid)])

                @pl.when(first)
                def _():
                    for cp in descriptors()[0]:
                        cp.start()

            body(*pre, *ins, *outs, *scr)

            if tasks:
                @pl.when(last)
                def _():
                    starts, arrivals = descriptors()
                    for arrival in arrivals:
                        arrival().wait_recv()
                    for cp in starts:
                        cp.wait_send()

        sems = [pltpu.SemaphoreType.DMA((nsem,)), pltpu.SemaphoreType.DMA((nsem,))] if tasks else []
        res = pl.pallas_call(
            wrapped, name=name,
            grid_spec=pltpu.PrefetchScalarGridSpec(
                num_scalar_prefetch=num_prefetch, grid=tuple(grid),
                in_specs=in_specs + [ANY_SPEC] * len(c_ops),
                out_specs=out_specs + [ANY_SPEC] * len(c_outs),
                scratch_shapes=scratch_shapes + sems),
            out_shape=out_shape + c_outs,
            input_output_aliases=aliases,
            compiler_params=_cparams(nax),
        )(*operands, *c_ops)
        co = n_out
        for t, souts in zip(tasks, t_outs):
            t.finish(res[co:co + len(souts)])
            co += len(souts)
        for fn in after:
            fn()
        return res[0] if single else list(res[:n_out])

    return run


def _comm_call(name):
    def body(o_ref):
        o_ref[...] = jnp.zeros_like(o_ref)

    _pcall(body, name=name, grid=(1,), in_specs=[], out_specs=_spec((8, 128), lambda i: (0, 0)),
           out_shape=_sds((8, 128), F32))()


def _gemm(name, terms, grid, outs, acc_shape, extras=(), epilogue=None):
    kinds = [t[4] for t in terms]
    nt, ne, no = len(terms), len(extras), len(outs)
    nred = grid[-1]
    nax = len(grid)

    def body(*refs):
        trefs = refs[:2 * nt]
        erefs = refs[2 * nt:2 * nt + ne]
        orefs = refs[2 * nt + ne:2 * nt + ne + no]
        ids = [pl.program_id(k) for k in range(nax)]
        tot = None
        for t in range(nt):
            d = _dot(trefs[2 * t][...], trefs[2 * t + 1][...], kinds[t])
            tot = d if tot is None else tot + d

        def finish(acc):
            if epilogue is None:
                orefs[0][...] = acc.astype(orefs[0].dtype)
            else:
                epilogue(acc, erefs, orefs, ids)

        if nred == 1:
            finish(tot)
        else:
            acc_ref = refs[-1]
            r = ids[-1]

            @pl.when(r == 0)
            def _():
                acc_ref[...] = tot

            @pl.when(r > 0)
            def _():
                acc_ref[...] += tot

            @pl.when(r == nred - 1)
            def _():
                finish(acc_ref[...])

    operands, in_specs = [], []
    for a, a_spec, b, b_spec, _ in terms:
        operands += [a, b]
        in_specs += [a_spec, b_spec]
    for e, e_spec in extras:
        operands.append(e)
        in_specs.append(e_spec)
    scratch = [pltpu.VMEM(tuple(acc_shape), F32)] if nred > 1 else []
    return _pcall(body, name=name, grid=tuple(grid), in_specs=in_specs, out_specs=[o[1] for o in outs],
                  out_shape=[o[0] for o in outs], scratch_shapes=scratch)(*operands)


def _rowwise(name, fn, ins, outs, grid):
    ni = len(ins)
    nax = len(grid)

    def body(*refs):
        ids = [pl.program_id(k) for k in range(nax)]
        fn(refs[:ni], refs[ni:], ids)

    return _pcall(body, name=name, grid=tuple(grid), in_specs=[i[1] for i in ins],
                  out_specs=[o[1] for o in outs], out_shape=[o[0] for o in outs])(*[i[0] for i in ins])


def _ffn_up(name, h, w1buf, w1_idx, w3buf, w3_idx):
    T = h.shape[0]
    tm = min(WIDE_ROW_TILE, T)

    def body(h_ref, w1_ref, w3_ref, a_ref, b_ref, s_ref):
        hv = h_ref[...]
        a = _dot(hv, w1_ref[...], "nt")
        b = _dot(hv, w3_ref[...], "nt")
        a_ref[...] = a.astype(BF16)
        b_ref[...] = b.astype(BF16)
        s_ref[...] = ((a * _sigmoid(a)) * b).astype(BF16)

    blk = _spec((None, tm, FF_BLK), lambda j, i: (j, i, 0))
    return _pcall(
        body, name=name, grid=(N_CHIPS, T // tm),
        in_specs=[_spec((tm, D), lambda j, i: (i, 0)),
                  _spec((None, None, FF_BLK, D), lambda j, i: (j, w1_idx, 0, 0)),
                  _spec((None, None, FF_BLK, D), lambda j, i: (j, w3_idx, 0, 0))],
        out_specs=[blk, blk, blk],
        out_shape=[_sds((N_CHIPS, T, FF_BLK), BF16)] * 3,
    )(h, w1buf, w3buf)


def _ffn_down(name, s, wrow2, w2_idx, x_res, g_next=None):
    T = x_res.shape[0]
    tm = min(WIDE_ROW_TILE, T)
    row = lambda i, j, r: (i, 0)

    def epilogue(acc, erefs, orefs, ids):
        xo = erefs[0][...] + 0.5 * acc
        orefs[0][...] = xo
        if g_next is not None:
            orefs[1][...] = _rms_fwd(xo, erefs[1][...]).astype(BF16)

    extras = [(x_res, _spec((tm, D), row))]
    outs = [(_sds((T, D), F32), _spec((tm, D), row))]
    if g_next is not None:
        extras.append((g_next, _spec((1, D), lambda i, j, r: (0, 0))))
        outs.append((_sds((T, D), BF16), _spec((tm, D), row)))
    return _gemm(
        name,
        [(s, _spec((None, tm, FF_BLK), lambda i, j, r: (r, i, 0)),
          wrow2, _spec((None, None, FF_BLK, D), lambda i, j, r: (r, w2_idx, 0, 0)), "nn")],
        (T // tm, 1, N_CHIPS), outs, (tm, D), extras, epilogue)


def _ffn_bwd_mid(name, dx, wrow2, w2_idx, a, b):
    T = dx.shape[0]
    tm = min(WIDE_ROW_TILE, T)

    def body(dx_ref, w2_ref, a_ref, b_ref, dab_ref):
        ds = _dot(0.5 * dx_ref[...], w2_ref[...], "nt")
        av = a_ref[...].astype(F32)
        sg = _sigmoid(av)
        dab_ref[0] = (ds * b_ref[...].astype(F32) * (sg * (1.0 + av * (1.0 - sg)))).astype(BF16)
        dab_ref[1] = (ds * (av * sg)).astype(BF16)

    blk = _spec((None, tm, FF_BLK), lambda j, i: (j, i, 0))
    return _pcall(
        body, name=name, grid=(N_CHIPS, T // tm),
        in_specs=[_spec((tm, D), lambda j, i: (i, 0)),
                  _spec((None, None, FF_BLK, D), lambda j, i: (j, w2_idx, 0, 0)),
                  blk, blk],
        out_specs=_spec((2, None, tm, FF_BLK), lambda j, i: (0, j, i, 0)),
        out_shape=_sds((2, N_CHIPS, T, FF_BLK), BF16),
    )(dx, wrow2, a, b)


def _rms_bwd_epilogue(acc, erefs, orefs, ids):
    dx, dgp = _rms_bwd(erefs[0][...], erefs[1][...], acc)
    orefs[0][...] = dx + erefs[2][...]
    _accumulate(orefs[1], dgp, ids[0] == 0)


def _rms_bwd_io(x, g, dres, T, tm):
    row = lambda i, j, r: (i, 0)
    vec = lambda i, j, r: (0, 0)
    extras = [(x, _spec((tm, D), row)), (g, _spec((1, D), vec)), (dres, _spec((tm, D), row))]
    outs = [(_sds((T, D), F32), _spec((tm, D), row)), (_sds((1, D), F32), _spec((1, D), vec))]
    return extras, outs


def _ffn_bwd(tag, dx_out, h, a, b, s, w1buf, w1_idx, w3buf, w3_idx, wrow2, w2_idx, x_in, g, big):
    T = dx_out.shape[0]
    tk = T
    dab = _ffn_bwd_mid(tag + "_bwd_mid", dx_out, wrow2, w2_idx, a, b)

    def half_scale(acc, erefs, orefs, ids):
        orefs[0][...] = 0.5 * acc

    big[tag + "_w2"] = _gemm(
        tag + "_dw2",
        [(s, _spec((None, tk, FF_BLK), lambda j, n, r: (j, r, 0)),
          dx_out, _spec((tk, D), lambda j, n, r: (r, 0)), "tn")],
        (N_CHIPS, 1, T // tk),
        [(_sds((N_CHIPS, FF_BLK, D), F32), _spec((None, FF_BLK, D), lambda j, n, r: (j, 0, 0)))],
        (FF_BLK, D), (), half_scale)[0][None]
    for widx, wname in ((0, "_w1"), (1, "_w3")):
        big[tag + wname] = _gemm(
            tag + "_d" + wname[1:],
            [(dab, _spec((None, None, tk, FF_BLK), functools.partial(lambda w, j, n, r: (w, j, r, 0), widx)),
              h, _spec((tk, D), lambda j, n, r: (r, 0)), "tn")],
            (N_CHIPS, 1, T // tk),
            [(_sds((N_CHIPS, FF_BLK, D), F32), _spec((None, FF_BLK, D), lambda j, n, r: (j, 0, 0)))],
            (FF_BLK, D))[0][None]
    tw = min(WIDE_ROW_TILE, T)
    extras, outs = _rms_bwd_io(x_in, g, dx_out, T, tw)
    dx_in, dg = _gemm(
        tag + "_dh",
        [(dab, _spec((None, None, tw, FF_BLK), lambda i, j, r: (0, r, i, 0)),
          w1buf, _spec((None, None, FF_BLK, D), lambda i, j, r: (r, w1_idx, 0, 0)), "nn"),
         (dab, _spec((None, None, tw, FF_BLK), lambda i, j, r: (1, r, i, 0)),
          w3buf, _spec((None, None, FF_BLK, D), lambda i, j, r: (r, w3_idx, 0, 0)), "nn")],
        (T // tw, 1, N_CHIPS), outs, (tw, D), extras, _rms_bwd_epilogue)
    return dx_in, dg


def _proj_sq(name, a, wsq, idx, kind, out_dtype=F32, extras=(), epilogue=None, outs=None):
    M = a.shape[0]
    tm = min(ROW_TILE, M)
    if outs is None:
        outs = [(_sds((M, D), out_dtype), _spec((tm, D), lambda i, j, r: (i, 0)))]
    return _gemm(
        name,
        [(a, _spec((tm, D), lambda i, j, r: (i, 0)),
          wsq, _spec((N_CHIPS, None, SQ_BLK, D), lambda i, j, r: (0, idx, 0, 0)), kind)],
        (M // tm, 1, 1), outs, (tm, D), extras, epilogue)


def _dw_sq(name, a, b):
    M = a.shape[0]
    tk = M
    whole = _gemm(
        name,
        [(a, _spec((tk, D), lambda i, j, r: (r, 0)), b, _spec((tk, D), lambda i, j, r: (r, 0)), "tn")],
        (1, 1, M // tk),
        [(_sds((D, D), F32), _spec((D, D), lambda i, j, r: (0, 0)))],
        (D, D))[0]
    return whole.reshape(N_CHIPS, SQ_BLK, D)


def _retention_constants(T):
    pos = jnp.arange(T, dtype=F32)
    inv_freq = ROPE_BASE ** (-jnp.arange(0, RET_DK, 2, dtype=F32) / RET_DK)
    ang = pos[:, None] * inv_freq[None, :]
    cosf = jnp.concatenate([jnp.cos(ang), jnp.cos(ang)], axis=1)
    sins = jnp.concatenate([-jnp.sin(ang), jnp.sin(ang)], axis=1)
    lg = jnp.log(1.0 - 2.0 ** (-5.0 - jnp.arange(RET_HEADS, dtype=F32)))
    p = jnp.arange(CHUNK, dtype=F32)
    rel = p[:, None] - p[None, :]
    dmat = jnp.where(rel[None] >= 0, jnp.exp(rel[None] * lg[:, None, None]), 0.0)
    kd = jnp.exp((CHUNK - 1.0 - p)[None, :] * lg[:, None])[:, :, None]
    qd = jnp.exp((p + 1.0)[None, :] * lg[:, None])[:, :, None]
    cd = jnp.exp(CHUNK * lg)[:, None, None]
    return cosf, sins, dmat, kd, qd, cd


def _rot(t, cosv, sinv):
    return t * cosv + pltpu.roll(t, RET_DK // 2, 1) * sinv


def _unrot(t, cosv, sinv):
    return t * cosv - pltpu.roll(t, RET_DK // 2, 1) * sinv


def _ret_const_specs(cm):
    whole = lambda shape: _spec(shape, lambda c: (0,) * len(shape))
    return [
        _spec((CHUNK, RET_DK), lambda c: (cm(c), 0)),
        _spec((CHUNK, RET_DK), lambda c: (cm(c), 0)),
        whole((RET_HEADS, CHUNK, CHUNK)), whole((RET_HEADS, CHUNK, 1)), whole((RET_HEADS, CHUNK, 1)),
        whole((RET_HEADS, 1, 1)),
    ]


def _head(h, width):
    return slice(h * width, (h + 1) * width)


def _ret_fwd(u, consts, ret_gn):
    T = u.shape[0]
    nC = T // CHUNK
    kscale = RET_DK ** -0.5

    def body(q_ref, k_ref, v_ref, g_ref, cos_ref, sin_ref, dm_ref, kd_ref, qd_ref, cd_ref, gn_ref,
             qr_ref, kr_ref, ret_ref, yr_ref, st_ref, state):
        @pl.when(pl.program_id(0) == 0)
        def _():
            state[...] = jnp.zeros_like(state)

        cosv, sinv = cos_ref[...], sin_ref[...]
        for h in range(RET_HEADS):
            hk, hv = _head(h, RET_DK), _head(h, RET_DV)
            q = _rot(q_ref[:, hk], cosv, sinv)
            k = _rot(k_ref[:, hk], cosv, sinv) * kscale
            v = v_ref[:, hv]
            qr_ref[:, hk] = q
            kr_ref[:, hk] = k
            prev = state[h]
            st_ref[h] = prev
            s = _dot(q, k, "nt") * dm_ref[h]
            ret = _dot(s, v, "nn") + _dot(q, prev, "nn") * qd_ref[h]
            state[h] = cd_ref[h] * prev + _dot(k * kd_ref[h], v, "tn")
            ret_ref[:, hv] = ret
            mu = jnp.mean(ret, axis=-1, keepdims=True)
            xc = ret - mu
            yn = xc * lax.rsqrt(jnp.mean(xc * xc, axis=-1, keepdims=True) + EPS)
            g = g_ref[:, hv]
            yr_ref[:, hv] = ((g * _sigmoid(g)) * (yn * gn_ref[:, hv])).astype(BF16)

    cm = lambda c: c
    qk_w, v_w = RET_HEADS * RET_DK, RET_HEADS * RET_DV
    in_specs = [
        _spec((CHUNK, qk_w), lambda c: (c, 0)), _spec((CHUNK, qk_w), lambda c: (c, 1)),
        _spec((CHUNK, v_w), lambda c: (c, 1)), _spec((CHUNK, v_w), lambda c: (c, 2)),
    ] + _ret_const_specs(cm) + [_spec((1, v_w), lambda c: (0, 0))]
    qk_out = _spec((CHUNK, qk_w), lambda c: (c, 0))
    v_out = _spec((CHUNK, v_w), lambda c: (c, 0))
    return _pcall(
        body, name="ret_fwd", grid=(nC,),
        in_specs=in_specs,
        out_specs=[qk_out, qk_out, v_out, v_out,
                   _spec((RET_HEADS, None, RET_DK, RET_DV), lambda c: (0, c, 0, 0))],
        out_shape=[_sds((T, qk_w), F32), _sds((T, qk_w), F32), _sds((T, v_w), F32), _sds((T, v_w), BF16),
                   _sds((RET_HEADS, nC, RET_DK, RET_DV), F32)],
        scratch_shapes=[pltpu.VMEM((RET_HEADS, RET_DK, RET_DV), F32)],
    )(u, u, u, u, *consts, ret_gn)


def _ret_bwd(dyr, ret, u, qr, kr, states, consts, ret_gn):
    T = u.shape[0]
    nC = T // CHUNK
    kscale = RET_DK ** -0.5

    def body(dyr_ref, ret_ref, g_ref, q_ref, k_ref, v_ref, st_ref,
             cos_ref, sin_ref, dm_ref, kd_ref, qd_ref, cd_ref, gn_ref,
             dq_ref, dk_ref, dv_ref, dg_ref, dgn_ref, gstate):
        first = pl.program_id(0) == 0

        @pl.when(first)
        def _():
            gstate[...] = jnp.zeros_like(gstate)

        cosv, sinv = cos_ref[...], sin_ref[...]
        dgn_parts = []
        for h in range(RET_HEADS):
            hk, hv = _head(h, RET_DK), _head(h, RET_DV)
            ret = ret_ref[:, hv]
            mu = jnp.mean(ret, axis=-1, keepdims=True)
            xc = ret - mu
            rs = lax.rsqrt(jnp.mean(xc * xc, axis=-1, keepdims=True) + EPS)
            yn = xc * rs
            gn = gn_ref[:, hv]
            g = g_ref[:, hv]
            sg = _sigmoid(g)
            dyr_v = dyr_ref[:, hv]
            dretn = dyr_v * (g * sg)
            dg_ref[:, hv] = (dyr_v * (yn * gn) * (sg * (1.0 + g * (1.0 - sg)))).astype(BF16)
            dgn_parts.append(jnp.sum(dretn * yn, axis=0, keepdims=True))
            dyn = dretn * gn
            d_o = rs * (dyn - jnp.mean(dyn, axis=-1, keepdims=True)
                        - yn * jnp.mean(dyn * yn, axis=-1, keepdims=True))

            q, k, v = q_ref[:, hk], k_ref[:, hk], v_ref[:, hv]
            dmat, kd, qd = dm_ref[h], kd_ref[h], qd_ref[h]
            prev = st_ref[h]
            gnext = gstate[h]
            s = _dot(q, k, "nt") * dmat
            ds = _dot(d_o, v, "nt") * dmat
            doq = d_o * qd
            dq = _dot(ds, k, "nn") + _dot(doq, prev, "nt")
            dk = _dot(ds, q, "tn") + _dot(v, gnext, "nt") * kd
            dv = _dot(s, d_o, "tn") + _dot(k * kd, gnext, "nn")
            gstate[h] = cd_ref[h] * gnext + _dot(q, doq, "tn")
            dq_ref[:, hk] = _unrot(dq, cosv, sinv).astype(BF16)
            dk_ref[:, hk] = _unrot(dk * kscale, cosv, sinv).astype(BF16)
            dv_ref[:, hv] = dv.astype(BF16)
        _accumulate(dgn_ref, jnp.concatenate(dgn_parts, axis=1), first)

    cm = lambda c: nC - 1 - c
    qk_w, v_w = RET_HEADS * RET_DK, RET_HEADS * RET_DV
    vspec = lambda blk: _spec((CHUNK, v_w), lambda c: (cm(c), blk))
    qspec = _spec((CHUNK, qk_w), lambda c: (cm(c), 0))
    in_specs = [vspec(0), vspec(0), vspec(2), qspec, qspec, vspec(1),
                _spec((RET_HEADS, None, RET_DK, RET_DV), lambda c: (0, cm(c), 0, 0)),
                ] + _ret_const_specs(cm) + [_spec((1, v_w), lambda c: (0, 0))]
    return _pcall(
        body, name="ret_bwd", grid=(nC,),
        in_specs=in_specs,
        out_specs=[qspec, qspec, vspec(0), vspec(0), _spec((1, v_w), lambda c: (0, 0))],
        out_shape=[_sds((T, qk_w), BF16), _sds((T, qk_w), BF16), _sds((T, v_w), BF16), _sds((T, v_w), BF16),
                   _sds((1, v_w), F32)],
        scratch_shapes=[pltpu.VMEM((RET_HEADS, RET_DK, RET_DV), F32)],
    )(dyr, ret, u, qr, kr, u, states, *consts, ret_gn)


def _shift_down(x, s):
    rows = lax.broadcasted_iota(jnp.int32, x.shape, 0)
    return jnp.where(rows >= s, pltpu.roll(x, s, 0), 0.0)


def _shift_up(x, s):
    n = x.shape[0]
    rows = lax.broadcasted_iota(jnp.int32, x.shape, 0)
    return jnp.where(rows < n - s, pltpu.roll(x, n - s, 0), 0.0)


def _lru_specs(T):
    col = lambda off: _spec((T, LRU_BLOCK), lambda g: (0, off + g))
    vec = _spec((1, LRU_BLOCK), lambda g: (0, g))
    wblk = _spec((None, LRU_BLOCK, LRU_BLOCK), lambda g: (g, 0, 0))
    cw = _spec((CONV_TAPS, LRU_BLOCK), lambda g: (0, g))
    return col, vec, wblk, cw


def _lru_gates_fwd(u, conv_w, conv_b, w_r, b_r, w_i, b_i, lam):
    T = u.shape[0]
    col, vec, wblk, cw = _lru_specs(T)

    def body(x_ref, cw_ref, cb_ref, wr_ref, br_ref, wi_ref, bi_ref, lam_ref,
             xc_ref, r_ref, i_ref, a_ref, bx_ref):
        x = x_ref[...]
        w = cw_ref[...]
        xc = (_shift_down(x, 3) * w[0:1] + _shift_down(x, 2) * w[1:2] + _shift_down(x, 1) * w[2:3]
              + x * w[3:4] + cb_ref[...])
        r = _sigmoid(_dot(xc, wr_ref[...], "nn") + br_ref[...])
        i = _sigmoid(_dot(xc, wi_ref[...], "nn") + bi_ref[...])
        la = (-LRU_C) * r * _softplus(-lam_ref[...])
        xc_ref[...] = xc
        r_ref[...] = r
        i_ref[...] = i
        a_ref[...] = jnp.exp(la)
        bx_ref[...] = jnp.sqrt(-_expm1(2.0 * la)) * (i * xc)

    out = col(0)
    return _pcall(
        body, name="lru_gates_fwd", grid=(LRU_BLOCKS,),
        in_specs=[col(24), cw, vec, wblk, vec, wblk, vec, vec],
        out_specs=[out] * 5,
        out_shape=[_sds((T, D), F32)] * 5,
    )(u, conv_w, conv_b, w_r, b_r, w_i, b_i, lam)


def _lru_scan(name, a3, b3, reverse):
    T = a3.shape[0]
    nt = T // SCAN_TILE
    unroll = 8

    def body(a_ref, b_ref, o_ref, carry):
        @pl.when(pl.program_id(0) == 0)
        def _():
            carry[...] = jnp.zeros_like(carry)

        if not reverse:
            def step(t, h):
                h = a_ref[t] * h + b_ref[t]
                o_ref[t] = h
                return h
        else:
            def step(k, c):
                t = SCAN_TILE - 1 - k
                l = b_ref[t] + c
                o_ref[t] = l
                return a_ref[t] * l
        carry[...] = lax.fori_loop(0, SCAN_TILE, step, carry[...], unroll=unroll)

    idx = (lambda i: (nt - 1 - i, 0, 0)) if reverse else (lambda i: (i, 0, 0))
    blk = _spec((SCAN_TILE, LRU_BLOCKS, LRU_BLOCK), idx)
    return _pcall(
        body, name=name, grid=(nt,),
        in_specs=[blk, blk], out_specs=blk,
        out_shape=_sds((T, LRU_BLOCKS, LRU_BLOCK), F32),
        scratch_shapes=[pltpu.VMEM((LRU_BLOCKS, LRU_BLOCK), F32)],
    )(a3, b3)


def _lru_gates_bwd(lmb, hl, a, r, i, xc, u, conv_w, w_r, w_i, lam):
    T = u.shape[0]
    col, vec, wblk, cw = _lru_specs(T)

    def body(l_ref, h_ref, a_ref, r_ref, i_ref, xc_ref, x_ref, cw_ref, wr_ref, wi_ref, lam_ref,
             dx_ref, dwr_ref, dwi_ref, dvec_ref, dcw_ref):
        l = l_ref[...]
        av, rv, iv, xc = a_ref[...], r_ref[...], i_ref[...], xc_ref[...]
        lam_v = lam_ref[...]
        sp = _softplus(-lam_v)
        la = (-LRU_C) * rv * sp
        mult = jnp.sqrt(-_expm1(2.0 * la))
        da = l * _shift_down(h_ref[...], 1)
        dmult = l * (iv * xc)
        di = l * mult * xc
        dxc = l * mult * iv
        dla = da * av - dmult * (av * av) / mult
        dzr = (dla * ((-LRU_C) * sp)) * rv * (1.0 - rv)
        dzi = di * iv * (1.0 - iv)
        dsp = jnp.sum(dla * ((-LRU_C) * rv), axis=0, keepdims=True)
        dlam = dsp * (-_sigmoid(-lam_v))
        dwr_ref[...] = _dot(xc, dzr, "tn")
        dwi_ref[...] = _dot(xc, dzi, "tn")
        dxc = dxc + _dot(dzr, wr_ref[...], "nt") + _dot(dzi, wi_ref[...], "nt")
        x = x_ref[...]
        w = cw_ref[...]
        dx = (dxc * w[3:4] + _shift_up(dxc, 1) * w[2:3] + _shift_up(dxc, 2) * w[1:2]
              + _shift_up(dxc, 3) * w[0:1])
        dx_ref[...] = dx.astype(BF16)
        dvec_ref[...] = jnp.concatenate(
            [jnp.sum(dzr, axis=0, keepdims=True), jnp.sum(dzi, axis=0, keepdims=True), dlam,
             jnp.sum(dxc, axis=0, keepdims=True)], axis=0)
        dcw_ref[...] = jnp.concatenate(
            [jnp.sum(dxc * _shift_down(x, 3 - tap), axis=0, keepdims=True) if tap < 3
             else jnp.sum(dxc * x, axis=0, keepdims=True) for tap in range(CONV_TAPS)], axis=0)

    c0 = col(0)
    return _pcall(
        body, name="lru_gates_bwd", grid=(LRU_BLOCKS,),
        in_specs=[c0, c0, c0, c0, c0, c0, col(24), cw, wblk, wblk, vec],
        out_specs=[c0, wblk, wblk, cw, cw],
        out_shape=[_sds((T, D), BF16), _sds((LRU_BLOCKS, LRU_BLOCK, LRU_BLOCK), F32),
                   _sds((LRU_BLOCKS, LRU_BLOCK, LRU_BLOCK), F32), _sds((4, D), F32), _sds((CONV_TAPS, D), F32)],
    )(lmb, hl, a, r, i, xc, u, conv_w, w_r, w_i, lam)


def _xattn_probs(q, k):
    sc = _dot(q, k, "nt") * (X_HD ** -0.5)
    e = jnp.exp(sc - jnp.max(sc, axis=-1, keepdims=True))
    return e / jnp.sum(e, axis=-1, keepdims=True)


def _xattn_fwd(xq, xk, xv):
    T = xq.shape[0]
    tq = ROW_TILE
    M = xk.shape[0]

    def body(q_ref, k_ref, v_ref, o_ref):
        p = _xattn_probs(q_ref[...], k_ref[...])
        o_ref[...] = _dot(p, v_ref[...], "nn").astype(BF16)

    qs = _spec((tq, X_HD), lambda h, i: (i, h))
    kv = _spec((M, X_HD), lambda h, i: (0, h))
    return _pcall(
        body, name="xattn_fwd", grid=(X_HEADS, T // tq),
        in_specs=[qs, kv, kv], out_specs=qs, out_shape=_sds((T, D), BF16),
    )(xq, xk, xv)


def _xattn_bwd(xq, xk, xv, dxo):
    T = xq.shape[0]
    tq = ROW_TILE
    M = xk.shape[0]

    def body(q_ref, k_ref, v_ref, do_ref, dq_ref, dk_ref, dv_ref):
        first = pl.program_id(1) == 0
        q, k, v, do = q_ref[...], k_ref[...], v_ref[...], do_ref[...]
        p = _xattn_probs(q, k)
        dp = _dot(do, v, "nt")
        ds = p * (dp - jnp.sum(dp * p, axis=-1, keepdims=True)) * (X_HD ** -0.5)
        dq_ref[...] = _dot(ds, k, "nn").astype(BF16)
        _accumulate(dk_ref, _dot(ds, q, "tn"), first)
        _accumulate(dv_ref, _dot(p, do, "tn"), first)

    qs = _spec((tq, X_HD), lambda h, i: (i, h))
    kv = _spec((M, X_HD), lambda h, i: (0, h))
    return _pcall(
        body, name="xattn_bwd", grid=(X_HEADS, T // tq),
        in_specs=[qs, kv, kv, qs], out_specs=[qs, kv, kv],
        out_shape=[_sds((T, D), BF16), _sds((M, D), F32), _sds((M, D), F32)],
    )(xq, xk, xv, dxo)


def _final_loss(x, g, tgt):
    T = x.shape[0]
    tm = ROW_TILE

    def fn(irefs, orefs, ids):
        xv, gv = irefs[0][...], irefs[1][...]
        err = _rms_fwd(xv, gv) - irefs[2][...]
        lp = 0.5 * jnp.sum(jnp.mean(err * err, axis=-1, keepdims=True), axis=0, keepdims=True)
        first = ids[0] == 0
        _accumulate(orefs[0], jnp.broadcast_to(lp, (1, 128)), first)
        dx, dgp = _rms_bwd(xv, gv, err * (1.0 / D))
        orefs[1][...] = dx
        _accumulate(orefs[2], dgp, first)

    row = _spec((tm, D), lambda i: (i, 0))
    vec = _spec((1, D), lambda i: (0, 0))
    return _rowwise(
        "final_loss", fn, [(x, row), (g, vec), (tgt, row)],
        [(_sds((1, 128), F32), _spec((1, 128), lambda i: (0, 0))), (_sds((T, D), F32), row),
         (_sds((1, D), F32), vec)],
        (T // tm,))


def _adamw(name, w, g, m, v):
    R, C = w.shape
    tr = R
    for cand in (512, 352, 256):
        if R % cand == 0:
            tr = cand
            break

    def fn(irefs, orefs, ids):
        delta, mn, vn = _adamw_update(*(r[...] for r in irefs))
        orefs[0][...] = delta
        orefs[1][...] = mn
        orefs[2][...] = vn

    blk = _spec((tr, C), lambda i: (i, 0))
    return _rowwise(name, fn, [(w, blk), (g, blk), (m, blk), (v, blk)],
                    [(_sds((R, C), F32), blk)] * 3, (R // tr,))


def _adamw_update(wv, gv, mv, vv):
    c1 = 1.0 - ADAM_B1 ** ADAM_STEP
    c2 = 1.0 - ADAM_B2 ** ADAM_STEP
    mn = ADAM_B1 * mv + (1.0 - ADAM_B1) * gv
    vn = ADAM_B2 * vv + (1.0 - ADAM_B2) * (gv * gv)
    delta = -ADAM_LR * ((mn / c1) / (jnp.sqrt(vn / c2) + ADAM_EPS) + ADAM_WD * wv)
    return delta, mn, vn


def _adamw_halves(name, w, mine, theirs, widx, m, v, core):
    R, C = w.shape
    H = R // 2
    tr = H
    while tr * C * 4 > (1 << 20) and tr % 16 == 0:
        tr //= 2
    nb = H // tr

    def body(core_ref, w_ref, mine_ref, theirs_ref, m_ref, v_ref, g_out, d_out, m_out, v_out):
        gv = jnp.where(pl.program_id(0) == core_ref[0], mine_ref[...], theirs_ref[...])
        delta, mn, vn = _adamw_update(w_ref[...], gv, m_ref[...], v_ref[...])
        g_out[...] = gv
        d_out[...] = delta
        m_out[...] = mn
        v_out[...] = vn

    full = pl.BlockSpec((tr, C), lambda h, i, core_ref: (h * nb + i, 0))
    mine_spec = pl.BlockSpec((None, tr, C), lambda h, i, core_ref: (widx, jnp.where(h == core_ref[0], i, 0), 0))
    theirs_spec = pl.BlockSpec((None, tr, C), lambda h, i, core_ref: (widx, jnp.where(h == core_ref[0], 0, i), 0))
    return _pcall(
        body, name=name, grid=(2, nb), num_prefetch=1,
        in_specs=[full, mine_spec, theirs_spec, full, full], out_specs=[full] * 4,
        out_shape=[_sds((R, C), F32)] * 4,
    )(core, w, mine, theirs, m, v)


def _rmsnorm(name, x, g):
    M = x.shape[0]
    tm = min(ROW_TILE, M)

    def fn(irefs, orefs, ids):
        orefs[0][...] = _rms_fwd(irefs[0][...], irefs[1][...]).astype(BF16)

    row = _spec((tm, D), lambda i: (i, 0))
    return _rowwise(name, fn, [(x, row), (g, _spec((1, D), lambda i: (0, 0)))],
                    [(_sds((M, D), BF16), row)], (M // tm,))[0]


WEIGHT_AT = {
    "ffn1_w1": ("col1", 0), "ffn1_w3": ("col1", 1), "ffn1_w2": ("row2a", 0),
    "w_ret_o": ("sqA", 0), "w_lru_o": ("sqA", 1), "w_out": ("sqA", 2),
    "w_xq": ("sqB", 0), "w_xk": ("sqB", 1), "w_xv": ("sqC", 0), "w_xo": ("sqC", 1),
    "ffn2_w1": ("col2a", 0), "ffn2_w3": ("col2b", 0), "ffn2_w2": ("row2b", 0),
}


def _local_step(x, mem, tgt, gw, sm, big):
    T = x.shape[0]
    tm = ROW_TILE

    def wt(name):
        key, idx = WEIGHT_AT[name]
        return gw[key], idx

    row3 = lambda i, j, r: (i, 0)
    vec3 = lambda i, j, r: (0, 0)
    rowD = _spec((tm, D), row3)
    vecD = _spec((1, D), vec3)

    def residual_norm(acc, erefs, orefs, ids):
        xo = erefs[0][...] + acc
        orefs[0][...] = xo
        orefs[1][...] = _rms_fwd(xo, erefs[1][...]).astype(BF16)

    def res_norm_io(x_res, g):
        return ([(x_res, rowD), (g, vecD)],
                [(_sds((T, D), F32), rowD), (_sds((T, D), BF16), rowD)])

    h1 = _rmsnorm("ffn1_norm", x, sm["ffn1_norm"])
    a1, b1, s1 = _ffn_up("ffn1_up", h1, *wt("ffn1_w1"), *wt("ffn1_w3"))
    x1, h2 = _ffn_down("ffn1_down", s1, *wt("ffn1_w2"), x, sm["mix_norm"])

    tw = min(WIDE_ROW_TILE, T)
    wideD = _spec((tw, D), row3)
    u = _gemm(
        "mix_in",
        [(h2, wideD, gw["win"], _spec((None, None, D, IN_BLK), lambda i, j, r: (j, 0, 0, 0)), "nn")],
        (T // tw, N_CHIPS, 1),
        [(_sds((T, 5120), F32), _spec((tw, IN_BLK), lambda i, j, r: (i, j)))], (tw, IN_BLK))[0]

    consts = _retention_constants(T)
    qr, kr, ret, yr, states = _ret_fwd(u, consts, sm["ret_gn"])

    conv_w = gw["conv"][:, 0].transpose(1, 0, 2).reshape(CONV_TAPS, D)
    xc, rg, ig, av, bx = _lru_gates_fwd(u, conv_w, sm["conv_b"], sm["w_rgate"], sm["b_rgate"],
                                        sm["w_igate"], sm["b_igate"], sm["lru_lambda"])
    a3 = av.reshape(T, LRU_BLOCKS, LRU_BLOCK)
    hl = _lru_scan("lru_scan_fwd", a3, bx.reshape(T, LRU_BLOCKS, LRU_BLOCK), False).reshape(T, D)

    row1 = _spec((tm, D), lambda i: (i, 0))
    glru1 = _spec((tm, D), lambda i: (i, 4))

    def lru_out(irefs, orefs, ids):
        gl, _ = _gelu_and_grad(irefs[1][...])
        orefs[0][...] = (irefs[0][...] * gl).astype(BF16)

    yl = _rowwise("lru_out", lru_out, [(hl, row1), (u, glru1)], [(_sds((T, D), BF16), row1)], (T // tm,))[0]

    def gate_epilogue(acc, erefs, orefs, ids):
        orefs[0][...] = _sigmoid(acc + erefs[0][...])

    gates = _gemm(
        "mix_gates",
        [(h2, wideD, gw["wbg"], _spec((None, None, D, BG_BLK), lambda i, j, r: (j, 0, 0, 0)), "nn")],
        (T // tw, N_CHIPS, 1),
        [(_sds((T, 2 * D), F32), _spec((tw, BG_BLK), lambda i, j, r: (i, j)))], (tw, BG_BLK),
        [(sm["b_branch_gate"], _spec((1, BG_BLK), lambda i, j, r: (0, j)))], gate_epilogue)[0]

    y_ret = _proj_sq("y_ret", yr, *wt("w_ret_o"), "nn")[0]

    def merge_epilogue(acc, erefs, orefs, ids):
        orefs[0][...] = acc
        orefs[1][...] = (erefs[0][...] * erefs[2][...] + erefs[1][...] * acc).astype(BF16)

    y_lru, merged = _proj_sq(
        "y_lru", yl, *wt("w_lru_o"), "nn",
        extras=[(gates, _spec((tm, D), lambda i, j, r: (i, 0))), (gates, _spec((tm, D), lambda i, j, r: (i, 1))),
                (y_ret, rowD)],
        epilogue=merge_epilogue,
        outs=[(_sds((T, D), F32), rowD), (_sds((T, D), BF16), rowD)])

    ex, ou = res_norm_io(x1, sm["xattn_norm"])
    x2, hq = _proj_sq("mix_out", merged, *wt("w_out"), "nn", extras=ex, epilogue=residual_norm, outs=ou)

    m = _rmsnorm("mem_norm", mem, sm["mem_norm"])
    xq = _proj_sq("xq", hq, *wt("w_xq"), "nn", BF16)[0]
    xk = _proj_sq("xk", m, *wt("w_xk"), "nn", BF16)[0]
    xv = _proj_sq("xv", m, *wt("w_xv"), "nn", BF16)[0]
    xo = _xattn_fwd(xq, xk, xv)
    ex, ou = res_norm_io(x2, sm["ffn2_norm"])
    x3, h3 = _proj_sq("xattn_out", xo, *wt("w_xo"), "nn", extras=ex, epilogue=residual_norm, outs=ou)

    a2, b2, s2 = _ffn_up("ffn2_up", h3, *wt("ffn2_w1"), *wt("ffn2_w3"))
    x4 = _ffn_down("ffn2_down", s2, *wt("ffn2_w2"), x3)[0]
    loss, dx4, dg_final = _final_loss(x4, sm["final_norm"], tgt)

    dx3, dg_ffn2 = _ffn_bwd("ffn2", dx4, h3, a2, b2, s2, *wt("ffn2_w1"), *wt("ffn2_w3"),
                            *wt("ffn2_w2"), x3, sm["ffn2_norm"], big)

    dxo = _proj_sq("d_xo", dx3, *wt("w_xo"), "nt", BF16)[0]
    big["w_xo"] = _dw_sq("dw_xo", xo, dx3)[None]
    dxq, dxk, dxv = _xattn_bwd(xq, xk, xv, dxo)
    big["w_xq"] = _dw_sq("dw_xq", hq, dxq)[None]
    ex, ou = _rms_bwd_io(x2, sm["xattn_norm"], dx3, T, tm)
    dx2, dg_xattn = _proj_sq("d_hq", dxq, *wt("w_xq"), "nt", extras=ex, epilogue=_rms_bwd_epilogue, outs=ou)
    big["w_xk"] = _dw_sq("dw_xk", m, dxk)[None]
    big["w_xv"] = _dw_sq("dw_xv", m, dxv)[None]

    M = mem.shape[0]

    def mem_norm_epilogue(acc, erefs, orefs, ids):
        _, dgp = _rms_bwd(erefs[0][...], erefs[1][...], acc)
        orefs[0][...] = dgp

    wsq_spec = lambda idx: _spec((N_CHIPS, None, SQ_BLK, D), lambda i, j, r: (0, idx, 0, 0))
    memD = _spec((M, D), row3)
    dg_mem = _gemm(
        "d_mem_norm",
        [(dxk, memD, wt("w_xk")[0], wsq_spec(wt("w_xk")[1]), "nt"),
         (dxv, memD, wt("w_xv")[0], wsq_spec(wt("w_xv")[1]), "nt")],
        (1, 1, 1), [(_sds((1, D), F32), vecD)], (M, D),
        [(mem, memD), (sm["mem_norm"], vecD)], mem_norm_epilogue)[0]

    def merged_bwd_epilogue(acc, erefs, orefs, ids):
        gr, gl, yrv, ylv = (e[...] for e in erefs)
        orefs[0][...] = (acc * gr).astype(BF16)
        orefs[1][...] = (acc * gl).astype(BF16)
        dgr = acc * yrv * gr * (1.0 - gr)
        dgl = acc * ylv * gl * (1.0 - gl)
        orefs[2][:, :D] = dgr.astype(BF16)
        orefs[2][:, D:] = dgl.astype(BF16)
        dbb = jnp.concatenate([jnp.sum(dgr, axis=0, keepdims=True), jnp.sum(dgl, axis=0, keepdims=True)], axis=1)
        _accumulate(orefs[3], dbb, ids[0] == 0)

    dy_ret, dy_lru, dgpre, db_bg = _proj_sq(
        "d_merged", dx2, *wt("w_out"), "nt",
        extras=[(gates, _spec((tm, D), lambda i, j, r: (i, 0))), (gates, _spec((tm, D), lambda i, j, r: (i, 1))),
                (y_ret, rowD), (y_lru, rowD)],
        epilogue=merged_bwd_epilogue,
        outs=[(_sds((T, D), BF16), rowD), (_sds((T, D), BF16), rowD),
              (_sds((T, 2 * D), BF16), _spec((tm, 2 * D), row3)),
              (_sds((1, 2 * D), F32), _spec((1, 2 * D), vec3))])
    big["w_branch_gate"] = _gemm(
        "dw_bg",
        [(h2, _spec((T, D), lambda j, n, r: (r, 0)), dgpre, _spec((T, BG_BLK), lambda j, n, r: (r, j)), "tn")],
        (N_CHIPS, 1, 1),
        [(_sds((N_CHIPS, D, BG_BLK), F32), _spec((None, D, BG_BLK), lambda j, n, r: (j, 0, 0)))],
        (D, BG_BLK))[0][None]
    big["w_out"] = _dw_sq("dw_out", merged, dx2)[None]
    dyr = _proj_sq("d_yr", dy_ret, *wt("w_ret_o"), "nt")[0]
    big["w_ret_o"] = _dw_sq("dw_ret_o", yr, dy_ret)[None]
    dyl = _proj_sq("d_yl", dy_lru, *wt("w_lru_o"), "nt")[0]
    big["w_lru_o"] = _dw_sq("dw_lru_o", yl, dy_lru)[None]

    dq, dk, dv, dgr, dg_retgn = _ret_bwd(dyr, ret, u, qr, kr, states, consts, sm["ret_gn"])

    def lru_out_bwd(irefs, orefs, ids):
        gl, dgl = _gelu_and_grad(irefs[2][...])
        dyl_v = irefs[0][...]
        orefs[0][...] = dyl_v * gl
        orefs[1][...] = (dyl_v * irefs[1][...] * dgl).astype(BF16)

    dhl, dglru = _rowwise("lru_out_bwd", lru_out_bwd, [(dyl, row1), (hl, row1), (u, glru1)],
                          [(_sds((T, D), F32), row1), (_sds((T, D), BF16), row1)], (T // tm,))
    lmb = _lru_scan("lru_scan_bwd", a3, dhl.reshape(T, LRU_BLOCKS, LRU_BLOCK), True).reshape(T, D)
    dxl, dw_r, dw_i, dvec, dcw = _lru_gates_bwd(lmb, hl, av, rg, ig, xc, u, conv_w,
                                                sm["w_rgate"], sm["w_igate"], sm["lru_lambda"])

    du = jnp.concatenate([dq, dk, dv, dgr, dxl, dglru], axis=1)
    tk = T
    big["w_in"] = _gemm(
        "dw_in",
        [(h2, _spec((tk, D), lambda j, n, r: (r, 0)), du, _spec((tk, IN_BLK), lambda j, n, r: (r, j)), "tn")],
        (N_CHIPS, 1, T // tk),
        [(_sds((N_CHIPS, D, IN_BLK), F32), _spec((None, D, IN_BLK), lambda j, n, r: (j, 0, 0)))],
        (D, IN_BLK))[0][None]
    ex, ou = _rms_bwd_io(x1, sm["mix_norm"], dx2, T, tw)
    dx1, dg_mix = _gemm(
        "d_h2",
        [(du, _spec((tw, IN_BLK), lambda i, j, r: (i, r)),
          gw["win"], _spec((None, None, D, IN_BLK), lambda i, j, r: (r, 0, 0, 0)), "nt"),
         (dgpre, _spec((tw, BG_BLK), lambda i, j, r: (i, r)),
          gw["wbg"], _spec((None, None, D, BG_BLK), lambda i, j, r: (r, 0, 0, 0)), "nt")],
        (T // tw, 1, N_CHIPS), ou, (tw, D), ex, _rms_bwd_epilogue)

    grad_x, dg_ffn1 = _ffn_bwd("ffn1", dx1, h1, a1, b1, s1, *wt("ffn1_w1"), *wt("ffn1_w3"),
                               *wt("ffn1_w2"), x, sm["ffn1_norm"], big)

    small = {
        "ffn1_norm": dg_ffn1, "mix_norm": dg_mix, "ret_gn": dg_retgn, "conv_b": dvec[3:4],
        "b_rgate": dvec[0:1], "b_igate": dvec[1:2], "lru_lambda": dvec[2:3], "xattn_norm": dg_xattn,
        "mem_norm": dg_mem, "ffn2_norm": dg_ffn2, "final_norm": dg_final, "b_branch_gate": db_bg,
        "conv_w": dcw, "w_rgate": dw_r, "w_igate": dw_i,
    }
    return loss, grad_x, small


ANY_SPEC = pl.BlockSpec(memory_space=pl.ANY)
VMEM_SPEC = pl.BlockSpec(memory_space=pltpu.VMEM)
N_PEER_CHIPS = N_CHIPS - 1


def _mesh_position():
    x, y, c = lax.axis_index("x"), lax.axis_index("y"), lax.axis_index("c")
    chips = [(1 - x, y), (x, 1 - y), (1 - x, 1 - y)]
    return x, y, c, chips


def _chip_index(x, y):
    return 2 * x + y


def _rows_half(ref, axis, h):
    n = ref.shape[axis] // 2
    idx = [slice(None)] * len(ref.shape)
    idx[axis] = pl.ds(pl.multiple_of(h * n, 16), n)
    return ref.at[tuple(idx)]


def _remote(src, dst, send_sem, recv_sem, device):
    return pltpu.make_async_remote_copy(src_ref=src, dst_ref=dst, send_sem=send_sem, recv_sem=recv_sem,
                                        device_id=device, device_id_type=MESH)


def _gather_chips_task(shards, split, landed, part=0, nparts=1):
    keys = list(shards)
    n = len(keys)

    def operands():
        if part:
            return [shards[k] for k in keys] + [landed[k] for k in keys]
        chip_me = _chip_index(lax.axis_index("x"), lax.axis_index("y"))
        bases = [lax.dynamic_update_slice(lax.empty((N_CHIPS,) + shards[k].shape, shards[k].dtype), shards[k][None],
                                          (chip_me,) + (0,) * shards[k].ndim) for k in keys]
        return [shards[k] for k in keys] + bases

    def my_rows(ref, c):
        rows = ref.shape[1] // (2 * nparts)
        return ref.at[:, pl.ds(pl.multiple_of((c * nparts + part) * rows, 16), rows), :]

    def make(ins, outs, send_sem, recv_sem):
        x, y, c, chips = _mesh_position()
        s_me = _chip_index(x, y)
        starts, arrivals = [], []
        for g in range(n):
            mine = my_rows(ins[g], c) if split else ins[g]
            for k, chip in enumerate(chips):
                def landing(s):
                    o = outs[g].at[s]
                    return my_rows(o, c) if split else o
                starts.append(_remote(mine, landing(s_me), send_sem(3 * g + k), recv_sem(3 * g + k), (*chip, c)))
                got = landing(_chip_index(*chip))
                arrivals.append(functools.partial(_remote, got, got, send_sem(3 * g + k), recv_sem(3 * g + k),
                                                  (*chip, c)))
        return starts, arrivals

    def finish(res):
        landed.update(zip(keys, res))

    return _Task(operands, lambda: [_sds((N_CHIPS,) + shards[k].shape, shards[k].dtype) for k in keys],
                 {n + g: g for g in range(n)}, 3 * n, make, finish)


def _gather_sibling_task(keys, landed, ready):
    n = len(keys)

    def make(ins, outs, send_sem, recv_sem):
        x, y, c, chips = _mesh_position()
        starts, arrivals = [], []
        for g in range(n):
            for k, chip in enumerate(chips):
                o = outs[g].at[_chip_index(*chip)]
                got, other = _rows_half(o, 1, c), _rows_half(o, 1, 1 - c)
                starts.append(_remote(got, got, send_sem(3 * g + k), recv_sem(3 * g + k), (x, y, 1 - c)))
                arrivals.append(functools.partial(_remote, other, other, send_sem(3 * g + k), recv_sem(3 * g + k),
                                                  (x, y, 1 - c)))
        return starts, arrivals

    def finish(res):
        ready.update(zip(keys, res))

    return _Task(lambda: [landed[k] for k in keys], lambda: [_sds(landed[k].shape, landed[k].dtype) for k in keys],
                 {g: g for g in range(n)}, 3 * n, make, finish)


def _pair_swap_task(names, big, got):
    n = len(names)

    def make(ins, outs, send_sem, recv_sem):
        x, y, c, _ = _mesh_position()
        copies = [_remote(_rows_half(ins[a], 2, 1 - c), outs[a], send_sem(a), recv_sem(a), (x, y, 1 - c))
                  for a in range(n)]
        return copies, [functools.partial(lambda cp: cp, cp) for cp in copies]

    def shapes():
        return [_sds(big[k].shape[:2] + (big[k].shape[2] // 2, big[k].shape[3]), F32) for k in names]

    return _Task(lambda: [big[k] for k in names], shapes, {}, n, make, lambda res: got.update(zip(names, res)))


def _rs_pair_sum(name, full, got, core):
    nw, ns, R, C = full.shape
    half = R // 2

    def body(core_ref, a_ref, b_ref, o_ref):
        o_ref[...] = (a_ref[...] + b_ref[...]).astype(BF16)

    blk = lambda fn: pl.BlockSpec((None, None, half, C), fn)
    return _pcall(
        body, name=name, grid=(nw, ns), num_prefetch=1,
        in_specs=[blk(lambda w, s, core_ref: (w, s, core_ref[0], 0)), blk(lambda w, s, core_ref: (w, s, 0, 0))],
        out_specs=blk(lambda w, s, core_ref: (w, s, 0, 0)),
        out_shape=_sds((nw, ns, half, C), BF16),
    )(core, full, got)


def _chip_exchange_task(names, pair_sums, by_source, part=0, nparts=1):
    n = len(names)

    def rows(ref):
        h = ref.shape[1] // nparts
        return ref.at[:, pl.ds(part * h, h), :]

    def make(ins, outs, send_sem, recv_sem):
        x, y, c, chips = _mesh_position()
        s_me = _chip_index(x, y)
        starts, arrivals = [], []
        for a in range(n):
            for k, chip in enumerate(chips):
                s_k = _chip_index(*chip)
                starts.append(_remote(rows(ins[a].at[:, s_k]), rows(outs[a].at[:, s_me]), send_sem(3 * a + k),
                                      recv_sem(3 * a + k), (*chip, c)))
                got = rows(outs[a].at[:, s_k])
                arrivals.append(functools.partial(_remote, got, got, send_sem(3 * a + k), recv_sem(3 * a + k),
                                                  (*chip, c)))
        return starts, arrivals

    def operands():
        return [pair_sums[k] for k in names] + ([by_source[k] for k in names] if part else [])

    return _Task(operands, lambda: [_sds(pair_sums[k].shape, pair_sums[k].dtype) for k in names],
                 {n + a: a for a in range(n)} if part else {}, 3 * n, make,
                 lambda res: by_source.update(zip(names, res)))


def _rs_chip_sum(name, own, parts, chip):
    nw, ns, H, C = parts.shape

    def body(chip_ref, own_ref, *rest):
        prefs, o_ref = rest[:ns], rest[ns]
        me = chip_ref[0]
        own_v = own_ref[...].astype(F32)
        tot = None
        for s in range(ns):
            term = jnp.where(me == s, own_v, prefs[s][...].astype(F32))
            tot = term if tot is None else tot + term
        o_ref[...] = tot

    blk = lambda fn: pl.BlockSpec((None, None, H, C), fn)

    def part_spec(s):
        return blk(lambda w, chip_ref: (w, jnp.where(chip_ref[0] == s, (s + 1) % ns, s), 0, 0))

    return _pcall(
        body, name=name, grid=(nw,), num_prefetch=1,
        in_specs=[blk(lambda w, chip_ref: (w, chip_ref[0], 0, 0))] + [part_spec(s) for s in range(ns)],
        out_specs=pl.BlockSpec((None, H, C), lambda w, chip_ref: (w, 0, 0)),
        out_shape=_sds((nw, H, C), F32),
    )(chip, own, *([parts] * ns))


def _pair_gather_task(names, halves, sibling_halves):
    n = len(names)

    def make(ins, outs, send_sem, recv_sem):
        x, y, c, _ = _mesh_position()
        copies = [_remote(ins[a], outs[a], send_sem(a), recv_sem(a), (x, y, 1 - c)) for a in range(n)]
        return copies, [functools.partial(lambda cp: cp, cp) for cp in copies]

    return _Task(lambda: [halves[k] for k in names], lambda: [_sds(halves[k].shape, F32) for k in names],
                 {}, n, make, lambda res: sibling_halves.update(zip(names, res)))


def _small_allreduce(v):
    R, C = v.shape
    Q = R // N_CHIPS
    nsem = 1 + 2 * N_PEER_CHIPS

    def body(v_ref, o_ref, sib_buf, pair_buf, part_buf, send_sems, recv_sems):
        x, y, c, chips = _mesh_position()
        s_me = _chip_index(x, y)

        def quarter(ref, s):
            return ref.at[pl.ds(pl.multiple_of(s * Q, 8), Q)]

        def exchange(first_sem, src, dst_of, arrival_of):
            sends = [_remote(src(_chip_index(*chip)), dst_of(s_me), send_sems.at[first_sem + k],
                             recv_sems.at[first_sem + k], (*chip, c)) for k, chip in enumerate(chips)]
            for cp in sends:
                cp.start()
            for k, chip in enumerate(chips):
                got = arrival_of(_chip_index(*chip))
                _remote(got, got, send_sems.at[first_sem + k], recv_sems.at[first_sem + k], (*chip, c)).wait_recv()
            for cp in sends:
                cp.wait_send()

        swap = _remote(v_ref, sib_buf, send_sems.at[0], recv_sems.at[0], (x, y, 1 - c))
        swap.start()
        swap.wait()
        pair_buf[...] = v_ref[...] + sib_buf[...]
        exchange(1, lambda s_k: quarter(pair_buf, s_k), lambda s: part_buf.at[s], lambda s_k: part_buf.at[s_k])
        part_buf[s_me] = quarter(pair_buf, s_me)[...]
        o_ref[pl.ds(pl.multiple_of(s_me * Q, 8), Q), :] = ((part_buf[0] + part_buf[1]) + part_buf[2]) + part_buf[3]
        exchange(1 + N_PEER_CHIPS, lambda s_k: quarter(o_ref, s_me), lambda s: quarter(o_ref, s),
                 lambda s_k: quarter(o_ref, s_k))

    return pl.pallas_call(
        body, name="small_allreduce",
        in_specs=[VMEM_SPEC], out_specs=VMEM_SPEC, out_shape=_sds((R, C), F32),
        scratch_shapes=[pltpu.VMEM((R, C), F32), pltpu.VMEM((R, C), F32), pltpu.VMEM((N_CHIPS, Q, C), F32),
                        pltpu.SemaphoreType.DMA((nsem,)), pltpu.SemaphoreType.DMA((nsem,))],
        compiler_params=pltpu.CompilerParams(vmem_limit_bytes=VMEM_LIMIT_BYTES),
    )(v)


TRANSPOSED_WEIGHTS = ("ffn1_w1", "ffn1_w3", "ffn2_w1", "ffn2_w3")
SMALL_LAYOUT = [("ffn1_norm", 1), ("mix_norm", 1), ("ret_gn", 1), ("conv_b", 1), ("b_rgate", 1), ("b_igate", 1),
                ("lru_lambda", 1), ("xattn_norm", 1), ("mem_norm", 1), ("ffn2_norm", 1), ("final_norm", 1),
                ("b_branch_gate", 2), ("conv_w", CONV_TAPS), ("w_rgate", LRU_BLOCK), ("w_igate", LRU_BLOCK)]
SMALL_ROWS = 288
WEIGHT_ORDER = ["ffn1_norm", "ffn1_w1", "ffn1_w3", "ffn1_w2", "mix_norm", "w_in", "ret_gn", "w_ret_o", "conv_w",
                "conv_b", "w_rgate", "b_rgate", "w_igate", "b_igate", "lru_lambda", "w_lru_o", "w_branch_gate",
                "b_branch_gate", "w_out", "xattn_norm", "mem_norm", "w_xq", "w_xk", "w_xv", "w_xo", "ffn2_norm",
                "ffn2_w1", "ffn2_w3", "ffn2_w2", "final_norm"]


def _pack_small(parts):
    rows = [parts[name].reshape(n, D) for name, n in SMALL_LAYOUT]
    used = sum(n for _, n in SMALL_LAYOUT)
    rows.append(jnp.zeros((SMALL_ROWS - used, D), F32))
    return jnp.concatenate(rows, axis=0)


def _unpack_small(packed, shapes):
    out, r = {}, 0
    for name, n in SMALL_LAYOUT:
        out[name] = packed[r:r + n].reshape(shapes[name])
        r += n
    return out


def kernel(x, mem, ffn1_norm, ffn1_w1, ffn1_w3, ffn1_w2, mix_norm, w_in, ret_gn, w_ret_o, conv_w, conv_b, w_rgate, b_rgate, w_igate, b_igate, lru_lambda, w_lru_o, w_branch_gate, b_branch_gate, w_out, xattn_norm, mem_norm, w_xq, w_xk, w_xv, w_xo, ffn2_norm, ffn2_w1, ffn2_w3, ffn2_w2, final_norm, loss_target, m_ffn1_norm, m_ffn1_w1, m_ffn1_w3, m_ffn1_w2, m_mix_norm, m_w_in, m_ret_gn, m_w_ret_o, m_conv_w, m_conv_b, m_w_rgate, m_b_rgate, m_w_igate, m_b_igate, m_lru_lambda, m_w_lru_o, m_w_branch_gate, m_b_branch_gate, m_w_out, m_xattn_norm, m_mem_norm, m_w_xq, m_w_xk, m_w_xv, m_w_xo, m_ffn2_norm, m_ffn2_w1, m_ffn2_w3, m_ffn2_w2, m_final_norm, v_ffn1_norm, v_ffn1_w1, v_ffn1_w3, v_ffn1_w2, v_mix_norm, v_w_in, v_ret_gn, v_w_ret_o, v_conv_w, v_conv_b, v_w_rgate, v_b_rgate, v_w_igate, v_b_igate, v_lru_lambda, v_w_lru_o, v_w_branch_gate, v_b_branch_gate, v_w_out, v_xattn_norm, v_mem_norm, v_w_xq, v_w_xk, v_w_xv, v_w_xo, v_ffn2_norm, v_ffn2_w1, v_ffn2_w3, v_ffn2_w2, v_final_norm):
    given = dict(locals())
    w = {n: given[n] for n in WEIGHT_ORDER}
    mom = {n: given["m_" + n] for n in WEIGHT_ORDER}
    var = {n: given["v_" + n] for n in WEIGHT_ORDER}
    chip = _chip_index(lax.axis_index("x"), lax.axis_index("y"))
    core = lax.axis_index("c").astype(jnp.int32).reshape(1)

    chip_id = chip.astype(jnp.int32).reshape(1)
    sm = {n: w[n] for n in ["ffn1_norm", "mix_norm", "ret_gn", "conv_b", "b_rgate", "b_igate", "lru_lambda",
                            "xattn_norm", "mem_norm", "ffn2_norm", "b_branch_gate"]}
    sm["final_norm"] = w["final_norm"].reshape(1, D)
    sm["w_rgate"] = w["w_rgate"][0]
    sm["w_igate"] = w["w_igate"][0]

    local = lambda a, n: jnp.swapaxes(a[0], 0, 1) if n in TRANSPOSED_WEIGHTS else a[0]
    stack = lambda names: jnp.stack([local(w[n], n) for n in names], axis=0).astype(BF16)
    shard = {"col1": stack(["ffn1_w1", "ffn1_w3"]), "row2a": stack(["ffn1_w2"]), "win": stack(["w_in"]),
             "wbg": stack(["w_branch_gate"]), "sqA": stack(["w_ret_o", "w_lru_o", "w_out"]),
             "sqB": stack(["w_xq", "w_xk"]), "sqC": stack(["w_xv", "w_xo"]), "col2a": stack(["ffn2_w1"]), "col2b": stack(["ffn2_w3"]),
             "row2b": stack(["ffn2_w2"]), "conv": w["conv_w"]}
    gw, landed = {}, {}
    over_chips = lambda keys: _gather_chips_task({k: shard[k] for k in keys}, True, landed)
    to_sibling = lambda keys: _gather_sibling_task(keys, landed, gw)

    big, got, pair_sums, by_source, halves, sibling_halves, outs = {}, {}, {}, {}, {}, {}, {}
    pair_swap = lambda names: _pair_swap_task(names, big, got)
    exchange = lambda names, part=0, nparts=1: _chip_exchange_task(names, pair_sums, by_source, part, nparts)
    pair_gather = lambda names: _pair_gather_task(names, halves, sibling_halves)

    def pair_sum(names):
        for n in names:
            pair_sums[n] = _rs_pair_sum("rs_pair_sum_" + n, big[n], got[n], core)

    def chip_sum(names):
        for n in names:
            halves[n] = _rs_chip_sum("rs_chip_sum_" + n, pair_sums[n], by_source[n], chip_id)

    def adamw(names):
        for n in names:
            res = _adamw_halves("adamw_" + n, local(w[n], n), halves[n], sibling_halves[n], 0, local(mom[n], n),
                                local(var[n], n), core)
            outs[n] = tuple((jnp.swapaxes(r, 0, 1) if n in TRANSPOSED_WEIGHTS else r)[None] for r in res)

    do = lambda fn, names: functools.partial(fn, names)
    ffn2_grads = ["ffn2_w2", "ffn2_w1", "ffn2_w3"]
    xattn_grads = ["w_xo", "w_xq", "w_xk", "w_xv"]
    mix_out_grads = ["w_branch_gate", "w_out", "w_ret_o", "w_lru_o"]
    conv_gather = _gather_chips_task({"conv": shard["conv"]}, False, gw)
    half = lambda key, part: _gather_chips_task({key: shard[key]}, True, landed, part, 2)
    plan = _Plan()
    plan.tasks = {
        "ag_first_chips": [over_chips(["col1", "row2a"])],
        "ag_first_sibling": [to_sibling(["col1", "row2a"])],
        "ffn1_up": [over_chips(["win"])],
        "ffn1_down": [to_sibling(["win"]), over_chips(["wbg"]), conv_gather],
        "mix_in": [to_sibling(["wbg"]), over_chips(["sqA"])],
        "ret_fwd": [to_sibling(["sqA"]), over_chips(["col2a"])],
        "lru_gates_fwd": [to_sibling(["col2a"]), over_chips(["sqB"])],
        "lru_scan_fwd": [to_sibling(["sqB"]), over_chips(["sqC"])],
        "lru_out": [to_sibling(["sqC"])],
        "mix_gates": [half("col2b", 0)],
        "y_lru": [half("col2b", 1)],
        "mix_out": [to_sibling(["col2b"]), half("row2b", 0)],
        "xattn_fwd": [half("row2b", 1)],
        "xattn_out": [to_sibling(["row2b"])],
        "ffn2_dh": [pair_swap(ffn2_grads)],
        "xattn_bwd": [exchange(["ffn2_w2"], 0, 2)],
        "d_hq": [exchange(["ffn2_w2"], 1, 2)],
        "d_merged": [exchange(["ffn2_w1"], 0, 2), pair_swap(xattn_grads)],
        "dw_bg": [exchange(["w_xo"])],
        "ret_bwd": [exchange(["ffn2_w1"], 1, 2), exchange(["ffn2_w3"], 0, 2), pair_swap(mix_out_grads)],
        "lru_scan_bwd": [exchange(["ffn2_w3"], 1, 2)],
        "lru_gates_bwd": [exchange(["w_xq", "w_xk"]), pair_gather(ffn2_grads)],
        "dw_in": [exchange(["w_xv", "w_out"])],
        "d_h2": [exchange(["w_branch_gate", "w_ret_o", "w_lru_o"]), pair_swap(["w_in"]), pair_gather(xattn_grads)],
        "ffn1_bwd_mid": [exchange(["w_in"], 0, 2), pair_gather(mix_out_grads)],
        "ffn1_dw2": [exchange(["w_in"], 2, 4)],
        "ffn1_dw1": [exchange(["w_in"], 3, 4), pair_swap(["ffn1_w2"])],
        "ffn1_dw3": [exchange(["ffn1_w2"], 0, 2), pair_swap(["ffn1_w1"]), pair_gather(["w_in"])],
        "ffn1_dh": [exchange(["ffn1_w2"], 1, 2), exchange(["ffn1_w1"]), pair_swap(["ffn1_w3"])],
        "rs_last": [exchange(["ffn1_w3"]), pair_gather(["ffn1_w2"])],
        "rs_last_gather": [pair_gather(["ffn1_w1", "ffn1_w3"])],
    }
    plan.after = {
        "ffn2_dh": [do(pair_sum, ffn2_grads)],
        "d_merged": [do(pair_sum, xattn_grads)],
        "ret_bwd": [do(pair_sum, mix_out_grads)],
        "lru_scan_bwd": [do(chip_sum, ffn2_grads)],
        "lru_gates_bwd": [do(adamw, ffn2_grads)],
        "dw_in": [do(chip_sum, xattn_grads)],
        "d_h2": [do(chip_sum, mix_out_grads), do(pair_sum, ["w_in"]), do(adamw, xattn_grads)],
        "ffn1_bwd_mid": [do(adamw, mix_out_grads)],
        "ffn1_dw1": [do(chip_sum, ["w_in"]), do(pair_sum, ["ffn1_w2"])],
        "ffn1_dw3": [do(pair_sum, ["ffn1_w1"]), do(adamw, ["w_in"])],
        "ffn1_dh": [do(pair_sum, ["ffn1_w3"]), do(chip_sum, ["ffn1_w2"])],
        "rs_last": [do(chip_sum, ["ffn1_w1", "ffn1_w3"]), functools.partial(_comm_call, "rs_last_gather"),
                    do(adamw, ["ffn1_w2", "ffn1_w1", "ffn1_w3"])],
    }
    global _plan
    _plan = plan
    try:
        _comm_call("ag_first_chips")
        _comm_call("ag_first_sibling")
        loss_part, grad_x, small = _local_step(x[0], mem[0], loss_target[0], gw, sm, big)
        _comm_call("rs_last")
    finally:
        _plan = None
    assert not plan.tasks and not plan.after, (list(plan.tasks), list(plan.after))
    loss = lax.psum(loss_part[0, 0], ("x", "y", "c"))

    small_shapes = {n: w[n].shape for n, _ in SMALL_LAYOUT}
    small_shapes["conv_w"] = (CONV_TAPS, D)
    small_sum = _small_allreduce(_pack_small(small))
    conv_grad = lax.dynamic_slice(small_sum[13:13 + CONV_TAPS], (0, chip * SQ_BLK), (CONV_TAPS, SQ_BLK))
    small_w = {n: w[n] for n, _ in SMALL_LAYOUT}
    small_m = {n: mom[n] for n, _ in SMALL_LAYOUT}
    small_v = {n: var[n] for n, _ in SMALL_LAYOUT}
    pad_cols = lambda a: jnp.pad(a[0], ((0, 0), (0, D - SQ_BLK)))
    for dct in (small_w, small_m, small_v):
        dct["conv_w"] = pad_cols(dct["conv_w"])
    g_pack = lax.dynamic_update_slice(small_sum, jnp.pad(conv_grad, ((0, 0), (0, D - SQ_BLK))), (13, 0))
    d_pack, m_pack, v_pack = _adamw("adamw_small", _pack_small(small_w), g_pack, _pack_small(small_m),
                                    _pack_small(small_v))
    unpacked = [_unpack_small(p, small_shapes) for p in (g_pack, d_pack, m_pack, v_pack)]
    for n, _ in SMALL_LAYOUT:
        if n == "conv_w":
            outs[n] = tuple(u[n][:, :SQ_BLK][None] for u in unpacked)
        else:
            outs[n] = tuple(u[n] for u in unpacked)

    result = [loss, grad_x[None]]
    for k in range(4):
        result += [outs[n][k] for n in WEIGHT_ORDER]
    return tuple(result)
```

```python
import functools
import math

import numpy as np
import jax
import jax.numpy as jnp
from jax import lax
from jax.experimental import pallas as pl
from jax.experimental.pallas import tpu as pltpu

F32 = jnp.float32
BF16 = jnp.bfloat16
MESH = pl.DeviceIdType.MESH

D = 1024
EPS = 1e-6
RET_HEADS = 4
RET_DK = 128
RET_DV = 256
CHUNK = 128
ROPE_BASE = 10000.0
LRU_BLOCKS = 8
LRU_BLOCK = 128
CONV_TAPS = 4
LRU_C = 8.0
D_FF = 2816
X_HEADS = 4
X_HD = 256
N_CHIPS = 4
FF_BLK = D_FF // N_CHIPS
IN_BLK = 5120 // N_CHIPS
BG_BLK = 2048 // N_CHIPS
SQ_BLK = D // N_CHIPS

ADAM_LR = 0.001
ADAM_B1 = 0.9
ADAM_B2 = 0.999
ADAM_EPS = 1e-08
ADAM_WD = 0.01
ADAM_STEP = 10

VMEM_LIMIT_BYTES = 56 * 1024 * 1024
ROW_TILE = 512
WIDE_ROW_TILE = 1024
FFN_ROW_TILE = 256
DW_BLK = D_FF // 2
SCAN_TILE = 256

_DN = {
    "nn": (((1,), (0,)), ((), ())),
    "nt": (((1,), (1,)), ((), ())),
    "tn": (((0,), (0,)), ((), ())),
}


def _cparams(n_axes):
    return pltpu.CompilerParams(dimension_semantics=("arbitrary",) * n_axes,
                                vmem_limit_bytes=VMEM_LIMIT_BYTES)


def _dot(a, b, kind):
    if b.ndim == 3:
        b = b.reshape(b.shape[0] * b.shape[1], b.shape[2])
    return lax.dot_general(a.astype(BF16), b.astype(BF16), _DN[kind], preferred_element_type=F32)


def _sigmoid(x):
    return 1.0 / (1.0 + jnp.exp(-x))


def _log1p_pos(e):
    u = 1.0 + e
    return jnp.where(u == 1.0, e, jnp.log(u) * (e / jnp.where(u == 1.0, 1.0, u - 1.0)))


def _expm1(x):
    u = jnp.exp(x)
    lu = jnp.log(u)
    safe = jnp.where(lu == 0.0, 1.0, lu)
    return jnp.where(u == 1.0, x, (u - 1.0) * (x / safe))


def _softplus(z):
    return jnp.maximum(z, 0.0) + _log1p_pos(jnp.exp(-jnp.abs(z)))


_GELU_C = math.sqrt(2.0 / math.pi)


def _gelu_and_grad(x):
    x2 = x * x
    t = jnp.tanh(_GELU_C * (x + 0.044715 * x * x2))
    g = 0.5 * x * (1.0 + t)
    dg = 0.5 * (1.0 + t) + 0.5 * x * (1.0 - t * t) * (_GELU_C * (1.0 + 3.0 * 0.044715 * x2))
    return g, dg


def _rms_fwd(x, g):
    r = lax.rsqrt(jnp.mean(x * x, axis=-1, keepdims=True) + EPS)
    return (x * r) * g


def _rms_bwd(x, g, dh):
    r = lax.rsqrt(jnp.mean(x * x, axis=-1, keepdims=True) + EPS)
    n = x * r
    dyg = dh * g
    dx = r * (dyg - n * jnp.mean(dyg * n, axis=-1, keepdims=True))
    return dx, jnp.sum(dh * n, axis=0, keepdims=True)


def _accumulate(ref, val, first):
    @pl.when(first)
    def _():
        ref[...] = val

    @pl.when(jnp.logical_not(first))
    def _():
        ref[...] += val


def _sds(shape, dtype):
    return jax.ShapeDtypeStruct(tuple(shape), dtype)


def _spec(shape, fn):
    return pl.BlockSpec(tuple(shape), fn)


class _Task:
    def __init__(self, operands, out_shapes, aliases, nsem, make, finish):
        self.operands, self.out_shapes, self.aliases = operands, out_shapes, aliases
        self.nsem, self.make, self.finish = nsem, make, finish


class _Plan:
    def __init__(self):
        self.tasks, self.after = {}, {}


_plan = None


def _pcall(body, *, name, grid, in_specs, out_specs, out_shape, scratch_shapes=(), num_prefetch=0):
    single = not isinstance(out_shape, (list, tuple))
    out_shape = [out_shape] if single else list(out_shape)
    out_specs = [out_specs] if single else list(out_specs)
    in_specs = list(in_specs)
    scratch_shapes = list(scratch_shapes)
    tasks = _plan.tasks.pop(name, []) if _plan is not None else []
    after = _plan.after.pop(name, []) if _plan is not None else []
    nax = len(grid)

    def run(*operands):
        n_in = len(operands) - num_prefetch
        n_out = len(out_shape)
        t_ops = [t.operands() for t in tasks]
        t_outs = [t.out_shapes() for t in tasks]
        c_ops = [a for ops in t_ops for a in ops]
        c_outs = [s for outs in t_outs for s in outs]
        aliases = {}
        i0, o0 = num_prefetch + n_in, n_out
        for t, ops, outs in zip(tasks, t_ops, t_outs):
            for i_loc, o_loc in t.aliases.items():
                aliases[i0 + i_loc] = o0 + o_loc
            i0 += len(ops)
            o0 += len(outs)
        nsem = sum(t.nsem for t in tasks)

        def wrapped(*refs):
            p = num_prefetch
            pre, ins = refs[:p], refs[p:p + n_in]
            cins = refs[p + n_in:p + n_in + len(c_ops)]
            q = p + n_in + len(c_ops)
            outs, couts = refs[q:q + n_out], refs[q + n_out:q + n_out + len(c_outs)]
            q += n_out + len(c_outs)
            scr = refs[q:q + len(scratch_shapes)]

            def descriptors():
                send_sems, recv_sems = refs[q + len(scratch_shapes):]
                starts, arrivals = [], []
                ci = co = so = 0
                for t, ops, souts in zip(tasks, t_ops, t_outs):
                    s, a = t.make(cins[ci:ci + len(ops)], couts[co:co + len(souts)],
                                  functools.partial(lambda base, k: send_sems.at[base + k], so),
                                  functools.partial(lambda base, k: recv_sems.at[base + k], so))
                    starts += s
                    arrivals += a
                    ci, co, so = ci + len(ops), co + len(souts), so + t.nsem
                return starts, arrivals

            if tasks:
                ids = [pl.program_id(k) for k in range(nax)]
                first = functools.reduce(jnp.logical_and, [i == 0 for i in ids])
                last = functools.reduce(jnp.logical_and, [i == g - 1 for i, g in zip(ids, grid)])

                @pl.when(first)
                def _():
                    for cp in descriptors()[0]:
                        cp.start()

            body(*pre, *ins, *outs, *scr)

            if tasks:
                @pl.when(last)
                def _():
                    starts, arrivals = descriptors()
                    for arrival in arrivals:
                        arrival().wait_recv()
                    for cp in starts:
                        cp.wait_send()

        sems = [pltpu.SemaphoreType.DMA((nsem,)), pltpu.SemaphoreType.DMA((nsem,))] if tasks else []
        res = pl.pallas_call(
            wrapped, name=name,
            grid_spec=pltpu.PrefetchScalarGridSpec(
                num_scalar_prefetch=num_prefetch, grid=tuple(grid),
                in_specs=in_specs + [ANY_SPEC] * len(c_ops),
                out_specs=out_specs + [ANY_SPEC] * len(c_outs),
                scratch_shapes=scratch_shapes + sems),
            out_shape=out_shape + c_outs,
            input_output_aliases=aliases,
            compiler_params=_cparams(nax),
        )(*operands, *c_ops)
        co = n_out
        for t, souts in zip(tasks, t_outs):
            t.finish(res[co:co + len(souts)])
            co += len(souts)
        for fn in after:
            fn()
        return res[0] if single else list(res[:n_out])

    return run


def _comm_call(name):
    def body(o_ref):
        o_ref[...] = jnp.zeros_like(o_ref)

    _pcall(body, name=name, grid=(1,), in_specs=[], out_specs=_spec((8, 128), lambda i: (0, 0)),
           out_shape=_sds((8, 128), F32))()


def _gemm(name, terms, grid, outs, acc_shape, extras=(), epilogue=None):
    kinds = [t[4] for t in terms]
    nt, ne, no = len(terms), len(extras), len(outs)
    nred = grid[-1]
    nax = len(grid)

    def body(*refs):
        trefs = refs[:2 * nt]
        erefs = refs[2 * nt:2 * nt + ne]
        orefs = refs[2 * nt + ne:2 * nt + ne + no]
        ids = [pl.program_id(k) for k in range(nax)]
        tot = None
        for t in range(nt):
            d = _dot(trefs[2 * t][...], trefs[2 * t + 1][...], kinds[t])
            tot = d if tot is None else tot + d

        def finish(acc):
            if epilogue is None:
                orefs[0][...] = acc.astype(orefs[0].dtype)
            else:
                epilogue(acc, erefs, orefs, ids)

        if nred == 1:
            finish(tot)
        else:
            acc_ref = refs[-1]
            r = ids[-1]

            @pl.when(r == 0)
            def _():
                acc_ref[...] = tot

            @pl.when(r > 0)
            def _():
                acc_ref[...] += tot

            @pl.when(r == nred - 1)
            def _():
                finish(acc_ref[...])

    operands, in_specs = [], []
    for a, a_spec, b, b_spec, _ in terms:
        operands += [a, b]
        in_specs += [a_spec, b_spec]
    for e, e_spec in extras:
        operands.append(e)
        in_specs.append(e_spec)
    scratch = [pltpu.VMEM(tuple(acc_shape), F32)] if nred > 1 else []
    return _pcall(body, name=name, grid=tuple(grid), in_specs=in_specs, out_specs=[o[1] for o in outs],
                  out_shape=[o[0] for o in outs], scratch_shapes=scratch)(*operands)


def _rowwise(name, fn, ins, outs, grid):
    ni = len(ins)
    nax = len(grid)

    def body(*refs):
        ids = [pl.program_id(k) for k in range(nax)]
        fn(refs[:ni], refs[ni:], ids)

    return _pcall(body, name=name, grid=tuple(grid), in_specs=[i[1] for i in ins],
                  out_specs=[o[1] for o in outs], out_shape=[o[0] for o in outs])(*[i[0] for i in ins])


def _ffn_up(name, h, w1buf, w1_idx, w3buf, w3_idx):
    T = h.shape[0]
    tm = min(FFN_ROW_TILE, T)

    def body(h_ref, w1_ref, w3_ref, a_ref, b_ref, s_ref):
        hv = h_ref[...]
        a = _dot(hv, w1_ref[...], "nt")
        b = _dot(hv, w3_ref[...], "nt")
        a_ref[...] = a.astype(BF16)
        b_ref[...] = b.astype(BF16)
        s_ref[...] = ((a * _sigmoid(a)) * b).astype(BF16)

    blk = _spec((tm, D_FF), lambda i: (i, 0))
    return _pcall(
        body, name=name, grid=(T // tm,),
        in_specs=[_spec((tm, D), lambda i: (i, 0)),
                  _spec((N_CHIPS, None, FF_BLK, D), lambda i: (0, w1_idx, 0, 0)),
                  _spec((N_CHIPS, None, FF_BLK, D), lambda i: (0, w3_idx, 0, 0))],
        out_specs=[blk, blk, blk],
        out_shape=[_sds((T, D_FF), BF16)] * 3,
    )(h, w1buf, w3buf)


def _ffn_down(name, s, wrow2, w2_idx, x_res, g_next=None):
    T = x_res.shape[0]
    tm = min(ROW_TILE, T)
    row = lambda i, j, r: (i, 0)

    def epilogue(acc, erefs, orefs, ids):
        xo = erefs[0][...] + 0.5 * acc
        orefs[0][...] = xo
        if g_next is not None:
            orefs[1][...] = _rms_fwd(xo, erefs[1][...]).astype(BF16)

    extras = [(x_res, _spec((tm, D), row))]
    outs = [(_sds((T, D), F32), _spec((tm, D), row))]
    if g_next is not None:
        extras.append((g_next, _spec((1, D), lambda i, j, r: (0, 0))))
        outs.append((_sds((T, D), BF16), _spec((tm, D), row)))
    return _gemm(
        name,
        [(s, _spec((tm, D_FF), row),
          wrow2, _spec((N_CHIPS, None, FF_BLK, D), lambda i, j, r: (0, w2_idx, 0, 0)), "nn")],
        (T // tm, 1, 1), outs, (tm, D), extras, epilogue)


def _ffn_bwd_mid(name, dx, wrow2, w2_idx, a, b):
    T = dx.shape[0]
    tm = min(FFN_ROW_TILE, T)

    def body(dx_ref, w2_ref, a_ref, b_ref, dab_ref):
        ds = _dot(0.5 * dx_ref[...], w2_ref[...], "nt")
        av = a_ref[...].astype(F32)
        sg = _sigmoid(av)
        dab_ref[0] = (ds * b_ref[...].astype(F32) * (sg * (1.0 + av * (1.0 - sg)))).astype(BF16)
        dab_ref[1] = (ds * (av * sg)).astype(BF16)

    blk = _spec((tm, D_FF), lambda i: (i, 0))
    return _pcall(
        body, name=name, grid=(T // tm,),
        in_specs=[_spec((tm, D), lambda i: (i, 0)),
                  _spec((N_CHIPS, None, FF_BLK, D), lambda i: (0, w2_idx, 0, 0)),
                  blk, blk],
        out_specs=_spec((2, tm, D_FF), lambda i: (0, i, 0)),
        out_shape=_sds((2, T, D_FF), BF16),
    )(dx, wrow2, a, b)


def _rms_bwd_epilogue(acc, erefs, orefs, ids):
    dx, dgp = _rms_bwd(erefs[0][...], erefs[1][...], acc)
    orefs[0][...] = dx + erefs[2][...]
    _accumulate(orefs[1], dgp, ids[0] == 0)


def _rms_bwd_io(x, g, dres, T, tm):
    row = lambda i, j, r: (i, 0)
    vec = lambda i, j, r: (0, 0)
    extras = [(x, _spec((tm, D), row)), (g, _spec((1, D), vec)), (dres, _spec((tm, D), row))]
    outs = [(_sds((T, D), F32), _spec((tm, D), row)), (_sds((1, D), F32), _spec((1, D), vec))]
    return extras, outs


def _ffn_bwd(tag, dx_out, h, a, b, s, w1buf, w1_idx, w3buf, w3_idx, wrow2, w2_idx, x_in, g, big):
    T = dx_out.shape[0]
    dab = _ffn_bwd_mid(tag + "_bwd_mid", dx_out, wrow2, w2_idx, a, b)

    def half_scale(acc, erefs, orefs, ids):
        orefs[0][...] = 0.5 * acc

    dw_grid = (D_FF // DW_BLK, 1, 1)
    dw_out = [(_sds((D_FF, D), F32), _spec((DW_BLK, D), lambda j, n, r: (j, 0)))]
    tokens = _spec((T, D), lambda j, n, r: (0, 0))
    big[tag + "_w2"] = _gemm(
        tag + "_dw2", [(s, _spec((T, DW_BLK), lambda j, n, r: (0, j)), dx_out, tokens, "tn")],
        dw_grid, dw_out, (DW_BLK, D), (), half_scale)[0].reshape(1, N_CHIPS, FF_BLK, D)
    for widx, wname in ((0, "_w1"), (1, "_w3")):
        big[tag + wname] = _gemm(
            tag + "_d" + wname[1:],
            [(dab, _spec((None, T, DW_BLK), functools.partial(lambda w, j, n, r: (w, 0, j), widx)), h, tokens, "tn")],
            dw_grid, dw_out, (DW_BLK, D))[0].reshape(1, N_CHIPS, FF_BLK, D)
    tm = min(FFN_ROW_TILE, T)
    extras, outs = _rms_bwd_io(x_in, g, dx_out, T, tm)
    whole = lambda idx: _spec((N_CHIPS, None, FF_BLK, D), lambda i, j, r: (0, idx, 0, 0))
    dx_in, dg = _gemm(
        tag + "_dh",
        [(dab, _spec((None, tm, D_FF), lambda i, j, r: (0, i, 0)), w1buf, whole(w1_idx), "nn"),
         (dab, _spec((None, tm, D_FF), lambda i, j, r: (1, i, 0)), w3buf, whole(w3_idx), "nn")],
        (T // tm, 1, 1), outs, (tm, D), extras, _rms_bwd_epilogue)
    return dx_in, dg


def _proj_sq(name, a, wsq, idx, kind, out_dtype=F32, extras=(), epilogue=None, outs=None):
    M = a.shape[0]
    tm = min(ROW_TILE, M)
    if outs is None:
        outs = [(_sds((M, D), out_dtype), _spec((tm, D), lambda i, j, r: (i, 0)))]
    return _gemm(
        name,
        [(a, _spec((tm, D), lambda i, j, r: (i, 0)),
          wsq, _spec((N_CHIPS, None, SQ_BLK, D), lambda i, j, r: (0, idx, 0, 0)), kind)],
        (M // tm, 1, 1), outs, (tm, D), extras, epilogue)


def _dw_sq(name, a, b):
    M = a.shape[0]
    tk = M
    whole = _gemm(
        name,
        [(a, _spec((tk, D), lambda i, j, r: (r, 0)), b, _spec((tk, D), lambda i, j, r: (r, 0)), "tn")],
        (1, 1, M // tk),
        [(_sds((D, D), F32), _spec((D, D), lambda i, j, r: (0, 0)))],
        (D, D))[0]
    return whole.reshape(N_CHIPS, SQ_BLK, D)


def _retention_constants(T):
    pos = jnp.arange(T, dtype=F32)
    inv_freq = ROPE_BASE ** (-jnp.arange(0, RET_DK, 2, dtype=F32) / RET_DK)
    ang = pos[:, None] * inv_freq[None, :]
    cosf = jnp.concatenate([jnp.cos(ang), jnp.cos(ang)], axis=1)
    sins = jnp.concatenate([-jnp.sin(ang), jnp.sin(ang)], axis=1)
    lg = jnp.log(1.0 - 2.0 ** (-5.0 - jnp.arange(RET_HEADS, dtype=F32)))
    p = jnp.arange(CHUNK, dtype=F32)
    rel = p[:, None] - p[None, :]
    dmat = jnp.where(rel[None] >= 0, jnp.exp(rel[None] * lg[:, None, None]), 0.0)
    kd = jnp.exp((CHUNK - 1.0 - p)[None, :] * lg[:, None])[:, :, None]
    qd = jnp.exp((p + 1.0)[None, :] * lg[:, None])[:, :, None]
    cd = jnp.exp(CHUNK * lg)[:, None, None]
    return cosf, sins, dmat, kd, qd, cd


def _rot(t, cosv, sinv):
    return t * cosv + pltpu.roll(t, RET_DK // 2, 1) * sinv


def _unrot(t, cosv, sinv):
    return t * cosv - pltpu.roll(t, RET_DK // 2, 1) * sinv


def _ret_const_specs(cm):
    whole = lambda shape: _spec(shape, lambda c: (0,) * len(shape))
    return [
        _spec((CHUNK, RET_DK), lambda c: (cm(c), 0)),
        _spec((CHUNK, RET_DK), lambda c: (cm(c), 0)),
        whole((RET_HEADS, CHUNK, CHUNK)), whole((RET_HEADS, CHUNK, 1)), whole((RET_HEADS, CHUNK, 1)),
        whole((RET_HEADS, 1, 1)),
    ]


def _head(h, width):
    return slice(h * width, (h + 1) * width)


def _ret_fwd(u, consts, ret_gn):
    T = u.shape[0]
    nC = T // CHUNK
    kscale = RET_DK ** -0.5

    def body(q_ref, k_ref, v_ref, g_ref, cos_ref, sin_ref, dm_ref, kd_ref, qd_ref, cd_ref, gn_ref,
             qr_ref, kr_ref, ret_ref, yr_ref, st_ref, state):
        @pl.when(pl.program_id(0) == 0)
        def _():
            state[...] = jnp.zeros_like(state)

        cosv, sinv = cos_ref[...], sin_ref[...]
        for h in range(RET_HEADS):
            hk, hv = _head(h, RET_DK), _head(h, RET_DV)
            q = _rot(q_ref[:, hk], cosv, sinv)
            k = _rot(k_ref[:, hk], cosv, sinv) * kscale
            v = v_ref[:, hv]
            qr_ref[:, hk] = q
            kr_ref[:, hk] = k
            prev = state[h]
            st_ref[h] = prev
            s = _dot(q, k, "nt") * dm_ref[h]
            ret = _dot(s, v, "nn") + _dot(q, prev, "nn") * qd_ref[h]
            state[h] = cd_ref[h] * prev + _dot(k * kd_ref[h], v, "tn")
            ret_ref[:, hv] = ret
            mu = jnp.mean(ret, axis=-1, keepdims=True)
            xc = ret - mu
            yn = xc * lax.rsqrt(jnp.mean(xc * xc, axis=-1, keepdims=True) + EPS)
            g = g_ref[:, hv]
            yr_ref[:, hv] = ((g * _sigmoid(g)) * (yn * gn_ref[:, hv])).astype(BF16)

    cm = lambda c: c
    qk_w, v_w = RET_HEADS * RET_DK, RET_HEADS * RET_DV
    in_specs = [
        _spec((CHUNK, qk_w), lambda c: (c, 0)), _spec((CHUNK, qk_w), lambda c: (c, 1)),
        _spec((CHUNK, v_w), lambda c: (c, 1)), _spec((CHUNK, v_w), lambda c: (c, 2)),
    ] + _ret_const_specs(cm) + [_spec((1, v_w), lambda c: (0, 0))]
    qk_out = _spec((CHUNK, qk_w), lambda c: (c, 0))
    v_out = _spec((CHUNK, v_w), lambda c: (c, 0))
    return _pcall(
        body, name="ret_fwd", grid=(nC,),
        in_specs=in_specs,
        out_specs=[qk_out, qk_out, v_out, v_out,
                   _spec((RET_HEADS, None, RET_DK, RET_DV), lambda c: (0, c, 0, 0))],
        out_shape=[_sds((T, qk_w), F32), _sds((T, qk_w), F32), _sds((T, v_w), F32), _sds((T, v_w), BF16),
                   _sds((RET_HEADS, nC, RET_DK, RET_DV), F32)],
        scratch_shapes=[pltpu.VMEM((RET_HEADS, RET_DK, RET_DV), F32)],
    )(u, u, u, u, *consts, ret_gn)


def _ret_bwd(dyr, ret, u, qr, kr, states, consts, ret_gn):
    T = u.shape[0]
    nC = T // CHUNK
    kscale = RET_DK ** -0.5

    def body(dyr_ref, ret_ref, g_ref, q_ref, k_ref, v_ref, st_ref,
             cos_ref, sin_ref, dm_ref, kd_ref, qd_ref, cd_ref, gn_ref,
             dq_ref, dk_ref, dv_ref, dg_ref, dgn_ref, gstate):
        first = pl.program_id(0) == 0

        @pl.when(first)
        def _():
            gstate[...] = jnp.zeros_like(gstate)

        cosv, sinv = cos_ref[...], sin_ref[...]
        dgn_parts = []
        for h in range(RET_HEADS):
            hk, hv = _head(h, RET_DK), _head(h, RET_DV)
            ret = ret_ref[:, hv]
            mu = jnp.mean(ret, axis=-1, keepdims=True)
            xc = ret - mu
            rs = lax.rsqrt(jnp.mean(xc * xc, axis=-1, keepdims=True) + EPS)
            yn = xc * rs
            gn = gn_ref[:, hv]
            g = g_ref[:, hv]
            sg = _sigmoid(g)
            dyr_v = dyr_ref[:, hv]
            dretn = dyr_v * (g * sg)
            dg_ref[:, hv] = (dyr_v * (yn * gn) * (sg * (1.0 + g * (1.0 - sg)))).astype(BF16)
            dgn_parts.append(jnp.sum(dretn * yn, axis=0, keepdims=True))
            dyn = dretn * gn
            d_o = rs * (dyn - jnp.mean(dyn, axis=-1, keepdims=True)
                        - yn * jnp.mean(dyn * yn, axis=-1, keepdims=True))

            q, k, v = q_ref[:, hk], k_ref[:, hk], v_ref[:, hv]
            dmat, kd, qd = dm_ref[h], kd_ref[h], qd_ref[h]
            prev = st_ref[h]
            gnext = gstate[h]
            s = _dot(q, k, "nt") * dmat
            ds = _dot(d_o, v, "nt") * dmat
            doq = d_o * qd
            dq = _dot(ds, k, "nn") + _dot(doq, prev, "nt")
            dk = _dot(ds, q, "tn") + _dot(v, gnext, "nt") * kd
            dv = _dot(s, d_o, "tn") + _dot(k * kd, gnext, "nn")
            gstate[h] = cd_ref[h] * gnext + _dot(q, doq, "tn")
            dq_ref[:, hk] = _unrot(dq, cosv, sinv).astype(BF16)
            dk_ref[:, hk] = _unrot(dk * kscale, cosv, sinv).astype(BF16)
            dv_ref[:, hv] = dv.astype(BF16)
        _accumulate(dgn_ref, jnp.concatenate(dgn_parts, axis=1), first)

    cm = lambda c: nC - 1 - c
    qk_w, v_w = RET_HEADS * RET_DK, RET_HEADS * RET_DV
    vspec = lambda blk: _spec((CHUNK, v_w), lambda c: (cm(c), blk))
    qspec = _spec((CHUNK, qk_w), lambda c: (cm(c), 0))
    in_specs = [vspec(0), vspec(0), vspec(2), qspec, qspec, vspec(1),
                _spec((RET_HEADS, None, RET_DK, RET_DV), lambda c: (0, cm(c), 0, 0)),
                ] + _ret_const_specs(cm) + [_spec((1, v_w), lambda c: (0, 0))]
    return _pcall(
        body, name="ret_bwd", grid=(nC,),
        in_specs=in_specs,
        out_specs=[qspec, qspec, vspec(0), vspec(0), _spec((1, v_w), lambda c: (0, 0))],
        out_shape=[_sds((T, qk_w), BF16), _sds((T, qk_w), BF16), _sds((T, v_w), BF16), _sds((T, v_w), BF16),
                   _sds((1, v_w), F32)],
        scratch_shapes=[pltpu.VMEM((RET_HEADS, RET_DK, RET_DV), F32)],
    )(dyr, ret, u, qr, kr, u, states, *consts, ret_gn)


def _shift_down(x, s):
    rows = lax.broadcasted_iota(jnp.int32, x.shape, 0)
    return jnp.where(rows >= s, pltpu.roll(x, s, 0), 0.0)


def _shift_up(x, s):
    n = x.shape[0]
    rows = lax.broadcasted_iota(jnp.int32, x.shape, 0)
    return jnp.where(rows < n - s, pltpu.roll(x, n - s, 0), 0.0)


def _lru_specs(T):
    col = lambda off: _spec((T, LRU_BLOCK), lambda g: (0, off + g))
    vec = _spec((1, LRU_BLOCK), lambda g: (0, g))
    wblk = _spec((None, LRU_BLOCK, LRU_BLOCK), lambda g: (g, 0, 0))
    cw = _spec((CONV_TAPS, LRU_BLOCK), lambda g: (0, g))
    return col, vec, wblk, cw


def _lru_gates_fwd(u, conv_w, conv_b, w_r, b_r, w_i, b_i, lam):
    T = u.shape[0]
    col, vec, wblk, cw = _lru_specs(T)

    def body(x_ref, cw_ref, cb_ref, wr_ref, br_ref, wi_ref, bi_ref, lam_ref,
             xc_ref, r_ref, i_ref, a_ref, bx_ref):
        x = x_ref[...]
        w = cw_ref[...]
        xc = (_shift_down(x, 3) * w[0:1] + _shift_down(x, 2) * w[1:2] + _shift_down(x, 1) * w[2:3]
              + x * w[3:4] + cb_ref[...])
        r = _sigmoid(_dot(xc, wr_ref[...], "nn") + br_ref[...])
        i = _sigmoid(_dot(xc, wi_ref[...], "nn") + bi_ref[...])
        la = (-LRU_C) * r * _softplus(-lam_ref[...])
        xc_ref[...] = xc
        r_ref[...] = r
        i_ref[...] = i
        a_ref[...] = jnp.exp(la)
        bx_ref[...] = jnp.sqrt(-_expm1(2.0 * la)) * (i * xc)

    out = col(0)
    return _pcall(
        body, name="lru_gates_fwd", grid=(LRU_BLOCKS,),
        in_specs=[col(24), cw, vec, wblk, vec, wblk, vec, vec],
        out_specs=[out] * 5,
        out_shape=[_sds((T, D), F32)] * 5,
    )(u, conv_w, conv_b, w_r, b_r, w_i, b_i, lam)


def _lru_scan(name, a3, b3, reverse):
    T = a3.shape[0]
    nt = T // SCAN_TILE
    unroll = 8

    def body(a_ref, b_ref, o_ref, carry):
        @pl.when(pl.program_id(0) == 0)
        def _():
            carry[...] = jnp.zeros_like(carry)

        if not reverse:
            def step(t, h):
                h = a_ref[t] * h + b_ref[t]
                o_ref[t] = h
                return h
        else:
            def step(k, c):
                t = SCAN_TILE - 1 - k
                l = b_ref[t] + c
                o_ref[t] = l
                return a_ref[t] * l
        carry[...] = lax.fori_loop(0, SCAN_TILE, step, carry[...], unroll=unroll)

    idx = (lambda i: (nt - 1 - i, 0, 0)) if reverse else (lambda i: (i, 0, 0))
    blk = _spec((SCAN_TILE, LRU_BLOCKS, LRU_BLOCK), idx)
    return _pcall(
        body, name=name, grid=(nt,),
        in_specs=[blk, blk], out_specs=blk,
        out_shape=_sds((T, LRU_BLOCKS, LRU_BLOCK), F32),
        scratch_shapes=[pltpu.VMEM((LRU_BLOCKS, LRU_BLOCK), F32)],
    )(a3, b3)


def _lru_gates_bwd(lmb, hl, a, r, i, xc, u, conv_w, w_r, w_i, lam):
    T = u.shape[0]
    col, vec, wblk, cw = _lru_specs(T)

    def body(l_ref, h_ref, a_ref, r_ref, i_ref, xc_ref, x_ref, cw_ref, wr_ref, wi_ref, lam_ref,
             dx_ref, dwr_ref, dwi_ref, dvec_ref, dcw_ref):
        l = l_ref[...]
        av, rv, iv, xc = a_ref[...], r_ref[...], i_ref[...], xc_ref[...]
        lam_v = lam_ref[...]
        sp = _softplus(-lam_v)
        la = (-LRU_C) * rv * sp
        mult = jnp.sqrt(-_expm1(2.0 * la))
        da = l * _shift_down(h_ref[...], 1)
        dmult = l * (iv * xc)
        di = l * mult * xc
        dxc = l * mult * iv
        dla = da * av - dmult * (av * av) / mult
        dzr = (dla * ((-LRU_C) * sp)) * rv * (1.0 - rv)
        dzi = di * iv * (1.0 - iv)
        dsp = jnp.sum(dla * ((-LRU_C) * rv), axis=0, keepdims=True)
        dlam = dsp * (-_sigmoid(-lam_v))
        dwr_ref[...] = _dot(xc, dzr, "tn")
        dwi_ref[...] = _dot(xc, dzi, "tn")
        dxc = dxc + _dot(dzr, wr_ref[...], "nt") + _dot(dzi, wi_ref[...], "nt")
        x = x_ref[...]
        w = cw_ref[...]
        dx = (dxc * w[3:4] + _shift_up(dxc, 1) * w[2:3] + _shift_up(dxc, 2) * w[1:2]
              + _shift_up(dxc, 3) * w[0:1])
        dx_ref[...] = dx.astype(BF16)
        dvec_ref[...] = jnp.concatenate(
            [jnp.sum(dzr, axis=0, keepdims=True), jnp.sum(dzi, axis=0, keepdims=True), dlam,
             jnp.sum(dxc, axis=0, keepdims=True)], axis=0)
        dcw_ref[...] = jnp.concatenate(
            [jnp.sum(dxc * _shift_down(x, 3 - tap), axis=0, keepdims=True) if tap < 3
             else jnp.sum(dxc * x, axis=0, keepdims=True) for tap in range(CONV_TAPS)], axis=0)

    c0 = col(0)
    return _pcall(
        body, name="lru_gates_bwd", grid=(LRU_BLOCKS,),
        in_specs=[c0, c0, c0, c0, c0, c0, col(24), cw, wblk, wblk, vec],
        out_specs=[c0, wblk, wblk, cw, cw],
        out_shape=[_sds((T, D), BF16), _sds((LRU_BLOCKS, LRU_BLOCK, LRU_BLOCK), F32),
                   _sds((LRU_BLOCKS, LRU_BLOCK, LRU_BLOCK), F32), _sds((4, D), F32), _sds((CONV_TAPS, D), F32)],
    )(lmb, hl, a, r, i, xc, u, conv_w, w_r, w_i, lam)


def _xattn_probs(q, k):
    sc = _dot(q, k, "nt") * (X_HD ** -0.5)
    e = jnp.exp(sc - jnp.max(sc, axis=-1, keepdims=True))
    return e / jnp.sum(e, axis=-1, keepdims=True)


def _xattn_fwd(xq, xk, xv):
    T = xq.shape[0]
    tq = ROW_TILE
    M = xk.shape[0]

    def body(q_ref, k_ref, v_ref, o_ref):
        p = _xattn_probs(q_ref[...], k_ref[...])
        o_ref[...] = _dot(p, v_ref[...], "nn").astype(BF16)

    qs = _spec((tq, X_HD), lambda h, i: (i, h))
    kv = _spec((M, X_HD), lambda h, i: (0, h))
    return _pcall(
        body, name="xattn_fwd", grid=(X_HEADS, T // tq),
        in_specs=[qs, kv, kv], out_specs=qs, out_shape=_sds((T, D), BF16),
    )(xq, xk, xv)


def _xattn_bwd(xq, xk, xv, dxo):
    T = xq.shape[0]
    tq = ROW_TILE
    M = xk.shape[0]

    def body(q_ref, k_ref, v_ref, do_ref, dq_ref, dk_ref, dv_ref):
        first = pl.program_id(1) == 0
        q, k, v, do = q_ref[...], k_ref[...], v_ref[...], do_ref[...]
        p = _xattn_probs(q, k)
        dp = _dot(do, v, "nt")
        ds = p * (dp - jnp.sum(dp * p, axis=-1, keepdims=True)) * (X_HD ** -0.5)
        dq_ref[...] = _dot(ds, k, "nn").astype(BF16)
        _accumulate(dk_ref, _dot(ds, q, "tn"), first)
        _accumulate(dv_ref, _dot(p, do, "tn"), first)

    qs = _spec((tq, X_HD), lambda h, i: (i, h))
    kv = _spec((M, X_HD), lambda h, i: (0, h))
    return _pcall(
        body, name="xattn_bwd", grid=(X_HEADS, T // tq),
        in_specs=[qs, kv, kv, qs], out_specs=[qs, kv, kv],
        out_shape=[_sds((T, D), BF16), _sds((M, D), F32), _sds((M, D), F32)],
    )(xq, xk, xv, dxo)


def _final_loss(x, g, tgt):
    T = x.shape[0]
    tm = ROW_TILE

    def fn(irefs, orefs, ids):
        xv, gv = irefs[0][...], irefs[1][...]
        err = _rms_fwd(xv, gv) - irefs[2][...]
        lp = 0.5 * jnp.sum(jnp.mean(err * err, axis=-1, keepdims=True), axis=0, keepdims=True)
        first = ids[0] == 0
        _accumulate(orefs[0], jnp.broadcast_to(lp, (1, 128)), first)
        dx, dgp = _rms_bwd(xv, gv, err * (1.0 / D))
        orefs[1][...] = dx
        _accumulate(orefs[2], dgp, first)

    row = _spec((tm, D), lambda i: (i, 0))
    vec = _spec((1, D), lambda i: (0, 0))
    return _rowwise(
        "final_loss", fn, [(x, row), (g, vec), (tgt, row)],
        [(_sds((1, 128), F32), _spec((1, 128), lambda i: (0, 0))), (_sds((T, D), F32), row),
         (_sds((1, D), F32), vec)],
        (T // tm,))


def _adamw(name, w, g, m, v):
    R, C = w.shape
    tr = R
    for cand in (512, 352, 256):
        if R % cand == 0:
            tr = cand
            break

    def fn(irefs, orefs, ids):
        delta, mn, vn = _adamw_update(*(r[...] for r in irefs))
        orefs[0][...] = delta
        orefs[1][...] = mn
        orefs[2][...] = vn

    blk = _spec((tr, C), lambda i: (i, 0))
    return _rowwise(name, fn, [(w, blk), (g, blk), (m, blk), (v, blk)],
                    [(_sds((R, C), F32), blk)] * 3, (R // tr,))


def _adamw_update(wv, gv, mv, vv):
    c1 = 1.0 - ADAM_B1 ** ADAM_STEP
    c2 = 1.0 - ADAM_B2 ** ADAM_STEP
    mn = ADAM_B1 * mv + (1.0 - ADAM_B1) * gv
    vn = ADAM_B2 * vv + (1.0 - ADAM_B2) * (gv * gv)
    delta = -ADAM_LR * ((mn / c1) / (jnp.sqrt(vn / c2) + ADAM_EPS) + ADAM_WD * wv)
    return delta, mn, vn


def _adamw_halves(name, w, mine, theirs, widx, m, v, core):
    R, C = w.shape
    H = R // 2
    tr = H
    while tr * C * 4 > (1 << 20) and tr % 16 == 0:
        tr //= 2
    nb = H // tr

    def body(core_ref, w_ref, mine_ref, theirs_ref, m_ref, v_ref, g_out, d_out, m_out, v_out):
        gv = jnp.where(pl.program_id(0) == core_ref[0], mine_ref[...], theirs_ref[...])
        delta, mn, vn = _adamw_update(w_ref[...], gv, m_ref[...], v_ref[...])
        g_out[...] = gv
        d_out[...] = delta
        m_out[...] = mn
        v_out[...] = vn

    full = pl.BlockSpec((tr, C), lambda h, i, core_ref: (h * nb + i, 0))
    mine_spec = pl.BlockSpec((None, tr, C), lambda h, i, core_ref: (widx, jnp.where(h == core_ref[0], i, 0), 0))
    theirs_spec = pl.BlockSpec((None, tr, C), lambda h, i, core_ref: (widx, jnp.where(h == core_ref[0], 0, i), 0))
    return _pcall(
        body, name=name, grid=(2, nb), num_prefetch=1,
        in_specs=[full, mine_spec, theirs_spec, full, full], out_specs=[full] * 4,
        out_shape=[_sds((R, C), F32)] * 4,
    )(core, w, mine, theirs, m, v)


def _rmsnorm(name, x, g):
    M = x.shape[0]
    tm = min(ROW_TILE, M)

    def fn(irefs, orefs, ids):
        orefs[0][...] = _rms_fwd(irefs[0][...], irefs[1][...]).astype(BF16)

    row = _spec((tm, D), lambda i: (i, 0))
    return _rowwise(name, fn, [(x, row), (g, _spec((1, D), lambda i: (0, 0)))],
                    [(_sds((M, D), BF16), row)], (M // tm,))[0]


WEIGHT_AT = {
    "ffn1_w1": ("col1", 0), "ffn1_w3": ("col1", 1), "ffn1_w2": ("row2a", 0),
    "w_ret_o": ("sqA", 0), "w_lru_o": ("sqA", 1), "w_out": ("sqA", 2),
    "w_xq": ("sqB", 0), "w_xk": ("sqB", 1), "w_xv": ("sqC", 0), "w_xo": ("sqC", 1),
    "ffn2_w1": ("col2a", 0), "ffn2_w3": ("col2b", 0), "ffn2_w2": ("row2b", 0),
}


def _local_step(x, mem, tgt, gw, sm, big):
    T = x.shape[0]
    tm = ROW_TILE

    def wt(name):
        key, idx = WEIGHT_AT[name]
        return gw[key], idx

    row3 = lambda i, j, r: (i, 0)
    vec3 = lambda i, j, r: (0, 0)
    rowD = _spec((tm, D), row3)
    vecD = _spec((1, D), vec3)

    def residual_norm(acc, erefs, orefs, ids):
        xo = erefs[0][...] + acc
        orefs[0][...] = xo
        orefs[1][...] = _rms_fwd(xo, erefs[1][...]).astype(BF16)

    def res_norm_io(x_res, g):
        return ([(x_res, rowD), (g, vecD)],
                [(_sds((T, D), F32), rowD), (_sds((T, D), BF16), rowD)])

    h1 = _rmsnorm("ffn1_norm", x, sm["ffn1_norm"])
    a1, b1, s1 = _ffn_up("ffn1_up", h1, *wt("ffn1_w1"), *wt("ffn1_w3"))
    x1, h2 = _ffn_down("ffn1_down", s1, *wt("ffn1_w2"), x, sm["mix_norm"])

    tw = min(WIDE_ROW_TILE, T)
    wideD = _spec((tw, D), row3)
    u = _gemm(
        "mix_in",
        [(h2, wideD, gw["win"], _spec((None, None, IN_BLK, D), lambda i, j, r: (j, 0, 0, 0)), "nt")],
        (T // tw, N_CHIPS, 1),
        [(_sds((T, 5120), F32), _spec((tw, IN_BLK), lambda i, j, r: (i, j)))], (tw, IN_BLK))[0]

    consts = _retention_constants(T)
    qr, kr, ret, yr, states = _ret_fwd(u, consts, sm["ret_gn"])

    conv_w = gw["conv"][:, 0].transpose(1, 0, 2).reshape(CONV_TAPS, D)
    xc, rg, ig, av, bx = _lru_gates_fwd(u, conv_w, sm["conv_b"], sm["w_rgate"], sm["b_rgate"],
                                        sm["w_igate"], sm["b_igate"], sm["lru_lambda"])
    a3 = av.reshape(T, LRU_BLOCKS, LRU_BLOCK)
    hl = _lru_scan("lru_scan_fwd", a3, bx.reshape(T, LRU_BLOCKS, LRU_BLOCK), False).reshape(T, D)

    row1 = _spec((tm, D), lambda i: (i, 0))
    glru1 = _spec((tm, D), lambda i: (i, 4))

    def lru_out(irefs, orefs, ids):
        gl, _ = _gelu_and_grad(irefs[1][...])
        orefs[0][...] = (irefs[0][...] * gl).astype(BF16)

    yl = _rowwise("lru_out", lru_out, [(hl, row1), (u, glru1)], [(_sds((T, D), BF16), row1)], (T // tm,))[0]

    def gate_epilogue(acc, erefs, orefs, ids):
        orefs[0][...] = _sigmoid(acc + erefs[0][...])

    gates = _gemm(
        "mix_gates",
        [(h2, wideD, gw["wbg"], _spec((None, None, BG_BLK, D), lambda i, j, r: (j, 0, 0, 0)), "nt")],
        (T // tw, N_CHIPS, 1),
        [(_sds((T, 2 * D), F32), _spec((tw, BG_BLK), lambda i, j, r: (i, j)))], (tw, BG_BLK),
        [(sm["b_branch_gate"], _spec((1, BG_BLK), lambda i, j, r: (0, j)))], gate_epilogue)[0]

    y_ret = _proj_sq("y_ret", yr, *wt("w_ret_o"), "nn")[0]

    def merge_epilogue(acc, erefs, orefs, ids):
        orefs[0][...] = acc
        orefs[1][...] = (erefs[0][...] * erefs[2][...] + erefs[1][...] * acc).astype(BF16)

    y_lru, merged = _proj_sq(
        "y_lru", yl, *wt("w_lru_o"), "nn",
        extras=[(gates, _spec((tm, D), lambda i, j, r: (i, 0))), (gates, _spec((tm, D), lambda i, j, r: (i, 1))),
                (y_ret, rowD)],
        epilogue=merge_epilogue,
        outs=[(_sds((T, D), F32), rowD), (_sds((T, D), BF16), rowD)])

    ex, ou = res_norm_io(x1, sm["xattn_norm"])
    x2, hq = _proj_sq("mix_out", merged, *wt("w_out"), "nn", extras=ex, epilogue=residual_norm, outs=ou)

    m = _rmsnorm("mem_norm", mem, sm["mem_norm"])
    xq = _proj_sq("xq", hq, *wt("w_xq"), "nn", BF16)[0]
    xk = _proj_sq("xk", m, *wt("w_xk"), "nn", BF16)[0]
    xv = _proj_sq("xv", m, *wt("w_xv"), "nn", BF16)[0]
    xo = _xattn_fwd(xq, xk, xv)
    ex, ou = res_norm_io(x2, sm["ffn2_norm"])
    x3, h3 = _proj_sq("xattn_out", xo, *wt("w_xo"), "nn", extras=ex, epilogue=residual_norm, outs=ou)

    a2, b2, s2 = _ffn_up("ffn2_up", h3, *wt("ffn2_w1"), *wt("ffn2_w3"))
    x4 = _ffn_down("ffn2_down", s2, *wt("ffn2_w2"), x3)[0]
    loss, dx4, dg_final = _final_loss(x4, sm["final_norm"], tgt)

    dx3, dg_ffn2 = _ffn_bwd("ffn2", dx4, h3, a2, b2, s2, *wt("ffn2_w1"), *wt("ffn2_w3"),
                            *wt("ffn2_w2"), x3, sm["ffn2_norm"], big)

    dxo = _proj_sq("d_xo", dx3, *wt("w_xo"), "nt", BF16)[0]
    big["w_xo"] = _dw_sq("dw_xo", xo, dx3)[None]
    dxq, dxk, dxv = _xattn_bwd(xq, xk, xv, dxo)
    big["w_xq"] = _dw_sq("dw_xq", hq, dxq)[None]
    ex, ou = _rms_bwd_io(x2, sm["xattn_norm"], dx3, T, tm)
    dx2, dg_xattn = _proj_sq("d_hq", dxq, *wt("w_xq"), "nt", extras=ex, epilogue=_rms_bwd_epilogue, outs=ou)
    big["w_xk"] = _dw_sq("dw_xk", m, dxk)[None]
    big["w_xv"] = _dw_sq("dw_xv", m, dxv)[None]

    M = mem.shape[0]

    def mem_norm_epilogue(acc, erefs, orefs, ids):
        _, dgp = _rms_bwd(erefs[0][...], erefs[1][...], acc)
        orefs[0][...] = dgp

    wsq_spec = lambda idx: _spec((N_CHIPS, None, SQ_BLK, D), lambda i, j, r: (0, idx, 0, 0))
    memD = _spec((M, D), row3)
    dg_mem = _gemm(
        "d_mem_norm",
        [(dxk, memD, wt("w_xk")[0], wsq_spec(wt("w_xk")[1]), "nt"),
         (dxv, memD, wt("w_xv")[0], wsq_spec(wt("w_xv")[1]), "nt")],
        (1, 1, 1), [(_sds((1, D), F32), vecD)], (M, D),
        [(mem, memD), (sm["mem_norm"], vecD)], mem_norm_epilogue)[0]

    def merged_bwd_epilogue(acc, erefs, orefs, ids):
        gr, gl, yrv, ylv = (e[...] for e in erefs)
        orefs[0][...] = (acc * gr).astype(BF16)
        orefs[1][...] = (acc * gl).astype(BF16)
        dgr = acc * yrv * gr * (1.0 - gr)
        dgl = acc * ylv * gl * (1.0 - gl)
        orefs[2][:, :D] = dgr.astype(BF16)
        orefs[2][:, D:] = dgl.astype(BF16)
        dbb = jnp.concatenate([jnp.sum(dgr, axis=0, keepdims=True), jnp.sum(dgl, axis=0, keepdims=True)], axis=1)
        _accumulate(orefs[3], dbb, ids[0] == 0)

    dy_ret, dy_lru, dgpre, db_bg = _proj_sq(
        "d_merged", dx2, *wt("w_out"), "nt",
        extras=[(gates, _spec((tm, D), lambda i, j, r: (i, 0))), (gates, _spec((tm, D), lambda i, j, r: (i, 1))),
                (y_ret, rowD), (y_lru, rowD)],
        epilogue=merged_bwd_epilogue,
        outs=[(_sds((T, D), BF16), rowD), (_sds((T, D), BF16), rowD),
              (_sds((T, 2 * D), BF16), _spec((tm, 2 * D), row3)),
              (_sds((1, 2 * D), F32), _spec((1, 2 * D), vec3))])
    big["w_branch_gate"] = _gemm(
        "dw_bg",
        [(h2, _spec((T, D), lambda j, n, r: (r, 0)), dgpre, _spec((T, BG_BLK), lambda j, n, r: (r, j)), "tn")],
        (N_CHIPS, 1, 1),
        [(_sds((N_CHIPS, D, BG_BLK), F32), _spec((None, D, BG_BLK), lambda j, n, r: (j, 0, 0)))],
        (D, BG_BLK))[0][None]
    big["w_out"] = _dw_sq("dw_out", merged, dx2)[None]
    dyr = _proj_sq("d_yr", dy_ret, *wt("w_ret_o"), "nt")[0]
    big["w_ret_o"] = _dw_sq("dw_ret_o", yr, dy_ret)[None]
    dyl = _proj_sq("d_yl", dy_lru, *wt("w_lru_o"), "nt")[0]
    big["w_lru_o"] = _dw_sq("dw_lru_o", yl, dy_lru)[None]

    dq, dk, dv, dgr, dg_retgn = _ret_bwd(dyr, ret, u, qr, kr, states, consts, sm["ret_gn"])

    def lru_out_bwd(irefs, orefs, ids):
        gl, dgl = _gelu_and_grad(irefs[2][...])
        dyl_v = irefs[0][...]
        orefs[0][...] = dyl_v * gl
        orefs[1][...] = (dyl_v * irefs[1][...] * dgl).astype(BF16)

    dhl, dglru = _rowwise("lru_out_bwd", lru_out_bwd, [(dyl, row1), (hl, row1), (u, glru1)],
                          [(_sds((T, D), F32), row1), (_sds((T, D), BF16), row1)], (T // tm,))
    lmb = _lru_scan("lru_scan_bwd", a3, dhl.reshape(T, LRU_BLOCKS, LRU_BLOCK), True).reshape(T, D)
    dxl, dw_r, dw_i, dvec, dcw = _lru_gates_bwd(lmb, hl, av, rg, ig, xc, u, conv_w,
                                                sm["w_rgate"], sm["w_igate"], sm["lru_lambda"])

    du = jnp.concatenate([dq, dk, dv, dgr, dxl, dglru], axis=1)
    tk = T
    big["w_in"] = _gemm(
        "dw_in",
        [(h2, _spec((tk, D), lambda j, n, r: (r, 0)), du, _spec((tk, IN_BLK), lambda j, n, r: (r, j)), "tn")],
        (N_CHIPS, 1, T // tk),
        [(_sds((N_CHIPS, D, IN_BLK), F32), _spec((None, D, IN_BLK), lambda j, n, r: (j, 0, 0)))],
        (D, IN_BLK))[0][None]
    tf = min(FFN_ROW_TILE, T)
    ex, ou = _rms_bwd_io(x1, sm["mix_norm"], dx2, T, tf)
    dx1, dg_mix = _gemm(
        "d_h2",
        [(du, _spec((tf, 5120), row3), gw["win"], _spec((N_CHIPS, None, IN_BLK, D), lambda i, j, r: (0, 0, 0, 0)), "nn"),
         (dgpre, _spec((tf, 2 * D), row3), gw["wbg"], _spec((N_CHIPS, None, BG_BLK, D), lambda i, j, r: (0, 0, 0, 0)),
          "nn")],
        (T // tf, 1, 1), ou, (tf, D), ex, _rms_bwd_epilogue)

    grad_x, dg_ffn1 = _ffn_bwd("ffn1", dx1, h1, a1, b1, s1, *wt("ffn1_w1"), *wt("ffn1_w3"),
                               *wt("ffn1_w2"), x, sm["ffn1_norm"], big)

    small = {
        "ffn1_norm": dg_ffn1, "mix_norm": dg_mix, "ret_gn": dg_retgn, "conv_b": dvec[3:4],
        "b_rgate": dvec[0:1], "b_igate": dvec[1:2], "lru_lambda": dvec[2:3], "xattn_norm": dg_xattn,
        "mem_norm": dg_mem, "ffn2_norm": dg_ffn2, "final_norm": dg_final, "b_branch_gate": db_bg,
        "conv_w": dcw, "w_rgate": dw_r, "w_igate": dw_i,
    }
    return loss, grad_x, small


ANY_SPEC = pl.BlockSpec(memory_space=pl.ANY)
VMEM_SPEC = pl.BlockSpec(memory_space=pltpu.VMEM)
N_PEER_CHIPS = N_CHIPS - 1


def _mesh_position():
    x, y, c = lax.axis_index("x"), lax.axis_index("y"), lax.axis_index("c")
    chips = [(1 - x, y), (x, 1 - y), (1 - x, 1 - y)]
    return x, y, c, chips


def _chip_index(x, y):
    return 2 * x + y


def _rows_half(ref, axis, h):
    n = ref.shape[axis] // 2
    idx = [slice(None)] * len(ref.shape)
    idx[axis] = pl.ds(pl.multiple_of(h * n, 16), n)
    return ref.at[tuple(idx)]


def _remote(src, dst, send_sem, recv_sem, device):
    return pltpu.make_async_remote_copy(src_ref=src, dst_ref=dst, send_sem=send_sem, recv_sem=recv_sem,
                                        device_id=device, device_id_type=MESH)


def _gather_chips_task(shards, split, landed, part=0, nparts=1):
    keys = list(shards)
    n = len(keys)

    def operands():
        if part:
            return [shards[k] for k in keys] + [landed[k] for k in keys]
        chip_me = _chip_index(lax.axis_index("x"), lax.axis_index("y"))
        bases = [lax.dynamic_update_slice(lax.empty((N_CHIPS,) + shards[k].shape, shards[k].dtype), shards[k][None],
                                          (chip_me,) + (0,) * shards[k].ndim) for k in keys]
        return [shards[k] for k in keys] + bases

    def my_rows(ref, c):
        rows = ref.shape[1] // (2 * nparts)
        return ref.at[:, pl.ds(pl.multiple_of((c * nparts + part) * rows, 16), rows), :]

    def make(ins, outs, send_sem, recv_sem):
        x, y, c, chips = _mesh_position()
        s_me = _chip_index(x, y)
        starts, arrivals = [], []
        for g in range(n):
            mine = my_rows(ins[g], c) if split else ins[g]
            for k, chip in enumerate(chips):
                def landing(s):
                    o = outs[g].at[s]
                    return my_rows(o, c) if split else o
                starts.append(_remote(mine, landing(s_me), send_sem(3 * g + k), recv_sem(3 * g + k), (*chip, c)))
                got = landing(_chip_index(*chip))
                arrivals.append(functools.partial(_remote, got, got, send_sem(3 * g + k), recv_sem(3 * g + k),
                                                  (*chip, c)))
        return starts, arrivals

    def finish(res):
        landed.update(zip(keys, res))

    return _Task(operands, lambda: [_sds((N_CHIPS,) + shards[k].shape, shards[k].dtype) for k in keys],
                 {n + g: g for g in range(n)}, 3 * n, make, finish)


def _gather_sibling_task(keys, landed, ready):
    n = len(keys)

    def make(ins, outs, send_sem, recv_sem):
        x, y, c, chips = _mesh_position()
        starts, arrivals = [], []
        for g in range(n):
            for k, chip in enumerate(chips):
                o = outs[g].at[_chip_index(*chip)]
                got, other = _rows_half(o, 1, c), _rows_half(o, 1, 1 - c)
                starts.append(_remote(got, got, send_sem(3 * g + k), recv_sem(3 * g + k), (x, y, 1 - c)))
                arrivals.append(functools.partial(_remote, other, other, send_sem(3 * g + k), recv_sem(3 * g + k),
                                                  (x, y, 1 - c)))
        return starts, arrivals

    def finish(res):
        ready.update(zip(keys, res))

    return _Task(lambda: [landed[k] for k in keys], lambda: [_sds(landed[k].shape, landed[k].dtype) for k in keys],
                 {g: g for g in range(n)}, 3 * n, make, finish)


def _pair_swap_task(names, big, got):
    n = len(names)

    def make(ins, outs, send_sem, recv_sem):
        x, y, c, _ = _mesh_position()
        copies = [_remote(_rows_half(ins[a], 2, 1 - c), outs[a], send_sem(a), recv_sem(a), (x, y, 1 - c))
                  for a in range(n)]
        return copies, [functools.partial(lambda cp: cp, cp) for cp in copies]

    def shapes():
        return [_sds(big[k].shape[:2] + (big[k].shape[2] // 2, big[k].shape[3]), F32) for k in names]

    return _Task(lambda: [big[k] for k in names], shapes, {}, n, make, lambda res: got.update(zip(names, res)))


def _rs_pair_sum(name, full, got, core):
    nw, ns, R, C = full.shape
    half = R // 2

    def body(core_ref, a_ref, b_ref, o_ref):
        o_ref[...] = (a_ref[...] + b_ref[...]).astype(BF16)

    blk = lambda fn: pl.BlockSpec((None, None, half, C), fn)
    return _pcall(
        body, name=name, grid=(nw, ns), num_prefetch=1,
        in_specs=[blk(lambda w, s, core_ref: (w, s, core_ref[0], 0)), blk(lambda w, s, core_ref: (w, s, 0, 0))],
        out_specs=blk(lambda w, s, core_ref: (w, s, 0, 0)),
        out_shape=_sds((nw, ns, half, C), BF16),
    )(core, full, got)


def _chip_exchange_task(names, pair_sums, by_source, part=0, nparts=1):
    n = len(names)

    def rows(ref):
        h = ref.shape[1] // nparts
        return ref.at[:, pl.ds(part * h, h), :]

    def make(ins, outs, send_sem, recv_sem):
        x, y, c, chips = _mesh_position()
        s_me = _chip_index(x, y)
        starts, arrivals = [], []
        for a in range(n):
            for k, chip in enumerate(chips):
                s_k = _chip_index(*chip)
                starts.append(_remote(rows(ins[a].at[:, s_k]), rows(outs[a].at[:, s_me]), send_sem(3 * a + k),
                                      recv_sem(3 * a + k), (*chip, c)))
                got = rows(outs[a].at[:, s_k])
                arrivals.append(functools.partial(_remote, got, got, send_sem(3 * a + k), recv_sem(3 * a + k),
                                                  (*chip, c)))
        return starts, arrivals

    def operands():
        return [pair_sums[k] for k in names] + ([by_source[k] for k in names] if part else [])

    return _Task(operands, lambda: [_sds(pair_sums[k].shape, pair_sums[k].dtype) for k in names],
                 {n + a: a for a in range(n)} if part else {}, 3 * n, make,
                 lambda res: by_source.update(zip(names, res)))


def _rs_chip_sum(name, own, parts, chip):
    nw, ns, H, C = parts.shape

    def body(chip_ref, own_ref, *rest):
        prefs, o_ref = rest[:ns], rest[ns]
        me = chip_ref[0]
        own_v = own_ref[...].astype(F32)
        tot = None
        for s in range(ns):
            term = jnp.where(me == s, own_v, prefs[s][...].astype(F32))
            tot = term if tot is None else tot + term
        o_ref[...] = tot

    blk = lambda fn: pl.BlockSpec((None, None, H, C), fn)

    def part_spec(s):
        return blk(lambda w, chip_ref: (w, jnp.where(chip_ref[0] == s, (s + 1) % ns, s), 0, 0))

    return _pcall(
        body, name=name, grid=(nw,), num_prefetch=1,
        in_specs=[blk(lambda w, chip_ref: (w, chip_ref[0], 0, 0))] + [part_spec(s) for s in range(ns)],
        out_specs=pl.BlockSpec((None, H, C), lambda w, chip_ref: (w, 0, 0)),
        out_shape=_sds((nw, H, C), F32),
    )(chip, own, *([parts] * ns))


def _pair_gather_task(names, halves, sibling_halves):
    n = len(names)

    def make(ins, outs, send_sem, recv_sem):
        x, y, c, _ = _mesh_position()
        copies = [_remote(ins[a], outs[a], send_sem(a), recv_sem(a), (x, y, 1 - c)) for a in range(n)]
        return copies, [functools.partial(lambda cp: cp, cp) for cp in copies]

    return _Task(lambda: [halves[k] for k in names], lambda: [_sds(halves[k].shape, F32) for k in names],
                 {}, n, make, lambda res: sibling_halves.update(zip(names, res)))


def _small_allreduce(v):
    R, C = v.shape
    Q = R // N_CHIPS
    nsem = 1 + 2 * N_PEER_CHIPS

    def body(v_ref, o_ref, sib_buf, pair_buf, part_buf, send_sems, recv_sems):
        x, y, c, chips = _mesh_position()
        s_me = _chip_index(x, y)

        def quarter(ref, s):
            return ref.at[pl.ds(pl.multiple_of(s * Q, 8), Q)]

        def exchange(first_sem, src, dst_of, arrival_of):
            sends = [_remote(src(_chip_index(*chip)), dst_of(s_me), send_sems.at[first_sem + k],
                             recv_sems.at[first_sem + k], (*chip, c)) for k, chip in enumerate(chips)]
            for cp in sends:
                cp.start()
            for k, chip in enumerate(chips):
                got = arrival_of(_chip_index(*chip))
                _remote(got, got, send_sems.at[first_sem + k], recv_sems.at[first_sem + k], (*chip, c)).wait_recv()
            for cp in sends:
                cp.wait_send()

        swap = _remote(v_ref, sib_buf, send_sems.at[0], recv_sems.at[0], (x, y, 1 - c))
        swap.start()
        swap.wait()
        pair_buf[...] = v_ref[...] + sib_buf[...]
        exchange(1, lambda s_k: quarter(pair_buf, s_k), lambda s: part_buf.at[s], lambda s_k: part_buf.at[s_k])
        part_buf[s_me] = quarter(pair_buf, s_me)[...]
        o_ref[pl.ds(pl.multiple_of(s_me * Q, 8), Q), :] = ((part_buf[0] + part_buf[1]) + part_buf[2]) + part_buf[3]
        exchange(1 + N_PEER_CHIPS, lambda s_k: quarter(o_ref, s_me), lambda s: quarter(o_ref, s),
                 lambda s_k: quarter(o_ref, s_k))

    return _pcall(
        body, name="small_allreduce", grid=(1,),
        in_specs=[VMEM_SPEC], out_specs=VMEM_SPEC, out_shape=_sds((R, C), F32),
        scratch_shapes=[pltpu.VMEM((R, C), F32), pltpu.VMEM((R, C), F32), pltpu.VMEM((N_CHIPS, Q, C), F32),
                        pltpu.SemaphoreType.DMA((nsem,)), pltpu.SemaphoreType.DMA((nsem,))],
    )(v)


TRANSPOSED_WEIGHTS = ("ffn1_w1", "ffn1_w3", "ffn2_w1", "ffn2_w3")
SMALL_LAYOUT = [("ffn1_norm", 1), ("mix_norm", 1), ("ret_gn", 1), ("conv_b", 1), ("b_rgate", 1), ("b_igate", 1),
                ("lru_lambda", 1), ("xattn_norm", 1), ("mem_norm", 1), ("ffn2_norm", 1), ("final_norm", 1),
                ("b_branch_gate", 2), ("conv_w", CONV_TAPS), ("w_rgate", LRU_BLOCK), ("w_igate", LRU_BLOCK)]
SMALL_ROWS = 288
WEIGHT_ORDER = ["ffn1_norm", "ffn1_w1", "ffn1_w3", "ffn1_w2", "mix_norm", "w_in", "ret_gn", "w_ret_o", "conv_w",
                "conv_b", "w_rgate", "b_rgate", "w_igate", "b_igate", "lru_lambda", "w_lru_o", "w_branch_gate",
                "b_branch_gate", "w_out", "xattn_norm", "mem_norm", "w_xq", "w_xk", "w_xv", "w_xo", "ffn2_norm",
                "ffn2_w1", "ffn2_w3", "ffn2_w2", "final_norm"]


def _pack_small(parts):
    rows = [parts[name].reshape(n, D) for name, n in SMALL_LAYOUT]
    used = sum(n for _, n in SMALL_LAYOUT)
    rows.append(jnp.zeros((SMALL_ROWS - used, D), F32))
    return jnp.concatenate(rows, axis=0)


def _unpack_small(packed, shapes):
    out, r = {}, 0
    for name, n in SMALL_LAYOUT:
        out[name] = packed[r:r + n].reshape(shapes[name])
        r += n
    return out


def kernel(x, mem, ffn1_norm, ffn1_w1, ffn1_w3, ffn1_w2, mix_norm, w_in, ret_gn, w_ret_o, conv_w, conv_b, w_rgate, b_rgate, w_igate, b_igate, lru_lambda, w_lru_o, w_branch_gate, b_branch_gate, w_out, xattn_norm, mem_norm, w_xq, w_xk, w_xv, w_xo, ffn2_norm, ffn2_w1, ffn2_w3, ffn2_w2, final_norm, loss_target, m_ffn1_norm, m_ffn1_w1, m_ffn1_w3, m_ffn1_w2, m_mix_norm, m_w_in, m_ret_gn, m_w_ret_o, m_conv_w, m_conv_b, m_w_rgate, m_b_rgate, m_w_igate, m_b_igate, m_lru_lambda, m_w_lru_o, m_w_branch_gate, m_b_branch_gate, m_w_out, m_xattn_norm, m_mem_norm, m_w_xq, m_w_xk, m_w_xv, m_w_xo, m_ffn2_norm, m_ffn2_w1, m_ffn2_w3, m_ffn2_w2, m_final_norm, v_ffn1_norm, v_ffn1_w1, v_ffn1_w3, v_ffn1_w2, v_mix_norm, v_w_in, v_ret_gn, v_w_ret_o, v_conv_w, v_conv_b, v_w_rgate, v_b_rgate, v_w_igate, v_b_igate, v_lru_lambda, v_w_lru_o, v_w_branch_gate, v_b_branch_gate, v_w_out, v_xattn_norm, v_mem_norm, v_w_xq, v_w_xk, v_w_xv, v_w_xo, v_ffn2_norm, v_ffn2_w1, v_ffn2_w3, v_ffn2_w2, v_final_norm):
    given = dict(locals())
    w = {n: given[n] for n in WEIGHT_ORDER}
    mom = {n: given["m_" + n] for n in WEIGHT_ORDER}
    var = {n: given["v_" + n] for n in WEIGHT_ORDER}
    chip = _chip_index(lax.axis_index("x"), lax.axis_index("y"))
    core = lax.axis_index("c").astype(jnp.int32).reshape(1)

    chip_id = chip.astype(jnp.int32).reshape(1)
    sm = {n: w[n] for n in ["ffn1_norm", "mix_norm", "ret_gn", "conv_b", "b_rgate", "b_igate", "lru_lambda",
                            "xattn_norm", "mem_norm", "ffn2_norm", "b_branch_gate"]}
    sm["final_norm"] = w["final_norm"].reshape(1, D)
    sm["w_rgate"] = w["w_rgate"][0]
    sm["w_igate"] = w["w_igate"][0]

    local = lambda a, n: jnp.swapaxes(a[0], 0, 1) if n in TRANSPOSED_WEIGHTS else a[0]
    stack = lambda names: jnp.stack([local(w[n], n) for n in names], axis=0).astype(BF16)
    shard = {"col1": stack(["ffn1_w1", "ffn1_w3"]), "row2a": stack(["ffn1_w2"]),
             "win": jnp.swapaxes(w["w_in"], 1, 2).astype(BF16), "wbg": jnp.swapaxes(w["w_branch_gate"], 1, 2).astype(BF16), "sqA": stack(["w_ret_o", "w_lru_o", "w_out"]),
             "sqB": stack(["w_xq", "w_xk"]), "sqC": stack(["w_xv", "w_xo"]), "col2a": stack(["ffn2_w1"]), "col2b": stack(["ffn2_w3"]),
             "row2b": stack(["ffn2_w2"]), "conv": w["conv_w"]}
    gw, landed = {}, {}
    over_chips = lambda keys: _gather_chips_task({k: shard[k] for k in keys}, True, landed)
    to_sibling = lambda keys: _gather_sibling_task(keys, landed, gw)

    big, got, pair_sums, by_source, halves, sibling_halves, outs = {}, {}, {}, {}, {}, {}, {}
    pair_swap = lambda names: _pair_swap_task(names, big, got)
    exchange = lambda names, part=0, nparts=1: _chip_exchange_task(names, pair_sums, by_source, part, nparts)
    pair_gather = lambda names: _pair_gather_task(names, halves, sibling_halves)

    def pair_sum(names):
        for n in names:
            pair_sums[n] = _rs_pair_sum("rs_pair_sum_" + n, big[n], got[n], core)

    def chip_sum(names):
        for n in names:
            halves[n] = _rs_chip_sum("rs_chip_sum_" + n, pair_sums[n], by_source[n], chip_id)

    def adamw(names):
        for n in names:
            res = _adamw_halves("adamw_" + n, local(w[n], n), halves[n], sibling_halves[n], 0, local(mom[n], n),
                                local(var[n], n), core)
            outs[n] = tuple((jnp.swapaxes(r, 0, 1) if n in TRANSPOSED_WEIGHTS else r)[None] for r in res)

    do = lambda fn, names: functools.partial(fn, names)
    ffn2_grads = ["ffn2_w2", "ffn2_w1", "ffn2_w3"]
    xattn_grads = ["w_xo", "w_xq", "w_xk", "w_xv"]
    mix_out_grads = ["w_branch_gate", "w_out", "w_ret_o", "w_lru_o"]
    conv_gather = _gather_chips_task({"conv": shard["conv"]}, False, gw)
    half = lambda key, part: _gather_chips_task({key: shard[key]}, True, landed, part, 2)
    plan = _Plan()
    plan.tasks = {
        "ag_first_chips": [over_chips(["col1", "row2a"])],
        "ag_first_sibling": [to_sibling(["col1", "row2a"])],
        "ffn1_up": [over_chips(["win"])],
        "ffn1_down": [to_sibling(["win"]), over_chips(["wbg"]), conv_gather],
        "mix_in": [to_sibling(["wbg"]), over_chips(["sqA"])],
        "ret_fwd": [to_sibling(["sqA"]), over_chips(["col2a"])],
        "lru_gates_fwd": [to_sibling(["col2a"]), over_chips(["sqB"])],
        "lru_scan_fwd": [to_sibling(["sqB"]), over_chips(["sqC"])],
        "lru_out": [to_sibling(["sqC"])],
        "mix_gates": [half("col2b", 0)],
        "y_lru": [half("col2b", 1)],
        "mix_out": [to_sibling(["col2b"]), half("row2b", 0)],
        "xattn_fwd": [half("row2b", 1)],
        "xattn_out": [to_sibling(["row2b"])],
        "ffn2_dh": [pair_swap(ffn2_grads)],
        "xattn_bwd": [exchange(["ffn2_w2"], 0, 2)],
        "d_hq": [exchange(["ffn2_w2"], 1, 2)],
        "d_merged": [exchange(["ffn2_w1"], 0, 2), pair_swap(xattn_grads)],
        "dw_bg": [exchange(["w_xo"])],
        "ret_bwd": [exchange(["ffn2_w1"], 1, 2), exchange(["ffn2_w3"], 0, 2), pair_swap(mix_out_grads)],
        "lru_scan_bwd": [exchange(["ffn2_w3"], 1, 2)],
        "lru_gates_bwd": [exchange(["w_xq", "w_xk"]), pair_gather(ffn2_grads)],
        "dw_in": [exchange(["w_xv", "w_out"])],
        "d_h2": [exchange(["w_branch_gate", "w_ret_o", "w_lru_o"]), pair_swap(["w_in"]), pair_gather(xattn_grads)],
        "ffn1_bwd_mid": [exchange(["w_in"], 0, 2), pair_gather(mix_out_grads)],
        "ffn1_dw2": [exchange(["w_in"], 2, 4)],
        "ffn1_dw1": [exchange(["w_in"], 3, 4), pair_swap(["ffn1_w2"])],
        "ffn1_dw3": [exchange(["ffn1_w2"], 0, 2), pair_swap(["ffn1_w1"]), pair_gather(["w_in"])],
        "ffn1_dh": [exchange(["ffn1_w2"], 1, 2), exchange(["ffn1_w1"]), pair_swap(["ffn1_w3"])],
        "small_allreduce": [exchange(["ffn1_w3"]), pair_gather(["ffn1_w2"])],
        "rs_last_gather": [pair_gather(["ffn1_w1", "ffn1_w3"])],
    }
    plan.after = {
        "ffn2_dh": [do(pair_sum, ffn2_grads)],
        "d_merged": [do(pair_sum, xattn_grads)],
        "ret_bwd": [do(pair_sum, mix_out_grads)],
        "lru_scan_bwd": [do(chip_sum, ffn2_grads)],
        "lru_gates_bwd": [do(adamw, ffn2_grads)],
        "dw_in": [do(chip_sum, xattn_grads)],
        "d_h2": [do(chip_sum, mix_out_grads), do(pair_sum, ["w_in"]), do(adamw, xattn_grads)],
        "ffn1_bwd_mid": [do(adamw, mix_out_grads)],
        "ffn1_dw1": [do(chip_sum, ["w_in"]), do(pair_sum, ["ffn1_w2"])],
        "ffn1_dw3": [do(pair_sum, ["ffn1_w1"]), do(adamw, ["w_in"])],
        "ffn1_dh": [do(pair_sum, ["ffn1_w3"]), do(chip_sum, ["ffn1_w2"])],
        "small_allreduce": [do(chip_sum, ["ffn1_w1", "ffn1_w3"]), functools.partial(_comm_call, "rs_last_gather"),
                    do(adamw, ["ffn1_w2", "ffn1_w1", "ffn1_w3"])],
    }
    global _plan
    _plan = plan
    try:
        _comm_call("ag_first_chips")
        _comm_call("ag_first_sibling")
        loss_part, grad_x, small = _local_step(x[0], mem[0], loss_target[0], gw, sm, big)
        small_sum = _small_allreduce(_pack_small(small))
    finally:
        _plan = None
    assert not plan.tasks and not plan.after, (list(plan.tasks), list(plan.after))
    loss = lax.psum(loss_part[0, 0], ("x", "y", "c"))

    small_shapes = {n: w[n].shape for n, _ in SMALL_LAYOUT}
    small_shapes["conv_w"] = (CONV_TAPS, D)
    conv_grad = lax.dynamic_slice(small_sum[13:13 + CONV_TAPS], (0, chip * SQ_BLK), (CONV_TAPS, SQ_BLK))
    small_w = {n: w[n] for n, _ in SMALL_LAYOUT}
    small_m = {n: mom[n] for n, _ in SMALL_LAYOUT}
    small_v = {n: var[n] for n, _ in SMALL_LAYOUT}
    pad_cols = lambda a: jnp.pad(a[0], ((0, 0), (0, D - SQ_BLK)))
    for dct in (small_w, small_m, small_v):
        dct["conv_w"] = pad_cols(dct["conv_w"])
    g_pack = lax.dynamic_update_slice(small_sum, jnp.pad(conv_grad, ((0, 0), (0, D - SQ_BLK))), (13, 0))
    d_pack, m_pack, v_pack = _adamw("adamw_small", _pack_small(small_w), g_pack, _pack_small(small_m),
                                    _pack_small(small_v))
    unpacked = [_unpack_small(p, small_shapes) for p in (g_pack, d_pack, m_pack, v_pack)]
    for n, _ in SMALL_LAYOUT:
        if n == "conv_w":
            outs[n] = tuple(u[n][:, :SQ_BLK][None] for u in unpacked)
        else:
            outs[n] = tuple(u[n] for u in unpacked)

    result = [loss, grad_x[None]]
    for k in range(4):
        result += [outs[n][k] for n in WEIGHT_ORDER]
    return tuple(result)
```

```python
import functools
import math

import numpy as np
import jax
import jax.numpy as jnp
from jax import lax
from jax.experimental import pallas as pl
from jax.experimental.pallas import tpu as pltpu

F32 = jnp.float32
BF16 = jnp.bfloat16
GRAD_WIRE_DTYPE = BF16
MESH = pl.DeviceIdType.MESH

D = 1024
EPS = 1e-6
RET_HEADS = 4
RET_DK = 128
RET_DV = 256
CHUNK = 128
ROPE_BASE = 10000.0
LRU_BLOCKS = 8
LRU_BLOCK = 128
CONV_TAPS = 4
LRU_C = 8.0
D_FF = 2816
X_HEADS = 4
X_HD = 256
N_CHIPS = 4
FF_BLK = D_FF // N_CHIPS
IN_BLK = 5120 // N_CHIPS
BG_BLK = 2048 // N_CHIPS
SQ_BLK = D // N_CHIPS

ADAM_LR = 0.001
ADAM_B1 = 0.9
ADAM_B2 = 0.999
ADAM_EPS = 1e-08
ADAM_WD = 0.01
ADAM_STEP = 10

VMEM_LIMIT_BYTES = 56 * 1024 * 1024
ROW_TILE = 512
WIDE_ROW_TILE = 1024
FFN_ROW_TILE = 256
DW_BLK = D_FF // 2
SCAN_TILE = 256

_DN = {
    "nn": (((1,), (0,)), ((), ())),
    "nt": (((1,), (1,)), ((), ())),
    "tn": (((0,), (0,)), ((), ())),
}


def _cparams(n_axes):
    return pltpu.CompilerParams(dimension_semantics=("arbitrary",) * n_axes,
                                vmem_limit_bytes=VMEM_LIMIT_BYTES)


def _dot(a, b, kind):
    if b.ndim == 3:
        b = b.reshape(b.shape[0] * b.shape[1], b.shape[2])
    return lax.dot_general(a.astype(BF16), b.astype(BF16), _DN[kind], preferred_element_type=F32)


def _sigmoid(x):
    return 1.0 / (1.0 + jnp.exp(-x))


def _log1p_pos(e):
    u = 1.0 + e
    return jnp.where(u == 1.0, e, jnp.log(u) * (e / jnp.where(u == 1.0, 1.0, u - 1.0)))


def _expm1(x):
    u = jnp.exp(x)
    lu = jnp.log(u)
    safe = jnp.where(lu == 0.0, 1.0, lu)
    return jnp.where(u == 1.0, x, (u - 1.0) * (x / safe))


def _softplus(z):
    return jnp.maximum(z, 0.0) + _log1p_pos(jnp.exp(-jnp.abs(z)))


_GELU_C = math.sqrt(2.0 / math.pi)


def _gelu_and_grad(x):
    x2 = x * x
    t = jnp.tanh(_GELU_C * (x + 0.044715 * x * x2))
    g = 0.5 * x * (1.0 + t)
    dg = 0.5 * (1.0 + t) + 0.5 * x * (1.0 - t * t) * (_GELU_C * (1.0 + 3.0 * 0.044715 * x2))
    return g, dg


def _rms_fwd(x, g):
    r = lax.rsqrt(jnp.mean(x * x, axis=-1, keepdims=True) + EPS)
    return (x * r) * g


def _rms_bwd(x, g, dh):
    r = lax.rsqrt(jnp.mean(x * x, axis=-1, keepdims=True) + EPS)
    n = x * r
    dyg = dh * g
    dx = r * (dyg - n * jnp.mean(dyg * n, axis=-1, keepdims=True))
    return dx, jnp.sum(dh * n, axis=0, keepdims=True)


def _accumulate(ref, val, first):
    @pl.when(first)
    def _():
        ref[...] = val

    @pl.when(jnp.logical_not(first))
    def _():
        ref[...] += val


def _sds(shape, dtype):
    return jax.ShapeDtypeStruct(tuple(shape), dtype)


def _spec(shape, fn):
    return pl.BlockSpec(tuple(shape), fn)


class _Task:
    def __init__(self, operands, out_shapes, aliases, nsem, make, finish):
        self.operands, self.out_shapes, self.aliases = operands, out_shapes, aliases
        self.nsem, self.make, self.finish = nsem, make, finish


class _Plan:
    def __init__(self):
        self.tasks, self.after = {}, {}


_plan = None


def _pcall(body, *, name, grid, in_specs, out_specs, out_shape, scratch_shapes=(), num_prefetch=0):
    single = not isinstance(out_shape, (list, tuple))
    out_shape = [out_shape] if single else list(out_shape)
    out_specs = [out_specs] if single else list(out_specs)
    in_specs = list(in_specs)
    scratch_shapes = list(scratch_shapes)
    tasks = _plan.tasks.pop(name, []) if _plan is not None else []
    after = _plan.after.pop(name, []) if _plan is not None else []
    nax = len(grid)

    def run(*operands):
        n_in = len(operands) - num_prefetch
        n_out = len(out_shape)
        t_ops = [t.operands() for t in tasks]
        t_outs = [t.out_shapes() for t in tasks]
        c_ops = [a for ops in t_ops for a in ops]
        c_outs = [s for outs in t_outs for s in outs]
        aliases = {}
        i0, o0 = num_prefetch + n_in, n_out
        for t, ops, outs in zip(tasks, t_ops, t_outs):
            for i_loc, o_loc in t.aliases.items():
                aliases[i0 + i_loc] = o0 + o_loc
            i0 += len(ops)
            o0 += len(outs)
        nsem = sum(t.nsem for t in tasks)

        def wrapped(*refs):
            p = num_prefetch
            pre, ins = refs[:p], refs[p:p + n_in]
            cins = refs[p + n_in:p + n_in + len(c_ops)]
            q = p + n_in + len(c_ops)
            outs, couts = refs[q:q + n_out], refs[q + n_out:q + n_out + len(c_outs)]
            q += n_out + len(c_outs)
            scr = refs[q:q + len(scratch_shapes)]

            def descriptors():
                send_sems, recv_sems = refs[q + len(scratch_shapes):]
                starts, arrivals = [], []
                ci = co = so = 0
                for t, ops, souts in zip(tasks, t_ops, t_outs):
                    s, a = t.make(cins[ci:ci + len(ops)], couts[co:co + len(souts)],
                                  functools.partial(lambda base, k: send_sems.at[base + k], so),
                                  functools.partial(lambda base, k: recv_sems.at[base + k], so))
                    starts += s
                    arrivals += a
                    ci, co, so = ci + len(ops), co + len(souts), so + t.nsem
                return starts, arrivals

            if tasks:
                ids = [pl.program_id(k) for k in range(nax)]
                first = functools.reduce(jnp.logical_and, [i == 0 for i in ids])
                last = functools.reduce(jnp.logical_and, [i == g - 1 for i, g in zip(ids, grid)])

                @pl.when(first)
                def _():
                    for cp in descriptors()[0]:
                        cp.start()

            body(*pre, *ins, *outs, *scr)

            if tasks:
                @pl.when(last)
                def _():
                    starts, arrivals = descriptors()
                    for arrival in arrivals:
                        arrival().wait_recv()
                    for cp in starts:
                        cp.wait_send()

        sems = [pltpu.SemaphoreType.DMA((nsem,)), pltpu.SemaphoreType.DMA((nsem,))] if tasks else []
        res = pl.pallas_call(
            wrapped, name=name,
            grid_spec=pltpu.PrefetchScalarGridSpec(
                num_scalar_prefetch=num_prefetch, grid=tuple(grid),
                in_specs=in_specs + [ANY_SPEC] * len(c_ops),
                out_specs=out_specs + [ANY_SPEC] * len(c_outs),
                scratch_shapes=scratch_shapes + sems),
            out_shape=out_shape + c_outs,
            input_output_aliases=aliases,
            compiler_params=_cparams(nax),
        )(*operands, *c_ops)
        co = n_out
        for t, souts in zip(tasks, t_outs):
            t.finish(res[co:co + len(souts)])
            co += len(souts)
        for fn in after:
            fn()
        return res[0] if single else list(res[:n_out])

    return run


def _comm_call(name):
    def body(o_ref):
        o_ref[...] = jnp.zeros_like(o_ref)

    _pcall(body, name=name, grid=(1,), in_specs=[], out_specs=_spec((8, 128), lambda i: (0, 0)),
           out_shape=_sds((8, 128), F32))()


def _gemm(name, terms, grid, outs, acc_shape, extras=(), epilogue=None):
    kinds = [t[4] for t in terms]
    nt, ne, no = len(terms), len(extras), len(outs)
    nred = grid[-1]
    nax = len(grid)

    def body(*refs):
        trefs = refs[:2 * nt]
        erefs = refs[2 * nt:2 * nt + ne]
        orefs = refs[2 * nt + ne:2 * nt + ne + no]
        ids = [pl.program_id(k) for k in range(nax)]
        tot = None
        for t in range(nt):
            d = _dot(trefs[2 * t][...], trefs[2 * t + 1][...], kinds[t])
            tot = d if tot is None else tot + d

        def finish(acc):
            if epilogue is None:
                orefs[0][...] = acc.astype(orefs[0].dtype)
            else:
                epilogue(acc, erefs, orefs, ids)

        if nred == 1:
            finish(tot)
        else:
            acc_ref = refs[-1]
            r = ids[-1]

            @pl.when(r == 0)
            def _():
                acc_ref[...] = tot

            @pl.when(r > 0)
            def _():
                acc_ref[...] += tot

            @pl.when(r == nred - 1)
            def _():
                finish(acc_ref[...])

    operands, in_specs = [], []
    for a, a_spec, b, b_spec, _ in terms:
        operands += [a, b]
        in_specs += [a_spec, b_spec]
    for e, e_spec in extras:
        operands.append(e)
        in_specs.append(e_spec)
    scratch = [pltpu.VMEM(tuple(acc_shape), F32)] if nred > 1 else []
    return _pcall(body, name=name, grid=tuple(grid), in_specs=in_specs, out_specs=[o[1] for o in outs],
                  out_shape=[o[0] for o in outs], scratch_shapes=scratch)(*operands)


def _rowwise(name, fn, ins, outs, grid):
    ni = len(ins)
    nax = len(grid)

    def body(*refs):
        ids = [pl.program_id(k) for k in range(nax)]
        fn(refs[:ni], refs[ni:], ids)

    return _pcall(body, name=name, grid=tuple(grid), in_specs=[i[1] for i in ins],
                  out_specs=[o[1] for o in outs], out_shape=[o[0] for o in outs])(*[i[0] for i in ins])


def _ffn_up(name, h, w1buf, w1_idx, w3buf, w3_idx):
    T = h.shape[0]
    tm = min(FFN_ROW_TILE, T)

    def body(h_ref, w1_ref, w3_ref, a_ref, b_ref, s_ref):
        hv = h_ref[...]
        a = _dot(hv, w1_ref[...], "nt")
        b = _dot(hv, w3_ref[...], "nt")
        a_ref[...] = a.astype(BF16)
        b_ref[...] = b.astype(BF16)
        s_ref[...] = ((a * _sigmoid(a)) * b).astype(BF16)

    blk = _spec((tm, D_FF), lambda i: (i, 0))
    return _pcall(
        body, name=name, grid=(T // tm,),
        in_specs=[_spec((tm, D), lambda i: (i, 0)),
                  _spec((N_CHIPS, None, FF_BLK, D), lambda i: (0, w1_idx, 0, 0)),
                  _spec((N_CHIPS, None, FF_BLK, D), lambda i: (0, w3_idx, 0, 0))],
        out_specs=[blk, blk, blk],
        out_shape=[_sds((T, D_FF), BF16)] * 3,
    )(h, w1buf, w3buf)


def _ffn_down(name, s, wrow2, w2_idx, x_res, g_next=None):
    T = x_res.shape[0]
    tm = min(ROW_TILE, T)
    row = lambda i, j, r: (i, 0)

    def epilogue(acc, erefs, orefs, ids):
        xo = erefs[0][...] + 0.5 * acc
        orefs[0][...] = xo
        if g_next is not None:
            orefs[1][...] = _rms_fwd(xo, erefs[1][...]).astype(BF16)

    extras = [(x_res, _spec((tm, D), row))]
    outs = [(_sds((T, D), F32), _spec((tm, D), row))]
    if g_next is not None:
        extras.append((g_next, _spec((1, D), lambda i, j, r: (0, 0))))
        outs.append((_sds((T, D), BF16), _spec((tm, D), row)))
    return _gemm(
        name,
        [(s, _spec((tm, D_FF), row),
          wrow2, _spec((N_CHIPS, None, FF_BLK, D), lambda i, j, r: (0, w2_idx, 0, 0)), "nn")],
        (T // tm, 1, 1), outs, (tm, D), extras, epilogue)


def _ffn_bwd_mid(name, dx, wrow2, w2_idx, a, b):
    T = dx.shape[0]
    tm = min(FFN_ROW_TILE, T)

    def body(dx_ref, w2_ref, a_ref, b_ref, dab_ref):
        ds = _dot(0.5 * dx_ref[...], w2_ref[...], "nt")
        av = a_ref[...].astype(F32)
        sg = _sigmoid(av)
        dab_ref[0] = (ds * b_ref[...].astype(F32) * (sg * (1.0 + av * (1.0 - sg)))).astype(BF16)
        dab_ref[1] = (ds * (av * sg)).astype(BF16)

    blk = _spec((tm, D_FF), lambda i: (i, 0))
    return _pcall(
        body, name=name, grid=(T // tm,),
        in_specs=[_spec((tm, D), lambda i: (i, 0)),
                  _spec((N_CHIPS, None, FF_BLK, D), lambda i: (0, w2_idx, 0, 0)),
                  blk, blk],
        out_specs=_spec((2, tm, D_FF), lambda i: (0, i, 0)),
        out_shape=_sds((2, T, D_FF), BF16),
    )(dx, wrow2, a, b)


def _rms_bwd_epilogue(acc, erefs, orefs, ids):
    dx, dgp = _rms_bwd(erefs[0][...], erefs[1][...], acc)
    orefs[0][...] = dx + erefs[2][...]
    _accumulate(orefs[1], dgp, ids[0] == 0)


def _rms_bwd_io(x, g, dres, T, tm):
    row = lambda i, j, r: (i, 0)
    vec = lambda i, j, r: (0, 0)
    extras = [(x, _spec((tm, D), row)), (g, _spec((1, D), vec)), (dres, _spec((tm, D), row))]
    outs = [(_sds((T, D), F32), _spec((tm, D), row)), (_sds((1, D), F32), _spec((1, D), vec))]
    return extras, outs


def _ffn_bwd(tag, dx_out, h, a, b, s, w1buf, w1_idx, w3buf, w3_idx, wrow2, w2_idx, x_in, g, big):
    T = dx_out.shape[0]
    dab = _ffn_bwd_mid(tag + "_bwd_mid", dx_out, wrow2, w2_idx, a, b)

    def half_scale(acc, erefs, orefs, ids):
        orefs[0][...] = (0.5 * acc).astype(orefs[0].dtype)

    dw_grid = (D_FF // DW_BLK, 1, 1)
    dw_out = [(_sds((D_FF, D), GRAD_WIRE_DTYPE), _spec((DW_BLK, D), lambda j, n, r: (j, 0)))]
    tokens = _spec((T, D), lambda j, n, r: (0, 0))
    big[tag + "_w2"] = _gemm(
        tag + "_dw2", [(s, _spec((T, DW_BLK), lambda j, n, r: (0, j)), dx_out, tokens, "tn")],
        dw_grid, dw_out, (DW_BLK, D), (), half_scale)[0].reshape(1, N_CHIPS, FF_BLK, D)
    for widx, wname in ((0, "_w1"), (1, "_w3")):
        big[tag + wname] = _gemm(
            tag + "_d" + wname[1:],
            [(dab, _spec((None, T, DW_BLK), functools.partial(lambda w, j, n, r: (w, 0, j), widx)), h, tokens, "tn")],
            dw_grid, dw_out, (DW_BLK, D))[0].reshape(1, N_CHIPS, FF_BLK, D)
    tm = min(FFN_ROW_TILE, T)
    extras, outs = _rms_bwd_io(x_in, g, dx_out, T, tm)
    whole = lambda idx: _spec((N_CHIPS, None, FF_BLK, D), lambda i, j, r: (0, idx, 0, 0))
    dx_in, dg = _gemm(
        tag + "_dh",
        [(dab, _spec((None, tm, D_FF), lambda i, j, r: (0, i, 0)), w1buf, whole(w1_idx), "nn"),
         (dab, _spec((None, tm, D_FF), lambda i, j, r: (1, i, 0)), w3buf, whole(w3_idx), "nn")],
        (T // tm, 1, 1), outs, (tm, D), extras, _rms_bwd_epilogue)
    return dx_in, dg


def _proj_sq(name, a, wsq, idx, kind, out_dtype=F32, extras=(), epilogue=None, outs=None):
    M = a.shape[0]
    tm = min(ROW_TILE, M)
    if outs is None:
        outs = [(_sds((M, D), out_dtype), _spec((tm, D), lambda i, j, r: (i, 0)))]
    return _gemm(
        name,
        [(a, _spec((tm, D), lambda i, j, r: (i, 0)),
          wsq, _spec((N_CHIPS, None, SQ_BLK, D), lambda i, j, r: (0, idx, 0, 0)), kind)],
        (M // tm, 1, 1), outs, (tm, D), extras, epilogue)


def _dw_sq(name, a, b):
    M = a.shape[0]
    tk = M
    whole = _gemm(
        name,
        [(a, _spec((tk, D), lambda i, j, r: (r, 0)), b, _spec((tk, D), lambda i, j, r: (r, 0)), "tn")],
        (1, 1, M // tk),
        [(_sds((D, D), GRAD_WIRE_DTYPE), _spec((D, D), lambda i, j, r: (0, 0)))],
        (D, D))[0]
    return whole.reshape(N_CHIPS, SQ_BLK, D)


def _retention_constants(T):
    pos = jnp.arange(T, dtype=F32)
    inv_freq = ROPE_BASE ** (-jnp.arange(0, RET_DK, 2, dtype=F32) / RET_DK)
    ang = pos[:, None] * inv_freq[None, :]
    cosf = jnp.concatenate([jnp.cos(ang), jnp.cos(ang)], axis=1)
    sins = jnp.concatenate([-jnp.sin(ang), jnp.sin(ang)], axis=1)
    lg = jnp.log(1.0 - 2.0 ** (-5.0 - jnp.arange(RET_HEADS, dtype=F32)))
    p = jnp.arange(CHUNK, dtype=F32)
    rel = p[:, None] - p[None, :]
    dmat = jnp.where(rel[None] >= 0, jnp.exp(rel[None] * lg[:, None, None]), 0.0)
    kd = jnp.exp((CHUNK - 1.0 - p)[None, :] * lg[:, None])[:, :, None]
    qd = jnp.exp((p + 1.0)[None, :] * lg[:, None])[:, :, None]
    cd = jnp.exp(CHUNK * lg)[:, None, None]
    return cosf, sins, dmat, kd, qd, cd


def _rot(t, cosv, sinv):
    return t * cosv + pltpu.roll(t, RET_DK // 2, 1) * sinv


def _unrot(t, cosv, sinv):
    return t * cosv - pltpu.roll(t, RET_DK // 2, 1) * sinv


def _ret_const_specs(cm):
    whole = lambda shape: _spec(shape, lambda c: (0,) * len(shape))
    return [
        _spec((CHUNK, RET_DK), lambda c: (cm(c), 0)),
        _spec((CHUNK, RET_DK), lambda c: (cm(c), 0)),
        whole((RET_HEADS, CHUNK, CHUNK)), whole((RET_HEADS, CHUNK, 1)), whole((RET_HEADS, CHUNK, 1)),
        whole((RET_HEADS, 1, 1)),
    ]


def _head(h, width):
    return slice(h * width, (h + 1) * width)


def _ret_fwd(u, consts, ret_gn):
    T = u.shape[0]
    nC = T // CHUNK
    kscale = RET_DK ** -0.5

    def body(q_ref, k_ref, v_ref, g_ref, cos_ref, sin_ref, dm_ref, kd_ref, qd_ref, cd_ref, gn_ref,
             qr_ref, kr_ref, ret_ref, yr_ref, st_ref, state):
        @pl.when(pl.program_id(0) == 0)
        def _():
            state[...] = jnp.zeros_like(state)

        cosv, sinv = cos_ref[...], sin_ref[...]
        for h in range(RET_HEADS):
            hk, hv = _head(h, RET_DK), _head(h, RET_DV)
            q = _rot(q_ref[:, hk], cosv, sinv)
            k = _rot(k_ref[:, hk], cosv, sinv) * kscale
            v = v_ref[:, hv]
            qr_ref[:, hk] = q
            kr_ref[:, hk] = k
            prev = state[h]
            st_ref[h] = prev
            s = _dot(q, k, "nt") * dm_ref[h]
            ret = _dot(s, v, "nn") + _dot(q, prev, "nn") * qd_ref[h]
            state[h] = cd_ref[h] * prev + _dot(k * kd_ref[h], v, "tn")
            ret_ref[:, hv] = ret
            mu = jnp.mean(ret, axis=-1, keepdims=True)
            xc = ret - mu
            yn = xc * lax.rsqrt(jnp.mean(xc * xc, axis=-1, keepdims=True) + EPS)
            g = g_ref[:, hv]
            yr_ref[:, hv] = ((g * _sigmoid(g)) * (yn * gn_ref[:, hv])).astype(BF16)

    cm = lambda c: c
    qk_w, v_w = RET_HEADS * RET_DK, RET_HEADS * RET_DV
    in_specs = [
        _spec((CHUNK, qk_w), lambda c: (c, 0)), _spec((CHUNK, qk_w), lambda c: (c, 1)),
        _spec((CHUNK, v_w), lambda c: (c, 1)), _spec((CHUNK, v_w), lambda c: (c, 2)),
    ] + _ret_const_specs(cm) + [_spec((1, v_w), lambda c: (0, 0))]
    qk_out = _spec((CHUNK, qk_w), lambda c: (c, 0))
    v_out = _spec((CHUNK, v_w), lambda c: (c, 0))
    return _pcall(
        body, name="ret_fwd", grid=(nC,),
        in_specs=in_specs,
        out_specs=[qk_out, qk_out, v_out, v_out,
                   _spec((RET_HEADS, None, RET_DK, RET_DV), lambda c: (0, c, 0, 0))],
        out_shape=[_sds((T, qk_w), F32), _sds((T, qk_w), F32), _sds((T, v_w), F32), _sds((T, v_w), BF16),
                   _sds((RET_HEADS, nC, RET_DK, RET_DV), F32)],
        scratch_shapes=[pltpu.VMEM((RET_HEADS, RET_DK, RET_DV), F32)],
    )(u, u, u, u, *consts, ret_gn)


def _ret_bwd(dyr, ret, u, qr, kr, states, consts, ret_gn):
    T = u.shape[0]
    nC = T // CHUNK
    kscale = RET_DK ** -0.5

    def body(dyr_ref, ret_ref, g_ref, q_ref, k_ref, v_ref, st_ref,
             cos_ref, sin_ref, dm_ref, kd_ref, qd_ref, cd_ref, gn_ref,
             dq_ref, dk_ref, dv_ref, dg_ref, dgn_ref, gstate):
        first = pl.program_id(0) == 0

        @pl.when(first)
        def _():
            gstate[...] = jnp.zeros_like(gstate)

        cosv, sinv = cos_ref[...], sin_ref[...]
        dgn_parts = []
        for h in range(RET_HEADS):
            hk, hv = _head(h, RET_DK), _head(h, RET_DV)
            ret = ret_ref[:, hv]
            mu = jnp.mean(ret, axis=-1, keepdims=True)
            xc = ret - mu
            rs = lax.rsqrt(jnp.mean(xc * xc, axis=-1, keepdims=True) + EPS)
            yn = xc * rs
            gn = gn_ref[:, hv]
            g = g_ref[:, hv]
            sg = _sigmoid(g)
            dyr_v = dyr_ref[:, hv]
            dretn = dyr_v * (g * sg)
            dg_ref[:, hv] = (dyr_v * (yn * gn) * (sg * (1.0 + g * (1.0 - sg)))).astype(BF16)
            dgn_parts.append(jnp.sum(dretn * yn, axis=0, keepdims=True))
            dyn = dretn * gn
            d_o = rs * (dyn - jnp.mean(dyn, axis=-1, keepdims=True)
                        - yn * jnp.mean(dyn * yn, axis=-1, keepdims=True))

            q, k, v = q_ref[:, hk], k_ref[:, hk], v_ref[:, hv]
            dmat, kd, qd = dm_ref[h], kd_ref[h], qd_ref[h]
            prev = st_ref[h]
            gnext = gstate[h]
            s = _dot(q, k, "nt") * dmat
            ds = _dot(d_o, v, "nt") * dmat
            doq = d_o * qd
            dq = _dot(ds, k, "nn") + _dot(doq, prev, "nt")
            dk = _dot(ds, q, "tn") + _dot(v, gnext, "nt") * kd
            dv = _dot(s, d_o, "tn") + _dot(k * kd, gnext, "nn")
            gstate[h] = cd_ref[h] * gnext + _dot(q, doq, "tn")
            dq_ref[:, hk] = _unrot(dq, cosv, sinv).astype(BF16)
            dk_ref[:, hk] = _unrot(dk * kscale, cosv, sinv).astype(BF16)
            dv_ref[:, hv] = dv.astype(BF16)
        _accumulate(dgn_ref, jnp.concatenate(dgn_parts, axis=1), first)

    cm = lambda c: nC - 1 - c
    qk_w, v_w = RET_HEADS * RET_DK, RET_HEADS * RET_DV
    vspec = lambda blk: _spec((CHUNK, v_w), lambda c: (cm(c), blk))
    qspec = _spec((CHUNK, qk_w), lambda c: (cm(c), 0))
    in_specs = [vspec(0), vspec(0), vspec(2), qspec, qspec, vspec(1),
                _spec((RET_HEADS, None, RET_DK, RET_DV), lambda c: (0, cm(c), 0, 0)),
                ] + _ret_const_specs(cm) + [_spec((1, v_w), lambda c: (0, 0))]
    return _pcall(
        body, name="ret_bwd", grid=(nC,),
        in_specs=in_specs,
        out_specs=[qspec, qspec, vspec(0), vspec(0), _spec((1, v_w), lambda c: (0, 0))],
        out_shape=[_sds((T, qk_w), BF16), _sds((T, qk_w), BF16), _sds((T, v_w), BF16), _sds((T, v_w), BF16),
                   _sds((1, v_w), F32)],
        scratch_shapes=[pltpu.VMEM((RET_HEADS, RET_DK, RET_DV), F32)],
    )(dyr, ret, u, qr, kr, u, states, *consts, ret_gn)


def _shift_down(x, s):
    rows = lax.broadcasted_iota(jnp.int32, x.shape, 0)
    return jnp.where(rows >= s, pltpu.roll(x, s, 0), 0.0)


def _shift_up(x, s):
    n = x.shape[0]
    rows = lax.broadcasted_iota(jnp.int32, x.shape, 0)
    return jnp.where(rows < n - s, pltpu.roll(x, n - s, 0), 0.0)


def _lru_specs(T):
    col = lambda off: _spec((T, LRU_BLOCK), lambda g: (0, off + g))
    vec = _spec((1, LRU_BLOCK), lambda g: (0, g))
    wblk = _spec((None, LRU_BLOCK, LRU_BLOCK), lambda g: (g, 0, 0))
    cw = _spec((CONV_TAPS, LRU_BLOCK), lambda g: (0, g))
    return col, vec, wblk, cw


def _lru_gates_fwd(u, conv_w, conv_b, w_r, b_r, w_i, b_i, lam):
    T = u.shape[0]
    col, vec, wblk, cw = _lru_specs(T)

    def body(x_ref, cw_ref, cb_ref, wr_ref, br_ref, wi_ref, bi_ref, lam_ref,
             xc_ref, r_ref, i_ref, a_ref, bx_ref):
        x = x_ref[...]
        w = cw_ref[...]
        xc = (_shift_down(x, 3) * w[0:1] + _shift_down(x, 2) * w[1:2] + _shift_down(x, 1) * w[2:3]
              + x * w[3:4] + cb_ref[...])
        r = _sigmoid(_dot(xc, wr_ref[...], "nn") + br_ref[...])
        i = _sigmoid(_dot(xc, wi_ref[...], "nn") + bi_ref[...])
        la = (-LRU_C) * r * _softplus(-lam_ref[...])
        xc_ref[...] = xc
        r_ref[...] = r
        i_ref[...] = i
        a_ref[...] = jnp.exp(la)
        bx_ref[...] = jnp.sqrt(-_expm1(2.0 * la)) * (i * xc)

    out = col(0)
    return _pcall(
        body, name="lru_gates_fwd", grid=(LRU_BLOCKS,),
        in_specs=[col(24), cw, vec, wblk, vec, wblk, vec, vec],
        out_specs=[out] * 5,
        out_shape=[_sds((T, D), F32)] * 5,
    )(u, conv_w, conv_b, w_r, b_r, w_i, b_i, lam)


def _lru_scan(name, a3, b3, reverse):
    T = a3.shape[0]
    nt = T // SCAN_TILE
    unroll = 8

    def body(a_ref, b_ref, o_ref, carry):
        @pl.when(pl.program_id(0) == 0)
        def _():
            carry[...] = jnp.zeros_like(carry)

        if not reverse:
            def step(t, h):
                h = a_ref[t] * h + b_ref[t]
                o_ref[t] = h
                return h
        else:
            def step(k, c):
                t = SCAN_TILE - 1 - k
                l = b_ref[t] + c
                o_ref[t] = l
                return a_ref[t] * l
        carry[...] = lax.fori_loop(0, SCAN_TILE, step, carry[...], unroll=unroll)

    idx = (lambda i: (nt - 1 - i, 0, 0)) if reverse else (lambda i: (i, 0, 0))
    blk = _spec((SCAN_TILE, LRU_BLOCKS, LRU_BLOCK), idx)
    return _pcall(
        body, name=name, grid=(nt,),
        in_specs=[blk, blk], out_specs=blk,
        out_shape=_sds((T, LRU_BLOCKS, LRU_BLOCK), F32),
        scratch_shapes=[pltpu.VMEM((LRU_BLOCKS, LRU_BLOCK), F32)],
    )(a3, b3)


def _lru_gates_bwd(lmb, hl, a, r, i, xc, u, conv_w, w_r, w_i, lam):
    T = u.shape[0]
    col, vec, wblk, cw = _lru_specs(T)

    def body(l_ref, h_ref, a_ref, r_ref, i_ref, xc_ref, x_ref, cw_ref, wr_ref, wi_ref, lam_ref,
             dx_ref, dwr_ref, dwi_ref, dvec_ref, dcw_ref):
        l = l_ref[...]
        av, rv, iv, xc = a_ref[...], r_ref[...], i_ref[...], xc_ref[...]
        lam_v = lam_ref[...]
        sp = _softplus(-lam_v)
        la = (-LRU_C) * rv * sp
        mult = jnp.sqrt(-_expm1(2.0 * la))
        da = l * _shift_down(h_ref[...], 1)
        dmult = l * (iv * xc)
        di = l * mult * xc
        dxc = l * mult * iv
        dla = da * av - dmult * (av * av) / mult
        dzr = (dla * ((-LRU_C) * sp)) * rv * (1.0 - rv)
        dzi = di * iv * (1.0 - iv)
        dsp = jnp.sum(dla * ((-LRU_C) * rv), axis=0, keepdims=True)
        dlam = dsp * (-_sigmoid(-lam_v))
        dwr_ref[...] = _dot(xc, dzr, "tn")
        dwi_ref[...] = _dot(xc, dzi, "tn")
        dxc = dxc + _dot(dzr, wr_ref[...], "nt") + _dot(dzi, wi_ref[...], "nt")
        x = x_ref[...]
        w = cw_ref[...]
        dx = (dxc * w[3:4] + _shift_up(dxc, 1) * w[2:3] + _shift_up(dxc, 2) * w[1:2]
              + _shift_up(dxc, 3) * w[0:1])
        dx_ref[...] = dx.astype(BF16)
        dvec_ref[...] = jnp.concatenate(
            [jnp.sum(dzr, axis=0, keepdims=True), jnp.sum(dzi, axis=0, keepdims=True), dlam,
             jnp.sum(dxc, axis=0, keepdims=True)], axis=0)
        dcw_ref[...] = jnp.concatenate(
            [jnp.sum(dxc * _shift_down(x, 3 - tap), axis=0, keepdims=True) if tap < 3
             else jnp.sum(dxc * x, axis=0, keepdims=True) for tap in range(CONV_TAPS)], axis=0)

    c0 = col(0)
    return _pcall(
        body, name="lru_gates_bwd", grid=(LRU_BLOCKS,),
        in_specs=[c0, c0, c0, c0, c0, c0, col(24), cw, wblk, wblk, vec],
        out_specs=[c0, wblk, wblk, cw, cw],
        out_shape=[_sds((T, D), BF16), _sds((LRU_BLOCKS, LRU_BLOCK, LRU_BLOCK), F32),
                   _sds((LRU_BLOCKS, LRU_BLOCK, LRU_BLOCK), F32), _sds((4, D), F32), _sds((CONV_TAPS, D), F32)],
    )(lmb, hl, a, r, i, xc, u, conv_w, w_r, w_i, lam)


def _xattn_probs(q, k):
    sc = _dot(q, k, "nt") * (X_HD ** -0.5)
    e = jnp.exp(sc - jnp.max(sc, axis=-1, keepdims=True))
    return e / jnp.sum(e, axis=-1, keepdims=True)


def _xattn_fwd(xq, xk, xv):
    T = xq.shape[0]
    tq = ROW_TILE
    M = xk.shape[0]

    def body(q_ref, k_ref, v_ref, o_ref):
        p = _xattn_probs(q_ref[...], k_ref[...])
        o_ref[...] = _dot(p, v_ref[...], "nn").astype(BF16)

    qs = _spec((tq, X_HD), lambda h, i: (i, h))
    kv = _spec((M, X_HD), lambda h, i: (0, h))
    return _pcall(
        body, name="xattn_fwd", grid=(X_HEADS, T // tq),
        in_specs=[qs, kv, kv], out_specs=qs, out_shape=_sds((T, D), BF16),
    )(xq, xk, xv)


def _xattn_bwd(xq, xk, xv, dxo):
    T = xq.shape[0]
    tq = ROW_TILE
    M = xk.shape[0]

    def body(q_ref, k_ref, v_ref, do_ref, dq_ref, dk_ref, dv_ref):
        first = pl.program_id(1) == 0
        q, k, v, do = q_ref[...], k_ref[...], v_ref[...], do_ref[...]
        p = _xattn_probs(q, k)
        dp = _dot(do, v, "nt")
        ds = p * (dp - jnp.sum(dp * p, axis=-1, keepdims=True)) * (X_HD ** -0.5)
        dq_ref[...] = _dot(ds, k, "nn").astype(BF16)
        _accumulate(dk_ref, _dot(ds, q, "tn"), first)
        _accumulate(dv_ref, _dot(p, do, "tn"), first)

    qs = _spec((tq, X_HD), lambda h, i: (i, h))
    kv = _spec((M, X_HD), lambda h, i: (0, h))
    return _pcall(
        body, name="xattn_bwd", grid=(X_HEADS, T // tq),
        in_specs=[qs, kv, kv, qs], out_specs=[qs, kv, kv],
        out_shape=[_sds((T, D), BF16), _sds((M, D), F32), _sds((M, D), F32)],
    )(xq, xk, xv, dxo)


def _final_loss(x, g, tgt):
    T = x.shape[0]
    tm = ROW_TILE

    def fn(irefs, orefs, ids):
        xv, gv = irefs[0][...], irefs[1][...]
        err = _rms_fwd(xv, gv) - irefs[2][...]
        lp = 0.5 * jnp.sum(jnp.mean(err * err, axis=-1, keepdims=True), axis=0, keepdims=True)
        first = ids[0] == 0
        _accumulate(orefs[0], jnp.broadcast_to(lp, (1, 128)), first)
        dx, dgp = _rms_bwd(xv, gv, err * (1.0 / D))
        orefs[1][...] = dx
        _accumulate(orefs[2], dgp, first)

    row = _spec((tm, D), lambda i: (i, 0))
    vec = _spec((1, D), lambda i: (0, 0))
    return _rowwise(
        "final_loss", fn, [(x, row), (g, vec), (tgt, row)],
        [(_sds((1, 128), F32), _spec((1, 128), lambda i: (0, 0))), (_sds((T, D), F32), row),
         (_sds((1, D), F32), vec)],
        (T // tm,))


def _adamw(name, w, g, m, v):
    R, C = w.shape
    tr = R
    for cand in (512, 352, 256):
        if R % cand == 0:
            tr = cand
            break

    def fn(irefs, orefs, ids):
        delta, mn, vn = _adamw_update(*(r[...] for r in irefs))
        orefs[0][...] = delta
        orefs[1][...] = mn
        orefs[2][...] = vn

    blk = _spec((tr, C), lambda i: (i, 0))
    return _rowwise(name, fn, [(w, blk), (g, blk), (m, blk), (v, blk)],
                    [(_sds((R, C), F32), blk)] * 3, (R // tr,))


def _adamw_update(wv, gv, mv, vv):
    c1 = 1.0 - ADAM_B1 ** ADAM_STEP
    c2 = 1.0 - ADAM_B2 ** ADAM_STEP
    mn = ADAM_B1 * mv + (1.0 - ADAM_B1) * gv
    vn = ADAM_B2 * vv + (1.0 - ADAM_B2) * (gv * gv)
    delta = -ADAM_LR * ((mn / c1) / (jnp.sqrt(vn / c2) + ADAM_EPS) + ADAM_WD * wv)
    return delta, mn, vn


def _adamw_halves(name, w, mine, theirs, widx, m, v, core):
    R, C = w.shape
    H = R // 2
    tr = H
    while tr * C * 4 > (1 << 20) and tr % 16 == 0:
        tr //= 2
    nb = H // tr

    def body(core_ref, w_ref, mine_ref, theirs_ref, m_ref, v_ref, g_out, d_out, m_out, v_out):
        gv = jnp.where(pl.program_id(0) == core_ref[0], mine_ref[...], theirs_ref[...])
        delta, mn, vn = _adamw_update(w_ref[...], gv, m_ref[...], v_ref[...])
        g_out[...] = gv
        d_out[...] = delta
        m_out[...] = mn
        v_out[...] = vn

    full = pl.BlockSpec((tr, C), lambda h, i, core_ref: (h * nb + i, 0))
    mine_spec = pl.BlockSpec((None, tr, C), lambda h, i, core_ref: (widx, jnp.where(h == core_ref[0], i, 0), 0))
    theirs_spec = pl.BlockSpec((None, tr, C), lambda h, i, core_ref: (widx, jnp.where(h == core_ref[0], 0, i), 0))
    return _pcall(
        body, name=name, grid=(2, nb), num_prefetch=1,
        in_specs=[full, mine_spec, theirs_spec, full, full], out_specs=[full] * 4,
        out_shape=[_sds((R, C), F32)] * 4,
    )(core, w, mine, theirs, m, v)


def _rmsnorm(name, x, g):
    M = x.shape[0]
    tm = min(ROW_TILE, M)

    def fn(irefs, orefs, ids):
        orefs[0][...] = _rms_fwd(irefs[0][...], irefs[1][...]).astype(BF16)

    row = _spec((tm, D), lambda i: (i, 0))
    return _rowwise(name, fn, [(x, row), (g, _spec((1, D), lambda i: (0, 0)))],
                    [(_sds((M, D), BF16), row)], (M // tm,))[0]


WEIGHT_AT = {
    "ffn1_w1": ("col1", 0), "ffn1_w3": ("col1", 1), "ffn1_w2": ("row2a", 0),
    "w_ret_o": ("sqA", 0), "w_lru_o": ("sqA", 1), "w_out": ("sqA", 2),
    "w_xq": ("sqB", 0), "w_xk": ("sqB", 1), "w_xv": ("sqC", 0), "w_xo": ("sqC", 1),
    "ffn2_w1": ("col2a", 0), "ffn2_w3": ("col2b", 0), "ffn2_w2": ("row2b", 0),
}


def _local_step(x, mem, tgt, gw, sm, big):
    T = x.shape[0]
    tm = ROW_TILE

    def wt(name):
        key, idx = WEIGHT_AT[name]
        return gw[key], idx

    row3 = lambda i, j, r: (i, 0)
    vec3 = lambda i, j, r: (0, 0)
    rowD = _spec((tm, D), row3)
    vecD = _spec((1, D), vec3)

    def residual_norm(acc, erefs, orefs, ids):
        xo = erefs[0][...] + acc
        orefs[0][...] = xo
        orefs[1][...] = _rms_fwd(xo, erefs[1][...]).astype(BF16)

    def res_norm_io(x_res, g):
        return ([(x_res, rowD), (g, vecD)],
                [(_sds((T, D), F32), rowD), (_sds((T, D), BF16), rowD)])

    h1 = _rmsnorm("ffn1_norm", x, sm["ffn1_norm"])
    a1, b1, s1 = _ffn_up("ffn1_up", h1, *wt("ffn1_w1"), *wt("ffn1_w3"))
    x1, h2 = _ffn_down("ffn1_down", s1, *wt("ffn1_w2"), x, sm["mix_norm"])

    tw = min(WIDE_ROW_TILE, T)
    wideD = _spec((tw, D), row3)
    u = _gemm(
        "mix_in",
        [(h2, wideD, gw["win"], _spec((None, None, IN_BLK, D), lambda i, j, r: (j, 0, 0, 0)), "nt")],
        (T // tw, N_CHIPS, 1),
        [(_sds((T, 5120), F32), _spec((tw, IN_BLK), lambda i, j, r: (i, j)))], (tw, IN_BLK))[0]

    consts = _retention_constants(T)
    qr, kr, ret, yr, states = _ret_fwd(u, consts, sm["ret_gn"])

    conv_w = gw["conv"][:, 0].transpose(1, 0, 2).reshape(CONV_TAPS, D)
    xc, rg, ig, av, bx = _lru_gates_fwd(u, conv_w, sm["conv_b"], sm["w_rgate"], sm["b_rgate"],
                                        sm["w_igate"], sm["b_igate"], sm["lru_lambda"])
    a3 = av.reshape(T, LRU_BLOCKS, LRU_BLOCK)
    hl = _lru_scan("lru_scan_fwd", a3, bx.reshape(T, LRU_BLOCKS, LRU_BLOCK), False).reshape(T, D)

    row1 = _spec((tm, D), lambda i: (i, 0))
    glru1 = _spec((tm, D), lambda i: (i, 4))

    def lru_out(irefs, orefs, ids):
        gl, _ = _gelu_and_grad(irefs[1][...])
        orefs[0][...] = (irefs[0][...] * gl).astype(BF16)

    yl = _rowwise("lru_out", lru_out, [(hl, row1), (u, glru1)], [(_sds((T, D), BF16), row1)], (T // tm,))[0]

    def gate_epilogue(acc, erefs, orefs, ids):
        orefs[0][...] = _sigmoid(acc + erefs[0][...])

    gates = _gemm(
        "mix_gates",
        [(h2, wideD, gw["wbg"], _spec((None, None, BG_BLK, D), lambda i, j, r: (j, 0, 0, 0)), "nt")],
        (T // tw, N_CHIPS, 1),
        [(_sds((T, 2 * D), F32), _spec((tw, BG_BLK), lambda i, j, r: (i, j)))], (tw, BG_BLK),
        [(sm["b_branch_gate"], _spec((1, BG_BLK), lambda i, j, r: (0, j)))], gate_epilogue)[0]

    y_ret = _proj_sq("y_ret", yr, *wt("w_ret_o"), "nn")[0]

    def merge_epilogue(acc, erefs, orefs, ids):
        orefs[0][...] = acc
        orefs[1][...] = (erefs[0][...] * erefs[2][...] + erefs[1][...] * acc).astype(BF16)

    y_lru, merged = _proj_sq(
        "y_lru", yl, *wt("w_lru_o"), "nn",
        extras=[(gates, _spec((tm, D), lambda i, j, r: (i, 0))), (gates, _spec((tm, D), lambda i, j, r: (i, 1))),
                (y_ret, rowD)],
        epilogue=merge_epilogue,
        outs=[(_sds((T, D), F32), rowD), (_sds((T, D), BF16), rowD)])

    ex, ou = res_norm_io(x1, sm["xattn_norm"])
    x2, hq = _proj_sq("mix_out", merged, *wt("w_out"), "nn", extras=ex, epilogue=residual_norm, outs=ou)

    m = _rmsnorm("mem_norm", mem, sm["mem_norm"])
    xq = _proj_sq("xq", hq, *wt("w_xq"), "nn", BF16)[0]
    xk = _proj_sq("xk", m, *wt("w_xk"), "nn", BF16)[0]
    xv = _proj_sq("xv", m, *wt("w_xv"), "nn", BF16)[0]
    xo = _xattn_fwd(xq, xk, xv)
    ex, ou = res_norm_io(x2, sm["ffn2_norm"])
    x3, h3 = _proj_sq("xattn_out", xo, *wt("w_xo"), "nn", extras=ex, epilogue=residual_norm, outs=ou)

    a2, b2, s2 = _ffn_up("ffn2_up", h3, *wt("ffn2_w1"), *wt("ffn2_w3"))
    x4 = _ffn_down("ffn2_down", s2, *wt("ffn2_w2"), x3)[0]
    loss, dx4, dg_final = _final_loss(x4, sm["final_norm"], tgt)

    dx3, dg_ffn2 = _ffn_bwd("ffn2", dx4, h3, a2, b2, s2, *wt("ffn2_w1"), *wt("ffn2_w3"),
                            *wt("ffn2_w2"), x3, sm["ffn2_norm"], big)

    dxo = _proj_sq("d_xo", dx3, *wt("w_xo"), "nt", BF16)[0]
    big["w_xo"] = _dw_sq("dw_xo", xo, dx3)[None]
    dxq, dxk, dxv = _xattn_bwd(xq, xk, xv, dxo)
    big["w_xq"] = _dw_sq("dw_xq", hq, dxq)[None]
    ex, ou = _rms_bwd_io(x2, sm["xattn_norm"], dx3, T, tm)
    dx2, dg_xattn = _proj_sq("d_hq", dxq, *wt("w_xq"), "nt", extras=ex, epilogue=_rms_bwd_epilogue, outs=ou)
    big["w_xk"] = _dw_sq("dw_xk", m, dxk)[None]
    big["w_xv"] = _dw_sq("dw_xv", m, dxv)[None]

    M = mem.shape[0]

    def mem_norm_epilogue(acc, erefs, orefs, ids):
        _, dgp = _rms_bwd(erefs[0][...], erefs[1][...], acc)
        orefs[0][...] = dgp

    wsq_spec = lambda idx: _spec((N_CHIPS, None, SQ_BLK, D), lambda i, j, r: (0, idx, 0, 0))
    memD = _spec((M, D), row3)
    dg_mem = _gemm(
        "d_mem_norm",
        [(dxk, memD, wt("w_xk")[0], wsq_spec(wt("w_xk")[1]), "nt"),
         (dxv, memD, wt("w_xv")[0], wsq_spec(wt("w_xv")[1]), "nt")],
        (1, 1, 1), [(_sds((1, D), F32), vecD)], (M, D),
        [(mem, memD), (sm["mem_norm"], vecD)], mem_norm_epilogue)[0]

    def merged_bwd_epilogue(acc, erefs, orefs, ids):
        gr, gl, yrv, ylv = (e[...] for e in erefs)
        orefs[0][...] = (acc * gr).astype(BF16)
        orefs[1][...] = (acc * gl).astype(BF16)
        dgr = acc * yrv * gr * (1.0 - gr)
        dgl = acc * ylv * gl * (1.0 - gl)
        orefs[2][:, :D] = dgr.astype(BF16)
        orefs[2][:, D:] = dgl.astype(BF16)
        dbb = jnp.concatenate([jnp.sum(dgr, axis=0, keepdims=True), jnp.sum(dgl, axis=0, keepdims=True)], axis=1)
        _accumulate(orefs[3], dbb, ids[0] == 0)

    dy_ret, dy_lru, dgpre, db_bg = _proj_sq(
        "d_merged", dx2, *wt("w_out"), "nt",
        extras=[(gates, _spec((tm, D), lambda i, j, r: (i, 0))), (gates, _spec((tm, D), lambda i, j, r: (i, 1))),
                (y_ret, rowD), (y_lru, rowD)],
        epilogue=merged_bwd_epilogue,
        outs=[(_sds((T, D), BF16), rowD), (_sds((T, D), BF16), rowD),
              (_sds((T, 2 * D), BF16), _spec((tm, 2 * D), row3)),
              (_sds((1, 2 * D), F32), _spec((1, 2 * D), vec3))])
    big["w_branch_gate"] = _gemm(
        "dw_bg",
        [(h2, _spec((T, D), lambda j, n, r: (r, 0)), dgpre, _spec((T, BG_BLK), lambda j, n, r: (r, j)), "tn")],
        (N_CHIPS, 1, 1),
        [(_sds((N_CHIPS, D, BG_BLK), GRAD_WIRE_DTYPE), _spec((None, D, BG_BLK), lambda j, n, r: (j, 0, 0)))],
        (D, BG_BLK))[0][None]
    big["w_out"] = _dw_sq("dw_out", merged, dx2)[None]
    dyr = _proj_sq("d_yr", dy_ret, *wt("w_ret_o"), "nt")[0]
    big["w_ret_o"] = _dw_sq("dw_ret_o", yr, dy_ret)[None]
    dyl = _proj_sq("d_yl", dy_lru, *wt("w_lru_o"), "nt")[0]
    big["w_lru_o"] = _dw_sq("dw_lru_o", yl, dy_lru)[None]

    dq, dk, dv, dgr, dg_retgn = _ret_bwd(dyr, ret, u, qr, kr, states, consts, sm["ret_gn"])

    def lru_out_bwd(irefs, orefs, ids):
        gl, dgl = _gelu_and_grad(irefs[2][...])
        dyl_v = irefs[0][...]
        orefs[0][...] = dyl_v * gl
        orefs[1][...] = (dyl_v * irefs[1][...] * dgl).astype(BF16)

    dhl, dglru = _rowwise("lru_out_bwd", lru_out_bwd, [(dyl, row1), (hl, row1), (u, glru1)],
                          [(_sds((T, D), F32), row1), (_sds((T, D), BF16), row1)], (T // tm,))
    lmb = _lru_scan("lru_scan_bwd", a3, dhl.reshape(T, LRU_BLOCKS, LRU_BLOCK), True).reshape(T, D)
    dxl, dw_r, dw_i, dvec, dcw = _lru_gates_bwd(lmb, hl, av, rg, ig, xc, u, conv_w,
                                                sm["w_rgate"], sm["w_igate"], sm["lru_lambda"])

    du = jnp.concatenate([dq, dk, dv, dgr, dxl, dglru], axis=1)
    tk = T
    big["w_in"] = _gemm(
        "dw_in",
        [(h2, _spec((tk, D), lambda j, n, r: (r, 0)), du, _spec((tk, IN_BLK), lambda j, n, r: (r, j)), "tn")],
        (N_CHIPS, 1, T // tk),
        [(_sds((N_CHIPS, D, IN_BLK), GRAD_WIRE_DTYPE), _spec((None, D, IN_BLK), lambda j, n, r: (j, 0, 0)))],
        (D, IN_BLK))[0][None]
    tf = min(FFN_ROW_TILE, T)
    ex, ou = _rms_bwd_io(x1, sm["mix_norm"], dx2, T, tf)
    dx1, dg_mix = _gemm(
        "d_h2",
        [(du, _spec((tf, 5120), row3), gw["win"], _spec((N_CHIPS, None, IN_BLK, D), lambda i, j, r: (0, 0, 0, 0)), "nn"),
         (dgpre, _spec((tf, 2 * D), row3), gw["wbg"], _spec((N_CHIPS, None, BG_BLK, D), lambda i, j, r: (0, 0, 0, 0)),
          "nn")],
        (T // tf, 1, 1), ou, (tf, D), ex, _rms_bwd_epilogue)

    grad_x, dg_ffn1 = _ffn_bwd("ffn1", dx1, h1, a1, b1, s1, *wt("ffn1_w1"), *wt("ffn1_w3"),
                               *wt("ffn1_w2"), x, sm["ffn1_norm"], big)

    small = {
        "ffn1_norm": dg_ffn1, "mix_norm": dg_mix, "ret_gn": dg_retgn, "conv_b": dvec[3:4],
        "b_rgate": dvec[0:1], "b_igate": dvec[1:2], "lru_lambda": dvec[2:3], "xattn_norm": dg_xattn,
        "mem_norm": dg_mem, "ffn2_norm": dg_ffn2, "final_norm": dg_final, "b_branch_gate": db_bg,
        "conv_w": dcw, "w_rgate": dw_r, "w_igate": dw_i,
    }
    return loss, grad_x, small


ANY_SPEC = pl.BlockSpec(memory_space=pl.ANY)
VMEM_SPEC = pl.BlockSpec(memory_space=pltpu.VMEM)
N_PEER_CHIPS = N_CHIPS - 1


def _mesh_position():
    x, y, c = lax.axis_index("x"), lax.axis_index("y"), lax.axis_index("c")
    chips = [(1 - x, y), (x, 1 - y), (1 - x, 1 - y)]
    return x, y, c, chips


def _chip_index(x, y):
    return 2 * x + y


def _rows_half(ref, axis, h):
    n = ref.shape[axis] // 2
    idx = [slice(None)] * len(ref.shape)
    idx[axis] = pl.ds(pl.multiple_of(h * n, 16), n)
    return ref.at[tuple(idx)]


def _remote(src, dst, send_sem, recv_sem, device):
    return pltpu.make_async_remote_copy(src_ref=src, dst_ref=dst, send_sem=send_sem, recv_sem=recv_sem,
                                        device_id=device, device_id_type=MESH)


def _gather_chips_task(shards, split, landed, part=0, nparts=1):
    keys = list(shards)
    n = len(keys)

    def operands():
        if part:
            return [shards[k] for k in keys] + [landed[k] for k in keys]
        chip_me = _chip_index(lax.axis_index("x"), lax.axis_index("y"))
        bases = [lax.dynamic_update_slice(lax.empty((N_CHIPS,) + shards[k].shape, shards[k].dtype), shards[k][None],
                                          (chip_me,) + (0,) * shards[k].ndim) for k in keys]
        return [shards[k] for k in keys] + bases

    def my_rows(ref, c):
        rows = ref.shape[1] // (2 * nparts)
        return ref.at[:, pl.ds(pl.multiple_of((c * nparts + part) * rows, 16), rows), :]

    def make(ins, outs, send_sem, recv_sem):
        x, y, c, chips = _mesh_position()
        s_me = _chip_index(x, y)
        starts, arrivals = [], []
        for g in range(n):
            mine = my_rows(ins[g], c) if split else ins[g]
            for k, chip in enumerate(chips):
                def landing(s):
                    o = outs[g].at[s]
                    return my_rows(o, c) if split else o
                starts.append(_remote(mine, landing(s_me), send_sem(3 * g + k), recv_sem(3 * g + k), (*chip, c)))
                got = landing(_chip_index(*chip))
                arrivals.append(functools.partial(_remote, got, got, send_sem(3 * g + k), recv_sem(3 * g + k),
                                                  (*chip, c)))
        return starts, arrivals

    def finish(res):
        landed.update(zip(keys, res))

    return _Task(operands, lambda: [_sds((N_CHIPS,) + shards[k].shape, shards[k].dtype) for k in keys],
                 {n + g: g for g in range(n)}, 3 * n, make, finish)


def _gather_sibling_task(keys, landed, ready):
    n = len(keys)

    def make(ins, outs, send_sem, recv_sem):
        x, y, c, chips = _mesh_position()
        starts, arrivals = [], []
        for g in range(n):
            for k, chip in enumerate(chips):
                o = outs[g].at[_chip_index(*chip)]
                got, other = _rows_half(o, 1, c), _rows_half(o, 1, 1 - c)
                starts.append(_remote(got, got, send_sem(3 * g + k), recv_sem(3 * g + k), (x, y, 1 - c)))
                arrivals.append(functools.partial(_remote, other, other, send_sem(3 * g + k), recv_sem(3 * g + k),
                                                  (x, y, 1 - c)))
        return starts, arrivals

    def finish(res):
        ready.update(zip(keys, res))

    return _Task(lambda: [landed[k] for k in keys], lambda: [_sds(landed[k].shape, landed[k].dtype) for k in keys],
                 {g: g for g in range(n)}, 3 * n, make, finish)


def _pair_swap_task(names, big, got):
    n = len(names)

    def make(ins, outs, send_sem, recv_sem):
        x, y, c, _ = _mesh_position()
        copies = [_remote(_rows_half(ins[a], 2, 1 - c), outs[a], send_sem(a), recv_sem(a), (x, y, 1 - c))
                  for a in range(n)]
        return copies, [functools.partial(lambda cp: cp, cp) for cp in copies]

    def shapes():
        return [_sds(big[k].shape[:2] + (big[k].shape[2] // 2, big[k].shape[3]), big[k].dtype) for k in names]

    return _Task(lambda: [big[k] for k in names], shapes, {}, n, make, lambda res: got.update(zip(names, res)))


def _rs_pair_sum(name, full, got, core):
    nw, ns, R, C = full.shape
    half = R // 2

    def body(core_ref, a_ref, b_ref, o_ref):
        o_ref[...] = (a_ref[...].astype(F32) + b_ref[...].astype(F32)).astype(BF16)

    blk = lambda fn: pl.BlockSpec((None, None, half, C), fn)
    return _pcall(
        body, name=name, grid=(nw, ns), num_prefetch=1,
        in_specs=[blk(lambda w, s, core_ref: (w, s, core_ref[0], 0)), blk(lambda w, s, core_ref: (w, s, 0, 0))],
        out_specs=blk(lambda w, s, core_ref: (w, s, 0, 0)),
        out_shape=_sds((nw, ns, half, C), BF16),
    )(core, full, got)


def _chip_exchange_task(names, pair_sums, by_source, part=0, nparts=1):
    n = len(names)

    def rows(ref):
        h = ref.shape[1] // nparts
        return ref.at[:, pl.ds(part * h, h), :]

    def make(ins, outs, send_sem, recv_sem):
        x, y, c, chips = _mesh_position()
        s_me = _chip_index(x, y)
        starts, arrivals = [], []
        for a in range(n):
            for k, chip in enumerate(chips):
                s_k = _chip_index(*chip)
                starts.append(_remote(rows(ins[a].at[:, s_k]), rows(outs[a].at[:, s_me]), send_sem(3 * a + k),
                                      recv_sem(3 * a + k), (*chip, c)))
                got = rows(outs[a].at[:, s_k])
                arrivals.append(functools.partial(_remote, got, got, send_sem(3 * a + k), recv_sem(3 * a + k),
                                                  (*chip, c)))
        return starts, arrivals

    def operands():
        return [pair_sums[k] for k in names] + ([by_source[k] for k in names] if part else [])

    return _Task(operands, lambda: [_sds(pair_sums[k].shape, pair_sums[k].dtype) for k in names],
                 {n + a: a for a in range(n)} if part else {}, 3 * n, make,
                 lambda res: by_source.update(zip(names, res)))


def _rs_chip_sum(name, own, parts, chip):
    nw, ns, H, C = parts.shape

    def body(chip_ref, own_ref, *rest):
        prefs, o_ref = rest[:ns], rest[ns]
        me = chip_ref[0]
        own_v = own_ref[...].astype(F32)
        tot = None
        for s in range(ns):
            term = jnp.where(me == s, own_v, prefs[s][...].astype(F32))
            tot = term if tot is None else tot + term
        o_ref[...] = tot

    blk = lambda fn: pl.BlockSpec((None, None, H, C), fn)

    def part_spec(s):
        return blk(lambda w, chip_ref: (w, jnp.where(chip_ref[0] == s, (s + 1) % ns, s), 0, 0))

    return _pcall(
        body, name=name, grid=(nw,), num_prefetch=1,
        in_specs=[blk(lambda w, chip_ref: (w, chip_ref[0], 0, 0))] + [part_spec(s) for s in range(ns)],
        out_specs=pl.BlockSpec((None, H, C), lambda w, chip_ref: (w, 0, 0)),
        out_shape=_sds((nw, H, C), F32),
    )(chip, own, *([parts] * ns))


def _pair_gather_task(names, halves, sibling_halves):
    n = len(names)

    def make(ins, outs, send_sem, recv_sem):
        x, y, c, _ = _mesh_position()
        copies = [_remote(ins[a], outs[a], send_sem(a), recv_sem(a), (x, y, 1 - c)) for a in range(n)]
        return copies, [functools.partial(lambda cp: cp, cp) for cp in copies]

    return _Task(lambda: [halves[k] for k in names], lambda: [_sds(halves[k].shape, F32) for k in names],
                 {}, n, make, lambda res: sibling_halves.update(zip(names, res)))


def _small_allreduce(arrs):
    n = len(arrs)
    per = 1 + 2 * N_PEER_CHIPS

    def body(*refs):
        v_refs, o_refs = refs[:n], refs[n:2 * n]
        sib, pair, part = refs[2 * n:3 * n], refs[3 * n:4 * n], refs[4 * n:5 * n]
        send_sems, recv_sems = refs[5 * n:]
        x, y, c, chips = _mesh_position()
        s_me = _chip_index(x, y)

        def quarter(ref, s):
            q = ref.shape[0] // N_CHIPS
            return ref.at[pl.ds(pl.multiple_of(s * q, 8), q)]

        def exchange(first_sem, src, dst_of, arrival_of):
            sems = lambda a, k: (send_sems.at[a * per + first_sem + k], recv_sems.at[a * per + first_sem + k])
            sends = [_remote(src(a, _chip_index(*chip)), dst_of(a, s_me), *sems(a, k), (*chip, c))
                     for a in range(n) for k, chip in enumerate(chips)]
            for cp in sends:
                cp.start()
            for a in range(n):
                for k, chip in enumerate(chips):
                    got = arrival_of(a, _chip_index(*chip))
                    _remote(got, got, *sems(a, k), (*chip, c)).wait_recv()
            for cp in sends:
                cp.wait_send()

        swaps = [_remote(v_refs[a], sib[a], send_sems.at[a * per], recv_sems.at[a * per], (x, y, 1 - c))
                 for a in range(n)]
        for cp in swaps:
            cp.start()
        for cp in swaps:
            cp.wait()
        for a in range(n):
            pair[a][...] = v_refs[a][...] + sib[a][...]
        exchange(1, lambda a, s_k: quarter(pair[a], s_k), lambda a, s: part[a].at[s], lambda a, s_k: part[a].at[s_k])
        for a in range(n):
            part[a][s_me] = quarter(pair[a], s_me)[...]
            q = o_refs[a].shape[0] // N_CHIPS
            o_refs[a][pl.ds(pl.multiple_of(s_me * q, 8), q), :] = (
                ((part[a][0] + part[a][1]) + part[a][2]) + part[a][3])
        exchange(1 + N_PEER_CHIPS, lambda a, s_k: quarter(o_refs[a], s_me), lambda a, s: quarter(o_refs[a], s),
                 lambda a, s_k: quarter(o_refs[a], s_k))

    shapes = [a.shape for a in arrs]
    return _pcall(
        body, name="small_allreduce", grid=(1,),
        in_specs=[VMEM_SPEC] * n, out_specs=[VMEM_SPEC] * n, out_shape=[_sds(s, F32) for s in shapes],
        scratch_shapes=([pltpu.VMEM(s, F32) for s in shapes] * 2
                        + [pltpu.VMEM((N_CHIPS, s[0] // N_CHIPS, s[1]), F32) for s in shapes]
                        + [pltpu.SemaphoreType.DMA((n * per,)), pltpu.SemaphoreType.DMA((n * per,))]),
    )(*arrs)


TRANSPOSED_WEIGHTS = ("ffn1_w1", "ffn1_w3", "ffn2_w1", "ffn2_w3")
SMALL_LAYOUT = [("ffn1_norm", 1), ("mix_norm", 1), ("ret_gn", 1), ("conv_b", 1), ("b_rgate", 1), ("b_igate", 1),
                ("lru_lambda", 1), ("xattn_norm", 1), ("mem_norm", 1), ("ffn2_norm", 1), ("final_norm", 1),
                ("b_branch_gate", 2), ("conv_w", CONV_TAPS)]
SMALL_ROWS = 32
GATE_WEIGHTS = ("w_rgate", "w_igate")
WEIGHT_ORDER = ["ffn1_norm", "ffn1_w1", "ffn1_w3", "ffn1_w2", "mix_norm", "w_in", "ret_gn", "w_ret_o", "conv_w",
                "conv_b", "w_rgate", "b_rgate", "w_igate", "b_igate", "lru_lambda", "w_lru_o", "w_branch_gate",
                "b_branch_gate", "w_out", "xattn_norm", "mem_norm", "w_xq", "w_xk", "w_xv", "w_xo", "ffn2_norm",
                "ffn2_w1", "ffn2_w3", "ffn2_w2", "final_norm"]


def _pack_small(parts):
    rows = [parts[name].reshape(n, D) for name, n in SMALL_LAYOUT]
    used = sum(n for _, n in SMALL_LAYOUT)
    rows.append(jnp.zeros((SMALL_ROWS - used, D), F32))
    return jnp.concatenate(rows, axis=0)


def _unpack_small(packed, shapes):
    out, r = {}, 0
    for name, n in SMALL_LAYOUT:
        out[name] = packed[r:r + n].reshape(shapes[name])
        r += n
    return out


def kernel(x, mem, ffn1_norm, ffn1_w1, ffn1_w3, ffn1_w2, mix_norm, w_in, ret_gn, w_ret_o, conv_w, conv_b, w_rgate, b_rgate, w_igate, b_igate, lru_lambda, w_lru_o, w_branch_gate, b_branch_gate, w_out, xattn_norm, mem_norm, w_xq, w_xk, w_xv, w_xo, ffn2_norm, ffn2_w1, ffn2_w3, ffn2_w2, final_norm, loss_target, m_ffn1_norm, m_ffn1_w1, m_ffn1_w3, m_ffn1_w2, m_mix_norm, m_w_in, m_ret_gn, m_w_ret_o, m_conv_w, m_conv_b, m_w_rgate, m_b_rgate, m_w_igate, m_b_igate, m_lru_lambda, m_w_lru_o, m_w_branch_gate, m_b_branch_gate, m_w_out, m_xattn_norm, m_mem_norm, m_w_xq, m_w_xk, m_w_xv, m_w_xo, m_ffn2_norm, m_ffn2_w1, m_ffn2_w3, m_ffn2_w2, m_final_norm, v_ffn1_norm, v_ffn1_w1, v_ffn1_w3, v_ffn1_w2, v_mix_norm, v_w_in, v_ret_gn, v_w_ret_o, v_conv_w, v_conv_b, v_w_rgate, v_b_rgate, v_w_igate, v_b_igate, v_lru_lambda, v_w_lru_o, v_w_branch_gate, v_b_branch_gate, v_w_out, v_xattn_norm, v_mem_norm, v_w_xq, v_w_xk, v_w_xv, v_w_xo, v_ffn2_norm, v_ffn2_w1, v_ffn2_w3, v_ffn2_w2, v_final_norm):
    given = dict(locals())
    w = {n: given[n] for n in WEIGHT_ORDER}
    mom = {n: given["m_" + n] for n in WEIGHT_ORDER}
    var = {n: given["v_" + n] for n in WEIGHT_ORDER}
    chip = _chip_index(lax.axis_index("x"), lax.axis_index("y"))
    core = lax.axis_index("c").astype(jnp.int32).reshape(1)

    chip_id = chip.astype(jnp.int32).reshape(1)
    sm = {n: w[n] for n in ["ffn1_norm", "mix_norm", "ret_gn", "conv_b", "b_rgate", "b_igate", "lru_lambda",
                            "xattn_norm", "mem_norm", "ffn2_norm", "b_branch_gate"]}
    sm["final_norm"] = w["final_norm"].reshape(1, D)
    sm["w_rgate"] = w["w_rgate"][0]
    sm["w_igate"] = w["w_igate"][0]

    local = lambda a, n: jnp.swapaxes(a[0], 0, 1) if n in TRANSPOSED_WEIGHTS else a[0]
    stack = lambda names: jnp.stack([local(w[n], n) for n in names], axis=0).astype(BF16)
    shard = {"col1": stack(["ffn1_w1", "ffn1_w3"]), "row2a": stack(["ffn1_w2"]),
             "win": jnp.swapaxes(w["w_in"], 1, 2).astype(BF16), "wbg": jnp.swapaxes(w["w_branch_gate"], 1, 2).astype(BF16), "sqA": stack(["w_ret_o", "w_lru_o", "w_out"]),
             "sqB": stack(["w_xq", "w_xk"]), "sqC": stack(["w_xv", "w_xo"]), "col2a": stack(["ffn2_w1"]), "col2b": stack(["ffn2_w3"]),
             "row2b": stack(["ffn2_w2"]), "conv": w["conv_w"]}
    gw, landed = {}, {}
    over_chips = lambda keys: _gather_chips_task({k: shard[k] for k in keys}, True, landed)
    to_sibling = lambda keys: _gather_sibling_task(keys, landed, gw)

    big, got, pair_sums, by_source, halves, sibling_halves, outs = {}, {}, {}, {}, {}, {}, {}
    pair_swap = lambda names: _pair_swap_task(names, big, got)
    exchange = lambda names, part=0, nparts=1: _chip_exchange_task(names, pair_sums, by_source, part, nparts)
    pair_gather = lambda names: _pair_gather_task(names, halves, sibling_halves)

    def pair_sum(names):
        for n in names:
            pair_sums[n] = _rs_pair_sum("rs_pair_sum_" + n, big[n], got[n], core)

    def chip_sum(names):
        for n in names:
            halves[n] = _rs_chip_sum("rs_chip_sum_" + n, pair_sums[n], by_source[n], chip_id)

    def adamw(names):
        for n in names:
            res = _adamw_halves("adamw_" + n, local(w[n], n), halves[n], sibling_halves[n], 0, local(mom[n], n),
                                local(var[n], n), core)
            outs[n] = tuple((jnp.swapaxes(r, 0, 1) if n in TRANSPOSED_WEIGHTS else r)[None] for r in res)

    do = lambda fn, names: functools.partial(fn, names)
    ffn2_grads = ["ffn2_w2", "ffn2_w1", "ffn2_w3"]
    xattn_grads = ["w_xo", "w_xq", "w_xk", "w_xv"]
    mix_out_grads = ["w_branch_gate", "w_out", "w_ret_o", "w_lru_o"]
    conv_gather = _gather_chips_task({"conv": shard["conv"]}, False, gw)
    half = lambda key, part: _gather_chips_task({key: shard[key]}, True, landed, part, 2)
    plan = _Plan()
    plan.tasks = {
        "ag_first_chips": [over_chips(["col1", "row2a"])],
        "ag_first_sibling": [to_sibling(["col1", "row2a"])],
        "ffn1_up": [over_chips(["win"])],
        "ffn1_down": [to_sibling(["win"]), over_chips(["wbg"]), conv_gather],
        "mix_in": [to_sibling(["wbg"]), over_chips(["sqA"])],
        "ret_fwd": [to_sibling(["sqA"]), over_chips(["col2a"])],
        "lru_gates_fwd": [to_sibling(["col2a"]), over_chips(["sqB"])],
        "lru_scan_fwd": [to_sibling(["sqB"]), over_chips(["sqC"])],
        "lru_out": [to_sibling(["sqC"])],
        "mix_gates": [half("col2b", 0)],
        "y_lru": [half("col2b", 1)],
        "mix_out": [to_sibling(["col2b"]), half("row2b", 0)],
        "xattn_fwd": [half("row2b", 1)],
        "xattn_out": [to_sibling(["row2b"])],
        "ffn2_dh": [pair_swap(ffn2_grads)],
        "xattn_bwd": [exchange(["ffn2_w2"], 0, 2)],
        "d_hq": [exchange(["ffn2_w2"], 1, 2)],
        "d_merged": [exchange(["ffn2_w1"], 0, 2), pair_swap(xattn_grads)],
        "dw_bg": [exchange(["w_xo"])],
        "ret_bwd": [exchange(["ffn2_w1"], 1, 2), exchange(["ffn2_w3"], 0, 2), pair_swap(mix_out_grads)],
        "lru_scan_bwd": [exchange(["ffn2_w3"], 1, 2)],
        "lru_gates_bwd": [exchange(["w_xq", "w_xk"]), pair_gather(ffn2_grads)],
        "dw_in": [exchange(["w_xv", "w_out"])],
        "d_h2": [exchange(["w_branch_gate", "w_ret_o", "w_lru_o"]), pair_swap(["w_in"]), pair_gather(xattn_grads)],
        "ffn1_bwd_mid": [exchange(["w_in"], 0, 2), pair_gather(mix_out_grads)],
        "ffn1_dw2": [exchange(["w_in"], 2, 4)],
        "ffn1_dw1": [exchange(["w_in"], 3, 4), pair_swap(["ffn1_w2"])],
        "ffn1_dw3": [exchange(["ffn1_w2"], 0, 2), pair_swap(["ffn1_w1"]), pair_gather(["w_in"])],
        "ffn1_dh": [exchange(["ffn1_w2"], 1, 2), exchange(["ffn1_w1"]), pair_swap(["ffn1_w3"])],
        "small_allreduce": [exchange(["ffn1_w3"]), pair_gather(["ffn1_w2"])],
        "rs_last_gather": [pair_gather(["ffn1_w1", "ffn1_w3"])],
    }
    plan.after = {
        "ffn2_dh": [do(pair_sum, ffn2_grads)],
        "d_merged": [do(pair_sum, xattn_grads)],
        "ret_bwd": [do(pair_sum, mix_out_grads)],
        "lru_scan_bwd": [do(chip_sum, ffn2_grads)],
        "lru_gates_bwd": [do(adamw, ffn2_grads)],
        "dw_in": [do(chip_sum, xattn_grads)],
        "d_h2": [do(chip_sum, mix_out_grads), do(pair_sum, ["w_in"]), do(adamw, xattn_grads)],
        "ffn1_bwd_mid": [do(adamw, mix_out_grads)],
        "ffn1_dw1": [do(chip_sum, ["w_in"]), do(pair_sum, ["ffn1_w2"])],
        "ffn1_dw3": [do(pair_sum, ["ffn1_w1"]), do(adamw, ["w_in"])],
        "ffn1_dh": [do(pair_sum, ["ffn1_w3"]), do(chip_sum, ["ffn1_w2"])],
        "small_allreduce": [do(chip_sum, ["ffn1_w1", "ffn1_w3"]), functools.partial(_comm_call, "rs_last_gather"),
                    do(adamw, ["ffn1_w2", "ffn1_w1", "ffn1_w3"])],
    }
    global _plan
    _plan = plan
    try:
        _comm_call("ag_first_chips")
        _comm_call("ag_first_sibling")
        loss_part, grad_x, small = _local_step(x[0], mem[0], loss_target[0], gw, sm, big)
        gate2d = lambda a: a.reshape(LRU_BLOCKS * LRU_BLOCK, LRU_BLOCK)
        small_sum, *gate_sums = _small_allreduce([_pack_small(small)] + [gate2d(small[n]) for n in GATE_WEIGHTS])
    finally:
        _plan = None
    assert not plan.tasks and not plan.after, (list(plan.tasks), list(plan.after))
    loss = lax.psum(loss_part[0, 0], ("x", "y", "c"))

    small_shapes = {n: w[n].shape for n, _ in SMALL_LAYOUT}
    small_shapes["conv_w"] = (CONV_TAPS, D)
    conv_grad = lax.dynamic_slice(small_sum[13:13 + CONV_TAPS], (0, chip * SQ_BLK), (CONV_TAPS, SQ_BLK))
    small_w = {n: w[n] for n, _ in SMALL_LAYOUT}
    small_m = {n: mom[n] for n, _ in SMALL_LAYOUT}
    small_v = {n: var[n] for n, _ in SMALL_LAYOUT}
    pad_cols = lambda a: jnp.pad(a[0], ((0, 0), (0, D - SQ_BLK)))
    for dct in (small_w, small_m, small_v):
        dct["conv_w"] = pad_cols(dct["conv_w"])
    g_pack = lax.dynamic_update_slice(small_sum, jnp.pad(conv_grad, ((0, 0), (0, D - SQ_BLK))), (13, 0))
    d_pack, m_pack, v_pack = _adamw("adamw_small", _pack_small(small_w), g_pack, _pack_small(small_m),
                                    _pack_small(small_v))
    unpacked = [_unpack_small(p, small_shapes) for p in (g_pack, d_pack, m_pack, v_pack)]
    for n, _ in SMALL_LAYOUT:
        if n == "conv_w":
            outs[n] = tuple(u[n][:, :SQ_BLK][None] for u in unpacked)
        else:
            outs[n] = tuple(u[n] for u in unpacked)
    for n, gsum in zip(GATE_WEIGHTS, gate_sums):
        d, nm, nv = _adamw("adamw_" + n, gate2d(w[n]), gsum, gate2d(mom[n]), gate2d(var[n]))
        outs[n] = tuple(r.reshape(w[n].shape) for r in (gsum, d, nm, nv))

    result = [loss, grad_x[None]]
    for k in range(4):
        result += [outs[n][k] for n in WEIGHT_ORDER]
    return tuple(result)
```

```python
import functools
import math

import numpy as np
import jax
import jax.numpy as jnp
from jax import lax
from jax.experimental import pallas as pl
from jax.experimental.pallas import tpu as pltpu

F32 = jnp.float32
BF16 = jnp.bfloat16
GRAD_WIRE_DTYPE = BF16
MESH = pl.DeviceIdType.MESH

D = 1024
EPS = 1e-6
RET_HEADS = 4
RET_DK = 128
RET_DV = 256
CHUNK = 128
ROPE_BASE = 10000.0
LRU_BLOCKS = 8
LRU_BLOCK = 128
CONV_TAPS = 4
LRU_C = 8.0
D_FF = 2816
X_HEADS = 4
X_HD = 256
N_CHIPS = 4
FF_BLK = D_FF // N_CHIPS
IN_BLK = 5120 // N_CHIPS
BG_BLK = 2048 // N_CHIPS
SQ_BLK = D // N_CHIPS

ADAM_LR = 0.001
ADAM_B1 = 0.9
ADAM_B2 = 0.999
ADAM_EPS = 1e-08
ADAM_WD = 0.01
ADAM_STEP = 10

VMEM_LIMIT_BYTES = 56 * 1024 * 1024
ROW_TILE = 512
WIDE_ROW_TILE = 1024
FFN_ROW_TILE = 256
DW_BLK = D_FF // 2
SCAN_TILE = 256

_DN = {
    "nn": (((1,), (0,)), ((), ())),
    "nt": (((1,), (1,)), ((), ())),
    "tn": (((0,), (0,)), ((), ())),
}


def _cparams(n_axes):
    return pltpu.CompilerParams(dimension_semantics=("arbitrary",) * n_axes,
                                vmem_limit_bytes=VMEM_LIMIT_BYTES)


def _dot(a, b, kind):
    if b.ndim == 3:
        b = b.reshape(b.shape[0] * b.shape[1], b.shape[2])
    return lax.dot_general(a.astype(BF16), b.astype(BF16), _DN[kind], preferred_element_type=F32)


def _sigmoid(x):
    return 1.0 / (1.0 + jnp.exp(-x))


def _log1p_pos(e):
    u = 1.0 + e
    return jnp.where(u == 1.0, e, jnp.log(u) * (e / jnp.where(u == 1.0, 1.0, u - 1.0)))


def _expm1(x):
    u = jnp.exp(x)
    lu = jnp.log(u)
    safe = jnp.where(lu == 0.0, 1.0, lu)
    return jnp.where(u == 1.0, x, (u - 1.0) * (x / safe))


def _softplus(z):
    return jnp.maximum(z, 0.0) + _log1p_pos(jnp.exp(-jnp.abs(z)))


_GELU_C = math.sqrt(2.0 / math.pi)


def _gelu_and_grad(x):
    x2 = x * x
    t = jnp.tanh(_GELU_C * (x + 0.044715 * x * x2))
    g = 0.5 * x * (1.0 + t)
    dg = 0.5 * (1.0 + t) + 0.5 * x * (1.0 - t * t) * (_GELU_C * (1.0 + 3.0 * 0.044715 * x2))
    return g, dg


def _rms_fwd(x, g):
    r = lax.rsqrt(jnp.mean(x * x, axis=-1, keepdims=True) + EPS)
    return (x * r) * g


def _rms_bwd(x, g, dh):
    r = lax.rsqrt(jnp.mean(x * x, axis=-1, keepdims=True) + EPS)
    n = x * r
    dyg = dh * g
    dx = r * (dyg - n * jnp.mean(dyg * n, axis=-1, keepdims=True))
    return dx, jnp.sum(dh * n, axis=0, keepdims=True)


def _accumulate(ref, val, first):
    @pl.when(first)
    def _():
        ref[...] = val

    @pl.when(jnp.logical_not(first))
    def _():
        ref[...] += val


def _sds(shape, dtype):
    return jax.ShapeDtypeStruct(tuple(shape), dtype)


def _spec(shape, fn):
    return pl.BlockSpec(tuple(shape), fn)


class _Task:
    def __init__(self, operands, out_shapes, aliases, nsem, make, finish):
        self.operands, self.out_shapes, self.aliases = operands, out_shapes, aliases
        self.nsem, self.make, self.finish = nsem, make, finish


class _Plan:
    def __init__(self):
        self.tasks, self.after = {}, {}


_plan = None


def _pcall(body, *, name, grid, in_specs, out_specs, out_shape, scratch_shapes=(), num_prefetch=0):
    single = not isinstance(out_shape, (list, tuple))
    out_shape = [out_shape] if single else list(out_shape)
    out_specs = [out_specs] if single else list(out_specs)
    in_specs = list(in_specs)
    scratch_shapes = list(scratch_shapes)
    tasks = _plan.tasks.pop(name, []) if _plan is not None else []
    after = _plan.after.pop(name, []) if _plan is not None else []
    nax = len(grid)

    def run(*operands):
        n_in = len(operands) - num_prefetch
        n_out = len(out_shape)
        t_ops = [t.operands() for t in tasks]
        t_outs = [t.out_shapes() for t in tasks]
        c_ops = [a for ops in t_ops for a in ops]
        c_outs = [s for outs in t_outs for s in outs]
        aliases = {}
        i0, o0 = num_prefetch + n_in, n_out
        for t, ops, outs in zip(tasks, t_ops, t_outs):
            for i_loc, o_loc in t.aliases.items():
                aliases[i0 + i_loc] = o0 + o_loc
            i0 += len(ops)
            o0 += len(outs)
        nsem = sum(t.nsem for t in tasks)

        def wrapped(*refs):
            p = num_prefetch
            pre, ins = refs[:p], refs[p:p + n_in]
            cins = refs[p + n_in:p + n_in + len(c_ops)]
            q = p + n_in + len(c_ops)
            outs, couts = refs[q:q + n_out], refs[q + n_out:q + n_out + len(c_outs)]
            q += n_out + len(c_outs)
            scr = refs[q:q + len(scratch_shapes)]

            def descriptors():
                send_sems, recv_sems = refs[q + len(scratch_shapes):]
                starts, arrivals = [], []
                ci = co = so = 0
                for t, ops, souts in zip(tasks, t_ops, t_outs):
                    s, a = t.make(cins[ci:ci + len(ops)], couts[co:co + len(souts)],
                                  functools.partial(lambda base, k: send_sems.at[base + k], so),
                                  functools.partial(lambda base, k: recv_sems.at[base + k], so))
                    starts += s
                    arrivals += a
                    ci, co, so = ci + len(ops), co + len(souts), so + t.nsem
                return starts, arrivals

            if tasks:
                ids = [pl.program_id(k) for k in range(nax)]
                first = functools.reduce(jnp.logical_and, [i == 0 for i in ids])
                last = functools.reduce(jnp.logical_and, [i == g - 1 for i, g in zip(ids, grid)])

                @pl.when(first)
                def _():
                    for cp in descriptors()[0]:
                        cp.start()

            body(*pre, *ins, *outs, *scr)

            if tasks:
                @pl.when(last)
                def _():
                    starts, arrivals = descriptors()
                    for arrival in arrivals:
                        arrival().wait_recv()
                    for cp in starts:
                        cp.wait_send()

        sems = [pltpu.SemaphoreType.DMA((nsem,)), pltpu.SemaphoreType.DMA((nsem,))] if tasks else []
        res = pl.pallas_call(
            wrapped, name=name,
            grid_spec=pltpu.PrefetchScalarGridSpec(
                num_scalar_prefetch=num_prefetch, grid=tuple(grid),
                in_specs=in_specs + [ANY_SPEC] * len(c_ops),
                out_specs=out_specs + [ANY_SPEC] * len(c_outs),
                scratch_shapes=scratch_shapes + sems),
            out_shape=out_shape + c_outs,
            input_output_aliases=aliases,
            compiler_params=_cparams(nax),
        )(*operands, *c_ops)
        co = n_out
        for t, souts in zip(tasks, t_outs):
            t.finish(res[co:co + len(souts)])
            co += len(souts)
        for fn in after:
            fn()
        return res[0] if single else list(res[:n_out])

    return run


def _comm_call(name):
    def body(o_ref):
        o_ref[...] = jnp.zeros_like(o_ref)

    _pcall(body, name=name, grid=(1,), in_specs=[], out_specs=_spec((8, 128), lambda i: (0, 0)),
           out_shape=_sds((8, 128), F32))()


def _gemm(name, terms, grid, outs, acc_shape, extras=(), epilogue=None):
    kinds = [t[4] for t in terms]
    nt, ne, no = len(terms), len(extras), len(outs)
    nred = grid[-1]
    nax = len(grid)

    def body(*refs):
        trefs = refs[:2 * nt]
        erefs = refs[2 * nt:2 * nt + ne]
        orefs = refs[2 * nt + ne:2 * nt + ne + no]
        ids = [pl.program_id(k) for k in range(nax)]
        tot = None
        for t in range(nt):
            d = _dot(trefs[2 * t][...], trefs[2 * t + 1][...], kinds[t])
            tot = d if tot is None else tot + d

        def finish(acc):
            if epilogue is None:
                orefs[0][...] = acc.astype(orefs[0].dtype)
            else:
                epilogue(acc, erefs, orefs, ids)

        if nred == 1:
            finish(tot)
        else:
            acc_ref = refs[-1]
            r = ids[-1]

            @pl.when(r == 0)
            def _():
                acc_ref[...] = tot

            @pl.when(r > 0)
            def _():
                acc_ref[...] += tot

            @pl.when(r == nred - 1)
            def _():
                finish(acc_ref[...])

    operands, in_specs = [], []
    for a, a_spec, b, b_spec, _ in terms:
        operands += [a, b]
        in_specs += [a_spec, b_spec]
    for e, e_spec in extras:
        operands.append(e)
        in_specs.append(e_spec)
    scratch = [pltpu.VMEM(tuple(acc_shape), F32)] if nred > 1 else []
    return _pcall(body, name=name, grid=tuple(grid), in_specs=in_specs, out_specs=[o[1] for o in outs],
                  out_shape=[o[0] for o in outs], scratch_shapes=scratch)(*operands)


def _rowwise(name, fn, ins, outs, grid):
    ni = len(ins)
    nax = len(grid)

    def body(*refs):
        ids = [pl.program_id(k) for k in range(nax)]
        fn(refs[:ni], refs[ni:], ids)

    return _pcall(body, name=name, grid=tuple(grid), in_specs=[i[1] for i in ins],
                  out_specs=[o[1] for o in outs], out_shape=[o[0] for o in outs])(*[i[0] for i in ins])


def _ffn_up(name, h, w1buf, w1_idx, w3buf, w3_idx):
    T = h.shape[0]
    tm = min(FFN_ROW_TILE, T)

    def body(h_ref, w1_ref, w3_ref, a_ref, b_ref, s_ref):
        hv = h_ref[...]
        a = _dot(hv, w1_ref[...], "nt")
        b = _dot(hv, w3_ref[...], "nt")
        a_ref[...] = a.astype(BF16)
        b_ref[...] = b.astype(BF16)
        s_ref[...] = ((a * _sigmoid(a)) * b).astype(BF16)

    blk = _spec((tm, D_FF), lambda i: (i, 0))
    return _pcall(
        body, name=name, grid=(T // tm,),
        in_specs=[_spec((tm, D), lambda i: (i, 0)),
                  _spec((N_CHIPS, None, FF_BLK, D), lambda i: (0, w1_idx, 0, 0)),
                  _spec((N_CHIPS, None, FF_BLK, D), lambda i: (0, w3_idx, 0, 0))],
        out_specs=[blk, blk, blk],
        out_shape=[_sds((T, D_FF), BF16)] * 3,
    )(h, w1buf, w3buf)


def _ffn_down(name, s, wrow2, w2_idx, x_res, g_next=None):
    T = x_res.shape[0]
    tm = min(ROW_TILE, T)
    row = lambda i, j, r: (i, 0)

    def epilogue(acc, erefs, orefs, ids):
        xo = erefs[0][...] + 0.5 * acc
        orefs[0][...] = xo
        if g_next is not None:
            orefs[1][...] = _rms_fwd(xo, erefs[1][...]).astype(BF16)

    extras = [(x_res, _spec((tm, D), row))]
    outs = [(_sds((T, D), F32), _spec((tm, D), row))]
    if g_next is not None:
        extras.append((g_next, _spec((1, D), lambda i, j, r: (0, 0))))
        outs.append((_sds((T, D), BF16), _spec((tm, D), row)))
    return _gemm(
        name,
        [(s, _spec((tm, D_FF), row),
          wrow2, _spec((N_CHIPS, None, FF_BLK, D), lambda i, j, r: (0, w2_idx, 0, 0)), "nn")],
        (T // tm, 1, 1), outs, (tm, D), extras, epilogue)


def _ffn_bwd_mid(name, dx, wrow2, w2_idx, a, b):
    T = dx.shape[0]
    tm = min(FFN_ROW_TILE, T)

    def body(dx_ref, w2_ref, a_ref, b_ref, dab_ref):
        ds = _dot(0.5 * dx_ref[...], w2_ref[...], "nt")
        av = a_ref[...].astype(F32)
        sg = _sigmoid(av)
        dab_ref[0] = (ds * b_ref[...].astype(F32) * (sg * (1.0 + av * (1.0 - sg)))).astype(BF16)
        dab_ref[1] = (ds * (av * sg)).astype(BF16)

    blk = _spec((tm, D_FF), lambda i: (i, 0))
    return _pcall(
        body, name=name, grid=(T // tm,),
        in_specs=[_spec((tm, D), lambda i: (i, 0)),
                  _spec((N_CHIPS, None, FF_BLK, D), lambda i: (0, w2_idx, 0, 0)),
                  blk, blk],
        out_specs=_spec((2, tm, D_FF), lambda i: (0, i, 0)),
        out_shape=_sds((2, T, D_FF), BF16),
    )(dx, wrow2, a, b)


def _rms_bwd_epilogue(acc, erefs, orefs, ids):
    dx, dgp = _rms_bwd(erefs[0][...], erefs[1][...], acc)
    orefs[0][...] = dx + erefs[2][...]
    _accumulate(orefs[1], dgp, ids[0] == 0)


def _rms_bwd_io(x, g, dres, T, tm):
    row = lambda i, j, r: (i, 0)
    vec = lambda i, j, r: (0, 0)
    extras = [(x, _spec((tm, D), row)), (g, _spec((1, D), vec)), (dres, _spec((tm, D), row))]
    outs = [(_sds((T, D), F32), _spec((tm, D), row)), (_sds((1, D), F32), _spec((1, D), vec))]
    return extras, outs


def _ffn_bwd(tag, dx_out, h, a, b, s, w1buf, w1_idx, w3buf, w3_idx, wrow2, w2_idx, x_in, g, big):
    T = dx_out.shape[0]
    dab = _ffn_bwd_mid(tag + "_bwd_mid", dx_out, wrow2, w2_idx, a, b)

    def half_scale(acc, erefs, orefs, ids):
        orefs[0][...] = (0.5 * acc).astype(orefs[0].dtype)

    dw_grid = (D_FF // DW_BLK, 1, 1)
    dw_out = [(_sds((D_FF, D), GRAD_WIRE_DTYPE), _spec((DW_BLK, D), lambda j, n, r: (j, 0)))]
    tokens = _spec((T, D), lambda j, n, r: (0, 0))
    big[tag + "_w2"] = _gemm(
        tag + "_dw2", [(s, _spec((T, DW_BLK), lambda j, n, r: (0, j)), dx_out, tokens, "tn")],
        dw_grid, dw_out, (DW_BLK, D), (), half_scale)[0].reshape(1, N_CHIPS, FF_BLK, D)
    for widx, wname in ((0, "_w1"), (1, "_w3")):
        big[tag + wname] = _gemm(
            tag + "_d" + wname[1:],
            [(dab, _spec((None, T, DW_BLK), functools.partial(lambda w, j, n, r: (w, 0, j), widx)), h, tokens, "tn")],
            dw_grid, dw_out, (DW_BLK, D))[0].reshape(1, N_CHIPS, FF_BLK, D)
    tm = min(FFN_ROW_TILE, T)
    extras, outs = _rms_bwd_io(x_in, g, dx_out, T, tm)
    whole = lambda idx: _spec((N_CHIPS, None, FF_BLK, D), lambda i, j, r: (0, idx, 0, 0))
    dx_in, dg = _gemm(
        tag + "_dh",
        [(dab, _spec((None, tm, D_FF), lambda i, j, r: (0, i, 0)), w1buf, whole(w1_idx), "nn"),
         (dab, _spec((None, tm, D_FF), lambda i, j, r: (1, i, 0)), w3buf, whole(w3_idx), "nn")],
        (T // tm, 1, 1), outs, (tm, D), extras, _rms_bwd_epilogue)
    return dx_in, dg


def _proj_sq(name, a, wsq, idx, kind, out_dtype=F32, extras=(), epilogue=None, outs=None):
    M = a.shape[0]
    tm = min(ROW_TILE, M)
    if outs is None:
        outs = [(_sds((M, D), out_dtype), _spec((tm, D), lambda i, j, r: (i, 0)))]
    return _gemm(
        name,
        [(a, _spec((tm, D), lambda i, j, r: (i, 0)),
          wsq, _spec((N_CHIPS, None, SQ_BLK, D), lambda i, j, r: (0, idx, 0, 0)), kind)],
        (M // tm, 1, 1), outs, (tm, D), extras, epilogue)


def _dw_sq(name, a, b):
    M = a.shape[0]
    tk = M
    whole = _gemm(
        name,
        [(a, _spec((tk, D), lambda i, j, r: (r, 0)), b, _spec((tk, D), lambda i, j, r: (r, 0)), "tn")],
        (1, 1, M // tk),
        [(_sds((D, D), GRAD_WIRE_DTYPE), _spec((D, D), lambda i, j, r: (0, 0)))],
        (D, D))[0]
    return whole.reshape(N_CHIPS, SQ_BLK, D)


def _retention_constants(T):
    pos = jnp.arange(T, dtype=F32)
    inv_freq = ROPE_BASE ** (-jnp.arange(0, RET_DK, 2, dtype=F32) / RET_DK)
    ang = pos[:, None] * inv_freq[None, :]
    cosf = jnp.concatenate([jnp.cos(ang), jnp.cos(ang)], axis=1)
    sins = jnp.concatenate([-jnp.sin(ang), jnp.sin(ang)], axis=1)
    lg = jnp.log(1.0 - 2.0 ** (-5.0 - jnp.arange(RET_HEADS, dtype=F32)))
    p = jnp.arange(CHUNK, dtype=F32)
    rel = p[:, None] - p[None, :]
    dmat = jnp.where(rel[None] >= 0, jnp.exp(rel[None] * lg[:, None, None]), 0.0)
    kd = jnp.exp((CHUNK - 1.0 - p)[None, :] * lg[:, None])[:, :, None]
    qd = jnp.exp((p + 1.0)[None, :] * lg[:, None])[:, :, None]
    cd = jnp.exp(CHUNK * lg)[:, None, None]
    return cosf, sins, dmat, kd, qd, cd


def _rot(t, cosv, sinv):
    return t * cosv + pltpu.roll(t, RET_DK // 2, 1) * sinv


def _unrot(t, cosv, sinv):
    return t * cosv - pltpu.roll(t, RET_DK // 2, 1) * sinv


def _ret_const_specs(cm):
    whole = lambda shape: _spec(shape, lambda c: (0,) * len(shape))
    return [
        _spec((CHUNK, RET_DK), lambda c: (cm(c), 0)),
        _spec((CHUNK, RET_DK), lambda c: (cm(c), 0)),
        whole((RET_HEADS, CHUNK, CHUNK)), whole((RET_HEADS, CHUNK, 1)), whole((RET_HEADS, CHUNK, 1)),
        whole((RET_HEADS, 1, 1)),
    ]


def _head(h, width):
    return slice(h * width, (h + 1) * width)


def _ret_fwd(u, consts, ret_gn):
    T = u.shape[0]
    nC = T // CHUNK
    kscale = RET_DK ** -0.5

    def body(q_ref, k_ref, v_ref, g_ref, cos_ref, sin_ref, dm_ref, kd_ref, qd_ref, cd_ref, gn_ref,
             qr_ref, kr_ref, ret_ref, yr_ref, st_ref, state):
        @pl.when(pl.program_id(0) == 0)
        def _():
            state[...] = jnp.zeros_like(state)

        cosv, sinv = cos_ref[...], sin_ref[...]
        for h in range(RET_HEADS):
            hk, hv = _head(h, RET_DK), _head(h, RET_DV)
            q = _rot(q_ref[:, hk], cosv, sinv)
            k = _rot(k_ref[:, hk], cosv, sinv) * kscale
            v = v_ref[:, hv]
            qr_ref[:, hk] = q
            kr_ref[:, hk] = k
            prev = state[h]
            st_ref[h] = prev
            s = _dot(q, k, "nt") * dm_ref[h]
            ret = _dot(s, v, "nn") + _dot(q, prev, "nn") * qd_ref[h]
            state[h] = cd_ref[h] * prev + _dot(k * kd_ref[h], v, "tn")
            ret_ref[:, hv] = ret
            mu = jnp.mean(ret, axis=-1, keepdims=True)
            xc = ret - mu
            yn = xc * lax.rsqrt(jnp.mean(xc * xc, axis=-1, keepdims=True) + EPS)
            g = g_ref[:, hv]
            yr_ref[:, hv] = ((g * _sigmoid(g)) * (yn * gn_ref[:, hv])).astype(BF16)

    cm = lambda c: c
    qk_w, v_w = RET_HEADS * RET_DK, RET_HEADS * RET_DV
    in_specs = [
        _spec((CHUNK, qk_w), lambda c: (c, 0)), _spec((CHUNK, qk_w), lambda c: (c, 1)),
        _spec((CHUNK, v_w), lambda c: (c, 1)), _spec((CHUNK, v_w), lambda c: (c, 2)),
    ] + _ret_const_specs(cm) + [_spec((1, v_w), lambda c: (0, 0))]
    qk_out = _spec((CHUNK, qk_w), lambda c: (c, 0))
    v_out = _spec((CHUNK, v_w), lambda c: (c, 0))
    return _pcall(
        body, name="ret_fwd", grid=(nC,),
        in_specs=in_specs,
        out_specs=[qk_out, qk_out, v_out, v_out,
                   _spec((RET_HEADS, None, RET_DK, RET_DV), lambda c: (0, c, 0, 0))],
        out_shape=[_sds((T, qk_w), F32), _sds((T, qk_w), F32), _sds((T, v_w), F32), _sds((T, v_w), BF16),
                   _sds((RET_HEADS, nC, RET_DK, RET_DV), F32)],
        scratch_shapes=[pltpu.VMEM((RET_HEADS, RET_DK, RET_DV), F32)],
    )(u, u, u, u, *consts, ret_gn)


def _ret_bwd(dyr, ret, u, qr, kr, states, consts, ret_gn):
    T = u.shape[0]
    nC = T // CHUNK
    kscale = RET_DK ** -0.5

    def body(dyr_ref, ret_ref, g_ref, q_ref, k_ref, v_ref, st_ref,
             cos_ref, sin_ref, dm_ref, kd_ref, qd_ref, cd_ref, gn_ref,
             dq_ref, dk_ref, dv_ref, dg_ref, dgn_ref, gstate):
        first = pl.program_id(0) == 0

        @pl.when(first)
        def _():
            gstate[...] = jnp.zeros_like(gstate)

        cosv, sinv = cos_ref[...], sin_ref[...]
        dgn_parts = []
        for h in range(RET_HEADS):
            hk, hv = _head(h, RET_DK), _head(h, RET_DV)
            ret = ret_ref[:, hv]
            mu = jnp.mean(ret, axis=-1, keepdims=True)
            xc = ret - mu
            rs = lax.rsqrt(jnp.mean(xc * xc, axis=-1, keepdims=True) + EPS)
            yn = xc * rs
            gn = gn_ref[:, hv]
            g = g_ref[:, hv]
            sg = _sigmoid(g)
            dyr_v = dyr_ref[:, hv]
            dretn = dyr_v * (g * sg)
            dg_ref[:, hv] = (dyr_v * (yn * gn) * (sg * (1.0 + g * (1.0 - sg)))).astype(BF16)
            dgn_parts.append(jnp.sum(dretn * yn, axis=0, keepdims=True))
            dyn = dretn * gn
            d_o = rs * (dyn - jnp.mean(dyn, axis=-1, keepdims=True)
                        - yn * jnp.mean(dyn * yn, axis=-1, keepdims=True))

            q, k, v = q_ref[:, hk], k_ref[:, hk], v_ref[:, hv]
            dmat, kd, qd = dm_ref[h], kd_ref[h], qd_ref[h]
            prev = st_ref[h]
            gnext = gstate[h]
            s = _dot(q, k, "nt") * dmat
            ds = _dot(d_o, v, "nt") * dmat
            doq = d_o * qd
            dq = _dot(ds, k, "nn") + _dot(doq, prev, "nt")
            dk = _dot(ds, q, "tn") + _dot(v, gnext, "nt") * kd
            dv = _dot(s, d_o, "tn") + _dot(k * kd, gnext, "nn")
            gstate[h] = cd_ref[h] * gnext + _dot(q, doq, "tn")
            dq_ref[:, hk] = _unrot(dq, cosv, sinv).astype(BF16)
            dk_ref[:, hk] = _unrot(dk * kscale, cosv, sinv).astype(BF16)
            dv_ref[:, hv] = dv.astype(BF16)
        _accumulate(dgn_ref, jnp.concatenate(dgn_parts, axis=1), first)

    cm = lambda c: nC - 1 - c
    qk_w, v_w = RET_HEADS * RET_DK, RET_HEADS * RET_DV
    vspec = lambda blk: _spec((CHUNK, v_w), lambda c: (cm(c), blk))
    qspec = _spec((CHUNK, qk_w), lambda c: (cm(c), 0))
    in_specs = [vspec(0), vspec(0), vspec(2), qspec, qspec, vspec(1),
                _spec((RET_HEADS, None, RET_DK, RET_DV), lambda c: (0, cm(c), 0, 0)),
                ] + _ret_const_specs(cm) + [_spec((1, v_w), lambda c: (0, 0))]
    return _pcall(
        body, name="ret_bwd", grid=(nC,),
        in_specs=in_specs,
        out_specs=[qspec, qspec, vspec(0), vspec(0), _spec((1, v_w), lambda c: (0, 0))],
        out_shape=[_sds((T, qk_w), BF16), _sds((T, qk_w), BF16), _sds((T, v_w), BF16), _sds((T, v_w), BF16),
                   _sds((1, v_w), F32)],
        scratch_shapes=[pltpu.VMEM((RET_HEADS, RET_DK, RET_DV), F32)],
    )(dyr, ret, u, qr, kr, u, states, *consts, ret_gn)


def _shift_down(x, s):
    rows = lax.broadcasted_iota(jnp.int32, x.shape, 0)
    return jnp.where(rows >= s, pltpu.roll(x, s, 0), 0.0)


def _shift_up(x, s):
    n = x.shape[0]
    rows = lax.broadcasted_iota(jnp.int32, x.shape, 0)
    return jnp.where(rows < n - s, pltpu.roll(x, n - s, 0), 0.0)


def _lru_specs(T):
    col = lambda off: _spec((T, LRU_BLOCK), lambda g: (0, off + g))
    vec = _spec((1, LRU_BLOCK), lambda g: (0, g))
    wblk = _spec((None, LRU_BLOCK, LRU_BLOCK), lambda g: (g, 0, 0))
    cw = _spec((CONV_TAPS, LRU_BLOCK), lambda g: (0, g))
    return col, vec, wblk, cw


def _lru_gates_fwd(u, conv_w, conv_b, w_r, b_r, w_i, b_i, lam):
    T = u.shape[0]
    col, vec, wblk, cw = _lru_specs(T)

    def body(x_ref, cw_ref, cb_ref, wr_ref, br_ref, wi_ref, bi_ref, lam_ref,
             xc_ref, r_ref, i_ref, a_ref, bx_ref):
        x = x_ref[...]
        w = cw_ref[...]
        xc = (_shift_down(x, 3) * w[0:1] + _shift_down(x, 2) * w[1:2] + _shift_down(x, 1) * w[2:3]
              + x * w[3:4] + cb_ref[...])
        r = _sigmoid(_dot(xc, wr_ref[...], "nn") + br_ref[...])
        i = _sigmoid(_dot(xc, wi_ref[...], "nn") + bi_ref[...])
        la = (-LRU_C) * r * _softplus(-lam_ref[...])
        xc_ref[...] = xc
        r_ref[...] = r
        i_ref[...] = i
        a_ref[...] = jnp.exp(la)
        bx_ref[...] = jnp.sqrt(-_expm1(2.0 * la)) * (i * xc)

    out = col(0)
    return _pcall(
        body, name="lru_gates_fwd", grid=(LRU_BLOCKS,),
        in_specs=[col(24), cw, vec, wblk, vec, wblk, vec, vec],
        out_specs=[out] * 5,
        out_shape=[_sds((T, D), F32)] * 5,
    )(u, conv_w, conv_b, w_r, b_r, w_i, b_i, lam)


def _lru_scan(name, a3, b3, reverse):
    T = a3.shape[0]
    nt = T // SCAN_TILE
    unroll = 8

    def body(a_ref, b_ref, o_ref, carry):
        @pl.when(pl.program_id(0) == 0)
        def _():
            carry[...] = jnp.zeros_like(carry)

        if not reverse:
            def step(t, h):
                h = a_ref[t] * h + b_ref[t]
                o_ref[t] = h
                return h
        else:
            def step(k, c):
                t = SCAN_TILE - 1 - k
                l = b_ref[t] + c
                o_ref[t] = l
                return a_ref[t] * l
        carry[...] = lax.fori_loop(0, SCAN_TILE, step, carry[...], unroll=unroll)

    idx = (lambda i: (nt - 1 - i, 0, 0)) if reverse else (lambda i: (i, 0, 0))
    blk = _spec((SCAN_TILE, LRU_BLOCKS, LRU_BLOCK), idx)
    return _pcall(
        body, name=name, grid=(nt,),
        in_specs=[blk, blk], out_specs=blk,
        out_shape=_sds((T, LRU_BLOCKS, LRU_BLOCK), F32),
        scratch_shapes=[pltpu.VMEM((LRU_BLOCKS, LRU_BLOCK), F32)],
    )(a3, b3)


def _lru_gates_bwd(lmb, hl, a, r, i, xc, u, conv_w, w_r, w_i, lam):
    T = u.shape[0]
    col, vec, wblk, cw = _lru_specs(T)

    def body(l_ref, h_ref, a_ref, r_ref, i_ref, xc_ref, x_ref, cw_ref, wr_ref, wi_ref, lam_ref,
             dx_ref, dwr_ref, dwi_ref, dvec_ref, dcw_ref):
        l = l_ref[...]
        av, rv, iv, xc = a_ref[...], r_ref[...], i_ref[...], xc_ref[...]
        lam_v = lam_ref[...]
        sp = _softplus(-lam_v)
        la = (-LRU_C) * rv * sp
        mult = jnp.sqrt(-_expm1(2.0 * la))
        da = l * _shift_down(h_ref[...], 1)
        dmult = l * (iv * xc)
        di = l * mult * xc
        dxc = l * mult * iv
        dla = da * av - dmult * (av * av) / mult
        dzr = (dla * ((-LRU_C) * sp)) * rv * (1.0 - rv)
        dzi = di * iv * (1.0 - iv)
        dsp = jnp.sum(dla * ((-LRU_C) * rv), axis=0, keepdims=True)
        dlam = dsp * (-_sigmoid(-lam_v))
        dwr_ref[...] = _dot(xc, dzr, "tn")
        dwi_ref[...] = _dot(xc, dzi, "tn")
        dxc = dxc + _dot(dzr, wr_ref[...], "nt") + _dot(dzi, wi_ref[...], "nt")
        x = x_ref[...]
        w = cw_ref[...]
        dx = (dxc * w[3:4] + _shift_up(dxc, 1) * w[2:3] + _shift_up(dxc, 2) * w[1:2]
              + _shift_up(dxc, 3) * w[0:1])
        dx_ref[...] = dx.astype(BF16)
        dvec_ref[...] = jnp.concatenate(
            [jnp.sum(dzr, axis=0, keepdims=True), jnp.sum(dzi, axis=0, keepdims=True), dlam,
             jnp.sum(dxc, axis=0, keepdims=True)], axis=0)
        dcw_ref[...] = jnp.concatenate(
            [jnp.sum(dxc * _shift_down(x, 3 - tap), axis=0, keepdims=True) if tap < 3
             else jnp.sum(dxc * x, axis=0, keepdims=True) for tap in range(CONV_TAPS)], axis=0)

    c0 = col(0)
    return _pcall(
        body, name="lru_gates_bwd", grid=(LRU_BLOCKS,),
        in_specs=[c0, c0, c0, c0, c0, c0, col(24), cw, wblk, wblk, vec],
        out_specs=[c0, wblk, wblk, cw, cw],
        out_shape=[_sds((T, D), BF16), _sds((LRU_BLOCKS, LRU_BLOCK, LRU_BLOCK), F32),
                   _sds((LRU_BLOCKS, LRU_BLOCK, LRU_BLOCK), F32), _sds((4, D), F32), _sds((CONV_TAPS, D), F32)],
    )(lmb, hl, a, r, i, xc, u, conv_w, w_r, w_i, lam)


def _xattn_probs(q, k):
    sc = _dot(q, k, "nt") * (X_HD ** -0.5)
    e = jnp.exp(sc - jnp.max(sc, axis=-1, keepdims=True))
    return e / jnp.sum(e, axis=-1, keepdims=True)


def _xattn_fwd(xq, xk, xv):
    T = xq.shape[0]
    tq = ROW_TILE
    M = xk.shape[0]

    def body(q_ref, k_ref, v_ref, o_ref):
        p = _xattn_probs(q_ref[...], k_ref[...])
        o_ref[...] = _dot(p, v_ref[...], "nn").astype(BF16)

    qs = _spec((tq, X_HD), lambda h, i: (i, h))
    kv = _spec((M, X_HD), lambda h, i: (0, h))
    return _pcall(
        body, name="xattn_fwd", grid=(X_HEADS, T // tq),
        in_specs=[qs, kv, kv], out_specs=qs, out_shape=_sds((T, D), BF16),
    )(xq, xk, xv)


def _xattn_bwd(xq, xk, xv, dxo):
    T = xq.shape[0]
    tq = ROW_TILE
    M = xk.shape[0]

    def body(q_ref, k_ref, v_ref, do_ref, dq_ref, dk_ref, dv_ref):
        first = pl.program_id(1) == 0
        q, k, v, do = q_ref[...], k_ref[...], v_ref[...], do_ref[...]
        p = _xattn_probs(q, k)
        dp = _dot(do, v, "nt")
        ds = p * (dp - jnp.sum(dp * p, axis=-1, keepdims=True)) * (X_HD ** -0.5)
        dq_ref[...] = _dot(ds, k, "nn").astype(BF16)
        _accumulate(dk_ref, _dot(ds, q, "tn"), first)
        _accumulate(dv_ref, _dot(p, do, "tn"), first)

    qs = _spec((tq, X_HD), lambda h, i: (i, h))
    kv = _spec((M, X_HD), lambda h, i: (0, h))
    return _pcall(
        body, name="xattn_bwd", grid=(X_HEADS, T // tq),
        in_specs=[qs, kv, kv, qs], out_specs=[qs, kv, kv],
        out_shape=[_sds((T, D), BF16), _sds((M, D), F32), _sds((M, D), F32)],
    )(xq, xk, xv, dxo)


def _final_loss(x, g, tgt):
    T = x.shape[0]
    tm = ROW_TILE

    def fn(irefs, orefs, ids):
        xv, gv = irefs[0][...], irefs[1][...]
        err = _rms_fwd(xv, gv) - irefs[2][...]
        lp = 0.5 * jnp.sum(jnp.mean(err * err, axis=-1, keepdims=True), axis=0, keepdims=True)
        first = ids[0] == 0
        _accumulate(orefs[0], jnp.broadcast_to(lp, (1, 128)), first)
        dx, dgp = _rms_bwd(xv, gv, err * (1.0 / D))
        orefs[1][...] = dx
        _accumulate(orefs[2], dgp, first)

    row = _spec((tm, D), lambda i: (i, 0))
    vec = _spec((1, D), lambda i: (0, 0))
    return _rowwise(
        "final_loss", fn, [(x, row), (g, vec), (tgt, row)],
        [(_sds((1, 128), F32), _spec((1, 128), lambda i: (0, 0))), (_sds((T, D), F32), row),
         (_sds((1, D), F32), vec)],
        (T // tm,))


def _adamw(name, w, g, m, v):
    R, C = w.shape
    tr = R
    for cand in (512, 352, 256):
        if R % cand == 0:
            tr = cand
            break

    def fn(irefs, orefs, ids):
        delta, mn, vn = _adamw_update(*(r[...] for r in irefs))
        orefs[0][...] = delta
        orefs[1][...] = mn
        orefs[2][...] = vn

    blk = _spec((tr, C), lambda i: (i, 0))
    return _rowwise(name, fn, [(w, blk), (g, blk), (m, blk), (v, blk)],
                    [(_sds((R, C), F32), blk)] * 3, (R // tr,))


def _adamw_update(wv, gv, mv, vv):
    c1 = 1.0 - ADAM_B1 ** ADAM_STEP
    c2 = 1.0 - ADAM_B2 ** ADAM_STEP
    mn = ADAM_B1 * mv + (1.0 - ADAM_B1) * gv
    vn = ADAM_B2 * vv + (1.0 - ADAM_B2) * (gv * gv)
    delta = -ADAM_LR * ((mn / c1) / (jnp.sqrt(vn / c2) + ADAM_EPS) + ADAM_WD * wv)
    return delta, mn, vn


def _adamw_halves(name, w, mine, theirs, widx, m, v, core):
    R, C = w.shape
    H = R // 2
    tr = H
    while tr * C * 4 > (1 << 20) and tr % 16 == 0:
        tr //= 2
    nb = H // tr

    def body(core_ref, w_ref, mine_ref, theirs_ref, m_ref, v_ref, g_out, d_out, m_out, v_out):
        gv = jnp.where(pl.program_id(0) == core_ref[0], mine_ref[...], theirs_ref[...])
        delta, mn, vn = _adamw_update(w_ref[...], gv, m_ref[...], v_ref[...])
        g_out[...] = gv
        d_out[...] = delta
        m_out[...] = mn
        v_out[...] = vn

    full = pl.BlockSpec((tr, C), lambda h, i, core_ref: (h * nb + i, 0))
    mine_spec = pl.BlockSpec((None, tr, C), lambda h, i, core_ref: (widx, jnp.where(h == core_ref[0], i, 0), 0))
    theirs_spec = pl.BlockSpec((None, tr, C), lambda h, i, core_ref: (widx, jnp.where(h == core_ref[0], 0, i), 0))
    return _pcall(
        body, name=name, grid=(2, nb), num_prefetch=1,
        in_specs=[full, mine_spec, theirs_spec, full, full], out_specs=[full] * 4,
        out_shape=[_sds((R, C), F32)] * 4,
    )(core, w, mine, theirs, m, v)


def _rmsnorm(name, x, g):
    M = x.shape[0]
    tm = min(ROW_TILE, M)

    def fn(irefs, orefs, ids):
        orefs[0][...] = _rms_fwd(irefs[0][...], irefs[1][...]).astype(BF16)

    row = _spec((tm, D), lambda i: (i, 0))
    return _rowwise(name, fn, [(x, row), (g, _spec((1, D), lambda i: (0, 0)))],
                    [(_sds((M, D), BF16), row)], (M // tm,))[0]


WEIGHT_AT = {
    "ffn1_w1": ("col1", 0), "ffn1_w3": ("col1", 1), "ffn1_w2": ("row2a", 0),
    "w_ret_o": ("sqA", 0), "w_lru_o": ("sqA", 1), "w_out": ("sqA", 2),
    "w_xq": ("sqB", 0), "w_xk": ("sqB", 1), "w_xv": ("sqC", 0), "w_xo": ("sqC", 1),
    "ffn2_w1": ("col2a", 0), "ffn2_w3": ("col2b", 0), "ffn2_w2": ("row2b", 0),
}


def _local_step(x, mem, tgt, gw, sm, big):
    T = x.shape[0]
    tm = ROW_TILE

    def wt(name):
        key, idx = WEIGHT_AT[name]
        return gw[key], idx

    row3 = lambda i, j, r: (i, 0)
    vec3 = lambda i, j, r: (0, 0)
    rowD = _spec((tm, D), row3)
    vecD = _spec((1, D), vec3)

    def residual_norm(acc, erefs, orefs, ids):
        xo = erefs[0][...] + acc
        orefs[0][...] = xo
        orefs[1][...] = _rms_fwd(xo, erefs[1][...]).astype(BF16)

    def res_norm_io(x_res, g):
        return ([(x_res, rowD), (g, vecD)],
                [(_sds((T, D), F32), rowD), (_sds((T, D), BF16), rowD)])

    h1 = _rmsnorm("ffn1_norm", x, sm["ffn1_norm"])
    a1, b1, s1 = _ffn_up("ffn1_up", h1, *wt("ffn1_w1"), *wt("ffn1_w3"))
    x1, h2 = _ffn_down("ffn1_down", s1, *wt("ffn1_w2"), x, sm["mix_norm"])

    tw = min(WIDE_ROW_TILE, T)
    wideD = _spec((tw, D), row3)
    u = _gemm(
        "mix_in",
        [(h2, wideD, gw["win"], _spec((None, None, IN_BLK, D), lambda i, j, r: (j, 0, 0, 0)), "nt")],
        (T // tw, N_CHIPS, 1),
        [(_sds((T, 5120), F32), _spec((tw, IN_BLK), lambda i, j, r: (i, j)))], (tw, IN_BLK))[0]

    consts = _retention_constants(T)
    qr, kr, ret, yr, states = _ret_fwd(u, consts, sm["ret_gn"])

    conv_w = gw["conv"][:, 0].transpose(1, 0, 2).reshape(CONV_TAPS, D)
    xc, rg, ig, av, bx = _lru_gates_fwd(u, conv_w, sm["conv_b"], sm["w_rgate"], sm["b_rgate"],
                                        sm["w_igate"], sm["b_igate"], sm["lru_lambda"])
    a3 = av.reshape(T, LRU_BLOCKS, LRU_BLOCK)
    hl = _lru_scan("lru_scan_fwd", a3, bx.reshape(T, LRU_BLOCKS, LRU_BLOCK), False).reshape(T, D)

    row1 = _spec((tm, D), lambda i: (i, 0))
    glru1 = _spec((tm, D), lambda i: (i, 4))

    def lru_out(irefs, orefs, ids):
        gl, _ = _gelu_and_grad(irefs[1][...])
        orefs[0][...] = (irefs[0][...] * gl).astype(BF16)

    yl = _rowwise("lru_out", lru_out, [(hl, row1), (u, glru1)], [(_sds((T, D), BF16), row1)], (T // tm,))[0]

    def gate_epilogue(acc, erefs, orefs, ids):
        orefs[0][...] = _sigmoid(acc + erefs[0][...])

    gates = _gemm(
        "mix_gates",
        [(h2, wideD, gw["wbg"], _spec((None, None, BG_BLK, D), lambda i, j, r: (j, 0, 0, 0)), "nt")],
        (T // tw, N_CHIPS, 1),
        [(_sds((T, 2 * D), F32), _spec((tw, BG_BLK), lambda i, j, r: (i, j)))], (tw, BG_BLK),
        [(sm["b_branch_gate"], _spec((1, BG_BLK), lambda i, j, r: (0, j)))], gate_epilogue)[0]

    y_ret = _proj_sq("y_ret", yr, *wt("w_ret_o"), "nn")[0]

    def merge_epilogue(acc, erefs, orefs, ids):
        orefs[0][...] = acc
        orefs[1][...] = (erefs[0][...] * erefs[2][...] + erefs[1][...] * acc).astype(BF16)

    y_lru, merged = _proj_sq(
        "y_lru", yl, *wt("w_lru_o"), "nn",
        extras=[(gates, _spec((tm, D), lambda i, j, r: (i, 0))), (gates, _spec((tm, D), lambda i, j, r: (i, 1))),
                (y_ret, rowD)],
        epilogue=merge_epilogue,
        outs=[(_sds((T, D), F32), rowD), (_sds((T, D), BF16), rowD)])

    ex, ou = res_norm_io(x1, sm["xattn_norm"])
    x2, hq = _proj_sq("mix_out", merged, *wt("w_out"), "nn", extras=ex, epilogue=residual_norm, outs=ou)

    m = _rmsnorm("mem_norm", mem, sm["mem_norm"])
    xq = _proj_sq("xq", hq, *wt("w_xq"), "nn", BF16)[0]
    xk = _proj_sq("xk", m, *wt("w_xk"), "nn", BF16)[0]
    xv = _proj_sq("xv", m, *wt("w_xv"), "nn", BF16)[0]
    xo = _xattn_fwd(xq, xk, xv)
    ex, ou = res_norm_io(x2, sm["ffn2_norm"])
    x3, h3 = _proj_sq("xattn_out", xo, *wt("w_xo"), "nn", extras=ex, epilogue=residual_norm, outs=ou)

    a2, b2, s2 = _ffn_up("ffn2_up", h3, *wt("ffn2_w1"), *wt("ffn2_w3"))
    x4 = _ffn_down("ffn2_down", s2, *wt("ffn2_w2"), x3)[0]
    loss, dx4, dg_final = _final_loss(x4, sm["final_norm"], tgt)

    dx3, dg_ffn2 = _ffn_bwd("ffn2", dx4, h3, a2, b2, s2, *wt("ffn2_w1"), *wt("ffn2_w3"),
                            *wt("ffn2_w2"), x3, sm["ffn2_norm"], big)

    dxo = _proj_sq("d_xo", dx3, *wt("w_xo"), "nt", BF16)[0]
    big["w_xo"] = _dw_sq("dw_xo", xo, dx3)[None]
    dxq, dxk, dxv = _xattn_bwd(xq, xk, xv, dxo)
    big["w_xq"] = _dw_sq("dw_xq", hq, dxq)[None]
    ex, ou = _rms_bwd_io(x2, sm["xattn_norm"], dx3, T, tm)
    dx2, dg_xattn = _proj_sq("d_hq", dxq, *wt("w_xq"), "nt", extras=ex, epilogue=_rms_bwd_epilogue, outs=ou)
    big["w_xk"] = _dw_sq("dw_xk", m, dxk)[None]
    big["w_xv"] = _dw_sq("dw_xv", m, dxv)[None]

    M = mem.shape[0]

    def mem_norm_epilogue(acc, erefs, orefs, ids):
        _, dgp = _rms_bwd(erefs[0][...], erefs[1][...], acc)
        orefs[0][...] = dgp

    wsq_spec = lambda idx: _spec((N_CHIPS, None, SQ_BLK, D), lambda i, j, r: (0, idx, 0, 0))
    memD = _spec((M, D), row3)
    dg_mem = _gemm(
        "d_mem_norm",
        [(dxk, memD, wt("w_xk")[0], wsq_spec(wt("w_xk")[1]), "nt"),
         (dxv, memD, wt("w_xv")[0], wsq_spec(wt("w_xv")[1]), "nt")],
        (1, 1, 1), [(_sds((1, D), F32), vecD)], (M, D),
        [(mem, memD), (sm["mem_norm"], vecD)], mem_norm_epilogue)[0]

    def merged_bwd_epilogue(acc, erefs, orefs, ids):
        gr, gl, yrv, ylv = (e[...] for e in erefs)
        orefs[0][...] = (acc * gr).astype(BF16)
        orefs[1][...] = (acc * gl).astype(BF16)
        dgr = acc * yrv * gr * (1.0 - gr)
        dgl = acc * ylv * gl * (1.0 - gl)
        orefs[2][:, :D] = dgr.astype(BF16)
        orefs[2][:, D:] = dgl.astype(BF16)
        dbb = jnp.concatenate([jnp.sum(dgr, axis=0, keepdims=True), jnp.sum(dgl, axis=0, keepdims=True)], axis=1)
        _accumulate(orefs[3], dbb, ids[0] == 0)

    dy_ret, dy_lru, dgpre, db_bg = _proj_sq(
        "d_merged", dx2, *wt("w_out"), "nt",
        extras=[(gates, _spec((tm, D), lambda i, j, r: (i, 0))), (gates, _spec((tm, D), lambda i, j, r: (i, 1))),
                (y_ret, rowD), (y_lru, rowD)],
        epilogue=merged_bwd_epilogue,
        outs=[(_sds((T, D), BF16), rowD), (_sds((T, D), BF16), rowD),
              (_sds((T, 2 * D), BF16), _spec((tm, 2 * D), row3)),
              (_sds((1, 2 * D), F32), _spec((1, 2 * D), vec3))])
    big["w_branch_gate"] = _gemm(
        "dw_bg",
        [(h2, _spec((T, D), lambda j, n, r: (r, 0)), dgpre, _spec((T, BG_BLK), lambda j, n, r: (r, j)), "tn")],
        (N_CHIPS, 1, 1),
        [(_sds((N_CHIPS, D, BG_BLK), GRAD_WIRE_DTYPE), _spec((None, D, BG_BLK), lambda j, n, r: (j, 0, 0)))],
        (D, BG_BLK))[0][None]
    big["w_out"] = _dw_sq("dw_out", merged, dx2)[None]
    dyr = _proj_sq("d_yr", dy_ret, *wt("w_ret_o"), "nt")[0]
    big["w_ret_o"] = _dw_sq("dw_ret_o", yr, dy_ret)[None]
    dyl = _proj_sq("d_yl", dy_lru, *wt("w_lru_o"), "nt")[0]
    big["w_lru_o"] = _dw_sq("dw_lru_o", yl, dy_lru)[None]

    dq, dk, dv, dgr, dg_retgn = _ret_bwd(dyr, ret, u, qr, kr, states, consts, sm["ret_gn"])

    def lru_out_bwd(irefs, orefs, ids):
        gl, dgl = _gelu_and_grad(irefs[2][...])
        dyl_v = irefs[0][...]
        orefs[0][...] = dyl_v * gl
        orefs[1][...] = (dyl_v * irefs[1][...] * dgl).astype(BF16)

    dhl, dglru = _rowwise("lru_out_bwd", lru_out_bwd, [(dyl, row1), (hl, row1), (u, glru1)],
                          [(_sds((T, D), F32), row1), (_sds((T, D), BF16), row1)], (T // tm,))
    lmb = _lru_scan("lru_scan_bwd", a3, dhl.reshape(T, LRU_BLOCKS, LRU_BLOCK), True).reshape(T, D)
    dxl, dw_r, dw_i, dvec, dcw = _lru_gates_bwd(lmb, hl, av, rg, ig, xc, u, conv_w,
                                                sm["w_rgate"], sm["w_igate"], sm["lru_lambda"])

    du = jnp.concatenate([dq, dk, dv, dgr, dxl, dglru], axis=1)
    tk = T
    big["w_in"] = _gemm(
        "dw_in",
        [(h2, _spec((tk, D), lambda j, n, r: (r, 0)), du, _spec((tk, IN_BLK), lambda j, n, r: (r, j)), "tn")],
        (N_CHIPS, 1, T // tk),
        [(_sds((N_CHIPS, D, IN_BLK), GRAD_WIRE_DTYPE), _spec((None, D, IN_BLK), lambda j, n, r: (j, 0, 0)))],
        (D, IN_BLK))[0][None]
    tf = min(FFN_ROW_TILE, T)
    ex, ou = _rms_bwd_io(x1, sm["mix_norm"], dx2, T, tf)
    dx1, dg_mix = _gemm(
        "d_h2",
        [(du, _spec((tf, 5120), row3), gw["win"], _spec((N_CHIPS, None, IN_BLK, D), lambda i, j, r: (0, 0, 0, 0)), "nn"),
         (dgpre, _spec((tf, 2 * D), row3), gw["wbg"], _spec((N_CHIPS, None, BG_BLK, D), lambda i, j, r: (0, 0, 0, 0)),
          "nn")],
        (T // tf, 1, 1), ou, (tf, D), ex, _rms_bwd_epilogue)

    grad_x, dg_ffn1 = _ffn_bwd("ffn1", dx1, h1, a1, b1, s1, *wt("ffn1_w1"), *wt("ffn1_w3"),
                               *wt("ffn1_w2"), x, sm["ffn1_norm"], big)

    small = {
        "ffn1_norm": dg_ffn1, "mix_norm": dg_mix, "ret_gn": dg_retgn, "conv_b": dvec[3:4],
        "b_rgate": dvec[0:1], "b_igate": dvec[1:2], "lru_lambda": dvec[2:3], "xattn_norm": dg_xattn,
        "mem_norm": dg_mem, "ffn2_norm": dg_ffn2, "final_norm": dg_final, "b_branch_gate": db_bg,
        "conv_w": dcw, "w_rgate": dw_r, "w_igate": dw_i,
    }
    return loss, grad_x, small


ANY_SPEC = pl.BlockSpec(memory_space=pl.ANY)
VMEM_SPEC = pl.BlockSpec(memory_space=pltpu.VMEM)
N_PEER_CHIPS = N_CHIPS - 1


def _mesh_position():
    x, y, c = lax.axis_index("x"), lax.axis_index("y"), lax.axis_index("c")
    chips = [(1 - x, y), (x, 1 - y), (1 - x, 1 - y)]
    return x, y, c, chips


def _chip_index(x, y):
    return 2 * x + y


def _rows_half(ref, axis, h):
    n = ref.shape[axis] // 2
    idx = [slice(None)] * len(ref.shape)
    idx[axis] = pl.ds(pl.multiple_of(h * n, 16), n)
    return ref.at[tuple(idx)]


def _remote(src, dst, send_sem, recv_sem, device):
    return pltpu.make_async_remote_copy(src_ref=src, dst_ref=dst, send_sem=send_sem, recv_sem=recv_sem,
                                        device_id=device, device_id_type=MESH)


def _gather_chips_task(shards, split, landed, part=0, nparts=1):
    keys = list(shards)
    n = len(keys)

    def operands():
        if part:
            return [shards[k] for k in keys] + [landed[k] for k in keys]
        chip_me = _chip_index(lax.axis_index("x"), lax.axis_index("y"))
        bases = [lax.dynamic_update_slice(lax.empty((N_CHIPS,) + shards[k].shape, shards[k].dtype), shards[k][None],
                                          (chip_me,) + (0,) * shards[k].ndim) for k in keys]
        return [shards[k] for k in keys] + bases

    def my_rows(ref, c):
        rows = ref.shape[1] // (2 * nparts)
        return ref.at[:, pl.ds(pl.multiple_of((c * nparts + part) * rows, 16), rows), :]

    def make(ins, outs, send_sem, recv_sem):
        x, y, c, chips = _mesh_position()
        s_me = _chip_index(x, y)
        starts, arrivals = [], []
        for g in range(n):
            mine = my_rows(ins[g], c) if split else ins[g]
            for k, chip in enumerate(chips):
                def landing(s):
                    o = outs[g].at[s]
                    return my_rows(o, c) if split else o
                starts.append(_remote(mine, landing(s_me), send_sem(3 * g + k), recv_sem(3 * g + k), (*chip, c)))
                got = landing(_chip_index(*chip))
                arrivals.append(functools.partial(_remote, got, got, send_sem(3 * g + k), recv_sem(3 * g + k),
                                                  (*chip, c)))
        return starts, arrivals

    def finish(res):
        landed.update(zip(keys, res))

    return _Task(operands, lambda: [_sds((N_CHIPS,) + shards[k].shape, shards[k].dtype) for k in keys],
                 {n + g: g for g in range(n)}, 3 * n, make, finish)


def _gather_sibling_task(keys, landed, ready):
    n = len(keys)

    def make(ins, outs, send_sem, recv_sem):
        x, y, c, chips = _mesh_position()
        starts, arrivals = [], []
        for g in range(n):
            for k, chip in enumerate(chips):
                o = outs[g].at[_chip_index(*chip)]
                got, other = _rows_half(o, 1, c), _rows_half(o, 1, 1 - c)
                starts.append(_remote(got, got, send_sem(3 * g + k), recv_sem(3 * g + k), (x, y, 1 - c)))
                arrivals.append(functools.partial(_remote, other, other, send_sem(3 * g + k), recv_sem(3 * g + k),
                                                  (x, y, 1 - c)))
        return starts, arrivals

    def finish(res):
        ready.update(zip(keys, res))

    return _Task(lambda: [landed[k] for k in keys], lambda: [_sds(landed[k].shape, landed[k].dtype) for k in keys],
                 {g: g for g in range(n)}, 3 * n, make, finish)


def _pair_swap_task(names, big, got):
    n = len(names)

    def make(ins, outs, send_sem, recv_sem):
        x, y, c, _ = _mesh_position()
        copies = [_remote(_rows_half(ins[a], 2, 1 - c), outs[a], send_sem(a), recv_sem(a), (x, y, 1 - c))
                  for a in range(n)]
        return copies, [functools.partial(lambda cp: cp, cp) for cp in copies]

    def shapes():
        return [_sds(big[k].shape[:2] + (big[k].shape[2] // 2, big[k].shape[3]), big[k].dtype) for k in names]

    return _Task(lambda: [big[k] for k in names], shapes, {}, n, make, lambda res: got.update(zip(names, res)))


def _rs_pair_sum(name, full, got, core):
    nw, ns, R, C = full.shape
    half = R // 2

    def body(core_ref, a_ref, b_ref, o_ref):
        o_ref[...] = (a_ref[...].astype(F32) + b_ref[...].astype(F32)).astype(BF16)

    blk = lambda fn: pl.BlockSpec((None, None, half, C), fn)
    return _pcall(
        body, name=name, grid=(nw, ns), num_prefetch=1,
        in_specs=[blk(lambda w, s, core_ref: (w, s, core_ref[0], 0)), blk(lambda w, s, core_ref: (w, s, 0, 0))],
        out_specs=blk(lambda w, s, core_ref: (w, s, 0, 0)),
        out_shape=_sds((nw, ns, half, C), BF16),
    )(core, full, got)


def _chip_exchange_task(names, pair_sums, by_source, part=0, nparts=1):
    n = len(names)

    def rows(ref):
        h = ref.shape[1] // nparts
        return ref.at[:, pl.ds(part * h, h), :]

    def make(ins, outs, send_sem, recv_sem):
        x, y, c, chips = _mesh_position()
        s_me = _chip_index(x, y)
        starts, arrivals = [], []
        for a in range(n):
            for k, chip in enumerate(chips):
                s_k = _chip_index(*chip)
                starts.append(_remote(rows(ins[a].at[:, s_k]), rows(outs[a].at[:, s_me]), send_sem(3 * a + k),
                                      recv_sem(3 * a + k), (*chip, c)))
                got = rows(outs[a].at[:, s_k])
                arrivals.append(functools.partial(_remote, got, got, send_sem(3 * a + k), recv_sem(3 * a + k),
                                                  (*chip, c)))
        return starts, arrivals

    def operands():
        return [pair_sums[k] for k in names] + ([by_source[k] for k in names] if part else [])

    return _Task(operands, lambda: [_sds(pair_sums[k].shape, pair_sums[k].dtype) for k in names],
                 {n + a: a for a in range(n)} if part else {}, 3 * n, make,
                 lambda res: by_source.update(zip(names, res)))


def _rs_chip_sum(name, own, parts, chip):
    nw, ns, H, C = parts.shape

    def body(chip_ref, own_ref, *rest):
        prefs, o_ref = rest[:ns], rest[ns]
        me = chip_ref[0]
        own_v = own_ref[...].astype(F32)
        tot = None
        for s in range(ns):
            term = jnp.where(me == s, own_v, prefs[s][...].astype(F32))
            tot = term if tot is None else tot + term
        o_ref[...] = tot

    blk = lambda fn: pl.BlockSpec((None, None, H, C), fn)

    def part_spec(s):
        return blk(lambda w, chip_ref: (w, jnp.where(chip_ref[0] == s, (s + 1) % ns, s), 0, 0))

    return _pcall(
        body, name=name, grid=(nw,), num_prefetch=1,
        in_specs=[blk(lambda w, chip_ref: (w, chip_ref[0], 0, 0))] + [part_spec(s) for s in range(ns)],
        out_specs=pl.BlockSpec((None, H, C), lambda w, chip_ref: (w, 0, 0)),
        out_shape=_sds((nw, H, C), F32),
    )(chip, own, *([parts] * ns))


def _pair_gather_task(names, halves, sibling_halves):
    n = len(names)

    def make(ins, outs, send_sem, recv_sem):
        x, y, c, _ = _mesh_position()
        copies = [_remote(ins[a], outs[a], send_sem(a), recv_sem(a), (x, y, 1 - c)) for a in range(n)]
        return copies, [functools.partial(lambda cp: cp, cp) for cp in copies]

    return _Task(lambda: [halves[k] for k in names], lambda: [_sds(halves[k].shape, F32) for k in names],
                 {}, n, make, lambda res: sibling_halves.update(zip(names, res)))


def _small_allreduce(arrs):
    n = len(arrs)
    per = 1 + 2 * N_PEER_CHIPS

    def body(*refs):
        v_refs, o_refs = refs[:n], refs[n:2 * n]
        sib, pair, part = refs[2 * n:3 * n], refs[3 * n:4 * n], refs[4 * n:5 * n]
        send_sems, recv_sems = refs[5 * n:]
        x, y, c, chips = _mesh_position()
        s_me = _chip_index(x, y)

        def quarter(ref, s):
            q = ref.shape[0] // N_CHIPS
            return ref.at[pl.ds(pl.multiple_of(s * q, 8), q)]

        def exchange(first_sem, src, dst_of, arrival_of):
            sems = lambda a, k: (send_sems.at[a * per + first_sem + k], recv_sems.at[a * per + first_sem + k])
            sends = [_remote(src(a, _chip_index(*chip)), dst_of(a, s_me), *sems(a, k), (*chip, c))
                     for a in range(n) for k, chip in enumerate(chips)]
            for cp in sends:
                cp.start()
            for a in range(n):
                for k, chip in enumerate(chips):
                    got = arrival_of(a, _chip_index(*chip))
                    _remote(got, got, *sems(a, k), (*chip, c)).wait_recv()
            for cp in sends:
                cp.wait_send()

        swaps = [_remote(v_refs[a], sib[a], send_sems.at[a * per], recv_sems.at[a * per], (x, y, 1 - c))
                 for a in range(n)]
        for cp in swaps:
            cp.start()
        for cp in swaps:
            cp.wait()
        for a in range(n):
            pair[a][...] = v_refs[a][...] + sib[a][...]
        exchange(1, lambda a, s_k: quarter(pair[a], s_k), lambda a, s: part[a].at[s], lambda a, s_k: part[a].at[s_k])
        for a in range(n):
            part[a][s_me] = quarter(pair[a], s_me)[...]
            q = o_refs[a].shape[0] // N_CHIPS
            o_refs[a][pl.ds(pl.multiple_of(s_me * q, 8), q), :] = (
                ((part[a][0] + part[a][1]) + part[a][2]) + part[a][3])
        exchange(1 + N_PEER_CHIPS, lambda a, s_k: quarter(o_refs[a], s_me), lambda a, s: quarter(o_refs[a], s),
                 lambda a, s_k: quarter(o_refs[a], s_k))

    shapes = [a.shape for a in arrs]
    return _pcall(
        body, name="small_allreduce", grid=(1,),
        in_specs=[VMEM_SPEC] * n, out_specs=[VMEM_SPEC] * n, out_shape=[_sds(s, F32) for s in shapes],
        scratch_shapes=([pltpu.VMEM(s, F32) for s in shapes] * 2
                        + [pltpu.VMEM((N_CHIPS, s[0] // N_CHIPS, s[1]), F32) for s in shapes]
                        + [pltpu.SemaphoreType.DMA((n * per,)), pltpu.SemaphoreType.DMA((n * per,))]),
    )(*arrs)


TRANSPOSED_WEIGHTS = ("ffn1_w1", "ffn1_w3", "ffn2_w1", "ffn2_w3")
SMALL_LAYOUT = [("ffn1_norm", 1), ("mix_norm", 1), ("ret_gn", 1), ("conv_b", 1), ("b_rgate", 1), ("b_igate", 1),
                ("lru_lambda", 1), ("xattn_norm", 1), ("mem_norm", 1), ("ffn2_norm", 1), ("final_norm", 1),
                ("b_branch_gate", 2), ("conv_w", CONV_TAPS)]
SMALL_ROWS = 32
GATE_WEIGHTS = ("w_rgate", "w_igate")
WEIGHT_ORDER = ["ffn1_norm", "ffn1_w1", "ffn1_w3", "ffn1_w2", "mix_norm", "w_in", "ret_gn", "w_ret_o", "conv_w",
                "conv_b", "w_rgate", "b_rgate", "w_igate", "b_igate", "lru_lambda", "w_lru_o", "w_branch_gate",
                "b_branch_gate", "w_out", "xattn_norm", "mem_norm", "w_xq", "w_xk", "w_xv", "w_xo", "ffn2_norm",
                "ffn2_w1", "ffn2_w3", "ffn2_w2", "final_norm"]


def _pack_small(parts):
    rows = [parts[name].reshape(n, D) for name, n in SMALL_LAYOUT]
    used = sum(n for _, n in SMALL_LAYOUT)
    rows.append(jnp.zeros((SMALL_ROWS - used, D), F32))
    return jnp.concatenate(rows, axis=0)


def _unpack_small(packed, shapes):
    out, r = {}, 0
    for name, n in SMALL_LAYOUT:
        out[name] = packed[r:r + n].reshape(shapes[name])
        r += n
    return out


def kernel(x, mem, ffn1_norm, ffn1_w1, ffn1_w3, ffn1_w2, mix_norm, w_in, ret_gn, w_ret_o, conv_w, conv_b, w_rgate, b_rgate, w_igate, b_igate, lru_lambda, w_lru_o, w_branch_gate, b_branch_gate, w_out, xattn_norm, mem_norm, w_xq, w_xk, w_xv, w_xo, ffn2_norm, ffn2_w1, ffn2_w3, ffn2_w2, final_norm, loss_target, m_ffn1_norm, m_ffn1_w1, m_ffn1_w3, m_ffn1_w2, m_mix_norm, m_w_in, m_ret_gn, m_w_ret_o, m_conv_w, m_conv_b, m_w_rgate, m_b_rgate, m_w_igate, m_b_igate, m_lru_lambda, m_w_lru_o, m_w_branch_gate, m_b_branch_gate, m_w_out, m_xattn_norm, m_mem_norm, m_w_xq, m_w_xk, m_w_xv, m_w_xo, m_ffn2_norm, m_ffn2_w1, m_ffn2_w3, m_ffn2_w2, m_final_norm, v_ffn1_norm, v_ffn1_w1, v_ffn1_w3, v_ffn1_w2, v_mix_norm, v_w_in, v_ret_gn, v_w_ret_o, v_conv_w, v_conv_b, v_w_rgate, v_b_rgate, v_w_igate, v_b_igate, v_lru_lambda, v_w_lru_o, v_w_branch_gate, v_b_branch_gate, v_w_out, v_xattn_norm, v_mem_norm, v_w_xq, v_w_xk, v_w_xv, v_w_xo, v_ffn2_norm, v_ffn2_w1, v_ffn2_w3, v_ffn2_w2, v_final_norm):
    given = dict(locals())
    w = {n: given[n] for n in WEIGHT_ORDER}
    mom = {n: given["m_" + n] for n in WEIGHT_ORDER}
    var = {n: given["v_" + n] for n in WEIGHT_ORDER}
    chip = _chip_index(lax.axis_index("x"), lax.axis_index("y"))
    core = lax.axis_index("c").astype(jnp.int32).reshape(1)

    chip_id = chip.astype(jnp.int32).reshape(1)
    sm = {n: w[n] for n in ["ffn1_norm", "mix_norm", "ret_gn", "conv_b", "b_rgate", "b_igate", "lru_lambda",
                            "xattn_norm", "mem_norm", "ffn2_norm", "b_branch_gate"]}
    sm["final_norm"] = w["final_norm"].reshape(1, D)
    sm["w_rgate"] = w["w_rgate"][0]
    sm["w_igate"] = w["w_igate"][0]

    local = lambda a, n: jnp.swapaxes(a[0], 0, 1) if n in TRANSPOSED_WEIGHTS else a[0]
    stack = lambda names: jnp.stack([local(w[n], n) for n in names], axis=0).astype(BF16)
    shard = {"col1": stack(["ffn1_w1", "ffn1_w3"]), "row2a": stack(["ffn1_w2"]),
             "win": jnp.swapaxes(w["w_in"], 1, 2).astype(BF16), "wbg": jnp.swapaxes(w["w_branch_gate"], 1, 2).astype(BF16), "sqA": stack(["w_ret_o", "w_lru_o", "w_out"]),
             "sqB": stack(["w_xq", "w_xk"]), "sqC": stack(["w_xv", "w_xo"]), "col2a": stack(["ffn2_w1"]), "col2b": stack(["ffn2_w3"]),
             "row2b": stack(["ffn2_w2"]), "conv": w["conv_w"]}
    gw, landed = {}, {}
    over_chips = lambda keys: _gather_chips_task({k: shard[k] for k in keys}, True, landed)
    to_sibling = lambda keys: _gather_sibling_task(keys, landed, gw)

    big, got, pair_sums, by_source, halves, sibling_halves, outs = {}, {}, {}, {}, {}, {}, {}
    pair_swap = lambda names: _pair_swap_task(names, big, got)
    exchange = lambda names, part=0, nparts=1: _chip_exchange_task(names, pair_sums, by_source, part, nparts)
    pair_gather = lambda names: _pair_gather_task(names, halves, sibling_halves)

    def pair_sum(names):
        for n in names:
            pair_sums[n] = _rs_pair_sum("rs_pair_sum_" + n, big[n], got[n], core)

    def chip_sum(names):
        for n in names:
            halves[n] = _rs_chip_sum("rs_chip_sum_" + n, pair_sums[n], by_source[n], chip_id)

    def adamw(names):
        for n in names:
            res = _adamw_halves("adamw_" + n, local(w[n], n), halves[n], sibling_halves[n], 0, local(mom[n], n),
                                local(var[n], n), core)
            outs[n] = tuple((jnp.swapaxes(r, 0, 1) if n in TRANSPOSED_WEIGHTS else r)[None] for r in res)

    do = lambda fn, names: functools.partial(fn, names)
    ffn2_grads = ["ffn2_w2", "ffn2_w1", "ffn2_w3"]
    xattn_grads = ["w_xo", "w_xq", "w_xk", "w_xv"]
    mix_out_grads = ["w_branch_gate", "w_out", "w_ret_o", "w_lru_o"]
    conv_gather = _gather_chips_task({"conv": shard["conv"]}, False, gw)
    half = lambda key, part: _gather_chips_task({key: shard[key]}, True, landed, part, 2)
    plan = _Plan()
    plan.tasks = {
        "ag_first_chips": [over_chips(["col1", "row2a"])],
        "ag_first_sibling": [to_sibling(["col1", "row2a"])],
        "ffn1_up": [half("win", 0)],
        "ffn1_down": [half("win", 1)],
        "ag_win_sibling": [to_sibling(["win"])],
        "mix_in": [over_chips(["wbg"]), conv_gather, half("sqA", 0)],
        "ret_fwd": [to_sibling(["wbg"]), half("sqA", 1), half("col2a", 0)],
        "lru_gates_fwd": [to_sibling(["sqA"]), half("col2a", 1)],
        "lru_scan_fwd": [to_sibling(["col2a"]), half("sqB", 0)],
        "lru_out": [half("sqB", 1)],
        "mix_gates": [to_sibling(["sqB"]), half("col2b", 0)],
        "y_ret": [half("sqC", 0)],
        "y_lru": [half("col2b", 1)],
        "mix_out": [half("sqC", 1), to_sibling(["col2b"])],
        "xq": [to_sibling(["sqC"])],
        "xattn_fwd": [half("row2b", 0)],
        "xattn_out": [half("row2b", 1)],
        "ffn2_up": [to_sibling(["row2b"])],
        "ffn2_dh": [pair_swap(ffn2_grads)],
        "xattn_bwd": [exchange(["ffn2_w2"], 0, 2)],
        "d_hq": [exchange(["ffn2_w2"], 1, 2)],
        "d_merged": [exchange(["ffn2_w1"], 0, 2), pair_swap(xattn_grads)],
        "lru_out_bwd": [exchange(["w_xo"])],
        "ret_bwd": [exchange(["ffn2_w1"], 1, 2), exchange(["ffn2_w3"], 0, 2), pair_swap(mix_out_grads)],
        "lru_scan_bwd": [exchange(["ffn2_w3"], 1, 2)],
        "lru_gates_bwd": [exchange(["w_xq", "w_xk"]), pair_gather(ffn2_grads)],
        "dw_in": [exchange(["w_xv", "w_out"])],
        "d_h2": [exchange(["w_branch_gate", "w_ret_o", "w_lru_o"]), pair_swap(["w_in"]), pair_gather(xattn_grads)],
        "ffn1_bwd_mid": [exchange(["w_in"], 0, 2), pair_gather(mix_out_grads)],
        "ffn1_dw2": [exchange(["w_in"], 2, 4)],
        "ffn1_dw1": [exchange(["w_in"], 3, 4), pair_swap(["ffn1_w2"])],
        "ffn1_dw3": [exchange(["ffn1_w2"], 0, 2), pair_swap(["ffn1_w1"]), pair_gather(["w_in"])],
        "ffn1_dh": [exchange(["ffn1_w2"], 1, 2), exchange(["ffn1_w1"]), pair_swap(["ffn1_w3"])],
        "small_allreduce": [exchange(["ffn1_w3"]), pair_gather(["ffn1_w2"])],
        "rs_last_gather": [pair_gather(["ffn1_w1", "ffn1_w3"])],
    }
    plan.after = {
        "ffn1_down": [functools.partial(_comm_call, "ag_win_sibling")],
        "ffn2_dh": [do(pair_sum, ffn2_grads)],
        "d_merged": [do(pair_sum, xattn_grads)],
        "ret_bwd": [do(pair_sum, mix_out_grads)],
        "lru_scan_bwd": [do(chip_sum, ffn2_grads)],
        "lru_gates_bwd": [do(adamw, ffn2_grads)],
        "dw_in": [do(chip_sum, xattn_grads)],
        "d_h2": [do(chip_sum, mix_out_grads), do(pair_sum, ["w_in"]), do(adamw, xattn_grads)],
        "ffn1_bwd_mid": [do(adamw, mix_out_grads)],
        "ffn1_dw1": [do(chip_sum, ["w_in"]), do(pair_sum, ["ffn1_w2"])],
        "ffn1_dw3": [do(pair_sum, ["ffn1_w1"]), do(adamw, ["w_in"])],
        "ffn1_dh": [do(pair_sum, ["ffn1_w3"]), do(chip_sum, ["ffn1_w2"])],
        "small_allreduce": [do(chip_sum, ["ffn1_w1", "ffn1_w3"]), functools.partial(_comm_call, "rs_last_gather"),
                    do(adamw, ["ffn1_w2", "ffn1_w1", "ffn1_w3"])],
    }
    global _plan
    _plan = plan
    try:
        _comm_call("ag_first_chips")
        _comm_call("ag_first_sibling")
        loss_part, grad_x, small = _local_step(x[0], mem[0], loss_target[0], gw, sm, big)
        gate2d = lambda a: a.reshape(LRU_BLOCKS * LRU_BLOCK, LRU_BLOCK)
        small_sum, *gate_sums = _small_allreduce([_pack_small(small)] + [gate2d(small[n]) for n in GATE_WEIGHTS])
    finally:
        _plan = None
    assert not plan.tasks and not plan.after, (list(plan.tasks), list(plan.after))
    loss = lax.psum(loss_part[0, 0], ("x", "y", "c"))

    small_shapes = {n: w[n].shape for n, _ in SMALL_LAYOUT}
    small_shapes["conv_w"] = (CONV_TAPS, D)
    conv_grad = lax.dynamic_slice(small_sum[13:13 + CONV_TAPS], (0, chip * SQ_BLK), (CONV_TAPS, SQ_BLK))
    small_w = {n: w[n] for n, _ in SMALL_LAYOUT}
    small_m = {n: mom[n] for n, _ in SMALL_LAYOUT}
    small_v = {n: var[n] for n, _ in SMALL_LAYOUT}
    pad_cols = lambda a: jnp.pad(a[0], ((0, 0), (0, D - SQ_BLK)))
    for dct in (small_w, small_m, small_v):
        dct["conv_w"] = pad_cols(dct["conv_w"])
    g_pack = lax.dynamic_update_slice(small_sum, jnp.pad(conv_grad, ((0, 0), (0, D - SQ_BLK))), (13, 0))
    d_pack, m_pack, v_pack = _adamw("adamw_small", _pack_small(small_w), g_pack, _pack_small(small_m),
                                    _pack_small(small_v))
    unpacked = [_unpack_small(p, small_shapes) for p in (g_pack, d_pack, m_pack, v_pack)]
    for n, _ in SMALL_LAYOUT:
        if n == "conv_w":
            outs[n] = tuple(u[n][:, :SQ_BLK][None] for u in unpacked)
        else:
            outs[n] = tuple(u[n] for u in unpacked)
    for n, gsum in zip(GATE_WEIGHTS, gate_sums):
        d, nm, nv = _adamw("adamw_" + n, gate2d(w[n]), gsum, gate2d(mom[n]), gate2d(var[n]))
        outs[n] = tuple(r.reshape(w[n].shape) for r in (gsum, d, nm, nv))

    result = [loss, grad_x[None]]
    for k in range(4):
        result += [outs[n][k] for n in WEIGHT_ORDER]
    return tuple(result)
```

```python
import functools
import math

import numpy as np
import jax
import jax.numpy as jnp
from jax import lax
from jax.experimental import pallas as pl
from jax.experimental.pallas import tpu as pltpu

F32 = jnp.float32
BF16 = jnp.bfloat16
GRAD_WIRE_DTYPE = BF16
MESH = pl.DeviceIdType.MESH

D = 1024
EPS = 1e-6
RET_HEADS = 4
RET_DK = 128
RET_DV = 256
CHUNK = 128
ROPE_BASE = 10000.0
LRU_BLOCKS = 8
LRU_BLOCK = 128
CONV_TAPS = 4
LRU_C = 8.0
D_FF = 2816
X_HEADS = 4
X_HD = 256
N_CHIPS = 4
FF_BLK = D_FF // N_CHIPS
IN_BLK = 5120 // N_CHIPS
BG_BLK = 2048 // N_CHIPS
SQ_BLK = D // N_CHIPS

ADAM_LR = 0.001
ADAM_B1 = 0.9
ADAM_B2 = 0.999
ADAM_EPS = 1e-08
ADAM_WD = 0.01
ADAM_STEP = 10

VMEM_LIMIT_BYTES = 56 * 1024 * 1024
ROW_TILE = 512
WIDE_ROW_TILE = 1024
FFN_ROW_TILE = 256
DW_BLK = D_FF // 2
SCAN_TILE = 256

_DN = {
    "nn": (((1,), (0,)), ((), ())),
    "nt": (((1,), (1,)), ((), ())),
    "tn": (((0,), (0,)), ((), ())),
}


def _cparams(n_axes, collective_id=None):
    return pltpu.CompilerParams(dimension_semantics=("arbitrary",) * n_axes,
                                vmem_limit_bytes=VMEM_LIMIT_BYTES, collective_id=collective_id)


def _dot(a, b, kind):
    if b.ndim == 3:
        b = b.reshape(b.shape[0] * b.shape[1], b.shape[2])
    return lax.dot_general(a.astype(BF16), b.astype(BF16), _DN[kind], preferred_element_type=F32)


def _sigmoid(x):
    return 1.0 / (1.0 + jnp.exp(-x))


def _log1p_pos(e):
    u = 1.0 + e
    return jnp.where(u == 1.0, e, jnp.log(u) * (e / jnp.where(u == 1.0, 1.0, u - 1.0)))


def _expm1(x):
    u = jnp.exp(x)
    lu = jnp.log(u)
    safe = jnp.where(lu == 0.0, 1.0, lu)
    return jnp.where(u == 1.0, x, (u - 1.0) * (x / safe))


def _softplus(z):
    return jnp.maximum(z, 0.0) + _log1p_pos(jnp.exp(-jnp.abs(z)))


_GELU_C = math.sqrt(2.0 / math.pi)


def _gelu_and_grad(x):
    x2 = x * x
    t = jnp.tanh(_GELU_C * (x + 0.044715 * x * x2))
    g = 0.5 * x * (1.0 + t)
    dg = 0.5 * (1.0 + t) + 0.5 * x * (1.0 - t * t) * (_GELU_C * (1.0 + 3.0 * 0.044715 * x2))
    return g, dg


def _rms_fwd(x, g):
    r = lax.rsqrt(jnp.mean(x * x, axis=-1, keepdims=True) + EPS)
    return (x * r) * g


def _rms_bwd(x, g, dh):
    r = lax.rsqrt(jnp.mean(x * x, axis=-1, keepdims=True) + EPS)
    n = x * r
    dyg = dh * g
    dx = r * (dyg - n * jnp.mean(dyg * n, axis=-1, keepdims=True))
    return dx, jnp.sum(dh * n, axis=0, keepdims=True)


def _accumulate(ref, val, first):
    @pl.when(first)
    def _():
        ref[...] = val

    @pl.when(jnp.logical_not(first))
    def _():
        ref[...] += val


def _sds(shape, dtype):
    return jax.ShapeDtypeStruct(tuple(shape), dtype)


def _spec(shape, fn):
    return pl.BlockSpec(tuple(shape), fn)


class _Task:
    def __init__(self, peers, operands, out_shapes, aliases, nsem, make, finish):
        self.peers = peers
        self.operands, self.out_shapes, self.aliases = operands, out_shapes, aliases
        self.nsem, self.make, self.finish = nsem, make, finish


class _Plan:
    def __init__(self):
        self.tasks, self.after = {}, {}


_plan = None


PEER_SET_COLLECTIVE_ID = {frozenset({"sibling"}): 1, frozenset({"chips"}): 2, frozenset({"sibling", "chips"}): 3}


def _entry_handshake(peer_set):
    x, y, c, chips = _mesh_position()
    peers = ([(x, y, 1 - c)] if "sibling" in peer_set else []) + ([(*chip, c) for chip in chips]
                                                                   if "chips" in peer_set else [])
    barrier = pltpu.get_barrier_semaphore()
    for peer in peers:
        pl.semaphore_signal(barrier, inc=1, device_id=peer, device_id_type=MESH)
    pl.semaphore_wait(barrier, len(peers))


def _pcall(body, *, name, grid, in_specs, out_specs, out_shape, scratch_shapes=(), num_prefetch=0, own_peers=()):
    single = not isinstance(out_shape, (list, tuple))
    out_shape = [out_shape] if single else list(out_shape)
    out_specs = [out_specs] if single else list(out_specs)
    in_specs = list(in_specs)
    scratch_shapes = list(scratch_shapes)
    tasks = _plan.tasks.pop(name, []) if _plan is not None else []
    after = _plan.after.pop(name, []) if _plan is not None else []
    peer_set = frozenset(t.peers for t in tasks) | frozenset(own_peers)
    nax = len(grid)

    def run(*operands):
        n_in = len(operands) - num_prefetch
        n_out = len(out_shape)
        t_ops = [t.operands() for t in tasks]
        t_outs = [t.out_shapes() for t in tasks]
        c_ops = [a for ops in t_ops for a in ops]
        c_outs = [s for outs in t_outs for s in outs]
        aliases = {}
        i0, o0 = num_prefetch + n_in, n_out
        for t, ops, outs in zip(tasks, t_ops, t_outs):
            for i_loc, o_loc in t.aliases.items():
                aliases[i0 + i_loc] = o0 + o_loc
            i0 += len(ops)
            o0 += len(outs)
        nsem = sum(t.nsem for t in tasks)

        def wrapped(*refs):
            p = num_prefetch
            pre, ins = refs[:p], refs[p:p + n_in]
            cins = refs[p + n_in:p + n_in + len(c_ops)]
            q = p + n_in + len(c_ops)
            outs, couts = refs[q:q + n_out], refs[q + n_out:q + n_out + len(c_outs)]
            q += n_out + len(c_outs)
            scr = refs[q:q + len(scratch_shapes)]

            def descriptors():
                send_sems, recv_sems = refs[q + len(scratch_shapes):]
                starts, arrivals = [], []
                ci = co = so = 0
                for t, ops, souts in zip(tasks, t_ops, t_outs):
                    s, a = t.make(cins[ci:ci + len(ops)], couts[co:co + len(souts)],
                                  functools.partial(lambda base, k: send_sems.at[base + k], so),
                                  functools.partial(lambda base, k: recv_sems.at[base + k], so))
                    starts += s
                    arrivals += a
                    ci, co, so = ci + len(ops), co + len(souts), so + t.nsem
                return starts, arrivals

            if peer_set:
                ids = [pl.program_id(k) for k in range(nax)]
                first = functools.reduce(jnp.logical_and, [i == 0 for i in ids])
                last = functools.reduce(jnp.logical_and, [i == g - 1 for i, g in zip(ids, grid)])

                @pl.when(first)
                def _():
                    _entry_handshake(peer_set)
                    if tasks:
                        for cp in descriptors()[0]:
                            cp.start()

            body(*pre, *ins, *outs, *scr)

            if tasks:
                @pl.when(last)
                def _():
                    starts, arrivals = descriptors()
                    for arrival in arrivals:
                        arrival().wait_recv()
                    for cp in starts:
                        cp.wait_send()

        sems = [pltpu.SemaphoreType.DMA((nsem,)), pltpu.SemaphoreType.DMA((nsem,))] if tasks else []
        res = pl.pallas_call(
            wrapped, name=name,
            grid_spec=pltpu.PrefetchScalarGridSpec(
                num_scalar_prefetch=num_prefetch, grid=tuple(grid),
                in_specs=in_specs + [ANY_SPEC] * len(c_ops),
                out_specs=out_specs + [ANY_SPEC] * len(c_outs),
                scratch_shapes=scratch_shapes + sems),
            out_shape=out_shape + c_outs,
            input_output_aliases=aliases,
            compiler_params=_cparams(nax, PEER_SET_COLLECTIVE_ID[peer_set] if peer_set else None),
        )(*operands, *c_ops)
        co = n_out
        for t, souts in zip(tasks, t_outs):
            t.finish(res[co:co + len(souts)])
            co += len(souts)
        for fn in after:
            fn()
        return res[0] if single else list(res[:n_out])

    return run


def _comm_call(name):
    def body(o_ref):
        o_ref[...] = jnp.zeros_like(o_ref)

    _pcall(body, name=name, grid=(1,), in_specs=[], out_specs=_spec((8, 128), lambda i: (0, 0)),
           out_shape=_sds((8, 128), F32))()


def _gemm(name, terms, grid, outs, acc_shape, extras=(), epilogue=None):
    kinds = [t[4] for t in terms]
    nt, ne, no = len(terms), len(extras), len(outs)
    nred = grid[-1]
    nax = len(grid)

    def body(*refs):
        trefs = refs[:2 * nt]
        erefs = refs[2 * nt:2 * nt + ne]
        orefs = refs[2 * nt + ne:2 * nt + ne + no]
        ids = [pl.program_id(k) for k in range(nax)]
        tot = None
        for t in range(nt):
            d = _dot(trefs[2 * t][...], trefs[2 * t + 1][...], kinds[t])
            tot = d if tot is None else tot + d

        def finish(acc):
            if epilogue is None:
                orefs[0][...] = acc.astype(orefs[0].dtype)
            else:
                epilogue(acc, erefs, orefs, ids)

        if nred == 1:
            finish(tot)
        else:
            acc_ref = refs[-1]
            r = ids[-1]

            @pl.when(r == 0)
            def _():
                acc_ref[...] = tot

            @pl.when(r > 0)
            def _():
                acc_ref[...] += tot

            @pl.when(r == nred - 1)
            def _():
                finish(acc_ref[...])

    operands, in_specs = [], []
    for a, a_spec, b, b_spec, _ in terms:
        operands += [a, b]
        in_specs += [a_spec, b_spec]
    for e, e_spec in extras:
        operands.append(e)
        in_specs.append(e_spec)
    scratch = [pltpu.VMEM(tuple(acc_shape), F32)] if nred > 1 else []
    return _pcall(body, name=name, grid=tuple(grid), in_specs=in_specs, out_specs=[o[1] for o in outs],
                  out_shape=[o[0] for o in outs], scratch_shapes=scratch)(*operands)


def _rowwise(name, fn, ins, outs, grid):
    ni = len(ins)
    nax = len(grid)

    def body(*refs):
        ids = [pl.program_id(k) for k in range(nax)]
        fn(refs[:ni], refs[ni:], ids)

    return _pcall(body, name=name, grid=tuple(grid), in_specs=[i[1] for i in ins],
                  out_specs=[o[1] for o in outs], out_shape=[o[0] for o in outs])(*[i[0] for i in ins])


def _ffn_up(name, h, w1buf, w1_idx, w3buf, w3_idx):
    T = h.shape[0]
    tm = min(FFN_ROW_TILE, T)

    def body(h_ref, w1_ref, w3_ref, a_ref, b_ref, s_ref):
        hv = h_ref[...]
        a = _dot(hv, w1_ref[...], "nt")
        b = _dot(hv, w3_ref[...], "nt")
        a_ref[...] = a.astype(BF16)
        b_ref[...] = b.astype(BF16)
        s_ref[...] = ((a * _sigmoid(a)) * b).astype(BF16)

    blk = _spec((tm, D_FF), lambda i: (i, 0))
    return _pcall(
        body, name=name, grid=(T // tm,),
        in_specs=[_spec((tm, D), lambda i: (i, 0)),
                  _spec((N_CHIPS, None, FF_BLK, D), lambda i: (0, w1_idx, 0, 0)),
                  _spec((N_CHIPS, None, FF_BLK, D), lambda i: (0, w3_idx, 0, 0))],
        out_specs=[blk, blk, blk],
        out_shape=[_sds((T, D_FF), BF16)] * 3,
    )(h, w1buf, w3buf)


def _ffn_down(name, s, wrow2, w2_idx, x_res, g_next=None):
    T = x_res.shape[0]
    tm = min(ROW_TILE, T)
    row = lambda i, j, r: (i, 0)

    def epilogue(acc, erefs, orefs, ids):
        xo = erefs[0][...] + 0.5 * acc
        orefs[0][...] = xo
        if g_next is not None:
            orefs[1][...] = _rms_fwd(xo, erefs[1][...]).astype(BF16)

    extras = [(x_res, _spec((tm, D), row))]
    outs = [(_sds((T, D), F32), _spec((tm, D), row))]
    if g_next is not None:
        extras.append((g_next, _spec((1, D), lambda i, j, r: (0, 0))))
        outs.append((_sds((T, D), BF16), _spec((tm, D), row)))
    return _gemm(
        name,
        [(s, _spec((tm, D_FF), row),
          wrow2, _spec((N_CHIPS, None, FF_BLK, D), lambda i, j, r: (0, w2_idx, 0, 0)), "nn")],
        (T // tm, 1, 1), outs, (tm, D), extras, epilogue)


def _ffn_bwd_mid(name, dx, wrow2, w2_idx, a, b):
    T = dx.shape[0]
    tm = min(FFN_ROW_TILE, T)

    def body(dx_ref, w2_ref, a_ref, b_ref, dab_ref):
        ds = _dot(0.5 * dx_ref[...], w2_ref[...], "nt")
        av = a_ref[...].astype(F32)
        sg = _sigmoid(av)
        dab_ref[0] = (ds * b_ref[...].astype(F32) * (sg * (1.0 + av * (1.0 - sg)))).astype(BF16)
        dab_ref[1] = (ds * (av * sg)).astype(BF16)

    blk = _spec((tm, D_FF), lambda i: (i, 0))
    return _pcall(
        body, name=name, grid=(T // tm,),
        in_specs=[_spec((tm, D), lambda i: (i, 0)),
                  _spec((N_CHIPS, None, FF_BLK, D), lambda i: (0, w2_idx, 0, 0)),
                  blk, blk],
        out_specs=_spec((2, tm, D_FF), lambda i: (0, i, 0)),
        out_shape=_sds((2, T, D_FF), BF16),
    )(dx, wrow2, a, b)


def _rms_bwd_epilogue(acc, erefs, orefs, ids):
    dx, dgp = _rms_bwd(erefs[0][...], erefs[1][...], acc)
    orefs[0][...] = dx + erefs[2][...]
    _accumulate(orefs[1], dgp, ids[0] == 0)


def _rms_bwd_io(x, g, dres, T, tm):
    row = lambda i, j, r: (i, 0)
    vec = lambda i, j, r: (0, 0)
    extras = [(x, _spec((tm, D), row)), (g, _spec((1, D), vec)), (dres, _spec((tm, D), row))]
    outs = [(_sds((T, D), F32), _spec((tm, D), row)), (_sds((1, D), F32), _spec((1, D), vec))]
    return extras, outs


def _ffn_bwd(tag, dx_out, h, a, b, s, w1buf, w1_idx, w3buf, w3_idx, wrow2, w2_idx, x_in, g, big):
    T = dx_out.shape[0]
    dab = _ffn_bwd_mid(tag + "_bwd_mid", dx_out, wrow2, w2_idx, a, b)

    def half_scale(acc, erefs, orefs, ids):
        orefs[0][...] = (0.5 * acc).astype(orefs[0].dtype)

    dw_grid = (D_FF // DW_BLK, 1, 1)
    dw_out = [(_sds((D_FF, D), GRAD_WIRE_DTYPE), _spec((DW_BLK, D), lambda j, n, r: (j, 0)))]
    tokens = _spec((T, D), lambda j, n, r: (0, 0))
    big[tag + "_w2"] = _gemm(
        tag + "_dw2", [(s, _spec((T, DW_BLK), lambda j, n, r: (0, j)), dx_out, tokens, "tn")],
        dw_grid, dw_out, (DW_BLK, D), (), half_scale)[0].reshape(1, N_CHIPS, FF_BLK, D)
    for widx, wname in ((0, "_w1"), (1, "_w3")):
        big[tag + wname] = _gemm(
            tag + "_d" + wname[1:],
            [(dab, _spec((None, T, DW_BLK), functools.partial(lambda w, j, n, r: (w, 0, j), widx)), h, tokens, "tn")],
            dw_grid, dw_out, (DW_BLK, D))[0].reshape(1, N_CHIPS, FF_BLK, D)
    tm = min(FFN_ROW_TILE, T)
    extras, outs = _rms_bwd_io(x_in, g, dx_out, T, tm)
    whole = lambda idx: _spec((N_CHIPS, None, FF_BLK, D), lambda i, j, r: (0, idx, 0, 0))
    dx_in, dg = _gemm(
        tag + "_dh",
        [(dab, _spec((None, tm, D_FF), lambda i, j, r: (0, i, 0)), w1buf, whole(w1_idx), "nn"),
         (dab, _spec((None, tm, D_FF), lambda i, j, r: (1, i, 0)), w3buf, whole(w3_idx), "nn")],
        (T // tm, 1, 1), outs, (tm, D), extras, _rms_bwd_epilogue)
    return dx_in, dg


def _proj_sq(name, a, wsq, idx, kind, out_dtype=F32, extras=(), epilogue=None, outs=None):
    M = a.shape[0]
    tm = min(ROW_TILE, M)
    if outs is None:
        outs = [(_sds((M, D), out_dtype), _spec((tm, D), lambda i, j, r: (i, 0)))]
    return _gemm(
        name,
        [(a, _spec((tm, D), lambda i, j, r: (i, 0)),
          wsq, _spec((N_CHIPS, None, SQ_BLK, D), lambda i, j, r: (0, idx, 0, 0)), kind)],
        (M // tm, 1, 1), outs, (tm, D), extras, epilogue)


def _dw_sq(name, a, b):
    M = a.shape[0]
    tk = M
    whole = _gemm(
        name,
        [(a, _spec((tk, D), lambda i, j, r: (r, 0)), b, _spec((tk, D), lambda i, j, r: (r, 0)), "tn")],
        (1, 1, M // tk),
        [(_sds((D, D), GRAD_WIRE_DTYPE), _spec((D, D), lambda i, j, r: (0, 0)))],
        (D, D))[0]
    return whole.reshape(N_CHIPS, SQ_BLK, D)


def _retention_constants(T):
    pos = jnp.arange(T, dtype=F32)
    inv_freq = ROPE_BASE ** (-jnp.arange(0, RET_DK, 2, dtype=F32) / RET_DK)
    ang = pos[:, None] * inv_freq[None, :]
    cosf = jnp.concatenate([jnp.cos(ang), jnp.cos(ang)], axis=1)
    sins = jnp.concatenate([-jnp.sin(ang), jnp.sin(ang)], axis=1)
    lg = jnp.log(1.0 - 2.0 ** (-5.0 - jnp.arange(RET_HEADS, dtype=F32)))
    p = jnp.arange(CHUNK, dtype=F32)
    rel = p[:, None] - p[None, :]
    dmat = jnp.where(rel[None] >= 0, jnp.exp(rel[None] * lg[:, None, None]), 0.0)
    kd = jnp.exp((CHUNK - 1.0 - p)[None, :] * lg[:, None])[:, :, None]
    qd = jnp.exp((p + 1.0)[None, :] * lg[:, None])[:, :, None]
    cd = jnp.exp(CHUNK * lg)[:, None, None]
    return cosf, sins, dmat, kd, qd, cd


def _rot(t, cosv, sinv):
    return t * cosv + pltpu.roll(t, RET_DK // 2, 1) * sinv


def _unrot(t, cosv, sinv):
    return t * cosv - pltpu.roll(t, RET_DK // 2, 1) * sinv


def _ret_const_specs(cm):
    whole = lambda shape: _spec(shape, lambda c: (0,) * len(shape))
    return [
        _spec((CHUNK, RET_DK), lambda c: (cm(c), 0)),
        _spec((CHUNK, RET_DK), lambda c: (cm(c), 0)),
        whole((RET_HEADS, CHUNK, CHUNK)), whole((RET_HEADS, CHUNK, 1)), whole((RET_HEADS, CHUNK, 1)),
        whole((RET_HEADS, 1, 1)),
    ]


def _head(h, width):
    return slice(h * width, (h + 1) * width)


def _ret_fwd(u, consts, ret_gn):
    T = u.shape[0]
    nC = T // CHUNK
    kscale = RET_DK ** -0.5

    def body(q_ref, k_ref, v_ref, g_ref, cos_ref, sin_ref, dm_ref, kd_ref, qd_ref, cd_ref, gn_ref,
             qr_ref, kr_ref, ret_ref, yr_ref, st_ref, state):
        @pl.when(pl.program_id(0) == 0)
        def _():
            state[...] = jnp.zeros_like(state)

        cosv, sinv = cos_ref[...], sin_ref[...]
        for h in range(RET_HEADS):
            hk, hv = _head(h, RET_DK), _head(h, RET_DV)
            q = _rot(q_ref[:, hk], cosv, sinv)
            k = _rot(k_ref[:, hk], cosv, sinv) * kscale
            v = v_ref[:, hv]
            qr_ref[:, hk] = q
            kr_ref[:, hk] = k
            prev = state[h]
            st_ref[h] = prev
            s = _dot(q, k, "nt") * dm_ref[h]
            ret = _dot(s, v, "nn") + _dot(q, prev, "nn") * qd_ref[h]
            state[h] = cd_ref[h] * prev + _dot(k * kd_ref[h], v, "tn")
            ret_ref[:, hv] = ret
            mu = jnp.mean(ret, axis=-1, keepdims=True)
            xc = ret - mu
            yn = xc * lax.rsqrt(jnp.mean(xc * xc, axis=-1, keepdims=True) + EPS)
            g = g_ref[:, hv]
            yr_ref[:, hv] = ((g * _sigmoid(g)) * (yn * gn_ref[:, hv])).astype(BF16)

    cm = lambda c: c
    qk_w, v_w = RET_HEADS * RET_DK, RET_HEADS * RET_DV
    in_specs = [
        _spec((CHUNK, qk_w), lambda c: (c, 0)), _spec((CHUNK, qk_w), lambda c: (c, 1)),
        _spec((CHUNK, v_w), lambda c: (c, 1)), _spec((CHUNK, v_w), lambda c: (c, 2)),
    ] + _ret_const_specs(cm) + [_spec((1, v_w), lambda c: (0, 0))]
    qk_out = _spec((CHUNK, qk_w), lambda c: (c, 0))
    v_out = _spec((CHUNK, v_w), lambda c: (c, 0))
    return _pcall(
        body, name="ret_fwd", grid=(nC,),
        in_specs=in_specs,
        out_specs=[qk_out, qk_out, v_out, v_out,
                   _spec((RET_HEADS, None, RET_DK, RET_DV), lambda c: (0, c, 0, 0))],
        out_shape=[_sds((T, qk_w), F32), _sds((T, qk_w), F32), _sds((T, v_w), F32), _sds((T, v_w), BF16),
                   _sds((RET_HEADS, nC, RET_DK, RET_DV), F32)],
        scratch_shapes=[pltpu.VMEM((RET_HEADS, RET_DK, RET_DV), F32)],
    )(u, u, u, u, *consts, ret_gn)


def _ret_bwd(dyr, ret, u, qr, kr, states, consts, ret_gn):
    T = u.shape[0]
    nC = T // CHUNK
    kscale = RET_DK ** -0.5

    def body(dyr_ref, ret_ref, g_ref, q_ref, k_ref, v_ref, st_ref,
             cos_ref, sin_ref, dm_ref, kd_ref, qd_ref, cd_ref, gn_ref,
             dq_ref, dk_ref, dv_ref, dg_ref, dgn_ref, gstate):
        first = pl.program_id(0) == 0

        @pl.when(first)
        def _():
            gstate[...] = jnp.zeros_like(gstate)

        cosv, sinv = cos_ref[...], sin_ref[...]
        dgn_parts = []
        for h in range(RET_HEADS):
            hk, hv = _head(h, RET_DK), _head(h, RET_DV)
            ret = ret_ref[:, hv]
            mu = jnp.mean(ret, axis=-1, keepdims=True)
            xc = ret - mu
            rs = lax.rsqrt(jnp.mean(xc * xc, axis=-1, keepdims=True) + EPS)
            yn = xc * rs
            gn = gn_ref[:, hv]
            g = g_ref[:, hv]
            sg = _sigmoid(g)
            dyr_v = dyr_ref[:, hv]
            dretn = dyr_v * (g * sg)
            dg_ref[:, hv] = (dyr_v * (yn * gn) * (sg * (1.0 + g * (1.0 - sg)))).astype(BF16)
            dgn_parts.append(jnp.sum(dretn * yn, axis=0, keepdims=True))
            dyn = dretn * gn
            d_o = rs * (dyn - jnp.mean(dyn, axis=-1, keepdims=True)
                        - yn * jnp.mean(dyn * yn, axis=-1, keepdims=True))

            q, k, v = q_ref[:, hk], k_ref[:, hk], v_ref[:, hv]
            dmat, kd, qd = dm_ref[h], kd_ref[h], qd_ref[h]
            prev = st_ref[h]
            gnext = gstate[h]
            s = _dot(q, k, "nt") * dmat
            ds = _dot(d_o, v, "nt") * dmat
            doq = d_o * qd
            dq = _dot(ds, k, "nn") + _dot(doq, prev, "nt")
            dk = _dot(ds, q, "tn") + _dot(v, gnext, "nt") * kd
            dv = _dot(s, d_o, "tn") + _dot(k * kd, gnext, "nn")
            gstate[h] = cd_ref[h] * gnext + _dot(q, doq, "tn")
            dq_ref[:, hk] = _unrot(dq, cosv, sinv).astype(BF16)
            dk_ref[:, hk] = _unrot(dk * kscale, cosv, sinv).astype(BF16)
            dv_ref[:, hv] = dv.astype(BF16)
        _accumulate(dgn_ref, jnp.concatenate(dgn_parts, axis=1), first)

    cm = lambda c: nC - 1 - c
    qk_w, v_w = RET_HEADS * RET_DK, RET_HEADS * RET_DV
    vspec = lambda blk: _spec((CHUNK, v_w), lambda c: (cm(c), blk))
    qspec = _spec((CHUNK, qk_w), lambda c: (cm(c), 0))
    in_specs = [vspec(0), vspec(0), vspec(2), qspec, qspec, vspec(1),
                _spec((RET_HEADS, None, RET_DK, RET_DV), lambda c: (0, cm(c), 0, 0)),
                ] + _ret_const_specs(cm) + [_spec((1, v_w), lambda c: (0, 0))]
    return _pcall(
        body, name="ret_bwd", grid=(nC,),
        in_specs=in_specs,
        out_specs=[qspec, qspec, vspec(0), vspec(0), _spec((1, v_w), lambda c: (0, 0))],
        out_shape=[_sds((T, qk_w), BF16), _sds((T, qk_w), BF16), _sds((T, v_w), BF16), _sds((T, v_w), BF16),
                   _sds((1, v_w), F32)],
        scratch_shapes=[pltpu.VMEM((RET_HEADS, RET_DK, RET_DV), F32)],
    )(dyr, ret, u, qr, kr, u, states, *consts, ret_gn)


def _shift_down(x, s):
    rows = lax.broadcasted_iota(jnp.int32, x.shape, 0)
    return jnp.where(rows >= s, pltpu.roll(x, s, 0), 0.0)


def _shift_up(x, s):
    n = x.shape[0]
    rows = lax.broadcasted_iota(jnp.int32, x.shape, 0)
    return jnp.where(rows < n - s, pltpu.roll(x, n - s, 0), 0.0)


def _lru_specs(T):
    col = lambda off: _spec((T, LRU_BLOCK), lambda g: (0, off + g))
    vec = _spec((1, LRU_BLOCK), lambda g: (0, g))
    wblk = _spec((None, LRU_BLOCK, LRU_BLOCK), lambda g: (g, 0, 0))
    cw = _spec((CONV_TAPS, LRU_BLOCK), lambda g: (0, g))
    return col, vec, wblk, cw


def _lru_gates_fwd(u, conv_w, conv_b, w_r, b_r, w_i, b_i, lam):
    T = u.shape[0]
    col, vec, wblk, cw = _lru_specs(T)

    def body(x_ref, cw_ref, cb_ref, wr_ref, br_ref, wi_ref, bi_ref, lam_ref,
             xc_ref, r_ref, i_ref, a_ref, bx_ref):
        x = x_ref[...]
        w = cw_ref[...]
        xc = (_shift_down(x, 3) * w[0:1] + _shift_down(x, 2) * w[1:2] + _shift_down(x, 1) * w[2:3]
              + x * w[3:4] + cb_ref[...])
        r = _sigmoid(_dot(xc, wr_ref[...], "nn") + br_ref[...])
        i = _sigmoid(_dot(xc, wi_ref[...], "nn") + bi_ref[...])
        la = (-LRU_C) * r * _softplus(-lam_ref[...])
        xc_ref[...] = xc
        r_ref[...] = r
        i_ref[...] = i
        a_ref[...] = jnp.exp(la)
        bx_ref[...] = jnp.sqrt(-_expm1(2.0 * la)) * (i * xc)

    out = col(0)
    return _pcall(
        body, name="lru_gates_fwd", grid=(LRU_BLOCKS,),
        in_specs=[col(24), cw, vec, wblk, vec, wblk, vec, vec],
        out_specs=[out] * 5,
        out_shape=[_sds((T, D), F32)] * 5,
    )(u, conv_w, conv_b, w_r, b_r, w_i, b_i, lam)


def _lru_scan(name, a3, b3, reverse):
    T = a3.shape[0]
    nt = T // SCAN_TILE
    unroll = 8

    def body(a_ref, b_ref, o_ref, carry):
        @pl.when(pl.program_id(0) == 0)
        def _():
            carry[...] = jnp.zeros_like(carry)

        if not reverse:
            def step(t, h):
                h = a_ref[t] * h + b_ref[t]
                o_ref[t] = h
                return h
        else:
            def step(k, c):
                t = SCAN_TILE - 1 - k
                l = b_ref[t] + c
                o_ref[t] = l
                return a_ref[t] * l
        carry[...] = lax.fori_loop(0, SCAN_TILE, step, carry[...], unroll=unroll)

    idx = (lambda i: (nt - 1 - i, 0, 0)) if reverse else (lambda i: (i, 0, 0))
    blk = _spec((SCAN_TILE, LRU_BLOCKS, LRU_BLOCK), idx)
    return _pcall(
        body, name=name, grid=(nt,),
        in_specs=[blk, blk], out_specs=blk,
        out_shape=_sds((T, LRU_BLOCKS, LRU_BLOCK), F32),
        scratch_shapes=[pltpu.VMEM((LRU_BLOCKS, LRU_BLOCK), F32)],
    )(a3, b3)


def _lru_gates_bwd(lmb, hl, a, r, i, xc, u, conv_w, w_r, w_i, lam):
    T = u.shape[0]
    col, vec, wblk, cw = _lru_specs(T)

    def body(l_ref, h_ref, a_ref, r_ref, i_ref, xc_ref, x_ref, cw_ref, wr_ref, wi_ref, lam_ref,
             dx_ref, dwr_ref, dwi_ref, dvec_ref, dcw_ref):
        l = l_ref[...]
        av, rv, iv, xc = a_ref[...], r_ref[...], i_ref[...], xc_ref[...]
        lam_v = lam_ref[...]
        sp = _softplus(-lam_v)
        la = (-LRU_C) * rv * sp
        mult = jnp.sqrt(-_expm1(2.0 * la))
        da = l * _shift_down(h_ref[...], 1)
        dmult = l * (iv * xc)
        di = l * mult * xc
        dxc = l * mult * iv
        dla = da * av - dmult * (av * av) / mult
        dzr = (dla * ((-LRU_C) * sp)) * rv * (1.0 - rv)
        dzi = di * iv * (1.0 - iv)
        dsp = jnp.sum(dla * ((-LRU_C) * rv), axis=0, keepdims=True)
        dlam = dsp * (-_sigmoid(-lam_v))
        dwr_ref[...] = _dot(xc, dzr, "tn")
        dwi_ref[...] = _dot(xc, dzi, "tn")
        dxc = dxc + _dot(dzr, wr_ref[...], "nt") + _dot(dzi, wi_ref[...], "nt")
        x = x_ref[...]
        w = cw_ref[...]
        dx = (dxc * w[3:4] + _shift_up(dxc, 1) * w[2:3] + _shift_up(dxc, 2) * w[1:2]
              + _shift_up(dxc, 3) * w[0:1])
        dx_ref[...] = dx.astype(BF16)
        dvec_ref[...] = jnp.concatenate(
            [jnp.sum(dzr, axis=0, keepdims=True), jnp.sum(dzi, axis=0, keepdims=True), dlam,
             jnp.sum(dxc, axis=0, keepdims=True)], axis=0)
        dcw_ref[...] = jnp.concatenate(
            [jnp.sum(dxc * _shift_down(x, 3 - tap), axis=0, keepdims=True) if tap < 3
             else jnp.sum(dxc * x, axis=0, keepdims=True) for tap in range(CONV_TAPS)], axis=0)

    c0 = col(0)
    return _pcall(
        body, name="lru_gates_bwd", grid=(LRU_BLOCKS,),
        in_specs=[c0, c0, c0, c0, c0, c0, col(24), cw, wblk, wblk, vec],
        out_specs=[c0, wblk, wblk, cw, cw],
        out_shape=[_sds((T, D), BF16), _sds((LRU_BLOCKS, LRU_BLOCK, LRU_BLOCK), F32),
                   _sds((LRU_BLOCKS, LRU_BLOCK, LRU_BLOCK), F32), _sds((4, D), F32), _sds((CONV_TAPS, D), F32)],
    )(lmb, hl, a, r, i, xc, u, conv_w, w_r, w_i, lam)


def _xattn_probs(q, k):
    sc = _dot(q, k, "nt") * (X_HD ** -0.5)
    e = jnp.exp(sc - jnp.max(sc, axis=-1, keepdims=True))
    return e / jnp.sum(e, axis=-1, keepdims=True)


def _xattn_fwd(xq, xk, xv):
    T = xq.shape[0]
    tq = ROW_TILE
    M = xk.shape[0]

    def body(q_ref, k_ref, v_ref, o_ref):
        p = _xattn_probs(q_ref[...], k_ref[...])
        o_ref[...] = _dot(p, v_ref[...], "nn").astype(BF16)

    qs = _spec((tq, X_HD), lambda h, i: (i, h))
    kv = _spec((M, X_HD), lambda h, i: (0, h))
    return _pcall(
        body, name="xattn_fwd", grid=(X_HEADS, T // tq),
        in_specs=[qs, kv, kv], out_specs=qs, out_shape=_sds((T, D), BF16),
    )(xq, xk, xv)


def _xattn_bwd(xq, xk, xv, dxo):
    T = xq.shape[0]
    tq = ROW_TILE
    M = xk.shape[0]

    def body(q_ref, k_ref, v_ref, do_ref, dq_ref, dk_ref, dv_ref):
        first = pl.program_id(1) == 0
        q, k, v, do = q_ref[...], k_ref[...], v_ref[...], do_ref[...]
        p = _xattn_probs(q, k)
        dp = _dot(do, v, "nt")
        ds = p * (dp - jnp.sum(dp * p, axis=-1, keepdims=True)) * (X_HD ** -0.5)
        dq_ref[...] = _dot(ds, k, "nn").astype(BF16)
        _accumulate(dk_ref, _dot(ds, q, "tn"), first)
        _accumulate(dv_ref, _dot(p, do, "tn"), first)

    qs = _spec((tq, X_HD), lambda h, i: (i, h))
    kv = _spec((M, X_HD), lambda h, i: (0, h))
    return _pcall(
        body, name="xattn_bwd", grid=(X_HEADS, T // tq),
        in_specs=[qs, kv, kv, qs], out_specs=[qs, kv, kv],
        out_shape=[_sds((T, D), BF16), _sds((M, D), F32), _sds((M, D), F32)],
    )(xq, xk, xv, dxo)


def _final_loss(x, g, tgt):
    T = x.shape[0]
    tm = ROW_TILE

    def fn(irefs, orefs, ids):
        xv, gv = irefs[0][...], irefs[1][...]
        err = _rms_fwd(xv, gv) - irefs[2][...]
        lp = 0.5 * jnp.sum(jnp.mean(err * err, axis=-1, keepdims=True), axis=0, keepdims=True)
        first = ids[0] == 0
        _accumulate(orefs[0], jnp.broadcast_to(lp, (1, 128)), first)
        dx, dgp = _rms_bwd(xv, gv, err * (1.0 / D))
        orefs[1][...] = dx
        _accumulate(orefs[2], dgp, first)

    row = _spec((tm, D), lambda i: (i, 0))
    vec = _spec((1, D), lambda i: (0, 0))
    return _rowwise(
        "final_loss", fn, [(x, row), (g, vec), (tgt, row)],
        [(_sds((1, 128), F32), _spec((1, 128), lambda i: (0, 0))), (_sds((T, D), F32), row),
         (_sds((1, D), F32), vec)],
        (T // tm,))


def _adamw(name, w, g, m, v):
    R, C = w.shape
    tr = R
    for cand in (512, 352, 256):
        if R % cand == 0:
            tr = cand
            break

    def fn(irefs, orefs, ids):
        delta, mn, vn = _adamw_update(*(r[...] for r in irefs))
        orefs[0][...] = delta
        orefs[1][...] = mn
        orefs[2][...] = vn

    blk = _spec((tr, C), lambda i: (i, 0))
    return _rowwise(name, fn, [(w, blk), (g, blk), (m, blk), (v, blk)],
                    [(_sds((R, C), F32), blk)] * 3, (R // tr,))


def _adamw_update(wv, gv, mv, vv):
    c1 = 1.0 - ADAM_B1 ** ADAM_STEP
    c2 = 1.0 - ADAM_B2 ** ADAM_STEP
    mn = ADAM_B1 * mv + (1.0 - ADAM_B1) * gv
    vn = ADAM_B2 * vv + (1.0 - ADAM_B2) * (gv * gv)
    delta = -ADAM_LR * ((mn / c1) / (jnp.sqrt(vn / c2) + ADAM_EPS) + ADAM_WD * wv)
    return delta, mn, vn


def _adamw_halves(name, w, mine, theirs, widx, m, v, core):
    R, C = w.shape
    H = R // 2
    tr = H
    while tr * C * 4 > (1 << 20) and tr % 16 == 0:
        tr //= 2
    nb = H // tr

    def body(core_ref, w_ref, mine_ref, theirs_ref, m_ref, v_ref, g_out, d_out, m_out, v_out):
        gv = jnp.where(pl.program_id(0) == core_ref[0], mine_ref[...], theirs_ref[...])
        delta, mn, vn = _adamw_update(w_ref[...], gv, m_ref[...], v_ref[...])
        g_out[...] = gv
        d_out[...] = delta
        m_out[...] = mn
        v_out[...] = vn

    full = pl.BlockSpec((tr, C), lambda h, i, core_ref: (h * nb + i, 0))
    mine_spec = pl.BlockSpec((None, tr, C), lambda h, i, core_ref: (widx, jnp.where(h == core_ref[0], i, 0), 0))
    theirs_spec = pl.BlockSpec((None, tr, C), lambda h, i, core_ref: (widx, jnp.where(h == core_ref[0], 0, i), 0))
    return _pcall(
        body, name=name, grid=(2, nb), num_prefetch=1,
        in_specs=[full, mine_spec, theirs_spec, full, full], out_specs=[full] * 4,
        out_shape=[_sds((R, C), F32)] * 4,
    )(core, w, mine, theirs, m, v)


def _rmsnorm(name, x, g):
    M = x.shape[0]
    tm = min(ROW_TILE, M)

    def fn(irefs, orefs, ids):
        orefs[0][...] = _rms_fwd(irefs[0][...], irefs[1][...]).astype(BF16)

    row = _spec((tm, D), lambda i: (i, 0))
    return _rowwise(name, fn, [(x, row), (g, _spec((1, D), lambda i: (0, 0)))],
                    [(_sds((M, D), BF16), row)], (M // tm,))[0]


WEIGHT_AT = {
    "ffn1_w1": ("col1", 0), "ffn1_w3": ("col1", 1), "ffn1_w2": ("row2a", 0),
    "w_ret_o": ("sqA", 0), "w_lru_o": ("sqA", 1), "w_out": ("sqA", 2),
    "w_xq": ("sqB", 0), "w_xk": ("sqB", 1), "w_xv": ("sqC", 0), "w_xo": ("sqC", 1),
    "ffn2_w1": ("col2a", 0), "ffn2_w3": ("col2b", 0), "ffn2_w2": ("row2b", 0),
}


def _local_step(x, mem, tgt, gw, sm, big):
    T = x.shape[0]
    tm = ROW_TILE

    def wt(name):
        key, idx = WEIGHT_AT[name]
        return gw[key], idx

    row3 = lambda i, j, r: (i, 0)
    vec3 = lambda i, j, r: (0, 0)
    rowD = _spec((tm, D), row3)
    vecD = _spec((1, D), vec3)

    def residual_norm(acc, erefs, orefs, ids):
        xo = erefs[0][...] + acc
        orefs[0][...] = xo
        orefs[1][...] = _rms_fwd(xo, erefs[1][...]).astype(BF16)

    def res_norm_io(x_res, g):
        return ([(x_res, rowD), (g, vecD)],
                [(_sds((T, D), F32), rowD), (_sds((T, D), BF16), rowD)])

    h1 = _rmsnorm("ffn1_norm", x, sm["ffn1_norm"])
    a1, b1, s1 = _ffn_up("ffn1_up", h1, *wt("ffn1_w1"), *wt("ffn1_w3"))
    x1, h2 = _ffn_down("ffn1_down", s1, *wt("ffn1_w2"), x, sm["mix_norm"])

    tw = min(WIDE_ROW_TILE, T)
    wideD = _spec((tw, D), row3)
    u = _gemm(
        "mix_in",
        [(h2, wideD, gw["win"], _spec((None, None, IN_BLK, D), lambda i, j, r: (j, 0, 0, 0)), "nt")],
        (T // tw, N_CHIPS, 1),
        [(_sds((T, 5120), F32), _spec((tw, IN_BLK), lambda i, j, r: (i, j)))], (tw, IN_BLK))[0]

    consts = _retention_constants(T)
    qr, kr, ret, yr, states = _ret_fwd(u, consts, sm["ret_gn"])

    conv_w = gw["conv"][:, 0].transpose(1, 0, 2).reshape(CONV_TAPS, D)
    xc, rg, ig, av, bx = _lru_gates_fwd(u, conv_w, sm["conv_b"], sm["w_rgate"], sm["b_rgate"],
                                        sm["w_igate"], sm["b_igate"], sm["lru_lambda"])
    a3 = av.reshape(T, LRU_BLOCKS, LRU_BLOCK)
    hl = _lru_scan("lru_scan_fwd", a3, bx.reshape(T, LRU_BLOCKS, LRU_BLOCK), False).reshape(T, D)

    row1 = _spec((tm, D), lambda i: (i, 0))
    glru1 = _spec((tm, D), lambda i: (i, 4))

    def lru_out(irefs, orefs, ids):
        gl, _ = _gelu_and_grad(irefs[1][...])
        orefs[0][...] = (irefs[0][...] * gl).astype(BF16)

    yl = _rowwise("lru_out", lru_out, [(hl, row1), (u, glru1)], [(_sds((T, D), BF16), row1)], (T // tm,))[0]

    def gate_epilogue(acc, erefs, orefs, ids):
        orefs[0][...] = _sigmoid(acc + erefs[0][...])

    gates = _gemm(
        "mix_gates",
        [(h2, wideD, gw["wbg"], _spec((None, None, BG_BLK, D), lambda i, j, r: (j, 0, 0, 0)), "nt")],
        (T // tw, N_CHIPS, 1),
        [(_sds((T, 2 * D), F32), _spec((tw, BG_BLK), lambda i, j, r: (i, j)))], (tw, BG_BLK),
        [(sm["b_branch_gate"], _spec((1, BG_BLK), lambda i, j, r: (0, j)))], gate_epilogue)[0]

    y_ret = _proj_sq("y_ret", yr, *wt("w_ret_o"), "nn")[0]

    def merge_epilogue(acc, erefs, orefs, ids):
        orefs[0][...] = acc
        orefs[1][...] = (erefs[0][...] * erefs[2][...] + erefs[1][...] * acc).astype(BF16)

    y_lru, merged = _proj_sq(
        "y_lru", yl, *wt("w_lru_o"), "nn",
        extras=[(gates, _spec((tm, D), lambda i, j, r: (i, 0))), (gates, _spec((tm, D), lambda i, j, r: (i, 1))),
                (y_ret, rowD)],
        epilogue=merge_epilogue,
        outs=[(_sds((T, D), F32), rowD), (_sds((T, D), BF16), rowD)])

    ex, ou = res_norm_io(x1, sm["xattn_norm"])
    x2, hq = _proj_sq("mix_out", merged, *wt("w_out"), "nn", extras=ex, epilogue=residual_norm, outs=ou)

    m = _rmsnorm("mem_norm", mem, sm["mem_norm"])
    xq = _proj_sq("xq", hq, *wt("w_xq"), "nn", BF16)[0]
    xk = _proj_sq("xk", m, *wt("w_xk"), "nn", BF16)[0]
    xv = _proj_sq("xv", m, *wt("w_xv"), "nn", BF16)[0]
    xo = _xattn_fwd(xq, xk, xv)
    ex, ou = res_norm_io(x2, sm["ffn2_norm"])
    x3, h3 = _proj_sq("xattn_out", xo, *wt("w_xo"), "nn", extras=ex, epilogue=residual_norm, outs=ou)

    a2, b2, s2 = _ffn_up("ffn2_up", h3, *wt("ffn2_w1"), *wt("ffn2_w3"))
    x4 = _ffn_down("ffn2_down", s2, *wt("ffn2_w2"), x3)[0]
    loss, dx4, dg_final = _final_loss(x4, sm["final_norm"], tgt)

    dx3, dg_ffn2 = _ffn_bwd("ffn2", dx4, h3, a2, b2, s2, *wt("ffn2_w1"), *wt("ffn2_w3"),
                            *wt("ffn2_w2"), x3, sm["ffn2_norm"], big)

    dxo = _proj_sq("d_xo", dx3, *wt("w_xo"), "nt", BF16)[0]
    big["w_xo"] = _dw_sq("dw_xo", xo, dx3)[None]
    dxq, dxk, dxv = _xattn_bwd(xq, xk, xv, dxo)
    big["w_xq"] = _dw_sq("dw_xq", hq, dxq)[None]
    ex, ou = _rms_bwd_io(x2, sm["xattn_norm"], dx3, T, tm)
    dx2, dg_xattn = _proj_sq("d_hq", dxq, *wt("w_xq"), "nt", extras=ex, epilogue=_rms_bwd_epilogue, outs=ou)
    big["w_xk"] = _dw_sq("dw_xk", m, dxk)[None]
    big["w_xv"] = _dw_sq("dw_xv", m, dxv)[None]

    M = mem.shape[0]

    def mem_norm_epilogue(acc, erefs, orefs, ids):
        _, dgp = _rms_bwd(erefs[0][...], erefs[1][...], acc)
        orefs[0][...] = dgp

    wsq_spec = lambda idx: _spec((N_CHIPS, None, SQ_BLK, D), lambda i, j, r: (0, idx, 0, 0))
    memD = _spec((M, D), row3)
    dg_mem = _gemm(
        "d_mem_norm",
        [(dxk, memD, wt("w_xk")[0], wsq_spec(wt("w_xk")[1]), "nt"),
         (dxv, memD, wt("w_xv")[0], wsq_spec(wt("w_xv")[1]), "nt")],
        (1, 1, 1), [(_sds((1, D), F32), vecD)], (M, D),
        [(mem, memD), (sm["mem_norm"], vecD)], mem_norm_epilogue)[0]

    def merged_bwd_epilogue(acc, erefs, orefs, ids):
        gr, gl, yrv, ylv = (e[...] for e in erefs)
        orefs[0][...] = (acc * gr).astype(BF16)
        orefs[1][...] = (acc * gl).astype(BF16)
        dgr = acc * yrv * gr * (1.0 - gr)
        dgl = acc * ylv * gl * (1.0 - gl)
        orefs[2][:, :D] = dgr.astype(BF16)
        orefs[2][:, D:] = dgl.astype(BF16)
        dbb = jnp.concatenate([jnp.sum(dgr, axis=0, keepdims=True), jnp.sum(dgl, axis=0, keepdims=True)], axis=1)
        _accumulate(orefs[3], dbb, ids[0] == 0)

    dy_ret, dy_lru, dgpre, db_bg = _proj_sq(
        "d_merged", dx2, *wt("w_out"), "nt",
        extras=[(gates, _spec((tm, D), lambda i, j, r: (i, 0))), (gates, _spec((tm, D), lambda i, j, r: (i, 1))),
                (y_ret, rowD), (y_lru, rowD)],
        epilogue=merged_bwd_epilogue,
        outs=[(_sds((T, D), BF16), rowD), (_sds((T, D), BF16), rowD),
              (_sds((T, 2 * D), BF16), _spec((tm, 2 * D), row3)),
              (_sds((1, 2 * D), F32), _spec((1, 2 * D), vec3))])
    big["w_branch_gate"] = _gemm(
        "dw_bg",
        [(h2, _spec((T, D), lambda j, n, r: (r, 0)), dgpre, _spec((T, BG_BLK), lambda j, n, r: (r, j)), "tn")],
        (N_CHIPS, 1, 1),
        [(_sds((N_CHIPS, D, BG_BLK), GRAD_WIRE_DTYPE), _spec((None, D, BG_BLK), lambda j, n, r: (j, 0, 0)))],
        (D, BG_BLK))[0][None]
    big["w_out"] = _dw_sq("dw_out", merged, dx2)[None]
    dyr = _proj_sq("d_yr", dy_ret, *wt("w_ret_o"), "nt")[0]
    big["w_ret_o"] = _dw_sq("dw_ret_o", yr, dy_ret)[None]
    dyl = _proj_sq("d_yl", dy_lru, *wt("w_lru_o"), "nt")[0]
    big["w_lru_o"] = _dw_sq("dw_lru_o", yl, dy_lru)[None]

    dq, dk, dv, dgr, dg_retgn = _ret_bwd(dyr, ret, u, qr, kr, states, consts, sm["ret_gn"])

    def lru_out_bwd(irefs, orefs, ids):
        gl, dgl = _gelu_and_grad(irefs[2][...])
        dyl_v = irefs[0][...]
        orefs[0][...] = dyl_v * gl
        orefs[1][...] = (dyl_v * irefs[1][...] * dgl).astype(BF16)

    dhl, dglru = _rowwise("lru_out_bwd", lru_out_bwd, [(dyl, row1), (hl, row1), (u, glru1)],
                          [(_sds((T, D), F32), row1), (_sds((T, D), BF16), row1)], (T // tm,))
    lmb = _lru_scan("lru_scan_bwd", a3, dhl.reshape(T, LRU_BLOCKS, LRU_BLOCK), True).reshape(T, D)
    dxl, dw_r, dw_i, dvec, dcw = _lru_gates_bwd(lmb, hl, av, rg, ig, xc, u, conv_w,
                                                sm["w_rgate"], sm["w_igate"], sm["lru_lambda"])

    du = jnp.concatenate([dq, dk, dv, dgr, dxl, dglru], axis=1)
    tk = T
    big["w_in"] = _gemm(
        "dw_in",
        [(h2, _spec((tk, D), lambda j, n, r: (r, 0)), du, _spec((tk, IN_BLK), lambda j, n, r: (r, j)), "tn")],
        (N_CHIPS, 1, T // tk),
        [(_sds((N_CHIPS, D, IN_BLK), GRAD_WIRE_DTYPE), _spec((None, D, IN_BLK), lambda j, n, r: (j, 0, 0)))],
        (D, IN_BLK))[0][None]
    tf = min(FFN_ROW_TILE, T)
    ex, ou = _rms_bwd_io(x1, sm["mix_norm"], dx2, T, tf)
    dx1, dg_mix = _gemm(
        "d_h2",
        [(du, _spec((tf, 5120), row3), gw["win"], _spec((N_CHIPS, None, IN_BLK, D), lambda i, j, r: (0, 0, 0, 0)), "nn"),
         (dgpre, _spec((tf, 2 * D), row3), gw["wbg"], _spec((N_CHIPS, None, BG_BLK, D), lambda i, j, r: (0, 0, 0, 0)),
          "nn")],
        (T // tf, 1, 1), ou, (tf, D), ex, _rms_bwd_epilogue)

    grad_x, dg_ffn1 = _ffn_bwd("ffn1", dx1, h1, a1, b1, s1, *wt("ffn1_w1"), *wt("ffn1_w3"),
                               *wt("ffn1_w2"), x, sm["ffn1_norm"], big)

    small = {
        "ffn1_norm": dg_ffn1, "mix_norm": dg_mix, "ret_gn": dg_retgn, "conv_b": dvec[3:4],
        "b_rgate": dvec[0:1], "b_igate": dvec[1:2], "lru_lambda": dvec[2:3], "xattn_norm": dg_xattn,
        "mem_norm": dg_mem, "ffn2_norm": dg_ffn2, "final_norm": dg_final, "b_branch_gate": db_bg,
        "conv_w": dcw, "w_rgate": dw_r, "w_igate": dw_i,
    }
    return loss, grad_x, small


ANY_SPEC = pl.BlockSpec(memory_space=pl.ANY)
VMEM_SPEC = pl.BlockSpec(memory_space=pltpu.VMEM)
N_PEER_CHIPS = N_CHIPS - 1


def _mesh_position():
    x, y, c = lax.axis_index("x"), lax.axis_index("y"), lax.axis_index("c")
    chips = [(1 - x, y), (x, 1 - y), (1 - x, 1 - y)]
    return x, y, c, chips


def _chip_index(x, y):
    return 2 * x + y


def _rows_half(ref, axis, h):
    n = ref.shape[axis] // 2
    idx = [slice(None)] * len(ref.shape)
    idx[axis] = pl.ds(pl.multiple_of(h * n, 16), n)
    return ref.at[tuple(idx)]


def _remote(src, dst, send_sem, recv_sem, device):
    return pltpu.make_async_remote_copy(src_ref=src, dst_ref=dst, send_sem=send_sem, recv_sem=recv_sem,
                                        device_id=device, device_id_type=MESH)


def _gather_chips_task(shards, split, landed, part=0, nparts=1):
    keys = list(shards)
    n = len(keys)

    def operands():
        if part:
            return [shards[k] for k in keys] + [landed[k] for k in keys]
        chip_me = _chip_index(lax.axis_index("x"), lax.axis_index("y"))
        bases = [lax.dynamic_update_slice(lax.empty((N_CHIPS,) + shards[k].shape, shards[k].dtype), shards[k][None],
                                          (chip_me,) + (0,) * shards[k].ndim) for k in keys]
        return [shards[k] for k in keys] + bases

    def my_rows(ref, c):
        rows = ref.shape[1] // (2 * nparts)
        return ref.at[:, pl.ds(pl.multiple_of((c * nparts + part) * rows, 16), rows), :]

    def make(ins, outs, send_sem, recv_sem):
        x, y, c, chips = _mesh_position()
        s_me = _chip_index(x, y)
        starts, arrivals = [], []
        for g in range(n):
            mine = my_rows(ins[g], c) if split else ins[g]
            for k, chip in enumerate(chips):
                def landing(s):
                    o = outs[g].at[s]
                    return my_rows(o, c) if split else o
                starts.append(_remote(mine, landing(s_me), send_sem(3 * g + k), recv_sem(3 * g + k), (*chip, c)))
                got = landing(_chip_index(*chip))
                arrivals.append(functools.partial(_remote, got, got, send_sem(3 * g + k), recv_sem(3 * g + k),
                                                  (*chip, c)))
        return starts, arrivals

    def finish(res):
        landed.update(zip(keys, res))

    return _Task("chips", operands, lambda: [_sds((N_CHIPS,) + shards[k].shape, shards[k].dtype) for k in keys],
                 {n + g: g for g in range(n)}, 3 * n, make, finish)


def _gather_sibling_task(keys, landed, ready):
    n = len(keys)

    def make(ins, outs, send_sem, recv_sem):
        x, y, c, chips = _mesh_position()
        starts, arrivals = [], []
        for g in range(n):
            for k, chip in enumerate(chips):
                o = outs[g].at[_chip_index(*chip)]
                got, other = _rows_half(o, 1, c), _rows_half(o, 1, 1 - c)
                starts.append(_remote(got, got, send_sem(3 * g + k), recv_sem(3 * g + k), (x, y, 1 - c)))
                arrivals.append(functools.partial(_remote, other, other, send_sem(3 * g + k), recv_sem(3 * g + k),
                                                  (x, y, 1 - c)))
        return starts, arrivals

    def finish(res):
        ready.update(zip(keys, res))

    return _Task("sibling", lambda: [landed[k] for k in keys],
                 lambda: [_sds(landed[k].shape, landed[k].dtype) for k in keys],
                 {g: g for g in range(n)}, 3 * n, make, finish)


def _pair_swap_task(names, big, got):
    n = len(names)

    def make(ins, outs, send_sem, recv_sem):
        x, y, c, _ = _mesh_position()
        copies = [_remote(_rows_half(ins[a], 2, 1 - c), outs[a], send_sem(a), recv_sem(a), (x, y, 1 - c))
                  for a in range(n)]
        return copies, [functools.partial(lambda cp: cp, cp) for cp in copies]

    def shapes():
        return [_sds(big[k].shape[:2] + (big[k].shape[2] // 2, big[k].shape[3]), big[k].dtype) for k in names]

    return _Task("sibling", lambda: [big[k] for k in names], shapes, {}, n, make,
                 lambda res: got.update(zip(names, res)))


def _rs_pair_sum(name, full, got, core):
    nw, ns, R, C = full.shape
    half = R // 2

    def body(core_ref, a_ref, b_ref, o_ref):
        o_ref[...] = (a_ref[...].astype(F32) + b_ref[...].astype(F32)).astype(BF16)

    blk = lambda fn: pl.BlockSpec((None, None, half, C), fn)
    return _pcall(
        body, name=name, grid=(nw, ns), num_prefetch=1,
        in_specs=[blk(lambda w, s, core_ref: (w, s, core_ref[0], 0)), blk(lambda w, s, core_ref: (w, s, 0, 0))],
        out_specs=blk(lambda w, s, core_ref: (w, s, 0, 0)),
        out_shape=_sds((nw, ns, half, C), BF16),
    )(core, full, got)


def _chip_exchange_task(names, pair_sums, by_source, part=0, nparts=1):
    n = len(names)

    def rows(ref):
        h = ref.shape[1] // nparts
        return ref.at[:, pl.ds(part * h, h), :]

    def make(ins, outs, send_sem, recv_sem):
        x, y, c, chips = _mesh_position()
        s_me = _chip_index(x, y)
        starts, arrivals = [], []
        for a in range(n):
            for k, chip in enumerate(chips):
                s_k = _chip_index(*chip)
                starts.append(_remote(rows(ins[a].at[:, s_k]), rows(outs[a].at[:, s_me]), send_sem(3 * a + k),
                                      recv_sem(3 * a + k), (*chip, c)))
                got = rows(outs[a].at[:, s_k])
                arrivals.append(functools.partial(_remote, got, got, send_sem(3 * a + k), recv_sem(3 * a + k),
                                                  (*chip, c)))
        return starts, arrivals

    def operands():
        return [pair_sums[k] for k in names] + ([by_source[k] for k in names] if part else [])

    return _Task("chips", operands, lambda: [_sds(pair_sums[k].shape, pair_sums[k].dtype) for k in names],
                 {n + a: a for a in range(n)} if part else {}, 3 * n, make,
                 lambda res: by_source.update(zip(names, res)))


def _rs_chip_sum(name, own, parts, chip):
    nw, ns, H, C = parts.shape

    def body(chip_ref, own_ref, *rest):
        prefs, o_ref = rest[:ns], rest[ns]
        me = chip_ref[0]
        own_v = own_ref[...].astype(F32)
        tot = None
        for s in range(ns):
            term = jnp.where(me == s, own_v, prefs[s][...].astype(F32))
            tot = term if tot is None else tot + term
        o_ref[...] = tot

    blk = lambda fn: pl.BlockSpec((None, None, H, C), fn)

    def part_spec(s):
        return blk(lambda w, chip_ref: (w, jnp.where(chip_ref[0] == s, (s + 1) % ns, s), 0, 0))

    return _pcall(
        body, name=name, grid=(nw,), num_prefetch=1,
        in_specs=[blk(lambda w, chip_ref: (w, chip_ref[0], 0, 0))] + [part_spec(s) for s in range(ns)],
        out_specs=pl.BlockSpec((None, H, C), lambda w, chip_ref: (w, 0, 0)),
        out_shape=_sds((nw, H, C), F32),
    )(chip, own, *([parts] * ns))


def _pair_gather_task(names, halves, sibling_halves):
    n = len(names)

    def make(ins, outs, send_sem, recv_sem):
        x, y, c, _ = _mesh_position()
        copies = [_remote(ins[a], outs[a], send_sem(a), recv_sem(a), (x, y, 1 - c)) for a in range(n)]
        return copies, [functools.partial(lambda cp: cp, cp) for cp in copies]

    return _Task("sibling", lambda: [halves[k] for k in names], lambda: [_sds(halves[k].shape, F32) for k in names],
                 {}, n, make, lambda res: sibling_halves.update(zip(names, res)))


def _small_allreduce(arrs):
    n = len(arrs)
    per = 1 + 2 * N_PEER_CHIPS

    def body(*refs):
        v_refs, o_refs = refs[:n], refs[n:2 * n]
        sib, pair, part = refs[2 * n:3 * n], refs[3 * n:4 * n], refs[4 * n:5 * n]
        send_sems, recv_sems = refs[5 * n:]
        x, y, c, chips = _mesh_position()
        s_me = _chip_index(x, y)

        def quarter(ref, s):
            q = ref.shape[0] // N_CHIPS
            return ref.at[pl.ds(pl.multiple_of(s * q, 8), q)]

        def exchange(first_sem, src, dst_of, arrival_of):
            sems = lambda a, k: (send_sems.at[a * per + first_sem + k], recv_sems.at[a * per + first_sem + k])
            sends = [_remote(src(a, _chip_index(*chip)), dst_of(a, s_me), *sems(a, k), (*chip, c))
                     for a in range(n) for k, chip in enumerate(chips)]
            for cp in sends:
                cp.start()
            for a in range(n):
                for k, chip in enumerate(chips):
                    got = arrival_of(a, _chip_index(*chip))
                    _remote(got, got, *sems(a, k), (*chip, c)).wait_recv()
            for cp in sends:
                cp.wait_send()

        swaps = [_remote(v_refs[a], sib[a], send_sems.at[a * per], recv_sems.at[a * per], (x, y, 1 - c))
                 for a in range(n)]
        for cp in swaps:
            cp.start()
        for cp in swaps:
            cp.wait()
        for a in range(n):
            pair[a][...] = v_refs[a][...] + sib[a][...]
        exchange(1, lambda a, s_k: quarter(pair[a], s_k), lambda a, s: part[a].at[s], lambda a, s_k: part[a].at[s_k])
        for a in range(n):
            part[a][s_me] = quarter(pair[a], s_me)[...]
            q = o_refs[a].shape[0] // N_CHIPS
            o_refs[a][pl.ds(pl.multiple_of(s_me * q, 8), q), :] = (
                ((part[a][0] + part[a][1]) + part[a][2]) + part[a][3])
        exchange(1 + N_PEER_CHIPS, lambda a, s_k: quarter(o_refs[a], s_me), lambda a, s: quarter(o_refs[a], s),
                 lambda a, s_k: quarter(o_refs[a], s_k))

    shapes = [a.shape for a in arrs]
    return _pcall(
        body, name="small_allreduce", grid=(1,), own_peers=("sibling", "chips"),
        in_specs=[VMEM_SPEC] * n, out_specs=[VMEM_SPEC] * n, out_shape=[_sds(s, F32) for s in shapes],
        scratch_shapes=([pltpu.VMEM(s, F32) for s in shapes] * 2
                        + [pltpu.VMEM((N_CHIPS, s[0] // N_CHIPS, s[1]), F32) for s in shapes]
                        + [pltpu.SemaphoreType.DMA((n * per,)), pltpu.SemaphoreType.DMA((n * per,))]),
    )(*arrs)


TRANSPOSED_WEIGHTS = ("ffn1_w1", "ffn1_w3", "ffn2_w1", "ffn2_w3")
SMALL_LAYOUT = [("ffn1_norm", 1), ("mix_norm", 1), ("ret_gn", 1), ("conv_b", 1), ("b_rgate", 1), ("b_igate", 1),
                ("lru_lambda", 1), ("xattn_norm", 1), ("mem_norm", 1), ("ffn2_norm", 1), ("final_norm", 1),
                ("b_branch_gate", 2), ("conv_w", CONV_TAPS)]
SMALL_ROWS = 32
GATE_WEIGHTS = ("w_rgate", "w_igate")
WEIGHT_ORDER = ["ffn1_norm", "ffn1_w1", "ffn1_w3", "ffn1_w2", "mix_norm", "w_in", "ret_gn", "w_ret_o", "conv_w",
                "conv_b", "w_rgate", "b_rgate", "w_igate", "b_igate", "lru_lambda", "w_lru_o", "w_branch_gate",
                "b_branch_gate", "w_out", "xattn_norm", "mem_norm", "w_xq", "w_xk", "w_xv", "w_xo", "ffn2_norm",
                "ffn2_w1", "ffn2_w3", "ffn2_w2", "final_norm"]


def _pack_small(parts):
    rows = [parts[name].reshape(n, D) for name, n in SMALL_LAYOUT]
    used = sum(n for _, n in SMALL_LAYOUT)
    rows.append(jnp.zeros((SMALL_ROWS - used, D), F32))
    return jnp.concatenate(rows, axis=0)


def _unpack_small(packed, shapes):
    out, r = {}, 0
    for name, n in SMALL_LAYOUT:
        out[name] = packed[r:r + n].reshape(shapes[name])
        r += n
    return out


def kernel(x, mem, ffn1_norm, ffn1_w1, ffn1_w3, ffn1_w2, mix_norm, w_in, ret_gn, w_ret_o, conv_w, conv_b, w_rgate, b_rgate, w_igate, b_igate, lru_lambda, w_lru_o, w_branch_gate, b_branch_gate, w_out, xattn_norm, mem_norm, w_xq, w_xk, w_xv, w_xo, ffn2_norm, ffn2_w1, ffn2_w3, ffn2_w2, final_norm, loss_target, m_ffn1_norm, m_ffn1_w1, m_ffn1_w3, m_ffn1_w2, m_mix_norm, m_w_in, m_ret_gn, m_w_ret_o, m_conv_w, m_conv_b, m_w_rgate, m_b_rgate, m_w_igate, m_b_igate, m_lru_lambda, m_w_lru_o, m_w_branch_gate, m_b_branch_gate, m_w_out, m_xattn_norm, m_mem_norm, m_w_xq, m_w_xk, m_w_xv, m_w_xo, m_ffn2_norm, m_ffn2_w1, m_ffn2_w3, m_ffn2_w2, m_final_norm, v_ffn1_norm, v_ffn1_w1, v_ffn1_w3, v_ffn1_w2, v_mix_norm, v_w_in, v_ret_gn, v_w_ret_o, v_conv_w, v_conv_b, v_w_rgate, v_b_rgate, v_w_igate, v_b_igate, v_lru_lambda, v_w_lru_o, v_w_branch_gate, v_b_branch_gate, v_w_out, v_xattn_norm, v_mem_norm, v_w_xq, v_w_xk, v_w_xv, v_w_xo, v_ffn2_norm, v_ffn2_w1, v_ffn2_w3, v_ffn2_w2, v_final_norm):
    given = dict(locals())
    w = {n: given[n] for n in WEIGHT_ORDER}
    mom = {n: given["m_" + n] for n in WEIGHT_ORDER}
    var = {n: given["v_" + n] for n in WEIGHT_ORDER}
    chip = _chip_index(lax.axis_index("x"), lax.axis_index("y"))
    core = lax.axis_index("c").astype(jnp.int32).reshape(1)

    chip_id = chip.astype(jnp.int32).reshape(1)
    sm = {n: w[n] for n in ["ffn1_norm", "mix_norm", "ret_gn", "conv_b", "b_rgate", "b_igate", "lru_lambda",
                            "xattn_norm", "mem_norm", "ffn2_norm", "b_branch_gate"]}
    sm["final_norm"] = w["final_norm"].reshape(1, D)
    sm["w_rgate"] = w["w_rgate"][0]
    sm["w_igate"] = w["w_igate"][0]

    local = lambda a, n: jnp.swapaxes(a[0], 0, 1) if n in TRANSPOSED_WEIGHTS else a[0]
    stack = lambda names: jnp.stack([local(w[n], n) for n in names], axis=0).astype(BF16)
    shard = {"col1": stack(["ffn1_w1", "ffn1_w3"]), "row2a": stack(["ffn1_w2"]),
             "win": jnp.swapaxes(w["w_in"], 1, 2).astype(BF16), "wbg": jnp.swapaxes(w["w_branch_gate"], 1, 2).astype(BF16), "sqA": stack(["w_ret_o", "w_lru_o", "w_out"]),
             "sqB": stack(["w_xq", "w_xk"]), "sqC": stack(["w_xv", "w_xo"]), "col2a": stack(["ffn2_w1"]), "col2b": stack(["ffn2_w3"]),
             "row2b": stack(["ffn2_w2"]), "conv": w["conv_w"]}
    gw, landed = {}, {}
    over_chips = lambda keys: _gather_chips_task({k: shard[k] for k in keys}, True, landed)
    to_sibling = lambda keys: _gather_sibling_task(keys, landed, gw)

    big, got, pair_sums, by_source, halves, sibling_halves, outs = {}, {}, {}, {}, {}, {}, {}
    pair_swap = lambda names: _pair_swap_task(names, big, got)
    exchange = lambda names, part=0, nparts=1: _chip_exchange_task(names, pair_sums, by_source, part, nparts)
    pair_gather = lambda names: _pair_gather_task(names, halves, sibling_halves)

    def pair_sum(names):
        for n in names:
            pair_sums[n] = _rs_pair_sum("rs_pair_sum_" + n, big[n], got[n], core)

    def chip_sum(names):
        for n in names:
            halves[n] = _rs_chip_sum("rs_chip_sum_" + n, pair_sums[n], by_source[n], chip_id)

    def adamw(names):
        for n in names:
            res = _adamw_halves("adamw_" + n, local(w[n], n), halves[n], sibling_halves[n], 0, local(mom[n], n),
                                local(var[n], n), core)
            outs[n] = tuple((jnp.swapaxes(r, 0, 1) if n in TRANSPOSED_WEIGHTS else r)[None] for r in res)

    do = lambda fn, names: functools.partial(fn, names)
    ffn2_grads = ["ffn2_w2", "ffn2_w1", "ffn2_w3"]
    xattn_grads = ["w_xo", "w_xq", "w_xk", "w_xv"]
    mix_out_grads = ["w_branch_gate", "w_out", "w_ret_o", "w_lru_o"]
    conv_gather = _gather_chips_task({"conv": shard["conv"]}, False, gw)
    half = lambda key, part: _gather_chips_task({key: shard[key]}, True, landed, part, 2)
    plan = _Plan()
    plan.tasks = {
        "ag_first_chips": [over_chips(["col1", "row2a"])],
        "ag_first_sibling": [to_sibling(["col1", "row2a"])],
        "ffn1_up": [over_chips(["win"])],
        "ffn1_down": [to_sibling(["win"]), over_chips(["wbg"]), conv_gather],
        "mix_in": [to_sibling(["wbg"]), over_chips(["sqA"])],
        "ret_fwd": [to_sibling(["sqA"]), over_chips(["col2a"])],
        "lru_gates_fwd": [to_sibling(["col2a"]), over_chips(["sqB"])],
        "lru_scan_fwd": [to_sibling(["sqB"]), over_chips(["sqC"])],
        "mix_gates": [to_sibling(["sqC"]), half("col2b", 0)],
        "y_lru": [half("col2b", 1)],
        "xattn_fwd": [to_sibling(["col2b"])],
        "ffn2_up": [over_chips(["row2b"])],
        "ffn2_up_sibling": [to_sibling(["row2b"])],
        "ffn2_dh": [pair_swap(ffn2_grads)],
        "xattn_bwd": [exchange(["ffn2_w2"], 0, 2)],
        "d_hq": [exchange(["ffn2_w2"], 1, 2)],
        "d_merged": [exchange(["ffn2_w1"], 0, 2), pair_swap(xattn_grads)],
        "lru_out_bwd": [exchange(["w_xo"])],
        "ret_bwd": [exchange(["ffn2_w1"], 1, 2), exchange(["ffn2_w3"], 0, 2), pair_swap(mix_out_grads)],
        "lru_scan_bwd": [exchange(["ffn2_w3"], 1, 2)],
        "lru_gates_bwd": [exchange(["w_xq", "w_xk"]), pair_gather(ffn2_grads)],
        "dw_in": [exchange(["w_xv", "w_out"])],
        "d_h2": [exchange(["w_branch_gate", "w_ret_o", "w_lru_o"]), pair_swap(["w_in"]), pair_gather(xattn_grads)],
        "ffn1_bwd_mid": [exchange(["w_in"], 0, 2), pair_gather(mix_out_grads)],
        "ffn1_dw2": [exchange(["w_in"], 2, 4)],
        "ffn1_dw1": [exchange(["w_in"], 3, 4), pair_swap(["ffn1_w2"])],
        "ffn1_dw3": [exchange(["ffn1_w2"], 0, 2), pair_swap(["ffn1_w1"]), pair_gather(["w_in"])],
        "ffn1_dh": [exchange(["ffn1_w2"], 1, 2), exchange(["ffn1_w1"]), pair_swap(["ffn1_w3"])],
        "small_allreduce": [exchange(["ffn1_w3"]), pair_gather(["ffn1_w2"])],
        "rs_last_gather": [pair_gather(["ffn1_w1", "ffn1_w3"])],
    }
    plan.after = {
        "ffn2_up": [functools.partial(_comm_call, "ffn2_up_sibling")],
        "ffn2_dh": [do(pair_sum, ffn2_grads)],
        "d_merged": [do(pair_sum, xattn_grads)],
        "ret_bwd": [do(pair_sum, mix_out_grads)],
        "lru_scan_bwd": [do(chip_sum, ffn2_grads)],
        "lru_gates_bwd": [do(adamw, ffn2_grads)],
        "dw_in": [do(chip_sum, xattn_grads)],
        "d_h2": [do(chip_sum, mix_out_grads), do(pair_sum, ["w_in"]), do(adamw, xattn_grads)],
        "ffn1_bwd_mid": [do(adamw, mix_out_grads)],
        "ffn1_dw1": [do(chip_sum, ["w_in"]), do(pair_sum, ["ffn1_w2"])],
        "ffn1_dw3": [do(pair_sum, ["ffn1_w1"]), do(adamw, ["w_in"])],
        "ffn1_dh": [do(pair_sum, ["ffn1_w3"]), do(chip_sum, ["ffn1_w2"])],
        "small_allreduce": [do(chip_sum, ["ffn1_w1", "ffn1_w3"]), functools.partial(_comm_call, "rs_last_gather"),
                    do(adamw, ["ffn1_w2", "ffn1_w1", "ffn1_w3"])],
    }
    global _plan
    _plan = plan
    try:
        _comm_call("ag_first_chips")
        _comm_call("ag_first_sibling")
        loss_part, grad_x, small = _local_step(x[0], mem[0], loss_target[0], gw, sm, big)
        gate2d = lambda a: a.reshape(LRU_BLOCKS * LRU_BLOCK, LRU_BLOCK)
        small_sum, *gate_sums = _small_allreduce([_pack_small(small)] + [gate2d(small[n]) for n in GATE_WEIGHTS])
    finally:
        _plan = None
    assert not plan.tasks and not plan.after, (list(plan.tasks), list(plan.after))
    loss = lax.psum(loss_part[0, 0], ("x", "y", "c"))

    small_shapes = {n: w[n].shape for n, _ in SMALL_LAYOUT}
    small_shapes["conv_w"] = (CONV_TAPS, D)
    conv_grad = lax.dynamic_slice(small_sum[13:13 + CONV_TAPS], (0, chip * SQ_BLK), (CONV_TAPS, SQ_BLK))
    small_w = {n: w[n] for n, _ in SMALL_LAYOUT}
    small_m = {n: mom[n] for n, _ in SMALL_LAYOUT}
    small_v = {n: var[n] for n, _ in SMALL_LAYOUT}
    pad_cols = lambda a: jnp.pad(a[0], ((0, 0), (0, D - SQ_BLK)))
    for dct in (small_w, small_m, small_v):
        dct["conv_w"] = pad_cols(dct["conv_w"])
    g_pack = lax.dynamic_update_slice(small_sum, jnp.pad(conv_grad, ((0, 0), (0, D - SQ_BLK))), (13, 0))
    d_pack, m_pack, v_pack = _adamw("adamw_small", _pack_small(small_w), g_pack, _pack_small(small_m),
                                    _pack_small(small_v))
    unpacked = [_unpack_small(p, small_shapes) for p in (g_pack, d_pack, m_pack, v_pack)]
    for n, _ in SMALL_LAYOUT:
        if n == "conv_w":
            outs[n] = tuple(u[n][:, :SQ_BLK][None] for u in unpacked)
        else:
            outs[n] = tuple(u[n] for u in unpacked)
    for n, gsum in zip(GATE_WEIGHTS, gate_sums):
        d, nm, nv = _adamw("adamw_" + n, gate2d(w[n]), gsum, gate2d(mom[n]), gate2d(var[n]))
        outs[n] = tuple(r.reshape(w[n].shape) for r in (gsum, d, nm, nv))

    result = [loss, grad_x[None]]
    for k in range(4):
        result += [outs[n][k] for n in WEIGHT_ORDER]
    return tuple(result)
```

```python
import functools
import math

import numpy as np
import jax
import jax.numpy as jnp
from jax import lax
from jax.experimental import pallas as pl
from jax.experimental.pallas import tpu as pltpu

F32 = jnp.float32
BF16 = jnp.bfloat16
GRAD_WIRE_DTYPE = BF16
MESH = pl.DeviceIdType.MESH

D = 1024
EPS = 1e-6
RET_HEADS = 4
RET_DK = 128
RET_DV = 256
CHUNK = 128
ROPE_BASE = 10000.0
LRU_BLOCKS = 8
LRU_BLOCK = 128
CONV_TAPS = 4
LRU_C = 8.0
D_FF = 2816
X_HEADS = 4
X_HD = 256
N_CHIPS = 4
FF_BLK = D_FF // N_CHIPS
IN_BLK = 5120 // N_CHIPS
BG_BLK = 2048 // N_CHIPS
SQ_BLK = D // N_CHIPS

ADAM_LR = 0.001
ADAM_B1 = 0.9
ADAM_B2 = 0.999
ADAM_EPS = 1e-08
ADAM_WD = 0.01
ADAM_STEP = 10

VMEM_LIMIT_BYTES = 56 * 1024 * 1024
ROW_TILE = 512
WIDE_ROW_TILE = 1024
FFN_ROW_TILE = 256
DW_BLK = D_FF // 2
SCAN_TILE = 256

_DN = {
    "nn": (((1,), (0,)), ((), ())),
    "nt": (((1,), (1,)), ((), ())),
    "tn": (((0,), (0,)), ((), ())),
}


def _cparams(n_axes, collective_id=None):
    return pltpu.CompilerParams(dimension_semantics=("arbitrary",) * n_axes,
                                vmem_limit_bytes=VMEM_LIMIT_BYTES, collective_id=collective_id)


def _dot(a, b, kind):
    if b.ndim == 3:
        b = b.reshape(b.shape[0] * b.shape[1], b.shape[2])
    return lax.dot_general(a.astype(BF16), b.astype(BF16), _DN[kind], preferred_element_type=F32)


def _sigmoid(x):
    return 1.0 / (1.0 + jnp.exp(-x))


def _log1p_pos(e):
    u = 1.0 + e
    return jnp.where(u == 1.0, e, jnp.log(u) * (e / jnp.where(u == 1.0, 1.0, u - 1.0)))


def _expm1(x):
    u = jnp.exp(x)
    lu = jnp.log(u)
    safe = jnp.where(lu == 0.0, 1.0, lu)
    return jnp.where(u == 1.0, x, (u - 1.0) * (x / safe))


def _softplus(z):
    return jnp.maximum(z, 0.0) + _log1p_pos(jnp.exp(-jnp.abs(z)))


_GELU_C = math.sqrt(2.0 / math.pi)


def _gelu_and_grad(x):
    x2 = x * x
    t = jnp.tanh(_GELU_C * (x + 0.044715 * x * x2))
    g = 0.5 * x * (1.0 + t)
    dg = 0.5 * (1.0 + t) + 0.5 * x * (1.0 - t * t) * (_GELU_C * (1.0 + 3.0 * 0.044715 * x2))
    return g, dg


def _rms_fwd(x, g):
    r = lax.rsqrt(jnp.mean(x * x, axis=-1, keepdims=True) + EPS)
    return (x * r) * g


def _rms_bwd(x, g, dh):
    r = lax.rsqrt(jnp.mean(x * x, axis=-1, keepdims=True) + EPS)
    n = x * r
    dyg = dh * g
    dx = r * (dyg - n * jnp.mean(dyg * n, axis=-1, keepdims=True))
    return dx, jnp.sum(dh * n, axis=0, keepdims=True)


def _accumulate(ref, val, first):
    @pl.when(first)
    def _():
        ref[...] = val

    @pl.when(jnp.logical_not(first))
    def _():
        ref[...] += val


def _sds(shape, dtype):
    return jax.ShapeDtypeStruct(tuple(shape), dtype)


def _spec(shape, fn):
    return pl.BlockSpec(tuple(shape), fn)


class _Task:
    def __init__(self, peers, operands, out_shapes, aliases, nsem, make, finish, make_second=None):
        self.peers = peers
        self.operands, self.out_shapes, self.aliases = operands, out_shapes, aliases
        self.nsem, self.make, self.finish = nsem, make, finish
        self.make_second = make_second


class _Plan:
    def __init__(self):
        self.tasks, self.after = {}, {}


_plan = None


PEER_SET_COLLECTIVE_ID = {frozenset({"sibling"}): 1, frozenset({"chips"}): 2, frozenset({"sibling", "chips"}): 3,
                          frozenset({"neighbours"}): 4, frozenset({"sibling", "neighbours"}): 5}


def _peer_set(names):
    names = frozenset(names)
    return names - {"neighbours"} if "chips" in names else names


def _entry_handshake(peer_set):
    x, y, c, chips = _mesh_position()
    peers = [(x, y, 1 - c)] if "sibling" in peer_set else []
    if "chips" in peer_set:
        peers += [(*chip, c) for chip in chips]
    if "neighbours" in peer_set:
        peers += [(*chip, c) for chip in chips[:2]]
    barrier = pltpu.get_barrier_semaphore()
    for peer in peers:
        pl.semaphore_signal(barrier, inc=1, device_id=peer, device_id_type=MESH)
    pl.semaphore_wait(barrier, len(peers))


def _pcall(body, *, name, grid, in_specs, out_specs, out_shape, scratch_shapes=(), num_prefetch=0, own_peers=()):
    single = not isinstance(out_shape, (list, tuple))
    out_shape = [out_shape] if single else list(out_shape)
    out_specs = [out_specs] if single else list(out_specs)
    in_specs = list(in_specs)
    scratch_shapes = list(scratch_shapes)
    tasks = _plan.tasks.pop(name, []) if _plan is not None else []
    after = _plan.after.pop(name, []) if _plan is not None else []
    peer_set = _peer_set([t.peers for t in tasks] + list(own_peers))
    nax = len(grid)

    def run(*operands):
        n_in = len(operands) - num_prefetch
        n_out = len(out_shape)
        t_ops = [t.operands() for t in tasks]
        t_outs = [t.out_shapes() for t in tasks]
        c_ops = [a for ops in t_ops for a in ops]
        c_outs = [s for outs in t_outs for s in outs]
        aliases = {}
        i0, o0 = num_prefetch + n_in, n_out
        for t, ops, outs in zip(tasks, t_ops, t_outs):
            for i_loc, o_loc in t.aliases.items():
                aliases[i0 + i_loc] = o0 + o_loc
            i0 += len(ops)
            o0 += len(outs)
        nsem = sum(t.nsem for t in tasks)

        def wrapped(*refs):
            p = num_prefetch
            pre, ins = refs[:p], refs[p:p + n_in]
            cins = refs[p + n_in:p + n_in + len(c_ops)]
            q = p + n_in + len(c_ops)
            outs, couts = refs[q:q + n_out], refs[q + n_out:q + n_out + len(c_outs)]
            q += n_out + len(c_outs)
            scr = refs[q:q + len(scratch_shapes)]

            def rounds(second):
                send_sems, recv_sems = refs[q + len(scratch_shapes):]
                out = []
                ci = co = so = 0
                for t, ops, souts in zip(tasks, t_ops, t_outs):
                    make = t.make_second if second else t.make
                    out.append(([], []) if make is None else
                               make(cins[ci:ci + len(ops)], couts[co:co + len(souts)],
                                    functools.partial(lambda base, k: send_sems.at[base + k], so),
                                    functools.partial(lambda base, k: recv_sems.at[base + k], so)))
                    ci, co, so = ci + len(ops), co + len(souts), so + t.nsem
                return out

            two_rounds = [t.make_second is not None for t in tasks]
            if peer_set:
                ids = [pl.program_id(k) for k in range(nax)]
                first = functools.reduce(jnp.logical_and, [i == 0 for i in ids])
                last = functools.reduce(jnp.logical_and, [i == g - 1 for i, g in zip(ids, grid)])
                step = functools.reduce(lambda acc, ig: acc * ig[1] + ig[0], zip(ids, grid), 0)
                middle = step == math.prod(grid) // 2

                @pl.when(first)
                def _():
                    _entry_handshake(peer_set)
                    for starts, _ in rounds(False):
                        for copy in starts:
                            copy().start()

            body(*pre, *ins, *outs, *scr)

            if any(two_rounds):
                @pl.when(middle)
                def _():
                    for (_, arrivals), two in zip(rounds(False), two_rounds):
                        if two:
                            for arrival in arrivals:
                                arrival().wait_recv()
                    for starts, _ in rounds(True):
                        for copy in starts:
                            copy().start()

            if tasks:
                @pl.when(last)
                def _():
                    first_round, second_round = rounds(False), rounds(True)
                    for (_, arrivals1), (_, arrivals2), two in zip(first_round, second_round, two_rounds):
                        for arrival in (arrivals2 if two else arrivals1):
                            arrival().wait_recv()
                    for starts, _ in first_round + second_round:
                        for copy in starts:
                            copy().wait_send()

        sems = [pltpu.SemaphoreType.DMA((nsem,)), pltpu.SemaphoreType.DMA((nsem,))] if tasks else []
        res = pl.pallas_call(
            wrapped, name=name,
            grid_spec=pltpu.PrefetchScalarGridSpec(
                num_scalar_prefetch=num_prefetch, grid=tuple(grid),
                in_specs=in_specs + [ANY_SPEC] * len(c_ops),
                out_specs=out_specs + [ANY_SPEC] * len(c_outs),
                scratch_shapes=scratch_shapes + sems),
            out_shape=out_shape + c_outs,
            input_output_aliases=aliases,
            compiler_params=_cparams(nax, PEER_SET_COLLECTIVE_ID[peer_set] if peer_set else None),
        )(*operands, *c_ops)
        co = n_out
        for t, souts in zip(tasks, t_outs):
            t.finish(res[co:co + len(souts)])
            co += len(souts)
        for fn in after:
            fn()
        return res[0] if single else list(res[:n_out])

    return run


def _comm_call(name):
    def body(o_ref):
        o_ref[...] = jnp.zeros_like(o_ref)

    _pcall(body, name=name, grid=(1,), in_specs=[], out_specs=_spec((8, 128), lambda i: (0, 0)),
           out_shape=_sds((8, 128), F32))()


def _gemm(name, terms, grid, outs, acc_shape, extras=(), epilogue=None):
    kinds = [t[4] for t in terms]
    nt, ne, no = len(terms), len(extras), len(outs)
    nred = grid[-1]
    nax = len(grid)

    def body(*refs):
        trefs = refs[:2 * nt]
        erefs = refs[2 * nt:2 * nt + ne]
        orefs = refs[2 * nt + ne:2 * nt + ne + no]
        ids = [pl.program_id(k) for k in range(nax)]
        tot = None
        for t in range(nt):
            d = _dot(trefs[2 * t][...], trefs[2 * t + 1][...], kinds[t])
            tot = d if tot is None else tot + d

        def finish(acc):
            if epilogue is None:
                orefs[0][...] = acc.astype(orefs[0].dtype)
            else:
                epilogue(acc, erefs, orefs, ids)

        if nred == 1:
            finish(tot)
        else:
            acc_ref = refs[-1]
            r = ids[-1]

            @pl.when(r == 0)
            def _():
                acc_ref[...] = tot

            @pl.when(r > 0)
            def _():
                acc_ref[...] += tot

            @pl.when(r == nred - 1)
            def _():
                finish(acc_ref[...])

    operands, in_specs = [], []
    for a, a_spec, b, b_spec, _ in terms:
        operands += [a, b]
        in_specs += [a_spec, b_spec]
    for e, e_spec in extras:
        operands.append(e)
        in_specs.append(e_spec)
    scratch = [pltpu.VMEM(tuple(acc_shape), F32)] if nred > 1 else []
    return _pcall(body, name=name, grid=tuple(grid), in_specs=in_specs, out_specs=[o[1] for o in outs],
                  out_shape=[o[0] for o in outs], scratch_shapes=scratch)(*operands)


def _rowwise(name, fn, ins, outs, grid):
    ni = len(ins)
    nax = len(grid)

    def body(*refs):
        ids = [pl.program_id(k) for k in range(nax)]
        fn(refs[:ni], refs[ni:], ids)

    return _pcall(body, name=name, grid=tuple(grid), in_specs=[i[1] for i in ins],
                  out_specs=[o[1] for o in outs], out_shape=[o[0] for o in outs])(*[i[0] for i in ins])


def _ffn_up(name, h, w1buf, w1_idx, w3buf, w3_idx):
    T = h.shape[0]
    tm = min(FFN_ROW_TILE, T)

    def body(h_ref, w1_ref, w3_ref, a_ref, b_ref, s_ref):
        hv = h_ref[...]
        a = _dot(hv, w1_ref[...], "nt")
        b = _dot(hv, w3_ref[...], "nt")
        a_ref[...] = a.astype(BF16)
        b_ref[...] = b.astype(BF16)
        s_ref[...] = ((a * _sigmoid(a)) * b).astype(BF16)

    blk = _spec((tm, D_FF), lambda i: (i, 0))
    return _pcall(
        body, name=name, grid=(T // tm,),
        in_specs=[_spec((tm, D), lambda i: (i, 0)),
                  _spec((N_CHIPS, None, FF_BLK, D), lambda i: (0, w1_idx, 0, 0)),
                  _spec((N_CHIPS, None, FF_BLK, D), lambda i: (0, w3_idx, 0, 0))],
        out_specs=[blk, blk, blk],
        out_shape=[_sds((T, D_FF), BF16)] * 3,
    )(h, w1buf, w3buf)


def _ffn_down(name, s, wrow2, w2_idx, x_res, g_next=None):
    T = x_res.shape[0]
    tm = min(ROW_TILE, T)
    row = lambda i, j, r: (i, 0)

    def epilogue(acc, erefs, orefs, ids):
        xo = erefs[0][...] + 0.5 * acc
        orefs[0][...] = xo
        if g_next is not None:
            orefs[1][...] = _rms_fwd(xo, erefs[1][...]).astype(BF16)

    extras = [(x_res, _spec((tm, D), row))]
    outs = [(_sds((T, D), F32), _spec((tm, D), row))]
    if g_next is not None:
        extras.append((g_next, _spec((1, D), lambda i, j, r: (0, 0))))
        outs.append((_sds((T, D), BF16), _spec((tm, D), row)))
    return _gemm(
        name,
        [(s, _spec((tm, D_FF), row),
          wrow2, _spec((N_CHIPS, None, FF_BLK, D), lambda i, j, r: (0, w2_idx, 0, 0)), "nn")],
        (T // tm, 1, 1), outs, (tm, D), extras, epilogue)


def _ffn_bwd_mid(name, dx, wrow2, w2_idx, a, b):
    T = dx.shape[0]
    tm = min(FFN_ROW_TILE, T)

    def body(dx_ref, w2_ref, a_ref, b_ref, dab_ref):
        ds = _dot(0.5 * dx_ref[...], w2_ref[...], "nt")
        av = a_ref[...].astype(F32)
        sg = _sigmoid(av)
        dab_ref[0] = (ds * b_ref[...].astype(F32) * (sg * (1.0 + av * (1.0 - sg)))).astype(BF16)
        dab_ref[1] = (ds * (av * sg)).astype(BF16)

    blk = _spec((tm, D_FF), lambda i: (i, 0))
    return _pcall(
        body, name=name, grid=(T // tm,),
        in_specs=[_spec((tm, D), lambda i: (i, 0)),
                  _spec((N_CHIPS, None, FF_BLK, D), lambda i: (0, w2_idx, 0, 0)),
                  blk, blk],
        out_specs=_spec((2, tm, D_FF), lambda i: (0, i, 0)),
        out_shape=_sds((2, T, D_FF), BF16),
    )(dx, wrow2, a, b)


def _rms_bwd_epilogue(acc, erefs, orefs, ids):
    dx, dgp = _rms_bwd(erefs[0][...], erefs[1][...], acc)
    orefs[0][...] = dx + erefs[2][...]
    _accumulate(orefs[1], dgp, ids[0] == 0)


def _rms_bwd_io(x, g, dres, T, tm):
    row = lambda i, j, r: (i, 0)
    vec = lambda i, j, r: (0, 0)
    extras = [(x, _spec((tm, D), row)), (g, _spec((1, D), vec)), (dres, _spec((tm, D), row))]
    outs = [(_sds((T, D), F32), _spec((tm, D), row)), (_sds((1, D), F32), _spec((1, D), vec))]
    return extras, outs


def _ffn_bwd(tag, dx_out, h, a, b, s, w1buf, w1_idx, w3buf, w3_idx, wrow2, w2_idx, x_in, g, big):
    T = dx_out.shape[0]
    dab = _ffn_bwd_mid(tag + "_bwd_mid", dx_out, wrow2, w2_idx, a, b)

    def half_scale(acc, erefs, orefs, ids):
        orefs[0][...] = (0.5 * acc).astype(orefs[0].dtype)

    dw_grid = (D_FF // DW_BLK, 1, 1)
    dw_out = [(_sds((D_FF, D), GRAD_WIRE_DTYPE), _spec((DW_BLK, D), lambda j, n, r: (j, 0)))]
    tokens = _spec((T, D), lambda j, n, r: (0, 0))
    big[tag + "_w2"] = _gemm(
        tag + "_dw2", [(s, _spec((T, DW_BLK), lambda j, n, r: (0, j)), dx_out, tokens, "tn")],
        dw_grid, dw_out, (DW_BLK, D), (), half_scale)[0].reshape(1, N_CHIPS, FF_BLK, D)
    for widx, wname in ((0, "_w1"), (1, "_w3")):
        big[tag + wname] = _gemm(
            tag + "_d" + wname[1:],
            [(dab, _spec((None, T, DW_BLK), functools.partial(lambda w, j, n, r: (w, 0, j), widx)), h, tokens, "tn")],
            dw_grid, dw_out, (DW_BLK, D))[0].reshape(1, N_CHIPS, FF_BLK, D)
    tm = min(FFN_ROW_TILE, T)
    extras, outs = _rms_bwd_io(x_in, g, dx_out, T, tm)
    whole = lambda idx: _spec((N_CHIPS, None, FF_BLK, D), lambda i, j, r: (0, idx, 0, 0))
    dx_in, dg = _gemm(
        tag + "_dh",
        [(dab, _spec((None, tm, D_FF), lambda i, j, r: (0, i, 0)), w1buf, whole(w1_idx), "nn"),
         (dab, _spec((None, tm, D_FF), lambda i, j, r: (1, i, 0)), w3buf, whole(w3_idx), "nn")],
        (T // tm, 1, 1), outs, (tm, D), extras, _rms_bwd_epilogue)
    return dx_in, dg


def _proj_sq(name, a, wsq, idx, kind, out_dtype=F32, extras=(), epilogue=None, outs=None):
    M = a.shape[0]
    tm = min(ROW_TILE, M)
    if outs is None:
        outs = [(_sds((M, D), out_dtype), _spec((tm, D), lambda i, j, r: (i, 0)))]
    return _gemm(
        name,
        [(a, _spec((tm, D), lambda i, j, r: (i, 0)),
          wsq, _spec((N_CHIPS, None, SQ_BLK, D), lambda i, j, r: (0, idx, 0, 0)), kind)],
        (M // tm, 1, 1), outs, (tm, D), extras, epilogue)


def _dw_sq(name, a, b):
    M = a.shape[0]
    tk = M
    whole = _gemm(
        name,
        [(a, _spec((tk, D), lambda i, j, r: (r, 0)), b, _spec((tk, D), lambda i, j, r: (r, 0)), "tn")],
        (1, 1, M // tk),
        [(_sds((D, D), GRAD_WIRE_DTYPE), _spec((D, D), lambda i, j, r: (0, 0)))],
        (D, D))[0]
    return whole.reshape(N_CHIPS, SQ_BLK, D)


def _retention_constants(T):
    pos = jnp.arange(T, dtype=F32)
    inv_freq = ROPE_BASE ** (-jnp.arange(0, RET_DK, 2, dtype=F32) / RET_DK)
    ang = pos[:, None] * inv_freq[None, :]
    cosf = jnp.concatenate([jnp.cos(ang), jnp.cos(ang)], axis=1)
    sins = jnp.concatenate([-jnp.sin(ang), jnp.sin(ang)], axis=1)
    lg = jnp.log(1.0 - 2.0 ** (-5.0 - jnp.arange(RET_HEADS, dtype=F32)))
    p = jnp.arange(CHUNK, dtype=F32)
    rel = p[:, None] - p[None, :]
    dmat = jnp.where(rel[None] >= 0, jnp.exp(rel[None] * lg[:, None, None]), 0.0)
    kd = jnp.exp((CHUNK - 1.0 - p)[None, :] * lg[:, None])[:, :, None]
    qd = jnp.exp((p + 1.0)[None, :] * lg[:, None])[:, :, None]
    cd = jnp.exp(CHUNK * lg)[:, None, None]
    return cosf, sins, dmat, kd, qd, cd


def _rot(t, cosv, sinv):
    return t * cosv + pltpu.roll(t, RET_DK // 2, 1) * sinv


def _unrot(t, cosv, sinv):
    return t * cosv - pltpu.roll(t, RET_DK // 2, 1) * sinv


def _ret_const_specs(cm):
    whole = lambda shape: _spec(shape, lambda c: (0,) * len(shape))
    return [
        _spec((CHUNK, RET_DK), lambda c: (cm(c), 0)),
        _spec((CHUNK, RET_DK), lambda c: (cm(c), 0)),
        whole((RET_HEADS, CHUNK, CHUNK)), whole((RET_HEADS, CHUNK, 1)), whole((RET_HEADS, CHUNK, 1)),
        whole((RET_HEADS, 1, 1)),
    ]


def _head(h, width):
    return slice(h * width, (h + 1) * width)


def _ret_fwd(u, consts, ret_gn):
    T = u.shape[0]
    nC = T // CHUNK
    kscale = RET_DK ** -0.5

    def body(q_ref, k_ref, v_ref, g_ref, cos_ref, sin_ref, dm_ref, kd_ref, qd_ref, cd_ref, gn_ref,
             qr_ref, kr_ref, ret_ref, yr_ref, st_ref, state):
        @pl.when(pl.program_id(0) == 0)
        def _():
            state[...] = jnp.zeros_like(state)

        cosv, sinv = cos_ref[...], sin_ref[...]
        for h in range(RET_HEADS):
            hk, hv = _head(h, RET_DK), _head(h, RET_DV)
            q = _rot(q_ref[:, hk], cosv, sinv)
            k = _rot(k_ref[:, hk], cosv, sinv) * kscale
            v = v_ref[:, hv]
            qr_ref[:, hk] = q
            kr_ref[:, hk] = k
            prev = state[h]
            st_ref[h] = prev
            s = _dot(q, k, "nt") * dm_ref[h]
            ret = _dot(s, v, "nn") + _dot(q, prev, "nn") * qd_ref[h]
            state[h] = cd_ref[h] * prev + _dot(k * kd_ref[h], v, "tn")
            ret_ref[:, hv] = ret
            mu = jnp.mean(ret, axis=-1, keepdims=True)
            xc = ret - mu
            yn = xc * lax.rsqrt(jnp.mean(xc * xc, axis=-1, keepdims=True) + EPS)
            g = g_ref[:, hv]
            yr_ref[:, hv] = ((g * _sigmoid(g)) * (yn * gn_ref[:, hv])).astype(BF16)

    cm = lambda c: c
    qk_w, v_w = RET_HEADS * RET_DK, RET_HEADS * RET_DV
    in_specs = [
        _spec((CHUNK, qk_w), lambda c: (c, 0)), _spec((CHUNK, qk_w), lambda c: (c, 1)),
        _spec((CHUNK, v_w), lambda c: (c, 1)), _spec((CHUNK, v_w), lambda c: (c, 2)),
    ] + _ret_const_specs(cm) + [_spec((1, v_w), lambda c: (0, 0))]
    qk_out = _spec((CHUNK, qk_w), lambda c: (c, 0))
    v_out = _spec((CHUNK, v_w), lambda c: (c, 0))
    return _pcall(
        body, name="ret_fwd", grid=(nC,),
        in_specs=in_specs,
        out_specs=[qk_out, qk_out, v_out, v_out,
                   _spec((RET_HEADS, None, RET_DK, RET_DV), lambda c: (0, c, 0, 0))],
        out_shape=[_sds((T, qk_w), F32), _sds((T, qk_w), F32), _sds((T, v_w), F32), _sds((T, v_w), BF16),
                   _sds((RET_HEADS, nC, RET_DK, RET_DV), F32)],
        scratch_shapes=[pltpu.VMEM((RET_HEADS, RET_DK, RET_DV), F32)],
    )(u, u, u, u, *consts, ret_gn)


def _ret_bwd(dyr, ret, u, qr, kr, states, consts, ret_gn):
    T = u.shape[0]
    nC = T // CHUNK
    kscale = RET_DK ** -0.5

    def body(dyr_ref, ret_ref, g_ref, q_ref, k_ref, v_ref, st_ref,
             cos_ref, sin_ref, dm_ref, kd_ref, qd_ref, cd_ref, gn_ref,
             dq_ref, dk_ref, dv_ref, dg_ref, dgn_ref, gstate):
        first = pl.program_id(0) == 0

        @pl.when(first)
        def _():
            gstate[...] = jnp.zeros_like(gstate)

        cosv, sinv = cos_ref[...], sin_ref[...]
        dgn_parts = []
        for h in range(RET_HEADS):
            hk, hv = _head(h, RET_DK), _head(h, RET_DV)
            ret = ret_ref[:, hv]
            mu = jnp.mean(ret, axis=-1, keepdims=True)
            xc = ret - mu
            rs = lax.rsqrt(jnp.mean(xc * xc, axis=-1, keepdims=True) + EPS)
            yn = xc * rs
            gn = gn_ref[:, hv]
            g = g_ref[:, hv]
            sg = _sigmoid(g)
            dyr_v = dyr_ref[:, hv]
            dretn = dyr_v * (g * sg)
            dg_ref[:, hv] = (dyr_v * (yn * gn) * (sg * (1.0 + g * (1.0 - sg)))).astype(BF16)
            dgn_parts.append(jnp.sum(dretn * yn, axis=0, keepdims=True))
            dyn = dretn * gn
            d_o = rs * (dyn - jnp.mean(dyn, axis=-1, keepdims=True)
                        - yn * jnp.mean(dyn * yn, axis=-1, keepdims=True))

            q, k, v = q_ref[:, hk], k_ref[:, hk], v_ref[:, hv]
            dmat, kd, qd = dm_ref[h], kd_ref[h], qd_ref[h]
            prev = st_ref[h]
            gnext = gstate[h]
            s = _dot(q, k, "nt") * dmat
            ds = _dot(d_o, v, "nt") * dmat
            doq = d_o * qd
            dq = _dot(ds, k, "nn") + _dot(doq, prev, "nt")
            dk = _dot(ds, q, "tn") + _dot(v, gnext, "nt") * kd
            dv = _dot(s, d_o, "tn") + _dot(k * kd, gnext, "nn")
            gstate[h] = cd_ref[h] * gnext + _dot(q, doq, "tn")
            dq_ref[:, hk] = _unrot(dq, cosv, sinv).astype(BF16)
            dk_ref[:, hk] = _unrot(dk * kscale, cosv, sinv).astype(BF16)
            dv_ref[:, hv] = dv.astype(BF16)
        _accumulate(dgn_ref, jnp.concatenate(dgn_parts, axis=1), first)

    cm = lambda c: nC - 1 - c
    qk_w, v_w = RET_HEADS * RET_DK, RET_HEADS * RET_DV
    vspec = lambda blk: _spec((CHUNK, v_w), lambda c: (cm(c), blk))
    qspec = _spec((CHUNK, qk_w), lambda c: (cm(c), 0))
    in_specs = [vspec(0), vspec(0), vspec(2), qspec, qspec, vspec(1),
                _spec((RET_HEADS, None, RET_DK, RET_DV), lambda c: (0, cm(c), 0, 0)),
                ] + _ret_const_specs(cm) + [_spec((1, v_w), lambda c: (0, 0))]
    return _pcall(
        body, name="ret_bwd", grid=(nC,),
        in_specs=in_specs,
        out_specs=[qspec, qspec, vspec(0), vspec(0), _spec((1, v_w), lambda c: (0, 0))],
        out_shape=[_sds((T, qk_w), BF16), _sds((T, qk_w), BF16), _sds((T, v_w), BF16), _sds((T, v_w), BF16),
                   _sds((1, v_w), F32)],
        scratch_shapes=[pltpu.VMEM((RET_HEADS, RET_DK, RET_DV), F32)],
    )(dyr, ret, u, qr, kr, u, states, *consts, ret_gn)


def _shift_down(x, s):
    rows = lax.broadcasted_iota(jnp.int32, x.shape, 0)
    return jnp.where(rows >= s, pltpu.roll(x, s, 0), 0.0)


def _shift_up(x, s):
    n = x.shape[0]
    rows = lax.broadcasted_iota(jnp.int32, x.shape, 0)
    return jnp.where(rows < n - s, pltpu.roll(x, n - s, 0), 0.0)


def _lru_specs(T):
    col = lambda off: _spec((T, LRU_BLOCK), lambda g: (0, off + g))
    vec = _spec((1, LRU_BLOCK), lambda g: (0, g))
    wblk = _spec((None, LRU_BLOCK, LRU_BLOCK), lambda g: (g, 0, 0))
    cw = _spec((CONV_TAPS, LRU_BLOCK), lambda g: (0, g))
    return col, vec, wblk, cw


def _lru_gates_fwd(u, conv_w, conv_b, w_r, b_r, w_i, b_i, lam):
    T = u.shape[0]
    col, vec, wblk, cw = _lru_specs(T)

    def body(x_ref, cw_ref, cb_ref, wr_ref, br_ref, wi_ref, bi_ref, lam_ref,
             xc_ref, r_ref, i_ref, a_ref, bx_ref):
        x = x_ref[...]
        w = cw_ref[...]
        xc = (_shift_down(x, 3) * w[0:1] + _shift_down(x, 2) * w[1:2] + _shift_down(x, 1) * w[2:3]
              + x * w[3:4] + cb_ref[...])
        r = _sigmoid(_dot(xc, wr_ref[...], "nn") + br_ref[...])
        i = _sigmoid(_dot(xc, wi_ref[...], "nn") + bi_ref[...])
        la = (-LRU_C) * r * _softplus(-lam_ref[...])
        xc_ref[...] = xc
        r_ref[...] = r
        i_ref[...] = i
        a_ref[...] = jnp.exp(la)
        bx_ref[...] = jnp.sqrt(-_expm1(2.0 * la)) * (i * xc)

    out = col(0)
    return _pcall(
        body, name="lru_gates_fwd", grid=(LRU_BLOCKS,),
        in_specs=[col(24), cw, vec, wblk, vec, wblk, vec, vec],
        out_specs=[out] * 5,
        out_shape=[_sds((T, D), F32)] * 5,
    )(u, conv_w, conv_b, w_r, b_r, w_i, b_i, lam)


def _lru_scan(name, a3, b3, reverse):
    T = a3.shape[0]
    nt = T // SCAN_TILE
    unroll = 8

    def body(a_ref, b_ref, o_ref, carry):
        @pl.when(pl.program_id(0) == 0)
        def _():
            carry[...] = jnp.zeros_like(carry)

        if not reverse:
            def step(t, h):
                h = a_ref[t] * h + b_ref[t]
                o_ref[t] = h
                return h
        else:
            def step(k, c):
                t = SCAN_TILE - 1 - k
                l = b_ref[t] + c
                o_ref[t] = l
                return a_ref[t] * l
        carry[...] = lax.fori_loop(0, SCAN_TILE, step, carry[...], unroll=unroll)

    idx = (lambda i: (nt - 1 - i, 0, 0)) if reverse else (lambda i: (i, 0, 0))
    blk = _spec((SCAN_TILE, LRU_BLOCKS, LRU_BLOCK), idx)
    return _pcall(
        body, name=name, grid=(nt,),
        in_specs=[blk, blk], out_specs=blk,
        out_shape=_sds((T, LRU_BLOCKS, LRU_BLOCK), F32),
        scratch_shapes=[pltpu.VMEM((LRU_BLOCKS, LRU_BLOCK), F32)],
    )(a3, b3)


def _lru_gates_bwd(lmb, hl, a, r, i, xc, u, conv_w, w_r, w_i, lam):
    T = u.shape[0]
    col, vec, wblk, cw = _lru_specs(T)

    def body(l_ref, h_ref, a_ref, r_ref, i_ref, xc_ref, x_ref, cw_ref, wr_ref, wi_ref, lam_ref,
             dx_ref, dwr_ref, dwi_ref, dvec_ref, dcw_ref):
        l = l_ref[...]
        av, rv, iv, xc = a_ref[...], r_ref[...], i_ref[...], xc_ref[...]
        lam_v = lam_ref[...]
        sp = _softplus(-lam_v)
        la = (-LRU_C) * rv * sp
        mult = jnp.sqrt(-_expm1(2.0 * la))
        da = l * _shift_down(h_ref[...], 1)
        dmult = l * (iv * xc)
        di = l * mult * xc
        dxc = l * mult * iv
        dla = da * av - dmult * (av * av) / mult
        dzr = (dla * ((-LRU_C) * sp)) * rv * (1.0 - rv)
        dzi = di * iv * (1.0 - iv)
        dsp = jnp.sum(dla * ((-LRU_C) * rv), axis=0, keepdims=True)
        dlam = dsp * (-_sigmoid(-lam_v))
        dwr_ref[...] = _dot(xc, dzr, "tn")
        dwi_ref[...] = _dot(xc, dzi, "tn")
        dxc = dxc + _dot(dzr, wr_ref[...], "nt") + _dot(dzi, wi_ref[...], "nt")
        x = x_ref[...]
        w = cw_ref[...]
        dx = (dxc * w[3:4] + _shift_up(dxc, 1) * w[2:3] + _shift_up(dxc, 2) * w[1:2]
              + _shift_up(dxc, 3) * w[0:1])
        dx_ref[...] = dx.astype(BF16)
        dvec_ref[...] = jnp.concatenate(
            [jnp.sum(dzr, axis=0, keepdims=True), jnp.sum(dzi, axis=0, keepdims=True), dlam,
             jnp.sum(dxc, axis=0, keepdims=True)], axis=0)
        dcw_ref[...] = jnp.concatenate(
            [jnp.sum(dxc * _shift_down(x, 3 - tap), axis=0, keepdims=True) if tap < 3
             else jnp.sum(dxc * x, axis=0, keepdims=True) for tap in range(CONV_TAPS)], axis=0)

    c0 = col(0)
    return _pcall(
        body, name="lru_gates_bwd", grid=(LRU_BLOCKS,),
        in_specs=[c0, c0, c0, c0, c0, c0, col(24), cw, wblk, wblk, vec],
        out_specs=[c0, wblk, wblk, cw, cw],
        out_shape=[_sds((T, D), BF16), _sds((LRU_BLOCKS, LRU_BLOCK, LRU_BLOCK), F32),
                   _sds((LRU_BLOCKS, LRU_BLOCK, LRU_BLOCK), F32), _sds((4, D), F32), _sds((CONV_TAPS, D), F32)],
    )(lmb, hl, a, r, i, xc, u, conv_w, w_r, w_i, lam)


def _xattn_probs(q, k):
    sc = _dot(q, k, "nt") * (X_HD ** -0.5)
    e = jnp.exp(sc - jnp.max(sc, axis=-1, keepdims=True))
    return e / jnp.sum(e, axis=-1, keepdims=True)


def _xattn_fwd(xq, xk, xv):
    T = xq.shape[0]
    tq = ROW_TILE
    M = xk.shape[0]

    def body(q_ref, k_ref, v_ref, o_ref):
        p = _xattn_probs(q_ref[...], k_ref[...])
        o_ref[...] = _dot(p, v_ref[...], "nn").astype(BF16)

    qs = _spec((tq, X_HD), lambda h, i: (i, h))
    kv = _spec((M, X_HD), lambda h, i: (0, h))
    return _pcall(
        body, name="xattn_fwd", grid=(X_HEADS, T // tq),
        in_specs=[qs, kv, kv], out_specs=qs, out_shape=_sds((T, D), BF16),
    )(xq, xk, xv)


def _xattn_bwd(xq, xk, xv, dxo):
    T = xq.shape[0]
    tq = ROW_TILE
    M = xk.shape[0]

    def body(q_ref, k_ref, v_ref, do_ref, dq_ref, dk_ref, dv_ref):
        first = pl.program_id(1) == 0
        q, k, v, do = q_ref[...], k_ref[...], v_ref[...], do_ref[...]
        p = _xattn_probs(q, k)
        dp = _dot(do, v, "nt")
        ds = p * (dp - jnp.sum(dp * p, axis=-1, keepdims=True)) * (X_HD ** -0.5)
        dq_ref[...] = _dot(ds, k, "nn").astype(BF16)
        _accumulate(dk_ref, _dot(ds, q, "tn"), first)
        _accumulate(dv_ref, _dot(p, do, "tn"), first)

    qs = _spec((tq, X_HD), lambda h, i: (i, h))
    kv = _spec((M, X_HD), lambda h, i: (0, h))
    return _pcall(
        body, name="xattn_bwd", grid=(X_HEADS, T // tq),
        in_specs=[qs, kv, kv, qs], out_specs=[qs, kv, kv],
        out_shape=[_sds((T, D), BF16), _sds((M, D), F32), _sds((M, D), F32)],
    )(xq, xk, xv, dxo)


def _final_loss(x, g, tgt):
    T = x.shape[0]
    tm = ROW_TILE

    def fn(irefs, orefs, ids):
        xv, gv = irefs[0][...], irefs[1][...]
        err = _rms_fwd(xv, gv) - irefs[2][...]
        lp = 0.5 * jnp.sum(jnp.mean(err * err, axis=-1, keepdims=True), axis=0, keepdims=True)
        first = ids[0] == 0
        _accumulate(orefs[0], jnp.broadcast_to(lp, (1, 128)), first)
        dx, dgp = _rms_bwd(xv, gv, err * (1.0 / D))
        orefs[1][...] = dx
        _accumulate(orefs[2], dgp, first)

    row = _spec((tm, D), lambda i: (i, 0))
    vec = _spec((1, D), lambda i: (0, 0))
    return _rowwise(
        "final_loss", fn, [(x, row), (g, vec), (tgt, row)],
        [(_sds((1, 128), F32), _spec((1, 128), lambda i: (0, 0))), (_sds((T, D), F32), row),
         (_sds((1, D), F32), vec)],
        (T // tm,))


def _adamw(name, w, g, m, v):
    R, C = w.shape
    tr = R
    for cand in (512, 352, 256):
        if R % cand == 0:
            tr = cand
            break

    def fn(irefs, orefs, ids):
        delta, mn, vn = _adamw_update(*(r[...] for r in irefs))
        orefs[0][...] = delta
        orefs[1][...] = mn
        orefs[2][...] = vn

    blk = _spec((tr, C), lambda i: (i, 0))
    return _rowwise(name, fn, [(w, blk), (g, blk), (m, blk), (v, blk)],
                    [(_sds((R, C), F32), blk)] * 3, (R // tr,))


def _adamw_update(wv, gv, mv, vv):
    c1 = 1.0 - ADAM_B1 ** ADAM_STEP
    c2 = 1.0 - ADAM_B2 ** ADAM_STEP
    mn = ADAM_B1 * mv + (1.0 - ADAM_B1) * gv
    vn = ADAM_B2 * vv + (1.0 - ADAM_B2) * (gv * gv)
    delta = -ADAM_LR * ((mn / c1) / (jnp.sqrt(vn / c2) + ADAM_EPS) + ADAM_WD * wv)
    return delta, mn, vn


def _adamw_halves(name, w, mine, theirs, widx, m, v, core):
    R, C = w.shape
    H = R // 2
    tr = H
    while tr * C * 4 > (1 << 20) and tr % 16 == 0:
        tr //= 2
    nb = H // tr

    def body(core_ref, w_ref, mine_ref, theirs_ref, m_ref, v_ref, g_out, d_out, m_out, v_out):
        gv = jnp.where(pl.program_id(0) == core_ref[0], mine_ref[...], theirs_ref[...])
        delta, mn, vn = _adamw_update(w_ref[...], gv, m_ref[...], v_ref[...])
        g_out[...] = gv
        d_out[...] = delta
        m_out[...] = mn
        v_out[...] = vn

    full = pl.BlockSpec((tr, C), lambda h, i, core_ref: (h * nb + i, 0))
    mine_spec = pl.BlockSpec((None, tr, C), lambda h, i, core_ref: (widx, jnp.where(h == core_ref[0], i, 0), 0))
    theirs_spec = pl.BlockSpec((None, tr, C), lambda h, i, core_ref: (widx, jnp.where(h == core_ref[0], 0, i), 0))
    return _pcall(
        body, name=name, grid=(2, nb), num_prefetch=1,
        in_specs=[full, mine_spec, theirs_spec, full, full], out_specs=[full] * 4,
        out_shape=[_sds((R, C), F32)] * 4,
    )(core, w, mine, theirs, m, v)


def _rmsnorm(name, x, g):
    M = x.shape[0]
    tm = min(ROW_TILE, M)

    def fn(irefs, orefs, ids):
        orefs[0][...] = _rms_fwd(irefs[0][...], irefs[1][...]).astype(BF16)

    row = _spec((tm, D), lambda i: (i, 0))
    return _rowwise(name, fn, [(x, row), (g, _spec((1, D), lambda i: (0, 0)))],
                    [(_sds((M, D), BF16), row)], (M // tm,))[0]


WEIGHT_AT = {
    "ffn1_w1": ("col1", 0), "ffn1_w3": ("col1", 1), "ffn1_w2": ("row2a", 0),
    "w_ret_o": ("sqA", 0), "w_lru_o": ("sqA", 1), "w_out": ("sqA", 2),
    "w_xq": ("sqB", 0), "w_xk": ("sqB", 1), "w_xv": ("sqC", 0), "w_xo": ("sqC", 1),
    "ffn2_w1": ("col2a", 0), "ffn2_w3": ("col2b", 0), "ffn2_w2": ("row2b", 0),
}


def _local_step(x, mem, tgt, gw, sm, big):
    T = x.shape[0]
    tm = ROW_TILE

    def wt(name):
        key, idx = WEIGHT_AT[name]
        return gw[key], idx

    row3 = lambda i, j, r: (i, 0)
    vec3 = lambda i, j, r: (0, 0)
    rowD = _spec((tm, D), row3)
    vecD = _spec((1, D), vec3)

    def residual_norm(acc, erefs, orefs, ids):
        xo = erefs[0][...] + acc
        orefs[0][...] = xo
        orefs[1][...] = _rms_fwd(xo, erefs[1][...]).astype(BF16)

    def res_norm_io(x_res, g):
        return ([(x_res, rowD), (g, vecD)],
                [(_sds((T, D), F32), rowD), (_sds((T, D), BF16), rowD)])

    h1 = _rmsnorm("ffn1_norm", x, sm["ffn1_norm"])
    a1, b1, s1 = _ffn_up("ffn1_up", h1, *wt("ffn1_w1"), *wt("ffn1_w3"))
    x1, h2 = _ffn_down("ffn1_down", s1, *wt("ffn1_w2"), x, sm["mix_norm"])

    tw = min(WIDE_ROW_TILE, T)
    wideD = _spec((tw, D), row3)
    u = _gemm(
        "mix_in",
        [(h2, wideD, gw["win"], _spec((None, None, IN_BLK, D), lambda i, j, r: (j, 0, 0, 0)), "nt")],
        (T // tw, N_CHIPS, 1),
        [(_sds((T, 5120), F32), _spec((tw, IN_BLK), lambda i, j, r: (i, j)))], (tw, IN_BLK))[0]

    consts = _retention_constants(T)
    qr, kr, ret, yr, states = _ret_fwd(u, consts, sm["ret_gn"])

    conv_w = gw["conv"][:, 0].transpose(1, 0, 2).reshape(CONV_TAPS, D)
    xc, rg, ig, av, bx = _lru_gates_fwd(u, conv_w, sm["conv_b"], sm["w_rgate"], sm["b_rgate"],
                                        sm["w_igate"], sm["b_igate"], sm["lru_lambda"])
    a3 = av.reshape(T, LRU_BLOCKS, LRU_BLOCK)
    hl = _lru_scan("lru_scan_fwd", a3, bx.reshape(T, LRU_BLOCKS, LRU_BLOCK), False).reshape(T, D)

    row1 = _spec((tm, D), lambda i: (i, 0))
    glru1 = _spec((tm, D), lambda i: (i, 4))

    def lru_out(irefs, orefs, ids):
        gl, _ = _gelu_and_grad(irefs[1][...])
        orefs[0][...] = (irefs[0][...] * gl).astype(BF16)

    yl = _rowwise("lru_out", lru_out, [(hl, row1), (u, glru1)], [(_sds((T, D), BF16), row1)], (T // tm,))[0]

    def gate_epilogue(acc, erefs, orefs, ids):
        orefs[0][...] = _sigmoid(acc + erefs[0][...])

    gates = _gemm(
        "mix_gates",
        [(h2, wideD, gw["wbg"], _spec((None, None, BG_BLK, D), lambda i, j, r: (j, 0, 0, 0)), "nt")],
        (T // tw, N_CHIPS, 1),
        [(_sds((T, 2 * D), F32), _spec((tw, BG_BLK), lambda i, j, r: (i, j)))], (tw, BG_BLK),
        [(sm["b_branch_gate"], _spec((1, BG_BLK), lambda i, j, r: (0, j)))], gate_epilogue)[0]

    y_ret = _proj_sq("y_ret", yr, *wt("w_ret_o"), "nn")[0]

    def merge_epilogue(acc, erefs, orefs, ids):
        orefs[0][...] = acc
        orefs[1][...] = (erefs[0][...] * erefs[2][...] + erefs[1][...] * acc).astype(BF16)

    y_lru, merged = _proj_sq(
        "y_lru", yl, *wt("w_lru_o"), "nn",
        extras=[(gates, _spec((tm, D), lambda i, j, r: (i, 0))), (gates, _spec((tm, D), lambda i, j, r: (i, 1))),
                (y_ret, rowD)],
        epilogue=merge_epilogue,
        outs=[(_sds((T, D), F32), rowD), (_sds((T, D), BF16), rowD)])

    ex, ou = res_norm_io(x1, sm["xattn_norm"])
    x2, hq = _proj_sq("mix_out", merged, *wt("w_out"), "nn", extras=ex, epilogue=residual_norm, outs=ou)

    m = _rmsnorm("mem_norm", mem, sm["mem_norm"])
    xq = _proj_sq("xq", hq, *wt("w_xq"), "nn", BF16)[0]
    xk = _proj_sq("xk", m, *wt("w_xk"), "nn", BF16)[0]
    xv = _proj_sq("xv", m, *wt("w_xv"), "nn", BF16)[0]
    xo = _xattn_fwd(xq, xk, xv)
    ex, ou = res_norm_io(x2, sm["ffn2_norm"])
    x3, h3 = _proj_sq("xattn_out", xo, *wt("w_xo"), "nn", extras=ex, epilogue=residual_norm, outs=ou)

    a2, b2, s2 = _ffn_up("ffn2_up", h3, *wt("ffn2_w1"), *wt("ffn2_w3"))
    x4 = _ffn_down("ffn2_down", s2, *wt("ffn2_w2"), x3)[0]
    loss, dx4, dg_final = _final_loss(x4, sm["final_norm"], tgt)

    dx3, dg_ffn2 = _ffn_bwd("ffn2", dx4, h3, a2, b2, s2, *wt("ffn2_w1"), *wt("ffn2_w3"),
                            *wt("ffn2_w2"), x3, sm["ffn2_norm"], big)

    dxo = _proj_sq("d_xo", dx3, *wt("w_xo"), "nt", BF16)[0]
    big["w_xo"] = _dw_sq("dw_xo", xo, dx3)[None]
    dxq, dxk, dxv = _xattn_bwd(xq, xk, xv, dxo)
    big["w_xq"] = _dw_sq("dw_xq", hq, dxq)[None]
    ex, ou = _rms_bwd_io(x2, sm["xattn_norm"], dx3, T, tm)
    dx2, dg_xattn = _proj_sq("d_hq", dxq, *wt("w_xq"), "nt", extras=ex, epilogue=_rms_bwd_epilogue, outs=ou)
    big["w_xk"] = _dw_sq("dw_xk", m, dxk)[None]
    big["w_xv"] = _dw_sq("dw_xv", m, dxv)[None]

    M = mem.shape[0]

    def mem_norm_epilogue(acc, erefs, orefs, ids):
        _, dgp = _rms_bwd(erefs[0][...], erefs[1][...], acc)
        orefs[0][...] = dgp

    wsq_spec = lambda idx: _spec((N_CHIPS, None, SQ_BLK, D), lambda i, j, r: (0, idx, 0, 0))
    memD = _spec((M, D), row3)
    dg_mem = _gemm(
        "d_mem_norm",
        [(dxk, memD, wt("w_xk")[0], wsq_spec(wt("w_xk")[1]), "nt"),
         (dxv, memD, wt("w_xv")[0], wsq_spec(wt("w_xv")[1]), "nt")],
        (1, 1, 1), [(_sds((1, D), F32), vecD)], (M, D),
        [(mem, memD), (sm["mem_norm"], vecD)], mem_norm_epilogue)[0]

    def merged_bwd_epilogue(acc, erefs, orefs, ids):
        gr, gl, yrv, ylv = (e[...] for e in erefs)
        orefs[0][...] = (acc * gr).astype(BF16)
        orefs[1][...] = (acc * gl).astype(BF16)
        dgr = acc * yrv * gr * (1.0 - gr)
        dgl = acc * ylv * gl * (1.0 - gl)
        orefs[2][:, :D] = dgr.astype(BF16)
        orefs[2][:, D:] = dgl.astype(BF16)
        dbb = jnp.concatenate([jnp.sum(dgr, axis=0, keepdims=True), jnp.sum(dgl, axis=0, keepdims=True)], axis=1)
        _accumulate(orefs[3], dbb, ids[0] == 0)

    dy_ret, dy_lru, dgpre, db_bg = _proj_sq(
        "d_merged", dx2, *wt("w_out"), "nt",
        extras=[(gates, _spec((tm, D), lambda i, j, r: (i, 0))), (gates, _spec((tm, D), lambda i, j, r: (i, 1))),
                (y_ret, rowD), (y_lru, rowD)],
        epilogue=merged_bwd_epilogue,
        outs=[(_sds((T, D), BF16), rowD), (_sds((T, D), BF16), rowD),
              (_sds((T, 2 * D), BF16), _spec((tm, 2 * D), row3)),
              (_sds((1, 2 * D), F32), _spec((1, 2 * D), vec3))])
    big["w_branch_gate"] = _gemm(
        "dw_bg",
        [(h2, _spec((T, D), lambda j, n, r: (r, 0)), dgpre, _spec((T, BG_BLK), lambda j, n, r: (r, j)), "tn")],
        (N_CHIPS, 1, 1),
        [(_sds((N_CHIPS, D, BG_BLK), GRAD_WIRE_DTYPE), _spec((None, D, BG_BLK), lambda j, n, r: (j, 0, 0)))],
        (D, BG_BLK))[0][None]
    big["w_out"] = _dw_sq("dw_out", merged, dx2)[None]
    dyr = _proj_sq("d_yr", dy_ret, *wt("w_ret_o"), "nt")[0]
    big["w_ret_o"] = _dw_sq("dw_ret_o", yr, dy_ret)[None]
    dyl = _proj_sq("d_yl", dy_lru, *wt("w_lru_o"), "nt")[0]
    big["w_lru_o"] = _dw_sq("dw_lru_o", yl, dy_lru)[None]

    dq, dk, dv, dgr, dg_retgn = _ret_bwd(dyr, ret, u, qr, kr, states, consts, sm["ret_gn"])

    def lru_out_bwd(irefs, orefs, ids):
        gl, dgl = _gelu_and_grad(irefs[2][...])
        dyl_v = irefs[0][...]
        orefs[0][...] = dyl_v * gl
        orefs[1][...] = (dyl_v * irefs[1][...] * dgl).astype(BF16)

    dhl, dglru = _rowwise("lru_out_bwd", lru_out_bwd, [(dyl, row1), (hl, row1), (u, glru1)],
                          [(_sds((T, D), F32), row1), (_sds((T, D), BF16), row1)], (T // tm,))
    lmb = _lru_scan("lru_scan_bwd", a3, dhl.reshape(T, LRU_BLOCKS, LRU_BLOCK), True).reshape(T, D)
    dxl, dw_r, dw_i, dvec, dcw = _lru_gates_bwd(lmb, hl, av, rg, ig, xc, u, conv_w,
                                                sm["w_rgate"], sm["w_igate"], sm["lru_lambda"])

    du = jnp.concatenate([dq, dk, dv, dgr, dxl, dglru], axis=1)
    tk = T
    big["w_in"] = _gemm(
        "dw_in",
        [(h2, _spec((tk, D), lambda j, n, r: (r, 0)), du, _spec((tk, IN_BLK), lambda j, n, r: (r, j)), "tn")],
        (N_CHIPS, 1, T // tk),
        [(_sds((N_CHIPS, D, IN_BLK), GRAD_WIRE_DTYPE), _spec((None, D, IN_BLK), lambda j, n, r: (j, 0, 0)))],
        (D, IN_BLK))[0][None]
    tf = min(FFN_ROW_TILE, T)
    ex, ou = _rms_bwd_io(x1, sm["mix_norm"], dx2, T, tf)
    dx1, dg_mix = _gemm(
        "d_h2",
        [(du, _spec((tf, 5120), row3), gw["win"], _spec((N_CHIPS, None, IN_BLK, D), lambda i, j, r: (0, 0, 0, 0)), "nn"),
         (dgpre, _spec((tf, 2 * D), row3), gw["wbg"], _spec((N_CHIPS, None, BG_BLK, D), lambda i, j, r: (0, 0, 0, 0)),
          "nn")],
        (T // tf, 1, 1), ou, (tf, D), ex, _rms_bwd_epilogue)

    grad_x, dg_ffn1 = _ffn_bwd("ffn1", dx1, h1, a1, b1, s1, *wt("ffn1_w1"), *wt("ffn1_w3"),
                               *wt("ffn1_w2"), x, sm["ffn1_norm"], big)

    small = {
        "ffn1_norm": dg_ffn1, "mix_norm": dg_mix, "ret_gn": dg_retgn, "conv_b": dvec[3:4],
        "b_rgate": dvec[0:1], "b_igate": dvec[1:2], "lru_lambda": dvec[2:3], "xattn_norm": dg_xattn,
        "mem_norm": dg_mem, "ffn2_norm": dg_ffn2, "final_norm": dg_final, "b_branch_gate": db_bg,
        "conv_w": dcw, "w_rgate": dw_r, "w_igate": dw_i,
    }
    return loss, grad_x, small


ANY_SPEC = pl.BlockSpec(memory_space=pl.ANY)
VMEM_SPEC = pl.BlockSpec(memory_space=pltpu.VMEM)
N_PEER_CHIPS = N_CHIPS - 1


def _mesh_position():
    x, y, c = lax.axis_index("x"), lax.axis_index("y"), lax.axis_index("c")
    chips = [(1 - x, y), (x, 1 - y), (1 - x, 1 - y)]
    return x, y, c, chips


def _chip_index(x, y):
    return 2 * x + y


def _rows_half(ref, axis, h):
    n = ref.shape[axis] // 2
    idx = [slice(None)] * len(ref.shape)
    idx[axis] = pl.ds(pl.multiple_of(h * n, 16), n)
    return ref.at[tuple(idx)]


def _remote(src, dst, send_sem, recv_sem, device):
    return pltpu.make_async_remote_copy(src_ref=src, dst_ref=dst, send_sem=send_sem, recv_sem=recv_sem,
                                        device_id=device, device_id_type=MESH)


def _gather_chips_task(shards, split, landed, part=0, nparts=1):
    keys = list(shards)
    n = len(keys)

    def operands():
        if part:
            return [shards[k] for k in keys] + [landed[k] for k in keys]
        chip_me = _chip_index(lax.axis_index("x"), lax.axis_index("y"))
        bases = [lax.dynamic_update_slice(lax.empty((N_CHIPS,) + shards[k].shape, shards[k].dtype), shards[k][None],
                                          (chip_me,) + (0,) * shards[k].ndim) for k in keys]
        return [shards[k] for k in keys] + bases

    def my_rows(ref, c):
        rows = ref.shape[1] // (2 * nparts)
        return ref.at[:, pl.ds(pl.multiple_of((c * nparts + part) * rows, 16), rows), :]

    def make_direct(ins, outs, send_sem, recv_sem):
        x, y, c, chips = _mesh_position()
        s_me = _chip_index(x, y)
        starts, arrivals = [], []
        for g in range(n):
            for k, chip in enumerate(chips):
                sems = (send_sem(3 * g + k), recv_sem(3 * g + k))
                starts.append(functools.partial(_remote, ins[g], outs[g].at[s_me], *sems, (*chip, c)))
                got = outs[g].at[_chip_index(*chip)]
                arrivals.append(functools.partial(_remote, got, got, *sems, (*chip, c)))
        return starts, arrivals

    def axis_neighbours(x, y, c):
        flip = lambda v, f: v + f * (1 - 2 * v)
        return (flip(x, 1 - c), flip(y, c)), (flip(x, c), flip(y, 1 - c))

    def make_swap(ins, outs, send_sem, recv_sem):
        x, y, c, _ = _mesh_position()
        first, _ = axis_neighbours(x, y, c)
        starts, arrivals = [], []
        for g in range(n):
            sems = (send_sem(3 * g), recv_sem(3 * g))
            starts.append(functools.partial(_remote, my_rows(ins[g], c), my_rows(outs[g].at[_chip_index(x, y)], c),
                                            *sems, (*first, c)))
            got = my_rows(outs[g].at[_chip_index(*first)], c)
            arrivals.append(functools.partial(_remote, got, got, *sems, (*first, c)))
        return starts, arrivals

    def make_pass_on(ins, outs, send_sem, recv_sem):
        x, y, c, _ = _mesh_position()
        first, second = axis_neighbours(x, y, c)
        diagonal = (1 - x, 1 - y)
        starts, arrivals = [], []
        for g in range(n):
            half = lambda chip: my_rows(outs[g].at[_chip_index(*chip)], c)
            for k, (sent, arriving) in enumerate([((x, y), second), (first, diagonal)]):
                sems = (send_sem(3 * g + 1 + k), recv_sem(3 * g + 1 + k))
                src = my_rows(ins[g], c) if k == 0 else half(sent)
                starts.append(functools.partial(_remote, src, half(sent), *sems, (*second, c)))
                arrivals.append(functools.partial(_remote, half(arriving), half(arriving), *sems, (*second, c)))
        return starts, arrivals

    def finish(res):
        landed.update(zip(keys, res))

    shapes = lambda: [_sds((N_CHIPS,) + shards[k].shape, shards[k].dtype) for k in keys]
    aliases = {n + g: g for g in range(n)}
    if not split:
        return _Task("chips", operands, shapes, aliases, 3 * n, make_direct, finish)
    return _Task("neighbours", operands, shapes, aliases, 3 * n, make_swap, finish, make_second=make_pass_on)


def _gather_sibling_task(keys, landed, ready):
    n = len(keys)

    def make(ins, outs, send_sem, recv_sem):
        x, y, c, chips = _mesh_position()
        starts, arrivals = [], []
        for g in range(n):
            for k, chip in enumerate(chips):
                o = outs[g].at[_chip_index(*chip)]
                got, other = _rows_half(o, 1, c), _rows_half(o, 1, 1 - c)
                starts.append(functools.partial(_remote, got, got, send_sem(3 * g + k), recv_sem(3 * g + k),
                                                (x, y, 1 - c)))
                arrivals.append(functools.partial(_remote, other, other, send_sem(3 * g + k), recv_sem(3 * g + k),
                                                  (x, y, 1 - c)))
        return starts, arrivals

    def finish(res):
        ready.update(zip(keys, res))

    return _Task("sibling", lambda: [landed[k] for k in keys],
                 lambda: [_sds(landed[k].shape, landed[k].dtype) for k in keys],
                 {g: g for g in range(n)}, 3 * n, make, finish)


def _pair_swap_task(names, big, got):
    n = len(names)

    def make(ins, outs, send_sem, recv_sem):
        x, y, c, _ = _mesh_position()
        copies = [functools.partial(_remote, _rows_half(ins[a], 2, 1 - c), outs[a], send_sem(a), recv_sem(a),
                                    (x, y, 1 - c)) for a in range(n)]
        return copies, copies

    def shapes():
        return [_sds(big[k].shape[:2] + (big[k].shape[2] // 2, big[k].shape[3]), big[k].dtype) for k in names]

    return _Task("sibling", lambda: [big[k] for k in names], shapes, {}, n, make,
                 lambda res: got.update(zip(names, res)))


def _rs_pair_sum(name, full, got, core):
    nw, ns, R, C = full.shape
    half = R // 2

    def body(core_ref, a_ref, b_ref, o_ref):
        o_ref[...] = (a_ref[...].astype(F32) + b_ref[...].astype(F32)).astype(BF16)

    blk = lambda fn: pl.BlockSpec((None, None, half, C), fn)
    return _pcall(
        body, name=name, grid=(nw, ns), num_prefetch=1,
        in_specs=[blk(lambda w, s, core_ref: (w, s, core_ref[0], 0)), blk(lambda w, s, core_ref: (w, s, 0, 0))],
        out_specs=blk(lambda w, s, core_ref: (w, s, 0, 0)),
        out_shape=_sds((nw, ns, half, C), BF16),
    )(core, full, got)


def _chip_exchange_task(names, pair_sums, by_source, part=0, nparts=1):
    n = len(names)

    def rows(ref):
        h = ref.shape[1] // nparts
        return ref.at[:, pl.ds(part * h, h), :]

    def make(ins, outs, send_sem, recv_sem):
        x, y, c, chips = _mesh_position()
        s_me = _chip_index(x, y)
        starts, arrivals = [], []
        for a in range(n):
            for k, chip in enumerate(chips):
                s_k = _chip_index(*chip)
                starts.append(functools.partial(_remote, rows(ins[a].at[:, s_k]), rows(outs[a].at[:, s_me]),
                                                send_sem(3 * a + k), recv_sem(3 * a + k), (*chip, c)))
                got = rows(outs[a].at[:, s_k])
                arrivals.append(functools.partial(_remote, got, got, send_sem(3 * a + k), recv_sem(3 * a + k),
                                                  (*chip, c)))
        return starts, arrivals

    def operands():
        return [pair_sums[k] for k in names] + ([by_source[k] for k in names] if part else [])

    return _Task("chips", operands, lambda: [_sds(pair_sums[k].shape, pair_sums[k].dtype) for k in names],
                 {n + a: a for a in range(n)} if part else {}, 3 * n, make,
                 lambda res: by_source.update(zip(names, res)))


def _rs_chip_sum(name, own, parts, chip):
    nw, ns, H, C = parts.shape

    def body(chip_ref, own_ref, *rest):
        prefs, o_ref = rest[:ns], rest[ns]
        me = chip_ref[0]
        own_v = own_ref[...].astype(F32)
        tot = None
        for s in range(ns):
            term = jnp.where(me == s, own_v, prefs[s][...].astype(F32))
            tot = term if tot is None else tot + term
        o_ref[...] = tot

    blk = lambda fn: pl.BlockSpec((None, None, H, C), fn)

    def part_spec(s):
        return blk(lambda w, chip_ref: (w, jnp.where(chip_ref[0] == s, (s + 1) % ns, s), 0, 0))

    return _pcall(
        body, name=name, grid=(nw,), num_prefetch=1,
        in_specs=[blk(lambda w, chip_ref: (w, chip_ref[0], 0, 0))] + [part_spec(s) for s in range(ns)],
        out_specs=pl.BlockSpec((None, H, C), lambda w, chip_ref: (w, 0, 0)),
        out_shape=_sds((nw, H, C), F32),
    )(chip, own, *([parts] * ns))


def _pair_gather_task(names, halves, sibling_halves):
    n = len(names)

    def make(ins, outs, send_sem, recv_sem):
        x, y, c, _ = _mesh_position()
        copies = [functools.partial(_remote, ins[a], outs[a], send_sem(a), recv_sem(a), (x, y, 1 - c))
                  for a in range(n)]
        return copies, copies

    return _Task("sibling", lambda: [halves[k] for k in names], lambda: [_sds(halves[k].shape, F32) for k in names],
                 {}, n, make, lambda res: sibling_halves.update(zip(names, res)))


def _small_allreduce(arrs):
    n = len(arrs)
    per = 1 + 2 * N_PEER_CHIPS

    def body(*refs):
        v_refs, o_refs = refs[:n], refs[n:2 * n]
        sib, pair, part = refs[2 * n:3 * n], refs[3 * n:4 * n], refs[4 * n:5 * n]
        send_sems, recv_sems = refs[5 * n:]
        x, y, c, chips = _mesh_position()
        s_me = _chip_index(x, y)

        def quarter(ref, s):
            q = ref.shape[0] // N_CHIPS
            return ref.at[pl.ds(pl.multiple_of(s * q, 8), q)]

        def exchange(first_sem, src, dst_of, arrival_of):
            sems = lambda a, k: (send_sems.at[a * per + first_sem + k], recv_sems.at[a * per + first_sem + k])
            sends = [_remote(src(a, _chip_index(*chip)), dst_of(a, s_me), *sems(a, k), (*chip, c))
                     for a in range(n) for k, chip in enumerate(chips)]
            for cp in sends:
                cp.start()
            for a in range(n):
                for k, chip in enumerate(chips):
                    got = arrival_of(a, _chip_index(*chip))
                    _remote(got, got, *sems(a, k), (*chip, c)).wait_recv()
            for cp in sends:
                cp.wait_send()

        swaps = [_remote(v_refs[a], sib[a], send_sems.at[a * per], recv_sems.at[a * per], (x, y, 1 - c))
                 for a in range(n)]
        for cp in swaps:
            cp.start()
        for cp in swaps:
            cp.wait()
        for a in range(n):
            pair[a][...] = v_refs[a][...] + sib[a][...]
        exchange(1, lambda a, s_k: quarter(pair[a], s_k), lambda a, s: part[a].at[s], lambda a, s_k: part[a].at[s_k])
        for a in range(n):
            part[a][s_me] = quarter(pair[a], s_me)[...]
            q = o_refs[a].shape[0] // N_CHIPS
            o_refs[a][pl.ds(pl.multiple_of(s_me * q, 8), q), :] = (
                ((part[a][0] + part[a][1]) + part[a][2]) + part[a][3])
        exchange(1 + N_PEER_CHIPS, lambda a, s_k: quarter(o_refs[a], s_me), lambda a, s: quarter(o_refs[a], s),
                 lambda a, s_k: quarter(o_refs[a], s_k))

    shapes = [a.shape for a in arrs]
    return _pcall(
        body, name="small_allreduce", grid=(1,), own_peers=("sibling", "chips"),
        in_specs=[VMEM_SPEC] * n, out_specs=[VMEM_SPEC] * n, out_shape=[_sds(s, F32) for s in shapes],
        scratch_shapes=([pltpu.VMEM(s, F32) for s in shapes] * 2
                        + [pltpu.VMEM((N_CHIPS, s[0] // N_CHIPS, s[1]), F32) for s in shapes]
                        + [pltpu.SemaphoreType.DMA((n * per,)), pltpu.SemaphoreType.DMA((n * per,))]),
    )(*arrs)


TRANSPOSED_WEIGHTS = ("ffn1_w1", "ffn1_w3", "ffn2_w1", "ffn2_w3")
SMALL_LAYOUT = [("ffn1_norm", 1), ("mix_norm", 1), ("ret_gn", 1), ("conv_b", 1), ("b_rgate", 1), ("b_igate", 1),
                ("lru_lambda", 1), ("xattn_norm", 1), ("mem_norm", 1), ("ffn2_norm", 1), ("final_norm", 1),
                ("b_branch_gate", 2), ("conv_w", CONV_TAPS)]
SMALL_ROWS = 32
GATE_WEIGHTS = ("w_rgate", "w_igate")
WEIGHT_ORDER = ["ffn1_norm", "ffn1_w1", "ffn1_w3", "ffn1_w2", "mix_norm", "w_in", "ret_gn", "w_ret_o", "conv_w",
                "conv_b", "w_rgate", "b_rgate", "w_igate", "b_igate", "lru_lambda", "w_lru_o", "w_branch_gate",
                "b_branch_gate", "w_out", "xattn_norm", "mem_norm", "w_xq", "w_xk", "w_xv", "w_xo", "ffn2_norm",
                "ffn2_w1", "ffn2_w3", "ffn2_w2", "final_norm"]


def _pack_small(parts):
    rows = [parts[name].reshape(n, D) for name, n in SMALL_LAYOUT]
    used = sum(n for _, n in SMALL_LAYOUT)
    rows.append(jnp.zeros((SMALL_ROWS - used, D), F32))
    return jnp.concatenate(rows, axis=0)


def _unpack_small(packed, shapes):
    out, r = {}, 0
    for name, n in SMALL_LAYOUT:
        out[name] = packed[r:r + n].reshape(shapes[name])
        r += n
    return out


def kernel(x, mem, ffn1_norm, ffn1_w1, ffn1_w3, ffn1_w2, mix_norm, w_in, ret_gn, w_ret_o, conv_w, conv_b, w_rgate, b_rgate, w_igate, b_igate, lru_lambda, w_lru_o, w_branch_gate, b_branch_gate, w_out, xattn_norm, mem_norm, w_xq, w_xk, w_xv, w_xo, ffn2_norm, ffn2_w1, ffn2_w3, ffn2_w2, final_norm, loss_target, m_ffn1_norm, m_ffn1_w1, m_ffn1_w3, m_ffn1_w2, m_mix_norm, m_w_in, m_ret_gn, m_w_ret_o, m_conv_w, m_conv_b, m_w_rgate, m_b_rgate, m_w_igate, m_b_igate, m_lru_lambda, m_w_lru_o, m_w_branch_gate, m_b_branch_gate, m_w_out, m_xattn_norm, m_mem_norm, m_w_xq, m_w_xk, m_w_xv, m_w_xo, m_ffn2_norm, m_ffn2_w1, m_ffn2_w3, m_ffn2_w2, m_final_norm, v_ffn1_norm, v_ffn1_w1, v_ffn1_w3, v_ffn1_w2, v_mix_norm, v_w_in, v_ret_gn, v_w_ret_o, v_conv_w, v_conv_b, v_w_rgate, v_b_rgate, v_w_igate, v_b_igate, v_lru_lambda, v_w_lru_o, v_w_branch_gate, v_b_branch_gate, v_w_out, v_xattn_norm, v_mem_norm, v_w_xq, v_w_xk, v_w_xv, v_w_xo, v_ffn2_norm, v_ffn2_w1, v_ffn2_w3, v_ffn2_w2, v_final_norm):
    given = dict(locals())
    w = {n: given[n] for n in WEIGHT_ORDER}
    mom = {n: given["m_" + n] for n in WEIGHT_ORDER}
    var = {n: given["v_" + n] for n in WEIGHT_ORDER}
    chip = _chip_index(lax.axis_index("x"), lax.axis_index("y"))
    core = lax.axis_index("c").astype(jnp.int32).reshape(1)

    chip_id = chip.astype(jnp.int32).reshape(1)
    sm = {n: w[n] for n in ["ffn1_norm", "mix_norm", "ret_gn", "conv_b", "b_rgate", "b_igate", "lru_lambda",
                            "xattn_norm", "mem_norm", "ffn2_norm", "b_branch_gate"]}
    sm["final_norm"] = w["final_norm"].reshape(1, D)
    sm["w_rgate"] = w["w_rgate"][0]
    sm["w_igate"] = w["w_igate"][0]

    local = lambda a, n: jnp.swapaxes(a[0], 0, 1) if n in TRANSPOSED_WEIGHTS else a[0]
    stack = lambda names: jnp.stack([local(w[n], n) for n in names], axis=0).astype(BF16)
    shard = {"col1": stack(["ffn1_w1", "ffn1_w3"]), "row2a": stack(["ffn1_w2"]),
             "win": jnp.swapaxes(w["w_in"], 1, 2).astype(BF16), "wbg": jnp.swapaxes(w["w_branch_gate"], 1, 2).astype(BF16), "sqA": stack(["w_ret_o", "w_lru_o", "w_out"]),
             "sqB": stack(["w_xq", "w_xk"]), "sqC": stack(["w_xv", "w_xo"]), "col2a": stack(["ffn2_w1"]), "col2b": stack(["ffn2_w3"]),
             "row2b": stack(["ffn2_w2"]), "conv": w["conv_w"]}
    gw, landed = {}, {}
    over_chips = lambda keys: _gather_chips_task({k: shard[k] for k in keys}, True, landed)
    to_sibling = lambda keys: _gather_sibling_task(keys, landed, gw)

    big, got, pair_sums, by_source, halves, sibling_halves, outs = {}, {}, {}, {}, {}, {}, {}
    pair_swap = lambda names: _pair_swap_task(names, big, got)
    exchange = lambda names, part=0, nparts=1: _chip_exchange_task(names, pair_sums, by_source, part, nparts)
    pair_gather = lambda names: _pair_gather_task(names, halves, sibling_halves)

    def pair_sum(names):
        for n in names:
            pair_sums[n] = _rs_pair_sum("rs_pair_sum_" + n, big[n], got[n], core)

    def chip_sum(names):
        for n in names:
            halves[n] = _rs_chip_sum("rs_chip_sum_" + n, pair_sums[n], by_source[n], chip_id)

    def adamw(names):
        for n in names:
            res = _adamw_halves("adamw_" + n, local(w[n], n), halves[n], sibling_halves[n], 0, local(mom[n], n),
                                local(var[n], n), core)
            outs[n] = tuple((jnp.swapaxes(r, 0, 1) if n in TRANSPOSED_WEIGHTS else r)[None] for r in res)

    do = lambda fn, names: functools.partial(fn, names)
    ffn2_grads = ["ffn2_w2", "ffn2_w1", "ffn2_w3"]
    xattn_grads = ["w_xo", "w_xq", "w_xk", "w_xv"]
    mix_out_grads = ["w_branch_gate", "w_out", "w_ret_o", "w_lru_o"]
    conv_gather = _gather_chips_task({"conv": shard["conv"]}, False, gw)
    half = lambda key, part: _gather_chips_task({key: shard[key]}, True, landed, part, 2)
    plan = _Plan()
    plan.tasks = {
        "ag_first_chips": [over_chips(["col1", "row2a"])],
        "ag_first_sibling": [to_sibling(["col1", "row2a"])],
        "ffn1_up": [over_chips(["win"])],
        "ffn1_down": [to_sibling(["win"]), over_chips(["wbg"]), conv_gather],
        "mix_in": [to_sibling(["wbg"]), over_chips(["sqA"])],
        "ret_fwd": [to_sibling(["sqA"]), over_chips(["col2a"])],
        "lru_gates_fwd": [to_sibling(["col2a"]), over_chips(["sqB"])],
        "lru_scan_fwd": [to_sibling(["sqB"]), over_chips(["sqC"])],
        "mix_gates": [to_sibling(["sqC"]), half("col2b", 0)],
        "y_lru": [half("col2b", 1)],
        "xattn_fwd": [to_sibling(["col2b"])],
        "ffn2_up": [over_chips(["row2b"])],
        "ffn2_up_sibling": [to_sibling(["row2b"])],
        "ffn2_dh": [pair_swap(ffn2_grads)],
        "xattn_bwd": [exchange(["ffn2_w2"], 0, 2)],
        "d_hq": [exchange(["ffn2_w2"], 1, 2)],
        "d_merged": [exchange(["ffn2_w1"], 0, 2), pair_swap(xattn_grads)],
        "lru_out_bwd": [exchange(["w_xo"])],
        "ret_bwd": [exchange(["ffn2_w1"], 1, 2), exchange(["ffn2_w3"], 0, 2), pair_swap(mix_out_grads)],
        "lru_scan_bwd": [exchange(["ffn2_w3"], 1, 2)],
        "lru_gates_bwd": [exchange(["w_xq", "w_xk"]), pair_gather(ffn2_grads)],
        "dw_in": [exchange(["w_xv", "w_out"])],
        "d_h2": [exchange(["w_branch_gate", "w_ret_o", "w_lru_o"]), pair_swap(["w_in"]), pair_gather(xattn_grads)],
        "ffn1_bwd_mid": [exchange(["w_in"], 0, 2), pair_gather(mix_out_grads)],
        "ffn1_dw2": [exchange(["w_in"], 2, 4)],
        "ffn1_dw1": [exchange(["w_in"], 3, 4), pair_swap(["ffn1_w2"])],
        "ffn1_dw3": [exchange(["ffn1_w2"], 0, 2), pair_swap(["ffn1_w1"]), pair_gather(["w_in"])],
        "ffn1_dh": [exchange(["ffn1_w2"], 1, 2), exchange(["ffn1_w1"]), pair_swap(["ffn1_w3"])],
        "small_allreduce": [exchange(["ffn1_w3"]), pair_gather(["ffn1_w2"])],
        "rs_last_gather": [pair_gather(["ffn1_w1", "ffn1_w3"])],
    }
    plan.after = {
        "ffn2_up": [functools.partial(_comm_call, "ffn2_up_sibling")],
        "ffn2_dh": [do(pair_sum, ffn2_grads)],
        "d_merged": [do(pair_sum, xattn_grads)],
        "ret_bwd": [do(pair_sum, mix_out_grads)],
        "lru_scan_bwd": [do(chip_sum, ffn2_grads)],
        "lru_gates_bwd": [do(adamw, ffn2_grads)],
        "dw_in": [do(chip_sum, xattn_grads)],
        "d_h2": [do(chip_sum, mix_out_grads), do(pair_sum, ["w_in"]), do(adamw, xattn_grads)],
        "ffn1_bwd_mid": [do(adamw, mix_out_grads)],
        "ffn1_dw1": [do(chip_sum, ["w_in"]), do(pair_sum, ["ffn1_w2"])],
        "ffn1_dw3": [do(pair_sum, ["ffn1_w1"]), do(adamw, ["w_in"])],
        "ffn1_dh": [do(pair_sum, ["ffn1_w3"]), do(chip_sum, ["ffn1_w2"])],
        "small_allreduce": [do(chip_sum, ["ffn1_w1", "ffn1_w3"]), functools.partial(_comm_call, "rs_last_gather"),
                    do(adamw, ["ffn1_w2", "ffn1_w1", "ffn1_w3"])],
    }
    global _plan
    _plan = plan
    try:
        _comm_call("ag_first_chips")
        _comm_call("ag_first_sibling")
        loss_part, grad_x, small = _local_step(x[0], mem[0], loss_target[0], gw, sm, big)
        gate2d = lambda a: a.reshape(LRU_BLOCKS * LRU_BLOCK, LRU_BLOCK)
        small_sum, *gate_sums = _small_allreduce([_pack_small(small)] + [gate2d(small[n]) for n in GATE_WEIGHTS])
    finally:
        _plan = None
    assert not plan.tasks and not plan.after, (list(plan.tasks), list(plan.after))
    loss = lax.psum(loss_part[0, 0], ("x", "y", "c"))

    small_shapes = {n: w[n].shape for n, _ in SMALL_LAYOUT}
    small_shapes["conv_w"] = (CONV_TAPS, D)
    conv_grad = lax.dynamic_slice(small_sum[13:13 + CONV_TAPS], (0, chip * SQ_BLK), (CONV_TAPS, SQ_BLK))
    small_w = {n: w[n] for n, _ in SMALL_LAYOUT}
    small_m = {n: mom[n] for n, _ in SMALL_LAYOUT}
    small_v = {n: var[n] for n, _ in SMALL_LAYOUT}
    pad_cols = lambda a: jnp.pad(a[0], ((0, 0), (0, D - SQ_BLK)))
    for dct in (small_w, small_m, small_v):
        dct["conv_w"] = pad_cols(dct["conv_w"])
    g_pack = lax.dynamic_update_slice(small_sum, jnp.pad(conv_grad, ((0, 0), (0, D - SQ_BLK))), (13, 0))
    d_pack, m_pack, v_pack = _adamw("adamw_small", _pack_small(small_w), g_pack, _pack_small(small_m),
                                    _pack_small(small_v))
    unpacked = [_unpack_small(p, small_shapes) for p in (g_pack, d_pack, m_pack, v_pack)]
    for n, _ in SMALL_LAYOUT:
        if n == "conv_w":
            outs[n] = tuple(u[n][:, :SQ_BLK][None] for u in unpacked)
        else:
            outs[n] = tuple(u[n] for u in unpacked)
    for n, gsum in zip(GATE_WEIGHTS, gate_sums):
        d, nm, nv = _adamw("adamw_" + n, gate2d(w[n]), gsum, gate2d(mom[n]), gate2d(var[n]))
        outs[n] = tuple(r.reshape(w[n].shape) for r in (gsum, d, nm, nv))

    result = [loss, grad_x[None]]
    for k in range(4):
        result += [outs[n][k] for n in WEIGHT_ORDER]
    return tuple(result)
```

```python
import functools
import math

import numpy as np
import jax
import jax.numpy as jnp
from jax import lax
from jax.experimental import pallas as pl
from jax.experimental.pallas import tpu as pltpu

F32 = jnp.float32
BF16 = jnp.bfloat16
GRAD_WIRE_DTYPE = BF16
MESH = pl.DeviceIdType.MESH

D = 1024
EPS = 1e-6
RET_HEADS = 4
RET_DK = 128
RET_DV = 256
CHUNK = 128
ROPE_BASE = 10000.0
LRU_BLOCKS = 8
LRU_BLOCK = 128
CONV_TAPS = 4
LRU_C = 8.0
D_FF = 2816
X_HEADS = 4
X_HD = 256
N_CHIPS = 4
FF_BLK = D_FF // N_CHIPS
IN_BLK = 5120 // N_CHIPS
BG_BLK = 2048 // N_CHIPS
SQ_BLK = D // N_CHIPS

ADAM_LR = 0.001
ADAM_B1 = 0.9
ADAM_B2 = 0.999
ADAM_EPS = 1e-08
ADAM_WD = 0.01
ADAM_STEP = 10

VMEM_LIMIT_BYTES = 56 * 1024 * 1024
ROW_TILE = 512
WIDE_ROW_TILE = 1024
FFN_ROW_TILE = 256
DW_BLK = D_FF // 2
SCAN_TILE = 256

_DN = {
    "nn": (((1,), (0,)), ((), ())),
    "nt": (((1,), (1,)), ((), ())),
    "tn": (((0,), (0,)), ((), ())),
}


def _cparams(n_axes, collective_id=None):
    return pltpu.CompilerParams(dimension_semantics=("arbitrary",) * n_axes,
                                vmem_limit_bytes=VMEM_LIMIT_BYTES, collective_id=collective_id)


def _dot(a, b, kind):
    if b.ndim == 3:
        b = b.reshape(b.shape[0] * b.shape[1], b.shape[2])
    return lax.dot_general(a.astype(BF16), b.astype(BF16), _DN[kind], preferred_element_type=F32)


def _sigmoid(x):
    return 1.0 / (1.0 + jnp.exp(-x))


def _log1p_pos(e):
    u = 1.0 + e
    return jnp.where(u == 1.0, e, jnp.log(u) * (e / jnp.where(u == 1.0, 1.0, u - 1.0)))


def _expm1(x):
    u = jnp.exp(x)
    lu = jnp.log(u)
    safe = jnp.where(lu == 0.0, 1.0, lu)
    return jnp.where(u == 1.0, x, (u - 1.0) * (x / safe))


def _softplus(z):
    return jnp.maximum(z, 0.0) + _log1p_pos(jnp.exp(-jnp.abs(z)))


_GELU_C = math.sqrt(2.0 / math.pi)


def _gelu_and_grad(x):
    x2 = x * x
    t = jnp.tanh(_GELU_C * (x + 0.044715 * x * x2))
    g = 0.5 * x * (1.0 + t)
    dg = 0.5 * (1.0 + t) + 0.5 * x * (1.0 - t * t) * (_GELU_C * (1.0 + 3.0 * 0.044715 * x2))
    return g, dg


def _rms_fwd(x, g):
    r = lax.rsqrt(jnp.mean(x * x, axis=-1, keepdims=True) + EPS)
    return (x * r) * g


def _rms_bwd(x, g, dh):
    r = lax.rsqrt(jnp.mean(x * x, axis=-1, keepdims=True) + EPS)
    n = x * r
    dyg = dh * g
    dx = r * (dyg - n * jnp.mean(dyg * n, axis=-1, keepdims=True))
    return dx, jnp.sum(dh * n, axis=0, keepdims=True)


def _accumulate(ref, val, first):
    @pl.when(first)
    def _():
        ref[...] = val

    @pl.when(jnp.logical_not(first))
    def _():
        ref[...] += val


def _sds(shape, dtype):
    return jax.ShapeDtypeStruct(tuple(shape), dtype)


def _spec(shape, fn):
    return pl.BlockSpec(tuple(shape), fn)


class _Task:
    def __init__(self, peers, operands, out_shapes, aliases, nsem, make, finish, make_second=None):
        self.peers = peers
        self.operands, self.out_shapes, self.aliases = operands, out_shapes, aliases
        self.nsem, self.make, self.finish = nsem, make, finish
        self.make_second = make_second


class _Plan:
    def __init__(self):
        self.tasks, self.after = {}, {}


_plan = None


PEER_SET_COLLECTIVE_ID = {frozenset({"sibling"}): 1, frozenset({"chips"}): 2, frozenset({"sibling", "chips"}): 3,
                          frozenset({"neighbours"}): 4, frozenset({"sibling", "neighbours"}): 5}


def _peer_set(names):
    names = frozenset(names)
    return names - {"neighbours"} if "chips" in names else names


def _entry_handshake(peer_set):
    x, y, c, chips = _mesh_position()
    peers = [(x, y, 1 - c)] if "sibling" in peer_set else []
    if "chips" in peer_set:
        peers += [(*chip, c) for chip in chips]
    if "neighbours" in peer_set:
        peers += [(*chip, c) for chip in chips[:2]]
    barrier = pltpu.get_barrier_semaphore()
    for peer in peers:
        pl.semaphore_signal(barrier, inc=1, device_id=peer, device_id_type=MESH)
    pl.semaphore_wait(barrier, len(peers))


def _pcall(body, *, name, grid, in_specs, out_specs, out_shape, scratch_shapes=(), num_prefetch=0, own_peers=()):
    single = not isinstance(out_shape, (list, tuple))
    out_shape = [out_shape] if single else list(out_shape)
    out_specs = [out_specs] if single else list(out_specs)
    in_specs = list(in_specs)
    scratch_shapes = list(scratch_shapes)
    tasks = _plan.tasks.pop(name, []) if _plan is not None else []
    after = _plan.after.pop(name, []) if _plan is not None else []
    peer_set = _peer_set([t.peers for t in tasks] + list(own_peers))
    nax = len(grid)

    def run(*operands):
        n_in = len(operands) - num_prefetch
        n_out = len(out_shape)
        t_ops = [t.operands() for t in tasks]
        t_outs = [t.out_shapes() for t in tasks]
        c_ops = [a for ops in t_ops for a in ops]
        c_outs = [s for outs in t_outs for s in outs]
        aliases = {}
        i0, o0 = num_prefetch + n_in, n_out
        for t, ops, outs in zip(tasks, t_ops, t_outs):
            for i_loc, o_loc in t.aliases.items():
                aliases[i0 + i_loc] = o0 + o_loc
            i0 += len(ops)
            o0 += len(outs)
        nsem = sum(t.nsem for t in tasks)

        def wrapped(*refs):
            p = num_prefetch
            pre, ins = refs[:p], refs[p:p + n_in]
            cins = refs[p + n_in:p + n_in + len(c_ops)]
            q = p + n_in + len(c_ops)
            outs, couts = refs[q:q + n_out], refs[q + n_out:q + n_out + len(c_outs)]
            q += n_out + len(c_outs)
            scr = refs[q:q + len(scratch_shapes)]

            def rounds(second):
                send_sems, recv_sems = refs[q + len(scratch_shapes):]
                out = []
                ci = co = so = 0
                for t, ops, souts in zip(tasks, t_ops, t_outs):
                    make = t.make_second if second else t.make
                    out.append(([], []) if make is None else
                               make(cins[ci:ci + len(ops)], couts[co:co + len(souts)],
                                    functools.partial(lambda base, k: send_sems.at[base + k], so),
                                    functools.partial(lambda base, k: recv_sems.at[base + k], so)))
                    ci, co, so = ci + len(ops), co + len(souts), so + t.nsem
                return out

            two_rounds = [t.make_second is not None for t in tasks]
            if peer_set:
                ids = [pl.program_id(k) for k in range(nax)]
                first = functools.reduce(jnp.logical_and, [i == 0 for i in ids])
                last = functools.reduce(jnp.logical_and, [i == g - 1 for i, g in zip(ids, grid)])
                step = functools.reduce(lambda acc, ig: acc * ig[1] + ig[0], zip(ids, grid), 0)
                middle = step == math.prod(grid) // 3

                @pl.when(first)
                def _():
                    _entry_handshake(peer_set)
                    for starts, _ in rounds(False):
                        for copy in starts:
                            copy().start()

            body(*pre, *ins, *outs, *scr)

            if any(two_rounds):
                @pl.when(middle)
                def _():
                    for (_, arrivals), two in zip(rounds(False), two_rounds):
                        if two:
                            for arrival in arrivals:
                                arrival().wait_recv()
                    for starts, _ in rounds(True):
                        for copy in starts:
                            copy().start()

            if tasks:
                @pl.when(last)
                def _():
                    first_round, second_round = rounds(False), rounds(True)
                    for (_, arrivals1), (_, arrivals2), two in zip(first_round, second_round, two_rounds):
                        for arrival in (arrivals2 if two else arrivals1):
                            arrival().wait_recv()
                    for starts, _ in first_round + second_round:
                        for copy in starts:
                            copy().wait_send()

        sems = [pltpu.SemaphoreType.DMA((nsem,)), pltpu.SemaphoreType.DMA((nsem,))] if tasks else []
        res = pl.pallas_call(
            wrapped, name=name,
            grid_spec=pltpu.PrefetchScalarGridSpec(
                num_scalar_prefetch=num_prefetch, grid=tuple(grid),
                in_specs=in_specs + [ANY_SPEC] * len(c_ops),
                out_specs=out_specs + [ANY_SPEC] * len(c_outs),
                scratch_shapes=scratch_shapes + sems),
            out_shape=out_shape + c_outs,
            input_output_aliases=aliases,
            compiler_params=_cparams(nax, PEER_SET_COLLECTIVE_ID[peer_set] if peer_set else None),
        )(*operands, *c_ops)
        co = n_out
        for t, souts in zip(tasks, t_outs):
            t.finish(res[co:co + len(souts)])
            co += len(souts)
        for fn in after:
            fn()
        return res[0] if single else list(res[:n_out])

    return run


def _comm_call(name):
    def body(o_ref):
        o_ref[...] = jnp.zeros_like(o_ref)

    _pcall(body, name=name, grid=(1,), in_specs=[], out_specs=_spec((8, 128), lambda i: (0, 0)),
           out_shape=_sds((8, 128), F32))()


def _gemm(name, terms, grid, outs, acc_shape, extras=(), epilogue=None):
    kinds = [t[4] for t in terms]
    nt, ne, no = len(terms), len(extras), len(outs)
    nred = grid[-1]
    nax = len(grid)

    def body(*refs):
        trefs = refs[:2 * nt]
        erefs = refs[2 * nt:2 * nt + ne]
        orefs = refs[2 * nt + ne:2 * nt + ne + no]
        ids = [pl.program_id(k) for k in range(nax)]
        tot = None
        for t in range(nt):
            d = _dot(trefs[2 * t][...], trefs[2 * t + 1][...], kinds[t])
            tot = d if tot is None else tot + d

        def finish(acc):
            if epilogue is None:
                orefs[0][...] = acc.astype(orefs[0].dtype)
            else:
                epilogue(acc, erefs, orefs, ids)

        if nred == 1:
            finish(tot)
        else:
            acc_ref = refs[-1]
            r = ids[-1]

            @pl.when(r == 0)
            def _():
                acc_ref[...] = tot

            @pl.when(r > 0)
            def _():
                acc_ref[...] += tot

            @pl.when(r == nred - 1)
            def _():
                finish(acc_ref[...])

    operands, in_specs = [], []
    for a, a_spec, b, b_spec, _ in terms:
        operands += [a, b]
        in_specs += [a_spec, b_spec]
    for e, e_spec in extras:
        operands.append(e)
        in_specs.append(e_spec)
    scratch = [pltpu.VMEM(tuple(acc_shape), F32)] if nred > 1 else []
    return _pcall(body, name=name, grid=tuple(grid), in_specs=in_specs, out_specs=[o[1] for o in outs],
                  out_shape=[o[0] for o in outs], scratch_shapes=scratch)(*operands)


def _rowwise(name, fn, ins, outs, grid):
    ni = len(ins)
    nax = len(grid)

    def body(*refs):
        ids = [pl.program_id(k) for k in range(nax)]
        fn(refs[:ni], refs[ni:], ids)

    return _pcall(body, name=name, grid=tuple(grid), in_specs=[i[1] for i in ins],
                  out_specs=[o[1] for o in outs], out_shape=[o[0] for o in outs])(*[i[0] for i in ins])


def _ffn_up(name, h, w1buf, w1_idx, w3buf, w3_idx):
    T = h.shape[0]
    tm = min(FFN_ROW_TILE, T)

    def body(h_ref, w1_ref, w3_ref, a_ref, b_ref, s_ref):
        hv = h_ref[...]
        a = _dot(hv, w1_ref[...], "nt")
        b = _dot(hv, w3_ref[...], "nt")
        a_ref[...] = a.astype(BF16)
        b_ref[...] = b.astype(BF16)
        s_ref[...] = ((a * _sigmoid(a)) * b).astype(BF16)

    blk = _spec((tm, D_FF), lambda i: (i, 0))
    return _pcall(
        body, name=name, grid=(T // tm,),
        in_specs=[_spec((tm, D), lambda i: (i, 0)),
                  _spec((N_CHIPS, None, FF_BLK, D), lambda i: (0, w1_idx, 0, 0)),
                  _spec((N_CHIPS, None, FF_BLK, D), lambda i: (0, w3_idx, 0, 0))],
        out_specs=[blk, blk, blk],
        out_shape=[_sds((T, D_FF), BF16)] * 3,
    )(h, w1buf, w3buf)


def _ffn_down(name, s, wrow2, w2_idx, x_res, g_next=None):
    T = x_res.shape[0]
    tm = min(ROW_TILE, T)
    row = lambda i, j, r: (i, 0)

    def epilogue(acc, erefs, orefs, ids):
        xo = erefs[0][...] + 0.5 * acc
        orefs[0][...] = xo
        if g_next is not None:
            orefs[1][...] = _rms_fwd(xo, erefs[1][...]).astype(BF16)

    extras = [(x_res, _spec((tm, D), row))]
    outs = [(_sds((T, D), F32), _spec((tm, D), row))]
    if g_next is not None:
        extras.append((g_next, _spec((1, D), lambda i, j, r: (0, 0))))
        outs.append((_sds((T, D), BF16), _spec((tm, D), row)))
    return _gemm(
        name,
        [(s, _spec((tm, D_FF), row),
          wrow2, _spec((N_CHIPS, None, FF_BLK, D), lambda i, j, r: (0, w2_idx, 0, 0)), "nn")],
        (T // tm, 1, 1), outs, (tm, D), extras, epilogue)


def _ffn_bwd_mid(name, dx, wrow2, w2_idx, a, b):
    T = dx.shape[0]
    tm = min(FFN_ROW_TILE, T)

    def body(dx_ref, w2_ref, a_ref, b_ref, dab_ref):
        ds = _dot(0.5 * dx_ref[...], w2_ref[...], "nt")
        av = a_ref[...].astype(F32)
        sg = _sigmoid(av)
        dab_ref[0] = (ds * b_ref[...].astype(F32) * (sg * (1.0 + av * (1.0 - sg)))).astype(BF16)
        dab_ref[1] = (ds * (av * sg)).astype(BF16)

    blk = _spec((tm, D_FF), lambda i: (i, 0))
    return _pcall(
        body, name=name, grid=(T // tm,),
        in_specs=[_spec((tm, D), lambda i: (i, 0)),
                  _spec((N_CHIPS, None, FF_BLK, D), lambda i: (0, w2_idx, 0, 0)),
                  blk, blk],
        out_specs=_spec((2, tm, D_FF), lambda i: (0, i, 0)),
        out_shape=_sds((2, T, D_FF), BF16),
    )(dx, wrow2, a, b)


def _rms_bwd_epilogue(acc, erefs, orefs, ids):
    dx, dgp = _rms_bwd(erefs[0][...], erefs[1][...], acc)
    orefs[0][...] = dx + erefs[2][...]
    _accumulate(orefs[1], dgp, ids[0] == 0)


def _rms_bwd_io(x, g, dres, T, tm):
    row = lambda i, j, r: (i, 0)
    vec = lambda i, j, r: (0, 0)
    extras = [(x, _spec((tm, D), row)), (g, _spec((1, D), vec)), (dres, _spec((tm, D), row))]
    outs = [(_sds((T, D), F32), _spec((tm, D), row)), (_sds((1, D), F32), _spec((1, D), vec))]
    return extras, outs


def _ffn_bwd(tag, dx_out, h, a, b, s, w1buf, w1_idx, w3buf, w3_idx, wrow2, w2_idx, x_in, g, big):
    T = dx_out.shape[0]
    dab = _ffn_bwd_mid(tag + "_bwd_mid", dx_out, wrow2, w2_idx, a, b)

    def half_scale(acc, erefs, orefs, ids):
        orefs[0][...] = (0.5 * acc).astype(orefs[0].dtype)

    dw_grid = (D_FF // DW_BLK, 1, 1)
    dw_out = [(_sds((D_FF, D), GRAD_WIRE_DTYPE), _spec((DW_BLK, D), lambda j, n, r: (j, 0)))]
    tokens = _spec((T, D), lambda j, n, r: (0, 0))
    big[tag + "_w2"] = _gemm(
        tag + "_dw2", [(s, _spec((T, DW_BLK), lambda j, n, r: (0, j)), dx_out, tokens, "tn")],
        dw_grid, dw_out, (DW_BLK, D), (), half_scale)[0].reshape(1, N_CHIPS, FF_BLK, D)
    for widx, wname in ((0, "_w1"), (1, "_w3")):
        big[tag + wname] = _gemm(
            tag + "_d" + wname[1:],
            [(dab, _spec((None, T, DW_BLK), functools.partial(lambda w, j, n, r: (w, 0, j), widx)), h, tokens, "tn")],
            dw_grid, dw_out, (DW_BLK, D))[0].reshape(1, N_CHIPS, FF_BLK, D)
    tm = min(FFN_ROW_TILE, T)
    extras, outs = _rms_bwd_io(x_in, g, dx_out, T, tm)
    whole = lambda idx: _spec((N_CHIPS, None, FF_BLK, D), lambda i, j, r: (0, idx, 0, 0))
    dx_in, dg = _gemm(
        tag + "_dh",
        [(dab, _spec((None, tm, D_FF), lambda i, j, r: (0, i, 0)), w1buf, whole(w1_idx), "nn"),
         (dab, _spec((None, tm, D_FF), lambda i, j, r: (1, i, 0)), w3buf, whole(w3_idx), "nn")],
        (T // tm, 1, 1), outs, (tm, D), extras, _rms_bwd_epilogue)
    return dx_in, dg


def _proj_sq(name, a, wsq, idx, kind, out_dtype=F32, extras=(), epilogue=None, outs=None):
    M = a.shape[0]
    tm = min(ROW_TILE, M)
    if outs is None:
        outs = [(_sds((M, D), out_dtype), _spec((tm, D), lambda i, j, r: (i, 0)))]
    return _gemm(
        name,
        [(a, _spec((tm, D), lambda i, j, r: (i, 0)),
          wsq, _spec((N_CHIPS, None, SQ_BLK, D), lambda i, j, r: (0, idx, 0, 0)), kind)],
        (M // tm, 1, 1), outs, (tm, D), extras, epilogue)


def _dw_sq(name, a, b):
    M = a.shape[0]
    tk = M
    whole = _gemm(
        name,
        [(a, _spec((tk, D), lambda i, j, r: (r, 0)), b, _spec((tk, D), lambda i, j, r: (r, 0)), "tn")],
        (1, 1, M // tk),
        [(_sds((D, D), GRAD_WIRE_DTYPE), _spec((D, D), lambda i, j, r: (0, 0)))],
        (D, D))[0]
    return whole.reshape(N_CHIPS, SQ_BLK, D)


def _retention_constants(T):
    pos = jnp.arange(T, dtype=F32)
    inv_freq = ROPE_BASE ** (-jnp.arange(0, RET_DK, 2, dtype=F32) / RET_DK)
    ang = pos[:, None] * inv_freq[None, :]
    cosf = jnp.concatenate([jnp.cos(ang), jnp.cos(ang)], axis=1)
    sins = jnp.concatenate([-jnp.sin(ang), jnp.sin(ang)], axis=1)
    lg = jnp.log(1.0 - 2.0 ** (-5.0 - jnp.arange(RET_HEADS, dtype=F32)))
    p = jnp.arange(CHUNK, dtype=F32)
    rel = p[:, None] - p[None, :]
    dmat = jnp.where(rel[None] >= 0, jnp.exp(rel[None] * lg[:, None, None]), 0.0)
    kd = jnp.exp((CHUNK - 1.0 - p)[None, :] * lg[:, None])[:, :, None]
    qd = jnp.exp((p + 1.0)[None, :] * lg[:, None])[:, :, None]
    cd = jnp.exp(CHUNK * lg)[:, None, None]
    return cosf, sins, dmat, kd, qd, cd


def _rot(t, cosv, sinv):
    return t * cosv + pltpu.roll(t, RET_DK // 2, 1) * sinv


def _unrot(t, cosv, sinv):
    return t * cosv - pltpu.roll(t, RET_DK // 2, 1) * sinv


def _ret_const_specs(cm):
    whole = lambda shape: _spec(shape, lambda c: (0,) * len(shape))
    return [
        _spec((CHUNK, RET_DK), lambda c: (cm(c), 0)),
        _spec((CHUNK, RET_DK), lambda c: (cm(c), 0)),
        whole((RET_HEADS, CHUNK, CHUNK)), whole((RET_HEADS, CHUNK, 1)), whole((RET_HEADS, CHUNK, 1)),
        whole((RET_HEADS, 1, 1)),
    ]


def _head(h, width):
    return slice(h * width, (h + 1) * width)


def _ret_fwd(u, consts, ret_gn):
    T = u.shape[0]
    nC = T // CHUNK
    kscale = RET_DK ** -0.5

    def body(q_ref, k_ref, v_ref, g_ref, cos_ref, sin_ref, dm_ref, kd_ref, qd_ref, cd_ref, gn_ref,
             qr_ref, kr_ref, ret_ref, yr_ref, st_ref, state):
        @pl.when(pl.program_id(0) == 0)
        def _():
            state[...] = jnp.zeros_like(state)

        cosv, sinv = cos_ref[...], sin_ref[...]
        for h in range(RET_HEADS):
            hk, hv = _head(h, RET_DK), _head(h, RET_DV)
            q = _rot(q_ref[:, hk], cosv, sinv)
            k = _rot(k_ref[:, hk], cosv, sinv) * kscale
            v = v_ref[:, hv]
            qr_ref[:, hk] = q
            kr_ref[:, hk] = k
            prev = state[h]
            st_ref[h] = prev
            s = _dot(q, k, "nt") * dm_ref[h]
            ret = _dot(s, v, "nn") + _dot(q, prev, "nn") * qd_ref[h]
            state[h] = cd_ref[h] * prev + _dot(k * kd_ref[h], v, "tn")
            ret_ref[:, hv] = ret
            mu = jnp.mean(ret, axis=-1, keepdims=True)
            xc = ret - mu
            yn = xc * lax.rsqrt(jnp.mean(xc * xc, axis=-1, keepdims=True) + EPS)
            g = g_ref[:, hv]
            yr_ref[:, hv] = ((g * _sigmoid(g)) * (yn * gn_ref[:, hv])).astype(BF16)

    cm = lambda c: c
    qk_w, v_w = RET_HEADS * RET_DK, RET_HEADS * RET_DV
    in_specs = [
        _spec((CHUNK, qk_w), lambda c: (c, 0)), _spec((CHUNK, qk_w), lambda c: (c, 1)),
        _spec((CHUNK, v_w), lambda c: (c, 1)), _spec((CHUNK, v_w), lambda c: (c, 2)),
    ] + _ret_const_specs(cm) + [_spec((1, v_w), lambda c: (0, 0))]
    qk_out = _spec((CHUNK, qk_w), lambda c: (c, 0))
    v_out = _spec((CHUNK, v_w), lambda c: (c, 0))
    return _pcall(
        body, name="ret_fwd", grid=(nC,),
        in_specs=in_specs,
        out_specs=[qk_out, qk_out, v_out, v_out,
                   _spec((RET_HEADS, None, RET_DK, RET_DV), lambda c: (0, c, 0, 0))],
        out_shape=[_sds((T, qk_w), F32), _sds((T, qk_w), F32), _sds((T, v_w), F32), _sds((T, v_w), BF16),
                   _sds((RET_HEADS, nC, RET_DK, RET_DV), F32)],
        scratch_shapes=[pltpu.VMEM((RET_HEADS, RET_DK, RET_DV), F32)],
    )(u, u, u, u, *consts, ret_gn)


def _ret_bwd(dyr, ret, u, qr, kr, states, consts, ret_gn):
    T = u.shape[0]
    nC = T // CHUNK
    kscale = RET_DK ** -0.5

    def body(dyr_ref, ret_ref, g_ref, q_ref, k_ref, v_ref, st_ref,
             cos_ref, sin_ref, dm_ref, kd_ref, qd_ref, cd_ref, gn_ref,
             dq_ref, dk_ref, dv_ref, dg_ref, dgn_ref, gstate):
        first = pl.program_id(0) == 0

        @pl.when(first)
        def _():
            gstate[...] = jnp.zeros_like(gstate)

        cosv, sinv = cos_ref[...], sin_ref[...]
        dgn_parts = []
        for h in range(RET_HEADS):
            hk, hv = _head(h, RET_DK), _head(h, RET_DV)
            ret = ret_ref[:, hv]
            mu = jnp.mean(ret, axis=-1, keepdims=True)
            xc = ret - mu
            rs = lax.rsqrt(jnp.mean(xc * xc, axis=-1, keepdims=True) + EPS)
            yn = xc * rs
            gn = gn_ref[:, hv]
            g = g_ref[:, hv]
            sg = _sigmoid(g)
            dyr_v = dyr_ref[:, hv]
            dretn = dyr_v * (g * sg)
            dg_ref[:, hv] = (dyr_v * (yn * gn) * (sg * (1.0 + g * (1.0 - sg)))).astype(BF16)
            dgn_parts.append(jnp.sum(dretn * yn, axis=0, keepdims=True))
            dyn = dretn * gn
            d_o = rs * (dyn - jnp.mean(dyn, axis=-1, keepdims=True)
                        - yn * jnp.mean(dyn * yn, axis=-1, keepdims=True))

            q, k, v = q_ref[:, hk], k_ref[:, hk], v_ref[:, hv]
            dmat, kd, qd = dm_ref[h], kd_ref[h], qd_ref[h]
            prev = st_ref[h]
            gnext = gstate[h]
            s = _dot(q, k, "nt") * dmat
            ds = _dot(d_o, v, "nt") * dmat
            doq = d_o * qd
            dq = _dot(ds, k, "nn") + _dot(doq, prev, "nt")
            dk = _dot(ds, q, "tn") + _dot(v, gnext, "nt") * kd
            dv = _dot(s, d_o, "tn") + _dot(k * kd, gnext, "nn")
            gstate[h] = cd_ref[h] * gnext + _dot(q, doq, "tn")
            dq_ref[:, hk] = _unrot(dq, cosv, sinv).astype(BF16)
            dk_ref[:, hk] = _unrot(dk * kscale, cosv, sinv).astype(BF16)
            dv_ref[:, hv] = dv.astype(BF16)
        _accumulate(dgn_ref, jnp.concatenate(dgn_parts, axis=1), first)

    cm = lambda c: nC - 1 - c
    qk_w, v_w = RET_HEADS * RET_DK, RET_HEADS * RET_DV
    vspec = lambda blk: _spec((CHUNK, v_w), lambda c: (cm(c), blk))
    qspec = _spec((CHUNK, qk_w), lambda c: (cm(c), 0))
    in_specs = [vspec(0), vspec(0), vspec(2), qspec, qspec, vspec(1),
                _spec((RET_HEADS, None, RET_DK, RET_DV), lambda c: (0, cm(c), 0, 0)),
                ] + _ret_const_specs(cm) + [_spec((1, v_w), lambda c: (0, 0))]
    return _pcall(
        body, name="ret_bwd", grid=(nC,),
        in_specs=in_specs,
        out_specs=[qspec, qspec, vspec(0), vspec(0), _spec((1, v_w), lambda c: (0, 0))],
        out_shape=[_sds((T, qk_w), BF16), _sds((T, qk_w), BF16), _sds((T, v_w), BF16), _sds((T, v_w), BF16),
                   _sds((1, v_w), F32)],
        scratch_shapes=[pltpu.VMEM((RET_HEADS, RET_DK, RET_DV), F32)],
    )(dyr, ret, u, qr, kr, u, states, *consts, ret_gn)


def _shift_down(x, s):
    rows = lax.broadcasted_iota(jnp.int32, x.shape, 0)
    return jnp.where(rows >= s, pltpu.roll(x, s, 0), 0.0)


def _shift_up(x, s):
    n = x.shape[0]
    rows = lax.broadcasted_iota(jnp.int32, x.shape, 0)
    return jnp.where(rows < n - s, pltpu.roll(x, n - s, 0), 0.0)


def _lru_specs(T):
    col = lambda off: _spec((T, LRU_BLOCK), lambda g: (0, off + g))
    vec = _spec((1, LRU_BLOCK), lambda g: (0, g))
    wblk = _spec((None, LRU_BLOCK, LRU_BLOCK), lambda g: (g, 0, 0))
    cw = _spec((CONV_TAPS, LRU_BLOCK), lambda g: (0, g))
    return col, vec, wblk, cw


def _lru_gates_fwd(u, conv_w, conv_b, w_r, b_r, w_i, b_i, lam):
    T = u.shape[0]
    col, vec, wblk, cw = _lru_specs(T)

    def body(x_ref, cw_ref, cb_ref, wr_ref, br_ref, wi_ref, bi_ref, lam_ref,
             xc_ref, r_ref, i_ref, a_ref, bx_ref):
        x = x_ref[...]
        w = cw_ref[...]
        xc = (_shift_down(x, 3) * w[0:1] + _shift_down(x, 2) * w[1:2] + _shift_down(x, 1) * w[2:3]
              + x * w[3:4] + cb_ref[...])
        r = _sigmoid(_dot(xc, wr_ref[...], "nn") + br_ref[...])
        i = _sigmoid(_dot(xc, wi_ref[...], "nn") + bi_ref[...])
        la = (-LRU_C) * r * _softplus(-lam_ref[...])
        xc_ref[...] = xc
        r_ref[...] = r
        i_ref[...] = i
        a_ref[...] = jnp.exp(la)
        bx_ref[...] = jnp.sqrt(-_expm1(2.0 * la)) * (i * xc)

    out = col(0)
    return _pcall(
        body, name="lru_gates_fwd", grid=(LRU_BLOCKS,),
        in_specs=[col(24), cw, vec, wblk, vec, wblk, vec, vec],
        out_specs=[out] * 5,
        out_shape=[_sds((T, D), F32)] * 5,
    )(u, conv_w, conv_b, w_r, b_r, w_i, b_i, lam)


def _lru_scan(name, a3, b3, reverse):
    T = a3.shape[0]
    nt = T // SCAN_TILE
    unroll = 8

    def body(a_ref, b_ref, o_ref, carry):
        @pl.when(pl.program_id(0) == 0)
        def _():
            carry[...] = jnp.zeros_like(carry)

        if not reverse:
            def step(t, h):
                h = a_ref[t] * h + b_ref[t]
                o_ref[t] = h
                return h
        else:
            def step(k, c):
                t = SCAN_TILE - 1 - k
                l = b_ref[t] + c
                o_ref[t] = l
                return a_ref[t] * l
        carry[...] = lax.fori_loop(0, SCAN_TILE, step, carry[...], unroll=unroll)

    idx = (lambda i: (nt - 1 - i, 0, 0)) if reverse else (lambda i: (i, 0, 0))
    blk = _spec((SCAN_TILE, LRU_BLOCKS, LRU_BLOCK), idx)
    return _pcall(
        body, name=name, grid=(nt,),
        in_specs=[blk, blk], out_specs=blk,
        out_shape=_sds((T, LRU_BLOCKS, LRU_BLOCK), F32),
        scratch_shapes=[pltpu.VMEM((LRU_BLOCKS, LRU_BLOCK), F32)],
    )(a3, b3)


def _lru_gates_bwd(lmb, hl, a, r, i, xc, u, conv_w, w_r, w_i, lam):
    T = u.shape[0]
    col, vec, wblk, cw = _lru_specs(T)

    def body(l_ref, h_ref, a_ref, r_ref, i_ref, xc_ref, x_ref, cw_ref, wr_ref, wi_ref, lam_ref,
             dx_ref, dwr_ref, dwi_ref, dvec_ref, dcw_ref):
        l = l_ref[...]
        av, rv, iv, xc = a_ref[...], r_ref[...], i_ref[...], xc_ref[...]
        lam_v = lam_ref[...]
        sp = _softplus(-lam_v)
        la = (-LRU_C) * rv * sp
        mult = jnp.sqrt(-_expm1(2.0 * la))
        da = l * _shift_down(h_ref[...], 1)
        dmult = l * (iv * xc)
        di = l * mult * xc
        dxc = l * mult * iv
        dla = da * av - dmult * (av * av) / mult
        dzr = (dla * ((-LRU_C) * sp)) * rv * (1.0 - rv)
        dzi = di * iv * (1.0 - iv)
        dsp = jnp.sum(dla * ((-LRU_C) * rv), axis=0, keepdims=True)
        dlam = dsp * (-_sigmoid(-lam_v))
        dwr_ref[...] = _dot(xc, dzr, "tn")
        dwi_ref[...] = _dot(xc, dzi, "tn")
        dxc = dxc + _dot(dzr, wr_ref[...], "nt") + _dot(dzi, wi_ref[...], "nt")
        x = x_ref[...]
        w = cw_ref[...]
        dx = (dxc * w[3:4] + _shift_up(dxc, 1) * w[2:3] + _shift_up(dxc, 2) * w[1:2]
              + _shift_up(dxc, 3) * w[0:1])
        dx_ref[...] = dx.astype(BF16)
        dvec_ref[...] = jnp.concatenate(
            [jnp.sum(dzr, axis=0, keepdims=True), jnp.sum(dzi, axis=0, keepdims=True), dlam,
             jnp.sum(dxc, axis=0, keepdims=True)], axis=0)
        dcw_ref[...] = jnp.concatenate(
            [jnp.sum(dxc * _shift_down(x, 3 - tap), axis=0, keepdims=True) if tap < 3
             else jnp.sum(dxc * x, axis=0, keepdims=True) for tap in range(CONV_TAPS)], axis=0)

    c0 = col(0)
    return _pcall(
        body, name="lru_gates_bwd", grid=(LRU_BLOCKS,),
        in_specs=[c0, c0, c0, c0, c0, c0, col(24), cw, wblk, wblk, vec],
        out_specs=[c0, wblk, wblk, cw, cw],
        out_shape=[_sds((T, D), BF16), _sds((LRU_BLOCKS, LRU_BLOCK, LRU_BLOCK), F32),
                   _sds((LRU_BLOCKS, LRU_BLOCK, LRU_BLOCK), F32), _sds((4, D), F32), _sds((CONV_TAPS, D), F32)],
    )(lmb, hl, a, r, i, xc, u, conv_w, w_r, w_i, lam)


def _xattn_probs(q, k):
    sc = _dot(q, k, "nt") * (X_HD ** -0.5)
    e = jnp.exp(sc - jnp.max(sc, axis=-1, keepdims=True))
    return e / jnp.sum(e, axis=-1, keepdims=True)


def _xattn_fwd(xq, xk, xv):
    T = xq.shape[0]
    tq = ROW_TILE
    M = xk.shape[0]

    def body(q_ref, k_ref, v_ref, o_ref):
        p = _xattn_probs(q_ref[...], k_ref[...])
        o_ref[...] = _dot(p, v_ref[...], "nn").astype(BF16)

    qs = _spec((tq, X_HD), lambda h, i: (i, h))
    kv = _spec((M, X_HD), lambda h, i: (0, h))
    return _pcall(
        body, name="xattn_fwd", grid=(X_HEADS, T // tq),
        in_specs=[qs, kv, kv], out_specs=qs, out_shape=_sds((T, D), BF16),
    )(xq, xk, xv)


def _xattn_bwd(xq, xk, xv, dxo):
    T = xq.shape[0]
    tq = ROW_TILE
    M = xk.shape[0]

    def body(q_ref, k_ref, v_ref, do_ref, dq_ref, dk_ref, dv_ref):
        first = pl.program_id(1) == 0
        q, k, v, do = q_ref[...], k_ref[...], v_ref[...], do_ref[...]
        p = _xattn_probs(q, k)
        dp = _dot(do, v, "nt")
        ds = p * (dp - jnp.sum(dp * p, axis=-1, keepdims=True)) * (X_HD ** -0.5)
        dq_ref[...] = _dot(ds, k, "nn").astype(BF16)
        _accumulate(dk_ref, _dot(ds, q, "tn"), first)
        _accumulate(dv_ref, _dot(p, do, "tn"), first)

    qs = _spec((tq, X_HD), lambda h, i: (i, h))
    kv = _spec((M, X_HD), lambda h, i: (0, h))
    return _pcall(
        body, name="xattn_bwd", grid=(X_HEADS, T // tq),
        in_specs=[qs, kv, kv, qs], out_specs=[qs, kv, kv],
        out_shape=[_sds((T, D), BF16), _sds((M, D), F32), _sds((M, D), F32)],
    )(xq, xk, xv, dxo)


def _final_loss(x, g, tgt):
    T = x.shape[0]
    tm = ROW_TILE

    def fn(irefs, orefs, ids):
        xv, gv = irefs[0][...], irefs[1][...]
        err = _rms_fwd(xv, gv) - irefs[2][...]
        lp = 0.5 * jnp.sum(jnp.mean(err * err, axis=-1, keepdims=True), axis=0, keepdims=True)
        first = ids[0] == 0
        _accumulate(orefs[0], jnp.broadcast_to(lp, (1, 128)), first)
        dx, dgp = _rms_bwd(xv, gv, err * (1.0 / D))
        orefs[1][...] = dx
        _accumulate(orefs[2], dgp, first)

    row = _spec((tm, D), lambda i: (i, 0))
    vec = _spec((1, D), lambda i: (0, 0))
    return _rowwise(
        "final_loss", fn, [(x, row), (g, vec), (tgt, row)],
        [(_sds((1, 128), F32), _spec((1, 128), lambda i: (0, 0))), (_sds((T, D), F32), row),
         (_sds((1, D), F32), vec)],
        (T // tm,))


def _adamw(name, w, g, m, v):
    R, C = w.shape
    tr = R
    for cand in (512, 352, 256):
        if R % cand == 0:
            tr = cand
            break

    def fn(irefs, orefs, ids):
        delta, mn, vn = _adamw_update(*(r[...] for r in irefs))
        orefs[0][...] = delta
        orefs[1][...] = mn
        orefs[2][...] = vn

    blk = _spec((tr, C), lambda i: (i, 0))
    return _rowwise(name, fn, [(w, blk), (g, blk), (m, blk), (v, blk)],
                    [(_sds((R, C), F32), blk)] * 3, (R // tr,))


def _adamw_update(wv, gv, mv, vv):
    c1 = 1.0 - ADAM_B1 ** ADAM_STEP
    c2 = 1.0 - ADAM_B2 ** ADAM_STEP
    mn = ADAM_B1 * mv + (1.0 - ADAM_B1) * gv
    vn = ADAM_B2 * vv + (1.0 - ADAM_B2) * (gv * gv)
    delta = -ADAM_LR * ((mn / c1) / (jnp.sqrt(vn / c2) + ADAM_EPS) + ADAM_WD * wv)
    return delta, mn, vn


def _adamw_halves(name, w, mine, theirs, widx, m, v, core):
    R, C = w.shape
    H = R // 2
    tr = H
    while tr * C * 4 > (1 << 20) and tr % 16 == 0:
        tr //= 2
    nb = H // tr

    def body(core_ref, w_ref, mine_ref, theirs_ref, m_ref, v_ref, g_out, d_out, m_out, v_out):
        gv = jnp.where(pl.program_id(0) == core_ref[0], mine_ref[...], theirs_ref[...])
        delta, mn, vn = _adamw_update(w_ref[...], gv, m_ref[...], v_ref[...])
        g_out[...] = gv
        d_out[...] = delta
        m_out[...] = mn
        v_out[...] = vn

    full = pl.BlockSpec((tr, C), lambda h, i, core_ref: (h * nb + i, 0))
    mine_spec = pl.BlockSpec((None, tr, C), lambda h, i, core_ref: (widx, jnp.where(h == core_ref[0], i, 0), 0))
    theirs_spec = pl.BlockSpec((None, tr, C), lambda h, i, core_ref: (widx, jnp.where(h == core_ref[0], 0, i), 0))
    return _pcall(
        body, name=name, grid=(2, nb), num_prefetch=1,
        in_specs=[full, mine_spec, theirs_spec, full, full], out_specs=[full] * 4,
        out_shape=[_sds((R, C), F32)] * 4,
    )(core, w, mine, theirs, m, v)


def _rmsnorm(name, x, g):
    M = x.shape[0]
    tm = min(ROW_TILE, M)

    def fn(irefs, orefs, ids):
        orefs[0][...] = _rms_fwd(irefs[0][...], irefs[1][...]).astype(BF16)

    row = _spec((tm, D), lambda i: (i, 0))
    return _rowwise(name, fn, [(x, row), (g, _spec((1, D), lambda i: (0, 0)))],
                    [(_sds((M, D), BF16), row)], (M // tm,))[0]


WEIGHT_AT = {
    "ffn1_w1": ("col1", 0), "ffn1_w3": ("col1", 1), "ffn1_w2": ("row2a", 0),
    "w_ret_o": ("sqA", 0), "w_lru_o": ("sqA", 1), "w_out": ("sqA", 2),
    "w_xq": ("sqB", 0), "w_xk": ("sqB", 1), "w_xv": ("sqC", 0), "w_xo": ("sqC", 1),
    "ffn2_w1": ("col2a", 0), "ffn2_w3": ("col2b", 0), "ffn2_w2": ("row2b", 0),
}


def _local_step(x, mem, tgt, gw, sm, big):
    T = x.shape[0]
    tm = ROW_TILE

    def wt(name):
        key, idx = WEIGHT_AT[name]
        return gw[key], idx

    row3 = lambda i, j, r: (i, 0)
    vec3 = lambda i, j, r: (0, 0)
    rowD = _spec((tm, D), row3)
    vecD = _spec((1, D), vec3)

    def residual_norm(acc, erefs, orefs, ids):
        xo = erefs[0][...] + acc
        orefs[0][...] = xo
        orefs[1][...] = _rms_fwd(xo, erefs[1][...]).astype(BF16)

    def res_norm_io(x_res, g):
        return ([(x_res, rowD), (g, vecD)],
                [(_sds((T, D), F32), rowD), (_sds((T, D), BF16), rowD)])

    h1 = _rmsnorm("ffn1_norm", x, sm["ffn1_norm"])
    a1, b1, s1 = _ffn_up("ffn1_up", h1, *wt("ffn1_w1"), *wt("ffn1_w3"))
    x1, h2 = _ffn_down("ffn1_down", s1, *wt("ffn1_w2"), x, sm["mix_norm"])

    tw = min(WIDE_ROW_TILE, T)
    wideD = _spec((tw, D), row3)
    u = _gemm(
        "mix_in",
        [(h2, wideD, gw["win"], _spec((None, None, IN_BLK, D), lambda i, j, r: (j, 0, 0, 0)), "nt")],
        (T // tw, N_CHIPS, 1),
        [(_sds((T, 5120), F32), _spec((tw, IN_BLK), lambda i, j, r: (i, j)))], (tw, IN_BLK))[0]

    consts = _retention_constants(T)
    qr, kr, ret, yr, states = _ret_fwd(u, consts, sm["ret_gn"])

    conv_w = gw["conv"][:, 0].transpose(1, 0, 2).reshape(CONV_TAPS, D)
    xc, rg, ig, av, bx = _lru_gates_fwd(u, conv_w, sm["conv_b"], sm["w_rgate"], sm["b_rgate"],
                                        sm["w_igate"], sm["b_igate"], sm["lru_lambda"])
    a3 = av.reshape(T, LRU_BLOCKS, LRU_BLOCK)
    hl = _lru_scan("lru_scan_fwd", a3, bx.reshape(T, LRU_BLOCKS, LRU_BLOCK), False).reshape(T, D)

    row1 = _spec((tm, D), lambda i: (i, 0))
    glru1 = _spec((tm, D), lambda i: (i, 4))

    def lru_out(irefs, orefs, ids):
        gl, _ = _gelu_and_grad(irefs[1][...])
        orefs[0][...] = (irefs[0][...] * gl).astype(BF16)

    yl = _rowwise("lru_out", lru_out, [(hl, row1), (u, glru1)], [(_sds((T, D), BF16), row1)], (T // tm,))[0]

    def gate_epilogue(acc, erefs, orefs, ids):
        orefs[0][...] = _sigmoid(acc + erefs[0][...])

    gates = _gemm(
        "mix_gates",
        [(h2, wideD, gw["wbg"], _spec((None, None, BG_BLK, D), lambda i, j, r: (j, 0, 0, 0)), "nt")],
        (T // tw, N_CHIPS, 1),
        [(_sds((T, 2 * D), F32), _spec((tw, BG_BLK), lambda i, j, r: (i, j)))], (tw, BG_BLK),
        [(sm["b_branch_gate"], _spec((1, BG_BLK), lambda i, j, r: (0, j)))], gate_epilogue)[0]

    y_ret = _proj_sq("y_ret", yr, *wt("w_ret_o"), "nn")[0]

    def merge_epilogue(acc, erefs, orefs, ids):
        orefs[0][...] = acc
        orefs[1][...] = (erefs[0][...] * erefs[2][...] + erefs[1][...] * acc).astype(BF16)

    y_lru, merged = _proj_sq(
        "y_lru", yl, *wt("w_lru_o"), "nn",
        extras=[(gates, _spec((tm, D), lambda i, j, r: (i, 0))), (gates, _spec((tm, D), lambda i, j, r: (i, 1))),
                (y_ret, rowD)],
        epilogue=merge_epilogue,
        outs=[(_sds((T, D), F32), rowD), (_sds((T, D), BF16), rowD)])

    ex, ou = res_norm_io(x1, sm["xattn_norm"])
    x2, hq = _proj_sq("mix_out", merged, *wt("w_out"), "nn", extras=ex, epilogue=residual_norm, outs=ou)

    m = _rmsnorm("mem_norm", mem, sm["mem_norm"])
    xq = _proj_sq("xq", hq, *wt("w_xq"), "nn", BF16)[0]
    xk = _proj_sq("xk", m, *wt("w_xk"), "nn", BF16)[0]
    xv = _proj_sq("xv", m, *wt("w_xv"), "nn", BF16)[0]
    xo = _xattn_fwd(xq, xk, xv)
    ex, ou = res_norm_io(x2, sm["ffn2_norm"])
    x3, h3 = _proj_sq("xattn_out", xo, *wt("w_xo"), "nn", extras=ex, epilogue=residual_norm, outs=ou)

    a2, b2, s2 = _ffn_up("ffn2_up", h3, *wt("ffn2_w1"), *wt("ffn2_w3"))
    x4 = _ffn_down("ffn2_down", s2, *wt("ffn2_w2"), x3)[0]
    loss, dx4, dg_final = _final_loss(x4, sm["final_norm"], tgt)

    dx3, dg_ffn2 = _ffn_bwd("ffn2", dx4, h3, a2, b2, s2, *wt("ffn2_w1"), *wt("ffn2_w3"),
                            *wt("ffn2_w2"), x3, sm["ffn2_norm"], big)

    dxo = _proj_sq("d_xo", dx3, *wt("w_xo"), "nt", BF16)[0]
    big["w_xo"] = _dw_sq("dw_xo", xo, dx3)[None]
    dxq, dxk, dxv = _xattn_bwd(xq, xk, xv, dxo)
    big["w_xq"] = _dw_sq("dw_xq", hq, dxq)[None]
    ex, ou = _rms_bwd_io(x2, sm["xattn_norm"], dx3, T, tm)
    dx2, dg_xattn = _proj_sq("d_hq", dxq, *wt("w_xq"), "nt", extras=ex, epilogue=_rms_bwd_epilogue, outs=ou)
    big["w_xk"] = _dw_sq("dw_xk", m, dxk)[None]
    big["w_xv"] = _dw_sq("dw_xv", m, dxv)[None]

    M = mem.shape[0]

    def mem_norm_epilogue(acc, erefs, orefs, ids):
        _, dgp = _rms_bwd(erefs[0][...], erefs[1][...], acc)
        orefs[0][...] = dgp

    wsq_spec = lambda idx: _spec((N_CHIPS, None, SQ_BLK, D), lambda i, j, r: (0, idx, 0, 0))
    memD = _spec((M, D), row3)
    dg_mem = _gemm(
        "d_mem_norm",
        [(dxk, memD, wt("w_xk")[0], wsq_spec(wt("w_xk")[1]), "nt"),
         (dxv, memD, wt("w_xv")[0], wsq_spec(wt("w_xv")[1]), "nt")],
        (1, 1, 1), [(_sds((1, D), F32), vecD)], (M, D),
        [(mem, memD), (sm["mem_norm"], vecD)], mem_norm_epilogue)[0]

    def merged_bwd_epilogue(acc, erefs, orefs, ids):
        gr, gl, yrv, ylv = (e[...] for e in erefs)
        orefs[0][...] = (acc * gr).astype(BF16)
        orefs[1][...] = (acc * gl).astype(BF16)
        dgr = acc * yrv * gr * (1.0 - gr)
        dgl = acc * ylv * gl * (1.0 - gl)
        orefs[2][:, :D] = dgr.astype(BF16)
        orefs[2][:, D:] = dgl.astype(BF16)
        dbb = jnp.concatenate([jnp.sum(dgr, axis=0, keepdims=True), jnp.sum(dgl, axis=0, keepdims=True)], axis=1)
        _accumulate(orefs[3], dbb, ids[0] == 0)

    dy_ret, dy_lru, dgpre, db_bg = _proj_sq(
        "d_merged", dx2, *wt("w_out"), "nt",
        extras=[(gates, _spec((tm, D), lambda i, j, r: (i, 0))), (gates, _spec((tm, D), lambda i, j, r: (i, 1))),
                (y_ret, rowD), (y_lru, rowD)],
        epilogue=merged_bwd_epilogue,
        outs=[(_sds((T, D), BF16), rowD), (_sds((T, D), BF16), rowD),
              (_sds((T, 2 * D), BF16), _spec((tm, 2 * D), row3)),
              (_sds((1, 2 * D), F32), _spec((1, 2 * D), vec3))])
    big["w_branch_gate"] = _gemm(
        "dw_bg",
        [(h2, _spec((T, D), lambda j, n, r: (r, 0)), dgpre, _spec((T, BG_BLK), lambda j, n, r: (r, j)), "tn")],
        (N_CHIPS, 1, 1),
        [(_sds((N_CHIPS, D, BG_BLK), GRAD_WIRE_DTYPE), _spec((None, D, BG_BLK), lambda j, n, r: (j, 0, 0)))],
        (D, BG_BLK))[0][None]
    big["w_out"] = _dw_sq("dw_out", merged, dx2)[None]
    dyr = _proj_sq("d_yr", dy_ret, *wt("w_ret_o"), "nt")[0]
    big["w_ret_o"] = _dw_sq("dw_ret_o", yr, dy_ret)[None]
    dyl = _proj_sq("d_yl", dy_lru, *wt("w_lru_o"), "nt")[0]
    big["w_lru_o"] = _dw_sq("dw_lru_o", yl, dy_lru)[None]

    dq, dk, dv, dgr, dg_retgn = _ret_bwd(dyr, ret, u, qr, kr, states, consts, sm["ret_gn"])

    def lru_out_bwd(irefs, orefs, ids):
        gl, dgl = _gelu_and_grad(irefs[2][...])
        dyl_v = irefs[0][...]
        orefs[0][...] = dyl_v * gl
        orefs[1][...] = (dyl_v * irefs[1][...] * dgl).astype(BF16)

    dhl, dglru = _rowwise("lru_out_bwd", lru_out_bwd, [(dyl, row1), (hl, row1), (u, glru1)],
                          [(_sds((T, D), F32), row1), (_sds((T, D), BF16), row1)], (T // tm,))
    lmb = _lru_scan("lru_scan_bwd", a3, dhl.reshape(T, LRU_BLOCKS, LRU_BLOCK), True).reshape(T, D)
    dxl, dw_r, dw_i, dvec, dcw = _lru_gates_bwd(lmb, hl, av, rg, ig, xc, u, conv_w,
                                                sm["w_rgate"], sm["w_igate"], sm["lru_lambda"])

    du = jnp.concatenate([dq, dk, dv, dgr, dxl, dglru], axis=1)
    tk = T
    big["w_in"] = _gemm(
        "dw_in",
        [(h2, _spec((tk, D), lambda j, n, r: (r, 0)), du, _spec((tk, IN_BLK), lambda j, n, r: (r, j)), "tn")],
        (N_CHIPS, 1, T // tk),
        [(_sds((N_CHIPS, D, IN_BLK), GRAD_WIRE_DTYPE), _spec((None, D, IN_BLK), lambda j, n, r: (j, 0, 0)))],
        (D, IN_BLK))[0][None]
    tf = min(FFN_ROW_TILE, T)
    ex, ou = _rms_bwd_io(x1, sm["mix_norm"], dx2, T, tf)
    dx1, dg_mix = _gemm(
        "d_h2",
        [(du, _spec((tf, 5120), row3), gw["win"], _spec((N_CHIPS, None, IN_BLK, D), lambda i, j, r: (0, 0, 0, 0)), "nn"),
         (dgpre, _spec((tf, 2 * D), row3), gw["wbg"], _spec((N_CHIPS, None, BG_BLK, D), lambda i, j, r: (0, 0, 0, 0)),
          "nn")],
        (T // tf, 1, 1), ou, (tf, D), ex, _rms_bwd_epilogue)

    grad_x, dg_ffn1 = _ffn_bwd("ffn1", dx1, h1, a1, b1, s1, *wt("ffn1_w1"), *wt("ffn1_w3"),
                               *wt("ffn1_w2"), x, sm["ffn1_norm"], big)

    small = {
        "ffn1_norm": dg_ffn1, "mix_norm": dg_mix, "ret_gn": dg_retgn, "conv_b": dvec[3:4],
        "b_rgate": dvec[0:1], "b_igate": dvec[1:2], "lru_lambda": dvec[2:3], "xattn_norm": dg_xattn,
        "mem_norm": dg_mem, "ffn2_norm": dg_ffn2, "final_norm": dg_final, "b_branch_gate": db_bg,
        "conv_w": dcw, "w_rgate": dw_r, "w_igate": dw_i,
    }
    return loss, grad_x, small


ANY_SPEC = pl.BlockSpec(memory_space=pl.ANY)
VMEM_SPEC = pl.BlockSpec(memory_space=pltpu.VMEM)
N_PEER_CHIPS = N_CHIPS - 1


def _mesh_position():
    x, y, c = lax.axis_index("x"), lax.axis_index("y"), lax.axis_index("c")
    chips = [(1 - x, y), (x, 1 - y), (1 - x, 1 - y)]
    return x, y, c, chips


def _chip_index(x, y):
    return 2 * x + y


def _rows_half(ref, axis, h):
    n = ref.shape[axis] // 2
    idx = [slice(None)] * len(ref.shape)
    idx[axis] = pl.ds(pl.multiple_of(h * n, 16), n)
    return ref.at[tuple(idx)]


def _remote(src, dst, send_sem, recv_sem, device):
    return pltpu.make_async_remote_copy(src_ref=src, dst_ref=dst, send_sem=send_sem, recv_sem=recv_sem,
                                        device_id=device, device_id_type=MESH)


def _gather_chips_task(shards, split, landed, part=0, nparts=1):
    keys = list(shards)
    n = len(keys)

    def operands():
        if part:
            return [shards[k] for k in keys] + [landed[k] for k in keys]
        chip_me = _chip_index(lax.axis_index("x"), lax.axis_index("y"))
        bases = [lax.dynamic_update_slice(lax.empty((N_CHIPS,) + shards[k].shape, shards[k].dtype), shards[k][None],
                                          (chip_me,) + (0,) * shards[k].ndim) for k in keys]
        return [shards[k] for k in keys] + bases

    def my_rows(ref, c):
        rows = ref.shape[1] // (2 * nparts)
        return ref.at[:, pl.ds(pl.multiple_of((c * nparts + part) * rows, 16), rows), :]

    def make_direct(ins, outs, send_sem, recv_sem):
        x, y, c, chips = _mesh_position()
        s_me = _chip_index(x, y)
        starts, arrivals = [], []
        for g in range(n):
            for k, chip in enumerate(chips):
                sems = (send_sem(3 * g + k), recv_sem(3 * g + k))
                starts.append(functools.partial(_remote, ins[g], outs[g].at[s_me], *sems, (*chip, c)))
                got = outs[g].at[_chip_index(*chip)]
                arrivals.append(functools.partial(_remote, got, got, *sems, (*chip, c)))
        return starts, arrivals

    def axis_neighbours(x, y, c):
        flip = lambda v, f: v + f * (1 - 2 * v)
        return (flip(x, 1 - c), flip(y, c)), (flip(x, c), flip(y, 1 - c))

    def make_swap(ins, outs, send_sem, recv_sem):
        x, y, c, _ = _mesh_position()
        first, _ = axis_neighbours(x, y, c)
        starts, arrivals = [], []
        for g in range(n):
            sems = (send_sem(3 * g), recv_sem(3 * g))
            starts.append(functools.partial(_remote, my_rows(ins[g], c), my_rows(outs[g].at[_chip_index(x, y)], c),
                                            *sems, (*first, c)))
            got = my_rows(outs[g].at[_chip_index(*first)], c)
            arrivals.append(functools.partial(_remote, got, got, *sems, (*first, c)))
        return starts, arrivals

    def make_pass_on(ins, outs, send_sem, recv_sem):
        x, y, c, _ = _mesh_position()
        first, second = axis_neighbours(x, y, c)
        diagonal = (1 - x, 1 - y)
        starts, arrivals = [], []
        for g in range(n):
            half = lambda chip: my_rows(outs[g].at[_chip_index(*chip)], c)
            for k, (sent, arriving) in enumerate([((x, y), second), (first, diagonal)]):
                sems = (send_sem(3 * g + 1 + k), recv_sem(3 * g + 1 + k))
                src = my_rows(ins[g], c) if k == 0 else half(sent)
                starts.append(functools.partial(_remote, src, half(sent), *sems, (*second, c)))
                arrivals.append(functools.partial(_remote, half(arriving), half(arriving), *sems, (*second, c)))
        return starts, arrivals

    def finish(res):
        landed.update(zip(keys, res))

    shapes = lambda: [_sds((N_CHIPS,) + shards[k].shape, shards[k].dtype) for k in keys]
    aliases = {n + g: g for g in range(n)}
    if not split:
        return _Task("chips", operands, shapes, aliases, 3 * n, make_direct, finish)
    return _Task("neighbours", operands, shapes, aliases, 3 * n, make_swap, finish, make_second=make_pass_on)


def _gather_sibling_task(keys, landed, ready):
    n = len(keys)

    def make(ins, outs, send_sem, recv_sem):
        x, y, c, chips = _mesh_position()
        starts, arrivals = [], []
        for g in range(n):
            for k, chip in enumerate(chips):
                o = outs[g].at[_chip_index(*chip)]
                got, other = _rows_half(o, 1, c), _rows_half(o, 1, 1 - c)
                starts.append(functools.partial(_remote, got, got, send_sem(3 * g + k), recv_sem(3 * g + k),
                                                (x, y, 1 - c)))
                arrivals.append(functools.partial(_remote, other, other, send_sem(3 * g + k), recv_sem(3 * g + k),
                                                  (x, y, 1 - c)))
        return starts, arrivals

    def finish(res):
        ready.update(zip(keys, res))

    return _Task("sibling", lambda: [landed[k] for k in keys],
                 lambda: [_sds(landed[k].shape, landed[k].dtype) for k in keys],
                 {g: g for g in range(n)}, 3 * n, make, finish)


def _pair_swap_task(names, big, got):
    n = len(names)

    def make(ins, outs, send_sem, recv_sem):
        x, y, c, _ = _mesh_position()
        copies = [functools.partial(_remote, _rows_half(ins[a], 2, 1 - c), outs[a], send_sem(a), recv_sem(a),
                                    (x, y, 1 - c)) for a in range(n)]
        return copies, copies

    def shapes():
        return [_sds(big[k].shape[:2] + (big[k].shape[2] // 2, big[k].shape[3]), big[k].dtype) for k in names]

    return _Task("sibling", lambda: [big[k] for k in names], shapes, {}, n, make,
                 lambda res: got.update(zip(names, res)))


def _rs_pair_sum(name, full, got, core):
    nw, ns, R, C = full.shape
    half = R // 2

    def body(core_ref, a_ref, b_ref, o_ref):
        o_ref[...] = (a_ref[...].astype(F32) + b_ref[...].astype(F32)).astype(BF16)

    blk = lambda fn: pl.BlockSpec((None, None, half, C), fn)
    return _pcall(
        body, name=name, grid=(nw, ns), num_prefetch=1,
        in_specs=[blk(lambda w, s, core_ref: (w, s, core_ref[0], 0)), blk(lambda w, s, core_ref: (w, s, 0, 0))],
        out_specs=blk(lambda w, s, core_ref: (w, s, 0, 0)),
        out_shape=_sds((nw, ns, half, C), BF16),
    )(core, full, got)


def _chip_exchange_task(names, pair_sums, by_source, part=0, nparts=1):
    n = len(names)

    def rows(ref):
        h = ref.shape[1] // nparts
        return ref.at[:, pl.ds(part * h, h), :]

    def make(ins, outs, send_sem, recv_sem):
        x, y, c, chips = _mesh_position()
        s_me = _chip_index(x, y)
        starts, arrivals = [], []
        for a in range(n):
            for k, chip in enumerate(chips):
                s_k = _chip_index(*chip)
                starts.append(functools.partial(_remote, rows(ins[a].at[:, s_k]), rows(outs[a].at[:, s_me]),
                                                send_sem(3 * a + k), recv_sem(3 * a + k), (*chip, c)))
                got = rows(outs[a].at[:, s_k])
                arrivals.append(functools.partial(_remote, got, got, send_sem(3 * a + k), recv_sem(3 * a + k),
                                                  (*chip, c)))
        return starts, arrivals

    def operands():
        return [pair_sums[k] for k in names] + ([by_source[k] for k in names] if part else [])

    return _Task("chips", operands, lambda: [_sds(pair_sums[k].shape, pair_sums[k].dtype) for k in names],
                 {n + a: a for a in range(n)} if part else {}, 3 * n, make,
                 lambda res: by_source.update(zip(names, res)))


def _rs_chip_sum(name, own, parts, chip):
    nw, ns, H, C = parts.shape

    def body(chip_ref, own_ref, *rest):
        prefs, o_ref = rest[:ns], rest[ns]
        me = chip_ref[0]
        own_v = own_ref[...].astype(F32)
        tot = None
        for s in range(ns):
            term = jnp.where(me == s, own_v, prefs[s][...].astype(F32))
            tot = term if tot is None else tot + term
        o_ref[...] = tot

    blk = lambda fn: pl.BlockSpec((None, None, H, C), fn)

    def part_spec(s):
        return blk(lambda w, chip_ref: (w, jnp.where(chip_ref[0] == s, (s + 1) % ns, s), 0, 0))

    return _pcall(
        body, name=name, grid=(nw,), num_prefetch=1,
        in_specs=[blk(lambda w, chip_ref: (w, chip_ref[0], 0, 0))] + [part_spec(s) for s in range(ns)],
        out_specs=pl.BlockSpec((None, H, C), lambda w, chip_ref: (w, 0, 0)),
        out_shape=_sds((nw, H, C), F32),
    )(chip, own, *([parts] * ns))


def _pair_gather_task(names, halves, sibling_halves):
    n = len(names)

    def make(ins, outs, send_sem, recv_sem):
        x, y, c, _ = _mesh_position()
        copies = [functools.partial(_remote, ins[a], outs[a], send_sem(a), recv_sem(a), (x, y, 1 - c))
                  for a in range(n)]
        return copies, copies

    return _Task("sibling", lambda: [halves[k] for k in names], lambda: [_sds(halves[k].shape, F32) for k in names],
                 {}, n, make, lambda res: sibling_halves.update(zip(names, res)))


def _small_allreduce(arrs):
    n = len(arrs)
    per = 1 + 2 * N_PEER_CHIPS

    def body(*refs):
        v_refs, o_refs = refs[:n], refs[n:2 * n]
        sib, pair, part = refs[2 * n:3 * n], refs[3 * n:4 * n], refs[4 * n:5 * n]
        send_sems, recv_sems = refs[5 * n:]
        x, y, c, chips = _mesh_position()
        s_me = _chip_index(x, y)

        def quarter(ref, s):
            q = ref.shape[0] // N_CHIPS
            return ref.at[pl.ds(pl.multiple_of(s * q, 8), q)]

        def exchange(first_sem, src, dst_of, arrival_of):
            sems = lambda a, k: (send_sems.at[a * per + first_sem + k], recv_sems.at[a * per + first_sem + k])
            sends = [_remote(src(a, _chip_index(*chip)), dst_of(a, s_me), *sems(a, k), (*chip, c))
                     for a in range(n) for k, chip in enumerate(chips)]
            for cp in sends:
                cp.start()
            for a in range(n):
                for k, chip in enumerate(chips):
                    got = arrival_of(a, _chip_index(*chip))
                    _remote(got, got, *sems(a, k), (*chip, c)).wait_recv()
            for cp in sends:
                cp.wait_send()

        swaps = [_remote(v_refs[a], sib[a], send_sems.at[a * per], recv_sems.at[a * per], (x, y, 1 - c))
                 for a in range(n)]
        for cp in swaps:
            cp.start()
        for cp in swaps:
            cp.wait()
        for a in range(n):
            pair[a][...] = v_refs[a][...] + sib[a][...]
        exchange(1, lambda a, s_k: quarter(pair[a], s_k), lambda a, s: part[a].at[s], lambda a, s_k: part[a].at[s_k])
        for a in range(n):
            part[a][s_me] = quarter(pair[a], s_me)[...]
            q = o_refs[a].shape[0] // N_CHIPS
            o_refs[a][pl.ds(pl.multiple_of(s_me * q, 8), q), :] = (
                ((part[a][0] + part[a][1]) + part[a][2]) + part[a][3])
        exchange(1 + N_PEER_CHIPS, lambda a, s_k: quarter(o_refs[a], s_me), lambda a, s: quarter(o_refs[a], s),
                 lambda a, s_k: quarter(o_refs[a], s_k))

    shapes = [a.shape for a in arrs]
    return _pcall(
        body, name="small_allreduce", grid=(1,), own_peers=("sibling", "chips"),
        in_specs=[VMEM_SPEC] * n, out_specs=[VMEM_SPEC] * n, out_shape=[_sds(s, F32) for s in shapes],
        scratch_shapes=([pltpu.VMEM(s, F32) for s in shapes] * 2
                        + [pltpu.VMEM((N_CHIPS, s[0] // N_CHIPS, s[1]), F32) for s in shapes]
                        + [pltpu.SemaphoreType.DMA((n * per,)), pltpu.SemaphoreType.DMA((n * per,))]),
    )(*arrs)


TRANSPOSED_WEIGHTS = ("ffn1_w1", "ffn1_w3", "ffn2_w1", "ffn2_w3")
SMALL_LAYOUT = [("ffn1_norm", 1), ("mix_norm", 1), ("ret_gn", 1), ("conv_b", 1), ("b_rgate", 1), ("b_igate", 1),
                ("lru_lambda", 1), ("xattn_norm", 1), ("mem_norm", 1), ("ffn2_norm", 1), ("final_norm", 1),
                ("b_branch_gate", 2), ("conv_w", CONV_TAPS)]
SMALL_ROWS = 32
GATE_WEIGHTS = ("w_rgate", "w_igate")
WEIGHT_ORDER = ["ffn1_norm", "ffn1_w1", "ffn1_w3", "ffn1_w2", "mix_norm", "w_in", "ret_gn", "w_ret_o", "conv_w",
                "conv_b", "w_rgate", "b_rgate", "w_igate", "b_igate", "lru_lambda", "w_lru_o", "w_branch_gate",
                "b_branch_gate", "w_out", "xattn_norm", "mem_norm", "w_xq", "w_xk", "w_xv", "w_xo", "ffn2_norm",
                "ffn2_w1", "ffn2_w3", "ffn2_w2", "final_norm"]


def _pack_small(parts):
    rows = [parts[name].reshape(n, D) for name, n in SMALL_LAYOUT]
    used = sum(n for _, n in SMALL_LAYOUT)
    rows.append(jnp.zeros((SMALL_ROWS - used, D), F32))
    return jnp.concatenate(rows, axis=0)


def _unpack_small(packed, shapes):
    out, r = {}, 0
    for name, n in SMALL_LAYOUT:
        out[name] = packed[r:r + n].reshape(shapes[name])
        r += n
    return out


def kernel(x, mem, ffn1_norm, ffn1_w1, ffn1_w3, ffn1_w2, mix_norm, w_in, ret_gn, w_ret_o, conv_w, conv_b, w_rgate, b_rgate, w_igate, b_igate, lru_lambda, w_lru_o, w_branch_gate, b_branch_gate, w_out, xattn_norm, mem_norm, w_xq, w_xk, w_xv, w_xo, ffn2_norm, ffn2_w1, ffn2_w3, ffn2_w2, final_norm, loss_target, m_ffn1_norm, m_ffn1_w1, m_ffn1_w3, m_ffn1_w2, m_mix_norm, m_w_in, m_ret_gn, m_w_ret_o, m_conv_w, m_conv_b, m_w_rgate, m_b_rgate, m_w_igate, m_b_igate, m_lru_lambda, m_w_lru_o, m_w_branch_gate, m_b_branch_gate, m_w_out, m_xattn_norm, m_mem_norm, m_w_xq, m_w_xk, m_w_xv, m_w_xo, m_ffn2_norm, m_ffn2_w1, m_ffn2_w3, m_ffn2_w2, m_final_norm, v_ffn1_norm, v_ffn1_w1, v_ffn1_w3, v_ffn1_w2, v_mix_norm, v_w_in, v_ret_gn, v_w_ret_o, v_conv_w, v_conv_b, v_w_rgate, v_b_rgate, v_w_igate, v_b_igate, v_lru_lambda, v_w_lru_o, v_w_branch_gate, v_b_branch_gate, v_w_out, v_xattn_norm, v_mem_norm, v_w_xq, v_w_xk, v_w_xv, v_w_xo, v_ffn2_norm, v_ffn2_w1, v_ffn2_w3, v_ffn2_w2, v_final_norm):
    given = dict(locals())
    w = {n: given[n] for n in WEIGHT_ORDER}
    mom = {n: given["m_" + n] for n in WEIGHT_ORDER}
    var = {n: given["v_" + n] for n in WEIGHT_ORDER}
    chip = _chip_index(lax.axis_index("x"), lax.axis_index("y"))
    core = lax.axis_index("c").astype(jnp.int32).reshape(1)

    chip_id = chip.astype(jnp.int32).reshape(1)
    sm = {n: w[n] for n in ["ffn1_norm", "mix_norm", "ret_gn", "conv_b", "b_rgate", "b_igate", "lru_lambda",
                            "xattn_norm", "mem_norm", "ffn2_norm", "b_branch_gate"]}
    sm["final_norm"] = w["final_norm"].reshape(1, D)
    sm["w_rgate"] = w["w_rgate"][0]
    sm["w_igate"] = w["w_igate"][0]

    local = lambda a, n: jnp.swapaxes(a[0], 0, 1) if n in TRANSPOSED_WEIGHTS else a[0]
    stack = lambda names: jnp.stack([local(w[n], n) for n in names], axis=0).astype(BF16)
    shard = {"col1": stack(["ffn1_w1", "ffn1_w3"]), "row2a": stack(["ffn1_w2"]),
             "win": jnp.swapaxes(w["w_in"], 1, 2).astype(BF16), "wbg": jnp.swapaxes(w["w_branch_gate"], 1, 2).astype(BF16), "sqA": stack(["w_ret_o", "w_lru_o", "w_out"]),
             "sqB": stack(["w_xq", "w_xk"]), "sqC": stack(["w_xv", "w_xo"]), "col2a": stack(["ffn2_w1"]), "col2b": stack(["ffn2_w3"]),
             "row2b": stack(["ffn2_w2"]), "conv": w["conv_w"]}
    gw, landed = {}, {}
    over_chips = lambda keys: _gather_chips_task({k: shard[k] for k in keys}, True, landed)
    to_sibling = lambda keys: _gather_sibling_task(keys, landed, gw)

    big, got, pair_sums, by_source, halves, sibling_halves, outs = {}, {}, {}, {}, {}, {}, {}
    pair_swap = lambda names: _pair_swap_task(names, big, got)
    exchange = lambda names, part=0, nparts=1: _chip_exchange_task(names, pair_sums, by_source, part, nparts)
    pair_gather = lambda names: _pair_gather_task(names, halves, sibling_halves)

    def pair_sum(names):
        for n in names:
            pair_sums[n] = _rs_pair_sum("rs_pair_sum_" + n, big[n], got[n], core)

    def chip_sum(names):
        for n in names:
            halves[n] = _rs_chip_sum("rs_chip_sum_" + n, pair_sums[n], by_source[n], chip_id)

    def adamw(names):
        for n in names:
            res = _adamw_halves("adamw_" + n, local(w[n], n), halves[n], sibling_halves[n], 0, local(mom[n], n),
                                local(var[n], n), core)
            outs[n] = tuple((jnp.swapaxes(r, 0, 1) if n in TRANSPOSED_WEIGHTS else r)[None] for r in res)

    do = lambda fn, names: functools.partial(fn, names)
    ffn2_grads = ["ffn2_w2", "ffn2_w1", "ffn2_w3"]
    xattn_grads = ["w_xo", "w_xq", "w_xk", "w_xv"]
    mix_out_grads = ["w_branch_gate", "w_out", "w_ret_o", "w_lru_o"]
    conv_gather = _gather_chips_task({"conv": shard["conv"]}, False, gw)
    half = lambda key, part: _gather_chips_task({key: shard[key]}, True, landed, part, 2)
    plan = _Plan()
    plan.tasks = {
        "ag_first_chips": [over_chips(["col1", "row2a"])],
        "ag_first_sibling": [to_sibling(["col1", "row2a"])],
        "ffn1_up": [over_chips(["win"])],
        "ffn1_down": [to_sibling(["win"]), over_chips(["wbg"]), conv_gather],
        "mix_in": [to_sibling(["wbg"]), over_chips(["sqA"])],
        "ret_fwd": [to_sibling(["sqA"]), over_chips(["col2a"])],
        "lru_gates_fwd": [to_sibling(["col2a"]), over_chips(["sqB"])],
        "lru_scan_fwd": [to_sibling(["sqB"]), over_chips(["sqC"])],
        "mix_gates": [to_sibling(["sqC"]), half("col2b", 0)],
        "y_lru": [half("col2b", 1)],
        "xattn_fwd": [to_sibling(["col2b"])],
        "ffn2_up": [over_chips(["row2b"])],
        "ffn2_up_sibling": [to_sibling(["row2b"])],
        "ffn2_dh": [pair_swap(ffn2_grads)],
        "xattn_bwd": [exchange(["ffn2_w2"], 0, 2)],
        "d_hq": [exchange(["ffn2_w2"], 1, 2)],
        "d_merged": [exchange(["ffn2_w1"], 0, 2), pair_swap(xattn_grads)],
        "lru_out_bwd": [exchange(["w_xo"])],
        "ret_bwd": [exchange(["ffn2_w1"], 1, 2), exchange(["ffn2_w3"], 0, 2), pair_swap(mix_out_grads)],
        "lru_scan_bwd": [exchange(["ffn2_w3"], 1, 2)],
        "lru_gates_bwd": [exchange(["w_xq", "w_xk"]), pair_gather(ffn2_grads)],
        "dw_in": [exchange(["w_xv", "w_out"])],
        "d_h2": [exchange(["w_branch_gate", "w_ret_o", "w_lru_o"]), pair_swap(["w_in"]), pair_gather(xattn_grads)],
        "ffn1_bwd_mid": [exchange(["w_in"], 0, 2), pair_gather(mix_out_grads)],
        "ffn1_dw2": [exchange(["w_in"], 2, 4)],
        "ffn1_dw1": [exchange(["w_in"], 3, 4), pair_swap(["ffn1_w2"])],
        "ffn1_dw3": [exchange(["ffn1_w2"], 0, 2), pair_swap(["ffn1_w1"]), pair_gather(["w_in"])],
        "ffn1_dh": [exchange(["ffn1_w2"], 1, 2), exchange(["ffn1_w1"]), pair_swap(["ffn1_w3"])],
        "small_allreduce": [exchange(["ffn1_w3"]), pair_gather(["ffn1_w2"])],
        "rs_last_gather": [pair_gather(["ffn1_w1", "ffn1_w3"])],
    }
    plan.after = {
        "ffn2_up": [functools.partial(_comm_call, "ffn2_up_sibling")],
        "ffn2_dh": [do(pair_sum, ffn2_grads)],
        "d_merged": [do(pair_sum, xattn_grads)],
        "ret_bwd": [do(pair_sum, mix_out_grads)],
        "lru_scan_bwd": [do(chip_sum, ffn2_grads)],
        "lru_gates_bwd": [do(adamw, ffn2_grads)],
        "dw_in": [do(chip_sum, xattn_grads)],
        "d_h2": [do(chip_sum, mix_out_grads), do(pair_sum, ["w_in"]), do(adamw, xattn_grads)],
        "ffn1_bwd_mid": [do(adamw, mix_out_grads)],
        "ffn1_dw1": [do(chip_sum, ["w_in"]), do(pair_sum, ["ffn1_w2"])],
        "ffn1_dw3": [do(pair_sum, ["ffn1_w1"]), do(adamw, ["w_in"])],
        "ffn1_dh": [do(pair_sum, ["ffn1_w3"]), do(chip_sum, ["ffn1_w2"])],
        "small_allreduce": [do(chip_sum, ["ffn1_w1", "ffn1_w3"]), functools.partial(_comm_call, "rs_last_gather"),
                    do(adamw, ["ffn1_w2", "ffn1_w1", "ffn1_w3"])],
    }
    global _plan
    _plan = plan
    try:
        _comm_call("ag_first_chips")
        _comm_call("ag_first_sibling")
        loss_part, grad_x, small = _local_step(x[0], mem[0], loss_target[0], gw, sm, big)
        gate2d = lambda a: a.reshape(LRU_BLOCKS * LRU_BLOCK, LRU_BLOCK)
        small_sum, *gate_sums = _small_allreduce([_pack_small(small)] + [gate2d(small[n]) for n in GATE_WEIGHTS])
    finally:
        _plan = None
    assert not plan.tasks and not plan.after, (list(plan.tasks), list(plan.after))
    loss = lax.psum(loss_part[0, 0], ("x", "y", "c"))

    small_shapes = {n: w[n].shape for n, _ in SMALL_LAYOUT}
    small_shapes["conv_w"] = (CONV_TAPS, D)
    conv_grad = lax.dynamic_slice(small_sum[13:13 + CONV_TAPS], (0, chip * SQ_BLK), (CONV_TAPS, SQ_BLK))
    small_w = {n: w[n] for n, _ in SMALL_LAYOUT}
    small_m = {n: mom[n] for n, _ in SMALL_LAYOUT}
    small_v = {n: var[n] for n, _ in SMALL_LAYOUT}
    pad_cols = lambda a: jnp.pad(a[0], ((0, 0), (0, D - SQ_BLK)))
    for dct in (small_w, small_m, small_v):
        dct["conv_w"] = pad_cols(dct["conv_w"])
    g_pack = lax.dynamic_update_slice(small_sum, jnp.pad(conv_grad, ((0, 0), (0, D - SQ_BLK))), (13, 0))
    d_pack, m_pack, v_pack = _adamw("adamw_small", _pack_small(small_w), g_pack, _pack_small(small_m),
                                    _pack_small(small_v))
    unpacked = [_unpack_small(p, small_shapes) for p in (g_pack, d_pack, m_pack, v_pack)]
    for n, _ in SMALL_LAYOUT:
        if n == "conv_w":
            outs[n] = tuple(u[n][:, :SQ_BLK][None] for u in unpacked)
        else:
            outs[n] = tuple(u[n] for u in unpacked)
    for n, gsum in zip(GATE_WEIGHTS, gate_sums):
        d, nm, nv = _adamw("adamw_" + n, gate2d(w[n]), gsum, gate2d(mom[n]), gate2d(var[n]))
        outs[n] = tuple(r.reshape(w[n].shape) for r in (gsum, d, nm, nv))

    result = [loss, grad_x[None]]
    for k in range(4):
        result += [outs[n][k] for n in WEIGHT_ORDER]
    return tuple(result)
```

```python
import functools
import math

import numpy as np
import jax
import jax.numpy as jnp
from jax import lax
from jax.experimental import pallas as pl
from jax.experimental.pallas import tpu as pltpu

F32 = jnp.float32
BF16 = jnp.bfloat16
GRAD_WIRE_DTYPE = BF16
MESH = pl.DeviceIdType.MESH

D = 1024
EPS = 1e-6
RET_HEADS = 4
RET_DK = 128
RET_DV = 256
CHUNK = 128
ROPE_BASE = 10000.0
LRU_BLOCKS = 8
LRU_BLOCK = 128
CONV_TAPS = 4
LRU_C = 8.0
D_FF = 2816
X_HEADS = 4
X_HD = 256
N_CHIPS = 4
FF_BLK = D_FF // N_CHIPS
IN_BLK = 5120 // N_CHIPS
BG_BLK = 2048 // N_CHIPS
SQ_BLK = D // N_CHIPS

ADAM_LR = 0.001
ADAM_B1 = 0.9
ADAM_B2 = 0.999
ADAM_EPS = 1e-08
ADAM_WD = 0.01
ADAM_STEP = 10

VMEM_LIMIT_BYTES = 56 * 1024 * 1024
ROW_TILE = 512
WIDE_ROW_TILE = 1024
FFN_ROW_TILE = 256
DW_BLK = D_FF // 2
SCAN_TILE = 256

_DN = {
    "nn": (((1,), (0,)), ((), ())),
    "nt": (((1,), (1,)), ((), ())),
    "tn": (((0,), (0,)), ((), ())),
}


def _cparams(n_axes, collective_id=None):
    return pltpu.CompilerParams(dimension_semantics=("arbitrary",) * n_axes,
                                vmem_limit_bytes=VMEM_LIMIT_BYTES, collective_id=collective_id)


def _dot(a, b, kind):
    if b.ndim == 3:
        b = b.reshape(b.shape[0] * b.shape[1], b.shape[2])
    return lax.dot_general(a.astype(BF16), b.astype(BF16), _DN[kind], preferred_element_type=F32)


def _sigmoid(x):
    return 1.0 / (1.0 + jnp.exp(-x))


def _log1p_pos(e):
    u = 1.0 + e
    return jnp.where(u == 1.0, e, jnp.log(u) * (e / jnp.where(u == 1.0, 1.0, u - 1.0)))


def _expm1(x):
    u = jnp.exp(x)
    lu = jnp.log(u)
    safe = jnp.where(lu == 0.0, 1.0, lu)
    return jnp.where(u == 1.0, x, (u - 1.0) * (x / safe))


def _softplus(z):
    return jnp.maximum(z, 0.0) + _log1p_pos(jnp.exp(-jnp.abs(z)))


_GELU_C = math.sqrt(2.0 / math.pi)


def _gelu_and_grad(x):
    x2 = x * x
    t = jnp.tanh(_GELU_C * (x + 0.044715 * x * x2))
    g = 0.5 * x * (1.0 + t)
    dg = 0.5 * (1.0 + t) + 0.5 * x * (1.0 - t * t) * (_GELU_C * (1.0 + 3.0 * 0.044715 * x2))
    return g, dg


def _rms_fwd(x, g):
    r = lax.rsqrt(jnp.mean(x * x, axis=-1, keepdims=True) + EPS)
    return (x * r) * g


def _rms_bwd(x, g, dh):
    r = lax.rsqrt(jnp.mean(x * x, axis=-1, keepdims=True) + EPS)
    n = x * r
    dyg = dh * g
    dx = r * (dyg - n * jnp.mean(dyg * n, axis=-1, keepdims=True))
    return dx, jnp.sum(dh * n, axis=0, keepdims=True)


def _accumulate(ref, val, first):
    @pl.when(first)
    def _():
        ref[...] = val

    @pl.when(jnp.logical_not(first))
    def _():
        ref[...] += val


def _sds(shape, dtype):
    return jax.ShapeDtypeStruct(tuple(shape), dtype)


def _spec(shape, fn):
    return pl.BlockSpec(tuple(shape), fn)


class _Task:
    def __init__(self, peers, operands, out_shapes, aliases, nsem, make, finish, make_second=None):
        self.peers = peers
        self.operands, self.out_shapes, self.aliases = operands, out_shapes, aliases
        self.nsem, self.make, self.finish = nsem, make, finish
        self.make_second = make_second


class _Plan:
    def __init__(self):
        self.tasks, self.after = {}, {}


_plan = None


PEER_SET_COLLECTIVE_ID = {frozenset({"sibling"}): 1, frozenset({"chips"}): 2, frozenset({"sibling", "chips"}): 3,
                          frozenset({"neighbours"}): 4, frozenset({"sibling", "neighbours"}): 5}


def _peer_set(names):
    names = frozenset(names)
    return names - {"neighbours"} if "chips" in names else names


def _entry_handshake(peer_set):
    x, y, c, chips = _mesh_position()
    peers = [(x, y, 1 - c)] if "sibling" in peer_set else []
    if "chips" in peer_set:
        peers += [(*chip, c) for chip in chips]
    if "neighbours" in peer_set:
        peers += [(*chip, c) for chip in chips[:2]]
    barrier = pltpu.get_barrier_semaphore()
    for peer in peers:
        pl.semaphore_signal(barrier, inc=1, device_id=peer, device_id_type=MESH)
    pl.semaphore_wait(barrier, len(peers))


def _pcall(body, *, name, grid, in_specs, out_specs, out_shape, scratch_shapes=(), num_prefetch=0, own_peers=()):
    single = not isinstance(out_shape, (list, tuple))
    out_shape = [out_shape] if single else list(out_shape)
    out_specs = [out_specs] if single else list(out_specs)
    in_specs = list(in_specs)
    scratch_shapes = list(scratch_shapes)
    tasks = _plan.tasks.pop(name, []) if _plan is not None else []
    after = _plan.after.pop(name, []) if _plan is not None else []
    peer_set = _peer_set([t.peers for t in tasks] + list(own_peers))
    nax = len(grid)

    def run(*operands):
        n_in = len(operands) - num_prefetch
        n_out = len(out_shape)
        t_ops = [t.operands() for t in tasks]
        t_outs = [t.out_shapes() for t in tasks]
        c_ops = [a for ops in t_ops for a in ops]
        c_outs = [s for outs in t_outs for s in outs]
        aliases = {}
        i0, o0 = num_prefetch + n_in, n_out
        for t, ops, outs in zip(tasks, t_ops, t_outs):
            for i_loc, o_loc in t.aliases.items():
                aliases[i0 + i_loc] = o0 + o_loc
            i0 += len(ops)
            o0 += len(outs)
        nsem = sum(t.nsem for t in tasks)

        def wrapped(*refs):
            p = num_prefetch
            pre, ins = refs[:p], refs[p:p + n_in]
            cins = refs[p + n_in:p + n_in + len(c_ops)]
            q = p + n_in + len(c_ops)
            outs, couts = refs[q:q + n_out], refs[q + n_out:q + n_out + len(c_outs)]
            q += n_out + len(c_outs)
            scr = refs[q:q + len(scratch_shapes)]

            def rounds(second):
                send_sems, recv_sems = refs[q + len(scratch_shapes):]
                out = []
                ci = co = so = 0
                for t, ops, souts in zip(tasks, t_ops, t_outs):
                    make = t.make_second if second else t.make
                    out.append(([], []) if make is None else
                               make(cins[ci:ci + len(ops)], couts[co:co + len(souts)],
                                    functools.partial(lambda base, k: send_sems.at[base + k], so),
                                    functools.partial(lambda base, k: recv_sems.at[base + k], so)))
                    ci, co, so = ci + len(ops), co + len(souts), so + t.nsem
                return out

            two_rounds = [t.make_second is not None for t in tasks]
            if peer_set:
                ids = [pl.program_id(k) for k in range(nax)]
                first = functools.reduce(jnp.logical_and, [i == 0 for i in ids])
                last = functools.reduce(jnp.logical_and, [i == g - 1 for i, g in zip(ids, grid)])
                step = functools.reduce(lambda acc, ig: acc * ig[1] + ig[0], zip(ids, grid), 0)
                middle = step == math.prod(grid) // 3

                @pl.when(first)
                def _():
                    _entry_handshake(peer_set)
                    for starts, _ in rounds(False):
                        for copy in starts:
                            copy().start()

            body(*pre, *ins, *outs, *scr)

            if any(two_rounds):
                @pl.when(middle)
                def _():
                    for (_, arrivals), two in zip(rounds(False), two_rounds):
                        if two:
                            for arrival in arrivals:
                                arrival().wait_recv()
                    for starts, _ in rounds(True):
                        for copy in starts:
                            copy().start()

            if tasks:
                @pl.when(last)
                def _():
                    first_round, second_round = rounds(False), rounds(True)
                    for (_, arrivals1), (_, arrivals2), two in zip(first_round, second_round, two_rounds):
                        for arrival in (arrivals2 if two else arrivals1):
                            arrival().wait_recv()
                    for starts, _ in first_round + second_round:
                        for copy in starts:
                            copy().wait_send()

        sems = [pltpu.SemaphoreType.DMA((nsem,)), pltpu.SemaphoreType.DMA((nsem,))] if tasks else []
        res = pl.pallas_call(
            wrapped, name=name,
            grid_spec=pltpu.PrefetchScalarGridSpec(
                num_scalar_prefetch=num_prefetch, grid=tuple(grid),
                in_specs=in_specs + [ANY_SPEC] * len(c_ops),
                out_specs=out_specs + [ANY_SPEC] * len(c_outs),
                scratch_shapes=scratch_shapes + sems),
            out_shape=out_shape + c_outs,
            input_output_aliases=aliases,
            compiler_params=_cparams(nax, PEER_SET_COLLECTIVE_ID[peer_set] if peer_set else None),
        )(*operands, *c_ops)
        co = n_out
        for t, souts in zip(tasks, t_outs):
            t.finish(res[co:co + len(souts)])
            co += len(souts)
        for fn in after:
            fn()
        return res[0] if single else list(res[:n_out])

    return run


def _comm_call(name):
    def body(o_ref):
        o_ref[...] = jnp.zeros_like(o_ref)

    _pcall(body, name=name, grid=(1,), in_specs=[], out_specs=_spec((8, 128), lambda i: (0, 0)),
           out_shape=_sds((8, 128), F32))()


def _gemm(name, terms, grid, outs, acc_shape, extras=(), epilogue=None):
    kinds = [t[4] for t in terms]
    nt, ne, no = len(terms), len(extras), len(outs)
    nred = grid[-1]
    nax = len(grid)

    def body(*refs):
        trefs = refs[:2 * nt]
        erefs = refs[2 * nt:2 * nt + ne]
        orefs = refs[2 * nt + ne:2 * nt + ne + no]
        ids = [pl.program_id(k) for k in range(nax)]
        tot = None
        for t in range(nt):
            d = _dot(trefs[2 * t][...], trefs[2 * t + 1][...], kinds[t])
            tot = d if tot is None else tot + d

        def finish(acc):
            if epilogue is None:
                orefs[0][...] = acc.astype(orefs[0].dtype)
            else:
                epilogue(acc, erefs, orefs, ids)

        if nred == 1:
            finish(tot)
        else:
            acc_ref = refs[-1]
            r = ids[-1]

            @pl.when(r == 0)
            def _():
                acc_ref[...] = tot

            @pl.when(r > 0)
            def _():
                acc_ref[...] += tot

            @pl.when(r == nred - 1)
            def _():
                finish(acc_ref[...])

    operands, in_specs = [], []
    for a, a_spec, b, b_spec, _ in terms:
        operands += [a, b]
        in_specs += [a_spec, b_spec]
    for e, e_spec in extras:
        operands.append(e)
        in_specs.append(e_spec)
    scratch = [pltpu.VMEM(tuple(acc_shape), F32)] if nred > 1 else []
    return _pcall(body, name=name, grid=tuple(grid), in_specs=in_specs, out_specs=[o[1] for o in outs],
                  out_shape=[o[0] for o in outs], scratch_shapes=scratch)(*operands)


def _rowwise(name, fn, ins, outs, grid):
    ni = len(ins)
    nax = len(grid)

    def body(*refs):
        ids = [pl.program_id(k) for k in range(nax)]
        fn(refs[:ni], refs[ni:], ids)

    return _pcall(body, name=name, grid=tuple(grid), in_specs=[i[1] for i in ins],
                  out_specs=[o[1] for o in outs], out_shape=[o[0] for o in outs])(*[i[0] for i in ins])


def _ffn_up(name, h, w1buf, w1_idx, w3buf, w3_idx):
    T = h.shape[0]
    tm = min(FFN_ROW_TILE, T)

    def body(h_ref, w1_ref, w3_ref, a_ref, b_ref, s_ref):
        hv = h_ref[...]
        a = _dot(hv, w1_ref[...], "nt")
        b = _dot(hv, w3_ref[...], "nt")
        a_ref[...] = a.astype(BF16)
        b_ref[...] = b.astype(BF16)
        s_ref[...] = ((a * _sigmoid(a)) * b).astype(BF16)

    blk = _spec((tm, D_FF), lambda i: (i, 0))
    return _pcall(
        body, name=name, grid=(T // tm,),
        in_specs=[_spec((tm, D), lambda i: (i, 0)),
                  _spec((N_CHIPS, None, FF_BLK, D), lambda i: (0, w1_idx, 0, 0)),
                  _spec((N_CHIPS, None, FF_BLK, D), lambda i: (0, w3_idx, 0, 0))],
        out_specs=[blk, blk, blk],
        out_shape=[_sds((T, D_FF), BF16)] * 3,
    )(h, w1buf, w3buf)


def _ffn_down(name, s, wrow2, w2_idx, x_res, g_next=None):
    T = x_res.shape[0]
    tm = min(ROW_TILE, T)
    row = lambda i, j, r: (i, 0)

    def epilogue(acc, erefs, orefs, ids):
        xo = erefs[0][...] + 0.5 * acc
        orefs[0][...] = xo
        if g_next is not None:
            orefs[1][...] = _rms_fwd(xo, erefs[1][...]).astype(BF16)

    extras = [(x_res, _spec((tm, D), row))]
    outs = [(_sds((T, D), F32), _spec((tm, D), row))]
    if g_next is not None:
        extras.append((g_next, _spec((1, D), lambda i, j, r: (0, 0))))
        outs.append((_sds((T, D), BF16), _spec((tm, D), row)))
    return _gemm(
        name,
        [(s, _spec((tm, D_FF), row),
          wrow2, _spec((N_CHIPS, None, FF_BLK, D), lambda i, j, r: (0, w2_idx, 0, 0)), "nn")],
        (T // tm, 1, 1), outs, (tm, D), extras, epilogue)


def _ffn_bwd_mid(name, dx, wrow2, w2_idx, a, b):
    T = dx.shape[0]
    tm = min(FFN_ROW_TILE, T)

    def body(dx_ref, w2_ref, a_ref, b_ref, dab_ref):
        ds = _dot(0.5 * dx_ref[...], w2_ref[...], "nt")
        av = a_ref[...].astype(F32)
        sg = _sigmoid(av)
        dab_ref[0] = (ds * b_ref[...].astype(F32) * (sg * (1.0 + av * (1.0 - sg)))).astype(BF16)
        dab_ref[1] = (ds * (av * sg)).astype(BF16)

    blk = _spec((tm, D_FF), lambda i: (i, 0))
    return _pcall(
        body, name=name, grid=(T // tm,),
        in_specs=[_spec((tm, D), lambda i: (i, 0)),
                  _spec((N_CHIPS, None, FF_BLK, D), lambda i: (0, w2_idx, 0, 0)),
                  blk, blk],
        out_specs=_spec((2, tm, D_FF), lambda i: (0, i, 0)),
        out_shape=_sds((2, T, D_FF), BF16),
    )(dx, wrow2, a, b)


def _rms_bwd_epilogue(acc, erefs, orefs, ids):
    dx, dgp = _rms_bwd(erefs[0][...], erefs[1][...], acc)
    orefs[0][...] = dx + erefs[2][...]
    _accumulate(orefs[1], dgp, ids[0] == 0)


def _rms_bwd_io(x, g, dres, T, tm):
    row = lambda i, j, r: (i, 0)
    vec = lambda i, j, r: (0, 0)
    extras = [(x, _spec((tm, D), row)), (g, _spec((1, D), vec)), (dres, _spec((tm, D), row))]
    outs = [(_sds((T, D), F32), _spec((tm, D), row)), (_sds((1, D), F32), _spec((1, D), vec))]
    return extras, outs


def _ffn_bwd(tag, dx_out, h, a, b, s, w1buf, w1_idx, w3buf, w3_idx, wrow2, w2_idx, x_in, g, big):
    T = dx_out.shape[0]
    dab = _ffn_bwd_mid(tag + "_bwd_mid", dx_out, wrow2, w2_idx, a, b)

    def half_scale(acc, erefs, orefs, ids):
        orefs[0][...] = (0.5 * acc).astype(orefs[0].dtype)

    dw_grid = (D_FF // DW_BLK, 1, 1)
    dw_out = [(_sds((D_FF, D), GRAD_WIRE_DTYPE), _spec((DW_BLK, D), lambda j, n, r: (j, 0)))]
    tokens = _spec((T, D), lambda j, n, r: (0, 0))
    big[tag + "_w2"] = _gemm(
        tag + "_dw2", [(s, _spec((T, DW_BLK), lambda j, n, r: (0, j)), dx_out, tokens, "tn")],
        dw_grid, dw_out, (DW_BLK, D), (), half_scale)[0].reshape(1, N_CHIPS, FF_BLK, D)
    for widx, wname in ((0, "_w1"), (1, "_w3")):
        big[tag + wname] = _gemm(
            tag + "_d" + wname[1:],
            [(dab, _spec((None, T, DW_BLK), functools.partial(lambda w, j, n, r: (w, 0, j), widx)), h, tokens, "tn")],
            dw_grid, dw_out, (DW_BLK, D))[0].reshape(1, N_CHIPS, FF_BLK, D)
    tm = min(FFN_ROW_TILE, T)
    extras, outs = _rms_bwd_io(x_in, g, dx_out, T, tm)
    whole = lambda idx: _spec((N_CHIPS, None, FF_BLK, D), lambda i, j, r: (0, idx, 0, 0))
    dx_in, dg = _gemm(
        tag + "_dh",
        [(dab, _spec((None, tm, D_FF), lambda i, j, r: (0, i, 0)), w1buf, whole(w1_idx), "nn"),
         (dab, _spec((None, tm, D_FF), lambda i, j, r: (1, i, 0)), w3buf, whole(w3_idx), "nn")],
        (T // tm, 1, 1), outs, (tm, D), extras, _rms_bwd_epilogue)
    return dx_in, dg


def _proj_sq(name, a, wsq, idx, kind, out_dtype=F32, extras=(), epilogue=None, outs=None):
    M = a.shape[0]
    tm = min(ROW_TILE, M)
    if outs is None:
        outs = [(_sds((M, D), out_dtype), _spec((tm, D), lambda i, j, r: (i, 0)))]
    return _gemm(
        name,
        [(a, _spec((tm, D), lambda i, j, r: (i, 0)),
          wsq, _spec((N_CHIPS, None, SQ_BLK, D), lambda i, j, r: (0, idx, 0, 0)), kind)],
        (M // tm, 1, 1), outs, (tm, D), extras, epilogue)


def _dw_sq(name, a, b):
    M = a.shape[0]
    tk = M
    whole = _gemm(
        name,
        [(a, _spec((tk, D), lambda i, j, r: (r, 0)), b, _spec((tk, D), lambda i, j, r: (r, 0)), "tn")],
        (1, 1, M // tk),
        [(_sds((D, D), GRAD_WIRE_DTYPE), _spec((D, D), lambda i, j, r: (0, 0)))],
        (D, D))[0]
    return whole.reshape(N_CHIPS, SQ_BLK, D)


def _retention_constants(T):
    pos = jnp.arange(T, dtype=F32)
    inv_freq = ROPE_BASE ** (-jnp.arange(0, RET_DK, 2, dtype=F32) / RET_DK)
    ang = pos[:, None] * inv_freq[None, :]
    cosf = jnp.concatenate([jnp.cos(ang), jnp.cos(ang)], axis=1)
    sins = jnp.concatenate([-jnp.sin(ang), jnp.sin(ang)], axis=1)
    lg = jnp.log(1.0 - 2.0 ** (-5.0 - jnp.arange(RET_HEADS, dtype=F32)))
    p = jnp.arange(CHUNK, dtype=F32)
    rel = p[:, None] - p[None, :]
    dmat = jnp.where(rel[None] >= 0, jnp.exp(rel[None] * lg[:, None, None]), 0.0)
    kd = jnp.exp((CHUNK - 1.0 - p)[None, :] * lg[:, None])[:, :, None]
    qd = jnp.exp((p + 1.0)[None, :] * lg[:, None])[:, :, None]
    cd = jnp.exp(CHUNK * lg)[:, None, None]
    return cosf, sins, dmat, kd, qd, cd


def _rot(t, cosv, sinv):
    return t * cosv + pltpu.roll(t, RET_DK // 2, 1) * sinv


def _unrot(t, cosv, sinv):
    return t * cosv - pltpu.roll(t, RET_DK // 2, 1) * sinv


def _ret_const_specs(cm):
    whole = lambda shape: _spec(shape, lambda c: (0,) * len(shape))
    return [
        _spec((CHUNK, RET_DK), lambda c: (cm(c), 0)),
        _spec((CHUNK, RET_DK), lambda c: (cm(c), 0)),
        whole((RET_HEADS, CHUNK, CHUNK)), whole((RET_HEADS, CHUNK, 1)), whole((RET_HEADS, CHUNK, 1)),
        whole((RET_HEADS, 1, 1)),
    ]


def _head(h, width):
    return slice(h * width, (h + 1) * width)


def _ret_fwd(u, consts, ret_gn):
    T = u.shape[0]
    nC = T // CHUNK
    kscale = RET_DK ** -0.5

    def body(q_ref, k_ref, v_ref, g_ref, cos_ref, sin_ref, dm_ref, kd_ref, qd_ref, cd_ref, gn_ref,
             qr_ref, kr_ref, ret_ref, yr_ref, st_ref, state):
        @pl.when(pl.program_id(0) == 0)
        def _():
            state[...] = jnp.zeros_like(state)

        cosv, sinv = cos_ref[...], sin_ref[...]
        for h in range(RET_HEADS):
            hk, hv = _head(h, RET_DK), _head(h, RET_DV)
            q = _rot(q_ref[:, hk], cosv, sinv)
            k = _rot(k_ref[:, hk], cosv, sinv) * kscale
            v = v_ref[:, hv]
            qr_ref[:, hk] = q
            kr_ref[:, hk] = k
            prev = state[h]
            st_ref[h] = prev
            s = _dot(q, k, "nt") * dm_ref[h]
            ret = _dot(s, v, "nn") + _dot(q, prev, "nn") * qd_ref[h]
            state[h] = cd_ref[h] * prev + _dot(k * kd_ref[h], v, "tn")
            ret_ref[:, hv] = ret
            mu = jnp.mean(ret, axis=-1, keepdims=True)
            xc = ret - mu
            yn = xc * lax.rsqrt(jnp.mean(xc * xc, axis=-1, keepdims=True) + EPS)
            g = g_ref[:, hv]
            yr_ref[:, hv] = ((g * _sigmoid(g)) * (yn * gn_ref[:, hv])).astype(BF16)

    cm = lambda c: c
    qk_w, v_w = RET_HEADS * RET_DK, RET_HEADS * RET_DV
    in_specs = [
        _spec((CHUNK, qk_w), lambda c: (c, 0)), _spec((CHUNK, qk_w), lambda c: (c, 1)),
        _spec((CHUNK, v_w), lambda c: (c, 1)), _spec((CHUNK, v_w), lambda c: (c, 2)),
    ] + _ret_const_specs(cm) + [_spec((1, v_w), lambda c: (0, 0))]
    qk_out = _spec((CHUNK, qk_w), lambda c: (c, 0))
    v_out = _spec((CHUNK, v_w), lambda c: (c, 0))
    return _pcall(
        body, name="ret_fwd", grid=(nC,),
        in_specs=in_specs,
        out_specs=[qk_out, qk_out, v_out, v_out,
                   _spec((RET_HEADS, None, RET_DK, RET_DV), lambda c: (0, c, 0, 0))],
        out_shape=[_sds((T, qk_w), F32), _sds((T, qk_w), F32), _sds((T, v_w), F32), _sds((T, v_w), BF16),
                   _sds((RET_HEADS, nC, RET_DK, RET_DV), F32)],
        scratch_shapes=[pltpu.VMEM((RET_HEADS, RET_DK, RET_DV), F32)],
    )(u, u, u, u, *consts, ret_gn)


def _ret_bwd(dyr, ret, u, qr, kr, states, consts, ret_gn):
    T = u.shape[0]
    nC = T // CHUNK
    kscale = RET_DK ** -0.5

    def body(dyr_ref, ret_ref, g_ref, q_ref, k_ref, v_ref, st_ref,
             cos_ref, sin_ref, dm_ref, kd_ref, qd_ref, cd_ref, gn_ref,
             dq_ref, dk_ref, dv_ref, dg_ref, dgn_ref, gstate):
        first = pl.program_id(0) == 0

        @pl.when(first)
        def _():
            gstate[...] = jnp.zeros_like(gstate)

        cosv, sinv = cos_ref[...], sin_ref[...]
        dgn_parts = []
        for h in range(RET_HEADS):
            hk, hv = _head(h, RET_DK), _head(h, RET_DV)
            ret = ret_ref[:, hv]
            mu = jnp.mean(ret, axis=-1, keepdims=True)
            xc = ret - mu
            rs = lax.rsqrt(jnp.mean(xc * xc, axis=-1, keepdims=True) + EPS)
            yn = xc * rs
            gn = gn_ref[:, hv]
            g = g_ref[:, hv]
            sg = _sigmoid(g)
            dyr_v = dyr_ref[:, hv]
            dretn = dyr_v * (g * sg)
            dg_ref[:, hv] = (dyr_v * (yn * gn) * (sg * (1.0 + g * (1.0 - sg)))).astype(BF16)
            dgn_parts.append(jnp.sum(dretn * yn, axis=0, keepdims=True))
            dyn = dretn * gn
            d_o = rs * (dyn - jnp.mean(dyn, axis=-1, keepdims=True)
                        - yn * jnp.mean(dyn * yn, axis=-1, keepdims=True))

            q, k, v = q_ref[:, hk], k_ref[:, hk], v_ref[:, hv]
            dmat, kd, qd = dm_ref[h], kd_ref[h], qd_ref[h]
            prev = st_ref[h]
            gnext = gstate[h]
            s = _dot(q, k, "nt") * dmat
            ds = _dot(d_o, v, "nt") * dmat
            doq = d_o * qd
            dq = _dot(ds, k, "nn") + _dot(doq, prev, "nt")
            dk = _dot(ds, q, "tn") + _dot(v, gnext, "nt") * kd
            dv = _dot(s, d_o, "tn") + _dot(k * kd, gnext, "nn")
            gstate[h] = cd_ref[h] * gnext + _dot(q, doq, "tn")
            dq_ref[:, hk] = _unrot(dq, cosv, sinv).astype(BF16)
            dk_ref[:, hk] = _unrot(dk * kscale, cosv, sinv).astype(BF16)
            dv_ref[:, hv] = dv.astype(BF16)
        _accumulate(dgn_ref, jnp.concatenate(dgn_parts, axis=1), first)

    cm = lambda c: nC - 1 - c
    qk_w, v_w = RET_HEADS * RET_DK, RET_HEADS * RET_DV
    vspec = lambda blk: _spec((CHUNK, v_w), lambda c: (cm(c), blk))
    qspec = _spec((CHUNK, qk_w), lambda c: (cm(c), 0))
    in_specs = [vspec(0), vspec(0), vspec(2), qspec, qspec, vspec(1),
                _spec((RET_HEADS, None, RET_DK, RET_DV), lambda c: (0, cm(c), 0, 0)),
                ] + _ret_const_specs(cm) + [_spec((1, v_w), lambda c: (0, 0))]
    return _pcall(
        body, name="ret_bwd", grid=(nC,),
        in_specs=in_specs,
        out_specs=[qspec, qspec, vspec(0), vspec(0), _spec((1, v_w), lambda c: (0, 0))],
        out_shape=[_sds((T, qk_w), BF16), _sds((T, qk_w), BF16), _sds((T, v_w), BF16), _sds((T, v_w), BF16),
                   _sds((1, v_w), F32)],
        scratch_shapes=[pltpu.VMEM((RET_HEADS, RET_DK, RET_DV), F32)],
    )(dyr, ret, u, qr, kr, u, states, *consts, ret_gn)


def _shift_down(x, s):
    rows = lax.broadcasted_iota(jnp.int32, x.shape, 0)
    return jnp.where(rows >= s, pltpu.roll(x, s, 0), 0.0)


def _shift_up(x, s):
    n = x.shape[0]
    rows = lax.broadcasted_iota(jnp.int32, x.shape, 0)
    return jnp.where(rows < n - s, pltpu.roll(x, n - s, 0), 0.0)


def _lru_specs(T):
    col = lambda off: _spec((T, LRU_BLOCK), lambda g: (0, off + g))
    vec = _spec((1, LRU_BLOCK), lambda g: (0, g))
    wblk = _spec((None, LRU_BLOCK, LRU_BLOCK), lambda g: (g, 0, 0))
    cw = _spec((CONV_TAPS, LRU_BLOCK), lambda g: (0, g))
    return col, vec, wblk, cw


def _lru_gates_fwd(u, conv_w, conv_b, w_r, b_r, w_i, b_i, lam):
    T = u.shape[0]
    col, vec, wblk, cw = _lru_specs(T)

    def body(x_ref, cw_ref, cb_ref, wr_ref, br_ref, wi_ref, bi_ref, lam_ref,
             xc_ref, r_ref, i_ref, a_ref, bx_ref):
        x = x_ref[...]
        w = cw_ref[...]
        xc = (_shift_down(x, 3) * w[0:1] + _shift_down(x, 2) * w[1:2] + _shift_down(x, 1) * w[2:3]
              + x * w[3:4] + cb_ref[...])
        r = _sigmoid(_dot(xc, wr_ref[...], "nn") + br_ref[...])
        i = _sigmoid(_dot(xc, wi_ref[...], "nn") + bi_ref[...])
        la = (-LRU_C) * r * _softplus(-lam_ref[...])
        xc_ref[...] = xc
        r_ref[...] = r
        i_ref[...] = i
        a_ref[...] = jnp.exp(la)
        bx_ref[...] = jnp.sqrt(-_expm1(2.0 * la)) * (i * xc)

    out = col(0)
    return _pcall(
        body, name="lru_gates_fwd", grid=(LRU_BLOCKS,),
        in_specs=[col(24), cw, vec, wblk, vec, wblk, vec, vec],
        out_specs=[out] * 5,
        out_shape=[_sds((T, D), F32)] * 5,
    )(u, conv_w, conv_b, w_r, b_r, w_i, b_i, lam)


def _lru_scan(name, a3, b3, reverse):
    T = a3.shape[0]
    nt = T // SCAN_TILE
    unroll = 8

    def body(a_ref, b_ref, o_ref, carry):
        @pl.when(pl.program_id(0) == 0)
        def _():
            carry[...] = jnp.zeros_like(carry)

        if not reverse:
            def step(t, h):
                h = a_ref[t] * h + b_ref[t]
                o_ref[t] = h
                return h
        else:
            def step(k, c):
                t = SCAN_TILE - 1 - k
                l = b_ref[t] + c
                o_ref[t] = l
                return a_ref[t] * l
        carry[...] = lax.fori_loop(0, SCAN_TILE, step, carry[...], unroll=unroll)

    idx = (lambda i: (nt - 1 - i, 0, 0)) if reverse else (lambda i: (i, 0, 0))
    blk = _spec((SCAN_TILE, LRU_BLOCKS, LRU_BLOCK), idx)
    return _pcall(
        body, name=name, grid=(nt,),
        in_specs=[blk, blk], out_specs=blk,
        out_shape=_sds((T, LRU_BLOCKS, LRU_BLOCK), F32),
        scratch_shapes=[pltpu.VMEM((LRU_BLOCKS, LRU_BLOCK), F32)],
    )(a3, b3)


def _lru_gates_bwd(lmb, hl, a, r, i, xc, u, conv_w, w_r, w_i, lam):
    T = u.shape[0]
    col, vec, wblk, cw = _lru_specs(T)

    def body(l_ref, h_ref, a_ref, r_ref, i_ref, xc_ref, x_ref, cw_ref, wr_ref, wi_ref, lam_ref,
             dx_ref, dwr_ref, dwi_ref, dvec_ref, dcw_ref):
        l = l_ref[...]
        av, rv, iv, xc = a_ref[...], r_ref[...], i_ref[...], xc_ref[...]
        lam_v = lam_ref[...]
        sp = _softplus(-lam_v)
        la = (-LRU_C) * rv * sp
        mult = jnp.sqrt(-_expm1(2.0 * la))
        da = l * _shift_down(h_ref[...], 1)
        dmult = l * (iv * xc)
        di = l * mult * xc
        dxc = l * mult * iv
        dla = da * av - dmult * (av * av) / mult
        dzr = (dla * ((-LRU_C) * sp)) * rv * (1.0 - rv)
        dzi = di * iv * (1.0 - iv)
        dsp = jnp.sum(dla * ((-LRU_C) * rv), axis=0, keepdims=True)
        dlam = dsp * (-_sigmoid(-lam_v))
        dwr_ref[...] = _dot(xc, dzr, "tn")
        dwi_ref[...] = _dot(xc, dzi, "tn")
        dxc = dxc + _dot(dzr, wr_ref[...], "nt") + _dot(dzi, wi_ref[...], "nt")
        x = x_ref[...]
        w = cw_ref[...]
        dx = (dxc * w[3:4] + _shift_up(dxc, 1) * w[2:3] + _shift_up(dxc, 2) * w[1:2]
              + _shift_up(dxc, 3) * w[0:1])
        dx_ref[...] = dx.astype(BF16)
        dvec_ref[...] = jnp.concatenate(
            [jnp.sum(dzr, axis=0, keepdims=True), jnp.sum(dzi, axis=0, keepdims=True), dlam,
             jnp.sum(dxc, axis=0, keepdims=True)], axis=0)
        dcw_ref[...] = jnp.concatenate(
            [jnp.sum(dxc * _shift_down(x, 3 - tap), axis=0, keepdims=True) if tap < 3
             else jnp.sum(dxc * x, axis=0, keepdims=True) for tap in range(CONV_TAPS)], axis=0)

    c0 = col(0)
    return _pcall(
        body, name="lru_gates_bwd", grid=(LRU_BLOCKS,),
        in_specs=[c0, c0, c0, c0, c0, c0, col(24), cw, wblk, wblk, vec],
        out_specs=[c0, wblk, wblk, cw, cw],
        out_shape=[_sds((T, D), BF16), _sds((LRU_BLOCKS, LRU_BLOCK, LRU_BLOCK), F32),
                   _sds((LRU_BLOCKS, LRU_BLOCK, LRU_BLOCK), F32), _sds((4, D), F32), _sds((CONV_TAPS, D), F32)],
    )(lmb, hl, a, r, i, xc, u, conv_w, w_r, w_i, lam)


def _xattn_probs(q, k):
    sc = _dot(q, k, "nt") * (X_HD ** -0.5)
    e = jnp.exp(sc - jnp.max(sc, axis=-1, keepdims=True))
    return e / jnp.sum(e, axis=-1, keepdims=True)


def _xattn_fwd(xq, xk, xv):
    T = xq.shape[0]
    tq = ROW_TILE
    M = xk.shape[0]

    def body(q_ref, k_ref, v_ref, o_ref):
        p = _xattn_probs(q_ref[...], k_ref[...])
        o_ref[...] = _dot(p, v_ref[...], "nn").astype(BF16)

    qs = _spec((tq, X_HD), lambda h, i: (i, h))
    kv = _spec((M, X_HD), lambda h, i: (0, h))
    return _pcall(
        body, name="xattn_fwd", grid=(X_HEADS, T // tq),
        in_specs=[qs, kv, kv], out_specs=qs, out_shape=_sds((T, D), BF16),
    )(xq, xk, xv)


def _xattn_bwd(xq, xk, xv, dxo):
    T = xq.shape[0]
    tq = ROW_TILE
    M = xk.shape[0]

    def body(q_ref, k_ref, v_ref, do_ref, dq_ref, dk_ref, dv_ref):
        first = pl.program_id(1) == 0
        q, k, v, do = q_ref[...], k_ref[...], v_ref[...], do_ref[...]
        p = _xattn_probs(q, k)
        dp = _dot(do, v, "nt")
        ds = p * (dp - jnp.sum(dp * p, axis=-1, keepdims=True)) * (X_HD ** -0.5)
        dq_ref[...] = _dot(ds, k, "nn").astype(BF16)
        _accumulate(dk_ref, _dot(ds, q, "tn"), first)
        _accumulate(dv_ref, _dot(p, do, "tn"), first)

    qs = _spec((tq, X_HD), lambda h, i: (i, h))
    kv = _spec((M, X_HD), lambda h, i: (0, h))
    return _pcall(
        body, name="xattn_bwd", grid=(X_HEADS, T // tq),
        in_specs=[qs, kv, kv, qs], out_specs=[qs, kv, kv],
        out_shape=[_sds((T, D), BF16), _sds((M, D), F32), _sds((M, D), F32)],
    )(xq, xk, xv, dxo)


def _final_loss(x, g, tgt):
    T = x.shape[0]
    tm = ROW_TILE

    def fn(irefs, orefs, ids):
        xv, gv = irefs[0][...], irefs[1][...]
        err = _rms_fwd(xv, gv) - irefs[2][...]
        lp = 0.5 * jnp.sum(jnp.mean(err * err, axis=-1, keepdims=True), axis=0, keepdims=True)
        first = ids[0] == 0
        _accumulate(orefs[0], jnp.broadcast_to(lp, (1, 128)), first)
        dx, dgp = _rms_bwd(xv, gv, err * (1.0 / D))
        orefs[1][...] = dx
        _accumulate(orefs[2], dgp, first)

    row = _spec((tm, D), lambda i: (i, 0))
    vec = _spec((1, D), lambda i: (0, 0))
    return _rowwise(
        "final_loss", fn, [(x, row), (g, vec), (tgt, row)],
        [(_sds((1, 128), F32), _spec((1, 128), lambda i: (0, 0))), (_sds((T, D), F32), row),
         (_sds((1, D), F32), vec)],
        (T // tm,))


def _adamw(name, w, g, m, v):
    R, C = w.shape
    tr = R
    for cand in (512, 352, 256):
        if R % cand == 0:
            tr = cand
            break

    def fn(irefs, orefs, ids):
        delta, mn, vn = _adamw_update(*(r[...] for r in irefs))
        orefs[0][...] = delta
        orefs[1][...] = mn
        orefs[2][...] = vn

    blk = _spec((tr, C), lambda i: (i, 0))
    return _rowwise(name, fn, [(w, blk), (g, blk), (m, blk), (v, blk)],
                    [(_sds((R, C), F32), blk)] * 3, (R // tr,))


def _adamw_update(wv, gv, mv, vv):
    c1 = 1.0 - ADAM_B1 ** ADAM_STEP
    c2 = 1.0 - ADAM_B2 ** ADAM_STEP
    mn = ADAM_B1 * mv + (1.0 - ADAM_B1) * gv
    vn = ADAM_B2 * vv + (1.0 - ADAM_B2) * (gv * gv)
    delta = -ADAM_LR * ((mn / c1) / (jnp.sqrt(vn / c2) + ADAM_EPS) + ADAM_WD * wv)
    return delta, mn, vn


def _adamw_halves(name, w, mine, theirs, widx, m, v, core):
    R, C = w.shape
    H = R // 2
    tr = H
    while tr * C * 4 > (1 << 20) and tr % 16 == 0:
        tr //= 2
    nb = H // tr

    def body(core_ref, w_ref, mine_ref, theirs_ref, m_ref, v_ref, g_out, d_out, m_out, v_out):
        gv = jnp.where(pl.program_id(0) == core_ref[0], mine_ref[...], theirs_ref[...])
        delta, mn, vn = _adamw_update(w_ref[...], gv, m_ref[...], v_ref[...])
        g_out[...] = gv
        d_out[...] = delta
        m_out[...] = mn
        v_out[...] = vn

    full = pl.BlockSpec((tr, C), lambda h, i, core_ref: (h * nb + i, 0))
    mine_spec = pl.BlockSpec((None, tr, C), lambda h, i, core_ref: (widx, jnp.where(h == core_ref[0], i, 0), 0))
    theirs_spec = pl.BlockSpec((None, tr, C), lambda h, i, core_ref: (widx, jnp.where(h == core_ref[0], 0, i), 0))
    return _pcall(
        body, name=name, grid=(2, nb), num_prefetch=1,
        in_specs=[full, mine_spec, theirs_spec, full, full], out_specs=[full] * 4,
        out_shape=[_sds((R, C), F32)] * 4,
    )(core, w, mine, theirs, m, v)


def _rmsnorm(name, x, g):
    M = x.shape[0]
    tm = min(ROW_TILE, M)

    def fn(irefs, orefs, ids):
        orefs[0][...] = _rms_fwd(irefs[0][...], irefs[1][...]).astype(BF16)

    row = _spec((tm, D), lambda i: (i, 0))
    return _rowwise(name, fn, [(x, row), (g, _spec((1, D), lambda i: (0, 0)))],
                    [(_sds((M, D), BF16), row)], (M // tm,))[0]


WEIGHT_AT = {
    "ffn1_w1": ("col1", 0), "ffn1_w3": ("col1", 1), "ffn1_w2": ("row2a", 0),
    "w_ret_o": ("sqA", 0), "w_lru_o": ("sqA", 1), "w_out": ("sqA", 2),
    "w_xq": ("sqB", 0), "w_xk": ("sqB", 1), "w_xv": ("sqC", 0), "w_xo": ("sqC", 1),
    "ffn2_w1": ("col2a", 0), "ffn2_w3": ("col2b", 0), "ffn2_w2": ("row2b", 0),
}


def _local_step(x, mem, tgt, gw, sm, big):
    T = x.shape[0]
    tm = ROW_TILE

    def wt(name):
        key, idx = WEIGHT_AT[name]
        return gw[key], idx

    row3 = lambda i, j, r: (i, 0)
    vec3 = lambda i, j, r: (0, 0)
    rowD = _spec((tm, D), row3)
    vecD = _spec((1, D), vec3)

    def residual_norm(acc, erefs, orefs, ids):
        xo = erefs[0][...] + acc
        orefs[0][...] = xo
        orefs[1][...] = _rms_fwd(xo, erefs[1][...]).astype(BF16)

    def res_norm_io(x_res, g):
        return ([(x_res, rowD), (g, vecD)],
                [(_sds((T, D), F32), rowD), (_sds((T, D), BF16), rowD)])

    h1 = _rmsnorm("ffn1_norm", x, sm["ffn1_norm"])
    a1, b1, s1 = _ffn_up("ffn1_up", h1, *wt("ffn1_w1"), *wt("ffn1_w3"))
    x1, h2 = _ffn_down("ffn1_down", s1, *wt("ffn1_w2"), x, sm["mix_norm"])

    tw = min(WIDE_ROW_TILE, T)
    wideD = _spec((tw, D), row3)
    u = _gemm(
        "mix_in",
        [(h2, wideD, gw["win"], _spec((None, None, IN_BLK, D), lambda i, j, r: (j, 0, 0, 0)), "nt")],
        (T // tw, N_CHIPS, 1),
        [(_sds((T, 5120), F32), _spec((tw, IN_BLK), lambda i, j, r: (i, j)))], (tw, IN_BLK))[0]

    consts = _retention_constants(T)
    qr, kr, ret, yr, states = _ret_fwd(u, consts, sm["ret_gn"])

    conv_w = gw["conv"][:, 0].transpose(1, 0, 2).reshape(CONV_TAPS, D)
    xc, rg, ig, av, bx = _lru_gates_fwd(u, conv_w, sm["conv_b"], sm["w_rgate"], sm["b_rgate"],
                                        sm["w_igate"], sm["b_igate"], sm["lru_lambda"])
    a3 = av.reshape(T, LRU_BLOCKS, LRU_BLOCK)
    hl = _lru_scan("lru_scan_fwd", a3, bx.reshape(T, LRU_BLOCKS, LRU_BLOCK), False).reshape(T, D)

    row1 = _spec((tm, D), lambda i: (i, 0))
    glru1 = _spec((tm, D), lambda i: (i, 4))

    def lru_out(irefs, orefs, ids):
        gl, _ = _gelu_and_grad(irefs[1][...])
        orefs[0][...] = (irefs[0][...] * gl).astype(BF16)

    yl = _rowwise("lru_out", lru_out, [(hl, row1), (u, glru1)], [(_sds((T, D), BF16), row1)], (T // tm,))[0]

    def gate_epilogue(acc, erefs, orefs, ids):
        orefs[0][...] = _sigmoid(acc + erefs[0][...])

    gates = _gemm(
        "mix_gates",
        [(h2, wideD, gw["wbg"], _spec((None, None, BG_BLK, D), lambda i, j, r: (j, 0, 0, 0)), "nt")],
        (T // tw, N_CHIPS, 1),
        [(_sds((T, 2 * D), F32), _spec((tw, BG_BLK), lambda i, j, r: (i, j)))], (tw, BG_BLK),
        [(sm["b_branch_gate"], _spec((1, BG_BLK), lambda i, j, r: (0, j)))], gate_epilogue)[0]

    y_ret = _proj_sq("y_ret", yr, *wt("w_ret_o"), "nn")[0]

    def merge_epilogue(acc, erefs, orefs, ids):
        orefs[0][...] = acc
        orefs[1][...] = (erefs[0][...] * erefs[2][...] + erefs[1][...] * acc).astype(BF16)

    y_lru, merged = _proj_sq(
        "y_lru", yl, *wt("w_lru_o"), "nn",
        extras=[(gates, _spec((tm, D), lambda i, j, r: (i, 0))), (gates, _spec((tm, D), lambda i, j, r: (i, 1))),
                (y_ret, rowD)],
        epilogue=merge_epilogue,
        outs=[(_sds((T, D), F32), rowD), (_sds((T, D), BF16), rowD)])

    ex, ou = res_norm_io(x1, sm["xattn_norm"])
    x2, hq = _proj_sq("mix_out", merged, *wt("w_out"), "nn", extras=ex, epilogue=residual_norm, outs=ou)

    m = _rmsnorm("mem_norm", mem, sm["mem_norm"])
    xq = _proj_sq("xq", hq, *wt("w_xq"), "nn", BF16)[0]
    xk = _proj_sq("xk", m, *wt("w_xk"), "nn", BF16)[0]
    xv = _proj_sq("xv", m, *wt("w_xv"), "nn", BF16)[0]
    xo = _xattn_fwd(xq, xk, xv)
    ex, ou = res_norm_io(x2, sm["ffn2_norm"])
    x3, h3 = _proj_sq("xattn_out", xo, *wt("w_xo"), "nn", extras=ex, epilogue=residual_norm, outs=ou)

    a2, b2, s2 = _ffn_up("ffn2_up", h3, *wt("ffn2_w1"), *wt("ffn2_w3"))
    x4 = _ffn_down("ffn2_down", s2, *wt("ffn2_w2"), x3)[0]
    loss, dx4, dg_final = _final_loss(x4, sm["final_norm"], tgt)

    dx3, dg_ffn2 = _ffn_bwd("ffn2", dx4, h3, a2, b2, s2, *wt("ffn2_w1"), *wt("ffn2_w3"),
                            *wt("ffn2_w2"), x3, sm["ffn2_norm"], big)

    dxo = _proj_sq("d_xo", dx3, *wt("w_xo"), "nt", BF16)[0]
    big["w_xo"] = _dw_sq("dw_xo", xo, dx3)[None]
    dxq, dxk, dxv = _xattn_bwd(xq, xk, xv, dxo)
    big["w_xq"] = _dw_sq("dw_xq", hq, dxq)[None]
    ex, ou = _rms_bwd_io(x2, sm["xattn_norm"], dx3, T, tm)
    dx2, dg_xattn = _proj_sq("d_hq", dxq, *wt("w_xq"), "nt", extras=ex, epilogue=_rms_bwd_epilogue, outs=ou)
    big["w_xk"] = _dw_sq("dw_xk", m, dxk)[None]
    big["w_xv"] = _dw_sq("dw_xv", m, dxv)[None]

    M = mem.shape[0]

    def mem_norm_epilogue(acc, erefs, orefs, ids):
        _, dgp = _rms_bwd(erefs[0][...], erefs[1][...], acc)
        orefs[0][...] = dgp

    wsq_spec = lambda idx: _spec((N_CHIPS, None, SQ_BLK, D), lambda i, j, r: (0, idx, 0, 0))
    memD = _spec((M, D), row3)
    dg_mem = _gemm(
        "d_mem_norm",
        [(dxk, memD, wt("w_xk")[0], wsq_spec(wt("w_xk")[1]), "nt"),
         (dxv, memD, wt("w_xv")[0], wsq_spec(wt("w_xv")[1]), "nt")],
        (1, 1, 1), [(_sds((1, D), F32), vecD)], (M, D),
        [(mem, memD), (sm["mem_norm"], vecD)], mem_norm_epilogue)[0]

    def merged_bwd_epilogue(acc, erefs, orefs, ids):
        gr, gl, yrv, ylv = (e[...] for e in erefs)
        orefs[0][...] = (acc * gr).astype(BF16)
        orefs[1][...] = (acc * gl).astype(BF16)
        dgr = acc * yrv * gr * (1.0 - gr)
        dgl = acc * ylv * gl * (1.0 - gl)
        orefs[2][:, :D] = dgr.astype(BF16)
        orefs[2][:, D:] = dgl.astype(BF16)
        dbb = jnp.concatenate([jnp.sum(dgr, axis=0, keepdims=True), jnp.sum(dgl, axis=0, keepdims=True)], axis=1)
        _accumulate(orefs[3], dbb, ids[0] == 0)

    dy_ret, dy_lru, dgpre, db_bg = _proj_sq(
        "d_merged", dx2, *wt("w_out"), "nt",
        extras=[(gates, _spec((tm, D), lambda i, j, r: (i, 0))), (gates, _spec((tm, D), lambda i, j, r: (i, 1))),
                (y_ret, rowD), (y_lru, rowD)],
        epilogue=merged_bwd_epilogue,
        outs=[(_sds((T, D), BF16), rowD), (_sds((T, D), BF16), rowD),
              (_sds((T, 2 * D), BF16), _spec((tm, 2 * D), row3)),
              (_sds((1, 2 * D), F32), _spec((1, 2 * D), vec3))])
    big["w_branch_gate"] = _gemm(
        "dw_bg",
        [(h2, _spec((T, D), lambda j, n, r: (r, 0)), dgpre, _spec((T, BG_BLK), lambda j, n, r: (r, j)), "tn")],
        (N_CHIPS, 1, 1),
        [(_sds((N_CHIPS, D, BG_BLK), GRAD_WIRE_DTYPE), _spec((None, D, BG_BLK), lambda j, n, r: (j, 0, 0)))],
        (D, BG_BLK))[0][None]
    big["w_out"] = _dw_sq("dw_out", merged, dx2)[None]
    dyr = _proj_sq("d_yr", dy_ret, *wt("w_ret_o"), "nt")[0]
    big["w_ret_o"] = _dw_sq("dw_ret_o", yr, dy_ret)[None]
    dyl = _proj_sq("d_yl", dy_lru, *wt("w_lru_o"), "nt")[0]
    big["w_lru_o"] = _dw_sq("dw_lru_o", yl, dy_lru)[None]

    dq, dk, dv, dgr, dg_retgn = _ret_bwd(dyr, ret, u, qr, kr, states, consts, sm["ret_gn"])

    def lru_out_bwd(irefs, orefs, ids):
        gl, dgl = _gelu_and_grad(irefs[2][...])
        dyl_v = irefs[0][...]
        orefs[0][...] = dyl_v * gl
        orefs[1][...] = (dyl_v * irefs[1][...] * dgl).astype(BF16)

    dhl, dglru = _rowwise("lru_out_bwd", lru_out_bwd, [(dyl, row1), (hl, row1), (u, glru1)],
                          [(_sds((T, D), F32), row1), (_sds((T, D), BF16), row1)], (T // tm,))
    lmb = _lru_scan("lru_scan_bwd", a3, dhl.reshape(T, LRU_BLOCKS, LRU_BLOCK), True).reshape(T, D)
    dxl, dw_r, dw_i, dvec, dcw = _lru_gates_bwd(lmb, hl, av, rg, ig, xc, u, conv_w,
                                                sm["w_rgate"], sm["w_igate"], sm["lru_lambda"])

    du = jnp.concatenate([dq, dk, dv, dgr, dxl, dglru], axis=1)
    tk = T
    big["w_in"] = _gemm(
        "dw_in",
        [(h2, _spec((tk, D), lambda j, n, r: (r, 0)), du, _spec((tk, IN_BLK), lambda j, n, r: (r, j)), "tn")],
        (N_CHIPS, 1, T // tk),
        [(_sds((N_CHIPS, D, IN_BLK), GRAD_WIRE_DTYPE), _spec((None, D, IN_BLK), lambda j, n, r: (j, 0, 0)))],
        (D, IN_BLK))[0][None]
    tf = min(FFN_ROW_TILE, T)
    ex, ou = _rms_bwd_io(x1, sm["mix_norm"], dx2, T, tf)
    dx1, dg_mix = _gemm(
        "d_h2",
        [(du, _spec((tf, 5120), row3), gw["win"], _spec((N_CHIPS, None, IN_BLK, D), lambda i, j, r: (0, 0, 0, 0)), "nn"),
         (dgpre, _spec((tf, 2 * D), row3), gw["wbg"], _spec((N_CHIPS, None, BG_BLK, D), lambda i, j, r: (0, 0, 0, 0)),
          "nn")],
        (T // tf, 1, 1), ou, (tf, D), ex, _rms_bwd_epilogue)

    grad_x, dg_ffn1 = _ffn_bwd("ffn1", dx1, h1, a1, b1, s1, *wt("ffn1_w1"), *wt("ffn1_w3"),
                               *wt("ffn1_w2"), x, sm["ffn1_norm"], big)

    small = {
        "ffn1_norm": dg_ffn1, "mix_norm": dg_mix, "ret_gn": dg_retgn, "conv_b": dvec[3:4],
        "b_rgate": dvec[0:1], "b_igate": dvec[1:2], "lru_lambda": dvec[2:3], "xattn_norm": dg_xattn,
        "mem_norm": dg_mem, "ffn2_norm": dg_ffn2, "final_norm": dg_final, "b_branch_gate": db_bg,
        "conv_w": dcw, "w_rgate": dw_r, "w_igate": dw_i,
    }
    return loss, grad_x, small


ANY_SPEC = pl.BlockSpec(memory_space=pl.ANY)
VMEM_SPEC = pl.BlockSpec(memory_space=pltpu.VMEM)
N_PEER_CHIPS = N_CHIPS - 1


def _mesh_position():
    x, y, c = lax.axis_index("x"), lax.axis_index("y"), lax.axis_index("c")
    chips = [(1 - x, y), (x, 1 - y), (1 - x, 1 - y)]
    return x, y, c, chips


def _chip_index(x, y):
    return 2 * x + y


def _rows_half(ref, axis, h):
    n = ref.shape[axis] // 2
    idx = [slice(None)] * len(ref.shape)
    idx[axis] = pl.ds(pl.multiple_of(h * n, 16), n)
    return ref.at[tuple(idx)]


def _remote(src, dst, send_sem, recv_sem, device):
    return pltpu.make_async_remote_copy(src_ref=src, dst_ref=dst, send_sem=send_sem, recv_sem=recv_sem,
                                        device_id=device, device_id_type=MESH)


def _gather_chips_task(shards, split, landed, part=0, nparts=1):
    keys = list(shards)
    n = len(keys)

    def operands():
        if part:
            return [shards[k] for k in keys] + [landed[k] for k in keys]
        chip_me = _chip_index(lax.axis_index("x"), lax.axis_index("y"))
        bases = [lax.dynamic_update_slice(lax.empty((N_CHIPS,) + shards[k].shape, shards[k].dtype), shards[k][None],
                                          (chip_me,) + (0,) * shards[k].ndim) for k in keys]
        return [shards[k] for k in keys] + bases

    def my_rows(ref, c):
        rows = ref.shape[1] // (2 * nparts)
        return ref.at[:, pl.ds(pl.multiple_of((c * nparts + part) * rows, 16), rows), :]

    def make_direct(ins, outs, send_sem, recv_sem):
        x, y, c, chips = _mesh_position()
        s_me = _chip_index(x, y)
        starts, arrivals = [], []
        for g in range(n):
            for k, chip in enumerate(chips):
                sems = (send_sem(3 * g + k), recv_sem(3 * g + k))
                starts.append(functools.partial(_remote, ins[g], outs[g].at[s_me], *sems, (*chip, c)))
                got = outs[g].at[_chip_index(*chip)]
                arrivals.append(functools.partial(_remote, got, got, *sems, (*chip, c)))
        return starts, arrivals

    def axis_neighbours(x, y, c):
        flip = lambda v, f: v + f * (1 - 2 * v)
        return (flip(x, 1 - c), flip(y, c)), (flip(x, c), flip(y, 1 - c))

    def make_swap(ins, outs, send_sem, recv_sem):
        x, y, c, _ = _mesh_position()
        first, _ = axis_neighbours(x, y, c)
        starts, arrivals = [], []
        for g in range(n):
            sems = (send_sem(3 * g), recv_sem(3 * g))
            starts.append(functools.partial(_remote, my_rows(ins[g], c), my_rows(outs[g].at[_chip_index(x, y)], c),
                                            *sems, (*first, c)))
            got = my_rows(outs[g].at[_chip_index(*first)], c)
            arrivals.append(functools.partial(_remote, got, got, *sems, (*first, c)))
        return starts, arrivals

    def make_pass_on(ins, outs, send_sem, recv_sem):
        x, y, c, _ = _mesh_position()
        first, second = axis_neighbours(x, y, c)
        diagonal = (1 - x, 1 - y)
        starts, arrivals = [], []
        for g in range(n):
            half = lambda chip: my_rows(outs[g].at[_chip_index(*chip)], c)
            for k, (sent, arriving) in enumerate([((x, y), second), (first, diagonal)]):
                sems = (send_sem(3 * g + 1 + k), recv_sem(3 * g + 1 + k))
                src = my_rows(ins[g], c) if k == 0 else half(sent)
                starts.append(functools.partial(_remote, src, half(sent), *sems, (*second, c)))
                arrivals.append(functools.partial(_remote, half(arriving), half(arriving), *sems, (*second, c)))
        return starts, arrivals

    def finish(res):
        landed.update(zip(keys, res))

    shapes = lambda: [_sds((N_CHIPS,) + shards[k].shape, shards[k].dtype) for k in keys]
    aliases = {n + g: g for g in range(n)}
    if not split:
        return _Task("chips", operands, shapes, aliases, 3 * n, make_direct, finish)
    return _Task("neighbours", operands, shapes, aliases, 3 * n, make_swap, finish, make_second=make_pass_on)


def _gather_sibling_task(keys, landed, ready):
    n = len(keys)

    def make(ins, outs, send_sem, recv_sem):
        x, y, c, chips = _mesh_position()
        starts, arrivals = [], []
        for g in range(n):
            for k, chip in enumerate(chips):
                o = outs[g].at[_chip_index(*chip)]
                got, other = _rows_half(o, 1, c), _rows_half(o, 1, 1 - c)
                starts.append(functools.partial(_remote, got, got, send_sem(3 * g + k), recv_sem(3 * g + k),
                                                (x, y, 1 - c)))
                arrivals.append(functools.partial(_remote, other, other, send_sem(3 * g + k), recv_sem(3 * g + k),
                                                  (x, y, 1 - c)))
        return starts, arrivals

    def finish(res):
        ready.update(zip(keys, res))

    return _Task("sibling", lambda: [landed[k] for k in keys],
                 lambda: [_sds(landed[k].shape, landed[k].dtype) for k in keys],
                 {g: g for g in range(n)}, 3 * n, make, finish)


def _pair_swap_task(names, big, got):
    n = len(names)

    def make(ins, outs, send_sem, recv_sem):
        x, y, c, _ = _mesh_position()
        copies = [functools.partial(_remote, _rows_half(ins[a], 2, 1 - c), outs[a], send_sem(a), recv_sem(a),
                                    (x, y, 1 - c)) for a in range(n)]
        return copies, copies

    def shapes():
        return [_sds(big[k].shape[:2] + (big[k].shape[2] // 2, big[k].shape[3]), big[k].dtype) for k in names]

    return _Task("sibling", lambda: [big[k] for k in names], shapes, {}, n, make,
                 lambda res: got.update(zip(names, res)))


def _rs_pair_sum(name, fulls, gots, core):
    n = len(fulls)
    shapes = [(f.shape[2] // 2, f.shape[3]) for f in fulls]

    def body(core_ref, *refs):
        for a_ref, b_ref, o_ref in zip(refs[:n], refs[n:2 * n], refs[2 * n:]):
            o_ref[...] = (a_ref[...].astype(F32) + b_ref[...].astype(F32)).astype(BF16)

    mine = [pl.BlockSpec((None, None) + hc, lambda s, core_ref: (0, s, core_ref[0], 0)) for hc in shapes]
    slot = [pl.BlockSpec((None, None) + hc, lambda s, core_ref: (0, s, 0, 0)) for hc in shapes]
    return _pcall(
        body, name=name, grid=(N_CHIPS,), num_prefetch=1,
        in_specs=mine + slot, out_specs=slot,
        out_shape=[_sds((1, N_CHIPS) + hc, BF16) for hc in shapes],
    )(core, *fulls, *gots)


def _chip_exchange_task(names, pair_sums, by_source, part=0, nparts=1):
    n = len(names)

    def rows(ref):
        h = ref.shape[1] // nparts
        return ref.at[:, pl.ds(part * h, h), :]

    def make(ins, outs, send_sem, recv_sem):
        x, y, c, chips = _mesh_position()
        s_me = _chip_index(x, y)
        starts, arrivals = [], []
        for a in range(n):
            for k, chip in enumerate(chips):
                s_k = _chip_index(*chip)
                starts.append(functools.partial(_remote, rows(ins[a].at[:, s_k]), rows(outs[a].at[:, s_me]),
                                                send_sem(3 * a + k), recv_sem(3 * a + k), (*chip, c)))
                got = rows(outs[a].at[:, s_k])
                arrivals.append(functools.partial(_remote, got, got, send_sem(3 * a + k), recv_sem(3 * a + k),
                                                  (*chip, c)))
        return starts, arrivals

    def operands():
        return [pair_sums[k] for k in names] + ([by_source[k] for k in names] if part else [])

    return _Task("chips", operands, lambda: [_sds(pair_sums[k].shape, pair_sums[k].dtype) for k in names],
                 {n + a: a for a in range(n)} if part else {}, 3 * n, make,
                 lambda res: by_source.update(zip(names, res)))


def _rs_chip_sum(name, owns, parts, chip):
    n = len(owns)
    ns = N_CHIPS
    shapes = [p.shape[2:] for p in parts]

    def body(chip_ref, *refs):
        me = chip_ref[0]
        for i in range(n):
            own_v = refs[i][...].astype(F32)
            slots = refs[n + ns * i:n + ns * (i + 1)]
            tot = None
            for s in range(ns):
                term = jnp.where(me == s, own_v, slots[s][...].astype(F32))
                tot = term if tot is None else tot + term
            refs[n + ns * n + i][...] = tot

    def slot_spec(hc, s):
        return pl.BlockSpec((None, None) + hc,
                            lambda g, chip_ref: (0, jnp.where(chip_ref[0] == s, (s + 1) % ns, s), 0, 0))

    own_specs = [pl.BlockSpec((None, None) + hc, lambda g, chip_ref: (0, chip_ref[0], 0, 0)) for hc in shapes]
    slot_specs = [slot_spec(hc, s) for hc in shapes for s in range(ns)]
    return _pcall(
        body, name=name, grid=(1,), num_prefetch=1,
        in_specs=own_specs + slot_specs,
        out_specs=[pl.BlockSpec((None,) + hc, lambda g, chip_ref: (0, 0, 0)) for hc in shapes],
        out_shape=[_sds((1,) + hc, F32) for hc in shapes],
    )(chip, *owns, *[p for p in parts for _ in range(ns)])


def _pair_gather_task(names, halves, sibling_halves):
    n = len(names)

    def make(ins, outs, send_sem, recv_sem):
        x, y, c, _ = _mesh_position()
        copies = [functools.partial(_remote, ins[a], outs[a], send_sem(a), recv_sem(a), (x, y, 1 - c))
                  for a in range(n)]
        return copies, copies

    return _Task("sibling", lambda: [halves[k] for k in names], lambda: [_sds(halves[k].shape, F32) for k in names],
                 {}, n, make, lambda res: sibling_halves.update(zip(names, res)))


def _small_allreduce(arrs):
    n = len(arrs)
    per = 1 + 2 * N_PEER_CHIPS

    def body(*refs):
        v_refs, o_refs = refs[:n], refs[n:2 * n]
        sib, pair, part = refs[2 * n:3 * n], refs[3 * n:4 * n], refs[4 * n:5 * n]
        send_sems, recv_sems = refs[5 * n:]
        x, y, c, chips = _mesh_position()
        s_me = _chip_index(x, y)

        def quarter(ref, s):
            q = ref.shape[0] // N_CHIPS
            return ref.at[pl.ds(pl.multiple_of(s * q, 8), q)]

        def exchange(first_sem, src, dst_of, arrival_of):
            sems = lambda a, k: (send_sems.at[a * per + first_sem + k], recv_sems.at[a * per + first_sem + k])
            sends = [_remote(src(a, _chip_index(*chip)), dst_of(a, s_me), *sems(a, k), (*chip, c))
                     for a in range(n) for k, chip in enumerate(chips)]
            for cp in sends:
                cp.start()
            for a in range(n):
                for k, chip in enumerate(chips):
                    got = arrival_of(a, _chip_index(*chip))
                    _remote(got, got, *sems(a, k), (*chip, c)).wait_recv()
            for cp in sends:
                cp.wait_send()

        swaps = [_remote(v_refs[a], sib[a], send_sems.at[a * per], recv_sems.at[a * per], (x, y, 1 - c))
                 for a in range(n)]
        for cp in swaps:
            cp.start()
        for cp in swaps:
            cp.wait()
        for a in range(n):
            pair[a][...] = v_refs[a][...] + sib[a][...]
        exchange(1, lambda a, s_k: quarter(pair[a], s_k), lambda a, s: part[a].at[s], lambda a, s_k: part[a].at[s_k])
        for a in range(n):
            part[a][s_me] = quarter(pair[a], s_me)[...]
            q = o_refs[a].shape[0] // N_CHIPS
            o_refs[a][pl.ds(pl.multiple_of(s_me * q, 8), q), :] = (
                ((part[a][0] + part[a][1]) + part[a][2]) + part[a][3])
        exchange(1 + N_PEER_CHIPS, lambda a, s_k: quarter(o_refs[a], s_me), lambda a, s: quarter(o_refs[a], s),
                 lambda a, s_k: quarter(o_refs[a], s_k))

    shapes = [a.shape for a in arrs]
    return _pcall(
        body, name="small_allreduce", grid=(1,), own_peers=("sibling", "chips"),
        in_specs=[VMEM_SPEC] * n, out_specs=[VMEM_SPEC] * n, out_shape=[_sds(s, F32) for s in shapes],
        scratch_shapes=([pltpu.VMEM(s, F32) for s in shapes] * 2
                        + [pltpu.VMEM((N_CHIPS, s[0] // N_CHIPS, s[1]), F32) for s in shapes]
                        + [pltpu.SemaphoreType.DMA((n * per,)), pltpu.SemaphoreType.DMA((n * per,))]),
    )(*arrs)


TRANSPOSED_WEIGHTS = ("ffn1_w1", "ffn1_w3", "ffn2_w1", "ffn2_w3")
SMALL_LAYOUT = [("ffn1_norm", 1), ("mix_norm", 1), ("ret_gn", 1), ("conv_b", 1), ("b_rgate", 1), ("b_igate", 1),
                ("lru_lambda", 1), ("xattn_norm", 1), ("mem_norm", 1), ("ffn2_norm", 1), ("final_norm", 1),
                ("b_branch_gate", 2), ("conv_w", CONV_TAPS)]
SMALL_ROWS = 32
GATE_WEIGHTS = ("w_rgate", "w_igate")
WEIGHT_ORDER = ["ffn1_norm", "ffn1_w1", "ffn1_w3", "ffn1_w2", "mix_norm", "w_in", "ret_gn", "w_ret_o", "conv_w",
                "conv_b", "w_rgate", "b_rgate", "w_igate", "b_igate", "lru_lambda", "w_lru_o", "w_branch_gate",
                "b_branch_gate", "w_out", "xattn_norm", "mem_norm", "w_xq", "w_xk", "w_xv", "w_xo", "ffn2_norm",
                "ffn2_w1", "ffn2_w3", "ffn2_w2", "final_norm"]


SMALL_USED_ROWS = sum(n for _, n in SMALL_LAYOUT)


def _pack_small(parts, extra_row=None):
    rows = [parts[name].reshape(n, D) for name, n in SMALL_LAYOUT]
    if extra_row is not None:
        rows.append(extra_row)
    rows.append(jnp.zeros((SMALL_ROWS - sum(r.shape[0] for r in rows), D), F32))
    return jnp.concatenate(rows, axis=0)


def _unpack_small(packed, shapes):
    out, r = {}, 0
    for name, n in SMALL_LAYOUT:
        out[name] = packed[r:r + n].reshape(shapes[name])
        r += n
    return out


def kernel(x, mem, ffn1_norm, ffn1_w1, ffn1_w3, ffn1_w2, mix_norm, w_in, ret_gn, w_ret_o, conv_w, conv_b, w_rgate, b_rgate, w_igate, b_igate, lru_lambda, w_lru_o, w_branch_gate, b_branch_gate, w_out, xattn_norm, mem_norm, w_xq, w_xk, w_xv, w_xo, ffn2_norm, ffn2_w1, ffn2_w3, ffn2_w2, final_norm, loss_target, m_ffn1_norm, m_ffn1_w1, m_ffn1_w3, m_ffn1_w2, m_mix_norm, m_w_in, m_ret_gn, m_w_ret_o, m_conv_w, m_conv_b, m_w_rgate, m_b_rgate, m_w_igate, m_b_igate, m_lru_lambda, m_w_lru_o, m_w_branch_gate, m_b_branch_gate, m_w_out, m_xattn_norm, m_mem_norm, m_w_xq, m_w_xk, m_w_xv, m_w_xo, m_ffn2_norm, m_ffn2_w1, m_ffn2_w3, m_ffn2_w2, m_final_norm, v_ffn1_norm, v_ffn1_w1, v_ffn1_w3, v_ffn1_w2, v_mix_norm, v_w_in, v_ret_gn, v_w_ret_o, v_conv_w, v_conv_b, v_w_rgate, v_b_rgate, v_w_igate, v_b_igate, v_lru_lambda, v_w_lru_o, v_w_branch_gate, v_b_branch_gate, v_w_out, v_xattn_norm, v_mem_norm, v_w_xq, v_w_xk, v_w_xv, v_w_xo, v_ffn2_norm, v_ffn2_w1, v_ffn2_w3, v_ffn2_w2, v_final_norm):
    given = dict(locals())
    w = {n: given[n] for n in WEIGHT_ORDER}
    mom = {n: given["m_" + n] for n in WEIGHT_ORDER}
    var = {n: given["v_" + n] for n in WEIGHT_ORDER}
    chip = _chip_index(lax.axis_index("x"), lax.axis_index("y"))
    core = lax.axis_index("c").astype(jnp.int32).reshape(1)

    chip_id = chip.astype(jnp.int32).reshape(1)
    sm = {n: w[n] for n in ["ffn1_norm", "mix_norm", "ret_gn", "conv_b", "b_rgate", "b_igate", "lru_lambda",
                            "xattn_norm", "mem_norm", "ffn2_norm", "b_branch_gate"]}
    sm["final_norm"] = w["final_norm"].reshape(1, D)
    sm["w_rgate"] = w["w_rgate"][0]
    sm["w_igate"] = w["w_igate"][0]

    local = lambda a, n: jnp.swapaxes(a[0], 0, 1) if n in TRANSPOSED_WEIGHTS else a[0]
    stack = lambda names: jnp.stack([local(w[n], n) for n in names], axis=0).astype(BF16)
    shard = {"col1": stack(["ffn1_w1", "ffn1_w3"]), "row2a": stack(["ffn1_w2"]),
             "win": jnp.swapaxes(w["w_in"], 1, 2).astype(BF16), "wbg": jnp.swapaxes(w["w_branch_gate"], 1, 2).astype(BF16), "sqA": stack(["w_ret_o", "w_lru_o", "w_out"]),
             "sqB": stack(["w_xq", "w_xk"]), "sqC": stack(["w_xv", "w_xo"]), "col2a": stack(["ffn2_w1"]), "col2b": stack(["ffn2_w3"]),
             "row2b": stack(["ffn2_w2"]), "conv": w["conv_w"]}
    gw, landed = {}, {}
    over_chips = lambda keys: _gather_chips_task({k: shard[k] for k in keys}, True, landed)
    to_sibling = lambda keys: _gather_sibling_task(keys, landed, gw)

    big, got, pair_sums, by_source, halves, sibling_halves, outs = {}, {}, {}, {}, {}, {}, {}
    pair_swap = lambda names: _pair_swap_task(names, big, got)
    exchange = lambda names, part=0, nparts=1: _chip_exchange_task(names, pair_sums, by_source, part, nparts)
    pair_gather = lambda names: _pair_gather_task(names, halves, sibling_halves)

    def pair_sum(names):
        res = _rs_pair_sum("rs_pair_sum_" + names[0], [big[n] for n in names], [got[n] for n in names], core)
        pair_sums.update(zip(names, res))

    def chip_sum(names):
        res = _rs_chip_sum("rs_chip_sum_" + names[0], [pair_sums[n] for n in names], [by_source[n] for n in names],
                           chip_id)
        halves.update(zip(names, res))

    def adamw(names):
        for n in names:
            res = _adamw_halves("adamw_" + n, local(w[n], n), halves[n], sibling_halves[n], 0, local(mom[n], n),
                                local(var[n], n), core)
            outs[n] = tuple((jnp.swapaxes(r, 0, 1) if n in TRANSPOSED_WEIGHTS else r)[None] for r in res)

    do = lambda fn, names: functools.partial(fn, names)
    ffn2_grads = ["ffn2_w2", "ffn2_w1", "ffn2_w3"]
    xattn_grads = ["w_xo", "w_xq", "w_xk", "w_xv"]
    mix_out_grads = ["w_branch_gate", "w_out", "w_ret_o", "w_lru_o"]
    conv_gather = _gather_chips_task({"conv": shard["conv"]}, False, gw)
    half = lambda key, part: _gather_chips_task({key: shard[key]}, True, landed, part, 2)
    plan = _Plan()
    plan.tasks = {
        "ag_first_chips": [over_chips(["col1", "row2a"])],
        "ag_first_sibling": [to_sibling(["col1", "row2a"])],
        "ffn1_up": [over_chips(["win"])],
        "ffn1_down": [to_sibling(["win"]), over_chips(["wbg"]), conv_gather],
        "mix_in": [to_sibling(["wbg"]), over_chips(["sqA"])],
        "ret_fwd": [to_sibling(["sqA"]), over_chips(["col2a"])],
        "lru_gates_fwd": [to_sibling(["col2a"]), over_chips(["sqB"])],
        "lru_scan_fwd": [to_sibling(["sqB"]), over_chips(["sqC"])],
        "mix_gates": [to_sibling(["sqC"]), half("col2b", 0)],
        "y_lru": [half("col2b", 1)],
        "xattn_fwd": [to_sibling(["col2b"])],
        "ffn2_up": [over_chips(["row2b"])],
        "ffn2_up_sibling": [to_sibling(["row2b"])],
        "ffn2_dh": [pair_swap(ffn2_grads)],
        "xattn_bwd": [exchange(["ffn2_w2"], 0, 2)],
        "d_hq": [exchange(["ffn2_w2"], 1, 2)],
        "d_merged": [exchange(["ffn2_w1"], 0, 2), pair_swap(xattn_grads)],
        "lru_out_bwd": [exchange(["w_xo"])],
        "ret_bwd": [exchange(["ffn2_w1"], 1, 2), exchange(["ffn2_w3"], 0, 2), pair_swap(mix_out_grads)],
        "lru_scan_bwd": [exchange(["ffn2_w3"], 1, 2)],
        "lru_gates_bwd": [exchange(["w_xq", "w_xk"]), pair_gather(ffn2_grads)],
        "dw_in": [exchange(["w_xv", "w_out"])],
        "d_h2": [exchange(["w_branch_gate", "w_ret_o", "w_lru_o"]), pair_swap(["w_in"]), pair_gather(xattn_grads)],
        "ffn1_bwd_mid": [exchange(["w_in"], 0, 2), pair_gather(mix_out_grads)],
        "ffn1_dw2": [exchange(["w_in"], 2, 4)],
        "ffn1_dw1": [exchange(["w_in"], 3, 4), pair_swap(["ffn1_w2"])],
        "ffn1_dw3": [exchange(["ffn1_w2"], 0, 2), pair_swap(["ffn1_w1"]), pair_gather(["w_in"])],
        "ffn1_dh": [exchange(["ffn1_w2"], 1, 2), exchange(["ffn1_w1"]), pair_swap(["ffn1_w3"])],
        "small_allreduce": [exchange(["ffn1_w3"]), pair_gather(["ffn1_w2"])],
        "rs_last_gather": [pair_gather(["ffn1_w1", "ffn1_w3"])],
    }
    plan.after = {
        "ffn2_up": [functools.partial(_comm_call, "ffn2_up_sibling")],
        "ffn2_dh": [do(pair_sum, ffn2_grads)],
        "d_merged": [do(pair_sum, xattn_grads)],
        "ret_bwd": [do(pair_sum, mix_out_grads)],
        "lru_scan_bwd": [do(chip_sum, ffn2_grads)],
        "lru_gates_bwd": [do(adamw, ffn2_grads)],
        "dw_in": [do(chip_sum, xattn_grads)],
        "d_h2": [do(chip_sum, mix_out_grads), do(pair_sum, ["w_in"]), do(adamw, xattn_grads)],
        "ffn1_bwd_mid": [do(adamw, mix_out_grads)],
        "ffn1_dw1": [do(chip_sum, ["w_in"]), do(pair_sum, ["ffn1_w2"])],
        "ffn1_dw3": [do(pair_sum, ["ffn1_w1"]), do(adamw, ["w_in"])],
        "ffn1_dh": [do(pair_sum, ["ffn1_w3"]), do(chip_sum, ["ffn1_w2"])],
        "small_allreduce": [do(chip_sum, ["ffn1_w1", "ffn1_w3"]), functools.partial(_comm_call, "rs_last_gather"),
                    do(adamw, ["ffn1_w2", "ffn1_w1", "ffn1_w3"])],
    }
    global _plan
    _plan = plan
    try:
        _comm_call("ag_first_chips")
        _comm_call("ag_first_sibling")
        loss_part, grad_x, small = _local_step(x[0], mem[0], loss_target[0], gw, sm, big)
        gate2d = lambda a: a.reshape(LRU_BLOCKS * LRU_BLOCK, LRU_BLOCK)
        loss_row = jnp.pad(loss_part, ((0, 0), (0, D - loss_part.shape[1])))
        small_sum, *gate_sums = _small_allreduce([_pack_small(small, loss_row)]
                                                 + [gate2d(small[n]) for n in GATE_WEIGHTS])
    finally:
        _plan = None
    assert not plan.tasks and not plan.after, (list(plan.tasks), list(plan.after))
    loss = small_sum[SMALL_USED_ROWS, 0]

    small_shapes = {n: w[n].shape for n, _ in SMALL_LAYOUT}
    small_shapes["conv_w"] = (CONV_TAPS, D)
    conv_grad = lax.dynamic_slice(small_sum[13:13 + CONV_TAPS], (0, chip * SQ_BLK), (CONV_TAPS, SQ_BLK))
    small_w = {n: w[n] for n, _ in SMALL_LAYOUT}
    small_m = {n: mom[n] for n, _ in SMALL_LAYOUT}
    small_v = {n: var[n] for n, _ in SMALL_LAYOUT}
    pad_cols = lambda a: jnp.pad(a[0], ((0, 0), (0, D - SQ_BLK)))
    for dct in (small_w, small_m, small_v):
        dct["conv_w"] = pad_cols(dct["conv_w"])
    g_pack = lax.dynamic_update_slice(small_sum, jnp.pad(conv_grad, ((0, 0), (0, D - SQ_BLK))), (13, 0))
    d_pack, m_pack, v_pack = _adamw("adamw_small", _pack_small(small_w), g_pack, _pack_small(small_m),
                                    _pack_small(small_v))
    unpacked = [_unpack_small(p, small_shapes) for p in (g_pack, d_pack, m_pack, v_pack)]
    for n, _ in SMALL_LAYOUT:
        if n == "conv_w":
            outs[n] = tuple(u[n][:, :SQ_BLK][None] for u in unpacked)
        else:
            outs[n] = tuple(u[n] for u in unpacked)
    for n, gsum in zip(GATE_WEIGHTS, gate_sums):
        d, nm, nv = _adamw("adamw_" + n, gate2d(w[n]), gsum, gate2d(mom[n]), gate2d(var[n]))
        outs[n] = tuple(r.reshape(w[n].shape) for r in (gsum, d, nm, nv))

    result = [loss, grad_x[None]]
    for k in range(4):
        result += [outs[n][k] for n in WEIGHT_ORDER]
    return tuple(result)
```

```python
import functools
import math

import jax
import jax.numpy as jnp
from jax import lax
from jax.experimental import pallas as pl
from jax.experimental.pallas import tpu as pltpu

F32 = jnp.float32
BF16 = jnp.bfloat16
GRAD_WIRE_DTYPE = BF16
MESH = pl.DeviceIdType.MESH

D = 1024
EPS = 1e-6
RET_HEADS = 4
RET_DK = 128
RET_DV = 256
CHUNK = 128
ROPE_BASE = 10000.0
LRU_BLOCKS = 8
LRU_BLOCK = 128
CONV_TAPS = 4
LRU_C = 8.0
D_FF = 2816
X_HEADS = 4
X_HD = 256
N_CHIPS = 4
FF_BLK = D_FF // N_CHIPS
IN_BLK = 5120 // N_CHIPS
BG_BLK = 2048 // N_CHIPS
SQ_BLK = D // N_CHIPS

ADAM_LR = 0.001
ADAM_B1 = 0.9
ADAM_B2 = 0.999
ADAM_EPS = 1e-08
ADAM_WD = 0.01
ADAM_STEP = 10

VMEM_LIMIT_BYTES = 56 * 1024 * 1024
ROW_TILE = 512
WIDE_ROW_TILE = 1024
FFN_ROW_TILE = 256
DW_BLK = D_FF // 2
SCAN_TILE = 256

_DN = {
    "nn": (((1,), (0,)), ((), ())),
    "nt": (((1,), (1,)), ((), ())),
    "tn": (((0,), (0,)), ((), ())),
}


def _cparams(n_axes, collective_id=None):
    return pltpu.CompilerParams(dimension_semantics=("arbitrary",) * n_axes,
                                vmem_limit_bytes=VMEM_LIMIT_BYTES, collective_id=collective_id)


def _dot(a, b, kind):
    if b.ndim == 3:
        b = b.reshape(b.shape[0] * b.shape[1], b.shape[2])
    return lax.dot_general(a.astype(BF16), b.astype(BF16), _DN[kind], preferred_element_type=F32)


def _sigmoid(x):
    return 1.0 / (1.0 + jnp.exp(-x))


def _log1p_pos(e):
    u = 1.0 + e
    return jnp.where(u == 1.0, e, jnp.log(u) * (e / jnp.where(u == 1.0, 1.0, u - 1.0)))


def _expm1(x):
    u = jnp.exp(x)
    lu = jnp.log(u)
    safe = jnp.where(lu == 0.0, 1.0, lu)
    return jnp.where(u == 1.0, x, (u - 1.0) * (x / safe))


def _softplus(z):
    return jnp.maximum(z, 0.0) + _log1p_pos(jnp.exp(-jnp.abs(z)))


_GELU_C = math.sqrt(2.0 / math.pi)


def _gelu_and_grad(x):
    x2 = x * x
    t = jnp.tanh(_GELU_C * (x + 0.044715 * x * x2))
    g = 0.5 * x * (1.0 + t)
    dg = 0.5 * (1.0 + t) + 0.5 * x * (1.0 - t * t) * (_GELU_C * (1.0 + 3.0 * 0.044715 * x2))
    return g, dg


def _rms_fwd(x, g):
    r = lax.rsqrt(jnp.mean(x * x, axis=-1, keepdims=True) + EPS)
    return (x * r) * g


def _rms_bwd(x, g, dh):
    r = lax.rsqrt(jnp.mean(x * x, axis=-1, keepdims=True) + EPS)
    n = x * r
    dyg = dh * g
    dx = r * (dyg - n * jnp.mean(dyg * n, axis=-1, keepdims=True))
    return dx, jnp.sum(dh * n, axis=0, keepdims=True)


def _accumulate(ref, val, first):
    @pl.when(first)
    def _():
        ref[...] = val

    @pl.when(jnp.logical_not(first))
    def _():
        ref[...] += val


def _sds(shape, dtype):
    return jax.ShapeDtypeStruct(tuple(shape), dtype)


def _spec(shape, fn):
    return pl.BlockSpec(tuple(shape), fn)


class _Task:
    def __init__(self, peers, operands, out_shapes, aliases, nsem, make, finish, make_second=None):
        self.peers = peers
        self.operands, self.out_shapes, self.aliases = operands, out_shapes, aliases
        self.nsem, self.make, self.finish = nsem, make, finish
        self.make_second = make_second


class _Plan:
    def __init__(self):
        self.tasks, self.after = {}, {}


_plan = None


PEER_SET_COLLECTIVE_ID = {frozenset({"sibling"}): 1, frozenset({"chips"}): 2, frozenset({"sibling", "chips"}): 3,
                          frozenset({"neighbours"}): 4, frozenset({"sibling", "neighbours"}): 5}


def _peer_set(names):
    names = frozenset(names)
    return names - {"neighbours"} if "chips" in names else names


def _entry_handshake(peer_set):
    x, y, c, chips = _mesh_position()
    peers = [(x, y, 1 - c)] if "sibling" in peer_set else []
    if "chips" in peer_set:
        peers += [(*chip, c) for chip in chips]
    if "neighbours" in peer_set:
        peers += [(*chip, c) for chip in chips[:2]]
    barrier = pltpu.get_barrier_semaphore()
    for peer in peers:
        pl.semaphore_signal(barrier, inc=1, device_id=peer, device_id_type=MESH)
    pl.semaphore_wait(barrier, len(peers))


def _pcall(body, *, name, grid, in_specs, out_specs, out_shape, scratch_shapes=(), num_prefetch=0, own_peers=()):
    single = not isinstance(out_shape, (list, tuple))
    out_shape = [out_shape] if single else list(out_shape)
    out_specs = [out_specs] if single else list(out_specs)
    in_specs = list(in_specs)
    scratch_shapes = list(scratch_shapes)
    tasks = _plan.tasks.pop(name, []) if _plan is not None else []
    after = _plan.after.pop(name, []) if _plan is not None else []
    peer_set = _peer_set([t.peers for t in tasks] + list(own_peers))
    nax = len(grid)

    def run(*operands):
        n_in = len(operands) - num_prefetch
        n_out = len(out_shape)
        t_ops = [t.operands() for t in tasks]
        t_outs = [t.out_shapes() for t in tasks]
        c_ops = [a for ops in t_ops for a in ops]
        c_outs = [s for outs in t_outs for s in outs]
        aliases = {}
        i0, o0 = num_prefetch + n_in, n_out
        for t, ops, outs in zip(tasks, t_ops, t_outs):
            for i_loc, o_loc in t.aliases.items():
                aliases[i0 + i_loc] = o0 + o_loc
            i0 += len(ops)
            o0 += len(outs)
        nsem = sum(t.nsem for t in tasks)

        def wrapped(*refs):
            p = num_prefetch
            pre, ins = refs[:p], refs[p:p + n_in]
            cins = refs[p + n_in:p + n_in + len(c_ops)]
            q = p + n_in + len(c_ops)
            outs, couts = refs[q:q + n_out], refs[q + n_out:q + n_out + len(c_outs)]
            q += n_out + len(c_outs)
            scr = refs[q:q + len(scratch_shapes)]

            def rounds(second):
                send_sems, recv_sems = refs[q + len(scratch_shapes):]
                out = []
                ci = co = so = 0
                for t, ops, souts in zip(tasks, t_ops, t_outs):
                    make = t.make_second if second else t.make
                    out.append(([], []) if make is None else
                               make(cins[ci:ci + len(ops)], couts[co:co + len(souts)],
                                    functools.partial(lambda base, k: send_sems.at[base + k], so),
                                    functools.partial(lambda base, k: recv_sems.at[base + k], so)))
                    ci, co, so = ci + len(ops), co + len(souts), so + t.nsem
                return out

            two_rounds = [t.make_second is not None for t in tasks]
            if peer_set:
                ids = [pl.program_id(k) for k in range(nax)]
                first = functools.reduce(jnp.logical_and, [i == 0 for i in ids])
                last = functools.reduce(jnp.logical_and, [i == g - 1 for i, g in zip(ids, grid)])
                step = functools.reduce(lambda acc, ig: acc * ig[1] + ig[0], zip(ids, grid), 0)
                middle = step == math.prod(grid) // 3

                @pl.when(first)
                def _():
                    _entry_handshake(peer_set)
                    for starts, _ in rounds(False):
                        for copy in starts:
                            copy().start()

            body(*pre, *ins, *outs, *scr)

            if any(two_rounds):
                @pl.when(middle)
                def _():
                    for (_, arrivals), two in zip(rounds(False), two_rounds):
                        if two:
                            for arrival in arrivals:
                                arrival().wait_recv()
                    for starts, _ in rounds(True):
                        for copy in starts:
                            copy().start()

            if tasks:
                @pl.when(last)
                def _():
                    first_round, second_round = rounds(False), rounds(True)
                    for (_, arrivals1), (_, arrivals2), two in zip(first_round, second_round, two_rounds):
                        for arrival in (arrivals2 if two else arrivals1):
                            arrival().wait_recv()
                    for starts, _ in first_round + second_round:
                        for copy in starts:
                            copy().wait_send()

        sems = [pltpu.SemaphoreType.DMA((nsem,)), pltpu.SemaphoreType.DMA((nsem,))] if tasks else []
        res = pl.pallas_call(
            wrapped, name=name,
            grid_spec=pltpu.PrefetchScalarGridSpec(
                num_scalar_prefetch=num_prefetch, grid=tuple(grid),
                in_specs=in_specs + [ANY_SPEC] * len(c_ops),
                out_specs=out_specs + [ANY_SPEC] * len(c_outs),
                scratch_shapes=scratch_shapes + sems),
            out_shape=out_shape + c_outs,
            input_output_aliases=aliases,
            compiler_params=_cparams(nax, PEER_SET_COLLECTIVE_ID[peer_set] if peer_set else None),
        )(*operands, *c_ops)
        co = n_out
        for t, souts in zip(tasks, t_outs):
            t.finish(res[co:co + len(souts)])
            co += len(souts)
        for fn in after:
            fn()
        return res[0] if single else list(res[:n_out])

    return run


def _comm_call(name):
    def body(o_ref):
        o_ref[...] = jnp.zeros_like(o_ref)

    _pcall(body, name=name, grid=(1,), in_specs=[], out_specs=_spec((8, 128), lambda i: (0, 0)),
           out_shape=_sds((8, 128), F32))()


def _gemm(name, terms, grid, outs, acc_shape, extras=(), epilogue=None):
    kinds = [t[4] for t in terms]
    nt, ne, no = len(terms), len(extras), len(outs)
    nred = grid[-1]
    nax = len(grid)

    def body(*refs):
        trefs = refs[:2 * nt]
        erefs = refs[2 * nt:2 * nt + ne]
        orefs = refs[2 * nt + ne:2 * nt + ne + no]
        ids = [pl.program_id(k) for k in range(nax)]
        tot = None
        for t in range(nt):
            d = _dot(trefs[2 * t][...], trefs[2 * t + 1][...], kinds[t])
            tot = d if tot is None else tot + d

        def finish(acc):
            if epilogue is None:
                orefs[0][...] = acc.astype(orefs[0].dtype)
            else:
                epilogue(acc, erefs, orefs, ids)

        if nred == 1:
            finish(tot)
        else:
            acc_ref = refs[-1]
            r = ids[-1]

            @pl.when(r == 0)
            def _():
                acc_ref[...] = tot

            @pl.when(r > 0)
            def _():
                acc_ref[...] += tot

            @pl.when(r == nred - 1)
            def _():
                finish(acc_ref[...])

    operands, in_specs = [], []
    for a, a_spec, b, b_spec, _ in terms:
        operands += [a, b]
        in_specs += [a_spec, b_spec]
    for e, e_spec in extras:
        operands.append(e)
        in_specs.append(e_spec)
    scratch = [pltpu.VMEM(tuple(acc_shape), F32)] if nred > 1 else []
    return _pcall(body, name=name, grid=tuple(grid), in_specs=in_specs, out_specs=[o[1] for o in outs],
                  out_shape=[o[0] for o in outs], scratch_shapes=scratch)(*operands)


def _rowwise(name, fn, ins, outs, grid):
    ni = len(ins)
    nax = len(grid)

    def body(*refs):
        ids = [pl.program_id(k) for k in range(nax)]
        fn(refs[:ni], refs[ni:], ids)

    return _pcall(body, name=name, grid=tuple(grid), in_specs=[i[1] for i in ins],
                  out_specs=[o[1] for o in outs], out_shape=[o[0] for o in outs])(*[i[0] for i in ins])


def _ffn_up(name, h, w1buf, w1_idx, w3buf, w3_idx):
    T = h.shape[0]
    tm = min(FFN_ROW_TILE, T)

    def body(h_ref, w1_ref, w3_ref, a_ref, b_ref, s_ref):
        hv = h_ref[...]
        a = _dot(hv, w1_ref[...], "nt")
        b = _dot(hv, w3_ref[...], "nt")
        a_ref[...] = a.astype(BF16)
        b_ref[...] = b.astype(BF16)
        s_ref[...] = ((a * _sigmoid(a)) * b).astype(BF16)

    blk = _spec((tm, D_FF), lambda i: (i, 0))
    return _pcall(
        body, name=name, grid=(T // tm,),
        in_specs=[_spec((tm, D), lambda i: (i, 0)),
                  _spec((N_CHIPS, None, FF_BLK, D), lambda i: (0, w1_idx, 0, 0)),
                  _spec((N_CHIPS, None, FF_BLK, D), lambda i: (0, w3_idx, 0, 0))],
        out_specs=[blk, blk, blk],
        out_shape=[_sds((T, D_FF), BF16)] * 3,
    )(h, w1buf, w3buf)


def _ffn_down(name, s, wrow2, w2_idx, x_res, g_next=None):
    T = x_res.shape[0]
    tm = min(ROW_TILE, T)
    row = lambda i, j, r: (i, 0)

    def epilogue(acc, erefs, orefs, ids):
        xo = erefs[0][...] + 0.5 * acc
        orefs[0][...] = xo
        if g_next is not None:
            orefs[1][...] = _rms_fwd(xo, erefs[1][...]).astype(BF16)

    extras = [(x_res, _spec((tm, D), row))]
    outs = [(_sds((T, D), F32), _spec((tm, D), row))]
    if g_next is not None:
        extras.append((g_next, _spec((1, D), lambda i, j, r: (0, 0))))
        outs.append((_sds((T, D), BF16), _spec((tm, D), row)))
    return _gemm(
        name,
        [(s, _spec((tm, D_FF), row),
          wrow2, _spec((N_CHIPS, None, FF_BLK, D), lambda i, j, r: (0, w2_idx, 0, 0)), "nn")],
        (T // tm, 1, 1), outs, (tm, D), extras, epilogue)


def _ffn_bwd_mid(name, dx, wrow2, w2_idx, a, b):
    T = dx.shape[0]
    tm = min(FFN_ROW_TILE, T)

    def body(dx_ref, w2_ref, a_ref, b_ref, dab_ref):
        ds = _dot(0.5 * dx_ref[...], w2_ref[...], "nt")
        av = a_ref[...].astype(F32)
        sg = _sigmoid(av)
        dab_ref[0] = (ds * b_ref[...].astype(F32) * (sg * (1.0 + av * (1.0 - sg)))).astype(BF16)
        dab_ref[1] = (ds * (av * sg)).astype(BF16)

    blk = _spec((tm, D_FF), lambda i: (i, 0))
    return _pcall(
        body, name=name, grid=(T // tm,),
        in_specs=[_spec((tm, D), lambda i: (i, 0)),
                  _spec((N_CHIPS, None, FF_BLK, D), lambda i: (0, w2_idx, 0, 0)),
                  blk, blk],
        out_specs=_spec((2, tm, D_FF), lambda i: (0, i, 0)),
        out_shape=_sds((2, T, D_FF), BF16),
    )(dx, wrow2, a, b)


def _rms_bwd_epilogue(acc, erefs, orefs, ids):
    dx, dgp = _rms_bwd(erefs[0][...], erefs[1][...], acc)
    orefs[0][...] = dx + erefs[2][...]
    _accumulate(orefs[1], dgp, ids[0] == 0)


def _rms_bwd_io(x, g, dres, T, tm):
    row = lambda i, j, r: (i, 0)
    vec = lambda i, j, r: (0, 0)
    extras = [(x, _spec((tm, D), row)), (g, _spec((1, D), vec)), (dres, _spec((tm, D), row))]
    outs = [(_sds((T, D), F32), _spec((tm, D), row)), (_sds((1, D), F32), _spec((1, D), vec))]
    return extras, outs


def _ffn_bwd(tag, dx_out, h, a, b, s, w1buf, w1_idx, w3buf, w3_idx, wrow2, w2_idx, x_in, g, big):
    T = dx_out.shape[0]
    dab = _ffn_bwd_mid(tag + "_bwd_mid", dx_out, wrow2, w2_idx, a, b)

    def half_scale(acc, erefs, orefs, ids):
        orefs[0][...] = (0.5 * acc).astype(orefs[0].dtype)

    dw_grid = (D_FF // DW_BLK, 1, 1)
    dw_out = [(_sds((D_FF, D), GRAD_WIRE_DTYPE), _spec((DW_BLK, D), lambda j, n, r: (j, 0)))]
    tokens = _spec((T, D), lambda j, n, r: (0, 0))
    big[tag + "_w2"] = _gemm(
        tag + "_dw2", [(s, _spec((T, DW_BLK), lambda j, n, r: (0, j)), dx_out, tokens, "tn")],
        dw_grid, dw_out, (DW_BLK, D), (), half_scale)[0].reshape(1, N_CHIPS, FF_BLK, D)
    for widx, wname in ((0, "_w1"), (1, "_w3")):
        big[tag + wname] = _gemm(
            tag + "_d" + wname[1:],
            [(dab, _spec((None, T, DW_BLK), functools.partial(lambda w, j, n, r: (w, 0, j), widx)), h, tokens, "tn")],
            dw_grid, dw_out, (DW_BLK, D))[0].reshape(1, N_CHIPS, FF_BLK, D)
    tm = min(FFN_ROW_TILE, T)
    extras, outs = _rms_bwd_io(x_in, g, dx_out, T, tm)
    whole = lambda idx: _spec((N_CHIPS, None, FF_BLK, D), lambda i, j, r: (0, idx, 0, 0))
    dx_in, dg = _gemm(
        tag + "_dh",
        [(dab, _spec((None, tm, D_FF), lambda i, j, r: (0, i, 0)), w1buf, whole(w1_idx), "nn"),
         (dab, _spec((None, tm, D_FF), lambda i, j, r: (1, i, 0)), w3buf, whole(w3_idx), "nn")],
        (T // tm, 1, 1), outs, (tm, D), extras, _rms_bwd_epilogue)
    return dx_in, dg


def _proj_sq(name, a, wsq, idx, kind, out_dtype=F32, extras=(), epilogue=None, outs=None):
    M = a.shape[0]
    tm = min(ROW_TILE, M)
    if outs is None:
        outs = [(_sds((M, D), out_dtype), _spec((tm, D), lambda i, j, r: (i, 0)))]
    return _gemm(
        name,
        [(a, _spec((tm, D), lambda i, j, r: (i, 0)),
          wsq, _spec((N_CHIPS, None, SQ_BLK, D), lambda i, j, r: (0, idx, 0, 0)), kind)],
        (M // tm, 1, 1), outs, (tm, D), extras, epilogue)


def _dw_sq(name, a, b):
    M = a.shape[0]
    tn = D // 2
    whole = _gemm(
        name,
        [(a, _spec((M, D), lambda i, j, r: (0, 0)), b, _spec((M, tn), lambda i, j, r: (0, j)), "tn")],
        (1, D // tn, 1),
        [(_sds((D, D), GRAD_WIRE_DTYPE), _spec((D, tn), lambda i, j, r: (0, j)))],
        (D, tn))[0]
    return whole.reshape(N_CHIPS, SQ_BLK, D)


def _retention_constants(T):
    pos = jnp.arange(T, dtype=F32)
    inv_freq = ROPE_BASE ** (-jnp.arange(0, RET_DK, 2, dtype=F32) / RET_DK)
    ang = pos[:, None] * inv_freq[None, :]
    cosf = jnp.concatenate([jnp.cos(ang), jnp.cos(ang)], axis=1)
    sins = jnp.concatenate([-jnp.sin(ang), jnp.sin(ang)], axis=1)
    lg = jnp.log(1.0 - 2.0 ** (-5.0 - jnp.arange(RET_HEADS, dtype=F32)))
    p = jnp.arange(CHUNK, dtype=F32)
    rel = p[:, None] - p[None, :]
    dmat = jnp.where(rel[None] >= 0, jnp.exp(rel[None] * lg[:, None, None]), 0.0)
    kd = jnp.exp((CHUNK - 1.0 - p)[None, :] * lg[:, None])[:, :, None]
    qd = jnp.exp((p + 1.0)[None, :] * lg[:, None])[:, :, None]
    cd = jnp.exp(CHUNK * lg)[:, None, None]
    return cosf, sins, dmat, kd, qd, cd


def _rot(t, cosv, sinv):
    return t * cosv + pltpu.roll(t, RET_DK // 2, 1) * sinv


def _unrot(t, cosv, sinv):
    return t * cosv - pltpu.roll(t, RET_DK // 2, 1) * sinv


def _ret_const_specs(cm):
    whole = lambda shape: _spec(shape, lambda c: (0,) * len(shape))
    return [
        _spec((CHUNK, RET_DK), lambda c: (cm(c), 0)),
        _spec((CHUNK, RET_DK), lambda c: (cm(c), 0)),
        whole((RET_HEADS, CHUNK, CHUNK)), whole((RET_HEADS, CHUNK, 1)), whole((RET_HEADS, CHUNK, 1)),
        whole((RET_HEADS, 1, 1)),
    ]


def _head(h, width):
    return slice(h * width, (h + 1) * width)


def _ret_fwd(u, consts, ret_gn):
    T = u.shape[0]
    nC = T // CHUNK
    kscale = RET_DK ** -0.5

    def body(q_ref, k_ref, v_ref, g_ref, cos_ref, sin_ref, dm_ref, kd_ref, qd_ref, cd_ref, gn_ref,
             qr_ref, kr_ref, ret_ref, yr_ref, st_ref, state):
        @pl.when(pl.program_id(0) == 0)
        def _():
            state[...] = jnp.zeros_like(state)

        cosv, sinv = cos_ref[...], sin_ref[...]
        for h in range(RET_HEADS):
            hk, hv = _head(h, RET_DK), _head(h, RET_DV)
            q = _rot(q_ref[:, hk], cosv, sinv)
            k = _rot(k_ref[:, hk], cosv, sinv) * kscale
            v = v_ref[:, hv]
            qr_ref[:, hk] = q
            kr_ref[:, hk] = k
            prev = state[h]
            st_ref[h] = prev
            s = _dot(q, k, "nt") * dm_ref[h]
            ret = _dot(s, v, "nn") + _dot(q, prev, "nn") * qd_ref[h]
            state[h] = cd_ref[h] * prev + _dot(k * kd_ref[h], v, "tn")
            ret_ref[:, hv] = ret
            mu = jnp.mean(ret, axis=-1, keepdims=True)
            xc = ret - mu
            yn = xc * lax.rsqrt(jnp.mean(xc * xc, axis=-1, keepdims=True) + EPS)
            g = g_ref[:, hv]
            yr_ref[:, hv] = ((g * _sigmoid(g)) * (yn * gn_ref[:, hv])).astype(BF16)

    cm = lambda c: c
    qk_w, v_w = RET_HEADS * RET_DK, RET_HEADS * RET_DV
    in_specs = [
        _spec((CHUNK, qk_w), lambda c: (c, 0)), _spec((CHUNK, qk_w), lambda c: (c, 1)),
        _spec((CHUNK, v_w), lambda c: (c, 1)), _spec((CHUNK, v_w), lambda c: (c, 2)),
    ] + _ret_const_specs(cm) + [_spec((1, v_w), lambda c: (0, 0))]
    qk_out = _spec((CHUNK, qk_w), lambda c: (c, 0))
    v_out = _spec((CHUNK, v_w), lambda c: (c, 0))
    return _pcall(
        body, name="ret_fwd", grid=(nC,),
        in_specs=in_specs,
        out_specs=[qk_out, qk_out, v_out, v_out,
                   _spec((RET_HEADS, None, RET_DK, RET_DV), lambda c: (0, c, 0, 0))],
        out_shape=[_sds((T, qk_w), F32), _sds((T, qk_w), F32), _sds((T, v_w), F32), _sds((T, v_w), BF16),
                   _sds((RET_HEADS, nC, RET_DK, RET_DV), F32)],
        scratch_shapes=[pltpu.VMEM((RET_HEADS, RET_DK, RET_DV), F32)],
    )(u, u, u, u, *consts, ret_gn)


def _ret_bwd(dyr, ret, u, qr, kr, states, consts, ret_gn):
    T = u.shape[0]
    nC = T // CHUNK
    kscale = RET_DK ** -0.5

    def body(dyr_ref, ret_ref, g_ref, q_ref, k_ref, v_ref, st_ref,
             cos_ref, sin_ref, dm_ref, kd_ref, qd_ref, cd_ref, gn_ref,
             dq_ref, dk_ref, dv_ref, dg_ref, dgn_ref, gstate):
        first = pl.program_id(0) == 0

        @pl.when(first)
        def _():
            gstate[...] = jnp.zeros_like(gstate)

        cosv, sinv = cos_ref[...], sin_ref[...]
        dgn_parts = []
        for h in range(RET_HEADS):
            hk, hv = _head(h, RET_DK), _head(h, RET_DV)
            ret = ret_ref[:, hv]
            mu = jnp.mean(ret, axis=-1, keepdims=True)
            xc = ret - mu
            rs = lax.rsqrt(jnp.mean(xc * xc, axis=-1, keepdims=True) + EPS)
            yn = xc * rs
            gn = gn_ref[:, hv]
            g = g_ref[:, hv]
            sg = _sigmoid(g)
            dyr_v = dyr_ref[:, hv]
            dretn = dyr_v * (g * sg)
            dg_ref[:, hv] = (dyr_v * (yn * gn) * (sg * (1.0 + g * (1.0 - sg)))).astype(BF16)
            dgn_parts.append(jnp.sum(dretn * yn, axis=0, keepdims=True))
            dyn = dretn * gn
            d_o = rs * (dyn - jnp.mean(dyn, axis=-1, keepdims=True)
                        - yn * jnp.mean(dyn * yn, axis=-1, keepdims=True))

            q, k, v = q_ref[:, hk], k_ref[:, hk], v_ref[:, hv]
            dmat, kd, qd = dm_ref[h], kd_ref[h], qd_ref[h]
            prev = st_ref[h]
            gnext = gstate[h]
            s = _dot(q, k, "nt") * dmat
            ds = _dot(d_o, v, "nt") * dmat
            doq = d_o * qd
            dq = _dot(ds, k, "nn") + _dot(doq, prev, "nt")
            dk = _dot(ds, q, "tn") + _dot(v, gnext, "nt") * kd
            dv = _dot(s, d_o, "tn") + _dot(k * kd, gnext, "nn")
            gstate[h] = cd_ref[h] * gnext + _dot(q, doq, "tn")
            dq_ref[:, hk] = _unrot(dq, cosv, sinv).astype(BF16)
            dk_ref[:, hk] = _unrot(dk * kscale, cosv, sinv).astype(BF16)
            dv_ref[:, hv] = dv.astype(BF16)
        _accumulate(dgn_ref, jnp.concatenate(dgn_parts, axis=1), first)

    cm = lambda c: nC - 1 - c
    qk_w, v_w = RET_HEADS * RET_DK, RET_HEADS * RET_DV
    vspec = lambda blk: _spec((CHUNK, v_w), lambda c: (cm(c), blk))
    qspec = _spec((CHUNK, qk_w), lambda c: (cm(c), 0))
    in_specs = [vspec(0), vspec(0), vspec(2), qspec, qspec, vspec(1),
                _spec((RET_HEADS, None, RET_DK, RET_DV), lambda c: (0, cm(c), 0, 0)),
                ] + _ret_const_specs(cm) + [_spec((1, v_w), lambda c: (0, 0))]
    return _pcall(
        body, name="ret_bwd", grid=(nC,),
        in_specs=in_specs,
        out_specs=[qspec, qspec, vspec(0), vspec(0), _spec((1, v_w), lambda c: (0, 0))],
        out_shape=[_sds((T, qk_w), BF16), _sds((T, qk_w), BF16), _sds((T, v_w), BF16), _sds((T, v_w), BF16),
                   _sds((1, v_w), F32)],
        scratch_shapes=[pltpu.VMEM((RET_HEADS, RET_DK, RET_DV), F32)],
    )(dyr, ret, u, qr, kr, u, states, *consts, ret_gn)


def _shift_down(x, s):
    rows = lax.broadcasted_iota(jnp.int32, x.shape, 0)
    return jnp.where(rows >= s, pltpu.roll(x, s, 0), 0.0)


def _shift_up(x, s):
    n = x.shape[0]
    rows = lax.broadcasted_iota(jnp.int32, x.shape, 0)
    return jnp.where(rows < n - s, pltpu.roll(x, n - s, 0), 0.0)


def _lru_specs(T):
    col = lambda off: _spec((T, LRU_BLOCK), lambda g: (0, off + g))
    vec = _spec((1, LRU_BLOCK), lambda g: (0, g))
    wblk = _spec((None, LRU_BLOCK, LRU_BLOCK), lambda g: (g, 0, 0))
    cw = _spec((CONV_TAPS, LRU_BLOCK), lambda g: (0, g))
    return col, vec, wblk, cw


def _lru_gates_fwd(u, conv_w, conv_b, w_r, b_r, w_i, b_i, lam):
    T = u.shape[0]
    col, vec, wblk, cw = _lru_specs(T)

    def body(x_ref, cw_ref, cb_ref, wr_ref, br_ref, wi_ref, bi_ref, lam_ref,
             xc_ref, r_ref, i_ref, a_ref, bx_ref):
        x = x_ref[...]
        w = cw_ref[...]
        xc = (_shift_down(x, 3) * w[0:1] + _shift_down(x, 2) * w[1:2] + _shift_down(x, 1) * w[2:3]
              + x * w[3:4] + cb_ref[...])
        r = _sigmoid(_dot(xc, wr_ref[...], "nn") + br_ref[...])
        i = _sigmoid(_dot(xc, wi_ref[...], "nn") + bi_ref[...])
        la = (-LRU_C) * r * _softplus(-lam_ref[...])
        xc_ref[...] = xc
        r_ref[...] = r
        i_ref[...] = i
        a_ref[...] = jnp.exp(la)
        bx_ref[...] = jnp.sqrt(-_expm1(2.0 * la)) * (i * xc)

    out = col(0)
    return _pcall(
        body, name="lru_gates_fwd", grid=(LRU_BLOCKS,),
        in_specs=[col(24), cw, vec, wblk, vec, wblk, vec, vec],
        out_specs=[out] * 5,
        out_shape=[_sds((T, D), F32)] * 5,
    )(u, conv_w, conv_b, w_r, b_r, w_i, b_i, lam)


def _lru_scan(name, a3, b3, reverse):
    T = a3.shape[0]
    nt = T // SCAN_TILE
    unroll = 8

    def body(a_ref, b_ref, o_ref, carry):
        @pl.when(pl.program_id(0) == 0)
        def _():
            carry[...] = jnp.zeros_like(carry)

        if not reverse:
            def step(t, h):
                h = a_ref[t] * h + b_ref[t]
                o_ref[t] = h
                return h
        else:
            def step(k, c):
                t = SCAN_TILE - 1 - k
                l = b_ref[t] + c
                o_ref[t] = l
                return a_ref[t] * l
        carry[...] = lax.fori_loop(0, SCAN_TILE, step, carry[...], unroll=unroll)

    idx = (lambda i: (nt - 1 - i, 0, 0)) if reverse else (lambda i: (i, 0, 0))
    blk = _spec((SCAN_TILE, LRU_BLOCKS, LRU_BLOCK), idx)
    return _pcall(
        body, name=name, grid=(nt,),
        in_specs=[blk, blk], out_specs=blk,
        out_shape=_sds((T, LRU_BLOCKS, LRU_BLOCK), F32),
        scratch_shapes=[pltpu.VMEM((LRU_BLOCKS, LRU_BLOCK), F32)],
    )(a3, b3)


def _lru_gates_bwd(lmb, hl, a, r, i, xc, u, conv_w, w_r, w_i, lam):
    T = u.shape[0]
    col, vec, wblk, cw = _lru_specs(T)

    def body(l_ref, h_ref, a_ref, r_ref, i_ref, xc_ref, x_ref, cw_ref, wr_ref, wi_ref, lam_ref,
             dx_ref, dwr_ref, dwi_ref, dvec_ref, dcw_ref):
        l = l_ref[...]
        av, rv, iv, xc = a_ref[...], r_ref[...], i_ref[...], xc_ref[...]
        lam_v = lam_ref[...]
        sp = _softplus(-lam_v)
        la = (-LRU_C) * rv * sp
        mult = jnp.sqrt(-_expm1(2.0 * la))
        da = l * _shift_down(h_ref[...], 1)
        dmult = l * (iv * xc)
        di = l * mult * xc
        dxc = l * mult * iv
        dla = da * av - dmult * (av * av) / mult
        dzr = (dla * ((-LRU_C) * sp)) * rv * (1.0 - rv)
        dzi = di * iv * (1.0 - iv)
        dsp = jnp.sum(dla * ((-LRU_C) * rv), axis=0, keepdims=True)
        dlam = dsp * (-_sigmoid(-lam_v))
        dwr_ref[...] = _dot(xc, dzr, "tn")
        dwi_ref[...] = _dot(xc, dzi, "tn")
        dxc = dxc + _dot(dzr, wr_ref[...], "nt") + _dot(dzi, wi_ref[...], "nt")
        x = x_ref[...]
        w = cw_ref[...]
        dx = (dxc * w[3:4] + _shift_up(dxc, 1) * w[2:3] + _shift_up(dxc, 2) * w[1:2]
              + _shift_up(dxc, 3) * w[0:1])
        dx_ref[...] = dx.astype(BF16)
        dvec_ref[...] = jnp.concatenate(
            [jnp.sum(dzr, axis=0, keepdims=True), jnp.sum(dzi, axis=0, keepdims=True), dlam,
             jnp.sum(dxc, axis=0, keepdims=True)], axis=0)
        dcw_ref[...] = jnp.concatenate(
            [jnp.sum(dxc * _shift_down(x, 3 - tap), axis=0, keepdims=True) if tap < 3
             else jnp.sum(dxc * x, axis=0, keepdims=True) for tap in range(CONV_TAPS)], axis=0)

    c0 = col(0)
    return _pcall(
        body, name="lru_gates_bwd", grid=(LRU_BLOCKS,),
        in_specs=[c0, c0, c0, c0, c0, c0, col(24), cw, wblk, wblk, vec],
        out_specs=[c0, wblk, wblk, cw, cw],
        out_shape=[_sds((T, D), BF16), _sds((LRU_BLOCKS, LRU_BLOCK, LRU_BLOCK), F32),
                   _sds((LRU_BLOCKS, LRU_BLOCK, LRU_BLOCK), F32), _sds((4, D), F32), _sds((CONV_TAPS, D), F32)],
    )(lmb, hl, a, r, i, xc, u, conv_w, w_r, w_i, lam)


def _xattn_probs(q, k):
    sc = _dot(q, k, "nt") * (X_HD ** -0.5)
    e = jnp.exp(sc - jnp.max(sc, axis=-1, keepdims=True))
    return e / jnp.sum(e, axis=-1, keepdims=True)


def _xattn_fwd(xq, xk, xv):
    T = xq.shape[0]
    tq = ROW_TILE
    M = xk.shape[0]

    def body(q_ref, k_ref, v_ref, o_ref):
        p = _xattn_probs(q_ref[...], k_ref[...])
        o_ref[...] = _dot(p, v_ref[...], "nn").astype(BF16)

    qs = _spec((tq, X_HD), lambda h, i: (i, h))
    kv = _spec((M, X_HD), lambda h, i: (0, h))
    return _pcall(
        body, name="xattn_fwd", grid=(X_HEADS, T // tq),
        in_specs=[qs, kv, kv], out_specs=qs, out_shape=_sds((T, D), BF16),
    )(xq, xk, xv)


def _xattn_bwd(xq, xk, xv, dxo):
    T = xq.shape[0]
    tq = ROW_TILE
    M = xk.shape[0]

    def body(q_ref, k_ref, v_ref, do_ref, dq_ref, dk_ref, dv_ref):
        first = pl.program_id(1) == 0
        q, k, v, do = q_ref[...], k_ref[...], v_ref[...], do_ref[...]
        p = _xattn_probs(q, k)
        dp = _dot(do, v, "nt")
        ds = p * (dp - jnp.sum(dp * p, axis=-1, keepdims=True)) * (X_HD ** -0.5)
        dq_ref[...] = _dot(ds, k, "nn").astype(BF16)
        _accumulate(dk_ref, _dot(ds, q, "tn"), first)
        _accumulate(dv_ref, _dot(p, do, "tn"), first)

    qs = _spec((tq, X_HD), lambda h, i: (i, h))
    kv = _spec((M, X_HD), lambda h, i: (0, h))
    return _pcall(
        body, name="xattn_bwd", grid=(X_HEADS, T // tq),
        in_specs=[qs, kv, kv, qs], out_specs=[qs, kv, kv],
        out_shape=[_sds((T, D), BF16), _sds((M, D), F32), _sds((M, D), F32)],
    )(xq, xk, xv, dxo)


def _final_loss(x, g, tgt):
    T = x.shape[0]
    tm = ROW_TILE

    def fn(irefs, orefs, ids):
        xv, gv = irefs[0][...], irefs[1][...]
        err = _rms_fwd(xv, gv) - irefs[2][...]
        lp = 0.5 * jnp.sum(jnp.mean(err * err, axis=-1, keepdims=True), axis=0, keepdims=True)
        first = ids[0] == 0
        _accumulate(orefs[0], jnp.broadcast_to(lp, (1, 128)), first)
        dx, dgp = _rms_bwd(xv, gv, err * (1.0 / D))
        orefs[1][...] = dx
        _accumulate(orefs[2], dgp, first)

    row = _spec((tm, D), lambda i: (i, 0))
    vec = _spec((1, D), lambda i: (0, 0))
    return _rowwise(
        "final_loss", fn, [(x, row), (g, vec), (tgt, row)],
        [(_sds((1, 128), F32), _spec((1, 128), lambda i: (0, 0))), (_sds((T, D), F32), row),
         (_sds((1, D), F32), vec)],
        (T // tm,))


def _adamw(name, w, g, m, v):
    R, C = w.shape
    tr = R
    for cand in (512, 352, 256):
        if R % cand == 0:
            tr = cand
            break

    def fn(irefs, orefs, ids):
        delta, mn, vn = _adamw_update(*(r[...] for r in irefs))
        orefs[0][...] = delta
        orefs[1][...] = mn
        orefs[2][...] = vn

    blk = _spec((tr, C), lambda i: (i, 0))
    return _rowwise(name, fn, [(w, blk), (g, blk), (m, blk), (v, blk)],
                    [(_sds((R, C), F32), blk)] * 3, (R // tr,))


def _adamw_update(wv, gv, mv, vv):
    c1 = 1.0 - ADAM_B1 ** ADAM_STEP
    c2 = 1.0 - ADAM_B2 ** ADAM_STEP
    mn = ADAM_B1 * mv + (1.0 - ADAM_B1) * gv
    vn = ADAM_B2 * vv + (1.0 - ADAM_B2) * (gv * gv)
    delta = -ADAM_LR * ((mn / c1) / (jnp.sqrt(vn / c2) + ADAM_EPS) + ADAM_WD * wv)
    return delta, mn, vn


def _adamw_halves(name, w, mine, theirs, widx, m, v, core):
    R, C = w.shape
    H = R // 2
    tr = H
    while tr * C * 4 > (1 << 20) and tr % 16 == 0:
        tr //= 2
    nb = H // tr

    def body(core_ref, w_ref, mine_ref, theirs_ref, m_ref, v_ref, g_out, d_out, m_out, v_out):
        gv = jnp.where(pl.program_id(0) == core_ref[0], mine_ref[...], theirs_ref[...])
        delta, mn, vn = _adamw_update(w_ref[...], gv, m_ref[...], v_ref[...])
        g_out[...] = gv
        d_out[...] = delta
        m_out[...] = mn
        v_out[...] = vn

    full = pl.BlockSpec((tr, C), lambda h, i, core_ref: (h * nb + i, 0))
    mine_spec = pl.BlockSpec((None, tr, C), lambda h, i, core_ref: (widx, jnp.where(h == core_ref[0], i, 0), 0))
    theirs_spec = pl.BlockSpec((None, tr, C), lambda h, i, core_ref: (widx, jnp.where(h == core_ref[0], 0, i), 0))
    return _pcall(
        body, name=name, grid=(2, nb), num_prefetch=1,
        in_specs=[full, mine_spec, theirs_spec, full, full], out_specs=[full] * 4,
        out_shape=[_sds((R, C), F32)] * 4,
    )(core, w, mine, theirs, m, v)


def _rmsnorm(name, x, g):
    M = x.shape[0]
    tm = min(ROW_TILE, M)

    def fn(irefs, orefs, ids):
        orefs[0][...] = _rms_fwd(irefs[0][...], irefs[1][...]).astype(BF16)

    row = _spec((tm, D), lambda i: (i, 0))
    return _rowwise(name, fn, [(x, row), (g, _spec((1, D), lambda i: (0, 0)))],
                    [(_sds((M, D), BF16), row)], (M // tm,))[0]


WEIGHT_AT = {
    "ffn1_w1": ("col1", 0), "ffn1_w3": ("col1", 1), "ffn1_w2": ("row2a", 0),
    "w_ret_o": ("sqA", 0), "w_lru_o": ("sqA", 1), "w_out": ("sqA", 2),
    "w_xq": ("sqB", 0), "w_xk": ("sqB", 1), "w_xv": ("sqC", 0), "w_xo": ("sqC", 1),
    "ffn2_w1": ("col2a", 0), "ffn2_w3": ("col2b", 0), "ffn2_w2": ("row2b", 0),
}


def _local_step(x, mem, tgt, gw, sm, big):
    T = x.shape[0]
    tm = ROW_TILE

    def wt(name):
        key, idx = WEIGHT_AT[name]
        return gw[key], idx

    row3 = lambda i, j, r: (i, 0)
    vec3 = lambda i, j, r: (0, 0)
    rowD = _spec((tm, D), row3)
    vecD = _spec((1, D), vec3)

    def residual_norm(acc, erefs, orefs, ids):
        xo = erefs[0][...] + acc
        orefs[0][...] = xo
        orefs[1][...] = _rms_fwd(xo, erefs[1][...]).astype(BF16)

    def res_norm_io(x_res, g):
        return ([(x_res, rowD), (g, vecD)],
                [(_sds((T, D), F32), rowD), (_sds((T, D), BF16), rowD)])

    h1 = _rmsnorm("ffn1_norm", x, sm["ffn1_norm"])
    a1, b1, s1 = _ffn_up("ffn1_up", h1, *wt("ffn1_w1"), *wt("ffn1_w3"))
    x1, h2 = _ffn_down("ffn1_down", s1, *wt("ffn1_w2"), x, sm["mix_norm"])

    tw = min(WIDE_ROW_TILE, T)
    wideD = _spec((tw, D), row3)
    u = _gemm(
        "mix_in",
        [(h2, wideD, gw["win"], _spec((None, None, IN_BLK, D), lambda i, j, r: (j, 0, 0, 0)), "nt")],
        (T // tw, N_CHIPS, 1),
        [(_sds((T, 5120), F32), _spec((tw, IN_BLK), lambda i, j, r: (i, j)))], (tw, IN_BLK))[0]

    consts = _retention_constants(T)
    qr, kr, ret, yr, states = _ret_fwd(u, consts, sm["ret_gn"])

    conv_w = gw["conv"][:, 0].transpose(1, 0, 2).reshape(CONV_TAPS, D)
    xc, rg, ig, av, bx = _lru_gates_fwd(u, conv_w, sm["conv_b"], sm["w_rgate"], sm["b_rgate"],
                                        sm["w_igate"], sm["b_igate"], sm["lru_lambda"])
    a3 = av.reshape(T, LRU_BLOCKS, LRU_BLOCK)
    hl = _lru_scan("lru_scan_fwd", a3, bx.reshape(T, LRU_BLOCKS, LRU_BLOCK), False).reshape(T, D)

    row1 = _spec((tm, D), lambda i: (i, 0))
    glru1 = _spec((tm, D), lambda i: (i, 4))

    def lru_out(irefs, orefs, ids):
        gl, _ = _gelu_and_grad(irefs[1][...])
        orefs[0][...] = (irefs[0][...] * gl).astype(BF16)

    yl = _rowwise("lru_out", lru_out, [(hl, row1), (u, glru1)], [(_sds((T, D), BF16), row1)], (T // tm,))[0]

    def gate_epilogue(acc, erefs, orefs, ids):
        orefs[0][...] = _sigmoid(acc + erefs[0][...])

    gates = _gemm(
        "mix_gates",
        [(h2, wideD, gw["wbg"], _spec((None, None, BG_BLK, D), lambda i, j, r: (j, 0, 0, 0)), "nt")],
        (T // tw, N_CHIPS, 1),
        [(_sds((T, 2 * D), F32), _spec((tw, BG_BLK), lambda i, j, r: (i, j)))], (tw, BG_BLK),
        [(sm["b_branch_gate"], _spec((1, BG_BLK), lambda i, j, r: (0, j)))], gate_epilogue)[0]

    y_ret = _proj_sq("y_ret", yr, *wt("w_ret_o"), "nn")[0]

    def merge_epilogue(acc, erefs, orefs, ids):
        orefs[0][...] = acc
        orefs[1][...] = (erefs[0][...] * erefs[2][...] + erefs[1][...] * acc).astype(BF16)

    y_lru, merged = _proj_sq(
        "y_lru", yl, *wt("w_lru_o"), "nn",
        extras=[(gates, _spec((tm, D), lambda i, j, r: (i, 0))), (gates, _spec((tm, D), lambda i, j, r: (i, 1))),
                (y_ret, rowD)],
        epilogue=merge_epilogue,
        outs=[(_sds((T, D), F32), rowD), (_sds((T, D), BF16), rowD)])

    ex, ou = res_norm_io(x1, sm["xattn_norm"])
    x2, hq = _proj_sq("mix_out", merged, *wt("w_out"), "nn", extras=ex, epilogue=residual_norm, outs=ou)

    m = _rmsnorm("mem_norm", mem, sm["mem_norm"])
    xq = _proj_sq("xq", hq, *wt("w_xq"), "nn", BF16)[0]
    xk = _proj_sq("xk", m, *wt("w_xk"), "nn", BF16)[0]
    xv = _proj_sq("xv", m, *wt("w_xv"), "nn", BF16)[0]
    xo = _xattn_fwd(xq, xk, xv)
    ex, ou = res_norm_io(x2, sm["ffn2_norm"])
    x3, h3 = _proj_sq("xattn_out", xo, *wt("w_xo"), "nn", extras=ex, epilogue=residual_norm, outs=ou)

    a2, b2, s2 = _ffn_up("ffn2_up", h3, *wt("ffn2_w1"), *wt("ffn2_w3"))
    x4 = _ffn_down("ffn2_down", s2, *wt("ffn2_w2"), x3)[0]
    loss, dx4, dg_final = _final_loss(x4, sm["final_norm"], tgt)

    dx3, dg_ffn2 = _ffn_bwd("ffn2", dx4, h3, a2, b2, s2, *wt("ffn2_w1"), *wt("ffn2_w3"),
                            *wt("ffn2_w2"), x3, sm["ffn2_norm"], big)

    dxo = _proj_sq("d_xo", dx3, *wt("w_xo"), "nt", BF16)[0]
    big["w_xo"] = _dw_sq("dw_xo", xo, dx3)[None]
    dxq, dxk, dxv = _xattn_bwd(xq, xk, xv, dxo)
    big["w_xq"] = _dw_sq("dw_xq", hq, dxq)[None]
    ex, ou = _rms_bwd_io(x2, sm["xattn_norm"], dx3, T, tm)
    dx2, dg_xattn = _proj_sq("d_hq", dxq, *wt("w_xq"), "nt", extras=ex, epilogue=_rms_bwd_epilogue, outs=ou)
    big["w_xk"] = _dw_sq("dw_xk", m, dxk)[None]
    big["w_xv"] = _dw_sq("dw_xv", m, dxv)[None]

    M = mem.shape[0]

    def mem_norm_epilogue(acc, erefs, orefs, ids):
        _, dgp = _rms_bwd(erefs[0][...], erefs[1][...], acc)
        orefs[0][...] = dgp

    wsq_spec = lambda idx: _spec((N_CHIPS, None, SQ_BLK, D), lambda i, j, r: (0, idx, 0, 0))
    memD = _spec((M, D), row3)
    dg_mem = _gemm(
        "d_mem_norm",
        [(dxk, memD, wt("w_xk")[0], wsq_spec(wt("w_xk")[1]), "nt"),
         (dxv, memD, wt("w_xv")[0], wsq_spec(wt("w_xv")[1]), "nt")],
        (1, 1, 1), [(_sds((1, D), F32), vecD)], (M, D),
        [(mem, memD), (sm["mem_norm"], vecD)], mem_norm_epilogue)[0]

    def merged_bwd_epilogue(acc, erefs, orefs, ids):
        gr, gl, yrv, ylv = (e[...] for e in erefs)
        orefs[0][...] = (acc * gr).astype(BF16)
        orefs[1][...] = (acc * gl).astype(BF16)
        dgr = acc * yrv * gr * (1.0 - gr)
        dgl = acc * ylv * gl * (1.0 - gl)
        orefs[2][:, :D] = dgr.astype(BF16)
        orefs[2][:, D:] = dgl.astype(BF16)
        dbb = jnp.concatenate([jnp.sum(dgr, axis=0, keepdims=True), jnp.sum(dgl, axis=0, keepdims=True)], axis=1)
        _accumulate(orefs[3], dbb, ids[0] == 0)

    dy_ret, dy_lru, dgpre, db_bg = _proj_sq(
        "d_merged", dx2, *wt("w_out"), "nt",
        extras=[(gates, _spec((tm, D), lambda i, j, r: (i, 0))), (gates, _spec((tm, D), lambda i, j, r: (i, 1))),
                (y_ret, rowD), (y_lru, rowD)],
        epilogue=merged_bwd_epilogue,
        outs=[(_sds((T, D), BF16), rowD), (_sds((T, D), BF16), rowD),
              (_sds((T, 2 * D), BF16), _spec((tm, 2 * D), row3)),
              (_sds((1, 2 * D), F32), _spec((1, 2 * D), vec3))])
    big["w_branch_gate"] = _gemm(
        "dw_bg",
        [(h2, _spec((T, D), lambda j, n, r: (r, 0)), dgpre, _spec((T, BG_BLK), lambda j, n, r: (r, j)), "tn")],
        (N_CHIPS, 1, 1),
        [(_sds((N_CHIPS, D, BG_BLK), GRAD_WIRE_DTYPE), _spec((None, D, BG_BLK), lambda j, n, r: (j, 0, 0)))],
        (D, BG_BLK))[0][None]
    big["w_out"] = _dw_sq("dw_out", merged, dx2)[None]
    dyr = _proj_sq("d_yr", dy_ret, *wt("w_ret_o"), "nt")[0]
    big["w_ret_o"] = _dw_sq("dw_ret_o", yr, dy_ret)[None]
    dyl = _proj_sq("d_yl", dy_lru, *wt("w_lru_o"), "nt")[0]
    big["w_lru_o"] = _dw_sq("dw_lru_o", yl, dy_lru)[None]

    dq, dk, dv, dgr, dg_retgn = _ret_bwd(dyr, ret, u, qr, kr, states, consts, sm["ret_gn"])

    def lru_out_bwd(irefs, orefs, ids):
        gl, dgl = _gelu_and_grad(irefs[2][...])
        dyl_v = irefs[0][...]
        orefs[0][...] = dyl_v * gl
        orefs[1][...] = (dyl_v * irefs[1][...] * dgl).astype(BF16)

    dhl, dglru = _rowwise("lru_out_bwd", lru_out_bwd, [(dyl, row1), (hl, row1), (u, glru1)],
                          [(_sds((T, D), F32), row1), (_sds((T, D), BF16), row1)], (T // tm,))
    lmb = _lru_scan("lru_scan_bwd", a3, dhl.reshape(T, LRU_BLOCKS, LRU_BLOCK), True).reshape(T, D)
    dxl, dw_r, dw_i, dvec, dcw = _lru_gates_bwd(lmb, hl, av, rg, ig, xc, u, conv_w,
                                                sm["w_rgate"], sm["w_igate"], sm["lru_lambda"])

    du = jnp.concatenate([dq, dk, dv, dgr, dxl, dglru], axis=1)
    tk = T
    big["w_in"] = _gemm(
        "dw_in",
        [(h2, _spec((tk, D), lambda j, n, r: (r, 0)), du, _spec((tk, IN_BLK), lambda j, n, r: (r, j)), "tn")],
        (N_CHIPS, 1, T // tk),
        [(_sds((N_CHIPS, D, IN_BLK), GRAD_WIRE_DTYPE), _spec((None, D, IN_BLK), lambda j, n, r: (j, 0, 0)))],
        (D, IN_BLK))[0][None]
    tf = min(FFN_ROW_TILE, T)
    ex, ou = _rms_bwd_io(x1, sm["mix_norm"], dx2, T, tf)
    dx1, dg_mix = _gemm(
        "d_h2",
        [(du, _spec((tf, 5120), row3), gw["win"], _spec((N_CHIPS, None, IN_BLK, D), lambda i, j, r: (0, 0, 0, 0)), "nn"),
         (dgpre, _spec((tf, 2 * D), row3), gw["wbg"], _spec((N_CHIPS, None, BG_BLK, D), lambda i, j, r: (0, 0, 0, 0)),
          "nn")],
        (T // tf, 1, 1), ou, (tf, D), ex, _rms_bwd_epilogue)

    grad_x, dg_ffn1 = _ffn_bwd("ffn1", dx1, h1, a1, b1, s1, *wt("ffn1_w1"), *wt("ffn1_w3"),
                               *wt("ffn1_w2"), x, sm["ffn1_norm"], big)

    small = {
        "ffn1_norm": dg_ffn1, "mix_norm": dg_mix, "ret_gn": dg_retgn, "conv_b": dvec[3:4],
        "b_rgate": dvec[0:1], "b_igate": dvec[1:2], "lru_lambda": dvec[2:3], "xattn_norm": dg_xattn,
        "mem_norm": dg_mem, "ffn2_norm": dg_ffn2, "final_norm": dg_final, "b_branch_gate": db_bg,
        "conv_w": dcw, "w_rgate": dw_r, "w_igate": dw_i,
    }
    return loss, grad_x, small


ANY_SPEC = pl.BlockSpec(memory_space=pl.ANY)
VMEM_SPEC = pl.BlockSpec(memory_space=pltpu.VMEM)
N_PEER_CHIPS = N_CHIPS - 1


def _mesh_position():
    x, y, c = lax.axis_index("x"), lax.axis_index("y"), lax.axis_index("c")
    chips = [(1 - x, y), (x, 1 - y), (1 - x, 1 - y)]
    return x, y, c, chips


def _chip_index(x, y):
    return 2 * x + y


def _rows_half(ref, axis, h):
    n = ref.shape[axis] // 2
    idx = [slice(None)] * len(ref.shape)
    idx[axis] = pl.ds(pl.multiple_of(h * n, 16), n)
    return ref.at[tuple(idx)]


def _remote(src, dst, send_sem, recv_sem, device):
    return pltpu.make_async_remote_copy(src_ref=src, dst_ref=dst, send_sem=send_sem, recv_sem=recv_sem,
                                        device_id=device, device_id_type=MESH)


def _gather_chips_task(shards, split, landed, part=0, nparts=1):
    keys = list(shards)
    n = len(keys)

    def operands():
        if part:
            return [shards[k] for k in keys] + [landed[k] for k in keys]
        chip_me = _chip_index(lax.axis_index("x"), lax.axis_index("y"))
        bases = [lax.dynamic_update_slice(lax.empty((N_CHIPS,) + shards[k].shape, shards[k].dtype), shards[k][None],
                                          (chip_me,) + (0,) * shards[k].ndim) for k in keys]
        return [shards[k] for k in keys] + bases

    def my_rows(ref, c):
        rows = ref.shape[1] // (2 * nparts)
        return ref.at[:, pl.ds(pl.multiple_of((c * nparts + part) * rows, 16), rows), :]

    def make_direct(ins, outs, send_sem, recv_sem):
        x, y, c, chips = _mesh_position()
        s_me = _chip_index(x, y)
        starts, arrivals = [], []
        for g in range(n):
            for k, chip in enumerate(chips):
                sems = (send_sem(3 * g + k), recv_sem(3 * g + k))
                starts.append(functools.partial(_remote, ins[g], outs[g].at[s_me], *sems, (*chip, c)))
                got = outs[g].at[_chip_index(*chip)]
                arrivals.append(functools.partial(_remote, got, got, *sems, (*chip, c)))
        return starts, arrivals

    def axis_neighbours(x, y, c):
        flip = lambda v, f: v + f * (1 - 2 * v)
        return (flip(x, 1 - c), flip(y, c)), (flip(x, c), flip(y, 1 - c))

    def make_swap(ins, outs, send_sem, recv_sem):
        x, y, c, _ = _mesh_position()
        first, _ = axis_neighbours(x, y, c)
        starts, arrivals = [], []
        for g in range(n):
            sems = (send_sem(3 * g), recv_sem(3 * g))
            starts.append(functools.partial(_remote, my_rows(ins[g], c), my_rows(outs[g].at[_chip_index(x, y)], c),
                                            *sems, (*first, c)))
            got = my_rows(outs[g].at[_chip_index(*first)], c)
            arrivals.append(functools.partial(_remote, got, got, *sems, (*first, c)))
        return starts, arrivals

    def make_pass_on(ins, outs, send_sem, recv_sem):
        x, y, c, _ = _mesh_position()
        first, second = axis_neighbours(x, y, c)
        diagonal = (1 - x, 1 - y)
        starts, arrivals = [], []
        for g in range(n):
            half = lambda chip: my_rows(outs[g].at[_chip_index(*chip)], c)
            for k, (sent, arriving) in enumerate([((x, y), second), (first, diagonal)]):
                sems = (send_sem(3 * g + 1 + k), recv_sem(3 * g + 1 + k))
                src = my_rows(ins[g], c) if k == 0 else half(sent)
                starts.append(functools.partial(_remote, src, half(sent), *sems, (*second, c)))
                arrivals.append(functools.partial(_remote, half(arriving), half(arriving), *sems, (*second, c)))
        return starts, arrivals

    def finish(res):
        landed.update(zip(keys, res))

    shapes = lambda: [_sds((N_CHIPS,) + shards[k].shape, shards[k].dtype) for k in keys]
    aliases = {n + g: g for g in range(n)}
    if not split:
        return _Task("chips", operands, shapes, aliases, 3 * n, make_direct, finish)
    return _Task("neighbours", operands, shapes, aliases, 3 * n, make_swap, finish, make_second=make_pass_on)


def _gather_sibling_task(keys, landed, ready):
    n = len(keys)

    def make(ins, outs, send_sem, recv_sem):
        x, y, c, chips = _mesh_position()
        starts, arrivals = [], []
        for g in range(n):
            for k, chip in enumerate(chips):
                o = outs[g].at[_chip_index(*chip)]
                got, other = _rows_half(o, 1, c), _rows_half(o, 1, 1 - c)
                starts.append(functools.partial(_remote, got, got, send_sem(3 * g + k), recv_sem(3 * g + k),
                                                (x, y, 1 - c)))
                arrivals.append(functools.partial(_remote, other, other, send_sem(3 * g + k), recv_sem(3 * g + k),
                                                  (x, y, 1 - c)))
        return starts, arrivals

    def finish(res):
        ready.update(zip(keys, res))

    return _Task("sibling", lambda: [landed[k] for k in keys],
                 lambda: [_sds(landed[k].shape, landed[k].dtype) for k in keys],
                 {g: g for g in range(n)}, 3 * n, make, finish)


def _pair_swap_task(names, big, got):
    n = len(names)

    def make(ins, outs, send_sem, recv_sem):
        x, y, c, _ = _mesh_position()
        copies = [functools.partial(_remote, _rows_half(ins[a], 2, 1 - c), outs[a], send_sem(a), recv_sem(a),
                                    (x, y, 1 - c)) for a in range(n)]
        return copies, copies

    def shapes():
        return [_sds(big[k].shape[:2] + (big[k].shape[2] // 2, big[k].shape[3]), big[k].dtype) for k in names]

    return _Task("sibling", lambda: [big[k] for k in names], shapes, {}, n, make,
                 lambda res: got.update(zip(names, res)))


def _rs_pair_sum(name, fulls, gots, core):
    n = len(fulls)
    shapes = [(f.shape[2] // 2, f.shape[3]) for f in fulls]

    def body(core_ref, *refs):
        for a_ref, b_ref, o_ref in zip(refs[:n], refs[n:2 * n], refs[2 * n:]):
            o_ref[...] = (a_ref[...].astype(F32) + b_ref[...].astype(F32)).astype(BF16)

    mine = [pl.BlockSpec((None, None) + hc, lambda s, core_ref: (0, s, core_ref[0], 0)) for hc in shapes]
    slot = [pl.BlockSpec((None, None) + hc, lambda s, core_ref: (0, s, 0, 0)) for hc in shapes]
    return _pcall(
        body, name=name, grid=(N_CHIPS,), num_prefetch=1,
        in_specs=mine + slot, out_specs=slot,
        out_shape=[_sds((1, N_CHIPS) + hc, BF16) for hc in shapes],
    )(core, *fulls, *gots)


def _chip_exchange_task(names, pair_sums, by_source, part=0, nparts=1):
    n = len(names)

    def rows(ref):
        h = ref.shape[1] // nparts
        return ref.at[:, pl.ds(part * h, h), :]

    def make(ins, outs, send_sem, recv_sem):
        x, y, c, chips = _mesh_position()
        s_me = _chip_index(x, y)
        starts, arrivals = [], []
        for a in range(n):
            for k, chip in enumerate(chips):
                s_k = _chip_index(*chip)
                starts.append(functools.partial(_remote, rows(ins[a].at[:, s_k]), rows(outs[a].at[:, s_me]),
                                                send_sem(3 * a + k), recv_sem(3 * a + k), (*chip, c)))
                got = rows(outs[a].at[:, s_k])
                arrivals.append(functools.partial(_remote, got, got, send_sem(3 * a + k), recv_sem(3 * a + k),
                                                  (*chip, c)))
        return starts, arrivals

    def operands():
        return [pair_sums[k] for k in names] + ([by_source[k] for k in names] if part else [])

    return _Task("chips", operands, lambda: [_sds(pair_sums[k].shape, pair_sums[k].dtype) for k in names],
                 {n + a: a for a in range(n)} if part else {}, 3 * n, make,
                 lambda res: by_source.update(zip(names, res)))


def _rs_chip_sum(name, owns, parts, chip):
    n = len(owns)
    ns = N_CHIPS
    shapes = [p.shape[2:] for p in parts]

    def body(chip_ref, *refs):
        me = chip_ref[0]
        for i in range(n):
            own_v = refs[i][...].astype(F32)
            slots = refs[n + ns * i:n + ns * (i + 1)]
            tot = None
            for s in range(ns):
                term = jnp.where(me == s, own_v, slots[s][...].astype(F32))
                tot = term if tot is None else tot + term
            refs[n + ns * n + i][...] = tot

    def slot_spec(hc, s):
        return pl.BlockSpec((None, None) + hc,
                            lambda g, chip_ref: (0, jnp.where(chip_ref[0] == s, (s + 1) % ns, s), 0, 0))

    own_specs = [pl.BlockSpec((None, None) + hc, lambda g, chip_ref: (0, chip_ref[0], 0, 0)) for hc in shapes]
    slot_specs = [slot_spec(hc, s) for hc in shapes for s in range(ns)]
    return _pcall(
        body, name=name, grid=(1,), num_prefetch=1,
        in_specs=own_specs + slot_specs,
        out_specs=[pl.BlockSpec((None,) + hc, lambda g, chip_ref: (0, 0, 0)) for hc in shapes],
        out_shape=[_sds((1,) + hc, F32) for hc in shapes],
    )(chip, *owns, *[p for p in parts for _ in range(ns)])


def _pair_gather_task(names, halves, sibling_halves):
    n = len(names)

    def make(ins, outs, send_sem, recv_sem):
        x, y, c, _ = _mesh_position()
        copies = [functools.partial(_remote, ins[a], outs[a], send_sem(a), recv_sem(a), (x, y, 1 - c))
                  for a in range(n)]
        return copies, copies

    return _Task("sibling", lambda: [halves[k] for k in names], lambda: [_sds(halves[k].shape, F32) for k in names],
                 {}, n, make, lambda res: sibling_halves.update(zip(names, res)))


def _small_allreduce(arrs):
    n = len(arrs)
    per = 1 + 2 * N_PEER_CHIPS

    def body(*refs):
        v_refs, o_refs = refs[:n], refs[n:2 * n]
        sib, pair, part = refs[2 * n:3 * n], refs[3 * n:4 * n], refs[4 * n:5 * n]
        send_sems, recv_sems = refs[5 * n:]
        x, y, c, chips = _mesh_position()
        s_me = _chip_index(x, y)

        def quarter(ref, s):
            q = ref.shape[0] // N_CHIPS
            return ref.at[pl.ds(pl.multiple_of(s * q, 8), q)]

        def exchange(first_sem, src, dst_of, arrival_of):
            sems = lambda a, k: (send_sems.at[a * per + first_sem + k], recv_sems.at[a * per + first_sem + k])
            sends = [_remote(src(a, _chip_index(*chip)), dst_of(a, s_me), *sems(a, k), (*chip, c))
                     for a in range(n) for k, chip in enumerate(chips)]
            for cp in sends:
                cp.start()
            for a in range(n):
                for k, chip in enumerate(chips):
                    got = arrival_of(a, _chip_index(*chip))
                    _remote(got, got, *sems(a, k), (*chip, c)).wait_recv()
            for cp in sends:
                cp.wait_send()

        swaps = [_remote(v_refs[a], sib[a], send_sems.at[a * per], recv_sems.at[a * per], (x, y, 1 - c))
                 for a in range(n)]
        for cp in swaps:
            cp.start()
        for cp in swaps:
            cp.wait()
        for a in range(n):
            pair[a][...] = v_refs[a][...] + sib[a][...]
        exchange(1, lambda a, s_k: quarter(pair[a], s_k), lambda a, s: part[a].at[s], lambda a, s_k: part[a].at[s_k])
        for a in range(n):
            part[a][s_me] = quarter(pair[a], s_me)[...]
            q = o_refs[a].shape[0] // N_CHIPS
            o_refs[a][pl.ds(pl.multiple_of(s_me * q, 8), q), :] = (
                ((part[a][0] + part[a][1]) + part[a][2]) + part[a][3])
        exchange(1 + N_PEER_CHIPS, lambda a, s_k: quarter(o_refs[a], s_me), lambda a, s: quarter(o_refs[a], s),
                 lambda a, s_k: quarter(o_refs[a], s_k))

    shapes = [a.shape for a in arrs]
    return _pcall(
        body, name="small_allreduce", grid=(1,), own_peers=("sibling", "chips"),
        in_specs=[VMEM_SPEC] * n, out_specs=[VMEM_SPEC] * n, out_shape=[_sds(s, F32) for s in shapes],
        scratch_shapes=([pltpu.VMEM(s, F32) for s in shapes] * 2
                        + [pltpu.VMEM((N_CHIPS, s[0] // N_CHIPS, s[1]), F32) for s in shapes]
                        + [pltpu.SemaphoreType.DMA((n * per,)), pltpu.SemaphoreType.DMA((n * per,))]),
    )(*arrs)


TRANSPOSED_WEIGHTS = ("ffn1_w1", "ffn1_w3", "ffn2_w1", "ffn2_w3")
SMALL_LAYOUT = [("ffn1_norm", 1), ("mix_norm", 1), ("ret_gn", 1), ("conv_b", 1), ("b_rgate", 1), ("b_igate", 1),
                ("lru_lambda", 1), ("xattn_norm", 1), ("mem_norm", 1), ("ffn2_norm", 1), ("final_norm", 1),
                ("b_branch_gate", 2), ("conv_w", CONV_TAPS)]
SMALL_ROWS = 32
GATE_WEIGHTS = ("w_rgate", "w_igate")
WEIGHT_ORDER = ["ffn1_norm", "ffn1_w1", "ffn1_w3", "ffn1_w2", "mix_norm", "w_in", "ret_gn", "w_ret_o", "conv_w",
                "conv_b", "w_rgate", "b_rgate", "w_igate", "b_igate", "lru_lambda", "w_lru_o", "w_branch_gate",
                "b_branch_gate", "w_out", "xattn_norm", "mem_norm", "w_xq", "w_xk", "w_xv", "w_xo", "ffn2_norm",
                "ffn2_w1", "ffn2_w3", "ffn2_w2", "final_norm"]


SMALL_USED_ROWS = sum(n for _, n in SMALL_LAYOUT)


def _pack_small(parts, extra_row=None):
    rows = [parts[name].reshape(n, D) for name, n in SMALL_LAYOUT]
    if extra_row is not None:
        rows.append(extra_row)
    rows.append(jnp.zeros((SMALL_ROWS - sum(r.shape[0] for r in rows), D), F32))
    return jnp.concatenate(rows, axis=0)


def _unpack_small(packed, shapes):
    out, r = {}, 0
    for name, n in SMALL_LAYOUT:
        out[name] = packed[r:r + n].reshape(shapes[name])
        r += n
    return out


def kernel(x, mem, ffn1_norm, ffn1_w1, ffn1_w3, ffn1_w2, mix_norm, w_in, ret_gn, w_ret_o, conv_w, conv_b, w_rgate, b_rgate, w_igate, b_igate, lru_lambda, w_lru_o, w_branch_gate, b_branch_gate, w_out, xattn_norm, mem_norm, w_xq, w_xk, w_xv, w_xo, ffn2_norm, ffn2_w1, ffn2_w3, ffn2_w2, final_norm, loss_target, m_ffn1_norm, m_ffn1_w1, m_ffn1_w3, m_ffn1_w2, m_mix_norm, m_w_in, m_ret_gn, m_w_ret_o, m_conv_w, m_conv_b, m_w_rgate, m_b_rgate, m_w_igate, m_b_igate, m_lru_lambda, m_w_lru_o, m_w_branch_gate, m_b_branch_gate, m_w_out, m_xattn_norm, m_mem_norm, m_w_xq, m_w_xk, m_w_xv, m_w_xo, m_ffn2_norm, m_ffn2_w1, m_ffn2_w3, m_ffn2_w2, m_final_norm, v_ffn1_norm, v_ffn1_w1, v_ffn1_w3, v_ffn1_w2, v_mix_norm, v_w_in, v_ret_gn, v_w_ret_o, v_conv_w, v_conv_b, v_w_rgate, v_b_rgate, v_w_igate, v_b_igate, v_lru_lambda, v_w_lru_o, v_w_branch_gate, v_b_branch_gate, v_w_out, v_xattn_norm, v_mem_norm, v_w_xq, v_w_xk, v_w_xv, v_w_xo, v_ffn2_norm, v_ffn2_w1, v_ffn2_w3, v_ffn2_w2, v_final_norm):
    given = dict(locals())
    w = {n: given[n] for n in WEIGHT_ORDER}
    mom = {n: given["m_" + n] for n in WEIGHT_ORDER}
    var = {n: given["v_" + n] for n in WEIGHT_ORDER}
    chip = _chip_index(lax.axis_index("x"), lax.axis_index("y"))
    core = lax.axis_index("c").astype(jnp.int32).reshape(1)

    chip_id = chip.astype(jnp.int32).reshape(1)
    sm = {n: w[n] for n in ["ffn1_norm", "mix_norm", "ret_gn", "conv_b", "b_rgate", "b_igate", "lru_lambda",
                            "xattn_norm", "mem_norm", "ffn2_norm", "b_branch_gate"]}
    sm["final_norm"] = w["final_norm"].reshape(1, D)
    sm["w_rgate"] = w["w_rgate"][0]
    sm["w_igate"] = w["w_igate"][0]

    local = lambda a, n: jnp.swapaxes(a[0], 0, 1) if n in TRANSPOSED_WEIGHTS else a[0]
    stack = lambda names: jnp.stack([local(w[n], n) for n in names], axis=0).astype(BF16)
    shard = {"col1": stack(["ffn1_w1", "ffn1_w3"]), "row2a": stack(["ffn1_w2"]),
             "win": jnp.swapaxes(w["w_in"], 1, 2).astype(BF16),
             "wbg": jnp.swapaxes(w["w_branch_gate"], 1, 2).astype(BF16),
             "sqA": stack(["w_ret_o", "w_lru_o", "w_out"]), "sqB": stack(["w_xq", "w_xk"]),
             "sqC": stack(["w_xv", "w_xo"]), "col2a": stack(["ffn2_w1"]), "col2b": stack(["ffn2_w3"]),
             "row2b": stack(["ffn2_w2"]), "conv": w["conv_w"]}
    gw, landed = {}, {}
    over_chips = lambda keys: _gather_chips_task({k: shard[k] for k in keys}, True, landed)
    to_sibling = lambda keys: _gather_sibling_task(keys, landed, gw)

    big, got, pair_sums, by_source, halves, sibling_halves, outs = {}, {}, {}, {}, {}, {}, {}
    pair_swap = lambda names: _pair_swap_task(names, big, got)
    exchange = lambda names, part=0, nparts=1: _chip_exchange_task(names, pair_sums, by_source, part, nparts)
    pair_gather = lambda names: _pair_gather_task(names, halves, sibling_halves)

    def pair_sum(names):
        res = _rs_pair_sum("rs_pair_sum_" + names[0], [big[n] for n in names], [got[n] for n in names], core)
        pair_sums.update(zip(names, res))

    def chip_sum(names):
        res = _rs_chip_sum("rs_chip_sum_" + names[0], [pair_sums[n] for n in names], [by_source[n] for n in names],
                           chip_id)
        halves.update(zip(names, res))

    def adamw(names):
        for n in names:
            res = _adamw_halves("adamw_" + n, local(w[n], n), halves[n], sibling_halves[n], 0, local(mom[n], n),
                                local(var[n], n), core)
            outs[n] = tuple((jnp.swapaxes(r, 0, 1) if n in TRANSPOSED_WEIGHTS else r)[None] for r in res)

    do = lambda fn, names: functools.partial(fn, names)
    ffn2_grads = ["ffn2_w2", "ffn2_w1", "ffn2_w3"]
    xattn_grads = ["w_xo", "w_xq", "w_xk", "w_xv"]
    mix_out_grads = ["w_branch_gate", "w_out", "w_ret_o", "w_lru_o"]
    conv_gather = _gather_chips_task({"conv": shard["conv"]}, False, gw)
    half = lambda key, part: _gather_chips_task({key: shard[key]}, True, landed, part, 2)
    plan = _Plan()
    plan.tasks = {
        "ag_first_chips": [over_chips(["col1", "row2a"])],
        "ag_first_sibling": [to_sibling(["col1", "row2a"])],
        "ffn1_up": [over_chips(["win"])],
        "ffn1_down": [to_sibling(["win"]), over_chips(["wbg"]), conv_gather],
        "mix_in": [to_sibling(["wbg"]), over_chips(["sqA"])],
        "ret_fwd": [to_sibling(["sqA"]), over_chips(["col2a"])],
        "lru_gates_fwd": [to_sibling(["col2a"]), over_chips(["sqB"])],
        "lru_scan_fwd": [to_sibling(["sqB"]), over_chips(["sqC"])],
        "mix_gates": [to_sibling(["sqC"]), half("col2b", 0)],
        "y_lru": [half("col2b", 1)],
        "xattn_fwd": [to_sibling(["col2b"])],
        "ffn2_up": [over_chips(["row2b"])],
        "ffn2_up_sibling": [to_sibling(["row2b"])],
        "ffn2_dh": [pair_swap(ffn2_grads)],
        "xattn_bwd": [exchange(["ffn2_w2"], 0, 2)],
        "d_hq": [exchange(["ffn2_w2"], 1, 2)],
        "d_merged": [exchange(["ffn2_w1"], 0, 2), pair_swap(xattn_grads)],
        "lru_out_bwd": [exchange(["w_xo"])],
        "ret_bwd": [exchange(["ffn2_w1"], 1, 2), exchange(["ffn2_w3"], 0, 2), pair_swap(mix_out_grads)],
        "lru_scan_bwd": [exchange(["ffn2_w3"], 1, 2)],
        "lru_gates_bwd": [exchange(["w_xq", "w_xk"]), pair_gather(ffn2_grads)],
        "dw_in": [exchange(["w_xv", "w_out"])],
        "d_h2": [exchange(["w_branch_gate", "w_ret_o", "w_lru_o"]), pair_swap(["w_in"]), pair_gather(xattn_grads)],
        "ffn1_bwd_mid": [exchange(["w_in"], 0, 2), pair_gather(mix_out_grads)],
        "ffn1_dw2": [exchange(["w_in"], 2, 4)],
        "ffn1_dw1": [exchange(["w_in"], 3, 4), pair_swap(["ffn1_w2"])],
        "ffn1_dw3": [exchange(["ffn1_w2"], 0, 2), pair_swap(["ffn1_w1"]), pair_gather(["w_in"])],
        "ffn1_dh": [exchange(["ffn1_w2"], 1, 2), exchange(["ffn1_w1"]), pair_swap(["ffn1_w3"])],
        "small_allreduce": [exchange(["ffn1_w3"]), pair_gather(["ffn1_w2"])],
        "rs_last_gather": [pair_gather(["ffn1_w1", "ffn1_w3"])],
    }
    plan.after = {
        "ffn2_up": [functools.partial(_comm_call, "ffn2_up_sibling")],
        "ffn2_dh": [do(pair_sum, ffn2_grads)],
        "d_merged": [do(pair_sum, xattn_grads)],
        "ret_bwd": [do(pair_sum, mix_out_grads)],
        "lru_scan_bwd": [do(chip_sum, ffn2_grads)],
        "lru_gates_bwd": [do(adamw, ffn2_grads)],
        "dw_in": [do(chip_sum, xattn_grads)],
        "d_h2": [do(chip_sum, mix_out_grads), do(pair_sum, ["w_in"]), do(adamw, xattn_grads)],
        "ffn1_bwd_mid": [do(adamw, mix_out_grads)],
        "ffn1_dw1": [do(chip_sum, ["w_in"]), do(pair_sum, ["ffn1_w2"])],
        "ffn1_dw3": [do(pair_sum, ["ffn1_w1"]), do(adamw, ["w_in"])],
        "ffn1_dh": [do(pair_sum, ["ffn1_w3"]), do(chip_sum, ["ffn1_w2"])],
        "small_allreduce": [do(chip_sum, ["ffn1_w1", "ffn1_w3"]), functools.partial(_comm_call, "rs_last_gather"),
                    do(adamw, ["ffn1_w2", "ffn1_w1", "ffn1_w3"])],
    }
    global _plan
    _plan = plan
    try:
        _comm_call("ag_first_chips")
        _comm_call("ag_first_sibling")
        loss_part, grad_x, small = _local_step(x[0], mem[0], loss_target[0], gw, sm, big)
        gate2d = lambda a: a.reshape(LRU_BLOCKS * LRU_BLOCK, LRU_BLOCK)
        loss_row = jnp.pad(loss_part, ((0, 0), (0, D - loss_part.shape[1])))
        small_sum, *gate_sums = _small_allreduce([_pack_small(small, loss_row)]
                                                 + [gate2d(small[n]) for n in GATE_WEIGHTS])
    finally:
        _plan = None
    assert not plan.tasks and not plan.after, (list(plan.tasks), list(plan.after))
    loss = small_sum[SMALL_USED_ROWS, 0]

    small_shapes = {n: w[n].shape for n, _ in SMALL_LAYOUT}
    small_shapes["conv_w"] = (CONV_TAPS, D)
    conv_row = SMALL_USED_ROWS - CONV_TAPS
    conv_grad = lax.dynamic_slice(small_sum[conv_row:conv_row + CONV_TAPS], (0, chip * SQ_BLK), (CONV_TAPS, SQ_BLK))
    small_w = {n: w[n] for n, _ in SMALL_LAYOUT}
    small_m = {n: mom[n] for n, _ in SMALL_LAYOUT}
    small_v = {n: var[n] for n, _ in SMALL_LAYOUT}
    pad_cols = lambda a: jnp.pad(a[0], ((0, 0), (0, D - SQ_BLK)))
    for dct in (small_w, small_m, small_v):
        dct["conv_w"] = pad_cols(dct["conv_w"])
    g_pack = lax.dynamic_update_slice(small_sum, jnp.pad(conv_grad, ((0, 0), (0, D - SQ_BLK))), (conv_row, 0))
    d_pack, m_pack, v_pack = _adamw("adamw_small", _pack_small(small_w), g_pack, _pack_small(small_m),
                                    _pack_small(small_v))
    unpacked = [_unpack_small(p, small_shapes) for p in (g_pack, d_pack, m_pack, v_pack)]
    for n, _ in SMALL_LAYOUT:
        if n == "conv_w":
            outs[n] = tuple(u[n][:, :SQ_BLK][None] for u in unpacked)
        else:
            outs[n] = tuple(u[n] for u in unpacked)
    for n, gsum in zip(GATE_WEIGHTS, gate_sums):
        d, nm, nv = _adamw("adamw_" + n, gate2d(w[n]), gsum, gate2d(mom[n]), gate2d(var[n]))
        outs[n] = tuple(r.reshape(w[n].shape) for r in (gsum, d, nm, nv))

    result = [loss, grad_x[None]]
    for k in range(4):
        result += [outs[n][k] for n in WEIGHT_ORDER]
    return tuple(result)
```

```python
import functools
import math

import jax
import jax.numpy as jnp
from jax import lax
from jax.experimental import pallas as pl
from jax.experimental.pallas import tpu as pltpu

F32 = jnp.float32
BF16 = jnp.bfloat16
GRAD_WIRE_DTYPE = BF16
MESH = pl.DeviceIdType.MESH

D = 1024
EPS = 1e-6
RET_HEADS = 4
RET_DK = 128
RET_DV = 256
CHUNK = 128
ROPE_BASE = 10000.0
LRU_BLOCKS = 8
LRU_BLOCK = 128
CONV_TAPS = 4
LRU_C = 8.0
D_FF = 2816
X_HEADS = 4
X_HD = 256
N_CHIPS = 4
FF_BLK = D_FF // N_CHIPS
IN_BLK = 5120 // N_CHIPS
BG_BLK = 2048 // N_CHIPS
SQ_BLK = D // N_CHIPS

ADAM_LR = 0.001
ADAM_B1 = 0.9
ADAM_B2 = 0.999
ADAM_EPS = 1e-08
ADAM_WD = 0.01
ADAM_STEP = 10

VMEM_LIMIT_BYTES = 56 * 1024 * 1024
ROW_TILE = 512
WIDE_ROW_TILE = 1024
FFN_ROW_TILE = 256
DW_BLK = D_FF // 2
SCAN_TILE = 256
RET_STEP_CHUNKS = 2
RET_STEP_ROWS = RET_STEP_CHUNKS * CHUNK

_DN = {
    "nn": (((1,), (0,)), ((), ())),
    "nt": (((1,), (1,)), ((), ())),
    "tn": (((0,), (0,)), ((), ())),
}


def _cparams(n_axes, collective_id=None):
    return pltpu.CompilerParams(dimension_semantics=("arbitrary",) * n_axes,
                                vmem_limit_bytes=VMEM_LIMIT_BYTES, collective_id=collective_id)


def _dot(a, b, kind):
    if b.ndim == 3:
        b = b.reshape(b.shape[0] * b.shape[1], b.shape[2])
    return lax.dot_general(a.astype(BF16), b.astype(BF16), _DN[kind], preferred_element_type=F32)


def _sigmoid(x):
    return 1.0 / (1.0 + jnp.exp(-x))


def _log1p_pos(e):
    u = 1.0 + e
    return jnp.where(u == 1.0, e, jnp.log(u) * (e / jnp.where(u == 1.0, 1.0, u - 1.0)))


def _expm1(x):
    u = jnp.exp(x)
    lu = jnp.log(u)
    safe = jnp.where(lu == 0.0, 1.0, lu)
    return jnp.where(u == 1.0, x, (u - 1.0) * (x / safe))


def _softplus(z):
    return jnp.maximum(z, 0.0) + _log1p_pos(jnp.exp(-jnp.abs(z)))


_GELU_C = math.sqrt(2.0 / math.pi)


def _gelu_and_grad(x):
    x2 = x * x
    t = jnp.tanh(_GELU_C * (x + 0.044715 * x * x2))
    g = 0.5 * x * (1.0 + t)
    dg = 0.5 * (1.0 + t) + 0.5 * x * (1.0 - t * t) * (_GELU_C * (1.0 + 3.0 * 0.044715 * x2))
    return g, dg


def _rms_fwd(x, g):
    r = lax.rsqrt(jnp.mean(x * x, axis=-1, keepdims=True) + EPS)
    return (x * r) * g


def _rms_bwd(x, g, dh):
    r = lax.rsqrt(jnp.mean(x * x, axis=-1, keepdims=True) + EPS)
    n = x * r
    dyg = dh * g
    dx = r * (dyg - n * jnp.mean(dyg * n, axis=-1, keepdims=True))
    return dx, jnp.sum(dh * n, axis=0, keepdims=True)


def _accumulate(ref, val, first):
    @pl.when(first)
    def _():
        ref[...] = val

    @pl.when(jnp.logical_not(first))
    def _():
        ref[...] += val


def _sds(shape, dtype):
    return jax.ShapeDtypeStruct(tuple(shape), dtype)


def _spec(shape, fn):
    return pl.BlockSpec(tuple(shape), fn)


class _Task:
    def __init__(self, peers, operands, out_shapes, aliases, nsem, make, finish, make_second=None):
        self.peers = peers
        self.operands, self.out_shapes, self.aliases = operands, out_shapes, aliases
        self.nsem, self.make, self.finish = nsem, make, finish
        self.make_second = make_second


class _Plan:
    def __init__(self):
        self.tasks, self.after = {}, {}


_plan = None


PEER_SET_COLLECTIVE_ID = {frozenset({"sibling"}): 1, frozenset({"chips"}): 2, frozenset({"sibling", "chips"}): 3,
                          frozenset({"neighbours"}): 4, frozenset({"sibling", "neighbours"}): 5}


def _peer_set(names):
    names = frozenset(names)
    return names - {"neighbours"} if "chips" in names else names


def _entry_handshake(peer_set):
    x, y, c, chips = _mesh_position()
    peers = [(x, y, 1 - c)] if "sibling" in peer_set else []
    if "chips" in peer_set:
        peers += [(*chip, c) for chip in chips]
    if "neighbours" in peer_set:
        peers += [(*chip, c) for chip in chips[:2]]
    barrier = pltpu.get_barrier_semaphore()
    for peer in peers:
        pl.semaphore_signal(barrier, inc=1, device_id=peer, device_id_type=MESH)
    pl.semaphore_wait(barrier, len(peers))


def _pcall(body, *, name, grid, in_specs, out_specs, out_shape, scratch_shapes=(), num_prefetch=0, own_peers=()):
    single = not isinstance(out_shape, (list, tuple))
    out_shape = [out_shape] if single else list(out_shape)
    out_specs = [out_specs] if single else list(out_specs)
    in_specs = list(in_specs)
    scratch_shapes = list(scratch_shapes)
    tasks = _plan.tasks.pop(name, []) if _plan is not None else []
    after = _plan.after.pop(name, []) if _plan is not None else []
    peer_set = _peer_set([t.peers for t in tasks] + list(own_peers))
    nax = len(grid)

    def run(*operands):
        n_in = len(operands) - num_prefetch
        n_out = len(out_shape)
        t_ops = [t.operands() for t in tasks]
        t_outs = [t.out_shapes() for t in tasks]
        c_ops = [a for ops in t_ops for a in ops]
        c_outs = [s for outs in t_outs for s in outs]
        aliases = {}
        i0, o0 = num_prefetch + n_in, n_out
        for t, ops, outs in zip(tasks, t_ops, t_outs):
            for i_loc, o_loc in t.aliases.items():
                aliases[i0 + i_loc] = o0 + o_loc
            i0 += len(ops)
            o0 += len(outs)
        nsem = sum(t.nsem for t in tasks)

        def wrapped(*refs):
            p = num_prefetch
            pre, ins = refs[:p], refs[p:p + n_in]
            cins = refs[p + n_in:p + n_in + len(c_ops)]
            q = p + n_in + len(c_ops)
            outs, couts = refs[q:q + n_out], refs[q + n_out:q + n_out + len(c_outs)]
            q += n_out + len(c_outs)
            scr = refs[q:q + len(scratch_shapes)]

            def rounds(second):
                send_sems, recv_sems = refs[q + len(scratch_shapes):]
                out = []
                ci = co = so = 0
                for t, ops, souts in zip(tasks, t_ops, t_outs):
                    make = t.make_second if second else t.make
                    out.append(([], []) if make is None else
                               make(cins[ci:ci + len(ops)], couts[co:co + len(souts)],
                                    functools.partial(lambda base, k: send_sems.at[base + k], so),
                                    functools.partial(lambda base, k: recv_sems.at[base + k], so)))
                    ci, co, so = ci + len(ops), co + len(souts), so + t.nsem
                return out

            two_rounds = [t.make_second is not None for t in tasks]
            if peer_set:
                ids = [pl.program_id(k) for k in range(nax)]
                first = functools.reduce(jnp.logical_and, [i == 0 for i in ids])
                last = functools.reduce(jnp.logical_and, [i == g - 1 for i, g in zip(ids, grid)])
                step = functools.reduce(lambda acc, ig: acc * ig[1] + ig[0], zip(ids, grid), 0)
                middle = step == math.prod(grid) // 3

                @pl.when(first)
                def _():
                    _entry_handshake(peer_set)
                    for starts, _ in rounds(False):
                        for copy in starts:
                            copy().start()

            body(*pre, *ins, *outs, *scr)

            if any(two_rounds):
                @pl.when(middle)
                def _():
                    for (_, arrivals), two in zip(rounds(False), two_rounds):
                        if two:
                            for arrival in arrivals:
                                arrival().wait_recv()
                    for starts, _ in rounds(True):
                        for copy in starts:
                            copy().start()

            if tasks:
                @pl.when(last)
                def _():
                    first_round, second_round = rounds(False), rounds(True)
                    for (_, arrivals1), (_, arrivals2), two in zip(first_round, second_round, two_rounds):
                        for arrival in (arrivals2 if two else arrivals1):
                            arrival().wait_recv()
                    for starts, _ in first_round + second_round:
                        for copy in starts:
                            copy().wait_send()

        sems = [pltpu.SemaphoreType.DMA((nsem,)), pltpu.SemaphoreType.DMA((nsem,))] if tasks else []
        res = pl.pallas_call(
            wrapped, name=name,
            grid_spec=pltpu.PrefetchScalarGridSpec(
                num_scalar_prefetch=num_prefetch, grid=tuple(grid),
                in_specs=in_specs + [ANY_SPEC] * len(c_ops),
                out_specs=out_specs + [ANY_SPEC] * len(c_outs),
                scratch_shapes=scratch_shapes + sems),
            out_shape=out_shape + c_outs,
            input_output_aliases=aliases,
            compiler_params=_cparams(nax, PEER_SET_COLLECTIVE_ID[peer_set] if peer_set else None),
        )(*operands, *c_ops)
        co = n_out
        for t, souts in zip(tasks, t_outs):
            t.finish(res[co:co + len(souts)])
            co += len(souts)
        for fn in after:
            fn()
        return res[0] if single else list(res[:n_out])

    return run


def _comm_call(name):
    def body(o_ref):
        o_ref[...] = jnp.zeros_like(o_ref)

    _pcall(body, name=name, grid=(1,), in_specs=[], out_specs=_spec((8, 128), lambda i: (0, 0)),
           out_shape=_sds((8, 128), F32))()


def _gemm(name, terms, grid, outs, acc_shape, extras=(), epilogue=None):
    kinds = [t[4] for t in terms]
    nt, ne, no = len(terms), len(extras), len(outs)
    nred = grid[-1]
    nax = len(grid)

    def body(*refs):
        trefs = refs[:2 * nt]
        erefs = refs[2 * nt:2 * nt + ne]
        orefs = refs[2 * nt + ne:2 * nt + ne + no]
        ids = [pl.program_id(k) for k in range(nax)]
        tot = None
        for t in range(nt):
            d = _dot(trefs[2 * t][...], trefs[2 * t + 1][...], kinds[t])
            tot = d if tot is None else tot + d

        def finish(acc):
            if epilogue is None:
                orefs[0][...] = acc.astype(orefs[0].dtype)
            else:
                epilogue(acc, erefs, orefs, ids)

        if nred == 1:
            finish(tot)
        else:
            acc_ref = refs[-1]
            r = ids[-1]

            @pl.when(r == 0)
            def _():
                acc_ref[...] = tot

            @pl.when(r > 0)
            def _():
                acc_ref[...] += tot

            @pl.when(r == nred - 1)
            def _():
                finish(acc_ref[...])

    operands, in_specs = [], []
    for a, a_spec, b, b_spec, _ in terms:
        operands += [a, b]
        in_specs += [a_spec, b_spec]
    for e, e_spec in extras:
        operands.append(e)
        in_specs.append(e_spec)
    scratch = [pltpu.VMEM(tuple(acc_shape), F32)] if nred > 1 else []
    return _pcall(body, name=name, grid=tuple(grid), in_specs=in_specs, out_specs=[o[1] for o in outs],
                  out_shape=[o[0] for o in outs], scratch_shapes=scratch)(*operands)


def _rowwise(name, fn, ins, outs, grid):
    ni = len(ins)
    nax = len(grid)

    def body(*refs):
        ids = [pl.program_id(k) for k in range(nax)]
        fn(refs[:ni], refs[ni:], ids)

    return _pcall(body, name=name, grid=tuple(grid), in_specs=[i[1] for i in ins],
                  out_specs=[o[1] for o in outs], out_shape=[o[0] for o in outs])(*[i[0] for i in ins])


def _ffn_up(name, h, w1buf, w1_idx, w3buf, w3_idx):
    T = h.shape[0]
    tm = min(FFN_ROW_TILE, T)

    def body(h_ref, w1_ref, w3_ref, a_ref, b_ref, s_ref):
        hv = h_ref[...]
        a = _dot(hv, w1_ref[...], "nt")
        b = _dot(hv, w3_ref[...], "nt")
        a_ref[...] = a.astype(BF16)
        b_ref[...] = b.astype(BF16)
        s_ref[...] = ((a * _sigmoid(a)) * b).astype(BF16)

    blk = _spec((tm, D_FF), lambda i: (i, 0))
    return _pcall(
        body, name=name, grid=(T // tm,),
        in_specs=[_spec((tm, D), lambda i: (i, 0)),
                  _spec((N_CHIPS, None, FF_BLK, D), lambda i: (0, w1_idx, 0, 0)),
                  _spec((N_CHIPS, None, FF_BLK, D), lambda i: (0, w3_idx, 0, 0))],
        out_specs=[blk, blk, blk],
        out_shape=[_sds((T, D_FF), BF16)] * 3,
    )(h, w1buf, w3buf)


def _ffn_down(name, s, wrow2, w2_idx, x_res, g_next=None):
    T = x_res.shape[0]
    tm = min(ROW_TILE, T)
    row = lambda i, j, r: (i, 0)

    def epilogue(acc, erefs, orefs, ids):
        xo = erefs[0][...] + 0.5 * acc
        orefs[0][...] = xo
        if g_next is not None:
            orefs[1][...] = _rms_fwd(xo, erefs[1][...]).astype(BF16)

    extras = [(x_res, _spec((tm, D), row))]
    outs = [(_sds((T, D), F32), _spec((tm, D), row))]
    if g_next is not None:
        extras.append((g_next, _spec((1, D), lambda i, j, r: (0, 0))))
        outs.append((_sds((T, D), BF16), _spec((tm, D), row)))
    return _gemm(
        name,
        [(s, _spec((tm, D_FF), row),
          wrow2, _spec((N_CHIPS, None, FF_BLK, D), lambda i, j, r: (0, w2_idx, 0, 0)), "nn")],
        (T // tm, 1, 1), outs, (tm, D), extras, epilogue)


def _ffn_bwd_mid(name, dx, wrow2, w2_idx, a, b):
    T = dx.shape[0]
    tm = min(FFN_ROW_TILE, T)

    def body(dx_ref, w2_ref, a_ref, b_ref, dab_ref):
        ds = _dot(0.5 * dx_ref[...], w2_ref[...], "nt")
        av = a_ref[...].astype(F32)
        sg = _sigmoid(av)
        dab_ref[0] = (ds * b_ref[...].astype(F32) * (sg * (1.0 + av * (1.0 - sg)))).astype(BF16)
        dab_ref[1] = (ds * (av * sg)).astype(BF16)

    blk = _spec((tm, D_FF), lambda i: (i, 0))
    return _pcall(
        body, name=name, grid=(T // tm,),
        in_specs=[_spec((tm, D), lambda i: (i, 0)),
                  _spec((N_CHIPS, None, FF_BLK, D), lambda i: (0, w2_idx, 0, 0)),
                  blk, blk],
        out_specs=_spec((2, tm, D_FF), lambda i: (0, i, 0)),
        out_shape=_sds((2, T, D_FF), BF16),
    )(dx, wrow2, a, b)


def _rms_bwd_epilogue(acc, erefs, orefs, ids):
    dx, dgp = _rms_bwd(erefs[0][...], erefs[1][...], acc)
    orefs[0][...] = dx + erefs[2][...]
    _accumulate(orefs[1], dgp, ids[0] == 0)


def _rms_bwd_io(x, g, dres, T, tm):
    row = lambda i, j, r: (i, 0)
    vec = lambda i, j, r: (0, 0)
    extras = [(x, _spec((tm, D), row)), (g, _spec((1, D), vec)), (dres, _spec((tm, D), row))]
    outs = [(_sds((T, D), F32), _spec((tm, D), row)), (_sds((1, D), F32), _spec((1, D), vec))]
    return extras, outs


def _ffn_bwd(tag, dx_out, h, a, b, s, w1buf, w1_idx, w3buf, w3_idx, wrow2, w2_idx, x_in, g, big):
    T = dx_out.shape[0]
    dab = _ffn_bwd_mid(tag + "_bwd_mid", dx_out, wrow2, w2_idx, a, b)

    def half_scale(acc, erefs, orefs, ids):
        orefs[0][...] = (0.5 * acc).astype(orefs[0].dtype)

    dw_grid = (D_FF // DW_BLK, 1, 1)
    dw_out = [(_sds((D_FF, D), GRAD_WIRE_DTYPE), _spec((DW_BLK, D), lambda j, n, r: (j, 0)))]
    tokens = _spec((T, D), lambda j, n, r: (0, 0))
    big[tag + "_w2"] = _gemm(
        tag + "_dw2", [(s, _spec((T, DW_BLK), lambda j, n, r: (0, j)), dx_out, tokens, "tn")],
        dw_grid, dw_out, (DW_BLK, D), (), half_scale)[0].reshape(1, N_CHIPS, FF_BLK, D)
    for widx, wname in ((0, "_w1"), (1, "_w3")):
        big[tag + wname] = _gemm(
            tag + "_d" + wname[1:],
            [(dab, _spec((None, T, DW_BLK), functools.partial(lambda w, j, n, r: (w, 0, j), widx)), h, tokens, "tn")],
            dw_grid, dw_out, (DW_BLK, D))[0].reshape(1, N_CHIPS, FF_BLK, D)
    tm = min(FFN_ROW_TILE, T)
    extras, outs = _rms_bwd_io(x_in, g, dx_out, T, tm)
    whole = lambda idx: _spec((N_CHIPS, None, FF_BLK, D), lambda i, j, r: (0, idx, 0, 0))
    dx_in, dg = _gemm(
        tag + "_dh",
        [(dab, _spec((None, tm, D_FF), lambda i, j, r: (0, i, 0)), w1buf, whole(w1_idx), "nn"),
         (dab, _spec((None, tm, D_FF), lambda i, j, r: (1, i, 0)), w3buf, whole(w3_idx), "nn")],
        (T // tm, 1, 1), outs, (tm, D), extras, _rms_bwd_epilogue)
    return dx_in, dg


def _proj_sq(name, a, wsq, idx, kind, out_dtype=F32, extras=(), epilogue=None, outs=None):
    M = a.shape[0]
    tm = min(ROW_TILE, M)
    if outs is None:
        outs = [(_sds((M, D), out_dtype), _spec((tm, D), lambda i, j, r: (i, 0)))]
    return _gemm(
        name,
        [(a, _spec((tm, D), lambda i, j, r: (i, 0)),
          wsq, _spec((N_CHIPS, None, SQ_BLK, D), lambda i, j, r: (0, idx, 0, 0)), kind)],
        (M // tm, 1, 1), outs, (tm, D), extras, epilogue)


def _dw_sq(name, a, b):
    M = a.shape[0]
    tn = D // 2
    whole = _gemm(
        name,
        [(a, _spec((M, D), lambda i, j, r: (0, 0)), b, _spec((M, tn), lambda i, j, r: (0, j)), "tn")],
        (1, D // tn, 1),
        [(_sds((D, D), GRAD_WIRE_DTYPE), _spec((D, tn), lambda i, j, r: (0, j)))],
        (D, tn))[0]
    return whole.reshape(N_CHIPS, SQ_BLK, D)


def _retention_constants(T):
    pos = jnp.arange(T, dtype=F32)
    inv_freq = ROPE_BASE ** (-jnp.arange(0, RET_DK, 2, dtype=F32) / RET_DK)
    ang = pos[:, None] * inv_freq[None, :]
    cosf = jnp.concatenate([jnp.cos(ang), jnp.cos(ang)], axis=1)
    sins = jnp.concatenate([-jnp.sin(ang), jnp.sin(ang)], axis=1)
    lg = jnp.log(1.0 - 2.0 ** (-5.0 - jnp.arange(RET_HEADS, dtype=F32)))
    p = jnp.arange(CHUNK, dtype=F32)
    rel = p[:, None] - p[None, :]
    dmat = jnp.where(rel[None] >= 0, jnp.exp(rel[None] * lg[:, None, None]), 0.0)
    kd = jnp.exp((CHUNK - 1.0 - p)[None, :] * lg[:, None])[:, :, None]
    qd = jnp.exp((p + 1.0)[None, :] * lg[:, None])[:, :, None]
    cd = jnp.exp(CHUNK * lg)[:, None, None]
    return cosf, sins, dmat, kd, qd, cd


def _rot(t, cosv, sinv):
    return t * cosv + pltpu.roll(t, RET_DK // 2, 1) * sinv


def _unrot(t, cosv, sinv):
    return t * cosv - pltpu.roll(t, RET_DK // 2, 1) * sinv


def _ret_const_specs(cm):
    whole = lambda shape: _spec(shape, lambda c: (0,) * len(shape))
    return [
        _spec((RET_STEP_ROWS, RET_DK), lambda c: (cm(c), 0)),
        _spec((RET_STEP_ROWS, RET_DK), lambda c: (cm(c), 0)),
        whole((RET_HEADS, CHUNK, CHUNK)), whole((RET_HEADS, CHUNK, 1)), whole((RET_HEADS, CHUNK, 1)),
        whole((RET_HEADS, 1, 1)),
    ]


def _head(h, width):
    return slice(h * width, (h + 1) * width)


def _ret_fwd(u, consts, ret_gn):
    T = u.shape[0]
    nC = T // CHUNK
    kscale = RET_DK ** -0.5

    def body(q_ref, k_ref, v_ref, g_ref, cos_ref, sin_ref, dm_ref, kd_ref, qd_ref, cd_ref, gn_ref,
             qr_ref, kr_ref, ret_ref, yr_ref, st_ref, state):
        @pl.when(pl.program_id(0) == 0)
        def _():
            state[...] = jnp.zeros_like(state)

        for cc in range(RET_STEP_CHUNKS):
            rows = slice(cc * CHUNK, (cc + 1) * CHUNK)
            cosv, sinv = cos_ref[rows, :], sin_ref[rows, :]
            for h in range(RET_HEADS):
                hk, hv = _head(h, RET_DK), _head(h, RET_DV)
                q = _rot(q_ref[rows, hk], cosv, sinv)
                k = _rot(k_ref[rows, hk], cosv, sinv) * kscale
                v = v_ref[rows, hv]
                qr_ref[rows, hk] = q
                kr_ref[rows, hk] = k
                prev = state[h]
                st_ref[h, cc] = prev
                s = _dot(q, k, "nt") * dm_ref[h]
                ret = _dot(s, v, "nn") + _dot(q, prev, "nn") * qd_ref[h]
                state[h] = cd_ref[h] * prev + _dot(k * kd_ref[h], v, "tn")
                ret_ref[rows, hv] = ret
                mu = jnp.mean(ret, axis=-1, keepdims=True)
                xc = ret - mu
                yn = xc * lax.rsqrt(jnp.mean(xc * xc, axis=-1, keepdims=True) + EPS)
                g = g_ref[rows, hv]
                yr_ref[rows, hv] = ((g * _sigmoid(g)) * (yn * gn_ref[:, hv])).astype(BF16)

    cm = lambda c: c
    qk_w, v_w = RET_HEADS * RET_DK, RET_HEADS * RET_DV
    in_specs = [
        _spec((RET_STEP_ROWS, qk_w), lambda c: (c, 0)), _spec((RET_STEP_ROWS, qk_w), lambda c: (c, 1)),
        _spec((RET_STEP_ROWS, v_w), lambda c: (c, 1)), _spec((RET_STEP_ROWS, v_w), lambda c: (c, 2)),
    ] + _ret_const_specs(cm) + [_spec((1, v_w), lambda c: (0, 0))]
    qk_out = _spec((RET_STEP_ROWS, qk_w), lambda c: (c, 0))
    v_out = _spec((RET_STEP_ROWS, v_w), lambda c: (c, 0))
    return _pcall(
        body, name="ret_fwd", grid=(nC // RET_STEP_CHUNKS,),
        in_specs=in_specs,
        out_specs=[qk_out, qk_out, v_out, v_out,
                   _spec((RET_HEADS, RET_STEP_CHUNKS, RET_DK, RET_DV), lambda c: (0, c, 0, 0))],
        out_shape=[_sds((T, qk_w), F32), _sds((T, qk_w), F32), _sds((T, v_w), F32), _sds((T, v_w), BF16),
                   _sds((RET_HEADS, nC, RET_DK, RET_DV), F32)],
        scratch_shapes=[pltpu.VMEM((RET_HEADS, RET_DK, RET_DV), F32)],
    )(u, u, u, u, *consts, ret_gn)


def _ret_bwd(dyr, ret, u, qr, kr, states, consts, ret_gn):
    T = u.shape[0]
    nC = T // CHUNK
    kscale = RET_DK ** -0.5

    def body(dyr_ref, ret_ref, g_ref, q_ref, k_ref, v_ref, st_ref,
             cos_ref, sin_ref, dm_ref, kd_ref, qd_ref, cd_ref, gn_ref,
             dq_ref, dk_ref, dv_ref, dg_ref, dgn_ref, gstate):
        first = pl.program_id(0) == 0

        @pl.when(first)
        def _():
            gstate[...] = jnp.zeros_like(gstate)

        dgn_total = None
        for cc in reversed(range(RET_STEP_CHUNKS)):
            rows = slice(cc * CHUNK, (cc + 1) * CHUNK)
            cosv, sinv = cos_ref[rows, :], sin_ref[rows, :]
            dgn_parts = []
            for h in range(RET_HEADS):
                hk, hv = _head(h, RET_DK), _head(h, RET_DV)
                ret = ret_ref[rows, hv]
                mu = jnp.mean(ret, axis=-1, keepdims=True)
                xc = ret - mu
                rs = lax.rsqrt(jnp.mean(xc * xc, axis=-1, keepdims=True) + EPS)
                yn = xc * rs
                gn = gn_ref[:, hv]
                g = g_ref[rows, hv]
                sg = _sigmoid(g)
                dyr_v = dyr_ref[rows, hv]
                dretn = dyr_v * (g * sg)
                dg_ref[rows, hv] = (dyr_v * (yn * gn) * (sg * (1.0 + g * (1.0 - sg)))).astype(BF16)
                dgn_parts.append(jnp.sum(dretn * yn, axis=0, keepdims=True))
                dyn = dretn * gn
                d_o = rs * (dyn - jnp.mean(dyn, axis=-1, keepdims=True)
                            - yn * jnp.mean(dyn * yn, axis=-1, keepdims=True))

                q, k, v = q_ref[rows, hk], k_ref[rows, hk], v_ref[rows, hv]
                dmat, kd, qd = dm_ref[h], kd_ref[h], qd_ref[h]
                prev = st_ref[h, cc]
                gnext = gstate[h]
                s = _dot(q, k, "nt") * dmat
                ds = _dot(d_o, v, "nt") * dmat
                doq = d_o * qd
                dq = _dot(ds, k, "nn") + _dot(doq, prev, "nt")
                dk = _dot(ds, q, "tn") + _dot(v, gnext, "nt") * kd
                dv = _dot(s, d_o, "tn") + _dot(k * kd, gnext, "nn")
                gstate[h] = cd_ref[h] * gnext + _dot(q, doq, "tn")
                dq_ref[rows, hk] = _unrot(dq, cosv, sinv).astype(BF16)
                dk_ref[rows, hk] = _unrot(dk * kscale, cosv, sinv).astype(BF16)
                dv_ref[rows, hv] = dv.astype(BF16)
            dgn = jnp.concatenate(dgn_parts, axis=1)
            dgn_total = dgn if dgn_total is None else dgn_total + dgn
        _accumulate(dgn_ref, dgn_total, first)

    n_steps = nC // RET_STEP_CHUNKS
    cm = lambda c: n_steps - 1 - c
    qk_w, v_w = RET_HEADS * RET_DK, RET_HEADS * RET_DV
    vspec = lambda blk: _spec((RET_STEP_ROWS, v_w), lambda c: (cm(c), blk))
    qspec = _spec((RET_STEP_ROWS, qk_w), lambda c: (cm(c), 0))
    in_specs = [vspec(0), vspec(0), vspec(2), qspec, qspec, vspec(1),
                _spec((RET_HEADS, RET_STEP_CHUNKS, RET_DK, RET_DV), lambda c: (0, cm(c), 0, 0)),
                ] + _ret_const_specs(cm) + [_spec((1, v_w), lambda c: (0, 0))]
    return _pcall(
        body, name="ret_bwd", grid=(n_steps,),
        in_specs=in_specs,
        out_specs=[qspec, qspec, vspec(0), vspec(0), _spec((1, v_w), lambda c: (0, 0))],
        out_shape=[_sds((T, qk_w), BF16), _sds((T, qk_w), BF16), _sds((T, v_w), BF16), _sds((T, v_w), BF16),
                   _sds((1, v_w), F32)],
        scratch_shapes=[pltpu.VMEM((RET_HEADS, RET_DK, RET_DV), F32)],
    )(dyr, ret, u, qr, kr, u, states, *consts, ret_gn)


def _shift_down(x, s):
    rows = lax.broadcasted_iota(jnp.int32, x.shape, 0)
    return jnp.where(rows >= s, pltpu.roll(x, s, 0), 0.0)


def _shift_up(x, s):
    n = x.shape[0]
    rows = lax.broadcasted_iota(jnp.int32, x.shape, 0)
    return jnp.where(rows < n - s, pltpu.roll(x, n - s, 0), 0.0)


def _lru_specs(T):
    col = lambda off: _spec((T, LRU_BLOCK), lambda g: (0, off + g))
    vec = _spec((1, LRU_BLOCK), lambda g: (0, g))
    wblk = _spec((None, LRU_BLOCK, LRU_BLOCK), lambda g: (g, 0, 0))
    cw = _spec((CONV_TAPS, LRU_BLOCK), lambda g: (0, g))
    return col, vec, wblk, cw


def _lru_gates_fwd(u, conv_w, conv_b, w_r, b_r, w_i, b_i, lam):
    T = u.shape[0]
    col, vec, wblk, cw = _lru_specs(T)

    def body(x_ref, cw_ref, cb_ref, wr_ref, br_ref, wi_ref, bi_ref, lam_ref,
             xc_ref, r_ref, i_ref, a_ref, bx_ref):
        x = x_ref[...]
        w = cw_ref[...]
        xc = (_shift_down(x, 3) * w[0:1] + _shift_down(x, 2) * w[1:2] + _shift_down(x, 1) * w[2:3]
              + x * w[3:4] + cb_ref[...])
        r = _sigmoid(_dot(xc, wr_ref[...], "nn") + br_ref[...])
        i = _sigmoid(_dot(xc, wi_ref[...], "nn") + bi_ref[...])
        la = (-LRU_C) * r * _softplus(-lam_ref[...])
        xc_ref[...] = xc
        r_ref[...] = r
        i_ref[...] = i
        a_ref[...] = jnp.exp(la)
        bx_ref[...] = jnp.sqrt(-_expm1(2.0 * la)) * (i * xc)

    out = col(0)
    return _pcall(
        body, name="lru_gates_fwd", grid=(LRU_BLOCKS,),
        in_specs=[col(24), cw, vec, wblk, vec, wblk, vec, vec],
        out_specs=[out] * 5,
        out_shape=[_sds((T, D), F32)] * 5,
    )(u, conv_w, conv_b, w_r, b_r, w_i, b_i, lam)


def _lru_scan(name, a3, b3, reverse):
    T = a3.shape[0]
    nt = T // SCAN_TILE
    unroll = 8

    def body(a_ref, b_ref, o_ref, carry):
        @pl.when(pl.program_id(0) == 0)
        def _():
            carry[...] = jnp.zeros_like(carry)

        if not reverse:
            def step(t, h):
                h = a_ref[t] * h + b_ref[t]
                o_ref[t] = h
                return h
        else:
            def step(k, c):
                t = SCAN_TILE - 1 - k
                l = b_ref[t] + c
                o_ref[t] = l
                return a_ref[t] * l
        carry[...] = lax.fori_loop(0, SCAN_TILE, step, carry[...], unroll=unroll)

    idx = (lambda i: (nt - 1 - i, 0, 0)) if reverse else (lambda i: (i, 0, 0))
    blk = _spec((SCAN_TILE, LRU_BLOCKS, LRU_BLOCK), idx)
    return _pcall(
        body, name=name, grid=(nt,),
        in_specs=[blk, blk], out_specs=blk,
        out_shape=_sds((T, LRU_BLOCKS, LRU_BLOCK), F32),
        scratch_shapes=[pltpu.VMEM((LRU_BLOCKS, LRU_BLOCK), F32)],
    )(a3, b3)


def _lru_gates_bwd(lmb, hl, a, r, i, xc, u, conv_w, w_r, w_i, lam):
    T = u.shape[0]
    col, vec, wblk, cw = _lru_specs(T)

    def body(l_ref, h_ref, a_ref, r_ref, i_ref, xc_ref, x_ref, cw_ref, wr_ref, wi_ref, lam_ref,
             dx_ref, dwr_ref, dwi_ref, dvec_ref, dcw_ref):
        l = l_ref[...]
        av, rv, iv, xc = a_ref[...], r_ref[...], i_ref[...], xc_ref[...]
        lam_v = lam_ref[...]
        sp = _softplus(-lam_v)
        la = (-LRU_C) * rv * sp
        mult = jnp.sqrt(-_expm1(2.0 * la))
        da = l * _shift_down(h_ref[...], 1)
        dmult = l * (iv * xc)
        di = l * mult * xc
        dxc = l * mult * iv
        dla = da * av - dmult * (av * av) / mult
        dzr = (dla * ((-LRU_C) * sp)) * rv * (1.0 - rv)
        dzi = di * iv * (1.0 - iv)
        dsp = jnp.sum(dla * ((-LRU_C) * rv), axis=0, keepdims=True)
        dlam = dsp * (-_sigmoid(-lam_v))
        dwr_ref[...] = _dot(xc, dzr, "tn")
        dwi_ref[...] = _dot(xc, dzi, "tn")
        dxc = dxc + _dot(dzr, wr_ref[...], "nt") + _dot(dzi, wi_ref[...], "nt")
        x = x_ref[...]
        w = cw_ref[...]
        dx = (dxc * w[3:4] + _shift_up(dxc, 1) * w[2:3] + _shift_up(dxc, 2) * w[1:2]
              + _shift_up(dxc, 3) * w[0:1])
        dx_ref[...] = dx.astype(BF16)
        dvec_ref[...] = jnp.concatenate(
            [jnp.sum(dzr, axis=0, keepdims=True), jnp.sum(dzi, axis=0, keepdims=True), dlam,
             jnp.sum(dxc, axis=0, keepdims=True)], axis=0)
        dcw_ref[...] = jnp.concatenate(
            [jnp.sum(dxc * _shift_down(x, 3 - tap), axis=0, keepdims=True) if tap < 3
             else jnp.sum(dxc * x, axis=0, keepdims=True) for tap in range(CONV_TAPS)], axis=0)

    c0 = col(0)
    return _pcall(
        body, name="lru_gates_bwd", grid=(LRU_BLOCKS,),
        in_specs=[c0, c0, c0, c0, c0, c0, col(24), cw, wblk, wblk, vec],
        out_specs=[c0, wblk, wblk, cw, cw],
        out_shape=[_sds((T, D), BF16), _sds((LRU_BLOCKS, LRU_BLOCK, LRU_BLOCK), F32),
                   _sds((LRU_BLOCKS, LRU_BLOCK, LRU_BLOCK), F32), _sds((4, D), F32), _sds((CONV_TAPS, D), F32)],
    )(lmb, hl, a, r, i, xc, u, conv_w, w_r, w_i, lam)


def _xattn_probs(q, k):
    sc = _dot(q, k, "nt") * (X_HD ** -0.5)
    e = jnp.exp(sc - jnp.max(sc, axis=-1, keepdims=True))
    return e / jnp.sum(e, axis=-1, keepdims=True)


def _xattn_fwd(xq, xk, xv):
    T = xq.shape[0]
    tq = min(WIDE_ROW_TILE, T)
    M = xk.shape[0]

    def body(q_ref, k_ref, v_ref, o_ref):
        p = _xattn_probs(q_ref[...], k_ref[...])
        o_ref[...] = _dot(p, v_ref[...], "nn").astype(BF16)

    qs = _spec((tq, X_HD), lambda h, i: (i, h))
    kv = _spec((M, X_HD), lambda h, i: (0, h))
    return _pcall(
        body, name="xattn_fwd", grid=(X_HEADS, T // tq),
        in_specs=[qs, kv, kv], out_specs=qs, out_shape=_sds((T, D), BF16),
    )(xq, xk, xv)


def _xattn_bwd(xq, xk, xv, dxo):
    T = xq.shape[0]
    tq = min(WIDE_ROW_TILE, T)
    M = xk.shape[0]

    def body(q_ref, k_ref, v_ref, do_ref, dq_ref, dk_ref, dv_ref):
        first = pl.program_id(1) == 0
        q, k, v, do = q_ref[...], k_ref[...], v_ref[...], do_ref[...]
        p = _xattn_probs(q, k)
        dp = _dot(do, v, "nt")
        ds = p * (dp - jnp.sum(dp * p, axis=-1, keepdims=True)) * (X_HD ** -0.5)
        dq_ref[...] = _dot(ds, k, "nn").astype(BF16)
        _accumulate(dk_ref, _dot(ds, q, "tn"), first)
        _accumulate(dv_ref, _dot(p, do, "tn"), first)

    qs = _spec((tq, X_HD), lambda h, i: (i, h))
    kv = _spec((M, X_HD), lambda h, i: (0, h))
    return _pcall(
        body, name="xattn_bwd", grid=(X_HEADS, T // tq),
        in_specs=[qs, kv, kv, qs], out_specs=[qs, kv, kv],
        out_shape=[_sds((T, D), BF16), _sds((M, D), F32), _sds((M, D), F32)],
    )(xq, xk, xv, dxo)


def _final_loss(x, g, tgt):
    T = x.shape[0]
    tm = ROW_TILE

    def fn(irefs, orefs, ids):
        xv, gv = irefs[0][...], irefs[1][...]
        err = _rms_fwd(xv, gv) - irefs[2][...]
        lp = 0.5 * jnp.sum(jnp.mean(err * err, axis=-1, keepdims=True), axis=0, keepdims=True)
        first = ids[0] == 0
        _accumulate(orefs[0], jnp.broadcast_to(lp, (1, 128)), first)
        dx, dgp = _rms_bwd(xv, gv, err * (1.0 / D))
        orefs[1][...] = dx
        _accumulate(orefs[2], dgp, first)

    row = _spec((tm, D), lambda i: (i, 0))
    vec = _spec((1, D), lambda i: (0, 0))
    return _rowwise(
        "final_loss", fn, [(x, row), (g, vec), (tgt, row)],
        [(_sds((1, 128), F32), _spec((1, 128), lambda i: (0, 0))), (_sds((T, D), F32), row),
         (_sds((1, D), F32), vec)],
        (T // tm,))


def _adamw(name, w, g, m, v):
    R, C = w.shape
    tr = R
    for cand in (512, 352, 256):
        if R % cand == 0:
            tr = cand
            break

    def fn(irefs, orefs, ids):
        delta, mn, vn = _adamw_update(*(r[...] for r in irefs))
        orefs[0][...] = delta
        orefs[1][...] = mn
        orefs[2][...] = vn

    blk = _spec((tr, C), lambda i: (i, 0))
    return _rowwise(name, fn, [(w, blk), (g, blk), (m, blk), (v, blk)],
                    [(_sds((R, C), F32), blk)] * 3, (R // tr,))


def _adamw_update(wv, gv, mv, vv):
    c1 = 1.0 - ADAM_B1 ** ADAM_STEP
    c2 = 1.0 - ADAM_B2 ** ADAM_STEP
    mn = ADAM_B1 * mv + (1.0 - ADAM_B1) * gv
    vn = ADAM_B2 * vv + (1.0 - ADAM_B2) * (gv * gv)
    delta = -ADAM_LR * ((mn / c1) / (jnp.sqrt(vn / c2) + ADAM_EPS) + ADAM_WD * wv)
    return delta, mn, vn


def _adamw_halves(name, w, mine, theirs, widx, m, v, core):
    R, C = w.shape
    H = R // 2
    tr = H
    while tr * C * 4 > (1 << 20) and tr % 16 == 0:
        tr //= 2
    nb = H // tr

    def body(core_ref, w_ref, mine_ref, theirs_ref, m_ref, v_ref, g_out, d_out, m_out, v_out):
        gv = jnp.where(pl.program_id(0) == core_ref[0], mine_ref[...], theirs_ref[...])
        delta, mn, vn = _adamw_update(w_ref[...], gv, m_ref[...], v_ref[...])
        g_out[...] = gv
        d_out[...] = delta
        m_out[...] = mn
        v_out[...] = vn

    full = pl.BlockSpec((tr, C), lambda h, i, core_ref: (h * nb + i, 0))
    mine_spec = pl.BlockSpec((None, tr, C), lambda h, i, core_ref: (widx, jnp.where(h == core_ref[0], i, 0), 0))
    theirs_spec = pl.BlockSpec((None, tr, C), lambda h, i, core_ref: (widx, jnp.where(h == core_ref[0], 0, i), 0))
    return _pcall(
        body, name=name, grid=(2, nb), num_prefetch=1,
        in_specs=[full, mine_spec, theirs_spec, full, full], out_specs=[full] * 4,
        out_shape=[_sds((R, C), F32)] * 4,
    )(core, w, mine, theirs, m, v)


def _rmsnorm(name, x, g):
    M = x.shape[0]
    tm = min(ROW_TILE, M)

    def fn(irefs, orefs, ids):
        orefs[0][...] = _rms_fwd(irefs[0][...], irefs[1][...]).astype(BF16)

    row = _spec((tm, D), lambda i: (i, 0))
    return _rowwise(name, fn, [(x, row), (g, _spec((1, D), lambda i: (0, 0)))],
                    [(_sds((M, D), BF16), row)], (M // tm,))[0]


WEIGHT_AT = {
    "ffn1_w1": ("col1", 0), "ffn1_w3": ("col1", 1), "ffn1_w2": ("row2a", 0),
    "w_ret_o": ("sqA", 0), "w_lru_o": ("sqA", 1), "w_out": ("sqA", 2),
    "w_xq": ("sqB", 0), "w_xk": ("sqB", 1), "w_xv": ("sqC", 0), "w_xo": ("sqC", 1),
    "ffn2_w1": ("col2a", 0), "ffn2_w3": ("col2b", 0), "ffn2_w2": ("row2b", 0),
}


def _local_step(x, mem, tgt, gw, sm, big):
    T = x.shape[0]
    tm = ROW_TILE

    def wt(name):
        key, idx = WEIGHT_AT[name]
        return gw[key], idx

    row3 = lambda i, j, r: (i, 0)
    vec3 = lambda i, j, r: (0, 0)
    rowD = _spec((tm, D), row3)
    vecD = _spec((1, D), vec3)

    def residual_norm(acc, erefs, orefs, ids):
        xo = erefs[0][...] + acc
        orefs[0][...] = xo
        orefs[1][...] = _rms_fwd(xo, erefs[1][...]).astype(BF16)

    def res_norm_io(x_res, g):
        return ([(x_res, rowD), (g, vecD)],
                [(_sds((T, D), F32), rowD), (_sds((T, D), BF16), rowD)])

    h1 = _rmsnorm("ffn1_norm", x, sm["ffn1_norm"])
    a1, b1, s1 = _ffn_up("ffn1_up", h1, *wt("ffn1_w1"), *wt("ffn1_w3"))
    x1, h2 = _ffn_down("ffn1_down", s1, *wt("ffn1_w2"), x, sm["mix_norm"])

    tw = min(WIDE_ROW_TILE, T)
    wideD = _spec((tw, D), row3)
    u = _gemm(
        "mix_in",
        [(h2, wideD, gw["win"], _spec((None, None, IN_BLK, D), lambda i, j, r: (j, 0, 0, 0)), "nt")],
        (T // tw, N_CHIPS, 1),
        [(_sds((T, 5120), F32), _spec((tw, IN_BLK), lambda i, j, r: (i, j)))], (tw, IN_BLK))[0]

    consts = _retention_constants(T)
    qr, kr, ret, yr, states = _ret_fwd(u, consts, sm["ret_gn"])

    conv_w = gw["conv"][:, 0].transpose(1, 0, 2).reshape(CONV_TAPS, D)
    xc, rg, ig, av, bx = _lru_gates_fwd(u, conv_w, sm["conv_b"], sm["w_rgate"], sm["b_rgate"],
                                        sm["w_igate"], sm["b_igate"], sm["lru_lambda"])
    a3 = av.reshape(T, LRU_BLOCKS, LRU_BLOCK)
    hl = _lru_scan("lru_scan_fwd", a3, bx.reshape(T, LRU_BLOCKS, LRU_BLOCK), False).reshape(T, D)

    row1 = _spec((tm, D), lambda i: (i, 0))
    glru1 = _spec((tm, D), lambda i: (i, 4))

    def lru_out(irefs, orefs, ids):
        gl, _ = _gelu_and_grad(irefs[1][...])
        orefs[0][...] = (irefs[0][...] * gl).astype(BF16)

    yl = _rowwise("lru_out", lru_out, [(hl, row1), (u, glru1)], [(_sds((T, D), BF16), row1)], (T // tm,))[0]

    def gate_epilogue(acc, erefs, orefs, ids):
        orefs[0][...] = _sigmoid(acc + erefs[0][...])

    gates = _gemm(
        "mix_gates",
        [(h2, wideD, gw["wbg"], _spec((None, None, BG_BLK, D), lambda i, j, r: (j, 0, 0, 0)), "nt")],
        (T // tw, N_CHIPS, 1),
        [(_sds((T, 2 * D), F32), _spec((tw, BG_BLK), lambda i, j, r: (i, j)))], (tw, BG_BLK),
        [(sm["b_branch_gate"], _spec((1, BG_BLK), lambda i, j, r: (0, j)))], gate_epilogue)[0]

    y_ret = _proj_sq("y_ret", yr, *wt("w_ret_o"), "nn")[0]

    def merge_epilogue(acc, erefs, orefs, ids):
        orefs[0][...] = acc
        orefs[1][...] = (erefs[0][...] * erefs[2][...] + erefs[1][...] * acc).astype(BF16)

    y_lru, merged = _proj_sq(
        "y_lru", yl, *wt("w_lru_o"), "nn",
        extras=[(gates, _spec((tm, D), lambda i, j, r: (i, 0))), (gates, _spec((tm, D), lambda i, j, r: (i, 1))),
                (y_ret, rowD)],
        epilogue=merge_epilogue,
        outs=[(_sds((T, D), F32), rowD), (_sds((T, D), BF16), rowD)])

    ex, ou = res_norm_io(x1, sm["xattn_norm"])
    x2, hq = _proj_sq("mix_out", merged, *wt("w_out"), "nn", extras=ex, epilogue=residual_norm, outs=ou)

    m = _rmsnorm("mem_norm", mem, sm["mem_norm"])
    xq = _proj_sq("xq", hq, *wt("w_xq"), "nn", BF16)[0]
    xk = _proj_sq("xk", m, *wt("w_xk"), "nn", BF16)[0]
    xv = _proj_sq("xv", m, *wt("w_xv"), "nn", BF16)[0]
    xo = _xattn_fwd(xq, xk, xv)
    ex, ou = res_norm_io(x2, sm["ffn2_norm"])
    x3, h3 = _proj_sq("xattn_out", xo, *wt("w_xo"), "nn", extras=ex, epilogue=residual_norm, outs=ou)

    a2, b2, s2 = _ffn_up("ffn2_up", h3, *wt("ffn2_w1"), *wt("ffn2_w3"))
    x4 = _ffn_down("ffn2_down", s2, *wt("ffn2_w2"), x3)[0]
    loss, dx4, dg_final = _final_loss(x4, sm["final_norm"], tgt)

    dx3, dg_ffn2 = _ffn_bwd("ffn2", dx4, h3, a2, b2, s2, *wt("ffn2_w1"), *wt("ffn2_w3"),
                            *wt("ffn2_w2"), x3, sm["ffn2_norm"], big)

    dxo = _proj_sq("d_xo", dx3, *wt("w_xo"), "nt", BF16)[0]
    big["w_xo"] = _dw_sq("dw_xo", xo, dx3)[None]
    dxq, dxk, dxv = _xattn_bwd(xq, xk, xv, dxo)
    big["w_xq"] = _dw_sq("dw_xq", hq, dxq)[None]
    ex, ou = _rms_bwd_io(x2, sm["xattn_norm"], dx3, T, tm)
    dx2, dg_xattn = _proj_sq("d_hq", dxq, *wt("w_xq"), "nt", extras=ex, epilogue=_rms_bwd_epilogue, outs=ou)
    big["w_xk"] = _dw_sq("dw_xk", m, dxk)[None]
    big["w_xv"] = _dw_sq("dw_xv", m, dxv)[None]

    M = mem.shape[0]

    def mem_norm_epilogue(acc, erefs, orefs, ids):
        _, dgp = _rms_bwd(erefs[0][...], erefs[1][...], acc)
        orefs[0][...] = dgp

    wsq_spec = lambda idx: _spec((N_CHIPS, None, SQ_BLK, D), lambda i, j, r: (0, idx, 0, 0))
    memD = _spec((M, D), row3)
    dg_mem = _gemm(
        "d_mem_norm",
        [(dxk, memD, wt("w_xk")[0], wsq_spec(wt("w_xk")[1]), "nt"),
         (dxv, memD, wt("w_xv")[0], wsq_spec(wt("w_xv")[1]), "nt")],
        (1, 1, 1), [(_sds((1, D), F32), vecD)], (M, D),
        [(mem, memD), (sm["mem_norm"], vecD)], mem_norm_epilogue)[0]

    def merged_bwd_epilogue(acc, erefs, orefs, ids):
        gr, gl, yrv, ylv = (e[...] for e in erefs)
        orefs[0][...] = (acc * gr).astype(BF16)
        orefs[1][...] = (acc * gl).astype(BF16)
        dgr = acc * yrv * gr * (1.0 - gr)
        dgl = acc * ylv * gl * (1.0 - gl)
        orefs[2][:, :D] = dgr.astype(BF16)
        orefs[2][:, D:] = dgl.astype(BF16)
        dbb = jnp.concatenate([jnp.sum(dgr, axis=0, keepdims=True), jnp.sum(dgl, axis=0, keepdims=True)], axis=1)
        _accumulate(orefs[3], dbb, ids[0] == 0)

    dy_ret, dy_lru, dgpre, db_bg = _proj_sq(
        "d_merged", dx2, *wt("w_out"), "nt",
        extras=[(gates, _spec((tm, D), lambda i, j, r: (i, 0))), (gates, _spec((tm, D), lambda i, j, r: (i, 1))),
                (y_ret, rowD), (y_lru, rowD)],
        epilogue=merged_bwd_epilogue,
        outs=[(_sds((T, D), BF16), rowD), (_sds((T, D), BF16), rowD),
              (_sds((T, 2 * D), BF16), _spec((tm, 2 * D), row3)),
              (_sds((1, 2 * D), F32), _spec((1, 2 * D), vec3))])
    big["w_branch_gate"] = _gemm(
        "dw_bg",
        [(h2, _spec((T, D), lambda j, n, r: (r, 0)), dgpre, _spec((T, BG_BLK), lambda j, n, r: (r, j)), "tn")],
        (N_CHIPS, 1, 1),
        [(_sds((N_CHIPS, D, BG_BLK), GRAD_WIRE_DTYPE), _spec((None, D, BG_BLK), lambda j, n, r: (j, 0, 0)))],
        (D, BG_BLK))[0][None]
    big["w_out"] = _dw_sq("dw_out", merged, dx2)[None]
    dyr = _proj_sq("d_yr", dy_ret, *wt("w_ret_o"), "nt")[0]
    big["w_ret_o"] = _dw_sq("dw_ret_o", yr, dy_ret)[None]
    dyl = _proj_sq("d_yl", dy_lru, *wt("w_lru_o"), "nt")[0]
    big["w_lru_o"] = _dw_sq("dw_lru_o", yl, dy_lru)[None]

    dq, dk, dv, dgr, dg_retgn = _ret_bwd(dyr, ret, u, qr, kr, states, consts, sm["ret_gn"])

    def lru_out_bwd(irefs, orefs, ids):
        gl, dgl = _gelu_and_grad(irefs[2][...])
        dyl_v = irefs[0][...]
        orefs[0][...] = dyl_v * gl
        orefs[1][...] = (dyl_v * irefs[1][...] * dgl).astype(BF16)

    dhl, dglru = _rowwise("lru_out_bwd", lru_out_bwd, [(dyl, row1), (hl, row1), (u, glru1)],
                          [(_sds((T, D), F32), row1), (_sds((T, D), BF16), row1)], (T // tm,))
    lmb = _lru_scan("lru_scan_bwd", a3, dhl.reshape(T, LRU_BLOCKS, LRU_BLOCK), True).reshape(T, D)
    dxl, dw_r, dw_i, dvec, dcw = _lru_gates_bwd(lmb, hl, av, rg, ig, xc, u, conv_w,
                                                sm["w_rgate"], sm["w_igate"], sm["lru_lambda"])

    du = jnp.concatenate([dq, dk, dv, dgr, dxl, dglru], axis=1)
    tk = T
    big["w_in"] = _gemm(
        "dw_in",
        [(h2, _spec((tk, D), lambda j, n, r: (r, 0)), du, _spec((tk, IN_BLK), lambda j, n, r: (r, j)), "tn")],
        (N_CHIPS, 1, T // tk),
        [(_sds((N_CHIPS, D, IN_BLK), GRAD_WIRE_DTYPE), _spec((None, D, IN_BLK), lambda j, n, r: (j, 0, 0)))],
        (D, IN_BLK))[0][None]
    tf = min(FFN_ROW_TILE, T)
    ex, ou = _rms_bwd_io(x1, sm["mix_norm"], dx2, T, tf)
    dx1, dg_mix = _gemm(
        "d_h2",
        [(du, _spec((tf, 5120), row3), gw["win"], _spec((N_CHIPS, None, IN_BLK, D), lambda i, j, r: (0, 0, 0, 0)), "nn"),
         (dgpre, _spec((tf, 2 * D), row3), gw["wbg"], _spec((N_CHIPS, None, BG_BLK, D), lambda i, j, r: (0, 0, 0, 0)),
          "nn")],
        (T // tf, 1, 1), ou, (tf, D), ex, _rms_bwd_epilogue)

    grad_x, dg_ffn1 = _ffn_bwd("ffn1", dx1, h1, a1, b1, s1, *wt("ffn1_w1"), *wt("ffn1_w3"),
                               *wt("ffn1_w2"), x, sm["ffn1_norm"], big)

    small = {
        "ffn1_norm": dg_ffn1, "mix_norm": dg_mix, "ret_gn": dg_retgn, "conv_b": dvec[3:4],
        "b_rgate": dvec[0:1], "b_igate": dvec[1:2], "lru_lambda": dvec[2:3], "xattn_norm": dg_xattn,
        "mem_norm": dg_mem, "ffn2_norm": dg_ffn2, "final_norm": dg_final, "b_branch_gate": db_bg,
        "conv_w": dcw, "w_rgate": dw_r, "w_igate": dw_i,
    }
    return loss, grad_x, small


ANY_SPEC = pl.BlockSpec(memory_space=pl.ANY)
VMEM_SPEC = pl.BlockSpec(memory_space=pltpu.VMEM)
N_PEER_CHIPS = N_CHIPS - 1


def _mesh_position():
    x, y, c = lax.axis_index("x"), lax.axis_index("y"), lax.axis_index("c")
    chips = [(1 - x, y), (x, 1 - y), (1 - x, 1 - y)]
    return x, y, c, chips


def _chip_index(x, y):
    return 2 * x + y


def _rows_half(ref, axis, h):
    n = ref.shape[axis] // 2
    idx = [slice(None)] * len(ref.shape)
    idx[axis] = pl.ds(pl.multiple_of(h * n, 16), n)
    return ref.at[tuple(idx)]


def _remote(src, dst, send_sem, recv_sem, device):
    return pltpu.make_async_remote_copy(src_ref=src, dst_ref=dst, send_sem=send_sem, recv_sem=recv_sem,
                                        device_id=device, device_id_type=MESH)


def _gather_chips_task(shards, split, landed, part=0, nparts=1):
    keys = list(shards)
    n = len(keys)

    def operands():
        if part:
            return [shards[k] for k in keys] + [landed[k] for k in keys]
        chip_me = _chip_index(lax.axis_index("x"), lax.axis_index("y"))
        bases = [lax.dynamic_update_slice(lax.empty((N_CHIPS,) + shards[k].shape, shards[k].dtype), shards[k][None],
                                          (chip_me,) + (0,) * shards[k].ndim) for k in keys]
        return [shards[k] for k in keys] + bases

    def my_rows(ref, c):
        rows = ref.shape[1] // (2 * nparts)
        return ref.at[:, pl.ds(pl.multiple_of((c * nparts + part) * rows, 16), rows), :]

    def make_direct(ins, outs, send_sem, recv_sem):
        x, y, c, chips = _mesh_position()
        s_me = _chip_index(x, y)
        starts, arrivals = [], []
        for g in range(n):
            for k, chip in enumerate(chips):
                sems = (send_sem(3 * g + k), recv_sem(3 * g + k))
                starts.append(functools.partial(_remote, ins[g], outs[g].at[s_me], *sems, (*chip, c)))
                got = outs[g].at[_chip_index(*chip)]
                arrivals.append(functools.partial(_remote, got, got, *sems, (*chip, c)))
        return starts, arrivals

    def axis_neighbours(x, y, c):
        flip = lambda v, f: v + f * (1 - 2 * v)
        return (flip(x, 1 - c), flip(y, c)), (flip(x, c), flip(y, 1 - c))

    def make_swap(ins, outs, send_sem, recv_sem):
        x, y, c, _ = _mesh_position()
        first, _ = axis_neighbours(x, y, c)
        starts, arrivals = [], []
        for g in range(n):
            sems = (send_sem(3 * g), recv_sem(3 * g))
            starts.append(functools.partial(_remote, my_rows(ins[g], c), my_rows(outs[g].at[_chip_index(x, y)], c),
                                            *sems, (*first, c)))
            got = my_rows(outs[g].at[_chip_index(*first)], c)
            arrivals.append(functools.partial(_remote, got, got, *sems, (*first, c)))
        return starts, arrivals

    def make_pass_on(ins, outs, send_sem, recv_sem):
        x, y, c, _ = _mesh_position()
        first, second = axis_neighbours(x, y, c)
        diagonal = (1 - x, 1 - y)
        starts, arrivals = [], []
        for g in range(n):
            half = lambda chip: my_rows(outs[g].at[_chip_index(*chip)], c)
            for k, (sent, arriving) in enumerate([((x, y), second), (first, diagonal)]):
                sems = (send_sem(3 * g + 1 + k), recv_sem(3 * g + 1 + k))
                src = my_rows(ins[g], c) if k == 0 else half(sent)
                starts.append(functools.partial(_remote, src, half(sent), *sems, (*second, c)))
                arrivals.append(functools.partial(_remote, half(arriving), half(arriving), *sems, (*second, c)))
        return starts, arrivals

    def finish(res):
        landed.update(zip(keys, res))

    shapes = lambda: [_sds((N_CHIPS,) + shards[k].shape, shards[k].dtype) for k in keys]
    aliases = {n + g: g for g in range(n)}
    if not split:
        return _Task("chips", operands, shapes, aliases, 3 * n, make_direct, finish)
    return _Task("neighbours", operands, shapes, aliases, 3 * n, make_swap, finish, make_second=make_pass_on)


def _gather_sibling_task(keys, landed, ready):
    n = len(keys)

    def make(ins, outs, send_sem, recv_sem):
        x, y, c, chips = _mesh_position()
        starts, arrivals = [], []
        for g in range(n):
            for k, chip in enumerate(chips):
                o = outs[g].at[_chip_index(*chip)]
                got, other = _rows_half(o, 1, c), _rows_half(o, 1, 1 - c)
                starts.append(functools.partial(_remote, got, got, send_sem(3 * g + k), recv_sem(3 * g + k),
                                                (x, y, 1 - c)))
                arrivals.append(functools.partial(_remote, other, other, send_sem(3 * g + k), recv_sem(3 * g + k),
                                                  (x, y, 1 - c)))
        return starts, arrivals

    def finish(res):
        ready.update(zip(keys, res))

    return _Task("sibling", lambda: [landed[k] for k in keys],
                 lambda: [_sds(landed[k].shape, landed[k].dtype) for k in keys],
                 {g: g for g in range(n)}, 3 * n, make, finish)


def _pair_swap_task(names, big, got):
    n = len(names)

    def make(ins, outs, send_sem, recv_sem):
        x, y, c, _ = _mesh_position()
        copies = [functools.partial(_remote, _rows_half(ins[a], 2, 1 - c), outs[a], send_sem(a), recv_sem(a),
                                    (x, y, 1 - c)) for a in range(n)]
        return copies, copies

    def shapes():
        return [_sds(big[k].shape[:2] + (big[k].shape[2] // 2, big[k].shape[3]), big[k].dtype) for k in names]

    return _Task("sibling", lambda: [big[k] for k in names], shapes, {}, n, make,
                 lambda res: got.update(zip(names, res)))


def _rs_pair_sum(name, fulls, gots, core):
    n = len(fulls)
    shapes = [(f.shape[2] // 2, f.shape[3]) for f in fulls]

    def body(core_ref, *refs):
        for a_ref, b_ref, o_ref in zip(refs[:n], refs[n:2 * n], refs[2 * n:]):
            o_ref[...] = (a_ref[...].astype(F32) + b_ref[...].astype(F32)).astype(BF16)

    mine = [pl.BlockSpec((None, None) + hc, lambda s, core_ref: (0, s, core_ref[0], 0)) for hc in shapes]
    slot = [pl.BlockSpec((None, None) + hc, lambda s, core_ref: (0, s, 0, 0)) for hc in shapes]
    return _pcall(
        body, name=name, grid=(N_CHIPS,), num_prefetch=1,
        in_specs=mine + slot, out_specs=slot,
        out_shape=[_sds((1, N_CHIPS) + hc, BF16) for hc in shapes],
    )(core, *fulls, *gots)


def _chip_exchange_task(names, pair_sums, by_source, part=0, nparts=1):
    n = len(names)

    def rows(ref):
        h = ref.shape[1] // nparts
        return ref.at[:, pl.ds(part * h, h), :]

    def make(ins, outs, send_sem, recv_sem):
        x, y, c, chips = _mesh_position()
        s_me = _chip_index(x, y)
        starts, arrivals = [], []
        for a in range(n):
            for k, chip in enumerate(chips):
                s_k = _chip_index(*chip)
                starts.append(functools.partial(_remote, rows(ins[a].at[:, s_k]), rows(outs[a].at[:, s_me]),
                                                send_sem(3 * a + k), recv_sem(3 * a + k), (*chip, c)))
                got = rows(outs[a].at[:, s_k])
                arrivals.append(functools.partial(_remote, got, got, send_sem(3 * a + k), recv_sem(3 * a + k),
                                                  (*chip, c)))
        return starts, arrivals

    def operands():
        return [pair_sums[k] for k in names] + ([by_source[k] for k in names] if part else [])

    return _Task("chips", operands, lambda: [_sds(pair_sums[k].shape, pair_sums[k].dtype) for k in names],
                 {n + a: a for a in range(n)} if part else {}, 3 * n, make,
                 lambda res: by_source.update(zip(names, res)))


def _rs_chip_sum(name, owns, parts, chip):
    n = len(owns)
    ns = N_CHIPS
    shapes = [p.shape[2:] for p in parts]

    def body(chip_ref, *refs):
        me = chip_ref[0]
        for i in range(n):
            own_v = refs[i][...].astype(F32)
            slots = refs[n + ns * i:n + ns * (i + 1)]
            tot = None
            for s in range(ns):
                term = jnp.where(me == s, own_v, slots[s][...].astype(F32))
                tot = term if tot is None else tot + term
            refs[n + ns * n + i][...] = tot

    def slot_spec(hc, s):
        return pl.BlockSpec((None, None) + hc,
                            lambda g, chip_ref: (0, jnp.where(chip_ref[0] == s, (s + 1) % ns, s), 0, 0))

    own_specs = [pl.BlockSpec((None, None) + hc, lambda g, chip_ref: (0, chip_ref[0], 0, 0)) for hc in shapes]
    slot_specs = [slot_spec(hc, s) for hc in shapes for s in range(ns)]
    return _pcall(
        body, name=name, grid=(1,), num_prefetch=1,
        in_specs=own_specs + slot_specs,
        out_specs=[pl.BlockSpec((None,) + hc, lambda g, chip_ref: (0, 0, 0)) for hc in shapes],
        out_shape=[_sds((1,) + hc, F32) for hc in shapes],
    )(chip, *owns, *[p for p in parts for _ in range(ns)])


def _pair_gather_task(names, halves, sibling_halves):
    n = len(names)

    def make(ins, outs, send_sem, recv_sem):
        x, y, c, _ = _mesh_position()
        copies = [functools.partial(_remote, ins[a], outs[a], send_sem(a), recv_sem(a), (x, y, 1 - c))
                  for a in range(n)]
        return copies, copies

    return _Task("sibling", lambda: [halves[k] for k in names], lambda: [_sds(halves[k].shape, F32) for k in names],
                 {}, n, make, lambda res: sibling_halves.update(zip(names, res)))


def _small_allreduce(arrs):
    n = len(arrs)
    per = 1 + 2 * N_PEER_CHIPS

    def body(*refs):
        v_refs, o_refs = refs[:n], refs[n:2 * n]
        sib, pair, part = refs[2 * n:3 * n], refs[3 * n:4 * n], refs[4 * n:5 * n]
        send_sems, recv_sems = refs[5 * n:]
        x, y, c, chips = _mesh_position()
        s_me = _chip_index(x, y)

        def quarter(ref, s):
            q = ref.shape[0] // N_CHIPS
            return ref.at[pl.ds(pl.multiple_of(s * q, 8), q)]

        def exchange(first_sem, src, dst_of, arrival_of):
            sems = lambda a, k: (send_sems.at[a * per + first_sem + k], recv_sems.at[a * per + first_sem + k])
            sends = [_remote(src(a, _chip_index(*chip)), dst_of(a, s_me), *sems(a, k), (*chip, c))
                     for a in range(n) for k, chip in enumerate(chips)]
            for cp in sends:
                cp.start()
            for a in range(n):
                for k, chip in enumerate(chips):
                    got = arrival_of(a, _chip_index(*chip))
                    _remote(got, got, *sems(a, k), (*chip, c)).wait_recv()
            for cp in sends:
                cp.wait_send()

        swaps = [_remote(v_refs[a], sib[a], send_sems.at[a * per], recv_sems.at[a * per], (x, y, 1 - c))
                 for a in range(n)]
        for cp in swaps:
            cp.start()
        for cp in swaps:
            cp.wait()
        for a in range(n):
            pair[a][...] = v_refs[a][...] + sib[a][...]
        exchange(1, lambda a, s_k: quarter(pair[a], s_k), lambda a, s: part[a].at[s], lambda a, s_k: part[a].at[s_k])
        for a in range(n):
            part[a][s_me] = quarter(pair[a], s_me)[...]
            q = o_refs[a].shape[0] // N_CHIPS
            o_refs[a][pl.ds(pl.multiple_of(s_me * q, 8), q), :] = (
                ((part[a][0] + part[a][1]) + part[a][2]) + part[a][3])
        exchange(1 + N_PEER_CHIPS, lambda a, s_k: quarter(o_refs[a], s_me), lambda a, s: quarter(o_refs[a], s),
                 lambda a, s_k: quarter(o_refs[a], s_k))

    shapes = [a.shape for a in arrs]
    return _pcall(
        body, name="small_allreduce", grid=(1,), own_peers=("sibling", "chips"),
        in_specs=[VMEM_SPEC] * n, out_specs=[VMEM_SPEC] * n, out_shape=[_sds(s, F32) for s in shapes],
        scratch_shapes=([pltpu.VMEM(s, F32) for s in shapes] * 2
                        + [pltpu.VMEM((N_CHIPS, s[0] // N_CHIPS, s[1]), F32) for s in shapes]
                        + [pltpu.SemaphoreType.DMA((n * per,)), pltpu.SemaphoreType.DMA((n * per,))]),
    )(*arrs)


TRANSPOSED_WEIGHTS = ("ffn1_w1", "ffn1_w3", "ffn2_w1", "ffn2_w3")
SMALL_LAYOUT = [("ffn1_norm", 1), ("mix_norm", 1), ("ret_gn", 1), ("conv_b", 1), ("b_rgate", 1), ("b_igate", 1),
                ("lru_lambda", 1), ("xattn_norm", 1), ("mem_norm", 1), ("ffn2_norm", 1), ("final_norm", 1),
                ("b_branch_gate", 2), ("conv_w", CONV_TAPS)]
SMALL_ROWS = 32
GATE_WEIGHTS = ("w_rgate", "w_igate")
WEIGHT_ORDER = ["ffn1_norm", "ffn1_w1", "ffn1_w3", "ffn1_w2", "mix_norm", "w_in", "ret_gn", "w_ret_o", "conv_w",
                "conv_b", "w_rgate", "b_rgate", "w_igate", "b_igate", "lru_lambda", "w_lru_o", "w_branch_gate",
                "b_branch_gate", "w_out", "xattn_norm", "mem_norm", "w_xq", "w_xk", "w_xv", "w_xo", "ffn2_norm",
                "ffn2_w1", "ffn2_w3", "ffn2_w2", "final_norm"]


SMALL_USED_ROWS = sum(n for _, n in SMALL_LAYOUT)


def _pack_small(parts, extra_row=None):
    rows = [parts[name].reshape(n, D) for name, n in SMALL_LAYOUT]
    if extra_row is not None:
        rows.append(extra_row)
    rows.append(jnp.zeros((SMALL_ROWS - sum(r.shape[0] for r in rows), D), F32))
    return jnp.concatenate(rows, axis=0)


def _unpack_small(packed, shapes):
    out, r = {}, 0
    for name, n in SMALL_LAYOUT:
        out[name] = packed[r:r + n].reshape(shapes[name])
        r += n
    return out


def kernel(x, mem, ffn1_norm, ffn1_w1, ffn1_w3, ffn1_w2, mix_norm, w_in, ret_gn, w_ret_o, conv_w, conv_b, w_rgate, b_rgate, w_igate, b_igate, lru_lambda, w_lru_o, w_branch_gate, b_branch_gate, w_out, xattn_norm, mem_norm, w_xq, w_xk, w_xv, w_xo, ffn2_norm, ffn2_w1, ffn2_w3, ffn2_w2, final_norm, loss_target, m_ffn1_norm, m_ffn1_w1, m_ffn1_w3, m_ffn1_w2, m_mix_norm, m_w_in, m_ret_gn, m_w_ret_o, m_conv_w, m_conv_b, m_w_rgate, m_b_rgate, m_w_igate, m_b_igate, m_lru_lambda, m_w_lru_o, m_w_branch_gate, m_b_branch_gate, m_w_out, m_xattn_norm, m_mem_norm, m_w_xq, m_w_xk, m_w_xv, m_w_xo, m_ffn2_norm, m_ffn2_w1, m_ffn2_w3, m_ffn2_w2, m_final_norm, v_ffn1_norm, v_ffn1_w1, v_ffn1_w3, v_ffn1_w2, v_mix_norm, v_w_in, v_ret_gn, v_w_ret_o, v_conv_w, v_conv_b, v_w_rgate, v_b_rgate, v_w_igate, v_b_igate, v_lru_lambda, v_w_lru_o, v_w_branch_gate, v_b_branch_gate, v_w_out, v_xattn_norm, v_mem_norm, v_w_xq, v_w_xk, v_w_xv, v_w_xo, v_ffn2_norm, v_ffn2_w1, v_ffn2_w3, v_ffn2_w2, v_final_norm):
    given = dict(locals())
    w = {n: given[n] for n in WEIGHT_ORDER}
    mom = {n: given["m_" + n] for n in WEIGHT_ORDER}
    var = {n: given["v_" + n] for n in WEIGHT_ORDER}
    chip = _chip_index(lax.axis_index("x"), lax.axis_index("y"))
    core = lax.axis_index("c").astype(jnp.int32).reshape(1)

    chip_id = chip.astype(jnp.int32).reshape(1)
    sm = {n: w[n] for n in ["ffn1_norm", "mix_norm", "ret_gn", "conv_b", "b_rgate", "b_igate", "lru_lambda",
                            "xattn_norm", "mem_norm", "ffn2_norm", "b_branch_gate"]}
    sm["final_norm"] = w["final_norm"].reshape(1, D)
    sm["w_rgate"] = w["w_rgate"][0]
    sm["w_igate"] = w["w_igate"][0]

    local = lambda a, n: jnp.swapaxes(a[0], 0, 1) if n in TRANSPOSED_WEIGHTS else a[0]
    stack = lambda names: jnp.stack([local(w[n], n) for n in names], axis=0).astype(BF16)
    shard = {"col1": stack(["ffn1_w1", "ffn1_w3"]), "row2a": stack(["ffn1_w2"]),
             "win": jnp.swapaxes(w["w_in"], 1, 2).astype(BF16),
             "wbg": jnp.swapaxes(w["w_branch_gate"], 1, 2).astype(BF16),
             "sqA": stack(["w_ret_o", "w_lru_o", "w_out"]), "sqB": stack(["w_xq", "w_xk"]),
             "sqC": stack(["w_xv", "w_xo"]), "col2a": stack(["ffn2_w1"]), "col2b": stack(["ffn2_w3"]),
             "row2b": stack(["ffn2_w2"]), "conv": w["conv_w"]}
    gw, landed = {}, {}
    over_chips = lambda keys: _gather_chips_task({k: shard[k] for k in keys}, True, landed)
    to_sibling = lambda keys: _gather_sibling_task(keys, landed, gw)

    big, got, pair_sums, by_source, halves, sibling_halves, outs = {}, {}, {}, {}, {}, {}, {}
    pair_swap = lambda names: _pair_swap_task(names, big, got)
    exchange = lambda names, part=0, nparts=1: _chip_exchange_task(names, pair_sums, by_source, part, nparts)
    pair_gather = lambda names: _pair_gather_task(names, halves, sibling_halves)

    def pair_sum(names):
        res = _rs_pair_sum("rs_pair_sum_" + names[0], [big[n] for n in names], [got[n] for n in names], core)
        pair_sums.update(zip(names, res))

    def chip_sum(names):
        res = _rs_chip_sum("rs_chip_sum_" + names[0], [pair_sums[n] for n in names], [by_source[n] for n in names],
                           chip_id)
        halves.update(zip(names, res))

    def adamw(names):
        for n in names:
            res = _adamw_halves("adamw_" + n, local(w[n], n), halves[n], sibling_halves[n], 0, local(mom[n], n),
                                local(var[n], n), core)
            outs[n] = tuple((jnp.swapaxes(r, 0, 1) if n in TRANSPOSED_WEIGHTS else r)[None] for r in res)

    do = lambda fn, names: functools.partial(fn, names)
    ffn2_grads = ["ffn2_w2", "ffn2_w1", "ffn2_w3"]
    xattn_grads = ["w_xo", "w_xq", "w_xk", "w_xv"]
    mix_out_grads = ["w_branch_gate", "w_out", "w_ret_o", "w_lru_o"]
    conv_gather = _gather_chips_task({"conv": shard["conv"]}, False, gw)
    half = lambda key, part: _gather_chips_task({key: shard[key]}, True, landed, part, 2)
    plan = _Plan()
    plan.tasks = {
        "ag_first_chips": [over_chips(["col1", "row2a"])],
        "ag_first_sibling": [to_sibling(["col1", "row2a"])],
        "ffn1_up": [over_chips(["win"])],
        "ffn1_down": [to_sibling(["win"]), over_chips(["wbg"]), conv_gather],
        "mix_in": [to_sibling(["wbg"]), over_chips(["sqA"])],
        "ret_fwd": [to_sibling(["sqA"]), over_chips(["col2a"])],
        "lru_gates_fwd": [to_sibling(["col2a"]), over_chips(["sqB"])],
        "lru_scan_fwd": [to_sibling(["sqB"]), over_chips(["sqC"])],
        "mix_gates": [to_sibling(["sqC"]), half("col2b", 0)],
        "y_lru": [half("col2b", 1)],
        "xattn_fwd": [to_sibling(["col2b"])],
        "ffn2_up": [over_chips(["row2b"])],
        "ffn2_up_sibling": [to_sibling(["row2b"])],
        "ffn2_dh": [pair_swap(ffn2_grads)],
        "xattn_bwd": [exchange(["ffn2_w2"], 0, 2)],
        "d_hq": [exchange(["ffn2_w2"], 1, 2)],
        "d_merged": [exchange(["ffn2_w1"], 0, 2), pair_swap(xattn_grads)],
        "lru_out_bwd": [exchange(["w_xo"])],
        "ret_bwd": [exchange(["ffn2_w1"], 1, 2), exchange(["ffn2_w3"], 0, 2), pair_swap(mix_out_grads)],
        "lru_scan_bwd": [exchange(["ffn2_w3"], 1, 2)],
        "lru_gates_bwd": [exchange(["w_xq", "w_xk"]), pair_gather(ffn2_grads)],
        "dw_in": [exchange(["w_xv", "w_out"])],
        "d_h2": [exchange(["w_branch_gate", "w_ret_o", "w_lru_o"]), pair_swap(["w_in"]), pair_gather(xattn_grads)],
        "ffn1_bwd_mid": [exchange(["w_in"], 0, 2), pair_gather(mix_out_grads)],
        "ffn1_dw2": [exchange(["w_in"], 2, 4)],
        "ffn1_dw1": [exchange(["w_in"], 3, 4), pair_swap(["ffn1_w2"])],
        "ffn1_dw3": [exchange(["ffn1_w2"], 0, 2), pair_swap(["ffn1_w1"]), pair_gather(["w_in"])],
        "ffn1_dh": [exchange(["ffn1_w2"], 1, 2), exchange(["ffn1_w1"]), pair_swap(["ffn1_w3"])],
        "small_allreduce": [exchange(["ffn1_w3"]), pair_gather(["ffn1_w2"])],
        "rs_last_gather": [pair_gather(["ffn1_w1", "ffn1_w3"])],
    }
    plan.after = {
        "ffn2_up": [functools.partial(_comm_call, "ffn2_up_sibling")],
        "ffn2_dh": [do(pair_sum, ffn2_grads)],
        "d_merged": [do(pair_sum, xattn_grads)],
        "ret_bwd": [do(pair_sum, mix_out_grads)],
        "lru_scan_bwd": [do(chip_sum, ffn2_grads)],
        "lru_gates_bwd": [do(adamw, ffn2_grads)],
        "dw_in": [do(chip_sum, xattn_grads)],
        "d_h2": [do(chip_sum, mix_out_grads), do(pair_sum, ["w_in"]), do(adamw, xattn_grads)],
        "ffn1_bwd_mid": [do(adamw, mix_out_grads)],
        "ffn1_dw1": [do(chip_sum, ["w_in"]), do(pair_sum, ["ffn1_w2"])],
        "ffn1_dw3": [do(pair_sum, ["ffn1_w1"]), do(adamw, ["w_in"])],
        "ffn1_dh": [do(pair_sum, ["ffn1_w3"]), do(chip_sum, ["ffn1_w2"])],
        "small_allreduce": [do(chip_sum, ["ffn1_w1", "ffn1_w3"]), functools.partial(_comm_call, "rs_last_gather"),
                    do(adamw, ["ffn1_w2", "ffn1_w1", "ffn1_w3"])],
    }
    global _plan
    _plan = plan
    try:
        _comm_call("ag_first_chips")
        _comm_call("ag_first_sibling")
        loss_part, grad_x, small = _local_step(x[0], mem[0], loss_target[0], gw, sm, big)
        gate2d = lambda a: a.reshape(LRU_BLOCKS * LRU_BLOCK, LRU_BLOCK)
        loss_row = jnp.pad(loss_part, ((0, 0), (0, D - loss_part.shape[1])))
        small_sum, *gate_sums = _small_allreduce([_pack_small(small, loss_row)]
                                                 + [gate2d(small[n]) for n in GATE_WEIGHTS])
    finally:
        _plan = None
    assert not plan.tasks and not plan.after, (list(plan.tasks), list(plan.after))
    loss = small_sum[SMALL_USED_ROWS, 0]

    small_shapes = {n: w[n].shape for n, _ in SMALL_LAYOUT}
    small_shapes["conv_w"] = (CONV_TAPS, D)
    conv_row = SMALL_USED_ROWS - CONV_TAPS
    conv_grad = lax.dynamic_slice(small_sum[conv_row:conv_row + CONV_TAPS], (0, chip * SQ_BLK), (CONV_TAPS, SQ_BLK))
    small_w = {n: w[n] for n, _ in SMALL_LAYOUT}
    small_m = {n: mom[n] for n, _ in SMALL_LAYOUT}
    small_v = {n: var[n] for n, _ in SMALL_LAYOUT}
    pad_cols = lambda a: jnp.pad(a[0], ((0, 0), (0, D - SQ_BLK)))
    for dct in (small_w, small_m, small_v):
        dct["conv_w"] = pad_cols(dct["conv_w"])
    g_pack = lax.dynamic_update_slice(small_sum, jnp.pad(conv_grad, ((0, 0), (0, D - SQ_BLK))), (conv_row, 0))
    d_pack, m_pack, v_pack = _adamw("adamw_small", _pack_small(small_w), g_pack, _pack_small(small_m),
                                    _pack_small(small_v))
    unpacked = [_unpack_small(p, small_shapes) for p in (g_pack, d_pack, m_pack, v_pack)]
    for n, _ in SMALL_LAYOUT:
        if n == "conv_w":
            outs[n] = tuple(u[n][:, :SQ_BLK][None] for u in unpacked)
        else:
            outs[n] = tuple(u[n] for u in unpacked)
    for n, gsum in zip(GATE_WEIGHTS, gate_sums):
        d, nm, nv = _adamw("adamw_" + n, gate2d(w[n]), gsum, gate2d(mom[n]), gate2d(var[n]))
        outs[n] = tuple(r.reshape(w[n].shape) for r in (gsum, d, nm, nv))

    result = [loss, grad_x[None]]
    for k in range(4):
        result += [outs[n][k] for n in WEIGHT_ORDER]
    return tuple(result)
```

```python
import functools
import math

import jax
import jax.numpy as jnp
from jax import lax
from jax.experimental import pallas as pl
from jax.experimental.pallas import tpu as pltpu

F32 = jnp.float32
BF16 = jnp.bfloat16
GRAD_WIRE_DTYPE = BF16
MESH = pl.DeviceIdType.MESH

D = 1024
EPS = 1e-6
RET_HEADS = 4
RET_DK = 128
RET_DV = 256
CHUNK = 128
ROPE_BASE = 10000.0
LRU_BLOCKS = 8
LRU_BLOCK = 128
CONV_TAPS = 4
LRU_C = 8.0
D_FF = 2816
X_HEADS = 4
X_HD = 256
N_CHIPS = 4
FF_BLK = D_FF // N_CHIPS
IN_BLK = 5120 // N_CHIPS
BG_BLK = 2048 // N_CHIPS
SQ_BLK = D // N_CHIPS

ADAM_LR = 0.001
ADAM_B1 = 0.9
ADAM_B2 = 0.999
ADAM_EPS = 1e-08
ADAM_WD = 0.01
ADAM_STEP = 10

VMEM_LIMIT_BYTES = 56 * 1024 * 1024
ROW_TILE = 512
WIDE_ROW_TILE = 1024
FFN_ROW_TILE = 256
DW_BLK = D_FF // 2
SCAN_TILE = 256
RET_STEP_CHUNKS = 2
RET_STEP_ROWS = RET_STEP_CHUNKS * CHUNK

_DN = {
    "nn": (((1,), (0,)), ((), ())),
    "nt": (((1,), (1,)), ((), ())),
    "tn": (((0,), (0,)), ((), ())),
}


def _cparams(n_axes, collective_id=None):
    return pltpu.CompilerParams(dimension_semantics=("arbitrary",) * n_axes,
                                vmem_limit_bytes=VMEM_LIMIT_BYTES, collective_id=collective_id)


def _dot(a, b, kind):
    if b.ndim == 3:
        b = b.reshape(b.shape[0] * b.shape[1], b.shape[2])
    return lax.dot_general(a.astype(BF16), b.astype(BF16), _DN[kind], preferred_element_type=F32)


def _sigmoid(x):
    return 1.0 / (1.0 + jnp.exp(-x))


def _log1p_pos(e):
    u = 1.0 + e
    return jnp.where(u == 1.0, e, jnp.log(u) * (e / jnp.where(u == 1.0, 1.0, u - 1.0)))


def _expm1(x):
    u = jnp.exp(x)
    lu = jnp.log(u)
    safe = jnp.where(lu == 0.0, 1.0, lu)
    return jnp.where(u == 1.0, x, (u - 1.0) * (x / safe))


def _softplus(z):
    return jnp.maximum(z, 0.0) + _log1p_pos(jnp.exp(-jnp.abs(z)))


_GELU_C = math.sqrt(2.0 / math.pi)


def _gelu_and_grad(x):
    x2 = x * x
    t = jnp.tanh(_GELU_C * (x + 0.044715 * x * x2))
    g = 0.5 * x * (1.0 + t)
    dg = 0.5 * (1.0 + t) + 0.5 * x * (1.0 - t * t) * (_GELU_C * (1.0 + 3.0 * 0.044715 * x2))
    return g, dg


def _rms_fwd(x, g):
    r = lax.rsqrt(jnp.mean(x * x, axis=-1, keepdims=True) + EPS)
    return (x * r) * g


def _rms_bwd(x, g, dh):
    r = lax.rsqrt(jnp.mean(x * x, axis=-1, keepdims=True) + EPS)
    n = x * r
    dyg = dh * g
    dx = r * (dyg - n * jnp.mean(dyg * n, axis=-1, keepdims=True))
    return dx, jnp.sum(dh * n, axis=0, keepdims=True)


def _accumulate(ref, val, first):
    @pl.when(first)
    def _():
        ref[...] = val

    @pl.when(jnp.logical_not(first))
    def _():
        ref[...] += val


def _sds(shape, dtype):
    return jax.ShapeDtypeStruct(tuple(shape), dtype)


def _spec(shape, fn):
    return pl.BlockSpec(tuple(shape), fn)


class _Task:
    def __init__(self, peers, operands, out_shapes, aliases, nsem, make, finish, make_second=None):
        self.peers = peers
        self.operands, self.out_shapes, self.aliases = operands, out_shapes, aliases
        self.nsem, self.make, self.finish = nsem, make, finish
        self.make_second = make_second


class _Plan:
    def __init__(self):
        self.tasks, self.after = {}, {}


_plan = None


PEER_SET_COLLECTIVE_ID = {frozenset({"sibling"}): 1, frozenset({"chips"}): 2, frozenset({"sibling", "chips"}): 3,
                          frozenset({"neighbours"}): 4, frozenset({"sibling", "neighbours"}): 5}


def _peer_set(names):
    names = frozenset(names)
    return names - {"neighbours"} if "chips" in names else names


def _entry_handshake(peer_set):
    x, y, c, chips = _mesh_position()
    peers = [(x, y, 1 - c)] if "sibling" in peer_set else []
    if "chips" in peer_set:
        peers += [(*chip, c) for chip in chips]
    if "neighbours" in peer_set:
        peers += [(*chip, c) for chip in chips[:2]]
    barrier = pltpu.get_barrier_semaphore()
    for peer in peers:
        pl.semaphore_signal(barrier, inc=1, device_id=peer, device_id_type=MESH)
    pl.semaphore_wait(barrier, len(peers))


def _pcall(body, *, name, grid, in_specs, out_specs, out_shape, scratch_shapes=(), num_prefetch=0, own_peers=()):
    single = not isinstance(out_shape, (list, tuple))
    out_shape = [out_shape] if single else list(out_shape)
    out_specs = [out_specs] if single else list(out_specs)
    in_specs = list(in_specs)
    scratch_shapes = list(scratch_shapes)
    tasks = _plan.tasks.pop(name, []) if _plan is not None else []
    after = _plan.after.pop(name, []) if _plan is not None else []
    peer_set = _peer_set([t.peers for t in tasks] + list(own_peers))
    nax = len(grid)

    def run(*operands):
        n_in = len(operands) - num_prefetch
        n_out = len(out_shape)
        t_ops = [t.operands() for t in tasks]
        t_outs = [t.out_shapes() for t in tasks]
        c_ops = [a for ops in t_ops for a in ops]
        c_outs = [s for outs in t_outs for s in outs]
        aliases = {}
        i0, o0 = num_prefetch + n_in, n_out
        for t, ops, outs in zip(tasks, t_ops, t_outs):
            for i_loc, o_loc in t.aliases.items():
                aliases[i0 + i_loc] = o0 + o_loc
            i0 += len(ops)
            o0 += len(outs)
        nsem = sum(t.nsem for t in tasks)

        def wrapped(*refs):
            p = num_prefetch
            pre, ins = refs[:p], refs[p:p + n_in]
            cins = refs[p + n_in:p + n_in + len(c_ops)]
            q = p + n_in + len(c_ops)
            outs, couts = refs[q:q + n_out], refs[q + n_out:q + n_out + len(c_outs)]
            q += n_out + len(c_outs)
            scr = refs[q:q + len(scratch_shapes)]

            def rounds(second):
                send_sems, recv_sems = refs[q + len(scratch_shapes):]
                out = []
                ci = co = so = 0
                for t, ops, souts in zip(tasks, t_ops, t_outs):
                    make = t.make_second if second else t.make
                    out.append(([], []) if make is None else
                               make(cins[ci:ci + len(ops)], couts[co:co + len(souts)],
                                    functools.partial(lambda base, k: send_sems.at[base + k], so),
                                    functools.partial(lambda base, k: recv_sems.at[base + k], so)))
                    ci, co, so = ci + len(ops), co + len(souts), so + t.nsem
                return out

            two_rounds = [t.make_second is not None for t in tasks]
            if peer_set:
                ids = [pl.program_id(k) for k in range(nax)]
                first = functools.reduce(jnp.logical_and, [i == 0 for i in ids])
                last = functools.reduce(jnp.logical_and, [i == g - 1 for i, g in zip(ids, grid)])
                step = functools.reduce(lambda acc, ig: acc * ig[1] + ig[0], zip(ids, grid), 0)
                middle = step == math.prod(grid) // 3

                @pl.when(first)
                def _():
                    _entry_handshake(peer_set)
                    for starts, _ in rounds(False):
                        for copy in starts:
                            copy().start()

            body(*pre, *ins, *outs, *scr)

            if any(two_rounds):
                @pl.when(middle)
                def _():
                    for (_, arrivals), two in zip(rounds(False), two_rounds):
                        if two:
                            for arrival in arrivals:
                                arrival().wait_recv()
                    for starts, _ in rounds(True):
                        for copy in starts:
                            copy().start()

            if tasks:
                @pl.when(last)
                def _():
                    first_round, second_round = rounds(False), rounds(True)
                    for (_, arrivals1), (_, arrivals2), two in zip(first_round, second_round, two_rounds):
                        for arrival in (arrivals2 if two else arrivals1):
                            arrival().wait_recv()
                    for starts, _ in first_round + second_round:
                        for copy in starts:
                            copy().wait_send()

        sems = [pltpu.SemaphoreType.DMA((nsem,)), pltpu.SemaphoreType.DMA((nsem,))] if tasks else []
        res = pl.pallas_call(
            wrapped, name=name,
            grid_spec=pltpu.PrefetchScalarGridSpec(
                num_scalar_prefetch=num_prefetch, grid=tuple(grid),
                in_specs=in_specs + [ANY_SPEC] * len(c_ops),
                out_specs=out_specs + [ANY_SPEC] * len(c_outs),
                scratch_shapes=scratch_shapes + sems),
            out_shape=out_shape + c_outs,
            input_output_aliases=aliases,
            compiler_params=_cparams(nax, PEER_SET_COLLECTIVE_ID[peer_set] if peer_set else None),
        )(*operands, *c_ops)
        co = n_out
        for t, souts in zip(tasks, t_outs):
            t.finish(res[co:co + len(souts)])
            co += len(souts)
        for fn in after:
            fn()
        return res[0] if single else list(res[:n_out])

    return run


def _comm_call(name):
    def body(o_ref):
        o_ref[...] = jnp.zeros_like(o_ref)

    _pcall(body, name=name, grid=(1,), in_specs=[], out_specs=_spec((8, 128), lambda i: (0, 0)),
           out_shape=_sds((8, 128), F32))()


def _gemm(name, terms, grid, outs, acc_shape, extras=(), epilogue=None):
    kinds = [t[4] for t in terms]
    nt, ne, no = len(terms), len(extras), len(outs)
    nred = grid[-1]
    nax = len(grid)

    def body(*refs):
        trefs = refs[:2 * nt]
        erefs = refs[2 * nt:2 * nt + ne]
        orefs = refs[2 * nt + ne:2 * nt + ne + no]
        ids = [pl.program_id(k) for k in range(nax)]
        tot = None
        for t in range(nt):
            d = _dot(trefs[2 * t][...], trefs[2 * t + 1][...], kinds[t])
            tot = d if tot is None else tot + d

        def finish(acc):
            if epilogue is None:
                orefs[0][...] = acc.astype(orefs[0].dtype)
            else:
                epilogue(acc, erefs, orefs, ids)

        if nred == 1:
            finish(tot)
        else:
            acc_ref = refs[-1]
            r = ids[-1]

            @pl.when(r == 0)
            def _():
                acc_ref[...] = tot

            @pl.when(r > 0)
            def _():
                acc_ref[...] += tot

            @pl.when(r == nred - 1)
            def _():
                finish(acc_ref[...])

    operands, in_specs = [], []
    for a, a_spec, b, b_spec, _ in terms:
        operands += [a, b]
        in_specs += [a_spec, b_spec]
    for e, e_spec in extras:
        operands.append(e)
        in_specs.append(e_spec)
    scratch = [pltpu.VMEM(tuple(acc_shape), F32)] if nred > 1 else []
    return _pcall(body, name=name, grid=tuple(grid), in_specs=in_specs, out_specs=[o[1] for o in outs],
                  out_shape=[o[0] for o in outs], scratch_shapes=scratch)(*operands)


def _rowwise(name, fn, ins, outs, grid):
    ni = len(ins)
    nax = len(grid)

    def body(*refs):
        ids = [pl.program_id(k) for k in range(nax)]
        fn(refs[:ni], refs[ni:], ids)

    return _pcall(body, name=name, grid=tuple(grid), in_specs=[i[1] for i in ins],
                  out_specs=[o[1] for o in outs], out_shape=[o[0] for o in outs])(*[i[0] for i in ins])


def _ffn_up(name, h, w1buf, w1_idx, w3buf, w3_idx):
    T = h.shape[0]
    tm = min(FFN_ROW_TILE, T)

    def body(h_ref, w1_ref, w3_ref, a_ref, b_ref, s_ref):
        hv = h_ref[...]
        a = _dot(hv, w1_ref[...], "nt")
        b = _dot(hv, w3_ref[...], "nt")
        a_ref[...] = a.astype(BF16)
        b_ref[...] = b.astype(BF16)
        s_ref[...] = ((a * _sigmoid(a)) * b).astype(BF16)

    blk = _spec((tm, D_FF), lambda i: (i, 0))
    return _pcall(
        body, name=name, grid=(T // tm,),
        in_specs=[_spec((tm, D), lambda i: (i, 0)),
                  _spec((N_CHIPS, None, FF_BLK, D), lambda i: (0, w1_idx, 0, 0)),
                  _spec((N_CHIPS, None, FF_BLK, D), lambda i: (0, w3_idx, 0, 0))],
        out_specs=[blk, blk, blk],
        out_shape=[_sds((T, D_FF), BF16)] * 3,
    )(h, w1buf, w3buf)


def _ffn_down(name, s, wrow2, w2_idx, x_res, g_next=None):
    T = x_res.shape[0]
    tm = min(ROW_TILE, T)
    row = lambda i, j, r: (i, 0)

    def epilogue(acc, erefs, orefs, ids):
        xo = erefs[0][...] + 0.5 * acc
        orefs[0][...] = xo
        if g_next is not None:
            orefs[1][...] = _rms_fwd(xo, erefs[1][...]).astype(BF16)

    extras = [(x_res, _spec((tm, D), row))]
    outs = [(_sds((T, D), F32), _spec((tm, D), row))]
    if g_next is not None:
        extras.append((g_next, _spec((1, D), lambda i, j, r: (0, 0))))
        outs.append((_sds((T, D), BF16), _spec((tm, D), row)))
    return _gemm(
        name,
        [(s, _spec((tm, D_FF), row),
          wrow2, _spec((N_CHIPS, None, FF_BLK, D), lambda i, j, r: (0, w2_idx, 0, 0)), "nn")],
        (T // tm, 1, 1), outs, (tm, D), extras, epilogue)


def _ffn_bwd_mid(name, dx, wrow2, w2_idx, a, b):
    T = dx.shape[0]
    tm = min(FFN_ROW_TILE, T)

    def body(dx_ref, w2_ref, a_ref, b_ref, dab_ref):
        ds = _dot(0.5 * dx_ref[...], w2_ref[...], "nt")
        av = a_ref[...].astype(F32)
        sg = _sigmoid(av)
        dab_ref[0] = (ds * b_ref[...].astype(F32) * (sg * (1.0 + av * (1.0 - sg)))).astype(BF16)
        dab_ref[1] = (ds * (av * sg)).astype(BF16)

    blk = _spec((tm, D_FF), lambda i: (i, 0))
    return _pcall(
        body, name=name, grid=(T // tm,),
        in_specs=[_spec((tm, D), lambda i: (i, 0)),
                  _spec((N_CHIPS, None, FF_BLK, D), lambda i: (0, w2_idx, 0, 0)),
                  blk, blk],
        out_specs=_spec((2, tm, D_FF), lambda i: (0, i, 0)),
        out_shape=_sds((2, T, D_FF), BF16),
    )(dx, wrow2, a, b)


def _rms_bwd_epilogue(acc, erefs, orefs, ids):
    dx, dgp = _rms_bwd(erefs[0][...], erefs[1][...], acc)
    orefs[0][...] = dx + erefs[2][...]
    _accumulate(orefs[1], dgp, ids[0] == 0)


def _rms_bwd_io(x, g, dres, T, tm):
    row = lambda i, j, r: (i, 0)
    vec = lambda i, j, r: (0, 0)
    extras = [(x, _spec((tm, D), row)), (g, _spec((1, D), vec)), (dres, _spec((tm, D), row))]
    outs = [(_sds((T, D), F32), _spec((tm, D), row)), (_sds((1, D), F32), _spec((1, D), vec))]
    return extras, outs


def _ffn_bwd(tag, dx_out, h, a, b, s, w1buf, w1_idx, w3buf, w3_idx, wrow2, w2_idx, x_in, g, big):
    T = dx_out.shape[0]
    dab = _ffn_bwd_mid(tag + "_bwd_mid", dx_out, wrow2, w2_idx, a, b)

    def half_scale(acc, erefs, orefs, ids):
        orefs[0][...] = (0.5 * acc).astype(orefs[0].dtype)

    dw_grid = (D_FF // DW_BLK, 1, 1)
    dw_out = [(_sds((D_FF, D), GRAD_WIRE_DTYPE), _spec((DW_BLK, D), lambda j, n, r: (j, 0)))]
    tokens = _spec((T, D), lambda j, n, r: (0, 0))
    big[tag + "_w2"] = _gemm(
        tag + "_dw2", [(s, _spec((T, DW_BLK), lambda j, n, r: (0, j)), dx_out, tokens, "tn")],
        dw_grid, dw_out, (DW_BLK, D), (), half_scale)[0].reshape(1, N_CHIPS, FF_BLK, D)
    for widx, wname in ((0, "_w1"), (1, "_w3")):
        big[tag + wname] = _gemm(
            tag + "_d" + wname[1:],
            [(dab, _spec((None, T, DW_BLK), functools.partial(lambda w, j, n, r: (w, 0, j), widx)), h, tokens, "tn")],
            dw_grid, dw_out, (DW_BLK, D))[0].reshape(1, N_CHIPS, FF_BLK, D)
    tm = min(FFN_ROW_TILE, T)
    extras, outs = _rms_bwd_io(x_in, g, dx_out, T, tm)
    whole = lambda idx: _spec((N_CHIPS, None, FF_BLK, D), lambda i, j, r: (0, idx, 0, 0))
    dx_in, dg = _gemm(
        tag + "_dh",
        [(dab, _spec((None, tm, D_FF), lambda i, j, r: (0, i, 0)), w1buf, whole(w1_idx), "nn"),
         (dab, _spec((None, tm, D_FF), lambda i, j, r: (1, i, 0)), w3buf, whole(w3_idx), "nn")],
        (T // tm, 1, 1), outs, (tm, D), extras, _rms_bwd_epilogue)
    return dx_in, dg


def _proj_sq(name, a, wsq, idx, kind, out_dtype=F32, extras=(), epilogue=None, outs=None):
    M = a.shape[0]
    tm = min(ROW_TILE, M)
    if outs is None:
        outs = [(_sds((M, D), out_dtype), _spec((tm, D), lambda i, j, r: (i, 0)))]
    return _gemm(
        name,
        [(a, _spec((tm, D), lambda i, j, r: (i, 0)),
          wsq, _spec((N_CHIPS, None, SQ_BLK, D), lambda i, j, r: (0, idx, 0, 0)), kind)],
        (M // tm, 1, 1), outs, (tm, D), extras, epilogue)


def _dw_sq(name, a, b):
    M = a.shape[0]
    tn = D // 2
    whole = _gemm(
        name,
        [(a, _spec((M, D), lambda i, j, r: (0, 0)), b, _spec((M, tn), lambda i, j, r: (0, j)), "tn")],
        (1, D // tn, 1),
        [(_sds((D, D), GRAD_WIRE_DTYPE), _spec((D, tn), lambda i, j, r: (0, j)))],
        (D, tn))[0]
    return whole.reshape(N_CHIPS, SQ_BLK, D)


def _retention_constants(T):
    pos = jnp.arange(T, dtype=F32)
    inv_freq = ROPE_BASE ** (-jnp.arange(0, RET_DK, 2, dtype=F32) / RET_DK)
    ang = pos[:, None] * inv_freq[None, :]
    cosf = jnp.concatenate([jnp.cos(ang), jnp.cos(ang)], axis=1)
    sins = jnp.concatenate([-jnp.sin(ang), jnp.sin(ang)], axis=1)
    lg = jnp.log(1.0 - 2.0 ** (-5.0 - jnp.arange(RET_HEADS, dtype=F32)))
    p = jnp.arange(CHUNK, dtype=F32)
    rel = p[:, None] - p[None, :]
    dmat = jnp.where(rel[None] >= 0, jnp.exp(rel[None] * lg[:, None, None]), 0.0)
    kd = jnp.exp((CHUNK - 1.0 - p)[None, :] * lg[:, None])[:, :, None]
    qd = jnp.exp((p + 1.0)[None, :] * lg[:, None])[:, :, None]
    cd = jnp.exp(CHUNK * lg)[:, None, None]
    return cosf, sins, dmat, kd, qd, cd


def _rot(t, cosv, sinv):
    return t * cosv + pltpu.roll(t, RET_DK // 2, 1) * sinv


def _unrot(t, cosv, sinv):
    return t * cosv - pltpu.roll(t, RET_DK // 2, 1) * sinv


def _ret_const_specs(cm):
    whole = lambda shape: _spec(shape, lambda c: (0,) * len(shape))
    return [
        _spec((RET_STEP_ROWS, RET_DK), lambda c: (cm(c), 0)),
        _spec((RET_STEP_ROWS, RET_DK), lambda c: (cm(c), 0)),
        whole((RET_HEADS, CHUNK, CHUNK)), whole((RET_HEADS, CHUNK, 1)), whole((RET_HEADS, CHUNK, 1)),
        whole((RET_HEADS, 1, 1)),
    ]


def _head(h, width):
    return slice(h * width, (h + 1) * width)


def _ret_fwd(u, consts, ret_gn):
    T = u.shape[0]
    nC = T // CHUNK
    kscale = RET_DK ** -0.5

    def body(q_ref, k_ref, v_ref, g_ref, cos_ref, sin_ref, dm_ref, kd_ref, qd_ref, cd_ref, gn_ref,
             qr_ref, kr_ref, ret_ref, yr_ref, st_ref, state):
        @pl.when(pl.program_id(0) == 0)
        def _():
            state[...] = jnp.zeros_like(state)

        for cc in range(RET_STEP_CHUNKS):
            rows = slice(cc * CHUNK, (cc + 1) * CHUNK)
            cosv, sinv = cos_ref[rows, :], sin_ref[rows, :]
            for h in range(RET_HEADS):
                hk, hv = _head(h, RET_DK), _head(h, RET_DV)
                q = _rot(q_ref[rows, hk], cosv, sinv)
                k = _rot(k_ref[rows, hk], cosv, sinv) * kscale
                v = v_ref[rows, hv]
                qr_ref[rows, hk] = q
                kr_ref[rows, hk] = k
                prev = state[h]
                st_ref[h, cc] = prev
                s = _dot(q, k, "nt") * dm_ref[h]
                ret = _dot(s, v, "nn") + _dot(q, prev, "nn") * qd_ref[h]
                state[h] = cd_ref[h] * prev + _dot(k * kd_ref[h], v, "tn")
                ret_ref[rows, hv] = ret
                mu = jnp.mean(ret, axis=-1, keepdims=True)
                xc = ret - mu
                yn = xc * lax.rsqrt(jnp.mean(xc * xc, axis=-1, keepdims=True) + EPS)
                g = g_ref[rows, hv]
                yr_ref[rows, hv] = ((g * _sigmoid(g)) * (yn * gn_ref[:, hv])).astype(BF16)

    cm = lambda c: c
    qk_w, v_w = RET_HEADS * RET_DK, RET_HEADS * RET_DV
    in_specs = [
        _spec((RET_STEP_ROWS, qk_w), lambda c: (c, 0)), _spec((RET_STEP_ROWS, qk_w), lambda c: (c, 1)),
        _spec((RET_STEP_ROWS, v_w), lambda c: (c, 1)), _spec((RET_STEP_ROWS, v_w), lambda c: (c, 2)),
    ] + _ret_const_specs(cm) + [_spec((1, v_w), lambda c: (0, 0))]
    qk_out = _spec((RET_STEP_ROWS, qk_w), lambda c: (c, 0))
    v_out = _spec((RET_STEP_ROWS, v_w), lambda c: (c, 0))
    return _pcall(
        body, name="ret_fwd", grid=(nC // RET_STEP_CHUNKS,),
        in_specs=in_specs,
        out_specs=[qk_out, qk_out, v_out, v_out,
                   _spec((RET_HEADS, RET_STEP_CHUNKS, RET_DK, RET_DV), lambda c: (0, c, 0, 0))],
        out_shape=[_sds((T, qk_w), F32), _sds((T, qk_w), F32), _sds((T, v_w), F32), _sds((T, v_w), BF16),
                   _sds((RET_HEADS, nC, RET_DK, RET_DV), F32)],
        scratch_shapes=[pltpu.VMEM((RET_HEADS, RET_DK, RET_DV), F32)],
    )(u, u, u, u, *consts, ret_gn)


def _ret_bwd(dyr, ret, u, qr, kr, states, consts, ret_gn):
    T = u.shape[0]
    nC = T // CHUNK
    kscale = RET_DK ** -0.5

    def body(dyr_ref, ret_ref, g_ref, q_ref, k_ref, v_ref, st_ref,
             cos_ref, sin_ref, dm_ref, kd_ref, qd_ref, cd_ref, gn_ref,
             dq_ref, dk_ref, dv_ref, dg_ref, dgn_ref, gstate):
        first = pl.program_id(0) == 0

        @pl.when(first)
        def _():
            gstate[...] = jnp.zeros_like(gstate)

        dgn_total = None
        for cc in reversed(range(RET_STEP_CHUNKS)):
            rows = slice(cc * CHUNK, (cc + 1) * CHUNK)
            cosv, sinv = cos_ref[rows, :], sin_ref[rows, :]
            dgn_parts = []
            for h in range(RET_HEADS):
                hk, hv = _head(h, RET_DK), _head(h, RET_DV)
                ret = ret_ref[rows, hv]
                mu = jnp.mean(ret, axis=-1, keepdims=True)
                xc = ret - mu
                rs = lax.rsqrt(jnp.mean(xc * xc, axis=-1, keepdims=True) + EPS)
                yn = xc * rs
                gn = gn_ref[:, hv]
                g = g_ref[rows, hv]
                sg = _sigmoid(g)
                dyr_v = dyr_ref[rows, hv]
                dretn = dyr_v * (g * sg)
                dg_ref[rows, hv] = (dyr_v * (yn * gn) * (sg * (1.0 + g * (1.0 - sg)))).astype(BF16)
                dgn_parts.append(jnp.sum(dretn * yn, axis=0, keepdims=True))
                dyn = dretn * gn
                d_o = rs * (dyn - jnp.mean(dyn, axis=-1, keepdims=True)
                            - yn * jnp.mean(dyn * yn, axis=-1, keepdims=True))

                q, k, v = q_ref[rows, hk], k_ref[rows, hk], v_ref[rows, hv]
                dmat, kd, qd = dm_ref[h], kd_ref[h], qd_ref[h]
                prev = st_ref[h, cc]
                gnext = gstate[h]
                s = _dot(q, k, "nt") * dmat
                ds = _dot(d_o, v, "nt") * dmat
                doq = d_o * qd
                dq = _dot(ds, k, "nn") + _dot(doq, prev, "nt")
                dk = _dot(ds, q, "tn") + _dot(v, gnext, "nt") * kd
                dv = _dot(s, d_o, "tn") + _dot(k * kd, gnext, "nn")
                gstate[h] = cd_ref[h] * gnext + _dot(q, doq, "tn")
                dq_ref[rows, hk] = _unrot(dq, cosv, sinv).astype(BF16)
                dk_ref[rows, hk] = _unrot(dk * kscale, cosv, sinv).astype(BF16)
                dv_ref[rows, hv] = dv.astype(BF16)
            dgn = jnp.concatenate(dgn_parts, axis=1)
            dgn_total = dgn if dgn_total is None else dgn_total + dgn
        _accumulate(dgn_ref, dgn_total, first)

    n_steps = nC // RET_STEP_CHUNKS
    cm = lambda c: n_steps - 1 - c
    qk_w, v_w = RET_HEADS * RET_DK, RET_HEADS * RET_DV
    vspec = lambda blk: _spec((RET_STEP_ROWS, v_w), lambda c: (cm(c), blk))
    qspec = _spec((RET_STEP_ROWS, qk_w), lambda c: (cm(c), 0))
    in_specs = [vspec(0), vspec(0), vspec(2), qspec, qspec, vspec(1),
                _spec((RET_HEADS, RET_STEP_CHUNKS, RET_DK, RET_DV), lambda c: (0, cm(c), 0, 0)),
                ] + _ret_const_specs(cm) + [_spec((1, v_w), lambda c: (0, 0))]
    return _pcall(
        body, name="ret_bwd", grid=(n_steps,),
        in_specs=in_specs,
        out_specs=[qspec, qspec, vspec(0), vspec(0), _spec((1, v_w), lambda c: (0, 0))],
        out_shape=[_sds((T, qk_w), BF16), _sds((T, qk_w), BF16), _sds((T, v_w), BF16), _sds((T, v_w), BF16),
                   _sds((1, v_w), F32)],
        scratch_shapes=[pltpu.VMEM((RET_HEADS, RET_DK, RET_DV), F32)],
    )(dyr, ret, u, qr, kr, u, states, *consts, ret_gn)


def _shift_down(x, s):
    rows = lax.broadcasted_iota(jnp.int32, x.shape, 0)
    return jnp.where(rows >= s, pltpu.roll(x, s, 0), 0.0)


def _shift_up(x, s):
    n = x.shape[0]
    rows = lax.broadcasted_iota(jnp.int32, x.shape, 0)
    return jnp.where(rows < n - s, pltpu.roll(x, n - s, 0), 0.0)


def _lru_specs(T):
    col = lambda off: _spec((T, LRU_BLOCK), lambda g: (0, off + g))
    vec = _spec((1, LRU_BLOCK), lambda g: (0, g))
    wblk = _spec((None, LRU_BLOCK, LRU_BLOCK), lambda g: (g, 0, 0))
    cw = _spec((CONV_TAPS, LRU_BLOCK), lambda g: (0, g))
    return col, vec, wblk, cw


def _lru_gates_fwd(u, conv_w, conv_b, w_r, b_r, w_i, b_i, lam):
    T = u.shape[0]
    col, vec, wblk, cw = _lru_specs(T)

    def body(x_ref, cw_ref, cb_ref, wr_ref, br_ref, wi_ref, bi_ref, lam_ref,
             xc_ref, r_ref, i_ref, a_ref, bx_ref):
        x = x_ref[...]
        w = cw_ref[...]
        xc = (_shift_down(x, 3) * w[0:1] + _shift_down(x, 2) * w[1:2] + _shift_down(x, 1) * w[2:3]
              + x * w[3:4] + cb_ref[...])
        r = _sigmoid(_dot(xc, wr_ref[...], "nn") + br_ref[...])
        i = _sigmoid(_dot(xc, wi_ref[...], "nn") + bi_ref[...])
        la = (-LRU_C) * r * _softplus(-lam_ref[...])
        xc_ref[...] = xc
        r_ref[...] = r
        i_ref[...] = i
        a_ref[...] = jnp.exp(la)
        bx_ref[...] = jnp.sqrt(-_expm1(2.0 * la)) * (i * xc)

    out = col(0)
    return _pcall(
        body, name="lru_gates_fwd", grid=(LRU_BLOCKS,),
        in_specs=[col(24), cw, vec, wblk, vec, wblk, vec, vec],
        out_specs=[out] * 5,
        out_shape=[_sds((T, D), F32)] * 5,
    )(u, conv_w, conv_b, w_r, b_r, w_i, b_i, lam)


def _lru_scan(name, a3, b3, reverse):
    T = a3.shape[0]
    nt = T // SCAN_TILE
    unroll = 8

    def body(a_ref, b_ref, o_ref, carry):
        @pl.when(pl.program_id(0) == 0)
        def _():
            carry[...] = jnp.zeros_like(carry)

        if not reverse:
            def step(t, h):
                h = a_ref[t] * h + b_ref[t]
                o_ref[t] = h
                return h
        else:
            def step(k, c):
                t = SCAN_TILE - 1 - k
                l = b_ref[t] + c
                o_ref[t] = l
                return a_ref[t] * l
        carry[...] = lax.fori_loop(0, SCAN_TILE, step, carry[...], unroll=unroll)

    idx = (lambda i: (nt - 1 - i, 0, 0)) if reverse else (lambda i: (i, 0, 0))
    blk = _spec((SCAN_TILE, LRU_BLOCKS, LRU_BLOCK), idx)
    return _pcall(
        body, name=name, grid=(nt,),
        in_specs=[blk, blk], out_specs=blk,
        out_shape=_sds((T, LRU_BLOCKS, LRU_BLOCK), F32),
        scratch_shapes=[pltpu.VMEM((LRU_BLOCKS, LRU_BLOCK), F32)],
    )(a3, b3)


def _lru_gates_bwd(lmb, hl, a, r, i, xc, u, conv_w, w_r, w_i, lam):
    T = u.shape[0]
    col, vec, wblk, cw = _lru_specs(T)

    def body(l_ref, h_ref, a_ref, r_ref, i_ref, xc_ref, x_ref, cw_ref, wr_ref, wi_ref, lam_ref,
             dx_ref, dwr_ref, dwi_ref, dvec_ref, dcw_ref):
        l = l_ref[...]
        av, rv, iv, xc = a_ref[...], r_ref[...], i_ref[...], xc_ref[...]
        lam_v = lam_ref[...]
        sp = _softplus(-lam_v)
        la = (-LRU_C) * rv * sp
        mult = jnp.sqrt(-_expm1(2.0 * la))
        da = l * _shift_down(h_ref[...], 1)
        dmult = l * (iv * xc)
        di = l * mult * xc
        dxc = l * mult * iv
        dla = da * av - dmult * (av * av) / mult
        dzr = (dla * ((-LRU_C) * sp)) * rv * (1.0 - rv)
        dzi = di * iv * (1.0 - iv)
        dsp = jnp.sum(dla * ((-LRU_C) * rv), axis=0, keepdims=True)
        dlam = dsp * (-_sigmoid(-lam_v))
        dwr_ref[...] = _dot(xc, dzr, "tn")
        dwi_ref[...] = _dot(xc, dzi, "tn")
        dxc = dxc + _dot(dzr, wr_ref[...], "nt") + _dot(dzi, wi_ref[...], "nt")
        x = x_ref[...]
        w = cw_ref[...]
        dx = (dxc * w[3:4] + _shift_up(dxc, 1) * w[2:3] + _shift_up(dxc, 2) * w[1:2]
              + _shift_up(dxc, 3) * w[0:1])
        dx_ref[...] = dx.astype(BF16)
        dvec_ref[...] = jnp.concatenate(
            [jnp.sum(dzr, axis=0, keepdims=True), jnp.sum(dzi, axis=0, keepdims=True), dlam,
             jnp.sum(dxc, axis=0, keepdims=True)], axis=0)
        dcw_ref[...] = jnp.concatenate(
            [jnp.sum(dxc * _shift_down(x, 3 - tap), axis=0, keepdims=True) if tap < 3
             else jnp.sum(dxc * x, axis=0, keepdims=True) for tap in range(CONV_TAPS)], axis=0)

    c0 = col(0)
    return _pcall(
        body, name="lru_gates_bwd", grid=(LRU_BLOCKS,),
        in_specs=[c0, c0, c0, c0, c0, c0, col(24), cw, wblk, wblk, vec],
        out_specs=[c0, wblk, wblk, cw, cw],
        out_shape=[_sds((T, D), BF16), _sds((LRU_BLOCKS, LRU_BLOCK, LRU_BLOCK), F32),
                   _sds((LRU_BLOCKS, LRU_BLOCK, LRU_BLOCK), F32), _sds((4, D), F32), _sds((CONV_TAPS, D), F32)],
    )(lmb, hl, a, r, i, xc, u, conv_w, w_r, w_i, lam)


def _xattn_probs(q, k):
    sc = _dot(q, k, "nt") * (X_HD ** -0.5)
    e = jnp.exp(sc - jnp.max(sc, axis=-1, keepdims=True))
    return e / jnp.sum(e, axis=-1, keepdims=True)


def _xattn_fwd(xq, xk, xv):
    T = xq.shape[0]
    tq = min(WIDE_ROW_TILE, T)
    M = xk.shape[0]

    def body(q_ref, k_ref, v_ref, o_ref):
        p = _xattn_probs(q_ref[...], k_ref[...])
        o_ref[...] = _dot(p, v_ref[...], "nn").astype(BF16)

    qs = _spec((tq, X_HD), lambda h, i: (i, h))
    kv = _spec((M, X_HD), lambda h, i: (0, h))
    return _pcall(
        body, name="xattn_fwd", grid=(X_HEADS, T // tq),
        in_specs=[qs, kv, kv], out_specs=qs, out_shape=_sds((T, D), BF16),
    )(xq, xk, xv)


def _xattn_bwd(xq, xk, xv, dxo):
    T = xq.shape[0]
    tq = min(WIDE_ROW_TILE, T)
    M = xk.shape[0]

    def body(q_ref, k_ref, v_ref, do_ref, dq_ref, dk_ref, dv_ref):
        first = pl.program_id(1) == 0
        q, k, v, do = q_ref[...], k_ref[...], v_ref[...], do_ref[...]
        p = _xattn_probs(q, k)
        dp = _dot(do, v, "nt")
        ds = p * (dp - jnp.sum(dp * p, axis=-1, keepdims=True)) * (X_HD ** -0.5)
        dq_ref[...] = _dot(ds, k, "nn").astype(BF16)
        _accumulate(dk_ref, _dot(ds, q, "tn"), first)
        _accumulate(dv_ref, _dot(p, do, "tn"), first)

    qs = _spec((tq, X_HD), lambda h, i: (i, h))
    kv = _spec((M, X_HD), lambda h, i: (0, h))
    return _pcall(
        body, name="xattn_bwd", grid=(X_HEADS, T // tq),
        in_specs=[qs, kv, kv, qs], out_specs=[qs, kv, kv],
        out_shape=[_sds((T, D), BF16), _sds((M, D), F32), _sds((M, D), F32)],
    )(xq, xk, xv, dxo)


def _final_loss(x, g, tgt):
    T = x.shape[0]
    tm = ROW_TILE

    def fn(irefs, orefs, ids):
        xv, gv = irefs[0][...], irefs[1][...]
        err = _rms_fwd(xv, gv) - irefs[2][...]
        lp = 0.5 * jnp.sum(jnp.mean(err * err, axis=-1, keepdims=True), axis=0, keepdims=True)
        first = ids[0] == 0
        _accumulate(orefs[0], jnp.broadcast_to(lp, (1, 128)), first)
        dx, dgp = _rms_bwd(xv, gv, err * (1.0 / D))
        orefs[1][...] = dx
        _accumulate(orefs[2], dgp, first)

    row = _spec((tm, D), lambda i: (i, 0))
    vec = _spec((1, D), lambda i: (0, 0))
    return _rowwise(
        "final_loss", fn, [(x, row), (g, vec), (tgt, row)],
        [(_sds((1, 128), F32), _spec((1, 128), lambda i: (0, 0))), (_sds((T, D), F32), row),
         (_sds((1, D), F32), vec)],
        (T // tm,))


def _adamw(name, w, g, m, v):
    R, C = w.shape
    tr = R
    for cand in (512, 352, 256):
        if R % cand == 0:
            tr = cand
            break

    def fn(irefs, orefs, ids):
        delta, mn, vn = _adamw_update(*(r[...] for r in irefs))
        orefs[0][...] = delta
        orefs[1][...] = mn
        orefs[2][...] = vn

    blk = _spec((tr, C), lambda i: (i, 0))
    return _rowwise(name, fn, [(w, blk), (g, blk), (m, blk), (v, blk)],
                    [(_sds((R, C), F32), blk)] * 3, (R // tr,))


def _adamw_update(wv, gv, mv, vv):
    c1 = 1.0 - ADAM_B1 ** ADAM_STEP
    c2 = 1.0 - ADAM_B2 ** ADAM_STEP
    mn = ADAM_B1 * mv + (1.0 - ADAM_B1) * gv
    vn = ADAM_B2 * vv + (1.0 - ADAM_B2) * (gv * gv)
    delta = -ADAM_LR * ((mn / c1) / (jnp.sqrt(vn / c2) + ADAM_EPS) + ADAM_WD * wv)
    return delta, mn, vn


def _adamw_halves(name, w, mine, theirs, widx, m, v, core):
    R, C = w.shape
    H = R // 2
    tr = H
    while tr * C * 4 > (1 << 20) and tr % 16 == 0:
        tr //= 2
    nb = H // tr

    def body(core_ref, w_ref, mine_ref, theirs_ref, m_ref, v_ref, g_out, d_out, m_out, v_out):
        gv = jnp.where(pl.program_id(0) == core_ref[0], mine_ref[...], theirs_ref[...])
        delta, mn, vn = _adamw_update(w_ref[...], gv, m_ref[...], v_ref[...])
        g_out[...] = gv
        d_out[...] = delta
        m_out[...] = mn
        v_out[...] = vn

    full = pl.BlockSpec((tr, C), lambda h, i, core_ref: (h * nb + i, 0))
    mine_spec = pl.BlockSpec((None, tr, C), lambda h, i, core_ref: (widx, jnp.where(h == core_ref[0], i, 0), 0))
    theirs_spec = pl.BlockSpec((None, tr, C), lambda h, i, core_ref: (widx, jnp.where(h == core_ref[0], 0, i), 0))
    return _pcall(
        body, name=name, grid=(2, nb), num_prefetch=1,
        in_specs=[full, mine_spec, theirs_spec, full, full], out_specs=[full] * 4,
        out_shape=[_sds((R, C), F32)] * 4,
    )(core, w, mine, theirs, m, v)


def _rmsnorm(name, x, g):
    M = x.shape[0]
    tm = min(ROW_TILE, M)

    def fn(irefs, orefs, ids):
        orefs[0][...] = _rms_fwd(irefs[0][...], irefs[1][...]).astype(BF16)

    row = _spec((tm, D), lambda i: (i, 0))
    return _rowwise(name, fn, [(x, row), (g, _spec((1, D), lambda i: (0, 0)))],
                    [(_sds((M, D), BF16), row)], (M // tm,))[0]


WEIGHT_AT = {
    "ffn1_w1": ("col1", 0), "ffn1_w3": ("col1", 1), "ffn1_w2": ("row2a", 0),
    "w_ret_o": ("sqA", 0), "w_lru_o": ("sqA", 1), "w_out": ("sqA", 2),
    "w_xq": ("sqB", 0), "w_xk": ("sqB", 1), "w_xv": ("sqC", 0), "w_xo": ("sqC", 1),
    "ffn2_w1": ("col2a", 0), "ffn2_w3": ("col2b", 0), "ffn2_w2": ("row2b", 0),
}


def _local_step(x, mem, tgt, gw, sm, big):
    T = x.shape[0]
    tm = ROW_TILE

    def wt(name):
        key, idx = WEIGHT_AT[name]
        return gw[key], idx

    row3 = lambda i, j, r: (i, 0)
    vec3 = lambda i, j, r: (0, 0)
    rowD = _spec((tm, D), row3)
    vecD = _spec((1, D), vec3)

    def residual_norm(acc, erefs, orefs, ids):
        xo = erefs[0][...] + acc
        orefs[0][...] = xo
        orefs[1][...] = _rms_fwd(xo, erefs[1][...]).astype(BF16)

    def res_norm_io(x_res, g):
        return ([(x_res, rowD), (g, vecD)],
                [(_sds((T, D), F32), rowD), (_sds((T, D), BF16), rowD)])

    h1 = _rmsnorm("ffn1_norm", x, sm["ffn1_norm"])
    a1, b1, s1 = _ffn_up("ffn1_up", h1, *wt("ffn1_w1"), *wt("ffn1_w3"))
    x1, h2 = _ffn_down("ffn1_down", s1, *wt("ffn1_w2"), x, sm["mix_norm"])

    tw = min(WIDE_ROW_TILE, T)
    wideD = _spec((tw, D), row3)
    u = _gemm(
        "mix_in",
        [(h2, wideD, gw["win"], _spec((None, None, IN_BLK, D), lambda i, j, r: (j, 0, 0, 0)), "nt")],
        (T // tw, N_CHIPS, 1),
        [(_sds((T, 5120), F32), _spec((tw, IN_BLK), lambda i, j, r: (i, j)))], (tw, IN_BLK))[0]

    consts = _retention_constants(T)
    qr, kr, ret, yr, states = _ret_fwd(u, consts, sm["ret_gn"])

    conv_w = gw["conv"][:, 0].transpose(1, 0, 2).reshape(CONV_TAPS, D)
    xc, rg, ig, av, bx = _lru_gates_fwd(u, conv_w, sm["conv_b"], sm["w_rgate"], sm["b_rgate"],
                                        sm["w_igate"], sm["b_igate"], sm["lru_lambda"])
    a3 = av.reshape(T, LRU_BLOCKS, LRU_BLOCK)
    hl = _lru_scan("lru_scan_fwd", a3, bx.reshape(T, LRU_BLOCKS, LRU_BLOCK), False).reshape(T, D)

    row1 = _spec((tm, D), lambda i: (i, 0))
    glru1 = _spec((tm, D), lambda i: (i, 4))

    def lru_out(irefs, orefs, ids):
        gl, _ = _gelu_and_grad(irefs[1][...])
        orefs[0][...] = (irefs[0][...] * gl).astype(BF16)

    yl = _rowwise("lru_out", lru_out, [(hl, row1), (u, glru1)], [(_sds((T, D), BF16), row1)], (T // tm,))[0]

    def gate_epilogue(acc, erefs, orefs, ids):
        orefs[0][...] = _sigmoid(acc + erefs[0][...])

    gates = _gemm(
        "mix_gates",
        [(h2, wideD, gw["wbg"], _spec((None, None, BG_BLK, D), lambda i, j, r: (j, 0, 0, 0)), "nt")],
        (T // tw, N_CHIPS, 1),
        [(_sds((T, 2 * D), F32), _spec((tw, BG_BLK), lambda i, j, r: (i, j)))], (tw, BG_BLK),
        [(sm["b_branch_gate"], _spec((1, BG_BLK), lambda i, j, r: (0, j)))], gate_epilogue)[0]

    y_ret = _proj_sq("y_ret", yr, *wt("w_ret_o"), "nn")[0]

    def merge_epilogue(acc, erefs, orefs, ids):
        orefs[0][...] = acc
        orefs[1][...] = (erefs[0][...] * erefs[2][...] + erefs[1][...] * acc).astype(BF16)

    y_lru, merged = _proj_sq(
        "y_lru", yl, *wt("w_lru_o"), "nn",
        extras=[(gates, _spec((tm, D), lambda i, j, r: (i, 0))), (gates, _spec((tm, D), lambda i, j, r: (i, 1))),
                (y_ret, rowD)],
        epilogue=merge_epilogue,
        outs=[(_sds((T, D), F32), rowD), (_sds((T, D), BF16), rowD)])

    ex, ou = res_norm_io(x1, sm["xattn_norm"])
    x2, hq = _proj_sq("mix_out", merged, *wt("w_out"), "nn", extras=ex, epilogue=residual_norm, outs=ou)

    m = _rmsnorm("mem_norm", mem, sm["mem_norm"])
    xq = _proj_sq("xq", hq, *wt("w_xq"), "nn", BF16)[0]
    xk = _proj_sq("xk", m, *wt("w_xk"), "nn", BF16)[0]
    xv = _proj_sq("xv", m, *wt("w_xv"), "nn", BF16)[0]
    xo = _xattn_fwd(xq, xk, xv)
    ex, ou = res_norm_io(x2, sm["ffn2_norm"])
    x3, h3 = _proj_sq("xattn_out", xo, *wt("w_xo"), "nn", extras=ex, epilogue=residual_norm, outs=ou)

    a2, b2, s2 = _ffn_up("ffn2_up", h3, *wt("ffn2_w1"), *wt("ffn2_w3"))
    x4 = _ffn_down("ffn2_down", s2, *wt("ffn2_w2"), x3)[0]
    loss, dx4, dg_final = _final_loss(x4, sm["final_norm"], tgt)

    dx3, dg_ffn2 = _ffn_bwd("ffn2", dx4, h3, a2, b2, s2, *wt("ffn2_w1"), *wt("ffn2_w3"),
                            *wt("ffn2_w2"), x3, sm["ffn2_norm"], big)

    dxo = _proj_sq("d_xo", dx3, *wt("w_xo"), "nt", BF16)[0]
    big["w_xo"] = _dw_sq("dw_xo", xo, dx3)[None]
    dxq, dxk, dxv = _xattn_bwd(xq, xk, xv, dxo)
    big["w_xq"] = _dw_sq("dw_xq", hq, dxq)[None]
    ex, ou = _rms_bwd_io(x2, sm["xattn_norm"], dx3, T, tm)
    dx2, dg_xattn = _proj_sq("d_hq", dxq, *wt("w_xq"), "nt", extras=ex, epilogue=_rms_bwd_epilogue, outs=ou)
    big["w_xk"] = _dw_sq("dw_xk", m, dxk)[None]
    big["w_xv"] = _dw_sq("dw_xv", m, dxv)[None]

    M = mem.shape[0]

    def mem_norm_epilogue(acc, erefs, orefs, ids):
        _, dgp = _rms_bwd(erefs[0][...], erefs[1][...], acc)
        orefs[0][...] = dgp

    wsq_spec = lambda idx: _spec((N_CHIPS, None, SQ_BLK, D), lambda i, j, r: (0, idx, 0, 0))
    memD = _spec((M, D), row3)
    dg_mem = _gemm(
        "d_mem_norm",
        [(dxk, memD, wt("w_xk")[0], wsq_spec(wt("w_xk")[1]), "nt"),
         (dxv, memD, wt("w_xv")[0], wsq_spec(wt("w_xv")[1]), "nt")],
        (1, 1, 1), [(_sds((1, D), F32), vecD)], (M, D),
        [(mem, memD), (sm["mem_norm"], vecD)], mem_norm_epilogue)[0]

    def merged_bwd_epilogue(acc, erefs, orefs, ids):
        gr, gl, yrv, ylv = (e[...] for e in erefs)
        orefs[0][...] = (acc * gr).astype(BF16)
        orefs[1][...] = (acc * gl).astype(BF16)
        dgr = acc * yrv * gr * (1.0 - gr)
        dgl = acc * ylv * gl * (1.0 - gl)
        orefs[2][:, :D] = dgr.astype(BF16)
        orefs[2][:, D:] = dgl.astype(BF16)
        dbb = jnp.concatenate([jnp.sum(dgr, axis=0, keepdims=True), jnp.sum(dgl, axis=0, keepdims=True)], axis=1)
        _accumulate(orefs[3], dbb, ids[0] == 0)

    dy_ret, dy_lru, dgpre, db_bg = _proj_sq(
        "d_merged", dx2, *wt("w_out"), "nt",
        extras=[(gates, _spec((tm, D), lambda i, j, r: (i, 0))), (gates, _spec((tm, D), lambda i, j, r: (i, 1))),
                (y_ret, rowD), (y_lru, rowD)],
        epilogue=merged_bwd_epilogue,
        outs=[(_sds((T, D), BF16), rowD), (_sds((T, D), BF16), rowD),
              (_sds((T, 2 * D), BF16), _spec((tm, 2 * D), row3)),
              (_sds((1, 2 * D), F32), _spec((1, 2 * D), vec3))])
    big["w_branch_gate"] = _gemm(
        "dw_bg",
        [(h2, _spec((T, D), lambda j, n, r: (r, 0)), dgpre, _spec((T, BG_BLK), lambda j, n, r: (r, j)), "tn")],
        (N_CHIPS, 1, 1),
        [(_sds((N_CHIPS, D, BG_BLK), GRAD_WIRE_DTYPE), _spec((None, D, BG_BLK), lambda j, n, r: (j, 0, 0)))],
        (D, BG_BLK))[0][None]
    big["w_out"] = _dw_sq("dw_out", merged, dx2)[None]
    dyr = _proj_sq("d_yr", dy_ret, *wt("w_ret_o"), "nt")[0]
    big["w_ret_o"] = _dw_sq("dw_ret_o", yr, dy_ret)[None]
    dyl = _proj_sq("d_yl", dy_lru, *wt("w_lru_o"), "nt")[0]
    big["w_lru_o"] = _dw_sq("dw_lru_o", yl, dy_lru)[None]

    dq, dk, dv, dgr, dg_retgn = _ret_bwd(dyr, ret, u, qr, kr, states, consts, sm["ret_gn"])

    def lru_out_bwd(irefs, orefs, ids):
        gl, dgl = _gelu_and_grad(irefs[2][...])
        dyl_v = irefs[0][...]
        orefs[0][...] = dyl_v * gl
        orefs[1][...] = (dyl_v * irefs[1][...] * dgl).astype(BF16)

    dhl, dglru = _rowwise("lru_out_bwd", lru_out_bwd, [(dyl, row1), (hl, row1), (u, glru1)],
                          [(_sds((T, D), F32), row1), (_sds((T, D), BF16), row1)], (T // tm,))
    lmb = _lru_scan("lru_scan_bwd", a3, dhl.reshape(T, LRU_BLOCKS, LRU_BLOCK), True).reshape(T, D)
    dxl, dw_r, dw_i, dvec, dcw = _lru_gates_bwd(lmb, hl, av, rg, ig, xc, u, conv_w,
                                                sm["w_rgate"], sm["w_igate"], sm["lru_lambda"])

    du = jnp.concatenate([dq, dk, dv, dgr, dxl, dglru], axis=1)
    tk = T
    big["w_in"] = _gemm(
        "dw_in",
        [(h2, _spec((tk, D), lambda j, n, r: (r, 0)), du, _spec((tk, IN_BLK), lambda j, n, r: (r, j)), "tn")],
        (N_CHIPS, 1, T // tk),
        [(_sds((N_CHIPS, D, IN_BLK), GRAD_WIRE_DTYPE), _spec((None, D, IN_BLK), lambda j, n, r: (j, 0, 0)))],
        (D, IN_BLK))[0][None]
    tf = min(FFN_ROW_TILE, T)
    ex, ou = _rms_bwd_io(x1, sm["mix_norm"], dx2, T, tf)
    dx1, dg_mix = _gemm(
        "d_h2",
        [(du, _spec((tf, 5120), row3), gw["win"], _spec((N_CHIPS, None, IN_BLK, D), lambda i, j, r: (0, 0, 0, 0)), "nn"),
         (dgpre, _spec((tf, 2 * D), row3), gw["wbg"], _spec((N_CHIPS, None, BG_BLK, D), lambda i, j, r: (0, 0, 0, 0)),
          "nn")],
        (T // tf, 1, 1), ou, (tf, D), ex, _rms_bwd_epilogue)

    grad_x, dg_ffn1 = _ffn_bwd("ffn1", dx1, h1, a1, b1, s1, *wt("ffn1_w1"), *wt("ffn1_w3"),
                               *wt("ffn1_w2"), x, sm["ffn1_norm"], big)

    small = {
        "ffn1_norm": dg_ffn1, "mix_norm": dg_mix, "ret_gn": dg_retgn, "conv_b": dvec[3:4],
        "b_rgate": dvec[0:1], "b_igate": dvec[1:2], "lru_lambda": dvec[2:3], "xattn_norm": dg_xattn,
        "mem_norm": dg_mem, "ffn2_norm": dg_ffn2, "final_norm": dg_final, "b_branch_gate": db_bg,
        "conv_w": dcw, "w_rgate": dw_r, "w_igate": dw_i,
    }
    return loss, grad_x, small


ANY_SPEC = pl.BlockSpec(memory_space=pl.ANY)
VMEM_SPEC = pl.BlockSpec(memory_space=pltpu.VMEM)
N_PEER_CHIPS = N_CHIPS - 1


def _mesh_position():
    x, y, c = lax.axis_index("x"), lax.axis_index("y"), lax.axis_index("c")
    chips = [(1 - x, y), (x, 1 - y), (1 - x, 1 - y)]
    return x, y, c, chips


def _chip_index(x, y):
    return 2 * x + y


def _rows_half(ref, axis, h):
    n = ref.shape[axis] // 2
    idx = [slice(None)] * len(ref.shape)
    idx[axis] = pl.ds(pl.multiple_of(h * n, 16), n)
    return ref.at[tuple(idx)]


def _remote(src, dst, send_sem, recv_sem, device):
    return pltpu.make_async_remote_copy(src_ref=src, dst_ref=dst, send_sem=send_sem, recv_sem=recv_sem,
                                        device_id=device, device_id_type=MESH)


def _gather_chips_task(shards, split, landed, part=0, nparts=1):
    keys = list(shards)
    n = len(keys)

    def operands():
        if part:
            return [landed[k] for k in keys]
        chip_me = _chip_index(lax.axis_index("x"), lax.axis_index("y"))
        return [lax.dynamic_update_slice(lax.empty((N_CHIPS,) + shards[k].shape, shards[k].dtype), shards[k][None],
                                         (chip_me,) + (0,) * shards[k].ndim) for k in keys]

    def my_rows(ref, c):
        rows = ref.shape[1] // (2 * nparts)
        return ref.at[:, pl.ds(pl.multiple_of((c * nparts + part) * rows, 16), rows), :]

    def make_direct(ins, outs, send_sem, recv_sem):
        x, y, c, chips = _mesh_position()
        s_me = _chip_index(x, y)
        starts, arrivals = [], []
        for g in range(n):
            for k, chip in enumerate(chips):
                sems = (send_sem(3 * g + k), recv_sem(3 * g + k))
                starts.append(functools.partial(_remote, outs[g].at[s_me], outs[g].at[s_me], *sems, (*chip, c)))
                got = outs[g].at[_chip_index(*chip)]
                arrivals.append(functools.partial(_remote, got, got, *sems, (*chip, c)))
        return starts, arrivals

    def axis_neighbours(x, y, c):
        flip = lambda v, f: v + f * (1 - 2 * v)
        return (flip(x, 1 - c), flip(y, c)), (flip(x, c), flip(y, 1 - c))

    def make_swap(ins, outs, send_sem, recv_sem):
        x, y, c, _ = _mesh_position()
        first, _ = axis_neighbours(x, y, c)
        starts, arrivals = [], []
        for g in range(n):
            sems = (send_sem(3 * g), recv_sem(3 * g))
            mine = my_rows(outs[g].at[_chip_index(x, y)], c)
            starts.append(functools.partial(_remote, mine, mine, *sems, (*first, c)))
            got = my_rows(outs[g].at[_chip_index(*first)], c)
            arrivals.append(functools.partial(_remote, got, got, *sems, (*first, c)))
        return starts, arrivals

    def make_pass_on(ins, outs, send_sem, recv_sem):
        x, y, c, _ = _mesh_position()
        first, second = axis_neighbours(x, y, c)
        diagonal = (1 - x, 1 - y)
        starts, arrivals = [], []
        for g in range(n):
            half = lambda chip: my_rows(outs[g].at[_chip_index(*chip)], c)
            for k, (sent, arriving) in enumerate([((x, y), second), (first, diagonal)]):
                sems = (send_sem(3 * g + 1 + k), recv_sem(3 * g + 1 + k))
                starts.append(functools.partial(_remote, half(sent), half(sent), *sems, (*second, c)))
                arrivals.append(functools.partial(_remote, half(arriving), half(arriving), *sems, (*second, c)))
        return starts, arrivals

    def finish(res):
        landed.update(zip(keys, res))

    shapes = lambda: [_sds((N_CHIPS,) + shards[k].shape, shards[k].dtype) for k in keys]
    aliases = {g: g for g in range(n)}
    if not split:
        return _Task("chips", operands, shapes, aliases, 3 * n, make_direct, finish)
    return _Task("neighbours", operands, shapes, aliases, 3 * n, make_swap, finish, make_second=make_pass_on)


def _gather_sibling_task(keys, landed, ready):
    n = len(keys)

    def make(ins, outs, send_sem, recv_sem):
        x, y, c, chips = _mesh_position()
        starts, arrivals = [], []
        for g in range(n):
            for k, chip in enumerate(chips):
                o = outs[g].at[_chip_index(*chip)]
                got, other = _rows_half(o, 1, c), _rows_half(o, 1, 1 - c)
                starts.append(functools.partial(_remote, got, got, send_sem(3 * g + k), recv_sem(3 * g + k),
                                                (x, y, 1 - c)))
                arrivals.append(functools.partial(_remote, other, other, send_sem(3 * g + k), recv_sem(3 * g + k),
                                                  (x, y, 1 - c)))
        return starts, arrivals

    def finish(res):
        ready.update(zip(keys, res))

    return _Task("sibling", lambda: [landed[k] for k in keys],
                 lambda: [_sds(landed[k].shape, landed[k].dtype) for k in keys],
                 {g: g for g in range(n)}, 3 * n, make, finish)


def _pair_swap_task(names, big, got):
    n = len(names)

    def make(ins, outs, send_sem, recv_sem):
        x, y, c, _ = _mesh_position()
        copies = [functools.partial(_remote, _rows_half(ins[a], 2, 1 - c), outs[a], send_sem(a), recv_sem(a),
                                    (x, y, 1 - c)) for a in range(n)]
        return copies, copies

    def shapes():
        return [_sds(big[k].shape[:2] + (big[k].shape[2] // 2, big[k].shape[3]), big[k].dtype) for k in names]

    return _Task("sibling", lambda: [big[k] for k in names], shapes, {}, n, make,
                 lambda res: got.update(zip(names, res)))


def _rs_pair_sum(name, fulls, gots, core):
    n = len(fulls)
    shapes = [(f.shape[2] // 2, f.shape[3]) for f in fulls]

    def body(core_ref, *refs):
        for a_ref, b_ref, o_ref in zip(refs[:n], refs[n:2 * n], refs[2 * n:]):
            o_ref[...] = (a_ref[...].astype(F32) + b_ref[...].astype(F32)).astype(BF16)

    mine = [pl.BlockSpec((None, None) + hc, lambda s, core_ref: (0, s, core_ref[0], 0)) for hc in shapes]
    slot = [pl.BlockSpec((None, None) + hc, lambda s, core_ref: (0, s, 0, 0)) for hc in shapes]
    return _pcall(
        body, name=name, grid=(N_CHIPS,), num_prefetch=1,
        in_specs=mine + slot, out_specs=slot,
        out_shape=[_sds((1, N_CHIPS) + hc, BF16) for hc in shapes],
    )(core, *fulls, *gots)


def _chip_exchange_task(names, pair_sums, by_source, part=0, nparts=1):
    n = len(names)

    def rows(ref):
        h = ref.shape[1] // nparts
        return ref.at[:, pl.ds(part * h, h), :]

    def make(ins, outs, send_sem, recv_sem):
        x, y, c, chips = _mesh_position()
        s_me = _chip_index(x, y)
        starts, arrivals = [], []
        for a in range(n):
            for k, chip in enumerate(chips):
                s_k = _chip_index(*chip)
                starts.append(functools.partial(_remote, rows(ins[a].at[:, s_k]), rows(outs[a].at[:, s_me]),
                                                send_sem(3 * a + k), recv_sem(3 * a + k), (*chip, c)))
                got = rows(outs[a].at[:, s_k])
                arrivals.append(functools.partial(_remote, got, got, send_sem(3 * a + k), recv_sem(3 * a + k),
                                                  (*chip, c)))
        return starts, arrivals

    def operands():
        return [pair_sums[k] for k in names] + ([by_source[k] for k in names] if part else [])

    return _Task("chips", operands, lambda: [_sds(pair_sums[k].shape, pair_sums[k].dtype) for k in names],
                 {n + a: a for a in range(n)} if part else {}, 3 * n, make,
                 lambda res: by_source.update(zip(names, res)))


def _rs_chip_sum(name, owns, parts, chip):
    n = len(owns)
    ns = N_CHIPS
    shapes = [p.shape[2:] for p in parts]

    def body(chip_ref, *refs):
        me = chip_ref[0]
        for i in range(n):
            own_v = refs[i][...].astype(F32)
            slots = refs[n + ns * i:n + ns * (i + 1)]
            tot = None
            for s in range(ns):
                term = jnp.where(me == s, own_v, slots[s][...].astype(F32))
                tot = term if tot is None else tot + term
            refs[n + ns * n + i][...] = tot

    def slot_spec(hc, s):
        return pl.BlockSpec((None, None) + hc,
                            lambda g, chip_ref: (0, jnp.where(chip_ref[0] == s, (s + 1) % ns, s), 0, 0))

    own_specs = [pl.BlockSpec((None, None) + hc, lambda g, chip_ref: (0, chip_ref[0], 0, 0)) for hc in shapes]
    slot_specs = [slot_spec(hc, s) for hc in shapes for s in range(ns)]
    return _pcall(
        body, name=name, grid=(1,), num_prefetch=1,
        in_specs=own_specs + slot_specs,
        out_specs=[pl.BlockSpec((None,) + hc, lambda g, chip_ref: (0, 0, 0)) for hc in shapes],
        out_shape=[_sds((1,) + hc, F32) for hc in shapes],
    )(chip, *owns, *[p for p in parts for _ in range(ns)])


def _pair_gather_task(names, halves, sibling_halves):
    n = len(names)

    def make(ins, outs, send_sem, recv_sem):
        x, y, c, _ = _mesh_position()
        copies = [functools.partial(_remote, ins[a], outs[a], send_sem(a), recv_sem(a), (x, y, 1 - c))
                  for a in range(n)]
        return copies, copies

    return _Task("sibling", lambda: [halves[k] for k in names], lambda: [_sds(halves[k].shape, F32) for k in names],
                 {}, n, make, lambda res: sibling_halves.update(zip(names, res)))


def _small_allreduce(arrs):
    n = len(arrs)
    per = 1 + 2 * N_PEER_CHIPS

    def body(*refs):
        v_refs, o_refs = refs[:n], refs[n:2 * n]
        sib, pair, part = refs[2 * n:3 * n], refs[3 * n:4 * n], refs[4 * n:5 * n]
        send_sems, recv_sems = refs[5 * n:]
        x, y, c, chips = _mesh_position()
        s_me = _chip_index(x, y)

        def quarter(ref, s):
            q = ref.shape[0] // N_CHIPS
            return ref.at[pl.ds(pl.multiple_of(s * q, 8), q)]

        def exchange(first_sem, src, dst_of, arrival_of):
            sems = lambda a, k: (send_sems.at[a * per + first_sem + k], recv_sems.at[a * per + first_sem + k])
            sends = [_remote(src(a, _chip_index(*chip)), dst_of(a, s_me), *sems(a, k), (*chip, c))
                     for a in range(n) for k, chip in enumerate(chips)]
            for cp in sends:
                cp.start()
            for a in range(n):
                for k, chip in enumerate(chips):
                    got = arrival_of(a, _chip_index(*chip))
                    _remote(got, got, *sems(a, k), (*chip, c)).wait_recv()
            for cp in sends:
                cp.wait_send()

        swaps = [_remote(v_refs[a], sib[a], send_sems.at[a * per], recv_sems.at[a * per], (x, y, 1 - c))
                 for a in range(n)]
        for cp in swaps:
            cp.start()
        for cp in swaps:
            cp.wait()
        for a in range(n):
            pair[a][...] = v_refs[a][...] + sib[a][...]
        exchange(1, lambda a, s_k: quarter(pair[a], s_k), lambda a, s: part[a].at[s], lambda a, s_k: part[a].at[s_k])
        for a in range(n):
            part[a][s_me] = quarter(pair[a], s_me)[...]
            q = o_refs[a].shape[0] // N_CHIPS
            o_refs[a][pl.ds(pl.multiple_of(s_me * q, 8), q), :] = (
                ((part[a][0] + part[a][1]) + part[a][2]) + part[a][3])
        exchange(1 + N_PEER_CHIPS, lambda a, s_k: quarter(o_refs[a], s_me), lambda a, s: quarter(o_refs[a], s),
                 lambda a, s_k: quarter(o_refs[a], s_k))

    shapes = [a.shape for a in arrs]
    return _pcall(
        body, name="small_allreduce", grid=(1,), own_peers=("sibling", "chips"),
        in_specs=[VMEM_SPEC] * n, out_specs=[VMEM_SPEC] * n, out_shape=[_sds(s, F32) for s in shapes],
        scratch_shapes=([pltpu.VMEM(s, F32) for s in shapes] * 2
                        + [pltpu.VMEM((N_CHIPS, s[0] // N_CHIPS, s[1]), F32) for s in shapes]
                        + [pltpu.SemaphoreType.DMA((n * per,)), pltpu.SemaphoreType.DMA((n * per,))]),
    )(*arrs)


TRANSPOSED_WEIGHTS = ("ffn1_w1", "ffn1_w3", "ffn2_w1", "ffn2_w3")
SMALL_LAYOUT = [("ffn1_norm", 1), ("mix_norm", 1), ("ret_gn", 1), ("conv_b", 1), ("b_rgate", 1), ("b_igate", 1),
                ("lru_lambda", 1), ("xattn_norm", 1), ("mem_norm", 1), ("ffn2_norm", 1), ("final_norm", 1),
                ("b_branch_gate", 2), ("conv_w", CONV_TAPS)]
SMALL_ROWS = 32
GATE_WEIGHTS = ("w_rgate", "w_igate")
WEIGHT_ORDER = ["ffn1_norm", "ffn1_w1", "ffn1_w3", "ffn1_w2", "mix_norm", "w_in", "ret_gn", "w_ret_o", "conv_w",
                "conv_b", "w_rgate", "b_rgate", "w_igate", "b_igate", "lru_lambda", "w_lru_o", "w_branch_gate",
                "b_branch_gate", "w_out", "xattn_norm", "mem_norm", "w_xq", "w_xk", "w_xv", "w_xo", "ffn2_norm",
                "ffn2_w1", "ffn2_w3", "ffn2_w2", "final_norm"]


SMALL_USED_ROWS = sum(n for _, n in SMALL_LAYOUT)


def _pack_small(parts, extra_row=None):
    rows = [parts[name].reshape(n, D) for name, n in SMALL_LAYOUT]
    if extra_row is not None:
        rows.append(extra_row)
    rows.append(jnp.zeros((SMALL_ROWS - sum(r.shape[0] for r in rows), D), F32))
    return jnp.concatenate(rows, axis=0)


def _unpack_small(packed, shapes):
    out, r = {}, 0
    for name, n in SMALL_LAYOUT:
        out[name] = packed[r:r + n].reshape(shapes[name])
        r += n
    return out


def kernel(x, mem, ffn1_norm, ffn1_w1, ffn1_w3, ffn1_w2, mix_norm, w_in, ret_gn, w_ret_o, conv_w, conv_b, w_rgate, b_rgate, w_igate, b_igate, lru_lambda, w_lru_o, w_branch_gate, b_branch_gate, w_out, xattn_norm, mem_norm, w_xq, w_xk, w_xv, w_xo, ffn2_norm, ffn2_w1, ffn2_w3, ffn2_w2, final_norm, loss_target, m_ffn1_norm, m_ffn1_w1, m_ffn1_w3, m_ffn1_w2, m_mix_norm, m_w_in, m_ret_gn, m_w_ret_o, m_conv_w, m_conv_b, m_w_rgate, m_b_rgate, m_w_igate, m_b_igate, m_lru_lambda, m_w_lru_o, m_w_branch_gate, m_b_branch_gate, m_w_out, m_xattn_norm, m_mem_norm, m_w_xq, m_w_xk, m_w_xv, m_w_xo, m_ffn2_norm, m_ffn2_w1, m_ffn2_w3, m_ffn2_w2, m_final_norm, v_ffn1_norm, v_ffn1_w1, v_ffn1_w3, v_ffn1_w2, v_mix_norm, v_w_in, v_ret_gn, v_w_ret_o, v_conv_w, v_conv_b, v_w_rgate, v_b_rgate, v_w_igate, v_b_igate, v_lru_lambda, v_w_lru_o, v_w_branch_gate, v_b_branch_gate, v_w_out, v_xattn_norm, v_mem_norm, v_w_xq, v_w_xk, v_w_xv, v_w_xo, v_ffn2_norm, v_ffn2_w1, v_ffn2_w3, v_ffn2_w2, v_final_norm):
    given = dict(locals())
    w = {n: given[n] for n in WEIGHT_ORDER}
    mom = {n: given["m_" + n] for n in WEIGHT_ORDER}
    var = {n: given["v_" + n] for n in WEIGHT_ORDER}
    chip = _chip_index(lax.axis_index("x"), lax.axis_index("y"))
    core = lax.axis_index("c").astype(jnp.int32).reshape(1)

    chip_id = chip.astype(jnp.int32).reshape(1)
    sm = {n: w[n] for n in ["ffn1_norm", "mix_norm", "ret_gn", "conv_b", "b_rgate", "b_igate", "lru_lambda",
                            "xattn_norm", "mem_norm", "ffn2_norm", "b_branch_gate"]}
    sm["final_norm"] = w["final_norm"].reshape(1, D)
    sm["w_rgate"] = w["w_rgate"][0]
    sm["w_igate"] = w["w_igate"][0]

    local = lambda a, n: jnp.swapaxes(a[0], 0, 1) if n in TRANSPOSED_WEIGHTS else a[0]
    stack = lambda names: jnp.stack([local(w[n], n) for n in names], axis=0).astype(BF16)
    shard = {"col1": stack(["ffn1_w1", "ffn1_w3"]), "row2a": stack(["ffn1_w2"]),
             "win": jnp.swapaxes(w["w_in"], 1, 2).astype(BF16),
             "wbg": jnp.swapaxes(w["w_branch_gate"], 1, 2).astype(BF16),
             "sqA": stack(["w_ret_o", "w_lru_o", "w_out"]), "sqB": stack(["w_xq", "w_xk"]),
             "sqC": stack(["w_xv", "w_xo"]), "col2a": stack(["ffn2_w1"]), "col2b": stack(["ffn2_w3"]),
             "row2b": stack(["ffn2_w2"]), "conv": w["conv_w"]}
    gw, landed = {}, {}
    over_chips = lambda keys: _gather_chips_task({k: shard[k] for k in keys}, True, landed)
    to_sibling = lambda keys: _gather_sibling_task(keys, landed, gw)

    big, got, pair_sums, by_source, halves, sibling_halves, outs = {}, {}, {}, {}, {}, {}, {}
    pair_swap = lambda names: _pair_swap_task(names, big, got)
    exchange = lambda names, part=0, nparts=1: _chip_exchange_task(names, pair_sums, by_source, part, nparts)
    pair_gather = lambda names: _pair_gather_task(names, halves, sibling_halves)

    def pair_sum(names):
        res = _rs_pair_sum("rs_pair_sum_" + names[0], [big[n] for n in names], [got[n] for n in names], core)
        pair_sums.update(zip(names, res))

    def chip_sum(names):
        res = _rs_chip_sum("rs_chip_sum_" + names[0], [pair_sums[n] for n in names], [by_source[n] for n in names],
                           chip_id)
        halves.update(zip(names, res))

    def adamw(names):
        for n in names:
            res = _adamw_halves("adamw_" + n, local(w[n], n), halves[n], sibling_halves[n], 0, local(mom[n], n),
                                local(var[n], n), core)
            outs[n] = tuple((jnp.swapaxes(r, 0, 1) if n in TRANSPOSED_WEIGHTS else r)[None] for r in res)

    do = lambda fn, names: functools.partial(fn, names)
    ffn2_grads = ["ffn2_w2", "ffn2_w1", "ffn2_w3"]
    xattn_grads = ["w_xo", "w_xq", "w_xk", "w_xv"]
    mix_out_grads = ["w_branch_gate", "w_out", "w_ret_o", "w_lru_o"]
    conv_gather = _gather_chips_task({"conv": shard["conv"]}, False, gw)
    half = lambda key, part: _gather_chips_task({key: shard[key]}, True, landed, part, 2)
    plan = _Plan()
    plan.tasks = {
        "ag_first_chips": [over_chips(["col1", "row2a"])],
        "ag_first_sibling": [to_sibling(["col1", "row2a"])],
        "ffn1_up": [over_chips(["win"])],
        "ffn1_down": [to_sibling(["win"]), over_chips(["wbg"]), conv_gather],
        "mix_in": [to_sibling(["wbg"]), over_chips(["sqA"])],
        "ret_fwd": [to_sibling(["sqA"]), over_chips(["col2a"])],
        "lru_gates_fwd": [to_sibling(["col2a"]), over_chips(["sqB"])],
        "lru_scan_fwd": [to_sibling(["sqB"]), over_chips(["sqC"])],
        "mix_gates": [to_sibling(["sqC"]), half("col2b", 0)],
        "y_lru": [half("col2b", 1)],
        "xattn_fwd": [to_sibling(["col2b"])],
        "ffn2_up": [over_chips(["row2b"])],
        "ffn2_up_sibling": [to_sibling(["row2b"])],
        "ffn2_dh": [pair_swap(ffn2_grads)],
        "xattn_bwd": [exchange(["ffn2_w2"], 0, 2)],
        "d_hq": [exchange(["ffn2_w2"], 1, 2)],
        "d_merged": [exchange(["ffn2_w1"], 0, 2), pair_swap(xattn_grads)],
        "lru_out_bwd": [exchange(["w_xo"])],
        "ret_bwd": [exchange(["ffn2_w1"], 1, 2), exchange(["ffn2_w3"], 0, 2), pair_swap(mix_out_grads)],
        "lru_scan_bwd": [exchange(["ffn2_w3"], 1, 2)],
        "lru_gates_bwd": [exchange(["w_xq", "w_xk"]), pair_gather(ffn2_grads)],
        "dw_in": [exchange(["w_xv", "w_out"])],
        "d_h2": [exchange(["w_branch_gate", "w_ret_o", "w_lru_o"]), pair_swap(["w_in"]), pair_gather(xattn_grads)],
        "ffn1_bwd_mid": [exchange(["w_in"], 0, 2), pair_gather(mix_out_grads)],
        "ffn1_dw2": [exchange(["w_in"], 2, 4)],
        "ffn1_dw1": [exchange(["w_in"], 3, 4), pair_swap(["ffn1_w2"])],
        "ffn1_dw3": [exchange(["ffn1_w2"], 0, 2), pair_swap(["ffn1_w1"]), pair_gather(["w_in"])],
        "ffn1_dh": [exchange(["ffn1_w2"], 1, 2), exchange(["ffn1_w1"]), pair_swap(["ffn1_w3"])],
        "small_allreduce": [exchange(["ffn1_w3"]), pair_gather(["ffn1_w2"])],
        "rs_last_gather": [pair_gather(["ffn1_w1", "ffn1_w3"])],
    }
    plan.after = {
        "ffn2_up": [functools.partial(_comm_call, "ffn2_up_sibling")],
        "ffn2_dh": [do(pair_sum, ffn2_grads)],
        "d_merged": [do(pair_sum, xattn_grads)],
        "ret_bwd": [do(pair_sum, mix_out_grads)],
        "lru_scan_bwd": [do(chip_sum, ffn2_grads)],
        "lru_gates_bwd": [do(adamw, ffn2_grads)],
        "dw_in": [do(chip_sum, xattn_grads)],
        "d_h2": [do(chip_sum, mix_out_grads), do(pair_sum, ["w_in"]), do(adamw, xattn_grads)],
        "ffn1_bwd_mid": [do(adamw, mix_out_grads)],
        "ffn1_dw1": [do(chip_sum, ["w_in"]), do(pair_sum, ["ffn1_w2"])],
        "ffn1_dw3": [do(pair_sum, ["ffn1_w1"]), do(adamw, ["w_in"])],
        "ffn1_dh": [do(pair_sum, ["ffn1_w3"]), do(chip_sum, ["ffn1_w2"])],
        "small_allreduce": [do(chip_sum, ["ffn1_w1", "ffn1_w3"]), functools.partial(_comm_call, "rs_last_gather"),
                    do(adamw, ["ffn1_w2", "ffn1_w1", "ffn1_w3"])],
    }
    global _plan
    _plan = plan
    try:
        _comm_call("ag_first_chips")
        _comm_call("ag_first_sibling")
        loss_part, grad_x, small = _local_step(x[0], mem[0], loss_target[0], gw, sm, big)
        gate2d = lambda a: a.reshape(LRU_BLOCKS * LRU_BLOCK, LRU_BLOCK)
        loss_row = jnp.pad(loss_part, ((0, 0), (0, D - loss_part.shape[1])))
        small_sum, *gate_sums = _small_allreduce([_pack_small(small, loss_row)]
                                                 + [gate2d(small[n]) for n in GATE_WEIGHTS])
    finally:
        _plan = None
    assert not plan.tasks and not plan.after, (list(plan.tasks), list(plan.after))
    loss = small_sum[SMALL_USED_ROWS, 0]

    small_shapes = {n: w[n].shape for n, _ in SMALL_LAYOUT}
    small_shapes["conv_w"] = (CONV_TAPS, D)
    conv_row = SMALL_USED_ROWS - CONV_TAPS
    conv_grad = lax.dynamic_slice(small_sum[conv_row:conv_row + CONV_TAPS], (0, chip * SQ_BLK), (CONV_TAPS, SQ_BLK))
    small_w = {n: w[n] for n, _ in SMALL_LAYOUT}
    small_m = {n: mom[n] for n, _ in SMALL_LAYOUT}
    small_v = {n: var[n] for n, _ in SMALL_LAYOUT}
    pad_cols = lambda a: jnp.pad(a[0], ((0, 0), (0, D - SQ_BLK)))
    for dct in (small_w, small_m, small_v):
        dct["conv_w"] = pad_cols(dct["conv_w"])
    g_pack = lax.dynamic_update_slice(small_sum, jnp.pad(conv_grad, ((0, 0), (0, D - SQ_BLK))), (conv_row, 0))
    d_pack, m_pack, v_pack = _adamw("adamw_small", _pack_small(small_w), g_pack, _pack_small(small_m),
                                    _pack_small(small_v))
    unpacked = [_unpack_small(p, small_shapes) for p in (g_pack, d_pack, m_pack, v_pack)]
    for n, _ in SMALL_LAYOUT:
        if n == "conv_w":
            outs[n] = tuple(u[n][:, :SQ_BLK][None] for u in unpacked)
        else:
            outs[n] = tuple(u[n] for u in unpacked)
    for n, gsum in zip(GATE_WEIGHTS, gate_sums):
        d, nm, nv = _adamw("adamw_" + n, gate2d(w[n]), gsum, gate2d(mom[n]), gate2d(var[n]))
        outs[n] = tuple(r.reshape(w[n].shape) for r in (gsum, d, nm, nv))

    result = [loss, grad_x[None]]
    for k in range(4):
        result += [outs[n][k] for n in WEIGHT_ORDER]
    return tuple(result)
```

```python
import functools
import math

import jax
import jax.numpy as jnp
from jax import lax
from jax.experimental import pallas as pl
from jax.experimental.pallas import tpu as pltpu

F32 = jnp.float32
BF16 = jnp.bfloat16
GRAD_WIRE_DTYPE = BF16
MESH = pl.DeviceIdType.MESH

D = 1024
EPS = 1e-6
RET_HEADS = 4
RET_DK = 128
RET_DV = 256
CHUNK = 128
ROPE_BASE = 10000.0
LRU_BLOCKS = 8
LRU_BLOCK = 128
CONV_TAPS = 4
LRU_C = 8.0
D_FF = 2816
X_HEADS = 4
X_HD = 256
N_CHIPS = 4
FF_BLK = D_FF // N_CHIPS
IN_BLK = 5120 // N_CHIPS
BG_BLK = 2048 // N_CHIPS
SQ_BLK = D // N_CHIPS

ADAM_LR = 0.001
ADAM_B1 = 0.9
ADAM_B2 = 0.999
ADAM_EPS = 1e-08
ADAM_WD = 0.01
ADAM_STEP = 10

VMEM_LIMIT_BYTES = 56 * 1024 * 1024
ROW_TILE = 512
WIDE_ROW_TILE = 1024
FFN_ROW_TILE = 256
DW_BLK = D_FF // 2
SCAN_TILE = 256
RET_STEP_CHUNKS = 2
RET_STEP_ROWS = RET_STEP_CHUNKS * CHUNK

_DN = {
    "nn": (((1,), (0,)), ((), ())),
    "nt": (((1,), (1,)), ((), ())),
    "tn": (((0,), (0,)), ((), ())),
}


def _cparams(n_axes, collective_id=None):
    return pltpu.CompilerParams(dimension_semantics=("arbitrary",) * n_axes,
                                vmem_limit_bytes=VMEM_LIMIT_BYTES, collective_id=collective_id)


def _dot(a, b, kind):
    if b.ndim == 3:
        b = b.reshape(b.shape[0] * b.shape[1], b.shape[2])
    return lax.dot_general(a.astype(BF16), b.astype(BF16), _DN[kind], preferred_element_type=F32)


def _sigmoid(x):
    return 1.0 / (1.0 + jnp.exp(-x))


def _log1p_pos(e):
    u = 1.0 + e
    return jnp.where(u == 1.0, e, jnp.log(u) * (e / jnp.where(u == 1.0, 1.0, u - 1.0)))


def _expm1(x):
    u = jnp.exp(x)
    lu = jnp.log(u)
    safe = jnp.where(lu == 0.0, 1.0, lu)
    return jnp.where(u == 1.0, x, (u - 1.0) * (x / safe))


def _softplus(z):
    return jnp.maximum(z, 0.0) + _log1p_pos(jnp.exp(-jnp.abs(z)))


_GELU_C = math.sqrt(2.0 / math.pi)


def _gelu_and_grad(x):
    x2 = x * x
    t = jnp.tanh(_GELU_C * (x + 0.044715 * x * x2))
    g = 0.5 * x * (1.0 + t)
    dg = 0.5 * (1.0 + t) + 0.5 * x * (1.0 - t * t) * (_GELU_C * (1.0 + 3.0 * 0.044715 * x2))
    return g, dg


def _rms_fwd(x, g):
    r = lax.rsqrt(jnp.mean(x * x, axis=-1, keepdims=True) + EPS)
    return (x * r) * g


def _rms_bwd(x, g, dh):
    r = lax.rsqrt(jnp.mean(x * x, axis=-1, keepdims=True) + EPS)
    n = x * r
    dyg = dh * g
    dx = r * (dyg - n * jnp.mean(dyg * n, axis=-1, keepdims=True))
    return dx, jnp.sum(dh * n, axis=0, keepdims=True)


def _accumulate(ref, val, first):
    @pl.when(first)
    def _():
        ref[...] = val

    @pl.when(jnp.logical_not(first))
    def _():
        ref[...] += val


def _sds(shape, dtype):
    return jax.ShapeDtypeStruct(tuple(shape), dtype)


def _spec(shape, fn):
    return pl.BlockSpec(tuple(shape), fn)


class _Task:
    def __init__(self, peers, operands, out_shapes, aliases, nsem, make, finish, make_second=None):
        self.peers = peers
        self.operands, self.out_shapes, self.aliases = operands, out_shapes, aliases
        self.nsem, self.make, self.finish = nsem, make, finish
        self.make_second = make_second


class _Plan:
    def __init__(self):
        self.tasks, self.after = {}, {}


_plan = None


PEER_SET_COLLECTIVE_ID = {frozenset({"sibling"}): 1, frozenset({"chips"}): 2, frozenset({"sibling", "chips"}): 3,
                          frozenset({"neighbours"}): 4, frozenset({"sibling", "neighbours"}): 5}


def _peer_set(names):
    names = frozenset(names)
    return names - {"neighbours"} if "chips" in names else names


def _entry_handshake(peer_set):
    x, y, c, chips = _mesh_position()
    peers = [(x, y, 1 - c)] if "sibling" in peer_set else []
    if "chips" in peer_set:
        peers += [(*chip, c) for chip in chips]
    if "neighbours" in peer_set:
        peers += [(*chip, c) for chip in chips[:2]]
    barrier = pltpu.get_barrier_semaphore()
    for peer in peers:
        pl.semaphore_signal(barrier, inc=1, device_id=peer, device_id_type=MESH)
    pl.semaphore_wait(barrier, len(peers))


def _pcall(body, *, name, grid, in_specs, out_specs, out_shape, scratch_shapes=(), num_prefetch=0, own_peers=()):
    single = not isinstance(out_shape, (list, tuple))
    out_shape = [out_shape] if single else list(out_shape)
    out_specs = [out_specs] if single else list(out_specs)
    in_specs = list(in_specs)
    scratch_shapes = list(scratch_shapes)
    tasks = _plan.tasks.pop(name, []) if _plan is not None else []
    after = _plan.after.pop(name, []) if _plan is not None else []
    peer_set = _peer_set([t.peers for t in tasks] + list(own_peers))
    nax = len(grid)

    def run(*operands):
        n_in = len(operands) - num_prefetch
        n_out = len(out_shape)
        t_ops = [t.operands() for t in tasks]
        t_outs = [t.out_shapes() for t in tasks]
        c_ops = [a for ops in t_ops for a in ops]
        c_outs = [s for outs in t_outs for s in outs]
        aliases = {}
        i0, o0 = num_prefetch + n_in, n_out
        for t, ops, outs in zip(tasks, t_ops, t_outs):
            for i_loc, o_loc in t.aliases.items():
                aliases[i0 + i_loc] = o0 + o_loc
            i0 += len(ops)
            o0 += len(outs)
        nsem = sum(t.nsem for t in tasks)

        def wrapped(*refs):
            p = num_prefetch
            pre, ins = refs[:p], refs[p:p + n_in]
            cins = refs[p + n_in:p + n_in + len(c_ops)]
            q = p + n_in + len(c_ops)
            outs, couts = refs[q:q + n_out], refs[q + n_out:q + n_out + len(c_outs)]
            q += n_out + len(c_outs)
            scr = refs[q:q + len(scratch_shapes)]

            def rounds(second):
                send_sems, recv_sems = refs[q + len(scratch_shapes):]
                out = []
                ci = co = so = 0
                for t, ops, souts in zip(tasks, t_ops, t_outs):
                    make = t.make_second if second else t.make
                    out.append(([], []) if make is None else
                               make(cins[ci:ci + len(ops)], couts[co:co + len(souts)],
                                    functools.partial(lambda base, k: send_sems.at[base + k], so),
                                    functools.partial(lambda base, k: recv_sems.at[base + k], so)))
                    ci, co, so = ci + len(ops), co + len(souts), so + t.nsem
                return out

            two_rounds = [t.make_second is not None for t in tasks]
            if peer_set:
                ids = [pl.program_id(k) for k in range(nax)]
                first = functools.reduce(jnp.logical_and, [i == 0 for i in ids])
                last = functools.reduce(jnp.logical_and, [i == g - 1 for i, g in zip(ids, grid)])
                step = functools.reduce(lambda acc, ig: acc * ig[1] + ig[0], zip(ids, grid), 0)
                middle = step == math.prod(grid) // 3

                @pl.when(first)
                def _():
                    _entry_handshake(peer_set)
                    for starts, _ in rounds(False):
                        for copy in starts:
                            copy().start()

            body(*pre, *ins, *outs, *scr)

            if any(two_rounds):
                @pl.when(middle)
                def _():
                    for (_, arrivals), two in zip(rounds(False), two_rounds):
                        if two:
                            for arrival in arrivals:
                                arrival().wait_recv()
                    for starts, _ in rounds(True):
                        for copy in starts:
                            copy().start()

            if tasks:
                @pl.when(last)
                def _():
                    first_round, second_round = rounds(False), rounds(True)
                    for (_, arrivals1), (_, arrivals2), two in zip(first_round, second_round, two_rounds):
                        for arrival in (arrivals2 if two else arrivals1):
                            arrival().wait_recv()
                    for starts, _ in first_round + second_round:
                        for copy in starts:
                            copy().wait_send()

        sems = [pltpu.SemaphoreType.DMA((nsem,)), pltpu.SemaphoreType.DMA((nsem,))] if tasks else []
        res = pl.pallas_call(
            wrapped, name=name,
            grid_spec=pltpu.PrefetchScalarGridSpec(
                num_scalar_prefetch=num_prefetch, grid=tuple(grid),
                in_specs=in_specs + [ANY_SPEC] * len(c_ops),
                out_specs=out_specs + [ANY_SPEC] * len(c_outs),
                scratch_shapes=scratch_shapes + sems),
            out_shape=out_shape + c_outs,
            input_output_aliases=aliases,
            compiler_params=_cparams(nax, PEER_SET_COLLECTIVE_ID[peer_set] if peer_set else None),
        )(*operands, *c_ops)
        co = n_out
        for t, souts in zip(tasks, t_outs):
            t.finish(res[co:co + len(souts)])
            co += len(souts)
        for fn in after:
            fn()
        return res[0] if single else list(res[:n_out])

    return run


def _comm_call(name):
    def body(o_ref):
        o_ref[...] = jnp.zeros_like(o_ref)

    _pcall(body, name=name, grid=(1,), in_specs=[], out_specs=_spec((8, 128), lambda i: (0, 0)),
           out_shape=_sds((8, 128), F32))()


def _gemm(name, terms, grid, outs, acc_shape, extras=(), epilogue=None):
    kinds = [t[4] for t in terms]
    nt, ne, no = len(terms), len(extras), len(outs)
    nred = grid[-1]
    nax = len(grid)

    def body(*refs):
        trefs = refs[:2 * nt]
        erefs = refs[2 * nt:2 * nt + ne]
        orefs = refs[2 * nt + ne:2 * nt + ne + no]
        ids = [pl.program_id(k) for k in range(nax)]
        tot = None
        for t in range(nt):
            d = _dot(trefs[2 * t][...], trefs[2 * t + 1][...], kinds[t])
            tot = d if tot is None else tot + d

        def finish(acc):
            if epilogue is None:
                orefs[0][...] = acc.astype(orefs[0].dtype)
            else:
                epilogue(acc, erefs, orefs, ids)

        if nred == 1:
            finish(tot)
        else:
            acc_ref = refs[-1]
            r = ids[-1]

            @pl.when(r == 0)
            def _():
                acc_ref[...] = tot

            @pl.when(r > 0)
            def _():
                acc_ref[...] += tot

            @pl.when(r == nred - 1)
            def _():
                finish(acc_ref[...])

    operands, in_specs = [], []
    for a, a_spec, b, b_spec, _ in terms:
        operands += [a, b]
        in_specs += [a_spec, b_spec]
    for e, e_spec in extras:
        operands.append(e)
        in_specs.append(e_spec)
    scratch = [pltpu.VMEM(tuple(acc_shape), F32)] if nred > 1 else []
    return _pcall(body, name=name, grid=tuple(grid), in_specs=in_specs, out_specs=[o[1] for o in outs],
                  out_shape=[o[0] for o in outs], scratch_shapes=scratch)(*operands)


def _rowwise(name, fn, ins, outs, grid):
    ni = len(ins)
    nax = len(grid)

    def body(*refs):
        ids = [pl.program_id(k) for k in range(nax)]
        fn(refs[:ni], refs[ni:], ids)

    return _pcall(body, name=name, grid=tuple(grid), in_specs=[i[1] for i in ins],
                  out_specs=[o[1] for o in outs], out_shape=[o[0] for o in outs])(*[i[0] for i in ins])


def _ffn_up(name, h, w1buf, w1_idx, w3buf, w3_idx, norm_gain=None):
    T = h.shape[0]
    tm = min(FFN_ROW_TILE, T)
    normed = norm_gain is not None

    def body(h_ref, *refs):
        if normed:
            g_ref, w1_ref, w3_ref, a_ref, b_ref, s_ref, hn_ref = refs
            hv = _rms_fwd(h_ref[...], g_ref[...]).astype(BF16)
            hn_ref[...] = hv
        else:
            w1_ref, w3_ref, a_ref, b_ref, s_ref = refs
            hv = h_ref[...]
        a = _dot(hv, w1_ref[...], "nt")
        b = _dot(hv, w3_ref[...], "nt")
        a_ref[...] = a.astype(BF16)
        b_ref[...] = b.astype(BF16)
        s_ref[...] = ((a * _sigmoid(a)) * b).astype(BF16)

    row = _spec((tm, D), lambda i: (i, 0))
    blk = _spec((tm, D_FF), lambda i: (i, 0))
    return _pcall(
        body, name=name, grid=(T // tm,),
        in_specs=[row] + ([_spec((1, D), lambda i: (0, 0))] if normed else [])
        + [_spec((N_CHIPS, None, FF_BLK, D), lambda i: (0, w1_idx, 0, 0)),
           _spec((N_CHIPS, None, FF_BLK, D), lambda i: (0, w3_idx, 0, 0))],
        out_specs=[blk, blk, blk] + ([row] if normed else []),
        out_shape=[_sds((T, D_FF), BF16)] * 3 + ([_sds((T, D), BF16)] if normed else []),
    )(h, *([norm_gain] if normed else []), w1buf, w3buf)


def _loss_head(x, g, tgt, loss_ref, dx_ref, dg_ref, first):
    err = _rms_fwd(x, g) - tgt
    lp = 0.5 * jnp.sum(jnp.mean(err * err, axis=-1, keepdims=True), axis=0, keepdims=True)
    _accumulate(loss_ref, jnp.broadcast_to(lp, (1, 128)), first)
    dx, dgp = _rms_bwd(x, g, err * (1.0 / D))
    dx_ref[...] = dx
    _accumulate(dg_ref, dgp, first)


def _ffn_down(name, s, wrow2, w2_idx, x_res, g_next=None, loss_target=None):
    T = x_res.shape[0]
    tm = min(ROW_TILE, T)
    row = lambda i, j, r: (i, 0)
    vec = lambda i, j, r: (0, 0)

    def epilogue(acc, erefs, orefs, ids):
        xo = erefs[0][...] + 0.5 * acc
        if loss_target is not None:
            _loss_head(xo, erefs[1][...], erefs[2][...], orefs[0], orefs[1], orefs[2], ids[0] == 0)
            return
        orefs[0][...] = xo
        orefs[1][...] = _rms_fwd(xo, erefs[1][...]).astype(BF16)

    extras = [(x_res, _spec((tm, D), row)), (g_next, _spec((1, D), vec))]
    if loss_target is None:
        outs = [(_sds((T, D), F32), _spec((tm, D), row)), (_sds((T, D), BF16), _spec((tm, D), row))]
    else:
        extras.append((loss_target, _spec((tm, D), row)))
        outs = [(_sds((1, 128), F32), _spec((1, 128), vec)), (_sds((T, D), F32), _spec((tm, D), row)),
                (_sds((1, D), F32), _spec((1, D), vec))]
    return _gemm(
        name,
        [(s, _spec((tm, D_FF), row),
          wrow2, _spec((N_CHIPS, None, FF_BLK, D), lambda i, j, r: (0, w2_idx, 0, 0)), "nn")],
        (T // tm, 1, 1), outs, (tm, D), extras, epilogue)


def _ffn_bwd_mid(name, dx, wrow2, w2_idx, a, b):
    T = dx.shape[0]
    tm = min(FFN_ROW_TILE, T)

    def body(dx_ref, w2_ref, a_ref, b_ref, dab_ref):
        ds = _dot(0.5 * dx_ref[...], w2_ref[...], "nt")
        av = a_ref[...].astype(F32)
        sg = _sigmoid(av)
        dab_ref[0] = (ds * b_ref[...].astype(F32) * (sg * (1.0 + av * (1.0 - sg)))).astype(BF16)
        dab_ref[1] = (ds * (av * sg)).astype(BF16)

    blk = _spec((tm, D_FF), lambda i: (i, 0))
    return _pcall(
        body, name=name, grid=(T // tm,),
        in_specs=[_spec((tm, D), lambda i: (i, 0)),
                  _spec((N_CHIPS, None, FF_BLK, D), lambda i: (0, w2_idx, 0, 0)),
                  blk, blk],
        out_specs=_spec((2, tm, D_FF), lambda i: (0, i, 0)),
        out_shape=_sds((2, T, D_FF), BF16),
    )(dx, wrow2, a, b)


def _rms_bwd_epilogue(acc, erefs, orefs, ids):
    dx, dgp = _rms_bwd(erefs[0][...], erefs[1][...], acc)
    orefs[0][...] = dx + erefs[2][...]
    _accumulate(orefs[1], dgp, ids[0] == 0)


def _rms_bwd_io(x, g, dres, T, tm):
    row = lambda i, j, r: (i, 0)
    vec = lambda i, j, r: (0, 0)
    extras = [(x, _spec((tm, D), row)), (g, _spec((1, D), vec)), (dres, _spec((tm, D), row))]
    outs = [(_sds((T, D), F32), _spec((tm, D), row)), (_sds((1, D), F32), _spec((1, D), vec))]
    return extras, outs


def _ffn_bwd(tag, dx_out, h, a, b, s, w1buf, w1_idx, w3buf, w3_idx, wrow2, w2_idx, x_in, g, big):
    T = dx_out.shape[0]
    dab = _ffn_bwd_mid(tag + "_bwd_mid", dx_out, wrow2, w2_idx, a, b)

    def half_scale(acc, erefs, orefs, ids):
        orefs[0][...] = (0.5 * acc).astype(orefs[0].dtype)

    dw_grid = (D_FF // DW_BLK, 1, 1)
    dw_out = [(_sds((D_FF, D), GRAD_WIRE_DTYPE), _spec((DW_BLK, D), lambda j, n, r: (j, 0)))]
    tokens = _spec((T, D), lambda j, n, r: (0, 0))
    big[tag + "_w2"] = _gemm(
        tag + "_dw2", [(s, _spec((T, DW_BLK), lambda j, n, r: (0, j)), dx_out, tokens, "tn")],
        dw_grid, dw_out, (DW_BLK, D), (), half_scale)[0].reshape(1, N_CHIPS, FF_BLK, D)
    for widx, wname in ((0, "_w1"), (1, "_w3")):
        big[tag + wname] = _gemm(
            tag + "_d" + wname[1:],
            [(dab, _spec((None, T, DW_BLK), functools.partial(lambda w, j, n, r: (w, 0, j), widx)), h, tokens, "tn")],
            dw_grid, dw_out, (DW_BLK, D))[0].reshape(1, N_CHIPS, FF_BLK, D)
    tm = min(FFN_ROW_TILE, T)
    extras, outs = _rms_bwd_io(x_in, g, dx_out, T, tm)
    whole = lambda idx: _spec((N_CHIPS, None, FF_BLK, D), lambda i, j, r: (0, idx, 0, 0))
    dx_in, dg = _gemm(
        tag + "_dh",
        [(dab, _spec((None, tm, D_FF), lambda i, j, r: (0, i, 0)), w1buf, whole(w1_idx), "nn"),
         (dab, _spec((None, tm, D_FF), lambda i, j, r: (1, i, 0)), w3buf, whole(w3_idx), "nn")],
        (T // tm, 1, 1), outs, (tm, D), extras, _rms_bwd_epilogue)
    return dx_in, dg


def _proj_sq(name, a, wsq, idx, kind, out_dtype=F32, extras=(), epilogue=None, outs=None):
    M = a.shape[0]
    tm = min(ROW_TILE, M)
    if outs is None:
        outs = [(_sds((M, D), out_dtype), _spec((tm, D), lambda i, j, r: (i, 0)))]
    return _gemm(
        name,
        [(a, _spec((tm, D), lambda i, j, r: (i, 0)),
          wsq, _spec((N_CHIPS, None, SQ_BLK, D), lambda i, j, r: (0, idx, 0, 0)), kind)],
        (M // tm, 1, 1), outs, (tm, D), extras, epilogue)


def _dw_sq(name, a, b):
    M = a.shape[0]
    tn = D // 2
    whole = _gemm(
        name,
        [(a, _spec((M, D), lambda i, j, r: (0, 0)), b, _spec((M, tn), lambda i, j, r: (0, j)), "tn")],
        (1, D // tn, 1),
        [(_sds((D, D), GRAD_WIRE_DTYPE), _spec((D, tn), lambda i, j, r: (0, j)))],
        (D, tn))[0]
    return whole.reshape(N_CHIPS, SQ_BLK, D)


def _retention_constants(T):
    pos = jnp.arange(T, dtype=F32)
    inv_freq = ROPE_BASE ** (-jnp.arange(0, RET_DK, 2, dtype=F32) / RET_DK)
    ang = pos[:, None] * inv_freq[None, :]
    cosf = jnp.concatenate([jnp.cos(ang), jnp.cos(ang)], axis=1)
    sins = jnp.concatenate([-jnp.sin(ang), jnp.sin(ang)], axis=1)
    lg = jnp.log(1.0 - 2.0 ** (-5.0 - jnp.arange(RET_HEADS, dtype=F32)))
    p = jnp.arange(CHUNK, dtype=F32)
    rel = p[:, None] - p[None, :]
    dmat = jnp.where(rel[None] >= 0, jnp.exp(rel[None] * lg[:, None, None]), 0.0)
    kd = jnp.exp((CHUNK - 1.0 - p)[None, :] * lg[:, None])[:, :, None]
    qd = jnp.exp((p + 1.0)[None, :] * lg[:, None])[:, :, None]
    cd = jnp.exp(CHUNK * lg)[:, None, None]
    return cosf, sins, dmat, kd, qd, cd


def _rot(t, cosv, sinv):
    return t * cosv + pltpu.roll(t, RET_DK // 2, 1) * sinv


def _unrot(t, cosv, sinv):
    return t * cosv - pltpu.roll(t, RET_DK // 2, 1) * sinv


def _ret_const_specs(cm):
    whole = lambda shape: _spec(shape, lambda c: (0,) * len(shape))
    return [
        _spec((RET_STEP_ROWS, RET_DK), lambda c: (cm(c), 0)),
        _spec((RET_STEP_ROWS, RET_DK), lambda c: (cm(c), 0)),
        whole((RET_HEADS, CHUNK, CHUNK)), whole((RET_HEADS, CHUNK, 1)), whole((RET_HEADS, CHUNK, 1)),
        whole((RET_HEADS, 1, 1)),
    ]


def _head(h, width):
    return slice(h * width, (h + 1) * width)


def _ret_fwd(u, consts, ret_gn):
    T = u.shape[0]
    nC = T // CHUNK
    kscale = RET_DK ** -0.5

    def body(q_ref, k_ref, v_ref, g_ref, cos_ref, sin_ref, dm_ref, kd_ref, qd_ref, cd_ref, gn_ref,
             qr_ref, kr_ref, ret_ref, yr_ref, st_ref, state):
        @pl.when(pl.program_id(0) == 0)
        def _():
            state[...] = jnp.zeros_like(state)

        for cc in range(RET_STEP_CHUNKS):
            rows = slice(cc * CHUNK, (cc + 1) * CHUNK)
            cosv, sinv = cos_ref[rows, :], sin_ref[rows, :]
            for h in range(RET_HEADS):
                hk, hv = _head(h, RET_DK), _head(h, RET_DV)
                q = _rot(q_ref[rows, hk], cosv, sinv)
                k = _rot(k_ref[rows, hk], cosv, sinv) * kscale
                v = v_ref[rows, hv]
                qr_ref[rows, hk] = q
                kr_ref[rows, hk] = k
                prev = state[h]
                st_ref[h, cc] = prev
                s = _dot(q, k, "nt") * dm_ref[h]
                ret = _dot(s, v, "nn") + _dot(q, prev, "nn") * qd_ref[h]
                state[h] = cd_ref[h] * prev + _dot(k * kd_ref[h], v, "tn")
                ret_ref[rows, hv] = ret
                mu = jnp.mean(ret, axis=-1, keepdims=True)
                xc = ret - mu
                yn = xc * lax.rsqrt(jnp.mean(xc * xc, axis=-1, keepdims=True) + EPS)
                g = g_ref[rows, hv]
                yr_ref[rows, hv] = ((g * _sigmoid(g)) * (yn * gn_ref[:, hv])).astype(BF16)

    cm = lambda c: c
    qk_w, v_w = RET_HEADS * RET_DK, RET_HEADS * RET_DV
    in_specs = [
        _spec((RET_STEP_ROWS, qk_w), lambda c: (c, 0)), _spec((RET_STEP_ROWS, qk_w), lambda c: (c, 1)),
        _spec((RET_STEP_ROWS, v_w), lambda c: (c, 1)), _spec((RET_STEP_ROWS, v_w), lambda c: (c, 2)),
    ] + _ret_const_specs(cm) + [_spec((1, v_w), lambda c: (0, 0))]
    qk_out = _spec((RET_STEP_ROWS, qk_w), lambda c: (c, 0))
    v_out = _spec((RET_STEP_ROWS, v_w), lambda c: (c, 0))
    return _pcall(
        body, name="ret_fwd", grid=(nC // RET_STEP_CHUNKS,),
        in_specs=in_specs,
        out_specs=[qk_out, qk_out, v_out, v_out,
                   _spec((RET_HEADS, RET_STEP_CHUNKS, RET_DK, RET_DV), lambda c: (0, c, 0, 0))],
        out_shape=[_sds((T, qk_w), F32), _sds((T, qk_w), F32), _sds((T, v_w), F32), _sds((T, v_w), BF16),
                   _sds((RET_HEADS, nC, RET_DK, RET_DV), F32)],
        scratch_shapes=[pltpu.VMEM((RET_HEADS, RET_DK, RET_DV), F32)],
    )(u, u, u, u, *consts, ret_gn)


def _ret_bwd(dyr, ret, u, qr, kr, states, consts, ret_gn):
    T = u.shape[0]
    nC = T // CHUNK
    kscale = RET_DK ** -0.5

    def body(dyr_ref, ret_ref, g_ref, q_ref, k_ref, v_ref, st_ref,
             cos_ref, sin_ref, dm_ref, kd_ref, qd_ref, cd_ref, gn_ref,
             dq_ref, dk_ref, dv_ref, dg_ref, dgn_ref, gstate):
        first = pl.program_id(0) == 0

        @pl.when(first)
        def _():
            gstate[...] = jnp.zeros_like(gstate)

        dgn_total = None
        for cc in reversed(range(RET_STEP_CHUNKS)):
            rows = slice(cc * CHUNK, (cc + 1) * CHUNK)
            cosv, sinv = cos_ref[rows, :], sin_ref[rows, :]
            dgn_parts = []
            for h in range(RET_HEADS):
                hk, hv = _head(h, RET_DK), _head(h, RET_DV)
                ret = ret_ref[rows, hv]
                mu = jnp.mean(ret, axis=-1, keepdims=True)
                xc = ret - mu
                rs = lax.rsqrt(jnp.mean(xc * xc, axis=-1, keepdims=True) + EPS)
                yn = xc * rs
                gn = gn_ref[:, hv]
                g = g_ref[rows, hv]
                sg = _sigmoid(g)
                dyr_v = dyr_ref[rows, hv]
                dretn = dyr_v * (g * sg)
                dg_ref[rows, hv] = (dyr_v * (yn * gn) * (sg * (1.0 + g * (1.0 - sg)))).astype(BF16)
                dgn_parts.append(jnp.sum(dretn * yn, axis=0, keepdims=True))
                dyn = dretn * gn
                d_o = rs * (dyn - jnp.mean(dyn, axis=-1, keepdims=True)
                            - yn * jnp.mean(dyn * yn, axis=-1, keepdims=True))

                q, k, v = q_ref[rows, hk], k_ref[rows, hk], v_ref[rows, hv]
                dmat, kd, qd = dm_ref[h], kd_ref[h], qd_ref[h]
                prev = st_ref[h, cc]
                gnext = gstate[h]
                s = _dot(q, k, "nt") * dmat
                ds = _dot(d_o, v, "nt") * dmat
                doq = d_o * qd
                dq = _dot(ds, k, "nn") + _dot(doq, prev, "nt")
                dk = _dot(ds, q, "tn") + _dot(v, gnext, "nt") * kd
                dv = _dot(s, d_o, "tn") + _dot(k * kd, gnext, "nn")
                gstate[h] = cd_ref[h] * gnext + _dot(q, doq, "tn")
                dq_ref[rows, hk] = _unrot(dq, cosv, sinv).astype(BF16)
                dk_ref[rows, hk] = _unrot(dk * kscale, cosv, sinv).astype(BF16)
                dv_ref[rows, hv] = dv.astype(BF16)
            dgn = jnp.concatenate(dgn_parts, axis=1)
            dgn_total = dgn if dgn_total is None else dgn_total + dgn
        _accumulate(dgn_ref, dgn_total, first)

    n_steps = nC // RET_STEP_CHUNKS
    cm = lambda c: n_steps - 1 - c
    qk_w, v_w = RET_HEADS * RET_DK, RET_HEADS * RET_DV
    vspec = lambda blk: _spec((RET_STEP_ROWS, v_w), lambda c: (cm(c), blk))
    qspec = _spec((RET_STEP_ROWS, qk_w), lambda c: (cm(c), 0))
    in_specs = [vspec(0), vspec(0), vspec(2), qspec, qspec, vspec(1),
                _spec((RET_HEADS, RET_STEP_CHUNKS, RET_DK, RET_DV), lambda c: (0, cm(c), 0, 0)),
                ] + _ret_const_specs(cm) + [_spec((1, v_w), lambda c: (0, 0))]
    return _pcall(
        body, name="ret_bwd", grid=(n_steps,),
        in_specs=in_specs,
        out_specs=[qspec, qspec, vspec(0), vspec(0), _spec((1, v_w), lambda c: (0, 0))],
        out_shape=[_sds((T, qk_w), BF16), _sds((T, qk_w), BF16), _sds((T, v_w), BF16), _sds((T, v_w), BF16),
                   _sds((1, v_w), F32)],
        scratch_shapes=[pltpu.VMEM((RET_HEADS, RET_DK, RET_DV), F32)],
    )(dyr, ret, u, qr, kr, u, states, *consts, ret_gn)


def _shift_down(x, s):
    rows = lax.broadcasted_iota(jnp.int32, x.shape, 0)
    return jnp.where(rows >= s, pltpu.roll(x, s, 0), 0.0)


def _shift_up(x, s):
    n = x.shape[0]
    rows = lax.broadcasted_iota(jnp.int32, x.shape, 0)
    return jnp.where(rows < n - s, pltpu.roll(x, n - s, 0), 0.0)


def _lru_specs(T):
    col = lambda off: _spec((T, LRU_BLOCK), lambda g: (0, off + g))
    vec = _spec((1, LRU_BLOCK), lambda g: (0, g))
    wblk = _spec((None, LRU_BLOCK, LRU_BLOCK), lambda g: (g, 0, 0))
    cw = _spec((CONV_TAPS, LRU_BLOCK), lambda g: (0, g))
    return col, vec, wblk, cw


def _lru_gates_fwd(u, conv_w, conv_b, w_r, b_r, w_i, b_i, lam):
    T = u.shape[0]
    col, vec, wblk, cw = _lru_specs(T)

    def body(x_ref, cw_ref, cb_ref, wr_ref, br_ref, wi_ref, bi_ref, lam_ref,
             xc_ref, r_ref, i_ref, a_ref, bx_ref):
        x = x_ref[...]
        w = cw_ref[...]
        xc = (_shift_down(x, 3) * w[0:1] + _shift_down(x, 2) * w[1:2] + _shift_down(x, 1) * w[2:3]
              + x * w[3:4] + cb_ref[...])
        r = _sigmoid(_dot(xc, wr_ref[...], "nn") + br_ref[...])
        i = _sigmoid(_dot(xc, wi_ref[...], "nn") + bi_ref[...])
        la = (-LRU_C) * r * _softplus(-lam_ref[...])
        xc_ref[...] = xc
        r_ref[...] = r
        i_ref[...] = i
        a_ref[...] = jnp.exp(la)
        bx_ref[...] = jnp.sqrt(-_expm1(2.0 * la)) * (i * xc)

    out = col(0)
    return _pcall(
        body, name="lru_gates_fwd", grid=(LRU_BLOCKS,),
        in_specs=[col(24), cw, vec, wblk, vec, wblk, vec, vec],
        out_specs=[out] * 5,
        out_shape=[_sds((T, D), F32)] * 5,
    )(u, conv_w, conv_b, w_r, b_r, w_i, b_i, lam)


def _lru_scan(name, a3, b3, reverse):
    T = a3.shape[0]
    nt = T // SCAN_TILE
    unroll = 8

    def body(a_ref, b_ref, o_ref, carry):
        @pl.when(pl.program_id(0) == 0)
        def _():
            carry[...] = jnp.zeros_like(carry)

        if not reverse:
            def step(t, h):
                h = a_ref[t] * h + b_ref[t]
                o_ref[t] = h
                return h
        else:
            def step(k, c):
                t = SCAN_TILE - 1 - k
                l = b_ref[t] + c
                o_ref[t] = l
                return a_ref[t] * l
        carry[...] = lax.fori_loop(0, SCAN_TILE, step, carry[...], unroll=unroll)

    idx = (lambda i: (nt - 1 - i, 0, 0)) if reverse else (lambda i: (i, 0, 0))
    blk = _spec((SCAN_TILE, LRU_BLOCKS, LRU_BLOCK), idx)
    return _pcall(
        body, name=name, grid=(nt,),
        in_specs=[blk, blk], out_specs=blk,
        out_shape=_sds((T, LRU_BLOCKS, LRU_BLOCK), F32),
        scratch_shapes=[pltpu.VMEM((LRU_BLOCKS, LRU_BLOCK), F32)],
    )(a3, b3)


def _lru_gates_bwd(lmb, hl, a, r, i, xc, u, conv_w, w_r, w_i, lam):
    T = u.shape[0]
    col, vec, wblk, cw = _lru_specs(T)

    def body(l_ref, h_ref, a_ref, r_ref, i_ref, xc_ref, x_ref, cw_ref, wr_ref, wi_ref, lam_ref,
             dx_ref, dwr_ref, dwi_ref, dvec_ref, dcw_ref):
        l = l_ref[...]
        av, rv, iv, xc = a_ref[...], r_ref[...], i_ref[...], xc_ref[...]
        lam_v = lam_ref[...]
        sp = _softplus(-lam_v)
        la = (-LRU_C) * rv * sp
        mult = jnp.sqrt(-_expm1(2.0 * la))
        da = l * _shift_down(h_ref[...], 1)
        dmult = l * (iv * xc)
        di = l * mult * xc
        dxc = l * mult * iv
        dla = da * av - dmult * (av * av) / mult
        dzr = (dla * ((-LRU_C) * sp)) * rv * (1.0 - rv)
        dzi = di * iv * (1.0 - iv)
        dsp = jnp.sum(dla * ((-LRU_C) * rv), axis=0, keepdims=True)
        dlam = dsp * (-_sigmoid(-lam_v))
        dwr_ref[...] = _dot(xc, dzr, "tn")
        dwi_ref[...] = _dot(xc, dzi, "tn")
        dxc = dxc + _dot(dzr, wr_ref[...], "nt") + _dot(dzi, wi_ref[...], "nt")
        x = x_ref[...]
        w = cw_ref[...]
        dx = (dxc * w[3:4] + _shift_up(dxc, 1) * w[2:3] + _shift_up(dxc, 2) * w[1:2]
              + _shift_up(dxc, 3) * w[0:1])
        dx_ref[...] = dx.astype(BF16)
        dvec_ref[...] = jnp.concatenate(
            [jnp.sum(dzr, axis=0, keepdims=True), jnp.sum(dzi, axis=0, keepdims=True), dlam,
             jnp.sum(dxc, axis=0, keepdims=True)], axis=0)
        dcw_ref[...] = jnp.concatenate(
            [jnp.sum(dxc * _shift_down(x, 3 - tap), axis=0, keepdims=True) if tap < 3
             else jnp.sum(dxc * x, axis=0, keepdims=True) for tap in range(CONV_TAPS)], axis=0)

    c0 = col(0)
    return _pcall(
        body, name="lru_gates_bwd", grid=(LRU_BLOCKS,),
        in_specs=[c0, c0, c0, c0, c0, c0, col(24), cw, wblk, wblk, vec],
        out_specs=[c0, wblk, wblk, cw, cw],
        out_shape=[_sds((T, D), BF16), _sds((LRU_BLOCKS, LRU_BLOCK, LRU_BLOCK), F32),
                   _sds((LRU_BLOCKS, LRU_BLOCK, LRU_BLOCK), F32), _sds((4, D), F32), _sds((CONV_TAPS, D), F32)],
    )(lmb, hl, a, r, i, xc, u, conv_w, w_r, w_i, lam)


def _xattn_probs(q, k):
    sc = _dot(q, k, "nt") * (X_HD ** -0.5)
    e = jnp.exp(sc - jnp.max(sc, axis=-1, keepdims=True))
    return e / jnp.sum(e, axis=-1, keepdims=True)


def _xattn_fwd(xq, xk, xv):
    T = xq.shape[0]
    tq = min(WIDE_ROW_TILE, T)
    M = xk.shape[0]

    def body(q_ref, k_ref, v_ref, o_ref):
        p = _xattn_probs(q_ref[...], k_ref[...])
        o_ref[...] = _dot(p, v_ref[...], "nn").astype(BF16)

    qs = _spec((tq, X_HD), lambda h, i: (i, h))
    kv = _spec((M, X_HD), lambda h, i: (0, h))
    return _pcall(
        body, name="xattn_fwd", grid=(X_HEADS, T // tq),
        in_specs=[qs, kv, kv], out_specs=qs, out_shape=_sds((T, D), BF16),
    )(xq, xk, xv)


def _xattn_bwd(xq, xk, xv, dxo):
    T = xq.shape[0]
    tq = min(WIDE_ROW_TILE, T)
    M = xk.shape[0]

    def body(q_ref, k_ref, v_ref, do_ref, dq_ref, dk_ref, dv_ref):
        first = pl.program_id(1) == 0
        q, k, v, do = q_ref[...], k_ref[...], v_ref[...], do_ref[...]
        p = _xattn_probs(q, k)
        dp = _dot(do, v, "nt")
        ds = p * (dp - jnp.sum(dp * p, axis=-1, keepdims=True)) * (X_HD ** -0.5)
        dq_ref[...] = _dot(ds, k, "nn").astype(BF16)
        _accumulate(dk_ref, _dot(ds, q, "tn"), first)
        _accumulate(dv_ref, _dot(p, do, "tn"), first)

    qs = _spec((tq, X_HD), lambda h, i: (i, h))
    kv = _spec((M, X_HD), lambda h, i: (0, h))
    return _pcall(
        body, name="xattn_bwd", grid=(X_HEADS, T // tq),
        in_specs=[qs, kv, kv, qs], out_specs=[qs, kv, kv],
        out_shape=[_sds((T, D), BF16), _sds((M, D), F32), _sds((M, D), F32)],
    )(xq, xk, xv, dxo)


def _adamw(name, w, g, m, v):
    R, C = w.shape
    tr = R
    for cand in (512, 352, 256):
        if R % cand == 0:
            tr = cand
            break

    def fn(irefs, orefs, ids):
        delta, mn, vn = _adamw_update(*(r[...] for r in irefs))
        orefs[0][...] = delta
        orefs[1][...] = mn
        orefs[2][...] = vn

    blk = _spec((tr, C), lambda i: (i, 0))
    return _rowwise(name, fn, [(w, blk), (g, blk), (m, blk), (v, blk)],
                    [(_sds((R, C), F32), blk)] * 3, (R // tr,))


def _adamw_update(wv, gv, mv, vv):
    c1 = 1.0 - ADAM_B1 ** ADAM_STEP
    c2 = 1.0 - ADAM_B2 ** ADAM_STEP
    mn = ADAM_B1 * mv + (1.0 - ADAM_B1) * gv
    vn = ADAM_B2 * vv + (1.0 - ADAM_B2) * (gv * gv)
    delta = -ADAM_LR * ((mn / c1) / (jnp.sqrt(vn / c2) + ADAM_EPS) + ADAM_WD * wv)
    return delta, mn, vn


def _adamw_halves(name, w, mine, theirs, widx, m, v, core):
    R, C = w.shape
    H = R // 2
    tr = H
    while tr * C * 4 > (1 << 20) and tr % 16 == 0:
        tr //= 2
    nb = H // tr

    def body(core_ref, w_ref, mine_ref, theirs_ref, m_ref, v_ref, g_out, d_out, m_out, v_out):
        gv = jnp.where(pl.program_id(0) == core_ref[0], mine_ref[...], theirs_ref[...])
        delta, mn, vn = _adamw_update(w_ref[...], gv, m_ref[...], v_ref[...])
        g_out[...] = gv
        d_out[...] = delta
        m_out[...] = mn
        v_out[...] = vn

    full = pl.BlockSpec((tr, C), lambda h, i, core_ref: (h * nb + i, 0))
    mine_spec = pl.BlockSpec((None, tr, C), lambda h, i, core_ref: (widx, jnp.where(h == core_ref[0], i, 0), 0))
    theirs_spec = pl.BlockSpec((None, tr, C), lambda h, i, core_ref: (widx, jnp.where(h == core_ref[0], 0, i), 0))
    return _pcall(
        body, name=name, grid=(2, nb), num_prefetch=1,
        in_specs=[full, mine_spec, theirs_spec, full, full], out_specs=[full] * 4,
        out_shape=[_sds((R, C), F32)] * 4,
    )(core, w, mine, theirs, m, v)


def _rmsnorm(name, x, g):
    M = x.shape[0]
    tm = min(ROW_TILE, M)

    def fn(irefs, orefs, ids):
        orefs[0][...] = _rms_fwd(irefs[0][...], irefs[1][...]).astype(BF16)

    row = _spec((tm, D), lambda i: (i, 0))
    return _rowwise(name, fn, [(x, row), (g, _spec((1, D), lambda i: (0, 0)))],
                    [(_sds((M, D), BF16), row)], (M // tm,))[0]


WEIGHT_AT = {
    "ffn1_w1": ("col1", 0), "ffn1_w3": ("col1", 1), "ffn1_w2": ("row2a", 0),
    "w_ret_o": ("sqA", 0), "w_lru_o": ("sqA", 1), "w_out": ("sqA", 2),
    "w_xq": ("sqB", 0), "w_xk": ("sqB", 1), "w_xv": ("sqC", 0), "w_xo": ("sqC", 1),
    "ffn2_w1": ("col2a", 0), "ffn2_w3": ("col2b", 0), "ffn2_w2": ("row2b", 0),
}


def _local_step(x, mem, tgt, gw, sm, big):
    T = x.shape[0]
    tm = ROW_TILE

    def wt(name):
        key, idx = WEIGHT_AT[name]
        return gw[key], idx

    row3 = lambda i, j, r: (i, 0)
    vec3 = lambda i, j, r: (0, 0)
    rowD = _spec((tm, D), row3)
    vecD = _spec((1, D), vec3)

    def residual_norm(acc, erefs, orefs, ids):
        xo = erefs[0][...] + acc
        orefs[0][...] = xo
        orefs[1][...] = _rms_fwd(xo, erefs[1][...]).astype(BF16)

    def res_norm_io(x_res, g):
        return ([(x_res, rowD), (g, vecD)],
                [(_sds((T, D), F32), rowD), (_sds((T, D), BF16), rowD)])

    a1, b1, s1, h1 = _ffn_up("ffn1_up", x, *wt("ffn1_w1"), *wt("ffn1_w3"), norm_gain=sm["ffn1_norm"])
    x1, h2 = _ffn_down("ffn1_down", s1, *wt("ffn1_w2"), x, sm["mix_norm"])

    tw = min(WIDE_ROW_TILE, T)
    wideD = _spec((tw, D), row3)
    u = _gemm(
        "mix_in",
        [(h2, wideD, gw["win"], _spec((None, None, IN_BLK, D), lambda i, j, r: (j, 0, 0, 0)), "nt")],
        (T // tw, N_CHIPS, 1),
        [(_sds((T, 5120), F32), _spec((tw, IN_BLK), lambda i, j, r: (i, j)))], (tw, IN_BLK))[0]

    consts = _retention_constants(T)
    qr, kr, ret, yr, states = _ret_fwd(u, consts, sm["ret_gn"])

    conv_w = gw["conv"][:, 0].transpose(1, 0, 2).reshape(CONV_TAPS, D)
    xc, rg, ig, av, bx = _lru_gates_fwd(u, conv_w, sm["conv_b"], sm["w_rgate"], sm["b_rgate"],
                                        sm["w_igate"], sm["b_igate"], sm["lru_lambda"])
    a3 = av.reshape(T, LRU_BLOCKS, LRU_BLOCK)
    hl = _lru_scan("lru_scan_fwd", a3, bx.reshape(T, LRU_BLOCKS, LRU_BLOCK), False).reshape(T, D)

    row1 = _spec((tm, D), lambda i: (i, 0))
    glru1 = _spec((tm, D), lambda i: (i, 4))

    def lru_out(irefs, orefs, ids):
        gl, _ = _gelu_and_grad(irefs[1][...])
        orefs[0][...] = (irefs[0][...] * gl).astype(BF16)

    yl = _rowwise("lru_out", lru_out, [(hl, row1), (u, glru1)], [(_sds((T, D), BF16), row1)], (T // tm,))[0]

    def gate_epilogue(acc, erefs, orefs, ids):
        orefs[0][...] = _sigmoid(acc + erefs[0][...])

    gates = _gemm(
        "mix_gates",
        [(h2, wideD, gw["wbg"], _spec((None, None, BG_BLK, D), lambda i, j, r: (j, 0, 0, 0)), "nt")],
        (T // tw, N_CHIPS, 1),
        [(_sds((T, 2 * D), F32), _spec((tw, BG_BLK), lambda i, j, r: (i, j)))], (tw, BG_BLK),
        [(sm["b_branch_gate"], _spec((1, BG_BLK), lambda i, j, r: (0, j)))], gate_epilogue)[0]

    y_ret = _proj_sq("y_ret", yr, *wt("w_ret_o"), "nn")[0]

    def merge_epilogue(acc, erefs, orefs, ids):
        orefs[0][...] = acc
        orefs[1][...] = (erefs[0][...] * erefs[2][...] + erefs[1][...] * acc).astype(BF16)

    y_lru, merged = _proj_sq(
        "y_lru", yl, *wt("w_lru_o"), "nn",
        extras=[(gates, _spec((tm, D), lambda i, j, r: (i, 0))), (gates, _spec((tm, D), lambda i, j, r: (i, 1))),
                (y_ret, rowD)],
        epilogue=merge_epilogue,
        outs=[(_sds((T, D), F32), rowD), (_sds((T, D), BF16), rowD)])

    ex, ou = res_norm_io(x1, sm["xattn_norm"])
    x2, hq = _proj_sq("mix_out", merged, *wt("w_out"), "nn", extras=ex, epilogue=residual_norm, outs=ou)

    m = _rmsnorm("mem_norm", mem, sm["mem_norm"])
    xq = _proj_sq("xq", hq, *wt("w_xq"), "nn", BF16)[0]
    xk = _proj_sq("xk", m, *wt("w_xk"), "nn", BF16)[0]
    xv = _proj_sq("xv", m, *wt("w_xv"), "nn", BF16)[0]
    xo = _xattn_fwd(xq, xk, xv)
    ex, ou = res_norm_io(x2, sm["ffn2_norm"])
    x3, h3 = _proj_sq("xattn_out", xo, *wt("w_xo"), "nn", extras=ex, epilogue=residual_norm, outs=ou)

    a2, b2, s2 = _ffn_up("ffn2_up", h3, *wt("ffn2_w1"), *wt("ffn2_w3"))
    loss, dx4, dg_final = _ffn_down("ffn2_down", s2, *wt("ffn2_w2"), x3, sm["final_norm"], loss_target=tgt)

    dx3, dg_ffn2 = _ffn_bwd("ffn2", dx4, h3, a2, b2, s2, *wt("ffn2_w1"), *wt("ffn2_w3"),
                            *wt("ffn2_w2"), x3, sm["ffn2_norm"], big)

    dxo = _proj_sq("d_xo", dx3, *wt("w_xo"), "nt", BF16)[0]
    big["w_xo"] = _dw_sq("dw_xo", xo, dx3)[None]
    dxq, dxk, dxv = _xattn_bwd(xq, xk, xv, dxo)
    big["w_xq"] = _dw_sq("dw_xq", hq, dxq)[None]
    ex, ou = _rms_bwd_io(x2, sm["xattn_norm"], dx3, T, tm)
    dx2, dg_xattn = _proj_sq("d_hq", dxq, *wt("w_xq"), "nt", extras=ex, epilogue=_rms_bwd_epilogue, outs=ou)
    big["w_xk"] = _dw_sq("dw_xk", m, dxk)[None]
    big["w_xv"] = _dw_sq("dw_xv", m, dxv)[None]

    M = mem.shape[0]

    def mem_norm_epilogue(acc, erefs, orefs, ids):
        _, dgp = _rms_bwd(erefs[0][...], erefs[1][...], acc)
        orefs[0][...] = dgp

    wsq_spec = lambda idx: _spec((N_CHIPS, None, SQ_BLK, D), lambda i, j, r: (0, idx, 0, 0))
    memD = _spec((M, D), row3)
    dg_mem = _gemm(
        "d_mem_norm",
        [(dxk, memD, wt("w_xk")[0], wsq_spec(wt("w_xk")[1]), "nt"),
         (dxv, memD, wt("w_xv")[0], wsq_spec(wt("w_xv")[1]), "nt")],
        (1, 1, 1), [(_sds((1, D), F32), vecD)], (M, D),
        [(mem, memD), (sm["mem_norm"], vecD)], mem_norm_epilogue)[0]

    def merged_bwd_epilogue(acc, erefs, orefs, ids):
        gr, gl, yrv, ylv = (e[...] for e in erefs)
        orefs[0][...] = (acc * gr).astype(BF16)
        orefs[1][...] = (acc * gl).astype(BF16)
        dgr = acc * yrv * gr * (1.0 - gr)
        dgl = acc * ylv * gl * (1.0 - gl)
        orefs[2][:, :D] = dgr.astype(BF16)
        orefs[2][:, D:] = dgl.astype(BF16)
        dbb = jnp.concatenate([jnp.sum(dgr, axis=0, keepdims=True), jnp.sum(dgl, axis=0, keepdims=True)], axis=1)
        _accumulate(orefs[3], dbb, ids[0] == 0)

    dy_ret, dy_lru, dgpre, db_bg = _proj_sq(
        "d_merged", dx2, *wt("w_out"), "nt",
        extras=[(gates, _spec((tm, D), lambda i, j, r: (i, 0))), (gates, _spec((tm, D), lambda i, j, r: (i, 1))),
                (y_ret, rowD), (y_lru, rowD)],
        epilogue=merged_bwd_epilogue,
        outs=[(_sds((T, D), BF16), rowD), (_sds((T, D), BF16), rowD),
              (_sds((T, 2 * D), BF16), _spec((tm, 2 * D), row3)),
              (_sds((1, 2 * D), F32), _spec((1, 2 * D), vec3))])
    big["w_branch_gate"] = _gemm(
        "dw_bg",
        [(h2, _spec((T, D), lambda j, n, r: (r, 0)), dgpre, _spec((T, BG_BLK), lambda j, n, r: (r, j)), "tn")],
        (N_CHIPS, 1, 1),
        [(_sds((N_CHIPS, D, BG_BLK), GRAD_WIRE_DTYPE), _spec((None, D, BG_BLK), lambda j, n, r: (j, 0, 0)))],
        (D, BG_BLK))[0][None]
    big["w_out"] = _dw_sq("dw_out", merged, dx2)[None]
    dyr = _proj_sq("d_yr", dy_ret, *wt("w_ret_o"), "nt")[0]
    big["w_ret_o"] = _dw_sq("dw_ret_o", yr, dy_ret)[None]
    dyl = _proj_sq("d_yl", dy_lru, *wt("w_lru_o"), "nt")[0]
    big["w_lru_o"] = _dw_sq("dw_lru_o", yl, dy_lru)[None]

    dq, dk, dv, dgr, dg_retgn = _ret_bwd(dyr, ret, u, qr, kr, states, consts, sm["ret_gn"])

    def lru_out_bwd(irefs, orefs, ids):
        gl, dgl = _gelu_and_grad(irefs[2][...])
        dyl_v = irefs[0][...]
        orefs[0][...] = dyl_v * gl
        orefs[1][...] = (dyl_v * irefs[1][...] * dgl).astype(BF16)

    dhl, dglru = _rowwise("lru_out_bwd", lru_out_bwd, [(dyl, row1), (hl, row1), (u, glru1)],
                          [(_sds((T, D), F32), row1), (_sds((T, D), BF16), row1)], (T // tm,))
    lmb = _lru_scan("lru_scan_bwd", a3, dhl.reshape(T, LRU_BLOCKS, LRU_BLOCK), True).reshape(T, D)
    dxl, dw_r, dw_i, dvec, dcw = _lru_gates_bwd(lmb, hl, av, rg, ig, xc, u, conv_w,
                                                sm["w_rgate"], sm["w_igate"], sm["lru_lambda"])

    du = jnp.concatenate([dq, dk, dv, dgr, dxl, dglru], axis=1)
    tk = T
    big["w_in"] = _gemm(
        "dw_in",
        [(h2, _spec((tk, D), lambda j, n, r: (r, 0)), du, _spec((tk, IN_BLK), lambda j, n, r: (r, j)), "tn")],
        (N_CHIPS, 1, T // tk),
        [(_sds((N_CHIPS, D, IN_BLK), GRAD_WIRE_DTYPE), _spec((None, D, IN_BLK), lambda j, n, r: (j, 0, 0)))],
        (D, IN_BLK))[0][None]
    tf = min(FFN_ROW_TILE, T)
    ex, ou = _rms_bwd_io(x1, sm["mix_norm"], dx2, T, tf)
    dx1, dg_mix = _gemm(
        "d_h2",
        [(du, _spec((tf, 5120), row3), gw["win"], _spec((N_CHIPS, None, IN_BLK, D), lambda i, j, r: (0, 0, 0, 0)), "nn"),
         (dgpre, _spec((tf, 2 * D), row3), gw["wbg"], _spec((N_CHIPS, None, BG_BLK, D), lambda i, j, r: (0, 0, 0, 0)),
          "nn")],
        (T // tf, 1, 1), ou, (tf, D), ex, _rms_bwd_epilogue)

    grad_x, dg_ffn1 = _ffn_bwd("ffn1", dx1, h1, a1, b1, s1, *wt("ffn1_w1"), *wt("ffn1_w3"),
                               *wt("ffn1_w2"), x, sm["ffn1_norm"], big)

    small = {
        "ffn1_norm": dg_ffn1, "mix_norm": dg_mix, "ret_gn": dg_retgn, "conv_b": dvec[3:4],
        "b_rgate": dvec[0:1], "b_igate": dvec[1:2], "lru_lambda": dvec[2:3], "xattn_norm": dg_xattn,
        "mem_norm": dg_mem, "ffn2_norm": dg_ffn2, "final_norm": dg_final, "b_branch_gate": db_bg,
        "conv_w": dcw, "w_rgate": dw_r, "w_igate": dw_i,
    }
    return loss, grad_x, small


ANY_SPEC = pl.BlockSpec(memory_space=pl.ANY)
VMEM_SPEC = pl.BlockSpec(memory_space=pltpu.VMEM)
N_PEER_CHIPS = N_CHIPS - 1


def _mesh_position():
    x, y, c = lax.axis_index("x"), lax.axis_index("y"), lax.axis_index("c")
    chips = [(1 - x, y), (x, 1 - y), (1 - x, 1 - y)]
    return x, y, c, chips


def _chip_index(x, y):
    return 2 * x + y


def _rows_half(ref, axis, h):
    n = ref.shape[axis] // 2
    idx = [slice(None)] * len(ref.shape)
    idx[axis] = pl.ds(pl.multiple_of(h * n, 16), n)
    return ref.at[tuple(idx)]


def _remote(src, dst, send_sem, recv_sem, device):
    return pltpu.make_async_remote_copy(src_ref=src, dst_ref=dst, send_sem=send_sem, recv_sem=recv_sem,
                                        device_id=device, device_id_type=MESH)


def _gather_chips_task(shards, split, landed, part=0, nparts=1):
    keys = list(shards)
    n = len(keys)

    def operands():
        if part:
            return [landed[k] for k in keys]
        chip_me = _chip_index(lax.axis_index("x"), lax.axis_index("y"))
        return [lax.dynamic_update_slice(lax.empty((N_CHIPS,) + shards[k].shape, shards[k].dtype), shards[k][None],
                                         (chip_me,) + (0,) * shards[k].ndim) for k in keys]

    def my_rows(ref, c):
        rows = ref.shape[1] // (2 * nparts)
        return ref.at[:, pl.ds(pl.multiple_of((c * nparts + part) * rows, 16), rows), :]

    def make_direct(ins, outs, send_sem, recv_sem):
        x, y, c, chips = _mesh_position()
        s_me = _chip_index(x, y)
        starts, arrivals = [], []
        for g in range(n):
            for k, chip in enumerate(chips):
                sems = (send_sem(3 * g + k), recv_sem(3 * g + k))
                starts.append(functools.partial(_remote, outs[g].at[s_me], outs[g].at[s_me], *sems, (*chip, c)))
                got = outs[g].at[_chip_index(*chip)]
                arrivals.append(functools.partial(_remote, got, got, *sems, (*chip, c)))
        return starts, arrivals

    def axis_neighbours(x, y, c):
        flip = lambda v, f: v + f * (1 - 2 * v)
        return (flip(x, 1 - c), flip(y, c)), (flip(x, c), flip(y, 1 - c))

    def make_swap(ins, outs, send_sem, recv_sem):
        x, y, c, _ = _mesh_position()
        first, _ = axis_neighbours(x, y, c)
        starts, arrivals = [], []
        for g in range(n):
            sems = (send_sem(3 * g), recv_sem(3 * g))
            mine = my_rows(outs[g].at[_chip_index(x, y)], c)
            starts.append(functools.partial(_remote, mine, mine, *sems, (*first, c)))
            got = my_rows(outs[g].at[_chip_index(*first)], c)
            arrivals.append(functools.partial(_remote, got, got, *sems, (*first, c)))
        return starts, arrivals

    def make_pass_on(ins, outs, send_sem, recv_sem):
        x, y, c, _ = _mesh_position()
        first, second = axis_neighbours(x, y, c)
        diagonal = (1 - x, 1 - y)
        starts, arrivals = [], []
        for g in range(n):
            half = lambda chip: my_rows(outs[g].at[_chip_index(*chip)], c)
            for k, (sent, arriving) in enumerate([((x, y), second), (first, diagonal)]):
                sems = (send_sem(3 * g + 1 + k), recv_sem(3 * g + 1 + k))
                starts.append(functools.partial(_remote, half(sent), half(sent), *sems, (*second, c)))
                arrivals.append(functools.partial(_remote, half(arriving), half(arriving), *sems, (*second, c)))
        return starts, arrivals

    def finish(res):
        landed.update(zip(keys, res))

    shapes = lambda: [_sds((N_CHIPS,) + shards[k].shape, shards[k].dtype) for k in keys]
    aliases = {g: g for g in range(n)}
    if not split:
        return _Task("chips", operands, shapes, aliases, 3 * n, make_direct, finish)
    return _Task("neighbours", operands, shapes, aliases, 3 * n, make_swap, finish, make_second=make_pass_on)


def _gather_sibling_task(keys, landed, ready):
    n = len(keys)

    def make(ins, outs, send_sem, recv_sem):
        x, y, c, chips = _mesh_position()
        starts, arrivals = [], []
        for g in range(n):
            for k, chip in enumerate(chips):
                o = outs[g].at[_chip_index(*chip)]
                got, other = _rows_half(o, 1, c), _rows_half(o, 1, 1 - c)
                starts.append(functools.partial(_remote, got, got, send_sem(3 * g + k), recv_sem(3 * g + k),
                                                (x, y, 1 - c)))
                arrivals.append(functools.partial(_remote, other, other, send_sem(3 * g + k), recv_sem(3 * g + k),
                                                  (x, y, 1 - c)))
        return starts, arrivals

    def finish(res):
        ready.update(zip(keys, res))

    return _Task("sibling", lambda: [landed[k] for k in keys],
                 lambda: [_sds(landed[k].shape, landed[k].dtype) for k in keys],
                 {g: g for g in range(n)}, 3 * n, make, finish)


def _pair_swap_task(names, big, got):
    n = len(names)

    def make(ins, outs, send_sem, recv_sem):
        x, y, c, _ = _mesh_position()
        copies = [functools.partial(_remote, _rows_half(ins[a], 2, 1 - c), outs[a], send_sem(a), recv_sem(a),
                                    (x, y, 1 - c)) for a in range(n)]
        return copies, copies

    def shapes():
        return [_sds(big[k].shape[:2] + (big[k].shape[2] // 2, big[k].shape[3]), big[k].dtype) for k in names]

    return _Task("sibling", lambda: [big[k] for k in names], shapes, {}, n, make,
                 lambda res: got.update(zip(names, res)))


def _rs_pair_sum(name, fulls, gots, core):
    n = len(fulls)
    shapes = [(f.shape[2] // 2, f.shape[3]) for f in fulls]

    def body(core_ref, *refs):
        for a_ref, b_ref, o_ref in zip(refs[:n], refs[n:2 * n], refs[2 * n:]):
            o_ref[...] = (a_ref[...].astype(F32) + b_ref[...].astype(F32)).astype(BF16)

    mine = [pl.BlockSpec((None, None) + hc, lambda s, core_ref: (0, s, core_ref[0], 0)) for hc in shapes]
    slot = [pl.BlockSpec((None, None) + hc, lambda s, core_ref: (0, s, 0, 0)) for hc in shapes]
    return _pcall(
        body, name=name, grid=(N_CHIPS,), num_prefetch=1,
        in_specs=mine + slot, out_specs=slot,
        out_shape=[_sds((1, N_CHIPS) + hc, BF16) for hc in shapes],
    )(core, *fulls, *gots)


def _chip_exchange_task(names, pair_sums, by_source, part=0, nparts=1):
    n = len(names)

    def rows(ref):
        h = ref.shape[1] // nparts
        return ref.at[:, pl.ds(part * h, h), :]

    def make(ins, outs, send_sem, recv_sem):
        x, y, c, chips = _mesh_position()
        s_me = _chip_index(x, y)
        starts, arrivals = [], []
        for a in range(n):
            for k, chip in enumerate(chips):
                s_k = _chip_index(*chip)
                starts.append(functools.partial(_remote, rows(ins[a].at[:, s_k]), rows(outs[a].at[:, s_me]),
                                                send_sem(3 * a + k), recv_sem(3 * a + k), (*chip, c)))
                got = rows(outs[a].at[:, s_k])
                arrivals.append(functools.partial(_remote, got, got, send_sem(3 * a + k), recv_sem(3 * a + k),
                                                  (*chip, c)))
        return starts, arrivals

    def operands():
        return [pair_sums[k] for k in names] + ([by_source[k] for k in names] if part else [])

    return _Task("chips", operands, lambda: [_sds(pair_sums[k].shape, pair_sums[k].dtype) for k in names],
                 {n + a: a for a in range(n)} if part else {}, 3 * n, make,
                 lambda res: by_source.update(zip(names, res)))


def _rs_chip_sum(name, owns, parts, chip):
    n = len(owns)
    ns = N_CHIPS
    shapes = [p.shape[2:] for p in parts]

    def body(chip_ref, *refs):
        me = chip_ref[0]
        for i in range(n):
            own_v = refs[i][...].astype(F32)
            slots = refs[n + ns * i:n + ns * (i + 1)]
            tot = None
            for s in range(ns):
                term = jnp.where(me == s, own_v, slots[s][...].astype(F32))
                tot = term if tot is None else tot + term
            refs[n + ns * n + i][...] = tot

    def slot_spec(hc, s):
        return pl.BlockSpec((None, None) + hc,
                            lambda g, chip_ref: (0, jnp.where(chip_ref[0] == s, (s + 1) % ns, s), 0, 0))

    own_specs = [pl.BlockSpec((None, None) + hc, lambda g, chip_ref: (0, chip_ref[0], 0, 0)) for hc in shapes]
    slot_specs = [slot_spec(hc, s) for hc in shapes for s in range(ns)]
    return _pcall(
        body, name=name, grid=(1,), num_prefetch=1,
        in_specs=own_specs + slot_specs,
        out_specs=[pl.BlockSpec((None,) + hc, lambda g, chip_ref: (0, 0, 0)) for hc in shapes],
        out_shape=[_sds((1,) + hc, F32) for hc in shapes],
    )(chip, *owns, *[p for p in parts for _ in range(ns)])


def _pair_gather_task(names, halves, sibling_halves):
    n = len(names)

    def make(ins, outs, send_sem, recv_sem):
        x, y, c, _ = _mesh_position()
        copies = [functools.partial(_remote, ins[a], outs[a], send_sem(a), recv_sem(a), (x, y, 1 - c))
                  for a in range(n)]
        return copies, copies

    return _Task("sibling", lambda: [halves[k] for k in names], lambda: [_sds(halves[k].shape, F32) for k in names],
                 {}, n, make, lambda res: sibling_halves.update(zip(names, res)))


def _small_allreduce(arrs):
    n = len(arrs)
    per = 1 + 2 * N_PEER_CHIPS

    def body(*refs):
        v_refs, o_refs = refs[:n], refs[n:2 * n]
        sib, pair, part = refs[2 * n:3 * n], refs[3 * n:4 * n], refs[4 * n:5 * n]
        send_sems, recv_sems = refs[5 * n:]
        x, y, c, chips = _mesh_position()
        s_me = _chip_index(x, y)

        def quarter(ref, s):
            q = ref.shape[0] // N_CHIPS
            return ref.at[pl.ds(pl.multiple_of(s * q, 8), q)]

        def exchange(first_sem, src, dst_of, arrival_of):
            sems = lambda a, k: (send_sems.at[a * per + first_sem + k], recv_sems.at[a * per + first_sem + k])
            sends = [_remote(src(a, _chip_index(*chip)), dst_of(a, s_me), *sems(a, k), (*chip, c))
                     for a in range(n) for k, chip in enumerate(chips)]
            for cp in sends:
                cp.start()
            for a in range(n):
                for k, chip in enumerate(chips):
                    got = arrival_of(a, _chip_index(*chip))
                    _remote(got, got, *sems(a, k), (*chip, c)).wait_recv()
            for cp in sends:
                cp.wait_send()

        swaps = [_remote(v_refs[a], sib[a], send_sems.at[a * per], recv_sems.at[a * per], (x, y, 1 - c))
                 for a in range(n)]
        for cp in swaps:
            cp.start()
        for cp in swaps:
            cp.wait()
        for a in range(n):
            pair[a][...] = v_refs[a][...] + sib[a][...]
        exchange(1, lambda a, s_k: quarter(pair[a], s_k), lambda a, s: part[a].at[s], lambda a, s_k: part[a].at[s_k])
        for a in range(n):
            part[a][s_me] = quarter(pair[a], s_me)[...]
            q = o_refs[a].shape[0] // N_CHIPS
            o_refs[a][pl.ds(pl.multiple_of(s_me * q, 8), q), :] = (
                ((part[a][0] + part[a][1]) + part[a][2]) + part[a][3])
        exchange(1 + N_PEER_CHIPS, lambda a, s_k: quarter(o_refs[a], s_me), lambda a, s: quarter(o_refs[a], s),
                 lambda a, s_k: quarter(o_refs[a], s_k))

    shapes = [a.shape for a in arrs]
    return _pcall(
        body, name="small_allreduce", grid=(1,), own_peers=("sibling", "chips"),
        in_specs=[VMEM_SPEC] * n, out_specs=[VMEM_SPEC] * n, out_shape=[_sds(s, F32) for s in shapes],
        scratch_shapes=([pltpu.VMEM(s, F32) for s in shapes] * 2
                        + [pltpu.VMEM((N_CHIPS, s[0] // N_CHIPS, s[1]), F32) for s in shapes]
                        + [pltpu.SemaphoreType.DMA((n * per,)), pltpu.SemaphoreType.DMA((n * per,))]),
    )(*arrs)


TRANSPOSED_WEIGHTS = ("ffn1_w1", "ffn1_w3", "ffn2_w1", "ffn2_w3")
SMALL_LAYOUT = [("ffn1_norm", 1), ("mix_norm", 1), ("ret_gn", 1), ("conv_b", 1), ("b_rgate", 1), ("b_igate", 1),
                ("lru_lambda", 1), ("xattn_norm", 1), ("mem_norm", 1), ("ffn2_norm", 1), ("final_norm", 1),
                ("b_branch_gate", 2), ("conv_w", CONV_TAPS)]
SMALL_ROWS = 32
GATE_WEIGHTS = ("w_rgate", "w_igate")
WEIGHT_ORDER = ["ffn1_norm", "ffn1_w1", "ffn1_w3", "ffn1_w2", "mix_norm", "w_in", "ret_gn", "w_ret_o", "conv_w",
                "conv_b", "w_rgate", "b_rgate", "w_igate", "b_igate", "lru_lambda", "w_lru_o", "w_branch_gate",
                "b_branch_gate", "w_out", "xattn_norm", "mem_norm", "w_xq", "w_xk", "w_xv", "w_xo", "ffn2_norm",
                "ffn2_w1", "ffn2_w3", "ffn2_w2", "final_norm"]


SMALL_USED_ROWS = sum(n for _, n in SMALL_LAYOUT)


def _pack_small(parts, extra_row=None):
    rows = [parts[name].reshape(n, D) for name, n in SMALL_LAYOUT]
    if extra_row is not None:
        rows.append(extra_row)
    rows.append(jnp.zeros((SMALL_ROWS - sum(r.shape[0] for r in rows), D), F32))
    return jnp.concatenate(rows, axis=0)


def _unpack_small(packed, shapes):
    out, r = {}, 0
    for name, n in SMALL_LAYOUT:
        out[name] = packed[r:r + n].reshape(shapes[name])
        r += n
    return out


def kernel(x, mem, ffn1_norm, ffn1_w1, ffn1_w3, ffn1_w2, mix_norm, w_in, ret_gn, w_ret_o, conv_w, conv_b, w_rgate, b_rgate, w_igate, b_igate, lru_lambda, w_lru_o, w_branch_gate, b_branch_gate, w_out, xattn_norm, mem_norm, w_xq, w_xk, w_xv, w_xo, ffn2_norm, ffn2_w1, ffn2_w3, ffn2_w2, final_norm, loss_target, m_ffn1_norm, m_ffn1_w1, m_ffn1_w3, m_ffn1_w2, m_mix_norm, m_w_in, m_ret_gn, m_w_ret_o, m_conv_w, m_conv_b, m_w_rgate, m_b_rgate, m_w_igate, m_b_igate, m_lru_lambda, m_w_lru_o, m_w_branch_gate, m_b_branch_gate, m_w_out, m_xattn_norm, m_mem_norm, m_w_xq, m_w_xk, m_w_xv, m_w_xo, m_ffn2_norm, m_ffn2_w1, m_ffn2_w3, m_ffn2_w2, m_final_norm, v_ffn1_norm, v_ffn1_w1, v_ffn1_w3, v_ffn1_w2, v_mix_norm, v_w_in, v_ret_gn, v_w_ret_o, v_conv_w, v_conv_b, v_w_rgate, v_b_rgate, v_w_igate, v_b_igate, v_lru_lambda, v_w_lru_o, v_w_branch_gate, v_b_branch_gate, v_w_out, v_xattn_norm, v_mem_norm, v_w_xq, v_w_xk, v_w_xv, v_w_xo, v_ffn2_norm, v_ffn2_w1, v_ffn2_w3, v_ffn2_w2, v_final_norm):
    given = dict(locals())
    w = {n: given[n] for n in WEIGHT_ORDER}
    mom = {n: given["m_" + n] for n in WEIGHT_ORDER}
    var = {n: given["v_" + n] for n in WEIGHT_ORDER}
    chip = _chip_index(lax.axis_index("x"), lax.axis_index("y"))
    core = lax.axis_index("c").astype(jnp.int32).reshape(1)

    chip_id = chip.astype(jnp.int32).reshape(1)
    sm = {n: w[n] for n in ["ffn1_norm", "mix_norm", "ret_gn", "conv_b", "b_rgate", "b_igate", "lru_lambda",
                            "xattn_norm", "mem_norm", "ffn2_norm", "b_branch_gate"]}
    sm["final_norm"] = w["final_norm"].reshape(1, D)
    sm["w_rgate"] = w["w_rgate"][0]
    sm["w_igate"] = w["w_igate"][0]

    local = lambda a, n: jnp.swapaxes(a[0], 0, 1) if n in TRANSPOSED_WEIGHTS else a[0]
    stack = lambda names: jnp.stack([local(w[n], n) for n in names], axis=0).astype(BF16)
    shard = {"col1": stack(["ffn1_w1", "ffn1_w3"]), "row2a": stack(["ffn1_w2"]),
             "win": jnp.swapaxes(w["w_in"], 1, 2).astype(BF16),
             "wbg": jnp.swapaxes(w["w_branch_gate"], 1, 2).astype(BF16),
             "sqA": stack(["w_ret_o", "w_lru_o", "w_out"]), "sqB": stack(["w_xq", "w_xk"]),
             "sqC": stack(["w_xv", "w_xo"]), "col2a": stack(["ffn2_w1"]), "col2b": stack(["ffn2_w3"]),
             "row2b": stack(["ffn2_w2"]), "conv": w["conv_w"]}
    gw, landed = {}, {}
    over_chips = lambda keys: _gather_chips_task({k: shard[k] for k in keys}, True, landed)
    to_sibling = lambda keys: _gather_sibling_task(keys, landed, gw)

    big, got, pair_sums, by_source, halves, sibling_halves, outs = {}, {}, {}, {}, {}, {}, {}
    pair_swap = lambda names: _pair_swap_task(names, big, got)
    exchange = lambda names, part=0, nparts=1: _chip_exchange_task(names, pair_sums, by_source, part, nparts)
    pair_gather = lambda names: _pair_gather_task(names, halves, sibling_halves)

    def pair_sum(names):
        res = _rs_pair_sum("rs_pair_sum_" + names[0], [big[n] for n in names], [got[n] for n in names], core)
        pair_sums.update(zip(names, res))

    def chip_sum(names):
        res = _rs_chip_sum("rs_chip_sum_" + names[0], [pair_sums[n] for n in names], [by_source[n] for n in names],
                           chip_id)
        halves.update(zip(names, res))

    def adamw(names):
        for n in names:
            res = _adamw_halves("adamw_" + n, local(w[n], n), halves[n], sibling_halves[n], 0, local(mom[n], n),
                                local(var[n], n), core)
            outs[n] = tuple((jnp.swapaxes(r, 0, 1) if n in TRANSPOSED_WEIGHTS else r)[None] for r in res)

    do = lambda fn, names: functools.partial(fn, names)
    ffn2_grads = ["ffn2_w2", "ffn2_w1", "ffn2_w3"]
    xattn_grads = ["w_xo", "w_xq", "w_xk", "w_xv"]
    mix_out_grads = ["w_branch_gate", "w_out", "w_ret_o", "w_lru_o"]
    conv_gather = _gather_chips_task({"conv": shard["conv"]}, False, gw)
    half = lambda key, part: _gather_chips_task({key: shard[key]}, True, landed, part, 2)
    plan = _Plan()
    plan.tasks = {
        "ag_first_chips": [over_chips(["col1", "row2a"])],
        "ag_first_sibling": [to_sibling(["col1", "row2a"])],
        "ffn1_up": [over_chips(["win"])],
        "ffn1_down": [to_sibling(["win"]), over_chips(["wbg"]), conv_gather],
        "mix_in": [to_sibling(["wbg"]), over_chips(["sqA"])],
        "ret_fwd": [to_sibling(["sqA"]), over_chips(["col2a"])],
        "lru_gates_fwd": [to_sibling(["col2a"]), over_chips(["sqB"])],
        "lru_scan_fwd": [to_sibling(["sqB"]), over_chips(["sqC"])],
        "mix_gates": [to_sibling(["sqC"]), half("col2b", 0)],
        "y_lru": [half("col2b", 1)],
        "xattn_fwd": [to_sibling(["col2b"])],
        "ffn2_up": [over_chips(["row2b"])],
        "ffn2_up_sibling": [to_sibling(["row2b"])],
        "ffn2_dh": [pair_swap(ffn2_grads)],
        "xattn_bwd": [exchange(["ffn2_w2"], 0, 2)],
        "d_hq": [exchange(["ffn2_w2"], 1, 2)],
        "d_merged": [exchange(["ffn2_w1"], 0, 2), pair_swap(xattn_grads)],
        "lru_out_bwd": [exchange(["w_xo"])],
        "ret_bwd": [exchange(["ffn2_w1"], 1, 2), exchange(["ffn2_w3"], 0, 2), pair_swap(mix_out_grads)],
        "lru_scan_bwd": [exchange(["ffn2_w3"], 1, 2)],
        "lru_gates_bwd": [exchange(["w_xq", "w_xk"]), pair_gather(ffn2_grads)],
        "dw_in": [exchange(["w_xv", "w_out"])],
        "d_h2": [exchange(["w_branch_gate", "w_ret_o", "w_lru_o"]), pair_swap(["w_in"]), pair_gather(xattn_grads)],
        "ffn1_bwd_mid": [exchange(["w_in"], 0, 2), pair_gather(mix_out_grads)],
        "ffn1_dw2": [exchange(["w_in"], 2, 4)],
        "ffn1_dw1": [exchange(["w_in"], 3, 4), pair_swap(["ffn1_w2"])],
        "ffn1_dw3": [exchange(["ffn1_w2"], 0, 2), pair_swap(["ffn1_w1"]), pair_gather(["w_in"])],
        "ffn1_dh": [exchange(["ffn1_w2"], 1, 2), exchange(["ffn1_w1"]), pair_swap(["ffn1_w3"])],
        "small_allreduce": [exchange(["ffn1_w3"]), pair_gather(["ffn1_w2"])],
        "rs_last_gather": [pair_gather(["ffn1_w1", "ffn1_w3"])],
    }
    plan.after = {
        "ffn2_up": [functools.partial(_comm_call, "ffn2_up_sibling")],
        "ffn2_dh": [do(pair_sum, ffn2_grads)],
        "d_merged": [do(pair_sum, xattn_grads)],
        "ret_bwd": [do(pair_sum, mix_out_grads)],
        "lru_scan_bwd": [do(chip_sum, ffn2_grads)],
        "lru_gates_bwd": [do(adamw, ffn2_grads)],
        "dw_in": [do(chip_sum, xattn_grads)],
        "d_h2": [do(chip_sum, mix_out_grads), do(pair_sum, ["w_in"]), do(adamw, xattn_grads)],
        "ffn1_bwd_mid": [do(adamw, mix_out_grads)],
        "ffn1_dw1": [do(chip_sum, ["w_in"]), do(pair_sum, ["ffn1_w2"])],
        "ffn1_dw3": [do(pair_sum, ["ffn1_w1"]), do(adamw, ["w_in"])],
        "ffn1_dh": [do(pair_sum, ["ffn1_w3"]), do(chip_sum, ["ffn1_w2"])],
        "small_allreduce": [do(chip_sum, ["ffn1_w1", "ffn1_w3"]), functools.partial(_comm_call, "rs_last_gather"),
                    do(adamw, ["ffn1_w2", "ffn1_w1", "ffn1_w3"])],
    }
    global _plan
    _plan = plan
    try:
        _comm_call("ag_first_chips")
        _comm_call("ag_first_sibling")
        loss_part, grad_x, small = _local_step(x[0], mem[0], loss_target[0], gw, sm, big)
        gate2d = lambda a: a.reshape(LRU_BLOCKS * LRU_BLOCK, LRU_BLOCK)
        loss_row = jnp.pad(loss_part, ((0, 0), (0, D - loss_part.shape[1])))
        small_sum, *gate_sums = _small_allreduce([_pack_small(small, loss_row)]
                                                 + [gate2d(small[n]) for n in GATE_WEIGHTS])
    finally:
        _plan = None
    assert not plan.tasks and not plan.after, (list(plan.tasks), list(plan.after))
    loss = small_sum[SMALL_USED_ROWS, 0]

    small_shapes = {n: w[n].shape for n, _ in SMALL_LAYOUT}
    small_shapes["conv_w"] = (CONV_TAPS, D)
    conv_row = SMALL_USED_ROWS - CONV_TAPS
    conv_grad = lax.dynamic_slice(small_sum[conv_row:conv_row + CONV_TAPS], (0, chip * SQ_BLK), (CONV_TAPS, SQ_BLK))
    small_w = {n: w[n] for n, _ in SMALL_LAYOUT}
    small_m = {n: mom[n] for n, _ in SMALL_LAYOUT}
    small_v = {n: var[n] for n, _ in SMALL_LAYOUT}
    pad_cols = lambda a: jnp.pad(a[0], ((0, 0), (0, D - SQ_BLK)))
    for dct in (small_w, small_m, small_v):
        dct["conv_w"] = pad_cols(dct["conv_w"])
    g_pack = lax.dynamic_update_slice(small_sum, jnp.pad(conv_grad, ((0, 0), (0, D - SQ_BLK))), (conv_row, 0))
    d_pack, m_pack, v_pack = _adamw("adamw_small", _pack_small(small_w), g_pack, _pack_small(small_m),
                                    _pack_small(small_v))
    unpacked = [_unpack_small(p, small_shapes) for p in (g_pack, d_pack, m_pack, v_pack)]
    for n, _ in SMALL_LAYOUT:
        if n == "conv_w":
            outs[n] = tuple(u[n][:, :SQ_BLK][None] for u in unpacked)
        else:
            outs[n] = tuple(u[n] for u in unpacked)
    for n, gsum in zip(GATE_WEIGHTS, gate_sums):
        d, nm, nv = _adamw("adamw_" + n, gate2d(w[n]), gsum, gate2d(mom[n]), gate2d(var[n]))
        outs[n] = tuple(r.reshape(w[n].shape) for r in (gsum, d, nm, nv))

    result = [loss, grad_x[None]]
    for k in range(4):
        result += [outs[n][k] for n in WEIGHT_ORDER]
    return tuple(result)
```

```python
import functools
import math

import jax
import jax.numpy as jnp
from jax import lax
from jax.experimental import pallas as pl
from jax.experimental.pallas import tpu as pltpu

F32 = jnp.float32
BF16 = jnp.bfloat16
GRAD_WIRE_DTYPE = BF16
MESH = pl.DeviceIdType.MESH

D = 1024
EPS = 1e-6
RET_HEADS = 4
RET_DK = 128
RET_DV = 256
CHUNK = 128
ROPE_BASE = 10000.0
LRU_BLOCKS = 8
LRU_BLOCK = 128
CONV_TAPS = 4
LRU_C = 8.0
D_FF = 2816
X_HEADS = 4
X_HD = 256
N_CHIPS = 4
FF_BLK = D_FF // N_CHIPS
IN_BLK = 5120 // N_CHIPS
BG_BLK = 2048 // N_CHIPS
SQ_BLK = D // N_CHIPS

ADAM_LR = 0.001
ADAM_B1 = 0.9
ADAM_B2 = 0.999
ADAM_EPS = 1e-08
ADAM_WD = 0.01
ADAM_STEP = 10

VMEM_LIMIT_BYTES = 56 * 1024 * 1024
ROW_TILE = 512
WIDE_ROW_TILE = 1024
FFN_ROW_TILE = 256
DW_BLK = D_FF // 2
SCAN_TILE = 256
RET_STEP_CHUNKS = 2
RET_STEP_ROWS = RET_STEP_CHUNKS * CHUNK

_DN = {
    "nn": (((1,), (0,)), ((), ())),
    "nt": (((1,), (1,)), ((), ())),
    "tn": (((0,), (0,)), ((), ())),
}


def _cparams(n_axes, collective_id=None):
    return pltpu.CompilerParams(dimension_semantics=("arbitrary",) * n_axes,
                                vmem_limit_bytes=VMEM_LIMIT_BYTES, collective_id=collective_id)


def _dot(a, b, kind):
    if b.ndim == 3:
        b = b.reshape(b.shape[0] * b.shape[1], b.shape[2])
    return lax.dot_general(a.astype(BF16), b.astype(BF16), _DN[kind], preferred_element_type=F32)


def _sigmoid(x):
    return 1.0 / (1.0 + jnp.exp(-x))


def _log1p_pos(e):
    u = 1.0 + e
    return jnp.where(u == 1.0, e, jnp.log(u) * (e / jnp.where(u == 1.0, 1.0, u - 1.0)))


def _expm1(x):
    u = jnp.exp(x)
    lu = jnp.log(u)
    safe = jnp.where(lu == 0.0, 1.0, lu)
    return jnp.where(u == 1.0, x, (u - 1.0) * (x / safe))


def _softplus(z):
    return jnp.maximum(z, 0.0) + _log1p_pos(jnp.exp(-jnp.abs(z)))


_GELU_C = math.sqrt(2.0 / math.pi)


def _gelu_and_grad(x):
    x2 = x * x
    t = jnp.tanh(_GELU_C * (x + 0.044715 * x * x2))
    g = 0.5 * x * (1.0 + t)
    dg = 0.5 * (1.0 + t) + 0.5 * x * (1.0 - t * t) * (_GELU_C * (1.0 + 3.0 * 0.044715 * x2))
    return g, dg


def _rms_fwd(x, g):
    r = lax.rsqrt(jnp.mean(x * x, axis=-1, keepdims=True) + EPS)
    return (x * r) * g


def _rms_bwd(x, g, dh):
    r = lax.rsqrt(jnp.mean(x * x, axis=-1, keepdims=True) + EPS)
    n = x * r
    dyg = dh * g
    dx = r * (dyg - n * jnp.mean(dyg * n, axis=-1, keepdims=True))
    return dx, jnp.sum(dh * n, axis=0, keepdims=True)


def _accumulate(ref, val, first):
    @pl.when(first)
    def _():
        ref[...] = val

    @pl.when(jnp.logical_not(first))
    def _():
        ref[...] += val


def _sds(shape, dtype):
    return jax.ShapeDtypeStruct(tuple(shape), dtype)


def _spec(shape, fn):
    return pl.BlockSpec(tuple(shape), fn)


class _Task:
    def __init__(self, peers, operands, out_shapes, aliases, nsem, make, finish, make_second=None):
        self.peers = peers
        self.operands, self.out_shapes, self.aliases = operands, out_shapes, aliases
        self.nsem, self.make, self.finish = nsem, make, finish
        self.make_second = make_second


class _Plan:
    def __init__(self):
        self.tasks, self.after = {}, {}


_plan = None


PEER_SET_COLLECTIVE_ID = {frozenset({"sibling"}): 1, frozenset({"chips"}): 2, frozenset({"sibling", "chips"}): 3,
                          frozenset({"neighbours"}): 4, frozenset({"sibling", "neighbours"}): 5}


def _peer_set(names):
    names = frozenset(names)
    return names - {"neighbours"} if "chips" in names else names


def _entry_handshake(peer_set):
    x, y, c, chips = _mesh_position()
    peers = [(x, y, 1 - c)] if "sibling" in peer_set else []
    if "chips" in peer_set:
        peers += [(*chip, c) for chip in chips]
    if "neighbours" in peer_set:
        peers += [(*chip, c) for chip in chips[:2]]
    barrier = pltpu.get_barrier_semaphore()
    for peer in peers:
        pl.semaphore_signal(barrier, inc=1, device_id=peer, device_id_type=MESH)
    pl.semaphore_wait(barrier, len(peers))


def _pcall(body, *, name, grid, in_specs, out_specs, out_shape, scratch_shapes=(), num_prefetch=0, own_peers=()):
    single = not isinstance(out_shape, (list, tuple))
    out_shape = [out_shape] if single else list(out_shape)
    out_specs = [out_specs] if single else list(out_specs)
    in_specs = list(in_specs)
    scratch_shapes = list(scratch_shapes)
    tasks = _plan.tasks.pop(name, []) if _plan is not None else []
    after = _plan.after.pop(name, []) if _plan is not None else []
    peer_set = _peer_set([t.peers for t in tasks] + list(own_peers))
    nax = len(grid)

    def run(*operands):
        n_in = len(operands) - num_prefetch
        n_out = len(out_shape)
        t_ops = [t.operands() for t in tasks]
        t_outs = [t.out_shapes() for t in tasks]
        c_ops = [a for ops in t_ops for a in ops]
        c_outs = [s for outs in t_outs for s in outs]
        aliases = {}
        i0, o0 = num_prefetch + n_in, n_out
        for t, ops, outs in zip(tasks, t_ops, t_outs):
            for i_loc, o_loc in t.aliases.items():
                aliases[i0 + i_loc] = o0 + o_loc
            i0 += len(ops)
            o0 += len(outs)
        nsem = sum(t.nsem for t in tasks)

        def wrapped(*refs):
            p = num_prefetch
            pre, ins = refs[:p], refs[p:p + n_in]
            cins = refs[p + n_in:p + n_in + len(c_ops)]
            q = p + n_in + len(c_ops)
            outs, couts = refs[q:q + n_out], refs[q + n_out:q + n_out + len(c_outs)]
            q += n_out + len(c_outs)
            scr = refs[q:q + len(scratch_shapes)]

            def rounds(second):
                send_sems, recv_sems = refs[q + len(scratch_shapes):]
                out = []
                ci = co = so = 0
                for t, ops, souts in zip(tasks, t_ops, t_outs):
                    make = t.make_second if second else t.make
                    out.append(([], []) if make is None else
                               make(cins[ci:ci + len(ops)], couts[co:co + len(souts)],
                                    functools.partial(lambda base, k: send_sems.at[base + k], so),
                                    functools.partial(lambda base, k: recv_sems.at[base + k], so)))
                    ci, co, so = ci + len(ops), co + len(souts), so + t.nsem
                return out

            two_rounds = [t.make_second is not None for t in tasks]
            if peer_set:
                ids = [pl.program_id(k) for k in range(nax)]
                first = functools.reduce(jnp.logical_and, [i == 0 for i in ids])
                last = functools.reduce(jnp.logical_and, [i == g - 1 for i, g in zip(ids, grid)])
                step = functools.reduce(lambda acc, ig: acc * ig[1] + ig[0], zip(ids, grid), 0)
                middle = step == math.prod(grid) // 3

                @pl.when(first)
                def _():
                    _entry_handshake(peer_set)
                    for starts, _ in rounds(False):
                        for copy in starts:
                            copy().start()

            body(*pre, *ins, *outs, *scr)

            if any(two_rounds):
                @pl.when(middle)
                def _():
                    for (_, arrivals), two in zip(rounds(False), two_rounds):
                        if two:
                            for arrival in arrivals:
                                arrival().wait_recv()
                    for starts, _ in rounds(True):
                        for copy in starts:
                            copy().start()

            if tasks:
                @pl.when(last)
                def _():
                    first_round, second_round = rounds(False), rounds(True)
                    for (_, arrivals1), (_, arrivals2), two in zip(first_round, second_round, two_rounds):
                        for arrival in (arrivals2 if two else arrivals1):
                            arrival().wait_recv()
                    for starts, _ in first_round + second_round:
                        for copy in starts:
                            copy().wait_send()

        sems = [pltpu.SemaphoreType.DMA((nsem,)), pltpu.SemaphoreType.DMA((nsem,))] if tasks else []
        res = pl.pallas_call(
            wrapped, name=name,
            grid_spec=pltpu.PrefetchScalarGridSpec(
                num_scalar_prefetch=num_prefetch, grid=tuple(grid),
                in_specs=in_specs + [ANY_SPEC] * len(c_ops),
                out_specs=out_specs + [ANY_SPEC] * len(c_outs),
                scratch_shapes=scratch_shapes + sems),
            out_shape=out_shape + c_outs,
            input_output_aliases=aliases,
            compiler_params=_cparams(nax, PEER_SET_COLLECTIVE_ID[peer_set] if peer_set else None),
        )(*operands, *c_ops)
        co = n_out
        for t, souts in zip(tasks, t_outs):
            t.finish(res[co:co + len(souts)])
            co += len(souts)
        for fn in after:
            fn()
        return res[0] if single else list(res[:n_out])

    return run


def _comm_call(name):
    def body(o_ref):
        o_ref[...] = jnp.zeros_like(o_ref)

    _pcall(body, name=name, grid=(1,), in_specs=[], out_specs=_spec((8, 128), lambda i: (0, 0)),
           out_shape=_sds((8, 128), F32))()


def _gemm(name, terms, grid, outs, acc_shape, extras=(), epilogue=None):
    kinds = [t[4] for t in terms]
    nt, ne, no = len(terms), len(extras), len(outs)
    nred = grid[-1]
    nax = len(grid)

    def body(*refs):
        trefs = refs[:2 * nt]
        erefs = refs[2 * nt:2 * nt + ne]
        orefs = refs[2 * nt + ne:2 * nt + ne + no]
        ids = [pl.program_id(k) for k in range(nax)]
        tot = None
        for t in range(nt):
            d = _dot(trefs[2 * t][...], trefs[2 * t + 1][...], kinds[t])
            tot = d if tot is None else tot + d

        def finish(acc):
            if epilogue is None:
                orefs[0][...] = acc.astype(orefs[0].dtype)
            else:
                epilogue(acc, erefs, orefs, ids)

        if nred == 1:
            finish(tot)
        else:
            acc_ref = refs[-1]
            r = ids[-1]

            @pl.when(r == 0)
            def _():
                acc_ref[...] = tot

            @pl.when(r > 0)
            def _():
                acc_ref[...] += tot

            @pl.when(r == nred - 1)
            def _():
                finish(acc_ref[...])

    operands, in_specs = [], []
    for a, a_spec, b, b_spec, _ in terms:
        operands += [a, b]
        in_specs += [a_spec, b_spec]
    for e, e_spec in extras:
        operands.append(e)
        in_specs.append(e_spec)
    scratch = [pltpu.VMEM(tuple(acc_shape), F32)] if nred > 1 else []
    return _pcall(body, name=name, grid=tuple(grid), in_specs=in_specs, out_specs=[o[1] for o in outs],
                  out_shape=[o[0] for o in outs], scratch_shapes=scratch)(*operands)


def _rowwise(name, fn, ins, outs, grid):
    ni = len(ins)
    nax = len(grid)

    def body(*refs):
        ids = [pl.program_id(k) for k in range(nax)]
        fn(refs[:ni], refs[ni:], ids)

    return _pcall(body, name=name, grid=tuple(grid), in_specs=[i[1] for i in ins],
                  out_specs=[o[1] for o in outs], out_shape=[o[0] for o in outs])(*[i[0] for i in ins])


def _ffn_up(name, h, w1buf, w1_idx, w3buf, w3_idx, norm_gain=None):
    T = h.shape[0]
    tm = min(FFN_ROW_TILE, T)
    normed = norm_gain is not None

    def body(h_ref, *refs):
        if normed:
            g_ref, w1_ref, w3_ref, a_ref, b_ref, s_ref, hn_ref = refs
            hv = _rms_fwd(h_ref[...], g_ref[...]).astype(BF16)
            hn_ref[...] = hv
        else:
            w1_ref, w3_ref, a_ref, b_ref, s_ref = refs
            hv = h_ref[...]
        a = _dot(hv, w1_ref[...], "nt")
        b = _dot(hv, w3_ref[...], "nt")
        a_ref[...] = a.astype(BF16)
        b_ref[...] = b.astype(BF16)
        s_ref[...] = ((a * _sigmoid(a)) * b).astype(BF16)

    row = _spec((tm, D), lambda i: (i, 0))
    blk = _spec((tm, D_FF), lambda i: (i, 0))
    return _pcall(
        body, name=name, grid=(T // tm,),
        in_specs=[row] + ([_spec((1, D), lambda i: (0, 0))] if normed else [])
        + [_spec((N_CHIPS, None, FF_BLK, D), lambda i: (0, w1_idx, 0, 0)),
           _spec((N_CHIPS, None, FF_BLK, D), lambda i: (0, w3_idx, 0, 0))],
        out_specs=[blk, blk, blk] + ([row] if normed else []),
        out_shape=[_sds((T, D_FF), BF16)] * 3 + ([_sds((T, D), BF16)] if normed else []),
    )(h, *([norm_gain] if normed else []), w1buf, w3buf)


def _loss_head(x, g, tgt, loss_ref, dx_ref, dg_ref, first):
    err = _rms_fwd(x, g) - tgt
    lp = 0.5 * jnp.sum(jnp.mean(err * err, axis=-1, keepdims=True), axis=0, keepdims=True)
    _accumulate(loss_ref, jnp.broadcast_to(lp, (1, 128)), first)
    dx, dgp = _rms_bwd(x, g, err * (1.0 / D))
    dx_ref[...] = dx
    _accumulate(dg_ref, dgp, first)


def _ffn_down(name, s, wrow2, w2_idx, x_res, g_next=None, loss_target=None):
    T = x_res.shape[0]
    tm = min(ROW_TILE, T)
    row = lambda i, j, r: (i, 0)
    vec = lambda i, j, r: (0, 0)

    def epilogue(acc, erefs, orefs, ids):
        xo = erefs[0][...] + 0.5 * acc
        if loss_target is not None:
            _loss_head(xo, erefs[1][...], erefs[2][...], orefs[0], orefs[1], orefs[2], ids[0] == 0)
            return
        orefs[0][...] = xo
        orefs[1][...] = _rms_fwd(xo, erefs[1][...]).astype(BF16)

    extras = [(x_res, _spec((tm, D), row)), (g_next, _spec((1, D), vec))]
    if loss_target is None:
        outs = [(_sds((T, D), F32), _spec((tm, D), row)), (_sds((T, D), BF16), _spec((tm, D), row))]
    else:
        extras.append((loss_target, _spec((tm, D), row)))
        outs = [(_sds((1, 128), F32), _spec((1, 128), vec)), (_sds((T, D), F32), _spec((tm, D), row)),
                (_sds((1, D), F32), _spec((1, D), vec))]
    return _gemm(
        name,
        [(s, _spec((tm, D_FF), row),
          wrow2, _spec((N_CHIPS, None, FF_BLK, D), lambda i, j, r: (0, w2_idx, 0, 0)), "nn")],
        (T // tm, 1, 1), outs, (tm, D), extras, epilogue)


def _ffn_bwd_mid(name, dx, wrow2, w2_idx, a, b):
    T = dx.shape[0]
    tm = min(FFN_ROW_TILE, T)

    def body(dx_ref, w2_ref, a_ref, b_ref, dab_ref):
        ds = _dot(0.5 * dx_ref[...], w2_ref[...], "nt")
        av = a_ref[...].astype(F32)
        sg = _sigmoid(av)
        dab_ref[0] = (ds * b_ref[...].astype(F32) * (sg * (1.0 + av * (1.0 - sg)))).astype(BF16)
        dab_ref[1] = (ds * (av * sg)).astype(BF16)

    blk = _spec((tm, D_FF), lambda i: (i, 0))
    return _pcall(
        body, name=name, grid=(T // tm,),
        in_specs=[_spec((tm, D), lambda i: (i, 0)),
                  _spec((N_CHIPS, None, FF_BLK, D), lambda i: (0, w2_idx, 0, 0)),
                  blk, blk],
        out_specs=_spec((2, tm, D_FF), lambda i: (0, i, 0)),
        out_shape=_sds((2, T, D_FF), BF16),
    )(dx, wrow2, a, b)


def _rms_bwd_epilogue(acc, erefs, orefs, ids):
    dx, dgp = _rms_bwd(erefs[0][...], erefs[1][...], acc)
    orefs[0][...] = dx + erefs[2][...]
    _accumulate(orefs[1], dgp, ids[0] == 0)


def _rms_bwd_io(x, g, dres, T, tm):
    row = lambda i, j, r: (i, 0)
    vec = lambda i, j, r: (0, 0)
    extras = [(x, _spec((tm, D), row)), (g, _spec((1, D), vec)), (dres, _spec((tm, D), row))]
    outs = [(_sds((T, D), F32), _spec((tm, D), row)), (_sds((1, D), F32), _spec((1, D), vec))]
    return extras, outs


def _ffn_bwd(tag, dx_out, h, a, b, s, w1buf, w1_idx, w3buf, w3_idx, wrow2, w2_idx, x_in, g, big):
    T = dx_out.shape[0]
    dab = _ffn_bwd_mid(tag + "_bwd_mid", dx_out, wrow2, w2_idx, a, b)

    def half_scale(acc, erefs, orefs, ids):
        orefs[0][...] = (0.5 * acc).astype(orefs[0].dtype)

    dw_grid = (D_FF // DW_BLK, 1, 1)
    dw_out = [(_sds((D_FF, D), GRAD_WIRE_DTYPE), _spec((DW_BLK, D), lambda j, n, r: (j, 0)))]
    tokens = _spec((T, D), lambda j, n, r: (0, 0))
    big[tag + "_w2"] = _gemm(
        tag + "_dw2", [(s, _spec((T, DW_BLK), lambda j, n, r: (0, j)), dx_out, tokens, "tn")],
        dw_grid, dw_out, (DW_BLK, D), (), half_scale)[0].reshape(1, N_CHIPS, FF_BLK, D)
    for widx, wname in ((0, "_w1"), (1, "_w3")):
        big[tag + wname] = _gemm(
            tag + "_d" + wname[1:],
            [(dab, _spec((None, T, DW_BLK), functools.partial(lambda w, j, n, r: (w, 0, j), widx)), h, tokens, "tn")],
            dw_grid, dw_out, (DW_BLK, D))[0].reshape(1, N_CHIPS, FF_BLK, D)
    tm = min(FFN_ROW_TILE, T)
    extras, outs = _rms_bwd_io(x_in, g, dx_out, T, tm)
    whole = lambda idx: _spec((N_CHIPS, None, FF_BLK, D), lambda i, j, r: (0, idx, 0, 0))
    dx_in, dg = _gemm(
        tag + "_dh",
        [(dab, _spec((None, tm, D_FF), lambda i, j, r: (0, i, 0)), w1buf, whole(w1_idx), "nn"),
         (dab, _spec((None, tm, D_FF), lambda i, j, r: (1, i, 0)), w3buf, whole(w3_idx), "nn")],
        (T // tm, 1, 1), outs, (tm, D), extras, _rms_bwd_epilogue)
    return dx_in, dg


def _proj_sq(name, a, wsq, idx, kind, out_dtype=F32, extras=(), epilogue=None, outs=None):
    M = a.shape[0]
    tm = min(ROW_TILE, M)
    if outs is None:
        outs = [(_sds((M, D), out_dtype), _spec((tm, D), lambda i, j, r: (i, 0)))]
    return _gemm(
        name,
        [(a, _spec((tm, D), lambda i, j, r: (i, 0)),
          wsq, _spec((N_CHIPS, None, SQ_BLK, D), lambda i, j, r: (0, idx, 0, 0)), kind)],
        (M // tm, 1, 1), outs, (tm, D), extras, epilogue)


def _dw_sq(name, a, b):
    M = a.shape[0]
    tn = D // 2
    whole = _gemm(
        name,
        [(a, _spec((M, D), lambda i, j, r: (0, 0)), b, _spec((M, tn), lambda i, j, r: (0, j)), "tn")],
        (1, D // tn, 1),
        [(_sds((D, D), GRAD_WIRE_DTYPE), _spec((D, tn), lambda i, j, r: (0, j)))],
        (D, tn))[0]
    return whole.reshape(N_CHIPS, SQ_BLK, D)


def _retention_constants(T):
    pos = jnp.arange(T, dtype=F32)
    inv_freq = ROPE_BASE ** (-jnp.arange(0, RET_DK, 2, dtype=F32) / RET_DK)
    ang = pos[:, None] * inv_freq[None, :]
    cosf = jnp.concatenate([jnp.cos(ang), jnp.cos(ang)], axis=1)
    sins = jnp.concatenate([-jnp.sin(ang), jnp.sin(ang)], axis=1)
    lg = jnp.log(1.0 - 2.0 ** (-5.0 - jnp.arange(RET_HEADS, dtype=F32)))
    p = jnp.arange(CHUNK, dtype=F32)
    rel = p[:, None] - p[None, :]
    dmat = jnp.where(rel[None] >= 0, jnp.exp(rel[None] * lg[:, None, None]), 0.0)
    kd = jnp.exp((CHUNK - 1.0 - p)[None, :] * lg[:, None])[:, :, None]
    qd = jnp.exp((p + 1.0)[None, :] * lg[:, None])[:, :, None]
    cd = jnp.exp(CHUNK * lg)[:, None, None]
    return cosf, sins, dmat, kd, qd, cd


def _rot(t, cosv, sinv):
    return t * cosv + pltpu.roll(t, RET_DK // 2, 1) * sinv


def _unrot(t, cosv, sinv):
    return t * cosv - pltpu.roll(t, RET_DK // 2, 1) * sinv


def _ret_const_specs(cm):
    whole = lambda shape: _spec(shape, lambda c: (0,) * len(shape))
    return [
        _spec((RET_STEP_ROWS, RET_DK), lambda c: (cm(c), 0)),
        _spec((RET_STEP_ROWS, RET_DK), lambda c: (cm(c), 0)),
        whole((RET_HEADS, CHUNK, CHUNK)), whole((RET_HEADS, CHUNK, 1)), whole((RET_HEADS, CHUNK, 1)),
        whole((RET_HEADS, 1, 1)),
    ]


def _head(h, width):
    return slice(h * width, (h + 1) * width)


def _ret_fwd(u, consts, ret_gn):
    T = u.shape[0]
    nC = T // CHUNK
    kscale = RET_DK ** -0.5

    def body(q_ref, k_ref, v_ref, g_ref, cos_ref, sin_ref, dm_ref, kd_ref, qd_ref, cd_ref, gn_ref,
             qr_ref, kr_ref, ret_ref, yr_ref, st_ref, state):
        @pl.when(pl.program_id(0) == 0)
        def _():
            state[...] = jnp.zeros_like(state)

        for cc in range(RET_STEP_CHUNKS):
            rows = slice(cc * CHUNK, (cc + 1) * CHUNK)
            cosv, sinv = cos_ref[rows, :], sin_ref[rows, :]
            for h in range(RET_HEADS):
                hk, hv = _head(h, RET_DK), _head(h, RET_DV)
                q = _rot(q_ref[rows, hk], cosv, sinv)
                k = _rot(k_ref[rows, hk], cosv, sinv) * kscale
                v = v_ref[rows, hv]
                qr_ref[rows, hk] = q
                kr_ref[rows, hk] = k
                prev = state[h]
                st_ref[h, cc] = prev
                s = _dot(q, k, "nt") * dm_ref[h]
                ret = _dot(s, v, "nn") + _dot(q, prev, "nn") * qd_ref[h]
                state[h] = cd_ref[h] * prev + _dot(k * kd_ref[h], v, "tn")
                ret_ref[rows, hv] = ret
                mu = jnp.mean(ret, axis=-1, keepdims=True)
                xc = ret - mu
                yn = xc * lax.rsqrt(jnp.mean(xc * xc, axis=-1, keepdims=True) + EPS)
                g = g_ref[rows, hv]
                yr_ref[rows, hv] = ((g * _sigmoid(g)) * (yn * gn_ref[:, hv])).astype(BF16)

    cm = lambda c: c
    qk_w, v_w = RET_HEADS * RET_DK, RET_HEADS * RET_DV
    in_specs = [
        _spec((RET_STEP_ROWS, qk_w), lambda c: (c, 0)), _spec((RET_STEP_ROWS, qk_w), lambda c: (c, 1)),
        _spec((RET_STEP_ROWS, v_w), lambda c: (c, 1)), _spec((RET_STEP_ROWS, v_w), lambda c: (c, 2)),
    ] + _ret_const_specs(cm) + [_spec((1, v_w), lambda c: (0, 0))]
    qk_out = _spec((RET_STEP_ROWS, qk_w), lambda c: (c, 0))
    v_out = _spec((RET_STEP_ROWS, v_w), lambda c: (c, 0))
    return _pcall(
        body, name="ret_fwd", grid=(nC // RET_STEP_CHUNKS,),
        in_specs=in_specs,
        out_specs=[qk_out, qk_out, v_out, v_out,
                   _spec((RET_HEADS, RET_STEP_CHUNKS, RET_DK, RET_DV), lambda c: (0, c, 0, 0))],
        out_shape=[_sds((T, qk_w), F32), _sds((T, qk_w), F32), _sds((T, v_w), F32), _sds((T, v_w), BF16),
                   _sds((RET_HEADS, nC, RET_DK, RET_DV), F32)],
        scratch_shapes=[pltpu.VMEM((RET_HEADS, RET_DK, RET_DV), F32)],
    )(u, u, u, u, *consts, ret_gn)


def _ret_bwd(dyr, ret, u, qr, kr, states, consts, ret_gn):
    T = u.shape[0]
    nC = T // CHUNK
    kscale = RET_DK ** -0.5

    def body(dyr_ref, ret_ref, g_ref, q_ref, k_ref, v_ref, st_ref,
             cos_ref, sin_ref, dm_ref, kd_ref, qd_ref, cd_ref, gn_ref,
             dq_ref, dk_ref, dv_ref, dg_ref, dgn_ref, gstate):
        first = pl.program_id(0) == 0

        @pl.when(first)
        def _():
            gstate[...] = jnp.zeros_like(gstate)

        dgn_total = None
        for cc in reversed(range(RET_STEP_CHUNKS)):
            rows = slice(cc * CHUNK, (cc + 1) * CHUNK)
            cosv, sinv = cos_ref[rows, :], sin_ref[rows, :]
            dgn_parts = []
            for h in range(RET_HEADS):
                hk, hv = _head(h, RET_DK), _head(h, RET_DV)
                ret = ret_ref[rows, hv]
                mu = jnp.mean(ret, axis=-1, keepdims=True)
                xc = ret - mu
                rs = lax.rsqrt(jnp.mean(xc * xc, axis=-1, keepdims=True) + EPS)
                yn = xc * rs
                gn = gn_ref[:, hv]
                g = g_ref[rows, hv]
                sg = _sigmoid(g)
                dyr_v = dyr_ref[rows, hv]
                dretn = dyr_v * (g * sg)
                dg_ref[rows, hv] = (dyr_v * (yn * gn) * (sg * (1.0 + g * (1.0 - sg)))).astype(BF16)
                dgn_parts.append(jnp.sum(dretn * yn, axis=0, keepdims=True))
                dyn = dretn * gn
                d_o = rs * (dyn - jnp.mean(dyn, axis=-1, keepdims=True)
                            - yn * jnp.mean(dyn * yn, axis=-1, keepdims=True))

                q, k, v = q_ref[rows, hk], k_ref[rows, hk], v_ref[rows, hv]
                dmat, kd, qd = dm_ref[h], kd_ref[h], qd_ref[h]
                prev = st_ref[h, cc]
                gnext = gstate[h]
                s = _dot(q, k, "nt") * dmat
                ds = _dot(d_o, v, "nt") * dmat
                doq = d_o * qd
                dq = _dot(ds, k, "nn") + _dot(doq, prev, "nt")
                dk = _dot(ds, q, "tn") + _dot(v, gnext, "nt") * kd
                dv = _dot(s, d_o, "tn") + _dot(k * kd, gnext, "nn")
                gstate[h] = cd_ref[h] * gnext + _dot(q, doq, "tn")
                dq_ref[rows, hk] = _unrot(dq, cosv, sinv).astype(BF16)
                dk_ref[rows, hk] = _unrot(dk * kscale, cosv, sinv).astype(BF16)
                dv_ref[rows, hv] = dv.astype(BF16)
            dgn = jnp.concatenate(dgn_parts, axis=1)
            dgn_total = dgn if dgn_total is None else dgn_total + dgn
        _accumulate(dgn_ref, dgn_total, first)

    n_steps = nC // RET_STEP_CHUNKS
    cm = lambda c: n_steps - 1 - c
    qk_w, v_w = RET_HEADS * RET_DK, RET_HEADS * RET_DV
    vspec = lambda blk: _spec((RET_STEP_ROWS, v_w), lambda c: (cm(c), blk))
    qspec = _spec((RET_STEP_ROWS, qk_w), lambda c: (cm(c), 0))
    in_specs = [vspec(0), vspec(0), vspec(2), qspec, qspec, vspec(1),
                _spec((RET_HEADS, RET_STEP_CHUNKS, RET_DK, RET_DV), lambda c: (0, cm(c), 0, 0)),
                ] + _ret_const_specs(cm) + [_spec((1, v_w), lambda c: (0, 0))]
    return _pcall(
        body, name="ret_bwd", grid=(n_steps,),
        in_specs=in_specs,
        out_specs=[qspec, qspec, vspec(0), vspec(0), _spec((1, v_w), lambda c: (0, 0))],
        out_shape=[_sds((T, qk_w), BF16), _sds((T, qk_w), BF16), _sds((T, v_w), BF16), _sds((T, v_w), BF16),
                   _sds((1, v_w), F32)],
        scratch_shapes=[pltpu.VMEM((RET_HEADS, RET_DK, RET_DV), F32)],
    )(dyr, ret, u, qr, kr, u, states, *consts, ret_gn)


def _shift_down(x, s):
    rows = lax.broadcasted_iota(jnp.int32, x.shape, 0)
    return jnp.where(rows >= s, pltpu.roll(x, s, 0), 0.0)


def _shift_up(x, s):
    n = x.shape[0]
    rows = lax.broadcasted_iota(jnp.int32, x.shape, 0)
    return jnp.where(rows < n - s, pltpu.roll(x, n - s, 0), 0.0)


def _lru_specs(T):
    col = lambda off: _spec((T, LRU_BLOCK), lambda g: (0, off + g))
    vec = _spec((1, LRU_BLOCK), lambda g: (0, g))
    wblk = _spec((None, LRU_BLOCK, LRU_BLOCK), lambda g: (g, 0, 0))
    cw = _spec((CONV_TAPS, LRU_BLOCK), lambda g: (0, g))
    return col, vec, wblk, cw


def _lru_gates_fwd(u, conv_w, conv_b, w_r, b_r, w_i, b_i, lam):
    T = u.shape[0]
    col, vec, wblk, cw = _lru_specs(T)

    def body(x_ref, cw_ref, cb_ref, wr_ref, br_ref, wi_ref, bi_ref, lam_ref,
             xc_ref, r_ref, i_ref, a_ref, bx_ref):
        x = x_ref[...]
        w = cw_ref[...]
        xc = (_shift_down(x, 3) * w[0:1] + _shift_down(x, 2) * w[1:2] + _shift_down(x, 1) * w[2:3]
              + x * w[3:4] + cb_ref[...])
        r = _sigmoid(_dot(xc, wr_ref[...], "nn") + br_ref[...])
        i = _sigmoid(_dot(xc, wi_ref[...], "nn") + bi_ref[...])
        la = (-LRU_C) * r * _softplus(-lam_ref[...])
        xc_ref[...] = xc
        r_ref[...] = r
        i_ref[...] = i
        a_ref[...] = jnp.exp(la)
        bx_ref[...] = jnp.sqrt(-_expm1(2.0 * la)) * (i * xc)

    out = col(0)
    return _pcall(
        body, name="lru_gates_fwd", grid=(LRU_BLOCKS,),
        in_specs=[col(24), cw, vec, wblk, vec, wblk, vec, vec],
        out_specs=[out] * 5,
        out_shape=[_sds((T, D), F32)] * 5,
    )(u, conv_w, conv_b, w_r, b_r, w_i, b_i, lam)


def _lru_scan(name, a3, b3, reverse):
    T = a3.shape[0]
    nt = T // SCAN_TILE
    unroll = 8

    def body(a_ref, b_ref, o_ref, carry):
        @pl.when(pl.program_id(0) == 0)
        def _():
            carry[...] = jnp.zeros_like(carry)

        if not reverse:
            def step(t, h):
                h = a_ref[t] * h + b_ref[t]
                o_ref[t] = h
                return h
        else:
            def step(k, c):
                t = SCAN_TILE - 1 - k
                l = b_ref[t] + c
                o_ref[t] = l
                return a_ref[t] * l
        carry[...] = lax.fori_loop(0, SCAN_TILE, step, carry[...], unroll=unroll)

    idx = (lambda i: (nt - 1 - i, 0, 0)) if reverse else (lambda i: (i, 0, 0))
    blk = _spec((SCAN_TILE, LRU_BLOCKS, LRU_BLOCK), idx)
    return _pcall(
        body, name=name, grid=(nt,),
        in_specs=[blk, blk], out_specs=blk,
        out_shape=_sds((T, LRU_BLOCKS, LRU_BLOCK), F32),
        scratch_shapes=[pltpu.VMEM((LRU_BLOCKS, LRU_BLOCK), F32)],
    )(a3, b3)


def _lru_gates_bwd(lmb, hl, a, r, i, xc, u, conv_w, w_r, w_i, lam):
    T = u.shape[0]
    col, vec, wblk, cw = _lru_specs(T)

    def body(l_ref, h_ref, a_ref, r_ref, i_ref, xc_ref, x_ref, cw_ref, wr_ref, wi_ref, lam_ref,
             dx_ref, dwr_ref, dwi_ref, dvec_ref, dcw_ref):
        l = l_ref[...]
        av, rv, iv, xc = a_ref[...], r_ref[...], i_ref[...], xc_ref[...]
        lam_v = lam_ref[...]
        sp = _softplus(-lam_v)
        la = (-LRU_C) * rv * sp
        mult = jnp.sqrt(-_expm1(2.0 * la))
        da = l * _shift_down(h_ref[...], 1)
        dmult = l * (iv * xc)
        di = l * mult * xc
        dxc = l * mult * iv
        dla = da * av - dmult * (av * av) / mult
        dzr = (dla * ((-LRU_C) * sp)) * rv * (1.0 - rv)
        dzi = di * iv * (1.0 - iv)
        dsp = jnp.sum(dla * ((-LRU_C) * rv), axis=0, keepdims=True)
        dlam = dsp * (-_sigmoid(-lam_v))
        dwr_ref[...] = _dot(xc, dzr, "tn")
        dwi_ref[...] = _dot(xc, dzi, "tn")
        dxc = dxc + _dot(dzr, wr_ref[...], "nt") + _dot(dzi, wi_ref[...], "nt")
        x = x_ref[...]
        w = cw_ref[...]
        dx = (dxc * w[3:4] + _shift_up(dxc, 1) * w[2:3] + _shift_up(dxc, 2) * w[1:2]
              + _shift_up(dxc, 3) * w[0:1])
        dx_ref[...] = dx.astype(BF16)
        dvec_ref[...] = jnp.concatenate(
            [jnp.sum(dzr, axis=0, keepdims=True), jnp.sum(dzi, axis=0, keepdims=True), dlam,
             jnp.sum(dxc, axis=0, keepdims=True)], axis=0)
        dcw_ref[...] = jnp.concatenate(
            [jnp.sum(dxc * _shift_down(x, 3 - tap), axis=0, keepdims=True) if tap < 3
             else jnp.sum(dxc * x, axis=0, keepdims=True) for tap in range(CONV_TAPS)], axis=0)

    c0 = col(0)
    return _pcall(
        body, name="lru_gates_bwd", grid=(LRU_BLOCKS,),
        in_specs=[c0, c0, c0, c0, c0, c0, col(24), cw, wblk, wblk, vec],
        out_specs=[c0, wblk, wblk, cw, cw],
        out_shape=[_sds((T, D), BF16), _sds((LRU_BLOCKS, LRU_BLOCK, LRU_BLOCK), F32),
                   _sds((LRU_BLOCKS, LRU_BLOCK, LRU_BLOCK), F32), _sds((4, D), F32), _sds((CONV_TAPS, D), F32)],
    )(lmb, hl, a, r, i, xc, u, conv_w, w_r, w_i, lam)


def _xattn_probs(q, k):
    sc = _dot(q, k, "nt") * (X_HD ** -0.5)
    e = jnp.exp(sc - jnp.max(sc, axis=-1, keepdims=True))
    return e / jnp.sum(e, axis=-1, keepdims=True)


def _xattn_fwd(xq, xk, xv):
    T = xq.shape[0]
    tq = min(WIDE_ROW_TILE, T)
    M = xk.shape[0]

    def body(q_ref, k_ref, v_ref, o_ref):
        p = _xattn_probs(q_ref[...], k_ref[...])
        o_ref[...] = _dot(p, v_ref[...], "nn").astype(BF16)

    qs = _spec((tq, X_HD), lambda h, i: (i, h))
    kv = _spec((M, X_HD), lambda h, i: (0, h))
    return _pcall(
        body, name="xattn_fwd", grid=(X_HEADS, T // tq),
        in_specs=[qs, kv, kv], out_specs=qs, out_shape=_sds((T, D), BF16),
    )(xq, xk, xv)


def _xattn_bwd(xq, xk, xv, dxo):
    T = xq.shape[0]
    tq = min(WIDE_ROW_TILE, T)
    M = xk.shape[0]

    def body(q_ref, k_ref, v_ref, do_ref, dq_ref, dk_ref, dv_ref):
        first = pl.program_id(1) == 0
        q, k, v, do = q_ref[...], k_ref[...], v_ref[...], do_ref[...]
        p = _xattn_probs(q, k)
        dp = _dot(do, v, "nt")
        ds = p * (dp - jnp.sum(dp * p, axis=-1, keepdims=True)) * (X_HD ** -0.5)
        dq_ref[...] = _dot(ds, k, "nn").astype(BF16)
        _accumulate(dk_ref, _dot(ds, q, "tn"), first)
        _accumulate(dv_ref, _dot(p, do, "tn"), first)

    qs = _spec((tq, X_HD), lambda h, i: (i, h))
    kv = _spec((M, X_HD), lambda h, i: (0, h))
    return _pcall(
        body, name="xattn_bwd", grid=(X_HEADS, T // tq),
        in_specs=[qs, kv, kv, qs], out_specs=[qs, kv, kv],
        out_shape=[_sds((T, D), BF16), _sds((M, D), F32), _sds((M, D), F32)],
    )(xq, xk, xv, dxo)


def _adamw(name, w, g, m, v):
    R, C = w.shape
    tr = R
    for cand in (512, 352, 256):
        if R % cand == 0:
            tr = cand
            break

    def fn(irefs, orefs, ids):
        delta, mn, vn = _adamw_update(*(r[...] for r in irefs))
        orefs[0][...] = delta
        orefs[1][...] = mn
        orefs[2][...] = vn

    blk = _spec((tr, C), lambda i: (i, 0))
    return _rowwise(name, fn, [(w, blk), (g, blk), (m, blk), (v, blk)],
                    [(_sds((R, C), F32), blk)] * 3, (R // tr,))


def _adamw_update(wv, gv, mv, vv):
    c1 = 1.0 - ADAM_B1 ** ADAM_STEP
    c2 = 1.0 - ADAM_B2 ** ADAM_STEP
    mn = ADAM_B1 * mv + (1.0 - ADAM_B1) * gv
    vn = ADAM_B2 * vv + (1.0 - ADAM_B2) * (gv * gv)
    delta = -ADAM_LR * ((mn / c1) / (jnp.sqrt(vn / c2) + ADAM_EPS) + ADAM_WD * wv)
    return delta, mn, vn


def _adamw_halves(name, w, mine, theirs, widx, m, v, core):
    R, C = w.shape
    H = R // 2
    tr = H
    while tr * C * 4 > (1 << 20) and tr % 16 == 0:
        tr //= 2
    nb = H // tr

    def body(core_ref, w_ref, mine_ref, theirs_ref, m_ref, v_ref, g_out, d_out, m_out, v_out):
        gv = jnp.where(pl.program_id(0) == core_ref[0], mine_ref[...], theirs_ref[...])
        delta, mn, vn = _adamw_update(w_ref[...], gv, m_ref[...], v_ref[...])
        g_out[...] = gv
        d_out[...] = delta
        m_out[...] = mn
        v_out[...] = vn

    full = pl.BlockSpec((tr, C), lambda h, i, core_ref: (h * nb + i, 0))
    mine_spec = pl.BlockSpec((None, tr, C), lambda h, i, core_ref: (widx, jnp.where(h == core_ref[0], i, 0), 0))
    theirs_spec = pl.BlockSpec((None, tr, C), lambda h, i, core_ref: (widx, jnp.where(h == core_ref[0], 0, i), 0))
    return _pcall(
        body, name=name, grid=(2, nb), num_prefetch=1,
        in_specs=[full, mine_spec, theirs_spec, full, full], out_specs=[full] * 4,
        out_shape=[_sds((R, C), F32)] * 4,
    )(core, w, mine, theirs, m, v)


def _rmsnorm(name, x, g):
    M = x.shape[0]
    tm = min(ROW_TILE, M)

    def fn(irefs, orefs, ids):
        orefs[0][...] = _rms_fwd(irefs[0][...], irefs[1][...]).astype(BF16)

    row = _spec((tm, D), lambda i: (i, 0))
    return _rowwise(name, fn, [(x, row), (g, _spec((1, D), lambda i: (0, 0)))],
                    [(_sds((M, D), BF16), row)], (M // tm,))[0]


WEIGHT_AT = {
    "ffn1_w1": ("col1", 0), "ffn1_w3": ("col1", 1), "ffn1_w2": ("row2a", 0),
    "w_ret_o": ("sqA", 0), "w_lru_o": ("sqA", 1), "w_out": ("sqA", 2),
    "w_xq": ("sqB", 0), "w_xk": ("sqB", 1), "w_xv": ("sqC", 0), "w_xo": ("sqC", 1),
    "ffn2_w1": ("col2a", 0), "ffn2_w3": ("col2b", 0), "ffn2_w2": ("row2b", 0),
}


def _local_step(x, mem, tgt, gw, sm, big):
    T = x.shape[0]
    tm = ROW_TILE

    def wt(name):
        key, idx = WEIGHT_AT[name]
        return gw[key], idx

    row3 = lambda i, j, r: (i, 0)
    vec3 = lambda i, j, r: (0, 0)
    rowD = _spec((tm, D), row3)
    vecD = _spec((1, D), vec3)

    def residual_norm(acc, erefs, orefs, ids):
        xo = erefs[0][...] + acc
        orefs[0][...] = xo
        orefs[1][...] = _rms_fwd(xo, erefs[1][...]).astype(BF16)

    def res_norm_io(x_res, g):
        return ([(x_res, rowD), (g, vecD)],
                [(_sds((T, D), F32), rowD), (_sds((T, D), BF16), rowD)])

    a1, b1, s1, h1 = _ffn_up("ffn1_up", x, *wt("ffn1_w1"), *wt("ffn1_w3"), norm_gain=sm["ffn1_norm"])
    x1, h2 = _ffn_down("ffn1_down", s1, *wt("ffn1_w2"), x, sm["mix_norm"])

    tw = min(WIDE_ROW_TILE, T)
    wideD = _spec((tw, D), row3)
    u = _gemm(
        "mix_in",
        [(h2, wideD, gw["win"], _spec((None, None, IN_BLK, D), lambda i, j, r: (j, 0, 0, 0)), "nt")],
        (T // tw, N_CHIPS, 1),
        [(_sds((T, 5120), F32), _spec((tw, IN_BLK), lambda i, j, r: (i, j)))], (tw, IN_BLK))[0]

    consts = _retention_constants(T)
    qr, kr, ret, yr, states = _ret_fwd(u, consts, sm["ret_gn"])

    conv_w = gw["conv"][:, 0].transpose(1, 0, 2).reshape(CONV_TAPS, D)
    xc, rg, ig, av, bx = _lru_gates_fwd(u, conv_w, sm["conv_b"], sm["w_rgate"], sm["b_rgate"],
                                        sm["w_igate"], sm["b_igate"], sm["lru_lambda"])
    a3 = av.reshape(T, LRU_BLOCKS, LRU_BLOCK)
    hl = _lru_scan("lru_scan_fwd", a3, bx.reshape(T, LRU_BLOCKS, LRU_BLOCK), False).reshape(T, D)

    row1 = _spec((tm, D), lambda i: (i, 0))
    glru1 = _spec((tm, D), lambda i: (i, 4))

    def lru_out(irefs, orefs, ids):
        gl, _ = _gelu_and_grad(irefs[1][...])
        orefs[0][...] = (irefs[0][...] * gl).astype(BF16)

    yl = _rowwise("lru_out", lru_out, [(hl, row1), (u, glru1)], [(_sds((T, D), BF16), row1)], (T // tm,))[0]

    def gate_epilogue(acc, erefs, orefs, ids):
        orefs[0][...] = _sigmoid(acc + erefs[0][...])

    gates = _gemm(
        "mix_gates",
        [(h2, wideD, gw["wbg"], _spec((None, None, BG_BLK, D), lambda i, j, r: (j, 0, 0, 0)), "nt")],
        (T // tw, N_CHIPS, 1),
        [(_sds((T, 2 * D), F32), _spec((tw, BG_BLK), lambda i, j, r: (i, j)))], (tw, BG_BLK),
        [(sm["b_branch_gate"], _spec((1, BG_BLK), lambda i, j, r: (0, j)))], gate_epilogue)[0]

    y_ret = _proj_sq("y_ret", yr, *wt("w_ret_o"), "nn")[0]

    def merge_epilogue(acc, erefs, orefs, ids):
        orefs[0][...] = acc
        orefs[1][...] = (erefs[0][...] * erefs[2][...] + erefs[1][...] * acc).astype(BF16)

    y_lru, merged = _proj_sq(
        "y_lru", yl, *wt("w_lru_o"), "nn",
        extras=[(gates, _spec((tm, D), lambda i, j, r: (i, 0))), (gates, _spec((tm, D), lambda i, j, r: (i, 1))),
                (y_ret, rowD)],
        epilogue=merge_epilogue,
        outs=[(_sds((T, D), F32), rowD), (_sds((T, D), BF16), rowD)])

    ex, ou = res_norm_io(x1, sm["xattn_norm"])
    x2, hq = _proj_sq("mix_out", merged, *wt("w_out"), "nn", extras=ex, epilogue=residual_norm, outs=ou)

    m = _rmsnorm("mem_norm", mem, sm["mem_norm"])
    xq = _proj_sq("xq", hq, *wt("w_xq"), "nn", BF16)[0]
    xk = _proj_sq("xk", m, *wt("w_xk"), "nn", BF16)[0]
    xv = _proj_sq("xv", m, *wt("w_xv"), "nn", BF16)[0]
    xo = _xattn_fwd(xq, xk, xv)
    ex, ou = res_norm_io(x2, sm["ffn2_norm"])
    x3, h3 = _proj_sq("xattn_out", xo, *wt("w_xo"), "nn", extras=ex, epilogue=residual_norm, outs=ou)

    a2, b2, s2 = _ffn_up("ffn2_up", h3, *wt("ffn2_w1"), *wt("ffn2_w3"))
    loss, dx4, dg_final = _ffn_down("ffn2_down", s2, *wt("ffn2_w2"), x3, sm["final_norm"], loss_target=tgt)

    dx3, dg_ffn2 = _ffn_bwd("ffn2", dx4, h3, a2, b2, s2, *wt("ffn2_w1"), *wt("ffn2_w3"),
                            *wt("ffn2_w2"), x3, sm["ffn2_norm"], big)

    dxo = _proj_sq("d_xo", dx3, *wt("w_xo"), "nt", BF16)[0]
    big["w_xo"] = _dw_sq("dw_xo", xo, dx3)[None]
    dxq, dxk, dxv = _xattn_bwd(xq, xk, xv, dxo)
    big["w_xq"] = _dw_sq("dw_xq", hq, dxq)[None]
    ex, ou = _rms_bwd_io(x2, sm["xattn_norm"], dx3, T, tm)
    dx2, dg_xattn = _proj_sq("d_hq", dxq, *wt("w_xq"), "nt", extras=ex, epilogue=_rms_bwd_epilogue, outs=ou)
    big["w_xk"] = _dw_sq("dw_xk", m, dxk)[None]
    big["w_xv"] = _dw_sq("dw_xv", m, dxv)[None]

    M = mem.shape[0]

    def mem_norm_epilogue(acc, erefs, orefs, ids):
        _, dgp = _rms_bwd(erefs[0][...], erefs[1][...], acc)
        orefs[0][...] = dgp

    wsq_spec = lambda idx: _spec((N_CHIPS, None, SQ_BLK, D), lambda i, j, r: (0, idx, 0, 0))
    memD = _spec((M, D), row3)
    dg_mem = _gemm(
        "d_mem_norm",
        [(dxk, memD, wt("w_xk")[0], wsq_spec(wt("w_xk")[1]), "nt"),
         (dxv, memD, wt("w_xv")[0], wsq_spec(wt("w_xv")[1]), "nt")],
        (1, 1, 1), [(_sds((1, D), F32), vecD)], (M, D),
        [(mem, memD), (sm["mem_norm"], vecD)], mem_norm_epilogue)[0]

    def merged_bwd_epilogue(acc, erefs, orefs, ids):
        gr, gl, yrv, ylv = (e[...] for e in erefs)
        orefs[0][...] = (acc * gr).astype(BF16)
        orefs[1][...] = (acc * gl).astype(BF16)
        dgr = acc * yrv * gr * (1.0 - gr)
        dgl = acc * ylv * gl * (1.0 - gl)
        orefs[2][:, :D] = dgr.astype(BF16)
        orefs[2][:, D:] = dgl.astype(BF16)
        dbb = jnp.concatenate([jnp.sum(dgr, axis=0, keepdims=True), jnp.sum(dgl, axis=0, keepdims=True)], axis=1)
        _accumulate(orefs[3], dbb, ids[0] == 0)

    dy_ret, dy_lru, dgpre, db_bg = _proj_sq(
        "d_merged", dx2, *wt("w_out"), "nt",
        extras=[(gates, _spec((tm, D), lambda i, j, r: (i, 0))), (gates, _spec((tm, D), lambda i, j, r: (i, 1))),
                (y_ret, rowD), (y_lru, rowD)],
        epilogue=merged_bwd_epilogue,
        outs=[(_sds((T, D), BF16), rowD), (_sds((T, D), BF16), rowD),
              (_sds((T, 2 * D), BF16), _spec((tm, 2 * D), row3)),
              (_sds((1, 2 * D), F32), _spec((1, 2 * D), vec3))])
    big["w_branch_gate"] = _gemm(
        "dw_bg",
        [(h2, _spec((T, D), lambda j, n, r: (r, 0)), dgpre, _spec((T, BG_BLK), lambda j, n, r: (r, j)), "tn")],
        (N_CHIPS, 1, 1),
        [(_sds((N_CHIPS, D, BG_BLK), GRAD_WIRE_DTYPE), _spec((None, D, BG_BLK), lambda j, n, r: (j, 0, 0)))],
        (D, BG_BLK))[0][None]
    big["w_out"] = _dw_sq("dw_out", merged, dx2)[None]
    dyr = _proj_sq("d_yr", dy_ret, *wt("w_ret_o"), "nt")[0]
    big["w_ret_o"] = _dw_sq("dw_ret_o", yr, dy_ret)[None]
    dyl = _proj_sq("d_yl", dy_lru, *wt("w_lru_o"), "nt")[0]
    big["w_lru_o"] = _dw_sq("dw_lru_o", yl, dy_lru)[None]

    dq, dk, dv, dgr, dg_retgn = _ret_bwd(dyr, ret, u, qr, kr, states, consts, sm["ret_gn"])

    def lru_out_bwd(irefs, orefs, ids):
        gl, dgl = _gelu_and_grad(irefs[2][...])
        dyl_v = irefs[0][...]
        orefs[0][...] = dyl_v * gl
        orefs[1][...] = (dyl_v * irefs[1][...] * dgl).astype(BF16)

    dhl, dglru = _rowwise("lru_out_bwd", lru_out_bwd, [(dyl, row1), (hl, row1), (u, glru1)],
                          [(_sds((T, D), F32), row1), (_sds((T, D), BF16), row1)], (T // tm,))
    lmb = _lru_scan("lru_scan_bwd", a3, dhl.reshape(T, LRU_BLOCKS, LRU_BLOCK), True).reshape(T, D)
    dxl, dw_r, dw_i, dvec, dcw = _lru_gates_bwd(lmb, hl, av, rg, ig, xc, u, conv_w,
                                                sm["w_rgate"], sm["w_igate"], sm["lru_lambda"])

    du = jnp.concatenate([dq, dk, dv, dgr, dxl, dglru], axis=1)
    tk = T
    big["w_in"] = _gemm(
        "dw_in",
        [(h2, _spec((tk, D), lambda j, n, r: (r, 0)), du, _spec((tk, IN_BLK), lambda j, n, r: (r, j)), "tn")],
        (N_CHIPS, 1, T // tk),
        [(_sds((N_CHIPS, D, IN_BLK), GRAD_WIRE_DTYPE), _spec((None, D, IN_BLK), lambda j, n, r: (j, 0, 0)))],
        (D, IN_BLK))[0][None]
    tf = min(FFN_ROW_TILE, T)
    ex, ou = _rms_bwd_io(x1, sm["mix_norm"], dx2, T, tf)
    dx1, dg_mix = _gemm(
        "d_h2",
        [(du, _spec((tf, 5120), row3), gw["win"], _spec((N_CHIPS, None, IN_BLK, D), lambda i, j, r: (0, 0, 0, 0)), "nn"),
         (dgpre, _spec((tf, 2 * D), row3), gw["wbg"], _spec((N_CHIPS, None, BG_BLK, D), lambda i, j, r: (0, 0, 0, 0)),
          "nn")],
        (T // tf, 1, 1), ou, (tf, D), ex, _rms_bwd_epilogue)

    grad_x, dg_ffn1 = _ffn_bwd("ffn1", dx1, h1, a1, b1, s1, *wt("ffn1_w1"), *wt("ffn1_w3"),
                               *wt("ffn1_w2"), x, sm["ffn1_norm"], big)

    small = {
        "ffn1_norm": dg_ffn1, "mix_norm": dg_mix, "ret_gn": dg_retgn, "conv_b": dvec[3:4],
        "b_rgate": dvec[0:1], "b_igate": dvec[1:2], "lru_lambda": dvec[2:3], "xattn_norm": dg_xattn,
        "mem_norm": dg_mem, "ffn2_norm": dg_ffn2, "final_norm": dg_final, "b_branch_gate": db_bg,
        "conv_w": dcw, "w_rgate": dw_r, "w_igate": dw_i,
    }
    return loss, grad_x, small


ANY_SPEC = pl.BlockSpec(memory_space=pl.ANY)
VMEM_SPEC = pl.BlockSpec(memory_space=pltpu.VMEM)
N_PEER_CHIPS = N_CHIPS - 1


def _mesh_position():
    x, y, c = lax.axis_index("x"), lax.axis_index("y"), lax.axis_index("c")
    chips = [(1 - x, y), (x, 1 - y), (1 - x, 1 - y)]
    return x, y, c, chips


def _chip_index(x, y):
    return 2 * x + y


def _rows_half(ref, axis, h):
    n = ref.shape[axis] // 2
    idx = [slice(None)] * len(ref.shape)
    idx[axis] = pl.ds(pl.multiple_of(h * n, 16), n)
    return ref.at[tuple(idx)]


def _remote(src, dst, send_sem, recv_sem, device):
    return pltpu.make_async_remote_copy(src_ref=src, dst_ref=dst, send_sem=send_sem, recv_sem=recv_sem,
                                        device_id=device, device_id_type=MESH)


def _gather_chips_task(shards, split, landed, part=0, nparts=1, legs="both"):
    keys = list(shards)
    n = len(keys)

    def operands():
        if part or legs == "pass_on":
            return [landed[k] for k in keys]
        chip_me = _chip_index(lax.axis_index("x"), lax.axis_index("y"))
        return [lax.dynamic_update_slice(lax.empty((N_CHIPS,) + shards[k].shape, shards[k].dtype), shards[k][None],
                                         (chip_me,) + (0,) * shards[k].ndim) for k in keys]

    def my_rows(ref, c):
        rows = ref.shape[1] // (2 * nparts)
        return ref.at[:, pl.ds(pl.multiple_of((c * nparts + part) * rows, 16), rows), :]

    def make_direct(ins, outs, send_sem, recv_sem):
        x, y, c, chips = _mesh_position()
        s_me = _chip_index(x, y)
        starts, arrivals = [], []
        for g in range(n):
            for k, chip in enumerate(chips):
                sems = (send_sem(3 * g + k), recv_sem(3 * g + k))
                starts.append(functools.partial(_remote, outs[g].at[s_me], outs[g].at[s_me], *sems, (*chip, c)))
                got = outs[g].at[_chip_index(*chip)]
                arrivals.append(functools.partial(_remote, got, got, *sems, (*chip, c)))
        return starts, arrivals

    def axis_neighbours(x, y, c):
        flip = lambda v, f: v + f * (1 - 2 * v)
        return (flip(x, 1 - c), flip(y, c)), (flip(x, c), flip(y, 1 - c))

    def make_swap(ins, outs, send_sem, recv_sem):
        x, y, c, _ = _mesh_position()
        first, _ = axis_neighbours(x, y, c)
        starts, arrivals = [], []
        for g in range(n):
            sems = (send_sem(3 * g), recv_sem(3 * g))
            mine = my_rows(outs[g].at[_chip_index(x, y)], c)
            starts.append(functools.partial(_remote, mine, mine, *sems, (*first, c)))
            got = my_rows(outs[g].at[_chip_index(*first)], c)
            arrivals.append(functools.partial(_remote, got, got, *sems, (*first, c)))
        return starts, arrivals

    def make_pass_on(ins, outs, send_sem, recv_sem):
        x, y, c, _ = _mesh_position()
        first, second = axis_neighbours(x, y, c)
        diagonal = (1 - x, 1 - y)
        starts, arrivals = [], []
        for g in range(n):
            half = lambda chip: my_rows(outs[g].at[_chip_index(*chip)], c)
            for k, (sent, arriving) in enumerate([((x, y), second), (first, diagonal)]):
                sems = (send_sem(3 * g + 1 + k), recv_sem(3 * g + 1 + k))
                starts.append(functools.partial(_remote, half(sent), half(sent), *sems, (*second, c)))
                arrivals.append(functools.partial(_remote, half(arriving), half(arriving), *sems, (*second, c)))
        return starts, arrivals

    def finish(res):
        landed.update(zip(keys, res))

    shapes = lambda: [_sds((N_CHIPS,) + shards[k].shape, shards[k].dtype) for k in keys]
    aliases = {g: g for g in range(n)}
    if not split:
        return _Task("chips", operands, shapes, aliases, 3 * n, make_direct, finish)
    if legs == "swap":
        return _Task("neighbours", operands, shapes, aliases, 3 * n, make_swap, finish)
    if legs == "pass_on":
        return _Task("neighbours", operands, shapes, aliases, 3 * n, make_pass_on, finish)
    return _Task("neighbours", operands, shapes, aliases, 3 * n, make_swap, finish, make_second=make_pass_on)


def _gather_sibling_task(keys, landed, ready):
    n = len(keys)

    def make(ins, outs, send_sem, recv_sem):
        x, y, c, chips = _mesh_position()
        starts, arrivals = [], []
        for g in range(n):
            for k, chip in enumerate(chips):
                o = outs[g].at[_chip_index(*chip)]
                got, other = _rows_half(o, 1, c), _rows_half(o, 1, 1 - c)
                starts.append(functools.partial(_remote, got, got, send_sem(3 * g + k), recv_sem(3 * g + k),
                                                (x, y, 1 - c)))
                arrivals.append(functools.partial(_remote, other, other, send_sem(3 * g + k), recv_sem(3 * g + k),
                                                  (x, y, 1 - c)))
        return starts, arrivals

    def finish(res):
        ready.update(zip(keys, res))

    return _Task("sibling", lambda: [landed[k] for k in keys],
                 lambda: [_sds(landed[k].shape, landed[k].dtype) for k in keys],
                 {g: g for g in range(n)}, 3 * n, make, finish)


def _pair_swap_task(names, big, got):
    n = len(names)

    def make(ins, outs, send_sem, recv_sem):
        x, y, c, _ = _mesh_position()
        copies = [functools.partial(_remote, _rows_half(ins[a], 2, 1 - c), outs[a], send_sem(a), recv_sem(a),
                                    (x, y, 1 - c)) for a in range(n)]
        return copies, copies

    def shapes():
        return [_sds(big[k].shape[:2] + (big[k].shape[2] // 2, big[k].shape[3]), big[k].dtype) for k in names]

    return _Task("sibling", lambda: [big[k] for k in names], shapes, {}, n, make,
                 lambda res: got.update(zip(names, res)))


def _rs_pair_sum(name, fulls, gots, core):
    n = len(fulls)
    shapes = [(f.shape[2] // 2, f.shape[3]) for f in fulls]

    def body(core_ref, *refs):
        for a_ref, b_ref, o_ref in zip(refs[:n], refs[n:2 * n], refs[2 * n:]):
            o_ref[...] = (a_ref[...].astype(F32) + b_ref[...].astype(F32)).astype(BF16)

    mine = [pl.BlockSpec((None, None) + hc, lambda s, core_ref: (0, s, core_ref[0], 0)) for hc in shapes]
    slot = [pl.BlockSpec((None, None) + hc, lambda s, core_ref: (0, s, 0, 0)) for hc in shapes]
    return _pcall(
        body, name=name, grid=(N_CHIPS,), num_prefetch=1,
        in_specs=mine + slot, out_specs=slot,
        out_shape=[_sds((1, N_CHIPS) + hc, BF16) for hc in shapes],
    )(core, *fulls, *gots)


def _chip_exchange_task(names, pair_sums, by_source, part=0, nparts=1):
    n = len(names)

    def rows(ref):
        h = ref.shape[1] // nparts
        return ref.at[:, pl.ds(part * h, h), :]

    def make(ins, outs, send_sem, recv_sem):
        x, y, c, chips = _mesh_position()
        s_me = _chip_index(x, y)
        starts, arrivals = [], []
        for a in range(n):
            for k, chip in enumerate(chips):
                s_k = _chip_index(*chip)
                starts.append(functools.partial(_remote, rows(ins[a].at[:, s_k]), rows(outs[a].at[:, s_me]),
                                                send_sem(3 * a + k), recv_sem(3 * a + k), (*chip, c)))
                got = rows(outs[a].at[:, s_k])
                arrivals.append(functools.partial(_remote, got, got, send_sem(3 * a + k), recv_sem(3 * a + k),
                                                  (*chip, c)))
        return starts, arrivals

    def operands():
        return [pair_sums[k] for k in names] + ([by_source[k] for k in names] if part else [])

    return _Task("chips", operands, lambda: [_sds(pair_sums[k].shape, pair_sums[k].dtype) for k in names],
                 {n + a: a for a in range(n)} if part else {}, 3 * n, make,
                 lambda res: by_source.update(zip(names, res)))


def _rs_chip_sum(name, owns, parts, chip):
    n = len(owns)
    ns = N_CHIPS
    shapes = [p.shape[2:] for p in parts]

    def body(chip_ref, *refs):
        me = chip_ref[0]
        for i in range(n):
            own_v = refs[i][...].astype(F32)
            slots = refs[n + ns * i:n + ns * (i + 1)]
            tot = None
            for s in range(ns):
                term = jnp.where(me == s, own_v, slots[s][...].astype(F32))
                tot = term if tot is None else tot + term
            refs[n + ns * n + i][...] = tot

    def slot_spec(hc, s):
        return pl.BlockSpec((None, None) + hc,
                            lambda g, chip_ref: (0, jnp.where(chip_ref[0] == s, (s + 1) % ns, s), 0, 0))

    own_specs = [pl.BlockSpec((None, None) + hc, lambda g, chip_ref: (0, chip_ref[0], 0, 0)) for hc in shapes]
    slot_specs = [slot_spec(hc, s) for hc in shapes for s in range(ns)]
    return _pcall(
        body, name=name, grid=(1,), num_prefetch=1,
        in_specs=own_specs + slot_specs,
        out_specs=[pl.BlockSpec((None,) + hc, lambda g, chip_ref: (0, 0, 0)) for hc in shapes],
        out_shape=[_sds((1,) + hc, F32) for hc in shapes],
    )(chip, *owns, *[p for p in parts for _ in range(ns)])


def _pair_gather_task(names, halves, sibling_halves):
    n = len(names)

    def make(ins, outs, send_sem, recv_sem):
        x, y, c, _ = _mesh_position()
        copies = [functools.partial(_remote, ins[a], outs[a], send_sem(a), recv_sem(a), (x, y, 1 - c))
                  for a in range(n)]
        return copies, copies

    return _Task("sibling", lambda: [halves[k] for k in names], lambda: [_sds(halves[k].shape, F32) for k in names],
                 {}, n, make, lambda res: sibling_halves.update(zip(names, res)))


def _small_allreduce(arrs):
    n = len(arrs)
    per = 1 + 2 * N_PEER_CHIPS

    def body(*refs):
        v_refs, o_refs = refs[:n], refs[n:2 * n]
        sib, pair, part = refs[2 * n:3 * n], refs[3 * n:4 * n], refs[4 * n:5 * n]
        send_sems, recv_sems = refs[5 * n:]
        x, y, c, chips = _mesh_position()
        s_me = _chip_index(x, y)

        def quarter(ref, s):
            q = ref.shape[0] // N_CHIPS
            return ref.at[pl.ds(pl.multiple_of(s * q, 8), q)]

        def exchange(first_sem, src, dst_of, arrival_of):
            sems = lambda a, k: (send_sems.at[a * per + first_sem + k], recv_sems.at[a * per + first_sem + k])
            sends = [_remote(src(a, _chip_index(*chip)), dst_of(a, s_me), *sems(a, k), (*chip, c))
                     for a in range(n) for k, chip in enumerate(chips)]
            for cp in sends:
                cp.start()
            for a in range(n):
                for k, chip in enumerate(chips):
                    got = arrival_of(a, _chip_index(*chip))
                    _remote(got, got, *sems(a, k), (*chip, c)).wait_recv()
            for cp in sends:
                cp.wait_send()

        swaps = [_remote(v_refs[a], sib[a], send_sems.at[a * per], recv_sems.at[a * per], (x, y, 1 - c))
                 for a in range(n)]
        for cp in swaps:
            cp.start()
        for cp in swaps:
            cp.wait()
        for a in range(n):
            pair[a][...] = v_refs[a][...] + sib[a][...]
        exchange(1, lambda a, s_k: quarter(pair[a], s_k), lambda a, s: part[a].at[s], lambda a, s_k: part[a].at[s_k])
        for a in range(n):
            part[a][s_me] = quarter(pair[a], s_me)[...]
            q = o_refs[a].shape[0] // N_CHIPS
            o_refs[a][pl.ds(pl.multiple_of(s_me * q, 8), q), :] = (
                ((part[a][0] + part[a][1]) + part[a][2]) + part[a][3])
        exchange(1 + N_PEER_CHIPS, lambda a, s_k: quarter(o_refs[a], s_me), lambda a, s: quarter(o_refs[a], s),
                 lambda a, s_k: quarter(o_refs[a], s_k))

    shapes = [a.shape for a in arrs]
    return _pcall(
        body, name="small_allreduce", grid=(1,), own_peers=("sibling", "chips"),
        in_specs=[VMEM_SPEC] * n, out_specs=[VMEM_SPEC] * n, out_shape=[_sds(s, F32) for s in shapes],
        scratch_shapes=([pltpu.VMEM(s, F32) for s in shapes] * 2
                        + [pltpu.VMEM((N_CHIPS, s[0] // N_CHIPS, s[1]), F32) for s in shapes]
                        + [pltpu.SemaphoreType.DMA((n * per,)), pltpu.SemaphoreType.DMA((n * per,))]),
    )(*arrs)


TRANSPOSED_WEIGHTS = ("ffn1_w1", "ffn1_w3", "ffn2_w1", "ffn2_w3")
SMALL_LAYOUT = [("ffn1_norm", 1), ("mix_norm", 1), ("ret_gn", 1), ("conv_b", 1), ("b_rgate", 1), ("b_igate", 1),
                ("lru_lambda", 1), ("xattn_norm", 1), ("mem_norm", 1), ("ffn2_norm", 1), ("final_norm", 1),
                ("b_branch_gate", 2), ("conv_w", CONV_TAPS)]
SMALL_ROWS = 32
GATE_WEIGHTS = ("w_rgate", "w_igate")
WEIGHT_ORDER = ["ffn1_norm", "ffn1_w1", "ffn1_w3", "ffn1_w2", "mix_norm", "w_in", "ret_gn", "w_ret_o", "conv_w",
                "conv_b", "w_rgate", "b_rgate", "w_igate", "b_igate", "lru_lambda", "w_lru_o", "w_branch_gate",
                "b_branch_gate", "w_out", "xattn_norm", "mem_norm", "w_xq", "w_xk", "w_xv", "w_xo", "ffn2_norm",
                "ffn2_w1", "ffn2_w3", "ffn2_w2", "final_norm"]


SMALL_USED_ROWS = sum(n for _, n in SMALL_LAYOUT)


def _pack_small(parts, extra_row=None):
    rows = [parts[name].reshape(n, D) for name, n in SMALL_LAYOUT]
    if extra_row is not None:
        rows.append(extra_row)
    rows.append(jnp.zeros((SMALL_ROWS - sum(r.shape[0] for r in rows), D), F32))
    return jnp.concatenate(rows, axis=0)


def _unpack_small(packed, shapes):
    out, r = {}, 0
    for name, n in SMALL_LAYOUT:
        out[name] = packed[r:r + n].reshape(shapes[name])
        r += n
    return out


def kernel(x, mem, ffn1_norm, ffn1_w1, ffn1_w3, ffn1_w2, mix_norm, w_in, ret_gn, w_ret_o, conv_w, conv_b, w_rgate, b_rgate, w_igate, b_igate, lru_lambda, w_lru_o, w_branch_gate, b_branch_gate, w_out, xattn_norm, mem_norm, w_xq, w_xk, w_xv, w_xo, ffn2_norm, ffn2_w1, ffn2_w3, ffn2_w2, final_norm, loss_target, m_ffn1_norm, m_ffn1_w1, m_ffn1_w3, m_ffn1_w2, m_mix_norm, m_w_in, m_ret_gn, m_w_ret_o, m_conv_w, m_conv_b, m_w_rgate, m_b_rgate, m_w_igate, m_b_igate, m_lru_lambda, m_w_lru_o, m_w_branch_gate, m_b_branch_gate, m_w_out, m_xattn_norm, m_mem_norm, m_w_xq, m_w_xk, m_w_xv, m_w_xo, m_ffn2_norm, m_ffn2_w1, m_ffn2_w3, m_ffn2_w2, m_final_norm, v_ffn1_norm, v_ffn1_w1, v_ffn1_w3, v_ffn1_w2, v_mix_norm, v_w_in, v_ret_gn, v_w_ret_o, v_conv_w, v_conv_b, v_w_rgate, v_b_rgate, v_w_igate, v_b_igate, v_lru_lambda, v_w_lru_o, v_w_branch_gate, v_b_branch_gate, v_w_out, v_xattn_norm, v_mem_norm, v_w_xq, v_w_xk, v_w_xv, v_w_xo, v_ffn2_norm, v_ffn2_w1, v_ffn2_w3, v_ffn2_w2, v_final_norm):
    given = dict(locals())
    w = {n: given[n] for n in WEIGHT_ORDER}
    mom = {n: given["m_" + n] for n in WEIGHT_ORDER}
    var = {n: given["v_" + n] for n in WEIGHT_ORDER}
    chip = _chip_index(lax.axis_index("x"), lax.axis_index("y"))
    core = lax.axis_index("c").astype(jnp.int32).reshape(1)

    chip_id = chip.astype(jnp.int32).reshape(1)
    sm = {n: w[n] for n in ["ffn1_norm", "mix_norm", "ret_gn", "conv_b", "b_rgate", "b_igate", "lru_lambda",
                            "xattn_norm", "mem_norm", "ffn2_norm", "b_branch_gate"]}
    sm["final_norm"] = w["final_norm"].reshape(1, D)
    sm["w_rgate"] = w["w_rgate"][0]
    sm["w_igate"] = w["w_igate"][0]

    local = lambda a, n: jnp.swapaxes(a[0], 0, 1) if n in TRANSPOSED_WEIGHTS else a[0]
    stack = lambda names: jnp.stack([local(w[n], n) for n in names], axis=0).astype(BF16)
    shard = {"col1": stack(["ffn1_w1", "ffn1_w3"]), "row2a": stack(["ffn1_w2"]),
             "win": jnp.swapaxes(w["w_in"], 1, 2).astype(BF16),
             "wbg": jnp.swapaxes(w["w_branch_gate"], 1, 2).astype(BF16),
             "sqA": stack(["w_ret_o", "w_lru_o", "w_out"]), "sqB": stack(["w_xq", "w_xk"]),
             "sqC": stack(["w_xv", "w_xo"]), "col2a": stack(["ffn2_w1"]), "col2b": stack(["ffn2_w3"]),
             "row2b": stack(["ffn2_w2"]), "conv": w["conv_w"]}
    gw, landed = {}, {}
    over_chips = lambda keys: _gather_chips_task({k: shard[k] for k in keys}, True, landed)
    to_sibling = lambda keys: _gather_sibling_task(keys, landed, gw)

    big, got, pair_sums, by_source, halves, sibling_halves, outs = {}, {}, {}, {}, {}, {}, {}
    pair_swap = lambda names: _pair_swap_task(names, big, got)
    exchange = lambda names, part=0, nparts=1: _chip_exchange_task(names, pair_sums, by_source, part, nparts)
    pair_gather = lambda names: _pair_gather_task(names, halves, sibling_halves)

    def pair_sum(names):
        res = _rs_pair_sum("rs_pair_sum_" + names[0], [big[n] for n in names], [got[n] for n in names], core)
        pair_sums.update(zip(names, res))

    def chip_sum(names):
        res = _rs_chip_sum("rs_chip_sum_" + names[0], [pair_sums[n] for n in names], [by_source[n] for n in names],
                           chip_id)
        halves.update(zip(names, res))

    def adamw(names):
        for n in names:
            res = _adamw_halves("adamw_" + n, local(w[n], n), halves[n], sibling_halves[n], 0, local(mom[n], n),
                                local(var[n], n), core)
            outs[n] = tuple((jnp.swapaxes(r, 0, 1) if n in TRANSPOSED_WEIGHTS else r)[None] for r in res)

    do = lambda fn, names: functools.partial(fn, names)
    ffn2_grads = ["ffn2_w2", "ffn2_w1", "ffn2_w3"]
    xattn_grads = ["w_xo", "w_xq", "w_xk", "w_xv"]
    mix_out_grads = ["w_branch_gate", "w_out", "w_ret_o", "w_lru_o"]
    conv_gather = _gather_chips_task({"conv": shard["conv"]}, False, gw)
    swap = lambda key: _gather_chips_task({key: shard[key]}, True, landed, legs="swap")
    pass_on = lambda key: _gather_chips_task({key: shard[key]}, True, landed, legs="pass_on")
    plan = _Plan()
    plan.tasks = {
        "ag_first_chips": [over_chips(["col1", "row2a"])],
        "ag_first_sibling": [to_sibling(["col1", "row2a"]), swap("win")],
        "ffn1_up": [pass_on("win"), swap("wbg")],
        "ffn1_down": [to_sibling(["win"]), pass_on("wbg"), swap("sqA"), conv_gather],
        "mix_in": [to_sibling(["wbg"]), pass_on("sqA"), swap("col2a")],
        "ret_fwd": [to_sibling(["sqA"]), pass_on("col2a"), swap("sqB")],
        "lru_gates_fwd": [to_sibling(["col2a"]), pass_on("sqB"), swap("sqC")],
        "lru_scan_fwd": [to_sibling(["sqB"]), pass_on("sqC"), swap("col2b")],
        "mix_gates": [to_sibling(["sqC"]), pass_on("col2b"), swap("row2b")],
        "y_lru": [to_sibling(["col2b"]), pass_on("row2b")],
        "xattn_fwd": [to_sibling(["row2b"])],
        "ffn2_dh": [pair_swap(ffn2_grads)],
        "xattn_bwd": [exchange(["ffn2_w2"], 0, 2)],
        "d_hq": [exchange(["ffn2_w2"], 1, 2)],
        "d_merged": [exchange(["ffn2_w1"], 0, 2), pair_swap(xattn_grads)],
        "lru_out_bwd": [exchange(["w_xo"])],
        "ret_bwd": [exchange(["ffn2_w1"], 1, 2), exchange(["ffn2_w3"], 0, 2), pair_swap(mix_out_grads)],
        "lru_scan_bwd": [exchange(["ffn2_w3"], 1, 2)],
        "lru_gates_bwd": [exchange(["w_xq", "w_xk"]), pair_gather(ffn2_grads)],
        "dw_in": [exchange(["w_xv", "w_out"])],
        "d_h2": [exchange(["w_branch_gate", "w_ret_o", "w_lru_o"]), pair_swap(["w_in"]), pair_gather(xattn_grads)],
        "ffn1_bwd_mid": [exchange(["w_in"], 0, 2), pair_gather(mix_out_grads)],
        "ffn1_dw2": [exchange(["w_in"], 2, 4)],
        "ffn1_dw1": [exchange(["w_in"], 3, 4), pair_swap(["ffn1_w2"])],
        "ffn1_dw3": [exchange(["ffn1_w2"], 0, 2), pair_swap(["ffn1_w1"]), pair_gather(["w_in"])],
        "ffn1_dh": [exchange(["ffn1_w2"], 1, 2), exchange(["ffn1_w1"]), pair_swap(["ffn1_w3"])],
        "small_allreduce": [exchange(["ffn1_w3"]), pair_gather(["ffn1_w2"])],
        "rs_last_gather": [pair_gather(["ffn1_w1", "ffn1_w3"])],
    }
    plan.after = {
        "ffn2_dh": [do(pair_sum, ffn2_grads)],
        "d_merged": [do(pair_sum, xattn_grads)],
        "ret_bwd": [do(pair_sum, mix_out_grads)],
        "lru_scan_bwd": [do(chip_sum, ffn2_grads)],
        "lru_gates_bwd": [do(adamw, ffn2_grads)],
        "dw_in": [do(chip_sum, xattn_grads)],
        "d_h2": [do(chip_sum, mix_out_grads), do(pair_sum, ["w_in"]), do(adamw, xattn_grads)],
        "ffn1_bwd_mid": [do(adamw, mix_out_grads)],
        "ffn1_dw1": [do(chip_sum, ["w_in"]), do(pair_sum, ["ffn1_w2"])],
        "ffn1_dw3": [do(pair_sum, ["ffn1_w1"]), do(adamw, ["w_in"])],
        "ffn1_dh": [do(pair_sum, ["ffn1_w3"]), do(chip_sum, ["ffn1_w2"])],
        "small_allreduce": [do(chip_sum, ["ffn1_w1", "ffn1_w3"]), functools.partial(_comm_call, "rs_last_gather"),
                    do(adamw, ["ffn1_w2", "ffn1_w1", "ffn1_w3"])],
    }
    global _plan
    _plan = plan
    try:
        _comm_call("ag_first_chips")
        _comm_call("ag_first_sibling")
        loss_part, grad_x, small = _local_step(x[0], mem[0], loss_target[0], gw, sm, big)
        gate2d = lambda a: a.reshape(LRU_BLOCKS * LRU_BLOCK, LRU_BLOCK)
        loss_row = jnp.pad(loss_part, ((0, 0), (0, D - loss_part.shape[1])))
        small_sum, *gate_sums = _small_allreduce([_pack_small(small, loss_row)]
                                                 + [gate2d(small[n]) for n in GATE_WEIGHTS])
    finally:
        _plan = None
    assert not plan.tasks and not plan.after, (list(plan.tasks), list(plan.after))
    loss = small_sum[SMALL_USED_ROWS, 0]

    small_shapes = {n: w[n].shape for n, _ in SMALL_LAYOUT}
    small_shapes["conv_w"] = (CONV_TAPS, D)
    conv_row = SMALL_USED_ROWS - CONV_TAPS
    conv_grad = lax.dynamic_slice(small_sum[conv_row:conv_row + CONV_TAPS], (0, chip * SQ_BLK), (CONV_TAPS, SQ_BLK))
    small_w = {n: w[n] for n, _ in SMALL_LAYOUT}
    small_m = {n: mom[n] for n, _ in SMALL_LAYOUT}
    small_v = {n: var[n] for n, _ in SMALL_LAYOUT}
    pad_cols = lambda a: jnp.pad(a[0], ((0, 0), (0, D - SQ_BLK)))
    for dct in (small_w, small_m, small_v):
        dct["conv_w"] = pad_cols(dct["conv_w"])
    g_pack = lax.dynamic_update_slice(small_sum, jnp.pad(conv_grad, ((0, 0), (0, D - SQ_BLK))), (conv_row, 0))
    d_pack, m_pack, v_pack = _adamw("adamw_small", _pack_small(small_w), g_pack, _pack_small(small_m),
                                    _pack_small(small_v))
    unpacked = [_unpack_small(p, small_shapes) for p in (g_pack, d_pack, m_pack, v_pack)]
    for n, _ in SMALL_LAYOUT:
        if n == "conv_w":
            outs[n] = tuple(u[n][:, :SQ_BLK][None] for u in unpacked)
        else:
            outs[n] = tuple(u[n] for u in unpacked)
    for n, gsum in zip(GATE_WEIGHTS, gate_sums):
        d, nm, nv = _adamw("adamw_" + n, gate2d(w[n]), gsum, gate2d(mom[n]), gate2d(var[n]))
        outs[n] = tuple(r.reshape(w[n].shape) for r in (gsum, d, nm, nv))

    result = [loss, grad_x[None]]
    for k in range(4):
        result += [outs[n][k] for n in WEIGHT_ORDER]
    return tuple(result)
```

```python
import functools
import math

import jax
import jax.numpy as jnp
from jax import lax
from jax.experimental import pallas as pl
from jax.experimental.pallas import tpu as pltpu

F32 = jnp.float32
BF16 = jnp.bfloat16
GRAD_WIRE_DTYPE = BF16
MESH = pl.DeviceIdType.MESH

D = 1024
EPS = 1e-6
RET_HEADS = 4
RET_DK = 128
RET_DV = 256
CHUNK = 128
ROPE_BASE = 10000.0
LRU_BLOCKS = 8
LRU_BLOCK = 128
CONV_TAPS = 4
LRU_C = 8.0
D_FF = 2816
X_HEADS = 4
X_HD = 256
N_CHIPS = 4
FF_BLK = D_FF // N_CHIPS
IN_BLK = 5120 // N_CHIPS
BG_BLK = 2048 // N_CHIPS
SQ_BLK = D // N_CHIPS

ADAM_LR = 0.001
ADAM_B1 = 0.9
ADAM_B2 = 0.999
ADAM_EPS = 1e-08
ADAM_WD = 0.01
ADAM_STEP = 10

VMEM_LIMIT_BYTES = 56 * 1024 * 1024
ROW_TILE = 512
WIDE_ROW_TILE = 1024
FFN_ROW_TILE = 256
DW_BLK = D_FF // 2
SCAN_TILE = 256
RET_STEP_CHUNKS = 2
RET_STEP_ROWS = RET_STEP_CHUNKS * CHUNK

_DN = {
    "nn": (((1,), (0,)), ((), ())),
    "nt": (((1,), (1,)), ((), ())),
    "tn": (((0,), (0,)), ((), ())),
}


def _cparams(n_axes, collective_id=None):
    return pltpu.CompilerParams(dimension_semantics=("arbitrary",) * n_axes,
                                vmem_limit_bytes=VMEM_LIMIT_BYTES, collective_id=collective_id)


def _dot(a, b, kind):
    if b.ndim == 3:
        b = b.reshape(b.shape[0] * b.shape[1], b.shape[2])
    return lax.dot_general(a.astype(BF16), b.astype(BF16), _DN[kind], preferred_element_type=F32)


def _sigmoid(x):
    return 1.0 / (1.0 + jnp.exp(-x))


def _log1p_pos(e):
    u = 1.0 + e
    return jnp.where(u == 1.0, e, jnp.log(u) * (e / jnp.where(u == 1.0, 1.0, u - 1.0)))


def _expm1(x):
    u = jnp.exp(x)
    lu = jnp.log(u)
    safe = jnp.where(lu == 0.0, 1.0, lu)
    return jnp.where(u == 1.0, x, (u - 1.0) * (x / safe))


def _softplus(z):
    return jnp.maximum(z, 0.0) + _log1p_pos(jnp.exp(-jnp.abs(z)))


_GELU_C = math.sqrt(2.0 / math.pi)


def _gelu_and_grad(x):
    x2 = x * x
    t = jnp.tanh(_GELU_C * (x + 0.044715 * x * x2))
    g = 0.5 * x * (1.0 + t)
    dg = 0.5 * (1.0 + t) + 0.5 * x * (1.0 - t * t) * (_GELU_C * (1.0 + 3.0 * 0.044715 * x2))
    return g, dg


def _rms_fwd(x, g):
    r = lax.rsqrt(jnp.mean(x * x, axis=-1, keepdims=True) + EPS)
    return (x * r) * g


def _rms_bwd(x, g, dh):
    r = lax.rsqrt(jnp.mean(x * x, axis=-1, keepdims=True) + EPS)
    n = x * r
    dyg = dh * g
    dx = r * (dyg - n * jnp.mean(dyg * n, axis=-1, keepdims=True))
    return dx, jnp.sum(dh * n, axis=0, keepdims=True)


def _accumulate(ref, val, first):
    @pl.when(first)
    def _():
        ref[...] = val

    @pl.when(jnp.logical_not(first))
    def _():
        ref[...] += val


def _sds(shape, dtype):
    return jax.ShapeDtypeStruct(tuple(shape), dtype)


def _spec(shape, fn):
    return pl.BlockSpec(tuple(shape), fn)


class _Task:
    def __init__(self, peers, operands, out_shapes, aliases, nsem, make, finish, make_second=None):
        self.peers = peers
        self.operands, self.out_shapes, self.aliases = operands, out_shapes, aliases
        self.nsem, self.make, self.finish = nsem, make, finish
        self.make_second = make_second


class _Plan:
    def __init__(self):
        self.tasks, self.after = {}, {}


_plan = None


PEER_SET_COLLECTIVE_ID = {frozenset({"sibling"}): 1, frozenset({"chips"}): 2, frozenset({"sibling", "chips"}): 3,
                          frozenset({"neighbours"}): 4, frozenset({"sibling", "neighbours"}): 5}


def _peer_set(names):
    names = frozenset(names)
    return names - {"neighbours"} if "chips" in names else names


def _entry_handshake(peer_set):
    x, y, c, chips = _mesh_position()
    peers = [(x, y, 1 - c)] if "sibling" in peer_set else []
    if "chips" in peer_set:
        peers += [(*chip, c) for chip in chips]
    if "neighbours" in peer_set:
        peers += [(*chip, c) for chip in chips[:2]]
    barrier = pltpu.get_barrier_semaphore()
    for peer in peers:
        pl.semaphore_signal(barrier, inc=1, device_id=peer, device_id_type=MESH)
    pl.semaphore_wait(barrier, len(peers))


def _pcall(body, *, name, grid, in_specs, out_specs, out_shape, scratch_shapes=(), num_prefetch=0, own_peers=()):
    single = not isinstance(out_shape, (list, tuple))
    out_shape = [out_shape] if single else list(out_shape)
    out_specs = [out_specs] if single else list(out_specs)
    in_specs = list(in_specs)
    scratch_shapes = list(scratch_shapes)
    tasks = _plan.tasks.pop(name, []) if _plan is not None else []
    after = _plan.after.pop(name, []) if _plan is not None else []
    peer_set = _peer_set([t.peers for t in tasks] + list(own_peers))
    nax = len(grid)

    def run(*operands):
        n_in = len(operands) - num_prefetch
        n_out = len(out_shape)
        t_ops = [t.operands() for t in tasks]
        t_outs = [t.out_shapes() for t in tasks]
        c_ops = [a for ops in t_ops for a in ops]
        c_outs = [s for outs in t_outs for s in outs]
        aliases = {}
        i0, o0 = num_prefetch + n_in, n_out
        for t, ops, outs in zip(tasks, t_ops, t_outs):
            for i_loc, o_loc in t.aliases.items():
                aliases[i0 + i_loc] = o0 + o_loc
            i0 += len(ops)
            o0 += len(outs)
        nsem = sum(t.nsem for t in tasks)

        def wrapped(*refs):
            p = num_prefetch
            pre, ins = refs[:p], refs[p:p + n_in]
            cins = refs[p + n_in:p + n_in + len(c_ops)]
            q = p + n_in + len(c_ops)
            outs, couts = refs[q:q + n_out], refs[q + n_out:q + n_out + len(c_outs)]
            q += n_out + len(c_outs)
            scr = refs[q:q + len(scratch_shapes)]

            def rounds(second):
                send_sems, recv_sems = refs[q + len(scratch_shapes):]
                out = []
                ci = co = so = 0
                for t, ops, souts in zip(tasks, t_ops, t_outs):
                    make = t.make_second if second else t.make
                    out.append(([], []) if make is None else
                               make(cins[ci:ci + len(ops)], couts[co:co + len(souts)],
                                    functools.partial(lambda base, k: send_sems.at[base + k], so),
                                    functools.partial(lambda base, k: recv_sems.at[base + k], so)))
                    ci, co, so = ci + len(ops), co + len(souts), so + t.nsem
                return out

            two_rounds = [t.make_second is not None for t in tasks]
            if peer_set:
                ids = [pl.program_id(k) for k in range(nax)]
                first = functools.reduce(jnp.logical_and, [i == 0 for i in ids])
                last = functools.reduce(jnp.logical_and, [i == g - 1 for i, g in zip(ids, grid)])
                step = functools.reduce(lambda acc, ig: acc * ig[1] + ig[0], zip(ids, grid), 0)
                middle = step == math.prod(grid) // 3

                @pl.when(first)
                def _():
                    _entry_handshake(peer_set)
                    for starts, _ in rounds(False):
                        for copy in starts:
                            copy().start()

            body(*pre, *ins, *outs, *scr)

            if any(two_rounds):
                @pl.when(middle)
                def _():
                    for (_, arrivals), two in zip(rounds(False), two_rounds):
                        if two:
                            for arrival in arrivals:
                                arrival().wait_recv()
                    for starts, _ in rounds(True):
                        for copy in starts:
                            copy().start()

            if tasks:
                @pl.when(last)
                def _():
                    first_round, second_round = rounds(False), rounds(True)
                    for (_, arrivals1), (_, arrivals2), two in zip(first_round, second_round, two_rounds):
                        for arrival in (arrivals2 if two else arrivals1):
                            arrival().wait_recv()
                    for starts, _ in first_round + second_round:
                        for copy in starts:
                            copy().wait_send()

        sems = [pltpu.SemaphoreType.DMA((nsem,)), pltpu.SemaphoreType.DMA((nsem,))] if tasks else []
        res = pl.pallas_call(
            wrapped, name=name,
            grid_spec=pltpu.PrefetchScalarGridSpec(
                num_scalar_prefetch=num_prefetch, grid=tuple(grid),
                in_specs=in_specs + [ANY_SPEC] * len(c_ops),
                out_specs=out_specs + [ANY_SPEC] * len(c_outs),
                scratch_shapes=scratch_shapes + sems),
            out_shape=out_shape + c_outs,
            input_output_aliases=aliases,
            compiler_params=_cparams(nax, PEER_SET_COLLECTIVE_ID[peer_set] if peer_set else None),
        )(*operands, *c_ops)
        co = n_out
        for t, souts in zip(tasks, t_outs):
            t.finish(res[co:co + len(souts)])
            co += len(souts)
        for fn in after:
            fn()
        return res[0] if single else list(res[:n_out])

    return run


def _comm_call(name):
    def body(o_ref):
        o_ref[...] = jnp.zeros_like(o_ref)

    _pcall(body, name=name, grid=(1,), in_specs=[], out_specs=_spec((8, 128), lambda i: (0, 0)),
           out_shape=_sds((8, 128), F32))()


def _gemm(name, terms, grid, outs, acc_shape, extras=(), epilogue=None):
    kinds = [t[4] for t in terms]
    nt, ne, no = len(terms), len(extras), len(outs)
    nred = grid[-1]
    nax = len(grid)

    def body(*refs):
        trefs = refs[:2 * nt]
        erefs = refs[2 * nt:2 * nt + ne]
        orefs = refs[2 * nt + ne:2 * nt + ne + no]
        ids = [pl.program_id(k) for k in range(nax)]
        tot = None
        for t in range(nt):
            d = _dot(trefs[2 * t][...], trefs[2 * t + 1][...], kinds[t])
            tot = d if tot is None else tot + d

        def finish(acc):
            if epilogue is None:
                orefs[0][...] = acc.astype(orefs[0].dtype)
            else:
                epilogue(acc, erefs, orefs, ids)

        if nred == 1:
            finish(tot)
        else:
            acc_ref = refs[-1]
            r = ids[-1]

            @pl.when(r == 0)
            def _():
                acc_ref[...] = tot

            @pl.when(r > 0)
            def _():
                acc_ref[...] += tot

            @pl.when(r == nred - 1)
            def _():
                finish(acc_ref[...])

    operands, in_specs = [], []
    for a, a_spec, b, b_spec, _ in terms:
        operands += [a, b]
        in_specs += [a_spec, b_spec]
    for e, e_spec in extras:
        operands.append(e)
        in_specs.append(e_spec)
    scratch = [pltpu.VMEM(tuple(acc_shape), F32)] if nred > 1 else []
    return _pcall(body, name=name, grid=tuple(grid), in_specs=in_specs, out_specs=[o[1] for o in outs],
                  out_shape=[o[0] for o in outs], scratch_shapes=scratch)(*operands)


def _rowwise(name, fn, ins, outs, grid):
    ni = len(ins)
    nax = len(grid)

    def body(*refs):
        ids = [pl.program_id(k) for k in range(nax)]
        fn(refs[:ni], refs[ni:], ids)

    return _pcall(body, name=name, grid=tuple(grid), in_specs=[i[1] for i in ins],
                  out_specs=[o[1] for o in outs], out_shape=[o[0] for o in outs])(*[i[0] for i in ins])


def _ffn_up(name, h, w1buf, w1_idx, w3buf, w3_idx, norm_gain=None):
    T = h.shape[0]
    tm = min(FFN_ROW_TILE, T)
    normed = norm_gain is not None

    def body(h_ref, *refs):
        if normed:
            g_ref, w1_ref, w3_ref, a_ref, b_ref, s_ref, hn_ref = refs
            hv = _rms_fwd(h_ref[...], g_ref[...]).astype(BF16)
            hn_ref[...] = hv
        else:
            w1_ref, w3_ref, a_ref, b_ref, s_ref = refs
            hv = h_ref[...]
        a = _dot(hv, w1_ref[...], "nt")
        b = _dot(hv, w3_ref[...], "nt")
        a_ref[...] = a.astype(BF16)
        b_ref[...] = b.astype(BF16)
        s_ref[...] = ((a * _sigmoid(a)) * b).astype(BF16)

    row = _spec((tm, D), lambda i: (i, 0))
    blk = _spec((tm, D_FF), lambda i: (i, 0))
    return _pcall(
        body, name=name, grid=(T // tm,),
        in_specs=[row] + ([_spec((1, D), lambda i: (0, 0))] if normed else [])
        + [_spec((N_CHIPS, None, FF_BLK, D), lambda i: (0, w1_idx, 0, 0)),
           _spec((N_CHIPS, None, FF_BLK, D), lambda i: (0, w3_idx, 0, 0))],
        out_specs=[blk, blk, blk] + ([row] if normed else []),
        out_shape=[_sds((T, D_FF), BF16)] * 3 + ([_sds((T, D), BF16)] if normed else []),
    )(h, *([norm_gain] if normed else []), w1buf, w3buf)


def _loss_head(x, g, tgt, loss_ref, dx_ref, dg_ref, first):
    err = _rms_fwd(x, g) - tgt
    lp = 0.5 * jnp.sum(jnp.mean(err * err, axis=-1, keepdims=True), axis=0, keepdims=True)
    _accumulate(loss_ref, jnp.broadcast_to(lp, (1, 128)), first)
    dx, dgp = _rms_bwd(x, g, err * (1.0 / D))
    dx_ref[...] = dx
    _accumulate(dg_ref, dgp, first)


def _ffn_down(name, s, wrow2, w2_idx, x_res, g_next=None, loss_target=None):
    T = x_res.shape[0]
    tm = min(ROW_TILE, T)
    row = lambda i, j, r: (i, 0)
    vec = lambda i, j, r: (0, 0)

    def epilogue(acc, erefs, orefs, ids):
        xo = erefs[0][...] + 0.5 * acc
        if loss_target is not None:
            _loss_head(xo, erefs[1][...], erefs[2][...], orefs[0], orefs[1], orefs[2], ids[0] == 0)
            return
        orefs[0][...] = xo
        orefs[1][...] = _rms_fwd(xo, erefs[1][...]).astype(BF16)

    extras = [(x_res, _spec((tm, D), row)), (g_next, _spec((1, D), vec))]
    if loss_target is None:
        outs = [(_sds((T, D), F32), _spec((tm, D), row)), (_sds((T, D), BF16), _spec((tm, D), row))]
    else:
        extras.append((loss_target, _spec((tm, D), row)))
        outs = [(_sds((1, 128), F32), _spec((1, 128), vec)), (_sds((T, D), F32), _spec((tm, D), row)),
                (_sds((1, D), F32), _spec((1, D), vec))]
    return _gemm(
        name,
        [(s, _spec((tm, D_FF), row),
          wrow2, _spec((N_CHIPS, None, FF_BLK, D), lambda i, j, r: (0, w2_idx, 0, 0)), "nn")],
        (T // tm, 1, 1), outs, (tm, D), extras, epilogue)


def _ffn_bwd_mid(name, dx, wrow2, w2_idx, a, b):
    T = dx.shape[0]
    tm = min(FFN_ROW_TILE, T)

    def body(dx_ref, w2_ref, a_ref, b_ref, dab_ref):
        ds = _dot(0.5 * dx_ref[...], w2_ref[...], "nt")
        av = a_ref[...].astype(F32)
        sg = _sigmoid(av)
        dab_ref[0] = (ds * b_ref[...].astype(F32) * (sg * (1.0 + av * (1.0 - sg)))).astype(BF16)
        dab_ref[1] = (ds * (av * sg)).astype(BF16)

    blk = _spec((tm, D_FF), lambda i: (i, 0))
    return _pcall(
        body, name=name, grid=(T // tm,),
        in_specs=[_spec((tm, D), lambda i: (i, 0)),
                  _spec((N_CHIPS, None, FF_BLK, D), lambda i: (0, w2_idx, 0, 0)),
                  blk, blk],
        out_specs=_spec((2, tm, D_FF), lambda i: (0, i, 0)),
        out_shape=_sds((2, T, D_FF), BF16),
    )(dx, wrow2, a, b)


def _rms_bwd_epilogue(acc, erefs, orefs, ids):
    dx, dgp = _rms_bwd(erefs[0][...], erefs[1][...], acc)
    orefs[0][...] = dx + erefs[2][...]
    _accumulate(orefs[1], dgp, ids[0] == 0)


def _rms_bwd_io(x, g, dres, T, tm):
    row = lambda i, j, r: (i, 0)
    vec = lambda i, j, r: (0, 0)
    extras = [(x, _spec((tm, D), row)), (g, _spec((1, D), vec)), (dres, _spec((tm, D), row))]
    outs = [(_sds((T, D), F32), _spec((tm, D), row)), (_sds((1, D), F32), _spec((1, D), vec))]
    return extras, outs


def _ffn_bwd(tag, dx_out, h, a, b, s, w1buf, w1_idx, w3buf, w3_idx, wrow2, w2_idx, x_in, g, big):
    T = dx_out.shape[0]
    dab = _ffn_bwd_mid(tag + "_bwd_mid", dx_out, wrow2, w2_idx, a, b)

    def half_scale(acc, erefs, orefs, ids):
        orefs[0][...] = (0.5 * acc).astype(orefs[0].dtype)

    dw_grid = (D_FF // DW_BLK, 1, 1)
    dw_out = [(_sds((D_FF, D), GRAD_WIRE_DTYPE), _spec((DW_BLK, D), lambda j, n, r: (j, 0)))]
    tokens = _spec((T, D), lambda j, n, r: (0, 0))
    big[tag + "_w2"] = _gemm(
        tag + "_dw2", [(s, _spec((T, DW_BLK), lambda j, n, r: (0, j)), dx_out, tokens, "tn")],
        dw_grid, dw_out, (DW_BLK, D), (), half_scale)[0].reshape(1, N_CHIPS, FF_BLK, D)
    for widx, wname in ((0, "_w1"), (1, "_w3")):
        big[tag + wname] = _gemm(
            tag + "_d" + wname[1:],
            [(dab, _spec((None, T, DW_BLK), functools.partial(lambda w, j, n, r: (w, 0, j), widx)), h, tokens, "tn")],
            dw_grid, dw_out, (DW_BLK, D))[0].reshape(1, N_CHIPS, FF_BLK, D)
    tm = min(FFN_ROW_TILE, T)
    extras, outs = _rms_bwd_io(x_in, g, dx_out, T, tm)
    whole = lambda idx: _spec((N_CHIPS, None, FF_BLK, D), lambda i, j, r: (0, idx, 0, 0))
    dx_in, dg = _gemm(
        tag + "_dh",
        [(dab, _spec((None, tm, D_FF), lambda i, j, r: (0, i, 0)), w1buf, whole(w1_idx), "nn"),
         (dab, _spec((None, tm, D_FF), lambda i, j, r: (1, i, 0)), w3buf, whole(w3_idx), "nn")],
        (T // tm, 1, 1), outs, (tm, D), extras, _rms_bwd_epilogue)
    return dx_in, dg


def _proj_sq(name, a, wsq, idx, kind, out_dtype=F32, extras=(), epilogue=None, outs=None):
    M = a.shape[0]
    tm = min(ROW_TILE, M)
    if outs is None:
        outs = [(_sds((M, D), out_dtype), _spec((tm, D), lambda i, j, r: (i, 0)))]
    return _gemm(
        name,
        [(a, _spec((tm, D), lambda i, j, r: (i, 0)),
          wsq, _spec((N_CHIPS, None, SQ_BLK, D), lambda i, j, r: (0, idx, 0, 0)), kind)],
        (M // tm, 1, 1), outs, (tm, D), extras, epilogue)


def _dw_sq(name, a, b):
    M = a.shape[0]
    tn = D // 2
    whole = _gemm(
        name,
        [(a, _spec((M, D), lambda i, j, r: (0, 0)), b, _spec((M, tn), lambda i, j, r: (0, j)), "tn")],
        (1, D // tn, 1),
        [(_sds((D, D), GRAD_WIRE_DTYPE), _spec((D, tn), lambda i, j, r: (0, j)))],
        (D, tn))[0]
    return whole.reshape(N_CHIPS, SQ_BLK, D)


def _retention_constants(T):
    pos = jnp.arange(T, dtype=F32)
    inv_freq = ROPE_BASE ** (-jnp.arange(0, RET_DK, 2, dtype=F32) / RET_DK)
    ang = pos[:, None] * inv_freq[None, :]
    cosf = jnp.concatenate([jnp.cos(ang), jnp.cos(ang)], axis=1)
    sins = jnp.concatenate([-jnp.sin(ang), jnp.sin(ang)], axis=1)
    lg = jnp.log(1.0 - 2.0 ** (-5.0 - jnp.arange(RET_HEADS, dtype=F32)))
    p = jnp.arange(CHUNK, dtype=F32)
    rel = p[:, None] - p[None, :]
    dmat = jnp.where(rel[None] >= 0, jnp.exp(rel[None] * lg[:, None, None]), 0.0)
    kd = jnp.exp((CHUNK - 1.0 - p)[None, :] * lg[:, None])[:, :, None]
    qd = jnp.exp((p + 1.0)[None, :] * lg[:, None])[:, :, None]
    cd = jnp.exp(CHUNK * lg)[:, None, None]
    return cosf, sins, dmat, kd, qd, cd


def _rot(t, cosv, sinv):
    return t * cosv + pltpu.roll(t, RET_DK // 2, 1) * sinv


def _unrot(t, cosv, sinv):
    return t * cosv - pltpu.roll(t, RET_DK // 2, 1) * sinv


def _ret_const_specs(cm):
    whole = lambda shape: _spec(shape, lambda c: (0,) * len(shape))
    return [
        _spec((RET_STEP_ROWS, RET_DK), lambda c: (cm(c), 0)),
        _spec((RET_STEP_ROWS, RET_DK), lambda c: (cm(c), 0)),
        whole((RET_HEADS, CHUNK, CHUNK)), whole((RET_HEADS, CHUNK, 1)), whole((RET_HEADS, CHUNK, 1)),
        whole((RET_HEADS, 1, 1)),
    ]


def _head(h, width):
    return slice(h * width, (h + 1) * width)


def _ret_fwd(u, consts, ret_gn):
    T = u.shape[0]
    nC = T // CHUNK
    kscale = RET_DK ** -0.5

    def body(q_ref, k_ref, v_ref, g_ref, cos_ref, sin_ref, dm_ref, kd_ref, qd_ref, cd_ref, gn_ref,
             qr_ref, kr_ref, ret_ref, yr_ref, st_ref, state):
        @pl.when(pl.program_id(0) == 0)
        def _():
            state[...] = jnp.zeros_like(state)

        for cc in range(RET_STEP_CHUNKS):
            rows = slice(cc * CHUNK, (cc + 1) * CHUNK)
            cosv, sinv = cos_ref[rows, :], sin_ref[rows, :]
            for h in range(RET_HEADS):
                hk, hv = _head(h, RET_DK), _head(h, RET_DV)
                q = _rot(q_ref[rows, hk], cosv, sinv)
                k = _rot(k_ref[rows, hk], cosv, sinv) * kscale
                v = v_ref[rows, hv]
                qr_ref[rows, hk] = q
                kr_ref[rows, hk] = k
                prev = state[h]
                st_ref[h, cc] = prev
                s = _dot(q, k, "nt") * dm_ref[h]
                ret = _dot(s, v, "nn") + _dot(q, prev, "nn") * qd_ref[h]
                state[h] = cd_ref[h] * prev + _dot(k * kd_ref[h], v, "tn")
                ret_ref[rows, hv] = ret
                mu = jnp.mean(ret, axis=-1, keepdims=True)
                xc = ret - mu
                yn = xc * lax.rsqrt(jnp.mean(xc * xc, axis=-1, keepdims=True) + EPS)
                g = g_ref[rows, hv]
                yr_ref[rows, hv] = ((g * _sigmoid(g)) * (yn * gn_ref[:, hv])).astype(BF16)

    cm = lambda c: c
    qk_w, v_w = RET_HEADS * RET_DK, RET_HEADS * RET_DV
    in_specs = [
        _spec((RET_STEP_ROWS, qk_w), lambda c: (c, 0)), _spec((RET_STEP_ROWS, qk_w), lambda c: (c, 1)),
        _spec((RET_STEP_ROWS, v_w), lambda c: (c, 1)), _spec((RET_STEP_ROWS, v_w), lambda c: (c, 2)),
    ] + _ret_const_specs(cm) + [_spec((1, v_w), lambda c: (0, 0))]
    qk_out = _spec((RET_STEP_ROWS, qk_w), lambda c: (c, 0))
    v_out = _spec((RET_STEP_ROWS, v_w), lambda c: (c, 0))
    return _pcall(
        body, name="ret_fwd", grid=(nC // RET_STEP_CHUNKS,),
        in_specs=in_specs,
        out_specs=[qk_out, qk_out, v_out, v_out,
                   _spec((RET_HEADS, RET_STEP_CHUNKS, RET_DK, RET_DV), lambda c: (0, c, 0, 0))],
        out_shape=[_sds((T, qk_w), F32), _sds((T, qk_w), F32), _sds((T, v_w), F32), _sds((T, v_w), BF16),
                   _sds((RET_HEADS, nC, RET_DK, RET_DV), F32)],
        scratch_shapes=[pltpu.VMEM((RET_HEADS, RET_DK, RET_DV), F32)],
    )(u, u, u, u, *consts, ret_gn)


def _ret_bwd(dyr, ret, u, qr, kr, states, consts, ret_gn):
    T = u.shape[0]
    nC = T // CHUNK
    kscale = RET_DK ** -0.5

    def body(dyr_ref, ret_ref, g_ref, q_ref, k_ref, v_ref, st_ref,
             cos_ref, sin_ref, dm_ref, kd_ref, qd_ref, cd_ref, gn_ref,
             dq_ref, dk_ref, dv_ref, dg_ref, dgn_ref, gstate):
        first = pl.program_id(0) == 0

        @pl.when(first)
        def _():
            gstate[...] = jnp.zeros_like(gstate)

        dgn_total = None
        for cc in reversed(range(RET_STEP_CHUNKS)):
            rows = slice(cc * CHUNK, (cc + 1) * CHUNK)
            cosv, sinv = cos_ref[rows, :], sin_ref[rows, :]
            dgn_parts = []
            for h in range(RET_HEADS):
                hk, hv = _head(h, RET_DK), _head(h, RET_DV)
                ret = ret_ref[rows, hv]
                mu = jnp.mean(ret, axis=-1, keepdims=True)
                xc = ret - mu
                rs = lax.rsqrt(jnp.mean(xc * xc, axis=-1, keepdims=True) + EPS)
                yn = xc * rs
                gn = gn_ref[:, hv]
                g = g_ref[rows, hv]
                sg = _sigmoid(g)
                dyr_v = dyr_ref[rows, hv]
                dretn = dyr_v * (g * sg)
                dg_ref[rows, hv] = (dyr_v * (yn * gn) * (sg * (1.0 + g * (1.0 - sg)))).astype(BF16)
                dgn_parts.append(jnp.sum(dretn * yn, axis=0, keepdims=True))
                dyn = dretn * gn
                d_o = rs * (dyn - jnp.mean(dyn, axis=-1, keepdims=True)
                            - yn * jnp.mean(dyn * yn, axis=-1, keepdims=True))

                q, k, v = q_ref[rows, hk], k_ref[rows, hk], v_ref[rows, hv]
                dmat, kd, qd = dm_ref[h], kd_ref[h], qd_ref[h]
                prev = st_ref[h, cc]
                gnext = gstate[h]
                s = _dot(q, k, "nt") * dmat
                ds = _dot(d_o, v, "nt") * dmat
                doq = d_o * qd
                dq = _dot(ds, k, "nn") + _dot(doq, prev, "nt")
                dk = _dot(ds, q, "tn") + _dot(v, gnext, "nt") * kd
                dv = _dot(s, d_o, "tn") + _dot(k * kd, gnext, "nn")
                gstate[h] = cd_ref[h] * gnext + _dot(q, doq, "tn")
                dq_ref[rows, hk] = _unrot(dq, cosv, sinv).astype(BF16)
                dk_ref[rows, hk] = _unrot(dk * kscale, cosv, sinv).astype(BF16)
                dv_ref[rows, hv] = dv.astype(BF16)
            dgn = jnp.concatenate(dgn_parts, axis=1)
            dgn_total = dgn if dgn_total is None else dgn_total + dgn
        _accumulate(dgn_ref, dgn_total, first)

    n_steps = nC // RET_STEP_CHUNKS
    cm = lambda c: n_steps - 1 - c
    qk_w, v_w = RET_HEADS * RET_DK, RET_HEADS * RET_DV
    vspec = lambda blk: _spec((RET_STEP_ROWS, v_w), lambda c: (cm(c), blk))
    qspec = _spec((RET_STEP_ROWS, qk_w), lambda c: (cm(c), 0))
    in_specs = [vspec(0), vspec(0), vspec(2), qspec, qspec, vspec(1),
                _spec((RET_HEADS, RET_STEP_CHUNKS, RET_DK, RET_DV), lambda c: (0, cm(c), 0, 0)),
                ] + _ret_const_specs(cm) + [_spec((1, v_w), lambda c: (0, 0))]
    return _pcall(
        body, name="ret_bwd", grid=(n_steps,),
        in_specs=in_specs,
        out_specs=[qspec, qspec, vspec(0), vspec(0), _spec((1, v_w), lambda c: (0, 0))],
        out_shape=[_sds((T, qk_w), BF16), _sds((T, qk_w), BF16), _sds((T, v_w), BF16), _sds((T, v_w), BF16),
                   _sds((1, v_w), F32)],
        scratch_shapes=[pltpu.VMEM((RET_HEADS, RET_DK, RET_DV), F32)],
    )(dyr, ret, u, qr, kr, u, states, *consts, ret_gn)


def _shift_down(x, s):
    rows = lax.broadcasted_iota(jnp.int32, x.shape, 0)
    return jnp.where(rows >= s, pltpu.roll(x, s, 0), 0.0)


def _shift_up(x, s):
    n = x.shape[0]
    rows = lax.broadcasted_iota(jnp.int32, x.shape, 0)
    return jnp.where(rows < n - s, pltpu.roll(x, n - s, 0), 0.0)


def _lru_specs(T):
    col = lambda off: _spec((T, LRU_BLOCK), lambda g: (0, off + g))
    vec = _spec((1, LRU_BLOCK), lambda g: (0, g))
    wblk = _spec((None, LRU_BLOCK, LRU_BLOCK), lambda g: (g, 0, 0))
    cw = _spec((CONV_TAPS, LRU_BLOCK), lambda g: (0, g))
    return col, vec, wblk, cw


def _lru_gates_fwd(u, conv_w, conv_b, w_r, b_r, w_i, b_i, lam):
    T = u.shape[0]
    col, vec, wblk, cw = _lru_specs(T)

    def body(x_ref, cw_ref, cb_ref, wr_ref, br_ref, wi_ref, bi_ref, lam_ref,
             xc_ref, r_ref, i_ref, a_ref, bx_ref):
        x = x_ref[...]
        w = cw_ref[...]
        xc = (_shift_down(x, 3) * w[0:1] + _shift_down(x, 2) * w[1:2] + _shift_down(x, 1) * w[2:3]
              + x * w[3:4] + cb_ref[...])
        r = _sigmoid(_dot(xc, wr_ref[...], "nn") + br_ref[...])
        i = _sigmoid(_dot(xc, wi_ref[...], "nn") + bi_ref[...])
        la = (-LRU_C) * r * _softplus(-lam_ref[...])
        xc_ref[...] = xc
        r_ref[...] = r
        i_ref[...] = i
        a_ref[...] = jnp.exp(la)
        bx_ref[...] = jnp.sqrt(-_expm1(2.0 * la)) * (i * xc)

    out = col(0)
    return _pcall(
        body, name="lru_gates_fwd", grid=(LRU_BLOCKS,),
        in_specs=[col(24), cw, vec, wblk, vec, wblk, vec, vec],
        out_specs=[out] * 5,
        out_shape=[_sds((T, D), F32)] * 5,
    )(u, conv_w, conv_b, w_r, b_r, w_i, b_i, lam)


def _lru_scan(name, a3, b3, reverse):
    T = a3.shape[0]
    nt = T // SCAN_TILE
    unroll = 8

    def body(a_ref, b_ref, o_ref, carry):
        @pl.when(pl.program_id(0) == 0)
        def _():
            carry[...] = jnp.zeros_like(carry)

        if not reverse:
            def step(t, h):
                h = a_ref[t] * h + b_ref[t]
                o_ref[t] = h
                return h
        else:
            def step(k, c):
                t = SCAN_TILE - 1 - k
                l = b_ref[t] + c
                o_ref[t] = l
                return a_ref[t] * l
        carry[...] = lax.fori_loop(0, SCAN_TILE, step, carry[...], unroll=unroll)

    idx = (lambda i: (nt - 1 - i, 0, 0)) if reverse else (lambda i: (i, 0, 0))
    blk = _spec((SCAN_TILE, LRU_BLOCKS, LRU_BLOCK), idx)
    return _pcall(
        body, name=name, grid=(nt,),
        in_specs=[blk, blk], out_specs=blk,
        out_shape=_sds((T, LRU_BLOCKS, LRU_BLOCK), F32),
        scratch_shapes=[pltpu.VMEM((LRU_BLOCKS, LRU_BLOCK), F32)],
    )(a3, b3)


def _lru_gates_bwd(lmb, hl, a, r, i, xc, u, conv_w, w_r, w_i, lam):
    T = u.shape[0]
    col, vec, wblk, cw = _lru_specs(T)

    def body(l_ref, h_ref, a_ref, r_ref, i_ref, xc_ref, x_ref, cw_ref, wr_ref, wi_ref, lam_ref,
             dx_ref, dwr_ref, dwi_ref, dvec_ref, dcw_ref):
        l = l_ref[...]
        av, rv, iv, xc = a_ref[...], r_ref[...], i_ref[...], xc_ref[...]
        lam_v = lam_ref[...]
        sp = _softplus(-lam_v)
        la = (-LRU_C) * rv * sp
        mult = jnp.sqrt(-_expm1(2.0 * la))
        da = l * _shift_down(h_ref[...], 1)
        dmult = l * (iv * xc)
        di = l * mult * xc
        dxc = l * mult * iv
        dla = da * av - dmult * (av * av) / mult
        dzr = (dla * ((-LRU_C) * sp)) * rv * (1.0 - rv)
        dzi = di * iv * (1.0 - iv)
        dsp = jnp.sum(dla * ((-LRU_C) * rv), axis=0, keepdims=True)
        dlam = dsp * (-_sigmoid(-lam_v))
        dwr_ref[...] = _dot(xc, dzr, "tn")
        dwi_ref[...] = _dot(xc, dzi, "tn")
        dxc = dxc + _dot(dzr, wr_ref[...], "nt") + _dot(dzi, wi_ref[...], "nt")
        x = x_ref[...]
        w = cw_ref[...]
        dx = (dxc * w[3:4] + _shift_up(dxc, 1) * w[2:3] + _shift_up(dxc, 2) * w[1:2]
              + _shift_up(dxc, 3) * w[0:1])
        dx_ref[...] = dx.astype(BF16)
        dvec_ref[...] = jnp.concatenate(
            [jnp.sum(dzr, axis=0, keepdims=True), jnp.sum(dzi, axis=0, keepdims=True), dlam,
             jnp.sum(dxc, axis=0, keepdims=True)], axis=0)
        dcw_ref[...] = jnp.concatenate(
            [jnp.sum(dxc * _shift_down(x, 3 - tap), axis=0, keepdims=True) if tap < 3
             else jnp.sum(dxc * x, axis=0, keepdims=True) for tap in range(CONV_TAPS)], axis=0)

    c0 = col(0)
    return _pcall(
        body, name="lru_gates_bwd", grid=(LRU_BLOCKS,),
        in_specs=[c0, c0, c0, c0, c0, c0, col(24), cw, wblk, wblk, vec],
        out_specs=[c0, wblk, wblk, cw, cw],
        out_shape=[_sds((T, D), BF16), _sds((LRU_BLOCKS, LRU_BLOCK, LRU_BLOCK), F32),
                   _sds((LRU_BLOCKS, LRU_BLOCK, LRU_BLOCK), F32), _sds((4, D), F32), _sds((CONV_TAPS, D), F32)],
    )(lmb, hl, a, r, i, xc, u, conv_w, w_r, w_i, lam)


def _xattn_probs(q, k):
    sc = _dot(q, k, "nt") * (X_HD ** -0.5)
    e = jnp.exp(sc - jnp.max(sc, axis=-1, keepdims=True))
    return e / jnp.sum(e, axis=-1, keepdims=True)


def _xattn_fwd(xq, xk, xv):
    T = xq.shape[0]
    tq = min(WIDE_ROW_TILE, T)
    M = xk.shape[0]

    def body(q_ref, k_ref, v_ref, o_ref):
        p = _xattn_probs(q_ref[...], k_ref[...])
        o_ref[...] = _dot(p, v_ref[...], "nn").astype(BF16)

    qs = _spec((tq, X_HD), lambda h, i: (i, h))
    kv = _spec((M, X_HD), lambda h, i: (0, h))
    return _pcall(
        body, name="xattn_fwd", grid=(X_HEADS, T // tq),
        in_specs=[qs, kv, kv], out_specs=qs, out_shape=_sds((T, D), BF16),
    )(xq, xk, xv)


def _xattn_bwd(xq, xk, xv, dxo):
    T = xq.shape[0]
    tq = min(WIDE_ROW_TILE, T)
    M = xk.shape[0]

    def body(q_ref, k_ref, v_ref, do_ref, dq_ref, dk_ref, dv_ref):
        first = pl.program_id(1) == 0
        q, k, v, do = q_ref[...], k_ref[...], v_ref[...], do_ref[...]
        p = _xattn_probs(q, k)
        dp = _dot(do, v, "nt")
        ds = p * (dp - jnp.sum(dp * p, axis=-1, keepdims=True)) * (X_HD ** -0.5)
        dq_ref[...] = _dot(ds, k, "nn").astype(BF16)
        _accumulate(dk_ref, _dot(ds, q, "tn"), first)
        _accumulate(dv_ref, _dot(p, do, "tn"), first)

    qs = _spec((tq, X_HD), lambda h, i: (i, h))
    kv = _spec((M, X_HD), lambda h, i: (0, h))
    return _pcall(
        body, name="xattn_bwd", grid=(X_HEADS, T // tq),
        in_specs=[qs, kv, kv, qs], out_specs=[qs, kv, kv],
        out_shape=[_sds((T, D), BF16), _sds((M, D), F32), _sds((M, D), F32)],
    )(xq, xk, xv, dxo)


def _adamw(name, w, g, m, v):
    R, C = w.shape
    tr = R
    for cand in (512, 352, 256):
        if R % cand == 0:
            tr = cand
            break

    def fn(irefs, orefs, ids):
        delta, mn, vn = _adamw_update(*(r[...] for r in irefs))
        orefs[0][...] = delta
        orefs[1][...] = mn
        orefs[2][...] = vn

    blk = _spec((tr, C), lambda i: (i, 0))
    return _rowwise(name, fn, [(w, blk), (g, blk), (m, blk), (v, blk)],
                    [(_sds((R, C), F32), blk)] * 3, (R // tr,))


def _adamw_update(wv, gv, mv, vv):
    c1 = 1.0 - ADAM_B1 ** ADAM_STEP
    c2 = 1.0 - ADAM_B2 ** ADAM_STEP
    mn = ADAM_B1 * mv + (1.0 - ADAM_B1) * gv
    vn = ADAM_B2 * vv + (1.0 - ADAM_B2) * (gv * gv)
    delta = -ADAM_LR * ((mn / c1) / (jnp.sqrt(vn / c2) + ADAM_EPS) + ADAM_WD * wv)
    return delta, mn, vn


def _adamw_halves(name, w, mine, theirs, widx, m, v, core):
    R, C = w.shape
    H = R // 2
    tr = H
    while tr * C * 4 > (1 << 20) and tr % 16 == 0:
        tr //= 2
    nb = H // tr

    def body(core_ref, w_ref, mine_ref, theirs_ref, m_ref, v_ref, g_out, d_out, m_out, v_out):
        gv = jnp.where(pl.program_id(0) == core_ref[0], mine_ref[...], theirs_ref[...])
        delta, mn, vn = _adamw_update(w_ref[...], gv, m_ref[...], v_ref[...])
        g_out[...] = gv
        d_out[...] = delta
        m_out[...] = mn
        v_out[...] = vn

    full = pl.BlockSpec((tr, C), lambda h, i, core_ref: (h * nb + i, 0))
    mine_spec = pl.BlockSpec((None, tr, C), lambda h, i, core_ref: (widx, jnp.where(h == core_ref[0], i, 0), 0))
    theirs_spec = pl.BlockSpec((None, tr, C), lambda h, i, core_ref: (widx, jnp.where(h == core_ref[0], 0, i), 0))
    return _pcall(
        body, name=name, grid=(2, nb), num_prefetch=1,
        in_specs=[full, mine_spec, theirs_spec, full, full], out_specs=[full] * 4,
        out_shape=[_sds((R, C), F32)] * 4,
    )(core, w, mine, theirs, m, v)


def _rmsnorm(name, x, g):
    M = x.shape[0]
    tm = min(ROW_TILE, M)

    def fn(irefs, orefs, ids):
        orefs[0][...] = _rms_fwd(irefs[0][...], irefs[1][...]).astype(BF16)

    row = _spec((tm, D), lambda i: (i, 0))
    return _rowwise(name, fn, [(x, row), (g, _spec((1, D), lambda i: (0, 0)))],
                    [(_sds((M, D), BF16), row)], (M // tm,))[0]


WEIGHT_AT = {
    "ffn1_w1": ("col1", 0), "ffn1_w3": ("col1", 1), "ffn1_w2": ("row2a", 0),
    "w_ret_o": ("sqA", 0), "w_lru_o": ("sqA", 1), "w_out": ("sqA", 2),
    "w_xq": ("sqB", 0), "w_xk": ("sqB", 1), "w_xv": ("sqC", 0), "w_xo": ("sqC", 1),
    "ffn2_w1": ("col2a", 0), "ffn2_w3": ("col2b", 0), "ffn2_w2": ("row2b", 0),
}


def _local_step(x, mem, tgt, gw, sm, big):
    T = x.shape[0]
    tm = ROW_TILE

    def wt(name):
        key, idx = WEIGHT_AT[name]
        return gw[key], idx

    row3 = lambda i, j, r: (i, 0)
    vec3 = lambda i, j, r: (0, 0)
    rowD = _spec((tm, D), row3)
    vecD = _spec((1, D), vec3)

    def residual_norm(acc, erefs, orefs, ids):
        xo = erefs[0][...] + acc
        orefs[0][...] = xo
        orefs[1][...] = _rms_fwd(xo, erefs[1][...]).astype(BF16)

    def res_norm_io(x_res, g):
        return ([(x_res, rowD), (g, vecD)],
                [(_sds((T, D), F32), rowD), (_sds((T, D), BF16), rowD)])

    a1, b1, s1, h1 = _ffn_up("ffn1_up", x, *wt("ffn1_w1"), *wt("ffn1_w3"), norm_gain=sm["ffn1_norm"])
    x1, h2 = _ffn_down("ffn1_down", s1, *wt("ffn1_w2"), x, sm["mix_norm"])

    tw = min(WIDE_ROW_TILE, T)
    wideD = _spec((tw, D), row3)
    u = _gemm(
        "mix_in",
        [(h2, wideD, gw["win"], _spec((None, None, IN_BLK, D), lambda i, j, r: (j, 0, 0, 0)), "nt")],
        (T // tw, N_CHIPS, 1),
        [(_sds((T, 5120), F32), _spec((tw, IN_BLK), lambda i, j, r: (i, j)))], (tw, IN_BLK))[0]

    consts = _retention_constants(T)
    qr, kr, ret, yr, states = _ret_fwd(u, consts, sm["ret_gn"])

    conv_w = gw["conv"][:, 0].transpose(1, 0, 2).reshape(CONV_TAPS, D)
    xc, rg, ig, av, bx = _lru_gates_fwd(u, conv_w, sm["conv_b"], sm["w_rgate"], sm["b_rgate"],
                                        sm["w_igate"], sm["b_igate"], sm["lru_lambda"])
    a3 = av.reshape(T, LRU_BLOCKS, LRU_BLOCK)
    hl = _lru_scan("lru_scan_fwd", a3, bx.reshape(T, LRU_BLOCKS, LRU_BLOCK), False).reshape(T, D)

    row1 = _spec((tm, D), lambda i: (i, 0))
    glru1 = _spec((tm, D), lambda i: (i, 4))

    def lru_out(irefs, orefs, ids):
        gl, _ = _gelu_and_grad(irefs[1][...])
        orefs[0][...] = (irefs[0][...] * gl).astype(BF16)

    yl = _rowwise("lru_out", lru_out, [(hl, row1), (u, glru1)], [(_sds((T, D), BF16), row1)], (T // tm,))[0]

    def gate_epilogue(acc, erefs, orefs, ids):
        orefs[0][...] = _sigmoid(acc + erefs[0][...])

    gates = _gemm(
        "mix_gates",
        [(h2, wideD, gw["wbg"], _spec((None, None, BG_BLK, D), lambda i, j, r: (j, 0, 0, 0)), "nt")],
        (T // tw, N_CHIPS, 1),
        [(_sds((T, 2 * D), F32), _spec((tw, BG_BLK), lambda i, j, r: (i, j)))], (tw, BG_BLK),
        [(sm["b_branch_gate"], _spec((1, BG_BLK), lambda i, j, r: (0, j)))], gate_epilogue)[0]

    y_ret = _proj_sq("y_ret", yr, *wt("w_ret_o"), "nn")[0]

    def merge_epilogue(acc, erefs, orefs, ids):
        orefs[0][...] = acc
        orefs[1][...] = (erefs[0][...] * erefs[2][...] + erefs[1][...] * acc).astype(BF16)

    y_lru, merged = _proj_sq(
        "y_lru", yl, *wt("w_lru_o"), "nn",
        extras=[(gates, _spec((tm, D), lambda i, j, r: (i, 0))), (gates, _spec((tm, D), lambda i, j, r: (i, 1))),
                (y_ret, rowD)],
        epilogue=merge_epilogue,
        outs=[(_sds((T, D), F32), rowD), (_sds((T, D), BF16), rowD)])

    ex, ou = res_norm_io(x1, sm["xattn_norm"])
    x2, hq = _proj_sq("mix_out", merged, *wt("w_out"), "nn", extras=ex, epilogue=residual_norm, outs=ou)

    m = _rmsnorm("mem_norm", mem, sm["mem_norm"])
    xq = _proj_sq("xq", hq, *wt("w_xq"), "nn", BF16)[0]
    xk = _proj_sq("xk", m, *wt("w_xk"), "nn", BF16)[0]
    xv = _proj_sq("xv", m, *wt("w_xv"), "nn", BF16)[0]
    xo = _xattn_fwd(xq, xk, xv)
    ex, ou = res_norm_io(x2, sm["ffn2_norm"])
    x3, h3 = _proj_sq("xattn_out", xo, *wt("w_xo"), "nn", extras=ex, epilogue=residual_norm, outs=ou)

    a2, b2, s2 = _ffn_up("ffn2_up", h3, *wt("ffn2_w1"), *wt("ffn2_w3"))
    loss, dx4, dg_final = _ffn_down("ffn2_down", s2, *wt("ffn2_w2"), x3, sm["final_norm"], loss_target=tgt)

    dx3, dg_ffn2 = _ffn_bwd("ffn2", dx4, h3, a2, b2, s2, *wt("ffn2_w1"), *wt("ffn2_w3"),
                            *wt("ffn2_w2"), x3, sm["ffn2_norm"], big)

    dxo = _proj_sq("d_xo", dx3, *wt("w_xo"), "nt", BF16)[0]
    big["w_xo"] = _dw_sq("dw_xo", xo, dx3)[None]
    dxq, dxk, dxv = _xattn_bwd(xq, xk, xv, dxo)
    big["w_xq"] = _dw_sq("dw_xq", hq, dxq)[None]
    ex, ou = _rms_bwd_io(x2, sm["xattn_norm"], dx3, T, tm)
    dx2, dg_xattn = _proj_sq("d_hq", dxq, *wt("w_xq"), "nt", extras=ex, epilogue=_rms_bwd_epilogue, outs=ou)
    big["w_xk"] = _dw_sq("dw_xk", m, dxk)[None]
    big["w_xv"] = _dw_sq("dw_xv", m, dxv)[None]

    M = mem.shape[0]

    def mem_norm_epilogue(acc, erefs, orefs, ids):
        _, dgp = _rms_bwd(erefs[0][...], erefs[1][...], acc)
        orefs[0][...] = dgp

    wsq_spec = lambda idx: _spec((N_CHIPS, None, SQ_BLK, D), lambda i, j, r: (0, idx, 0, 0))
    memD = _spec((M, D), row3)
    dg_mem = _gemm(
        "d_mem_norm",
        [(dxk, memD, wt("w_xk")[0], wsq_spec(wt("w_xk")[1]), "nt"),
         (dxv, memD, wt("w_xv")[0], wsq_spec(wt("w_xv")[1]), "nt")],
        (1, 1, 1), [(_sds((1, D), F32), vecD)], (M, D),
        [(mem, memD), (sm["mem_norm"], vecD)], mem_norm_epilogue)[0]

    def merged_bwd_epilogue(acc, erefs, orefs, ids):
        gr, gl, yrv, ylv = (e[...] for e in erefs)
        orefs[0][...] = (acc * gr).astype(BF16)
        orefs[1][...] = (acc * gl).astype(BF16)
        dgr = acc * yrv * gr * (1.0 - gr)
        dgl = acc * ylv * gl * (1.0 - gl)
        orefs[2][:, :D] = dgr.astype(BF16)
        orefs[2][:, D:] = dgl.astype(BF16)
        dbb = jnp.concatenate([jnp.sum(dgr, axis=0, keepdims=True), jnp.sum(dgl, axis=0, keepdims=True)], axis=1)
        _accumulate(orefs[3], dbb, ids[0] == 0)

    dy_ret, dy_lru, dgpre, db_bg = _proj_sq(
        "d_merged", dx2, *wt("w_out"), "nt",
        extras=[(gates, _spec((tm, D), lambda i, j, r: (i, 0))), (gates, _spec((tm, D), lambda i, j, r: (i, 1))),
                (y_ret, rowD), (y_lru, rowD)],
        epilogue=merged_bwd_epilogue,
        outs=[(_sds((T, D), BF16), rowD), (_sds((T, D), BF16), rowD),
              (_sds((T, 2 * D), BF16), _spec((tm, 2 * D), row3)),
              (_sds((1, 2 * D), F32), _spec((1, 2 * D), vec3))])
    big["w_branch_gate"] = _gemm(
        "dw_bg",
        [(h2, _spec((T, D), lambda j, n, r: (r, 0)), dgpre, _spec((T, BG_BLK), lambda j, n, r: (r, j)), "tn")],
        (N_CHIPS, 1, 1),
        [(_sds((N_CHIPS, D, BG_BLK), GRAD_WIRE_DTYPE), _spec((None, D, BG_BLK), lambda j, n, r: (j, 0, 0)))],
        (D, BG_BLK))[0][None]
    big["w_out"] = _dw_sq("dw_out", merged, dx2)[None]
    dyr = _proj_sq("d_yr", dy_ret, *wt("w_ret_o"), "nt")[0]
    big["w_ret_o"] = _dw_sq("dw_ret_o", yr, dy_ret)[None]
    dyl = _proj_sq("d_yl", dy_lru, *wt("w_lru_o"), "nt")[0]
    big["w_lru_o"] = _dw_sq("dw_lru_o", yl, dy_lru)[None]

    dq, dk, dv, dgr, dg_retgn = _ret_bwd(dyr, ret, u, qr, kr, states, consts, sm["ret_gn"])

    def lru_out_bwd(irefs, orefs, ids):
        gl, dgl = _gelu_and_grad(irefs[2][...])
        dyl_v = irefs[0][...]
        orefs[0][...] = dyl_v * gl
        orefs[1][...] = (dyl_v * irefs[1][...] * dgl).astype(BF16)

    dhl, dglru = _rowwise("lru_out_bwd", lru_out_bwd, [(dyl, row1), (hl, row1), (u, glru1)],
                          [(_sds((T, D), F32), row1), (_sds((T, D), BF16), row1)], (T // tm,))
    lmb = _lru_scan("lru_scan_bwd", a3, dhl.reshape(T, LRU_BLOCKS, LRU_BLOCK), True).reshape(T, D)
    dxl, dw_r, dw_i, dvec, dcw = _lru_gates_bwd(lmb, hl, av, rg, ig, xc, u, conv_w,
                                                sm["w_rgate"], sm["w_igate"], sm["lru_lambda"])

    du = jnp.concatenate([dq, dk, dv, dgr, dxl, dglru], axis=1)
    tk = T
    big["w_in"] = _gemm(
        "dw_in",
        [(h2, _spec((tk, D), lambda j, n, r: (r, 0)), du, _spec((tk, IN_BLK), lambda j, n, r: (r, j)), "tn")],
        (N_CHIPS, 1, T // tk),
        [(_sds((N_CHIPS, D, IN_BLK), GRAD_WIRE_DTYPE), _spec((None, D, IN_BLK), lambda j, n, r: (j, 0, 0)))],
        (D, IN_BLK))[0][None]
    tf = min(FFN_ROW_TILE, T)
    ex, ou = _rms_bwd_io(x1, sm["mix_norm"], dx2, T, tf)
    dx1, dg_mix = _gemm(
        "d_h2",
        [(du, _spec((tf, 5120), row3), gw["win"], _spec((N_CHIPS, None, IN_BLK, D), lambda i, j, r: (0, 0, 0, 0)), "nn"),
         (dgpre, _spec((tf, 2 * D), row3), gw["wbg"], _spec((N_CHIPS, None, BG_BLK, D), lambda i, j, r: (0, 0, 0, 0)),
          "nn")],
        (T // tf, 1, 1), ou, (tf, D), ex, _rms_bwd_epilogue)

    grad_x, dg_ffn1 = _ffn_bwd("ffn1", dx1, h1, a1, b1, s1, *wt("ffn1_w1"), *wt("ffn1_w3"),
                               *wt("ffn1_w2"), x, sm["ffn1_norm"], big)

    small = {
        "ffn1_norm": dg_ffn1, "mix_norm": dg_mix, "ret_gn": dg_retgn, "conv_b": dvec[3:4],
        "b_rgate": dvec[0:1], "b_igate": dvec[1:2], "lru_lambda": dvec[2:3], "xattn_norm": dg_xattn,
        "mem_norm": dg_mem, "ffn2_norm": dg_ffn2, "final_norm": dg_final, "b_branch_gate": db_bg,
        "conv_w": dcw, "w_rgate": dw_r, "w_igate": dw_i,
    }
    return loss, grad_x, small


ANY_SPEC = pl.BlockSpec(memory_space=pl.ANY)
VMEM_SPEC = pl.BlockSpec(memory_space=pltpu.VMEM)
N_PEER_CHIPS = N_CHIPS - 1


def _mesh_position():
    x, y, c = lax.axis_index("x"), lax.axis_index("y"), lax.axis_index("c")
    chips = [(1 - x, y), (x, 1 - y), (1 - x, 1 - y)]
    return x, y, c, chips


def _chip_index(x, y):
    return 2 * x + y


def _rows_half(ref, axis, h):
    n = ref.shape[axis] // 2
    idx = [slice(None)] * len(ref.shape)
    idx[axis] = pl.ds(pl.multiple_of(h * n, 16), n)
    return ref.at[tuple(idx)]


def _remote(src, dst, send_sem, recv_sem, device):
    return pltpu.make_async_remote_copy(src_ref=src, dst_ref=dst, send_sem=send_sem, recv_sem=recv_sem,
                                        device_id=device, device_id_type=MESH)


def _gather_chips_task(shards, split, landed, part=0, nparts=1, legs="both"):
    keys = list(shards)
    n = len(keys)

    def operands():
        if part or legs == "pass_on":
            return [landed[k] for k in keys]
        chip_me = _chip_index(lax.axis_index("x"), lax.axis_index("y"))
        return [lax.dynamic_update_slice(lax.empty((N_CHIPS,) + shards[k].shape, shards[k].dtype), shards[k][None],
                                         (chip_me,) + (0,) * shards[k].ndim) for k in keys]

    def my_rows(ref, c):
        rows = ref.shape[1] // (2 * nparts)
        return ref.at[:, pl.ds(pl.multiple_of((c * nparts + part) * rows, 16), rows), :]

    def make_direct(ins, outs, send_sem, recv_sem):
        x, y, c, chips = _mesh_position()
        s_me = _chip_index(x, y)
        starts, arrivals = [], []
        for g in range(n):
            for k, chip in enumerate(chips):
                sems = (send_sem(3 * g + k), recv_sem(3 * g + k))
                starts.append(functools.partial(_remote, outs[g].at[s_me], outs[g].at[s_me], *sems, (*chip, c)))
                got = outs[g].at[_chip_index(*chip)]
                arrivals.append(functools.partial(_remote, got, got, *sems, (*chip, c)))
        return starts, arrivals

    def axis_neighbours(x, y, c):
        flip = lambda v, f: v + f * (1 - 2 * v)
        return (flip(x, 1 - c), flip(y, c)), (flip(x, c), flip(y, 1 - c))

    def make_swap(ins, outs, send_sem, recv_sem):
        x, y, c, _ = _mesh_position()
        first, _ = axis_neighbours(x, y, c)
        starts, arrivals = [], []
        for g in range(n):
            sems = (send_sem(3 * g), recv_sem(3 * g))
            mine = my_rows(outs[g].at[_chip_index(x, y)], c)
            starts.append(functools.partial(_remote, mine, mine, *sems, (*first, c)))
            got = my_rows(outs[g].at[_chip_index(*first)], c)
            arrivals.append(functools.partial(_remote, got, got, *sems, (*first, c)))
        return starts, arrivals

    def make_pass_on(ins, outs, send_sem, recv_sem):
        x, y, c, _ = _mesh_position()
        first, second = axis_neighbours(x, y, c)
        diagonal = (1 - x, 1 - y)
        starts, arrivals = [], []
        for g in range(n):
            half = lambda chip: my_rows(outs[g].at[_chip_index(*chip)], c)
            for k, (sent, arriving) in enumerate([((x, y), second), (first, diagonal)]):
                sems = (send_sem(3 * g + 1 + k), recv_sem(3 * g + 1 + k))
                starts.append(functools.partial(_remote, half(sent), half(sent), *sems, (*second, c)))
                arrivals.append(functools.partial(_remote, half(arriving), half(arriving), *sems, (*second, c)))
        return starts, arrivals

    def finish(res):
        landed.update(zip(keys, res))

    shapes = lambda: [_sds((N_CHIPS,) + shards[k].shape, shards[k].dtype) for k in keys]
    aliases = {g: g for g in range(n)}
    if not split:
        return _Task("chips", operands, shapes, aliases, 3 * n, make_direct, finish)
    if legs == "swap":
        return _Task("neighbours", operands, shapes, aliases, 3 * n, make_swap, finish)
    if legs == "pass_on":
        return _Task("neighbours", operands, shapes, aliases, 3 * n, make_pass_on, finish)
    return _Task("neighbours", operands, shapes, aliases, 3 * n, make_swap, finish, make_second=make_pass_on)


def _gather_sibling_task(keys, landed, ready):
    n = len(keys)

    def make(ins, outs, send_sem, recv_sem):
        x, y, c, chips = _mesh_position()
        starts, arrivals = [], []
        for g in range(n):
            for k, chip in enumerate(chips):
                o = outs[g].at[_chip_index(*chip)]
                got, other = _rows_half(o, 1, c), _rows_half(o, 1, 1 - c)
                starts.append(functools.partial(_remote, got, got, send_sem(3 * g + k), recv_sem(3 * g + k),
                                                (x, y, 1 - c)))
                arrivals.append(functools.partial(_remote, other, other, send_sem(3 * g + k), recv_sem(3 * g + k),
                                                  (x, y, 1 - c)))
        return starts, arrivals

    def finish(res):
        ready.update(zip(keys, res))

    return _Task("sibling", lambda: [landed[k] for k in keys],
                 lambda: [_sds(landed[k].shape, landed[k].dtype) for k in keys],
                 {g: g for g in range(n)}, 3 * n, make, finish)


def _pair_swap_task(names, big, got):
    n = len(names)

    def make(ins, outs, send_sem, recv_sem):
        x, y, c, _ = _mesh_position()
        copies = [functools.partial(_remote, _rows_half(ins[a], 2, 1 - c), outs[a], send_sem(a), recv_sem(a),
                                    (x, y, 1 - c)) for a in range(n)]
        return copies, copies

    def shapes():
        return [_sds(big[k].shape[:2] + (big[k].shape[2] // 2, big[k].shape[3]), big[k].dtype) for k in names]

    return _Task("sibling", lambda: [big[k] for k in names], shapes, {}, n, make,
                 lambda res: got.update(zip(names, res)))


def _rs_pair_sum(name, fulls, gots, core):
    n = len(fulls)
    shapes = [(f.shape[2] // 2, f.shape[3]) for f in fulls]

    def body(core_ref, *refs):
        for a_ref, b_ref, o_ref in zip(refs[:n], refs[n:2 * n], refs[2 * n:]):
            o_ref[...] = (a_ref[...].astype(F32) + b_ref[...].astype(F32)).astype(BF16)

    mine = [pl.BlockSpec((None, None) + hc, lambda s, core_ref: (0, s, core_ref[0], 0)) for hc in shapes]
    slot = [pl.BlockSpec((None, None) + hc, lambda s, core_ref: (0, s, 0, 0)) for hc in shapes]
    return _pcall(
        body, name=name, grid=(N_CHIPS,), num_prefetch=1,
        in_specs=mine + slot, out_specs=slot,
        out_shape=[_sds((1, N_CHIPS) + hc, BF16) for hc in shapes],
    )(core, *fulls, *gots)


def _chip_exchange_task(names, pair_sums, by_source, part=0, nparts=1):
    n = len(names)

    def rows(ref):
        h = ref.shape[1] // nparts
        return ref.at[:, pl.ds(part * h, h), :]

    def make(ins, outs, send_sem, recv_sem):
        x, y, c, chips = _mesh_position()
        s_me = _chip_index(x, y)
        starts, arrivals = [], []
        for a in range(n):
            for k, chip in enumerate(chips):
                s_k = _chip_index(*chip)
                starts.append(functools.partial(_remote, rows(ins[a].at[:, s_k]), rows(outs[a].at[:, s_me]),
                                                send_sem(3 * a + k), recv_sem(3 * a + k), (*chip, c)))
                got = rows(outs[a].at[:, s_k])
                arrivals.append(functools.partial(_remote, got, got, send_sem(3 * a + k), recv_sem(3 * a + k),
                                                  (*chip, c)))
        return starts, arrivals

    def operands():
        return [pair_sums[k] for k in names] + ([by_source[k] for k in names] if part else [])

    return _Task("chips", operands, lambda: [_sds(pair_sums[k].shape, pair_sums[k].dtype) for k in names],
                 {n + a: a for a in range(n)} if part else {}, 3 * n, make,
                 lambda res: by_source.update(zip(names, res)))


def _rs_chip_sum(name, owns, parts, chip):
    n = len(owns)
    ns = N_CHIPS
    shapes = [p.shape[2:] for p in parts]

    def body(chip_ref, *refs):
        me = chip_ref[0]
        for i in range(n):
            own_v = refs[i][...].astype(F32)
            slots = refs[n + ns * i:n + ns * (i + 1)]
            tot = None
            for s in range(ns):
                term = jnp.where(me == s, own_v, slots[s][...].astype(F32))
                tot = term if tot is None else tot + term
            refs[n + ns * n + i][...] = tot

    def slot_spec(hc, s):
        return pl.BlockSpec((None, None) + hc,
                            lambda g, chip_ref: (0, jnp.where(chip_ref[0] == s, (s + 1) % ns, s), 0, 0))

    own_specs = [pl.BlockSpec((None, None) + hc, lambda g, chip_ref: (0, chip_ref[0], 0, 0)) for hc in shapes]
    slot_specs = [slot_spec(hc, s) for hc in shapes for s in range(ns)]
    return _pcall(
        body, name=name, grid=(1,), num_prefetch=1,
        in_specs=own_specs + slot_specs,
        out_specs=[pl.BlockSpec((None,) + hc, lambda g, chip_ref: (0, 0, 0)) for hc in shapes],
        out_shape=[_sds((1,) + hc, F32) for hc in shapes],
    )(chip, *owns, *[p for p in parts for _ in range(ns)])


def _pair_gather_task(names, halves, sibling_halves):
    n = len(names)

    def make(ins, outs, send_sem, recv_sem):
        x, y, c, _ = _mesh_position()
        copies = [functools.partial(_remote, ins[a], outs[a], send_sem(a), recv_sem(a), (x, y, 1 - c))
                  for a in range(n)]
        return copies, copies

    return _Task("sibling", lambda: [halves[k] for k in names], lambda: [_sds(halves[k].shape, F32) for k in names],
                 {}, n, make, lambda res: sibling_halves.update(zip(names, res)))


def _small_allreduce(arrs):
    n = len(arrs)
    per = 1 + 2 * N_PEER_CHIPS

    def body(*refs):
        v_refs, o_refs = refs[:n], refs[n:2 * n]
        sib, pair, part = refs[2 * n:3 * n], refs[3 * n:4 * n], refs[4 * n:5 * n]
        send_sems, recv_sems = refs[5 * n:]
        x, y, c, chips = _mesh_position()
        s_me = _chip_index(x, y)

        def quarter(ref, s):
            q = ref.shape[0] // N_CHIPS
            return ref.at[pl.ds(pl.multiple_of(s * q, 8), q)]

        def exchange(first_sem, src, dst_of, arrival_of):
            sems = lambda a, k: (send_sems.at[a * per + first_sem + k], recv_sems.at[a * per + first_sem + k])
            sends = [_remote(src(a, _chip_index(*chip)), dst_of(a, s_me), *sems(a, k), (*chip, c))
                     for a in range(n) for k, chip in enumerate(chips)]
            for cp in sends:
                cp.start()
            for a in range(n):
                for k, chip in enumerate(chips):
                    got = arrival_of(a, _chip_index(*chip))
                    _remote(got, got, *sems(a, k), (*chip, c)).wait_recv()
            for cp in sends:
                cp.wait_send()

        swaps = [_remote(v_refs[a], sib[a], send_sems.at[a * per], recv_sems.at[a * per], (x, y, 1 - c))
                 for a in range(n)]
        for cp in swaps:
            cp.start()
        for cp in swaps:
            cp.wait()
        for a in range(n):
            pair[a][...] = v_refs[a][...] + sib[a][...]
        exchange(1, lambda a, s_k: quarter(pair[a], s_k), lambda a, s: part[a].at[s], lambda a, s_k: part[a].at[s_k])
        for a in range(n):
            part[a][s_me] = quarter(pair[a], s_me)[...]
            q = o_refs[a].shape[0] // N_CHIPS
            o_refs[a][pl.ds(pl.multiple_of(s_me * q, 8), q), :] = (
                ((part[a][0] + part[a][1]) + part[a][2]) + part[a][3])
        exchange(1 + N_PEER_CHIPS, lambda a, s_k: quarter(o_refs[a], s_me), lambda a, s: quarter(o_refs[a], s),
                 lambda a, s_k: quarter(o_refs[a], s_k))

    shapes = [a.shape for a in arrs]
    return _pcall(
        body, name="small_allreduce", grid=(1,), own_peers=("sibling", "chips"),
        in_specs=[VMEM_SPEC] * n, out_specs=[VMEM_SPEC] * n, out_shape=[_sds(s, F32) for s in shapes],
        scratch_shapes=([pltpu.VMEM(s, F32) for s in shapes] * 2
                        + [pltpu.VMEM((N_CHIPS, s[0] // N_CHIPS, s[1]), F32) for s in shapes]
                        + [pltpu.SemaphoreType.DMA((n * per,)), pltpu.SemaphoreType.DMA((n * per,))]),
    )(*arrs)


TRANSPOSED_WEIGHTS = ("ffn1_w1", "ffn1_w3", "ffn2_w1", "ffn2_w3")
SMALL_LAYOUT = [("ffn1_norm", 1), ("mix_norm", 1), ("ret_gn", 1), ("conv_b", 1), ("b_rgate", 1), ("b_igate", 1),
                ("lru_lambda", 1), ("xattn_norm", 1), ("mem_norm", 1), ("ffn2_norm", 1), ("final_norm", 1),
                ("b_branch_gate", 2), ("conv_w", CONV_TAPS)]
SMALL_ROWS = 32
GATE_WEIGHTS = ("w_rgate", "w_igate")
WEIGHT_ORDER = ["ffn1_norm", "ffn1_w1", "ffn1_w3", "ffn1_w2", "mix_norm", "w_in", "ret_gn", "w_ret_o", "conv_w",
                "conv_b", "w_rgate", "b_rgate", "w_igate", "b_igate", "lru_lambda", "w_lru_o", "w_branch_gate",
                "b_branch_gate", "w_out", "xattn_norm", "mem_norm", "w_xq", "w_xk", "w_xv", "w_xo", "ffn2_norm",
                "ffn2_w1", "ffn2_w3", "ffn2_w2", "final_norm"]


SMALL_USED_ROWS = sum(n for _, n in SMALL_LAYOUT)


def _pack_small(parts, extra_row=None):
    rows = [parts[name].reshape(n, D) for name, n in SMALL_LAYOUT]
    if extra_row is not None:
        rows.append(extra_row)
    rows.append(jnp.zeros((SMALL_ROWS - sum(r.shape[0] for r in rows), D), F32))
    return jnp.concatenate(rows, axis=0)


def _unpack_small(packed, shapes):
    out, r = {}, 0
    for name, n in SMALL_LAYOUT:
        out[name] = packed[r:r + n].reshape(shapes[name])
        r += n
    return out


def kernel(x, mem, ffn1_norm, ffn1_w1, ffn1_w3, ffn1_w2, mix_norm, w_in, ret_gn, w_ret_o, conv_w, conv_b, w_rgate, b_rgate, w_igate, b_igate, lru_lambda, w_lru_o, w_branch_gate, b_branch_gate, w_out, xattn_norm, mem_norm, w_xq, w_xk, w_xv, w_xo, ffn2_norm, ffn2_w1, ffn2_w3, ffn2_w2, final_norm, loss_target, m_ffn1_norm, m_ffn1_w1, m_ffn1_w3, m_ffn1_w2, m_mix_norm, m_w_in, m_ret_gn, m_w_ret_o, m_conv_w, m_conv_b, m_w_rgate, m_b_rgate, m_w_igate, m_b_igate, m_lru_lambda, m_w_lru_o, m_w_branch_gate, m_b_branch_gate, m_w_out, m_xattn_norm, m_mem_norm, m_w_xq, m_w_xk, m_w_xv, m_w_xo, m_ffn2_norm, m_ffn2_w1, m_ffn2_w3, m_ffn2_w2, m_final_norm, v_ffn1_norm, v_ffn1_w1, v_ffn1_w3, v_ffn1_w2, v_mix_norm, v_w_in, v_ret_gn, v_w_ret_o, v_conv_w, v_conv_b, v_w_rgate, v_b_rgate, v_w_igate, v_b_igate, v_lru_lambda, v_w_lru_o, v_w_branch_gate, v_b_branch_gate, v_w_out, v_xattn_norm, v_mem_norm, v_w_xq, v_w_xk, v_w_xv, v_w_xo, v_ffn2_norm, v_ffn2_w1, v_ffn2_w3, v_ffn2_w2, v_final_norm):
    given = dict(locals())
    w = {n: given[n] for n in WEIGHT_ORDER}
    mom = {n: given["m_" + n] for n in WEIGHT_ORDER}
    var = {n: given["v_" + n] for n in WEIGHT_ORDER}
    chip = _chip_index(lax.axis_index("x"), lax.axis_index("y"))
    core = lax.axis_index("c").astype(jnp.int32).reshape(1)

    chip_id = chip.astype(jnp.int32).reshape(1)
    sm = {n: w[n] for n in ["ffn1_norm", "mix_norm", "ret_gn", "conv_b", "b_rgate", "b_igate", "lru_lambda",
                            "xattn_norm", "mem_norm", "ffn2_norm", "b_branch_gate"]}
    sm["final_norm"] = w["final_norm"].reshape(1, D)
    sm["w_rgate"] = w["w_rgate"][0]
    sm["w_igate"] = w["w_igate"][0]

    local = lambda a, n: jnp.swapaxes(a[0], 0, 1) if n in TRANSPOSED_WEIGHTS else a[0]
    stack = lambda names: jnp.stack([local(w[n], n) for n in names], axis=0).astype(BF16)
    shard = {"col1": stack(["ffn1_w1", "ffn1_w3"]), "row2a": stack(["ffn1_w2"]),
             "win": jnp.swapaxes(w["w_in"], 1, 2).astype(BF16),
             "wbg": jnp.swapaxes(w["w_branch_gate"], 1, 2).astype(BF16),
             "sqA": stack(["w_ret_o", "w_lru_o", "w_out"]), "sqB": stack(["w_xq", "w_xk"]),
             "sqC": stack(["w_xv", "w_xo"]), "col2a": stack(["ffn2_w1"]), "col2b": stack(["ffn2_w3"]),
             "row2b": stack(["ffn2_w2"]), "conv": w["conv_w"]}
    gw, landed = {}, {}
    over_chips = lambda keys: _gather_chips_task({k: shard[k] for k in keys}, True, landed)
    to_sibling = lambda keys: _gather_sibling_task(keys, landed, gw)

    big, got, pair_sums, by_source, halves, sibling_halves, outs = {}, {}, {}, {}, {}, {}, {}
    pair_swap = lambda names: _pair_swap_task(names, big, got)
    exchange = lambda names, part=0, nparts=1: _chip_exchange_task(names, pair_sums, by_source, part, nparts)
    pair_gather = lambda names: _pair_gather_task(names, halves, sibling_halves)

    def pair_sum(names):
        res = _rs_pair_sum("rs_pair_sum_" + names[0], [big[n] for n in names], [got[n] for n in names], core)
        pair_sums.update(zip(names, res))

    def chip_sum(names):
        res = _rs_chip_sum("rs_chip_sum_" + names[0], [pair_sums[n] for n in names], [by_source[n] for n in names],
                           chip_id)
        halves.update(zip(names, res))

    def adamw(names):
        for n in names:
            res = _adamw_halves("adamw_" + n, local(w[n], n), halves[n], sibling_halves[n], 0, local(mom[n], n),
                                local(var[n], n), core)
            outs[n] = tuple((jnp.swapaxes(r, 0, 1) if n in TRANSPOSED_WEIGHTS else r)[None] for r in res)

    do = lambda fn, names: functools.partial(fn, names)
    ffn2_grads = ["ffn2_w2", "ffn2_w1", "ffn2_w3"]
    xattn_grads = ["w_xo", "w_xq", "w_xk", "w_xv"]
    mix_out_grads = ["w_branch_gate", "w_out", "w_ret_o", "w_lru_o"]
    conv_gather = _gather_chips_task({"conv": shard["conv"]}, False, gw)
    swap = lambda key: _gather_chips_task({key: shard[key]}, True, landed, legs="swap")
    pass_on = lambda key: _gather_chips_task({key: shard[key]}, True, landed, legs="pass_on")
    plan = _Plan()
    plan.tasks = {
        "ag_first_chips": [over_chips(["col1"]), swap("row2a")],
        "ag_first_sibling": [to_sibling(["col1"]), pass_on("row2a"), swap("win")],
        "ffn1_up": [to_sibling(["row2a"]), pass_on("win"), swap("wbg")],
        "ffn1_down": [to_sibling(["win"]), pass_on("wbg"), swap("sqA"), conv_gather],
        "mix_in": [to_sibling(["wbg"]), pass_on("sqA"), swap("col2a")],
        "ret_fwd": [to_sibling(["sqA"]), pass_on("col2a"), swap("sqB")],
        "lru_gates_fwd": [to_sibling(["col2a"]), pass_on("sqB"), swap("sqC")],
        "lru_scan_fwd": [to_sibling(["sqB"]), pass_on("sqC"), swap("col2b")],
        "mix_gates": [to_sibling(["sqC"]), pass_on("col2b"), swap("row2b")],
        "y_lru": [to_sibling(["col2b"]), pass_on("row2b")],
        "xattn_fwd": [to_sibling(["row2b"])],
        "ffn2_dh": [pair_swap(ffn2_grads)],
        "xattn_bwd": [exchange(["ffn2_w2"], 0, 2)],
        "d_hq": [exchange(["ffn2_w2"], 1, 2)],
        "d_merged": [exchange(["ffn2_w1"], 0, 2), pair_swap(xattn_grads)],
        "lru_out_bwd": [exchange(["w_xo"])],
        "ret_bwd": [exchange(["ffn2_w1"], 1, 2), exchange(["ffn2_w3"], 0, 2), pair_swap(mix_out_grads)],
        "lru_scan_bwd": [exchange(["ffn2_w3"], 1, 2)],
        "lru_gates_bwd": [exchange(["w_xq", "w_xk"]), pair_gather(ffn2_grads)],
        "dw_in": [exchange(["w_xv", "w_out"])],
        "d_h2": [exchange(["w_branch_gate", "w_ret_o", "w_lru_o"]), pair_swap(["w_in"]), pair_gather(xattn_grads)],
        "ffn1_bwd_mid": [exchange(["w_in"], 0, 2), pair_gather(mix_out_grads)],
        "ffn1_dw2": [exchange(["w_in"], 2, 4)],
        "ffn1_dw1": [exchange(["w_in"], 3, 4), pair_swap(["ffn1_w2"])],
        "ffn1_dw3": [exchange(["ffn1_w2"], 0, 2), pair_swap(["ffn1_w1"]), pair_gather(["w_in"])],
        "ffn1_dh": [exchange(["ffn1_w2"], 1, 2), exchange(["ffn1_w1"]), pair_swap(["ffn1_w3"])],
        "small_allreduce": [exchange(["ffn1_w3"]), pair_gather(["ffn1_w2"])],
        "rs_last_gather": [pair_gather(["ffn1_w1", "ffn1_w3"])],
    }
    plan.after = {
        "ffn2_dh": [do(pair_sum, ffn2_grads)],
        "d_merged": [do(pair_sum, xattn_grads)],
        "ret_bwd": [do(pair_sum, mix_out_grads)],
        "lru_scan_bwd": [do(chip_sum, ffn2_grads)],
        "lru_gates_bwd": [do(adamw, ffn2_grads)],
        "dw_in": [do(chip_sum, xattn_grads)],
        "d_h2": [do(chip_sum, mix_out_grads), do(pair_sum, ["w_in"]), do(adamw, xattn_grads)],
        "ffn1_bwd_mid": [do(adamw, mix_out_grads)],
        "ffn1_dw1": [do(chip_sum, ["w_in"]), do(pair_sum, ["ffn1_w2"])],
        "ffn1_dw3": [do(pair_sum, ["ffn1_w1"]), do(adamw, ["w_in"])],
        "ffn1_dh": [do(pair_sum, ["ffn1_w3"]), do(chip_sum, ["ffn1_w2"])],
        "small_allreduce": [do(chip_sum, ["ffn1_w1", "ffn1_w3"]), functools.partial(_comm_call, "rs_last_gather"),
                    do(adamw, ["ffn1_w2", "ffn1_w1", "ffn1_w3"])],
    }
    global _plan
    _plan = plan
    try:
        _comm_call("ag_first_chips")
        _comm_call("ag_first_sibling")
        loss_part, grad_x, small = _local_step(x[0], mem[0], loss_target[0], gw, sm, big)
        gate2d = lambda a: a.reshape(LRU_BLOCKS * LRU_BLOCK, LRU_BLOCK)
        loss_row = jnp.pad(loss_part, ((0, 0), (0, D - loss_part.shape[1])))
        small_sum, *gate_sums = _small_allreduce([_pack_small(small, loss_row)]
                                                 + [gate2d(small[n]) for n in GATE_WEIGHTS])
    finally:
        _plan = None
    assert not plan.tasks and not plan.after, (list(plan.tasks), list(plan.after))
    loss = small_sum[SMALL_USED_ROWS, 0]

    small_shapes = {n: w[n].shape for n, _ in SMALL_LAYOUT}
    small_shapes["conv_w"] = (CONV_TAPS, D)
    conv_row = SMALL_USED_ROWS - CONV_TAPS
    conv_grad = lax.dynamic_slice(small_sum[conv_row:conv_row + CONV_TAPS], (0, chip * SQ_BLK), (CONV_TAPS, SQ_BLK))
    small_w = {n: w[n] for n, _ in SMALL_LAYOUT}
    small_m = {n: mom[n] for n, _ in SMALL_LAYOUT}
    small_v = {n: var[n] for n, _ in SMALL_LAYOUT}
    pad_cols = lambda a: jnp.pad(a[0], ((0, 0), (0, D - SQ_BLK)))
    for dct in (small_w, small_m, small_v):
        dct["conv_w"] = pad_cols(dct["conv_w"])
    g_pack = lax.dynamic_update_slice(small_sum, jnp.pad(conv_grad, ((0, 0), (0, D - SQ_BLK))), (conv_row, 0))
    d_pack, m_pack, v_pack = _adamw("adamw_small", _pack_small(small_w), g_pack, _pack_small(small_m),
                                    _pack_small(small_v))
    unpacked = [_unpack_small(p, small_shapes) for p in (g_pack, d_pack, m_pack, v_pack)]
    for n, _ in SMALL_LAYOUT:
        if n == "conv_w":
            outs[n] = tuple(u[n][:, :SQ_BLK][None] for u in unpacked)
        else:
            outs[n] = tuple(u[n] for u in unpacked)
    for n, gsum in zip(GATE_WEIGHTS, gate_sums):
        d, nm, nv = _adamw("adamw_" + n, gate2d(w[n]), gsum, gate2d(mom[n]), gate2d(var[n]))
        outs[n] = tuple(r.reshape(w[n].shape) for r in (gsum, d, nm, nv))

    result = [loss, grad_x[None]]
    for k in range(4):
        result += [outs[n][k] for n in WEIGHT_ORDER]
    return tuple(result)
```

```python
import functools
import math

import jax
import jax.numpy as jnp
from jax import lax
from jax.experimental import pallas as pl
from jax.experimental.pallas import tpu as pltpu

F32 = jnp.float32
BF16 = jnp.bfloat16
GRAD_WIRE_DTYPE = BF16
MESH = pl.DeviceIdType.MESH

D = 1024
EPS = 1e-6
RET_HEADS = 4
RET_DK = 128
RET_DV = 256
CHUNK = 128
ROPE_BASE = 10000.0
LRU_BLOCKS = 8
LRU_BLOCK = 128
CONV_TAPS = 4
LRU_C = 8.0
D_FF = 2816
X_HEADS = 4
X_HD = 256
N_CHIPS = 4
FF_BLK = D_FF // N_CHIPS
IN_BLK = 5120 // N_CHIPS
BG_BLK = 2048 // N_CHIPS
SQ_BLK = D // N_CHIPS

ADAM_LR = 0.001
ADAM_B1 = 0.9
ADAM_B2 = 0.999
ADAM_EPS = 1e-08
ADAM_WD = 0.01
ADAM_STEP = 10

F32_TILE_ROWS = 8
BF16_TILE_ROWS = 16
VMEM_LIMIT_BYTES = 56 * 1024 * 1024
ROW_TILE = 512
WIDE_ROW_TILE = 1024
FFN_ROW_TILE = 256
DW_BLK = D_FF // 2
SCAN_TILE = 256
RET_STEP_CHUNKS = 2
RET_STEP_ROWS = RET_STEP_CHUNKS * CHUNK

_DN = {
    "nn": (((1,), (0,)), ((), ())),
    "nt": (((1,), (1,)), ((), ())),
    "tn": (((0,), (0,)), ((), ())),
}


def _cparams(n_axes, collective_id=None):
    return pltpu.CompilerParams(dimension_semantics=("arbitrary",) * n_axes,
                                vmem_limit_bytes=VMEM_LIMIT_BYTES, collective_id=collective_id)


def _dot(a, b, kind):
    if b.ndim == 3:
        b = b.reshape(b.shape[0] * b.shape[1], b.shape[2])
    return lax.dot_general(a.astype(BF16), b.astype(BF16), _DN[kind], preferred_element_type=F32)


def _sigmoid(x):
    return 1.0 / (1.0 + jnp.exp(-x))


def _log1p_pos(e):
    u = 1.0 + e
    return jnp.where(u == 1.0, e, jnp.log(u) * (e / jnp.where(u == 1.0, 1.0, u - 1.0)))


def _expm1(x):
    u = jnp.exp(x)
    lu = jnp.log(u)
    safe = jnp.where(lu == 0.0, 1.0, lu)
    return jnp.where(u == 1.0, x, (u - 1.0) * (x / safe))


def _softplus(z):
    return jnp.maximum(z, 0.0) + _log1p_pos(jnp.exp(-jnp.abs(z)))


_GELU_C = math.sqrt(2.0 / math.pi)


def _gelu_and_grad(x):
    x2 = x * x
    t = jnp.tanh(_GELU_C * (x + 0.044715 * x * x2))
    g = 0.5 * x * (1.0 + t)
    dg = 0.5 * (1.0 + t) + 0.5 * x * (1.0 - t * t) * (_GELU_C * (1.0 + 3.0 * 0.044715 * x2))
    return g, dg


def _rms_fwd(x, g):
    r = lax.rsqrt(jnp.mean(x * x, axis=-1, keepdims=True) + EPS)
    return (x * r) * g


def _rms_bwd(x, g, dh):
    r = lax.rsqrt(jnp.mean(x * x, axis=-1, keepdims=True) + EPS)
    n = x * r
    dyg = dh * g
    dx = r * (dyg - n * jnp.mean(dyg * n, axis=-1, keepdims=True))
    return dx, jnp.sum(dh * n, axis=0, keepdims=True)


def _accumulate(ref, val, first):
    @pl.when(first)
    def _():
        ref[...] = val

    @pl.when(jnp.logical_not(first))
    def _():
        ref[...] += val


def _sds(shape, dtype):
    return jax.ShapeDtypeStruct(tuple(shape), dtype)


def _spec(shape, fn):
    return pl.BlockSpec(tuple(shape), fn)


class _Task:
    def __init__(self, peers, operands, out_shapes, aliases, nsem, make, finish, make_second=None):
        self.peers = peers
        self.operands, self.out_shapes, self.aliases = operands, out_shapes, aliases
        self.nsem, self.make, self.finish = nsem, make, finish
        self.make_second = make_second


class _Plan:
    def __init__(self):
        self.tasks, self.after = {}, {}


_plan = None


_CHIP_PEER_SETS = [frozenset({"chips"}), frozenset({"first"}), frozenset({"second"}), frozenset({"first", "second"})]
PEER_SET_COLLECTIVE_ID = {frozenset({"sibling"}): 1}
for _i, _chip_peers in enumerate(_CHIP_PEER_SETS):
    PEER_SET_COLLECTIVE_ID[_chip_peers] = 2 + 2 * _i
    PEER_SET_COLLECTIVE_ID[_chip_peers | {"sibling"}] = 3 + 2 * _i


def _peer_set(names):
    names = frozenset(n for name in names for n in name.split("+"))
    return names - {"first", "second"} if "chips" in names else names


def _axis_neighbours(x, y, c):
    flip = lambda v, f: v + f * (1 - 2 * v)
    return (flip(x, 1 - c), flip(y, c)), (flip(x, c), flip(y, 1 - c))


def _entry_handshake(peer_set):
    x, y, c, chips = _mesh_position()
    first, second = _axis_neighbours(x, y, c)
    peers = [(x, y, 1 - c)] if "sibling" in peer_set else []
    if "chips" in peer_set:
        peers += [(*chip, c) for chip in chips]
    if "first" in peer_set:
        peers.append((*first, c))
    if "second" in peer_set:
        peers.append((*second, c))
    barrier = pltpu.get_barrier_semaphore()
    for peer in peers:
        pl.semaphore_signal(barrier, inc=1, device_id=peer, device_id_type=MESH)
    pl.semaphore_wait(barrier, len(peers))


def _pcall(body, *, name, grid, in_specs, out_specs, out_shape, scratch_shapes=(), num_prefetch=0, own_peers=()):
    single = not isinstance(out_shape, (list, tuple))
    out_shape = [out_shape] if single else list(out_shape)
    out_specs = [out_specs] if single else list(out_specs)
    in_specs = list(in_specs)
    scratch_shapes = list(scratch_shapes)
    tasks = _plan.tasks.pop(name, []) if _plan is not None else []
    after = _plan.after.pop(name, []) if _plan is not None else []
    peer_set = _peer_set([t.peers for t in tasks] + list(own_peers))
    nax = len(grid)

    def run(*operands):
        n_in = len(operands) - num_prefetch
        n_out = len(out_shape)
        t_ops = [t.operands() for t in tasks]
        t_outs = [t.out_shapes() for t in tasks]
        c_ops = [a for ops in t_ops for a in ops]
        c_outs = [s for outs in t_outs for s in outs]
        aliases = {}
        i0, o0 = num_prefetch + n_in, n_out
        for t, ops, outs in zip(tasks, t_ops, t_outs):
            for i_loc, o_loc in t.aliases.items():
                aliases[i0 + i_loc] = o0 + o_loc
            i0 += len(ops)
            o0 += len(outs)
        nsem = sum(t.nsem for t in tasks)

        def wrapped(*refs):
            p = num_prefetch
            pre, ins = refs[:p], refs[p:p + n_in]
            cins = refs[p + n_in:p + n_in + len(c_ops)]
            q = p + n_in + len(c_ops)
            outs, couts = refs[q:q + n_out], refs[q + n_out:q + n_out + len(c_outs)]
            q += n_out + len(c_outs)
            scr = refs[q:q + len(scratch_shapes)]

            def rounds(second):
                send_sems, recv_sems = refs[q + len(scratch_shapes):]
                out = []
                ci = co = so = 0
                for t, ops, souts in zip(tasks, t_ops, t_outs):
                    make = t.make_second if second else t.make
                    out.append(([], []) if make is None else
                               make(cins[ci:ci + len(ops)], couts[co:co + len(souts)],
                                    functools.partial(lambda base, k: send_sems.at[base + k], so),
                                    functools.partial(lambda base, k: recv_sems.at[base + k], so)))
                    ci, co, so = ci + len(ops), co + len(souts), so + t.nsem
                return out

            two_rounds = [t.make_second is not None for t in tasks]
            if peer_set:
                ids = [pl.program_id(k) for k in range(nax)]
                first = functools.reduce(jnp.logical_and, [i == 0 for i in ids])
                last = functools.reduce(jnp.logical_and, [i == g - 1 for i, g in zip(ids, grid)])
                step = functools.reduce(lambda acc, ig: acc * ig[1] + ig[0], zip(ids, grid), 0)
                middle = step == math.prod(grid) // 3

                @pl.when(first)
                def _():
                    _entry_handshake(peer_set)
                    for starts, _ in rounds(False):
                        for copy in starts:
                            copy().start()

            body(*pre, *ins, *outs, *scr)

            if any(two_rounds):
                @pl.when(middle)
                def _():
                    for (_, arrivals), two in zip(rounds(False), two_rounds):
                        if two:
                            for arrival in arrivals:
                                arrival().wait_recv()
                    for starts, _ in rounds(True):
                        for copy in starts:
                            copy().start()

            if tasks:
                @pl.when(last)
                def _():
                    first_round, second_round = rounds(False), rounds(True)
                    for (_, arrivals1), (_, arrivals2), two in zip(first_round, second_round, two_rounds):
                        for arrival in (arrivals2 if two else arrivals1):
                            arrival().wait_recv()
                    for starts, _ in first_round + second_round:
                        for copy in starts:
                            copy().wait_send()

        sems = [pltpu.SemaphoreType.DMA((nsem,)), pltpu.SemaphoreType.DMA((nsem,))] if tasks else []
        res = pl.pallas_call(
            wrapped, name=name,
            grid_spec=pltpu.PrefetchScalarGridSpec(
                num_scalar_prefetch=num_prefetch, grid=tuple(grid),
                in_specs=in_specs + [ANY_SPEC] * len(c_ops),
                out_specs=out_specs + [ANY_SPEC] * len(c_outs),
                scratch_shapes=scratch_shapes + sems),
            out_shape=out_shape + c_outs,
            input_output_aliases=aliases,
            compiler_params=_cparams(nax, PEER_SET_COLLECTIVE_ID[peer_set] if peer_set else None),
        )(*operands, *c_ops)
        co = n_out
        for t, souts in zip(tasks, t_outs):
            t.finish(res[co:co + len(souts)])
            co += len(souts)
        for fn in after:
            fn()
        return res[0] if single else list(res[:n_out])

    return run


def _comm_call(name):
    def body(o_ref):
        o_ref[...] = jnp.zeros_like(o_ref)

    _pcall(body, name=name, grid=(1,), in_specs=[], out_specs=_spec((8, 128), lambda i: (0, 0)),
           out_shape=_sds((8, 128), F32))()


def _gemm(name, terms, grid, outs, acc_shape, extras=(), epilogue=None):
    kinds = [t[4] for t in terms]
    nt, ne, no = len(terms), len(extras), len(outs)
    nred = grid[-1]
    nax = len(grid)

    def body(*refs):
        trefs = refs[:2 * nt]
        erefs = refs[2 * nt:2 * nt + ne]
        orefs = refs[2 * nt + ne:2 * nt + ne + no]
        ids = [pl.program_id(k) for k in range(nax)]
        tot = None
        for t in range(nt):
            d = _dot(trefs[2 * t][...], trefs[2 * t + 1][...], kinds[t])
            tot = d if tot is None else tot + d

        def finish(acc):
            if epilogue is None:
                orefs[0][...] = acc.astype(orefs[0].dtype)
            else:
                epilogue(acc, erefs, orefs, ids)

        if nred == 1:
            finish(tot)
        else:
            acc_ref = refs[-1]
            r = ids[-1]

            @pl.when(r == 0)
            def _():
                acc_ref[...] = tot

            @pl.when(r > 0)
            def _():
                acc_ref[...] += tot

            @pl.when(r == nred - 1)
            def _():
                finish(acc_ref[...])

    operands, in_specs = [], []
    for a, a_spec, b, b_spec, _ in terms:
        operands += [a, b]
        in_specs += [a_spec, b_spec]
    for e, e_spec in extras:
        operands.append(e)
        in_specs.append(e_spec)
    scratch = [pltpu.VMEM(tuple(acc_shape), F32)] if nred > 1 else []
    return _pcall(body, name=name, grid=tuple(grid), in_specs=in_specs, out_specs=[o[1] for o in outs],
                  out_shape=[o[0] for o in outs], scratch_shapes=scratch)(*operands)


def _rowwise(name, fn, ins, outs, grid):
    ni = len(ins)
    nax = len(grid)

    def body(*refs):
        ids = [pl.program_id(k) for k in range(nax)]
        fn(refs[:ni], refs[ni:], ids)

    return _pcall(body, name=name, grid=tuple(grid), in_specs=[i[1] for i in ins],
                  out_specs=[o[1] for o in outs], out_shape=[o[0] for o in outs])(*[i[0] for i in ins])


def _ffn_up(name, h, w1buf, w1_idx, w3buf, w3_idx, norm_gain=None):
    T = h.shape[0]
    tm = min(FFN_ROW_TILE, T)
    normed = norm_gain is not None

    def body(h_ref, *refs):
        if normed:
            g_ref, w1_ref, w3_ref, a_ref, b_ref, s_ref, hn_ref = refs
            hv = _rms_fwd(h_ref[...], g_ref[...]).astype(BF16)
            hn_ref[...] = hv
        else:
            w1_ref, w3_ref, a_ref, b_ref, s_ref = refs
            hv = h_ref[...]
        a = _dot(hv, w1_ref[...], "nt")
        b = _dot(hv, w3_ref[...], "nt")
        a_ref[...] = a.astype(BF16)
        b_ref[...] = b.astype(BF16)
        s_ref[...] = ((a * _sigmoid(a)) * b).astype(BF16)

    row = _spec((tm, D), lambda i: (i, 0))
    blk = _spec((tm, D_FF), lambda i: (i, 0))
    return _pcall(
        body, name=name, grid=(T // tm,),
        in_specs=[row] + ([_spec((1, D), lambda i: (0, 0))] if normed else [])
        + [_spec((N_CHIPS, None, FF_BLK, D), lambda i: (0, w1_idx, 0, 0)),
           _spec((N_CHIPS, None, FF_BLK, D), lambda i: (0, w3_idx, 0, 0))],
        out_specs=[blk, blk, blk] + ([row] if normed else []),
        out_shape=[_sds((T, D_FF), BF16)] * 3 + ([_sds((T, D), BF16)] if normed else []),
    )(h, *([norm_gain] if normed else []), w1buf, w3buf)


def _loss_head(x, g, tgt, loss_ref, dx_ref, dg_ref, first):
    err = _rms_fwd(x, g) - tgt
    lp = 0.5 * jnp.sum(jnp.mean(err * err, axis=-1, keepdims=True), axis=0, keepdims=True)
    _accumulate(loss_ref, jnp.broadcast_to(lp, (1, 128)), first)
    dx, dgp = _rms_bwd(x, g, err * (1.0 / D))
    dx_ref[...] = dx
    _accumulate(dg_ref, dgp, first)


def _ffn_down(name, s, wrow2, w2_idx, x_res, g_next=None, loss_target=None):
    T = x_res.shape[0]
    tm = min(ROW_TILE, T)
    row = lambda i, j, r: (i, 0)
    vec = lambda i, j, r: (0, 0)

    def epilogue(acc, erefs, orefs, ids):
        xo = erefs[0][...] + 0.5 * acc
        if loss_target is not None:
            _loss_head(xo, erefs[1][...], erefs[2][...], orefs[0], orefs[1], orefs[2], ids[0] == 0)
            return
        orefs[0][...] = xo
        orefs[1][...] = _rms_fwd(xo, erefs[1][...]).astype(BF16)

    extras = [(x_res, _spec((tm, D), row)), (g_next, _spec((1, D), vec))]
    if loss_target is None:
        outs = [(_sds((T, D), F32), _spec((tm, D), row)), (_sds((T, D), BF16), _spec((tm, D), row))]
    else:
        extras.append((loss_target, _spec((tm, D), row)))
        outs = [(_sds((1, 128), F32), _spec((1, 128), vec)), (_sds((T, D), F32), _spec((tm, D), row)),
                (_sds((1, D), F32), _spec((1, D), vec))]
    return _gemm(
        name,
        [(s, _spec((tm, D_FF), row),
          wrow2, _spec((N_CHIPS, None, FF_BLK, D), lambda i, j, r: (0, w2_idx, 0, 0)), "nn")],
        (T // tm, 1, 1), outs, (tm, D), extras, epilogue)


def _ffn_bwd_mid(name, dx, wrow2, w2_idx, a, b):
    T = dx.shape[0]
    tm = min(FFN_ROW_TILE, T)

    def body(dx_ref, w2_ref, a_ref, b_ref, dab_ref):
        ds = _dot(0.5 * dx_ref[...], w2_ref[...], "nt")
        av = a_ref[...].astype(F32)
        sg = _sigmoid(av)
        dab_ref[0] = (ds * b_ref[...].astype(F32) * (sg * (1.0 + av * (1.0 - sg)))).astype(BF16)
        dab_ref[1] = (ds * (av * sg)).astype(BF16)

    blk = _spec((tm, D_FF), lambda i: (i, 0))
    return _pcall(
        body, name=name, grid=(T // tm,),
        in_specs=[_spec((tm, D), lambda i: (i, 0)),
                  _spec((N_CHIPS, None, FF_BLK, D), lambda i: (0, w2_idx, 0, 0)),
                  blk, blk],
        out_specs=_spec((2, tm, D_FF), lambda i: (0, i, 0)),
        out_shape=_sds((2, T, D_FF), BF16),
    )(dx, wrow2, a, b)


def _rms_bwd_epilogue(acc, erefs, orefs, ids):
    dx, dgp = _rms_bwd(erefs[0][...], erefs[1][...], acc)
    orefs[0][...] = dx + erefs[2][...]
    _accumulate(orefs[1], dgp, ids[0] == 0)


def _rms_bwd_io(x, g, dres, T, tm):
    row = lambda i, j, r: (i, 0)
    vec = lambda i, j, r: (0, 0)
    extras = [(x, _spec((tm, D), row)), (g, _spec((1, D), vec)), (dres, _spec((tm, D), row))]
    outs = [(_sds((T, D), F32), _spec((tm, D), row)), (_sds((1, D), F32), _spec((1, D), vec))]
    return extras, outs


def _ffn_bwd(tag, dx_out, h, a, b, s, w1buf, w1_idx, w3buf, w3_idx, wrow2, w2_idx, x_in, g, big):
    T = dx_out.shape[0]
    dab = _ffn_bwd_mid(tag + "_bwd_mid", dx_out, wrow2, w2_idx, a, b)

    def half_scale(acc, erefs, orefs, ids):
        orefs[0][...] = (0.5 * acc).astype(orefs[0].dtype)

    dw_grid = (D_FF // DW_BLK, 1, 1)
    dw_out = [(_sds((D_FF, D), GRAD_WIRE_DTYPE), _spec((DW_BLK, D), lambda j, n, r: (j, 0)))]
    tokens = _spec((T, D), lambda j, n, r: (0, 0))
    big[tag + "_w2"] = _gemm(
        tag + "_dw2", [(s, _spec((T, DW_BLK), lambda j, n, r: (0, j)), dx_out, tokens, "tn")],
        dw_grid, dw_out, (DW_BLK, D), (), half_scale)[0].reshape(1, N_CHIPS, FF_BLK, D)
    for widx, wname in ((0, "_w1"), (1, "_w3")):
        big[tag + wname] = _gemm(
            tag + "_d" + wname[1:],
            [(dab, _spec((None, T, DW_BLK), functools.partial(lambda w, j, n, r: (w, 0, j), widx)), h, tokens, "tn")],
            dw_grid, dw_out, (DW_BLK, D))[0].reshape(1, N_CHIPS, FF_BLK, D)
    tm = min(FFN_ROW_TILE, T)
    extras, outs = _rms_bwd_io(x_in, g, dx_out, T, tm)
    whole = lambda idx: _spec((N_CHIPS, None, FF_BLK, D), lambda i, j, r: (0, idx, 0, 0))
    dx_in, dg = _gemm(
        tag + "_dh",
        [(dab, _spec((None, tm, D_FF), lambda i, j, r: (0, i, 0)), w1buf, whole(w1_idx), "nn"),
         (dab, _spec((None, tm, D_FF), lambda i, j, r: (1, i, 0)), w3buf, whole(w3_idx), "nn")],
        (T // tm, 1, 1), outs, (tm, D), extras, _rms_bwd_epilogue)
    return dx_in, dg


def _proj_sq(name, a, wsq, idx, kind, out_dtype=F32, extras=(), epilogue=None, outs=None):
    M = a.shape[0]
    tm = min(ROW_TILE, M)
    if outs is None:
        outs = [(_sds((M, D), out_dtype), _spec((tm, D), lambda i, j, r: (i, 0)))]
    return _gemm(
        name,
        [(a, _spec((tm, D), lambda i, j, r: (i, 0)),
          wsq, _spec((N_CHIPS, None, SQ_BLK, D), lambda i, j, r: (0, idx, 0, 0)), kind)],
        (M // tm, 1, 1), outs, (tm, D), extras, epilogue)


def _dw_sq(name, a, b):
    M = a.shape[0]
    tn = D // 2
    whole = _gemm(
        name,
        [(a, _spec((M, D), lambda i, j, r: (0, 0)), b, _spec((M, tn), lambda i, j, r: (0, j)), "tn")],
        (1, D // tn, 1),
        [(_sds((D, D), GRAD_WIRE_DTYPE), _spec((D, tn), lambda i, j, r: (0, j)))],
        (D, tn))[0]
    return whole.reshape(N_CHIPS, SQ_BLK, D)


def _retention_constants(T):
    pos = jnp.arange(T, dtype=F32)
    inv_freq = ROPE_BASE ** (-jnp.arange(0, RET_DK, 2, dtype=F32) / RET_DK)
    ang = pos[:, None] * inv_freq[None, :]
    cosf = jnp.concatenate([jnp.cos(ang), jnp.cos(ang)], axis=1)
    sins = jnp.concatenate([-jnp.sin(ang), jnp.sin(ang)], axis=1)
    lg = jnp.log(1.0 - 2.0 ** (-5.0 - jnp.arange(RET_HEADS, dtype=F32)))
    p = jnp.arange(CHUNK, dtype=F32)
    rel = p[:, None] - p[None, :]
    dmat = jnp.where(rel[None] >= 0, jnp.exp(rel[None] * lg[:, None, None]), 0.0)
    kd = jnp.exp((CHUNK - 1.0 - p)[None, :] * lg[:, None])[:, :, None]
    qd = jnp.exp((p + 1.0)[None, :] * lg[:, None])[:, :, None]
    cd = jnp.exp(CHUNK * lg)[:, None, None]
    return cosf, sins, dmat, kd, qd, cd


def _rot(t, cosv, sinv):
    return t * cosv + pltpu.roll(t, RET_DK // 2, 1) * sinv


def _unrot(t, cosv, sinv):
    return t * cosv - pltpu.roll(t, RET_DK // 2, 1) * sinv


def _ret_const_specs(cm):
    whole = lambda shape: _spec(shape, lambda c: (0,) * len(shape))
    return [
        _spec((RET_STEP_ROWS, RET_DK), lambda c: (cm(c), 0)),
        _spec((RET_STEP_ROWS, RET_DK), lambda c: (cm(c), 0)),
        whole((RET_HEADS, CHUNK, CHUNK)), whole((RET_HEADS, CHUNK, 1)), whole((RET_HEADS, CHUNK, 1)),
        whole((RET_HEADS, 1, 1)),
    ]


def _head(h, width):
    return slice(h * width, (h + 1) * width)


def _ret_fwd(u, consts, ret_gn):
    T = u.shape[0]
    nC = T // CHUNK
    kscale = RET_DK ** -0.5

    def body(q_ref, k_ref, v_ref, g_ref, cos_ref, sin_ref, dm_ref, kd_ref, qd_ref, cd_ref, gn_ref,
             qr_ref, kr_ref, ret_ref, yr_ref, st_ref, state):
        @pl.when(pl.program_id(0) == 0)
        def _():
            state[...] = jnp.zeros_like(state)

        for cc in range(RET_STEP_CHUNKS):
            rows = slice(cc * CHUNK, (cc + 1) * CHUNK)
            cosv, sinv = cos_ref[rows, :], sin_ref[rows, :]
            for h in range(RET_HEADS):
                hk, hv = _head(h, RET_DK), _head(h, RET_DV)
                q = _rot(q_ref[rows, hk], cosv, sinv)
                k = _rot(k_ref[rows, hk], cosv, sinv) * kscale
                v = v_ref[rows, hv]
                qr_ref[rows, hk] = q
                kr_ref[rows, hk] = k
                prev = state[h]
                st_ref[h, cc] = prev
                s = _dot(q, k, "nt") * dm_ref[h]
                ret = _dot(s, v, "nn") + _dot(q, prev, "nn") * qd_ref[h]
                state[h] = cd_ref[h] * prev + _dot(k * kd_ref[h], v, "tn")
                ret_ref[rows, hv] = ret
                mu = jnp.mean(ret, axis=-1, keepdims=True)
                xc = ret - mu
                yn = xc * lax.rsqrt(jnp.mean(xc * xc, axis=-1, keepdims=True) + EPS)
                g = g_ref[rows, hv]
                yr_ref[rows, hv] = ((g * _sigmoid(g)) * (yn * gn_ref[:, hv])).astype(BF16)

    cm = lambda c: c
    qk_w, v_w = RET_HEADS * RET_DK, RET_HEADS * RET_DV
    in_specs = [
        _spec((RET_STEP_ROWS, qk_w), lambda c: (c, 0)), _spec((RET_STEP_ROWS, qk_w), lambda c: (c, 1)),
        _spec((RET_STEP_ROWS, v_w), lambda c: (c, 1)), _spec((RET_STEP_ROWS, v_w), lambda c: (c, 2)),
    ] + _ret_const_specs(cm) + [_spec((1, v_w), lambda c: (0, 0))]
    qk_out = _spec((RET_STEP_ROWS, qk_w), lambda c: (c, 0))
    v_out = _spec((RET_STEP_ROWS, v_w), lambda c: (c, 0))
    return _pcall(
        body, name="ret_fwd", grid=(nC // RET_STEP_CHUNKS,),
        in_specs=in_specs,
        out_specs=[qk_out, qk_out, v_out, v_out,
                   _spec((RET_HEADS, RET_STEP_CHUNKS, RET_DK, RET_DV), lambda c: (0, c, 0, 0))],
        out_shape=[_sds((T, qk_w), F32), _sds((T, qk_w), F32), _sds((T, v_w), F32), _sds((T, v_w), BF16),
                   _sds((RET_HEADS, nC, RET_DK, RET_DV), F32)],
        scratch_shapes=[pltpu.VMEM((RET_HEADS, RET_DK, RET_DV), F32)],
    )(u, u, u, u, *consts, ret_gn)


def _ret_bwd(dyr, ret, u, qr, kr, states, consts, ret_gn):
    T = u.shape[0]
    nC = T // CHUNK
    kscale = RET_DK ** -0.5

    def body(dyr_ref, ret_ref, g_ref, q_ref, k_ref, v_ref, st_ref,
             cos_ref, sin_ref, dm_ref, kd_ref, qd_ref, cd_ref, gn_ref,
             dq_ref, dk_ref, dv_ref, dg_ref, dgn_ref, gstate):
        first = pl.program_id(0) == 0

        @pl.when(first)
        def _():
            gstate[...] = jnp.zeros_like(gstate)

        dgn_total = None
        for cc in reversed(range(RET_STEP_CHUNKS)):
            rows = slice(cc * CHUNK, (cc + 1) * CHUNK)
            cosv, sinv = cos_ref[rows, :], sin_ref[rows, :]
            dgn_parts = []
            for h in range(RET_HEADS):
                hk, hv = _head(h, RET_DK), _head(h, RET_DV)
                ret = ret_ref[rows, hv]
                mu = jnp.mean(ret, axis=-1, keepdims=True)
                xc = ret - mu
                rs = lax.rsqrt(jnp.mean(xc * xc, axis=-1, keepdims=True) + EPS)
                yn = xc * rs
                gn = gn_ref[:, hv]
                g = g_ref[rows, hv]
                sg = _sigmoid(g)
                dyr_v = dyr_ref[rows, hv]
                dretn = dyr_v * (g * sg)
                dg_ref[rows, hv] = (dyr_v * (yn * gn) * (sg * (1.0 + g * (1.0 - sg)))).astype(BF16)
                dgn_parts.append(jnp.sum(dretn * yn, axis=0, keepdims=True))
                dyn = dretn * gn
                d_o = rs * (dyn - jnp.mean(dyn, axis=-1, keepdims=True)
                            - yn * jnp.mean(dyn * yn, axis=-1, keepdims=True))

                q, k, v = q_ref[rows, hk], k_ref[rows, hk], v_ref[rows, hv]
                dmat, kd, qd = dm_ref[h], kd_ref[h], qd_ref[h]
                prev = st_ref[h, cc]
                gnext = gstate[h]
                s = _dot(q, k, "nt") * dmat
                ds = _dot(d_o, v, "nt") * dmat
                doq = d_o * qd
                dq = _dot(ds, k, "nn") + _dot(doq, prev, "nt")
                dk = _dot(ds, q, "tn") + _dot(v, gnext, "nt") * kd
                dv = _dot(s, d_o, "tn") + _dot(k * kd, gnext, "nn")
                gstate[h] = cd_ref[h] * gnext + _dot(q, doq, "tn")
                dq_ref[rows, hk] = _unrot(dq, cosv, sinv).astype(BF16)
                dk_ref[rows, hk] = _unrot(dk * kscale, cosv, sinv).astype(BF16)
                dv_ref[rows, hv] = dv.astype(BF16)
            dgn = jnp.concatenate(dgn_parts, axis=1)
            dgn_total = dgn if dgn_total is None else dgn_total + dgn
        _accumulate(dgn_ref, dgn_total, first)

    n_steps = nC // RET_STEP_CHUNKS
    cm = lambda c: n_steps - 1 - c
    qk_w, v_w = RET_HEADS * RET_DK, RET_HEADS * RET_DV
    vspec = lambda blk: _spec((RET_STEP_ROWS, v_w), lambda c: (cm(c), blk))
    qspec = _spec((RET_STEP_ROWS, qk_w), lambda c: (cm(c), 0))
    in_specs = [vspec(0), vspec(0), vspec(2), qspec, qspec, vspec(1),
                _spec((RET_HEADS, RET_STEP_CHUNKS, RET_DK, RET_DV), lambda c: (0, cm(c), 0, 0)),
                ] + _ret_const_specs(cm) + [_spec((1, v_w), lambda c: (0, 0))]
    return _pcall(
        body, name="ret_bwd", grid=(n_steps,),
        in_specs=in_specs,
        out_specs=[qspec, qspec, vspec(0), vspec(0), _spec((1, v_w), lambda c: (0, 0))],
        out_shape=[_sds((T, qk_w), BF16), _sds((T, qk_w), BF16), _sds((T, v_w), BF16), _sds((T, v_w), BF16),
                   _sds((1, v_w), F32)],
        scratch_shapes=[pltpu.VMEM((RET_HEADS, RET_DK, RET_DV), F32)],
    )(dyr, ret, u, qr, kr, u, states, *consts, ret_gn)


def _shift_down(x, s):
    rows = lax.broadcasted_iota(jnp.int32, x.shape, 0)
    return jnp.where(rows >= s, pltpu.roll(x, s, 0), 0.0)


def _shift_up(x, s):
    n = x.shape[0]
    rows = lax.broadcasted_iota(jnp.int32, x.shape, 0)
    return jnp.where(rows < n - s, pltpu.roll(x, n - s, 0), 0.0)


def _lru_specs(T):
    col = lambda off: _spec((T, LRU_BLOCK), lambda g: (0, off + g))
    vec = _spec((1, LRU_BLOCK), lambda g: (0, g))
    wblk = _spec((None, LRU_BLOCK, LRU_BLOCK), lambda g: (g, 0, 0))
    cw = _spec((CONV_TAPS, LRU_BLOCK), lambda g: (0, g))
    return col, vec, wblk, cw


def _lru_gates_fwd(u, conv_w, conv_b, w_r, b_r, w_i, b_i, lam):
    T = u.shape[0]
    col, vec, wblk, cw = _lru_specs(T)

    def body(x_ref, cw_ref, cb_ref, wr_ref, br_ref, wi_ref, bi_ref, lam_ref,
             xc_ref, r_ref, i_ref, a_ref, bx_ref):
        x = x_ref[...]
        w = cw_ref[...]
        xc = (_shift_down(x, 3) * w[0:1] + _shift_down(x, 2) * w[1:2] + _shift_down(x, 1) * w[2:3]
              + x * w[3:4] + cb_ref[...])
        r = _sigmoid(_dot(xc, wr_ref[...], "nn") + br_ref[...])
        i = _sigmoid(_dot(xc, wi_ref[...], "nn") + bi_ref[...])
        la = (-LRU_C) * r * _softplus(-lam_ref[...])
        xc_ref[...] = xc
        r_ref[...] = r
        i_ref[...] = i
        a_ref[...] = jnp.exp(la)
        bx_ref[...] = jnp.sqrt(-_expm1(2.0 * la)) * (i * xc)

    out = col(0)
    return _pcall(
        body, name="lru_gates_fwd", grid=(LRU_BLOCKS,),
        in_specs=[col(24), cw, vec, wblk, vec, wblk, vec, vec],
        out_specs=[out] * 5,
        out_shape=[_sds((T, D), F32)] * 5,
    )(u, conv_w, conv_b, w_r, b_r, w_i, b_i, lam)


def _lru_scan(name, a3, b3, reverse):
    T = a3.shape[0]
    nt = T // SCAN_TILE
    unroll = 8

    def body(a_ref, b_ref, o_ref, carry):
        @pl.when(pl.program_id(0) == 0)
        def _():
            carry[...] = jnp.zeros_like(carry)

        if not reverse:
            def step(t, h):
                h = a_ref[t] * h + b_ref[t]
                o_ref[t] = h
                return h
        else:
            def step(k, c):
                t = SCAN_TILE - 1 - k
                l = b_ref[t] + c
                o_ref[t] = l
                return a_ref[t] * l
        carry[...] = lax.fori_loop(0, SCAN_TILE, step, carry[...], unroll=unroll)

    idx = (lambda i: (nt - 1 - i, 0, 0)) if reverse else (lambda i: (i, 0, 0))
    blk = _spec((SCAN_TILE, LRU_BLOCKS, LRU_BLOCK), idx)
    return _pcall(
        body, name=name, grid=(nt,),
        in_specs=[blk, blk], out_specs=blk,
        out_shape=_sds((T, LRU_BLOCKS, LRU_BLOCK), F32),
        scratch_shapes=[pltpu.VMEM((LRU_BLOCKS, LRU_BLOCK), F32)],
    )(a3, b3)


def _lru_gates_bwd(lmb, hl, a, r, i, xc, u, conv_w, w_r, w_i, lam):
    T = u.shape[0]
    col, vec, wblk, cw = _lru_specs(T)

    def body(l_ref, h_ref, a_ref, r_ref, i_ref, xc_ref, x_ref, cw_ref, wr_ref, wi_ref, lam_ref,
             dx_ref, dwr_ref, dwi_ref, dvec_ref, dcw_ref):
        l = l_ref[...]
        av, rv, iv, xc = a_ref[...], r_ref[...], i_ref[...], xc_ref[...]
        lam_v = lam_ref[...]
        sp = _softplus(-lam_v)
        la = (-LRU_C) * rv * sp
        mult = jnp.sqrt(-_expm1(2.0 * la))
        da = l * _shift_down(h_ref[...], 1)
        dmult = l * (iv * xc)
        di = l * mult * xc
        dxc = l * mult * iv
        dla = da * av - dmult * (av * av) / mult
        dzr = (dla * ((-LRU_C) * sp)) * rv * (1.0 - rv)
        dzi = di * iv * (1.0 - iv)
        dsp = jnp.sum(dla * ((-LRU_C) * rv), axis=0, keepdims=True)
        dlam = dsp * (-_sigmoid(-lam_v))
        dwr_ref[...] = _dot(xc, dzr, "tn")
        dwi_ref[...] = _dot(xc, dzi, "tn")
        dxc = dxc + _dot(dzr, wr_ref[...], "nt") + _dot(dzi, wi_ref[...], "nt")
        x = x_ref[...]
        w = cw_ref[...]
        dx = (dxc * w[3:4] + _shift_up(dxc, 1) * w[2:3] + _shift_up(dxc, 2) * w[1:2]
              + _shift_up(dxc, 3) * w[0:1])
        dx_ref[...] = dx.astype(BF16)
        dvec_ref[...] = jnp.concatenate(
            [jnp.sum(dzr, axis=0, keepdims=True), jnp.sum(dzi, axis=0, keepdims=True), dlam,
             jnp.sum(dxc, axis=0, keepdims=True)], axis=0)
        dcw_ref[...] = jnp.concatenate(
            [jnp.sum(dxc * _shift_down(x, 3 - tap), axis=0, keepdims=True) if tap < 3
             else jnp.sum(dxc * x, axis=0, keepdims=True) for tap in range(CONV_TAPS)], axis=0)

    c0 = col(0)
    return _pcall(
        body, name="lru_gates_bwd", grid=(LRU_BLOCKS,),
        in_specs=[c0, c0, c0, c0, c0, c0, col(24), cw, wblk, wblk, vec],
        out_specs=[c0, wblk, wblk, cw, cw],
        out_shape=[_sds((T, D), BF16), _sds((LRU_BLOCKS, LRU_BLOCK, LRU_BLOCK), F32),
                   _sds((LRU_BLOCKS, LRU_BLOCK, LRU_BLOCK), F32), _sds((4, D), F32), _sds((CONV_TAPS, D), F32)],
    )(lmb, hl, a, r, i, xc, u, conv_w, w_r, w_i, lam)


def _xattn_probs(q, k):
    sc = _dot(q, k, "nt") * (X_HD ** -0.5)
    e = jnp.exp(sc - jnp.max(sc, axis=-1, keepdims=True))
    return e / jnp.sum(e, axis=-1, keepdims=True)


def _xattn_fwd(xq, xk, xv):
    T = xq.shape[0]
    tq = min(WIDE_ROW_TILE, T)
    M = xk.shape[0]

    def body(q_ref, k_ref, v_ref, o_ref):
        p = _xattn_probs(q_ref[...], k_ref[...])
        o_ref[...] = _dot(p, v_ref[...], "nn").astype(BF16)

    qs = _spec((tq, X_HD), lambda h, i: (i, h))
    kv = _spec((M, X_HD), lambda h, i: (0, h))
    return _pcall(
        body, name="xattn_fwd", grid=(X_HEADS, T // tq),
        in_specs=[qs, kv, kv], out_specs=qs, out_shape=_sds((T, D), BF16),
    )(xq, xk, xv)


def _xattn_bwd(xq, xk, xv, dxo):
    T = xq.shape[0]
    tq = min(WIDE_ROW_TILE, T)
    M = xk.shape[0]

    def body(q_ref, k_ref, v_ref, do_ref, dq_ref, dk_ref, dv_ref):
        first = pl.program_id(1) == 0
        q, k, v, do = q_ref[...], k_ref[...], v_ref[...], do_ref[...]
        p = _xattn_probs(q, k)
        dp = _dot(do, v, "nt")
        ds = p * (dp - jnp.sum(dp * p, axis=-1, keepdims=True)) * (X_HD ** -0.5)
        dq_ref[...] = _dot(ds, k, "nn").astype(BF16)
        _accumulate(dk_ref, _dot(ds, q, "tn"), first)
        _accumulate(dv_ref, _dot(p, do, "tn"), first)

    qs = _spec((tq, X_HD), lambda h, i: (i, h))
    kv = _spec((M, X_HD), lambda h, i: (0, h))
    return _pcall(
        body, name="xattn_bwd", grid=(X_HEADS, T // tq),
        in_specs=[qs, kv, kv, qs], out_specs=[qs, kv, kv],
        out_shape=[_sds((T, D), BF16), _sds((M, D), F32), _sds((M, D), F32)],
    )(xq, xk, xv, dxo)


def _adamw(name, w, g, m, v):
    R, C = w.shape
    tr = R
    for cand in (512, 352, 256):
        if R % cand == 0:
            tr = cand
            break

    def fn(irefs, orefs, ids):
        delta, mn, vn = _adamw_update(*(r[...] for r in irefs))
        orefs[0][...] = delta
        orefs[1][...] = mn
        orefs[2][...] = vn

    blk = _spec((tr, C), lambda i: (i, 0))
    return _rowwise(name, fn, [(w, blk), (g, blk), (m, blk), (v, blk)],
                    [(_sds((R, C), F32), blk)] * 3, (R // tr,))


def _adamw_update(wv, gv, mv, vv):
    c1 = 1.0 - ADAM_B1 ** ADAM_STEP
    c2 = 1.0 - ADAM_B2 ** ADAM_STEP
    mn = ADAM_B1 * mv + (1.0 - ADAM_B1) * gv
    vn = ADAM_B2 * vv + (1.0 - ADAM_B2) * (gv * gv)
    delta = -ADAM_LR * ((mn / c1) / (jnp.sqrt(vn / c2) + ADAM_EPS) + ADAM_WD * wv)
    return delta, mn, vn


def _adamw_halves(name, w, mine, theirs, widx, m, v, core):
    R, C = w.shape
    H = R // 2
    tr = H
    while tr * C * 4 > (1 << 20) and tr % 16 == 0:
        tr //= 2
    nb = H // tr

    def body(core_ref, w_ref, mine_ref, theirs_ref, m_ref, v_ref, g_out, d_out, m_out, v_out):
        gv = jnp.where(pl.program_id(0) == core_ref[0], mine_ref[...], theirs_ref[...])
        delta, mn, vn = _adamw_update(w_ref[...], gv, m_ref[...], v_ref[...])
        g_out[...] = gv
        d_out[...] = delta
        m_out[...] = mn
        v_out[...] = vn

    full = pl.BlockSpec((tr, C), lambda h, i, core_ref: (h * nb + i, 0))
    mine_spec = pl.BlockSpec((None, tr, C), lambda h, i, core_ref: (widx, jnp.where(h == core_ref[0], i, 0), 0))
    theirs_spec = pl.BlockSpec((None, tr, C), lambda h, i, core_ref: (widx, jnp.where(h == core_ref[0], 0, i), 0))
    return _pcall(
        body, name=name, grid=(2, nb), num_prefetch=1,
        in_specs=[full, mine_spec, theirs_spec, full, full], out_specs=[full] * 4,
        out_shape=[_sds((R, C), F32)] * 4,
    )(core, w, mine, theirs, m, v)


def _rmsnorm(name, x, g):
    M = x.shape[0]
    tm = min(ROW_TILE, M)

    def fn(irefs, orefs, ids):
        orefs[0][...] = _rms_fwd(irefs[0][...], irefs[1][...]).astype(BF16)

    row = _spec((tm, D), lambda i: (i, 0))
    return _rowwise(name, fn, [(x, row), (g, _spec((1, D), lambda i: (0, 0)))],
                    [(_sds((M, D), BF16), row)], (M // tm,))[0]


WEIGHT_AT = {
    "ffn1_w1": ("col1", 0), "ffn1_w3": ("col1", 1), "ffn1_w2": ("row2a", 0),
    "w_ret_o": ("sqA", 0), "w_lru_o": ("sqA", 1), "w_out": ("sqA", 2),
    "w_xq": ("sqB", 0), "w_xk": ("sqB", 1), "w_xv": ("sqC", 0), "w_xo": ("sqC", 1),
    "ffn2_w1": ("col2a", 0), "ffn2_w3": ("col2b", 0), "ffn2_w2": ("row2b", 0),
}


def _local_step(x, mem, tgt, gw, sm, big):
    T = x.shape[0]
    tm = ROW_TILE

    def wt(name):
        key, idx = WEIGHT_AT[name]
        return gw[key], idx

    row3 = lambda i, j, r: (i, 0)
    vec3 = lambda i, j, r: (0, 0)
    rowD = _spec((tm, D), row3)
    vecD = _spec((1, D), vec3)

    def residual_norm(acc, erefs, orefs, ids):
        xo = erefs[0][...] + acc
        orefs[0][...] = xo
        orefs[1][...] = _rms_fwd(xo, erefs[1][...]).astype(BF16)

    def res_norm_io(x_res, g):
        return ([(x_res, rowD), (g, vecD)],
                [(_sds((T, D), F32), rowD), (_sds((T, D), BF16), rowD)])

    a1, b1, s1, h1 = _ffn_up("ffn1_up", x, *wt("ffn1_w1"), *wt("ffn1_w3"), norm_gain=sm["ffn1_norm"])
    x1, h2 = _ffn_down("ffn1_down", s1, *wt("ffn1_w2"), x, sm["mix_norm"])

    tw = min(WIDE_ROW_TILE, T)
    wideD = _spec((tw, D), row3)
    u = _gemm(
        "mix_in",
        [(h2, wideD, gw["win"], _spec((None, None, IN_BLK, D), lambda i, j, r: (j, 0, 0, 0)), "nt")],
        (T // tw, N_CHIPS, 1),
        [(_sds((T, 5120), F32), _spec((tw, IN_BLK), lambda i, j, r: (i, j)))], (tw, IN_BLK))[0]

    consts = _retention_constants(T)
    qr, kr, ret, yr, states = _ret_fwd(u, consts, sm["ret_gn"])

    conv_w = gw["conv"][:, 0].transpose(1, 0, 2).reshape(CONV_TAPS, D)
    xc, rg, ig, av, bx = _lru_gates_fwd(u, conv_w, sm["conv_b"], sm["w_rgate"], sm["b_rgate"],
                                        sm["w_igate"], sm["b_igate"], sm["lru_lambda"])
    a3 = av.reshape(T, LRU_BLOCKS, LRU_BLOCK)
    hl = _lru_scan("lru_scan_fwd", a3, bx.reshape(T, LRU_BLOCKS, LRU_BLOCK), False).reshape(T, D)

    row1 = _spec((tm, D), lambda i: (i, 0))
    glru1 = _spec((tm, D), lambda i: (i, 4))

    def lru_out(irefs, orefs, ids):
        gl, _ = _gelu_and_grad(irefs[1][...])
        orefs[0][...] = (irefs[0][...] * gl).astype(BF16)

    yl = _rowwise("lru_out", lru_out, [(hl, row1), (u, glru1)], [(_sds((T, D), BF16), row1)], (T // tm,))[0]

    def gate_epilogue(acc, erefs, orefs, ids):
        orefs[0][...] = _sigmoid(acc + erefs[0][...])

    gates = _gemm(
        "mix_gates",
        [(h2, wideD, gw["wbg"], _spec((None, None, BG_BLK, D), lambda i, j, r: (j, 0, 0, 0)), "nt")],
        (T // tw, N_CHIPS, 1),
        [(_sds((T, 2 * D), F32), _spec((tw, BG_BLK), lambda i, j, r: (i, j)))], (tw, BG_BLK),
        [(sm["b_branch_gate"], _spec((1, BG_BLK), lambda i, j, r: (0, j)))], gate_epilogue)[0]

    y_ret = _proj_sq("y_ret", yr, *wt("w_ret_o"), "nn")[0]

    def merge_epilogue(acc, erefs, orefs, ids):
        orefs[0][...] = acc
        orefs[1][...] = (erefs[0][...] * erefs[2][...] + erefs[1][...] * acc).astype(BF16)

    y_lru, merged = _proj_sq(
        "y_lru", yl, *wt("w_lru_o"), "nn",
        extras=[(gates, _spec((tm, D), lambda i, j, r: (i, 0))), (gates, _spec((tm, D), lambda i, j, r: (i, 1))),
                (y_ret, rowD)],
        epilogue=merge_epilogue,
        outs=[(_sds((T, D), F32), rowD), (_sds((T, D), BF16), rowD)])

    ex, ou = res_norm_io(x1, sm["xattn_norm"])
    x2, hq = _proj_sq("mix_out", merged, *wt("w_out"), "nn", extras=ex, epilogue=residual_norm, outs=ou)

    m = _rmsnorm("mem_norm", mem, sm["mem_norm"])
    xq = _proj_sq("xq", hq, *wt("w_xq"), "nn", BF16)[0]
    xk = _proj_sq("xk", m, *wt("w_xk"), "nn", BF16)[0]
    xv = _proj_sq("xv", m, *wt("w_xv"), "nn", BF16)[0]
    xo = _xattn_fwd(xq, xk, xv)
    ex, ou = res_norm_io(x2, sm["ffn2_norm"])
    x3, h3 = _proj_sq("xattn_out", xo, *wt("w_xo"), "nn", extras=ex, epilogue=residual_norm, outs=ou)

    a2, b2, s2 = _ffn_up("ffn2_up", h3, *wt("ffn2_w1"), *wt("ffn2_w3"))
    loss, dx4, dg_final = _ffn_down("ffn2_down", s2, *wt("ffn2_w2"), x3, sm["final_norm"], loss_target=tgt)

    dx3, dg_ffn2 = _ffn_bwd("ffn2", dx4, h3, a2, b2, s2, *wt("ffn2_w1"), *wt("ffn2_w3"),
                            *wt("ffn2_w2"), x3, sm["ffn2_norm"], big)

    dxo = _proj_sq("d_xo", dx3, *wt("w_xo"), "nt", BF16)[0]
    big["w_xo"] = _dw_sq("dw_xo", xo, dx3)[None]
    dxq, dxk, dxv = _xattn_bwd(xq, xk, xv, dxo)
    big["w_xq"] = _dw_sq("dw_xq", hq, dxq)[None]
    ex, ou = _rms_bwd_io(x2, sm["xattn_norm"], dx3, T, tm)
    dx2, dg_xattn = _proj_sq("d_hq", dxq, *wt("w_xq"), "nt", extras=ex, epilogue=_rms_bwd_epilogue, outs=ou)
    big["w_xk"] = _dw_sq("dw_xk", m, dxk)[None]
    big["w_xv"] = _dw_sq("dw_xv", m, dxv)[None]

    M = mem.shape[0]

    def mem_norm_epilogue(acc, erefs, orefs, ids):
        _, dgp = _rms_bwd(erefs[0][...], erefs[1][...], acc)
        orefs[0][...] = dgp

    wsq_spec = lambda idx: _spec((N_CHIPS, None, SQ_BLK, D), lambda i, j, r: (0, idx, 0, 0))
    memD = _spec((M, D), row3)
    dg_mem = _gemm(
        "d_mem_norm",
        [(dxk, memD, wt("w_xk")[0], wsq_spec(wt("w_xk")[1]), "nt"),
         (dxv, memD, wt("w_xv")[0], wsq_spec(wt("w_xv")[1]), "nt")],
        (1, 1, 1), [(_sds((1, D), F32), vecD)], (M, D),
        [(mem, memD), (sm["mem_norm"], vecD)], mem_norm_epilogue)[0]

    def merged_bwd_epilogue(acc, erefs, orefs, ids):
        gr, gl, yrv, ylv = (e[...] for e in erefs)
        orefs[0][...] = (acc * gr).astype(BF16)
        orefs[1][...] = (acc * gl).astype(BF16)
        dgr = acc * yrv * gr * (1.0 - gr)
        dgl = acc * ylv * gl * (1.0 - gl)
        orefs[2][:, :D] = dgr.astype(BF16)
        orefs[2][:, D:] = dgl.astype(BF16)
        dbb = jnp.concatenate([jnp.sum(dgr, axis=0, keepdims=True), jnp.sum(dgl, axis=0, keepdims=True)], axis=1)
        _accumulate(orefs[3], dbb, ids[0] == 0)

    dy_ret, dy_lru, dgpre, db_bg = _proj_sq(
        "d_merged", dx2, *wt("w_out"), "nt",
        extras=[(gates, _spec((tm, D), lambda i, j, r: (i, 0))), (gates, _spec((tm, D), lambda i, j, r: (i, 1))),
                (y_ret, rowD), (y_lru, rowD)],
        epilogue=merged_bwd_epilogue,
        outs=[(_sds((T, D), BF16), rowD), (_sds((T, D), BF16), rowD),
              (_sds((T, 2 * D), BF16), _spec((tm, 2 * D), row3)),
              (_sds((1, 2 * D), F32), _spec((1, 2 * D), vec3))])
    big["w_branch_gate"] = _gemm(
        "dw_bg",
        [(h2, _spec((T, D), lambda j, n, r: (r, 0)), dgpre, _spec((T, BG_BLK), lambda j, n, r: (r, j)), "tn")],
        (N_CHIPS, 1, 1),
        [(_sds((N_CHIPS, D, BG_BLK), GRAD_WIRE_DTYPE), _spec((None, D, BG_BLK), lambda j, n, r: (j, 0, 0)))],
        (D, BG_BLK))[0][None]
    big["w_out"] = _dw_sq("dw_out", merged, dx2)[None]
    dyr = _proj_sq("d_yr", dy_ret, *wt("w_ret_o"), "nt")[0]
    big["w_ret_o"] = _dw_sq("dw_ret_o", yr, dy_ret)[None]
    dyl = _proj_sq("d_yl", dy_lru, *wt("w_lru_o"), "nt")[0]
    big["w_lru_o"] = _dw_sq("dw_lru_o", yl, dy_lru)[None]

    dq, dk, dv, dgr, dg_retgn = _ret_bwd(dyr, ret, u, qr, kr, states, consts, sm["ret_gn"])

    def lru_out_bwd(irefs, orefs, ids):
        gl, dgl = _gelu_and_grad(irefs[2][...])
        dyl_v = irefs[0][...]
        orefs[0][...] = dyl_v * gl
        orefs[1][...] = (dyl_v * irefs[1][...] * dgl).astype(BF16)

    dhl, dglru = _rowwise("lru_out_bwd", lru_out_bwd, [(dyl, row1), (hl, row1), (u, glru1)],
                          [(_sds((T, D), F32), row1), (_sds((T, D), BF16), row1)], (T // tm,))
    lmb = _lru_scan("lru_scan_bwd", a3, dhl.reshape(T, LRU_BLOCKS, LRU_BLOCK), True).reshape(T, D)
    dxl, dw_r, dw_i, dvec, dcw = _lru_gates_bwd(lmb, hl, av, rg, ig, xc, u, conv_w,
                                                sm["w_rgate"], sm["w_igate"], sm["lru_lambda"])

    du = jnp.concatenate([dq, dk, dv, dgr, dxl, dglru], axis=1)
    tk = T
    big["w_in"] = _gemm(
        "dw_in",
        [(h2, _spec((tk, D), lambda j, n, r: (r, 0)), du, _spec((tk, IN_BLK), lambda j, n, r: (r, j)), "tn")],
        (N_CHIPS, 1, T // tk),
        [(_sds((N_CHIPS, D, IN_BLK), GRAD_WIRE_DTYPE), _spec((None, D, IN_BLK), lambda j, n, r: (j, 0, 0)))],
        (D, IN_BLK))[0][None]
    tf = min(FFN_ROW_TILE, T)
    ex, ou = _rms_bwd_io(x1, sm["mix_norm"], dx2, T, tf)
    dx1, dg_mix = _gemm(
        "d_h2",
        [(du, _spec((tf, 5120), row3), gw["win"], _spec((N_CHIPS, None, IN_BLK, D), lambda i, j, r: (0, 0, 0, 0)), "nn"),
         (dgpre, _spec((tf, 2 * D), row3), gw["wbg"], _spec((N_CHIPS, None, BG_BLK, D), lambda i, j, r: (0, 0, 0, 0)),
          "nn")],
        (T // tf, 1, 1), ou, (tf, D), ex, _rms_bwd_epilogue)

    grad_x, dg_ffn1 = _ffn_bwd("ffn1", dx1, h1, a1, b1, s1, *wt("ffn1_w1"), *wt("ffn1_w3"),
                               *wt("ffn1_w2"), x, sm["ffn1_norm"], big)

    small = {
        "ffn1_norm": dg_ffn1, "mix_norm": dg_mix, "ret_gn": dg_retgn, "conv_b": dvec[3:4],
        "b_rgate": dvec[0:1], "b_igate": dvec[1:2], "lru_lambda": dvec[2:3], "xattn_norm": dg_xattn,
        "mem_norm": dg_mem, "ffn2_norm": dg_ffn2, "final_norm": dg_final, "b_branch_gate": db_bg,
        "conv_w": dcw, "w_rgate": dw_r, "w_igate": dw_i,
    }
    return loss, grad_x, small


ANY_SPEC = pl.BlockSpec(memory_space=pl.ANY)
VMEM_SPEC = pl.BlockSpec(memory_space=pltpu.VMEM)
N_PEER_CHIPS = N_CHIPS - 1


def _mesh_position():
    x, y, c = lax.axis_index("x"), lax.axis_index("y"), lax.axis_index("c")
    chips = [(1 - x, y), (x, 1 - y), (1 - x, 1 - y)]
    return x, y, c, chips


def _chip_index(x, y):
    return 2 * x + y


def _rows_half(ref, axis, h):
    n = ref.shape[axis] // 2
    idx = [slice(None)] * len(ref.shape)
    idx[axis] = pl.ds(pl.multiple_of(h * n, BF16_TILE_ROWS), n)
    return ref.at[tuple(idx)]


def _remote(src, dst, send_sem, recv_sem, device):
    return pltpu.make_async_remote_copy(src_ref=src, dst_ref=dst, send_sem=send_sem, recv_sem=recv_sem,
                                        device_id=device, device_id_type=MESH)


def _gather_chips_task(shards, split, landed, legs="both"):
    keys = list(shards)
    n = len(keys)

    def operands():
        if legs == "pass_on":
            return [landed[k] for k in keys]
        chip_me = _chip_index(lax.axis_index("x"), lax.axis_index("y"))
        return [lax.dynamic_update_slice(lax.empty((N_CHIPS,) + shards[k].shape, shards[k].dtype), shards[k][None],
                                         (chip_me,) + (0,) * shards[k].ndim) for k in keys]

    def my_rows(ref, c):
        return _rows_half(ref, 1, c)

    def make_direct(ins, outs, send_sem, recv_sem):
        x, y, c, chips = _mesh_position()
        s_me = _chip_index(x, y)
        starts, arrivals = [], []
        for g in range(n):
            for k, chip in enumerate(chips):
                sems = (send_sem(3 * g + k), recv_sem(3 * g + k))
                starts.append(functools.partial(_remote, outs[g].at[s_me], outs[g].at[s_me], *sems, (*chip, c)))
                got = outs[g].at[_chip_index(*chip)]
                arrivals.append(functools.partial(_remote, got, got, *sems, (*chip, c)))
        return starts, arrivals

    def make_swap(ins, outs, send_sem, recv_sem):
        x, y, c, _ = _mesh_position()
        first, _ = _axis_neighbours(x, y, c)
        starts, arrivals = [], []
        for g in range(n):
            sems = (send_sem(3 * g), recv_sem(3 * g))
            mine = my_rows(outs[g].at[_chip_index(x, y)], c)
            starts.append(functools.partial(_remote, mine, mine, *sems, (*first, c)))
            got = my_rows(outs[g].at[_chip_index(*first)], c)
            arrivals.append(functools.partial(_remote, got, got, *sems, (*first, c)))
        return starts, arrivals

    def make_pass_on(ins, outs, send_sem, recv_sem):
        x, y, c, _ = _mesh_position()
        first, second = _axis_neighbours(x, y, c)
        diagonal = (1 - x, 1 - y)
        starts, arrivals = [], []
        for g in range(n):
            half = lambda chip: my_rows(outs[g].at[_chip_index(*chip)], c)
            for k, (sent, arriving) in enumerate([((x, y), second), (first, diagonal)]):
                sems = (send_sem(3 * g + 1 + k), recv_sem(3 * g + 1 + k))
                starts.append(functools.partial(_remote, half(sent), half(sent), *sems, (*second, c)))
                arrivals.append(functools.partial(_remote, half(arriving), half(arriving), *sems, (*second, c)))
        return starts, arrivals

    def finish(res):
        landed.update(zip(keys, res))

    shapes = lambda: [_sds((N_CHIPS,) + shards[k].shape, shards[k].dtype) for k in keys]
    aliases = {g: g for g in range(n)}
    if not split:
        return _Task("chips", operands, shapes, aliases, 3 * n, make_direct, finish)
    if legs == "swap":
        return _Task("first", operands, shapes, aliases, 3 * n, make_swap, finish)
    if legs == "pass_on":
        return _Task("second", operands, shapes, aliases, 3 * n, make_pass_on, finish)
    return _Task("first+second", operands, shapes, aliases, 3 * n, make_swap, finish, make_second=make_pass_on)


def _gather_sibling_task(keys, landed, ready):
    n = len(keys)

    def make(ins, outs, send_sem, recv_sem):
        x, y, c, chips = _mesh_position()
        starts, arrivals = [], []
        for g in range(n):
            for k, chip in enumerate(chips):
                o = outs[g].at[_chip_index(*chip)]
                got, other = _rows_half(o, 1, c), _rows_half(o, 1, 1 - c)
                starts.append(functools.partial(_remote, got, got, send_sem(3 * g + k), recv_sem(3 * g + k),
                                                (x, y, 1 - c)))
                arrivals.append(functools.partial(_remote, other, other, send_sem(3 * g + k), recv_sem(3 * g + k),
                                                  (x, y, 1 - c)))
        return starts, arrivals

    def finish(res):
        ready.update(zip(keys, res))

    return _Task("sibling", lambda: [landed[k] for k in keys],
                 lambda: [_sds(landed[k].shape, landed[k].dtype) for k in keys],
                 {g: g for g in range(n)}, 3 * n, make, finish)


def _pair_swap_task(names, big, got):
    n = len(names)

    def make(ins, outs, send_sem, recv_sem):
        x, y, c, _ = _mesh_position()
        copies = [functools.partial(_remote, _rows_half(ins[a], 2, 1 - c), outs[a], send_sem(a), recv_sem(a),
                                    (x, y, 1 - c)) for a in range(n)]
        return copies, copies

    def shapes():
        return [_sds(big[k].shape[:2] + (big[k].shape[2] // 2, big[k].shape[3]), big[k].dtype) for k in names]

    return _Task("sibling", lambda: [big[k] for k in names], shapes, {}, n, make,
                 lambda res: got.update(zip(names, res)))


def _rs_pair_sum(name, fulls, gots, core):
    n = len(fulls)
    shapes = [(f.shape[2] // 2, f.shape[3]) for f in fulls]

    def body(core_ref, *refs):
        for a_ref, b_ref, o_ref in zip(refs[:n], refs[n:2 * n], refs[2 * n:]):
            o_ref[...] = (a_ref[...].astype(F32) + b_ref[...].astype(F32)).astype(BF16)

    mine = [pl.BlockSpec((None, None) + hc, lambda s, core_ref: (0, s, core_ref[0], 0)) for hc in shapes]
    slot = [pl.BlockSpec((None, None) + hc, lambda s, core_ref: (0, s, 0, 0)) for hc in shapes]
    return _pcall(
        body, name=name, grid=(N_CHIPS,), num_prefetch=1,
        in_specs=mine + slot, out_specs=slot,
        out_shape=[_sds((1, N_CHIPS) + hc, BF16) for hc in shapes],
    )(core, *fulls, *gots)


def _chip_exchange_task(names, pair_sums, by_source, part=0, nparts=1):
    n = len(names)

    def rows(ref):
        h = ref.shape[1] // nparts
        return ref.at[:, pl.ds(part * h, h), :]

    def make(ins, outs, send_sem, recv_sem):
        x, y, c, chips = _mesh_position()
        s_me = _chip_index(x, y)
        starts, arrivals = [], []
        for a in range(n):
            for k, chip in enumerate(chips):
                s_k = _chip_index(*chip)
                starts.append(functools.partial(_remote, rows(ins[a].at[:, s_k]), rows(outs[a].at[:, s_me]),
                                                send_sem(3 * a + k), recv_sem(3 * a + k), (*chip, c)))
                got = rows(outs[a].at[:, s_k])
                arrivals.append(functools.partial(_remote, got, got, send_sem(3 * a + k), recv_sem(3 * a + k),
                                                  (*chip, c)))
        return starts, arrivals

    def operands():
        return [pair_sums[k] for k in names] + ([by_source[k] for k in names] if part else [])

    return _Task("chips", operands, lambda: [_sds(pair_sums[k].shape, pair_sums[k].dtype) for k in names],
                 {n + a: a for a in range(n)} if part else {}, 3 * n, make,
                 lambda res: by_source.update(zip(names, res)))


def _rs_chip_sum(name, owns, parts, chip):
    n = len(owns)
    ns = N_CHIPS
    shapes = [p.shape[2:] for p in parts]

    def body(chip_ref, *refs):
        me = chip_ref[0]
        for i in range(n):
            own_v = refs[i][...].astype(F32)
            slots = refs[n + ns * i:n + ns * (i + 1)]
            tot = None
            for s in range(ns):
                term = jnp.where(me == s, own_v, slots[s][...].astype(F32))
                tot = term if tot is None else tot + term
            refs[n + ns * n + i][...] = tot

    def slot_spec(hc, s):
        return pl.BlockSpec((None, None) + hc,
                            lambda g, chip_ref: (0, jnp.where(chip_ref[0] == s, (s + 1) % ns, s), 0, 0))

    own_specs = [pl.BlockSpec((None, None) + hc, lambda g, chip_ref: (0, chip_ref[0], 0, 0)) for hc in shapes]
    slot_specs = [slot_spec(hc, s) for hc in shapes for s in range(ns)]
    return _pcall(
        body, name=name, grid=(1,), num_prefetch=1,
        in_specs=own_specs + slot_specs,
        out_specs=[pl.BlockSpec((None,) + hc, lambda g, chip_ref: (0, 0, 0)) for hc in shapes],
        out_shape=[_sds((1,) + hc, F32) for hc in shapes],
    )(chip, *owns, *[p for p in parts for _ in range(ns)])


def _pair_gather_task(names, halves, sibling_halves):
    n = len(names)

    def make(ins, outs, send_sem, recv_sem):
        x, y, c, _ = _mesh_position()
        copies = [functools.partial(_remote, ins[a], outs[a], send_sem(a), recv_sem(a), (x, y, 1 - c))
                  for a in range(n)]
        return copies, copies

    return _Task("sibling", lambda: [halves[k] for k in names], lambda: [_sds(halves[k].shape, F32) for k in names],
                 {}, n, make, lambda res: sibling_halves.update(zip(names, res)))


def _small_allreduce(arrs):
    n = len(arrs)
    per = 1 + 2 * N_PEER_CHIPS

    def body(*refs):
        v_refs, o_refs = refs[:n], refs[n:2 * n]
        sib, pair, part = refs[2 * n:3 * n], refs[3 * n:4 * n], refs[4 * n:5 * n]
        send_sems, recv_sems = refs[5 * n:]
        x, y, c, chips = _mesh_position()
        s_me = _chip_index(x, y)

        def quarter(ref, s):
            q = ref.shape[0] // N_CHIPS
            return ref.at[pl.ds(pl.multiple_of(s * q, F32_TILE_ROWS), q)]

        def exchange(first_sem, src, dst_of, arrival_of):
            sems = lambda a, k: (send_sems.at[a * per + first_sem + k], recv_sems.at[a * per + first_sem + k])
            sends = [_remote(src(a, _chip_index(*chip)), dst_of(a, s_me), *sems(a, k), (*chip, c))
                     for a in range(n) for k, chip in enumerate(chips)]
            for cp in sends:
                cp.start()
            for a in range(n):
                for k, chip in enumerate(chips):
                    got = arrival_of(a, _chip_index(*chip))
                    _remote(got, got, *sems(a, k), (*chip, c)).wait_recv()
            for cp in sends:
                cp.wait_send()

        swaps = [_remote(v_refs[a], sib[a], send_sems.at[a * per], recv_sems.at[a * per], (x, y, 1 - c))
                 for a in range(n)]
        for cp in swaps:
            cp.start()
        for cp in swaps:
            cp.wait()
        for a in range(n):
            pair[a][...] = v_refs[a][...] + sib[a][...]
        exchange(1, lambda a, s_k: quarter(pair[a], s_k), lambda a, s: part[a].at[s], lambda a, s_k: part[a].at[s_k])
        for a in range(n):
            part[a][s_me] = quarter(pair[a], s_me)[...]
            q = o_refs[a].shape[0] // N_CHIPS
            o_refs[a][pl.ds(pl.multiple_of(s_me * q, F32_TILE_ROWS), q), :] = (
                ((part[a][0] + part[a][1]) + part[a][2]) + part[a][3])
        exchange(1 + N_PEER_CHIPS, lambda a, s_k: quarter(o_refs[a], s_me), lambda a, s: quarter(o_refs[a], s),
                 lambda a, s_k: quarter(o_refs[a], s_k))

    shapes = [a.shape for a in arrs]
    return _pcall(
        body, name="small_allreduce", grid=(1,), own_peers=("sibling", "chips"),
        in_specs=[VMEM_SPEC] * n, out_specs=[VMEM_SPEC] * n, out_shape=[_sds(s, F32) for s in shapes],
        scratch_shapes=([pltpu.VMEM(s, F32) for s in shapes] * 2
                        + [pltpu.VMEM((N_CHIPS, s[0] // N_CHIPS, s[1]), F32) for s in shapes]
                        + [pltpu.SemaphoreType.DMA((n * per,)), pltpu.SemaphoreType.DMA((n * per,))]),
    )(*arrs)


TRANSPOSED_WEIGHTS = ("ffn1_w1", "ffn1_w3", "ffn2_w1", "ffn2_w3")
SMALL_LAYOUT = [("ffn1_norm", 1), ("mix_norm", 1), ("ret_gn", 1), ("conv_b", 1), ("b_rgate", 1), ("b_igate", 1),
                ("lru_lambda", 1), ("xattn_norm", 1), ("mem_norm", 1), ("ffn2_norm", 1), ("final_norm", 1),
                ("b_branch_gate", 2), ("conv_w", CONV_TAPS)]
SMALL_ROWS = 32
GATE_WEIGHTS = ("w_rgate", "w_igate")
WEIGHT_ORDER = ["ffn1_norm", "ffn1_w1", "ffn1_w3", "ffn1_w2", "mix_norm", "w_in", "ret_gn", "w_ret_o", "conv_w",
                "conv_b", "w_rgate", "b_rgate", "w_igate", "b_igate", "lru_lambda", "w_lru_o", "w_branch_gate",
                "b_branch_gate", "w_out", "xattn_norm", "mem_norm", "w_xq", "w_xk", "w_xv", "w_xo", "ffn2_norm",
                "ffn2_w1", "ffn2_w3", "ffn2_w2", "final_norm"]


SMALL_USED_ROWS = sum(n for _, n in SMALL_LAYOUT)


def _pack_small(parts, extra_row=None):
    rows = [parts[name].reshape(n, D) for name, n in SMALL_LAYOUT]
    if extra_row is not None:
        rows.append(extra_row)
    rows.append(jnp.zeros((SMALL_ROWS - sum(r.shape[0] for r in rows), D), F32))
    return jnp.concatenate(rows, axis=0)


def _unpack_small(packed, shapes):
    out, r = {}, 0
    for name, n in SMALL_LAYOUT:
        out[name] = packed[r:r + n].reshape(shapes[name])
        r += n
    return out


def kernel(x, mem, ffn1_norm, ffn1_w1, ffn1_w3, ffn1_w2, mix_norm, w_in, ret_gn, w_ret_o, conv_w, conv_b, w_rgate, b_rgate, w_igate, b_igate, lru_lambda, w_lru_o, w_branch_gate, b_branch_gate, w_out, xattn_norm, mem_norm, w_xq, w_xk, w_xv, w_xo, ffn2_norm, ffn2_w1, ffn2_w3, ffn2_w2, final_norm, loss_target, m_ffn1_norm, m_ffn1_w1, m_ffn1_w3, m_ffn1_w2, m_mix_norm, m_w_in, m_ret_gn, m_w_ret_o, m_conv_w, m_conv_b, m_w_rgate, m_b_rgate, m_w_igate, m_b_igate, m_lru_lambda, m_w_lru_o, m_w_branch_gate, m_b_branch_gate, m_w_out, m_xattn_norm, m_mem_norm, m_w_xq, m_w_xk, m_w_xv, m_w_xo, m_ffn2_norm, m_ffn2_w1, m_ffn2_w3, m_ffn2_w2, m_final_norm, v_ffn1_norm, v_ffn1_w1, v_ffn1_w3, v_ffn1_w2, v_mix_norm, v_w_in, v_ret_gn, v_w_ret_o, v_conv_w, v_conv_b, v_w_rgate, v_b_rgate, v_w_igate, v_b_igate, v_lru_lambda, v_w_lru_o, v_w_branch_gate, v_b_branch_gate, v_w_out, v_xattn_norm, v_mem_norm, v_w_xq, v_w_xk, v_w_xv, v_w_xo, v_ffn2_norm, v_ffn2_w1, v_ffn2_w3, v_ffn2_w2, v_final_norm):
    given = dict(locals())
    w = {n: given[n] for n in WEIGHT_ORDER}
    mom = {n: given["m_" + n] for n in WEIGHT_ORDER}
    var = {n: given["v_" + n] for n in WEIGHT_ORDER}
    chip = _chip_index(lax.axis_index("x"), lax.axis_index("y"))
    core = lax.axis_index("c").astype(jnp.int32).reshape(1)

    chip_id = chip.astype(jnp.int32).reshape(1)
    sm = {n: w[n] for n in ["ffn1_norm", "mix_norm", "ret_gn", "conv_b", "b_rgate", "b_igate", "lru_lambda",
                            "xattn_norm", "mem_norm", "ffn2_norm", "b_branch_gate"]}
    sm["final_norm"] = w["final_norm"].reshape(1, D)
    sm["w_rgate"] = w["w_rgate"][0]
    sm["w_igate"] = w["w_igate"][0]

    local = lambda a, n: jnp.swapaxes(a[0], 0, 1) if n in TRANSPOSED_WEIGHTS else a[0]
    stack = lambda names: jnp.stack([local(w[n], n) for n in names], axis=0).astype(BF16)
    shard = {"col1": stack(["ffn1_w1", "ffn1_w3"]), "row2a": stack(["ffn1_w2"]),
             "win": jnp.swapaxes(w["w_in"], 1, 2).astype(BF16),
             "wbg": jnp.swapaxes(w["w_branch_gate"], 1, 2).astype(BF16),
             "sqA": stack(["w_ret_o", "w_lru_o", "w_out"]), "sqB": stack(["w_xq", "w_xk"]),
             "sqC": stack(["w_xv", "w_xo"]), "col2a": stack(["ffn2_w1"]), "col2b": stack(["ffn2_w3"]),
             "row2b": stack(["ffn2_w2"]), "conv": w["conv_w"]}
    gw, landed = {}, {}
    over_chips = lambda keys: _gather_chips_task({k: shard[k] for k in keys}, True, landed)
    to_sibling = lambda keys: _gather_sibling_task(keys, landed, gw)

    big, got, pair_sums, by_source, halves, sibling_halves, outs = {}, {}, {}, {}, {}, {}, {}
    pair_swap = lambda names: _pair_swap_task(names, big, got)
    exchange = lambda names, part=0, nparts=1: _chip_exchange_task(names, pair_sums, by_source, part, nparts)
    pair_gather = lambda names: _pair_gather_task(names, halves, sibling_halves)

    def pair_sum(names):
        res = _rs_pair_sum("rs_pair_sum_" + names[0], [big[n] for n in names], [got[n] for n in names], core)
        pair_sums.update(zip(names, res))

    def chip_sum(names):
        res = _rs_chip_sum("rs_chip_sum_" + names[0], [pair_sums[n] for n in names], [by_source[n] for n in names],
                           chip_id)
        halves.update(zip(names, res))

    def adamw(names):
        for n in names:
            res = _adamw_halves("adamw_" + n, local(w[n], n), halves[n], sibling_halves[n], 0, local(mom[n], n),
                                local(var[n], n), core)
            outs[n] = tuple((jnp.swapaxes(r, 0, 1) if n in TRANSPOSED_WEIGHTS else r)[None] for r in res)

    do = lambda fn, names: functools.partial(fn, names)
    ffn2_grads = ["ffn2_w2", "ffn2_w1", "ffn2_w3"]
    xattn_grads = ["w_xo", "w_xq", "w_xk", "w_xv"]
    mix_out_grads = ["w_branch_gate", "w_out", "w_ret_o", "w_lru_o"]
    conv_gather = _gather_chips_task({"conv": shard["conv"]}, False, gw)
    swap = lambda key: _gather_chips_task({key: shard[key]}, True, landed, legs="swap")
    pass_on = lambda key: _gather_chips_task({key: shard[key]}, True, landed, legs="pass_on")
    plan = _Plan()
    plan.tasks = {
        "ag_first_chips": [over_chips(["col1"]), swap("row2a")],
        "ag_first_sibling": [to_sibling(["col1"]), pass_on("row2a"), swap("win")],
        "ffn1_up": [to_sibling(["row2a"]), pass_on("win"), swap("wbg")],
        "ffn1_down": [to_sibling(["win"]), pass_on("wbg"), swap("sqA"), conv_gather],
        "mix_in": [to_sibling(["wbg"]), pass_on("sqA"), swap("col2a")],
        "ret_fwd": [to_sibling(["sqA"]), pass_on("col2a"), swap("sqB")],
        "lru_gates_fwd": [to_sibling(["col2a"]), pass_on("sqB"), swap("sqC")],
        "lru_scan_fwd": [to_sibling(["sqB"]), pass_on("sqC"), swap("col2b")],
        "mix_gates": [to_sibling(["sqC"]), pass_on("col2b")],
        "y_lru": [to_sibling(["col2b"]), swap("row2b")],
        "ffn2_up": [pass_on("row2b")],
        "ffn2_up_sibling": [to_sibling(["row2b"])],
        "ffn2_dh": [pair_swap(ffn2_grads)],
        "xattn_bwd": [exchange(["ffn2_w2"], 0, 2)],
        "d_hq": [exchange(["ffn2_w2"], 1, 2)],
        "d_merged": [exchange(["ffn2_w1"], 0, 2), pair_swap(xattn_grads)],
        "lru_out_bwd": [exchange(["w_xo"])],
        "ret_bwd": [exchange(["ffn2_w1"], 1, 2), exchange(["ffn2_w3"], 0, 2), pair_swap(mix_out_grads)],
        "lru_scan_bwd": [exchange(["ffn2_w3"], 1, 2)],
        "lru_gates_bwd": [exchange(["w_xq", "w_xk"]), pair_gather(ffn2_grads)],
        "dw_in": [exchange(["w_xv", "w_out"])],
        "d_h2": [exchange(["w_branch_gate", "w_ret_o", "w_lru_o"]), pair_swap(["w_in"]), pair_gather(xattn_grads)],
        "ffn1_bwd_mid": [exchange(["w_in"], 0, 2), pair_gather(mix_out_grads)],
        "ffn1_dw2": [exchange(["w_in"], 2, 4)],
        "ffn1_dw1": [exchange(["w_in"], 3, 4), pair_swap(["ffn1_w2"])],
        "ffn1_dw3": [exchange(["ffn1_w2"], 0, 2), pair_swap(["ffn1_w1"]), pair_gather(["w_in"])],
        "ffn1_dh": [exchange(["ffn1_w2"], 1, 2), exchange(["ffn1_w1"]), pair_swap(["ffn1_w3"])],
        "small_allreduce": [exchange(["ffn1_w3"]), pair_gather(["ffn1_w2"])],
        "rs_last_gather": [pair_gather(["ffn1_w1", "ffn1_w3"])],
    }
    plan.after = {
        "ffn2_up": [functools.partial(_comm_call, "ffn2_up_sibling")],
        "ffn2_dh": [do(pair_sum, ffn2_grads)],
        "d_merged": [do(pair_sum, xattn_grads)],
        "ret_bwd": [do(pair_sum, mix_out_grads)],
        "lru_scan_bwd": [do(chip_sum, ffn2_grads)],
        "lru_gates_bwd": [do(adamw, ffn2_grads)],
        "dw_in": [do(chip_sum, xattn_grads)],
        "d_h2": [do(chip_sum, mix_out_grads), do(pair_sum, ["w_in"]), do(adamw, xattn_grads)],
        "ffn1_bwd_mid": [do(adamw, mix_out_grads)],
        "ffn1_dw1": [do(chip_sum, ["w_in"]), do(pair_sum, ["ffn1_w2"])],
        "ffn1_dw3": [do(pair_sum, ["ffn1_w1"]), do(adamw, ["w_in"])],
        "ffn1_dh": [do(pair_sum, ["ffn1_w3"]), do(chip_sum, ["ffn1_w2"])],
        "small_allreduce": [do(chip_sum, ["ffn1_w1", "ffn1_w3"]), functools.partial(_comm_call, "rs_last_gather"),
                    do(adamw, ["ffn1_w2", "ffn1_w1", "ffn1_w3"])],
    }
    global _plan
    _plan = plan
    try:
        _comm_call("ag_first_chips")
        _comm_call("ag_first_sibling")
        loss_part, grad_x, small = _local_step(x[0], mem[0], loss_target[0], gw, sm, big)
        gate2d = lambda a: a.reshape(LRU_BLOCKS * LRU_BLOCK, LRU_BLOCK)
        loss_row = jnp.pad(loss_part, ((0, 0), (0, D - loss_part.shape[1])))
        small_sum, *gate_sums = _small_allreduce([_pack_small(small, loss_row)]
                                                 + [gate2d(small[n]) for n in GATE_WEIGHTS])
    finally:
        _plan = None
    assert not plan.tasks and not plan.after, (list(plan.tasks), list(plan.after))
    loss = small_sum[SMALL_USED_ROWS, 0]

    small_shapes = {n: w[n].shape for n, _ in SMALL_LAYOUT}
    small_shapes["conv_w"] = (CONV_TAPS, D)
    conv_row = SMALL_USED_ROWS - CONV_TAPS
    conv_grad = lax.dynamic_slice(small_sum[conv_row:conv_row + CONV_TAPS], (0, chip * SQ_BLK), (CONV_TAPS, SQ_BLK))
    small_w = {n: w[n] for n, _ in SMALL_LAYOUT}
    small_m = {n: mom[n] for n, _ in SMALL_LAYOUT}
    small_v = {n: var[n] for n, _ in SMALL_LAYOUT}
    pad_cols = lambda a: jnp.pad(a[0], ((0, 0), (0, D - SQ_BLK)))
    for dct in (small_w, small_m, small_v):
        dct["conv_w"] = pad_cols(dct["conv_w"])
    g_pack = lax.dynamic_update_slice(small_sum, jnp.pad(conv_grad, ((0, 0), (0, D - SQ_BLK))), (conv_row, 0))
    d_pack, m_pack, v_pack = _adamw("adamw_small", _pack_small(small_w), g_pack, _pack_small(small_m),
                                    _pack_small(small_v))
    unpacked = [_unpack_small(p, small_shapes) for p in (g_pack, d_pack, m_pack, v_pack)]
    for n, _ in SMALL_LAYOUT:
        if n == "conv_w":
            outs[n] = tuple(u[n][:, :SQ_BLK][None] for u in unpacked)
        else:
            outs[n] = tuple(u[n] for u in unpacked)
    for n, gsum in zip(GATE_WEIGHTS, gate_sums):
        d, nm, nv = _adamw("adamw_" + n, gate2d(w[n]), gsum, gate2d(mom[n]), gate2d(var[n]))
        outs[n] = tuple(r.reshape(w[n].shape) for r in (gsum, d, nm, nv))

    result = [loss, grad_x[None]]
    for k in range(4):
        result += [outs[n][k] for n in WEIGHT_ORDER]
    return tuple(result)
```

```python
import functools
import math

import jax
import jax.numpy as jnp
from jax import lax
from jax.experimental import pallas as pl
from jax.experimental.pallas import tpu as pltpu

F32 = jnp.float32
BF16 = jnp.bfloat16
GRAD_WIRE_DTYPE = BF16
MESH = pl.DeviceIdType.MESH

D = 1024
EPS = 1e-6
RET_HEADS = 4
RET_DK = 128
RET_DV = 256
CHUNK = 128
ROPE_BASE = 10000.0
LRU_BLOCKS = 8
LRU_BLOCK = 128
CONV_TAPS = 4
LRU_C = 8.0
D_FF = 2816
X_HEADS = 4
X_HD = 256
N_CHIPS = 4
FF_BLK = D_FF // N_CHIPS
IN_BLK = 5120 // N_CHIPS
BG_BLK = 2048 // N_CHIPS
SQ_BLK = D // N_CHIPS

ADAM_LR = 0.001
ADAM_B1 = 0.9
ADAM_B2 = 0.999
ADAM_EPS = 1e-08
ADAM_WD = 0.01
ADAM_STEP = 10

F32_TILE_ROWS = 8
BF16_TILE_ROWS = 16
VMEM_LIMIT_BYTES = 56 * 1024 * 1024
ROW_TILE = 512
WIDE_ROW_TILE = 1024
FFN_ROW_TILE = 256
DW_BLK = D_FF // 2
SCAN_TILE = 256
RET_STEP_CHUNKS = 2
RET_STEP_ROWS = RET_STEP_CHUNKS * CHUNK

_DN = {
    "nn": (((1,), (0,)), ((), ())),
    "nt": (((1,), (1,)), ((), ())),
    "tn": (((0,), (0,)), ((), ())),
}


def _cparams(n_axes, collective_id=None):
    return pltpu.CompilerParams(dimension_semantics=("arbitrary",) * n_axes,
                                vmem_limit_bytes=VMEM_LIMIT_BYTES, collective_id=collective_id)


def _dot(a, b, kind):
    if b.ndim == 3:
        b = b.reshape(b.shape[0] * b.shape[1], b.shape[2])
    return lax.dot_general(a.astype(BF16), b.astype(BF16), _DN[kind], preferred_element_type=F32)


def _sigmoid(x):
    return 1.0 / (1.0 + jnp.exp(-x))


def _log1p_pos(e):
    u = 1.0 + e
    return jnp.where(u == 1.0, e, jnp.log(u) * (e / jnp.where(u == 1.0, 1.0, u - 1.0)))


def _expm1(x):
    u = jnp.exp(x)
    lu = jnp.log(u)
    safe = jnp.where(lu == 0.0, 1.0, lu)
    return jnp.where(u == 1.0, x, (u - 1.0) * (x / safe))


def _softplus(z):
    return jnp.maximum(z, 0.0) + _log1p_pos(jnp.exp(-jnp.abs(z)))


_GELU_C = math.sqrt(2.0 / math.pi)


def _gelu_and_grad(x):
    x2 = x * x
    t = jnp.tanh(_GELU_C * (x + 0.044715 * x * x2))
    g = 0.5 * x * (1.0 + t)
    dg = 0.5 * (1.0 + t) + 0.5 * x * (1.0 - t * t) * (_GELU_C * (1.0 + 3.0 * 0.044715 * x2))
    return g, dg


def _rms_fwd(x, g):
    r = lax.rsqrt(jnp.mean(x * x, axis=-1, keepdims=True) + EPS)
    return (x * r) * g


def _rms_bwd(x, g, dh):
    r = lax.rsqrt(jnp.mean(x * x, axis=-1, keepdims=True) + EPS)
    n = x * r
    dyg = dh * g
    dx = r * (dyg - n * jnp.mean(dyg * n, axis=-1, keepdims=True))
    return dx, jnp.sum(dh * n, axis=0, keepdims=True)


def _accumulate(ref, val, first):
    @pl.when(first)
    def _():
        ref[...] = val

    @pl.when(jnp.logical_not(first))
    def _():
        ref[...] += val


def _sds(shape, dtype):
    return jax.ShapeDtypeStruct(tuple(shape), dtype)


def _spec(shape, fn):
    return pl.BlockSpec(tuple(shape), fn)


class _Task:
    def __init__(self, peers, operands, out_shapes, aliases, nsem, make, finish, make_second=None):
        self.peers = peers
        self.operands, self.out_shapes, self.aliases = operands, out_shapes, aliases
        self.nsem, self.make, self.finish = nsem, make, finish
        self.make_second = make_second


class _Plan:
    def __init__(self):
        self.tasks, self.after = {}, {}


_plan = None


_CHIP_PEER_SETS = [frozenset({"chips"}), frozenset({"first"}), frozenset({"second"}), frozenset({"first", "second"})]
PEER_SET_COLLECTIVE_ID = {frozenset({"sibling"}): 1}
for _i, _chip_peers in enumerate(_CHIP_PEER_SETS):
    PEER_SET_COLLECTIVE_ID[_chip_peers] = 2 + 2 * _i
    PEER_SET_COLLECTIVE_ID[_chip_peers | {"sibling"}] = 3 + 2 * _i


def _peer_set(names):
    names = frozenset(n for name in names for n in name.split("+"))
    return names - {"first", "second"} if "chips" in names else names


def _axis_neighbours(x, y, c):
    flip = lambda v, f: v + f * (1 - 2 * v)
    return (flip(x, 1 - c), flip(y, c)), (flip(x, c), flip(y, 1 - c))


def _entry_handshake(peer_set):
    x, y, c, chips = _mesh_position()
    first, second = _axis_neighbours(x, y, c)
    peers = [(x, y, 1 - c)] if "sibling" in peer_set else []
    if "chips" in peer_set:
        peers += [(*chip, c) for chip in chips]
    if "first" in peer_set:
        peers.append((*first, c))
    if "second" in peer_set:
        peers.append((*second, c))
    barrier = pltpu.get_barrier_semaphore()
    for peer in peers:
        pl.semaphore_signal(barrier, inc=1, device_id=peer, device_id_type=MESH)
    pl.semaphore_wait(barrier, len(peers))


def _pcall(body, *, name, grid, in_specs, out_specs, out_shape, scratch_shapes=(), num_prefetch=0, own_peers=()):
    single = not isinstance(out_shape, (list, tuple))
    out_shape = [out_shape] if single else list(out_shape)
    out_specs = [out_specs] if single else list(out_specs)
    in_specs = list(in_specs)
    scratch_shapes = list(scratch_shapes)
    tasks = _plan.tasks.pop(name, []) if _plan is not None else []
    after = _plan.after.pop(name, []) if _plan is not None else []
    peer_set = _peer_set([t.peers for t in tasks] + list(own_peers))
    nax = len(grid)

    def run(*operands):
        n_in = len(operands) - num_prefetch
        n_out = len(out_shape)
        t_ops = [t.operands() for t in tasks]
        t_outs = [t.out_shapes() for t in tasks]
        c_ops = [a for ops in t_ops for a in ops]
        c_outs = [s for outs in t_outs for s in outs]
        aliases = {}
        i0, o0 = num_prefetch + n_in, n_out
        for t, ops, outs in zip(tasks, t_ops, t_outs):
            for i_loc, o_loc in t.aliases.items():
                aliases[i0 + i_loc] = o0 + o_loc
            i0 += len(ops)
            o0 += len(outs)
        nsem = sum(t.nsem for t in tasks)

        def wrapped(*refs):
            p = num_prefetch
            pre, ins = refs[:p], refs[p:p + n_in]
            cins = refs[p + n_in:p + n_in + len(c_ops)]
            q = p + n_in + len(c_ops)
            outs, couts = refs[q:q + n_out], refs[q + n_out:q + n_out + len(c_outs)]
            q += n_out + len(c_outs)
            scr = refs[q:q + len(scratch_shapes)]

            def rounds(second):
                send_sems, recv_sems = refs[q + len(scratch_shapes):]
                out = []
                ci = co = so = 0
                for t, ops, souts in zip(tasks, t_ops, t_outs):
                    make = t.make_second if second else t.make
                    out.append(([], []) if make is None else
                               make(cins[ci:ci + len(ops)], couts[co:co + len(souts)],
                                    functools.partial(lambda base, k: send_sems.at[base + k], so),
                                    functools.partial(lambda base, k: recv_sems.at[base + k], so)))
                    ci, co, so = ci + len(ops), co + len(souts), so + t.nsem
                return out

            two_rounds = [t.make_second is not None for t in tasks]
            if peer_set:
                ids = [pl.program_id(k) for k in range(nax)]
                first = functools.reduce(jnp.logical_and, [i == 0 for i in ids])
                last = functools.reduce(jnp.logical_and, [i == g - 1 for i, g in zip(ids, grid)])
                step = functools.reduce(lambda acc, ig: acc * ig[1] + ig[0], zip(ids, grid), 0)
                middle = step == math.prod(grid) // 3

                @pl.when(first)
                def _():
                    _entry_handshake(peer_set)
                    for starts, _ in rounds(False):
                        for copy in starts:
                            copy().start()

            body(*pre, *ins, *outs, *scr)

            if any(two_rounds):
                @pl.when(middle)
                def _():
                    for (_, arrivals), two in zip(rounds(False), two_rounds):
                        if two:
                            for arrival in arrivals:
                                arrival().wait_recv()
                    for starts, _ in rounds(True):
                        for copy in starts:
                            copy().start()

            if tasks:
                @pl.when(last)
                def _():
                    first_round, second_round = rounds(False), rounds(True)
                    for (_, arrivals1), (_, arrivals2), two in zip(first_round, second_round, two_rounds):
                        for arrival in (arrivals2 if two else arrivals1):
                            arrival().wait_recv()
                    for starts, _ in first_round + second_round:
                        for copy in starts:
                            copy().wait_send()

        sems = [pltpu.SemaphoreType.DMA((nsem,)), pltpu.SemaphoreType.DMA((nsem,))] if tasks else []
        res = pl.pallas_call(
            wrapped, name=name,
            grid_spec=pltpu.PrefetchScalarGridSpec(
                num_scalar_prefetch=num_prefetch, grid=tuple(grid),
                in_specs=in_specs + [ANY_SPEC] * len(c_ops),
                out_specs=out_specs + [ANY_SPEC] * len(c_outs),
                scratch_shapes=scratch_shapes + sems),
            out_shape=out_shape + c_outs,
            input_output_aliases=aliases,
            compiler_params=_cparams(nax, PEER_SET_COLLECTIVE_ID[peer_set] if peer_set else None),
        )(*operands, *c_ops)
        co = n_out
        for t, souts in zip(tasks, t_outs):
            t.finish(res[co:co + len(souts)])
            co += len(souts)
        for fn in after:
            fn()
        return res[0] if single else list(res[:n_out])

    return run


def _comm_call(name):
    def body(o_ref):
        o_ref[...] = jnp.zeros_like(o_ref)

    _pcall(body, name=name, grid=(1,), in_specs=[], out_specs=_spec((8, 128), lambda i: (0, 0)),
           out_shape=_sds((8, 128), F32))()


def _gemm(name, terms, grid, outs, acc_shape, extras=(), epilogue=None):
    kinds = [t[4] for t in terms]
    nt, ne, no = len(terms), len(extras), len(outs)
    nred = grid[-1]
    nax = len(grid)

    def body(*refs):
        trefs = refs[:2 * nt]
        erefs = refs[2 * nt:2 * nt + ne]
        orefs = refs[2 * nt + ne:2 * nt + ne + no]
        ids = [pl.program_id(k) for k in range(nax)]
        tot = None
        for t in range(nt):
            d = _dot(trefs[2 * t][...], trefs[2 * t + 1][...], kinds[t])
            tot = d if tot is None else tot + d

        def finish(acc):
            if epilogue is None:
                orefs[0][...] = acc.astype(orefs[0].dtype)
            else:
                epilogue(acc, erefs, orefs, ids)

        if nred == 1:
            finish(tot)
        else:
            acc_ref = refs[-1]
            r = ids[-1]

            @pl.when(r == 0)
            def _():
                acc_ref[...] = tot

            @pl.when(r > 0)
            def _():
                acc_ref[...] += tot

            @pl.when(r == nred - 1)
            def _():
                finish(acc_ref[...])

    operands, in_specs = [], []
    for a, a_spec, b, b_spec, _ in terms:
        operands += [a, b]
        in_specs += [a_spec, b_spec]
    for e, e_spec in extras:
        operands.append(e)
        in_specs.append(e_spec)
    scratch = [pltpu.VMEM(tuple(acc_shape), F32)] if nred > 1 else []
    return _pcall(body, name=name, grid=tuple(grid), in_specs=in_specs, out_specs=[o[1] for o in outs],
                  out_shape=[o[0] for o in outs], scratch_shapes=scratch)(*operands)


def _rowwise(name, fn, ins, outs, grid):
    ni = len(ins)
    nax = len(grid)

    def body(*refs):
        ids = [pl.program_id(k) for k in range(nax)]
        fn(refs[:ni], refs[ni:], ids)

    return _pcall(body, name=name, grid=tuple(grid), in_specs=[i[1] for i in ins],
                  out_specs=[o[1] for o in outs], out_shape=[o[0] for o in outs])(*[i[0] for i in ins])


def _ffn_up(name, h, w1buf, w1_idx, w3buf, w3_idx, norm_gain=None):
    T = h.shape[0]
    tm = min(FFN_ROW_TILE, T)
    normed = norm_gain is not None

    def body(h_ref, *refs):
        if normed:
            g_ref, w1_ref, w3_ref, a_ref, b_ref, s_ref, hn_ref = refs
            hv = _rms_fwd(h_ref[...], g_ref[...]).astype(BF16)
            hn_ref[...] = hv
        else:
            w1_ref, w3_ref, a_ref, b_ref, s_ref = refs
            hv = h_ref[...]
        a = _dot(hv, w1_ref[...], "nt")
        b = _dot(hv, w3_ref[...], "nt")
        a_ref[...] = a.astype(BF16)
        b_ref[...] = b.astype(BF16)
        s_ref[...] = ((a * _sigmoid(a)) * b).astype(BF16)

    row = _spec((tm, D), lambda i: (i, 0))
    blk = _spec((tm, D_FF), lambda i: (i, 0))
    return _pcall(
        body, name=name, grid=(T // tm,),
        in_specs=[row] + ([_spec((1, D), lambda i: (0, 0))] if normed else [])
        + [_spec((N_CHIPS, None, FF_BLK, D), lambda i: (0, w1_idx, 0, 0)),
           _spec((N_CHIPS, None, FF_BLK, D), lambda i: (0, w3_idx, 0, 0))],
        out_specs=[blk, blk, blk] + ([row] if normed else []),
        out_shape=[_sds((T, D_FF), BF16)] * 3 + ([_sds((T, D), BF16)] if normed else []),
    )(h, *([norm_gain] if normed else []), w1buf, w3buf)


def _loss_head(x, g, tgt, loss_ref, dx_ref, dg_ref, first):
    err = _rms_fwd(x, g) - tgt
    lp = 0.5 * jnp.sum(jnp.mean(err * err, axis=-1, keepdims=True), axis=0, keepdims=True)
    _accumulate(loss_ref, jnp.broadcast_to(lp, (1, 128)), first)
    dx, dgp = _rms_bwd(x, g, err * (1.0 / D))
    dx_ref[...] = dx
    _accumulate(dg_ref, dgp, first)


def _ffn_down(name, s, wrow2, w2_idx, x_res, g_next=None, loss_target=None):
    T = x_res.shape[0]
    tm = min(ROW_TILE, T)
    row = lambda i, j, r: (i, 0)
    vec = lambda i, j, r: (0, 0)

    def epilogue(acc, erefs, orefs, ids):
        xo = erefs[0][...] + 0.5 * acc
        if loss_target is not None:
            _loss_head(xo, erefs[1][...], erefs[2][...], orefs[0], orefs[1], orefs[2], ids[0] == 0)
            return
        orefs[0][...] = xo
        orefs[1][...] = _rms_fwd(xo, erefs[1][...]).astype(BF16)

    extras = [(x_res, _spec((tm, D), row)), (g_next, _spec((1, D), vec))]
    if loss_target is None:
        outs = [(_sds((T, D), F32), _spec((tm, D), row)), (_sds((T, D), BF16), _spec((tm, D), row))]
    else:
        extras.append((loss_target, _spec((tm, D), row)))
        outs = [(_sds((1, 128), F32), _spec((1, 128), vec)), (_sds((T, D), F32), _spec((tm, D), row)),
                (_sds((1, D), F32), _spec((1, D), vec))]
    return _gemm(
        name,
        [(s, _spec((tm, D_FF), row),
          wrow2, _spec((N_CHIPS, None, FF_BLK, D), lambda i, j, r: (0, w2_idx, 0, 0)), "nn")],
        (T // tm, 1, 1), outs, (tm, D), extras, epilogue)


def _ffn_bwd_mid(name, dx, wrow2, w2_idx, a, b):
    T = dx.shape[0]
    tm = min(FFN_ROW_TILE, T)

    def body(dx_ref, w2_ref, a_ref, b_ref, dab_ref):
        ds = _dot(0.5 * dx_ref[...], w2_ref[...], "nt")
        av = a_ref[...].astype(F32)
        sg = _sigmoid(av)
        dab_ref[0] = (ds * b_ref[...].astype(F32) * (sg * (1.0 + av * (1.0 - sg)))).astype(BF16)
        dab_ref[1] = (ds * (av * sg)).astype(BF16)

    blk = _spec((tm, D_FF), lambda i: (i, 0))
    return _pcall(
        body, name=name, grid=(T // tm,),
        in_specs=[_spec((tm, D), lambda i: (i, 0)),
                  _spec((N_CHIPS, None, FF_BLK, D), lambda i: (0, w2_idx, 0, 0)),
                  blk, blk],
        out_specs=_spec((2, tm, D_FF), lambda i: (0, i, 0)),
        out_shape=_sds((2, T, D_FF), BF16),
    )(dx, wrow2, a, b)


def _rms_bwd_epilogue(acc, erefs, orefs, ids):
    dx, dgp = _rms_bwd(erefs[0][...], erefs[1][...], acc)
    orefs[0][...] = dx + erefs[2][...]
    _accumulate(orefs[1], dgp, ids[0] == 0)


def _rms_bwd_io(x, g, dres, T, tm):
    row = lambda i, j, r: (i, 0)
    vec = lambda i, j, r: (0, 0)
    extras = [(x, _spec((tm, D), row)), (g, _spec((1, D), vec)), (dres, _spec((tm, D), row))]
    outs = [(_sds((T, D), F32), _spec((tm, D), row)), (_sds((1, D), F32), _spec((1, D), vec))]
    return extras, outs


def _ffn_bwd(tag, dx_out, h, a, b, s, w1buf, w1_idx, w3buf, w3_idx, wrow2, w2_idx, x_in, g, big):
    T = dx_out.shape[0]
    dab = _ffn_bwd_mid(tag + "_bwd_mid", dx_out, wrow2, w2_idx, a, b)

    def half_scale(acc, erefs, orefs, ids):
        orefs[0][...] = (0.5 * acc).astype(orefs[0].dtype)

    dw_grid = (D_FF // DW_BLK, 1, 1)
    dw_out = [(_sds((D_FF, D), GRAD_WIRE_DTYPE), _spec((DW_BLK, D), lambda j, n, r: (j, 0)))]
    tokens = _spec((T, D), lambda j, n, r: (0, 0))
    big[tag + "_w2"] = _gemm(
        tag + "_dw2", [(s, _spec((T, DW_BLK), lambda j, n, r: (0, j)), dx_out, tokens, "tn")],
        dw_grid, dw_out, (DW_BLK, D), (), half_scale)[0].reshape(1, N_CHIPS, FF_BLK, D)
    for widx, wname in ((0, "_w1"), (1, "_w3")):
        big[tag + wname] = _gemm(
            tag + "_d" + wname[1:],
            [(dab, _spec((None, T, DW_BLK), functools.partial(lambda w, j, n, r: (w, 0, j), widx)), h, tokens, "tn")],
            dw_grid, dw_out, (DW_BLK, D))[0].reshape(1, N_CHIPS, FF_BLK, D)
    tm = min(FFN_ROW_TILE, T)
    extras, outs = _rms_bwd_io(x_in, g, dx_out, T, tm)
    whole = lambda idx: _spec((N_CHIPS, None, FF_BLK, D), lambda i, j, r: (0, idx, 0, 0))
    dx_in, dg = _gemm(
        tag + "_dh",
        [(dab, _spec((None, tm, D_FF), lambda i, j, r: (0, i, 0)), w1buf, whole(w1_idx), "nn"),
         (dab, _spec((None, tm, D_FF), lambda i, j, r: (1, i, 0)), w3buf, whole(w3_idx), "nn")],
        (T // tm, 1, 1), outs, (tm, D), extras, _rms_bwd_epilogue)
    return dx_in, dg


def _proj_sq(name, a, wsq, idx, kind, out_dtype=F32, extras=(), epilogue=None, outs=None):
    M = a.shape[0]
    tm = min(ROW_TILE, M)
    if outs is None:
        outs = [(_sds((M, D), out_dtype), _spec((tm, D), lambda i, j, r: (i, 0)))]
    return _gemm(
        name,
        [(a, _spec((tm, D), lambda i, j, r: (i, 0)),
          wsq, _spec((N_CHIPS, None, SQ_BLK, D), lambda i, j, r: (0, idx, 0, 0)), kind)],
        (M // tm, 1, 1), outs, (tm, D), extras, epilogue)


def _dw_sq(name, a, b):
    M = a.shape[0]
    tn = D // 2
    whole = _gemm(
        name,
        [(a, _spec((M, D), lambda i, j, r: (0, 0)), b, _spec((M, tn), lambda i, j, r: (0, j)), "tn")],
        (1, D // tn, 1),
        [(_sds((D, D), GRAD_WIRE_DTYPE), _spec((D, tn), lambda i, j, r: (0, j)))],
        (D, tn))[0]
    return whole.reshape(N_CHIPS, SQ_BLK, D)


def _retention_constants(T):
    pos = jnp.arange(T, dtype=F32)
    inv_freq = ROPE_BASE ** (-jnp.arange(0, RET_DK, 2, dtype=F32) / RET_DK)
    ang = pos[:, None] * inv_freq[None, :]
    cosf = jnp.concatenate([jnp.cos(ang), jnp.cos(ang)], axis=1)
    sins = jnp.concatenate([-jnp.sin(ang), jnp.sin(ang)], axis=1)
    lg = jnp.log(1.0 - 2.0 ** (-5.0 - jnp.arange(RET_HEADS, dtype=F32)))
    p = jnp.arange(CHUNK, dtype=F32)
    rel = p[:, None] - p[None, :]
    dmat = jnp.where(rel[None] >= 0, jnp.exp(rel[None] * lg[:, None, None]), 0.0)
    kd = jnp.exp((CHUNK - 1.0 - p)[None, :] * lg[:, None])[:, :, None]
    qd = jnp.exp((p + 1.0)[None, :] * lg[:, None])[:, :, None]
    cd = jnp.exp(CHUNK * lg)[:, None, None]
    return cosf, sins, dmat, kd, qd, cd


def _rot(t, cosv, sinv):
    return t * cosv + pltpu.roll(t, RET_DK // 2, 1) * sinv


def _unrot(t, cosv, sinv):
    return t * cosv - pltpu.roll(t, RET_DK // 2, 1) * sinv


def _ret_const_specs(cm):
    whole = lambda shape: _spec(shape, lambda c: (0,) * len(shape))
    return [
        _spec((RET_STEP_ROWS, RET_DK), lambda c: (cm(c), 0)),
        _spec((RET_STEP_ROWS, RET_DK), lambda c: (cm(c), 0)),
        whole((RET_HEADS, CHUNK, CHUNK)), whole((RET_HEADS, CHUNK, 1)), whole((RET_HEADS, CHUNK, 1)),
        whole((RET_HEADS, 1, 1)),
    ]


def _head(h, width):
    return slice(h * width, (h + 1) * width)


def _ret_fwd(u, consts, ret_gn):
    T = u.shape[0]
    nC = T // CHUNK
    kscale = RET_DK ** -0.5

    def body(q_ref, k_ref, v_ref, g_ref, cos_ref, sin_ref, dm_ref, kd_ref, qd_ref, cd_ref, gn_ref,
             qr_ref, kr_ref, ret_ref, yr_ref, st_ref, state):
        @pl.when(pl.program_id(0) == 0)
        def _():
            state[...] = jnp.zeros_like(state)

        for cc in range(RET_STEP_CHUNKS):
            rows = slice(cc * CHUNK, (cc + 1) * CHUNK)
            cosv, sinv = cos_ref[rows, :], sin_ref[rows, :]
            for h in range(RET_HEADS):
                hk, hv = _head(h, RET_DK), _head(h, RET_DV)
                q = _rot(q_ref[rows, hk], cosv, sinv)
                k = _rot(k_ref[rows, hk], cosv, sinv) * kscale
                v = v_ref[rows, hv]
                qr_ref[rows, hk] = q
                kr_ref[rows, hk] = k
                prev = state[h]
                st_ref[h, cc] = prev
                s = _dot(q, k, "nt") * dm_ref[h]
                ret = _dot(s, v, "nn") + _dot(q, prev, "nn") * qd_ref[h]
                state[h] = cd_ref[h] * prev + _dot(k * kd_ref[h], v, "tn")
                ret_ref[rows, hv] = ret
                mu = jnp.mean(ret, axis=-1, keepdims=True)
                xc = ret - mu
                yn = xc * lax.rsqrt(jnp.mean(xc * xc, axis=-1, keepdims=True) + EPS)
                g = g_ref[rows, hv]
                yr_ref[rows, hv] = ((g * _sigmoid(g)) * (yn * gn_ref[:, hv])).astype(BF16)

    cm = lambda c: c
    qk_w, v_w = RET_HEADS * RET_DK, RET_HEADS * RET_DV
    in_specs = [
        _spec((RET_STEP_ROWS, qk_w), lambda c: (c, 0)), _spec((RET_STEP_ROWS, qk_w), lambda c: (c, 1)),
        _spec((RET_STEP_ROWS, v_w), lambda c: (c, 1)), _spec((RET_STEP_ROWS, v_w), lambda c: (c, 2)),
    ] + _ret_const_specs(cm) + [_spec((1, v_w), lambda c: (0, 0))]
    qk_out = _spec((RET_STEP_ROWS, qk_w), lambda c: (c, 0))
    v_out = _spec((RET_STEP_ROWS, v_w), lambda c: (c, 0))
    return _pcall(
        body, name="ret_fwd", grid=(nC // RET_STEP_CHUNKS,),
        in_specs=in_specs,
        out_specs=[qk_out, qk_out, v_out, v_out,
                   _spec((RET_HEADS, RET_STEP_CHUNKS, RET_DK, RET_DV), lambda c: (0, c, 0, 0))],
        out_shape=[_sds((T, qk_w), F32), _sds((T, qk_w), F32), _sds((T, v_w), F32), _sds((T, v_w), BF16),
                   _sds((RET_HEADS, nC, RET_DK, RET_DV), F32)],
        scratch_shapes=[pltpu.VMEM((RET_HEADS, RET_DK, RET_DV), F32)],
    )(u, u, u, u, *consts, ret_gn)


def _ret_bwd(dyr, ret, u, qr, kr, states, consts, ret_gn):
    T = u.shape[0]
    nC = T // CHUNK
    kscale = RET_DK ** -0.5

    def body(dyr_ref, ret_ref, g_ref, q_ref, k_ref, v_ref, st_ref,
             cos_ref, sin_ref, dm_ref, kd_ref, qd_ref, cd_ref, gn_ref,
             dq_ref, dk_ref, dv_ref, dg_ref, dgn_ref, gstate):
        first = pl.program_id(0) == 0

        @pl.when(first)
        def _():
            gstate[...] = jnp.zeros_like(gstate)

        dgn_total = None
        for cc in reversed(range(RET_STEP_CHUNKS)):
            rows = slice(cc * CHUNK, (cc + 1) * CHUNK)
            cosv, sinv = cos_ref[rows, :], sin_ref[rows, :]
            dgn_parts = []
            for h in range(RET_HEADS):
                hk, hv = _head(h, RET_DK), _head(h, RET_DV)
                ret = ret_ref[rows, hv]
                mu = jnp.mean(ret, axis=-1, keepdims=True)
                xc = ret - mu
                rs = lax.rsqrt(jnp.mean(xc * xc, axis=-1, keepdims=True) + EPS)
                yn = xc * rs
                gn = gn_ref[:, hv]
                g = g_ref[rows, hv]
                sg = _sigmoid(g)
                dyr_v = dyr_ref[rows, hv]
                dretn = dyr_v * (g * sg)
                dg_ref[rows, hv] = (dyr_v * (yn * gn) * (sg * (1.0 + g * (1.0 - sg)))).astype(BF16)
                dgn_parts.append(jnp.sum(dretn * yn, axis=0, keepdims=True))
                dyn = dretn * gn
                d_o = rs * (dyn - jnp.mean(dyn, axis=-1, keepdims=True)
                            - yn * jnp.mean(dyn * yn, axis=-1, keepdims=True))

                q, k, v = q_ref[rows, hk], k_ref[rows, hk], v_ref[rows, hv]
                dmat, kd, qd = dm_ref[h], kd_ref[h], qd_ref[h]
                prev = st_ref[h, cc]
                gnext = gstate[h]
                s = _dot(q, k, "nt") * dmat
                ds = _dot(d_o, v, "nt") * dmat
                doq = d_o * qd
                dq = _dot(ds, k, "nn") + _dot(doq, prev, "nt")
                dk = _dot(ds, q, "tn") + _dot(v, gnext, "nt") * kd
                dv = _dot(s, d_o, "tn") + _dot(k * kd, gnext, "nn")
                gstate[h] = cd_ref[h] * gnext + _dot(q, doq, "tn")
                dq_ref[rows, hk] = _unrot(dq, cosv, sinv).astype(BF16)
                dk_ref[rows, hk] = _unrot(dk * kscale, cosv, sinv).astype(BF16)
                dv_ref[rows, hv] = dv.astype(BF16)
            dgn = jnp.concatenate(dgn_parts, axis=1)
            dgn_total = dgn if dgn_total is None else dgn_total + dgn
        _accumulate(dgn_ref, dgn_total, first)

    n_steps = nC // RET_STEP_CHUNKS
    cm = lambda c: n_steps - 1 - c
    qk_w, v_w = RET_HEADS * RET_DK, RET_HEADS * RET_DV
    vspec = lambda blk: _spec((RET_STEP_ROWS, v_w), lambda c: (cm(c), blk))
    qspec = _spec((RET_STEP_ROWS, qk_w), lambda c: (cm(c), 0))
    in_specs = [vspec(0), vspec(0), vspec(2), qspec, qspec, vspec(1),
                _spec((RET_HEADS, RET_STEP_CHUNKS, RET_DK, RET_DV), lambda c: (0, cm(c), 0, 0)),
                ] + _ret_const_specs(cm) + [_spec((1, v_w), lambda c: (0, 0))]
    return _pcall(
        body, name="ret_bwd", grid=(n_steps,),
        in_specs=in_specs,
        out_specs=[qspec, qspec, vspec(0), vspec(0), _spec((1, v_w), lambda c: (0, 0))],
        out_shape=[_sds((T, qk_w), BF16), _sds((T, qk_w), BF16), _sds((T, v_w), BF16), _sds((T, v_w), BF16),
                   _sds((1, v_w), F32)],
        scratch_shapes=[pltpu.VMEM((RET_HEADS, RET_DK, RET_DV), F32)],
    )(dyr, ret, u, qr, kr, u, states, *consts, ret_gn)


def _shift_down(x, s):
    rows = lax.broadcasted_iota(jnp.int32, x.shape, 0)
    return jnp.where(rows >= s, pltpu.roll(x, s, 0), 0.0)


def _shift_up(x, s):
    n = x.shape[0]
    rows = lax.broadcasted_iota(jnp.int32, x.shape, 0)
    return jnp.where(rows < n - s, pltpu.roll(x, n - s, 0), 0.0)


def _lru_specs(T):
    col = lambda off: _spec((T, LRU_BLOCK), lambda g: (0, off + g))
    vec = _spec((1, LRU_BLOCK), lambda g: (0, g))
    wblk = _spec((None, LRU_BLOCK, LRU_BLOCK), lambda g: (g, 0, 0))
    cw = _spec((CONV_TAPS, LRU_BLOCK), lambda g: (0, g))
    return col, vec, wblk, cw


def _lru_gates_fwd(u, conv_w, conv_b, w_r, b_r, w_i, b_i, lam):
    T = u.shape[0]
    col, vec, wblk, cw = _lru_specs(T)

    def body(x_ref, cw_ref, cb_ref, wr_ref, br_ref, wi_ref, bi_ref, lam_ref,
             xc_ref, r_ref, i_ref, a_ref, bx_ref):
        x = x_ref[...]
        w = cw_ref[...]
        xc = (_shift_down(x, 3) * w[0:1] + _shift_down(x, 2) * w[1:2] + _shift_down(x, 1) * w[2:3]
              + x * w[3:4] + cb_ref[...])
        r = _sigmoid(_dot(xc, wr_ref[...], "nn") + br_ref[...])
        i = _sigmoid(_dot(xc, wi_ref[...], "nn") + bi_ref[...])
        la = (-LRU_C) * r * _softplus(-lam_ref[...])
        xc_ref[...] = xc
        r_ref[...] = r
        i_ref[...] = i
        a_ref[...] = jnp.exp(la)
        bx_ref[...] = jnp.sqrt(-_expm1(2.0 * la)) * (i * xc)

    out = col(0)
    return _pcall(
        body, name="lru_gates_fwd", grid=(LRU_BLOCKS,),
        in_specs=[col(24), cw, vec, wblk, vec, wblk, vec, vec],
        out_specs=[out] * 5,
        out_shape=[_sds((T, D), F32)] * 5,
    )(u, conv_w, conv_b, w_r, b_r, w_i, b_i, lam)


def _lru_scan(name, a3, b3, reverse):
    T = a3.shape[0]
    nt = T // SCAN_TILE
    unroll = 8

    def body(a_ref, b_ref, o_ref, carry):
        @pl.when(pl.program_id(0) == 0)
        def _():
            carry[...] = jnp.zeros_like(carry)

        if not reverse:
            def step(t, h):
                h = a_ref[t] * h + b_ref[t]
                o_ref[t] = h
                return h
        else:
            def step(k, c):
                t = SCAN_TILE - 1 - k
                l = b_ref[t] + c
                o_ref[t] = l
                return a_ref[t] * l
        carry[...] = lax.fori_loop(0, SCAN_TILE, step, carry[...], unroll=unroll)

    idx = (lambda i: (nt - 1 - i, 0, 0)) if reverse else (lambda i: (i, 0, 0))
    blk = _spec((SCAN_TILE, LRU_BLOCKS, LRU_BLOCK), idx)
    return _pcall(
        body, name=name, grid=(nt,),
        in_specs=[blk, blk], out_specs=blk,
        out_shape=_sds((T, LRU_BLOCKS, LRU_BLOCK), F32),
        scratch_shapes=[pltpu.VMEM((LRU_BLOCKS, LRU_BLOCK), F32)],
    )(a3, b3)


def _lru_gates_bwd(lmb, hl, a, r, i, xc, u, conv_w, w_r, w_i, lam):
    T = u.shape[0]
    col, vec, wblk, cw = _lru_specs(T)

    def body(l_ref, h_ref, a_ref, r_ref, i_ref, xc_ref, x_ref, cw_ref, wr_ref, wi_ref, lam_ref,
             dx_ref, dwr_ref, dwi_ref, dvec_ref, dcw_ref):
        l = l_ref[...]
        av, rv, iv, xc = a_ref[...], r_ref[...], i_ref[...], xc_ref[...]
        lam_v = lam_ref[...]
        sp = _softplus(-lam_v)
        la = (-LRU_C) * rv * sp
        mult = jnp.sqrt(-_expm1(2.0 * la))
        da = l * _shift_down(h_ref[...], 1)
        dmult = l * (iv * xc)
        di = l * mult * xc
        dxc = l * mult * iv
        dla = da * av - dmult * (av * av) / mult
        dzr = (dla * ((-LRU_C) * sp)) * rv * (1.0 - rv)
        dzi = di * iv * (1.0 - iv)
        dsp = jnp.sum(dla * ((-LRU_C) * rv), axis=0, keepdims=True)
        dlam = dsp * (-_sigmoid(-lam_v))
        dwr_ref[...] = _dot(xc, dzr, "tn")
        dwi_ref[...] = _dot(xc, dzi, "tn")
        dxc = dxc + _dot(dzr, wr_ref[...], "nt") + _dot(dzi, wi_ref[...], "nt")
        x = x_ref[...]
        w = cw_ref[...]
        dx = (dxc * w[3:4] + _shift_up(dxc, 1) * w[2:3] + _shift_up(dxc, 2) * w[1:2]
              + _shift_up(dxc, 3) * w[0:1])
        dx_ref[...] = dx.astype(BF16)
        dvec_ref[...] = jnp.concatenate(
            [jnp.sum(dzr, axis=0, keepdims=True), jnp.sum(dzi, axis=0, keepdims=True), dlam,
             jnp.sum(dxc, axis=0, keepdims=True)], axis=0)
        dcw_ref[...] = jnp.concatenate(
            [jnp.sum(dxc * _shift_down(x, 3 - tap), axis=0, keepdims=True) if tap < 3
             else jnp.sum(dxc * x, axis=0, keepdims=True) for tap in range(CONV_TAPS)], axis=0)

    c0 = col(0)
    return _pcall(
        body, name="lru_gates_bwd", grid=(LRU_BLOCKS,),
        in_specs=[c0, c0, c0, c0, c0, c0, col(24), cw, wblk, wblk, vec],
        out_specs=[c0, wblk, wblk, cw, cw],
        out_shape=[_sds((T, D), BF16), _sds((LRU_BLOCKS, LRU_BLOCK, LRU_BLOCK), F32),
                   _sds((LRU_BLOCKS, LRU_BLOCK, LRU_BLOCK), F32), _sds((4, D), F32), _sds((CONV_TAPS, D), F32)],
    )(lmb, hl, a, r, i, xc, u, conv_w, w_r, w_i, lam)


def _xattn_probs(q, k):
    sc = _dot(q, k, "nt") * (X_HD ** -0.5)
    e = jnp.exp(sc - jnp.max(sc, axis=-1, keepdims=True))
    return e / jnp.sum(e, axis=-1, keepdims=True)


def _xattn_fwd(xq, xk, xv):
    T = xq.shape[0]
    tq = min(WIDE_ROW_TILE, T)
    M = xk.shape[0]

    def body(q_ref, k_ref, v_ref, o_ref):
        p = _xattn_probs(q_ref[...], k_ref[...])
        o_ref[...] = _dot(p, v_ref[...], "nn").astype(BF16)

    qs = _spec((tq, X_HD), lambda h, i: (i, h))
    kv = _spec((M, X_HD), lambda h, i: (0, h))
    return _pcall(
        body, name="xattn_fwd", grid=(X_HEADS, T // tq),
        in_specs=[qs, kv, kv], out_specs=qs, out_shape=_sds((T, D), BF16),
    )(xq, xk, xv)


def _xattn_bwd(xq, xk, xv, dxo):
    T = xq.shape[0]
    tq = min(WIDE_ROW_TILE, T)
    M = xk.shape[0]

    def body(q_ref, k_ref, v_ref, do_ref, dq_ref, dk_ref, dv_ref):
        first = pl.program_id(1) == 0
        q, k, v, do = q_ref[...], k_ref[...], v_ref[...], do_ref[...]
        p = _xattn_probs(q, k)
        dp = _dot(do, v, "nt")
        ds = p * (dp - jnp.sum(dp * p, axis=-1, keepdims=True)) * (X_HD ** -0.5)
        dq_ref[...] = _dot(ds, k, "nn").astype(BF16)
        _accumulate(dk_ref, _dot(ds, q, "tn"), first)
        _accumulate(dv_ref, _dot(p, do, "tn"), first)

    qs = _spec((tq, X_HD), lambda h, i: (i, h))
    kv = _spec((M, X_HD), lambda h, i: (0, h))
    return _pcall(
        body, name="xattn_bwd", grid=(X_HEADS, T // tq),
        in_specs=[qs, kv, kv, qs], out_specs=[qs, kv, kv],
        out_shape=[_sds((T, D), BF16), _sds((M, D), F32), _sds((M, D), F32)],
    )(xq, xk, xv, dxo)


def _adamw(name, w, g, m, v):
    R, C = w.shape
    tr = R
    for cand in (512, 352, 256):
        if R % cand == 0:
            tr = cand
            break

    def fn(irefs, orefs, ids):
        delta, mn, vn = _adamw_update(*(r[...] for r in irefs))
        orefs[0][...] = delta
        orefs[1][...] = mn
        orefs[2][...] = vn

    blk = _spec((tr, C), lambda i: (i, 0))
    return _rowwise(name, fn, [(w, blk), (g, blk), (m, blk), (v, blk)],
                    [(_sds((R, C), F32), blk)] * 3, (R // tr,))


def _adamw_update(wv, gv, mv, vv):
    c1 = 1.0 - ADAM_B1 ** ADAM_STEP
    c2 = 1.0 - ADAM_B2 ** ADAM_STEP
    mn = ADAM_B1 * mv + (1.0 - ADAM_B1) * gv
    vn = ADAM_B2 * vv + (1.0 - ADAM_B2) * (gv * gv)
    delta = -ADAM_LR * ((mn / c1) / (jnp.sqrt(vn / c2) + ADAM_EPS) + ADAM_WD * wv)
    return delta, mn, vn


def _adamw_halves(name, w, mine, theirs, widx, m, v, core):
    R, C = w.shape
    H = R // 2
    tr = H
    while tr * C * 4 > (1 << 20) and tr % 16 == 0:
        tr //= 2
    nb = H // tr

    def body(core_ref, w_ref, mine_ref, theirs_ref, m_ref, v_ref, g_out, d_out, m_out, v_out):
        gv = jnp.where(pl.program_id(0) == core_ref[0], mine_ref[...], theirs_ref[...])
        delta, mn, vn = _adamw_update(w_ref[...], gv, m_ref[...], v_ref[...])
        g_out[...] = gv
        d_out[...] = delta
        m_out[...] = mn
        v_out[...] = vn

    full = pl.BlockSpec((tr, C), lambda h, i, core_ref: (h * nb + i, 0))
    mine_spec = pl.BlockSpec((None, tr, C), lambda h, i, core_ref: (widx, jnp.where(h == core_ref[0], i, 0), 0))
    theirs_spec = pl.BlockSpec((None, tr, C), lambda h, i, core_ref: (widx, jnp.where(h == core_ref[0], 0, i), 0))
    return _pcall(
        body, name=name, grid=(2, nb), num_prefetch=1,
        in_specs=[full, mine_spec, theirs_spec, full, full], out_specs=[full] * 4,
        out_shape=[_sds((R, C), F32)] * 4,
    )(core, w, mine, theirs, m, v)


def _rmsnorm(name, x, g):
    M = x.shape[0]
    tm = min(ROW_TILE, M)

    def fn(irefs, orefs, ids):
        orefs[0][...] = _rms_fwd(irefs[0][...], irefs[1][...]).astype(BF16)

    row = _spec((tm, D), lambda i: (i, 0))
    return _rowwise(name, fn, [(x, row), (g, _spec((1, D), lambda i: (0, 0)))],
                    [(_sds((M, D), BF16), row)], (M // tm,))[0]


WEIGHT_AT = {
    "ffn1_w1": ("col1", 0), "ffn1_w3": ("col1", 1), "ffn1_w2": ("row2a", 0),
    "w_ret_o": ("sqA", 0), "w_lru_o": ("sqA", 1), "w_out": ("sqA", 2),
    "w_xq": ("sqB", 0), "w_xk": ("sqB", 1), "w_xv": ("sqC", 0), "w_xo": ("sqC", 1),
    "ffn2_w1": ("col2a", 0), "ffn2_w3": ("col2b", 0), "ffn2_w2": ("row2b", 0),
}


def _local_step(x, mem, tgt, gw, sm, big):
    T = x.shape[0]
    tm = ROW_TILE

    def wt(name):
        key, idx = WEIGHT_AT[name]
        return gw[key], idx

    row3 = lambda i, j, r: (i, 0)
    vec3 = lambda i, j, r: (0, 0)
    rowD = _spec((tm, D), row3)
    vecD = _spec((1, D), vec3)

    def residual_norm(acc, erefs, orefs, ids):
        xo = erefs[0][...] + acc
        orefs[0][...] = xo
        orefs[1][...] = _rms_fwd(xo, erefs[1][...]).astype(BF16)

    def res_norm_io(x_res, g):
        return ([(x_res, rowD), (g, vecD)],
                [(_sds((T, D), F32), rowD), (_sds((T, D), BF16), rowD)])

    a1, b1, s1, h1 = _ffn_up("ffn1_up", x, *wt("ffn1_w1"), *wt("ffn1_w3"), norm_gain=sm["ffn1_norm"])
    x1, h2 = _ffn_down("ffn1_down", s1, *wt("ffn1_w2"), x, sm["mix_norm"])

    tw = min(WIDE_ROW_TILE, T)
    wideD = _spec((tw, D), row3)
    u = _gemm(
        "mix_in",
        [(h2, wideD, gw["win"], _spec((None, None, IN_BLK, D), lambda i, j, r: (j, 0, 0, 0)), "nt")],
        (T // tw, N_CHIPS, 1),
        [(_sds((T, 5120), F32), _spec((tw, IN_BLK), lambda i, j, r: (i, j)))], (tw, IN_BLK))[0]

    consts = _retention_constants(T)
    qr, kr, ret, yr, states = _ret_fwd(u, consts, sm["ret_gn"])

    conv_w = gw["conv"][:, 0].transpose(1, 0, 2).reshape(CONV_TAPS, D)
    xc, rg, ig, av, bx = _lru_gates_fwd(u, conv_w, sm["conv_b"], sm["w_rgate"], sm["b_rgate"],
                                        sm["w_igate"], sm["b_igate"], sm["lru_lambda"])
    a3 = av.reshape(T, LRU_BLOCKS, LRU_BLOCK)
    hl = _lru_scan("lru_scan_fwd", a3, bx.reshape(T, LRU_BLOCKS, LRU_BLOCK), False).reshape(T, D)

    row1 = _spec((tm, D), lambda i: (i, 0))
    glru1 = _spec((tm, D), lambda i: (i, 4))

    def lru_out(irefs, orefs, ids):
        gl, _ = _gelu_and_grad(irefs[1][...])
        orefs[0][...] = (irefs[0][...] * gl).astype(BF16)

    yl = _rowwise("lru_out", lru_out, [(hl, row1), (u, glru1)], [(_sds((T, D), BF16), row1)], (T // tm,))[0]

    def gate_epilogue(acc, erefs, orefs, ids):
        orefs[0][...] = _sigmoid(acc + erefs[0][...])

    gates = _gemm(
        "mix_gates",
        [(h2, wideD, gw["wbg"], _spec((None, None, BG_BLK, D), lambda i, j, r: (j, 0, 0, 0)), "nt")],
        (T // tw, N_CHIPS, 1),
        [(_sds((T, 2 * D), F32), _spec((tw, BG_BLK), lambda i, j, r: (i, j)))], (tw, BG_BLK),
        [(sm["b_branch_gate"], _spec((1, BG_BLK), lambda i, j, r: (0, j)))], gate_epilogue)[0]

    y_ret = _proj_sq("y_ret", yr, *wt("w_ret_o"), "nn")[0]

    def merge_epilogue(acc, erefs, orefs, ids):
        orefs[0][...] = acc
        orefs[1][...] = (erefs[0][...] * erefs[2][...] + erefs[1][...] * acc).astype(BF16)

    y_lru, merged = _proj_sq(
        "y_lru", yl, *wt("w_lru_o"), "nn",
        extras=[(gates, _spec((tm, D), lambda i, j, r: (i, 0))), (gates, _spec((tm, D), lambda i, j, r: (i, 1))),
                (y_ret, rowD)],
        epilogue=merge_epilogue,
        outs=[(_sds((T, D), F32), rowD), (_sds((T, D), BF16), rowD)])

    ex, ou = res_norm_io(x1, sm["xattn_norm"])
    x2, hq = _proj_sq("mix_out", merged, *wt("w_out"), "nn", extras=ex, epilogue=residual_norm, outs=ou)

    m = _rmsnorm("mem_norm", mem, sm["mem_norm"])
    xq = _proj_sq("xq", hq, *wt("w_xq"), "nn", BF16)[0]
    xk = _proj_sq("xk", m, *wt("w_xk"), "nn", BF16)[0]
    xv = _proj_sq("xv", m, *wt("w_xv"), "nn", BF16)[0]
    xo = _xattn_fwd(xq, xk, xv)
    ex, ou = res_norm_io(x2, sm["ffn2_norm"])
    x3, h3 = _proj_sq("xattn_out", xo, *wt("w_xo"), "nn", extras=ex, epilogue=residual_norm, outs=ou)

    a2, b2, s2 = _ffn_up("ffn2_up", h3, *wt("ffn2_w1"), *wt("ffn2_w3"))
    loss, dx4, dg_final = _ffn_down("ffn2_down", s2, *wt("ffn2_w2"), x3, sm["final_norm"], loss_target=tgt)

    dx3, dg_ffn2 = _ffn_bwd("ffn2", dx4, h3, a2, b2, s2, *wt("ffn2_w1"), *wt("ffn2_w3"),
                            *wt("ffn2_w2"), x3, sm["ffn2_norm"], big)

    dxo = _proj_sq("d_xo", dx3, *wt("w_xo"), "nt", BF16)[0]
    big["w_xo"] = _dw_sq("dw_xo", xo, dx3)[None]
    dxq, dxk, dxv = _xattn_bwd(xq, xk, xv, dxo)
    big["w_xq"] = _dw_sq("dw_xq", hq, dxq)[None]
    ex, ou = _rms_bwd_io(x2, sm["xattn_norm"], dx3, T, tm)
    dx2, dg_xattn = _proj_sq("d_hq", dxq, *wt("w_xq"), "nt", extras=ex, epilogue=_rms_bwd_epilogue, outs=ou)
    big["w_xk"] = _dw_sq("dw_xk", m, dxk)[None]
    big["w_xv"] = _dw_sq("dw_xv", m, dxv)[None]

    M = mem.shape[0]

    def mem_norm_epilogue(acc, erefs, orefs, ids):
        _, dgp = _rms_bwd(erefs[0][...], erefs[1][...], acc)
        orefs[0][...] = dgp

    wsq_spec = lambda idx: _spec((N_CHIPS, None, SQ_BLK, D), lambda i, j, r: (0, idx, 0, 0))
    memD = _spec((M, D), row3)
    dg_mem = _gemm(
        "d_mem_norm",
        [(dxk, memD, wt("w_xk")[0], wsq_spec(wt("w_xk")[1]), "nt"),
         (dxv, memD, wt("w_xv")[0], wsq_spec(wt("w_xv")[1]), "nt")],
        (1, 1, 1), [(_sds((1, D), F32), vecD)], (M, D),
        [(mem, memD), (sm["mem_norm"], vecD)], mem_norm_epilogue)[0]

    def merged_bwd_epilogue(acc, erefs, orefs, ids):
        gr, gl, yrv, ylv = (e[...] for e in erefs)
        orefs[0][...] = (acc * gr).astype(BF16)
        orefs[1][...] = (acc * gl).astype(BF16)
        dgr = acc * yrv * gr * (1.0 - gr)
        dgl = acc * ylv * gl * (1.0 - gl)
        orefs[2][:, :D] = dgr.astype(BF16)
        orefs[2][:, D:] = dgl.astype(BF16)
        dbb = jnp.concatenate([jnp.sum(dgr, axis=0, keepdims=True), jnp.sum(dgl, axis=0, keepdims=True)], axis=1)
        _accumulate(orefs[3], dbb, ids[0] == 0)

    dy_ret, dy_lru, dgpre, db_bg = _proj_sq(
        "d_merged", dx2, *wt("w_out"), "nt",
        extras=[(gates, _spec((tm, D), lambda i, j, r: (i, 0))), (gates, _spec((tm, D), lambda i, j, r: (i, 1))),
                (y_ret, rowD), (y_lru, rowD)],
        epilogue=merged_bwd_epilogue,
        outs=[(_sds((T, D), BF16), rowD), (_sds((T, D), BF16), rowD),
              (_sds((T, 2 * D), BF16), _spec((tm, 2 * D), row3)),
              (_sds((1, 2 * D), F32), _spec((1, 2 * D), vec3))])
    big["w_branch_gate"] = _gemm(
        "dw_bg",
        [(h2, _spec((T, D), lambda j, n, r: (r, 0)), dgpre, _spec((T, BG_BLK), lambda j, n, r: (r, j)), "tn")],
        (N_CHIPS, 1, 1),
        [(_sds((N_CHIPS, D, BG_BLK), GRAD_WIRE_DTYPE), _spec((None, D, BG_BLK), lambda j, n, r: (j, 0, 0)))],
        (D, BG_BLK))[0][None]
    big["w_out"] = _dw_sq("dw_out", merged, dx2)[None]
    dyr = _proj_sq("d_yr", dy_ret, *wt("w_ret_o"), "nt")[0]
    big["w_ret_o"] = _dw_sq("dw_ret_o", yr, dy_ret)[None]
    dyl = _proj_sq("d_yl", dy_lru, *wt("w_lru_o"), "nt")[0]
    big["w_lru_o"] = _dw_sq("dw_lru_o", yl, dy_lru)[None]

    dq, dk, dv, dgr, dg_retgn = _ret_bwd(dyr, ret, u, qr, kr, states, consts, sm["ret_gn"])

    def lru_out_bwd(irefs, orefs, ids):
        gl, dgl = _gelu_and_grad(irefs[2][...])
        dyl_v = irefs[0][...]
        orefs[0][...] = dyl_v * gl
        orefs[1][...] = (dyl_v * irefs[1][...] * dgl).astype(BF16)

    dhl, dglru = _rowwise("lru_out_bwd", lru_out_bwd, [(dyl, row1), (hl, row1), (u, glru1)],
                          [(_sds((T, D), F32), row1), (_sds((T, D), BF16), row1)], (T // tm,))
    lmb = _lru_scan("lru_scan_bwd", a3, dhl.reshape(T, LRU_BLOCKS, LRU_BLOCK), True).reshape(T, D)
    dxl, dw_r, dw_i, dvec, dcw = _lru_gates_bwd(lmb, hl, av, rg, ig, xc, u, conv_w,
                                                sm["w_rgate"], sm["w_igate"], sm["lru_lambda"])

    du = jnp.concatenate([dq, dk, dv, dgr, dxl, dglru], axis=1)
    tk = T
    big["w_in"] = _gemm(
        "dw_in",
        [(h2, _spec((tk, D), lambda j, n, r: (r, 0)), du, _spec((tk, IN_BLK), lambda j, n, r: (r, j)), "tn")],
        (N_CHIPS, 1, T // tk),
        [(_sds((N_CHIPS, D, IN_BLK), GRAD_WIRE_DTYPE), _spec((None, D, IN_BLK), lambda j, n, r: (j, 0, 0)))],
        (D, IN_BLK))[0][None]
    tf = min(FFN_ROW_TILE, T)
    ex, ou = _rms_bwd_io(x1, sm["mix_norm"], dx2, T, tf)
    dx1, dg_mix = _gemm(
        "d_h2",
        [(du, _spec((tf, 5120), row3), gw["win"], _spec((N_CHIPS, None, IN_BLK, D), lambda i, j, r: (0, 0, 0, 0)), "nn"),
         (dgpre, _spec((tf, 2 * D), row3), gw["wbg"], _spec((N_CHIPS, None, BG_BLK, D), lambda i, j, r: (0, 0, 0, 0)),
          "nn")],
        (T // tf, 1, 1), ou, (tf, D), ex, _rms_bwd_epilogue)

    grad_x, dg_ffn1 = _ffn_bwd("ffn1", dx1, h1, a1, b1, s1, *wt("ffn1_w1"), *wt("ffn1_w3"),
                               *wt("ffn1_w2"), x, sm["ffn1_norm"], big)

    small = {
        "ffn1_norm": dg_ffn1, "mix_norm": dg_mix, "ret_gn": dg_retgn, "conv_b": dvec[3:4],
        "b_rgate": dvec[0:1], "b_igate": dvec[1:2], "lru_lambda": dvec[2:3], "xattn_norm": dg_xattn,
        "mem_norm": dg_mem, "ffn2_norm": dg_ffn2, "final_norm": dg_final, "b_branch_gate": db_bg,
        "conv_w": dcw, "w_rgate": dw_r, "w_igate": dw_i,
    }
    return loss, grad_x, small


ANY_SPEC = pl.BlockSpec(memory_space=pl.ANY)
VMEM_SPEC = pl.BlockSpec(memory_space=pltpu.VMEM)
N_PEER_CHIPS = N_CHIPS - 1


def _mesh_position():
    x, y, c = lax.axis_index("x"), lax.axis_index("y"), lax.axis_index("c")
    chips = [(1 - x, y), (x, 1 - y), (1 - x, 1 - y)]
    return x, y, c, chips


def _chip_index(x, y):
    return 2 * x + y


def _rows_half(ref, axis, h):
    n = ref.shape[axis] // 2
    idx = [slice(None)] * len(ref.shape)
    idx[axis] = pl.ds(pl.multiple_of(h * n, BF16_TILE_ROWS), n)
    return ref.at[tuple(idx)]


def _remote(src, dst, send_sem, recv_sem, device):
    return pltpu.make_async_remote_copy(src_ref=src, dst_ref=dst, send_sem=send_sem, recv_sem=recv_sem,
                                        device_id=device, device_id_type=MESH)


def _gather_chips_task(shards, split, landed, legs="both"):
    keys = list(shards)
    n = len(keys)

    def operands():
        if legs == "pass_on":
            return [landed[k] for k in keys]
        chip_me = _chip_index(lax.axis_index("x"), lax.axis_index("y"))
        return [lax.dynamic_update_slice(lax.empty((N_CHIPS,) + shards[k].shape, shards[k].dtype), shards[k][None],
                                         (chip_me,) + (0,) * shards[k].ndim) for k in keys]

    def my_rows(ref, c):
        return _rows_half(ref, 1, c)

    def make_direct(ins, outs, send_sem, recv_sem):
        x, y, c, chips = _mesh_position()
        s_me = _chip_index(x, y)
        starts, arrivals = [], []
        for g in range(n):
            for k, chip in enumerate(chips):
                sems = (send_sem(3 * g + k), recv_sem(3 * g + k))
                starts.append(functools.partial(_remote, outs[g].at[s_me], outs[g].at[s_me], *sems, (*chip, c)))
                got = outs[g].at[_chip_index(*chip)]
                arrivals.append(functools.partial(_remote, got, got, *sems, (*chip, c)))
        return starts, arrivals

    def make_swap(ins, outs, send_sem, recv_sem):
        x, y, c, _ = _mesh_position()
        first, _ = _axis_neighbours(x, y, c)
        starts, arrivals = [], []
        for g in range(n):
            sems = (send_sem(3 * g), recv_sem(3 * g))
            mine = my_rows(outs[g].at[_chip_index(x, y)], c)
            starts.append(functools.partial(_remote, mine, mine, *sems, (*first, c)))
            got = my_rows(outs[g].at[_chip_index(*first)], c)
            arrivals.append(functools.partial(_remote, got, got, *sems, (*first, c)))
        return starts, arrivals

    def make_pass_on(ins, outs, send_sem, recv_sem):
        x, y, c, _ = _mesh_position()
        first, second = _axis_neighbours(x, y, c)
        diagonal = (1 - x, 1 - y)
        starts, arrivals = [], []
        for g in range(n):
            half = lambda chip: my_rows(outs[g].at[_chip_index(*chip)], c)
            for k, (sent, arriving) in enumerate([((x, y), second), (first, diagonal)]):
                sems = (send_sem(3 * g + 1 + k), recv_sem(3 * g + 1 + k))
                starts.append(functools.partial(_remote, half(sent), half(sent), *sems, (*second, c)))
                arrivals.append(functools.partial(_remote, half(arriving), half(arriving), *sems, (*second, c)))
        return starts, arrivals

    def finish(res):
        landed.update(zip(keys, res))

    shapes = lambda: [_sds((N_CHIPS,) + shards[k].shape, shards[k].dtype) for k in keys]
    aliases = {g: g for g in range(n)}
    if not split:
        return _Task("chips", operands, shapes, aliases, 3 * n, make_direct, finish)
    if legs == "swap":
        return _Task("first", operands, shapes, aliases, 3 * n, make_swap, finish)
    if legs == "pass_on":
        return _Task("second", operands, shapes, aliases, 3 * n, make_pass_on, finish)
    return _Task("first+second", operands, shapes, aliases, 3 * n, make_swap, finish, make_second=make_pass_on)


def _gather_sibling_task(keys, landed, ready):
    n = len(keys)

    def make(ins, outs, send_sem, recv_sem):
        x, y, c, chips = _mesh_position()
        starts, arrivals = [], []
        for g in range(n):
            for k, chip in enumerate(chips):
                o = outs[g].at[_chip_index(*chip)]
                got, other = _rows_half(o, 1, c), _rows_half(o, 1, 1 - c)
                starts.append(functools.partial(_remote, got, got, send_sem(3 * g + k), recv_sem(3 * g + k),
                                                (x, y, 1 - c)))
                arrivals.append(functools.partial(_remote, other, other, send_sem(3 * g + k), recv_sem(3 * g + k),
                                                  (x, y, 1 - c)))
        return starts, arrivals

    def finish(res):
        ready.update(zip(keys, res))

    return _Task("sibling", lambda: [landed[k] for k in keys],
                 lambda: [_sds(landed[k].shape, landed[k].dtype) for k in keys],
                 {g: g for g in range(n)}, 3 * n, make, finish)


def _pair_swap_task(names, big, got):
    n = len(names)

    def make(ins, outs, send_sem, recv_sem):
        x, y, c, _ = _mesh_position()
        copies = [functools.partial(_remote, _rows_half(ins[a], 2, 1 - c), outs[a], send_sem(a), recv_sem(a),
                                    (x, y, 1 - c)) for a in range(n)]
        return copies, copies

    def shapes():
        return [_sds(big[k].shape[:2] + (big[k].shape[2] // 2, big[k].shape[3]), big[k].dtype) for k in names]

    return _Task("sibling", lambda: [big[k] for k in names], shapes, {}, n, make,
                 lambda res: got.update(zip(names, res)))


def _rs_pair_sum(name, fulls, gots, core):
    n = len(fulls)
    shapes = [(f.shape[2] // 2, f.shape[3]) for f in fulls]

    def body(core_ref, *refs):
        for a_ref, b_ref, o_ref in zip(refs[:n], refs[n:2 * n], refs[2 * n:]):
            o_ref[...] = (a_ref[...].astype(F32) + b_ref[...].astype(F32)).astype(BF16)

    mine = [pl.BlockSpec((None, None) + hc, lambda s, core_ref: (0, s, core_ref[0], 0)) for hc in shapes]
    slot = [pl.BlockSpec((None, None) + hc, lambda s, core_ref: (0, s, 0, 0)) for hc in shapes]
    return _pcall(
        body, name=name, grid=(N_CHIPS,), num_prefetch=1,
        in_specs=mine + slot, out_specs=slot,
        out_shape=[_sds((1, N_CHIPS) + hc, BF16) for hc in shapes],
    )(core, *fulls, *gots)


def _chip_exchange_task(names, pair_sums, by_source, part=0, nparts=1):
    n = len(names)

    def rows(ref):
        h = ref.shape[1] // nparts
        return ref.at[:, pl.ds(part * h, h), :]

    def make(ins, outs, send_sem, recv_sem):
        x, y, c, chips = _mesh_position()
        s_me = _chip_index(x, y)
        starts, arrivals = [], []
        for a in range(n):
            for k, chip in enumerate(chips):
                s_k = _chip_index(*chip)
                starts.append(functools.partial(_remote, rows(ins[a].at[:, s_k]), rows(outs[a].at[:, s_me]),
                                                send_sem(3 * a + k), recv_sem(3 * a + k), (*chip, c)))
                got = rows(outs[a].at[:, s_k])
                arrivals.append(functools.partial(_remote, got, got, send_sem(3 * a + k), recv_sem(3 * a + k),
                                                  (*chip, c)))
        return starts, arrivals

    def operands():
        return [pair_sums[k] for k in names] + ([by_source[k] for k in names] if part else [])

    return _Task("chips", operands, lambda: [_sds(pair_sums[k].shape, pair_sums[k].dtype) for k in names],
                 {n + a: a for a in range(n)} if part else {}, 3 * n, make,
                 lambda res: by_source.update(zip(names, res)))


def _rs_chip_sum(name, owns, parts, chip):
    n = len(owns)
    ns = N_CHIPS
    shapes = [p.shape[2:] for p in parts]

    def body(chip_ref, *refs):
        me = chip_ref[0]
        for i in range(n):
            own_v = refs[i][...].astype(F32)
            slots = refs[n + ns * i:n + ns * (i + 1)]
            tot = None
            for s in range(ns):
                term = jnp.where(me == s, own_v, slots[s][...].astype(F32))
                tot = term if tot is None else tot + term
            refs[n + ns * n + i][...] = tot

    def slot_spec(hc, s):
        return pl.BlockSpec((None, None) + hc,
                            lambda g, chip_ref: (0, jnp.where(chip_ref[0] == s, (s + 1) % ns, s), 0, 0))

    own_specs = [pl.BlockSpec((None, None) + hc, lambda g, chip_ref: (0, chip_ref[0], 0, 0)) for hc in shapes]
    slot_specs = [slot_spec(hc, s) for hc in shapes for s in range(ns)]
    return _pcall(
        body, name=name, grid=(1,), num_prefetch=1,
        in_specs=own_specs + slot_specs,
        out_specs=[pl.BlockSpec((None,) + hc, lambda g, chip_ref: (0, 0, 0)) for hc in shapes],
        out_shape=[_sds((1,) + hc, F32) for hc in shapes],
    )(chip, *owns, *[p for p in parts for _ in range(ns)])


def _pair_gather_task(names, halves, sibling_halves):
    n = len(names)

    def make(ins, outs, send_sem, recv_sem):
        x, y, c, _ = _mesh_position()
        copies = [functools.partial(_remote, ins[a], outs[a], send_sem(a), recv_sem(a), (x, y, 1 - c))
                  for a in range(n)]
        return copies, copies

    return _Task("sibling", lambda: [halves[k] for k in names], lambda: [_sds(halves[k].shape, F32) for k in names],
                 {}, n, make, lambda res: sibling_halves.update(zip(names, res)))


def _small_allreduce(arrs):
    n = len(arrs)
    per = 1 + 2 * N_PEER_CHIPS

    def body(*refs):
        v_refs, o_refs = refs[:n], refs[n:2 * n]
        sib, pair, part = refs[2 * n:3 * n], refs[3 * n:4 * n], refs[4 * n:5 * n]
        send_sems, recv_sems = refs[5 * n:]
        x, y, c, chips = _mesh_position()
        s_me = _chip_index(x, y)

        def quarter(ref, s):
            q = ref.shape[0] // N_CHIPS
            return ref.at[pl.ds(pl.multiple_of(s * q, F32_TILE_ROWS), q)]

        def exchange(first_sem, src, dst_of, arrival_of):
            sems = lambda a, k: (send_sems.at[a * per + first_sem + k], recv_sems.at[a * per + first_sem + k])
            sends = [_remote(src(a, _chip_index(*chip)), dst_of(a, s_me), *sems(a, k), (*chip, c))
                     for a in range(n) for k, chip in enumerate(chips)]
            for cp in sends:
                cp.start()
            for a in range(n):
                for k, chip in enumerate(chips):
                    got = arrival_of(a, _chip_index(*chip))
                    _remote(got, got, *sems(a, k), (*chip, c)).wait_recv()
            for cp in sends:
                cp.wait_send()

        swaps = [_remote(v_refs[a], sib[a], send_sems.at[a * per], recv_sems.at[a * per], (x, y, 1 - c))
                 for a in range(n)]
        for cp in swaps:
            cp.start()
        for cp in swaps:
            cp.wait()
        for a in range(n):
            pair[a][...] = v_refs[a][...] + sib[a][...]
        exchange(1, lambda a, s_k: quarter(pair[a], s_k), lambda a, s: part[a].at[s], lambda a, s_k: part[a].at[s_k])
        for a in range(n):
            part[a][s_me] = quarter(pair[a], s_me)[...]
            q = o_refs[a].shape[0] // N_CHIPS
            o_refs[a][pl.ds(pl.multiple_of(s_me * q, F32_TILE_ROWS), q), :] = (
                ((part[a][0] + part[a][1]) + part[a][2]) + part[a][3])
        exchange(1 + N_PEER_CHIPS, lambda a, s_k: quarter(o_refs[a], s_me), lambda a, s: quarter(o_refs[a], s),
                 lambda a, s_k: quarter(o_refs[a], s_k))

    shapes = [a.shape for a in arrs]
    return _pcall(
        body, name="small_allreduce", grid=(1,), own_peers=("sibling", "chips"),
        in_specs=[VMEM_SPEC] * n, out_specs=[VMEM_SPEC] * n, out_shape=[_sds(s, F32) for s in shapes],
        scratch_shapes=([pltpu.VMEM(s, F32) for s in shapes] * 2
                        + [pltpu.VMEM((N_CHIPS, s[0] // N_CHIPS, s[1]), F32) for s in shapes]
                        + [pltpu.SemaphoreType.DMA((n * per,)), pltpu.SemaphoreType.DMA((n * per,))]),
    )(*arrs)


TRANSPOSED_WEIGHTS = ("ffn1_w1", "ffn1_w3", "ffn2_w1", "ffn2_w3")
SMALL_LAYOUT = [("ffn1_norm", 1), ("mix_norm", 1), ("ret_gn", 1), ("conv_b", 1), ("b_rgate", 1), ("b_igate", 1),
                ("lru_lambda", 1), ("xattn_norm", 1), ("mem_norm", 1), ("ffn2_norm", 1), ("final_norm", 1),
                ("b_branch_gate", 2), ("conv_w", CONV_TAPS)]
SMALL_ROWS = 32
GATE_WEIGHTS = ("w_rgate", "w_igate")
WEIGHT_ORDER = ["ffn1_norm", "ffn1_w1", "ffn1_w3", "ffn1_w2", "mix_norm", "w_in", "ret_gn", "w_ret_o", "conv_w",
                "conv_b", "w_rgate", "b_rgate", "w_igate", "b_igate", "lru_lambda", "w_lru_o", "w_branch_gate",
                "b_branch_gate", "w_out", "xattn_norm", "mem_norm", "w_xq", "w_xk", "w_xv", "w_xo", "ffn2_norm",
                "ffn2_w1", "ffn2_w3", "ffn2_w2", "final_norm"]


SMALL_USED_ROWS = sum(n for _, n in SMALL_LAYOUT)


def _pack_small(parts, extra_row=None):
    rows = [parts[name].reshape(n, D) for name, n in SMALL_LAYOUT]
    if extra_row is not None:
        rows.append(extra_row)
    rows.append(jnp.zeros((SMALL_ROWS - sum(r.shape[0] for r in rows), D), F32))
    return jnp.concatenate(rows, axis=0)


def _unpack_small(packed, shapes):
    out, r = {}, 0
    for name, n in SMALL_LAYOUT:
        out[name] = packed[r:r + n].reshape(shapes[name])
        r += n
    return out


def kernel(x, mem, ffn1_norm, ffn1_w1, ffn1_w3, ffn1_w2, mix_norm, w_in, ret_gn, w_ret_o, conv_w, conv_b, w_rgate, b_rgate, w_igate, b_igate, lru_lambda, w_lru_o, w_branch_gate, b_branch_gate, w_out, xattn_norm, mem_norm, w_xq, w_xk, w_xv, w_xo, ffn2_norm, ffn2_w1, ffn2_w3, ffn2_w2, final_norm, loss_target, m_ffn1_norm, m_ffn1_w1, m_ffn1_w3, m_ffn1_w2, m_mix_norm, m_w_in, m_ret_gn, m_w_ret_o, m_conv_w, m_conv_b, m_w_rgate, m_b_rgate, m_w_igate, m_b_igate, m_lru_lambda, m_w_lru_o, m_w_branch_gate, m_b_branch_gate, m_w_out, m_xattn_norm, m_mem_norm, m_w_xq, m_w_xk, m_w_xv, m_w_xo, m_ffn2_norm, m_ffn2_w1, m_ffn2_w3, m_ffn2_w2, m_final_norm, v_ffn1_norm, v_ffn1_w1, v_ffn1_w3, v_ffn1_w2, v_mix_norm, v_w_in, v_ret_gn, v_w_ret_o, v_conv_w, v_conv_b, v_w_rgate, v_b_rgate, v_w_igate, v_b_igate, v_lru_lambda, v_w_lru_o, v_w_branch_gate, v_b_branch_gate, v_w_out, v_xattn_norm, v_mem_norm, v_w_xq, v_w_xk, v_w_xv, v_w_xo, v_ffn2_norm, v_ffn2_w1, v_ffn2_w3, v_ffn2_w2, v_final_norm):
    given = dict(locals())
    w = {n: given[n] for n in WEIGHT_ORDER}
    mom = {n: given["m_" + n] for n in WEIGHT_ORDER}
    var = {n: given["v_" + n] for n in WEIGHT_ORDER}
    chip = _chip_index(lax.axis_index("x"), lax.axis_index("y"))
    core = lax.axis_index("c").astype(jnp.int32).reshape(1)

    chip_id = chip.astype(jnp.int32).reshape(1)
    sm = {n: w[n] for n in ["ffn1_norm", "mix_norm", "ret_gn", "conv_b", "b_rgate", "b_igate", "lru_lambda",
                            "xattn_norm", "mem_norm", "ffn2_norm", "b_branch_gate"]}
    sm["final_norm"] = w["final_norm"].reshape(1, D)
    sm["w_rgate"] = w["w_rgate"][0]
    sm["w_igate"] = w["w_igate"][0]

    local = lambda a, n: jnp.swapaxes(a[0], 0, 1) if n in TRANSPOSED_WEIGHTS else a[0]
    stack = lambda names: jnp.stack([local(w[n], n) for n in names], axis=0).astype(BF16)
    shard = {"col1": stack(["ffn1_w1", "ffn1_w3"]), "row2a": stack(["ffn1_w2"]),
             "win": jnp.swapaxes(w["w_in"], 1, 2).astype(BF16),
             "wbg": jnp.swapaxes(w["w_branch_gate"], 1, 2).astype(BF16),
             "sqA": stack(["w_ret_o", "w_lru_o", "w_out"]), "sqB": stack(["w_xq", "w_xk"]),
             "sqC": stack(["w_xv", "w_xo"]), "col2a": stack(["ffn2_w1"]), "col2b": stack(["ffn2_w3"]),
             "row2b": stack(["ffn2_w2"]), "conv": w["conv_w"]}
    gw, landed = {}, {}
    over_chips = lambda keys: _gather_chips_task({k: shard[k] for k in keys}, True, landed)
    to_sibling = lambda keys: _gather_sibling_task(keys, landed, gw)

    big, got, pair_sums, by_source, halves, sibling_halves, outs = {}, {}, {}, {}, {}, {}, {}
    pair_swap = lambda names: _pair_swap_task(names, big, got)
    exchange = lambda names, part=0, nparts=1: _chip_exchange_task(names, pair_sums, by_source, part, nparts)
    pair_gather = lambda names: _pair_gather_task(names, halves, sibling_halves)

    def pair_sum(names):
        res = _rs_pair_sum("rs_pair_sum_" + names[0], [big[n] for n in names], [got[n] for n in names], core)
        pair_sums.update(zip(names, res))

    def chip_sum(names):
        res = _rs_chip_sum("rs_chip_sum_" + names[0], [pair_sums[n] for n in names], [by_source[n] for n in names],
                           chip_id)
        halves.update(zip(names, res))

    def adamw(names):
        for n in names:
            res = _adamw_halves("adamw_" + n, local(w[n], n), halves[n], sibling_halves[n], 0, local(mom[n], n),
                                local(var[n], n), core)
            outs[n] = tuple((jnp.swapaxes(r, 0, 1) if n in TRANSPOSED_WEIGHTS else r)[None] for r in res)

    do = lambda fn, names: functools.partial(fn, names)
    ffn2_grads = ["ffn2_w2", "ffn2_w1", "ffn2_w3"]
    xattn_grads = ["w_xo", "w_xq", "w_xk", "w_xv"]
    mix_out_grads = ["w_branch_gate", "w_out", "w_ret_o", "w_lru_o"]
    conv_gather = _gather_chips_task({"conv": shard["conv"]}, False, gw)
    swap = lambda key: _gather_chips_task({key: shard[key]}, True, landed, legs="swap")
    pass_on = lambda key: _gather_chips_task({key: shard[key]}, True, landed, legs="pass_on")
    plan = _Plan()
    plan.tasks = {
        "ag_first_chips": [over_chips(["col1"]), swap("row2a")],
        "ag_first_sibling": [to_sibling(["col1"]), pass_on("row2a"), swap("win")],
        "ffn1_up": [to_sibling(["row2a"]), pass_on("win"), swap("wbg")],
        "ffn1_down": [to_sibling(["win"]), pass_on("wbg"), swap("sqA")],
        "mix_in": [to_sibling(["wbg"]), pass_on("sqA"), swap("col2a"), conv_gather],
        "ret_fwd": [to_sibling(["sqA"]), pass_on("col2a"), swap("sqB")],
        "lru_gates_fwd": [to_sibling(["col2a"]), pass_on("sqB"), swap("sqC")],
        "lru_scan_fwd": [to_sibling(["sqB"]), pass_on("sqC"), swap("col2b")],
        "mix_gates": [to_sibling(["sqC"]), pass_on("col2b")],
        "y_lru": [to_sibling(["col2b"]), swap("row2b")],
        "ffn2_up": [pass_on("row2b")],
        "ffn2_up_sibling": [to_sibling(["row2b"])],
        "ffn2_dh": [pair_swap(ffn2_grads)],
        "xattn_bwd": [exchange(["ffn2_w2"], 0, 2)],
        "d_hq": [exchange(["ffn2_w2"], 1, 2)],
        "d_merged": [exchange(["ffn2_w1"], 0, 2), pair_swap(xattn_grads)],
        "lru_out_bwd": [exchange(["w_xo"])],
        "ret_bwd": [exchange(["ffn2_w1"], 1, 2), exchange(["ffn2_w3"], 0, 2), pair_swap(mix_out_grads)],
        "lru_scan_bwd": [exchange(["ffn2_w3"], 1, 2)],
        "lru_gates_bwd": [exchange(["w_xq", "w_xk"]), pair_gather(ffn2_grads)],
        "dw_in": [exchange(["w_xv", "w_out"])],
        "d_h2": [exchange(["w_branch_gate", "w_ret_o", "w_lru_o"]), pair_swap(["w_in"]), pair_gather(xattn_grads)],
        "ffn1_bwd_mid": [exchange(["w_in"], 0, 2), pair_gather(mix_out_grads)],
        "ffn1_dw2": [exchange(["w_in"], 2, 4)],
        "ffn1_dw1": [exchange(["w_in"], 3, 4), pair_swap(["ffn1_w2"])],
        "ffn1_dw3": [exchange(["ffn1_w2"], 0, 2), pair_swap(["ffn1_w1"]), pair_gather(["w_in"])],
        "ffn1_dh": [exchange(["ffn1_w2"], 1, 2), exchange(["ffn1_w1"]), pair_swap(["ffn1_w3"])],
        "small_allreduce": [exchange(["ffn1_w3"]), pair_gather(["ffn1_w2"])],
        "rs_last_gather": [pair_gather(["ffn1_w1", "ffn1_w3"])],
    }
    plan.after = {
        "ffn2_up": [functools.partial(_comm_call, "ffn2_up_sibling")],
        "ffn2_dh": [do(pair_sum, ffn2_grads)],
        "d_merged": [do(pair_sum, xattn_grads)],
        "ret_bwd": [do(pair_sum, mix_out_grads)],
        "lru_scan_bwd": [do(chip_sum, ffn2_grads)],
        "lru_gates_bwd": [do(adamw, ffn2_grads)],
        "dw_in": [do(chip_sum, xattn_grads)],
        "d_h2": [do(chip_sum, mix_out_grads), do(pair_sum, ["w_in"]), do(adamw, xattn_grads)],
        "ffn1_bwd_mid": [do(adamw, mix_out_grads)],
        "ffn1_dw1": [do(chip_sum, ["w_in"]), do(pair_sum, ["ffn1_w2"])],
        "ffn1_dw3": [do(pair_sum, ["ffn1_w1"]), do(adamw, ["w_in"])],
        "ffn1_dh": [do(pair_sum, ["ffn1_w3"]), do(chip_sum, ["ffn1_w2"])],
        "small_allreduce": [do(chip_sum, ["ffn1_w1", "ffn1_w3"]), functools.partial(_comm_call, "rs_last_gather"),
                    do(adamw, ["ffn1_w2", "ffn1_w1", "ffn1_w3"])],
    }
    global _plan
    _plan = plan
    try:
        _comm_call("ag_first_chips")
        _comm_call("ag_first_sibling")
        loss_part, grad_x, small = _local_step(x[0], mem[0], loss_target[0], gw, sm, big)
        gate2d = lambda a: a.reshape(LRU_BLOCKS * LRU_BLOCK, LRU_BLOCK)
        loss_row = jnp.pad(loss_part, ((0, 0), (0, D - loss_part.shape[1])))
        small_sum, *gate_sums = _small_allreduce([_pack_small(small, loss_row)]
                                                 + [gate2d(small[n]) for n in GATE_WEIGHTS])
    finally:
        _plan = None
    assert not plan.tasks and not plan.after, (list(plan.tasks), list(plan.after))
    loss = small_sum[SMALL_USED_ROWS, 0]

    small_shapes = {n: w[n].shape for n, _ in SMALL_LAYOUT}
    small_shapes["conv_w"] = (CONV_TAPS, D)
    conv_row = SMALL_USED_ROWS - CONV_TAPS
    conv_grad = lax.dynamic_slice(small_sum[conv_row:conv_row + CONV_TAPS], (0, chip * SQ_BLK), (CONV_TAPS, SQ_BLK))
    small_w = {n: w[n] for n, _ in SMALL_LAYOUT}
    small_m = {n: mom[n] for n, _ in SMALL_LAYOUT}
    small_v = {n: var[n] for n, _ in SMALL_LAYOUT}
    pad_cols = lambda a: jnp.pad(a[0], ((0, 0), (0, D - SQ_BLK)))
    for dct in (small_w, small_m, small_v):
        dct["conv_w"] = pad_cols(dct["conv_w"])
    g_pack = lax.dynamic_update_slice(small_sum, jnp.pad(conv_grad, ((0, 0), (0, D - SQ_BLK))), (conv_row, 0))
    d_pack, m_pack, v_pack = _adamw("adamw_small", _pack_small(small_w), g_pack, _pack_small(small_m),
                                    _pack_small(small_v))
    unpacked = [_unpack_small(p, small_shapes) for p in (g_pack, d_pack, m_pack, v_pack)]
    for n, _ in SMALL_LAYOUT:
        if n == "conv_w":
            outs[n] = tuple(u[n][:, :SQ_BLK][None] for u in unpacked)
        else:
            outs[n] = tuple(u[n] for u in unpacked)
    for n, gsum in zip(GATE_WEIGHTS, gate_sums):
        d, nm, nv = _adamw("adamw_" + n, gate2d(w[n]), gsum, gate2d(mom[n]), gate2d(var[n]))
        outs[n] = tuple(r.reshape(w[n].shape) for r in (gsum, d, nm, nv))

    result = [loss, grad_x[None]]
    for k in range(4):
        result += [outs[n][k] for n in WEIGHT_ORDER]
    return tuple(result)
```

```python
import functools
import math

import jax
import jax.numpy as jnp
from jax import lax
from jax.experimental import pallas as pl
from jax.experimental.pallas import tpu as pltpu

F32 = jnp.float32
BF16 = jnp.bfloat16
GRAD_WIRE_DTYPE = BF16
MESH = pl.DeviceIdType.MESH

D = 1024
EPS = 1e-6
RET_HEADS = 4
RET_DK = 128
RET_DV = 256
CHUNK = 128
ROPE_BASE = 10000.0
LRU_BLOCKS = 8
LRU_BLOCK = 128
CONV_TAPS = 4
LRU_C = 8.0
D_FF = 2816
X_HEADS = 4
X_HD = 256
N_CHIPS = 4
FF_BLK = D_FF // N_CHIPS
IN_BLK = 5120 // N_CHIPS
BG_BLK = 2048 // N_CHIPS
SQ_BLK = D // N_CHIPS

ADAM_LR = 0.001
ADAM_B1 = 0.9
ADAM_B2 = 0.999
ADAM_EPS = 1e-08
ADAM_WD = 0.01
ADAM_STEP = 10

F32_TILE_ROWS = 8
BF16_TILE_ROWS = 16
VMEM_LIMIT_BYTES = 56 * 1024 * 1024
ROW_TILE = 512
WIDE_ROW_TILE = 1024
FFN_ROW_TILE = 256
DW_BLK = D_FF // 2
SCAN_TILE = 256
RET_STEP_CHUNKS = 2
RET_STEP_ROWS = RET_STEP_CHUNKS * CHUNK

_DN = {
    "nn": (((1,), (0,)), ((), ())),
    "nt": (((1,), (1,)), ((), ())),
    "tn": (((0,), (0,)), ((), ())),
}


def _cparams(n_axes, collective_id=None):
    return pltpu.CompilerParams(dimension_semantics=("arbitrary",) * n_axes,
                                vmem_limit_bytes=VMEM_LIMIT_BYTES, collective_id=collective_id)


def _dot(a, b, kind):
    if b.ndim == 3:
        b = b.reshape(b.shape[0] * b.shape[1], b.shape[2])
    return lax.dot_general(a.astype(BF16), b.astype(BF16), _DN[kind], preferred_element_type=F32)


def _sigmoid(x):
    return 1.0 / (1.0 + jnp.exp(-x))


def _log1p_pos(e):
    u = 1.0 + e
    return jnp.where(u == 1.0, e, jnp.log(u) * (e / jnp.where(u == 1.0, 1.0, u - 1.0)))


def _expm1(x):
    u = jnp.exp(x)
    lu = jnp.log(u)
    safe = jnp.where(lu == 0.0, 1.0, lu)
    return jnp.where(u == 1.0, x, (u - 1.0) * (x / safe))


def _softplus(z):
    return jnp.maximum(z, 0.0) + _log1p_pos(jnp.exp(-jnp.abs(z)))


_GELU_C = math.sqrt(2.0 / math.pi)


def _gelu_and_grad(x):
    x2 = x * x
    t = jnp.tanh(_GELU_C * (x + 0.044715 * x * x2))
    g = 0.5 * x * (1.0 + t)
    dg = 0.5 * (1.0 + t) + 0.5 * x * (1.0 - t * t) * (_GELU_C * (1.0 + 3.0 * 0.044715 * x2))
    return g, dg


def _rms_fwd(x, g):
    r = lax.rsqrt(jnp.mean(x * x, axis=-1, keepdims=True) + EPS)
    return (x * r) * g


def _rms_bwd(x, g, dh):
    r = lax.rsqrt(jnp.mean(x * x, axis=-1, keepdims=True) + EPS)
    n = x * r
    dyg = dh * g
    dx = r * (dyg - n * jnp.mean(dyg * n, axis=-1, keepdims=True))
    return dx, jnp.sum(dh * n, axis=0, keepdims=True)


def _accumulate(ref, val, first):
    @pl.when(first)
    def _():
        ref[...] = val

    @pl.when(jnp.logical_not(first))
    def _():
        ref[...] += val


def _sds(shape, dtype):
    return jax.ShapeDtypeStruct(tuple(shape), dtype)


def _spec(shape, fn):
    return pl.BlockSpec(tuple(shape), fn)


class _Task:
    def __init__(self, peers, operands, out_shapes, aliases, nsem, make, finish, make_second=None):
        self.peers = peers
        self.operands, self.out_shapes, self.aliases = operands, out_shapes, aliases
        self.nsem, self.make, self.finish = nsem, make, finish
        self.make_second = make_second


class _Plan:
    def __init__(self):
        self.tasks, self.after = {}, {}


_plan = None


_CHIP_PEER_SETS = [frozenset({"chips"}), frozenset({"first"}), frozenset({"second"}), frozenset({"first", "second"})]
PEER_SET_COLLECTIVE_ID = {frozenset({"sibling"}): 1}
for _i, _chip_peers in enumerate(_CHIP_PEER_SETS):
    PEER_SET_COLLECTIVE_ID[_chip_peers] = 2 + 2 * _i
    PEER_SET_COLLECTIVE_ID[_chip_peers | {"sibling"}] = 3 + 2 * _i


def _peer_set(names):
    names = frozenset(n for name in names for n in name.split("+"))
    return names - {"first", "second"} if "chips" in names else names


def _axis_neighbours(x, y, c):
    flip = lambda v, f: v + f * (1 - 2 * v)
    return (flip(x, 1 - c), flip(y, c)), (flip(x, c), flip(y, 1 - c))


def _entry_handshake(peer_set):
    x, y, c, chips = _mesh_position()
    first, second = _axis_neighbours(x, y, c)
    peers = [(x, y, 1 - c)] if "sibling" in peer_set else []
    if "chips" in peer_set:
        peers += [(*chip, c) for chip in chips]
    if "first" in peer_set:
        peers.append((*first, c))
    if "second" in peer_set:
        peers.append((*second, c))
    barrier = pltpu.get_barrier_semaphore()
    for peer in peers:
        pl.semaphore_signal(barrier, inc=1, device_id=peer, device_id_type=MESH)
    pl.semaphore_wait(barrier, len(peers))


def _pcall(body, *, name, grid, in_specs, out_specs, out_shape, scratch_shapes=(), num_prefetch=0, own_peers=()):
    single = not isinstance(out_shape, (list, tuple))
    out_shape = [out_shape] if single else list(out_shape)
    out_specs = [out_specs] if single else list(out_specs)
    in_specs = list(in_specs)
    scratch_shapes = list(scratch_shapes)
    tasks = _plan.tasks.pop(name, []) if _plan is not None else []
    after = _plan.after.pop(name, []) if _plan is not None else []
    peer_set = _peer_set([t.peers for t in tasks] + list(own_peers))
    nax = len(grid)

    def run(*operands):
        n_in = len(operands) - num_prefetch
        n_out = len(out_shape)
        t_ops = [t.operands() for t in tasks]
        t_outs = [t.out_shapes() for t in tasks]
        c_ops = [a for ops in t_ops for a in ops]
        c_outs = [s for outs in t_outs for s in outs]
        aliases = {}
        i0, o0 = num_prefetch + n_in, n_out
        for t, ops, outs in zip(tasks, t_ops, t_outs):
            for i_loc, o_loc in t.aliases.items():
                aliases[i0 + i_loc] = o0 + o_loc
            i0 += len(ops)
            o0 += len(outs)
        nsem = sum(t.nsem for t in tasks)

        def wrapped(*refs):
            p = num_prefetch
            pre, ins = refs[:p], refs[p:p + n_in]
            cins = refs[p + n_in:p + n_in + len(c_ops)]
            q = p + n_in + len(c_ops)
            outs, couts = refs[q:q + n_out], refs[q + n_out:q + n_out + len(c_outs)]
            q += n_out + len(c_outs)
            scr = refs[q:q + len(scratch_shapes)]

            def rounds(second):
                send_sems, recv_sems = refs[q + len(scratch_shapes):]
                out = []
                ci = co = so = 0
                for t, ops, souts in zip(tasks, t_ops, t_outs):
                    make = t.make_second if second else t.make
                    out.append(([], []) if make is None else
                               make(cins[ci:ci + len(ops)], couts[co:co + len(souts)],
                                    functools.partial(lambda base, k: send_sems.at[base + k], so),
                                    functools.partial(lambda base, k: recv_sems.at[base + k], so)))
                    ci, co, so = ci + len(ops), co + len(souts), so + t.nsem
                return out

            two_rounds = [t.make_second is not None for t in tasks]
            if peer_set:
                ids = [pl.program_id(k) for k in range(nax)]
                first = functools.reduce(jnp.logical_and, [i == 0 for i in ids])
                last = functools.reduce(jnp.logical_and, [i == g - 1 for i, g in zip(ids, grid)])
                step = functools.reduce(lambda acc, ig: acc * ig[1] + ig[0], zip(ids, grid), 0)
                middle = step == math.prod(grid) // 3

                @pl.when(first)
                def _():
                    _entry_handshake(peer_set)
                    for starts, _ in rounds(False):
                        for copy in starts:
                            copy().start()

            body(*pre, *ins, *outs, *scr)

            if any(two_rounds):
                @pl.when(middle)
                def _():
                    for (_, arrivals), two in zip(rounds(False), two_rounds):
                        if two:
                            for arrival in arrivals:
                                arrival().wait_recv()
                    for starts, _ in rounds(True):
                        for copy in starts:
                            copy().start()

            if tasks:
                @pl.when(last)
                def _():
                    first_round, second_round = rounds(False), rounds(True)
                    for (_, arrivals1), (_, arrivals2), two in zip(first_round, second_round, two_rounds):
                        for arrival in (arrivals2 if two else arrivals1):
                            arrival().wait_recv()
                    for starts, _ in first_round + second_round:
                        for copy in starts:
                            copy().wait_send()

        sems = [pltpu.SemaphoreType.DMA((nsem,)), pltpu.SemaphoreType.DMA((nsem,))] if tasks else []
        res = pl.pallas_call(
            wrapped, name=name,
            grid_spec=pltpu.PrefetchScalarGridSpec(
                num_scalar_prefetch=num_prefetch, grid=tuple(grid),
                in_specs=in_specs + [ANY_SPEC] * len(c_ops),
                out_specs=out_specs + [ANY_SPEC] * len(c_outs),
                scratch_shapes=scratch_shapes + sems),
            out_shape=out_shape + c_outs,
            input_output_aliases=aliases,
            compiler_params=_cparams(nax, PEER_SET_COLLECTIVE_ID[peer_set] if peer_set else None),
        )(*operands, *c_ops)
        co = n_out
        for t, souts in zip(tasks, t_outs):
            t.finish(res[co:co + len(souts)])
            co += len(souts)
        for fn in after:
            fn()
        return res[0] if single else list(res[:n_out])

    return run


def _comm_call(name):
    def body(o_ref):
        o_ref[...] = jnp.zeros_like(o_ref)

    _pcall(body, name=name, grid=(1,), in_specs=[], out_specs=_spec((8, 128), lambda i: (0, 0)),
           out_shape=_sds((8, 128), F32))()


def _gemm(name, terms, grid, outs, acc_shape, extras=(), epilogue=None):
    kinds = [t[4] for t in terms]
    nt, ne, no = len(terms), len(extras), len(outs)
    nred = grid[-1]
    nax = len(grid)

    def body(*refs):
        trefs = refs[:2 * nt]
        erefs = refs[2 * nt:2 * nt + ne]
        orefs = refs[2 * nt + ne:2 * nt + ne + no]
        ids = [pl.program_id(k) for k in range(nax)]
        tot = None
        for t in range(nt):
            d = _dot(trefs[2 * t][...], trefs[2 * t + 1][...], kinds[t])
            tot = d if tot is None else tot + d

        def finish(acc):
            if epilogue is None:
                orefs[0][...] = acc.astype(orefs[0].dtype)
            else:
                epilogue(acc, erefs, orefs, ids)

        if nred == 1:
            finish(tot)
        else:
            acc_ref = refs[-1]
            r = ids[-1]

            @pl.when(r == 0)
            def _():
                acc_ref[...] = tot

            @pl.when(r > 0)
            def _():
                acc_ref[...] += tot

            @pl.when(r == nred - 1)
            def _():
                finish(acc_ref[...])

    operands, in_specs = [], []
    for a, a_spec, b, b_spec, _ in terms:
        operands += [a, b]
        in_specs += [a_spec, b_spec]
    for e, e_spec in extras:
        operands.append(e)
        in_specs.append(e_spec)
    scratch = [pltpu.VMEM(tuple(acc_shape), F32)] if nred > 1 else []
    return _pcall(body, name=name, grid=tuple(grid), in_specs=in_specs, out_specs=[o[1] for o in outs],
                  out_shape=[o[0] for o in outs], scratch_shapes=scratch)(*operands)


def _rowwise(name, fn, ins, outs, grid):
    ni = len(ins)
    nax = len(grid)

    def body(*refs):
        ids = [pl.program_id(k) for k in range(nax)]
        fn(refs[:ni], refs[ni:], ids)

    return _pcall(body, name=name, grid=tuple(grid), in_specs=[i[1] for i in ins],
                  out_specs=[o[1] for o in outs], out_shape=[o[0] for o in outs])(*[i[0] for i in ins])


def _ffn_up(name, h, w1buf, w1_idx, w3buf, w3_idx, norm_gain=None):
    T = h.shape[0]
    tm = min(FFN_ROW_TILE, T)
    normed = norm_gain is not None

    def body(h_ref, *refs):
        if normed:
            g_ref, w1_ref, w3_ref, a_ref, b_ref, s_ref, hn_ref = refs
            hv = _rms_fwd(h_ref[...], g_ref[...]).astype(BF16)
            hn_ref[...] = hv
        else:
            w1_ref, w3_ref, a_ref, b_ref, s_ref = refs
            hv = h_ref[...]
        a = _dot(hv, w1_ref[...], "nt")
        b = _dot(hv, w3_ref[...], "nt")
        a_ref[...] = a.astype(BF16)
        b_ref[...] = b.astype(BF16)
        s_ref[...] = ((a * _sigmoid(a)) * b).astype(BF16)

    row = _spec((tm, D), lambda i: (i, 0))
    blk = _spec((tm, D_FF), lambda i: (i, 0))
    return _pcall(
        body, name=name, grid=(T // tm,),
        in_specs=[row] + ([_spec((1, D), lambda i: (0, 0))] if normed else [])
        + [_spec((N_CHIPS, None, FF_BLK, D), lambda i: (0, w1_idx, 0, 0)),
           _spec((N_CHIPS, None, FF_BLK, D), lambda i: (0, w3_idx, 0, 0))],
        out_specs=[blk, blk, blk] + ([row] if normed else []),
        out_shape=[_sds((T, D_FF), BF16)] * 3 + ([_sds((T, D), BF16)] if normed else []),
    )(h, *([norm_gain] if normed else []), w1buf, w3buf)


def _loss_head(x, g, tgt, loss_ref, dx_ref, dg_ref, first):
    err = _rms_fwd(x, g) - tgt
    lp = 0.5 * jnp.sum(jnp.mean(err * err, axis=-1, keepdims=True), axis=0, keepdims=True)
    _accumulate(loss_ref, jnp.broadcast_to(lp, (1, 128)), first)
    dx, dgp = _rms_bwd(x, g, err * (1.0 / D))
    dx_ref[...] = dx
    _accumulate(dg_ref, dgp, first)


def _ffn_down(name, s, wrow2, w2_idx, x_res, g_next=None, loss_target=None):
    T = x_res.shape[0]
    tm = min(ROW_TILE, T)
    row = lambda i, j, r: (i, 0)
    vec = lambda i, j, r: (0, 0)

    def epilogue(acc, erefs, orefs, ids):
        xo = erefs[0][...] + 0.5 * acc
        if loss_target is not None:
            _loss_head(xo, erefs[1][...], erefs[2][...], orefs[0], orefs[1], orefs[2], ids[0] == 0)
            return
        orefs[0][...] = xo
        orefs[1][...] = _rms_fwd(xo, erefs[1][...]).astype(BF16)

    extras = [(x_res, _spec((tm, D), row)), (g_next, _spec((1, D), vec))]
    if loss_target is None:
        outs = [(_sds((T, D), F32), _spec((tm, D), row)), (_sds((T, D), BF16), _spec((tm, D), row))]
    else:
        extras.append((loss_target, _spec((tm, D), row)))
        outs = [(_sds((1, 128), F32), _spec((1, 128), vec)), (_sds((T, D), F32), _spec((tm, D), row)),
                (_sds((1, D), F32), _spec((1, D), vec))]
    return _gemm(
        name,
        [(s, _spec((tm, D_FF), row),
          wrow2, _spec((N_CHIPS, None, FF_BLK, D), lambda i, j, r: (0, w2_idx, 0, 0)), "nn")],
        (T // tm, 1, 1), outs, (tm, D), extras, epilogue)


def _ffn_bwd_mid(name, dx, wrow2, w2_idx, a, b):
    T = dx.shape[0]
    tm = min(FFN_ROW_TILE, T)

    def body(dx_ref, w2_ref, a_ref, b_ref, dab_ref):
        ds = _dot(0.5 * dx_ref[...], w2_ref[...], "nt")
        av = a_ref[...].astype(F32)
        sg = _sigmoid(av)
        dab_ref[0] = (ds * b_ref[...].astype(F32) * (sg * (1.0 + av * (1.0 - sg)))).astype(BF16)
        dab_ref[1] = (ds * (av * sg)).astype(BF16)

    blk = _spec((tm, D_FF), lambda i: (i, 0))
    return _pcall(
        body, name=name, grid=(T // tm,),
        in_specs=[_spec((tm, D), lambda i: (i, 0)),
                  _spec((N_CHIPS, None, FF_BLK, D), lambda i: (0, w2_idx, 0, 0)),
                  blk, blk],
        out_specs=_spec((2, tm, D_FF), lambda i: (0, i, 0)),
        out_shape=_sds((2, T, D_FF), BF16),
    )(dx, wrow2, a, b)


def _rms_bwd_epilogue(acc, erefs, orefs, ids):
    dx, dgp = _rms_bwd(erefs[0][...], erefs[1][...], acc)
    orefs[0][...] = dx + erefs[2][...]
    _accumulate(orefs[1], dgp, ids[0] == 0)


def _rms_bwd_io(x, g, dres, T, tm):
    row = lambda i, j, r: (i, 0)
    vec = lambda i, j, r: (0, 0)
    extras = [(x, _spec((tm, D), row)), (g, _spec((1, D), vec)), (dres, _spec((tm, D), row))]
    outs = [(_sds((T, D), F32), _spec((tm, D), row)), (_sds((1, D), F32), _spec((1, D), vec))]
    return extras, outs


def _ffn_bwd(tag, dx_out, h, a, b, s, w1buf, w1_idx, w3buf, w3_idx, wrow2, w2_idx, x_in, g, big):
    T = dx_out.shape[0]
    dab = _ffn_bwd_mid(tag + "_bwd_mid", dx_out, wrow2, w2_idx, a, b)

    def half_scale(acc, erefs, orefs, ids):
        orefs[0][...] = (0.5 * acc).astype(orefs[0].dtype)

    dw_grid = (D_FF // DW_BLK, 1, 1)
    dw_out = [(_sds((D_FF, D), GRAD_WIRE_DTYPE), _spec((DW_BLK, D), lambda j, n, r: (j, 0)))]
    tokens = _spec((T, D), lambda j, n, r: (0, 0))
    big[tag + "_w2"] = _gemm(
        tag + "_dw2", [(s, _spec((T, DW_BLK), lambda j, n, r: (0, j)), dx_out, tokens, "tn")],
        dw_grid, dw_out, (DW_BLK, D), (), half_scale)[0].reshape(1, N_CHIPS, FF_BLK, D)
    for widx, wname in ((0, "_w1"), (1, "_w3")):
        big[tag + wname] = _gemm(
            tag + "_d" + wname[1:],
            [(dab, _spec((None, T, DW_BLK), functools.partial(lambda w, j, n, r: (w, 0, j), widx)), h, tokens, "tn")],
            dw_grid, dw_out, (DW_BLK, D))[0].reshape(1, N_CHIPS, FF_BLK, D)
    tm = min(FFN_ROW_TILE, T)
    extras, outs = _rms_bwd_io(x_in, g, dx_out, T, tm)
    whole = lambda idx: _spec((N_CHIPS, None, FF_BLK, D), lambda i, j, r: (0, idx, 0, 0))
    dx_in, dg = _gemm(
        tag + "_dh",
        [(dab, _spec((None, tm, D_FF), lambda i, j, r: (0, i, 0)), w1buf, whole(w1_idx), "nn"),
         (dab, _spec((None, tm, D_FF), lambda i, j, r: (1, i, 0)), w3buf, whole(w3_idx), "nn")],
        (T // tm, 1, 1), outs, (tm, D), extras, _rms_bwd_epilogue)
    return dx_in, dg


def _proj_sq(name, a, wsq, idx, kind, out_dtype=F32, extras=(), epilogue=None, outs=None):
    M = a.shape[0]
    tm = min(ROW_TILE, M)
    if outs is None:
        outs = [(_sds((M, D), out_dtype), _spec((tm, D), lambda i, j, r: (i, 0)))]
    return _gemm(
        name,
        [(a, _spec((tm, D), lambda i, j, r: (i, 0)),
          wsq, _spec((N_CHIPS, None, SQ_BLK, D), lambda i, j, r: (0, idx, 0, 0)), kind)],
        (M // tm, 1, 1), outs, (tm, D), extras, epilogue)


def _dw_sq(name, a, b):
    M = a.shape[0]
    tn = D // 2
    whole = _gemm(
        name,
        [(a, _spec((M, D), lambda i, j, r: (0, 0)), b, _spec((M, tn), lambda i, j, r: (0, j)), "tn")],
        (1, D // tn, 1),
        [(_sds((D, D), GRAD_WIRE_DTYPE), _spec((D, tn), lambda i, j, r: (0, j)))],
        (D, tn))[0]
    return whole.reshape(N_CHIPS, SQ_BLK, D)


def _retention_constants(T):
    pos = jnp.arange(T, dtype=F32)
    inv_freq = ROPE_BASE ** (-jnp.arange(0, RET_DK, 2, dtype=F32) / RET_DK)
    ang = pos[:, None] * inv_freq[None, :]
    cosf = jnp.concatenate([jnp.cos(ang), jnp.cos(ang)], axis=1)
    sins = jnp.concatenate([-jnp.sin(ang), jnp.sin(ang)], axis=1)
    lg = jnp.log(1.0 - 2.0 ** (-5.0 - jnp.arange(RET_HEADS, dtype=F32)))
    p = jnp.arange(CHUNK, dtype=F32)
    rel = p[:, None] - p[None, :]
    dmat = jnp.where(rel[None] >= 0, jnp.exp(rel[None] * lg[:, None, None]), 0.0)
    kd = jnp.exp((CHUNK - 1.0 - p)[None, :] * lg[:, None])[:, :, None]
    qd = jnp.exp((p + 1.0)[None, :] * lg[:, None])[:, :, None]
    cd = jnp.exp(CHUNK * lg)[:, None, None]
    return cosf, sins, dmat, kd, qd, cd


def _rot(t, cosv, sinv):
    return t * cosv + pltpu.roll(t, RET_DK // 2, 1) * sinv


def _unrot(t, cosv, sinv):
    return t * cosv - pltpu.roll(t, RET_DK // 2, 1) * sinv


def _ret_const_specs(cm):
    whole = lambda shape: _spec(shape, lambda c: (0,) * len(shape))
    return [
        _spec((RET_STEP_ROWS, RET_DK), lambda c: (cm(c), 0)),
        _spec((RET_STEP_ROWS, RET_DK), lambda c: (cm(c), 0)),
        whole((RET_HEADS, CHUNK, CHUNK)), whole((RET_HEADS, CHUNK, 1)), whole((RET_HEADS, CHUNK, 1)),
        whole((RET_HEADS, 1, 1)),
    ]


def _head(h, width):
    return slice(h * width, (h + 1) * width)


def _ret_fwd(u, consts, ret_gn):
    T = u.shape[0]
    nC = T // CHUNK
    kscale = RET_DK ** -0.5

    def body(q_ref, k_ref, v_ref, g_ref, cos_ref, sin_ref, dm_ref, kd_ref, qd_ref, cd_ref, gn_ref,
             qr_ref, kr_ref, ret_ref, yr_ref, st_ref, state):
        @pl.when(pl.program_id(0) == 0)
        def _():
            state[...] = jnp.zeros_like(state)

        for cc in range(RET_STEP_CHUNKS):
            rows = slice(cc * CHUNK, (cc + 1) * CHUNK)
            cosv, sinv = cos_ref[rows, :], sin_ref[rows, :]
            for h in range(RET_HEADS):
                hk, hv = _head(h, RET_DK), _head(h, RET_DV)
                q = _rot(q_ref[rows, hk], cosv, sinv)
                k = _rot(k_ref[rows, hk], cosv, sinv) * kscale
                v = v_ref[rows, hv]
                qr_ref[rows, hk] = q
                kr_ref[rows, hk] = k
                prev = state[h]
                st_ref[h, cc] = prev
                s = _dot(q, k, "nt") * dm_ref[h]
                ret = _dot(s, v, "nn") + _dot(q, prev, "nn") * qd_ref[h]
                state[h] = cd_ref[h] * prev + _dot(k * kd_ref[h], v, "tn")
                ret_ref[rows, hv] = ret
                mu = jnp.mean(ret, axis=-1, keepdims=True)
                xc = ret - mu
                yn = xc * lax.rsqrt(jnp.mean(xc * xc, axis=-1, keepdims=True) + EPS)
                g = g_ref[rows, hv]
                yr_ref[rows, hv] = ((g * _sigmoid(g)) * (yn * gn_ref[:, hv])).astype(BF16)

    cm = lambda c: c
    qk_w, v_w = RET_HEADS * RET_DK, RET_HEADS * RET_DV
    in_specs = [
        _spec((RET_STEP_ROWS, qk_w), lambda c: (c, 0)), _spec((RET_STEP_ROWS, qk_w), lambda c: (c, 1)),
        _spec((RET_STEP_ROWS, v_w), lambda c: (c, 1)), _spec((RET_STEP_ROWS, v_w), lambda c: (c, 2)),
    ] + _ret_const_specs(cm) + [_spec((1, v_w), lambda c: (0, 0))]
    qk_out = _spec((RET_STEP_ROWS, qk_w), lambda c: (c, 0))
    v_out = _spec((RET_STEP_ROWS, v_w), lambda c: (c, 0))
    return _pcall(
        body, name="ret_fwd", grid=(nC // RET_STEP_CHUNKS,),
        in_specs=in_specs,
        out_specs=[qk_out, qk_out, v_out, v_out,
                   _spec((RET_HEADS, RET_STEP_CHUNKS, RET_DK, RET_DV), lambda c: (0, c, 0, 0))],
        out_shape=[_sds((T, qk_w), F32), _sds((T, qk_w), F32), _sds((T, v_w), F32), _sds((T, v_w), BF16),
                   _sds((RET_HEADS, nC, RET_DK, RET_DV), F32)],
        scratch_shapes=[pltpu.VMEM((RET_HEADS, RET_DK, RET_DV), F32)],
    )(u, u, u, u, *consts, ret_gn)


def _ret_bwd(dyr, ret, u, qr, kr, states, consts, ret_gn):
    T = u.shape[0]
    nC = T // CHUNK
    kscale = RET_DK ** -0.5

    def body(dyr_ref, ret_ref, g_ref, q_ref, k_ref, v_ref, st_ref,
             cos_ref, sin_ref, dm_ref, kd_ref, qd_ref, cd_ref, gn_ref,
             dq_ref, dk_ref, dv_ref, dg_ref, dgn_ref, gstate):
        first = pl.program_id(0) == 0

        @pl.when(first)
        def _():
            gstate[...] = jnp.zeros_like(gstate)

        dgn_total = None
        for cc in reversed(range(RET_STEP_CHUNKS)):
            rows = slice(cc * CHUNK, (cc + 1) * CHUNK)
            cosv, sinv = cos_ref[rows, :], sin_ref[rows, :]
            dgn_parts = []
            for h in range(RET_HEADS):
                hk, hv = _head(h, RET_DK), _head(h, RET_DV)
                ret = ret_ref[rows, hv]
                mu = jnp.mean(ret, axis=-1, keepdims=True)
                xc = ret - mu
                rs = lax.rsqrt(jnp.mean(xc * xc, axis=-1, keepdims=True) + EPS)
                yn = xc * rs
                gn = gn_ref[:, hv]
                g = g_ref[rows, hv]
                sg = _sigmoid(g)
                dyr_v = dyr_ref[rows, hv]
                dretn = dyr_v * (g * sg)
                dg_ref[rows, hv] = (dyr_v * (yn * gn) * (sg * (1.0 + g * (1.0 - sg)))).astype(BF16)
                dgn_parts.append(jnp.sum(dretn * yn, axis=0, keepdims=True))
                dyn = dretn * gn
                d_o = rs * (dyn - jnp.mean(dyn, axis=-1, keepdims=True)
                            - yn * jnp.mean(dyn * yn, axis=-1, keepdims=True))

                q, k, v = q_ref[rows, hk], k_ref[rows, hk], v_ref[rows, hv]
                dmat, kd, qd = dm_ref[h], kd_ref[h], qd_ref[h]
                prev = st_ref[h, cc]
                gnext = gstate[h]
                s = _dot(q, k, "nt") * dmat
                ds = _dot(d_o, v, "nt") * dmat
                doq = d_o * qd
                dq = _dot(ds, k, "nn") + _dot(doq, prev, "nt")
                dk = _dot(ds, q, "tn") + _dot(v, gnext, "nt") * kd
                dv = _dot(s, d_o, "tn") + _dot(k * kd, gnext, "nn")
                gstate[h] = cd_ref[h] * gnext + _dot(q, doq, "tn")
                dq_ref[rows, hk] = _unrot(dq, cosv, sinv).astype(BF16)
                dk_ref[rows, hk] = _unrot(dk * kscale, cosv, sinv).astype(BF16)
                dv_ref[rows, hv] = dv.astype(BF16)
            dgn = jnp.concatenate(dgn_parts, axis=1)
            dgn_total = dgn if dgn_total is None else dgn_total + dgn
        _accumulate(dgn_ref, dgn_total, first)

    n_steps = nC // RET_STEP_CHUNKS
    cm = lambda c: n_steps - 1 - c
    qk_w, v_w = RET_HEADS * RET_DK, RET_HEADS * RET_DV
    vspec = lambda blk: _spec((RET_STEP_ROWS, v_w), lambda c: (cm(c), blk))
    qspec = _spec((RET_STEP_ROWS, qk_w), lambda c: (cm(c), 0))
    in_specs = [vspec(0), vspec(0), vspec(2), qspec, qspec, vspec(1),
                _spec((RET_HEADS, RET_STEP_CHUNKS, RET_DK, RET_DV), lambda c: (0, cm(c), 0, 0)),
                ] + _ret_const_specs(cm) + [_spec((1, v_w), lambda c: (0, 0))]
    return _pcall(
        body, name="ret_bwd", grid=(n_steps,),
        in_specs=in_specs,
        out_specs=[qspec, qspec, vspec(0), vspec(0), _spec((1, v_w), lambda c: (0, 0))],
        out_shape=[_sds((T, qk_w), BF16), _sds((T, qk_w), BF16), _sds((T, v_w), BF16), _sds((T, v_w), BF16),
                   _sds((1, v_w), F32)],
        scratch_shapes=[pltpu.VMEM((RET_HEADS, RET_DK, RET_DV), F32)],
    )(dyr, ret, u, qr, kr, u, states, *consts, ret_gn)


def _shift_down(x, s):
    rows = lax.broadcasted_iota(jnp.int32, x.shape, 0)
    return jnp.where(rows >= s, pltpu.roll(x, s, 0), 0.0)


def _shift_up(x, s):
    n = x.shape[0]
    rows = lax.broadcasted_iota(jnp.int32, x.shape, 0)
    return jnp.where(rows < n - s, pltpu.roll(x, n - s, 0), 0.0)


def _lru_specs(T):
    col = lambda off: _spec((T, LRU_BLOCK), lambda g: (0, off + g))
    vec = _spec((1, LRU_BLOCK), lambda g: (0, g))
    wblk = _spec((None, LRU_BLOCK, LRU_BLOCK), lambda g: (g, 0, 0))
    cw = _spec((CONV_TAPS, LRU_BLOCK), lambda g: (0, g))
    return col, vec, wblk, cw


def _lru_gates_fwd(u, conv_w, conv_b, w_r, b_r, w_i, b_i, lam):
    T = u.shape[0]
    col, vec, wblk, cw = _lru_specs(T)

    def body(x_ref, cw_ref, cb_ref, wr_ref, br_ref, wi_ref, bi_ref, lam_ref,
             xc_ref, r_ref, i_ref, a_ref, bx_ref):
        x = x_ref[...]
        w = cw_ref[...]
        xc = (_shift_down(x, 3) * w[0:1] + _shift_down(x, 2) * w[1:2] + _shift_down(x, 1) * w[2:3]
              + x * w[3:4] + cb_ref[...])
        r = _sigmoid(_dot(xc, wr_ref[...], "nn") + br_ref[...])
        i = _sigmoid(_dot(xc, wi_ref[...], "nn") + bi_ref[...])
        la = (-LRU_C) * r * _softplus(-lam_ref[...])
        xc_ref[...] = xc
        r_ref[...] = r
        i_ref[...] = i
        a_ref[...] = jnp.exp(la)
        bx_ref[...] = jnp.sqrt(-_expm1(2.0 * la)) * (i * xc)

    out = col(0)
    return _pcall(
        body, name="lru_gates_fwd", grid=(LRU_BLOCKS,),
        in_specs=[col(24), cw, vec, wblk, vec, wblk, vec, vec],
        out_specs=[out] * 5,
        out_shape=[_sds((T, D), F32)] * 5,
    )(u, conv_w, conv_b, w_r, b_r, w_i, b_i, lam)


def _lru_scan(name, a3, b3, reverse):
    T = a3.shape[0]
    nt = T // SCAN_TILE
    unroll = 8

    def body(a_ref, b_ref, o_ref, carry):
        @pl.when(pl.program_id(0) == 0)
        def _():
            carry[...] = jnp.zeros_like(carry)

        if not reverse:
            def step(t, h):
                h = a_ref[t] * h + b_ref[t]
                o_ref[t] = h
                return h
        else:
            def step(k, c):
                t = SCAN_TILE - 1 - k
                l = b_ref[t] + c
                o_ref[t] = l
                return a_ref[t] * l
        carry[...] = lax.fori_loop(0, SCAN_TILE, step, carry[...], unroll=unroll)

    idx = (lambda i: (nt - 1 - i, 0, 0)) if reverse else (lambda i: (i, 0, 0))
    blk = _spec((SCAN_TILE, LRU_BLOCKS, LRU_BLOCK), idx)
    return _pcall(
        body, name=name, grid=(nt,),
        in_specs=[blk, blk], out_specs=blk,
        out_shape=_sds((T, LRU_BLOCKS, LRU_BLOCK), F32),
        scratch_shapes=[pltpu.VMEM((LRU_BLOCKS, LRU_BLOCK), F32)],
    )(a3, b3)


def _lru_gates_bwd(lmb, hl, a, r, i, xc, u, conv_w, w_r, w_i, lam):
    T = u.shape[0]
    col, vec, wblk, cw = _lru_specs(T)

    def body(l_ref, h_ref, a_ref, r_ref, i_ref, xc_ref, x_ref, cw_ref, wr_ref, wi_ref, lam_ref,
             dx_ref, dwr_ref, dwi_ref, dvec_ref, dcw_ref):
        l = l_ref[...]
        av, rv, iv, xc = a_ref[...], r_ref[...], i_ref[...], xc_ref[...]
        lam_v = lam_ref[...]
        sp = _softplus(-lam_v)
        la = (-LRU_C) * rv * sp
        mult = jnp.sqrt(-_expm1(2.0 * la))
        da = l * _shift_down(h_ref[...], 1)
        dmult = l * (iv * xc)
        di = l * mult * xc
        dxc = l * mult * iv
        dla = da * av - dmult * (av * av) / mult
        dzr = (dla * ((-LRU_C) * sp)) * rv * (1.0 - rv)
        dzi = di * iv * (1.0 - iv)
        dsp = jnp.sum(dla * ((-LRU_C) * rv), axis=0, keepdims=True)
        dlam = dsp * (-_sigmoid(-lam_v))
        dwr_ref[...] = _dot(xc, dzr, "tn")
        dwi_ref[...] = _dot(xc, dzi, "tn")
        dxc = dxc + _dot(dzr, wr_ref[...], "nt") + _dot(dzi, wi_ref[...], "nt")
        x = x_ref[...]
        w = cw_ref[...]
        dx = (dxc * w[3:4] + _shift_up(dxc, 1) * w[2:3] + _shift_up(dxc, 2) * w[1:2]
              + _shift_up(dxc, 3) * w[0:1])
        dx_ref[...] = dx.astype(BF16)
        dvec_ref[...] = jnp.concatenate(
            [jnp.sum(dzr, axis=0, keepdims=True), jnp.sum(dzi, axis=0, keepdims=True), dlam,
             jnp.sum(dxc, axis=0, keepdims=True)], axis=0)
        dcw_ref[...] = jnp.concatenate(
            [jnp.sum(dxc * _shift_down(x, 3 - tap), axis=0, keepdims=True) if tap < 3
             else jnp.sum(dxc * x, axis=0, keepdims=True) for tap in range(CONV_TAPS)], axis=0)

    c0 = col(0)
    return _pcall(
        body, name="lru_gates_bwd", grid=(LRU_BLOCKS,),
        in_specs=[c0, c0, c0, c0, c0, c0, col(24), cw, wblk, wblk, vec],
        out_specs=[c0, wblk, wblk, cw, cw],
        out_shape=[_sds((T, D), BF16), _sds((LRU_BLOCKS, LRU_BLOCK, LRU_BLOCK), F32),
                   _sds((LRU_BLOCKS, LRU_BLOCK, LRU_BLOCK), F32), _sds((4, D), F32), _sds((CONV_TAPS, D), F32)],
    )(lmb, hl, a, r, i, xc, u, conv_w, w_r, w_i, lam)


def _xattn_probs(q, k):
    sc = _dot(q, k, "nt") * (X_HD ** -0.5)
    e = jnp.exp(sc - jnp.max(sc, axis=-1, keepdims=True))
    return e / jnp.sum(e, axis=-1, keepdims=True)


def _xattn_fwd(xq, xk, xv):
    T = xq.shape[0]
    tq = min(WIDE_ROW_TILE, T)
    M = xk.shape[0]

    def body(q_ref, k_ref, v_ref, o_ref):
        p = _xattn_probs(q_ref[...], k_ref[...])
        o_ref[...] = _dot(p, v_ref[...], "nn").astype(BF16)

    qs = _spec((tq, X_HD), lambda h, i: (i, h))
    kv = _spec((M, X_HD), lambda h, i: (0, h))
    return _pcall(
        body, name="xattn_fwd", grid=(X_HEADS, T // tq),
        in_specs=[qs, kv, kv], out_specs=qs, out_shape=_sds((T, D), BF16),
    )(xq, xk, xv)


def _xattn_bwd(xq, xk, xv, dxo):
    T = xq.shape[0]
    tq = min(WIDE_ROW_TILE, T)
    M = xk.shape[0]

    def body(q_ref, k_ref, v_ref, do_ref, dq_ref, dk_ref, dv_ref):
        first = pl.program_id(1) == 0
        q, k, v, do = q_ref[...], k_ref[...], v_ref[...], do_ref[...]
        p = _xattn_probs(q, k)
        dp = _dot(do, v, "nt")
        ds = p * (dp - jnp.sum(dp * p, axis=-1, keepdims=True)) * (X_HD ** -0.5)
        dq_ref[...] = _dot(ds, k, "nn").astype(BF16)
        _accumulate(dk_ref, _dot(ds, q, "tn"), first)
        _accumulate(dv_ref, _dot(p, do, "tn"), first)

    qs = _spec((tq, X_HD), lambda h, i: (i, h))
    kv = _spec((M, X_HD), lambda h, i: (0, h))
    return _pcall(
        body, name="xattn_bwd", grid=(X_HEADS, T // tq),
        in_specs=[qs, kv, kv, qs], out_specs=[qs, kv, kv],
        out_shape=[_sds((T, D), BF16), _sds((M, D), F32), _sds((M, D), F32)],
    )(xq, xk, xv, dxo)


def _adamw(name, w, g, m, v):
    R, C = w.shape
    tr = R
    for cand in (512, 352, 256):
        if R % cand == 0:
            tr = cand
            break

    def fn(irefs, orefs, ids):
        delta, mn, vn = _adamw_update(*(r[...] for r in irefs))
        orefs[0][...] = delta
        orefs[1][...] = mn
        orefs[2][...] = vn

    blk = _spec((tr, C), lambda i: (i, 0))
    return _rowwise(name, fn, [(w, blk), (g, blk), (m, blk), (v, blk)],
                    [(_sds((R, C), F32), blk)] * 3, (R // tr,))


def _adamw_update(wv, gv, mv, vv):
    c1 = 1.0 - ADAM_B1 ** ADAM_STEP
    c2 = 1.0 - ADAM_B2 ** ADAM_STEP
    mn = ADAM_B1 * mv + (1.0 - ADAM_B1) * gv
    vn = ADAM_B2 * vv + (1.0 - ADAM_B2) * (gv * gv)
    delta = -ADAM_LR * ((mn / c1) / (jnp.sqrt(vn / c2) + ADAM_EPS) + ADAM_WD * wv)
    return delta, mn, vn


def _adamw_halves(name, w, mine, theirs, widx, m, v, core):
    R, C = w.shape
    H = R // 2
    tr = H
    while tr * C * 4 > (1 << 20) and tr % 16 == 0:
        tr //= 2
    nb = H // tr

    def body(core_ref, w_ref, mine_ref, theirs_ref, m_ref, v_ref, g_out, d_out, m_out, v_out):
        gv = jnp.where(pl.program_id(0) == core_ref[0], mine_ref[...], theirs_ref[...])
        delta, mn, vn = _adamw_update(w_ref[...], gv, m_ref[...], v_ref[...])
        g_out[...] = gv
        d_out[...] = delta
        m_out[...] = mn
        v_out[...] = vn

    full = pl.BlockSpec((tr, C), lambda h, i, core_ref: (h * nb + i, 0))
    mine_spec = pl.BlockSpec((None, tr, C), lambda h, i, core_ref: (widx, jnp.where(h == core_ref[0], i, 0), 0))
    theirs_spec = pl.BlockSpec((None, tr, C), lambda h, i, core_ref: (widx, jnp.where(h == core_ref[0], 0, i), 0))
    return _pcall(
        body, name=name, grid=(2, nb), num_prefetch=1,
        in_specs=[full, mine_spec, theirs_spec, full, full], out_specs=[full] * 4,
        out_shape=[_sds((R, C), F32)] * 4,
    )(core, w, mine, theirs, m, v)


def _rmsnorm(name, x, g):
    M = x.shape[0]
    tm = min(ROW_TILE, M)

    def fn(irefs, orefs, ids):
        orefs[0][...] = _rms_fwd(irefs[0][...], irefs[1][...]).astype(BF16)

    row = _spec((tm, D), lambda i: (i, 0))
    return _rowwise(name, fn, [(x, row), (g, _spec((1, D), lambda i: (0, 0)))],
                    [(_sds((M, D), BF16), row)], (M // tm,))[0]


WEIGHT_AT = {
    "ffn1_w1": ("col1", 0), "ffn1_w3": ("col1", 1), "ffn1_w2": ("row2a", 0),
    "w_ret_o": ("sqA", 0), "w_lru_o": ("sqA", 1), "w_out": ("sqA", 2),
    "w_xq": ("sqB", 0), "w_xk": ("sqB", 1), "w_xv": ("sqC", 0), "w_xo": ("sqC", 1),
    "ffn2_w1": ("col2a", 0), "ffn2_w3": ("col2b", 0), "ffn2_w2": ("row2b", 0),
}


def _local_step(x, mem, tgt, gw, sm, big):
    T = x.shape[0]
    tm = ROW_TILE

    def wt(name):
        key, idx = WEIGHT_AT[name]
        return gw[key], idx

    row3 = lambda i, j, r: (i, 0)
    vec3 = lambda i, j, r: (0, 0)
    rowD = _spec((tm, D), row3)
    vecD = _spec((1, D), vec3)

    def residual_norm(acc, erefs, orefs, ids):
        xo = erefs[0][...] + acc
        orefs[0][...] = xo
        orefs[1][...] = _rms_fwd(xo, erefs[1][...]).astype(BF16)

    def res_norm_io(x_res, g):
        return ([(x_res, rowD), (g, vecD)],
                [(_sds((T, D), F32), rowD), (_sds((T, D), BF16), rowD)])

    a1, b1, s1, h1 = _ffn_up("ffn1_up", x, *wt("ffn1_w1"), *wt("ffn1_w3"), norm_gain=sm["ffn1_norm"])
    x1, h2 = _ffn_down("ffn1_down", s1, *wt("ffn1_w2"), x, sm["mix_norm"])

    tw = min(WIDE_ROW_TILE, T)
    wideD = _spec((tw, D), row3)
    u = _gemm(
        "mix_in",
        [(h2, wideD, gw["win"], _spec((None, None, IN_BLK, D), lambda i, j, r: (j, 0, 0, 0)), "nt")],
        (T // tw, N_CHIPS, 1),
        [(_sds((T, 5120), F32), _spec((tw, IN_BLK), lambda i, j, r: (i, j)))], (tw, IN_BLK))[0]

    consts = _retention_constants(T)
    qr, kr, ret, yr, states = _ret_fwd(u, consts, sm["ret_gn"])

    conv_w = gw["conv"][:, 0].transpose(1, 0, 2).reshape(CONV_TAPS, D)
    xc, rg, ig, av, bx = _lru_gates_fwd(u, conv_w, sm["conv_b"], sm["w_rgate"], sm["b_rgate"],
                                        sm["w_igate"], sm["b_igate"], sm["lru_lambda"])
    a3 = av.reshape(T, LRU_BLOCKS, LRU_BLOCK)
    b3 = bx.reshape(T, LRU_BLOCKS, LRU_BLOCK)

    def gate_epilogue(acc, erefs, orefs, ids):
        orefs[0][...] = _sigmoid(acc + erefs[0][...])

    gates = _gemm(
        "mix_gates",
        [(h2, wideD, gw["wbg"], _spec((None, None, BG_BLK, D), lambda i, j, r: (j, 0, 0, 0)), "nt")],
        (T // tw, N_CHIPS, 1),
        [(_sds((T, 2 * D), F32), _spec((tw, BG_BLK), lambda i, j, r: (i, j)))], (tw, BG_BLK),
        [(sm["b_branch_gate"], _spec((1, BG_BLK), lambda i, j, r: (0, j)))], gate_epilogue)[0]

    hl = _lru_scan("lru_scan_fwd", a3, b3, False).reshape(T, D)

    row1 = _spec((tm, D), lambda i: (i, 0))
    glru1 = _spec((tm, D), lambda i: (i, 4))

    def lru_out(irefs, orefs, ids):
        gl, _ = _gelu_and_grad(irefs[1][...])
        orefs[0][...] = (irefs[0][...] * gl).astype(BF16)

    yl = _rowwise("lru_out", lru_out, [(hl, row1), (u, glru1)], [(_sds((T, D), BF16), row1)], (T // tm,))[0]

    y_ret = _proj_sq("y_ret", yr, *wt("w_ret_o"), "nn")[0]

    def merge_epilogue(acc, erefs, orefs, ids):
        orefs[0][...] = acc
        orefs[1][...] = (erefs[0][...] * erefs[2][...] + erefs[1][...] * acc).astype(BF16)

    y_lru, merged = _proj_sq(
        "y_lru", yl, *wt("w_lru_o"), "nn",
        extras=[(gates, _spec((tm, D), lambda i, j, r: (i, 0))), (gates, _spec((tm, D), lambda i, j, r: (i, 1))),
                (y_ret, rowD)],
        epilogue=merge_epilogue,
        outs=[(_sds((T, D), F32), rowD), (_sds((T, D), BF16), rowD)])

    ex, ou = res_norm_io(x1, sm["xattn_norm"])
    x2, hq = _proj_sq("mix_out", merged, *wt("w_out"), "nn", extras=ex, epilogue=residual_norm, outs=ou)

    m = _rmsnorm("mem_norm", mem, sm["mem_norm"])
    xq = _proj_sq("xq", hq, *wt("w_xq"), "nn", BF16)[0]
    xk = _proj_sq("xk", m, *wt("w_xk"), "nn", BF16)[0]
    xv = _proj_sq("xv", m, *wt("w_xv"), "nn", BF16)[0]
    xo = _xattn_fwd(xq, xk, xv)
    ex, ou = res_norm_io(x2, sm["ffn2_norm"])
    x3, h3 = _proj_sq("xattn_out", xo, *wt("w_xo"), "nn", extras=ex, epilogue=residual_norm, outs=ou)

    a2, b2, s2 = _ffn_up("ffn2_up", h3, *wt("ffn2_w1"), *wt("ffn2_w3"))
    loss, dx4, dg_final = _ffn_down("ffn2_down", s2, *wt("ffn2_w2"), x3, sm["final_norm"], loss_target=tgt)

    dx3, dg_ffn2 = _ffn_bwd("ffn2", dx4, h3, a2, b2, s2, *wt("ffn2_w1"), *wt("ffn2_w3"),
                            *wt("ffn2_w2"), x3, sm["ffn2_norm"], big)

    dxo = _proj_sq("d_xo", dx3, *wt("w_xo"), "nt", BF16)[0]
    big["w_xo"] = _dw_sq("dw_xo", xo, dx3)[None]
    dxq, dxk, dxv = _xattn_bwd(xq, xk, xv, dxo)
    big["w_xq"] = _dw_sq("dw_xq", hq, dxq)[None]
    ex, ou = _rms_bwd_io(x2, sm["xattn_norm"], dx3, T, tm)
    dx2, dg_xattn = _proj_sq("d_hq", dxq, *wt("w_xq"), "nt", extras=ex, epilogue=_rms_bwd_epilogue, outs=ou)
    big["w_xk"] = _dw_sq("dw_xk", m, dxk)[None]
    big["w_xv"] = _dw_sq("dw_xv", m, dxv)[None]

    M = mem.shape[0]

    def mem_norm_epilogue(acc, erefs, orefs, ids):
        _, dgp = _rms_bwd(erefs[0][...], erefs[1][...], acc)
        orefs[0][...] = dgp

    wsq_spec = lambda idx: _spec((N_CHIPS, None, SQ_BLK, D), lambda i, j, r: (0, idx, 0, 0))
    memD = _spec((M, D), row3)
    dg_mem = _gemm(
        "d_mem_norm",
        [(dxk, memD, wt("w_xk")[0], wsq_spec(wt("w_xk")[1]), "nt"),
         (dxv, memD, wt("w_xv")[0], wsq_spec(wt("w_xv")[1]), "nt")],
        (1, 1, 1), [(_sds((1, D), F32), vecD)], (M, D),
        [(mem, memD), (sm["mem_norm"], vecD)], mem_norm_epilogue)[0]

    def merged_bwd_epilogue(acc, erefs, orefs, ids):
        gr, gl, yrv, ylv = (e[...] for e in erefs)
        orefs[0][...] = (acc * gr).astype(BF16)
        orefs[1][...] = (acc * gl).astype(BF16)
        dgr = acc * yrv * gr * (1.0 - gr)
        dgl = acc * ylv * gl * (1.0 - gl)
        orefs[2][:, :D] = dgr.astype(BF16)
        orefs[2][:, D:] = dgl.astype(BF16)
        dbb = jnp.concatenate([jnp.sum(dgr, axis=0, keepdims=True), jnp.sum(dgl, axis=0, keepdims=True)], axis=1)
        _accumulate(orefs[3], dbb, ids[0] == 0)

    dy_ret, dy_lru, dgpre, db_bg = _proj_sq(
        "d_merged", dx2, *wt("w_out"), "nt",
        extras=[(gates, _spec((tm, D), lambda i, j, r: (i, 0))), (gates, _spec((tm, D), lambda i, j, r: (i, 1))),
                (y_ret, rowD), (y_lru, rowD)],
        epilogue=merged_bwd_epilogue,
        outs=[(_sds((T, D), BF16), rowD), (_sds((T, D), BF16), rowD),
              (_sds((T, 2 * D), BF16), _spec((tm, 2 * D), row3)),
              (_sds((1, 2 * D), F32), _spec((1, 2 * D), vec3))])
    big["w_branch_gate"] = _gemm(
        "dw_bg",
        [(h2, _spec((T, D), lambda j, n, r: (r, 0)), dgpre, _spec((T, BG_BLK), lambda j, n, r: (r, j)), "tn")],
        (N_CHIPS, 1, 1),
        [(_sds((N_CHIPS, D, BG_BLK), GRAD_WIRE_DTYPE), _spec((None, D, BG_BLK), lambda j, n, r: (j, 0, 0)))],
        (D, BG_BLK))[0][None]
    big["w_out"] = _dw_sq("dw_out", merged, dx2)[None]
    dyr = _proj_sq("d_yr", dy_ret, *wt("w_ret_o"), "nt")[0]
    big["w_ret_o"] = _dw_sq("dw_ret_o", yr, dy_ret)[None]
    dyl = _proj_sq("d_yl", dy_lru, *wt("w_lru_o"), "nt")[0]
    big["w_lru_o"] = _dw_sq("dw_lru_o", yl, dy_lru)[None]

    def lru_out_bwd(irefs, orefs, ids):
        gl, dgl = _gelu_and_grad(irefs[2][...])
        dyl_v = irefs[0][...]
        orefs[0][...] = dyl_v * gl
        orefs[1][...] = (dyl_v * irefs[1][...] * dgl).astype(BF16)

    dhl, dglru = _rowwise("lru_out_bwd", lru_out_bwd, [(dyl, row1), (hl, row1), (u, glru1)],
                          [(_sds((T, D), F32), row1), (_sds((T, D), BF16), row1)], (T // tm,))
    dhl3 = dhl.reshape(T, LRU_BLOCKS, LRU_BLOCK)
    dq, dk, dv, dgr, dg_retgn = _ret_bwd(dyr, ret, u, qr, kr, states, consts, sm["ret_gn"])
    lmb = _lru_scan("lru_scan_bwd", a3, dhl3, True).reshape(T, D)
    dxl, dw_r, dw_i, dvec, dcw = _lru_gates_bwd(lmb, hl, av, rg, ig, xc, u, conv_w,
                                                sm["w_rgate"], sm["w_igate"], sm["lru_lambda"])

    du = jnp.concatenate([dq, dk, dv, dgr, dxl, dglru], axis=1)
    tk = T
    big["w_in"] = _gemm(
        "dw_in",
        [(h2, _spec((tk, D), lambda j, n, r: (r, 0)), du, _spec((tk, IN_BLK), lambda j, n, r: (r, j)), "tn")],
        (N_CHIPS, 1, T // tk),
        [(_sds((N_CHIPS, D, IN_BLK), GRAD_WIRE_DTYPE), _spec((None, D, IN_BLK), lambda j, n, r: (j, 0, 0)))],
        (D, IN_BLK))[0][None]
    tf = min(FFN_ROW_TILE, T)
    ex, ou = _rms_bwd_io(x1, sm["mix_norm"], dx2, T, tf)
    dx1, dg_mix = _gemm(
        "d_h2",
        [(du, _spec((tf, 5120), row3), gw["win"], _spec((N_CHIPS, None, IN_BLK, D), lambda i, j, r: (0, 0, 0, 0)), "nn"),
         (dgpre, _spec((tf, 2 * D), row3), gw["wbg"], _spec((N_CHIPS, None, BG_BLK, D), lambda i, j, r: (0, 0, 0, 0)),
          "nn")],
        (T // tf, 1, 1), ou, (tf, D), ex, _rms_bwd_epilogue)

    grad_x, dg_ffn1 = _ffn_bwd("ffn1", dx1, h1, a1, b1, s1, *wt("ffn1_w1"), *wt("ffn1_w3"),
                               *wt("ffn1_w2"), x, sm["ffn1_norm"], big)

    small = {
        "ffn1_norm": dg_ffn1, "mix_norm": dg_mix, "ret_gn": dg_retgn, "conv_b": dvec[3:4],
        "b_rgate": dvec[0:1], "b_igate": dvec[1:2], "lru_lambda": dvec[2:3], "xattn_norm": dg_xattn,
        "mem_norm": dg_mem, "ffn2_norm": dg_ffn2, "final_norm": dg_final, "b_branch_gate": db_bg,
        "conv_w": dcw, "w_rgate": dw_r, "w_igate": dw_i,
    }
    return loss, grad_x, small


ANY_SPEC = pl.BlockSpec(memory_space=pl.ANY)
VMEM_SPEC = pl.BlockSpec(memory_space=pltpu.VMEM)
N_PEER_CHIPS = N_CHIPS - 1


def _mesh_position():
    x, y, c = lax.axis_index("x"), lax.axis_index("y"), lax.axis_index("c")
    chips = [(1 - x, y), (x, 1 - y), (1 - x, 1 - y)]
    return x, y, c, chips


def _chip_index(x, y):
    return 2 * x + y


def _rows_half(ref, axis, h):
    n = ref.shape[axis] // 2
    idx = [slice(None)] * len(ref.shape)
    idx[axis] = pl.ds(pl.multiple_of(h * n, BF16_TILE_ROWS), n)
    return ref.at[tuple(idx)]


def _remote(src, dst, send_sem, recv_sem, device):
    return pltpu.make_async_remote_copy(src_ref=src, dst_ref=dst, send_sem=send_sem, recv_sem=recv_sem,
                                        device_id=device, device_id_type=MESH)


def _gather_chips_task(shards, split, landed, legs="both"):
    keys = list(shards)
    n = len(keys)

    def operands():
        if legs == "pass_on":
            return [landed[k] for k in keys]
        chip_me = _chip_index(lax.axis_index("x"), lax.axis_index("y"))
        return [lax.dynamic_update_slice(lax.empty((N_CHIPS,) + shards[k].shape, shards[k].dtype), shards[k][None],
                                         (chip_me,) + (0,) * shards[k].ndim) for k in keys]

    def my_rows(ref, c):
        return _rows_half(ref, 1, c)

    def make_direct(ins, outs, send_sem, recv_sem):
        x, y, c, chips = _mesh_position()
        s_me = _chip_index(x, y)
        starts, arrivals = [], []
        for g in range(n):
            for k, chip in enumerate(chips):
                sems = (send_sem(3 * g + k), recv_sem(3 * g + k))
                starts.append(functools.partial(_remote, outs[g].at[s_me], outs[g].at[s_me], *sems, (*chip, c)))
                got = outs[g].at[_chip_index(*chip)]
                arrivals.append(functools.partial(_remote, got, got, *sems, (*chip, c)))
        return starts, arrivals

    def make_swap(ins, outs, send_sem, recv_sem):
        x, y, c, _ = _mesh_position()
        first, _ = _axis_neighbours(x, y, c)
        starts, arrivals = [], []
        for g in range(n):
            sems = (send_sem(3 * g), recv_sem(3 * g))
            mine = my_rows(outs[g].at[_chip_index(x, y)], c)
            starts.append(functools.partial(_remote, mine, mine, *sems, (*first, c)))
            got = my_rows(outs[g].at[_chip_index(*first)], c)
            arrivals.append(functools.partial(_remote, got, got, *sems, (*first, c)))
        return starts, arrivals

    def make_pass_on(ins, outs, send_sem, recv_sem):
        x, y, c, _ = _mesh_position()
        first, second = _axis_neighbours(x, y, c)
        diagonal = (1 - x, 1 - y)
        starts, arrivals = [], []
        for g in range(n):
            half = lambda chip: my_rows(outs[g].at[_chip_index(*chip)], c)
            for k, (sent, arriving) in enumerate([((x, y), second), (first, diagonal)]):
                sems = (send_sem(3 * g + 1 + k), recv_sem(3 * g + 1 + k))
                starts.append(functools.partial(_remote, half(sent), half(sent), *sems, (*second, c)))
                arrivals.append(functools.partial(_remote, half(arriving), half(arriving), *sems, (*second, c)))
        return starts, arrivals

    def finish(res):
        landed.update(zip(keys, res))

    shapes = lambda: [_sds((N_CHIPS,) + shards[k].shape, shards[k].dtype) for k in keys]
    aliases = {g: g for g in range(n)}
    if not split:
        return _Task("chips", operands, shapes, aliases, 3 * n, make_direct, finish)
    if legs == "swap":
        return _Task("first", operands, shapes, aliases, 3 * n, make_swap, finish)
    if legs == "pass_on":
        return _Task("second", operands, shapes, aliases, 3 * n, make_pass_on, finish)
    return _Task("first+second", operands, shapes, aliases, 3 * n, make_swap, finish, make_second=make_pass_on)


def _gather_sibling_task(keys, landed, ready):
    n = len(keys)

    def make(ins, outs, send_sem, recv_sem):
        x, y, c, chips = _mesh_position()
        starts, arrivals = [], []
        for g in range(n):
            for k, chip in enumerate(chips):
                o = outs[g].at[_chip_index(*chip)]
                got, other = _rows_half(o, 1, c), _rows_half(o, 1, 1 - c)
                starts.append(functools.partial(_remote, got, got, send_sem(3 * g + k), recv_sem(3 * g + k),
                                                (x, y, 1 - c)))
                arrivals.append(functools.partial(_remote, other, other, send_sem(3 * g + k), recv_sem(3 * g + k),
                                                  (x, y, 1 - c)))
        return starts, arrivals

    def finish(res):
        ready.update(zip(keys, res))

    return _Task("sibling", lambda: [landed[k] for k in keys],
                 lambda: [_sds(landed[k].shape, landed[k].dtype) for k in keys],
                 {g: g for g in range(n)}, 3 * n, make, finish)


def _pair_swap_task(names, big, got):
    n = len(names)

    def make(ins, outs, send_sem, recv_sem):
        x, y, c, _ = _mesh_position()
        copies = [functools.partial(_remote, _rows_half(ins[a], 2, 1 - c), outs[a], send_sem(a), recv_sem(a),
                                    (x, y, 1 - c)) for a in range(n)]
        return copies, copies

    def shapes():
        return [_sds(big[k].shape[:2] + (big[k].shape[2] // 2, big[k].shape[3]), big[k].dtype) for k in names]

    return _Task("sibling", lambda: [big[k] for k in names], shapes, {}, n, make,
                 lambda res: got.update(zip(names, res)))


def _rs_pair_sum(name, fulls, gots, core):
    n = len(fulls)
    shapes = [(f.shape[2] // 2, f.shape[3]) for f in fulls]

    def body(core_ref, *refs):
        for a_ref, b_ref, o_ref in zip(refs[:n], refs[n:2 * n], refs[2 * n:]):
            o_ref[...] = (a_ref[...].astype(F32) + b_ref[...].astype(F32)).astype(BF16)

    mine = [pl.BlockSpec((None, None) + hc, lambda s, core_ref: (0, s, core_ref[0], 0)) for hc in shapes]
    slot = [pl.BlockSpec((None, None) + hc, lambda s, core_ref: (0, s, 0, 0)) for hc in shapes]
    return _pcall(
        body, name=name, grid=(N_CHIPS,), num_prefetch=1,
        in_specs=mine + slot, out_specs=slot,
        out_shape=[_sds((1, N_CHIPS) + hc, BF16) for hc in shapes],
    )(core, *fulls, *gots)


def _chip_exchange_task(names, pair_sums, by_source, part=0, nparts=1):
    n = len(names)

    def rows(ref):
        h = ref.shape[1] // nparts
        return ref.at[:, pl.ds(part * h, h), :]

    def make(ins, outs, send_sem, recv_sem):
        x, y, c, chips = _mesh_position()
        s_me = _chip_index(x, y)
        starts, arrivals = [], []
        for a in range(n):
            for k, chip in enumerate(chips):
                s_k = _chip_index(*chip)
                starts.append(functools.partial(_remote, rows(ins[a].at[:, s_k]), rows(outs[a].at[:, s_me]),
                                                send_sem(3 * a + k), recv_sem(3 * a + k), (*chip, c)))
                got = rows(outs[a].at[:, s_k])
                arrivals.append(functools.partial(_remote, got, got, send_sem(3 * a + k), recv_sem(3 * a + k),
                                                  (*chip, c)))
        return starts, arrivals

    def operands():
        return [pair_sums[k] for k in names] + ([by_source[k] for k in names] if part else [])

    return _Task("chips", operands, lambda: [_sds(pair_sums[k].shape, pair_sums[k].dtype) for k in names],
                 {n + a: a for a in range(n)} if part else {}, 3 * n, make,
                 lambda res: by_source.update(zip(names, res)))


def _rs_chip_sum(name, owns, parts, chip):
    n = len(owns)
    ns = N_CHIPS
    shapes = [p.shape[2:] for p in parts]

    def body(chip_ref, *refs):
        me = chip_ref[0]
        for i in range(n):
            own_v = refs[i][...].astype(F32)
            slots = refs[n + ns * i:n + ns * (i + 1)]
            tot = None
            for s in range(ns):
                term = jnp.where(me == s, own_v, slots[s][...].astype(F32))
                tot = term if tot is None else tot + term
            refs[n + ns * n + i][...] = tot

    def slot_spec(hc, s):
        return pl.BlockSpec((None, None) + hc,
                            lambda g, chip_ref: (0, jnp.where(chip_ref[0] == s, (s + 1) % ns, s), 0, 0))

    own_specs = [pl.BlockSpec((None, None) + hc, lambda g, chip_ref: (0, chip_ref[0], 0, 0)) for hc in shapes]
    slot_specs = [slot_spec(hc, s) for hc in shapes for s in range(ns)]
    return _pcall(
        body, name=name, grid=(1,), num_prefetch=1,
        in_specs=own_specs + slot_specs,
        out_specs=[pl.BlockSpec((None,) + hc, lambda g, chip_ref: (0, 0, 0)) for hc in shapes],
        out_shape=[_sds((1,) + hc, F32) for hc in shapes],
    )(chip, *owns, *[p for p in parts for _ in range(ns)])


def _pair_gather_task(names, halves, sibling_halves):
    n = len(names)

    def make(ins, outs, send_sem, recv_sem):
        x, y, c, _ = _mesh_position()
        copies = [functools.partial(_remote, ins[a], outs[a], send_sem(a), recv_sem(a), (x, y, 1 - c))
                  for a in range(n)]
        return copies, copies

    return _Task("sibling", lambda: [halves[k] for k in names], lambda: [_sds(halves[k].shape, F32) for k in names],
                 {}, n, make, lambda res: sibling_halves.update(zip(names, res)))


def _small_allreduce(arrs):
    n = len(arrs)
    per = 1 + 2 * N_PEER_CHIPS

    def body(*refs):
        v_refs, o_refs = refs[:n], refs[n:2 * n]
        sib, pair, part = refs[2 * n:3 * n], refs[3 * n:4 * n], refs[4 * n:5 * n]
        send_sems, recv_sems = refs[5 * n:]
        x, y, c, chips = _mesh_position()
        s_me = _chip_index(x, y)

        def quarter(ref, s):
            q = ref.shape[0] // N_CHIPS
            return ref.at[pl.ds(pl.multiple_of(s * q, F32_TILE_ROWS), q)]

        def exchange(first_sem, src, dst_of, arrival_of):
            sems = lambda a, k: (send_sems.at[a * per + first_sem + k], recv_sems.at[a * per + first_sem + k])
            sends = [_remote(src(a, _chip_index(*chip)), dst_of(a, s_me), *sems(a, k), (*chip, c))
                     for a in range(n) for k, chip in enumerate(chips)]
            for cp in sends:
                cp.start()
            for a in range(n):
                for k, chip in enumerate(chips):
                    got = arrival_of(a, _chip_index(*chip))
                    _remote(got, got, *sems(a, k), (*chip, c)).wait_recv()
            for cp in sends:
                cp.wait_send()

        swaps = [_remote(v_refs[a], sib[a], send_sems.at[a * per], recv_sems.at[a * per], (x, y, 1 - c))
                 for a in range(n)]
        for cp in swaps:
            cp.start()
        for cp in swaps:
            cp.wait()
        for a in range(n):
            pair[a][...] = v_refs[a][...] + sib[a][...]
        exchange(1, lambda a, s_k: quarter(pair[a], s_k), lambda a, s: part[a].at[s], lambda a, s_k: part[a].at[s_k])
        for a in range(n):
            part[a][s_me] = quarter(pair[a], s_me)[...]
            q = o_refs[a].shape[0] // N_CHIPS
            o_refs[a][pl.ds(pl.multiple_of(s_me * q, F32_TILE_ROWS), q), :] = (
                ((part[a][0] + part[a][1]) + part[a][2]) + part[a][3])
        exchange(1 + N_PEER_CHIPS, lambda a, s_k: quarter(o_refs[a], s_me), lambda a, s: quarter(o_refs[a], s),
                 lambda a, s_k: quarter(o_refs[a], s_k))

    shapes = [a.shape for a in arrs]
    return _pcall(
        body, name="small_allreduce", grid=(1,), own_peers=("sibling", "chips"),
        in_specs=[VMEM_SPEC] * n, out_specs=[VMEM_SPEC] * n, out_shape=[_sds(s, F32) for s in shapes],
        scratch_shapes=([pltpu.VMEM(s, F32) for s in shapes] * 2
                        + [pltpu.VMEM((N_CHIPS, s[0] // N_CHIPS, s[1]), F32) for s in shapes]
                        + [pltpu.SemaphoreType.DMA((n * per,)), pltpu.SemaphoreType.DMA((n * per,))]),
    )(*arrs)


TRANSPOSED_WEIGHTS = ("ffn1_w1", "ffn1_w3", "ffn2_w1", "ffn2_w3")
SMALL_LAYOUT = [("ffn1_norm", 1), ("mix_norm", 1), ("ret_gn", 1), ("conv_b", 1), ("b_rgate", 1), ("b_igate", 1),
                ("lru_lambda", 1), ("xattn_norm", 1), ("mem_norm", 1), ("ffn2_norm", 1), ("final_norm", 1),
                ("b_branch_gate", 2), ("conv_w", CONV_TAPS)]
SMALL_ROWS = 32
GATE_WEIGHTS = ("w_rgate", "w_igate")
WEIGHT_ORDER = ["ffn1_norm", "ffn1_w1", "ffn1_w3", "ffn1_w2", "mix_norm", "w_in", "ret_gn", "w_ret_o", "conv_w",
                "conv_b", "w_rgate", "b_rgate", "w_igate", "b_igate", "lru_lambda", "w_lru_o", "w_branch_gate",
                "b_branch_gate", "w_out", "xattn_norm", "mem_norm", "w_xq", "w_xk", "w_xv", "w_xo", "ffn2_norm",
                "ffn2_w1", "ffn2_w3", "ffn2_w2", "final_norm"]


SMALL_USED_ROWS = sum(n for _, n in SMALL_LAYOUT)


def _pack_small(parts, extra_row=None):
    rows = [parts[name].reshape(n, D) for name, n in SMALL_LAYOUT]
    if extra_row is not None:
        rows.append(extra_row)
    rows.append(jnp.zeros((SMALL_ROWS - sum(r.shape[0] for r in rows), D), F32))
    return jnp.concatenate(rows, axis=0)


def _unpack_small(packed, shapes):
    out, r = {}, 0
    for name, n in SMALL_LAYOUT:
        out[name] = packed[r:r + n].reshape(shapes[name])
        r += n
    return out


def kernel(x, mem, ffn1_norm, ffn1_w1, ffn1_w3, ffn1_w2, mix_norm, w_in, ret_gn, w_ret_o, conv_w, conv_b, w_rgate, b_rgate, w_igate, b_igate, lru_lambda, w_lru_o, w_branch_gate, b_branch_gate, w_out, xattn_norm, mem_norm, w_xq, w_xk, w_xv, w_xo, ffn2_norm, ffn2_w1, ffn2_w3, ffn2_w2, final_norm, loss_target, m_ffn1_norm, m_ffn1_w1, m_ffn1_w3, m_ffn1_w2, m_mix_norm, m_w_in, m_ret_gn, m_w_ret_o, m_conv_w, m_conv_b, m_w_rgate, m_b_rgate, m_w_igate, m_b_igate, m_lru_lambda, m_w_lru_o, m_w_branch_gate, m_b_branch_gate, m_w_out, m_xattn_norm, m_mem_norm, m_w_xq, m_w_xk, m_w_xv, m_w_xo, m_ffn2_norm, m_ffn2_w1, m_ffn2_w3, m_ffn2_w2, m_final_norm, v_ffn1_norm, v_ffn1_w1, v_ffn1_w3, v_ffn1_w2, v_mix_norm, v_w_in, v_ret_gn, v_w_ret_o, v_conv_w, v_conv_b, v_w_rgate, v_b_rgate, v_w_igate, v_b_igate, v_lru_lambda, v_w_lru_o, v_w_branch_gate, v_b_branch_gate, v_w_out, v_xattn_norm, v_mem_norm, v_w_xq, v_w_xk, v_w_xv, v_w_xo, v_ffn2_norm, v_ffn2_w1, v_ffn2_w3, v_ffn2_w2, v_final_norm):
    given = dict(locals())
    w = {n: given[n] for n in WEIGHT_ORDER}
    mom = {n: given["m_" + n] for n in WEIGHT_ORDER}
    var = {n: given["v_" + n] for n in WEIGHT_ORDER}
    chip = _chip_index(lax.axis_index("x"), lax.axis_index("y"))
    core = lax.axis_index("c").astype(jnp.int32).reshape(1)

    chip_id = chip.astype(jnp.int32).reshape(1)
    sm = {n: w[n] for n in ["ffn1_norm", "mix_norm", "ret_gn", "conv_b", "b_rgate", "b_igate", "lru_lambda",
                            "xattn_norm", "mem_norm", "ffn2_norm", "b_branch_gate"]}
    sm["final_norm"] = w["final_norm"].reshape(1, D)
    sm["w_rgate"] = w["w_rgate"][0]
    sm["w_igate"] = w["w_igate"][0]

    local = lambda a, n: jnp.swapaxes(a[0], 0, 1) if n in TRANSPOSED_WEIGHTS else a[0]
    stack = lambda names: jnp.stack([local(w[n], n) for n in names], axis=0).astype(BF16)
    shard = {"col1": stack(["ffn1_w1", "ffn1_w3"]), "row2a": stack(["ffn1_w2"]),
             "win": jnp.swapaxes(w["w_in"], 1, 2).astype(BF16),
             "wbg": jnp.swapaxes(w["w_branch_gate"], 1, 2).astype(BF16),
             "sqA": stack(["w_ret_o", "w_lru_o", "w_out"]), "sqB": stack(["w_xq", "w_xk"]),
             "sqC": stack(["w_xv", "w_xo"]), "col2a": stack(["ffn2_w1"]), "col2b": stack(["ffn2_w3"]),
             "row2b": stack(["ffn2_w2"]), "conv": w["conv_w"]}
    gw, landed = {}, {}
    over_chips = lambda keys: _gather_chips_task({k: shard[k] for k in keys}, True, landed)
    to_sibling = lambda keys: _gather_sibling_task(keys, landed, gw)

    big, got, pair_sums, by_source, halves, sibling_halves, outs = {}, {}, {}, {}, {}, {}, {}
    pair_swap = lambda names: _pair_swap_task(names, big, got)
    exchange = lambda names, part=0, nparts=1: _chip_exchange_task(names, pair_sums, by_source, part, nparts)
    pair_gather = lambda names: _pair_gather_task(names, halves, sibling_halves)

    def pair_sum(names):
        res = _rs_pair_sum("rs_pair_sum_" + names[0], [big[n] for n in names], [got[n] for n in names], core)
        pair_sums.update(zip(names, res))

    def chip_sum(names):
        res = _rs_chip_sum("rs_chip_sum_" + names[0], [pair_sums[n] for n in names], [by_source[n] for n in names],
                           chip_id)
        halves.update(zip(names, res))

    def adamw(names):
        for n in names:
            res = _adamw_halves("adamw_" + n, local(w[n], n), halves[n], sibling_halves[n], 0, local(mom[n], n),
                                local(var[n], n), core)
            outs[n] = tuple((jnp.swapaxes(r, 0, 1) if n in TRANSPOSED_WEIGHTS else r)[None] for r in res)

    do = lambda fn, names: functools.partial(fn, names)
    ffn2_grads = ["ffn2_w2", "ffn2_w1", "ffn2_w3"]
    xattn_grads = ["w_xo", "w_xq", "w_xk", "w_xv"]
    mix_out_grads = ["w_branch_gate", "w_out", "w_ret_o", "w_lru_o"]
    conv_gather = _gather_chips_task({"conv": shard["conv"]}, False, gw)
    swap = lambda key: _gather_chips_task({key: shard[key]}, True, landed, legs="swap")
    pass_on = lambda key: _gather_chips_task({key: shard[key]}, True, landed, legs="pass_on")
    plan = _Plan()
    plan.tasks = {
        "ag_first_chips": [over_chips(["col1"]), swap("row2a")],
        "ag_first_sibling": [to_sibling(["col1"]), pass_on("row2a"), swap("win")],
        "ffn1_up": [to_sibling(["row2a"]), pass_on("win"), swap("wbg")],
        "ffn1_down": [to_sibling(["win"]), pass_on("wbg"), swap("sqA")],
        "mix_in": [to_sibling(["wbg"]), pass_on("sqA"), swap("col2a"), conv_gather],
        "ret_fwd": [to_sibling(["sqA"]), pass_on("col2a"), swap("sqB")],
        "lru_gates_fwd": [to_sibling(["col2a"]), pass_on("sqB"), swap("sqC")],
        "mix_gates": [to_sibling(["sqB"]), pass_on("sqC"), swap("col2b")],
        "lru_scan_fwd": [to_sibling(["sqC"]), pass_on("col2b")],
        "y_lru": [to_sibling(["col2b"]), swap("row2b")],
        "ffn2_up": [pass_on("row2b")],
        "ffn2_up_sibling": [to_sibling(["row2b"])],
        "ffn2_dh": [pair_swap(ffn2_grads)],
        "xattn_bwd": [exchange(["ffn2_w2"], 0, 2)],
        "d_hq": [exchange(["ffn2_w2"], 1, 2)],
        "d_merged": [exchange(["ffn2_w1"], 0, 2), pair_swap(xattn_grads)],
        "lru_out_bwd": [exchange(["w_xo"])],
        "ret_bwd": [exchange(["ffn2_w1"], 1, 2), exchange(["ffn2_w3"], 0, 2), pair_swap(mix_out_grads)],
        "lru_scan_bwd": [exchange(["ffn2_w3"], 1, 2)],
        "lru_gates_bwd": [exchange(["w_xq", "w_xk"]), pair_gather(ffn2_grads)],
        "dw_in": [exchange(["w_xv", "w_out"])],
        "d_h2": [exchange(["w_branch_gate", "w_ret_o", "w_lru_o"]), pair_swap(["w_in"]), pair_gather(xattn_grads)],
        "ffn1_bwd_mid": [exchange(["w_in"], 0, 2), pair_gather(mix_out_grads)],
        "ffn1_dw2": [exchange(["w_in"], 2, 4)],
        "ffn1_dw1": [exchange(["w_in"], 3, 4), pair_swap(["ffn1_w2"])],
        "ffn1_dw3": [exchange(["ffn1_w2"], 0, 2), pair_swap(["ffn1_w1"]), pair_gather(["w_in"])],
        "ffn1_dh": [exchange(["ffn1_w2"], 1, 2), exchange(["ffn1_w1"]), pair_swap(["ffn1_w3"])],
        "small_allreduce": [exchange(["ffn1_w3"]), pair_gather(["ffn1_w2"])],
        "rs_last_gather": [pair_gather(["ffn1_w1", "ffn1_w3"])],
    }
    plan.after = {
        "ffn2_up": [functools.partial(_comm_call, "ffn2_up_sibling")],
        "ffn2_dh": [do(pair_sum, ffn2_grads)],
        "d_merged": [do(pair_sum, xattn_grads)],
        "ret_bwd": [do(pair_sum, mix_out_grads)],
        "lru_scan_bwd": [do(chip_sum, ffn2_grads)],
        "lru_gates_bwd": [do(adamw, ffn2_grads)],
        "dw_in": [do(chip_sum, xattn_grads)],
        "d_h2": [do(chip_sum, mix_out_grads), do(pair_sum, ["w_in"]), do(adamw, xattn_grads)],
        "ffn1_bwd_mid": [do(adamw, mix_out_grads)],
        "ffn1_dw1": [do(chip_sum, ["w_in"]), do(pair_sum, ["ffn1_w2"])],
        "ffn1_dw3": [do(pair_sum, ["ffn1_w1"]), do(adamw, ["w_in"])],
        "ffn1_dh": [do(pair_sum, ["ffn1_w3"]), do(chip_sum, ["ffn1_w2"])],
        "small_allreduce": [do(chip_sum, ["ffn1_w1", "ffn1_w3"]), functools.partial(_comm_call, "rs_last_gather"),
                    do(adamw, ["ffn1_w2", "ffn1_w1", "ffn1_w3"])],
    }
    global _plan
    _plan = plan
    try:
        _comm_call("ag_first_chips")
        _comm_call("ag_first_sibling")
        loss_part, grad_x, small = _local_step(x[0], mem[0], loss_target[0], gw, sm, big)
        gate2d = lambda a: a.reshape(LRU_BLOCKS * LRU_BLOCK, LRU_BLOCK)
        loss_row = jnp.pad(loss_part, ((0, 0), (0, D - loss_part.shape[1])))
        small_sum, *gate_sums = _small_allreduce([_pack_small(small, loss_row)]
                                                 + [gate2d(small[n]) for n in GATE_WEIGHTS])
    finally:
        _plan = None
    assert not plan.tasks and not plan.after, (list(plan.tasks), list(plan.after))
    loss = small_sum[SMALL_USED_ROWS, 0]

    small_shapes = {n: w[n].shape for n, _ in SMALL_LAYOUT}
    small_shapes["conv_w"] = (CONV_TAPS, D)
    conv_row = SMALL_USED_ROWS - CONV_TAPS
    conv_grad = lax.dynamic_slice(small_sum[conv_row:conv_row + CONV_TAPS], (0, chip * SQ_BLK), (CONV_TAPS, SQ_BLK))
    small_w = {n: w[n] for n, _ in SMALL_LAYOUT}
    small_m = {n: mom[n] for n, _ in SMALL_LAYOUT}
    small_v = {n: var[n] for n, _ in SMALL_LAYOUT}
    pad_cols = lambda a: jnp.pad(a[0], ((0, 0), (0, D - SQ_BLK)))
    for dct in (small_w, small_m, small_v):
        dct["conv_w"] = pad_cols(dct["conv_w"])
    g_pack = lax.dynamic_update_slice(small_sum, jnp.pad(conv_grad, ((0, 0), (0, D - SQ_BLK))), (conv_row, 0))
    d_pack, m_pack, v_pack = _adamw("adamw_small", _pack_small(small_w), g_pack, _pack_small(small_m),
                                    _pack_small(small_v))
    unpacked = [_unpack_small(p, small_shapes) for p in (g_pack, d_pack, m_pack, v_pack)]
    for n, _ in SMALL_LAYOUT:
        if n == "conv_w":
            outs[n] = tuple(u[n][:, :SQ_BLK][None] for u in unpacked)
        else:
            outs[n] = tuple(u[n] for u in unpacked)
    for n, gsum in zip(GATE_WEIGHTS, gate_sums):
        d, nm, nv = _adamw("adamw_" + n, gate2d(w[n]), gsum, gate2d(mom[n]), gate2d(var[n]))
        outs[n] = tuple(r.reshape(w[n].shape) for r in (gsum, d, nm, nv))

    result = [loss, grad_x[None]]
    for k in range(4):
        result += [outs[n][k] for n in WEIGHT_ORDER]
    return tuple(result)
```

```python
import functools
import math

import jax
import jax.numpy as jnp
from jax import lax
from jax.experimental import pallas as pl
from jax.experimental.pallas import tpu as pltpu

F32 = jnp.float32
BF16 = jnp.bfloat16
GRAD_WIRE_DTYPE = BF16
MESH = pl.DeviceIdType.MESH

D = 1024
EPS = 1e-6
RET_HEADS = 4
RET_DK = 128
RET_DV = 256
CHUNK = 128
ROPE_BASE = 10000.0
LRU_BLOCKS = 8
LRU_BLOCK = 128
CONV_TAPS = 4
LRU_C = 8.0
D_FF = 2816
X_HEADS = 4
X_HD = 256
N_CHIPS = 4
FF_BLK = D_FF // N_CHIPS
IN_BLK = 5120 // N_CHIPS
BG_BLK = 2048 // N_CHIPS
SQ_BLK = D // N_CHIPS

ADAM_LR = 0.001
ADAM_B1 = 0.9
ADAM_B2 = 0.999
ADAM_EPS = 1e-08
ADAM_WD = 0.01
ADAM_STEP = 10

F32_TILE_ROWS = 8
BF16_TILE_ROWS = 16
VMEM_LIMIT_BYTES = 56 * 1024 * 1024
ROW_TILE = 512
WIDE_ROW_TILE = 1024
FFN_ROW_TILE = 256
DW_BLK = D_FF // 2
SCAN_TILE = 256
RET_STEP_CHUNKS = 2
RET_STEP_ROWS = RET_STEP_CHUNKS * CHUNK

_DN = {
    "nn": (((1,), (0,)), ((), ())),
    "nt": (((1,), (1,)), ((), ())),
    "tn": (((0,), (0,)), ((), ())),
}


def _cparams(n_axes, collective_id=None):
    return pltpu.CompilerParams(dimension_semantics=("arbitrary",) * n_axes,
                                vmem_limit_bytes=VMEM_LIMIT_BYTES, collective_id=collective_id)


def _dot(a, b, kind):
    if b.ndim == 3:
        b = b.reshape(b.shape[0] * b.shape[1], b.shape[2])
    return lax.dot_general(a.astype(BF16), b.astype(BF16), _DN[kind], preferred_element_type=F32)


def _sigmoid(x):
    return 1.0 / (1.0 + jnp.exp(-x))


def _log1p_pos(e):
    u = 1.0 + e
    return jnp.where(u == 1.0, e, jnp.log(u) * (e / jnp.where(u == 1.0, 1.0, u - 1.0)))


def _expm1(x):
    u = jnp.exp(x)
    lu = jnp.log(u)
    safe = jnp.where(lu == 0.0, 1.0, lu)
    return jnp.where(u == 1.0, x, (u - 1.0) * (x / safe))


def _softplus(z):
    return jnp.maximum(z, 0.0) + _log1p_pos(jnp.exp(-jnp.abs(z)))


_GELU_C = math.sqrt(2.0 / math.pi)


def _gelu_and_grad(x):
    x2 = x * x
    t = jnp.tanh(_GELU_C * (x + 0.044715 * x * x2))
    g = 0.5 * x * (1.0 + t)
    dg = 0.5 * (1.0 + t) + 0.5 * x * (1.0 - t * t) * (_GELU_C * (1.0 + 3.0 * 0.044715 * x2))
    return g, dg


def _rms_fwd(x, g):
    r = lax.rsqrt(jnp.mean(x * x, axis=-1, keepdims=True) + EPS)
    return (x * r) * g


def _rms_bwd(x, g, dh):
    r = lax.rsqrt(jnp.mean(x * x, axis=-1, keepdims=True) + EPS)
    n = x * r
    dyg = dh * g
    dx = r * (dyg - n * jnp.mean(dyg * n, axis=-1, keepdims=True))
    return dx, jnp.sum(dh * n, axis=0, keepdims=True)


def _accumulate(ref, val, first):
    @pl.when(first)
    def _():
        ref[...] = val

    @pl.when(jnp.logical_not(first))
    def _():
        ref[...] += val


def _sds(shape, dtype):
    return jax.ShapeDtypeStruct(tuple(shape), dtype)


def _spec(shape, fn):
    return pl.BlockSpec(tuple(shape), fn)


class _Task:
    def __init__(self, peers, operands, out_shapes, aliases, nsem, make, finish, make_second=None):
        self.peers = peers
        self.operands, self.out_shapes, self.aliases = operands, out_shapes, aliases
        self.nsem, self.make, self.finish = nsem, make, finish
        self.make_second = make_second


class _Plan:
    def __init__(self):
        self.tasks, self.after = {}, {}


_plan = None


_CHIP_PEER_SETS = [frozenset({"chips"}), frozenset({"first"}), frozenset({"second"}), frozenset({"first", "second"})]
PEER_SET_COLLECTIVE_ID = {frozenset({"sibling"}): 1}
for _i, _chip_peers in enumerate(_CHIP_PEER_SETS):
    PEER_SET_COLLECTIVE_ID[_chip_peers] = 2 + 2 * _i
    PEER_SET_COLLECTIVE_ID[_chip_peers | {"sibling"}] = 3 + 2 * _i


def _peer_set(names):
    names = frozenset(n for name in names for n in name.split("+"))
    return names - {"first", "second"} if "chips" in names else names


def _axis_neighbours(x, y, c):
    flip = lambda v, f: v + f * (1 - 2 * v)
    return (flip(x, 1 - c), flip(y, c)), (flip(x, c), flip(y, 1 - c))


def _entry_handshake(peer_set):
    x, y, c, chips = _mesh_position()
    first, second = _axis_neighbours(x, y, c)
    peers = [(x, y, 1 - c)] if "sibling" in peer_set else []
    if "chips" in peer_set:
        peers += [(*chip, c) for chip in chips]
    if "first" in peer_set:
        peers.append((*first, c))
    if "second" in peer_set:
        peers.append((*second, c))
    barrier = pltpu.get_barrier_semaphore()
    for peer in peers:
        pl.semaphore_signal(barrier, inc=1, device_id=peer, device_id_type=MESH)
    pl.semaphore_wait(barrier, len(peers))


def _pcall(body, *, name, grid, in_specs, out_specs, out_shape, scratch_shapes=(), num_prefetch=0, own_peers=()):
    single = not isinstance(out_shape, (list, tuple))
    out_shape = [out_shape] if single else list(out_shape)
    out_specs = [out_specs] if single else list(out_specs)
    in_specs = list(in_specs)
    scratch_shapes = list(scratch_shapes)
    tasks = _plan.tasks.pop(name, []) if _plan is not None else []
    after = _plan.after.pop(name, []) if _plan is not None else []
    peer_set = _peer_set([t.peers for t in tasks] + list(own_peers))
    nax = len(grid)

    def run(*operands):
        n_in = len(operands) - num_prefetch
        n_out = len(out_shape)
        t_ops = [t.operands() for t in tasks]
        t_outs = [t.out_shapes() for t in tasks]
        c_ops = [a for ops in t_ops for a in ops]
        c_outs = [s for outs in t_outs for s in outs]
        aliases = {}
        i0, o0 = num_prefetch + n_in, n_out
        for t, ops, outs in zip(tasks, t_ops, t_outs):
            for i_loc, o_loc in t.aliases.items():
                aliases[i0 + i_loc] = o0 + o_loc
            i0 += len(ops)
            o0 += len(outs)
        nsem = sum(t.nsem for t in tasks)

        def wrapped(*refs):
            p = num_prefetch
            pre, ins = refs[:p], refs[p:p + n_in]
            cins = refs[p + n_in:p + n_in + len(c_ops)]
            q = p + n_in + len(c_ops)
            outs, couts = refs[q:q + n_out], refs[q + n_out:q + n_out + len(c_outs)]
            q += n_out + len(c_outs)
            scr = refs[q:q + len(scratch_shapes)]

            def rounds(second):
                send_sems, recv_sems = refs[q + len(scratch_shapes):]
                out = []
                ci = co = so = 0
                for t, ops, souts in zip(tasks, t_ops, t_outs):
                    make = t.make_second if second else t.make
                    out.append(([], []) if make is None else
                               make(cins[ci:ci + len(ops)], couts[co:co + len(souts)],
                                    functools.partial(lambda base, k: send_sems.at[base + k], so),
                                    functools.partial(lambda base, k: recv_sems.at[base + k], so)))
                    ci, co, so = ci + len(ops), co + len(souts), so + t.nsem
                return out

            two_rounds = [t.make_second is not None for t in tasks]
            if peer_set:
                ids = [pl.program_id(k) for k in range(nax)]
                first = functools.reduce(jnp.logical_and, [i == 0 for i in ids])
                last = functools.reduce(jnp.logical_and, [i == g - 1 for i, g in zip(ids, grid)])
                step = functools.reduce(lambda acc, ig: acc * ig[1] + ig[0], zip(ids, grid), 0)
                middle = step == math.prod(grid) // 3

                @pl.when(first)
                def _():
                    _entry_handshake(peer_set)
                    for starts, _ in rounds(False):
                        for copy in starts:
                            copy().start()

            body(*pre, *ins, *outs, *scr)

            if any(two_rounds):
                @pl.when(middle)
                def _():
                    for (_, arrivals), two in zip(rounds(False), two_rounds):
                        if two:
                            for arrival in arrivals:
                                arrival().wait_recv()
                    for starts, _ in rounds(True):
                        for copy in starts:
                            copy().start()

            if tasks:
                @pl.when(last)
                def _():
                    first_round, second_round = rounds(False), rounds(True)
                    for (_, arrivals1), (_, arrivals2), two in zip(first_round, second_round, two_rounds):
                        for arrival in (arrivals2 if two else arrivals1):
                            arrival().wait_recv()
                    for starts, _ in first_round + second_round:
                        for copy in starts:
                            copy().wait_send()

        sems = [pltpu.SemaphoreType.DMA((nsem,)), pltpu.SemaphoreType.DMA((nsem,))] if tasks else []
        res = pl.pallas_call(
            wrapped, name=name,
            grid_spec=pltpu.PrefetchScalarGridSpec(
                num_scalar_prefetch=num_prefetch, grid=tuple(grid),
                in_specs=in_specs + [ANY_SPEC] * len(c_ops),
                out_specs=out_specs + [ANY_SPEC] * len(c_outs),
                scratch_shapes=scratch_shapes + sems),
            out_shape=out_shape + c_outs,
            input_output_aliases=aliases,
            compiler_params=_cparams(nax, PEER_SET_COLLECTIVE_ID[peer_set] if peer_set else None),
        )(*operands, *c_ops)
        co = n_out
        for t, souts in zip(tasks, t_outs):
            t.finish(res[co:co + len(souts)])
            co += len(souts)
        for fn in after:
            fn()
        return res[0] if single else list(res[:n_out])

    return run


def _comm_call(name):
    def body(o_ref):
        o_ref[...] = jnp.zeros_like(o_ref)

    _pcall(body, name=name, grid=(1,), in_specs=[], out_specs=_spec((8, 128), lambda i: (0, 0)),
           out_shape=_sds((8, 128), F32))()


def _gemm(name, terms, grid, outs, acc_shape, extras=(), epilogue=None):
    kinds = [t[4] for t in terms]
    nt, ne, no = len(terms), len(extras), len(outs)
    nred = grid[-1]
    nax = len(grid)

    def body(*refs):
        trefs = refs[:2 * nt]
        erefs = refs[2 * nt:2 * nt + ne]
        orefs = refs[2 * nt + ne:2 * nt + ne + no]
        ids = [pl.program_id(k) for k in range(nax)]
        tot = None
        for t in range(nt):
            d = _dot(trefs[2 * t][...], trefs[2 * t + 1][...], kinds[t])
            tot = d if tot is None else tot + d

        def finish(acc):
            if epilogue is None:
                orefs[0][...] = acc.astype(orefs[0].dtype)
            else:
                epilogue(acc, erefs, orefs, ids)

        if nred == 1:
            finish(tot)
        else:
            acc_ref = refs[-1]
            r = ids[-1]

            @pl.when(r == 0)
            def _():
                acc_ref[...] = tot

            @pl.when(r > 0)
            def _():
                acc_ref[...] += tot

            @pl.when(r == nred - 1)
            def _():
                finish(acc_ref[...])

    operands, in_specs = [], []
    for a, a_spec, b, b_spec, _ in terms:
        operands += [a, b]
        in_specs += [a_spec, b_spec]
    for e, e_spec in extras:
        operands.append(e)
        in_specs.append(e_spec)
    scratch = [pltpu.VMEM(tuple(acc_shape), F32)] if nred > 1 else []
    return _pcall(body, name=name, grid=tuple(grid), in_specs=in_specs, out_specs=[o[1] for o in outs],
                  out_shape=[o[0] for o in outs], scratch_shapes=scratch)(*operands)


def _rowwise(name, fn, ins, outs, grid):
    ni = len(ins)
    nax = len(grid)

    def body(*refs):
        ids = [pl.program_id(k) for k in range(nax)]
        fn(refs[:ni], refs[ni:], ids)

    return _pcall(body, name=name, grid=tuple(grid), in_specs=[i[1] for i in ins],
                  out_specs=[o[1] for o in outs], out_shape=[o[0] for o in outs])(*[i[0] for i in ins])


def _ffn_up(name, h, w1buf, w1_idx, w3buf, w3_idx, norm_gain=None):
    T = h.shape[0]
    tm = min(FFN_ROW_TILE, T)
    normed = norm_gain is not None

    def body(h_ref, *refs):
        if normed:
            g_ref, w1_ref, w3_ref, a_ref, b_ref, s_ref, hn_ref = refs
            hv = _rms_fwd(h_ref[...], g_ref[...]).astype(BF16)
            hn_ref[...] = hv
        else:
            w1_ref, w3_ref, a_ref, b_ref, s_ref = refs
            hv = h_ref[...]
        a = _dot(hv, w1_ref[...], "nt")
        b = _dot(hv, w3_ref[...], "nt")
        a_ref[...] = a.astype(BF16)
        b_ref[...] = b.astype(BF16)
        s_ref[...] = ((a * _sigmoid(a)) * b).astype(BF16)

    row = _spec((tm, D), lambda i: (i, 0))
    blk = _spec((tm, D_FF), lambda i: (i, 0))
    return _pcall(
        body, name=name, grid=(T // tm,),
        in_specs=[row] + ([_spec((1, D), lambda i: (0, 0))] if normed else [])
        + [_spec((N_CHIPS, None, FF_BLK, D), lambda i: (0, w1_idx, 0, 0)),
           _spec((N_CHIPS, None, FF_BLK, D), lambda i: (0, w3_idx, 0, 0))],
        out_specs=[blk, blk, blk] + ([row] if normed else []),
        out_shape=[_sds((T, D_FF), BF16)] * 3 + ([_sds((T, D), BF16)] if normed else []),
    )(h, *([norm_gain] if normed else []), w1buf, w3buf)


def _loss_head(x, g, tgt, loss_ref, dx_ref, dg_ref, first):
    err = _rms_fwd(x, g) - tgt
    lp = 0.5 * jnp.sum(jnp.mean(err * err, axis=-1, keepdims=True), axis=0, keepdims=True)
    _accumulate(loss_ref, jnp.broadcast_to(lp, (1, 128)), first)
    dx, dgp = _rms_bwd(x, g, err * (1.0 / D))
    dx_ref[...] = dx
    _accumulate(dg_ref, dgp, first)


def _ffn_down(name, s, wrow2, w2_idx, x_res, g_next=None, loss_target=None):
    T = x_res.shape[0]
    tm = min(ROW_TILE, T)
    row = lambda i, j, r: (i, 0)
    vec = lambda i, j, r: (0, 0)

    def epilogue(acc, erefs, orefs, ids):
        xo = erefs[0][...] + 0.5 * acc
        if loss_target is not None:
            _loss_head(xo, erefs[1][...], erefs[2][...], orefs[0], orefs[1], orefs[2], ids[0] == 0)
            return
        orefs[0][...] = xo
        orefs[1][...] = _rms_fwd(xo, erefs[1][...]).astype(BF16)

    extras = [(x_res, _spec((tm, D), row)), (g_next, _spec((1, D), vec))]
    if loss_target is None:
        outs = [(_sds((T, D), F32), _spec((tm, D), row)), (_sds((T, D), BF16), _spec((tm, D), row))]
    else:
        extras.append((loss_target, _spec((tm, D), row)))
        outs = [(_sds((1, 128), F32), _spec((1, 128), vec)), (_sds((T, D), F32), _spec((tm, D), row)),
                (_sds((1, D), F32), _spec((1, D), vec))]
    return _gemm(
        name,
        [(s, _spec((tm, D_FF), row),
          wrow2, _spec((N_CHIPS, None, FF_BLK, D), lambda i, j, r: (0, w2_idx, 0, 0)), "nn")],
        (T // tm, 1, 1), outs, (tm, D), extras, epilogue)


def _ffn_bwd_mid(name, dx, wrow2, w2_idx, a, b):
    T = dx.shape[0]
    tm = min(FFN_ROW_TILE, T)

    def body(dx_ref, w2_ref, a_ref, b_ref, dab_ref):
        ds = _dot(0.5 * dx_ref[...], w2_ref[...], "nt")
        av = a_ref[...].astype(F32)
        sg = _sigmoid(av)
        dab_ref[0] = (ds * b_ref[...].astype(F32) * (sg * (1.0 + av * (1.0 - sg)))).astype(BF16)
        dab_ref[1] = (ds * (av * sg)).astype(BF16)

    blk = _spec((tm, D_FF), lambda i: (i, 0))
    return _pcall(
        body, name=name, grid=(T // tm,),
        in_specs=[_spec((tm, D), lambda i: (i, 0)),
                  _spec((N_CHIPS, None, FF_BLK, D), lambda i: (0, w2_idx, 0, 0)),
                  blk, blk],
        out_specs=_spec((2, tm, D_FF), lambda i: (0, i, 0)),
        out_shape=_sds((2, T, D_FF), BF16),
    )(dx, wrow2, a, b)


def _rms_bwd_epilogue(acc, erefs, orefs, ids):
    dx, dgp = _rms_bwd(erefs[0][...], erefs[1][...], acc)
    orefs[0][...] = dx + erefs[2][...]
    _accumulate(orefs[1], dgp, ids[0] == 0)


def _rms_bwd_io(x, g, dres, T, tm):
    row = lambda i, j, r: (i, 0)
    vec = lambda i, j, r: (0, 0)
    extras = [(x, _spec((tm, D), row)), (g, _spec((1, D), vec)), (dres, _spec((tm, D), row))]
    outs = [(_sds((T, D), F32), _spec((tm, D), row)), (_sds((1, D), F32), _spec((1, D), vec))]
    return extras, outs


def _ffn_bwd(tag, dx_out, h, a, b, s, w1buf, w1_idx, w3buf, w3_idx, wrow2, w2_idx, x_in, g, big):
    T = dx_out.shape[0]
    dab = _ffn_bwd_mid(tag + "_bwd_mid", dx_out, wrow2, w2_idx, a, b)

    def half_scale(acc, erefs, orefs, ids):
        orefs[0][...] = (0.5 * acc).astype(orefs[0].dtype)

    dw_grid = (D_FF // DW_BLK, 1, 1)
    dw_out = [(_sds((D_FF, D), GRAD_WIRE_DTYPE), _spec((DW_BLK, D), lambda j, n, r: (j, 0)))]
    tokens = _spec((T, D), lambda j, n, r: (0, 0))
    big[tag + "_w2"] = _gemm(
        tag + "_dw2", [(s, _spec((T, DW_BLK), lambda j, n, r: (0, j)), dx_out, tokens, "tn")],
        dw_grid, dw_out, (DW_BLK, D), (), half_scale)[0].reshape(1, N_CHIPS, FF_BLK, D)
    for widx, wname in ((0, "_w1"), (1, "_w3")):
        big[tag + wname] = _gemm(
            tag + "_d" + wname[1:],
            [(dab, _spec((None, T, DW_BLK), functools.partial(lambda w, j, n, r: (w, 0, j), widx)), h, tokens, "tn")],
            dw_grid, dw_out, (DW_BLK, D))[0].reshape(1, N_CHIPS, FF_BLK, D)
    tm = min(FFN_ROW_TILE, T)
    extras, outs = _rms_bwd_io(x_in, g, dx_out, T, tm)
    whole = lambda idx: _spec((N_CHIPS, None, FF_BLK, D), lambda i, j, r: (0, idx, 0, 0))
    dx_in, dg = _gemm(
        tag + "_dh",
        [(dab, _spec((None, tm, D_FF), lambda i, j, r: (0, i, 0)), w1buf, whole(w1_idx), "nn"),
         (dab, _spec((None, tm, D_FF), lambda i, j, r: (1, i, 0)), w3buf, whole(w3_idx), "nn")],
        (T // tm, 1, 1), outs, (tm, D), extras, _rms_bwd_epilogue)
    return dx_in, dg


def _proj_sq(name, a, wsq, idx, kind, out_dtype=F32, extras=(), epilogue=None, outs=None):
    M = a.shape[0]
    tm = min(ROW_TILE, M)
    if outs is None:
        outs = [(_sds((M, D), out_dtype), _spec((tm, D), lambda i, j, r: (i, 0)))]
    return _gemm(
        name,
        [(a, _spec((tm, D), lambda i, j, r: (i, 0)),
          wsq, _spec((N_CHIPS, None, SQ_BLK, D), lambda i, j, r: (0, idx, 0, 0)), kind)],
        (M // tm, 1, 1), outs, (tm, D), extras, epilogue)


def _dw_sq(name, a, b):
    M = a.shape[0]
    tn = D // 2
    whole = _gemm(
        name,
        [(a, _spec((M, D), lambda i, j, r: (0, 0)), b, _spec((M, tn), lambda i, j, r: (0, j)), "tn")],
        (1, D // tn, 1),
        [(_sds((D, D), GRAD_WIRE_DTYPE), _spec((D, tn), lambda i, j, r: (0, j)))],
        (D, tn))[0]
    return whole.reshape(N_CHIPS, SQ_BLK, D)


def _retention_constants(T):
    pos = jnp.arange(T, dtype=F32)
    inv_freq = ROPE_BASE ** (-jnp.arange(0, RET_DK, 2, dtype=F32) / RET_DK)
    ang = pos[:, None] * inv_freq[None, :]
    cosf = jnp.concatenate([jnp.cos(ang), jnp.cos(ang)], axis=1)
    sins = jnp.concatenate([-jnp.sin(ang), jnp.sin(ang)], axis=1)
    lg = jnp.log(1.0 - 2.0 ** (-5.0 - jnp.arange(RET_HEADS, dtype=F32)))
    p = jnp.arange(CHUNK, dtype=F32)
    rel = p[:, None] - p[None, :]
    dmat = jnp.where(rel[None] >= 0, jnp.exp(rel[None] * lg[:, None, None]), 0.0)
    kd = jnp.exp((CHUNK - 1.0 - p)[None, :] * lg[:, None])[:, :, None]
    qd = jnp.exp((p + 1.0)[None, :] * lg[:, None])[:, :, None]
    cd = jnp.exp(CHUNK * lg)[:, None, None]
    return cosf, sins, dmat, kd, qd, cd


def _rot(t, cosv, sinv):
    return t * cosv + pltpu.roll(t, RET_DK // 2, 1) * sinv


def _unrot(t, cosv, sinv):
    return t * cosv - pltpu.roll(t, RET_DK // 2, 1) * sinv


def _ret_const_specs(cm):
    whole = lambda shape: _spec(shape, lambda c: (0,) * len(shape))
    return [
        _spec((RET_STEP_ROWS, RET_DK), lambda c: (cm(c), 0)),
        _spec((RET_STEP_ROWS, RET_DK), lambda c: (cm(c), 0)),
        whole((RET_HEADS, CHUNK, CHUNK)), whole((RET_HEADS, CHUNK, 1)), whole((RET_HEADS, CHUNK, 1)),
        whole((RET_HEADS, 1, 1)),
    ]


def _head(h, width):
    return slice(h * width, (h + 1) * width)


def _ret_fwd(u, consts, ret_gn):
    T = u.shape[0]
    nC = T // CHUNK
    kscale = RET_DK ** -0.5

    def body(q_ref, k_ref, v_ref, g_ref, cos_ref, sin_ref, dm_ref, kd_ref, qd_ref, cd_ref, gn_ref,
             qr_ref, kr_ref, ret_ref, yr_ref, st_ref, state):
        @pl.when(pl.program_id(0) == 0)
        def _():
            state[...] = jnp.zeros_like(state)

        for cc in range(RET_STEP_CHUNKS):
            rows = slice(cc * CHUNK, (cc + 1) * CHUNK)
            cosv, sinv = cos_ref[rows, :], sin_ref[rows, :]
            for h in range(RET_HEADS):
                hk, hv = _head(h, RET_DK), _head(h, RET_DV)
                q = _rot(q_ref[rows, hk], cosv, sinv)
                k = _rot(k_ref[rows, hk], cosv, sinv) * kscale
                v = v_ref[rows, hv]
                qr_ref[rows, hk] = q
                kr_ref[rows, hk] = k
                prev = state[h]
                st_ref[h, cc] = prev
                s = _dot(q, k, "nt") * dm_ref[h]
                ret = _dot(s, v, "nn") + _dot(q, prev, "nn") * qd_ref[h]
                state[h] = cd_ref[h] * prev + _dot(k * kd_ref[h], v, "tn")
                ret_ref[rows, hv] = ret
                mu = jnp.mean(ret, axis=-1, keepdims=True)
                xc = ret - mu
                yn = xc * lax.rsqrt(jnp.mean(xc * xc, axis=-1, keepdims=True) + EPS)
                g = g_ref[rows, hv]
                yr_ref[rows, hv] = ((g * _sigmoid(g)) * (yn * gn_ref[:, hv])).astype(BF16)

    cm = lambda c: c
    qk_w, v_w = RET_HEADS * RET_DK, RET_HEADS * RET_DV
    in_specs = [
        _spec((RET_STEP_ROWS, qk_w), lambda c: (c, 0)), _spec((RET_STEP_ROWS, qk_w), lambda c: (c, 1)),
        _spec((RET_STEP_ROWS, v_w), lambda c: (c, 1)), _spec((RET_STEP_ROWS, v_w), lambda c: (c, 2)),
    ] + _ret_const_specs(cm) + [_spec((1, v_w), lambda c: (0, 0))]
    qk_out = _spec((RET_STEP_ROWS, qk_w), lambda c: (c, 0))
    v_out = _spec((RET_STEP_ROWS, v_w), lambda c: (c, 0))
    return _pcall(
        body, name="ret_fwd", grid=(nC // RET_STEP_CHUNKS,),
        in_specs=in_specs,
        out_specs=[qk_out, qk_out, v_out, v_out,
                   _spec((RET_HEADS, RET_STEP_CHUNKS, RET_DK, RET_DV), lambda c: (0, c, 0, 0))],
        out_shape=[_sds((T, qk_w), F32), _sds((T, qk_w), F32), _sds((T, v_w), F32), _sds((T, v_w), BF16),
                   _sds((RET_HEADS, nC, RET_DK, RET_DV), F32)],
        scratch_shapes=[pltpu.VMEM((RET_HEADS, RET_DK, RET_DV), F32)],
    )(u, u, u, u, *consts, ret_gn)


def _ret_bwd(dyr, ret, u, qr, kr, states, consts, ret_gn):
    T = u.shape[0]
    nC = T // CHUNK
    kscale = RET_DK ** -0.5

    def body(dyr_ref, ret_ref, g_ref, q_ref, k_ref, v_ref, st_ref,
             cos_ref, sin_ref, dm_ref, kd_ref, qd_ref, cd_ref, gn_ref,
             dq_ref, dk_ref, dv_ref, dg_ref, dgn_ref, gstate):
        first = pl.program_id(0) == 0

        @pl.when(first)
        def _():
            gstate[...] = jnp.zeros_like(gstate)

        dgn_total = None
        for cc in reversed(range(RET_STEP_CHUNKS)):
            rows = slice(cc * CHUNK, (cc + 1) * CHUNK)
            cosv, sinv = cos_ref[rows, :], sin_ref[rows, :]
            dgn_parts = []
            for h in range(RET_HEADS):
                hk, hv = _head(h, RET_DK), _head(h, RET_DV)
                ret = ret_ref[rows, hv]
                mu = jnp.mean(ret, axis=-1, keepdims=True)
                xc = ret - mu
                rs = lax.rsqrt(jnp.mean(xc * xc, axis=-1, keepdims=True) + EPS)
                yn = xc * rs
                gn = gn_ref[:, hv]
                g = g_ref[rows, hv]
                sg = _sigmoid(g)
                dyr_v = dyr_ref[rows, hv]
                dretn = dyr_v * (g * sg)
                dg_ref[rows, hv] = (dyr_v * (yn * gn) * (sg * (1.0 + g * (1.0 - sg)))).astype(BF16)
                dgn_parts.append(jnp.sum(dretn * yn, axis=0, keepdims=True))
                dyn = dretn * gn
                d_o = rs * (dyn - jnp.mean(dyn, axis=-1, keepdims=True)
                            - yn * jnp.mean(dyn * yn, axis=-1, keepdims=True))

                q, k, v = q_ref[rows, hk], k_ref[rows, hk], v_ref[rows, hv]
                dmat, kd, qd = dm_ref[h], kd_ref[h], qd_ref[h]
                prev = st_ref[h, cc]
                gnext = gstate[h]
                s = _dot(q, k, "nt") * dmat
                ds = _dot(d_o, v, "nt") * dmat
                doq = d_o * qd
                dq = _dot(ds, k, "nn") + _dot(doq, prev, "nt")
                dk = _dot(ds, q, "tn") + _dot(v, gnext, "nt") * kd
                dv = _dot(s, d_o, "tn") + _dot(k * kd, gnext, "nn")
                gstate[h] = cd_ref[h] * gnext + _dot(q, doq, "tn")
                dq_ref[rows, hk] = _unrot(dq, cosv, sinv).astype(BF16)
                dk_ref[rows, hk] = _unrot(dk * kscale, cosv, sinv).astype(BF16)
                dv_ref[rows, hv] = dv.astype(BF16)
            dgn = jnp.concatenate(dgn_parts, axis=1)
            dgn_total = dgn if dgn_total is None else dgn_total + dgn
        _accumulate(dgn_ref, dgn_total, first)

    n_steps = nC // RET_STEP_CHUNKS
    cm = lambda c: n_steps - 1 - c
    qk_w, v_w = RET_HEADS * RET_DK, RET_HEADS * RET_DV
    vspec = lambda blk: _spec((RET_STEP_ROWS, v_w), lambda c: (cm(c), blk))
    qspec = _spec((RET_STEP_ROWS, qk_w), lambda c: (cm(c), 0))
    in_specs = [vspec(0), vspec(0), vspec(2), qspec, qspec, vspec(1),
                _spec((RET_HEADS, RET_STEP_CHUNKS, RET_DK, RET_DV), lambda c: (0, cm(c), 0, 0)),
                ] + _ret_const_specs(cm) + [_spec((1, v_w), lambda c: (0, 0))]
    return _pcall(
        body, name="ret_bwd", grid=(n_steps,),
        in_specs=in_specs,
        out_specs=[qspec, qspec, vspec(0), vspec(0), _spec((1, v_w), lambda c: (0, 0))],
        out_shape=[_sds((T, qk_w), BF16), _sds((T, qk_w), BF16), _sds((T, v_w), BF16), _sds((T, v_w), BF16),
                   _sds((1, v_w), F32)],
        scratch_shapes=[pltpu.VMEM((RET_HEADS, RET_DK, RET_DV), F32)],
    )(dyr, ret, u, qr, kr, u, states, *consts, ret_gn)


def _shift_down(x, s):
    rows = lax.broadcasted_iota(jnp.int32, x.shape, 0)
    return jnp.where(rows >= s, pltpu.roll(x, s, 0), 0.0)


def _shift_up(x, s):
    n = x.shape[0]
    rows = lax.broadcasted_iota(jnp.int32, x.shape, 0)
    return jnp.where(rows < n - s, pltpu.roll(x, n - s, 0), 0.0)


def _lru_specs(T):
    col = lambda off: _spec((T, LRU_BLOCK), lambda g: (0, off + g))
    vec = _spec((1, LRU_BLOCK), lambda g: (0, g))
    wblk = _spec((None, LRU_BLOCK, LRU_BLOCK), lambda g: (g, 0, 0))
    cw = _spec((CONV_TAPS, LRU_BLOCK), lambda g: (0, g))
    return col, vec, wblk, cw


def _lru_gates_fwd(u, conv_w, conv_b, w_r, b_r, w_i, b_i, lam):
    T = u.shape[0]
    col, vec, wblk, cw = _lru_specs(T)

    def body(x_ref, cw_ref, cb_ref, wr_ref, br_ref, wi_ref, bi_ref, lam_ref,
             xc_ref, r_ref, i_ref, a_ref, bx_ref):
        x = x_ref[...]
        w = cw_ref[...]
        xc = (_shift_down(x, 3) * w[0:1] + _shift_down(x, 2) * w[1:2] + _shift_down(x, 1) * w[2:3]
              + x * w[3:4] + cb_ref[...])
        r = _sigmoid(_dot(xc, wr_ref[...], "nn") + br_ref[...])
        i = _sigmoid(_dot(xc, wi_ref[...], "nn") + bi_ref[...])
        la = (-LRU_C) * r * _softplus(-lam_ref[...])
        xc_ref[...] = xc
        r_ref[...] = r
        i_ref[...] = i
        a_ref[...] = jnp.exp(la)
        bx_ref[...] = jnp.sqrt(-_expm1(2.0 * la)) * (i * xc)

    out = col(0)
    return _pcall(
        body, name="lru_gates_fwd", grid=(LRU_BLOCKS,),
        in_specs=[col(24), cw, vec, wblk, vec, wblk, vec, vec],
        out_specs=[out] * 5,
        out_shape=[_sds((T, D), F32)] * 5,
    )(u, conv_w, conv_b, w_r, b_r, w_i, b_i, lam)


def _lru_scan(name, a3, b3, reverse):
    T = a3.shape[0]
    nt = T // SCAN_TILE
    unroll = 8

    def body(a_ref, b_ref, o_ref, carry):
        @pl.when(pl.program_id(0) == 0)
        def _():
            carry[...] = jnp.zeros_like(carry)

        if not reverse:
            def step(t, h):
                h = a_ref[t] * h + b_ref[t]
                o_ref[t] = h
                return h
        else:
            def step(k, c):
                t = SCAN_TILE - 1 - k
                l = b_ref[t] + c
                o_ref[t] = l
                return a_ref[t] * l
        carry[...] = lax.fori_loop(0, SCAN_TILE, step, carry[...], unroll=unroll)

    idx = (lambda i: (nt - 1 - i, 0, 0)) if reverse else (lambda i: (i, 0, 0))
    blk = _spec((SCAN_TILE, LRU_BLOCKS, LRU_BLOCK), idx)
    return _pcall(
        body, name=name, grid=(nt,),
        in_specs=[blk, blk], out_specs=blk,
        out_shape=_sds((T, LRU_BLOCKS, LRU_BLOCK), F32),
        scratch_shapes=[pltpu.VMEM((LRU_BLOCKS, LRU_BLOCK), F32)],
    )(a3, b3)


def _lru_gates_bwd(lmb, hl, a, r, i, xc, u, conv_w, w_r, w_i, lam):
    T = u.shape[0]
    col, vec, wblk, cw = _lru_specs(T)

    def body(l_ref, h_ref, a_ref, r_ref, i_ref, xc_ref, x_ref, cw_ref, wr_ref, wi_ref, lam_ref,
             dx_ref, dwr_ref, dwi_ref, dvec_ref, dcw_ref):
        l = l_ref[...]
        av, rv, iv, xc = a_ref[...], r_ref[...], i_ref[...], xc_ref[...]
        lam_v = lam_ref[...]
        sp = _softplus(-lam_v)
        la = (-LRU_C) * rv * sp
        mult = jnp.sqrt(-_expm1(2.0 * la))
        da = l * _shift_down(h_ref[...], 1)
        dmult = l * (iv * xc)
        di = l * mult * xc
        dxc = l * mult * iv
        dla = da * av - dmult * (av * av) / mult
        dzr = (dla * ((-LRU_C) * sp)) * rv * (1.0 - rv)
        dzi = di * iv * (1.0 - iv)
        dsp = jnp.sum(dla * ((-LRU_C) * rv), axis=0, keepdims=True)
        dlam = dsp * (-_sigmoid(-lam_v))
        dwr_ref[...] = _dot(xc, dzr, "tn")
        dwi_ref[...] = _dot(xc, dzi, "tn")
        dxc = dxc + _dot(dzr, wr_ref[...], "nt") + _dot(dzi, wi_ref[...], "nt")
        x = x_ref[...]
        w = cw_ref[...]
        dx = (dxc * w[3:4] + _shift_up(dxc, 1) * w[2:3] + _shift_up(dxc, 2) * w[1:2]
              + _shift_up(dxc, 3) * w[0:1])
        dx_ref[...] = dx.astype(BF16)
        dvec_ref[...] = jnp.concatenate(
            [jnp.sum(dzr, axis=0, keepdims=True), jnp.sum(dzi, axis=0, keepdims=True), dlam,
             jnp.sum(dxc, axis=0, keepdims=True)], axis=0)
        dcw_ref[...] = jnp.concatenate(
            [jnp.sum(dxc * _shift_down(x, 3 - tap), axis=0, keepdims=True) if tap < 3
             else jnp.sum(dxc * x, axis=0, keepdims=True) for tap in range(CONV_TAPS)], axis=0)

    c0 = col(0)
    return _pcall(
        body, name="lru_gates_bwd", grid=(LRU_BLOCKS,),
        in_specs=[c0, c0, c0, c0, c0, c0, col(24), cw, wblk, wblk, vec],
        out_specs=[c0, wblk, wblk, cw, cw],
        out_shape=[_sds((T, D), BF16), _sds((LRU_BLOCKS, LRU_BLOCK, LRU_BLOCK), F32),
                   _sds((LRU_BLOCKS, LRU_BLOCK, LRU_BLOCK), F32), _sds((4, D), F32), _sds((CONV_TAPS, D), F32)],
    )(lmb, hl, a, r, i, xc, u, conv_w, w_r, w_i, lam)


def _xattn_probs(q, k):
    sc = _dot(q, k, "nt") * (X_HD ** -0.5)
    e = jnp.exp(sc - jnp.max(sc, axis=-1, keepdims=True))
    return e / jnp.sum(e, axis=-1, keepdims=True)


def _xattn_fwd(xq, xk, xv):
    T = xq.shape[0]
    tq = min(WIDE_ROW_TILE, T)
    M = xk.shape[0]

    def body(q_ref, k_ref, v_ref, o_ref):
        p = _xattn_probs(q_ref[...], k_ref[...])
        o_ref[...] = _dot(p, v_ref[...], "nn").astype(BF16)

    qs = _spec((tq, X_HD), lambda h, i: (i, h))
    kv = _spec((M, X_HD), lambda h, i: (0, h))
    return _pcall(
        body, name="xattn_fwd", grid=(X_HEADS, T // tq),
        in_specs=[qs, kv, kv], out_specs=qs, out_shape=_sds((T, D), BF16),
    )(xq, xk, xv)


def _xattn_bwd(xq, xk, xv, dxo):
    T = xq.shape[0]
    tq = min(WIDE_ROW_TILE, T)
    M = xk.shape[0]

    def body(q_ref, k_ref, v_ref, do_ref, dq_ref, dk_ref, dv_ref):
        first = pl.program_id(1) == 0
        q, k, v, do = q_ref[...], k_ref[...], v_ref[...], do_ref[...]
        p = _xattn_probs(q, k)
        dp = _dot(do, v, "nt")
        ds = p * (dp - jnp.sum(dp * p, axis=-1, keepdims=True)) * (X_HD ** -0.5)
        dq_ref[...] = _dot(ds, k, "nn").astype(BF16)
        _accumulate(dk_ref, _dot(ds, q, "tn"), first)
        _accumulate(dv_ref, _dot(p, do, "tn"), first)

    qs = _spec((tq, X_HD), lambda h, i: (i, h))
    kv = _spec((M, X_HD), lambda h, i: (0, h))
    return _pcall(
        body, name="xattn_bwd", grid=(X_HEADS, T // tq),
        in_specs=[qs, kv, kv, qs], out_specs=[qs, kv, kv],
        out_shape=[_sds((T, D), BF16), _sds((M, D), F32), _sds((M, D), F32)],
    )(xq, xk, xv, dxo)


def _adamw(name, w, g, m, v):
    R, C = w.shape
    tr = R
    for cand in (512, 352, 256):
        if R % cand == 0:
            tr = cand
            break

    def fn(irefs, orefs, ids):
        delta, mn, vn = _adamw_update(*(r[...] for r in irefs))
        orefs[0][...] = delta
        orefs[1][...] = mn
        orefs[2][...] = vn

    blk = _spec((tr, C), lambda i: (i, 0))
    return _rowwise(name, fn, [(w, blk), (g, blk), (m, blk), (v, blk)],
                    [(_sds((R, C), F32), blk)] * 3, (R // tr,))


def _adamw_update(wv, gv, mv, vv):
    c1 = 1.0 - ADAM_B1 ** ADAM_STEP
    c2 = 1.0 - ADAM_B2 ** ADAM_STEP
    mn = ADAM_B1 * mv + (1.0 - ADAM_B1) * gv
    vn = ADAM_B2 * vv + (1.0 - ADAM_B2) * (gv * gv)
    delta = -ADAM_LR * ((mn / c1) / (jnp.sqrt(vn / c2) + ADAM_EPS) + ADAM_WD * wv)
    return delta, mn, vn


def _adamw_halves(name, w, mine, theirs, widx, m, v, core):
    R, C = w.shape
    H = R // 2
    tr = H
    while tr * C * 4 > (1 << 20) and tr % 16 == 0:
        tr //= 2
    nb = H // tr

    def body(core_ref, w_ref, mine_ref, theirs_ref, m_ref, v_ref, g_out, d_out, m_out, v_out):
        gv = jnp.where(pl.program_id(0) == core_ref[0], mine_ref[...], theirs_ref[...])
        delta, mn, vn = _adamw_update(w_ref[...], gv, m_ref[...], v_ref[...])
        g_out[...] = gv
        d_out[...] = delta
        m_out[...] = mn
        v_out[...] = vn

    full = pl.BlockSpec((tr, C), lambda h, i, core_ref: (h * nb + i, 0))
    mine_spec = pl.BlockSpec((None, tr, C), lambda h, i, core_ref: (widx, jnp.where(h == core_ref[0], i, 0), 0))
    theirs_spec = pl.BlockSpec((None, tr, C), lambda h, i, core_ref: (widx, jnp.where(h == core_ref[0], 0, i), 0))
    return _pcall(
        body, name=name, grid=(2, nb), num_prefetch=1,
        in_specs=[full, mine_spec, theirs_spec, full, full], out_specs=[full] * 4,
        out_shape=[_sds((R, C), F32)] * 4,
    )(core, w, mine, theirs, m, v)


def _rmsnorm(name, x, g):
    M = x.shape[0]
    tm = min(ROW_TILE, M)

    def fn(irefs, orefs, ids):
        orefs[0][...] = _rms_fwd(irefs[0][...], irefs[1][...]).astype(BF16)

    row = _spec((tm, D), lambda i: (i, 0))
    return _rowwise(name, fn, [(x, row), (g, _spec((1, D), lambda i: (0, 0)))],
                    [(_sds((M, D), BF16), row)], (M // tm,))[0]


WEIGHT_AT = {
    "ffn1_w1": ("col1", 0), "ffn1_w3": ("col1", 1), "ffn1_w2": ("row2a", 0),
    "w_ret_o": ("sqA", 0), "w_lru_o": ("sqA", 1), "w_out": ("sqA", 2),
    "w_xq": ("sqB", 0), "w_xk": ("sqB", 1), "w_xv": ("sqC", 0), "w_xo": ("sqC", 1),
    "ffn2_w1": ("col2a", 0), "ffn2_w3": ("col2b", 0), "ffn2_w2": ("row2b", 0),
}


def _local_step(x, mem, tgt, gw, sm, big):
    T = x.shape[0]
    tm = ROW_TILE

    def wt(name):
        key, idx = WEIGHT_AT[name]
        return gw[key], idx

    row3 = lambda i, j, r: (i, 0)
    vec3 = lambda i, j, r: (0, 0)
    rowD = _spec((tm, D), row3)
    vecD = _spec((1, D), vec3)

    def residual_norm(acc, erefs, orefs, ids):
        xo = erefs[0][...] + acc
        orefs[0][...] = xo
        orefs[1][...] = _rms_fwd(xo, erefs[1][...]).astype(BF16)

    def res_norm_io(x_res, g):
        return ([(x_res, rowD), (g, vecD)],
                [(_sds((T, D), F32), rowD), (_sds((T, D), BF16), rowD)])

    a1, b1, s1, h1 = _ffn_up("ffn1_up", x, *wt("ffn1_w1"), *wt("ffn1_w3"), norm_gain=sm["ffn1_norm"])
    x1, h2 = _ffn_down("ffn1_down", s1, *wt("ffn1_w2"), x, sm["mix_norm"])

    tw = min(WIDE_ROW_TILE, T)
    wideD = _spec((tw, D), row3)
    u = _gemm(
        "mix_in",
        [(h2, wideD, gw["win"], _spec((None, None, IN_BLK, D), lambda i, j, r: (j, 0, 0, 0)), "nt")],
        (T // tw, N_CHIPS, 1),
        [(_sds((T, 5120), F32), _spec((tw, IN_BLK), lambda i, j, r: (i, j)))], (tw, IN_BLK))[0]

    consts = _retention_constants(T)
    qr, kr, ret, yr, states = _ret_fwd(u, consts, sm["ret_gn"])

    conv_w = gw["conv"][:, 0].transpose(1, 0, 2).reshape(CONV_TAPS, D)
    xc, rg, ig, av, bx = _lru_gates_fwd(u, conv_w, sm["conv_b"], sm["w_rgate"], sm["b_rgate"],
                                        sm["w_igate"], sm["b_igate"], sm["lru_lambda"])
    a3 = av.reshape(T, LRU_BLOCKS, LRU_BLOCK)
    b3 = bx.reshape(T, LRU_BLOCKS, LRU_BLOCK)

    def gate_epilogue(acc, erefs, orefs, ids):
        orefs[0][...] = _sigmoid(acc + erefs[0][...])

    gates = _gemm(
        "mix_gates",
        [(h2, wideD, gw["wbg"], _spec((None, None, BG_BLK, D), lambda i, j, r: (j, 0, 0, 0)), "nt")],
        (T // tw, N_CHIPS, 1),
        [(_sds((T, 2 * D), F32), _spec((tw, BG_BLK), lambda i, j, r: (i, j)))], (tw, BG_BLK),
        [(sm["b_branch_gate"], _spec((1, BG_BLK), lambda i, j, r: (0, j)))], gate_epilogue)[0]

    hl = _lru_scan("lru_scan_fwd", a3, b3, False).reshape(T, D)

    row1 = _spec((tm, D), lambda i: (i, 0))
    glru1 = _spec((tm, D), lambda i: (i, 4))

    def lru_out(irefs, orefs, ids):
        gl, _ = _gelu_and_grad(irefs[1][...])
        orefs[0][...] = (irefs[0][...] * gl).astype(BF16)

    yl = _rowwise("lru_out", lru_out, [(hl, row1), (u, glru1)], [(_sds((T, D), BF16), row1)], (T // tm,))[0]

    y_ret = _proj_sq("y_ret", yr, *wt("w_ret_o"), "nn")[0]

    def merge_epilogue(acc, erefs, orefs, ids):
        orefs[0][...] = acc
        orefs[1][...] = (erefs[0][...] * erefs[2][...] + erefs[1][...] * acc).astype(BF16)

    y_lru, merged = _proj_sq(
        "y_lru", yl, *wt("w_lru_o"), "nn",
        extras=[(gates, _spec((tm, D), lambda i, j, r: (i, 0))), (gates, _spec((tm, D), lambda i, j, r: (i, 1))),
                (y_ret, rowD)],
        epilogue=merge_epilogue,
        outs=[(_sds((T, D), F32), rowD), (_sds((T, D), BF16), rowD)])

    ex, ou = res_norm_io(x1, sm["xattn_norm"])
    x2, hq = _proj_sq("mix_out", merged, *wt("w_out"), "nn", extras=ex, epilogue=residual_norm, outs=ou)

    m = _rmsnorm("mem_norm", mem, sm["mem_norm"])
    xq = _proj_sq("xq", hq, *wt("w_xq"), "nn", BF16)[0]
    xk = _proj_sq("xk", m, *wt("w_xk"), "nn", BF16)[0]
    xv = _proj_sq("xv", m, *wt("w_xv"), "nn", BF16)[0]
    xo = _xattn_fwd(xq, xk, xv)
    ex, ou = res_norm_io(x2, sm["ffn2_norm"])
    x3, h3 = _proj_sq("xattn_out", xo, *wt("w_xo"), "nn", extras=ex, epilogue=residual_norm, outs=ou)

    a2, b2, s2 = _ffn_up("ffn2_up", h3, *wt("ffn2_w1"), *wt("ffn2_w3"))
    loss, dx4, dg_final = _ffn_down("ffn2_down", s2, *wt("ffn2_w2"), x3, sm["final_norm"], loss_target=tgt)

    dx3, dg_ffn2 = _ffn_bwd("ffn2", dx4, h3, a2, b2, s2, *wt("ffn2_w1"), *wt("ffn2_w3"),
                            *wt("ffn2_w2"), x3, sm["ffn2_norm"], big)

    dxo = _proj_sq("d_xo", dx3, *wt("w_xo"), "nt", BF16)[0]
    big["w_xo"] = _dw_sq("dw_xo", xo, dx3)[None]
    dxq, dxk, dxv = _xattn_bwd(xq, xk, xv, dxo)
    big["w_xq"] = _dw_sq("dw_xq", hq, dxq)[None]
    ex, ou = _rms_bwd_io(x2, sm["xattn_norm"], dx3, T, tm)
    dx2, dg_xattn = _proj_sq("d_hq", dxq, *wt("w_xq"), "nt", extras=ex, epilogue=_rms_bwd_epilogue, outs=ou)
    big["w_xk"] = _dw_sq("dw_xk", m, dxk)[None]
    big["w_xv"] = _dw_sq("dw_xv", m, dxv)[None]

    M = mem.shape[0]

    def mem_norm_epilogue(acc, erefs, orefs, ids):
        _, dgp = _rms_bwd(erefs[0][...], erefs[1][...], acc)
        orefs[0][...] = dgp

    wsq_spec = lambda idx: _spec((N_CHIPS, None, SQ_BLK, D), lambda i, j, r: (0, idx, 0, 0))
    memD = _spec((M, D), row3)
    dg_mem = _gemm(
        "d_mem_norm",
        [(dxk, memD, wt("w_xk")[0], wsq_spec(wt("w_xk")[1]), "nt"),
         (dxv, memD, wt("w_xv")[0], wsq_spec(wt("w_xv")[1]), "nt")],
        (1, 1, 1), [(_sds((1, D), F32), vecD)], (M, D),
        [(mem, memD), (sm["mem_norm"], vecD)], mem_norm_epilogue)[0]

    def merged_bwd_epilogue(acc, erefs, orefs, ids):
        gr, gl, yrv, ylv = (e[...] for e in erefs)
        orefs[0][...] = (acc * gr).astype(BF16)
        orefs[1][...] = (acc * gl).astype(BF16)
        dgr = acc * yrv * gr * (1.0 - gr)
        dgl = acc * ylv * gl * (1.0 - gl)
        orefs[2][:, :D] = dgr.astype(BF16)
        orefs[2][:, D:] = dgl.astype(BF16)
        dbb = jnp.concatenate([jnp.sum(dgr, axis=0, keepdims=True), jnp.sum(dgl, axis=0, keepdims=True)], axis=1)
        _accumulate(orefs[3], dbb, ids[0] == 0)

    dy_ret, dy_lru, dgpre, db_bg = _proj_sq(
        "d_merged", dx2, *wt("w_out"), "nt",
        extras=[(gates, _spec((tm, D), lambda i, j, r: (i, 0))), (gates, _spec((tm, D), lambda i, j, r: (i, 1))),
                (y_ret, rowD), (y_lru, rowD)],
        epilogue=merged_bwd_epilogue,
        outs=[(_sds((T, D), BF16), rowD), (_sds((T, D), BF16), rowD),
              (_sds((T, 2 * D), BF16), _spec((tm, 2 * D), row3)),
              (_sds((1, 2 * D), F32), _spec((1, 2 * D), vec3))])
    big["w_branch_gate"] = _gemm(
        "dw_bg",
        [(h2, _spec((T, D), lambda j, n, r: (r, 0)), dgpre, _spec((T, BG_BLK), lambda j, n, r: (r, j)), "tn")],
        (N_CHIPS, 1, 1),
        [(_sds((N_CHIPS, D, BG_BLK), GRAD_WIRE_DTYPE), _spec((None, D, BG_BLK), lambda j, n, r: (j, 0, 0)))],
        (D, BG_BLK))[0][None]
    big["w_out"] = _dw_sq("dw_out", merged, dx2)[None]
    dyr = _proj_sq("d_yr", dy_ret, *wt("w_ret_o"), "nt")[0]
    big["w_ret_o"] = _dw_sq("dw_ret_o", yr, dy_ret)[None]
    dyl = _proj_sq("d_yl", dy_lru, *wt("w_lru_o"), "nt")[0]
    big["w_lru_o"] = _dw_sq("dw_lru_o", yl, dy_lru)[None]

    def lru_out_bwd(irefs, orefs, ids):
        gl, dgl = _gelu_and_grad(irefs[2][...])
        dyl_v = irefs[0][...]
        orefs[0][...] = dyl_v * gl
        orefs[1][...] = (dyl_v * irefs[1][...] * dgl).astype(BF16)

    dhl, dglru = _rowwise("lru_out_bwd", lru_out_bwd, [(dyl, row1), (hl, row1), (u, glru1)],
                          [(_sds((T, D), F32), row1), (_sds((T, D), BF16), row1)], (T // tm,))
    dhl3 = dhl.reshape(T, LRU_BLOCKS, LRU_BLOCK)
    dq, dk, dv, dgr, dg_retgn = _ret_bwd(dyr, ret, u, qr, kr, states, consts, sm["ret_gn"])
    lmb = _lru_scan("lru_scan_bwd", a3, dhl3, True).reshape(T, D)
    dxl, dw_r, dw_i, dvec, dcw = _lru_gates_bwd(lmb, hl, av, rg, ig, xc, u, conv_w,
                                                sm["w_rgate"], sm["w_igate"], sm["lru_lambda"])

    du = jnp.concatenate([dq, dk, dv, dgr, dxl, dglru], axis=1)
    tk = T
    big["w_in"] = _gemm(
        "dw_in",
        [(h2, _spec((tk, D), lambda j, n, r: (r, 0)), du, _spec((tk, IN_BLK), lambda j, n, r: (r, j)), "tn")],
        (N_CHIPS, 1, T // tk),
        [(_sds((N_CHIPS, D, IN_BLK), GRAD_WIRE_DTYPE), _spec((None, D, IN_BLK), lambda j, n, r: (j, 0, 0)))],
        (D, IN_BLK))[0][None]
    tf = min(FFN_ROW_TILE, T)
    ex, ou = _rms_bwd_io(x1, sm["mix_norm"], dx2, T, tf)
    dx1, dg_mix = _gemm(
        "d_h2",
        [(du, _spec((tf, 5120), row3), gw["win"], _spec((N_CHIPS, None, IN_BLK, D), lambda i, j, r: (0, 0, 0, 0)), "nn"),
         (dgpre, _spec((tf, 2 * D), row3), gw["wbg"], _spec((N_CHIPS, None, BG_BLK, D), lambda i, j, r: (0, 0, 0, 0)),
          "nn")],
        (T // tf, 1, 1), ou, (tf, D), ex, _rms_bwd_epilogue)

    grad_x, dg_ffn1 = _ffn_bwd("ffn1", dx1, h1, a1, b1, s1, *wt("ffn1_w1"), *wt("ffn1_w3"),
                               *wt("ffn1_w2"), x, sm["ffn1_norm"], big)

    small = {
        "ffn1_norm": dg_ffn1, "mix_norm": dg_mix, "ret_gn": dg_retgn, "conv_b": dvec[3:4],
        "b_rgate": dvec[0:1], "b_igate": dvec[1:2], "lru_lambda": dvec[2:3], "xattn_norm": dg_xattn,
        "mem_norm": dg_mem, "ffn2_norm": dg_ffn2, "final_norm": dg_final, "b_branch_gate": db_bg,
        "conv_w": dcw, "w_rgate": dw_r, "w_igate": dw_i,
    }
    return loss, grad_x, small


ANY_SPEC = pl.BlockSpec(memory_space=pl.ANY)
VMEM_SPEC = pl.BlockSpec(memory_space=pltpu.VMEM)
N_PEER_CHIPS = N_CHIPS - 1


def _mesh_position():
    x, y, c = lax.axis_index("x"), lax.axis_index("y"), lax.axis_index("c")
    chips = [(1 - x, y), (x, 1 - y), (1 - x, 1 - y)]
    return x, y, c, chips


def _chip_index(x, y):
    return 2 * x + y


def _rows_half(ref, axis, h):
    n = ref.shape[axis] // 2
    idx = [slice(None)] * len(ref.shape)
    idx[axis] = pl.ds(pl.multiple_of(h * n, BF16_TILE_ROWS), n)
    return ref.at[tuple(idx)]


def _remote(src, dst, send_sem, recv_sem, device):
    return pltpu.make_async_remote_copy(src_ref=src, dst_ref=dst, send_sem=send_sem, recv_sem=recv_sem,
                                        device_id=device, device_id_type=MESH)


def _gather_chips_task(shards, split, landed, legs="both"):
    keys = list(shards)
    n = len(keys)

    def operands():
        if legs == "pass_on":
            return [landed[k] for k in keys]
        chip_me = _chip_index(lax.axis_index("x"), lax.axis_index("y"))
        return [lax.dynamic_update_slice(lax.empty((N_CHIPS,) + shards[k].shape, shards[k].dtype), shards[k][None],
                                         (chip_me,) + (0,) * shards[k].ndim) for k in keys]

    def my_rows(ref, c):
        return _rows_half(ref, 1, c)

    def make_direct(ins, outs, send_sem, recv_sem):
        x, y, c, chips = _mesh_position()
        s_me = _chip_index(x, y)
        starts, arrivals = [], []
        for g in range(n):
            for k, chip in enumerate(chips):
                sems = (send_sem(3 * g + k), recv_sem(3 * g + k))
                starts.append(functools.partial(_remote, outs[g].at[s_me], outs[g].at[s_me], *sems, (*chip, c)))
                got = outs[g].at[_chip_index(*chip)]
                arrivals.append(functools.partial(_remote, got, got, *sems, (*chip, c)))
        return starts, arrivals

    def make_swap(ins, outs, send_sem, recv_sem):
        x, y, c, _ = _mesh_position()
        first, _ = _axis_neighbours(x, y, c)
        starts, arrivals = [], []
        for g in range(n):
            sems = (send_sem(3 * g), recv_sem(3 * g))
            mine = my_rows(outs[g].at[_chip_index(x, y)], c)
            starts.append(functools.partial(_remote, mine, mine, *sems, (*first, c)))
            got = my_rows(outs[g].at[_chip_index(*first)], c)
            arrivals.append(functools.partial(_remote, got, got, *sems, (*first, c)))
        return starts, arrivals

    def make_pass_on(ins, outs, send_sem, recv_sem):
        x, y, c, _ = _mesh_position()
        first, second = _axis_neighbours(x, y, c)
        diagonal = (1 - x, 1 - y)
        starts, arrivals = [], []
        for g in range(n):
            half = lambda chip: my_rows(outs[g].at[_chip_index(*chip)], c)
            for k, (sent, arriving) in enumerate([((x, y), second), (first, diagonal)]):
                sems = (send_sem(3 * g + 1 + k), recv_sem(3 * g + 1 + k))
                starts.append(functools.partial(_remote, half(sent), half(sent), *sems, (*second, c)))
                arrivals.append(functools.partial(_remote, half(arriving), half(arriving), *sems, (*second, c)))
        return starts, arrivals

    def finish(res):
        landed.update(zip(keys, res))

    shapes = lambda: [_sds((N_CHIPS,) + shards[k].shape, shards[k].dtype) for k in keys]
    aliases = {g: g for g in range(n)}
    if not split:
        return _Task("chips", operands, shapes, aliases, 3 * n, make_direct, finish)
    if legs == "swap":
        return _Task("first", operands, shapes, aliases, 3 * n, make_swap, finish)
    if legs == "pass_on":
        return _Task("second", operands, shapes, aliases, 3 * n, make_pass_on, finish)
    return _Task("first+second", operands, shapes, aliases, 3 * n, make_swap, finish, make_second=make_pass_on)


def _gather_sibling_task(keys, landed, ready):
    n = len(keys)

    def make(ins, outs, send_sem, recv_sem):
        x, y, c, chips = _mesh_position()
        starts, arrivals = [], []
        for g in range(n):
            for k, chip in enumerate(chips):
                o = outs[g].at[_chip_index(*chip)]
                got, other = _rows_half(o, 1, c), _rows_half(o, 1, 1 - c)
                starts.append(functools.partial(_remote, got, got, send_sem(3 * g + k), recv_sem(3 * g + k),
                                                (x, y, 1 - c)))
                arrivals.append(functools.partial(_remote, other, other, send_sem(3 * g + k), recv_sem(3 * g + k),
                                                  (x, y, 1 - c)))
        return starts, arrivals

    def finish(res):
        ready.update(zip(keys, res))

    return _Task("sibling", lambda: [landed[k] for k in keys],
                 lambda: [_sds(landed[k].shape, landed[k].dtype) for k in keys],
                 {g: g for g in range(n)}, 3 * n, make, finish)


def _pair_swap_task(names, big, got):
    n = len(names)

    def make(ins, outs, send_sem, recv_sem):
        x, y, c, _ = _mesh_position()
        copies = [functools.partial(_remote, _rows_half(ins[a], 2, 1 - c), outs[a], send_sem(a), recv_sem(a),
                                    (x, y, 1 - c)) for a in range(n)]
        return copies, copies

    def shapes():
        return [_sds(big[k].shape[:2] + (big[k].shape[2] // 2, big[k].shape[3]), big[k].dtype) for k in names]

    return _Task("sibling", lambda: [big[k] for k in names], shapes, {}, n, make,
                 lambda res: got.update(zip(names, res)))


def _rs_pair_sum(name, fulls, gots, core):
    n = len(fulls)
    shapes = [(f.shape[2] // 2, f.shape[3]) for f in fulls]

    def body(core_ref, *refs):
        for a_ref, b_ref, o_ref in zip(refs[:n], refs[n:2 * n], refs[2 * n:]):
            o_ref[...] = (a_ref[...].astype(F32) + b_ref[...].astype(F32)).astype(BF16)

    mine = [pl.BlockSpec((None, None) + hc, lambda s, core_ref: (0, s, core_ref[0], 0)) for hc in shapes]
    slot = [pl.BlockSpec((None, None) + hc, lambda s, core_ref: (0, s, 0, 0)) for hc in shapes]
    return _pcall(
        body, name=name, grid=(N_CHIPS,), num_prefetch=1,
        in_specs=mine + slot, out_specs=slot,
        out_shape=[_sds((1, N_CHIPS) + hc, BF16) for hc in shapes],
    )(core, *fulls, *gots)


def _chip_exchange_task(names, pair_sums, by_source, part=0, nparts=1):
    n = len(names)

    def rows(ref):
        h = ref.shape[1] // nparts
        return ref.at[:, pl.ds(part * h, h), :]

    def make(ins, outs, send_sem, recv_sem):
        x, y, c, chips = _mesh_position()
        s_me = _chip_index(x, y)
        starts, arrivals = [], []
        for a in range(n):
            for k, chip in enumerate(chips):
                s_k = _chip_index(*chip)
                starts.append(functools.partial(_remote, rows(ins[a].at[:, s_k]), rows(outs[a].at[:, s_me]),
                                                send_sem(3 * a + k), recv_sem(3 * a + k), (*chip, c)))
                got = rows(outs[a].at[:, s_k])
                arrivals.append(functools.partial(_remote, got, got, send_sem(3 * a + k), recv_sem(3 * a + k),
                                                  (*chip, c)))
        return starts, arrivals

    def operands():
        return [pair_sums[k] for k in names] + ([by_source[k] for k in names] if part else [])

    return _Task("chips", operands, lambda: [_sds(pair_sums[k].shape, pair_sums[k].dtype) for k in names],
                 {n + a: a for a in range(n)} if part else {}, 3 * n, make,
                 lambda res: by_source.update(zip(names, res)))


def _rs_chip_sum(name, owns, parts, chip):
    n = len(owns)
    ns = N_CHIPS
    shapes = [p.shape[2:] for p in parts]

    def body(chip_ref, *refs):
        me = chip_ref[0]
        for i in range(n):
            own_v = refs[i][...].astype(F32)
            slots = refs[n + ns * i:n + ns * (i + 1)]
            tot = None
            for s in range(ns):
                term = jnp.where(me == s, own_v, slots[s][...].astype(F32))
                tot = term if tot is None else tot + term
            refs[n + ns * n + i][...] = tot

    def slot_spec(hc, s):
        return pl.BlockSpec((None, None) + hc,
                            lambda g, chip_ref: (0, jnp.where(chip_ref[0] == s, (s + 1) % ns, s), 0, 0))

    own_specs = [pl.BlockSpec((None, None) + hc, lambda g, chip_ref: (0, chip_ref[0], 0, 0)) for hc in shapes]
    slot_specs = [slot_spec(hc, s) for hc in shapes for s in range(ns)]
    return _pcall(
        body, name=name, grid=(1,), num_prefetch=1,
        in_specs=own_specs + slot_specs,
        out_specs=[pl.BlockSpec((None,) + hc, lambda g, chip_ref: (0, 0, 0)) for hc in shapes],
        out_shape=[_sds((1,) + hc, F32) for hc in shapes],
    )(chip, *owns, *[p for p in parts for _ in range(ns)])


def _pair_gather_task(names, halves, sibling_halves):
    n = len(names)

    def make(ins, outs, send_sem, recv_sem):
        x, y, c, _ = _mesh_position()
        copies = [functools.partial(_remote, ins[a], outs[a], send_sem(a), recv_sem(a), (x, y, 1 - c))
                  for a in range(n)]
        return copies, copies

    return _Task("sibling", lambda: [halves[k] for k in names], lambda: [_sds(halves[k].shape, F32) for k in names],
                 {}, n, make, lambda res: sibling_halves.update(zip(names, res)))


def _small_allreduce(arrs):
    n = len(arrs)
    per = 1 + 2 * N_PEER_CHIPS

    def body(*refs):
        v_refs, o_refs = refs[:n], refs[n:2 * n]
        sib, pair, part = refs[2 * n:3 * n], refs[3 * n:4 * n], refs[4 * n:5 * n]
        send_sems, recv_sems = refs[5 * n:]
        x, y, c, chips = _mesh_position()
        s_me = _chip_index(x, y)

        def quarter(ref, s):
            q = ref.shape[0] // N_CHIPS
            return ref.at[pl.ds(pl.multiple_of(s * q, F32_TILE_ROWS), q)]

        def exchange(first_sem, src, dst_of, arrival_of):
            sems = lambda a, k: (send_sems.at[a * per + first_sem + k], recv_sems.at[a * per + first_sem + k])
            sends = [_remote(src(a, _chip_index(*chip)), dst_of(a, s_me), *sems(a, k), (*chip, c))
                     for a in range(n) for k, chip in enumerate(chips)]
            for cp in sends:
                cp.start()
            for a in range(n):
                for k, chip in enumerate(chips):
                    got = arrival_of(a, _chip_index(*chip))
                    _remote(got, got, *sems(a, k), (*chip, c)).wait_recv()
            for cp in sends:
                cp.wait_send()

        swaps = [_remote(v_refs[a], sib[a], send_sems.at[a * per], recv_sems.at[a * per], (x, y, 1 - c))
                 for a in range(n)]
        for cp in swaps:
            cp.start()
        for cp in swaps:
            cp.wait()
        for a in range(n):
            pair[a][...] = v_refs[a][...] + sib[a][...]
        exchange(1, lambda a, s_k: quarter(pair[a], s_k), lambda a, s: part[a].at[s], lambda a, s_k: part[a].at[s_k])
        for a in range(n):
            part[a][s_me] = quarter(pair[a], s_me)[...]
            q = o_refs[a].shape[0] // N_CHIPS
            o_refs[a][pl.ds(pl.multiple_of(s_me * q, F32_TILE_ROWS), q), :] = (
                ((part[a][0] + part[a][1]) + part[a][2]) + part[a][3])
        exchange(1 + N_PEER_CHIPS, lambda a, s_k: quarter(o_refs[a], s_me), lambda a, s: quarter(o_refs[a], s),
                 lambda a, s_k: quarter(o_refs[a], s_k))

    shapes = [a.shape for a in arrs]
    return _pcall(
        body, name="small_allreduce", grid=(1,), own_peers=("sibling", "chips"),
        in_specs=[VMEM_SPEC] * n, out_specs=[VMEM_SPEC] * n, out_shape=[_sds(s, F32) for s in shapes],
        scratch_shapes=([pltpu.VMEM(s, F32) for s in shapes] * 2
                        + [pltpu.VMEM((N_CHIPS, s[0] // N_CHIPS, s[1]), F32) for s in shapes]
                        + [pltpu.SemaphoreType.DMA((n * per,)), pltpu.SemaphoreType.DMA((n * per,))]),
    )(*arrs)


TRANSPOSED_WEIGHTS = ("ffn1_w1", "ffn1_w3", "ffn2_w1", "ffn2_w3")
SMALL_LAYOUT = [("ffn1_norm", 1), ("mix_norm", 1), ("ret_gn", 1), ("conv_b", 1), ("b_rgate", 1), ("b_igate", 1),
                ("lru_lambda", 1), ("xattn_norm", 1), ("mem_norm", 1), ("ffn2_norm", 1), ("final_norm", 1),
                ("b_branch_gate", 2), ("conv_w", CONV_TAPS)]
SMALL_ROWS = 32
GATE_WEIGHTS = ("w_rgate", "w_igate")
WEIGHT_ORDER = ["ffn1_norm", "ffn1_w1", "ffn1_w3", "ffn1_w2", "mix_norm", "w_in", "ret_gn", "w_ret_o", "conv_w",
                "conv_b", "w_rgate", "b_rgate", "w_igate", "b_igate", "lru_lambda", "w_lru_o", "w_branch_gate",
                "b_branch_gate", "w_out", "xattn_norm", "mem_norm", "w_xq", "w_xk", "w_xv", "w_xo", "ffn2_norm",
                "ffn2_w1", "ffn2_w3", "ffn2_w2", "final_norm"]


SMALL_USED_ROWS = sum(n for _, n in SMALL_LAYOUT)


def _pack_small(parts, extra_row=None):
    rows = [parts[name].reshape(n, D) for name, n in SMALL_LAYOUT]
    if extra_row is not None:
        rows.append(extra_row)
    rows.append(jnp.zeros((SMALL_ROWS - sum(r.shape[0] for r in rows), D), F32))
    return jnp.concatenate(rows, axis=0)


def _unpack_small(packed, shapes):
    out, r = {}, 0
    for name, n in SMALL_LAYOUT:
        out[name] = packed[r:r + n].reshape(shapes[name])
        r += n
    return out


def kernel(x, mem, ffn1_norm, ffn1_w1, ffn1_w3, ffn1_w2, mix_norm, w_in, ret_gn, w_ret_o, conv_w, conv_b, w_rgate, b_rgate, w_igate, b_igate, lru_lambda, w_lru_o, w_branch_gate, b_branch_gate, w_out, xattn_norm, mem_norm, w_xq, w_xk, w_xv, w_xo, ffn2_norm, ffn2_w1, ffn2_w3, ffn2_w2, final_norm, loss_target, m_ffn1_norm, m_ffn1_w1, m_ffn1_w3, m_ffn1_w2, m_mix_norm, m_w_in, m_ret_gn, m_w_ret_o, m_conv_w, m_conv_b, m_w_rgate, m_b_rgate, m_w_igate, m_b_igate, m_lru_lambda, m_w_lru_o, m_w_branch_gate, m_b_branch_gate, m_w_out, m_xattn_norm, m_mem_norm, m_w_xq, m_w_xk, m_w_xv, m_w_xo, m_ffn2_norm, m_ffn2_w1, m_ffn2_w3, m_ffn2_w2, m_final_norm, v_ffn1_norm, v_ffn1_w1, v_ffn1_w3, v_ffn1_w2, v_mix_norm, v_w_in, v_ret_gn, v_w_ret_o, v_conv_w, v_conv_b, v_w_rgate, v_b_rgate, v_w_igate, v_b_igate, v_lru_lambda, v_w_lru_o, v_w_branch_gate, v_b_branch_gate, v_w_out, v_xattn_norm, v_mem_norm, v_w_xq, v_w_xk, v_w_xv, v_w_xo, v_ffn2_norm, v_ffn2_w1, v_ffn2_w3, v_ffn2_w2, v_final_norm):
    given = dict(locals())
    w = {n: given[n] for n in WEIGHT_ORDER}
    mom = {n: given["m_" + n] for n in WEIGHT_ORDER}
    var = {n: given["v_" + n] for n in WEIGHT_ORDER}
    chip = _chip_index(lax.axis_index("x"), lax.axis_index("y"))
    core = lax.axis_index("c").astype(jnp.int32).reshape(1)

    chip_id = chip.astype(jnp.int32).reshape(1)
    sm = {n: w[n] for n in ["ffn1_norm", "mix_norm", "ret_gn", "conv_b", "b_rgate", "b_igate", "lru_lambda",
                            "xattn_norm", "mem_norm", "ffn2_norm", "b_branch_gate"]}
    sm["final_norm"] = w["final_norm"].reshape(1, D)
    sm["w_rgate"] = w["w_rgate"][0]
    sm["w_igate"] = w["w_igate"][0]

    local = lambda a, n: jnp.swapaxes(a[0], 0, 1) if n in TRANSPOSED_WEIGHTS else a[0]
    stack = lambda names: jnp.stack([local(w[n], n) for n in names], axis=0).astype(BF16)
    shard = {"col1": stack(["ffn1_w1", "ffn1_w3"]), "row2a": stack(["ffn1_w2"]),
             "win": jnp.swapaxes(w["w_in"], 1, 2).astype(BF16),
             "wbg": jnp.swapaxes(w["w_branch_gate"], 1, 2).astype(BF16),
             "sqA": stack(["w_ret_o", "w_lru_o", "w_out"]), "sqB": stack(["w_xq", "w_xk"]),
             "sqC": stack(["w_xv", "w_xo"]), "col2a": stack(["ffn2_w1"]), "col2b": stack(["ffn2_w3"]),
             "row2b": stack(["ffn2_w2"]), "conv": w["conv_w"]}
    gw, landed = {}, {}
    over_chips = lambda keys: _gather_chips_task({k: shard[k] for k in keys}, True, landed)
    to_sibling = lambda keys: _gather_sibling_task(keys, landed, gw)

    big, got, pair_sums, by_source, halves, sibling_halves, outs = {}, {}, {}, {}, {}, {}, {}
    pair_swap = lambda names: _pair_swap_task(names, big, got)
    exchange = lambda names, part=0, nparts=1: _chip_exchange_task(names, pair_sums, by_source, part, nparts)
    pair_gather = lambda names: _pair_gather_task(names, halves, sibling_halves)

    def pair_sum(names):
        res = _rs_pair_sum("rs_pair_sum_" + names[0], [big[n] for n in names], [got[n] for n in names], core)
        pair_sums.update(zip(names, res))

    def chip_sum(names):
        res = _rs_chip_sum("rs_chip_sum_" + names[0], [pair_sums[n] for n in names], [by_source[n] for n in names],
                           chip_id)
        halves.update(zip(names, res))

    def adamw(names):
        for n in names:
            res = _adamw_halves("adamw_" + n, local(w[n], n), halves[n], sibling_halves[n], 0, local(mom[n], n),
                                local(var[n], n), core)
            outs[n] = tuple((jnp.swapaxes(r, 0, 1) if n in TRANSPOSED_WEIGHTS else r)[None] for r in res)

    do = lambda fn, names: functools.partial(fn, names)
    ffn2_grads = ["ffn2_w2", "ffn2_w1", "ffn2_w3"]
    xattn_grads = ["w_xo", "w_xq", "w_xk", "w_xv"]
    mix_out_grads = ["w_branch_gate", "w_out", "w_ret_o", "w_lru_o"]
    conv_gather = _gather_chips_task({"conv": shard["conv"]}, False, gw)
    swap = lambda key: _gather_chips_task({key: shard[key]}, True, landed, legs="swap")
    pass_on = lambda key: _gather_chips_task({key: shard[key]}, True, landed, legs="pass_on")
    plan = _Plan()
    plan.tasks = {
        "ag_first_chips": [over_chips(["col1"]), swap("row2a")],
        "ag_first_sibling": [to_sibling(["col1"]), pass_on("row2a"), swap("win")],
        "ffn1_up": [to_sibling(["row2a"]), pass_on("win"), swap("wbg")],
        "ffn1_down": [to_sibling(["win"]), pass_on("wbg"), swap("sqA")],
        "mix_in": [to_sibling(["wbg"]), pass_on("sqA"), swap("col2a"), conv_gather],
        "ret_fwd": [to_sibling(["sqA"]), pass_on("col2a"), swap("sqB")],
        "lru_gates_fwd": [to_sibling(["col2a"]), pass_on("sqB"), swap("sqC")],
        "mix_gates": [to_sibling(["sqB"]), pass_on("sqC"), swap("col2b")],
        "lru_scan_fwd": [to_sibling(["sqC"]), pass_on("col2b")],
        "y_lru": [to_sibling(["col2b"]), swap("row2b")],
        "ffn2_up": [pass_on("row2b")],
        "ffn2_up_sibling": [to_sibling(["row2b"])],
        "ffn2_dh": [pair_swap(ffn2_grads)],
        "xattn_bwd": [exchange(["ffn2_w2"], 0, 2)],
        "d_hq": [exchange(["ffn2_w2"], 1, 2)],
        "d_merged": [exchange(["ffn2_w1"], 0, 2), pair_swap(xattn_grads)],
        "lru_out_bwd": [exchange(["w_xo"])],
        "ret_bwd": [exchange(["ffn2_w1"], 1, 2), exchange(["ffn2_w3"], 0, 2), pair_swap(mix_out_grads)],
        "lru_scan_bwd": [exchange(["ffn2_w3"], 1, 2)],
        "lru_gates_bwd": [exchange(["w_xq", "w_xk"]), pair_gather(ffn2_grads)],
        "dw_in": [exchange(["w_xv", "w_out"])],
        "d_h2": [exchange(["w_branch_gate", "w_ret_o", "w_lru_o"]), pair_swap(["w_in"]), pair_gather(xattn_grads)],
        "ffn1_bwd_mid": [exchange(["w_in"], 0, 2), pair_gather(mix_out_grads)],
        "ffn1_dw2": [exchange(["w_in"], 2, 4)],
        "ffn1_dw1": [exchange(["w_in"], 3, 4), pair_swap(["ffn1_w2"])],
        "ffn1_dw3": [exchange(["ffn1_w2"], 0, 2), pair_swap(["ffn1_w1"]), pair_gather(["w_in"])],
        "ffn1_dh": [exchange(["ffn1_w2"], 1, 2), exchange(["ffn1_w1"]), pair_swap(["ffn1_w3"])],
        "small_allreduce": [exchange(["ffn1_w3"]), pair_gather(["ffn1_w2"])],
        "adamw_ffn1_w2": [pair_gather(["ffn1_w1", "ffn1_w3"])],
    }
    plan.after = {
        "ffn2_up": [functools.partial(_comm_call, "ffn2_up_sibling")],
        "ffn2_dh": [do(pair_sum, ffn2_grads)],
        "d_merged": [do(pair_sum, xattn_grads)],
        "ret_bwd": [do(pair_sum, mix_out_grads)],
        "lru_scan_bwd": [do(chip_sum, ffn2_grads)],
        "lru_gates_bwd": [do(adamw, ffn2_grads)],
        "dw_in": [do(chip_sum, xattn_grads)],
        "d_h2": [do(chip_sum, mix_out_grads), do(pair_sum, ["w_in"]), do(adamw, xattn_grads)],
        "ffn1_bwd_mid": [do(adamw, mix_out_grads)],
        "ffn1_dw1": [do(chip_sum, ["w_in"]), do(pair_sum, ["ffn1_w2"])],
        "ffn1_dw3": [do(pair_sum, ["ffn1_w1"]), do(adamw, ["w_in"])],
        "ffn1_dh": [do(pair_sum, ["ffn1_w3"]), do(chip_sum, ["ffn1_w2"])],
        "small_allreduce": [do(chip_sum, ["ffn1_w1", "ffn1_w3"]), do(adamw, ["ffn1_w2", "ffn1_w1", "ffn1_w3"])],
    }
    global _plan
    _plan = plan
    try:
        _comm_call("ag_first_chips")
        _comm_call("ag_first_sibling")
        loss_part, grad_x, small = _local_step(x[0], mem[0], loss_target[0], gw, sm, big)
        gate2d = lambda a: a.reshape(LRU_BLOCKS * LRU_BLOCK, LRU_BLOCK)
        loss_row = jnp.pad(loss_part, ((0, 0), (0, D - loss_part.shape[1])))
        small_sum, *gate_sums = _small_allreduce([_pack_small(small, loss_row)]
                                                 + [gate2d(small[n]) for n in GATE_WEIGHTS])
    finally:
        _plan = None
    assert not plan.tasks and not plan.after, (list(plan.tasks), list(plan.after))
    loss = small_sum[SMALL_USED_ROWS, 0]

    small_shapes = {n: w[n].shape for n, _ in SMALL_LAYOUT}
    small_shapes["conv_w"] = (CONV_TAPS, D)
    conv_row = SMALL_USED_ROWS - CONV_TAPS
    conv_grad = lax.dynamic_slice(small_sum[conv_row:conv_row + CONV_TAPS], (0, chip * SQ_BLK), (CONV_TAPS, SQ_BLK))
    small_w = {n: w[n] for n, _ in SMALL_LAYOUT}
    small_m = {n: mom[n] for n, _ in SMALL_LAYOUT}
    small_v = {n: var[n] for n, _ in SMALL_LAYOUT}
    pad_cols = lambda a: jnp.pad(a[0], ((0, 0), (0, D - SQ_BLK)))
    for dct in (small_w, small_m, small_v):
        dct["conv_w"] = pad_cols(dct["conv_w"])
    g_pack = lax.dynamic_update_slice(small_sum, jnp.pad(conv_grad, ((0, 0), (0, D - SQ_BLK))), (conv_row, 0))
    d_pack, m_pack, v_pack = _adamw("adamw_small", _pack_small(small_w), g_pack, _pack_small(small_m),
                                    _pack_small(small_v))
    unpacked = [_unpack_small(p, small_shapes) for p in (g_pack, d_pack, m_pack, v_pack)]
    for n, _ in SMALL_LAYOUT:
        if n == "conv_w":
            outs[n] = tuple(u[n][:, :SQ_BLK][None] for u in unpacked)
        else:
            outs[n] = tuple(u[n] for u in unpacked)
    for n, gsum in zip(GATE_WEIGHTS, gate_sums):
        d, nm, nv = _adamw("adamw_" + n, gate2d(w[n]), gsum, gate2d(mom[n]), gate2d(var[n]))
        outs[n] = tuple(r.reshape(w[n].shape) for r in (gsum, d, nm, nv))

    result = [loss, grad_x[None]]
    for k in range(4):
        result += [outs[n][k] for n in WEIGHT_ORDER]
    return tuple(result)
```

```python
import functools
import math

import jax
import jax.numpy as jnp
from jax import lax
from jax.experimental import pallas as pl
from jax.experimental.pallas import tpu as pltpu

F32 = jnp.float32
BF16 = jnp.bfloat16
GRAD_WIRE_DTYPE = BF16
MESH = pl.DeviceIdType.MESH

D = 1024
EPS = 1e-6
RET_HEADS = 4
RET_DK = 128
RET_DV = 256
CHUNK = 128
ROPE_BASE = 10000.0
LRU_BLOCKS = 8
LRU_BLOCK = 128
CONV_TAPS = 4
LRU_C = 8.0
D_FF = 2816
X_HEADS = 4
X_HD = 256
N_CHIPS = 4
FF_BLK = D_FF // N_CHIPS
IN_BLK = 5120 // N_CHIPS
BG_BLK = 2048 // N_CHIPS
SQ_BLK = D // N_CHIPS

ADAM_LR = 0.001
ADAM_B1 = 0.9
ADAM_B2 = 0.999
ADAM_EPS = 1e-08
ADAM_WD = 0.01
ADAM_STEP = 10

F32_TILE_ROWS = 8
BF16_TILE_ROWS = 16
VMEM_LIMIT_BYTES = 56 * 1024 * 1024
ROW_TILE = 512
WIDE_ROW_TILE = 1024
FFN_ROW_TILE = 256
DW_BLK = D_FF // 2
SCAN_TILE = 256
RET_STEP_CHUNKS = 2
RET_STEP_ROWS = RET_STEP_CHUNKS * CHUNK

_DN = {
    "nn": (((1,), (0,)), ((), ())),
    "nt": (((1,), (1,)), ((), ())),
    "tn": (((0,), (0,)), ((), ())),
}


def _cparams(n_axes, collective_id=None):
    return pltpu.CompilerParams(dimension_semantics=("arbitrary",) * n_axes,
                                vmem_limit_bytes=VMEM_LIMIT_BYTES, collective_id=collective_id)


def _dot(a, b, kind):
    if b.ndim == 3:
        b = b.reshape(b.shape[0] * b.shape[1], b.shape[2])
    return lax.dot_general(a.astype(BF16), b.astype(BF16), _DN[kind], preferred_element_type=F32)


def _sigmoid(x):
    return 1.0 / (1.0 + jnp.exp(-x))


def _log1p_pos(e):
    u = 1.0 + e
    return jnp.where(u == 1.0, e, jnp.log(u) * (e / jnp.where(u == 1.0, 1.0, u - 1.0)))


def _expm1(x):
    u = jnp.exp(x)
    lu = jnp.log(u)
    safe = jnp.where(lu == 0.0, 1.0, lu)
    return jnp.where(u == 1.0, x, (u - 1.0) * (x / safe))


def _softplus(z):
    return jnp.maximum(z, 0.0) + _log1p_pos(jnp.exp(-jnp.abs(z)))


_GELU_C = math.sqrt(2.0 / math.pi)


def _gelu_and_grad(x):
    x2 = x * x
    t = jnp.tanh(_GELU_C * (x + 0.044715 * x * x2))
    g = 0.5 * x * (1.0 + t)
    dg = 0.5 * (1.0 + t) + 0.5 * x * (1.0 - t * t) * (_GELU_C * (1.0 + 3.0 * 0.044715 * x2))
    return g, dg


def _rms_fwd(x, g):
    r = lax.rsqrt(jnp.mean(x * x, axis=-1, keepdims=True) + EPS)
    return (x * r) * g


def _rms_bwd(x, g, dh):
    r = lax.rsqrt(jnp.mean(x * x, axis=-1, keepdims=True) + EPS)
    n = x * r
    dyg = dh * g
    dx = r * (dyg - n * jnp.mean(dyg * n, axis=-1, keepdims=True))
    return dx, jnp.sum(dh * n, axis=0, keepdims=True)


def _accumulate(ref, val, first):
    @pl.when(first)
    def _():
        ref[...] = val

    @pl.when(jnp.logical_not(first))
    def _():
        ref[...] += val


def _sds(shape, dtype):
    return jax.ShapeDtypeStruct(tuple(shape), dtype)


def _spec(shape, fn):
    return pl.BlockSpec(tuple(shape), fn)


class _Task:
    def __init__(self, peers, operands, out_shapes, aliases, nsem, make, finish, make_second=None):
        self.peers = peers
        self.operands, self.out_shapes, self.aliases = operands, out_shapes, aliases
        self.nsem, self.make, self.finish = nsem, make, finish
        self.make_second = make_second


class _Plan:
    def __init__(self):
        self.tasks, self.after = {}, {}


_plan = None


_CHIP_PEER_SETS = [frozenset({"chips"}), frozenset({"first"}), frozenset({"second"}), frozenset({"first", "second"})]
PEER_SET_COLLECTIVE_ID = {frozenset({"sibling"}): 1}
for _i, _chip_peers in enumerate(_CHIP_PEER_SETS):
    PEER_SET_COLLECTIVE_ID[_chip_peers] = 2 + 2 * _i
    PEER_SET_COLLECTIVE_ID[_chip_peers | {"sibling"}] = 3 + 2 * _i


def _peer_set(names):
    names = frozenset(n for name in names for n in name.split("+"))
    return names - {"first", "second"} if "chips" in names else names


def _axis_neighbours(x, y, c):
    flip = lambda v, f: v + f * (1 - 2 * v)
    return (flip(x, 1 - c), flip(y, c)), (flip(x, c), flip(y, 1 - c))


def _entry_handshake(peer_set):
    x, y, c, chips = _mesh_position()
    first, second = _axis_neighbours(x, y, c)
    peers = [(x, y, 1 - c)] if "sibling" in peer_set else []
    if "chips" in peer_set:
        peers += [(*chip, c) for chip in chips]
    if "first" in peer_set:
        peers.append((*first, c))
    if "second" in peer_set:
        peers.append((*second, c))
    barrier = pltpu.get_barrier_semaphore()
    for peer in peers:
        pl.semaphore_signal(barrier, inc=1, device_id=peer, device_id_type=MESH)
    pl.semaphore_wait(barrier, len(peers))


def _pcall(body, *, name, grid, in_specs, out_specs, out_shape, scratch_shapes=(), num_prefetch=0, own_peers=()):
    single = not isinstance(out_shape, (list, tuple))
    out_shape = [out_shape] if single else list(out_shape)
    out_specs = [out_specs] if single else list(out_specs)
    in_specs = list(in_specs)
    scratch_shapes = list(scratch_shapes)
    tasks = _plan.tasks.pop(name, []) if _plan is not None else []
    after = _plan.after.pop(name, []) if _plan is not None else []
    peer_set = _peer_set([t.peers for t in tasks] + list(own_peers))
    nax = len(grid)

    def run(*operands):
        n_in = len(operands) - num_prefetch
        n_out = len(out_shape)
        t_ops = [t.operands() for t in tasks]
        t_outs = [t.out_shapes() for t in tasks]
        c_ops = [a for ops in t_ops for a in ops]
        c_outs = [s for outs in t_outs for s in outs]
        aliases = {}
        i0, o0 = num_prefetch + n_in, n_out
        for t, ops, outs in zip(tasks, t_ops, t_outs):
            for i_loc, o_loc in t.aliases.items():
                aliases[i0 + i_loc] = o0 + o_loc
            i0 += len(ops)
            o0 += len(outs)
        nsem = sum(t.nsem for t in tasks)

        def wrapped(*refs):
            p = num_prefetch
            pre, ins = refs[:p], refs[p:p + n_in]
            cins = refs[p + n_in:p + n_in + len(c_ops)]
            q = p + n_in + len(c_ops)
            outs, couts = refs[q:q + n_out], refs[q + n_out:q + n_out + len(c_outs)]
            q += n_out + len(c_outs)
            scr = refs[q:q + len(scratch_shapes)]

            def rounds(second):
                send_sems, recv_sems = refs[q + len(scratch_shapes):]
                out = []
                ci = co = so = 0
                for t, ops, souts in zip(tasks, t_ops, t_outs):
                    make = t.make_second if second else t.make
                    out.append(([], []) if make is None else
                               make(cins[ci:ci + len(ops)], couts[co:co + len(souts)],
                                    functools.partial(lambda base, k: send_sems.at[base + k], so),
                                    functools.partial(lambda base, k: recv_sems.at[base + k], so)))
                    ci, co, so = ci + len(ops), co + len(souts), so + t.nsem
                return out

            two_rounds = [t.make_second is not None for t in tasks]
            if peer_set:
                ids = [pl.program_id(k) for k in range(nax)]
                first = functools.reduce(jnp.logical_and, [i == 0 for i in ids])
                last = functools.reduce(jnp.logical_and, [i == g - 1 for i, g in zip(ids, grid)])
                step = functools.reduce(lambda acc, ig: acc * ig[1] + ig[0], zip(ids, grid), 0)
                middle = step == math.prod(grid) // 3

                @pl.when(first)
                def _():
                    _entry_handshake(peer_set)
                    for starts, _ in rounds(False):
                        for copy in starts:
                            copy().start()

            body(*pre, *ins, *outs, *scr)

            if any(two_rounds):
                @pl.when(middle)
                def _():
                    for (_, arrivals), two in zip(rounds(False), two_rounds):
                        if two:
                            for arrival in arrivals:
                                arrival().wait_recv()
                    for starts, _ in rounds(True):
                        for copy in starts:
                            copy().start()

            if tasks:
                @pl.when(last)
                def _():
                    first_round, second_round = rounds(False), rounds(True)
                    for (_, arrivals1), (_, arrivals2), two in zip(first_round, second_round, two_rounds):
                        for arrival in (arrivals2 if two else arrivals1):
                            arrival().wait_recv()
                    for starts, _ in first_round + second_round:
                        for copy in starts:
                            copy().wait_send()

        sems = [pltpu.SemaphoreType.DMA((nsem,)), pltpu.SemaphoreType.DMA((nsem,))] if tasks else []
        res = pl.pallas_call(
            wrapped, name=name,
            grid_spec=pltpu.PrefetchScalarGridSpec(
                num_scalar_prefetch=num_prefetch, grid=tuple(grid),
                in_specs=in_specs + [ANY_SPEC] * len(c_ops),
                out_specs=out_specs + [ANY_SPEC] * len(c_outs),
                scratch_shapes=scratch_shapes + sems),
            out_shape=out_shape + c_outs,
            input_output_aliases=aliases,
            compiler_params=_cparams(nax, PEER_SET_COLLECTIVE_ID[peer_set] if peer_set else None),
        )(*operands, *c_ops)
        co = n_out
        for t, souts in zip(tasks, t_outs):
            t.finish(res[co:co + len(souts)])
            co += len(souts)
        for fn in after:
            fn()
        return res[0] if single else list(res[:n_out])

    return run


def _comm_call(name):
    def body(o_ref):
        o_ref[...] = jnp.zeros_like(o_ref)

    _pcall(body, name=name, grid=(1,), in_specs=[], out_specs=_spec((8, 128), lambda i: (0, 0)),
           out_shape=_sds((8, 128), F32))()


def _gemm(name, terms, grid, outs, acc_shape, extras=(), epilogue=None):
    kinds = [t[4] for t in terms]
    nt, ne, no = len(terms), len(extras), len(outs)
    nred = grid[-1]
    nax = len(grid)

    def body(*refs):
        trefs = refs[:2 * nt]
        erefs = refs[2 * nt:2 * nt + ne]
        orefs = refs[2 * nt + ne:2 * nt + ne + no]
        ids = [pl.program_id(k) for k in range(nax)]
        tot = None
        for t in range(nt):
            d = _dot(trefs[2 * t][...], trefs[2 * t + 1][...], kinds[t])
            tot = d if tot is None else tot + d

        def finish(acc):
            if epilogue is None:
                orefs[0][...] = acc.astype(orefs[0].dtype)
            else:
                epilogue(acc, erefs, orefs, ids)

        if nred == 1:
            finish(tot)
        else:
            acc_ref = refs[-1]
            r = ids[-1]

            @pl.when(r == 0)
            def _():
                acc_ref[...] = tot

            @pl.when(r > 0)
            def _():
                acc_ref[...] += tot

            @pl.when(r == nred - 1)
            def _():
                finish(acc_ref[...])

    operands, in_specs = [], []
    for a, a_spec, b, b_spec, _ in terms:
        operands += [a, b]
        in_specs += [a_spec, b_spec]
    for e, e_spec in extras:
        operands.append(e)
        in_specs.append(e_spec)
    scratch = [pltpu.VMEM(tuple(acc_shape), F32)] if nred > 1 else []
    return _pcall(body, name=name, grid=tuple(grid), in_specs=in_specs, out_specs=[o[1] for o in outs],
                  out_shape=[o[0] for o in outs], scratch_shapes=scratch)(*operands)


def _rowwise(name, fn, ins, outs, grid):
    ni = len(ins)
    nax = len(grid)

    def body(*refs):
        ids = [pl.program_id(k) for k in range(nax)]
        fn(refs[:ni], refs[ni:], ids)

    return _pcall(body, name=name, grid=tuple(grid), in_specs=[i[1] for i in ins],
                  out_specs=[o[1] for o in outs], out_shape=[o[0] for o in outs])(*[i[0] for i in ins])


def _ffn_up(name, h, w1buf, w1_idx, w3buf, w3_idx, norm_gain=None):
    T = h.shape[0]
    tm = min(FFN_ROW_TILE, T)
    normed = norm_gain is not None

    def body(h_ref, *refs):
        if normed:
            g_ref, w1_ref, w3_ref, a_ref, b_ref, s_ref, hn_ref = refs
            hv = _rms_fwd(h_ref[...], g_ref[...]).astype(BF16)
            hn_ref[...] = hv
        else:
            w1_ref, w3_ref, a_ref, b_ref, s_ref = refs
            hv = h_ref[...]
        a = _dot(hv, w1_ref[...], "nt")
        b = _dot(hv, w3_ref[...], "nt")
        a_ref[...] = a.astype(BF16)
        b_ref[...] = b.astype(BF16)
        s_ref[...] = ((a * _sigmoid(a)) * b).astype(BF16)

    row = _spec((tm, D), lambda i: (i, 0))
    blk = _spec((tm, D_FF), lambda i: (i, 0))
    return _pcall(
        body, name=name, grid=(T // tm,),
        in_specs=[row] + ([_spec((1, D), lambda i: (0, 0))] if normed else [])
        + [_spec((N_CHIPS, None, FF_BLK, D), lambda i: (0, w1_idx, 0, 0)),
           _spec((N_CHIPS, None, FF_BLK, D), lambda i: (0, w3_idx, 0, 0))],
        out_specs=[blk, blk, blk] + ([row] if normed else []),
        out_shape=[_sds((T, D_FF), BF16)] * 3 + ([_sds((T, D), BF16)] if normed else []),
    )(h, *([norm_gain] if normed else []), w1buf, w3buf)


def _loss_head(x, g, tgt, loss_ref, dx_ref, dg_ref, first):
    err = _rms_fwd(x, g) - tgt
    lp = 0.5 * jnp.sum(jnp.mean(err * err, axis=-1, keepdims=True), axis=0, keepdims=True)
    _accumulate(loss_ref, jnp.broadcast_to(lp, (1, 128)), first)
    dx, dgp = _rms_bwd(x, g, err * (1.0 / D))
    dx_ref[...] = dx
    _accumulate(dg_ref, dgp, first)


def _ffn_down(name, s, wrow2, w2_idx, x_res, g_next=None, loss_target=None):
    T = x_res.shape[0]
    tm = min(ROW_TILE, T)
    row = lambda i, j, r: (i, 0)
    vec = lambda i, j, r: (0, 0)

    def epilogue(acc, erefs, orefs, ids):
        xo = erefs[0][...] + 0.5 * acc
        if loss_target is not None:
            _loss_head(xo, erefs[1][...], erefs[2][...], orefs[0], orefs[1], orefs[2], ids[0] == 0)
            return
        orefs[0][...] = xo
        orefs[1][...] = _rms_fwd(xo, erefs[1][...]).astype(BF16)

    extras = [(x_res, _spec((tm, D), row)), (g_next, _spec((1, D), vec))]
    if loss_target is None:
        outs = [(_sds((T, D), F32), _spec((tm, D), row)), (_sds((T, D), BF16), _spec((tm, D), row))]
    else:
        extras.append((loss_target, _spec((tm, D), row)))
        outs = [(_sds((1, 128), F32), _spec((1, 128), vec)), (_sds((T, D), F32), _spec((tm, D), row)),
                (_sds((1, D), F32), _spec((1, D), vec))]
    return _gemm(
        name,
        [(s, _spec((tm, D_FF), row),
          wrow2, _spec((N_CHIPS, None, FF_BLK, D), lambda i, j, r: (0, w2_idx, 0, 0)), "nn")],
        (T // tm, 1, 1), outs, (tm, D), extras, epilogue)


def _ffn_bwd_mid(name, dx, wrow2, w2_idx, a, b):
    T = dx.shape[0]
    tm = min(FFN_ROW_TILE, T)

    def body(dx_ref, w2_ref, a_ref, b_ref, dab_ref):
        ds = _dot(0.5 * dx_ref[...], w2_ref[...], "nt")
        av = a_ref[...].astype(F32)
        sg = _sigmoid(av)
        dab_ref[0] = (ds * b_ref[...].astype(F32) * (sg * (1.0 + av * (1.0 - sg)))).astype(BF16)
        dab_ref[1] = (ds * (av * sg)).astype(BF16)

    blk = _spec((tm, D_FF), lambda i: (i, 0))
    return _pcall(
        body, name=name, grid=(T // tm,),
        in_specs=[_spec((tm, D), lambda i: (i, 0)),
                  _spec((N_CHIPS, None, FF_BLK, D), lambda i: (0, w2_idx, 0, 0)),
                  blk, blk],
        out_specs=_spec((2, tm, D_FF), lambda i: (0, i, 0)),
        out_shape=_sds((2, T, D_FF), BF16),
    )(dx, wrow2, a, b)


def _rms_bwd_epilogue(acc, erefs, orefs, ids):
    dx, dgp = _rms_bwd(erefs[0][...], erefs[1][...], acc)
    orefs[0][...] = dx + erefs[2][...]
    _accumulate(orefs[1], dgp, ids[0] == 0)


def _rms_bwd_io(x, g, dres, T, tm):
    row = lambda i, j, r: (i, 0)
    vec = lambda i, j, r: (0, 0)
    extras = [(x, _spec((tm, D), row)), (g, _spec((1, D), vec)), (dres, _spec((tm, D), row))]
    outs = [(_sds((T, D), F32), _spec((tm, D), row)), (_sds((1, D), F32), _spec((1, D), vec))]
    return extras, outs


def _ffn_bwd(tag, dx_out, h, a, b, s, w1buf, w1_idx, w3buf, w3_idx, wrow2, w2_idx, x_in, g, big):
    T = dx_out.shape[0]
    dab = _ffn_bwd_mid(tag + "_bwd_mid", dx_out, wrow2, w2_idx, a, b)

    def half_scale(acc, erefs, orefs, ids):
        orefs[0][...] = (0.5 * acc).astype(orefs[0].dtype)

    dw_grid = (D_FF // DW_BLK, 1, 1)
    dw_out = [(_sds((D_FF, D), GRAD_WIRE_DTYPE), _spec((DW_BLK, D), lambda j, n, r: (j, 0)))]
    tokens = _spec((T, D), lambda j, n, r: (0, 0))
    big[tag + "_w2"] = _gemm(
        tag + "_dw2", [(s, _spec((T, DW_BLK), lambda j, n, r: (0, j)), dx_out, tokens, "tn")],
        dw_grid, dw_out, (DW_BLK, D), (), half_scale)[0].reshape(1, N_CHIPS, FF_BLK, D)
    for widx, wname in ((0, "_w1"), (1, "_w3")):
        big[tag + wname] = _gemm(
            tag + "_d" + wname[1:],
            [(dab, _spec((None, T, DW_BLK), functools.partial(lambda w, j, n, r: (w, 0, j), widx)), h, tokens, "tn")],
            dw_grid, dw_out, (DW_BLK, D))[0].reshape(1, N_CHIPS, FF_BLK, D)
    tm = min(FFN_ROW_TILE, T)
    extras, outs = _rms_bwd_io(x_in, g, dx_out, T, tm)
    whole = lambda idx: _spec((N_CHIPS, None, FF_BLK, D), lambda i, j, r: (0, idx, 0, 0))
    dx_in, dg = _gemm(
        tag + "_dh",
        [(dab, _spec((None, tm, D_FF), lambda i, j, r: (0, i, 0)), w1buf, whole(w1_idx), "nn"),
         (dab, _spec((None, tm, D_FF), lambda i, j, r: (1, i, 0)), w3buf, whole(w3_idx), "nn")],
        (T // tm, 1, 1), outs, (tm, D), extras, _rms_bwd_epilogue)
    return dx_in, dg


def _proj_sq(name, a, wsq, idx, kind, out_dtype=F32, extras=(), epilogue=None, outs=None):
    M = a.shape[0]
    tm = min(ROW_TILE, M)
    if outs is None:
        outs = [(_sds((M, D), out_dtype), _spec((tm, D), lambda i, j, r: (i, 0)))]
    return _gemm(
        name,
        [(a, _spec((tm, D), lambda i, j, r: (i, 0)),
          wsq, _spec((N_CHIPS, None, SQ_BLK, D), lambda i, j, r: (0, idx, 0, 0)), kind)],
        (M // tm, 1, 1), outs, (tm, D), extras, epilogue)


def _dw_sq(name, a, b):
    M = a.shape[0]
    tn = D // 2
    whole = _gemm(
        name,
        [(a, _spec((M, D), lambda i, j, r: (0, 0)), b, _spec((M, tn), lambda i, j, r: (0, j)), "tn")],
        (1, D // tn, 1),
        [(_sds((D, D), GRAD_WIRE_DTYPE), _spec((D, tn), lambda i, j, r: (0, j)))],
        (D, tn))[0]
    return whole.reshape(N_CHIPS, SQ_BLK, D)


def _retention_constants(T):
    pos = jnp.arange(T, dtype=F32)
    inv_freq = ROPE_BASE ** (-jnp.arange(0, RET_DK, 2, dtype=F32) / RET_DK)
    ang = pos[:, None] * inv_freq[None, :]
    cosf = jnp.concatenate([jnp.cos(ang), jnp.cos(ang)], axis=1)
    sins = jnp.concatenate([-jnp.sin(ang), jnp.sin(ang)], axis=1)
    lg = jnp.log(1.0 - 2.0 ** (-5.0 - jnp.arange(RET_HEADS, dtype=F32)))
    p = jnp.arange(CHUNK, dtype=F32)
    rel = p[:, None] - p[None, :]
    dmat = jnp.where(rel[None] >= 0, jnp.exp(rel[None] * lg[:, None, None]), 0.0)
    kd = jnp.exp((CHUNK - 1.0 - p)[None, :] * lg[:, None])[:, :, None]
    qd = jnp.exp((p + 1.0)[None, :] * lg[:, None])[:, :, None]
    cd = jnp.exp(CHUNK * lg)[:, None, None]
    return cosf, sins, dmat, kd, qd, cd


def _rot(t, cosv, sinv):
    return t * cosv + pltpu.roll(t, RET_DK // 2, 1) * sinv


def _unrot(t, cosv, sinv):
    return t * cosv - pltpu.roll(t, RET_DK // 2, 1) * sinv


def _ret_const_specs(cm):
    whole = lambda shape: _spec(shape, lambda c: (0,) * len(shape))
    return [
        _spec((RET_STEP_ROWS, RET_DK), lambda c: (cm(c), 0)),
        _spec((RET_STEP_ROWS, RET_DK), lambda c: (cm(c), 0)),
        whole((RET_HEADS, CHUNK, CHUNK)), whole((RET_HEADS, CHUNK, 1)), whole((RET_HEADS, CHUNK, 1)),
        whole((RET_HEADS, 1, 1)),
    ]


def _head(h, width):
    return slice(h * width, (h + 1) * width)


def _ret_fwd(u, consts, ret_gn):
    T = u.shape[0]
    nC = T // CHUNK
    kscale = RET_DK ** -0.5

    def body(q_ref, k_ref, v_ref, g_ref, cos_ref, sin_ref, dm_ref, kd_ref, qd_ref, cd_ref, gn_ref,
             qr_ref, kr_ref, ret_ref, yr_ref, st_ref, state):
        @pl.when(pl.program_id(0) == 0)
        def _():
            state[...] = jnp.zeros_like(state)

        for cc in range(RET_STEP_CHUNKS):
            rows = slice(cc * CHUNK, (cc + 1) * CHUNK)
            cosv, sinv = cos_ref[rows, :], sin_ref[rows, :]
            for h in range(RET_HEADS):
                hk, hv = _head(h, RET_DK), _head(h, RET_DV)
                q = _rot(q_ref[rows, hk], cosv, sinv)
                k = _rot(k_ref[rows, hk], cosv, sinv) * kscale
                v = v_ref[rows, hv]
                qr_ref[rows, hk] = q
                kr_ref[rows, hk] = k
                prev = state[h]
                st_ref[h, cc] = prev
                s = _dot(q, k, "nt") * dm_ref[h]
                ret = _dot(s, v, "nn") + _dot(q, prev, "nn") * qd_ref[h]
                state[h] = cd_ref[h] * prev + _dot(k * kd_ref[h], v, "tn")
                ret_ref[rows, hv] = ret
                mu = jnp.mean(ret, axis=-1, keepdims=True)
                xc = ret - mu
                yn = xc * lax.rsqrt(jnp.mean(xc * xc, axis=-1, keepdims=True) + EPS)
                g = g_ref[rows, hv]
                yr_ref[rows, hv] = ((g * _sigmoid(g)) * (yn * gn_ref[:, hv])).astype(BF16)

    cm = lambda c: c
    qk_w, v_w = RET_HEADS * RET_DK, RET_HEADS * RET_DV
    in_specs = [
        _spec((RET_STEP_ROWS, qk_w), lambda c: (c, 0)), _spec((RET_STEP_ROWS, qk_w), lambda c: (c, 1)),
        _spec((RET_STEP_ROWS, v_w), lambda c: (c, 1)), _spec((RET_STEP_ROWS, v_w), lambda c: (c, 2)),
    ] + _ret_const_specs(cm) + [_spec((1, v_w), lambda c: (0, 0))]
    qk_out = _spec((RET_STEP_ROWS, qk_w), lambda c: (c, 0))
    v_out = _spec((RET_STEP_ROWS, v_w), lambda c: (c, 0))
    return _pcall(
        body, name="ret_fwd", grid=(nC // RET_STEP_CHUNKS,),
        in_specs=in_specs,
        out_specs=[qk_out, qk_out, v_out, v_out,
                   _spec((RET_HEADS, RET_STEP_CHUNKS, RET_DK, RET_DV), lambda c: (0, c, 0, 0))],
        out_shape=[_sds((T, qk_w), F32), _sds((T, qk_w), F32), _sds((T, v_w), F32), _sds((T, v_w), BF16),
                   _sds((RET_HEADS, nC, RET_DK, RET_DV), F32)],
        scratch_shapes=[pltpu.VMEM((RET_HEADS, RET_DK, RET_DV), F32)],
    )(u, u, u, u, *consts, ret_gn)


def _ret_bwd(dyr, ret, u, qr, kr, states, consts, ret_gn):
    T = u.shape[0]
    nC = T // CHUNK
    kscale = RET_DK ** -0.5

    def body(dyr_ref, ret_ref, g_ref, q_ref, k_ref, v_ref, st_ref,
             cos_ref, sin_ref, dm_ref, kd_ref, qd_ref, cd_ref, gn_ref,
             dq_ref, dk_ref, dv_ref, dg_ref, dgn_ref, gstate):
        first = pl.program_id(0) == 0

        @pl.when(first)
        def _():
            gstate[...] = jnp.zeros_like(gstate)

        dgn_total = None
        for cc in reversed(range(RET_STEP_CHUNKS)):
            rows = slice(cc * CHUNK, (cc + 1) * CHUNK)
            cosv, sinv = cos_ref[rows, :], sin_ref[rows, :]
            dgn_parts = []
            for h in range(RET_HEADS):
                hk, hv = _head(h, RET_DK), _head(h, RET_DV)
                ret = ret_ref[rows, hv]
                mu = jnp.mean(ret, axis=-1, keepdims=True)
                xc = ret - mu
                rs = lax.rsqrt(jnp.mean(xc * xc, axis=-1, keepdims=True) + EPS)
                yn = xc * rs
                gn = gn_ref[:, hv]
                g = g_ref[rows, hv]
                sg = _sigmoid(g)
                dyr_v = dyr_ref[rows, hv]
                dretn = dyr_v * (g * sg)
                dg_ref[rows, hv] = (dyr_v * (yn * gn) * (sg * (1.0 + g * (1.0 - sg)))).astype(BF16)
                dgn_parts.append(jnp.sum(dretn * yn, axis=0, keepdims=True))
                dyn = dretn * gn
                d_o = rs * (dyn - jnp.mean(dyn, axis=-1, keepdims=True)
                            - yn * jnp.mean(dyn * yn, axis=-1, keepdims=True))

                q, k, v = q_ref[rows, hk], k_ref[rows, hk], v_ref[rows, hv]
                dmat, kd, qd = dm_ref[h], kd_ref[h], qd_ref[h]
                prev = st_ref[h, cc]
                gnext = gstate[h]
                s = _dot(q, k, "nt") * dmat
                ds = _dot(d_o, v, "nt") * dmat
                doq = d_o * qd
                dq = _dot(ds, k, "nn") + _dot(doq, prev, "nt")
                dk = _dot(ds, q, "tn") + _dot(v, gnext, "nt") * kd
                dv = _dot(s, d_o, "tn") + _dot(k * kd, gnext, "nn")
                gstate[h] = cd_ref[h] * gnext + _dot(q, doq, "tn")
                dq_ref[rows, hk] = _unrot(dq, cosv, sinv).astype(BF16)
                dk_ref[rows, hk] = _unrot(dk * kscale, cosv, sinv).astype(BF16)
                dv_ref[rows, hv] = dv.astype(BF16)
            dgn = jnp.concatenate(dgn_parts, axis=1)
            dgn_total = dgn if dgn_total is None else dgn_total + dgn
        _accumulate(dgn_ref, dgn_total, first)

    n_steps = nC // RET_STEP_CHUNKS
    cm = lambda c: n_steps - 1 - c
    qk_w, v_w = RET_HEADS * RET_DK, RET_HEADS * RET_DV
    vspec = lambda blk: _spec((RET_STEP_ROWS, v_w), lambda c: (cm(c), blk))
    qspec = _spec((RET_STEP_ROWS, qk_w), lambda c: (cm(c), 0))
    in_specs = [vspec(0), vspec(0), vspec(2), qspec, qspec, vspec(1),
                _spec((RET_HEADS, RET_STEP_CHUNKS, RET_DK, RET_DV), lambda c: (0, cm(c), 0, 0)),
                ] + _ret_const_specs(cm) + [_spec((1, v_w), lambda c: (0, 0))]
    return _pcall(
        body, name="ret_bwd", grid=(n_steps,),
        in_specs=in_specs,
        out_specs=[qspec, qspec, vspec(0), vspec(0), _spec((1, v_w), lambda c: (0, 0))],
        out_shape=[_sds((T, qk_w), BF16), _sds((T, qk_w), BF16), _sds((T, v_w), BF16), _sds((T, v_w), BF16),
                   _sds((1, v_w), F32)],
        scratch_shapes=[pltpu.VMEM((RET_HEADS, RET_DK, RET_DV), F32)],
    )(dyr, ret, u, qr, kr, u, states, *consts, ret_gn)


def _shift_down(x, s):
    rows = lax.broadcasted_iota(jnp.int32, x.shape, 0)
    return jnp.where(rows >= s, pltpu.roll(x, s, 0), 0.0)


def _shift_up(x, s):
    n = x.shape[0]
    rows = lax.broadcasted_iota(jnp.int32, x.shape, 0)
    return jnp.where(rows < n - s, pltpu.roll(x, n - s, 0), 0.0)


def _lru_specs(T):
    col = lambda off: _spec((T, LRU_BLOCK), lambda g: (0, off + g))
    vec = _spec((1, LRU_BLOCK), lambda g: (0, g))
    wblk = _spec((None, LRU_BLOCK, LRU_BLOCK), lambda g: (g, 0, 0))
    cw = _spec((CONV_TAPS, LRU_BLOCK), lambda g: (0, g))
    return col, vec, wblk, cw


def _lru_gates_fwd(u, conv_w, conv_b, w_r, b_r, w_i, b_i, lam):
    T = u.shape[0]
    col, vec, wblk, cw = _lru_specs(T)

    def body(x_ref, cw_ref, cb_ref, wr_ref, br_ref, wi_ref, bi_ref, lam_ref,
             xc_ref, r_ref, i_ref, a_ref, bx_ref):
        x = x_ref[...]
        w = cw_ref[...]
        xc = (_shift_down(x, 3) * w[0:1] + _shift_down(x, 2) * w[1:2] + _shift_down(x, 1) * w[2:3]
              + x * w[3:4] + cb_ref[...])
        r = _sigmoid(_dot(xc, wr_ref[...], "nn") + br_ref[...])
        i = _sigmoid(_dot(xc, wi_ref[...], "nn") + bi_ref[...])
        la = (-LRU_C) * r * _softplus(-lam_ref[...])
        xc_ref[...] = xc
        r_ref[...] = r
        i_ref[...] = i
        a_ref[...] = jnp.exp(la)
        bx_ref[...] = jnp.sqrt(-_expm1(2.0 * la)) * (i * xc)

    out = col(0)
    return _pcall(
        body, name="lru_gates_fwd", grid=(LRU_BLOCKS,),
        in_specs=[col(24), cw, vec, wblk, vec, wblk, vec, vec],
        out_specs=[out] * 5,
        out_shape=[_sds((T, D), F32)] * 5,
    )(u, conv_w, conv_b, w_r, b_r, w_i, b_i, lam)


def _lru_scan(name, a3, b3, reverse):
    T = a3.shape[0]
    nt = T // SCAN_TILE
    unroll = 8

    def body(a_ref, b_ref, o_ref, carry):
        @pl.when(pl.program_id(0) == 0)
        def _():
            carry[...] = jnp.zeros_like(carry)

        if not reverse:
            def step(t, h):
                h = a_ref[t] * h + b_ref[t]
                o_ref[t] = h
                return h
        else:
            def step(k, c):
                t = SCAN_TILE - 1 - k
                l = b_ref[t] + c
                o_ref[t] = l
                return a_ref[t] * l
        carry[...] = lax.fori_loop(0, SCAN_TILE, step, carry[...], unroll=unroll)

    idx = (lambda i: (nt - 1 - i, 0, 0)) if reverse else (lambda i: (i, 0, 0))
    blk = _spec((SCAN_TILE, LRU_BLOCKS, LRU_BLOCK), idx)
    return _pcall(
        body, name=name, grid=(nt,),
        in_specs=[blk, blk], out_specs=blk,
        out_shape=_sds((T, LRU_BLOCKS, LRU_BLOCK), F32),
        scratch_shapes=[pltpu.VMEM((LRU_BLOCKS, LRU_BLOCK), F32)],
    )(a3, b3)


def _lru_gates_bwd(lmb, hl, a, r, i, xc, u, conv_w, w_r, w_i, lam):
    T = u.shape[0]
    col, vec, wblk, cw = _lru_specs(T)

    def body(l_ref, h_ref, a_ref, r_ref, i_ref, xc_ref, x_ref, cw_ref, wr_ref, wi_ref, lam_ref,
             dx_ref, dwr_ref, dwi_ref, dvec_ref, dcw_ref):
        l = l_ref[...]
        av, rv, iv, xc = a_ref[...], r_ref[...], i_ref[...], xc_ref[...]
        lam_v = lam_ref[...]
        sp = _softplus(-lam_v)
        la = (-LRU_C) * rv * sp
        mult = jnp.sqrt(-_expm1(2.0 * la))
        da = l * _shift_down(h_ref[...], 1)
        dmult = l * (iv * xc)
        di = l * mult * xc
        dxc = l * mult * iv
        dla = da * av - dmult * (av * av) / mult
        dzr = (dla * ((-LRU_C) * sp)) * rv * (1.0 - rv)
        dzi = di * iv * (1.0 - iv)
        dsp = jnp.sum(dla * ((-LRU_C) * rv), axis=0, keepdims=True)
        dlam = dsp * (-_sigmoid(-lam_v))
        dwr_ref[...] = _dot(xc, dzr, "tn")
        dwi_ref[...] = _dot(xc, dzi, "tn")
        dxc = dxc + _dot(dzr, wr_ref[...], "nt") + _dot(dzi, wi_ref[...], "nt")
        x = x_ref[...]
        w = cw_ref[...]
        dx = (dxc * w[3:4] + _shift_up(dxc, 1) * w[2:3] + _shift_up(dxc, 2) * w[1:2]
              + _shift_up(dxc, 3) * w[0:1])
        dx_ref[...] = dx.astype(BF16)
        dvec_ref[...] = jnp.concatenate(
            [jnp.sum(dzr, axis=0, keepdims=True), jnp.sum(dzi, axis=0, keepdims=True), dlam,
             jnp.sum(dxc, axis=0, keepdims=True)], axis=0)
        dcw_ref[...] = jnp.concatenate(
            [jnp.sum(dxc * _shift_down(x, 3 - tap), axis=0, keepdims=True) if tap < 3
             else jnp.sum(dxc * x, axis=0, keepdims=True) for tap in range(CONV_TAPS)], axis=0)

    c0 = col(0)
    return _pcall(
        body, name="lru_gates_bwd", grid=(LRU_BLOCKS,),
        in_specs=[c0, c0, c0, c0, c0, c0, col(24), cw, wblk, wblk, vec],
        out_specs=[c0, wblk, wblk, cw, cw],
        out_shape=[_sds((T, D), BF16), _sds((LRU_BLOCKS, LRU_BLOCK, LRU_BLOCK), F32),
                   _sds((LRU_BLOCKS, LRU_BLOCK, LRU_BLOCK), F32), _sds((4, D), F32), _sds((CONV_TAPS, D), F32)],
    )(lmb, hl, a, r, i, xc, u, conv_w, w_r, w_i, lam)


def _xattn_probs(q, k):
    sc = _dot(q, k, "nt") * (X_HD ** -0.5)
    e = jnp.exp(sc - jnp.max(sc, axis=-1, keepdims=True))
    return e / jnp.sum(e, axis=-1, keepdims=True)


def _xattn_fwd(xq, xk, xv):
    T = xq.shape[0]
    tq = min(WIDE_ROW_TILE, T)
    M = xk.shape[0]

    def body(q_ref, k_ref, v_ref, o_ref):
        p = _xattn_probs(q_ref[...], k_ref[...])
        o_ref[...] = _dot(p, v_ref[...], "nn").astype(BF16)

    qs = _spec((tq, X_HD), lambda h, i: (i, h))
    kv = _spec((M, X_HD), lambda h, i: (0, h))
    return _pcall(
        body, name="xattn_fwd", grid=(X_HEADS, T // tq),
        in_specs=[qs, kv, kv], out_specs=qs, out_shape=_sds((T, D), BF16),
    )(xq, xk, xv)


def _xattn_bwd(xq, xk, xv, dxo):
    T = xq.shape[0]
    tq = min(WIDE_ROW_TILE, T)
    M = xk.shape[0]

    def body(q_ref, k_ref, v_ref, do_ref, dq_ref, dk_ref, dv_ref):
        first = pl.program_id(1) == 0
        q, k, v, do = q_ref[...], k_ref[...], v_ref[...], do_ref[...]
        p = _xattn_probs(q, k)
        dp = _dot(do, v, "nt")
        ds = p * (dp - jnp.sum(dp * p, axis=-1, keepdims=True)) * (X_HD ** -0.5)
        dq_ref[...] = _dot(ds, k, "nn").astype(BF16)
        _accumulate(dk_ref, _dot(ds, q, "tn"), first)
        _accumulate(dv_ref, _dot(p, do, "tn"), first)

    qs = _spec((tq, X_HD), lambda h, i: (i, h))
    kv = _spec((M, X_HD), lambda h, i: (0, h))
    return _pcall(
        body, name="xattn_bwd", grid=(X_HEADS, T // tq),
        in_specs=[qs, kv, kv, qs], out_specs=[qs, kv, kv],
        out_shape=[_sds((T, D), BF16), _sds((M, D), F32), _sds((M, D), F32)],
    )(xq, xk, xv, dxo)


def _adamw(name, w, g, m, v):
    R, C = w.shape
    tr = R
    for cand in (512, 352, 256):
        if R % cand == 0:
            tr = cand
            break

    def fn(irefs, orefs, ids):
        delta, mn, vn = _adamw_update(*(r[...] for r in irefs))
        orefs[0][...] = delta
        orefs[1][...] = mn
        orefs[2][...] = vn

    blk = _spec((tr, C), lambda i: (i, 0))
    return _rowwise(name, fn, [(w, blk), (g, blk), (m, blk), (v, blk)],
                    [(_sds((R, C), F32), blk)] * 3, (R // tr,))


def _adamw_update(wv, gv, mv, vv):
    c1 = 1.0 - ADAM_B1 ** ADAM_STEP
    c2 = 1.0 - ADAM_B2 ** ADAM_STEP
    mn = ADAM_B1 * mv + (1.0 - ADAM_B1) * gv
    vn = ADAM_B2 * vv + (1.0 - ADAM_B2) * (gv * gv)
    delta = -ADAM_LR * ((mn / c1) / (jnp.sqrt(vn / c2) + ADAM_EPS) + ADAM_WD * wv)
    return delta, mn, vn


def _adamw_halves(name, w, mine, theirs, widx, m, v, core):
    R, C = w.shape
    H = R // 2
    tr = H
    while tr * C * 4 > (1 << 20) and tr % 16 == 0:
        tr //= 2
    nb = H // tr

    def body(core_ref, w_ref, mine_ref, theirs_ref, m_ref, v_ref, g_out, d_out, m_out, v_out):
        gv = jnp.where(pl.program_id(0) == core_ref[0], mine_ref[...], theirs_ref[...])
        delta, mn, vn = _adamw_update(w_ref[...], gv, m_ref[...], v_ref[...])
        g_out[...] = gv
        d_out[...] = delta
        m_out[...] = mn
        v_out[...] = vn

    full = pl.BlockSpec((tr, C), lambda h, i, core_ref: (h * nb + i, 0))
    mine_spec = pl.BlockSpec((None, tr, C), lambda h, i, core_ref: (widx, jnp.where(h == core_ref[0], i, 0), 0))
    theirs_spec = pl.BlockSpec((None, tr, C), lambda h, i, core_ref: (widx, jnp.where(h == core_ref[0], 0, i), 0))
    return _pcall(
        body, name=name, grid=(2, nb), num_prefetch=1,
        in_specs=[full, mine_spec, theirs_spec, full, full], out_specs=[full] * 4,
        out_shape=[_sds((R, C), F32)] * 4,
    )(core, w, mine, theirs, m, v)


def _rmsnorm(name, x, g):
    M = x.shape[0]
    tm = min(ROW_TILE, M)

    def fn(irefs, orefs, ids):
        orefs[0][...] = _rms_fwd(irefs[0][...], irefs[1][...]).astype(BF16)

    row = _spec((tm, D), lambda i: (i, 0))
    return _rowwise(name, fn, [(x, row), (g, _spec((1, D), lambda i: (0, 0)))],
                    [(_sds((M, D), BF16), row)], (M // tm,))[0]


WEIGHT_AT = {
    "ffn1_w1": ("col1", 0), "ffn1_w3": ("col1", 1), "ffn1_w2": ("row2a", 0),
    "w_ret_o": ("sqA", 0), "w_lru_o": ("sqA", 1), "w_out": ("sqA", 2),
    "w_xq": ("sqB", 0), "w_xk": ("sqB", 1), "w_xv": ("sqC", 0), "w_xo": ("sqC", 1),
    "ffn2_w1": ("col2a", 0), "ffn2_w3": ("col2b", 0), "ffn2_w2": ("row2b", 0),
}


def _local_step(x, mem, tgt, gw, sm, big):
    T = x.shape[0]
    tm = ROW_TILE

    def wt(name):
        key, idx = WEIGHT_AT[name]
        return gw[key], idx

    row3 = lambda i, j, r: (i, 0)
    vec3 = lambda i, j, r: (0, 0)
    rowD = _spec((tm, D), row3)
    vecD = _spec((1, D), vec3)

    def residual_norm(acc, erefs, orefs, ids):
        xo = erefs[0][...] + acc
        orefs[0][...] = xo
        orefs[1][...] = _rms_fwd(xo, erefs[1][...]).astype(BF16)

    def res_norm_io(x_res, g):
        return ([(x_res, rowD), (g, vecD)],
                [(_sds((T, D), F32), rowD), (_sds((T, D), BF16), rowD)])

    a1, b1, s1, h1 = _ffn_up("ffn1_up", x, *wt("ffn1_w1"), *wt("ffn1_w3"), norm_gain=sm["ffn1_norm"])
    x1, h2 = _ffn_down("ffn1_down", s1, *wt("ffn1_w2"), x, sm["mix_norm"])

    tw = min(WIDE_ROW_TILE, T)
    wideD = _spec((tw, D), row3)
    u = _gemm(
        "mix_in",
        [(h2, wideD, gw["win"], _spec((None, None, IN_BLK, D), lambda i, j, r: (j, 0, 0, 0)), "nt")],
        (T // tw, N_CHIPS, 1),
        [(_sds((T, 5120), F32), _spec((tw, IN_BLK), lambda i, j, r: (i, j)))], (tw, IN_BLK))[0]

    consts = _retention_constants(T)
    qr, kr, ret, yr, states = _ret_fwd(u, consts, sm["ret_gn"])

    conv_w = gw["conv"][:, 0].transpose(1, 0, 2).reshape(CONV_TAPS, D)
    xc, rg, ig, av, bx = _lru_gates_fwd(u, conv_w, sm["conv_b"], sm["w_rgate"], sm["b_rgate"],
                                        sm["w_igate"], sm["b_igate"], sm["lru_lambda"])
    a3 = av.reshape(T, LRU_BLOCKS, LRU_BLOCK)
    b3 = bx.reshape(T, LRU_BLOCKS, LRU_BLOCK)

    def gate_epilogue(acc, erefs, orefs, ids):
        orefs[0][...] = _sigmoid(acc + erefs[0][...])

    gates = _gemm(
        "mix_gates",
        [(h2, wideD, gw["wbg"], _spec((None, None, BG_BLK, D), lambda i, j, r: (j, 0, 0, 0)), "nt")],
        (T // tw, N_CHIPS, 1),
        [(_sds((T, 2 * D), F32), _spec((tw, BG_BLK), lambda i, j, r: (i, j)))], (tw, BG_BLK),
        [(sm["b_branch_gate"], _spec((1, BG_BLK), lambda i, j, r: (0, j)))], gate_epilogue)[0]

    hl = _lru_scan("lru_scan_fwd", a3, b3, False).reshape(T, D)

    row1 = _spec((tm, D), lambda i: (i, 0))
    glru1 = _spec((tm, D), lambda i: (i, 4))

    def lru_out(irefs, orefs, ids):
        gl, _ = _gelu_and_grad(irefs[1][...])
        orefs[0][...] = (irefs[0][...] * gl).astype(BF16)

    yl = _rowwise("lru_out", lru_out, [(hl, row1), (u, glru1)], [(_sds((T, D), BF16), row1)], (T // tm,))[0]

    y_ret = _proj_sq("y_ret", yr, *wt("w_ret_o"), "nn")[0]

    def merge_epilogue(acc, erefs, orefs, ids):
        orefs[0][...] = acc
        orefs[1][...] = (erefs[0][...] * erefs[2][...] + erefs[1][...] * acc).astype(BF16)

    y_lru, merged = _proj_sq(
        "y_lru", yl, *wt("w_lru_o"), "nn",
        extras=[(gates, _spec((tm, D), lambda i, j, r: (i, 0))), (gates, _spec((tm, D), lambda i, j, r: (i, 1))),
                (y_ret, rowD)],
        epilogue=merge_epilogue,
        outs=[(_sds((T, D), F32), rowD), (_sds((T, D), BF16), rowD)])

    ex, ou = res_norm_io(x1, sm["xattn_norm"])
    x2, hq = _proj_sq("mix_out", merged, *wt("w_out"), "nn", extras=ex, epilogue=residual_norm, outs=ou)

    m = _rmsnorm("mem_norm", mem, sm["mem_norm"])
    xq = _proj_sq("xq", hq, *wt("w_xq"), "nn", BF16)[0]
    xk = _proj_sq("xk", m, *wt("w_xk"), "nn", BF16)[0]
    xv = _proj_sq("xv", m, *wt("w_xv"), "nn", BF16)[0]
    xo = _xattn_fwd(xq, xk, xv)
    ex, ou = res_norm_io(x2, sm["ffn2_norm"])
    x3, h3 = _proj_sq("xattn_out", xo, *wt("w_xo"), "nn", extras=ex, epilogue=residual_norm, outs=ou)

    a2, b2, s2 = _ffn_up("ffn2_up", h3, *wt("ffn2_w1"), *wt("ffn2_w3"))
    loss, dx4, dg_final = _ffn_down("ffn2_down", s2, *wt("ffn2_w2"), x3, sm["final_norm"], loss_target=tgt)

    dx3, dg_ffn2 = _ffn_bwd("ffn2", dx4, h3, a2, b2, s2, *wt("ffn2_w1"), *wt("ffn2_w3"),
                            *wt("ffn2_w2"), x3, sm["ffn2_norm"], big)

    dxo = _proj_sq("d_xo", dx3, *wt("w_xo"), "nt", BF16)[0]
    big["w_xo"] = _dw_sq("dw_xo", xo, dx3)[None]
    dxq, dxk, dxv = _xattn_bwd(xq, xk, xv, dxo)
    big["w_xq"] = _dw_sq("dw_xq", hq, dxq)[None]
    ex, ou = _rms_bwd_io(x2, sm["xattn_norm"], dx3, T, tm)
    dx2, dg_xattn = _proj_sq("d_hq", dxq, *wt("w_xq"), "nt", extras=ex, epilogue=_rms_bwd_epilogue, outs=ou)
    big["w_xk"] = _dw_sq("dw_xk", m, dxk)[None]
    big["w_xv"] = _dw_sq("dw_xv", m, dxv)[None]

    M = mem.shape[0]

    def mem_norm_epilogue(acc, erefs, orefs, ids):
        _, dgp = _rms_bwd(erefs[0][...], erefs[1][...], acc)
        orefs[0][...] = dgp

    wsq_spec = lambda idx: _spec((N_CHIPS, None, SQ_BLK, D), lambda i, j, r: (0, idx, 0, 0))
    memD = _spec((M, D), row3)
    dg_mem = _gemm(
        "d_mem_norm",
        [(dxk, memD, wt("w_xk")[0], wsq_spec(wt("w_xk")[1]), "nt"),
         (dxv, memD, wt("w_xv")[0], wsq_spec(wt("w_xv")[1]), "nt")],
        (1, 1, 1), [(_sds((1, D), F32), vecD)], (M, D),
        [(mem, memD), (sm["mem_norm"], vecD)], mem_norm_epilogue)[0]

    def merged_bwd_epilogue(acc, erefs, orefs, ids):
        gr, gl, yrv, ylv = (e[...] for e in erefs)
        orefs[0][...] = (acc * gr).astype(BF16)
        orefs[1][...] = (acc * gl).astype(BF16)
        dgr = acc * yrv * gr * (1.0 - gr)
        dgl = acc * ylv * gl * (1.0 - gl)
        orefs[2][:, :D] = dgr.astype(BF16)
        orefs[2][:, D:] = dgl.astype(BF16)
        dbb = jnp.concatenate([jnp.sum(dgr, axis=0, keepdims=True), jnp.sum(dgl, axis=0, keepdims=True)], axis=1)
        _accumulate(orefs[3], dbb, ids[0] == 0)

    dy_ret, dy_lru, dgpre, db_bg = _proj_sq(
        "d_merged", dx2, *wt("w_out"), "nt",
        extras=[(gates, _spec((tm, D), lambda i, j, r: (i, 0))), (gates, _spec((tm, D), lambda i, j, r: (i, 1))),
                (y_ret, rowD), (y_lru, rowD)],
        epilogue=merged_bwd_epilogue,
        outs=[(_sds((T, D), BF16), rowD), (_sds((T, D), BF16), rowD),
              (_sds((T, 2 * D), BF16), _spec((tm, 2 * D), row3)),
              (_sds((1, 2 * D), F32), _spec((1, 2 * D), vec3))])
    big["w_branch_gate"] = _gemm(
        "dw_bg",
        [(h2, _spec((T, D), lambda j, n, r: (r, 0)), dgpre, _spec((T, BG_BLK), lambda j, n, r: (r, j)), "tn")],
        (N_CHIPS, 1, 1),
        [(_sds((N_CHIPS, D, BG_BLK), GRAD_WIRE_DTYPE), _spec((None, D, BG_BLK), lambda j, n, r: (j, 0, 0)))],
        (D, BG_BLK))[0][None]
    big["w_out"] = _dw_sq("dw_out", merged, dx2)[None]
    dyr = _proj_sq("d_yr", dy_ret, *wt("w_ret_o"), "nt")[0]
    big["w_ret_o"] = _dw_sq("dw_ret_o", yr, dy_ret)[None]
    dyl = _proj_sq("d_yl", dy_lru, *wt("w_lru_o"), "nt")[0]
    big["w_lru_o"] = _dw_sq("dw_lru_o", yl, dy_lru)[None]

    def lru_out_bwd(irefs, orefs, ids):
        gl, dgl = _gelu_and_grad(irefs[2][...])
        dyl_v = irefs[0][...]
        orefs[0][...] = dyl_v * gl
        orefs[1][...] = (dyl_v * irefs[1][...] * dgl).astype(BF16)

    dhl, dglru = _rowwise("lru_out_bwd", lru_out_bwd, [(dyl, row1), (hl, row1), (u, glru1)],
                          [(_sds((T, D), F32), row1), (_sds((T, D), BF16), row1)], (T // tm,))
    dhl3 = dhl.reshape(T, LRU_BLOCKS, LRU_BLOCK)
    dq, dk, dv, dgr, dg_retgn = _ret_bwd(dyr, ret, u, qr, kr, states, consts, sm["ret_gn"])
    lmb = _lru_scan("lru_scan_bwd", a3, dhl3, True).reshape(T, D)
    dxl, dw_r, dw_i, dvec, dcw = _lru_gates_bwd(lmb, hl, av, rg, ig, xc, u, conv_w,
                                                sm["w_rgate"], sm["w_igate"], sm["lru_lambda"])

    du = jnp.concatenate([dq, dk, dv, dgr, dxl, dglru], axis=1)
    tk = T
    big["w_in"] = _gemm(
        "dw_in",
        [(h2, _spec((tk, D), lambda j, n, r: (r, 0)), du, _spec((tk, IN_BLK), lambda j, n, r: (r, j)), "tn")],
        (N_CHIPS, 1, T // tk),
        [(_sds((N_CHIPS, D, IN_BLK), GRAD_WIRE_DTYPE), _spec((None, D, IN_BLK), lambda j, n, r: (j, 0, 0)))],
        (D, IN_BLK))[0][None]
    tf = min(FFN_ROW_TILE, T)
    ex, ou = _rms_bwd_io(x1, sm["mix_norm"], dx2, T, tf)
    dx1, dg_mix = _gemm(
        "d_h2",
        [(du, _spec((tf, 5120), row3), gw["win"], _spec((N_CHIPS, None, IN_BLK, D), lambda i, j, r: (0, 0, 0, 0)), "nn"),
         (dgpre, _spec((tf, 2 * D), row3), gw["wbg"], _spec((N_CHIPS, None, BG_BLK, D), lambda i, j, r: (0, 0, 0, 0)),
          "nn")],
        (T // tf, 1, 1), ou, (tf, D), ex, _rms_bwd_epilogue)

    grad_x, dg_ffn1 = _ffn_bwd("ffn1", dx1, h1, a1, b1, s1, *wt("ffn1_w1"), *wt("ffn1_w3"),
                               *wt("ffn1_w2"), x, sm["ffn1_norm"], big)

    small = {
        "ffn1_norm": dg_ffn1, "mix_norm": dg_mix, "ret_gn": dg_retgn, "conv_b": dvec[3:4],
        "b_rgate": dvec[0:1], "b_igate": dvec[1:2], "lru_lambda": dvec[2:3], "xattn_norm": dg_xattn,
        "mem_norm": dg_mem, "ffn2_norm": dg_ffn2, "final_norm": dg_final, "b_branch_gate": db_bg,
        "conv_w": dcw, "w_rgate": dw_r, "w_igate": dw_i,
    }
    return loss, grad_x, small


ANY_SPEC = pl.BlockSpec(memory_space=pl.ANY)
VMEM_SPEC = pl.BlockSpec(memory_space=pltpu.VMEM)
N_PEER_CHIPS = N_CHIPS - 1


def _mesh_position():
    x, y, c = lax.axis_index("x"), lax.axis_index("y"), lax.axis_index("c")
    chips = [(1 - x, y), (x, 1 - y), (1 - x, 1 - y)]
    return x, y, c, chips


def _chip_index(x, y):
    return 2 * x + y


def _rows_half(ref, axis, h):
    n = ref.shape[axis] // 2
    idx = [slice(None)] * len(ref.shape)
    idx[axis] = pl.ds(pl.multiple_of(h * n, BF16_TILE_ROWS), n)
    return ref.at[tuple(idx)]


def _remote(src, dst, send_sem, recv_sem, device):
    return pltpu.make_async_remote_copy(src_ref=src, dst_ref=dst, send_sem=send_sem, recv_sem=recv_sem,
                                        device_id=device, device_id_type=MESH)


def _gather_chips_task(shards, split, landed, legs="both"):
    keys = list(shards)
    n = len(keys)

    def operands():
        if legs == "pass_on":
            return [landed[k] for k in keys]
        chip_me = _chip_index(lax.axis_index("x"), lax.axis_index("y"))
        return [lax.dynamic_update_slice(lax.empty((N_CHIPS,) + shards[k].shape, shards[k].dtype), shards[k][None],
                                         (chip_me,) + (0,) * shards[k].ndim) for k in keys]

    def my_rows(ref, c):
        return _rows_half(ref, 1, c)

    def make_direct(ins, outs, send_sem, recv_sem):
        x, y, c, chips = _mesh_position()
        s_me = _chip_index(x, y)
        starts, arrivals = [], []
        for g in range(n):
            for k, chip in enumerate(chips):
                sems = (send_sem(3 * g + k), recv_sem(3 * g + k))
                starts.append(functools.partial(_remote, outs[g].at[s_me], outs[g].at[s_me], *sems, (*chip, c)))
                got = outs[g].at[_chip_index(*chip)]
                arrivals.append(functools.partial(_remote, got, got, *sems, (*chip, c)))
        return starts, arrivals

    def make_swap(ins, outs, send_sem, recv_sem):
        x, y, c, _ = _mesh_position()
        first, _ = _axis_neighbours(x, y, c)
        starts, arrivals = [], []
        for g in range(n):
            sems = (send_sem(3 * g), recv_sem(3 * g))
            mine = my_rows(outs[g].at[_chip_index(x, y)], c)
            starts.append(functools.partial(_remote, mine, mine, *sems, (*first, c)))
            got = my_rows(outs[g].at[_chip_index(*first)], c)
            arrivals.append(functools.partial(_remote, got, got, *sems, (*first, c)))
        return starts, arrivals

    def make_pass_on(ins, outs, send_sem, recv_sem):
        x, y, c, _ = _mesh_position()
        first, second = _axis_neighbours(x, y, c)
        diagonal = (1 - x, 1 - y)
        starts, arrivals = [], []
        for g in range(n):
            half = lambda chip: my_rows(outs[g].at[_chip_index(*chip)], c)
            for k, (sent, arriving) in enumerate([((x, y), second), (first, diagonal)]):
                sems = (send_sem(3 * g + 1 + k), recv_sem(3 * g + 1 + k))
                starts.append(functools.partial(_remote, half(sent), half(sent), *sems, (*second, c)))
                arrivals.append(functools.partial(_remote, half(arriving), half(arriving), *sems, (*second, c)))
        return starts, arrivals

    def finish(res):
        landed.update(zip(keys, res))

    shapes = lambda: [_sds((N_CHIPS,) + shards[k].shape, shards[k].dtype) for k in keys]
    aliases = {g: g for g in range(n)}
    if not split:
        return _Task("chips", operands, shapes, aliases, 3 * n, make_direct, finish)
    if legs == "swap":
        return _Task("first", operands, shapes, aliases, 3 * n, make_swap, finish)
    if legs == "pass_on":
        return _Task("second", operands, shapes, aliases, 3 * n, make_pass_on, finish)
    return _Task("first+second", operands, shapes, aliases, 3 * n, make_swap, finish, make_second=make_pass_on)


def _gather_sibling_task(keys, landed, ready):
    n = len(keys)

    def make(ins, outs, send_sem, recv_sem):
        x, y, c, chips = _mesh_position()
        starts, arrivals = [], []
        for g in range(n):
            for k, chip in enumerate(chips):
                o = outs[g].at[_chip_index(*chip)]
                got, other = _rows_half(o, 1, c), _rows_half(o, 1, 1 - c)
                starts.append(functools.partial(_remote, got, got, send_sem(3 * g + k), recv_sem(3 * g + k),
                                                (x, y, 1 - c)))
                arrivals.append(functools.partial(_remote, other, other, send_sem(3 * g + k), recv_sem(3 * g + k),
                                                  (x, y, 1 - c)))
        return starts, arrivals

    def finish(res):
        ready.update(zip(keys, res))

    return _Task("sibling", lambda: [landed[k] for k in keys],
                 lambda: [_sds(landed[k].shape, landed[k].dtype) for k in keys],
                 {g: g for g in range(n)}, 3 * n, make, finish)


def _pair_swap_task(names, big, got):
    n = len(names)

    def make(ins, outs, send_sem, recv_sem):
        x, y, c, _ = _mesh_position()
        copies = [functools.partial(_remote, _rows_half(ins[a], 2, 1 - c), outs[a], send_sem(a), recv_sem(a),
                                    (x, y, 1 - c)) for a in range(n)]
        return copies, copies

    def shapes():
        return [_sds(big[k].shape[:2] + (big[k].shape[2] // 2, big[k].shape[3]), big[k].dtype) for k in names]

    return _Task("sibling", lambda: [big[k] for k in names], shapes, {}, n, make,
                 lambda res: got.update(zip(names, res)))


def _rs_pair_sum(name, fulls, gots, core):
    n = len(fulls)
    shapes = [(f.shape[2] // 2, f.shape[3]) for f in fulls]

    def body(core_ref, *refs):
        for a_ref, b_ref, o_ref in zip(refs[:n], refs[n:2 * n], refs[2 * n:]):
            o_ref[...] = (a_ref[...].astype(F32) + b_ref[...].astype(F32)).astype(BF16)

    mine = [pl.BlockSpec((None, None) + hc, lambda s, core_ref: (0, s, core_ref[0], 0)) for hc in shapes]
    slot = [pl.BlockSpec((None, None) + hc, lambda s, core_ref: (0, s, 0, 0)) for hc in shapes]
    return _pcall(
        body, name=name, grid=(N_CHIPS,), num_prefetch=1,
        in_specs=mine + slot, out_specs=slot,
        out_shape=[_sds((1, N_CHIPS) + hc, BF16) for hc in shapes],
    )(core, *fulls, *gots)


def _chip_exchange_task(names, pair_sums, by_source, part=0, nparts=1):
    n = len(names)

    def rows(ref):
        h = ref.shape[1] // nparts
        return ref.at[:, pl.ds(part * h, h), :]

    def make(ins, outs, send_sem, recv_sem):
        x, y, c, chips = _mesh_position()
        s_me = _chip_index(x, y)
        starts, arrivals = [], []
        for a in range(n):
            for k, chip in enumerate(chips):
                s_k = _chip_index(*chip)
                starts.append(functools.partial(_remote, rows(ins[a].at[:, s_k]), rows(outs[a].at[:, s_me]),
                                                send_sem(3 * a + k), recv_sem(3 * a + k), (*chip, c)))
                got = rows(outs[a].at[:, s_k])
                arrivals.append(functools.partial(_remote, got, got, send_sem(3 * a + k), recv_sem(3 * a + k),
                                                  (*chip, c)))
        return starts, arrivals

    def operands():
        return [pair_sums[k] for k in names] + ([by_source[k] for k in names] if part else [])

    return _Task("chips", operands, lambda: [_sds(pair_sums[k].shape, pair_sums[k].dtype) for k in names],
                 {n + a: a for a in range(n)} if part else {}, 3 * n, make,
                 lambda res: by_source.update(zip(names, res)))


def _rs_chip_sum(name, owns, parts, chip):
    n = len(owns)
    ns = N_CHIPS
    shapes = [p.shape[2:] for p in parts]

    def body(chip_ref, *refs):
        me = chip_ref[0]
        for i in range(n):
            own_v = refs[i][...].astype(F32)
            slots = refs[n + ns * i:n + ns * (i + 1)]
            tot = None
            for s in range(ns):
                term = jnp.where(me == s, own_v, slots[s][...].astype(F32))
                tot = term if tot is None else tot + term
            refs[n + ns * n + i][...] = tot

    def slot_spec(hc, s):
        return pl.BlockSpec((None, None) + hc,
                            lambda g, chip_ref: (0, jnp.where(chip_ref[0] == s, (s + 1) % ns, s), 0, 0))

    own_specs = [pl.BlockSpec((None, None) + hc, lambda g, chip_ref: (0, chip_ref[0], 0, 0)) for hc in shapes]
    slot_specs = [slot_spec(hc, s) for hc in shapes for s in range(ns)]
    return _pcall(
        body, name=name, grid=(1,), num_prefetch=1,
        in_specs=own_specs + slot_specs,
        out_specs=[pl.BlockSpec((None,) + hc, lambda g, chip_ref: (0, 0, 0)) for hc in shapes],
        out_shape=[_sds((1,) + hc, F32) for hc in shapes],
    )(chip, *owns, *[p for p in parts for _ in range(ns)])


def _pair_gather_task(names, halves, sibling_halves):
    n = len(names)

    def make(ins, outs, send_sem, recv_sem):
        x, y, c, _ = _mesh_position()
        copies = [functools.partial(_remote, ins[a], outs[a], send_sem(a), recv_sem(a), (x, y, 1 - c))
                  for a in range(n)]
        return copies, copies

    return _Task("sibling", lambda: [halves[k] for k in names], lambda: [_sds(halves[k].shape, F32) for k in names],
                 {}, n, make, lambda res: sibling_halves.update(zip(names, res)))


def _small_allreduce(arrs):
    n = len(arrs)
    per = 1 + 2 * N_PEER_CHIPS

    def body(*refs):
        v_refs, o_refs = refs[:n], refs[n:2 * n]
        sib, pair, part = refs[2 * n:3 * n], refs[3 * n:4 * n], refs[4 * n:5 * n]
        send_sems, recv_sems = refs[5 * n:]
        x, y, c, chips = _mesh_position()
        s_me = _chip_index(x, y)

        def quarter(ref, s):
            q = ref.shape[0] // N_CHIPS
            return ref.at[pl.ds(pl.multiple_of(s * q, F32_TILE_ROWS), q)]

        def exchange(first_sem, src, dst_of, arrival_of):
            sems = lambda a, k: (send_sems.at[a * per + first_sem + k], recv_sems.at[a * per + first_sem + k])
            sends = [_remote(src(a, _chip_index(*chip)), dst_of(a, s_me), *sems(a, k), (*chip, c))
                     for a in range(n) for k, chip in enumerate(chips)]
            for cp in sends:
                cp.start()
            for a in range(n):
                for k, chip in enumerate(chips):
                    got = arrival_of(a, _chip_index(*chip))
                    _remote(got, got, *sems(a, k), (*chip, c)).wait_recv()
            for cp in sends:
                cp.wait_send()

        swaps = [_remote(v_refs[a], sib[a], send_sems.at[a * per], recv_sems.at[a * per], (x, y, 1 - c))
                 for a in range(n)]
        for cp in swaps:
            cp.start()
        for cp in swaps:
            cp.wait()
        for a in range(n):
            pair[a][...] = v_refs[a][...] + sib[a][...]
        exchange(1, lambda a, s_k: quarter(pair[a], s_k), lambda a, s: part[a].at[s], lambda a, s_k: part[a].at[s_k])
        for a in range(n):
            part[a][s_me] = quarter(pair[a], s_me)[...]
            q = o_refs[a].shape[0] // N_CHIPS
            o_refs[a][pl.ds(pl.multiple_of(s_me * q, F32_TILE_ROWS), q), :] = (
                ((part[a][0] + part[a][1]) + part[a][2]) + part[a][3])
        exchange(1 + N_PEER_CHIPS, lambda a, s_k: quarter(o_refs[a], s_me), lambda a, s: quarter(o_refs[a], s),
                 lambda a, s_k: quarter(o_refs[a], s_k))

    shapes = [a.shape for a in arrs]
    return _pcall(
        body, name="small_allreduce", grid=(1,), own_peers=("sibling", "chips"),
        in_specs=[VMEM_SPEC] * n, out_specs=[VMEM_SPEC] * n, out_shape=[_sds(s, F32) for s in shapes],
        scratch_shapes=([pltpu.VMEM(s, F32) for s in shapes] * 2
                        + [pltpu.VMEM((N_CHIPS, s[0] // N_CHIPS, s[1]), F32) for s in shapes]
                        + [pltpu.SemaphoreType.DMA((n * per,)), pltpu.SemaphoreType.DMA((n * per,))]),
    )(*arrs)


TRANSPOSED_WEIGHTS = ("ffn1_w1", "ffn1_w3", "ffn2_w1", "ffn2_w3")
SMALL_LAYOUT = [("ffn1_norm", 1), ("mix_norm", 1), ("ret_gn", 1), ("conv_b", 1), ("b_rgate", 1), ("b_igate", 1),
                ("lru_lambda", 1), ("xattn_norm", 1), ("mem_norm", 1), ("ffn2_norm", 1), ("final_norm", 1),
                ("b_branch_gate", 2), ("conv_w", CONV_TAPS)]
SMALL_ROWS = 32
GATE_WEIGHTS = ("w_rgate", "w_igate")
WEIGHT_ORDER = ["ffn1_norm", "ffn1_w1", "ffn1_w3", "ffn1_w2", "mix_norm", "w_in", "ret_gn", "w_ret_o", "conv_w",
                "conv_b", "w_rgate", "b_rgate", "w_igate", "b_igate", "lru_lambda", "w_lru_o", "w_branch_gate",
                "b_branch_gate", "w_out", "xattn_norm", "mem_norm", "w_xq", "w_xk", "w_xv", "w_xo", "ffn2_norm",
                "ffn2_w1", "ffn2_w3", "ffn2_w2", "final_norm"]


SMALL_USED_ROWS = sum(n for _, n in SMALL_LAYOUT)


def _pack_small(parts, extra_row=None):
    rows = [parts[name].reshape(n, D) for name, n in SMALL_LAYOUT]
    if extra_row is not None:
        rows.append(extra_row)
    rows.append(jnp.zeros((SMALL_ROWS - sum(r.shape[0] for r in rows), D), F32))
    return jnp.concatenate(rows, axis=0)


def _unpack_small(packed, shapes):
    out, r = {}, 0
    for name, n in SMALL_LAYOUT:
        out[name] = packed[r:r + n].reshape(shapes[name])
        r += n
    return out


def kernel(x, mem, ffn1_norm, ffn1_w1, ffn1_w3, ffn1_w2, mix_norm, w_in, ret_gn, w_ret_o, conv_w, conv_b, w_rgate, b_rgate, w_igate, b_igate, lru_lambda, w_lru_o, w_branch_gate, b_branch_gate, w_out, xattn_norm, mem_norm, w_xq, w_xk, w_xv, w_xo, ffn2_norm, ffn2_w1, ffn2_w3, ffn2_w2, final_norm, loss_target, m_ffn1_norm, m_ffn1_w1, m_ffn1_w3, m_ffn1_w2, m_mix_norm, m_w_in, m_ret_gn, m_w_ret_o, m_conv_w, m_conv_b, m_w_rgate, m_b_rgate, m_w_igate, m_b_igate, m_lru_lambda, m_w_lru_o, m_w_branch_gate, m_b_branch_gate, m_w_out, m_xattn_norm, m_mem_norm, m_w_xq, m_w_xk, m_w_xv, m_w_xo, m_ffn2_norm, m_ffn2_w1, m_ffn2_w3, m_ffn2_w2, m_final_norm, v_ffn1_norm, v_ffn1_w1, v_ffn1_w3, v_ffn1_w2, v_mix_norm, v_w_in, v_ret_gn, v_w_ret_o, v_conv_w, v_conv_b, v_w_rgate, v_b_rgate, v_w_igate, v_b_igate, v_lru_lambda, v_w_lru_o, v_w_branch_gate, v_b_branch_gate, v_w_out, v_xattn_norm, v_mem_norm, v_w_xq, v_w_xk, v_w_xv, v_w_xo, v_ffn2_norm, v_ffn2_w1, v_ffn2_w3, v_ffn2_w2, v_final_norm):
    given = dict(locals())
    w = {n: given[n] for n in WEIGHT_ORDER}
    mom = {n: given["m_" + n] for n in WEIGHT_ORDER}
    var = {n: given["v_" + n] for n in WEIGHT_ORDER}
    chip = _chip_index(lax.axis_index("x"), lax.axis_index("y"))
    core = lax.axis_index("c").astype(jnp.int32).reshape(1)

    chip_id = chip.astype(jnp.int32).reshape(1)
    sm = {n: w[n] for n in ["ffn1_norm", "mix_norm", "ret_gn", "conv_b", "b_rgate", "b_igate", "lru_lambda",
                            "xattn_norm", "mem_norm", "ffn2_norm", "b_branch_gate"]}
    sm["final_norm"] = w["final_norm"].reshape(1, D)
    sm["w_rgate"] = w["w_rgate"][0]
    sm["w_igate"] = w["w_igate"][0]

    local = lambda a, n: jnp.swapaxes(a[0], 0, 1) if n in TRANSPOSED_WEIGHTS else a[0]
    stack = lambda names: jnp.stack([local(w[n], n) for n in names], axis=0).astype(BF16)
    shard = {"col1": stack(["ffn1_w1", "ffn1_w3"]), "row2a": stack(["ffn1_w2"]),
             "win": jnp.swapaxes(w["w_in"], 1, 2).astype(BF16),
             "wbg": jnp.swapaxes(w["w_branch_gate"], 1, 2).astype(BF16),
             "sqA": stack(["w_ret_o", "w_lru_o", "w_out"]), "sqB": stack(["w_xq", "w_xk"]),
             "sqC": stack(["w_xv", "w_xo"]), "col2a": stack(["ffn2_w1"]), "col2b": stack(["ffn2_w3"]),
             "row2b": stack(["ffn2_w2"]), "conv": w["conv_w"]}
    gw, landed = {}, {}
    over_chips = lambda keys: _gather_chips_task({k: shard[k] for k in keys}, True, landed)
    to_sibling = lambda keys: _gather_sibling_task(keys, landed, gw)

    big, got, pair_sums, by_source, halves, sibling_halves, outs = {}, {}, {}, {}, {}, {}, {}
    pair_swap = lambda names: _pair_swap_task(names, big, got)
    exchange = lambda names, part=0, nparts=1: _chip_exchange_task(names, pair_sums, by_source, part, nparts)
    pair_gather = lambda names: _pair_gather_task(names, halves, sibling_halves)

    def pair_sum(names):
        res = _rs_pair_sum("rs_pair_sum_" + names[0], [big[n] for n in names], [got[n] for n in names], core)
        pair_sums.update(zip(names, res))

    def chip_sum(names):
        res = _rs_chip_sum("rs_chip_sum_" + names[0], [pair_sums[n] for n in names], [by_source[n] for n in names],
                           chip_id)
        halves.update(zip(names, res))

    def adamw(names):
        for n in names:
            res = _adamw_halves("adamw_" + n, local(w[n], n), halves[n], sibling_halves[n], 0, local(mom[n], n),
                                local(var[n], n), core)
            outs[n] = tuple((jnp.swapaxes(r, 0, 1) if n in TRANSPOSED_WEIGHTS else r)[None] for r in res)

    do = lambda fn, names: functools.partial(fn, names)
    ffn2_grads = ["ffn2_w2", "ffn2_w1", "ffn2_w3"]
    xattn_grads = ["w_xo", "w_xq", "w_xk", "w_xv"]
    mix_out_grads = ["w_branch_gate", "w_out", "w_ret_o", "w_lru_o"]
    conv_gather = _gather_chips_task({"conv": shard["conv"]}, False, gw)
    swap = lambda key: _gather_chips_task({key: shard[key]}, True, landed, legs="swap")
    pass_on = lambda key: _gather_chips_task({key: shard[key]}, True, landed, legs="pass_on")
    plan = _Plan()
    plan.tasks = {
        "ag_first_chips": [over_chips(["col1"]), swap("row2a")],
        "ag_first_sibling": [to_sibling(["col1"]), pass_on("row2a"), swap("win")],
        "ffn1_up": [to_sibling(["row2a"]), pass_on("win"), swap("wbg")],
        "ffn1_down": [to_sibling(["win"]), pass_on("wbg"), swap("sqA")],
        "mix_in": [to_sibling(["wbg"]), pass_on("sqA"), swap("col2a"), conv_gather],
        "ret_fwd": [to_sibling(["sqA"]), pass_on("col2a"), swap("sqB")],
        "lru_gates_fwd": [to_sibling(["col2a"]), pass_on("sqB"), swap("sqC")],
        "mix_gates": [to_sibling(["sqB"]), pass_on("sqC"), swap("col2b")],
        "lru_scan_fwd": [to_sibling(["sqC"]), pass_on("col2b")],
        "y_lru": [to_sibling(["col2b"]), swap("row2b")],
        "ffn2_up": [pass_on("row2b")],
        "ffn2_up_sibling": [to_sibling(["row2b"])],
        "ffn2_dh": [pair_swap(ffn2_grads)],
        "xattn_bwd": [exchange(["ffn2_w2"], 0, 2)],
        "d_hq": [exchange(["ffn2_w2"], 1, 2)],
        "d_merged": [exchange(["ffn2_w1"], 0, 2), pair_swap(xattn_grads)],
        "lru_out_bwd": [exchange(["w_xo"])],
        "ret_bwd": [exchange(["ffn2_w1"], 1, 2), exchange(["ffn2_w3"], 0, 2), pair_swap(mix_out_grads)],
        "lru_scan_bwd": [exchange(["ffn2_w3"], 1, 2)],
        "lru_gates_bwd": [exchange(["w_xq", "w_xk"]), pair_gather(ffn2_grads)],
        "dw_in": [exchange(["w_xv", "w_out"])],
        "d_h2": [exchange(["w_branch_gate", "w_ret_o", "w_lru_o"]), pair_swap(["w_in"]), pair_gather(xattn_grads)],
        "ffn1_bwd_mid": [exchange(["w_in"], 0, 2), pair_gather(mix_out_grads)],
        "ffn1_dw2": [exchange(["w_in"], 2, 4)],
        "ffn1_dw1": [exchange(["w_in"], 3, 4), pair_swap(["ffn1_w2"])],
        "ffn1_dw3": [exchange(["ffn1_w2"], 0, 2), pair_swap(["ffn1_w1"]), pair_gather(["w_in"])],
        "ffn1_dh": [exchange(["ffn1_w2"], 1, 2), exchange(["ffn1_w1"]), pair_swap(["ffn1_w3"])],
        "small_allreduce": [exchange(["ffn1_w3"], 0, 2), pair_gather(["ffn1_w2", "ffn1_w1"])],
        "adamw_ffn1_w2": [exchange(["ffn1_w3"], 1, 2)],
        "adamw_ffn1_w1": [pair_gather(["ffn1_w3"])],
    }
    plan.after = {
        "ffn2_up": [functools.partial(_comm_call, "ffn2_up_sibling")],
        "ffn2_dh": [do(pair_sum, ffn2_grads)],
        "d_merged": [do(pair_sum, xattn_grads)],
        "ret_bwd": [do(pair_sum, mix_out_grads)],
        "lru_scan_bwd": [do(chip_sum, ffn2_grads)],
        "lru_gates_bwd": [do(adamw, ffn2_grads)],
        "dw_in": [do(chip_sum, xattn_grads)],
        "d_h2": [do(chip_sum, mix_out_grads), do(pair_sum, ["w_in"]), do(adamw, xattn_grads)],
        "ffn1_bwd_mid": [do(adamw, mix_out_grads)],
        "ffn1_dw1": [do(chip_sum, ["w_in"]), do(pair_sum, ["ffn1_w2"])],
        "ffn1_dw3": [do(pair_sum, ["ffn1_w1"]), do(adamw, ["w_in"])],
        "ffn1_dh": [do(pair_sum, ["ffn1_w3"]), do(chip_sum, ["ffn1_w2", "ffn1_w1"])],
        "small_allreduce": [do(adamw, ["ffn1_w2"])],
        "adamw_ffn1_w2": [do(chip_sum, ["ffn1_w3"]), do(adamw, ["ffn1_w1", "ffn1_w3"])],
    }
    global _plan
    _plan = plan
    try:
        _comm_call("ag_first_chips")
        _comm_call("ag_first_sibling")
        loss_part, grad_x, small = _local_step(x[0], mem[0], loss_target[0], gw, sm, big)
        gate2d = lambda a: a.reshape(LRU_BLOCKS * LRU_BLOCK, LRU_BLOCK)
        loss_row = jnp.pad(loss_part, ((0, 0), (0, D - loss_part.shape[1])))
        small_sum, *gate_sums = _small_allreduce([_pack_small(small, loss_row)]
                                                 + [gate2d(small[n]) for n in GATE_WEIGHTS])
    finally:
        _plan = None
    assert not plan.tasks and not plan.after, (list(plan.tasks), list(plan.after))
    loss = small_sum[SMALL_USED_ROWS, 0]

    small_shapes = {n: w[n].shape for n, _ in SMALL_LAYOUT}
    small_shapes["conv_w"] = (CONV_TAPS, D)
    conv_row = SMALL_USED_ROWS - CONV_TAPS
    conv_grad = lax.dynamic_slice(small_sum[conv_row:conv_row + CONV_TAPS], (0, chip * SQ_BLK), (CONV_TAPS, SQ_BLK))
    small_w = {n: w[n] for n, _ in SMALL_LAYOUT}
    small_m = {n: mom[n] for n, _ in SMALL_LAYOUT}
    small_v = {n: var[n] for n, _ in SMALL_LAYOUT}
    pad_cols = lambda a: jnp.pad(a[0], ((0, 0), (0, D - SQ_BLK)))
    for dct in (small_w, small_m, small_v):
        dct["conv_w"] = pad_cols(dct["conv_w"])
    g_pack = lax.dynamic_update_slice(small_sum, jnp.pad(conv_grad, ((0, 0), (0, D - SQ_BLK))), (conv_row, 0))
    d_pack, m_pack, v_pack = _adamw("adamw_small", _pack_small(small_w), g_pack, _pack_small(small_m),
                                    _pack_small(small_v))
    unpacked = [_unpack_small(p, small_shapes) for p in (g_pack, d_pack, m_pack, v_pack)]
    for n, _ in SMALL_LAYOUT:
        if n == "conv_w":
            outs[n] = tuple(u[n][:, :SQ_BLK][None] for u in unpacked)
        else:
            outs[n] = tuple(u[n] for u in unpacked)
    for n, gsum in zip(GATE_WEIGHTS, gate_sums):
        d, nm, nv = _adamw("adamw_" + n, gate2d(w[n]), gsum, gate2d(mom[n]), gate2d(var[n]))
        outs[n] = tuple(r.reshape(w[n].shape) for r in (gsum, d, nm, nv))

    result = [loss, grad_x[None]]
    for k in range(4):
        result += [outs[n][k] for n in WEIGHT_ORDER]
    return tuple(result)
```

```python
import functools
import math

import jax
import jax.numpy as jnp
from jax import lax
from jax.experimental import pallas as pl
from jax.experimental.pallas import tpu as pltpu

F32 = jnp.float32
BF16 = jnp.bfloat16
GRAD_WIRE_DTYPE = BF16
MESH = pl.DeviceIdType.MESH

D = 1024
EPS = 1e-6
RET_HEADS = 4
RET_DK = 128
RET_DV = 256
CHUNK = 128
ROPE_BASE = 10000.0
LRU_BLOCKS = 8
LRU_BLOCK = 128
CONV_TAPS = 4
LRU_C = 8.0
D_FF = 2816
X_HEADS = 4
X_HD = 256
N_CHIPS = 4
FF_BLK = D_FF // N_CHIPS
IN_BLK = 5120 // N_CHIPS
BG_BLK = 2048 // N_CHIPS
SQ_BLK = D // N_CHIPS

ADAM_LR = 0.001
ADAM_B1 = 0.9
ADAM_B2 = 0.999
ADAM_EPS = 1e-08
ADAM_WD = 0.01
ADAM_STEP = 10

F32_TILE_ROWS = 8
BF16_TILE_ROWS = 16
VMEM_LIMIT_BYTES = 56 * 1024 * 1024
ROW_TILE = 512
WIDE_ROW_TILE = 1024
FFN_ROW_TILE = 256
DW_BLK = D_FF // 2
SCAN_TILE = 256
RET_STEP_CHUNKS = 2
RET_STEP_ROWS = RET_STEP_CHUNKS * CHUNK

_DN = {
    "nn": (((1,), (0,)), ((), ())),
    "nt": (((1,), (1,)), ((), ())),
    "tn": (((0,), (0,)), ((), ())),
}


def _cparams(n_axes, collective_id=None):
    return pltpu.CompilerParams(dimension_semantics=("arbitrary",) * n_axes,
                                vmem_limit_bytes=VMEM_LIMIT_BYTES, collective_id=collective_id)


def _dot(a, b, kind):
    if b.ndim == 3:
        b = b.reshape(b.shape[0] * b.shape[1], b.shape[2])
    return lax.dot_general(a.astype(BF16), b.astype(BF16), _DN[kind], preferred_element_type=F32)


def _sigmoid(x):
    return 1.0 / (1.0 + jnp.exp(-x))


def _log1p_pos(e):
    u = 1.0 + e
    return jnp.where(u == 1.0, e, jnp.log(u) * (e / jnp.where(u == 1.0, 1.0, u - 1.0)))


def _expm1(x):
    u = jnp.exp(x)
    lu = jnp.log(u)
    safe = jnp.where(lu == 0.0, 1.0, lu)
    return jnp.where(u == 1.0, x, (u - 1.0) * (x / safe))


def _softplus(z):
    return jnp.maximum(z, 0.0) + _log1p_pos(jnp.exp(-jnp.abs(z)))


_GELU_C = math.sqrt(2.0 / math.pi)


def _gelu_and_grad(x):
    x2 = x * x
    t = jnp.tanh(_GELU_C * (x + 0.044715 * x * x2))
    g = 0.5 * x * (1.0 + t)
    dg = 0.5 * (1.0 + t) + 0.5 * x * (1.0 - t * t) * (_GELU_C * (1.0 + 3.0 * 0.044715 * x2))
    return g, dg


def _rms_fwd(x, g):
    r = lax.rsqrt(jnp.mean(x * x, axis=-1, keepdims=True) + EPS)
    return (x * r) * g


def _rms_bwd(x, g, dh):
    r = lax.rsqrt(jnp.mean(x * x, axis=-1, keepdims=True) + EPS)
    n = x * r
    dyg = dh * g
    dx = r * (dyg - n * jnp.mean(dyg * n, axis=-1, keepdims=True))
    return dx, jnp.sum(dh * n, axis=0, keepdims=True)


def _accumulate(ref, val, first):
    @pl.when(first)
    def _():
        ref[...] = val

    @pl.when(jnp.logical_not(first))
    def _():
        ref[...] += val


def _sds(shape, dtype):
    return jax.ShapeDtypeStruct(tuple(shape), dtype)


def _spec(shape, fn):
    return pl.BlockSpec(tuple(shape), fn)


class _Task:
    def __init__(self, peers, operands, out_shapes, aliases, nsem, make, finish, make_second=None):
        self.peers = peers
        self.operands, self.out_shapes, self.aliases = operands, out_shapes, aliases
        self.nsem, self.make, self.finish = nsem, make, finish
        self.make_second = make_second


class _Plan:
    def __init__(self):
        self.tasks, self.after = {}, {}


_plan = None


_CHIP_PEER_SETS = [frozenset({"chips"}), frozenset({"first"}), frozenset({"second"}), frozenset({"first", "second"})]
PEER_SET_COLLECTIVE_ID = {frozenset({"sibling"}): 1}
for _i, _chip_peers in enumerate(_CHIP_PEER_SETS):
    PEER_SET_COLLECTIVE_ID[_chip_peers] = 2 + 2 * _i
    PEER_SET_COLLECTIVE_ID[_chip_peers | {"sibling"}] = 3 + 2 * _i


def _peer_set(names):
    names = frozenset(n for name in names for n in name.split("+"))
    return names - {"first", "second"} if "chips" in names else names


def _axis_neighbours(x, y, c):
    flip = lambda v, f: v + f * (1 - 2 * v)
    return (flip(x, 1 - c), flip(y, c)), (flip(x, c), flip(y, 1 - c))


def _entry_handshake(peer_set):
    x, y, c, chips = _mesh_position()
    first, second = _axis_neighbours(x, y, c)
    peers = [(x, y, 1 - c)] if "sibling" in peer_set else []
    if "chips" in peer_set:
        peers += [(*chip, c) for chip in chips]
    if "first" in peer_set:
        peers.append((*first, c))
    if "second" in peer_set:
        peers.append((*second, c))
    barrier = pltpu.get_barrier_semaphore()
    for peer in peers:
        pl.semaphore_signal(barrier, inc=1, device_id=peer, device_id_type=MESH)
    pl.semaphore_wait(barrier, len(peers))


def _pcall(body, *, name, grid, in_specs, out_specs, out_shape, scratch_shapes=(), num_prefetch=0, own_peers=()):
    single = not isinstance(out_shape, (list, tuple))
    out_shape = [out_shape] if single else list(out_shape)
    out_specs = [out_specs] if single else list(out_specs)
    in_specs = list(in_specs)
    scratch_shapes = list(scratch_shapes)
    tasks = _plan.tasks.pop(name, []) if _plan is not None else []
    after = _plan.after.pop(name, []) if _plan is not None else []
    peer_set = _peer_set([t.peers for t in tasks] + list(own_peers))
    nax = len(grid)

    def run(*operands):
        n_in = len(operands) - num_prefetch
        n_out = len(out_shape)
        t_ops = [t.operands() for t in tasks]
        t_outs = [t.out_shapes() for t in tasks]
        c_ops = [a for ops in t_ops for a in ops]
        c_outs = [s for outs in t_outs for s in outs]
        aliases = {}
        i0, o0 = num_prefetch + n_in, n_out
        for t, ops, outs in zip(tasks, t_ops, t_outs):
            for i_loc, o_loc in t.aliases.items():
                aliases[i0 + i_loc] = o0 + o_loc
            i0 += len(ops)
            o0 += len(outs)
        nsem = sum(t.nsem for t in tasks)

        def wrapped(*refs):
            p = num_prefetch
            pre, ins = refs[:p], refs[p:p + n_in]
            cins = refs[p + n_in:p + n_in + len(c_ops)]
            q = p + n_in + len(c_ops)
            outs, couts = refs[q:q + n_out], refs[q + n_out:q + n_out + len(c_outs)]
            q += n_out + len(c_outs)
            scr = refs[q:q + len(scratch_shapes)]

            def rounds(second):
                send_sems, recv_sems = refs[q + len(scratch_shapes):]
                out = []
                ci = co = so = 0
                for t, ops, souts in zip(tasks, t_ops, t_outs):
                    make = t.make_second if second else t.make
                    out.append(([], []) if make is None else
                               make(cins[ci:ci + len(ops)], couts[co:co + len(souts)],
                                    functools.partial(lambda base, k: send_sems.at[base + k], so),
                                    functools.partial(lambda base, k: recv_sems.at[base + k], so)))
                    ci, co, so = ci + len(ops), co + len(souts), so + t.nsem
                return out

            two_rounds = [t.make_second is not None for t in tasks]
            if peer_set:
                ids = [pl.program_id(k) for k in range(nax)]
                first = functools.reduce(jnp.logical_and, [i == 0 for i in ids])
                last = functools.reduce(jnp.logical_and, [i == g - 1 for i, g in zip(ids, grid)])
                step = functools.reduce(lambda acc, ig: acc * ig[1] + ig[0], zip(ids, grid), 0)
                middle = step == math.prod(grid) // 3

                @pl.when(first)
                def _():
                    _entry_handshake(peer_set)
                    for starts, _ in rounds(False):
                        for copy in starts:
                            copy().start()

            body(*pre, *ins, *outs, *scr)

            if any(two_rounds):
                @pl.when(middle)
                def _():
                    for (_, arrivals), two in zip(rounds(False), two_rounds):
                        if two:
                            for arrival in arrivals:
                                arrival().wait_recv()
                    for starts, _ in rounds(True):
                        for copy in starts:
                            copy().start()

            if tasks:
                @pl.when(last)
                def _():
                    first_round, second_round = rounds(False), rounds(True)
                    for (_, arrivals1), (_, arrivals2), two in zip(first_round, second_round, two_rounds):
                        for arrival in (arrivals2 if two else arrivals1):
                            arrival().wait_recv()
                    for starts, _ in first_round + second_round:
                        for copy in starts:
                            copy().wait_send()

        sems = [pltpu.SemaphoreType.DMA((nsem,)), pltpu.SemaphoreType.DMA((nsem,))] if tasks else []
        res = pl.pallas_call(
            wrapped, name=name,
            grid_spec=pltpu.PrefetchScalarGridSpec(
                num_scalar_prefetch=num_prefetch, grid=tuple(grid),
                in_specs=in_specs + [ANY_SPEC] * len(c_ops),
                out_specs=out_specs + [ANY_SPEC] * len(c_outs),
                scratch_shapes=scratch_shapes + sems),
            out_shape=out_shape + c_outs,
            input_output_aliases=aliases,
            compiler_params=_cparams(nax, PEER_SET_COLLECTIVE_ID[peer_set] if peer_set else None),
        )(*operands, *c_ops)
        co = n_out
        for t, souts in zip(tasks, t_outs):
            t.finish(res[co:co + len(souts)])
            co += len(souts)
        for fn in after:
            fn()
        return res[0] if single else list(res[:n_out])

    return run


def _comm_call(name):
    def body(o_ref):
        o_ref[...] = jnp.zeros_like(o_ref)

    _pcall(body, name=name, grid=(1,), in_specs=[], out_specs=_spec((8, 128), lambda i: (0, 0)),
           out_shape=_sds((8, 128), F32))()


def _gemm(name, terms, grid, outs, acc_shape, extras=(), epilogue=None):
    kinds = [t[4] for t in terms]
    nt, ne, no = len(terms), len(extras), len(outs)
    nred = grid[-1]
    nax = len(grid)

    def body(*refs):
        trefs = refs[:2 * nt]
        erefs = refs[2 * nt:2 * nt + ne]
        orefs = refs[2 * nt + ne:2 * nt + ne + no]
        ids = [pl.program_id(k) for k in range(nax)]
        tot = None
        for t in range(nt):
            d = _dot(trefs[2 * t][...], trefs[2 * t + 1][...], kinds[t])
            tot = d if tot is None else tot + d

        def finish(acc):
            if epilogue is None:
                orefs[0][...] = acc.astype(orefs[0].dtype)
            else:
                epilogue(acc, erefs, orefs, ids)

        if nred == 1:
            finish(tot)
        else:
            acc_ref = refs[-1]
            r = ids[-1]

            @pl.when(r == 0)
            def _():
                acc_ref[...] = tot

            @pl.when(r > 0)
            def _():
                acc_ref[...] += tot

            @pl.when(r == nred - 1)
            def _():
                finish(acc_ref[...])

    operands, in_specs = [], []
    for a, a_spec, b, b_spec, _ in terms:
        operands += [a, b]
        in_specs += [a_spec, b_spec]
    for e, e_spec in extras:
        operands.append(e)
        in_specs.append(e_spec)
    scratch = [pltpu.VMEM(tuple(acc_shape), F32)] if nred > 1 else []
    return _pcall(body, name=name, grid=tuple(grid), in_specs=in_specs, out_specs=[o[1] for o in outs],
                  out_shape=[o[0] for o in outs], scratch_shapes=scratch)(*operands)


def _rowwise(name, fn, ins, outs, grid):
    ni = len(ins)
    nax = len(grid)

    def body(*refs):
        ids = [pl.program_id(k) for k in range(nax)]
        fn(refs[:ni], refs[ni:], ids)

    return _pcall(body, name=name, grid=tuple(grid), in_specs=[i[1] for i in ins],
                  out_specs=[o[1] for o in outs], out_shape=[o[0] for o in outs])(*[i[0] for i in ins])


def _ffn_up(name, h, w1buf, w1_idx, w3buf, w3_idx, norm_gain=None):
    T = h.shape[0]
    tm = min(FFN_ROW_TILE, T)
    normed = norm_gain is not None

    def body(h_ref, *refs):
        if normed:
            g_ref, w1_ref, w3_ref, a_ref, b_ref, s_ref, hn_ref = refs
            hv = _rms_fwd(h_ref[...], g_ref[...]).astype(BF16)
            hn_ref[...] = hv
        else:
            w1_ref, w3_ref, a_ref, b_ref, s_ref = refs
            hv = h_ref[...]
        a = _dot(hv, w1_ref[...], "nt")
        b = _dot(hv, w3_ref[...], "nt")
        a_ref[...] = a.astype(BF16)
        b_ref[...] = b.astype(BF16)
        s_ref[...] = ((a * _sigmoid(a)) * b).astype(BF16)

    row = _spec((tm, D), lambda i: (i, 0))
    blk = _spec((tm, D_FF), lambda i: (i, 0))
    return _pcall(
        body, name=name, grid=(T // tm,),
        in_specs=[row] + ([_spec((1, D), lambda i: (0, 0))] if normed else [])
        + [_spec((N_CHIPS, None, FF_BLK, D), lambda i: (0, w1_idx, 0, 0)),
           _spec((N_CHIPS, None, FF_BLK, D), lambda i: (0, w3_idx, 0, 0))],
        out_specs=[blk, blk, blk] + ([row] if normed else []),
        out_shape=[_sds((T, D_FF), BF16)] * 3 + ([_sds((T, D), BF16)] if normed else []),
    )(h, *([norm_gain] if normed else []), w1buf, w3buf)


def _loss_head(x, g, tgt, loss_ref, dx_ref, dg_ref, first):
    err = _rms_fwd(x, g) - tgt
    lp = 0.5 * jnp.sum(jnp.mean(err * err, axis=-1, keepdims=True), axis=0, keepdims=True)
    _accumulate(loss_ref, jnp.broadcast_to(lp, (1, 128)), first)
    dx, dgp = _rms_bwd(x, g, err * (1.0 / D))
    dx_ref[...] = dx
    _accumulate(dg_ref, dgp, first)


def _ffn_down(name, s, wrow2, w2_idx, x_res, g_next=None, loss_target=None):
    T = x_res.shape[0]
    tm = min(ROW_TILE, T)
    row = lambda i, j, r: (i, 0)
    vec = lambda i, j, r: (0, 0)

    def epilogue(acc, erefs, orefs, ids):
        xo = erefs[0][...] + 0.5 * acc
        if loss_target is not None:
            _loss_head(xo, erefs[1][...], erefs[2][...], orefs[0], orefs[1], orefs[2], ids[0] == 0)
            return
        orefs[0][...] = xo
        orefs[1][...] = _rms_fwd(xo, erefs[1][...]).astype(BF16)

    extras = [(x_res, _spec((tm, D), row)), (g_next, _spec((1, D), vec))]
    if loss_target is None:
        outs = [(_sds((T, D), F32), _spec((tm, D), row)), (_sds((T, D), BF16), _spec((tm, D), row))]
    else:
        extras.append((loss_target, _spec((tm, D), row)))
        outs = [(_sds((1, 128), F32), _spec((1, 128), vec)), (_sds((T, D), F32), _spec((tm, D), row)),
                (_sds((1, D), F32), _spec((1, D), vec))]
    return _gemm(
        name,
        [(s, _spec((tm, D_FF), row),
          wrow2, _spec((N_CHIPS, None, FF_BLK, D), lambda i, j, r: (0, w2_idx, 0, 0)), "nn")],
        (T // tm, 1, 1), outs, (tm, D), extras, epilogue)


def _ffn_bwd_mid(name, dx, wrow2, w2_idx, a, b):
    T = dx.shape[0]
    tm = min(FFN_ROW_TILE, T)

    def body(dx_ref, w2_ref, a_ref, b_ref, dab_ref):
        ds = _dot(0.5 * dx_ref[...], w2_ref[...], "nt")
        av = a_ref[...].astype(F32)
        sg = _sigmoid(av)
        dab_ref[0] = (ds * b_ref[...].astype(F32) * (sg * (1.0 + av * (1.0 - sg)))).astype(BF16)
        dab_ref[1] = (ds * (av * sg)).astype(BF16)

    blk = _spec((tm, D_FF), lambda i: (i, 0))
    return _pcall(
        body, name=name, grid=(T // tm,),
        in_specs=[_spec((tm, D), lambda i: (i, 0)),
                  _spec((N_CHIPS, None, FF_BLK, D), lambda i: (0, w2_idx, 0, 0)),
                  blk, blk],
        out_specs=_spec((2, tm, D_FF), lambda i: (0, i, 0)),
        out_shape=_sds((2, T, D_FF), BF16),
    )(dx, wrow2, a, b)


def _rms_bwd_epilogue(acc, erefs, orefs, ids):
    dx, dgp = _rms_bwd(erefs[0][...], erefs[1][...], acc)
    orefs[0][...] = dx + erefs[2][...]
    _accumulate(orefs[1], dgp, ids[0] == 0)


def _rms_bwd_io(x, g, dres, T, tm):
    row = lambda i, j, r: (i, 0)
    vec = lambda i, j, r: (0, 0)
    extras = [(x, _spec((tm, D), row)), (g, _spec((1, D), vec)), (dres, _spec((tm, D), row))]
    outs = [(_sds((T, D), F32), _spec((tm, D), row)), (_sds((1, D), F32), _spec((1, D), vec))]
    return extras, outs


def _ffn_bwd(tag, dx_out, h, a, b, s, w1buf, w1_idx, w3buf, w3_idx, wrow2, w2_idx, x_in, g, big):
    T = dx_out.shape[0]
    dab = _ffn_bwd_mid(tag + "_bwd_mid", dx_out, wrow2, w2_idx, a, b)

    def half_scale(acc, erefs, orefs, ids):
        orefs[0][...] = (0.5 * acc).astype(orefs[0].dtype)

    dw_grid = (D_FF // DW_BLK, 1, 1)
    dw_out = [(_sds((D_FF, D), GRAD_WIRE_DTYPE), _spec((DW_BLK, D), lambda j, n, r: (j, 0)))]
    tokens = _spec((T, D), lambda j, n, r: (0, 0))
    big[tag + "_w2"] = _gemm(
        tag + "_dw2", [(s, _spec((T, DW_BLK), lambda j, n, r: (0, j)), dx_out, tokens, "tn")],
        dw_grid, dw_out, (DW_BLK, D), (), half_scale)[0].reshape(1, N_CHIPS, FF_BLK, D)
    for widx, wname in ((0, "_w1"), (1, "_w3")):
        big[tag + wname] = _gemm(
            tag + "_d" + wname[1:],
            [(dab, _spec((None, T, DW_BLK), functools.partial(lambda w, j, n, r: (w, 0, j), widx)), h, tokens, "tn")],
            dw_grid, dw_out, (DW_BLK, D))[0].reshape(1, N_CHIPS, FF_BLK, D)
    tm = min(FFN_ROW_TILE, T)
    extras, outs = _rms_bwd_io(x_in, g, dx_out, T, tm)
    whole = lambda idx: _spec((N_CHIPS, None, FF_BLK, D), lambda i, j, r: (0, idx, 0, 0))
    dx_in, dg = _gemm(
        tag + "_dh",
        [(dab, _spec((None, tm, D_FF), lambda i, j, r: (0, i, 0)), w1buf, whole(w1_idx), "nn"),
         (dab, _spec((None, tm, D_FF), lambda i, j, r: (1, i, 0)), w3buf, whole(w3_idx), "nn")],
        (T // tm, 1, 1), outs, (tm, D), extras, _rms_bwd_epilogue)
    return dx_in, dg


def _proj_sq(name, a, wsq, idx, kind, out_dtype=F32, extras=(), epilogue=None, outs=None):
    M = a.shape[0]
    tm = min(ROW_TILE, M)
    if outs is None:
        outs = [(_sds((M, D), out_dtype), _spec((tm, D), lambda i, j, r: (i, 0)))]
    return _gemm(
        name,
        [(a, _spec((tm, D), lambda i, j, r: (i, 0)),
          wsq, _spec((N_CHIPS, None, SQ_BLK, D), lambda i, j, r: (0, idx, 0, 0)), kind)],
        (M // tm, 1, 1), outs, (tm, D), extras, epilogue)


def _dw_sq(name, a, b):
    M = a.shape[0]
    tn = D // 2
    whole = _gemm(
        name,
        [(a, _spec((M, D), lambda i, j, r: (0, 0)), b, _spec((M, tn), lambda i, j, r: (0, j)), "tn")],
        (1, D // tn, 1),
        [(_sds((D, D), GRAD_WIRE_DTYPE), _spec((D, tn), lambda i, j, r: (0, j)))],
        (D, tn))[0]
    return whole.reshape(N_CHIPS, SQ_BLK, D)


def _retention_constants(T):
    pos = jnp.arange(T, dtype=F32)
    inv_freq = ROPE_BASE ** (-jnp.arange(0, RET_DK, 2, dtype=F32) / RET_DK)
    ang = pos[:, None] * inv_freq[None, :]
    cosf = jnp.concatenate([jnp.cos(ang), jnp.cos(ang)], axis=1)
    sins = jnp.concatenate([-jnp.sin(ang), jnp.sin(ang)], axis=1)
    lg = jnp.log(1.0 - 2.0 ** (-5.0 - jnp.arange(RET_HEADS, dtype=F32)))
    p = jnp.arange(CHUNK, dtype=F32)
    rel = p[:, None] - p[None, :]
    dmat = jnp.where(rel[None] >= 0, jnp.exp(rel[None] * lg[:, None, None]), 0.0)
    kd = jnp.exp((CHUNK - 1.0 - p)[None, :] * lg[:, None])[:, :, None]
    qd = jnp.exp((p + 1.0)[None, :] * lg[:, None])[:, :, None]
    cd = jnp.exp(CHUNK * lg)[:, None, None]
    return cosf, sins, dmat, kd, qd, cd


def _rot(t, cosv, sinv):
    return t * cosv + pltpu.roll(t, RET_DK // 2, 1) * sinv


def _unrot(t, cosv, sinv):
    return t * cosv - pltpu.roll(t, RET_DK // 2, 1) * sinv


def _ret_const_specs(cm):
    whole = lambda shape: _spec(shape, lambda c: (0,) * len(shape))
    return [
        _spec((RET_STEP_ROWS, RET_DK), lambda c: (cm(c), 0)),
        _spec((RET_STEP_ROWS, RET_DK), lambda c: (cm(c), 0)),
        whole((RET_HEADS, CHUNK, CHUNK)), whole((RET_HEADS, CHUNK, 1)), whole((RET_HEADS, CHUNK, 1)),
        whole((RET_HEADS, 1, 1)),
    ]


def _head(h, width):
    return slice(h * width, (h + 1) * width)


def _ret_fwd(u, consts, ret_gn):
    T = u.shape[0]
    nC = T // CHUNK
    kscale = RET_DK ** -0.5

    def body(q_ref, k_ref, v_ref, g_ref, cos_ref, sin_ref, dm_ref, kd_ref, qd_ref, cd_ref, gn_ref,
             qr_ref, kr_ref, ret_ref, yr_ref, st_ref, state):
        @pl.when(pl.program_id(0) == 0)
        def _():
            state[...] = jnp.zeros_like(state)

        for cc in range(RET_STEP_CHUNKS):
            rows = slice(cc * CHUNK, (cc + 1) * CHUNK)
            cosv, sinv = cos_ref[rows, :], sin_ref[rows, :]
            for h in range(RET_HEADS):
                hk, hv = _head(h, RET_DK), _head(h, RET_DV)
                q = _rot(q_ref[rows, hk], cosv, sinv)
                k = _rot(k_ref[rows, hk], cosv, sinv) * kscale
                v = v_ref[rows, hv]
                qr_ref[rows, hk] = q
                kr_ref[rows, hk] = k
                prev = state[h]
                st_ref[h, cc] = prev
                s = _dot(q, k, "nt") * dm_ref[h]
                ret = _dot(s, v, "nn") + _dot(q, prev, "nn") * qd_ref[h]
                state[h] = cd_ref[h] * prev + _dot(k * kd_ref[h], v, "tn")
                ret_ref[rows, hv] = ret
                mu = jnp.mean(ret, axis=-1, keepdims=True)
                xc = ret - mu
                yn = xc * lax.rsqrt(jnp.mean(xc * xc, axis=-1, keepdims=True) + EPS)
                g = g_ref[rows, hv]
                yr_ref[rows, hv] = ((g * _sigmoid(g)) * (yn * gn_ref[:, hv])).astype(BF16)

    cm = lambda c: c
    qk_w, v_w = RET_HEADS * RET_DK, RET_HEADS * RET_DV
    in_specs = [
        _spec((RET_STEP_ROWS, qk_w), lambda c: (c, 0)), _spec((RET_STEP_ROWS, qk_w), lambda c: (c, 1)),
        _spec((RET_STEP_ROWS, v_w), lambda c: (c, 1)), _spec((RET_STEP_ROWS, v_w), lambda c: (c, 2)),
    ] + _ret_const_specs(cm) + [_spec((1, v_w), lambda c: (0, 0))]
    qk_out = _spec((RET_STEP_ROWS, qk_w), lambda c: (c, 0))
    v_out = _spec((RET_STEP_ROWS, v_w), lambda c: (c, 0))
    return _pcall(
        body, name="ret_fwd", grid=(nC // RET_STEP_CHUNKS,),
        in_specs=in_specs,
        out_specs=[qk_out, qk_out, v_out, v_out,
                   _spec((RET_HEADS, RET_STEP_CHUNKS, RET_DK, RET_DV), lambda c: (0, c, 0, 0))],
        out_shape=[_sds((T, qk_w), F32), _sds((T, qk_w), F32), _sds((T, v_w), F32), _sds((T, v_w), BF16),
                   _sds((RET_HEADS, nC, RET_DK, RET_DV), F32)],
        scratch_shapes=[pltpu.VMEM((RET_HEADS, RET_DK, RET_DV), F32)],
    )(u, u, u, u, *consts, ret_gn)


def _ret_bwd(dyr, ret, u, qr, kr, states, consts, ret_gn):
    T = u.shape[0]
    nC = T // CHUNK
    kscale = RET_DK ** -0.5

    def body(dyr_ref, ret_ref, g_ref, q_ref, k_ref, v_ref, st_ref,
             cos_ref, sin_ref, dm_ref, kd_ref, qd_ref, cd_ref, gn_ref,
             dq_ref, dk_ref, dv_ref, dg_ref, dgn_ref, gstate):
        first = pl.program_id(0) == 0

        @pl.when(first)
        def _():
            gstate[...] = jnp.zeros_like(gstate)

        dgn_total = None
        for cc in reversed(range(RET_STEP_CHUNKS)):
            rows = slice(cc * CHUNK, (cc + 1) * CHUNK)
            cosv, sinv = cos_ref[rows, :], sin_ref[rows, :]
            dgn_parts = []
            for h in range(RET_HEADS):
                hk, hv = _head(h, RET_DK), _head(h, RET_DV)
                ret = ret_ref[rows, hv]
                mu = jnp.mean(ret, axis=-1, keepdims=True)
                xc = ret - mu
                rs = lax.rsqrt(jnp.mean(xc * xc, axis=-1, keepdims=True) + EPS)
                yn = xc * rs
                gn = gn_ref[:, hv]
                g = g_ref[rows, hv]
                sg = _sigmoid(g)
                dyr_v = dyr_ref[rows, hv]
                dretn = dyr_v * (g * sg)
                dg_ref[rows, hv] = (dyr_v * (yn * gn) * (sg * (1.0 + g * (1.0 - sg)))).astype(BF16)
                dgn_parts.append(jnp.sum(dretn * yn, axis=0, keepdims=True))
                dyn = dretn * gn
                d_o = rs * (dyn - jnp.mean(dyn, axis=-1, keepdims=True)
                            - yn * jnp.mean(dyn * yn, axis=-1, keepdims=True))

                q, k, v = q_ref[rows, hk], k_ref[rows, hk], v_ref[rows, hv]
                dmat, kd, qd = dm_ref[h], kd_ref[h], qd_ref[h]
                prev = st_ref[h, cc]
                gnext = gstate[h]
                s = _dot(q, k, "nt") * dmat
                ds = _dot(d_o, v, "nt") * dmat
                doq = d_o * qd
                dq = _dot(ds, k, "nn") + _dot(doq, prev, "nt")
                dk = _dot(ds, q, "tn") + _dot(v, gnext, "nt") * kd
                dv = _dot(s, d_o, "tn") + _dot(k * kd, gnext, "nn")
                gstate[h] = cd_ref[h] * gnext + _dot(q, doq, "tn")
                dq_ref[rows, hk] = _unrot(dq, cosv, sinv).astype(BF16)
                dk_ref[rows, hk] = _unrot(dk * kscale, cosv, sinv).astype(BF16)
                dv_ref[rows, hv] = dv.astype(BF16)
            dgn = jnp.concatenate(dgn_parts, axis=1)
            dgn_total = dgn if dgn_total is None else dgn_total + dgn
        _accumulate(dgn_ref, dgn_total, first)

    n_steps = nC // RET_STEP_CHUNKS
    cm = lambda c: n_steps - 1 - c
    qk_w, v_w = RET_HEADS * RET_DK, RET_HEADS * RET_DV
    vspec = lambda blk: _spec((RET_STEP_ROWS, v_w), lambda c: (cm(c), blk))
    qspec = _spec((RET_STEP_ROWS, qk_w), lambda c: (cm(c), 0))
    in_specs = [vspec(0), vspec(0), vspec(2), qspec, qspec, vspec(1),
                _spec((RET_HEADS, RET_STEP_CHUNKS, RET_DK, RET_DV), lambda c: (0, cm(c), 0, 0)),
                ] + _ret_const_specs(cm) + [_spec((1, v_w), lambda c: (0, 0))]
    return _pcall(
        body, name="ret_bwd", grid=(n_steps,),
        in_specs=in_specs,
        out_specs=[qspec, qspec, vspec(0), vspec(0), _spec((1, v_w), lambda c: (0, 0))],
        out_shape=[_sds((T, qk_w), BF16), _sds((T, qk_w), BF16), _sds((T, v_w), BF16), _sds((T, v_w), BF16),
                   _sds((1, v_w), F32)],
        scratch_shapes=[pltpu.VMEM((RET_HEADS, RET_DK, RET_DV), F32)],
    )(dyr, ret, u, qr, kr, u, states, *consts, ret_gn)


def _shift_down(x, s):
    rows = lax.broadcasted_iota(jnp.int32, x.shape, 0)
    return jnp.where(rows >= s, pltpu.roll(x, s, 0), 0.0)


def _shift_up(x, s):
    n = x.shape[0]
    rows = lax.broadcasted_iota(jnp.int32, x.shape, 0)
    return jnp.where(rows < n - s, pltpu.roll(x, n - s, 0), 0.0)


def _lru_specs(T):
    col = lambda off: _spec((T, LRU_BLOCK), lambda g: (0, off + g))
    vec = _spec((1, LRU_BLOCK), lambda g: (0, g))
    wblk = _spec((None, LRU_BLOCK, LRU_BLOCK), lambda g: (g, 0, 0))
    cw = _spec((CONV_TAPS, LRU_BLOCK), lambda g: (0, g))
    return col, vec, wblk, cw


def _lru_gates_fwd(u, conv_w, conv_b, w_r, b_r, w_i, b_i, lam):
    T = u.shape[0]
    col, vec, wblk, cw = _lru_specs(T)

    def body(x_ref, cw_ref, cb_ref, wr_ref, br_ref, wi_ref, bi_ref, lam_ref,
             xc_ref, r_ref, i_ref, a_ref, bx_ref):
        x = x_ref[...]
        w = cw_ref[...]
        xc = (_shift_down(x, 3) * w[0:1] + _shift_down(x, 2) * w[1:2] + _shift_down(x, 1) * w[2:3]
              + x * w[3:4] + cb_ref[...])
        r = _sigmoid(_dot(xc, wr_ref[...], "nn") + br_ref[...])
        i = _sigmoid(_dot(xc, wi_ref[...], "nn") + bi_ref[...])
        la = (-LRU_C) * r * _softplus(-lam_ref[...])
        xc_ref[...] = xc
        r_ref[...] = r
        i_ref[...] = i
        a_ref[...] = jnp.exp(la)
        bx_ref[...] = jnp.sqrt(-_expm1(2.0 * la)) * (i * xc)

    out = col(0)
    return _pcall(
        body, name="lru_gates_fwd", grid=(LRU_BLOCKS,),
        in_specs=[col(24), cw, vec, wblk, vec, wblk, vec, vec],
        out_specs=[out] * 5,
        out_shape=[_sds((T, D), F32)] * 5,
    )(u, conv_w, conv_b, w_r, b_r, w_i, b_i, lam)


def _lru_scan(name, a3, b3, reverse):
    T = a3.shape[0]
    nt = T // SCAN_TILE
    unroll = 8

    def body(a_ref, b_ref, o_ref, carry):
        @pl.when(pl.program_id(0) == 0)
        def _():
            carry[...] = jnp.zeros_like(carry)

        if not reverse:
            def step(t, h):
                h = a_ref[t] * h + b_ref[t]
                o_ref[t] = h
                return h
        else:
            def step(k, c):
                t = SCAN_TILE - 1 - k
                l = b_ref[t] + c
                o_ref[t] = l
                return a_ref[t] * l
        carry[...] = lax.fori_loop(0, SCAN_TILE, step, carry[...], unroll=unroll)

    idx = (lambda i: (nt - 1 - i, 0, 0)) if reverse else (lambda i: (i, 0, 0))
    blk = _spec((SCAN_TILE, LRU_BLOCKS, LRU_BLOCK), idx)
    return _pcall(
        body, name=name, grid=(nt,),
        in_specs=[blk, blk], out_specs=blk,
        out_shape=_sds((T, LRU_BLOCKS, LRU_BLOCK), F32),
        scratch_shapes=[pltpu.VMEM((LRU_BLOCKS, LRU_BLOCK), F32)],
    )(a3, b3)


def _lru_gates_bwd(lmb, hl, a, r, i, xc, u, conv_w, w_r, w_i, lam):
    T = u.shape[0]
    col, vec, wblk, cw = _lru_specs(T)

    def body(l_ref, h_ref, a_ref, r_ref, i_ref, xc_ref, x_ref, cw_ref, wr_ref, wi_ref, lam_ref,
             dx_ref, dwr_ref, dwi_ref, dvec_ref, dcw_ref):
        l = l_ref[...]
        av, rv, iv, xc = a_ref[...], r_ref[...], i_ref[...], xc_ref[...]
        lam_v = lam_ref[...]
        sp = _softplus(-lam_v)
        la = (-LRU_C) * rv * sp
        mult = jnp.sqrt(-_expm1(2.0 * la))
        da = l * _shift_down(h_ref[...], 1)
        dmult = l * (iv * xc)
        di = l * mult * xc
        dxc = l * mult * iv
        dla = da * av - dmult * (av * av) / mult
        dzr = (dla * ((-LRU_C) * sp)) * rv * (1.0 - rv)
        dzi = di * iv * (1.0 - iv)
        dsp = jnp.sum(dla * ((-LRU_C) * rv), axis=0, keepdims=True)
        dlam = dsp * (-_sigmoid(-lam_v))
        dwr_ref[...] = _dot(xc, dzr, "tn")
        dwi_ref[...] = _dot(xc, dzi, "tn")
        dxc = dxc + _dot(dzr, wr_ref[...], "nt") + _dot(dzi, wi_ref[...], "nt")
        x = x_ref[...]
        w = cw_ref[...]
        dx = (dxc * w[3:4] + _shift_up(dxc, 1) * w[2:3] + _shift_up(dxc, 2) * w[1:2]
              + _shift_up(dxc, 3) * w[0:1])
        dx_ref[...] = dx.astype(BF16)
        dvec_ref[...] = jnp.concatenate(
            [jnp.sum(dzr, axis=0, keepdims=True), jnp.sum(dzi, axis=0, keepdims=True), dlam,
             jnp.sum(dxc, axis=0, keepdims=True)], axis=0)
        dcw_ref[...] = jnp.concatenate(
            [jnp.sum(dxc * _shift_down(x, 3 - tap), axis=0, keepdims=True) if tap < 3
             else jnp.sum(dxc * x, axis=0, keepdims=True) for tap in range(CONV_TAPS)], axis=0)

    c0 = col(0)
    return _pcall(
        body, name="lru_gates_bwd", grid=(LRU_BLOCKS,),
        in_specs=[c0, c0, c0, c0, c0, c0, col(24), cw, wblk, wblk, vec],
        out_specs=[c0, wblk, wblk, cw, cw],
        out_shape=[_sds((T, D), BF16), _sds((LRU_BLOCKS, LRU_BLOCK, LRU_BLOCK), F32),
                   _sds((LRU_BLOCKS, LRU_BLOCK, LRU_BLOCK), F32), _sds((4, D), F32), _sds((CONV_TAPS, D), F32)],
    )(lmb, hl, a, r, i, xc, u, conv_w, w_r, w_i, lam)


def _xattn_probs(q, k):
    sc = _dot(q, k, "nt") * (X_HD ** -0.5)
    e = jnp.exp(sc - jnp.max(sc, axis=-1, keepdims=True))
    return e / jnp.sum(e, axis=-1, keepdims=True)


def _xattn_fwd(xq, xk, xv):
    T = xq.shape[0]
    tq = min(WIDE_ROW_TILE, T)
    M = xk.shape[0]

    def body(q_ref, k_ref, v_ref, o_ref):
        p = _xattn_probs(q_ref[...], k_ref[...])
        o_ref[...] = _dot(p, v_ref[...], "nn").astype(BF16)

    qs = _spec((tq, X_HD), lambda h, i: (i, h))
    kv = _spec((M, X_HD), lambda h, i: (0, h))
    return _pcall(
        body, name="xattn_fwd", grid=(X_HEADS, T // tq),
        in_specs=[qs, kv, kv], out_specs=qs, out_shape=_sds((T, D), BF16),
    )(xq, xk, xv)


def _xattn_bwd(xq, xk, xv, dxo):
    T = xq.shape[0]
    tq = min(WIDE_ROW_TILE, T)
    M = xk.shape[0]

    def body(q_ref, k_ref, v_ref, do_ref, dq_ref, dk_ref, dv_ref):
        first = pl.program_id(1) == 0
        q, k, v, do = q_ref[...], k_ref[...], v_ref[...], do_ref[...]
        p = _xattn_probs(q, k)
        dp = _dot(do, v, "nt")
        ds = p * (dp - jnp.sum(dp * p, axis=-1, keepdims=True)) * (X_HD ** -0.5)
        dq_ref[...] = _dot(ds, k, "nn").astype(BF16)
        _accumulate(dk_ref, _dot(ds, q, "tn"), first)
        _accumulate(dv_ref, _dot(p, do, "tn"), first)

    qs = _spec((tq, X_HD), lambda h, i: (i, h))
    kv = _spec((M, X_HD), lambda h, i: (0, h))
    return _pcall(
        body, name="xattn_bwd", grid=(X_HEADS, T // tq),
        in_specs=[qs, kv, kv, qs], out_specs=[qs, kv, kv],
        out_shape=[_sds((T, D), BF16), _sds((M, D), F32), _sds((M, D), F32)],
    )(xq, xk, xv, dxo)


def _adamw(name, w, g, m, v):
    R, C = w.shape
    tr = R
    for cand in (512, 352, 256):
        if R % cand == 0:
            tr = cand
            break

    def fn(irefs, orefs, ids):
        delta, mn, vn = _adamw_update(*(r[...] for r in irefs))
        orefs[0][...] = delta
        orefs[1][...] = mn
        orefs[2][...] = vn

    blk = _spec((tr, C), lambda i: (i, 0))
    return _rowwise(name, fn, [(w, blk), (g, blk), (m, blk), (v, blk)],
                    [(_sds((R, C), F32), blk)] * 3, (R // tr,))


def _adamw_update(wv, gv, mv, vv):
    c1 = 1.0 - ADAM_B1 ** ADAM_STEP
    c2 = 1.0 - ADAM_B2 ** ADAM_STEP
    mn = ADAM_B1 * mv + (1.0 - ADAM_B1) * gv
    vn = ADAM_B2 * vv + (1.0 - ADAM_B2) * (gv * gv)
    delta = -ADAM_LR * ((mn / c1) / (jnp.sqrt(vn / c2) + ADAM_EPS) + ADAM_WD * wv)
    return delta, mn, vn


def _adamw_halves(name, w, mine, theirs, widx, m, v, core):
    R, C = w.shape
    H = R // 2
    tr = H
    while tr * C * 4 > (1 << 20) and tr % 16 == 0:
        tr //= 2
    nb = H // tr

    def body(core_ref, w_ref, mine_ref, theirs_ref, m_ref, v_ref, g_out, d_out, m_out, v_out):
        gv = jnp.where(pl.program_id(0) == core_ref[0], mine_ref[...], theirs_ref[...])
        delta, mn, vn = _adamw_update(w_ref[...], gv, m_ref[...], v_ref[...])
        g_out[...] = gv
        d_out[...] = delta
        m_out[...] = mn
        v_out[...] = vn

    full = pl.BlockSpec((tr, C), lambda h, i, core_ref: (h * nb + i, 0))
    mine_spec = pl.BlockSpec((None, tr, C), lambda h, i, core_ref: (widx, jnp.where(h == core_ref[0], i, 0), 0))
    theirs_spec = pl.BlockSpec((None, tr, C), lambda h, i, core_ref: (widx, jnp.where(h == core_ref[0], 0, i), 0))
    return _pcall(
        body, name=name, grid=(2, nb), num_prefetch=1,
        in_specs=[full, mine_spec, theirs_spec, full, full], out_specs=[full] * 4,
        out_shape=[_sds((R, C), F32)] * 4,
    )(core, w, mine, theirs, m, v)


def _rmsnorm(name, x, g):
    M = x.shape[0]
    tm = min(ROW_TILE, M)

    def fn(irefs, orefs, ids):
        orefs[0][...] = _rms_fwd(irefs[0][...], irefs[1][...]).astype(BF16)

    row = _spec((tm, D), lambda i: (i, 0))
    return _rowwise(name, fn, [(x, row), (g, _spec((1, D), lambda i: (0, 0)))],
                    [(_sds((M, D), BF16), row)], (M // tm,))[0]


WEIGHT_AT = {
    "ffn1_w1": ("col1", 0), "ffn1_w3": ("col1", 1), "ffn1_w2": ("row2a", 0),
    "w_ret_o": ("sqA", 0), "w_lru_o": ("sqA", 1), "w_out": ("sqA", 2),
    "w_xq": ("sqB", 0), "w_xk": ("sqB", 1), "w_xv": ("sqC", 0), "w_xo": ("sqC", 1),
    "ffn2_w1": ("col2a", 0), "ffn2_w3": ("col2b", 0), "ffn2_w2": ("row2b", 0),
}


def _local_step(x, mem, tgt, gw, sm, big):
    T = x.shape[0]
    tm = ROW_TILE

    def wt(name):
        key, idx = WEIGHT_AT[name]
        return gw[key], idx

    row3 = lambda i, j, r: (i, 0)
    vec3 = lambda i, j, r: (0, 0)
    rowD = _spec((tm, D), row3)
    vecD = _spec((1, D), vec3)

    def residual_norm(acc, erefs, orefs, ids):
        xo = erefs[0][...] + acc
        orefs[0][...] = xo
        orefs[1][...] = _rms_fwd(xo, erefs[1][...]).astype(BF16)

    def res_norm_io(x_res, g):
        return ([(x_res, rowD), (g, vecD)],
                [(_sds((T, D), F32), rowD), (_sds((T, D), BF16), rowD)])

    a1, b1, s1, h1 = _ffn_up("ffn1_up", x, *wt("ffn1_w1"), *wt("ffn1_w3"), norm_gain=sm["ffn1_norm"])
    x1, h2 = _ffn_down("ffn1_down", s1, *wt("ffn1_w2"), x, sm["mix_norm"])

    tw = min(WIDE_ROW_TILE, T)
    wideD = _spec((tw, D), row3)
    u = _gemm(
        "mix_in",
        [(h2, wideD, gw["win"], _spec((None, None, IN_BLK, D), lambda i, j, r: (j, 0, 0, 0)), "nt")],
        (T // tw, N_CHIPS, 1),
        [(_sds((T, 5120), F32), _spec((tw, IN_BLK), lambda i, j, r: (i, j)))], (tw, IN_BLK))[0]

    consts = _retention_constants(T)
    qr, kr, ret, yr, states = _ret_fwd(u, consts, sm["ret_gn"])

    conv_w = gw["conv"][:, 0].transpose(1, 0, 2).reshape(CONV_TAPS, D)
    xc, rg, ig, av, bx = _lru_gates_fwd(u, conv_w, sm["conv_b"], sm["w_rgate"], sm["b_rgate"],
                                        sm["w_igate"], sm["b_igate"], sm["lru_lambda"])
    a3 = av.reshape(T, LRU_BLOCKS, LRU_BLOCK)
    b3 = bx.reshape(T, LRU_BLOCKS, LRU_BLOCK)

    def gate_epilogue(acc, erefs, orefs, ids):
        orefs[0][...] = _sigmoid(acc + erefs[0][...])

    gates = _gemm(
        "mix_gates",
        [(h2, wideD, gw["wbg"], _spec((None, None, BG_BLK, D), lambda i, j, r: (j, 0, 0, 0)), "nt")],
        (T // tw, N_CHIPS, 1),
        [(_sds((T, 2 * D), F32), _spec((tw, BG_BLK), lambda i, j, r: (i, j)))], (tw, BG_BLK),
        [(sm["b_branch_gate"], _spec((1, BG_BLK), lambda i, j, r: (0, j)))], gate_epilogue)[0]

    hl = _lru_scan("lru_scan_fwd", a3, b3, False).reshape(T, D)

    row1 = _spec((tm, D), lambda i: (i, 0))
    glru1 = _spec((tm, D), lambda i: (i, 4))

    def lru_out(irefs, orefs, ids):
        gl, _ = _gelu_and_grad(irefs[1][...])
        orefs[0][...] = (irefs[0][...] * gl).astype(BF16)

    yl = _rowwise("lru_out", lru_out, [(hl, row1), (u, glru1)], [(_sds((T, D), BF16), row1)], (T // tm,))[0]

    y_ret = _proj_sq("y_ret", yr, *wt("w_ret_o"), "nn")[0]

    def merge_epilogue(acc, erefs, orefs, ids):
        orefs[0][...] = acc
        orefs[1][...] = (erefs[0][...] * erefs[2][...] + erefs[1][...] * acc).astype(BF16)

    y_lru, merged = _proj_sq(
        "y_lru", yl, *wt("w_lru_o"), "nn",
        extras=[(gates, _spec((tm, D), lambda i, j, r: (i, 0))), (gates, _spec((tm, D), lambda i, j, r: (i, 1))),
                (y_ret, rowD)],
        epilogue=merge_epilogue,
        outs=[(_sds((T, D), F32), rowD), (_sds((T, D), BF16), rowD)])

    ex, ou = res_norm_io(x1, sm["xattn_norm"])
    x2, hq = _proj_sq("mix_out", merged, *wt("w_out"), "nn", extras=ex, epilogue=residual_norm, outs=ou)

    m = _rmsnorm("mem_norm", mem, sm["mem_norm"])
    xq = _proj_sq("xq", hq, *wt("w_xq"), "nn", BF16)[0]
    xk = _proj_sq("xk", m, *wt("w_xk"), "nn", BF16)[0]
    xv = _proj_sq("xv", m, *wt("w_xv"), "nn", BF16)[0]
    xo = _xattn_fwd(xq, xk, xv)
    ex, ou = res_norm_io(x2, sm["ffn2_norm"])
    x3, h3 = _proj_sq("xattn_out", xo, *wt("w_xo"), "nn", extras=ex, epilogue=residual_norm, outs=ou)

    a2, b2, s2 = _ffn_up("ffn2_up", h3, *wt("ffn2_w1"), *wt("ffn2_w3"))
    loss, dx4, dg_final = _ffn_down("ffn2_down", s2, *wt("ffn2_w2"), x3, sm["final_norm"], loss_target=tgt)

    dx3, dg_ffn2 = _ffn_bwd("ffn2", dx4, h3, a2, b2, s2, *wt("ffn2_w1"), *wt("ffn2_w3"),
                            *wt("ffn2_w2"), x3, sm["ffn2_norm"], big)

    dxo = _proj_sq("d_xo", dx3, *wt("w_xo"), "nt", BF16)[0]
    big["w_xo"] = _dw_sq("dw_xo", xo, dx3)[None]
    dxq, dxk, dxv = _xattn_bwd(xq, xk, xv, dxo)
    big["w_xq"] = _dw_sq("dw_xq", hq, dxq)[None]
    ex, ou = _rms_bwd_io(x2, sm["xattn_norm"], dx3, T, tm)
    dx2, dg_xattn = _proj_sq("d_hq", dxq, *wt("w_xq"), "nt", extras=ex, epilogue=_rms_bwd_epilogue, outs=ou)
    big["w_xk"] = _dw_sq("dw_xk", m, dxk)[None]
    big["w_xv"] = _dw_sq("dw_xv", m, dxv)[None]

    M = mem.shape[0]

    def mem_norm_epilogue(acc, erefs, orefs, ids):
        _, dgp = _rms_bwd(erefs[0][...], erefs[1][...], acc)
        orefs[0][...] = dgp

    wsq_spec = lambda idx: _spec((N_CHIPS, None, SQ_BLK, D), lambda i, j, r: (0, idx, 0, 0))
    memD = _spec((M, D), row3)
    dg_mem = _gemm(
        "d_mem_norm",
        [(dxk, memD, wt("w_xk")[0], wsq_spec(wt("w_xk")[1]), "nt"),
         (dxv, memD, wt("w_xv")[0], wsq_spec(wt("w_xv")[1]), "nt")],
        (1, 1, 1), [(_sds((1, D), F32), vecD)], (M, D),
        [(mem, memD), (sm["mem_norm"], vecD)], mem_norm_epilogue)[0]

    def merged_bwd_epilogue(acc, erefs, orefs, ids):
        gr, gl, yrv, ylv = (e[...] for e in erefs)
        orefs[0][...] = (acc * gr).astype(BF16)
        orefs[1][...] = (acc * gl).astype(BF16)
        dgr = acc * yrv * gr * (1.0 - gr)
        dgl = acc * ylv * gl * (1.0 - gl)
        orefs[2][:, :D] = dgr.astype(BF16)
        orefs[2][:, D:] = dgl.astype(BF16)
        dbb = jnp.concatenate([jnp.sum(dgr, axis=0, keepdims=True), jnp.sum(dgl, axis=0, keepdims=True)], axis=1)
        _accumulate(orefs[3], dbb, ids[0] == 0)

    dy_ret, dy_lru, dgpre, db_bg = _proj_sq(
        "d_merged", dx2, *wt("w_out"), "nt",
        extras=[(gates, _spec((tm, D), lambda i, j, r: (i, 0))), (gates, _spec((tm, D), lambda i, j, r: (i, 1))),
                (y_ret, rowD), (y_lru, rowD)],
        epilogue=merged_bwd_epilogue,
        outs=[(_sds((T, D), BF16), rowD), (_sds((T, D), BF16), rowD),
              (_sds((T, 2 * D), BF16), _spec((tm, 2 * D), row3)),
              (_sds((1, 2 * D), F32), _spec((1, 2 * D), vec3))])
    big["w_branch_gate"] = _gemm(
        "dw_bg",
        [(h2, _spec((T, D), lambda j, n, r: (r, 0)), dgpre, _spec((T, BG_BLK), lambda j, n, r: (r, j)), "tn")],
        (N_CHIPS, 1, 1),
        [(_sds((N_CHIPS, D, BG_BLK), GRAD_WIRE_DTYPE), _spec((None, D, BG_BLK), lambda j, n, r: (j, 0, 0)))],
        (D, BG_BLK))[0][None]
    big["w_out"] = _dw_sq("dw_out", merged, dx2)[None]
    dyr = _proj_sq("d_yr", dy_ret, *wt("w_ret_o"), "nt")[0]
    big["w_ret_o"] = _dw_sq("dw_ret_o", yr, dy_ret)[None]
    dyl = _proj_sq("d_yl", dy_lru, *wt("w_lru_o"), "nt")[0]
    big["w_lru_o"] = _dw_sq("dw_lru_o", yl, dy_lru)[None]

    def lru_out_bwd(irefs, orefs, ids):
        gl, dgl = _gelu_and_grad(irefs[2][...])
        dyl_v = irefs[0][...]
        orefs[0][...] = dyl_v * gl
        orefs[1][...] = (dyl_v * irefs[1][...] * dgl).astype(BF16)

    dhl, dglru = _rowwise("lru_out_bwd", lru_out_bwd, [(dyl, row1), (hl, row1), (u, glru1)],
                          [(_sds((T, D), F32), row1), (_sds((T, D), BF16), row1)], (T // tm,))
    dhl3 = dhl.reshape(T, LRU_BLOCKS, LRU_BLOCK)
    dq, dk, dv, dgr, dg_retgn = _ret_bwd(dyr, ret, u, qr, kr, states, consts, sm["ret_gn"])
    lmb = _lru_scan("lru_scan_bwd", a3, dhl3, True).reshape(T, D)
    dxl, dw_r, dw_i, dvec, dcw = _lru_gates_bwd(lmb, hl, av, rg, ig, xc, u, conv_w,
                                                sm["w_rgate"], sm["w_igate"], sm["lru_lambda"])

    du = jnp.concatenate([dq, dk, dv, dgr, dxl, dglru], axis=1)
    tk = T
    big["w_in"] = _gemm(
        "dw_in",
        [(h2, _spec((tk, D), lambda j, n, r: (r, 0)), du, _spec((tk, IN_BLK), lambda j, n, r: (r, j)), "tn")],
        (N_CHIPS, 1, T // tk),
        [(_sds((N_CHIPS, D, IN_BLK), GRAD_WIRE_DTYPE), _spec((None, D, IN_BLK), lambda j, n, r: (j, 0, 0)))],
        (D, IN_BLK))[0][None]
    tf = min(FFN_ROW_TILE, T)
    ex, ou = _rms_bwd_io(x1, sm["mix_norm"], dx2, T, tf)
    dx1, dg_mix = _gemm(
        "d_h2",
        [(du, _spec((tf, 5120), row3), gw["win"], _spec((N_CHIPS, None, IN_BLK, D), lambda i, j, r: (0, 0, 0, 0)), "nn"),
         (dgpre, _spec((tf, 2 * D), row3), gw["wbg"], _spec((N_CHIPS, None, BG_BLK, D), lambda i, j, r: (0, 0, 0, 0)),
          "nn")],
        (T // tf, 1, 1), ou, (tf, D), ex, _rms_bwd_epilogue)

    grad_x, dg_ffn1 = _ffn_bwd("ffn1", dx1, h1, a1, b1, s1, *wt("ffn1_w1"), *wt("ffn1_w3"),
                               *wt("ffn1_w2"), x, sm["ffn1_norm"], big)

    small = {
        "ffn1_norm": dg_ffn1, "mix_norm": dg_mix, "ret_gn": dg_retgn, "conv_b": dvec[3:4],
        "b_rgate": dvec[0:1], "b_igate": dvec[1:2], "lru_lambda": dvec[2:3], "xattn_norm": dg_xattn,
        "mem_norm": dg_mem, "ffn2_norm": dg_ffn2, "final_norm": dg_final, "b_branch_gate": db_bg,
        "conv_w": dcw, "w_rgate": dw_r, "w_igate": dw_i,
    }
    return loss, grad_x, small


ANY_SPEC = pl.BlockSpec(memory_space=pl.ANY)
VMEM_SPEC = pl.BlockSpec(memory_space=pltpu.VMEM)
N_PEER_CHIPS = N_CHIPS - 1


def _mesh_position():
    x, y, c = lax.axis_index("x"), lax.axis_index("y"), lax.axis_index("c")
    chips = [(1 - x, y), (x, 1 - y), (1 - x, 1 - y)]
    return x, y, c, chips


def _chip_index(x, y):
    return 2 * x + y


def _rows_half(ref, axis, h):
    n = ref.shape[axis] // 2
    idx = [slice(None)] * len(ref.shape)
    idx[axis] = pl.ds(pl.multiple_of(h * n, BF16_TILE_ROWS), n)
    return ref.at[tuple(idx)]


def _remote(src, dst, send_sem, recv_sem, device):
    return pltpu.make_async_remote_copy(src_ref=src, dst_ref=dst, send_sem=send_sem, recv_sem=recv_sem,
                                        device_id=device, device_id_type=MESH)


def _gather_chips_task(shards, split, landed, legs="both"):
    keys = list(shards)
    n = len(keys)

    def operands():
        if legs == "pass_on":
            return [landed[k] for k in keys]
        chip_me = _chip_index(lax.axis_index("x"), lax.axis_index("y"))
        return [lax.dynamic_update_slice(lax.empty((N_CHIPS,) + shards[k].shape, shards[k].dtype), shards[k][None],
                                         (chip_me,) + (0,) * shards[k].ndim) for k in keys]

    def my_rows(ref, c):
        return _rows_half(ref, 1, c)

    def make_direct(ins, outs, send_sem, recv_sem):
        x, y, c, chips = _mesh_position()
        s_me = _chip_index(x, y)
        starts, arrivals = [], []
        for g in range(n):
            for k, chip in enumerate(chips):
                sems = (send_sem(3 * g + k), recv_sem(3 * g + k))
                starts.append(functools.partial(_remote, outs[g].at[s_me], outs[g].at[s_me], *sems, (*chip, c)))
                got = outs[g].at[_chip_index(*chip)]
                arrivals.append(functools.partial(_remote, got, got, *sems, (*chip, c)))
        return starts, arrivals

    def make_swap(ins, outs, send_sem, recv_sem):
        x, y, c, _ = _mesh_position()
        first, _ = _axis_neighbours(x, y, c)
        starts, arrivals = [], []
        for g in range(n):
            sems = (send_sem(3 * g), recv_sem(3 * g))
            mine = my_rows(outs[g].at[_chip_index(x, y)], c)
            starts.append(functools.partial(_remote, mine, mine, *sems, (*first, c)))
            got = my_rows(outs[g].at[_chip_index(*first)], c)
            arrivals.append(functools.partial(_remote, got, got, *sems, (*first, c)))
        return starts, arrivals

    def make_pass_on(ins, outs, send_sem, recv_sem):
        x, y, c, _ = _mesh_position()
        first, second = _axis_neighbours(x, y, c)
        diagonal = (1 - x, 1 - y)
        starts, arrivals = [], []
        for g in range(n):
            half = lambda chip: my_rows(outs[g].at[_chip_index(*chip)], c)
            for k, (sent, arriving) in enumerate([((x, y), second), (first, diagonal)]):
                sems = (send_sem(3 * g + 1 + k), recv_sem(3 * g + 1 + k))
                starts.append(functools.partial(_remote, half(sent), half(sent), *sems, (*second, c)))
                arrivals.append(functools.partial(_remote, half(arriving), half(arriving), *sems, (*second, c)))
        return starts, arrivals

    def finish(res):
        landed.update(zip(keys, res))

    shapes = lambda: [_sds((N_CHIPS,) + shards[k].shape, shards[k].dtype) for k in keys]
    aliases = {g: g for g in range(n)}
    if not split:
        return _Task("chips", operands, shapes, aliases, 3 * n, make_direct, finish)
    if legs == "swap":
        return _Task("first", operands, shapes, aliases, 3 * n, make_swap, finish)
    if legs == "pass_on":
        return _Task("second", operands, shapes, aliases, 3 * n, make_pass_on, finish)
    return _Task("first+second", operands, shapes, aliases, 3 * n, make_swap, finish, make_second=make_pass_on)


def _gather_sibling_task(keys, landed, ready):
    n = len(keys)

    def make(ins, outs, send_sem, recv_sem):
        x, y, c, chips = _mesh_position()
        starts, arrivals = [], []
        for g in range(n):
            for k, chip in enumerate(chips):
                o = outs[g].at[_chip_index(*chip)]
                got, other = _rows_half(o, 1, c), _rows_half(o, 1, 1 - c)
                starts.append(functools.partial(_remote, got, got, send_sem(3 * g + k), recv_sem(3 * g + k),
                                                (x, y, 1 - c)))
                arrivals.append(functools.partial(_remote, other, other, send_sem(3 * g + k), recv_sem(3 * g + k),
                                                  (x, y, 1 - c)))
        return starts, arrivals

    def finish(res):
        ready.update(zip(keys, res))

    return _Task("sibling", lambda: [landed[k] for k in keys],
                 lambda: [_sds(landed[k].shape, landed[k].dtype) for k in keys],
                 {g: g for g in range(n)}, 3 * n, make, finish)


def _pair_swap_task(names, big, got):
    n = len(names)

    def make(ins, outs, send_sem, recv_sem):
        x, y, c, _ = _mesh_position()
        copies = [functools.partial(_remote, _rows_half(ins[a], 2, 1 - c), outs[a], send_sem(a), recv_sem(a),
                                    (x, y, 1 - c)) for a in range(n)]
        return copies, copies

    def shapes():
        return [_sds(big[k].shape[:2] + (big[k].shape[2] // 2, big[k].shape[3]), big[k].dtype) for k in names]

    return _Task("sibling", lambda: [big[k] for k in names], shapes, {}, n, make,
                 lambda res: got.update(zip(names, res)))


def _rs_pair_sum(name, fulls, gots, core):
    n = len(fulls)
    shapes = [(f.shape[2] // 2, f.shape[3]) for f in fulls]

    def body(core_ref, *refs):
        for a_ref, b_ref, o_ref in zip(refs[:n], refs[n:2 * n], refs[2 * n:]):
            o_ref[...] = (a_ref[...].astype(F32) + b_ref[...].astype(F32)).astype(BF16)

    mine = [pl.BlockSpec((None, None) + hc, lambda s, core_ref: (0, s, core_ref[0], 0)) for hc in shapes]
    slot = [pl.BlockSpec((None, None) + hc, lambda s, core_ref: (0, s, 0, 0)) for hc in shapes]
    return _pcall(
        body, name=name, grid=(N_CHIPS,), num_prefetch=1,
        in_specs=mine + slot, out_specs=slot,
        out_shape=[_sds((1, N_CHIPS) + hc, BF16) for hc in shapes],
    )(core, *fulls, *gots)


def _chip_exchange_task(names, pair_sums, by_source, part=0, nparts=1):
    n = len(names)

    def rows(ref):
        h = ref.shape[1] // nparts
        return ref.at[:, pl.ds(part * h, h), :]

    def make(ins, outs, send_sem, recv_sem):
        x, y, c, chips = _mesh_position()
        s_me = _chip_index(x, y)
        starts, arrivals = [], []
        for a in range(n):
            for k, chip in enumerate(chips):
                s_k = _chip_index(*chip)
                starts.append(functools.partial(_remote, rows(ins[a].at[:, s_k]), rows(outs[a].at[:, s_me]),
                                                send_sem(3 * a + k), recv_sem(3 * a + k), (*chip, c)))
                got = rows(outs[a].at[:, s_k])
                arrivals.append(functools.partial(_remote, got, got, send_sem(3 * a + k), recv_sem(3 * a + k),
                                                  (*chip, c)))
        return starts, arrivals

    def operands():
        return [pair_sums[k] for k in names] + ([by_source[k] for k in names] if part else [])

    return _Task("chips", operands, lambda: [_sds(pair_sums[k].shape, pair_sums[k].dtype) for k in names],
                 {n + a: a for a in range(n)} if part else {}, 3 * n, make,
                 lambda res: by_source.update(zip(names, res)))


def _rs_chip_sum(name, owns, parts, chip):
    n = len(owns)
    ns = N_CHIPS
    shapes = [p.shape[2:] for p in parts]

    def body(chip_ref, *refs):
        me = chip_ref[0]
        for i in range(n):
            own_v = refs[i][...].astype(F32)
            slots = refs[n + ns * i:n + ns * (i + 1)]
            tot = None
            for s in range(ns):
                term = jnp.where(me == s, own_v, slots[s][...].astype(F32))
                tot = term if tot is None else tot + term
            refs[n + ns * n + i][...] = tot

    def slot_spec(hc, s):
        return pl.BlockSpec((None, None) + hc,
                            lambda g, chip_ref: (0, jnp.where(chip_ref[0] == s, (s + 1) % ns, s), 0, 0))

    own_specs = [pl.BlockSpec((None, None) + hc, lambda g, chip_ref: (0, chip_ref[0], 0, 0)) for hc in shapes]
    slot_specs = [slot_spec(hc, s) for hc in shapes for s in range(ns)]
    return _pcall(
        body, name=name, grid=(1,), num_prefetch=1,
        in_specs=own_specs + slot_specs,
        out_specs=[pl.BlockSpec((None,) + hc, lambda g, chip_ref: (0, 0, 0)) for hc in shapes],
        out_shape=[_sds((1,) + hc, F32) for hc in shapes],
    )(chip, *owns, *[p for p in parts for _ in range(ns)])


def _pair_gather_task(names, halves, sibling_halves):
    n = len(names)

    def make(ins, outs, send_sem, recv_sem):
        x, y, c, _ = _mesh_position()
        copies = [functools.partial(_remote, ins[a], outs[a], send_sem(a), recv_sem(a), (x, y, 1 - c))
                  for a in range(n)]
        return copies, copies

    return _Task("sibling", lambda: [halves[k] for k in names], lambda: [_sds(halves[k].shape, F32) for k in names],
                 {}, n, make, lambda res: sibling_halves.update(zip(names, res)))


def _small_allreduce(arrs):
    n = len(arrs)
    per = 1 + 2 * N_PEER_CHIPS

    def body(*refs):
        v_refs, o_refs = refs[:n], refs[n:2 * n]
        sib, pair, part = refs[2 * n:3 * n], refs[3 * n:4 * n], refs[4 * n:5 * n]
        send_sems, recv_sems = refs[5 * n:]
        x, y, c, chips = _mesh_position()
        s_me = _chip_index(x, y)

        def quarter(ref, s):
            q = ref.shape[0] // N_CHIPS
            return ref.at[pl.ds(pl.multiple_of(s * q, F32_TILE_ROWS), q)]

        def exchange(first_sem, src, dst_of, arrival_of):
            sems = lambda a, k: (send_sems.at[a * per + first_sem + k], recv_sems.at[a * per + first_sem + k])
            sends = [_remote(src(a, _chip_index(*chip)), dst_of(a, s_me), *sems(a, k), (*chip, c))
                     for a in range(n) for k, chip in enumerate(chips)]
            for cp in sends:
                cp.start()
            for a in range(n):
                for k, chip in enumerate(chips):
                    got = arrival_of(a, _chip_index(*chip))
                    _remote(got, got, *sems(a, k), (*chip, c)).wait_recv()
            for cp in sends:
                cp.wait_send()

        swaps = [_remote(v_refs[a], sib[a], send_sems.at[a * per], recv_sems.at[a * per], (x, y, 1 - c))
                 for a in range(n)]
        for cp in swaps:
            cp.start()
        for cp in swaps:
            cp.wait()
        for a in range(n):
            pair[a][...] = v_refs[a][...] + sib[a][...]
        exchange(1, lambda a, s_k: quarter(pair[a], s_k), lambda a, s: part[a].at[s], lambda a, s_k: part[a].at[s_k])
        for a in range(n):
            part[a][s_me] = quarter(pair[a], s_me)[...]
            q = o_refs[a].shape[0] // N_CHIPS
            o_refs[a][pl.ds(pl.multiple_of(s_me * q, F32_TILE_ROWS), q), :] = (
                ((part[a][0] + part[a][1]) + part[a][2]) + part[a][3])
        exchange(1 + N_PEER_CHIPS, lambda a, s_k: quarter(o_refs[a], s_me), lambda a, s: quarter(o_refs[a], s),
                 lambda a, s_k: quarter(o_refs[a], s_k))

    shapes = [a.shape for a in arrs]
    return _pcall(
        body, name="small_allreduce", grid=(1,), own_peers=("sibling", "chips"),
        in_specs=[VMEM_SPEC] * n, out_specs=[VMEM_SPEC] * n, out_shape=[_sds(s, F32) for s in shapes],
        scratch_shapes=([pltpu.VMEM(s, F32) for s in shapes] * 2
                        + [pltpu.VMEM((N_CHIPS, s[0] // N_CHIPS, s[1]), F32) for s in shapes]
                        + [pltpu.SemaphoreType.DMA((n * per,)), pltpu.SemaphoreType.DMA((n * per,))]),
    )(*arrs)


TRANSPOSED_WEIGHTS = ("ffn1_w1", "ffn1_w3", "ffn2_w1", "ffn2_w3")
SMALL_LAYOUT = [("ffn1_norm", 1), ("mix_norm", 1), ("ret_gn", 1), ("conv_b", 1), ("b_rgate", 1), ("b_igate", 1),
                ("lru_lambda", 1), ("xattn_norm", 1), ("mem_norm", 1), ("ffn2_norm", 1), ("final_norm", 1),
                ("b_branch_gate", 2), ("conv_w", CONV_TAPS)]
SMALL_ROWS = 32
GATE_WEIGHTS = ("w_rgate", "w_igate")
WEIGHT_ORDER = ["ffn1_norm", "ffn1_w1", "ffn1_w3", "ffn1_w2", "mix_norm", "w_in", "ret_gn", "w_ret_o", "conv_w",
                "conv_b", "w_rgate", "b_rgate", "w_igate", "b_igate", "lru_lambda", "w_lru_o", "w_branch_gate",
                "b_branch_gate", "w_out", "xattn_norm", "mem_norm", "w_xq", "w_xk", "w_xv", "w_xo", "ffn2_norm",
                "ffn2_w1", "ffn2_w3", "ffn2_w2", "final_norm"]


SMALL_USED_ROWS = sum(n for _, n in SMALL_LAYOUT)


def _pack_small(parts, extra_row=None):
    rows = [parts[name].reshape(n, D) for name, n in SMALL_LAYOUT]
    if extra_row is not None:
        rows.append(extra_row)
    rows.append(jnp.zeros((SMALL_ROWS - sum(r.shape[0] for r in rows), D), F32))
    return jnp.concatenate(rows, axis=0)


def _unpack_small(packed, shapes):
    out, r = {}, 0
    for name, n in SMALL_LAYOUT:
        out[name] = packed[r:r + n].reshape(shapes[name])
        r += n
    return out


def kernel(x, mem, ffn1_norm, ffn1_w1, ffn1_w3, ffn1_w2, mix_norm, w_in, ret_gn, w_ret_o, conv_w, conv_b, w_rgate, b_rgate, w_igate, b_igate, lru_lambda, w_lru_o, w_branch_gate, b_branch_gate, w_out, xattn_norm, mem_norm, w_xq, w_xk, w_xv, w_xo, ffn2_norm, ffn2_w1, ffn2_w3, ffn2_w2, final_norm, loss_target, m_ffn1_norm, m_ffn1_w1, m_ffn1_w3, m_ffn1_w2, m_mix_norm, m_w_in, m_ret_gn, m_w_ret_o, m_conv_w, m_conv_b, m_w_rgate, m_b_rgate, m_w_igate, m_b_igate, m_lru_lambda, m_w_lru_o, m_w_branch_gate, m_b_branch_gate, m_w_out, m_xattn_norm, m_mem_norm, m_w_xq, m_w_xk, m_w_xv, m_w_xo, m_ffn2_norm, m_ffn2_w1, m_ffn2_w3, m_ffn2_w2, m_final_norm, v_ffn1_norm, v_ffn1_w1, v_ffn1_w3, v_ffn1_w2, v_mix_norm, v_w_in, v_ret_gn, v_w_ret_o, v_conv_w, v_conv_b, v_w_rgate, v_b_rgate, v_w_igate, v_b_igate, v_lru_lambda, v_w_lru_o, v_w_branch_gate, v_b_branch_gate, v_w_out, v_xattn_norm, v_mem_norm, v_w_xq, v_w_xk, v_w_xv, v_w_xo, v_ffn2_norm, v_ffn2_w1, v_ffn2_w3, v_ffn2_w2, v_final_norm):
    given = dict(locals())
    w = {n: given[n] for n in WEIGHT_ORDER}
    mom = {n: given["m_" + n] for n in WEIGHT_ORDER}
    var = {n: given["v_" + n] for n in WEIGHT_ORDER}
    chip = _chip_index(lax.axis_index("x"), lax.axis_index("y"))
    core = lax.axis_index("c").astype(jnp.int32).reshape(1)

    chip_id = chip.astype(jnp.int32).reshape(1)
    sm = {n: w[n] for n in ["ffn1_norm", "mix_norm", "ret_gn", "conv_b", "b_rgate", "b_igate", "lru_lambda",
                            "xattn_norm", "mem_norm", "ffn2_norm", "b_branch_gate"]}
    sm["final_norm"] = w["final_norm"].reshape(1, D)
    sm["w_rgate"] = w["w_rgate"][0]
    sm["w_igate"] = w["w_igate"][0]

    local = lambda a, n: jnp.swapaxes(a[0], 0, 1) if n in TRANSPOSED_WEIGHTS else a[0]
    stack = lambda names: jnp.stack([local(w[n], n) for n in names], axis=0).astype(BF16)
    shard = {"col1": stack(["ffn1_w1", "ffn1_w3"]), "row2a": stack(["ffn1_w2"]),
             "win": jnp.swapaxes(w["w_in"], 1, 2).astype(BF16),
             "wbg": jnp.swapaxes(w["w_branch_gate"], 1, 2).astype(BF16),
             "sqA": stack(["w_ret_o", "w_lru_o", "w_out"]), "sqB": stack(["w_xq", "w_xk"]),
             "sqC": stack(["w_xv", "w_xo"]), "col2a": stack(["ffn2_w1"]), "col2b": stack(["ffn2_w3"]),
             "row2b": stack(["ffn2_w2"]), "conv": w["conv_w"]}
    gw, landed = {}, {}
    over_chips = lambda keys: _gather_chips_task({k: shard[k] for k in keys}, True, landed)
    to_sibling = lambda keys: _gather_sibling_task(keys, landed, gw)

    big, got, pair_sums, by_source, halves, sibling_halves, outs = {}, {}, {}, {}, {}, {}, {}
    pair_swap = lambda names: _pair_swap_task(names, big, got)
    exchange = lambda names, part=0, nparts=1: _chip_exchange_task(names, pair_sums, by_source, part, nparts)
    pair_gather = lambda names: _pair_gather_task(names, halves, sibling_halves)

    def pair_sum(names):
        res = _rs_pair_sum("rs_pair_sum_" + names[0], [big[n] for n in names], [got[n] for n in names], core)
        pair_sums.update(zip(names, res))

    def chip_sum(names):
        res = _rs_chip_sum("rs_chip_sum_" + names[0], [pair_sums[n] for n in names], [by_source[n] for n in names],
                           chip_id)
        halves.update(zip(names, res))

    def adamw(names):
        for n in names:
            res = _adamw_halves("adamw_" + n, local(w[n], n), halves[n], sibling_halves[n], 0, local(mom[n], n),
                                local(var[n], n), core)
            outs[n] = tuple((jnp.swapaxes(r, 0, 1) if n in TRANSPOSED_WEIGHTS else r)[None] for r in res)

    do = lambda fn, names: functools.partial(fn, names)
    ffn2_grads = ["ffn2_w2", "ffn2_w1", "ffn2_w3"]
    xattn_grads = ["w_xo", "w_xq", "w_xk", "w_xv"]
    mix_out_grads = ["w_branch_gate", "w_out", "w_ret_o", "w_lru_o"]
    conv_gather = _gather_chips_task({"conv": shard["conv"]}, False, gw)
    swap = lambda key: _gather_chips_task({key: shard[key]}, True, landed, legs="swap")
    pass_on = lambda key: _gather_chips_task({key: shard[key]}, True, landed, legs="pass_on")
    plan = _Plan()
    plan.tasks = {
        "ag_first_chips": [over_chips(["col1"]), swap("row2a")],
        "ag_first_sibling": [to_sibling(["col1"]), pass_on("row2a"), swap("win")],
        "ffn1_up": [to_sibling(["row2a"]), pass_on("win"), swap("wbg")],
        "ffn1_down": [to_sibling(["win"]), pass_on("wbg"), swap("sqA")],
        "mix_in": [to_sibling(["wbg"]), pass_on("sqA"), swap("col2a"), conv_gather],
        "ret_fwd": [to_sibling(["sqA"]), pass_on("col2a"), swap("sqB")],
        "lru_gates_fwd": [to_sibling(["col2a"]), pass_on("sqB"), swap("sqC")],
        "mix_gates": [to_sibling(["sqB"]), pass_on("sqC"), swap("col2b")],
        "lru_scan_fwd": [to_sibling(["sqC"]), pass_on("col2b")],
        "y_lru": [to_sibling(["col2b"]), swap("row2b")],
        "ffn2_up": [pass_on("row2b")],
        "ffn2_up_sibling": [to_sibling(["row2b"])],
        "ffn2_dh": [pair_swap(ffn2_grads)],
        "xattn_bwd": [exchange(["ffn2_w2"], 0, 2)],
        "d_hq": [exchange(["ffn2_w2"], 1, 2)],
        "d_merged": [exchange(["ffn2_w1"], 0, 2), pair_swap(xattn_grads)],
        "lru_out_bwd": [exchange(["w_xo"])],
        "ret_bwd": [exchange(["ffn2_w1"], 1, 2), exchange(["ffn2_w3"], 0, 2), pair_swap(mix_out_grads)],
        "lru_scan_bwd": [exchange(["ffn2_w3"], 1, 2)],
        "lru_gates_bwd": [exchange(["w_xq", "w_xk"]), pair_gather(ffn2_grads)],
        "dw_in": [exchange(["w_xv", "w_out"])],
        "d_h2": [exchange(["w_branch_gate", "w_ret_o", "w_lru_o"]), pair_swap(["w_in"]), pair_gather(xattn_grads)],
        "ffn1_bwd_mid": [exchange(["w_in"], 0, 2), pair_gather(mix_out_grads)],
        "ffn1_dw2": [exchange(["w_in"], 2, 4)],
        "ffn1_dw1": [exchange(["w_in"], 3, 4), pair_swap(["ffn1_w2"])],
        "ffn1_dw3": [exchange(["ffn1_w2"], 0, 2), pair_swap(["ffn1_w1"]), pair_gather(["w_in"])],
        "ffn1_dh": [exchange(["ffn1_w2"], 1, 2), exchange(["ffn1_w1"]), pair_swap(["ffn1_w3"])],
        "small_allreduce": [exchange(["ffn1_w3"]), pair_gather(["ffn1_w2"])],
        "adamw_w_rgate": [pair_gather(["ffn1_w1", "ffn1_w3"])],
    }
    plan.after = {
        "ffn2_up": [functools.partial(_comm_call, "ffn2_up_sibling")],
        "ffn2_dh": [do(pair_sum, ffn2_grads)],
        "d_merged": [do(pair_sum, xattn_grads)],
        "ret_bwd": [do(pair_sum, mix_out_grads)],
        "lru_scan_bwd": [do(chip_sum, ffn2_grads)],
        "lru_gates_bwd": [do(adamw, ffn2_grads)],
        "dw_in": [do(chip_sum, xattn_grads)],
        "d_h2": [do(chip_sum, mix_out_grads), do(pair_sum, ["w_in"]), do(adamw, xattn_grads)],
        "ffn1_bwd_mid": [do(adamw, mix_out_grads)],
        "ffn1_dw1": [do(chip_sum, ["w_in"]), do(pair_sum, ["ffn1_w2"])],
        "ffn1_dw3": [do(pair_sum, ["ffn1_w1"]), do(adamw, ["w_in"])],
        "ffn1_dh": [do(pair_sum, ["ffn1_w3"]), do(chip_sum, ["ffn1_w2"])],
        "small_allreduce": [do(chip_sum, ["ffn1_w1", "ffn1_w3"])],
        "adamw_w_rgate": [do(adamw, ["ffn1_w2", "ffn1_w1", "ffn1_w3"])],
    }
    global _plan
    _plan = plan
    try:
        _comm_call("ag_first_chips")
        _comm_call("ag_first_sibling")
        loss_part, grad_x, small = _local_step(x[0], mem[0], loss_target[0], gw, sm, big)
        gate2d = lambda a: a.reshape(LRU_BLOCKS * LRU_BLOCK, LRU_BLOCK)
        loss_row = jnp.pad(loss_part, ((0, 0), (0, D - loss_part.shape[1])))
        small_sum, *gate_sums = _small_allreduce([_pack_small(small, loss_row)]
                                                 + [gate2d(small[n]) for n in GATE_WEIGHTS])
        for n, gsum in zip(GATE_WEIGHTS, gate_sums):
            d, nm, nv = _adamw("adamw_" + n, gate2d(w[n]), gsum, gate2d(mom[n]), gate2d(var[n]))
            outs[n] = tuple(r.reshape(w[n].shape) for r in (gsum, d, nm, nv))
    finally:
        _plan = None
    assert not plan.tasks and not plan.after, (list(plan.tasks), list(plan.after))
    loss = small_sum[SMALL_USED_ROWS, 0]

    small_shapes = {n: w[n].shape for n, _ in SMALL_LAYOUT}
    small_shapes["conv_w"] = (CONV_TAPS, D)
    conv_row = SMALL_USED_ROWS - CONV_TAPS
    conv_grad = lax.dynamic_slice(small_sum[conv_row:conv_row + CONV_TAPS], (0, chip * SQ_BLK), (CONV_TAPS, SQ_BLK))
    small_w = {n: w[n] for n, _ in SMALL_LAYOUT}
    small_m = {n: mom[n] for n, _ in SMALL_LAYOUT}
    small_v = {n: var[n] for n, _ in SMALL_LAYOUT}
    pad_cols = lambda a: jnp.pad(a[0], ((0, 0), (0, D - SQ_BLK)))
    for dct in (small_w, small_m, small_v):
        dct["conv_w"] = pad_cols(dct["conv_w"])
    g_pack = lax.dynamic_update_slice(small_sum, jnp.pad(conv_grad, ((0, 0), (0, D - SQ_BLK))), (conv_row, 0))
    d_pack, m_pack, v_pack = _adamw("adamw_small", _pack_small(small_w), g_pack, _pack_small(small_m),
                                    _pack_small(small_v))
    unpacked = [_unpack_small(p, small_shapes) for p in (g_pack, d_pack, m_pack, v_pack)]
    for n, _ in SMALL_LAYOUT:
        if n == "conv_w":
            outs[n] = tuple(u[n][:, :SQ_BLK][None] for u in unpacked)
        else:
            outs[n] = tuple(u[n] for u in unpacked)

    result = [loss, grad_x[None]]
    for k in range(4):
        result += [outs[n][k] for n in WEIGHT_ORDER]
    return tuple(result)
```

```python
import functools
import math

import jax
import jax.numpy as jnp
from jax import lax
from jax.experimental import pallas as pl
from jax.experimental.pallas import tpu as pltpu

F32 = jnp.float32
BF16 = jnp.bfloat16
GRAD_WIRE_DTYPE = BF16
MESH = pl.DeviceIdType.MESH

D = 1024
EPS = 1e-6
RET_HEADS = 4
RET_DK = 128
RET_DV = 256
CHUNK = 128
ROPE_BASE = 10000.0
LRU_BLOCKS = 8
LRU_BLOCK = 128
CONV_TAPS = 4
LRU_C = 8.0
D_FF = 2816
X_HEADS = 4
X_HD = 256
N_CHIPS = 4
FF_BLK = D_FF // N_CHIPS
IN_BLK = 5120 // N_CHIPS
BG_BLK = 2048 // N_CHIPS
SQ_BLK = D // N_CHIPS

ADAM_LR = 0.001
ADAM_B1 = 0.9
ADAM_B2 = 0.999
ADAM_EPS = 1e-08
ADAM_WD = 0.01
ADAM_STEP = 10

F32_TILE_ROWS = 8
BF16_TILE_ROWS = 16
VMEM_LIMIT_BYTES = 56 * 1024 * 1024
ROW_TILE = 512
WIDE_ROW_TILE = 1024
FFN_ROW_TILE = 256
DW_BLK = D_FF // 2
SCAN_TILE = 256
RET_STEP_CHUNKS = 2
RET_STEP_ROWS = RET_STEP_CHUNKS * CHUNK

_DN = {
    "nn": (((1,), (0,)), ((), ())),
    "nt": (((1,), (1,)), ((), ())),
    "tn": (((0,), (0,)), ((), ())),
}


def _cparams(n_axes, collective_id=None):
    return pltpu.CompilerParams(dimension_semantics=("arbitrary",) * n_axes,
                                vmem_limit_bytes=VMEM_LIMIT_BYTES, collective_id=collective_id)


def _dot(a, b, kind):
    if b.ndim == 3:
        b = b.reshape(b.shape[0] * b.shape[1], b.shape[2])
    return lax.dot_general(a.astype(BF16), b.astype(BF16), _DN[kind], preferred_element_type=F32)


def _sigmoid(x):
    return 1.0 / (1.0 + jnp.exp(-x))


def _log1p_pos(e):
    u = 1.0 + e
    return jnp.where(u == 1.0, e, jnp.log(u) * (e / jnp.where(u == 1.0, 1.0, u - 1.0)))


def _expm1(x):
    u = jnp.exp(x)
    lu = jnp.log(u)
    safe = jnp.where(lu == 0.0, 1.0, lu)
    return jnp.where(u == 1.0, x, (u - 1.0) * (x / safe))


def _softplus(z):
    return jnp.maximum(z, 0.0) + _log1p_pos(jnp.exp(-jnp.abs(z)))


_GELU_C = math.sqrt(2.0 / math.pi)


def _gelu_and_grad(x):
    x2 = x * x
    t = jnp.tanh(_GELU_C * (x + 0.044715 * x * x2))
    g = 0.5 * x * (1.0 + t)
    dg = 0.5 * (1.0 + t) + 0.5 * x * (1.0 - t * t) * (_GELU_C * (1.0 + 3.0 * 0.044715 * x2))
    return g, dg


def _rms_fwd(x, g):
    r = lax.rsqrt(jnp.mean(x * x, axis=-1, keepdims=True) + EPS)
    return (x * r) * g


def _rms_bwd(x, g, dh):
    r = lax.rsqrt(jnp.mean(x * x, axis=-1, keepdims=True) + EPS)
    n = x * r
    dyg = dh * g
    dx = r * (dyg - n * jnp.mean(dyg * n, axis=-1, keepdims=True))
    return dx, jnp.sum(dh * n, axis=0, keepdims=True)


def _accumulate(ref, val, first):
    @pl.when(first)
    def _():
        ref[...] = val

    @pl.when(jnp.logical_not(first))
    def _():
        ref[...] += val


def _sds(shape, dtype):
    return jax.ShapeDtypeStruct(tuple(shape), dtype)


def _spec(shape, fn):
    return pl.BlockSpec(tuple(shape), fn)


class _Task:
    def __init__(self, peers, operands, out_shapes, aliases, nsem, make, finish, make_second=None):
        self.peers = peers
        self.operands, self.out_shapes, self.aliases = operands, out_shapes, aliases
        self.nsem, self.make, self.finish = nsem, make, finish
        self.make_second = make_second


class _Plan:
    def __init__(self):
        self.tasks, self.after = {}, {}


_plan = None


_CHIP_PEER_SETS = [frozenset({"chips"}), frozenset({"first"}), frozenset({"second"}), frozenset({"first", "second"})]
PEER_SET_COLLECTIVE_ID = {frozenset({"sibling"}): 1}
for _i, _chip_peers in enumerate(_CHIP_PEER_SETS):
    PEER_SET_COLLECTIVE_ID[_chip_peers] = 2 + 2 * _i
    PEER_SET_COLLECTIVE_ID[_chip_peers | {"sibling"}] = 3 + 2 * _i


def _peer_set(names):
    names = frozenset(n for name in names for n in name.split("+"))
    return names - {"first", "second"} if "chips" in names else names


def _axis_neighbours(x, y, c):
    flip = lambda v, f: v + f * (1 - 2 * v)
    return (flip(x, 1 - c), flip(y, c)), (flip(x, c), flip(y, 1 - c))


def _entry_handshake(peer_set):
    x, y, c, chips = _mesh_position()
    first, second = _axis_neighbours(x, y, c)
    peers = [(x, y, 1 - c)] if "sibling" in peer_set else []
    if "chips" in peer_set:
        peers += [(*chip, c) for chip in chips]
    if "first" in peer_set:
        peers.append((*first, c))
    if "second" in peer_set:
        peers.append((*second, c))
    barrier = pltpu.get_barrier_semaphore()
    for peer in peers:
        pl.semaphore_signal(barrier, inc=1, device_id=peer, device_id_type=MESH)
    pl.semaphore_wait(barrier, len(peers))


def _pcall(body, *, name, grid, in_specs, out_specs, out_shape, scratch_shapes=(), num_prefetch=0, own_peers=()):
    single = not isinstance(out_shape, (list, tuple))
    out_shape = [out_shape] if single else list(out_shape)
    out_specs = [out_specs] if single else list(out_specs)
    in_specs = list(in_specs)
    scratch_shapes = list(scratch_shapes)
    tasks = _plan.tasks.pop(name, []) if _plan is not None else []
    after = _plan.after.pop(name, []) if _plan is not None else []
    peer_set = _peer_set([t.peers for t in tasks] + list(own_peers))
    nax = len(grid)

    def run(*operands):
        n_in = len(operands) - num_prefetch
        n_out = len(out_shape)
        t_ops = [t.operands() for t in tasks]
        t_outs = [t.out_shapes() for t in tasks]
        c_ops = [a for ops in t_ops for a in ops]
        c_outs = [s for outs in t_outs for s in outs]
        aliases = {}
        i0, o0 = num_prefetch + n_in, n_out
        for t, ops, outs in zip(tasks, t_ops, t_outs):
            for i_loc, o_loc in t.aliases.items():
                aliases[i0 + i_loc] = o0 + o_loc
            i0 += len(ops)
            o0 += len(outs)
        nsem = sum(t.nsem for t in tasks)

        def wrapped(*refs):
            p = num_prefetch
            pre, ins = refs[:p], refs[p:p + n_in]
            cins = refs[p + n_in:p + n_in + len(c_ops)]
            q = p + n_in + len(c_ops)
            outs, couts = refs[q:q + n_out], refs[q + n_out:q + n_out + len(c_outs)]
            q += n_out + len(c_outs)
            scr = refs[q:q + len(scratch_shapes)]

            def rounds(second):
                send_sems, recv_sems = refs[q + len(scratch_shapes):]
                out = []
                ci = co = so = 0
                for t, ops, souts in zip(tasks, t_ops, t_outs):
                    make = t.make_second if second else t.make
                    out.append(([], []) if make is None else
                               make(cins[ci:ci + len(ops)], couts[co:co + len(souts)],
                                    functools.partial(lambda base, k: send_sems.at[base + k], so),
                                    functools.partial(lambda base, k: recv_sems.at[base + k], so)))
                    ci, co, so = ci + len(ops), co + len(souts), so + t.nsem
                return out

            two_rounds = [t.make_second is not None for t in tasks]
            if peer_set:
                ids = [pl.program_id(k) for k in range(nax)]
                first = functools.reduce(jnp.logical_and, [i == 0 for i in ids])
                last = functools.reduce(jnp.logical_and, [i == g - 1 for i, g in zip(ids, grid)])
                step = functools.reduce(lambda acc, ig: acc * ig[1] + ig[0], zip(ids, grid), 0)
                middle = step == math.prod(grid) // 3

                @pl.when(first)
                def _():
                    _entry_handshake(peer_set)
                    for starts, _ in rounds(False):
                        for copy in starts:
                            copy().start()

            body(*pre, *ins, *outs, *scr)

            if any(two_rounds):
                @pl.when(middle)
                def _():
                    for (_, arrivals), two in zip(rounds(False), two_rounds):
                        if two:
                            for arrival in arrivals:
                                arrival().wait_recv()
                    for starts, _ in rounds(True):
                        for copy in starts:
                            copy().start()

            if tasks:
                @pl.when(last)
                def _():
                    first_round, second_round = rounds(False), rounds(True)
                    for (_, arrivals1), (_, arrivals2), two in zip(first_round, second_round, two_rounds):
                        for arrival in (arrivals2 if two else arrivals1):
                            arrival().wait_recv()
                    for starts, _ in first_round + second_round:
                        for copy in starts:
                            copy().wait_send()

        sems = [pltpu.SemaphoreType.DMA((nsem,)), pltpu.SemaphoreType.DMA((nsem,))] if tasks else []
        res = pl.pallas_call(
            wrapped, name=name,
            grid_spec=pltpu.PrefetchScalarGridSpec(
                num_scalar_prefetch=num_prefetch, grid=tuple(grid),
                in_specs=in_specs + [ANY_SPEC] * len(c_ops),
                out_specs=out_specs + [ANY_SPEC] * len(c_outs),
                scratch_shapes=scratch_shapes + sems),
            out_shape=out_shape + c_outs,
            input_output_aliases=aliases,
            compiler_params=_cparams(nax, PEER_SET_COLLECTIVE_ID[peer_set] if peer_set else None),
        )(*operands, *c_ops)
        co = n_out
        for t, souts in zip(tasks, t_outs):
            t.finish(res[co:co + len(souts)])
            co += len(souts)
        for fn in after:
            fn()
        return res[0] if single else list(res[:n_out])

    return run


def _comm_call(name):
    def body(o_ref):
        o_ref[...] = jnp.zeros_like(o_ref)

    _pcall(body, name=name, grid=(1,), in_specs=[], out_specs=_spec((8, 128), lambda i: (0, 0)),
           out_shape=_sds((8, 128), F32))()


def _gemm(name, terms, grid, outs, acc_shape, extras=(), epilogue=None):
    kinds = [t[4] for t in terms]
    nt, ne, no = len(terms), len(extras), len(outs)
    nred = grid[-1]
    nax = len(grid)

    def body(*refs):
        trefs = refs[:2 * nt]
        erefs = refs[2 * nt:2 * nt + ne]
        orefs = refs[2 * nt + ne:2 * nt + ne + no]
        ids = [pl.program_id(k) for k in range(nax)]
        tot = None
        for t in range(nt):
            d = _dot(trefs[2 * t][...], trefs[2 * t + 1][...], kinds[t])
            tot = d if tot is None else tot + d

        def finish(acc):
            if epilogue is None:
                orefs[0][...] = acc.astype(orefs[0].dtype)
            else:
                epilogue(acc, erefs, orefs, ids)

        if nred == 1:
            finish(tot)
        else:
            acc_ref = refs[-1]
            r = ids[-1]

            @pl.when(r == 0)
            def _():
                acc_ref[...] = tot

            @pl.when(r > 0)
            def _():
                acc_ref[...] += tot

            @pl.when(r == nred - 1)
            def _():
                finish(acc_ref[...])

    operands, in_specs = [], []
    for a, a_spec, b, b_spec, _ in terms:
        operands += [a, b]
        in_specs += [a_spec, b_spec]
    for e, e_spec in extras:
        operands.append(e)
        in_specs.append(e_spec)
    scratch = [pltpu.VMEM(tuple(acc_shape), F32)] if nred > 1 else []
    return _pcall(body, name=name, grid=tuple(grid), in_specs=in_specs, out_specs=[o[1] for o in outs],
                  out_shape=[o[0] for o in outs], scratch_shapes=scratch)(*operands)


def _rowwise(name, fn, ins, outs, grid):
    ni = len(ins)
    nax = len(grid)

    def body(*refs):
        ids = [pl.program_id(k) for k in range(nax)]
        fn(refs[:ni], refs[ni:], ids)

    return _pcall(body, name=name, grid=tuple(grid), in_specs=[i[1] for i in ins],
                  out_specs=[o[1] for o in outs], out_shape=[o[0] for o in outs])(*[i[0] for i in ins])


def _ffn_up(name, h, w1buf, w1_idx, w3buf, w3_idx, norm_gain=None):
    T = h.shape[0]
    tm = min(FFN_ROW_TILE, T)
    normed = norm_gain is not None

    def body(h_ref, *refs):
        if normed:
            g_ref, w1_ref, w3_ref, a_ref, b_ref, s_ref, hn_ref = refs
            hv = _rms_fwd(h_ref[...], g_ref[...]).astype(BF16)
            hn_ref[...] = hv
        else:
            w1_ref, w3_ref, a_ref, b_ref, s_ref = refs
            hv = h_ref[...]
        a = _dot(hv, w1_ref[...], "nt")
        b = _dot(hv, w3_ref[...], "nt")
        a_ref[...] = a.astype(BF16)
        b_ref[...] = b.astype(BF16)
        s_ref[...] = ((a * _sigmoid(a)) * b).astype(BF16)

    row = _spec((tm, D), lambda i: (i, 0))
    blk = _spec((tm, D_FF), lambda i: (i, 0))
    return _pcall(
        body, name=name, grid=(T // tm,),
        in_specs=[row] + ([_spec((1, D), lambda i: (0, 0))] if normed else [])
        + [_spec((N_CHIPS, None, FF_BLK, D), lambda i: (0, w1_idx, 0, 0)),
           _spec((N_CHIPS, None, FF_BLK, D), lambda i: (0, w3_idx, 0, 0))],
        out_specs=[blk, blk, blk] + ([row] if normed else []),
        out_shape=[_sds((T, D_FF), BF16)] * 3 + ([_sds((T, D), BF16)] if normed else []),
    )(h, *([norm_gain] if normed else []), w1buf, w3buf)


def _loss_head(x, g, tgt, loss_ref, dx_ref, dg_ref, first):
    err = _rms_fwd(x, g) - tgt
    lp = 0.5 * jnp.sum(jnp.mean(err * err, axis=-1, keepdims=True), axis=0, keepdims=True)
    _accumulate(loss_ref, jnp.broadcast_to(lp, (1, 128)), first)
    dx, dgp = _rms_bwd(x, g, err * (1.0 / D))
    dx_ref[...] = dx
    _accumulate(dg_ref, dgp, first)


def _ffn_down(name, s, wrow2, w2_idx, x_res, g_next=None, loss_target=None):
    T = x_res.shape[0]
    tm = min(ROW_TILE, T)
    row = lambda i, j, r: (i, 0)
    vec = lambda i, j, r: (0, 0)

    def epilogue(acc, erefs, orefs, ids):
        xo = erefs[0][...] + 0.5 * acc
        if loss_target is not None:
            _loss_head(xo, erefs[1][...], erefs[2][...], orefs[0], orefs[1], orefs[2], ids[0] == 0)
            return
        orefs[0][...] = xo
        orefs[1][...] = _rms_fwd(xo, erefs[1][...]).astype(BF16)

    extras = [(x_res, _spec((tm, D), row)), (g_next, _spec((1, D), vec))]
    if loss_target is None:
        outs = [(_sds((T, D), F32), _spec((tm, D), row)), (_sds((T, D), BF16), _spec((tm, D), row))]
    else:
        extras.append((loss_target, _spec((tm, D), row)))
        outs = [(_sds((1, 128), F32), _spec((1, 128), vec)), (_sds((T, D), F32), _spec((tm, D), row)),
                (_sds((1, D), F32), _spec((1, D), vec))]
    return _gemm(
        name,
        [(s, _spec((tm, D_FF), row),
          wrow2, _spec((N_CHIPS, None, FF_BLK, D), lambda i, j, r: (0, w2_idx, 0, 0)), "nn")],
        (T // tm, 1, 1), outs, (tm, D), extras, epilogue)


def _ffn_bwd_mid(name, dx, wrow2, w2_idx, a, b):
    T = dx.shape[0]
    tm = min(FFN_ROW_TILE, T)

    def body(dx_ref, w2_ref, a_ref, b_ref, dab_ref):
        ds = _dot(0.5 * dx_ref[...], w2_ref[...], "nt")
        av = a_ref[...].astype(F32)
        sg = _sigmoid(av)
        dab_ref[0] = (ds * b_ref[...].astype(F32) * (sg * (1.0 + av * (1.0 - sg)))).astype(BF16)
        dab_ref[1] = (ds * (av * sg)).astype(BF16)

    blk = _spec((tm, D_FF), lambda i: (i, 0))
    return _pcall(
        body, name=name, grid=(T // tm,),
        in_specs=[_spec((tm, D), lambda i: (i, 0)),
                  _spec((N_CHIPS, None, FF_BLK, D), lambda i: (0, w2_idx, 0, 0)),
                  blk, blk],
        out_specs=_spec((2, tm, D_FF), lambda i: (0, i, 0)),
        out_shape=_sds((2, T, D_FF), BF16),
    )(dx, wrow2, a, b)


def _rms_bwd_epilogue(acc, erefs, orefs, ids):
    dx, dgp = _rms_bwd(erefs[0][...], erefs[1][...], acc)
    orefs[0][...] = dx + erefs[2][...]
    _accumulate(orefs[1], dgp, ids[0] == 0)


def _rms_bwd_io(x, g, dres, T, tm):
    row = lambda i, j, r: (i, 0)
    vec = lambda i, j, r: (0, 0)
    extras = [(x, _spec((tm, D), row)), (g, _spec((1, D), vec)), (dres, _spec((tm, D), row))]
    outs = [(_sds((T, D), F32), _spec((tm, D), row)), (_sds((1, D), F32), _spec((1, D), vec))]
    return extras, outs


def _ffn_bwd(tag, dx_out, h, a, b, s, w1buf, w1_idx, w3buf, w3_idx, wrow2, w2_idx, x_in, g, big):
    T = dx_out.shape[0]
    dab = _ffn_bwd_mid(tag + "_bwd_mid", dx_out, wrow2, w2_idx, a, b)

    def half_scale(acc, erefs, orefs, ids):
        orefs[0][...] = (0.5 * acc).astype(orefs[0].dtype)

    dw_grid = (D_FF // DW_BLK, 1, 1)
    dw_out = [(_sds((D_FF, D), GRAD_WIRE_DTYPE), _spec((DW_BLK, D), lambda j, n, r: (j, 0)))]
    tokens = _spec((T, D), lambda j, n, r: (0, 0))
    big[tag + "_w2"] = _gemm(
        tag + "_dw2", [(s, _spec((T, DW_BLK), lambda j, n, r: (0, j)), dx_out, tokens, "tn")],
        dw_grid, dw_out, (DW_BLK, D), (), half_scale)[0].reshape(1, N_CHIPS, FF_BLK, D)
    for widx, wname in ((0, "_w1"), (1, "_w3")):
        big[tag + wname] = _gemm(
            tag + "_d" + wname[1:],
            [(dab, _spec((None, T, DW_BLK), functools.partial(lambda w, j, n, r: (w, 0, j), widx)), h, tokens, "tn")],
            dw_grid, dw_out, (DW_BLK, D))[0].reshape(1, N_CHIPS, FF_BLK, D)
    tm = min(FFN_ROW_TILE, T)
    extras, outs = _rms_bwd_io(x_in, g, dx_out, T, tm)
    whole = lambda idx: _spec((N_CHIPS, None, FF_BLK, D), lambda i, j, r: (0, idx, 0, 0))
    dx_in, dg = _gemm(
        tag + "_dh",
        [(dab, _spec((None, tm, D_FF), lambda i, j, r: (0, i, 0)), w1buf, whole(w1_idx), "nn"),
         (dab, _spec((None, tm, D_FF), lambda i, j, r: (1, i, 0)), w3buf, whole(w3_idx), "nn")],
        (T // tm, 1, 1), outs, (tm, D), extras, _rms_bwd_epilogue)
    return dx_in, dg


def _proj_sq(name, a, wsq, idx, kind, out_dtype=F32, extras=(), epilogue=None, outs=None):
    M = a.shape[0]
    tm = min(ROW_TILE, M)
    if outs is None:
        outs = [(_sds((M, D), out_dtype), _spec((tm, D), lambda i, j, r: (i, 0)))]
    return _gemm(
        name,
        [(a, _spec((tm, D), lambda i, j, r: (i, 0)),
          wsq, _spec((N_CHIPS, None, SQ_BLK, D), lambda i, j, r: (0, idx, 0, 0)), kind)],
        (M // tm, 1, 1), outs, (tm, D), extras, epilogue)


def _dw_sq(name, a, b):
    M = a.shape[0]
    tn = D // 2
    whole = _gemm(
        name,
        [(a, _spec((M, D), lambda i, j, r: (0, 0)), b, _spec((M, tn), lambda i, j, r: (0, j)), "tn")],
        (1, D // tn, 1),
        [(_sds((D, D), GRAD_WIRE_DTYPE), _spec((D, tn), lambda i, j, r: (0, j)))],
        (D, tn))[0]
    return whole.reshape(N_CHIPS, SQ_BLK, D)


def _retention_constants(T):
    pos = jnp.arange(T, dtype=F32)
    inv_freq = ROPE_BASE ** (-jnp.arange(0, RET_DK, 2, dtype=F32) / RET_DK)
    ang = pos[:, None] * inv_freq[None, :]
    cosf = jnp.concatenate([jnp.cos(ang), jnp.cos(ang)], axis=1)
    sins = jnp.concatenate([-jnp.sin(ang), jnp.sin(ang)], axis=1)
    lg = jnp.log(1.0 - 2.0 ** (-5.0 - jnp.arange(RET_HEADS, dtype=F32)))
    p = jnp.arange(CHUNK, dtype=F32)
    rel = p[:, None] - p[None, :]
    dmat = jnp.where(rel[None] >= 0, jnp.exp(rel[None] * lg[:, None, None]), 0.0)
    kd = jnp.exp((CHUNK - 1.0 - p)[None, :] * lg[:, None])[:, :, None]
    qd = jnp.exp((p + 1.0)[None, :] * lg[:, None])[:, :, None]
    cd = jnp.exp(CHUNK * lg)[:, None, None]
    return cosf, sins, dmat, kd, qd, cd


def _rot(t, cosv, sinv):
    return t * cosv + pltpu.roll(t, RET_DK // 2, 1) * sinv


def _unrot(t, cosv, sinv):
    return t * cosv - pltpu.roll(t, RET_DK // 2, 1) * sinv


def _ret_const_specs(cm):
    whole = lambda shape: _spec(shape, lambda c: (0,) * len(shape))
    return [
        _spec((RET_STEP_ROWS, RET_DK), lambda c: (cm(c), 0)),
        _spec((RET_STEP_ROWS, RET_DK), lambda c: (cm(c), 0)),
        whole((RET_HEADS, CHUNK, CHUNK)), whole((RET_HEADS, CHUNK, 1)), whole((RET_HEADS, CHUNK, 1)),
        whole((RET_HEADS, 1, 1)),
    ]


def _head(h, width):
    return slice(h * width, (h + 1) * width)


def _ret_fwd(u, consts, ret_gn):
    T = u.shape[0]
    nC = T // CHUNK
    kscale = RET_DK ** -0.5

    def body(q_ref, k_ref, v_ref, g_ref, cos_ref, sin_ref, dm_ref, kd_ref, qd_ref, cd_ref, gn_ref,
             qr_ref, kr_ref, ret_ref, yr_ref, st_ref, state):
        @pl.when(pl.program_id(0) == 0)
        def _():
            state[...] = jnp.zeros_like(state)

        for cc in range(RET_STEP_CHUNKS):
            rows = slice(cc * CHUNK, (cc + 1) * CHUNK)
            cosv, sinv = cos_ref[rows, :], sin_ref[rows, :]
            for h in range(RET_HEADS):
                hk, hv = _head(h, RET_DK), _head(h, RET_DV)
                q = _rot(q_ref[rows, hk], cosv, sinv)
                k = _rot(k_ref[rows, hk], cosv, sinv) * kscale
                v = v_ref[rows, hv]
                qr_ref[rows, hk] = q
                kr_ref[rows, hk] = k
                prev = state[h]
                st_ref[h, cc] = prev
                s = _dot(q, k, "nt") * dm_ref[h]
                ret = _dot(s, v, "nn") + _dot(q, prev, "nn") * qd_ref[h]
                state[h] = cd_ref[h] * prev + _dot(k * kd_ref[h], v, "tn")
                ret_ref[rows, hv] = ret
                mu = jnp.mean(ret, axis=-1, keepdims=True)
                xc = ret - mu
                yn = xc * lax.rsqrt(jnp.mean(xc * xc, axis=-1, keepdims=True) + EPS)
                g = g_ref[rows, hv]
                yr_ref[rows, hv] = ((g * _sigmoid(g)) * (yn * gn_ref[:, hv])).astype(BF16)

    cm = lambda c: c
    qk_w, v_w = RET_HEADS * RET_DK, RET_HEADS * RET_DV
    in_specs = [
        _spec((RET_STEP_ROWS, qk_w), lambda c: (c, 0)), _spec((RET_STEP_ROWS, qk_w), lambda c: (c, 1)),
        _spec((RET_STEP_ROWS, v_w), lambda c: (c, 1)), _spec((RET_STEP_ROWS, v_w), lambda c: (c, 2)),
    ] + _ret_const_specs(cm) + [_spec((1, v_w), lambda c: (0, 0))]
    qk_out = _spec((RET_STEP_ROWS, qk_w), lambda c: (c, 0))
    v_out = _spec((RET_STEP_ROWS, v_w), lambda c: (c, 0))
    return _pcall(
        body, name="ret_fwd", grid=(nC // RET_STEP_CHUNKS,),
        in_specs=in_specs,
        out_specs=[qk_out, qk_out, v_out, v_out,
                   _spec((RET_HEADS, RET_STEP_CHUNKS, RET_DK, RET_DV), lambda c: (0, c, 0, 0))],
        out_shape=[_sds((T, qk_w), F32), _sds((T, qk_w), F32), _sds((T, v_w), F32), _sds((T, v_w), BF16),
                   _sds((RET_HEADS, nC, RET_DK, RET_DV), F32)],
        scratch_shapes=[pltpu.VMEM((RET_HEADS, RET_DK, RET_DV), F32)],
    )(u, u, u, u, *consts, ret_gn)


def _ret_bwd(dyr, ret, u, qr, kr, states, consts, ret_gn):
    T = u.shape[0]
    nC = T // CHUNK
    kscale = RET_DK ** -0.5

    def body(dyr_ref, ret_ref, g_ref, q_ref, k_ref, v_ref, st_ref,
             cos_ref, sin_ref, dm_ref, kd_ref, qd_ref, cd_ref, gn_ref,
             dq_ref, dk_ref, dv_ref, dg_ref, dgn_ref, gstate):
        first = pl.program_id(0) == 0

        @pl.when(first)
        def _():
            gstate[...] = jnp.zeros_like(gstate)

        dgn_total = None
        for cc in reversed(range(RET_STEP_CHUNKS)):
            rows = slice(cc * CHUNK, (cc + 1) * CHUNK)
            cosv, sinv = cos_ref[rows, :], sin_ref[rows, :]
            dgn_parts = []
            for h in range(RET_HEADS):
                hk, hv = _head(h, RET_DK), _head(h, RET_DV)
                ret = ret_ref[rows, hv]
                mu = jnp.mean(ret, axis=-1, keepdims=True)
                xc = ret - mu
                rs = lax.rsqrt(jnp.mean(xc * xc, axis=-1, keepdims=True) + EPS)
                yn = xc * rs
                gn = gn_ref[:, hv]
                g = g_ref[rows, hv]
                sg = _sigmoid(g)
                dyr_v = dyr_ref[rows, hv]
                dretn = dyr_v * (g * sg)
                dg_ref[rows, hv] = (dyr_v * (yn * gn) * (sg * (1.0 + g * (1.0 - sg)))).astype(BF16)
                dgn_parts.append(jnp.sum(dretn * yn, axis=0, keepdims=True))
                dyn = dretn * gn
                d_o = rs * (dyn - jnp.mean(dyn, axis=-1, keepdims=True)
                            - yn * jnp.mean(dyn * yn, axis=-1, keepdims=True))

                q, k, v = q_ref[rows, hk], k_ref[rows, hk], v_ref[rows, hv]
                dmat, kd, qd = dm_ref[h], kd_ref[h], qd_ref[h]
                prev = st_ref[h, cc]
                gnext = gstate[h]
                s = _dot(q, k, "nt") * dmat
                ds = _dot(d_o, v, "nt") * dmat
                doq = d_o * qd
                dq = _dot(ds, k, "nn") + _dot(doq, prev, "nt")
                dk = _dot(ds, q, "tn") + _dot(v, gnext, "nt") * kd
                dv = _dot(s, d_o, "tn") + _dot(k * kd, gnext, "nn")
                gstate[h] = cd_ref[h] * gnext + _dot(q, doq, "tn")
                dq_ref[rows, hk] = _unrot(dq, cosv, sinv).astype(BF16)
                dk_ref[rows, hk] = _unrot(dk * kscale, cosv, sinv).astype(BF16)
                dv_ref[rows, hv] = dv.astype(BF16)
            dgn = jnp.concatenate(dgn_parts, axis=1)
            dgn_total = dgn if dgn_total is None else dgn_total + dgn
        _accumulate(dgn_ref, dgn_total, first)

    n_steps = nC // RET_STEP_CHUNKS
    cm = lambda c: n_steps - 1 - c
    qk_w, v_w = RET_HEADS * RET_DK, RET_HEADS * RET_DV
    vspec = lambda blk: _spec((RET_STEP_ROWS, v_w), lambda c: (cm(c), blk))
    qspec = _spec((RET_STEP_ROWS, qk_w), lambda c: (cm(c), 0))
    in_specs = [vspec(0), vspec(0), vspec(2), qspec, qspec, vspec(1),
                _spec((RET_HEADS, RET_STEP_CHUNKS, RET_DK, RET_DV), lambda c: (0, cm(c), 0, 0)),
                ] + _ret_const_specs(cm) + [_spec((1, v_w), lambda c: (0, 0))]
    return _pcall(
        body, name="ret_bwd", grid=(n_steps,),
        in_specs=in_specs,
        out_specs=[qspec, qspec, vspec(0), vspec(0), _spec((1, v_w), lambda c: (0, 0))],
        out_shape=[_sds((T, qk_w), BF16), _sds((T, qk_w), BF16), _sds((T, v_w), BF16), _sds((T, v_w), BF16),
                   _sds((1, v_w), F32)],
        scratch_shapes=[pltpu.VMEM((RET_HEADS, RET_DK, RET_DV), F32)],
    )(dyr, ret, u, qr, kr, u, states, *consts, ret_gn)


def _shift_down(x, s):
    rows = lax.broadcasted_iota(jnp.int32, x.shape, 0)
    return jnp.where(rows >= s, pltpu.roll(x, s, 0), 0.0)


def _shift_up(x, s):
    n = x.shape[0]
    rows = lax.broadcasted_iota(jnp.int32, x.shape, 0)
    return jnp.where(rows < n - s, pltpu.roll(x, n - s, 0), 0.0)


def _lru_specs(T):
    col = lambda off: _spec((T, LRU_BLOCK), lambda g: (0, off + g))
    vec = _spec((1, LRU_BLOCK), lambda g: (0, g))
    wblk = _spec((None, LRU_BLOCK, LRU_BLOCK), lambda g: (g, 0, 0))
    cw = _spec((CONV_TAPS, LRU_BLOCK), lambda g: (0, g))
    return col, vec, wblk, cw


def _lru_gates_fwd(u, conv_w, conv_b, w_r, b_r, w_i, b_i, lam):
    T = u.shape[0]
    col, vec, wblk, cw = _lru_specs(T)

    def body(x_ref, cw_ref, cb_ref, wr_ref, br_ref, wi_ref, bi_ref, lam_ref,
             xc_ref, r_ref, i_ref, a_ref, bx_ref):
        x = x_ref[...]
        w = cw_ref[...]
        xc = (_shift_down(x, 3) * w[0:1] + _shift_down(x, 2) * w[1:2] + _shift_down(x, 1) * w[2:3]
              + x * w[3:4] + cb_ref[...])
        r = _sigmoid(_dot(xc, wr_ref[...], "nn") + br_ref[...])
        i = _sigmoid(_dot(xc, wi_ref[...], "nn") + bi_ref[...])
        la = (-LRU_C) * r * _softplus(-lam_ref[...])
        xc_ref[...] = xc
        r_ref[...] = r
        i_ref[...] = i
        a_ref[...] = jnp.exp(la)
        bx_ref[...] = jnp.sqrt(-_expm1(2.0 * la)) * (i * xc)

    out = col(0)
    return _pcall(
        body, name="lru_gates_fwd", grid=(LRU_BLOCKS,),
        in_specs=[col(24), cw, vec, wblk, vec, wblk, vec, vec],
        out_specs=[out] * 5,
        out_shape=[_sds((T, D), F32)] * 5,
    )(u, conv_w, conv_b, w_r, b_r, w_i, b_i, lam)


def _lru_scan(name, a3, b3, reverse):
    T = a3.shape[0]
    nt = T // SCAN_TILE
    unroll = 8

    def body(a_ref, b_ref, o_ref, carry):
        @pl.when(pl.program_id(0) == 0)
        def _():
            carry[...] = jnp.zeros_like(carry)

        if not reverse:
            def step(t, h):
                h = a_ref[t] * h + b_ref[t]
                o_ref[t] = h
                return h
        else:
            def step(k, c):
                t = SCAN_TILE - 1 - k
                l = b_ref[t] + c
                o_ref[t] = l
                return a_ref[t] * l
        carry[...] = lax.fori_loop(0, SCAN_TILE, step, carry[...], unroll=unroll)

    idx = (lambda i: (nt - 1 - i, 0, 0)) if reverse else (lambda i: (i, 0, 0))
    blk = _spec((SCAN_TILE, LRU_BLOCKS, LRU_BLOCK), idx)
    return _pcall(
        body, name=name, grid=(nt,),
        in_specs=[blk, blk], out_specs=blk,
        out_shape=_sds((T, LRU_BLOCKS, LRU_BLOCK), F32),
        scratch_shapes=[pltpu.VMEM((LRU_BLOCKS, LRU_BLOCK), F32)],
    )(a3, b3)


def _lru_gates_bwd(lmb, hl, a, r, i, xc, u, conv_w, w_r, w_i, lam):
    T = u.shape[0]
    col, vec, wblk, cw = _lru_specs(T)

    def body(l_ref, h_ref, a_ref, r_ref, i_ref, xc_ref, x_ref, cw_ref, wr_ref, wi_ref, lam_ref,
             dx_ref, dwr_ref, dwi_ref, dvec_ref, dcw_ref):
        l = l_ref[...]
        av, rv, iv, xc = a_ref[...], r_ref[...], i_ref[...], xc_ref[...]
        lam_v = lam_ref[...]
        sp = _softplus(-lam_v)
        la = (-LRU_C) * rv * sp
        mult = jnp.sqrt(-_expm1(2.0 * la))
        da = l * _shift_down(h_ref[...], 1)
        dmult = l * (iv * xc)
        di = l * mult * xc
        dxc = l * mult * iv
        dla = da * av - dmult * (av * av) / mult
        dzr = (dla * ((-LRU_C) * sp)) * rv * (1.0 - rv)
        dzi = di * iv * (1.0 - iv)
        dsp = jnp.sum(dla * ((-LRU_C) * rv), axis=0, keepdims=True)
        dlam = dsp * (-_sigmoid(-lam_v))
        dwr_ref[...] = _dot(xc, dzr, "tn")
        dwi_ref[...] = _dot(xc, dzi, "tn")
        dxc = dxc + _dot(dzr, wr_ref[...], "nt") + _dot(dzi, wi_ref[...], "nt")
        x = x_ref[...]
        w = cw_ref[...]
        dx = (dxc * w[3:4] + _shift_up(dxc, 1) * w[2:3] + _shift_up(dxc, 2) * w[1:2]
              + _shift_up(dxc, 3) * w[0:1])
        dx_ref[...] = dx.astype(BF16)
        dvec_ref[...] = jnp.concatenate(
            [jnp.sum(dzr, axis=0, keepdims=True), jnp.sum(dzi, axis=0, keepdims=True), dlam,
             jnp.sum(dxc, axis=0, keepdims=True)], axis=0)
        dcw_ref[...] = jnp.concatenate(
            [jnp.sum(dxc * _shift_down(x, 3 - tap), axis=0, keepdims=True) if tap < 3
             else jnp.sum(dxc * x, axis=0, keepdims=True) for tap in range(CONV_TAPS)], axis=0)

    c0 = col(0)
    return _pcall(
        body, name="lru_gates_bwd", grid=(LRU_BLOCKS,),
        in_specs=[c0, c0, c0, c0, c0, c0, col(24), cw, wblk, wblk, vec],
        out_specs=[c0, wblk, wblk, cw, cw],
        out_shape=[_sds((T, D), BF16), _sds((LRU_BLOCKS, LRU_BLOCK, LRU_BLOCK), F32),
                   _sds((LRU_BLOCKS, LRU_BLOCK, LRU_BLOCK), F32), _sds((4, D), F32), _sds((CONV_TAPS, D), F32)],
    )(lmb, hl, a, r, i, xc, u, conv_w, w_r, w_i, lam)


def _xattn_probs(q, k):
    sc = _dot(q, k, "nt") * (X_HD ** -0.5)
    e = jnp.exp(sc - jnp.max(sc, axis=-1, keepdims=True))
    return e / jnp.sum(e, axis=-1, keepdims=True)


def _xattn_fwd(xq, xk, xv):
    T = xq.shape[0]
    tq = min(WIDE_ROW_TILE, T)
    M = xk.shape[0]

    def body(q_ref, k_ref, v_ref, o_ref):
        p = _xattn_probs(q_ref[...], k_ref[...])
        o_ref[...] = _dot(p, v_ref[...], "nn").astype(BF16)

    qs = _spec((tq, X_HD), lambda h, i: (i, h))
    kv = _spec((M, X_HD), lambda h, i: (0, h))
    return _pcall(
        body, name="xattn_fwd", grid=(X_HEADS, T // tq),
        in_specs=[qs, kv, kv], out_specs=qs, out_shape=_sds((T, D), BF16),
    )(xq, xk, xv)


def _xattn_bwd(xq, xk, xv, dxo):
    T = xq.shape[0]
    tq = min(WIDE_ROW_TILE, T)
    M = xk.shape[0]

    def body(q_ref, k_ref, v_ref, do_ref, dq_ref, dk_ref, dv_ref):
        first = pl.program_id(1) == 0
        q, k, v, do = q_ref[...], k_ref[...], v_ref[...], do_ref[...]
        p = _xattn_probs(q, k)
        dp = _dot(do, v, "nt")
        ds = p * (dp - jnp.sum(dp * p, axis=-1, keepdims=True)) * (X_HD ** -0.5)
        dq_ref[...] = _dot(ds, k, "nn").astype(BF16)
        _accumulate(dk_ref, _dot(ds, q, "tn"), first)
        _accumulate(dv_ref, _dot(p, do, "tn"), first)

    qs = _spec((tq, X_HD), lambda h, i: (i, h))
    kv = _spec((M, X_HD), lambda h, i: (0, h))
    return _pcall(
        body, name="xattn_bwd", grid=(X_HEADS, T // tq),
        in_specs=[qs, kv, kv, qs], out_specs=[qs, kv, kv],
        out_shape=[_sds((T, D), BF16), _sds((M, D), F32), _sds((M, D), F32)],
    )(xq, xk, xv, dxo)


def _adamw(name, w, g, m, v):
    R, C = w.shape
    tr = R
    for cand in (512, 352, 256):
        if R % cand == 0:
            tr = cand
            break

    def fn(irefs, orefs, ids):
        delta, mn, vn = _adamw_update(*(r[...] for r in irefs))
        orefs[0][...] = delta
        orefs[1][...] = mn
        orefs[2][...] = vn

    blk = _spec((tr, C), lambda i: (i, 0))
    return _rowwise(name, fn, [(w, blk), (g, blk), (m, blk), (v, blk)],
                    [(_sds((R, C), F32), blk)] * 3, (R // tr,))


def _adamw_update(wv, gv, mv, vv):
    c1 = 1.0 - ADAM_B1 ** ADAM_STEP
    c2 = 1.0 - ADAM_B2 ** ADAM_STEP
    mn = ADAM_B1 * mv + (1.0 - ADAM_B1) * gv
    vn = ADAM_B2 * vv + (1.0 - ADAM_B2) * (gv * gv)
    delta = -ADAM_LR * ((mn / c1) / (jnp.sqrt(vn / c2) + ADAM_EPS) + ADAM_WD * wv)
    return delta, mn, vn


def _adamw_halves(name, w, mine, theirs, widx, m, v, core):
    R, C = w.shape
    H = R // 2
    tr = H
    while tr * C * 4 > (1 << 20) and tr % 16 == 0:
        tr //= 2
    nb = H // tr

    def body(core_ref, w_ref, mine_ref, theirs_ref, m_ref, v_ref, g_out, d_out, m_out, v_out):
        gv = jnp.where(pl.program_id(0) == core_ref[0], mine_ref[...], theirs_ref[...])
        delta, mn, vn = _adamw_update(w_ref[...], gv, m_ref[...], v_ref[...])
        g_out[...] = gv
        d_out[...] = delta
        m_out[...] = mn
        v_out[...] = vn

    full = pl.BlockSpec((tr, C), lambda h, i, core_ref: (h * nb + i, 0))
    mine_spec = pl.BlockSpec((None, tr, C), lambda h, i, core_ref: (widx, jnp.where(h == core_ref[0], i, 0), 0))
    theirs_spec = pl.BlockSpec((None, tr, C), lambda h, i, core_ref: (widx, jnp.where(h == core_ref[0], 0, i), 0))
    return _pcall(
        body, name=name, grid=(2, nb), num_prefetch=1,
        in_specs=[full, mine_spec, theirs_spec, full, full], out_specs=[full] * 4,
        out_shape=[_sds((R, C), F32)] * 4,
    )(core, w, mine, theirs, m, v)


def _rmsnorm(name, x, g):
    M = x.shape[0]
    tm = min(ROW_TILE, M)

    def fn(irefs, orefs, ids):
        orefs[0][...] = _rms_fwd(irefs[0][...], irefs[1][...]).astype(BF16)

    row = _spec((tm, D), lambda i: (i, 0))
    return _rowwise(name, fn, [(x, row), (g, _spec((1, D), lambda i: (0, 0)))],
                    [(_sds((M, D), BF16), row)], (M // tm,))[0]


WEIGHT_AT = {
    "ffn1_w1": ("col1", 0), "ffn1_w3": ("col1", 1), "ffn1_w2": ("row2a", 0),
    "w_ret_o": ("sqA", 0), "w_lru_o": ("sqA", 1), "w_out": ("sqA", 2),
    "w_xq": ("sqB", 0), "w_xk": ("sqB", 1), "w_xv": ("sqC", 0), "w_xo": ("sqC", 1),
    "ffn2_w1": ("col2a", 0), "ffn2_w3": ("col2b", 0), "ffn2_w2": ("row2b", 0),
}


def _local_step(x, mem, tgt, gw, sm, big):
    T = x.shape[0]
    tm = ROW_TILE

    def wt(name):
        key, idx = WEIGHT_AT[name]
        return gw[key], idx

    row3 = lambda i, j, r: (i, 0)
    vec3 = lambda i, j, r: (0, 0)
    rowD = _spec((tm, D), row3)
    vecD = _spec((1, D), vec3)

    def residual_norm(acc, erefs, orefs, ids):
        xo = erefs[0][...] + acc
        orefs[0][...] = xo
        orefs[1][...] = _rms_fwd(xo, erefs[1][...]).astype(BF16)

    def res_norm_io(x_res, g):
        return ([(x_res, rowD), (g, vecD)],
                [(_sds((T, D), F32), rowD), (_sds((T, D), BF16), rowD)])

    a1, b1, s1, h1 = _ffn_up("ffn1_up", x, *wt("ffn1_w1"), *wt("ffn1_w3"), norm_gain=sm["ffn1_norm"])
    x1, h2 = _ffn_down("ffn1_down", s1, *wt("ffn1_w2"), x, sm["mix_norm"])

    tw = min(WIDE_ROW_TILE, T)
    wideD = _spec((tw, D), row3)
    u = _gemm(
        "mix_in",
        [(h2, wideD, gw["win"], _spec((None, None, IN_BLK, D), lambda i, j, r: (j, 0, 0, 0)), "nt")],
        (T // tw, N_CHIPS, 1),
        [(_sds((T, 5120), F32), _spec((tw, IN_BLK), lambda i, j, r: (i, j)))], (tw, IN_BLK))[0]

    consts = _retention_constants(T)
    qr, kr, ret, yr, states = _ret_fwd(u, consts, sm["ret_gn"])

    conv_w = gw["conv"][:, 0].transpose(1, 0, 2).reshape(CONV_TAPS, D)
    xc, rg, ig, av, bx = _lru_gates_fwd(u, conv_w, sm["conv_b"], sm["w_rgate"], sm["b_rgate"],
                                        sm["w_igate"], sm["b_igate"], sm["lru_lambda"])
    a3 = av.reshape(T, LRU_BLOCKS, LRU_BLOCK)
    b3 = bx.reshape(T, LRU_BLOCKS, LRU_BLOCK)

    def gate_epilogue(acc, erefs, orefs, ids):
        orefs[0][...] = _sigmoid(acc + erefs[0][...])

    gates = _gemm(
        "mix_gates",
        [(h2, wideD, gw["wbg"], _spec((None, None, BG_BLK, D), lambda i, j, r: (j, 0, 0, 0)), "nt")],
        (T // tw, N_CHIPS, 1),
        [(_sds((T, 2 * D), F32), _spec((tw, BG_BLK), lambda i, j, r: (i, j)))], (tw, BG_BLK),
        [(sm["b_branch_gate"], _spec((1, BG_BLK), lambda i, j, r: (0, j)))], gate_epilogue)[0]

    hl = _lru_scan("lru_scan_fwd", a3, b3, False).reshape(T, D)

    row1 = _spec((tm, D), lambda i: (i, 0))
    glru1 = _spec((tm, D), lambda i: (i, 4))

    def lru_out(irefs, orefs, ids):
        gl, _ = _gelu_and_grad(irefs[1][...])
        orefs[0][...] = (irefs[0][...] * gl).astype(BF16)

    yl = _rowwise("lru_out", lru_out, [(hl, row1), (u, glru1)], [(_sds((T, D), BF16), row1)], (T // tm,))[0]

    y_ret = _proj_sq("y_ret", yr, *wt("w_ret_o"), "nn")[0]

    def merge_epilogue(acc, erefs, orefs, ids):
        orefs[0][...] = acc
        orefs[1][...] = (erefs[0][...] * erefs[2][...] + erefs[1][...] * acc).astype(BF16)

    y_lru, merged = _proj_sq(
        "y_lru", yl, *wt("w_lru_o"), "nn",
        extras=[(gates, _spec((tm, D), lambda i, j, r: (i, 0))), (gates, _spec((tm, D), lambda i, j, r: (i, 1))),
                (y_ret, rowD)],
        epilogue=merge_epilogue,
        outs=[(_sds((T, D), F32), rowD), (_sds((T, D), BF16), rowD)])

    ex, ou = res_norm_io(x1, sm["xattn_norm"])
    x2, hq = _proj_sq("mix_out", merged, *wt("w_out"), "nn", extras=ex, epilogue=residual_norm, outs=ou)

    m = _rmsnorm("mem_norm", mem, sm["mem_norm"])
    xq = _proj_sq("xq", hq, *wt("w_xq"), "nn", BF16)[0]
    xk = _proj_sq("xk", m, *wt("w_xk"), "nn", BF16)[0]
    xv = _proj_sq("xv", m, *wt("w_xv"), "nn", BF16)[0]
    xo = _xattn_fwd(xq, xk, xv)
    ex, ou = res_norm_io(x2, sm["ffn2_norm"])
    x3, h3 = _proj_sq("xattn_out", xo, *wt("w_xo"), "nn", extras=ex, epilogue=residual_norm, outs=ou)

    a2, b2, s2 = _ffn_up("ffn2_up", h3, *wt("ffn2_w1"), *wt("ffn2_w3"))
    loss, dx4, dg_final = _ffn_down("ffn2_down", s2, *wt("ffn2_w2"), x3, sm["final_norm"], loss_target=tgt)

    dx3, dg_ffn2 = _ffn_bwd("ffn2", dx4, h3, a2, b2, s2, *wt("ffn2_w1"), *wt("ffn2_w3"),
                            *wt("ffn2_w2"), x3, sm["ffn2_norm"], big)

    dxo = _proj_sq("d_xo", dx3, *wt("w_xo"), "nt", BF16)[0]
    big["w_xo"] = _dw_sq("dw_xo", xo, dx3)[None]
    dxq, dxk, dxv = _xattn_bwd(xq, xk, xv, dxo)
    big["w_xq"] = _dw_sq("dw_xq", hq, dxq)[None]
    ex, ou = _rms_bwd_io(x2, sm["xattn_norm"], dx3, T, tm)
    dx2, dg_xattn = _proj_sq("d_hq", dxq, *wt("w_xq"), "nt", extras=ex, epilogue=_rms_bwd_epilogue, outs=ou)
    big["w_xk"] = _dw_sq("dw_xk", m, dxk)[None]
    big["w_xv"] = _dw_sq("dw_xv", m, dxv)[None]

    M = mem.shape[0]

    def mem_norm_epilogue(acc, erefs, orefs, ids):
        _, dgp = _rms_bwd(erefs[0][...], erefs[1][...], acc)
        orefs[0][...] = dgp

    wsq_spec = lambda idx: _spec((N_CHIPS, None, SQ_BLK, D), lambda i, j, r: (0, idx, 0, 0))
    memD = _spec((M, D), row3)
    dg_mem = _gemm(
        "d_mem_norm",
        [(dxk, memD, wt("w_xk")[0], wsq_spec(wt("w_xk")[1]), "nt"),
         (dxv, memD, wt("w_xv")[0], wsq_spec(wt("w_xv")[1]), "nt")],
        (1, 1, 1), [(_sds((1, D), F32), vecD)], (M, D),
        [(mem, memD), (sm["mem_norm"], vecD)], mem_norm_epilogue)[0]

    def merged_bwd_epilogue(acc, erefs, orefs, ids):
        gr, gl, yrv, ylv = (e[...] for e in erefs)
        orefs[0][...] = (acc * gr).astype(BF16)
        orefs[1][...] = (acc * gl).astype(BF16)
        dgr = acc * yrv * gr * (1.0 - gr)
        dgl = acc * ylv * gl * (1.0 - gl)
        orefs[2][:, :D] = dgr.astype(BF16)
        orefs[2][:, D:] = dgl.astype(BF16)
        dbb = jnp.concatenate([jnp.sum(dgr, axis=0, keepdims=True), jnp.sum(dgl, axis=0, keepdims=True)], axis=1)
        _accumulate(orefs[3], dbb, ids[0] == 0)

    dy_ret, dy_lru, dgpre, db_bg = _proj_sq(
        "d_merged", dx2, *wt("w_out"), "nt",
        extras=[(gates, _spec((tm, D), lambda i, j, r: (i, 0))), (gates, _spec((tm, D), lambda i, j, r: (i, 1))),
                (y_ret, rowD), (y_lru, rowD)],
        epilogue=merged_bwd_epilogue,
        outs=[(_sds((T, D), BF16), rowD), (_sds((T, D), BF16), rowD),
              (_sds((T, 2 * D), BF16), _spec((tm, 2 * D), row3)),
              (_sds((1, 2 * D), F32), _spec((1, 2 * D), vec3))])
    big["w_branch_gate"] = _gemm(
        "dw_bg",
        [(h2, _spec((T, D), lambda j, n, r: (r, 0)), dgpre, _spec((T, BG_BLK), lambda j, n, r: (r, j)), "tn")],
        (N_CHIPS, 1, 1),
        [(_sds((N_CHIPS, D, BG_BLK), GRAD_WIRE_DTYPE), _spec((None, D, BG_BLK), lambda j, n, r: (j, 0, 0)))],
        (D, BG_BLK))[0][None]
    big["w_out"] = _dw_sq("dw_out", merged, dx2)[None]
    dyr = _proj_sq("d_yr", dy_ret, *wt("w_ret_o"), "nt")[0]
    big["w_ret_o"] = _dw_sq("dw_ret_o", yr, dy_ret)[None]
    dyl = _proj_sq("d_yl", dy_lru, *wt("w_lru_o"), "nt")[0]
    big["w_lru_o"] = _dw_sq("dw_lru_o", yl, dy_lru)[None]

    def lru_out_bwd(irefs, orefs, ids):
        gl, dgl = _gelu_and_grad(irefs[2][...])
        dyl_v = irefs[0][...]
        orefs[0][...] = dyl_v * gl
        orefs[1][...] = (dyl_v * irefs[1][...] * dgl).astype(BF16)

    dhl, dglru = _rowwise("lru_out_bwd", lru_out_bwd, [(dyl, row1), (hl, row1), (u, glru1)],
                          [(_sds((T, D), F32), row1), (_sds((T, D), BF16), row1)], (T // tm,))
    dhl3 = dhl.reshape(T, LRU_BLOCKS, LRU_BLOCK)
    dq, dk, dv, dgr, dg_retgn = _ret_bwd(dyr, ret, u, qr, kr, states, consts, sm["ret_gn"])
    lmb = _lru_scan("lru_scan_bwd", a3, dhl3, True).reshape(T, D)
    dxl, dw_r, dw_i, dvec, dcw = _lru_gates_bwd(lmb, hl, av, rg, ig, xc, u, conv_w,
                                                sm["w_rgate"], sm["w_igate"], sm["lru_lambda"])

    du = jnp.concatenate([dq, dk, dv, dgr, dxl, dglru], axis=1)
    tk = T
    big["w_in"] = _gemm(
        "dw_in",
        [(h2, _spec((tk, D), lambda j, n, r: (r, 0)), du, _spec((tk, IN_BLK), lambda j, n, r: (r, j)), "tn")],
        (N_CHIPS, 1, T // tk),
        [(_sds((N_CHIPS, D, IN_BLK), GRAD_WIRE_DTYPE), _spec((None, D, IN_BLK), lambda j, n, r: (j, 0, 0)))],
        (D, IN_BLK))[0][None]
    tf = min(FFN_ROW_TILE, T)
    ex, ou = _rms_bwd_io(x1, sm["mix_norm"], dx2, T, tf)
    dx1, dg_mix = _gemm(
        "d_h2",
        [(du, _spec((tf, 5120), row3), gw["win"], _spec((N_CHIPS, None, IN_BLK, D), lambda i, j, r: (0, 0, 0, 0)), "nn"),
         (dgpre, _spec((tf, 2 * D), row3), gw["wbg"], _spec((N_CHIPS, None, BG_BLK, D), lambda i, j, r: (0, 0, 0, 0)),
          "nn")],
        (T // tf, 1, 1), ou, (tf, D), ex, _rms_bwd_epilogue)

    grad_x, dg_ffn1 = _ffn_bwd("ffn1", dx1, h1, a1, b1, s1, *wt("ffn1_w1"), *wt("ffn1_w3"),
                               *wt("ffn1_w2"), x, sm["ffn1_norm"], big)

    small = {
        "ffn1_norm": dg_ffn1, "mix_norm": dg_mix, "ret_gn": dg_retgn, "conv_b": dvec[3:4],
        "b_rgate": dvec[0:1], "b_igate": dvec[1:2], "lru_lambda": dvec[2:3], "xattn_norm": dg_xattn,
        "mem_norm": dg_mem, "ffn2_norm": dg_ffn2, "final_norm": dg_final, "b_branch_gate": db_bg,
        "conv_w": dcw, "w_rgate": dw_r, "w_igate": dw_i,
    }
    return loss, grad_x, small


ANY_SPEC = pl.BlockSpec(memory_space=pl.ANY)
VMEM_SPEC = pl.BlockSpec(memory_space=pltpu.VMEM)
N_PEER_CHIPS = N_CHIPS - 1


def _mesh_position():
    x, y, c = lax.axis_index("x"), lax.axis_index("y"), lax.axis_index("c")
    chips = [(1 - x, y), (x, 1 - y), (1 - x, 1 - y)]
    return x, y, c, chips


def _chip_index(x, y):
    return 2 * x + y


def _rows_half(ref, axis, h):
    n = ref.shape[axis] // 2
    idx = [slice(None)] * len(ref.shape)
    idx[axis] = pl.ds(pl.multiple_of(h * n, BF16_TILE_ROWS), n)
    return ref.at[tuple(idx)]


def _remote(src, dst, send_sem, recv_sem, device):
    return pltpu.make_async_remote_copy(src_ref=src, dst_ref=dst, send_sem=send_sem, recv_sem=recv_sem,
                                        device_id=device, device_id_type=MESH)


def _gather_chips_task(shards, split, landed, legs="both"):
    keys = list(shards)
    n = len(keys)

    def operands():
        if legs == "pass_on":
            return [landed[k] for k in keys]
        chip_me = _chip_index(lax.axis_index("x"), lax.axis_index("y"))
        return [lax.dynamic_update_slice(lax.empty((N_CHIPS,) + shards[k].shape, shards[k].dtype), shards[k][None],
                                         (chip_me,) + (0,) * shards[k].ndim) for k in keys]

    def my_rows(ref, c):
        return _rows_half(ref, 1, c)

    def make_direct(ins, outs, send_sem, recv_sem):
        x, y, c, chips = _mesh_position()
        s_me = _chip_index(x, y)
        starts, arrivals = [], []
        for g in range(n):
            for k, chip in enumerate(chips):
                sems = (send_sem(3 * g + k), recv_sem(3 * g + k))
                starts.append(functools.partial(_remote, outs[g].at[s_me], outs[g].at[s_me], *sems, (*chip, c)))
                got = outs[g].at[_chip_index(*chip)]
                arrivals.append(functools.partial(_remote, got, got, *sems, (*chip, c)))
        return starts, arrivals

    def make_swap(ins, outs, send_sem, recv_sem):
        x, y, c, _ = _mesh_position()
        first, _ = _axis_neighbours(x, y, c)
        starts, arrivals = [], []
        for g in range(n):
            sems = (send_sem(3 * g), recv_sem(3 * g))
            mine = my_rows(outs[g].at[_chip_index(x, y)], c)
            starts.append(functools.partial(_remote, mine, mine, *sems, (*first, c)))
            got = my_rows(outs[g].at[_chip_index(*first)], c)
            arrivals.append(functools.partial(_remote, got, got, *sems, (*first, c)))
        return starts, arrivals

    def make_pass_on(ins, outs, send_sem, recv_sem):
        x, y, c, _ = _mesh_position()
        first, second = _axis_neighbours(x, y, c)
        diagonal = (1 - x, 1 - y)
        starts, arrivals = [], []
        for g in range(n):
            half = lambda chip: my_rows(outs[g].at[_chip_index(*chip)], c)
            for k, (sent, arriving) in enumerate([((x, y), second), (first, diagonal)]):
                sems = (send_sem(3 * g + 1 + k), recv_sem(3 * g + 1 + k))
                starts.append(functools.partial(_remote, half(sent), half(sent), *sems, (*second, c)))
                arrivals.append(functools.partial(_remote, half(arriving), half(arriving), *sems, (*second, c)))
        return starts, arrivals

    def finish(res):
        landed.update(zip(keys, res))

    shapes = lambda: [_sds((N_CHIPS,) + shards[k].shape, shards[k].dtype) for k in keys]
    aliases = {g: g for g in range(n)}
    if not split:
        return _Task("chips", operands, shapes, aliases, 3 * n, make_direct, finish)
    if legs == "swap":
        return _Task("first", operands, shapes, aliases, 3 * n, make_swap, finish)
    if legs == "pass_on":
        return _Task("second", operands, shapes, aliases, 3 * n, make_pass_on, finish)
    return _Task("first+second", operands, shapes, aliases, 3 * n, make_swap, finish, make_second=make_pass_on)


def _gather_sibling_task(keys, landed, ready):
    n = len(keys)

    def make(ins, outs, send_sem, recv_sem):
        x, y, c, chips = _mesh_position()
        starts, arrivals = [], []
        for g in range(n):
            for k, chip in enumerate(chips):
                o = outs[g].at[_chip_index(*chip)]
                got, other = _rows_half(o, 1, c), _rows_half(o, 1, 1 - c)
                starts.append(functools.partial(_remote, got, got, send_sem(3 * g + k), recv_sem(3 * g + k),
                                                (x, y, 1 - c)))
                arrivals.append(functools.partial(_remote, other, other, send_sem(3 * g + k), recv_sem(3 * g + k),
                                                  (x, y, 1 - c)))
        return starts, arrivals

    def finish(res):
        ready.update(zip(keys, res))

    return _Task("sibling", lambda: [landed[k] for k in keys],
                 lambda: [_sds(landed[k].shape, landed[k].dtype) for k in keys],
                 {g: g for g in range(n)}, 3 * n, make, finish)


def _pair_swap_task(names, big, got):
    n = len(names)

    def make(ins, outs, send_sem, recv_sem):
        x, y, c, _ = _mesh_position()
        copies = [functools.partial(_remote, _rows_half(ins[a], 2, 1 - c), outs[a], send_sem(a), recv_sem(a),
                                    (x, y, 1 - c)) for a in range(n)]
        return copies, copies

    def shapes():
        return [_sds(big[k].shape[:2] + (big[k].shape[2] // 2, big[k].shape[3]), big[k].dtype) for k in names]

    return _Task("sibling", lambda: [big[k] for k in names], shapes, {}, n, make,
                 lambda res: got.update(zip(names, res)))


def _rs_pair_sum(name, fulls, gots, core):
    n = len(fulls)
    shapes = [(f.shape[2] // 2, f.shape[3]) for f in fulls]

    def body(core_ref, *refs):
        for a_ref, b_ref, o_ref in zip(refs[:n], refs[n:2 * n], refs[2 * n:]):
            o_ref[...] = (a_ref[...].astype(F32) + b_ref[...].astype(F32)).astype(BF16)

    mine = [pl.BlockSpec((None, None) + hc, lambda s, core_ref: (0, s, core_ref[0], 0)) for hc in shapes]
    slot = [pl.BlockSpec((None, None) + hc, lambda s, core_ref: (0, s, 0, 0)) for hc in shapes]
    return _pcall(
        body, name=name, grid=(N_CHIPS,), num_prefetch=1,
        in_specs=mine + slot, out_specs=slot,
        out_shape=[_sds((1, N_CHIPS) + hc, BF16) for hc in shapes],
    )(core, *fulls, *gots)


def _chip_exchange_task(names, pair_sums, by_source, part=0, nparts=1):
    n = len(names)

    def rows(ref):
        h = ref.shape[1] // nparts
        return ref.at[:, pl.ds(part * h, h), :]

    def make(ins, outs, send_sem, recv_sem):
        x, y, c, chips = _mesh_position()
        s_me = _chip_index(x, y)
        starts, arrivals = [], []
        for a in range(n):
            for k, chip in enumerate(chips):
                s_k = _chip_index(*chip)
                starts.append(functools.partial(_remote, rows(ins[a].at[:, s_k]), rows(outs[a].at[:, s_me]),
                                                send_sem(3 * a + k), recv_sem(3 * a + k), (*chip, c)))
                got = rows(outs[a].at[:, s_k])
                arrivals.append(functools.partial(_remote, got, got, send_sem(3 * a + k), recv_sem(3 * a + k),
                                                  (*chip, c)))
        return starts, arrivals

    def operands():
        return [pair_sums[k] for k in names] + ([by_source[k] for k in names] if part else [])

    return _Task("chips", operands, lambda: [_sds(pair_sums[k].shape, pair_sums[k].dtype) for k in names],
                 {n + a: a for a in range(n)} if part else {}, 3 * n, make,
                 lambda res: by_source.update(zip(names, res)))


def _rs_chip_sum(name, owns, parts, chip):
    n = len(owns)
    ns = N_CHIPS
    shapes = [p.shape[2:] for p in parts]

    def body(chip_ref, *refs):
        me = chip_ref[0]
        for i in range(n):
            own_v = refs[i][...].astype(F32)
            slots = refs[n + ns * i:n + ns * (i + 1)]
            tot = None
            for s in range(ns):
                term = jnp.where(me == s, own_v, slots[s][...].astype(F32))
                tot = term if tot is None else tot + term
            refs[n + ns * n + i][...] = tot

    def slot_spec(hc, s):
        return pl.BlockSpec((None, None) + hc,
                            lambda g, chip_ref: (0, jnp.where(chip_ref[0] == s, (s + 1) % ns, s), 0, 0))

    own_specs = [pl.BlockSpec((None, None) + hc, lambda g, chip_ref: (0, chip_ref[0], 0, 0)) for hc in shapes]
    slot_specs = [slot_spec(hc, s) for hc in shapes for s in range(ns)]
    return _pcall(
        body, name=name, grid=(1,), num_prefetch=1,
        in_specs=own_specs + slot_specs,
        out_specs=[pl.BlockSpec((None,) + hc, lambda g, chip_ref: (0, 0, 0)) for hc in shapes],
        out_shape=[_sds((1,) + hc, F32) for hc in shapes],
    )(chip, *owns, *[p for p in parts for _ in range(ns)])


def _pair_gather_task(names, halves, sibling_halves):
    n = len(names)

    def make(ins, outs, send_sem, recv_sem):
        x, y, c, _ = _mesh_position()
        copies = [functools.partial(_remote, ins[a], outs[a], send_sem(a), recv_sem(a), (x, y, 1 - c))
                  for a in range(n)]
        return copies, copies

    return _Task("sibling", lambda: [halves[k] for k in names], lambda: [_sds(halves[k].shape, F32) for k in names],
                 {}, n, make, lambda res: sibling_halves.update(zip(names, res)))


def _small_allreduce(arrs):
    n = len(arrs)
    per = 1 + 2 * N_PEER_CHIPS

    def body(*refs):
        v_refs, o_refs = refs[:n], refs[n:2 * n]
        sib, pair, part = refs[2 * n:3 * n], refs[3 * n:4 * n], refs[4 * n:5 * n]
        send_sems, recv_sems = refs[5 * n:]
        x, y, c, chips = _mesh_position()
        s_me = _chip_index(x, y)

        def quarter(ref, s):
            q = ref.shape[0] // N_CHIPS
            return ref.at[pl.ds(pl.multiple_of(s * q, F32_TILE_ROWS), q)]

        def exchange(first_sem, src, dst_of, arrival_of):
            sems = lambda a, k: (send_sems.at[a * per + first_sem + k], recv_sems.at[a * per + first_sem + k])
            sends = [_remote(src(a, _chip_index(*chip)), dst_of(a, s_me), *sems(a, k), (*chip, c))
                     for a in range(n) for k, chip in enumerate(chips)]
            for cp in sends:
                cp.start()
            for a in range(n):
                for k, chip in enumerate(chips):
                    got = arrival_of(a, _chip_index(*chip))
                    _remote(got, got, *sems(a, k), (*chip, c)).wait_recv()
            for cp in sends:
                cp.wait_send()

        swaps = [_remote(v_refs[a], sib[a], send_sems.at[a * per], recv_sems.at[a * per], (x, y, 1 - c))
                 for a in range(n)]
        for cp in swaps:
            cp.start()
        for cp in swaps:
            cp.wait()
        for a in range(n):
            pair[a][...] = v_refs[a][...] + sib[a][...]
        exchange(1, lambda a, s_k: quarter(pair[a], s_k), lambda a, s: part[a].at[s], lambda a, s_k: part[a].at[s_k])
        for a in range(n):
            part[a][s_me] = quarter(pair[a], s_me)[...]
            q = o_refs[a].shape[0] // N_CHIPS
            o_refs[a][pl.ds(pl.multiple_of(s_me * q, F32_TILE_ROWS), q), :] = (
                ((part[a][0] + part[a][1]) + part[a][2]) + part[a][3])
        exchange(1 + N_PEER_CHIPS, lambda a, s_k: quarter(o_refs[a], s_me), lambda a, s: quarter(o_refs[a], s),
                 lambda a, s_k: quarter(o_refs[a], s_k))

    shapes = [a.shape for a in arrs]
    return _pcall(
        body, name="small_allreduce", grid=(1,), own_peers=("sibling", "chips"),
        in_specs=[VMEM_SPEC] * n, out_specs=[VMEM_SPEC] * n, out_shape=[_sds(s, F32) for s in shapes],
        scratch_shapes=([pltpu.VMEM(s, F32) for s in shapes] * 2
                        + [pltpu.VMEM((N_CHIPS, s[0] // N_CHIPS, s[1]), F32) for s in shapes]
                        + [pltpu.SemaphoreType.DMA((n * per,)), pltpu.SemaphoreType.DMA((n * per,))]),
    )(*arrs)


TRANSPOSED_WEIGHTS = ("ffn1_w1", "ffn1_w3", "ffn2_w1", "ffn2_w3")
SMALL_LAYOUT = [("ffn1_norm", 1), ("mix_norm", 1), ("ret_gn", 1), ("conv_b", 1), ("b_rgate", 1), ("b_igate", 1),
                ("lru_lambda", 1), ("xattn_norm", 1), ("mem_norm", 1), ("ffn2_norm", 1), ("final_norm", 1),
                ("b_branch_gate", 2), ("conv_w", CONV_TAPS)]
SMALL_ROWS = 32
GATE_WEIGHTS = ("w_rgate", "w_igate")
WEIGHT_ORDER = ["ffn1_norm", "ffn1_w1", "ffn1_w3", "ffn1_w2", "mix_norm", "w_in", "ret_gn", "w_ret_o", "conv_w",
                "conv_b", "w_rgate", "b_rgate", "w_igate", "b_igate", "lru_lambda", "w_lru_o", "w_branch_gate",
                "b_branch_gate", "w_out", "xattn_norm", "mem_norm", "w_xq", "w_xk", "w_xv", "w_xo", "ffn2_norm",
                "ffn2_w1", "ffn2_w3", "ffn2_w2", "final_norm"]


SMALL_USED_ROWS = sum(n for _, n in SMALL_LAYOUT)


def _pack_small(parts, extra_row=None):
    rows = [parts[name].reshape(n, D) for name, n in SMALL_LAYOUT]
    if extra_row is not None:
        rows.append(extra_row)
    rows.append(jnp.zeros((SMALL_ROWS - sum(r.shape[0] for r in rows), D), F32))
    return jnp.concatenate(rows, axis=0)


def _adamw_small(g_pack, ws, ms, vs):
    k = len(SMALL_LAYOUT)

    def body(g_ref, *refs):
        w_refs, m_refs, v_refs = refs[:k], refs[k:2 * k], refs[2 * k:3 * k]
        d_out, m_out, v_out = refs[3 * k:]
        for o_ref in (d_out, m_out, v_out):
            o_ref[...] = jnp.zeros_like(o_ref)
        r = 0
        for i, (_, n) in enumerate(SMALL_LAYOUT):
            delta, mn, vn = _adamw_update(w_refs[i][...], g_ref[r:r + n, :], m_refs[i][...], v_refs[i][...])
            d_out[r:r + n, :] = delta
            m_out[r:r + n, :] = mn
            v_out[r:r + n, :] = vn
            r += n

    full = _spec((SMALL_ROWS, D), lambda i: (0, 0))
    parts = [_spec((n, D), lambda i: (0, 0)) for _, n in SMALL_LAYOUT] * 3
    return _pcall(body, name="adamw_small", grid=(1,), in_specs=[full] + parts, out_specs=[full] * 3,
                  out_shape=[_sds((SMALL_ROWS, D), F32)] * 3)(g_pack, *ws, *ms, *vs)


def _unpack_small(packed, shapes):
    out, r = {}, 0
    for name, n in SMALL_LAYOUT:
        out[name] = packed[r:r + n].reshape(shapes[name])
        r += n
    return out


def kernel(x, mem, ffn1_norm, ffn1_w1, ffn1_w3, ffn1_w2, mix_norm, w_in, ret_gn, w_ret_o, conv_w, conv_b, w_rgate, b_rgate, w_igate, b_igate, lru_lambda, w_lru_o, w_branch_gate, b_branch_gate, w_out, xattn_norm, mem_norm, w_xq, w_xk, w_xv, w_xo, ffn2_norm, ffn2_w1, ffn2_w3, ffn2_w2, final_norm, loss_target, m_ffn1_norm, m_ffn1_w1, m_ffn1_w3, m_ffn1_w2, m_mix_norm, m_w_in, m_ret_gn, m_w_ret_o, m_conv_w, m_conv_b, m_w_rgate, m_b_rgate, m_w_igate, m_b_igate, m_lru_lambda, m_w_lru_o, m_w_branch_gate, m_b_branch_gate, m_w_out, m_xattn_norm, m_mem_norm, m_w_xq, m_w_xk, m_w_xv, m_w_xo, m_ffn2_norm, m_ffn2_w1, m_ffn2_w3, m_ffn2_w2, m_final_norm, v_ffn1_norm, v_ffn1_w1, v_ffn1_w3, v_ffn1_w2, v_mix_norm, v_w_in, v_ret_gn, v_w_ret_o, v_conv_w, v_conv_b, v_w_rgate, v_b_rgate, v_w_igate, v_b_igate, v_lru_lambda, v_w_lru_o, v_w_branch_gate, v_b_branch_gate, v_w_out, v_xattn_norm, v_mem_norm, v_w_xq, v_w_xk, v_w_xv, v_w_xo, v_ffn2_norm, v_ffn2_w1, v_ffn2_w3, v_ffn2_w2, v_final_norm):
    given = dict(locals())
    w = {n: given[n] for n in WEIGHT_ORDER}
    mom = {n: given["m_" + n] for n in WEIGHT_ORDER}
    var = {n: given["v_" + n] for n in WEIGHT_ORDER}
    chip = _chip_index(lax.axis_index("x"), lax.axis_index("y"))
    core = lax.axis_index("c").astype(jnp.int32).reshape(1)

    chip_id = chip.astype(jnp.int32).reshape(1)
    sm = {n: w[n] for n in ["ffn1_norm", "mix_norm", "ret_gn", "conv_b", "b_rgate", "b_igate", "lru_lambda",
                            "xattn_norm", "mem_norm", "ffn2_norm", "b_branch_gate"]}
    sm["final_norm"] = w["final_norm"].reshape(1, D)
    sm["w_rgate"] = w["w_rgate"][0]
    sm["w_igate"] = w["w_igate"][0]

    local = lambda a, n: jnp.swapaxes(a[0], 0, 1) if n in TRANSPOSED_WEIGHTS else a[0]
    stack = lambda names: jnp.stack([local(w[n], n) for n in names], axis=0).astype(BF16)
    shard = {"col1": stack(["ffn1_w1", "ffn1_w3"]), "row2a": stack(["ffn1_w2"]),
             "win": jnp.swapaxes(w["w_in"], 1, 2).astype(BF16),
             "wbg": jnp.swapaxes(w["w_branch_gate"], 1, 2).astype(BF16),
             "sqA": stack(["w_ret_o", "w_lru_o", "w_out"]), "sqB": stack(["w_xq", "w_xk"]),
             "sqC": stack(["w_xv", "w_xo"]), "col2a": stack(["ffn2_w1"]), "col2b": stack(["ffn2_w3"]),
             "row2b": stack(["ffn2_w2"]), "conv": w["conv_w"]}
    gw, landed = {}, {}
    over_chips = lambda keys: _gather_chips_task({k: shard[k] for k in keys}, True, landed)
    to_sibling = lambda keys: _gather_sibling_task(keys, landed, gw)

    big, got, pair_sums, by_source, halves, sibling_halves, outs = {}, {}, {}, {}, {}, {}, {}
    pair_swap = lambda names: _pair_swap_task(names, big, got)
    exchange = lambda names, part=0, nparts=1: _chip_exchange_task(names, pair_sums, by_source, part, nparts)
    pair_gather = lambda names: _pair_gather_task(names, halves, sibling_halves)

    def pair_sum(names):
        res = _rs_pair_sum("rs_pair_sum_" + names[0], [big[n] for n in names], [got[n] for n in names], core)
        pair_sums.update(zip(names, res))

    def chip_sum(names):
        res = _rs_chip_sum("rs_chip_sum_" + names[0], [pair_sums[n] for n in names], [by_source[n] for n in names],
                           chip_id)
        halves.update(zip(names, res))

    def adamw(names):
        for n in names:
            res = _adamw_halves("adamw_" + n, local(w[n], n), halves[n], sibling_halves[n], 0, local(mom[n], n),
                                local(var[n], n), core)
            outs[n] = tuple((jnp.swapaxes(r, 0, 1) if n in TRANSPOSED_WEIGHTS else r)[None] for r in res)

    do = lambda fn, names: functools.partial(fn, names)
    ffn2_grads = ["ffn2_w2", "ffn2_w1", "ffn2_w3"]
    xattn_grads = ["w_xo", "w_xq", "w_xk", "w_xv"]
    mix_out_grads = ["w_branch_gate", "w_out", "w_ret_o", "w_lru_o"]
    conv_gather = _gather_chips_task({"conv": shard["conv"]}, False, gw)
    swap = lambda key: _gather_chips_task({key: shard[key]}, True, landed, legs="swap")
    pass_on = lambda key: _gather_chips_task({key: shard[key]}, True, landed, legs="pass_on")
    plan = _Plan()
    plan.tasks = {
        "ag_first_chips": [over_chips(["col1"]), swap("row2a")],
        "ag_first_sibling": [to_sibling(["col1"]), pass_on("row2a"), swap("win")],
        "ffn1_up": [to_sibling(["row2a"]), pass_on("win"), swap("wbg")],
        "ffn1_down": [to_sibling(["win"]), pass_on("wbg"), swap("sqA")],
        "mix_in": [to_sibling(["wbg"]), pass_on("sqA"), swap("col2a"), conv_gather],
        "ret_fwd": [to_sibling(["sqA"]), pass_on("col2a"), swap("sqB")],
        "lru_gates_fwd": [to_sibling(["col2a"]), pass_on("sqB"), swap("sqC")],
        "mix_gates": [to_sibling(["sqB"]), pass_on("sqC"), swap("col2b")],
        "lru_scan_fwd": [to_sibling(["sqC"]), pass_on("col2b")],
        "y_lru": [to_sibling(["col2b"]), swap("row2b")],
        "ffn2_up": [pass_on("row2b")],
        "ffn2_up_sibling": [to_sibling(["row2b"])],
        "ffn2_dh": [pair_swap(ffn2_grads)],
        "xattn_bwd": [exchange(["ffn2_w2"], 0, 2)],
        "d_hq": [exchange(["ffn2_w2"], 1, 2)],
        "d_merged": [exchange(["ffn2_w1"], 0, 2), pair_swap(xattn_grads)],
        "lru_out_bwd": [exchange(["w_xo"])],
        "ret_bwd": [exchange(["ffn2_w1"], 1, 2), exchange(["ffn2_w3"], 0, 2), pair_swap(mix_out_grads)],
        "lru_scan_bwd": [exchange(["ffn2_w3"], 1, 2)],
        "lru_gates_bwd": [exchange(["w_xq", "w_xk"]), pair_gather(ffn2_grads)],
        "dw_in": [exchange(["w_xv", "w_out"])],
        "d_h2": [exchange(["w_branch_gate", "w_ret_o", "w_lru_o"]), pair_swap(["w_in"]), pair_gather(xattn_grads)],
        "ffn1_bwd_mid": [exchange(["w_in"], 0, 2), pair_gather(mix_out_grads)],
        "ffn1_dw2": [exchange(["w_in"], 2, 4)],
        "ffn1_dw1": [exchange(["w_in"], 3, 4), pair_swap(["ffn1_w2"])],
        "ffn1_dw3": [exchange(["ffn1_w2"], 0, 2), pair_swap(["ffn1_w1"]), pair_gather(["w_in"])],
        "ffn1_dh": [exchange(["ffn1_w2"], 1, 2), exchange(["ffn1_w1"]), pair_swap(["ffn1_w3"])],
        "small_allreduce": [exchange(["ffn1_w3"]), pair_gather(["ffn1_w2"])],
        "adamw_w_rgate": [pair_gather(["ffn1_w1", "ffn1_w3"])],
    }
    plan.after = {
        "ffn2_up": [functools.partial(_comm_call, "ffn2_up_sibling")],
        "ffn2_dh": [do(pair_sum, ffn2_grads)],
        "d_merged": [do(pair_sum, xattn_grads)],
        "ret_bwd": [do(pair_sum, mix_out_grads)],
        "lru_scan_bwd": [do(chip_sum, ffn2_grads)],
        "lru_gates_bwd": [do(adamw, ffn2_grads)],
        "dw_in": [do(chip_sum, xattn_grads)],
        "d_h2": [do(chip_sum, mix_out_grads), do(pair_sum, ["w_in"]), do(adamw, xattn_grads)],
        "ffn1_bwd_mid": [do(adamw, mix_out_grads)],
        "ffn1_dw1": [do(chip_sum, ["w_in"]), do(pair_sum, ["ffn1_w2"])],
        "ffn1_dw3": [do(pair_sum, ["ffn1_w1"]), do(adamw, ["w_in"])],
        "ffn1_dh": [do(pair_sum, ["ffn1_w3"]), do(chip_sum, ["ffn1_w2"])],
        "small_allreduce": [do(chip_sum, ["ffn1_w1", "ffn1_w3"])],
        "adamw_w_rgate": [do(adamw, ["ffn1_w2", "ffn1_w1", "ffn1_w3"])],
    }
    global _plan
    _plan = plan
    try:
        _comm_call("ag_first_chips")
        _comm_call("ag_first_sibling")
        loss_part, grad_x, small = _local_step(x[0], mem[0], loss_target[0], gw, sm, big)
        gate2d = lambda a: a.reshape(LRU_BLOCKS * LRU_BLOCK, LRU_BLOCK)
        loss_row = jnp.pad(loss_part, ((0, 0), (0, D - loss_part.shape[1])))
        small_sum, *gate_sums = _small_allreduce([_pack_small(small, loss_row)]
                                                 + [gate2d(small[n]) for n in GATE_WEIGHTS])
        for n, gsum in zip(GATE_WEIGHTS, gate_sums):
            d, nm, nv = _adamw("adamw_" + n, gate2d(w[n]), gsum, gate2d(mom[n]), gate2d(var[n]))
            outs[n] = tuple(r.reshape(w[n].shape) for r in (gsum, d, nm, nv))
    finally:
        _plan = None
    assert not plan.tasks and not plan.after, (list(plan.tasks), list(plan.after))
    loss = small_sum[SMALL_USED_ROWS, 0]

    small_shapes = {n: w[n].shape for n, _ in SMALL_LAYOUT}
    small_shapes["conv_w"] = (CONV_TAPS, D)
    conv_row = SMALL_USED_ROWS - CONV_TAPS
    conv_grad = lax.dynamic_slice(small_sum[conv_row:conv_row + CONV_TAPS], (0, chip * SQ_BLK), (CONV_TAPS, SQ_BLK))
    small_w = {n: w[n] for n, _ in SMALL_LAYOUT}
    small_m = {n: mom[n] for n, _ in SMALL_LAYOUT}
    small_v = {n: var[n] for n, _ in SMALL_LAYOUT}
    pad_cols = lambda a: jnp.pad(a[0], ((0, 0), (0, D - SQ_BLK)))
    for dct in (small_w, small_m, small_v):
        dct["conv_w"] = pad_cols(dct["conv_w"])
    g_pack = lax.dynamic_update_slice(small_sum, jnp.pad(conv_grad, ((0, 0), (0, D - SQ_BLK))), (conv_row, 0))
    rows_of = lambda dct: [dct[n].reshape(k, D) for n, k in SMALL_LAYOUT]
    d_pack, m_pack, v_pack = _adamw_small(g_pack, rows_of(small_w), rows_of(small_m), rows_of(small_v))
    unpacked = [_unpack_small(p, small_shapes) for p in (g_pack, d_pack, m_pack, v_pack)]
    for n, _ in SMALL_LAYOUT:
        if n == "conv_w":
            outs[n] = tuple(u[n][:, :SQ_BLK][None] for u in unpacked)
        else:
            outs[n] = tuple(u[n] for u in unpacked)

    result = [loss, grad_x[None]]
    for k in range(4):
        result += [outs[n][k] for n in WEIGHT_ORDER]
    return tuple(result)
```

```python
import functools
import math

import jax
import jax.numpy as jnp
from jax import lax
from jax.experimental import pallas as pl
from jax.experimental.pallas import tpu as pltpu

F32 = jnp.float32
BF16 = jnp.bfloat16
GRAD_WIRE_DTYPE = BF16
MESH = pl.DeviceIdType.MESH

D = 1024
EPS = 1e-6
RET_HEADS = 4
RET_DK = 128
RET_DV = 256
CHUNK = 128
ROPE_BASE = 10000.0
LRU_BLOCKS = 8
LRU_BLOCK = 128
CONV_TAPS = 4
LRU_C = 8.0
D_FF = 2816
X_HEADS = 4
X_HD = 256
N_CHIPS = 4
FF_BLK = D_FF // N_CHIPS
IN_BLK = 5120 // N_CHIPS
BG_BLK = 2048 // N_CHIPS
SQ_BLK = D // N_CHIPS

ADAM_LR = 0.001
ADAM_B1 = 0.9
ADAM_B2 = 0.999
ADAM_EPS = 1e-08
ADAM_WD = 0.01
ADAM_STEP = 10

F32_TILE_ROWS = 8
BF16_TILE_ROWS = 16
VMEM_LIMIT_BYTES = 56 * 1024 * 1024
ROW_TILE = 512
WIDE_ROW_TILE = 1024
FFN_ROW_TILE = 256
DW_BLK = D_FF // 2
SCAN_TILE = 256
RET_STEP_CHUNKS = 2
RET_STEP_ROWS = RET_STEP_CHUNKS * CHUNK

_DN = {
    "nn": (((1,), (0,)), ((), ())),
    "nt": (((1,), (1,)), ((), ())),
    "tn": (((0,), (0,)), ((), ())),
}


def _cparams(n_axes, collective_id=None):
    return pltpu.CompilerParams(dimension_semantics=("arbitrary",) * n_axes,
                                vmem_limit_bytes=VMEM_LIMIT_BYTES, collective_id=collective_id)


def _dot(a, b, kind):
    if b.ndim == 3:
        b = b.reshape(b.shape[0] * b.shape[1], b.shape[2])
    return lax.dot_general(a.astype(BF16), b.astype(BF16), _DN[kind], preferred_element_type=F32)


def _sigmoid(x):
    return 1.0 / (1.0 + jnp.exp(-x))


def _log1p_pos(e):
    u = 1.0 + e
    return jnp.where(u == 1.0, e, jnp.log(u) * (e / jnp.where(u == 1.0, 1.0, u - 1.0)))


def _expm1(x):
    u = jnp.exp(x)
    lu = jnp.log(u)
    safe = jnp.where(lu == 0.0, 1.0, lu)
    return jnp.where(u == 1.0, x, (u - 1.0) * (x / safe))


def _softplus(z):
    return jnp.maximum(z, 0.0) + _log1p_pos(jnp.exp(-jnp.abs(z)))


_GELU_C = math.sqrt(2.0 / math.pi)


def _gelu_and_grad(x):
    x2 = x * x
    t = jnp.tanh(_GELU_C * (x + 0.044715 * x * x2))
    g = 0.5 * x * (1.0 + t)
    dg = 0.5 * (1.0 + t) + 0.5 * x * (1.0 - t * t) * (_GELU_C * (1.0 + 3.0 * 0.044715 * x2))
    return g, dg


def _rms_fwd(x, g):
    r = lax.rsqrt(jnp.mean(x * x, axis=-1, keepdims=True) + EPS)
    return (x * r) * g


def _rms_bwd(x, g, dh):
    r = lax.rsqrt(jnp.mean(x * x, axis=-1, keepdims=True) + EPS)
    n = x * r
    dyg = dh * g
    dx = r * (dyg - n * jnp.mean(dyg * n, axis=-1, keepdims=True))
    return dx, jnp.sum(dh * n, axis=0, keepdims=True)


def _accumulate(ref, val, first):
    @pl.when(first)
    def _():
        ref[...] = val

    @pl.when(jnp.logical_not(first))
    def _():
        ref[...] += val


def _sds(shape, dtype):
    return jax.ShapeDtypeStruct(tuple(shape), dtype)


def _spec(shape, fn):
    return pl.BlockSpec(tuple(shape), fn)


class _Task:
    def __init__(self, peers, operands, out_shapes, aliases, nsem, make, finish, make_second=None):
        self.peers = peers
        self.operands, self.out_shapes, self.aliases = operands, out_shapes, aliases
        self.nsem, self.make, self.finish = nsem, make, finish
        self.make_second = make_second


class _Plan:
    def __init__(self):
        self.tasks, self.after = {}, {}


_plan = None


_CHIP_PEER_SETS = [frozenset({"chips"}), frozenset({"first"}), frozenset({"second"}), frozenset({"first", "second"})]
PEER_SET_COLLECTIVE_ID = {frozenset({"sibling"}): 1}
for _i, _chip_peers in enumerate(_CHIP_PEER_SETS):
    PEER_SET_COLLECTIVE_ID[_chip_peers] = 2 + 2 * _i
    PEER_SET_COLLECTIVE_ID[_chip_peers | {"sibling"}] = 3 + 2 * _i


def _peer_set(names):
    names = frozenset(n for name in names for n in name.split("+"))
    return names - {"first", "second"} if "chips" in names else names


def _axis_neighbours(x, y, c):
    flip = lambda v, f: v + f * (1 - 2 * v)
    return (flip(x, 1 - c), flip(y, c)), (flip(x, c), flip(y, 1 - c))


def _entry_handshake(peer_set):
    x, y, c, chips = _mesh_position()
    first, second = _axis_neighbours(x, y, c)
    peers = [(x, y, 1 - c)] if "sibling" in peer_set else []
    if "chips" in peer_set:
        peers += [(*chip, c) for chip in chips]
    if "first" in peer_set:
        peers.append((*first, c))
    if "second" in peer_set:
        peers.append((*second, c))
    barrier = pltpu.get_barrier_semaphore()
    for peer in peers:
        pl.semaphore_signal(barrier, inc=1, device_id=peer, device_id_type=MESH)
    pl.semaphore_wait(barrier, len(peers))


def _pcall(body, *, name, grid, in_specs, out_specs, out_shape, scratch_shapes=(), num_prefetch=0, own_peers=()):
    single = not isinstance(out_shape, (list, tuple))
    out_shape = [out_shape] if single else list(out_shape)
    out_specs = [out_specs] if single else list(out_specs)
    in_specs = list(in_specs)
    scratch_shapes = list(scratch_shapes)
    tasks = _plan.tasks.pop(name, []) if _plan is not None else []
    after = _plan.after.pop(name, []) if _plan is not None else []
    peer_set = _peer_set([t.peers for t in tasks] + list(own_peers))
    nax = len(grid)

    def run(*operands):
        n_in = len(operands) - num_prefetch
        n_out = len(out_shape)
        t_ops = [t.operands() for t in tasks]
        t_outs = [t.out_shapes() for t in tasks]
        c_ops = [a for ops in t_ops for a in ops]
        c_outs = [s for outs in t_outs for s in outs]
        aliases = {}
        i0, o0 = num_prefetch + n_in, n_out
        for t, ops, outs in zip(tasks, t_ops, t_outs):
            for i_loc, o_loc in t.aliases.items():
                aliases[i0 + i_loc] = o0 + o_loc
            i0 += len(ops)
            o0 += len(outs)
        nsem = sum(t.nsem for t in tasks)

        def wrapped(*refs):
            p = num_prefetch
            pre, ins = refs[:p], refs[p:p + n_in]
            cins = refs[p + n_in:p + n_in + len(c_ops)]
            q = p + n_in + len(c_ops)
            outs, couts = refs[q:q + n_out], refs[q + n_out:q + n_out + len(c_outs)]
            q += n_out + len(c_outs)
            scr = refs[q:q + len(scratch_shapes)]

            def rounds(second):
                send_sems, recv_sems = refs[q + len(scratch_shapes):]
                out = []
                ci = co = so = 0
                for t, ops, souts in zip(tasks, t_ops, t_outs):
                    make = t.make_second if second else t.make
                    out.append(([], []) if make is None else
                               make(cins[ci:ci + len(ops)], couts[co:co + len(souts)],
                                    functools.partial(lambda base, k: send_sems.at[base + k], so),
                                    functools.partial(lambda base, k: recv_sems.at[base + k], so)))
                    ci, co, so = ci + len(ops), co + len(souts), so + t.nsem
                return out

            two_rounds = [t.make_second is not None for t in tasks]
            if peer_set:
                ids = [pl.program_id(k) for k in range(nax)]
                first = functools.reduce(jnp.logical_and, [i == 0 for i in ids])
                last = functools.reduce(jnp.logical_and, [i == g - 1 for i, g in zip(ids, grid)])
                step = functools.reduce(lambda acc, ig: acc * ig[1] + ig[0], zip(ids, grid), 0)
                middle = step == math.prod(grid) // 3

                @pl.when(first)
                def _():
                    _entry_handshake(peer_set)
                    for starts, _ in rounds(False):
                        for copy in starts:
                            copy().start()

            body(*pre, *ins, *outs, *scr)

            if any(two_rounds):
                @pl.when(middle)
                def _():
                    for (_, arrivals), two in zip(rounds(False), two_rounds):
                        if two:
                            for arrival in arrivals:
                                arrival().wait_recv()
                    for starts, _ in rounds(True):
                        for copy in starts:
                            copy().start()

            if tasks:
                @pl.when(last)
                def _():
                    first_round, second_round = rounds(False), rounds(True)
                    for (_, arrivals1), (_, arrivals2), two in zip(first_round, second_round, two_rounds):
                        for arrival in (arrivals2 if two else arrivals1):
                            arrival().wait_recv()
                    for starts, _ in first_round + second_round:
                        for copy in starts:
                            copy().wait_send()

        sems = [pltpu.SemaphoreType.DMA((nsem,)), pltpu.SemaphoreType.DMA((nsem,))] if tasks else []
        res = pl.pallas_call(
            wrapped, name=name,
            grid_spec=pltpu.PrefetchScalarGridSpec(
                num_scalar_prefetch=num_prefetch, grid=tuple(grid),
                in_specs=in_specs + [ANY_SPEC] * len(c_ops),
                out_specs=out_specs + [ANY_SPEC] * len(c_outs),
                scratch_shapes=scratch_shapes + sems),
            out_shape=out_shape + c_outs,
            input_output_aliases=aliases,
            compiler_params=_cparams(nax, PEER_SET_COLLECTIVE_ID[peer_set] if peer_set else None),
        )(*operands, *c_ops)
        co = n_out
        for t, souts in zip(tasks, t_outs):
            t.finish(res[co:co + len(souts)])
            co += len(souts)
        for fn in after:
            fn()
        return res[0] if single else list(res[:n_out])

    return run


def _comm_call(name):
    def body(o_ref):
        o_ref[...] = jnp.zeros_like(o_ref)

    _pcall(body, name=name, grid=(1,), in_specs=[], out_specs=_spec((8, 128), lambda i: (0, 0)),
           out_shape=_sds((8, 128), F32))()


def _gemm(name, terms, grid, outs, acc_shape, extras=(), epilogue=None):
    kinds = [t[4] for t in terms]
    nt, ne, no = len(terms), len(extras), len(outs)
    nred = grid[-1]
    nax = len(grid)

    def body(*refs):
        trefs = refs[:2 * nt]
        erefs = refs[2 * nt:2 * nt + ne]
        orefs = refs[2 * nt + ne:2 * nt + ne + no]
        ids = [pl.program_id(k) for k in range(nax)]
        tot = None
        for t in range(nt):
            d = _dot(trefs[2 * t][...], trefs[2 * t + 1][...], kinds[t])
            tot = d if tot is None else tot + d

        def finish(acc):
            if epilogue is None:
                orefs[0][...] = acc.astype(orefs[0].dtype)
            else:
                epilogue(acc, erefs, orefs, ids)

        if nred == 1:
            finish(tot)
        else:
            acc_ref = refs[-1]
            r = ids[-1]

            @pl.when(r == 0)
            def _():
                acc_ref[...] = tot

            @pl.when(r > 0)
            def _():
                acc_ref[...] += tot

            @pl.when(r == nred - 1)
            def _():
                finish(acc_ref[...])

    operands, in_specs = [], []
    for a, a_spec, b, b_spec, _ in terms:
        operands += [a, b]
        in_specs += [a_spec, b_spec]
    for e, e_spec in extras:
        operands.append(e)
        in_specs.append(e_spec)
    scratch = [pltpu.VMEM(tuple(acc_shape), F32)] if nred > 1 else []
    return _pcall(body, name=name, grid=tuple(grid), in_specs=in_specs, out_specs=[o[1] for o in outs],
                  out_shape=[o[0] for o in outs], scratch_shapes=scratch)(*operands)


def _rowwise(name, fn, ins, outs, grid):
    ni = len(ins)
    nax = len(grid)

    def body(*refs):
        ids = [pl.program_id(k) for k in range(nax)]
        fn(refs[:ni], refs[ni:], ids)

    return _pcall(body, name=name, grid=tuple(grid), in_specs=[i[1] for i in ins],
                  out_specs=[o[1] for o in outs], out_shape=[o[0] for o in outs])(*[i[0] for i in ins])


def _ffn_up(name, h, w1buf, w1_idx, w3buf, w3_idx, norm_gain=None):
    T = h.shape[0]
    tm = min(FFN_ROW_TILE, T)
    normed = norm_gain is not None

    def body(h_ref, *refs):
        if normed:
            g_ref, w1_ref, w3_ref, a_ref, b_ref, s_ref, hn_ref = refs
            hv = _rms_fwd(h_ref[...], g_ref[...]).astype(BF16)
            hn_ref[...] = hv
        else:
            w1_ref, w3_ref, a_ref, b_ref, s_ref = refs
            hv = h_ref[...]
        a = _dot(hv, w1_ref[...], "nt")
        b = _dot(hv, w3_ref[...], "nt")
        a_ref[...] = a.astype(BF16)
        b_ref[...] = b.astype(BF16)
        s_ref[...] = ((a * _sigmoid(a)) * b).astype(BF16)

    row = _spec((tm, D), lambda i: (i, 0))
    blk = _spec((tm, D_FF), lambda i: (i, 0))
    return _pcall(
        body, name=name, grid=(T // tm,),
        in_specs=[row] + ([_spec((1, D), lambda i: (0, 0))] if normed else [])
        + [_spec((N_CHIPS, None, FF_BLK, D), lambda i: (0, w1_idx, 0, 0)),
           _spec((N_CHIPS, None, FF_BLK, D), lambda i: (0, w3_idx, 0, 0))],
        out_specs=[blk, blk, blk] + ([row] if normed else []),
        out_shape=[_sds((T, D_FF), BF16)] * 3 + ([_sds((T, D), BF16)] if normed else []),
    )(h, *([norm_gain] if normed else []), w1buf, w3buf)


def _loss_head(x, g, tgt, loss_ref, dx_ref, dg_ref, first):
    err = _rms_fwd(x, g) - tgt
    lp = 0.5 * jnp.sum(jnp.mean(err * err, axis=-1, keepdims=True), axis=0, keepdims=True)
    _accumulate(loss_ref, jnp.broadcast_to(lp, (1, 128)), first)
    dx, dgp = _rms_bwd(x, g, err * (1.0 / D))
    dx_ref[...] = dx
    _accumulate(dg_ref, dgp, first)


def _ffn_down(name, s, wrow2, w2_idx, x_res, g_next=None, loss_target=None):
    T = x_res.shape[0]
    tm = min(ROW_TILE, T)
    row = lambda i, j, r: (i, 0)
    vec = lambda i, j, r: (0, 0)

    def epilogue(acc, erefs, orefs, ids):
        xo = erefs[0][...] + 0.5 * acc
        if loss_target is not None:
            _loss_head(xo, erefs[1][...], erefs[2][...], orefs[0], orefs[1], orefs[2], ids[0] == 0)
            return
        orefs[0][...] = xo
        orefs[1][...] = _rms_fwd(xo, erefs[1][...]).astype(BF16)

    extras = [(x_res, _spec((tm, D), row)), (g_next, _spec((1, D), vec))]
    if loss_target is None:
        outs = [(_sds((T, D), F32), _spec((tm, D), row)), (_sds((T, D), BF16), _spec((tm, D), row))]
    else:
        extras.append((loss_target, _spec((tm, D), row)))
        outs = [(_sds((1, 128), F32), _spec((1, 128), vec)), (_sds((T, D), F32), _spec((tm, D), row)),
                (_sds((1, D), F32), _spec((1, D), vec))]
    return _gemm(
        name,
        [(s, _spec((tm, D_FF), row),
          wrow2, _spec((N_CHIPS, None, FF_BLK, D), lambda i, j, r: (0, w2_idx, 0, 0)), "nn")],
        (T // tm, 1, 1), outs, (tm, D), extras, epilogue)


def _ffn_bwd_mid(name, dx, wrow2, w2_idx, a, b):
    T = dx.shape[0]
    tm = min(FFN_ROW_TILE, T)

    def body(dx_ref, w2_ref, a_ref, b_ref, dab_ref):
        ds = _dot(0.5 * dx_ref[...], w2_ref[...], "nt")
        av = a_ref[...].astype(F32)
        sg = _sigmoid(av)
        dab_ref[0] = (ds * b_ref[...].astype(F32) * (sg * (1.0 + av * (1.0 - sg)))).astype(BF16)
        dab_ref[1] = (ds * (av * sg)).astype(BF16)

    blk = _spec((tm, D_FF), lambda i: (i, 0))
    return _pcall(
        body, name=name, grid=(T // tm,),
        in_specs=[_spec((tm, D), lambda i: (i, 0)),
                  _spec((N_CHIPS, None, FF_BLK, D), lambda i: (0, w2_idx, 0, 0)),
                  blk, blk],
        out_specs=_spec((2, tm, D_FF), lambda i: (0, i, 0)),
        out_shape=_sds((2, T, D_FF), BF16),
    )(dx, wrow2, a, b)


def _rms_bwd_epilogue(acc, erefs, orefs, ids):
    dx, dgp = _rms_bwd(erefs[0][...], erefs[1][...], acc)
    orefs[0][...] = dx + erefs[2][...]
    _accumulate(orefs[1], dgp, ids[0] == 0)


def _rms_bwd_io(x, g, dres, T, tm):
    row = lambda i, j, r: (i, 0)
    vec = lambda i, j, r: (0, 0)
    extras = [(x, _spec((tm, D), row)), (g, _spec((1, D), vec)), (dres, _spec((tm, D), row))]
    outs = [(_sds((T, D), F32), _spec((tm, D), row)), (_sds((1, D), F32), _spec((1, D), vec))]
    return extras, outs


def _ffn_bwd(tag, dx_out, h, a, b, s, w1buf, w1_idx, w3buf, w3_idx, wrow2, w2_idx, x_in, g, big):
    T = dx_out.shape[0]
    dab = _ffn_bwd_mid(tag + "_bwd_mid", dx_out, wrow2, w2_idx, a, b)

    def half_scale(acc, erefs, orefs, ids):
        orefs[0][...] = (0.5 * acc).astype(orefs[0].dtype)

    dw_grid = (D_FF // DW_BLK, 1, 1)
    dw_out = [(_sds((D_FF, D), GRAD_WIRE_DTYPE), _spec((DW_BLK, D), lambda j, n, r: (j, 0)))]
    tokens = _spec((T, D), lambda j, n, r: (0, 0))
    big[tag + "_w2"] = _gemm(
        tag + "_dw2", [(s, _spec((T, DW_BLK), lambda j, n, r: (0, j)), dx_out, tokens, "tn")],
        dw_grid, dw_out, (DW_BLK, D), (), half_scale)[0].reshape(1, N_CHIPS, FF_BLK, D)
    for widx, wname in ((0, "_w1"), (1, "_w3")):
        big[tag + wname] = _gemm(
            tag + "_d" + wname[1:],
            [(dab, _spec((None, T, DW_BLK), functools.partial(lambda w, j, n, r: (w, 0, j), widx)), h, tokens, "tn")],
            dw_grid, dw_out, (DW_BLK, D))[0].reshape(1, N_CHIPS, FF_BLK, D)
    tm = min(FFN_ROW_TILE, T)
    extras, outs = _rms_bwd_io(x_in, g, dx_out, T, tm)
    whole = lambda idx: _spec((N_CHIPS, None, FF_BLK, D), lambda i, j, r: (0, idx, 0, 0))
    dx_in, dg = _gemm(
        tag + "_dh",
        [(dab, _spec((None, tm, D_FF), lambda i, j, r: (0, i, 0)), w1buf, whole(w1_idx), "nn"),
         (dab, _spec((None, tm, D_FF), lambda i, j, r: (1, i, 0)), w3buf, whole(w3_idx), "nn")],
        (T // tm, 1, 1), outs, (tm, D), extras, _rms_bwd_epilogue)
    return dx_in, dg


def _proj_sq(name, a, wsq, idx, kind, out_dtype=F32, extras=(), epilogue=None, outs=None):
    M = a.shape[0]
    tm = min(ROW_TILE, M)
    if outs is None:
        outs = [(_sds((M, D), out_dtype), _spec((tm, D), lambda i, j, r: (i, 0)))]
    return _gemm(
        name,
        [(a, _spec((tm, D), lambda i, j, r: (i, 0)),
          wsq, _spec((N_CHIPS, None, SQ_BLK, D), lambda i, j, r: (0, idx, 0, 0)), kind)],
        (M // tm, 1, 1), outs, (tm, D), extras, epilogue)


def _dw_sq(name, a, b):
    M = a.shape[0]
    tn = D // 2
    whole = _gemm(
        name,
        [(a, _spec((M, D), lambda i, j, r: (0, 0)), b, _spec((M, tn), lambda i, j, r: (0, j)), "tn")],
        (1, D // tn, 1),
        [(_sds((D, D), GRAD_WIRE_DTYPE), _spec((D, tn), lambda i, j, r: (0, j)))],
        (D, tn))[0]
    return whole.reshape(N_CHIPS, SQ_BLK, D)


def _retention_constants(T):
    pos = jnp.arange(T, dtype=F32)
    inv_freq = ROPE_BASE ** (-jnp.arange(0, RET_DK, 2, dtype=F32) / RET_DK)
    ang = pos[:, None] * inv_freq[None, :]
    cosf = jnp.concatenate([jnp.cos(ang), jnp.cos(ang)], axis=1)
    sins = jnp.concatenate([-jnp.sin(ang), jnp.sin(ang)], axis=1)
    lg = jnp.log(1.0 - 2.0 ** (-5.0 - jnp.arange(RET_HEADS, dtype=F32)))
    p = jnp.arange(CHUNK, dtype=F32)
    rel = p[:, None] - p[None, :]
    dmat = jnp.where(rel[None] >= 0, jnp.exp(rel[None] * lg[:, None, None]), 0.0)
    kd = jnp.exp((CHUNK - 1.0 - p)[None, :] * lg[:, None])[:, :, None]
    qd = jnp.exp((p + 1.0)[None, :] * lg[:, None])[:, :, None]
    cd = jnp.exp(CHUNK * lg)[:, None, None]
    return cosf, sins, dmat, kd, qd, cd


def _rot(t, cosv, sinv):
    return t * cosv + pltpu.roll(t, RET_DK // 2, 1) * sinv


def _unrot(t, cosv, sinv):
    return t * cosv - pltpu.roll(t, RET_DK // 2, 1) * sinv


def _ret_const_specs(cm):
    whole = lambda shape: _spec(shape, lambda c: (0,) * len(shape))
    return [
        _spec((RET_STEP_ROWS, RET_DK), lambda c: (cm(c), 0)),
        _spec((RET_STEP_ROWS, RET_DK), lambda c: (cm(c), 0)),
        whole((RET_HEADS, CHUNK, CHUNK)), whole((RET_HEADS, CHUNK, 1)), whole((RET_HEADS, CHUNK, 1)),
        whole((RET_HEADS, 1, 1)),
    ]


def _head(h, width):
    return slice(h * width, (h + 1) * width)


def _ret_fwd(u, consts, ret_gn):
    T = u.shape[0]
    nC = T // CHUNK
    kscale = RET_DK ** -0.5

    def body(q_ref, k_ref, v_ref, g_ref, cos_ref, sin_ref, dm_ref, kd_ref, qd_ref, cd_ref, gn_ref,
             qr_ref, kr_ref, ret_ref, yr_ref, st_ref, state):
        @pl.when(pl.program_id(0) == 0)
        def _():
            state[...] = jnp.zeros_like(state)

        for cc in range(RET_STEP_CHUNKS):
            rows = slice(cc * CHUNK, (cc + 1) * CHUNK)
            cosv, sinv = cos_ref[rows, :], sin_ref[rows, :]
            for h in range(RET_HEADS):
                hk, hv = _head(h, RET_DK), _head(h, RET_DV)
                q = _rot(q_ref[rows, hk], cosv, sinv)
                k = _rot(k_ref[rows, hk], cosv, sinv) * kscale
                v = v_ref[rows, hv]
                qr_ref[rows, hk] = q
                kr_ref[rows, hk] = k
                prev = state[h]
                st_ref[h, cc] = prev
                s = _dot(q, k, "nt") * dm_ref[h]
                ret = _dot(s, v, "nn") + _dot(q, prev, "nn") * qd_ref[h]
                state[h] = cd_ref[h] * prev + _dot(k * kd_ref[h], v, "tn")
                ret_ref[rows, hv] = ret
                mu = jnp.mean(ret, axis=-1, keepdims=True)
                xc = ret - mu
                yn = xc * lax.rsqrt(jnp.mean(xc * xc, axis=-1, keepdims=True) + EPS)
                g = g_ref[rows, hv]
                yr_ref[rows, hv] = ((g * _sigmoid(g)) * (yn * gn_ref[:, hv])).astype(BF16)

    cm = lambda c: c
    qk_w, v_w = RET_HEADS * RET_DK, RET_HEADS * RET_DV
    in_specs = [
        _spec((RET_STEP_ROWS, qk_w), lambda c: (c, 0)), _spec((RET_STEP_ROWS, qk_w), lambda c: (c, 1)),
        _spec((RET_STEP_ROWS, v_w), lambda c: (c, 1)), _spec((RET_STEP_ROWS, v_w), lambda c: (c, 2)),
    ] + _ret_const_specs(cm) + [_spec((1, v_w), lambda c: (0, 0))]
    qk_out = _spec((RET_STEP_ROWS, qk_w), lambda c: (c, 0))
    v_out = _spec((RET_STEP_ROWS, v_w), lambda c: (c, 0))
    return _pcall(
        body, name="ret_fwd", grid=(nC // RET_STEP_CHUNKS,),
        in_specs=in_specs,
        out_specs=[qk_out, qk_out, v_out, v_out,
                   _spec((RET_HEADS, RET_STEP_CHUNKS, RET_DK, RET_DV), lambda c: (0, c, 0, 0))],
        out_shape=[_sds((T, qk_w), F32), _sds((T, qk_w), F32), _sds((T, v_w), F32), _sds((T, v_w), BF16),
                   _sds((RET_HEADS, nC, RET_DK, RET_DV), F32)],
        scratch_shapes=[pltpu.VMEM((RET_HEADS, RET_DK, RET_DV), F32)],
    )(u, u, u, u, *consts, ret_gn)


def _ret_bwd(dyr, ret, u, qr, kr, states, consts, ret_gn):
    T = u.shape[0]
    nC = T // CHUNK
    kscale = RET_DK ** -0.5

    def body(dyr_ref, ret_ref, g_ref, q_ref, k_ref, v_ref, st_ref,
             cos_ref, sin_ref, dm_ref, kd_ref, qd_ref, cd_ref, gn_ref,
             dq_ref, dk_ref, dv_ref, dg_ref, dgn_ref, gstate):
        first = pl.program_id(0) == 0

        @pl.when(first)
        def _():
            gstate[...] = jnp.zeros_like(gstate)

        dgn_total = None
        for cc in reversed(range(RET_STEP_CHUNKS)):
            rows = slice(cc * CHUNK, (cc + 1) * CHUNK)
            cosv, sinv = cos_ref[rows, :], sin_ref[rows, :]
            dgn_parts = []
            for h in range(RET_HEADS):
                hk, hv = _head(h, RET_DK), _head(h, RET_DV)
                ret = ret_ref[rows, hv]
                mu = jnp.mean(ret, axis=-1, keepdims=True)
                xc = ret - mu
                rs = lax.rsqrt(jnp.mean(xc * xc, axis=-1, keepdims=True) + EPS)
                yn = xc * rs
                gn = gn_ref[:, hv]
                g = g_ref[rows, hv]
                sg = _sigmoid(g)
                dyr_v = dyr_ref[rows, hv]
                dretn = dyr_v * (g * sg)
                dg_ref[rows, hv] = (dyr_v * (yn * gn) * (sg * (1.0 + g * (1.0 - sg)))).astype(BF16)
                dgn_parts.append(jnp.sum(dretn * yn, axis=0, keepdims=True))
                dyn = dretn * gn
                d_o = rs * (dyn - jnp.mean(dyn, axis=-1, keepdims=True)
                            - yn * jnp.mean(dyn * yn, axis=-1, keepdims=True))

                q, k, v = q_ref[rows, hk], k_ref[rows, hk], v_ref[rows, hv]
                dmat, kd, qd = dm_ref[h], kd_ref[h], qd_ref[h]
                prev = st_ref[h, cc]
                gnext = gstate[h]
                s = _dot(q, k, "nt") * dmat
                ds = _dot(d_o, v, "nt") * dmat
                doq = d_o * qd
                dq = _dot(ds, k, "nn") + _dot(doq, prev, "nt")
                dk = _dot(ds, q, "tn") + _dot(v, gnext, "nt") * kd
                dv = _dot(s, d_o, "tn") + _dot(k * kd, gnext, "nn")
                gstate[h] = cd_ref[h] * gnext + _dot(q, doq, "tn")
                dq_ref[rows, hk] = _unrot(dq, cosv, sinv).astype(BF16)
                dk_ref[rows, hk] = _unrot(dk * kscale, cosv, sinv).astype(BF16)
                dv_ref[rows, hv] = dv.astype(BF16)
            dgn = jnp.concatenate(dgn_parts, axis=1)
            dgn_total = dgn if dgn_total is None else dgn_total + dgn
        _accumulate(dgn_ref, dgn_total, first)

    n_steps = nC // RET_STEP_CHUNKS
    cm = lambda c: n_steps - 1 - c
    qk_w, v_w = RET_HEADS * RET_DK, RET_HEADS * RET_DV
    vspec = lambda blk: _spec((RET_STEP_ROWS, v_w), lambda c: (cm(c), blk))
    qspec = _spec((RET_STEP_ROWS, qk_w), lambda c: (cm(c), 0))
    in_specs = [vspec(0), vspec(0), vspec(2), qspec, qspec, vspec(1),
                _spec((RET_HEADS, RET_STEP_CHUNKS, RET_DK, RET_DV), lambda c: (0, cm(c), 0, 0)),
                ] + _ret_const_specs(cm) + [_spec((1, v_w), lambda c: (0, 0))]
    return _pcall(
        body, name="ret_bwd", grid=(n_steps,),
        in_specs=in_specs,
        out_specs=[qspec, qspec, vspec(0), vspec(0), _spec((1, v_w), lambda c: (0, 0))],
        out_shape=[_sds((T, qk_w), BF16), _sds((T, qk_w), BF16), _sds((T, v_w), BF16), _sds((T, v_w), BF16),
                   _sds((1, v_w), F32)],
        scratch_shapes=[pltpu.VMEM((RET_HEADS, RET_DK, RET_DV), F32)],
    )(dyr, ret, u, qr, kr, u, states, *consts, ret_gn)


def _shift_down(x, s):
    rows = lax.broadcasted_iota(jnp.int32, x.shape, 0)
    return jnp.where(rows >= s, pltpu.roll(x, s, 0), 0.0)


def _shift_up(x, s):
    n = x.shape[0]
    rows = lax.broadcasted_iota(jnp.int32, x.shape, 0)
    return jnp.where(rows < n - s, pltpu.roll(x, n - s, 0), 0.0)


def _lru_specs(T):
    col = lambda off: _spec((T, LRU_BLOCK), lambda g: (0, off + g))
    vec = _spec((1, LRU_BLOCK), lambda g: (0, g))
    wblk = _spec((None, LRU_BLOCK, LRU_BLOCK), lambda g: (g, 0, 0))
    cw = _spec((CONV_TAPS, LRU_BLOCK), lambda g: (0, g))
    return col, vec, wblk, cw


def _lru_gates_fwd(u, conv_w, conv_b, w_r, b_r, w_i, b_i, lam):
    T = u.shape[0]
    col, vec, wblk, cw = _lru_specs(T)

    def body(x_ref, cw_ref, cb_ref, wr_ref, br_ref, wi_ref, bi_ref, lam_ref,
             xc_ref, r_ref, i_ref, a_ref, bx_ref):
        x = x_ref[...]
        w = cw_ref[...]
        xc = (_shift_down(x, 3) * w[0:1] + _shift_down(x, 2) * w[1:2] + _shift_down(x, 1) * w[2:3]
              + x * w[3:4] + cb_ref[...])
        r = _sigmoid(_dot(xc, wr_ref[...], "nn") + br_ref[...])
        i = _sigmoid(_dot(xc, wi_ref[...], "nn") + bi_ref[...])
        la = (-LRU_C) * r * _softplus(-lam_ref[...])
        xc_ref[...] = xc
        r_ref[...] = r
        i_ref[...] = i
        a_ref[...] = jnp.exp(la)
        bx_ref[...] = jnp.sqrt(-_expm1(2.0 * la)) * (i * xc)

    out = col(0)
    return _pcall(
        body, name="lru_gates_fwd", grid=(LRU_BLOCKS,),
        in_specs=[col(24), cw, vec, wblk, vec, wblk, vec, vec],
        out_specs=[out] * 5,
        out_shape=[_sds((T, D), F32)] * 5,
    )(u, conv_w, conv_b, w_r, b_r, w_i, b_i, lam)


def _lru_scan(name, a3, b3, reverse):
    T = a3.shape[0]
    nt = T // SCAN_TILE
    unroll = 8

    def body(a_ref, b_ref, o_ref, carry):
        @pl.when(pl.program_id(0) == 0)
        def _():
            carry[...] = jnp.zeros_like(carry)

        if not reverse:
            def step(t, h):
                h = a_ref[t] * h + b_ref[t]
                o_ref[t] = h
                return h
        else:
            def step(k, c):
                t = SCAN_TILE - 1 - k
                l = b_ref[t] + c
                o_ref[t] = l
                return a_ref[t] * l
        carry[...] = lax.fori_loop(0, SCAN_TILE, step, carry[...], unroll=unroll)

    idx = (lambda i: (nt - 1 - i, 0, 0)) if reverse else (lambda i: (i, 0, 0))
    blk = _spec((SCAN_TILE, LRU_BLOCKS, LRU_BLOCK), idx)
    return _pcall(
        body, name=name, grid=(nt,),
        in_specs=[blk, blk], out_specs=blk,
        out_shape=_sds((T, LRU_BLOCKS, LRU_BLOCK), F32),
        scratch_shapes=[pltpu.VMEM((LRU_BLOCKS, LRU_BLOCK), F32)],
    )(a3, b3)


def _lru_gates_bwd(lmb, hl, a, r, i, xc, u, conv_w, w_r, w_i, lam):
    T = u.shape[0]
    col, vec, wblk, cw = _lru_specs(T)

    def body(l_ref, h_ref, a_ref, r_ref, i_ref, xc_ref, x_ref, cw_ref, wr_ref, wi_ref, lam_ref,
             dx_ref, dwr_ref, dwi_ref, dvec_ref, dcw_ref):
        l = l_ref[...]
        av, rv, iv, xc = a_ref[...], r_ref[...], i_ref[...], xc_ref[...]
        lam_v = lam_ref[...]
        sp = _softplus(-lam_v)
        la = (-LRU_C) * rv * sp
        mult = jnp.sqrt(-_expm1(2.0 * la))
        da = l * _shift_down(h_ref[...], 1)
        dmult = l * (iv * xc)
        di = l * mult * xc
        dxc = l * mult * iv
        dla = da * av - dmult * (av * av) / mult
        dzr = (dla * ((-LRU_C) * sp)) * rv * (1.0 - rv)
        dzi = di * iv * (1.0 - iv)
        dsp = jnp.sum(dla * ((-LRU_C) * rv), axis=0, keepdims=True)
        dlam = dsp * (-_sigmoid(-lam_v))
        dwr_ref[...] = _dot(xc, dzr, "tn")
        dwi_ref[...] = _dot(xc, dzi, "tn")
        dxc = dxc + _dot(dzr, wr_ref[...], "nt") + _dot(dzi, wi_ref[...], "nt")
        x = x_ref[...]
        w = cw_ref[...]
        dx = (dxc * w[3:4] + _shift_up(dxc, 1) * w[2:3] + _shift_up(dxc, 2) * w[1:2]
              + _shift_up(dxc, 3) * w[0:1])
        dx_ref[...] = dx.astype(BF16)
        dvec_ref[...] = jnp.concatenate(
            [jnp.sum(dzr, axis=0, keepdims=True), jnp.sum(dzi, axis=0, keepdims=True), dlam,
             jnp.sum(dxc, axis=0, keepdims=True)], axis=0)
        dcw_ref[...] = jnp.concatenate(
            [jnp.sum(dxc * _shift_down(x, 3 - tap), axis=0, keepdims=True) if tap < 3
             else jnp.sum(dxc * x, axis=0, keepdims=True) for tap in range(CONV_TAPS)], axis=0)

    c0 = col(0)
    return _pcall(
        body, name="lru_gates_bwd", grid=(LRU_BLOCKS,),
        in_specs=[c0, c0, c0, c0, c0, c0, col(24), cw, wblk, wblk, vec],
        out_specs=[c0, wblk, wblk, cw, cw],
        out_shape=[_sds((T, D), BF16), _sds((LRU_BLOCKS, LRU_BLOCK, LRU_BLOCK), F32),
                   _sds((LRU_BLOCKS, LRU_BLOCK, LRU_BLOCK), F32), _sds((4, D), F32), _sds((CONV_TAPS, D), F32)],
    )(lmb, hl, a, r, i, xc, u, conv_w, w_r, w_i, lam)


def _xattn_probs(q, k):
    sc = _dot(q, k, "nt") * (X_HD ** -0.5)
    e = jnp.exp(sc - jnp.max(sc, axis=-1, keepdims=True))
    return e / jnp.sum(e, axis=-1, keepdims=True)


def _xattn_fwd(xq, xk, xv):
    T = xq.shape[0]
    tq = min(WIDE_ROW_TILE, T)
    M = xk.shape[0]

    def body(q_ref, k_ref, v_ref, o_ref):
        p = _xattn_probs(q_ref[...], k_ref[...])
        o_ref[...] = _dot(p, v_ref[...], "nn").astype(BF16)

    qs = _spec((tq, X_HD), lambda h, i: (i, h))
    kv = _spec((M, X_HD), lambda h, i: (0, h))
    return _pcall(
        body, name="xattn_fwd", grid=(X_HEADS, T // tq),
        in_specs=[qs, kv, kv], out_specs=qs, out_shape=_sds((T, D), BF16),
    )(xq, xk, xv)


def _xattn_bwd(xq, xk, xv, dxo):
    T = xq.shape[0]
    tq = min(WIDE_ROW_TILE, T)
    M = xk.shape[0]

    def body(q_ref, k_ref, v_ref, do_ref, dq_ref, dk_ref, dv_ref):
        first = pl.program_id(1) == 0
        q, k, v, do = q_ref[...], k_ref[...], v_ref[...], do_ref[...]
        p = _xattn_probs(q, k)
        dp = _dot(do, v, "nt")
        ds = p * (dp - jnp.sum(dp * p, axis=-1, keepdims=True)) * (X_HD ** -0.5)
        dq_ref[...] = _dot(ds, k, "nn").astype(BF16)
        _accumulate(dk_ref, _dot(ds, q, "tn"), first)
        _accumulate(dv_ref, _dot(p, do, "tn"), first)

    qs = _spec((tq, X_HD), lambda h, i: (i, h))
    kv = _spec((M, X_HD), lambda h, i: (0, h))
    return _pcall(
        body, name="xattn_bwd", grid=(X_HEADS, T // tq),
        in_specs=[qs, kv, kv, qs], out_specs=[qs, kv, kv],
        out_shape=[_sds((T, D), BF16), _sds((M, D), F32), _sds((M, D), F32)],
    )(xq, xk, xv, dxo)


def _adamw(name, w, g, m, v):
    R, C = w.shape
    tr = R
    for cand in (512, 352, 256):
        if R % cand == 0:
            tr = cand
            break

    def fn(irefs, orefs, ids):
        delta, mn, vn = _adamw_update(*(r[...] for r in irefs))
        orefs[0][...] = delta
        orefs[1][...] = mn
        orefs[2][...] = vn

    blk = _spec((tr, C), lambda i: (i, 0))
    return _rowwise(name, fn, [(w, blk), (g, blk), (m, blk), (v, blk)],
                    [(_sds((R, C), F32), blk)] * 3, (R // tr,))


def _adamw_update(wv, gv, mv, vv):
    c1 = 1.0 - ADAM_B1 ** ADAM_STEP
    c2 = 1.0 - ADAM_B2 ** ADAM_STEP
    mn = ADAM_B1 * mv + (1.0 - ADAM_B1) * gv
    vn = ADAM_B2 * vv + (1.0 - ADAM_B2) * (gv * gv)
    delta = -ADAM_LR * ((mn / c1) / (jnp.sqrt(vn / c2) + ADAM_EPS) + ADAM_WD * wv)
    return delta, mn, vn


def _adamw_halves(name, w, mine, theirs, widx, m, v, core):
    R, C = w.shape
    H = R // 2
    tr = H
    while tr * C * 4 > (1 << 20) and tr % 16 == 0:
        tr //= 2
    nb = H // tr

    def body(core_ref, w_ref, mine_ref, theirs_ref, m_ref, v_ref, g_out, d_out, m_out, v_out):
        gv = jnp.where(pl.program_id(0) == core_ref[0], mine_ref[...], theirs_ref[...])
        delta, mn, vn = _adamw_update(w_ref[...], gv, m_ref[...], v_ref[...])
        g_out[...] = gv
        d_out[...] = delta
        m_out[...] = mn
        v_out[...] = vn

    full = pl.BlockSpec((tr, C), lambda h, i, core_ref: (h * nb + i, 0))
    mine_spec = pl.BlockSpec((None, tr, C), lambda h, i, core_ref: (widx, jnp.where(h == core_ref[0], i, 0), 0))
    theirs_spec = pl.BlockSpec((None, tr, C), lambda h, i, core_ref: (widx, jnp.where(h == core_ref[0], 0, i), 0))
    return _pcall(
        body, name=name, grid=(2, nb), num_prefetch=1,
        in_specs=[full, mine_spec, theirs_spec, full, full], out_specs=[full] * 4,
        out_shape=[_sds((R, C), F32)] * 4,
    )(core, w, mine, theirs, m, v)


def _rmsnorm(name, x, g):
    M = x.shape[0]
    tm = min(ROW_TILE, M)

    def fn(irefs, orefs, ids):
        orefs[0][...] = _rms_fwd(irefs[0][...], irefs[1][...]).astype(BF16)

    row = _spec((tm, D), lambda i: (i, 0))
    return _rowwise(name, fn, [(x, row), (g, _spec((1, D), lambda i: (0, 0)))],
                    [(_sds((M, D), BF16), row)], (M // tm,))[0]


WEIGHT_AT = {
    "ffn1_w1": ("col1", 0), "ffn1_w3": ("col1", 1), "ffn1_w2": ("row2a", 0),
    "w_ret_o": ("sqA", 0), "w_lru_o": ("sqA", 1), "w_out": ("sqA", 2),
    "w_xq": ("sqB", 0), "w_xk": ("sqB", 1), "w_xv": ("sqC", 0), "w_xo": ("sqC", 1),
    "ffn2_w1": ("col2a", 0), "ffn2_w3": ("col2b", 0), "ffn2_w2": ("row2b", 0),
}


def _local_step(x, mem, tgt, gw, sm, big):
    T = x.shape[0]
    tm = ROW_TILE

    def wt(name):
        key, idx = WEIGHT_AT[name]
        return gw[key], idx

    row3 = lambda i, j, r: (i, 0)
    vec3 = lambda i, j, r: (0, 0)
    rowD = _spec((tm, D), row3)
    vecD = _spec((1, D), vec3)

    def residual_norm(acc, erefs, orefs, ids):
        xo = erefs[0][...] + acc
        orefs[0][...] = xo
        orefs[1][...] = _rms_fwd(xo, erefs[1][...]).astype(BF16)

    def res_norm_io(x_res, g):
        return ([(x_res, rowD), (g, vecD)],
                [(_sds((T, D), F32), rowD), (_sds((T, D), BF16), rowD)])

    a1, b1, s1, h1 = _ffn_up("ffn1_up", x, *wt("ffn1_w1"), *wt("ffn1_w3"), norm_gain=sm["ffn1_norm"])
    x1, h2 = _ffn_down("ffn1_down", s1, *wt("ffn1_w2"), x, sm["mix_norm"])

    tw = min(WIDE_ROW_TILE, T)
    wideD = _spec((tw, D), row3)
    u = _gemm(
        "mix_in",
        [(h2, wideD, gw["win"], _spec((None, None, IN_BLK, D), lambda i, j, r: (j, 0, 0, 0)), "nt")],
        (T // tw, N_CHIPS, 1),
        [(_sds((T, 5120), F32), _spec((tw, IN_BLK), lambda i, j, r: (i, j)))], (tw, IN_BLK))[0]

    consts = _retention_constants(T)
    qr, kr, ret, yr, states = _ret_fwd(u, consts, sm["ret_gn"])

    conv_w = gw["conv"][:, 0].transpose(1, 0, 2).reshape(CONV_TAPS, D)
    xc, rg, ig, av, bx = _lru_gates_fwd(u, conv_w, sm["conv_b"], sm["w_rgate"], sm["b_rgate"],
                                        sm["w_igate"], sm["b_igate"], sm["lru_lambda"])
    a3 = av.reshape(T, LRU_BLOCKS, LRU_BLOCK)
    b3 = bx.reshape(T, LRU_BLOCKS, LRU_BLOCK)

    def gate_epilogue(acc, erefs, orefs, ids):
        orefs[0][...] = _sigmoid(acc + erefs[0][...])

    gates = _gemm(
        "mix_gates",
        [(h2, wideD, gw["wbg"], _spec((None, None, BG_BLK, D), lambda i, j, r: (j, 0, 0, 0)), "nt")],
        (T // tw, N_CHIPS, 1),
        [(_sds((T, 2 * D), F32), _spec((tw, BG_BLK), lambda i, j, r: (i, j)))], (tw, BG_BLK),
        [(sm["b_branch_gate"], _spec((1, BG_BLK), lambda i, j, r: (0, j)))], gate_epilogue)[0]

    hl = _lru_scan("lru_scan_fwd", a3, b3, False).reshape(T, D)

    row1 = _spec((tm, D), lambda i: (i, 0))
    glru1 = _spec((tm, D), lambda i: (i, 4))

    def lru_out(irefs, orefs, ids):
        gl, _ = _gelu_and_grad(irefs[1][...])
        orefs[0][...] = (irefs[0][...] * gl).astype(BF16)

    yl = _rowwise("lru_out", lru_out, [(hl, row1), (u, glru1)], [(_sds((T, D), BF16), row1)], (T // tm,))[0]

    y_ret = _proj_sq("y_ret", yr, *wt("w_ret_o"), "nn")[0]

    def merge_epilogue(acc, erefs, orefs, ids):
        orefs[0][...] = acc
        orefs[1][...] = (erefs[0][...] * erefs[2][...] + erefs[1][...] * acc).astype(BF16)

    y_lru, merged = _proj_sq(
        "y_lru", yl, *wt("w_lru_o"), "nn",
        extras=[(gates, _spec((tm, D), lambda i, j, r: (i, 0))), (gates, _spec((tm, D), lambda i, j, r: (i, 1))),
                (y_ret, rowD)],
        epilogue=merge_epilogue,
        outs=[(_sds((T, D), F32), rowD), (_sds((T, D), BF16), rowD)])

    ex, ou = res_norm_io(x1, sm["xattn_norm"])
    x2, hq = _proj_sq("mix_out", merged, *wt("w_out"), "nn", extras=ex, epilogue=residual_norm, outs=ou)

    m = _rmsnorm("mem_norm", mem, sm["mem_norm"])
    xq = _proj_sq("xq", hq, *wt("w_xq"), "nn", BF16)[0]
    xk = _proj_sq("xk", m, *wt("w_xk"), "nn", BF16)[0]
    xv = _proj_sq("xv", m, *wt("w_xv"), "nn", BF16)[0]
    xo = _xattn_fwd(xq, xk, xv)
    ex, ou = res_norm_io(x2, sm["ffn2_norm"])
    x3, h3 = _proj_sq("xattn_out", xo, *wt("w_xo"), "nn", extras=ex, epilogue=residual_norm, outs=ou)

    a2, b2, s2 = _ffn_up("ffn2_up", h3, *wt("ffn2_w1"), *wt("ffn2_w3"))
    loss, dx4, dg_final = _ffn_down("ffn2_down", s2, *wt("ffn2_w2"), x3, sm["final_norm"], loss_target=tgt)

    dx3, dg_ffn2 = _ffn_bwd("ffn2", dx4, h3, a2, b2, s2, *wt("ffn2_w1"), *wt("ffn2_w3"),
                            *wt("ffn2_w2"), x3, sm["ffn2_norm"], big)

    dxo = _proj_sq("d_xo", dx3, *wt("w_xo"), "nt", BF16)[0]
    big["w_xo"] = _dw_sq("dw_xo", xo, dx3)[None]
    dxq, dxk, dxv = _xattn_bwd(xq, xk, xv, dxo)
    big["w_xq"] = _dw_sq("dw_xq", hq, dxq)[None]
    ex, ou = _rms_bwd_io(x2, sm["xattn_norm"], dx3, T, tm)
    dx2, dg_xattn = _proj_sq("d_hq", dxq, *wt("w_xq"), "nt", extras=ex, epilogue=_rms_bwd_epilogue, outs=ou)
    big["w_xk"] = _dw_sq("dw_xk", m, dxk)[None]
    big["w_xv"] = _dw_sq("dw_xv", m, dxv)[None]

    M = mem.shape[0]

    def mem_norm_epilogue(acc, erefs, orefs, ids):
        _, dgp = _rms_bwd(erefs[0][...], erefs[1][...], acc)
        orefs[0][...] = dgp

    wsq_spec = lambda idx: _spec((N_CHIPS, None, SQ_BLK, D), lambda i, j, r: (0, idx, 0, 0))
    memD = _spec((M, D), row3)
    dg_mem = _gemm(
        "d_mem_norm",
        [(dxk, memD, wt("w_xk")[0], wsq_spec(wt("w_xk")[1]), "nt"),
         (dxv, memD, wt("w_xv")[0], wsq_spec(wt("w_xv")[1]), "nt")],
        (1, 1, 1), [(_sds((1, D), F32), vecD)], (M, D),
        [(mem, memD), (sm["mem_norm"], vecD)], mem_norm_epilogue)[0]

    def merged_bwd_epilogue(acc, erefs, orefs, ids):
        gr, gl, yrv, ylv = (e[...] for e in erefs)
        orefs[0][...] = (acc * gr).astype(BF16)
        orefs[1][...] = (acc * gl).astype(BF16)
        dgr = acc * yrv * gr * (1.0 - gr)
        dgl = acc * ylv * gl * (1.0 - gl)
        orefs[2][:, :D] = dgr.astype(BF16)
        orefs[2][:, D:] = dgl.astype(BF16)
        dbb = jnp.concatenate([jnp.sum(dgr, axis=0, keepdims=True), jnp.sum(dgl, axis=0, keepdims=True)], axis=1)
        _accumulate(orefs[3], dbb, ids[0] == 0)

    dy_ret, dy_lru, dgpre, db_bg = _proj_sq(
        "d_merged", dx2, *wt("w_out"), "nt",
        extras=[(gates, _spec((tm, D), lambda i, j, r: (i, 0))), (gates, _spec((tm, D), lambda i, j, r: (i, 1))),
                (y_ret, rowD), (y_lru, rowD)],
        epilogue=merged_bwd_epilogue,
        outs=[(_sds((T, D), BF16), rowD), (_sds((T, D), BF16), rowD),
              (_sds((T, 2 * D), BF16), _spec((tm, 2 * D), row3)),
              (_sds((1, 2 * D), F32), _spec((1, 2 * D), vec3))])
    big["w_branch_gate"] = _gemm(
        "dw_bg",
        [(h2, _spec((T, D), lambda j, n, r: (r, 0)), dgpre, _spec((T, BG_BLK), lambda j, n, r: (r, j)), "tn")],
        (N_CHIPS, 1, 1),
        [(_sds((N_CHIPS, D, BG_BLK), GRAD_WIRE_DTYPE), _spec((None, D, BG_BLK), lambda j, n, r: (j, 0, 0)))],
        (D, BG_BLK))[0][None]
    big["w_out"] = _dw_sq("dw_out", merged, dx2)[None]
    dyr = _proj_sq("d_yr", dy_ret, *wt("w_ret_o"), "nt")[0]
    big["w_ret_o"] = _dw_sq("dw_ret_o", yr, dy_ret)[None]
    dyl = _proj_sq("d_yl", dy_lru, *wt("w_lru_o"), "nt")[0]
    big["w_lru_o"] = _dw_sq("dw_lru_o", yl, dy_lru)[None]

    def lru_out_bwd(irefs, orefs, ids):
        gl, dgl = _gelu_and_grad(irefs[2][...])
        dyl_v = irefs[0][...]
        orefs[0][...] = dyl_v * gl
        orefs[1][...] = (dyl_v * irefs[1][...] * dgl).astype(BF16)

    dhl, dglru = _rowwise("lru_out_bwd", lru_out_bwd, [(dyl, row1), (hl, row1), (u, glru1)],
                          [(_sds((T, D), F32), row1), (_sds((T, D), BF16), row1)], (T // tm,))
    dhl3 = dhl.reshape(T, LRU_BLOCKS, LRU_BLOCK)
    dq, dk, dv, dgr, dg_retgn = _ret_bwd(dyr, ret, u, qr, kr, states, consts, sm["ret_gn"])
    lmb = _lru_scan("lru_scan_bwd", a3, dhl3, True).reshape(T, D)
    dxl, dw_r, dw_i, dvec, dcw = _lru_gates_bwd(lmb, hl, av, rg, ig, xc, u, conv_w,
                                                sm["w_rgate"], sm["w_igate"], sm["lru_lambda"])

    du = jnp.concatenate([dq, dk, dv, dgr, dxl, dglru], axis=1)
    tk = T
    big["w_in"] = _gemm(
        "dw_in",
        [(h2, _spec((tk, D), lambda j, n, r: (r, 0)), du, _spec((tk, IN_BLK), lambda j, n, r: (r, j)), "tn")],
        (N_CHIPS, 1, T // tk),
        [(_sds((N_CHIPS, D, IN_BLK), GRAD_WIRE_DTYPE), _spec((None, D, IN_BLK), lambda j, n, r: (j, 0, 0)))],
        (D, IN_BLK))[0][None]
    tf = min(FFN_ROW_TILE, T)
    ex, ou = _rms_bwd_io(x1, sm["mix_norm"], dx2, T, tf)
    dx1, dg_mix = _gemm(
        "d_h2",
        [(du, _spec((tf, 5120), row3), gw["win"], _spec((N_CHIPS, None, IN_BLK, D), lambda i, j, r: (0, 0, 0, 0)), "nn"),
         (dgpre, _spec((tf, 2 * D), row3), gw["wbg"], _spec((N_CHIPS, None, BG_BLK, D), lambda i, j, r: (0, 0, 0, 0)),
          "nn")],
        (T // tf, 1, 1), ou, (tf, D), ex, _rms_bwd_epilogue)

    grad_x, dg_ffn1 = _ffn_bwd("ffn1", dx1, h1, a1, b1, s1, *wt("ffn1_w1"), *wt("ffn1_w3"),
                               *wt("ffn1_w2"), x, sm["ffn1_norm"], big)

    small = {
        "ffn1_norm": dg_ffn1, "mix_norm": dg_mix, "ret_gn": dg_retgn, "conv_b": dvec[3:4],
        "b_rgate": dvec[0:1], "b_igate": dvec[1:2], "lru_lambda": dvec[2:3], "xattn_norm": dg_xattn,
        "mem_norm": dg_mem, "ffn2_norm": dg_ffn2, "final_norm": dg_final, "b_branch_gate": db_bg,
        "conv_w": dcw, "w_rgate": dw_r, "w_igate": dw_i,
    }
    return loss, grad_x, small


ANY_SPEC = pl.BlockSpec(memory_space=pl.ANY)
VMEM_SPEC = pl.BlockSpec(memory_space=pltpu.VMEM)
N_PEER_CHIPS = N_CHIPS - 1


def _mesh_position():
    x, y, c = lax.axis_index("x"), lax.axis_index("y"), lax.axis_index("c")
    chips = [(1 - x, y), (x, 1 - y), (1 - x, 1 - y)]
    return x, y, c, chips


def _chip_index(x, y):
    return 2 * x + y


def _rows_half(ref, axis, h):
    n = ref.shape[axis] // 2
    idx = [slice(None)] * len(ref.shape)
    idx[axis] = pl.ds(pl.multiple_of(h * n, BF16_TILE_ROWS), n)
    return ref.at[tuple(idx)]


def _remote(src, dst, send_sem, recv_sem, device):
    return pltpu.make_async_remote_copy(src_ref=src, dst_ref=dst, send_sem=send_sem, recv_sem=recv_sem,
                                        device_id=device, device_id_type=MESH)


def _gather_chips_task(shards, split, landed, legs="both"):
    keys = list(shards)
    n = len(keys)

    def operands():
        if legs == "pass_on":
            return [landed[k] for k in keys]
        chip_me = _chip_index(lax.axis_index("x"), lax.axis_index("y"))
        return [lax.dynamic_update_slice(lax.empty((N_CHIPS,) + shards[k].shape, shards[k].dtype), shards[k][None],
                                         (chip_me,) + (0,) * shards[k].ndim) for k in keys]

    def my_rows(ref, c):
        return _rows_half(ref, 1, c)

    def make_direct(ins, outs, send_sem, recv_sem):
        x, y, c, chips = _mesh_position()
        s_me = _chip_index(x, y)
        starts, arrivals = [], []
        for g in range(n):
            for k, chip in enumerate(chips):
                sems = (send_sem(3 * g + k), recv_sem(3 * g + k))
                starts.append(functools.partial(_remote, outs[g].at[s_me], outs[g].at[s_me], *sems, (*chip, c)))
                got = outs[g].at[_chip_index(*chip)]
                arrivals.append(functools.partial(_remote, got, got, *sems, (*chip, c)))
        return starts, arrivals

    def make_swap(ins, outs, send_sem, recv_sem):
        x, y, c, _ = _mesh_position()
        first, _ = _axis_neighbours(x, y, c)
        starts, arrivals = [], []
        for g in range(n):
            sems = (send_sem(3 * g), recv_sem(3 * g))
            mine = my_rows(outs[g].at[_chip_index(x, y)], c)
            starts.append(functools.partial(_remote, mine, mine, *sems, (*first, c)))
            got = my_rows(outs[g].at[_chip_index(*first)], c)
            arrivals.append(functools.partial(_remote, got, got, *sems, (*first, c)))
        return starts, arrivals

    def make_pass_on(ins, outs, send_sem, recv_sem):
        x, y, c, _ = _mesh_position()
        first, second = _axis_neighbours(x, y, c)
        diagonal = (1 - x, 1 - y)
        starts, arrivals = [], []
        for g in range(n):
            half = lambda chip: my_rows(outs[g].at[_chip_index(*chip)], c)
            for k, (sent, arriving) in enumerate([((x, y), second), (first, diagonal)]):
                sems = (send_sem(3 * g + 1 + k), recv_sem(3 * g + 1 + k))
                starts.append(functools.partial(_remote, half(sent), half(sent), *sems, (*second, c)))
                arrivals.append(functools.partial(_remote, half(arriving), half(arriving), *sems, (*second, c)))
        return starts, arrivals

    def finish(res):
        landed.update(zip(keys, res))

    shapes = lambda: [_sds((N_CHIPS,) + shards[k].shape, shards[k].dtype) for k in keys]
    aliases = {g: g for g in range(n)}
    if not split:
        return _Task("chips", operands, shapes, aliases, 3 * n, make_direct, finish)
    if legs == "swap":
        return _Task("first", operands, shapes, aliases, 3 * n, make_swap, finish)
    if legs == "pass_on":
        return _Task("second", operands, shapes, aliases, 3 * n, make_pass_on, finish)
    return _Task("first+second", operands, shapes, aliases, 3 * n, make_swap, finish, make_second=make_pass_on)


def _gather_sibling_task(keys, landed, ready):
    n = len(keys)

    def make(ins, outs, send_sem, recv_sem):
        x, y, c, chips = _mesh_position()
        starts, arrivals = [], []
        for g in range(n):
            for k, chip in enumerate(chips):
                o = outs[g].at[_chip_index(*chip)]
                got, other = _rows_half(o, 1, c), _rows_half(o, 1, 1 - c)
                starts.append(functools.partial(_remote, got, got, send_sem(3 * g + k), recv_sem(3 * g + k),
                                                (x, y, 1 - c)))
                arrivals.append(functools.partial(_remote, other, other, send_sem(3 * g + k), recv_sem(3 * g + k),
                                                  (x, y, 1 - c)))
        return starts, arrivals

    def finish(res):
        ready.update(zip(keys, res))

    return _Task("sibling", lambda: [landed[k] for k in keys],
                 lambda: [_sds(landed[k].shape, landed[k].dtype) for k in keys],
                 {g: g for g in range(n)}, 3 * n, make, finish)


def _pair_swap_task(names, big, got):
    n = len(names)

    def make(ins, outs, send_sem, recv_sem):
        x, y, c, _ = _mesh_position()
        copies = [functools.partial(_remote, _rows_half(ins[a], 2, 1 - c), outs[a], send_sem(a), recv_sem(a),
                                    (x, y, 1 - c)) for a in range(n)]
        return copies, copies

    def shapes():
        return [_sds(big[k].shape[:2] + (big[k].shape[2] // 2, big[k].shape[3]), big[k].dtype) for k in names]

    return _Task("sibling", lambda: [big[k] for k in names], shapes, {}, n, make,
                 lambda res: got.update(zip(names, res)))


def _rs_pair_sum(name, fulls, gots, core):
    n = len(fulls)
    shapes = [(f.shape[2] // 2, f.shape[3]) for f in fulls]

    def body(core_ref, *refs):
        for a_ref, b_ref, o_ref in zip(refs[:n], refs[n:2 * n], refs[2 * n:]):
            o_ref[...] = (a_ref[...].astype(F32) + b_ref[...].astype(F32)).astype(BF16)

    mine = [pl.BlockSpec((None, None) + hc, lambda s, core_ref: (0, s, core_ref[0], 0)) for hc in shapes]
    slot = [pl.BlockSpec((None, None) + hc, lambda s, core_ref: (0, s, 0, 0)) for hc in shapes]
    return _pcall(
        body, name=name, grid=(N_CHIPS,), num_prefetch=1,
        in_specs=mine + slot, out_specs=slot,
        out_shape=[_sds((1, N_CHIPS) + hc, BF16) for hc in shapes],
    )(core, *fulls, *gots)


def _chip_exchange_task(names, pair_sums, by_source, part=0, nparts=1):
    n = len(names)

    def rows(ref):
        h = ref.shape[1] // nparts
        return ref.at[:, pl.ds(part * h, h), :]

    def make(ins, outs, send_sem, recv_sem):
        x, y, c, chips = _mesh_position()
        s_me = _chip_index(x, y)
        starts, arrivals = [], []
        for a in range(n):
            for k, chip in enumerate(chips):
                s_k = _chip_index(*chip)
                starts.append(functools.partial(_remote, rows(ins[a].at[:, s_k]), rows(outs[a].at[:, s_me]),
                                                send_sem(3 * a + k), recv_sem(3 * a + k), (*chip, c)))
                got = rows(outs[a].at[:, s_k])
                arrivals.append(functools.partial(_remote, got, got, send_sem(3 * a + k), recv_sem(3 * a + k),
                                                  (*chip, c)))
        return starts, arrivals

    def operands():
        return [pair_sums[k] for k in names] + ([by_source[k] for k in names] if part else [])

    return _Task("chips", operands, lambda: [_sds(pair_sums[k].shape, pair_sums[k].dtype) for k in names],
                 {n + a: a for a in range(n)} if part else {}, 3 * n, make,
                 lambda res: by_source.update(zip(names, res)))


def _rs_chip_sum(name, owns, parts, chip):
    n = len(owns)
    ns = N_CHIPS
    shapes = [p.shape[2:] for p in parts]

    def body(chip_ref, *refs):
        me = chip_ref[0]
        for i in range(n):
            own_v = refs[i][...].astype(F32)
            slots = refs[n + ns * i:n + ns * (i + 1)]
            tot = None
            for s in range(ns):
                term = jnp.where(me == s, own_v, slots[s][...].astype(F32))
                tot = term if tot is None else tot + term
            refs[n + ns * n + i][...] = tot

    def slot_spec(hc, s):
        return pl.BlockSpec((None, None) + hc,
                            lambda g, chip_ref: (0, jnp.where(chip_ref[0] == s, (s + 1) % ns, s), 0, 0))

    own_specs = [pl.BlockSpec((None, None) + hc, lambda g, chip_ref: (0, chip_ref[0], 0, 0)) for hc in shapes]
    slot_specs = [slot_spec(hc, s) for hc in shapes for s in range(ns)]
    return _pcall(
        body, name=name, grid=(1,), num_prefetch=1,
        in_specs=own_specs + slot_specs,
        out_specs=[pl.BlockSpec((None,) + hc, lambda g, chip_ref: (0, 0, 0)) for hc in shapes],
        out_shape=[_sds((1,) + hc, F32) for hc in shapes],
    )(chip, *owns, *[p for p in parts for _ in range(ns)])


def _pair_gather_task(names, halves, sibling_halves):
    n = len(names)

    def make(ins, outs, send_sem, recv_sem):
        x, y, c, _ = _mesh_position()
        copies = [functools.partial(_remote, ins[a], outs[a], send_sem(a), recv_sem(a), (x, y, 1 - c))
                  for a in range(n)]
        return copies, copies

    return _Task("sibling", lambda: [halves[k] for k in names], lambda: [_sds(halves[k].shape, F32) for k in names],
                 {}, n, make, lambda res: sibling_halves.update(zip(names, res)))


def _small_allreduce(arrs):
    n = len(arrs)
    per = 1 + 2 * N_PEER_CHIPS

    def body(*refs):
        v_refs, o_refs = refs[:n], refs[n:2 * n]
        sib, pair, part = refs[2 * n:3 * n], refs[3 * n:4 * n], refs[4 * n:5 * n]
        send_sems, recv_sems = refs[5 * n:]
        x, y, c, chips = _mesh_position()
        s_me = _chip_index(x, y)

        def quarter(ref, s):
            q = ref.shape[0] // N_CHIPS
            return ref.at[pl.ds(pl.multiple_of(s * q, F32_TILE_ROWS), q)]

        def exchange(first_sem, src, dst_of, arrival_of):
            sems = lambda a, k: (send_sems.at[a * per + first_sem + k], recv_sems.at[a * per + first_sem + k])
            sends = [_remote(src(a, _chip_index(*chip)), dst_of(a, s_me), *sems(a, k), (*chip, c))
                     for a in range(n) for k, chip in enumerate(chips)]
            for cp in sends:
                cp.start()
            for a in range(n):
                for k, chip in enumerate(chips):
                    got = arrival_of(a, _chip_index(*chip))
                    _remote(got, got, *sems(a, k), (*chip, c)).wait_recv()
            for cp in sends:
                cp.wait_send()

        swaps = [_remote(v_refs[a], sib[a], send_sems.at[a * per], recv_sems.at[a * per], (x, y, 1 - c))
                 for a in range(n)]
        for cp in swaps:
            cp.start()
        for cp in swaps:
            cp.wait()
        for a in range(n):
            pair[a][...] = v_refs[a][...] + sib[a][...]
        exchange(1, lambda a, s_k: quarter(pair[a], s_k), lambda a, s: part[a].at[s], lambda a, s_k: part[a].at[s_k])
        for a in range(n):
            part[a][s_me] = quarter(pair[a], s_me)[...]
            q = o_refs[a].shape[0] // N_CHIPS
            o_refs[a][pl.ds(pl.multiple_of(s_me * q, F32_TILE_ROWS), q), :] = (
                ((part[a][0] + part[a][1]) + part[a][2]) + part[a][3])
        exchange(1 + N_PEER_CHIPS, lambda a, s_k: quarter(o_refs[a], s_me), lambda a, s: quarter(o_refs[a], s),
                 lambda a, s_k: quarter(o_refs[a], s_k))

    shapes = [a.shape for a in arrs]
    return _pcall(
        body, name="small_allreduce", grid=(1,), own_peers=("sibling", "chips"),
        in_specs=[VMEM_SPEC] * n, out_specs=[VMEM_SPEC] * n, out_shape=[_sds(s, F32) for s in shapes],
        scratch_shapes=([pltpu.VMEM(s, F32) for s in shapes] * 2
                        + [pltpu.VMEM((N_CHIPS, s[0] // N_CHIPS, s[1]), F32) for s in shapes]
                        + [pltpu.SemaphoreType.DMA((n * per,)), pltpu.SemaphoreType.DMA((n * per,))]),
    )(*arrs)


TRANSPOSED_WEIGHTS = ("ffn1_w1", "ffn1_w3", "ffn2_w1", "ffn2_w3")
SMALL_LAYOUT = [("ffn1_norm", 1), ("mix_norm", 1), ("ret_gn", 1), ("conv_b", 1), ("b_rgate", 1), ("b_igate", 1),
                ("lru_lambda", 1), ("xattn_norm", 1), ("mem_norm", 1), ("ffn2_norm", 1), ("final_norm", 1),
                ("b_branch_gate", 2), ("conv_w", CONV_TAPS)]
SMALL_ROWS = 32
GATE_WEIGHTS = ("w_rgate", "w_igate")
WEIGHT_ORDER = ["ffn1_norm", "ffn1_w1", "ffn1_w3", "ffn1_w2", "mix_norm", "w_in", "ret_gn", "w_ret_o", "conv_w",
                "conv_b", "w_rgate", "b_rgate", "w_igate", "b_igate", "lru_lambda", "w_lru_o", "w_branch_gate",
                "b_branch_gate", "w_out", "xattn_norm", "mem_norm", "w_xq", "w_xk", "w_xv", "w_xo", "ffn2_norm",
                "ffn2_w1", "ffn2_w3", "ffn2_w2", "final_norm"]


SMALL_USED_ROWS = sum(n for _, n in SMALL_LAYOUT)


def _pack_small(parts, extra_row=None):
    rows = [parts[name].reshape(n, D) for name, n in SMALL_LAYOUT]
    if extra_row is not None:
        rows.append(extra_row)
    rows.append(jnp.zeros((SMALL_ROWS - sum(r.shape[0] for r in rows), D), F32))
    return jnp.concatenate(rows, axis=0)


def _adamw_small(g_pack, ws, ms, vs):
    k = len(SMALL_LAYOUT)

    def body(g_ref, *refs):
        w_refs, m_refs, v_refs = refs[:k], refs[k:2 * k], refs[2 * k:3 * k]
        g_out, d_out, m_out, v_out = (refs[(3 + j) * k:(4 + j) * k] for j in range(4))
        r = 0
        for i, (_, n) in enumerate(SMALL_LAYOUT):
            gv = g_ref[r:r + n, :]
            delta, mn, vn = _adamw_update(w_refs[i][...], gv, m_refs[i][...], v_refs[i][...])
            g_out[i][...] = gv
            d_out[i][...] = delta
            m_out[i][...] = mn
            v_out[i][...] = vn
            r += n

    full = _spec((SMALL_ROWS, D), lambda i: (0, 0))
    parts = [_spec((n, D), lambda i: (0, 0)) for _, n in SMALL_LAYOUT]
    res = _pcall(body, name="adamw_small", grid=(1,), in_specs=[full] + parts * 3, out_specs=parts * 4,
                 out_shape=[_sds((n, D), F32) for _, n in SMALL_LAYOUT] * 4)(g_pack, *ws, *ms, *vs)
    return [res[j * k:(j + 1) * k] for j in range(4)]


def _unpack_small(packed, shapes):
    out, r = {}, 0
    for name, n in SMALL_LAYOUT:
        out[name] = packed[r:r + n].reshape(shapes[name])
        r += n
    return out


def kernel(x, mem, ffn1_norm, ffn1_w1, ffn1_w3, ffn1_w2, mix_norm, w_in, ret_gn, w_ret_o, conv_w, conv_b, w_rgate, b_rgate, w_igate, b_igate, lru_lambda, w_lru_o, w_branch_gate, b_branch_gate, w_out, xattn_norm, mem_norm, w_xq, w_xk, w_xv, w_xo, ffn2_norm, ffn2_w1, ffn2_w3, ffn2_w2, final_norm, loss_target, m_ffn1_norm, m_ffn1_w1, m_ffn1_w3, m_ffn1_w2, m_mix_norm, m_w_in, m_ret_gn, m_w_ret_o, m_conv_w, m_conv_b, m_w_rgate, m_b_rgate, m_w_igate, m_b_igate, m_lru_lambda, m_w_lru_o, m_w_branch_gate, m_b_branch_gate, m_w_out, m_xattn_norm, m_mem_norm, m_w_xq, m_w_xk, m_w_xv, m_w_xo, m_ffn2_norm, m_ffn2_w1, m_ffn2_w3, m_ffn2_w2, m_final_norm, v_ffn1_norm, v_ffn1_w1, v_ffn1_w3, v_ffn1_w2, v_mix_norm, v_w_in, v_ret_gn, v_w_ret_o, v_conv_w, v_conv_b, v_w_rgate, v_b_rgate, v_w_igate, v_b_igate, v_lru_lambda, v_w_lru_o, v_w_branch_gate, v_b_branch_gate, v_w_out, v_xattn_norm, v_mem_norm, v_w_xq, v_w_xk, v_w_xv, v_w_xo, v_ffn2_norm, v_ffn2_w1, v_ffn2_w3, v_ffn2_w2, v_final_norm):
    given = dict(locals())
    w = {n: given[n] for n in WEIGHT_ORDER}
    mom = {n: given["m_" + n] for n in WEIGHT_ORDER}
    var = {n: given["v_" + n] for n in WEIGHT_ORDER}
    chip = _chip_index(lax.axis_index("x"), lax.axis_index("y"))
    core = lax.axis_index("c").astype(jnp.int32).reshape(1)

    chip_id = chip.astype(jnp.int32).reshape(1)
    sm = {n: w[n] for n in ["ffn1_norm", "mix_norm", "ret_gn", "conv_b", "b_rgate", "b_igate", "lru_lambda",
                            "xattn_norm", "mem_norm", "ffn2_norm", "b_branch_gate"]}
    sm["final_norm"] = w["final_norm"].reshape(1, D)
    sm["w_rgate"] = w["w_rgate"][0]
    sm["w_igate"] = w["w_igate"][0]

    local = lambda a, n: jnp.swapaxes(a[0], 0, 1) if n in TRANSPOSED_WEIGHTS else a[0]
    stack = lambda names: jnp.stack([local(w[n], n) for n in names], axis=0).astype(BF16)
    shard = {"col1": stack(["ffn1_w1", "ffn1_w3"]), "row2a": stack(["ffn1_w2"]),
             "win": jnp.swapaxes(w["w_in"], 1, 2).astype(BF16),
             "wbg": jnp.swapaxes(w["w_branch_gate"], 1, 2).astype(BF16),
             "sqA": stack(["w_ret_o", "w_lru_o", "w_out"]), "sqB": stack(["w_xq", "w_xk"]),
             "sqC": stack(["w_xv", "w_xo"]), "col2a": stack(["ffn2_w1"]), "col2b": stack(["ffn2_w3"]),
             "row2b": stack(["ffn2_w2"]), "conv": w["conv_w"]}
    gw, landed = {}, {}
    over_chips = lambda keys: _gather_chips_task({k: shard[k] for k in keys}, True, landed)
    to_sibling = lambda keys: _gather_sibling_task(keys, landed, gw)

    big, got, pair_sums, by_source, halves, sibling_halves, outs = {}, {}, {}, {}, {}, {}, {}
    pair_swap = lambda names: _pair_swap_task(names, big, got)
    exchange = lambda names, part=0, nparts=1: _chip_exchange_task(names, pair_sums, by_source, part, nparts)
    pair_gather = lambda names: _pair_gather_task(names, halves, sibling_halves)

    def pair_sum(names):
        res = _rs_pair_sum("rs_pair_sum_" + names[0], [big[n] for n in names], [got[n] for n in names], core)
        pair_sums.update(zip(names, res))

    def chip_sum(names):
        res = _rs_chip_sum("rs_chip_sum_" + names[0], [pair_sums[n] for n in names], [by_source[n] for n in names],
                           chip_id)
        halves.update(zip(names, res))

    def adamw(names):
        for n in names:
            res = _adamw_halves("adamw_" + n, local(w[n], n), halves[n], sibling_halves[n], 0, local(mom[n], n),
                                local(var[n], n), core)
            outs[n] = tuple((jnp.swapaxes(r, 0, 1) if n in TRANSPOSED_WEIGHTS else r)[None] for r in res)

    do = lambda fn, names: functools.partial(fn, names)
    ffn2_grads = ["ffn2_w2", "ffn2_w1", "ffn2_w3"]
    xattn_grads = ["w_xo", "w_xq", "w_xk", "w_xv"]
    mix_out_grads = ["w_branch_gate", "w_out", "w_ret_o", "w_lru_o"]
    conv_gather = _gather_chips_task({"conv": shard["conv"]}, False, gw)
    swap = lambda key: _gather_chips_task({key: shard[key]}, True, landed, legs="swap")
    pass_on = lambda key: _gather_chips_task({key: shard[key]}, True, landed, legs="pass_on")
    plan = _Plan()
    plan.tasks = {
        "ag_first_chips": [over_chips(["col1"]), swap("row2a")],
        "ag_first_sibling": [to_sibling(["col1"]), pass_on("row2a"), swap("win")],
        "ffn1_up": [to_sibling(["row2a"]), pass_on("win"), swap("wbg")],
        "ffn1_down": [to_sibling(["win"]), pass_on("wbg"), swap("sqA")],
        "mix_in": [to_sibling(["wbg"]), pass_on("sqA"), swap("col2a"), conv_gather],
        "ret_fwd": [to_sibling(["sqA"]), pass_on("col2a"), swap("sqB")],
        "lru_gates_fwd": [to_sibling(["col2a"]), pass_on("sqB"), swap("sqC")],
        "mix_gates": [to_sibling(["sqB"]), pass_on("sqC"), swap("col2b")],
        "lru_scan_fwd": [to_sibling(["sqC"]), pass_on("col2b")],
        "y_lru": [to_sibling(["col2b"]), swap("row2b")],
        "ffn2_up": [pass_on("row2b")],
        "ffn2_up_sibling": [to_sibling(["row2b"])],
        "ffn2_dh": [pair_swap(ffn2_grads)],
        "xattn_bwd": [exchange(["ffn2_w2"], 0, 2)],
        "d_hq": [exchange(["ffn2_w2"], 1, 2)],
        "d_merged": [exchange(["ffn2_w1"], 0, 2), pair_swap(xattn_grads)],
        "lru_out_bwd": [exchange(["w_xo"])],
        "ret_bwd": [exchange(["ffn2_w1"], 1, 2), exchange(["ffn2_w3"], 0, 2), pair_swap(mix_out_grads)],
        "lru_scan_bwd": [exchange(["ffn2_w3"], 1, 2)],
        "lru_gates_bwd": [exchange(["w_xq", "w_xk"]), pair_gather(ffn2_grads)],
        "dw_in": [exchange(["w_xv", "w_out"])],
        "d_h2": [exchange(["w_branch_gate", "w_ret_o", "w_lru_o"]), pair_swap(["w_in"]), pair_gather(xattn_grads)],
        "ffn1_bwd_mid": [exchange(["w_in"], 0, 2), pair_gather(mix_out_grads)],
        "ffn1_dw2": [exchange(["w_in"], 2, 4)],
        "ffn1_dw1": [exchange(["w_in"], 3, 4), pair_swap(["ffn1_w2"])],
        "ffn1_dw3": [exchange(["ffn1_w2"], 0, 2), pair_swap(["ffn1_w1"]), pair_gather(["w_in"])],
        "ffn1_dh": [exchange(["ffn1_w2"], 1, 2), exchange(["ffn1_w1"]), pair_swap(["ffn1_w3"])],
        "small_allreduce": [exchange(["ffn1_w3"]), pair_gather(["ffn1_w2"])],
        "adamw_w_rgate": [pair_gather(["ffn1_w1", "ffn1_w3"])],
    }
    plan.after = {
        "ffn2_up": [functools.partial(_comm_call, "ffn2_up_sibling")],
        "ffn2_dh": [do(pair_sum, ffn2_grads)],
        "d_merged": [do(pair_sum, xattn_grads)],
        "ret_bwd": [do(pair_sum, mix_out_grads)],
        "lru_scan_bwd": [do(chip_sum, ffn2_grads)],
        "lru_gates_bwd": [do(adamw, ffn2_grads)],
        "dw_in": [do(chip_sum, xattn_grads)],
        "d_h2": [do(chip_sum, mix_out_grads), do(pair_sum, ["w_in"]), do(adamw, xattn_grads)],
        "ffn1_bwd_mid": [do(adamw, mix_out_grads)],
        "ffn1_dw1": [do(chip_sum, ["w_in"]), do(pair_sum, ["ffn1_w2"])],
        "ffn1_dw3": [do(pair_sum, ["ffn1_w1"]), do(adamw, ["w_in"])],
        "ffn1_dh": [do(pair_sum, ["ffn1_w3"]), do(chip_sum, ["ffn1_w2"])],
        "small_allreduce": [do(chip_sum, ["ffn1_w1", "ffn1_w3"])],
        "adamw_w_rgate": [do(adamw, ["ffn1_w2", "ffn1_w1", "ffn1_w3"])],
    }
    global _plan
    _plan = plan
    try:
        _comm_call("ag_first_chips")
        _comm_call("ag_first_sibling")
        loss_part, grad_x, small = _local_step(x[0], mem[0], loss_target[0], gw, sm, big)
        gate2d = lambda a: a.reshape(LRU_BLOCKS * LRU_BLOCK, LRU_BLOCK)
        loss_row = jnp.pad(loss_part, ((0, 0), (0, D - loss_part.shape[1])))
        small_sum, *gate_sums = _small_allreduce([_pack_small(small, loss_row)]
                                                 + [gate2d(small[n]) for n in GATE_WEIGHTS])
        for n, gsum in zip(GATE_WEIGHTS, gate_sums):
            d, nm, nv = _adamw("adamw_" + n, gate2d(w[n]), gsum, gate2d(mom[n]), gate2d(var[n]))
            outs[n] = tuple(r.reshape(w[n].shape) for r in (gsum, d, nm, nv))
    finally:
        _plan = None
    assert not plan.tasks and not plan.after, (list(plan.tasks), list(plan.after))
    loss = small_sum[SMALL_USED_ROWS, 0]

    small_shapes = {n: w[n].shape for n, _ in SMALL_LAYOUT}
    small_shapes["conv_w"] = (CONV_TAPS, D)
    conv_row = SMALL_USED_ROWS - CONV_TAPS
    conv_grad = lax.dynamic_slice(small_sum[conv_row:conv_row + CONV_TAPS], (0, chip * SQ_BLK), (CONV_TAPS, SQ_BLK))
    small_w = {n: w[n] for n, _ in SMALL_LAYOUT}
    small_m = {n: mom[n] for n, _ in SMALL_LAYOUT}
    small_v = {n: var[n] for n, _ in SMALL_LAYOUT}
    pad_cols = lambda a: jnp.pad(a[0], ((0, 0), (0, D - SQ_BLK)))
    for dct in (small_w, small_m, small_v):
        dct["conv_w"] = pad_cols(dct["conv_w"])
    g_pack = lax.dynamic_update_slice(small_sum, jnp.pad(conv_grad, ((0, 0), (0, D - SQ_BLK))), (conv_row, 0))
    rows_of = lambda dct: [dct[n].reshape(k, D) for n, k in SMALL_LAYOUT]
    unpacked = [{n: a.reshape(small_shapes[n]) for (n, _), a in zip(SMALL_LAYOUT, lst)}
                for lst in _adamw_small(g_pack, rows_of(small_w), rows_of(small_m), rows_of(small_v))]
    for n, _ in SMALL_LAYOUT:
        if n == "conv_w":
            outs[n] = tuple(u[n][:, :SQ_BLK][None] for u in unpacked)
        else:
            outs[n] = tuple(u[n] for u in unpacked)

    result = [loss, grad_x[None]]
    for k in range(4):
        result += [outs[n][k] for n in WEIGHT_ORDER]
    return tuple(result)
```
